```python
import math
import jax, jax.numpy as jnp
from jax import lax
import numpy as np

D_MODEL = 1024
BATCH = 8
SEQ = 4096
DEPTH = 1

D_MIX = D_MODEL
D_S5 = D_MIX // 2
D_RWKV = D_MIX - D_S5
S5_GROUP = 16
S5_GROUPS = D_S5 // S5_GROUP
S5_STATE = 64
RWKV_HEAD = 64
RWKV_HEADS = D_RWKV // RWKV_HEAD
DECAY_RANK = 64
ICLR_RANK = 64
D_RWKV_SHIFT = 3 * D_RWKV + DECAY_RANK + ICLR_RANK
D_IN = 2 * D_S5 + D_RWKV_SHIFT + D_RWKV
NORM_EPS = 1e-6
GN_EPS = 64e-5
DT_MIN = 1e-3
DT_MAX = 1e-1

kernel_name = "hymba_s5_rwkv7_hybrid_block"


def rms_norm(x, g):
    xf = x.astype(jnp.float32)
    y = xf * lax.rsqrt(jnp.mean(xf * xf, axis=-1, keepdims=True) + NORM_EPS)
    return (y * g.astype(jnp.float32)).astype(x.dtype)


def token_shift(h, mu):
    h_prev = jnp.pad(h, ((0, 0), (1, 0), (0, 0)))[:, :-1]
    return h + (h_prev - h) * mu


def s5_branch(u, lam_re, lam_im, log_dt, b_re, b_im, c_re, c_im, d, glu_w, glu_b):
    bsz, seqlen, _ = u.shape
    f32 = jnp.float32
    ug = u.reshape(bsz, seqlen, S5_GROUPS, S5_GROUP)
    lam = lax.complex(lam_re.astype(f32), lam_im.astype(f32))
    dt = jnp.exp(log_dt.astype(f32))[:, None]
    lam_bar = jnp.exp(lam * dt)
    b_mat = lax.complex(b_re.astype(f32), b_im.astype(f32))
    b_bar = ((lam_bar - 1.0) / lam)[..., None] * b_mat
    bu = jnp.einsum('blgh,gph->blgp', ug.astype(jnp.complex64), b_bar)
    a = jnp.broadcast_to(lam_bar, bu.shape)

    def combine(left, right):
        a_l, b_l = left
        a_r, b_r = right
        return a_r * a_l, a_r * b_l + b_r

    _, states = lax.associative_scan(combine, (a, bu), axis=1)
    c_mat = lax.complex(c_re.astype(f32), c_im.astype(f32))
    y = jnp.einsum('blgp,ghp->blgh', states, c_mat).real.reshape(bsz, seqlen, D_S5)
    y = y + d.astype(f32) * u
    y = jax.nn.gelu(y)
    return y * jax.nn.sigmoid(y @ glu_w.astype(f32) + glu_b.astype(f32))


def rwkv7_branch(r, k, v, wl, al, w0, w2, a0, a2, k_k, k_a, r_k, ln_w, ln_b):
    bsz, seqlen, _ = r.shape
    f32 = jnp.float32
    w = -jax.nn.softplus(-(w0.astype(f32) + jnp.tanh(wl) @ w2.astype(f32))) - 0.5
    decay = jnp.exp(-jnp.exp(w))
    a = jax.nn.sigmoid(a0.astype(f32) + al @ a2.astype(f32))
    heads = lambda t: t.reshape(bsz, seqlen, RWKV_HEADS, RWKV_HEAD)
    kk = heads(k * k_k.astype(f32))
    kk = kk / jnp.maximum(jnp.linalg.norm(kk, axis=-1, keepdims=True), 1e-12)
    k = k * (1.0 + (a - 1.0) * k_a.astype(f32))
    rh, kh, vh, wh, ah = heads(r), heads(k), heads(v), heads(decay), heads(a)
    tm = lambda t: jnp.moveaxis(t, 1, 0)
    xs = (tm(rh), tm(wh), tm(kh), tm(vh), tm(-kk), tm(kk * ah))

    def step(state, inp):
        r_t, w_t, k_t, v_t, a_t, b_t = inp
        sa = jnp.einsum('bhvk,bhk->bhv', state, a_t)
        state = (state * w_t[:, :, None, :]
                 + sa[..., None] * b_t[:, :, None, :]
                 + v_t[..., None] * k_t[:, :, None, :])
        return state, jnp.einsum('bhvk,bhk->bhv', state, r_t)

    s0 = jnp.zeros((bsz, RWKV_HEADS, RWKV_HEAD, RWKV_HEAD), f32)
    _, ys = lax.scan(step, s0, xs)
    y = jnp.moveaxis(ys, 0, 1)
    mean = jnp.mean(y, axis=-1, keepdims=True)
    var = jnp.mean(jnp.square(y - mean), axis=-1, keepdims=True)
    y = ((y - mean) * lax.rsqrt(var + GN_EPS)).reshape(bsz, seqlen, D_RWKV)
    y = y * ln_w.astype(f32) + ln_b.astype(f32)
    bonus = jnp.sum(rh * kh * r_k.astype(f32), axis=-1, keepdims=True) * vh
    return y + bonus.reshape(bsz, seqlen, D_RWKV)


def _fwd_setup_inputs(seed: int = 0) -> dict:
    key = jax.random.key(seed)
    ks = jax.random.split(key, 26)
    f32 = jnp.float32
    L = DEPTH
    G, P, H = S5_GROUPS, S5_STATE, S5_GROUP

    def nrm(k, shape, s):
        return jax.random.normal(k, shape, f32) * s

    x = nrm(ks[0], (BATCH, SEQ, D_MODEL), 1.0)
    norm_g = 1.0 + nrm(ks[1], (L, D_MODEL), 0.02)
    w_in = nrm(ks[2], (L, D_MODEL, D_IN), D_MODEL ** -0.5)
    n = jnp.arange(S5_STATE, dtype=f32)
    s5_lam_re = -0.5 * jnp.exp(nrm(ks[3], (L, G, P), 0.05))
    s5_lam_im = math.pi * n + nrm(ks[4], (L, G, P), 0.01)
    s5_log_dt = jax.random.uniform(ks[5], (L, G), f32, math.log(DT_MIN), math.log(DT_MAX))
    s5_b_re = nrm(ks[6], (L, G, P, H), (2 * H) ** -0.5)
    s5_b_im = nrm(ks[7], (L, G, P, H), (2 * H) ** -0.5)
    s5_c_re = nrm(ks[8], (L, G, H, P), (2 * P) ** -0.5)
    s5_c_im = nrm(ks[9], (L, G, H, P), (2 * P) ** -0.5)
    s5_d = nrm(ks[10], (L, D_S5), 0.5)
    s5_glu_w = nrm(ks[11], (L, D_S5, D_S5), D_S5 ** -0.5)
    s5_glu_b = nrm(ks[12], (L, D_S5), 0.01)
    rwkv_mu = jax.random.uniform(ks[13], (L, D_RWKV_SHIFT), f32)
    rwkv_w0 = jnp.linspace(-5.0, -0.5, D_RWKV, dtype=f32)[None] + nrm(ks[14], (L, D_RWKV), 0.1)
    rwkv_w2 = nrm(ks[15], (L, DECAY_RANK, D_RWKV), 0.1 * DECAY_RANK ** -0.5)
    rwkv_a0 = nrm(ks[16], (L, D_RWKV), 0.1)
    rwkv_a2 = nrm(ks[17], (L, ICLR_RANK, D_RWKV), 0.1 * ICLR_RANK ** -0.5)
    rwkv_k_k = 0.85 + nrm(ks[18], (L, D_RWKV), 0.02)
    rwkv_k_a = 1.0 + nrm(ks[19], (L, D_RWKV), 0.02)
    rwkv_r_k = nrm(ks[20], (L, RWKV_HEADS, RWKV_HEAD), 0.1)
    rwkv_ln_w = 1.0 + nrm(ks[21], (L, D_RWKV), 0.02)
    rwkv_ln_b = nrm(ks[22], (L, D_RWKV), 0.01)
    w_out = nrm(ks[23], (L, D_MIX, D_MODEL), D_MIX ** -0.5)
    final_g = 1.0 + nrm(ks[24], (D_MODEL,), 0.02)
    return {"x": x, "norm_g": norm_g, "w_in": w_in,
            "s5_lam_re": s5_lam_re, "s5_lam_im": s5_lam_im, "s5_log_dt": s5_log_dt,
            "s5_b_re": s5_b_re, "s5_b_im": s5_b_im, "s5_c_re": s5_c_re, "s5_c_im": s5_c_im,
            "s5_d": s5_d, "s5_glu_w": s5_glu_w, "s5_glu_b": s5_glu_b,
            "rwkv_mu": rwkv_mu, "rwkv_w0": rwkv_w0, "rwkv_w2": rwkv_w2,
            "rwkv_a0": rwkv_a0, "rwkv_a2": rwkv_a2, "rwkv_k_k": rwkv_k_k, "rwkv_k_a": rwkv_k_a,
            "rwkv_r_k": rwkv_r_k, "rwkv_ln_w": rwkv_ln_w, "rwkv_ln_b": rwkv_ln_b,
            "w_out": w_out, "final_g": final_g}


def _fwd_reference(x, norm_g, w_in, s5_lam_re, s5_lam_im, s5_log_dt, s5_b_re, s5_b_im, s5_c_re, s5_c_im,
              s5_d, s5_glu_w, s5_glu_b, rwkv_mu, rwkv_w0, rwkv_w2, rwkv_a0, rwkv_a2, rwkv_k_k, rwkv_k_a,
              rwkv_r_k, rwkv_ln_w, rwkv_ln_b, w_out, final_g):
    f32 = jnp.float32
    for l in range(DEPTH):
        h = rms_norm(x, norm_g[l])
        proj = (h @ w_in[l]).astype(f32)
        u, z_s5, rw, z_rwkv = jnp.split(
            proj, [D_S5, 2 * D_S5, 2 * D_S5 + D_RWKV_SHIFT], axis=-1)
        y_s5 = s5_branch(u, s5_lam_re[l], s5_lam_im[l], s5_log_dt[l], s5_b_re[l], s5_b_im[l],
                         s5_c_re[l], s5_c_im[l], s5_d[l], s5_glu_w[l], s5_glu_b[l])
        y_s5 = y_s5 * jax.nn.silu(z_s5)
        rw = token_shift(rw, rwkv_mu[l].astype(f32))
        r, k, v, wl, al = jnp.split(
            rw, [D_RWKV, 2 * D_RWKV, 3 * D_RWKV, 3 * D_RWKV + DECAY_RANK], axis=-1)
        y_rwkv = rwkv7_branch(r, k, v, wl, al, rwkv_w0[l], rwkv_w2[l], rwkv_a0[l], rwkv_a2[l],
                              rwkv_k_k[l], rwkv_k_a[l], rwkv_r_k[l], rwkv_ln_w[l], rwkv_ln_b[l])
        y_rwkv = y_rwkv * jax.nn.silu(z_rwkv)
        y = jnp.concatenate([y_s5, y_rwkv], axis=-1)
        x = (x.astype(f32) + y @ w_out[l].astype(f32)).astype(x.dtype)
    return rms_norm(x, final_g)


import jax as _jax
import jax.numpy as _jnp

TWIN_FORMAT = 'train_step'
FWD_PARAMS = ['x', 'norm_g', 'w_in', 's5_lam_re', 's5_lam_im', 's5_log_dt', 's5_b_re', 's5_b_im', 's5_c_re', 's5_c_im', 's5_d', 's5_glu_w', 's5_glu_b', 'rwkv_mu', 'rwkv_w0', 'rwkv_w2', 'rwkv_a0', 'rwkv_a2', 'rwkv_k_k', 'rwkv_k_a', 'rwkv_r_k', 'rwkv_ln_w', 'rwkv_ln_b', 'w_out', 'final_g']
TWIN_WEIGHTS = ['norm_g', 'w_in', 's5_lam_re', 's5_lam_im', 's5_log_dt', 's5_b_re', 's5_b_im', 's5_c_re', 's5_c_im', 's5_d', 's5_glu_w', 's5_glu_b', 'rwkv_mu', 'rwkv_w0', 'rwkv_w2', 'rwkv_a0', 'rwkv_a2', 'rwkv_k_k', 'rwkv_k_a', 'rwkv_r_k', 'rwkv_ln_w', 'rwkv_ln_b', 'w_out', 'final_g']
TWIN_DIFF_INPUT = 'x'
TWIN_INPUTS = ['x', 'norm_g', 'w_in', 's5_lam_re', 's5_lam_im', 's5_log_dt', 's5_b_re', 's5_b_im', 's5_c_re', 's5_c_im', 's5_d', 's5_glu_w', 's5_glu_b', 'rwkv_mu', 'rwkv_w0', 'rwkv_w2', 'rwkv_a0', 'rwkv_a2', 'rwkv_k_k', 'rwkv_k_a', 'rwkv_r_k', 'rwkv_ln_w', 'rwkv_ln_b', 'w_out', 'final_g', 'loss_target', 'm_norm_g', 'm_w_in', 'm_s5_lam_re', 'm_s5_lam_im', 'm_s5_log_dt', 'm_s5_b_re', 'm_s5_b_im', 'm_s5_c_re', 'm_s5_c_im', 'm_s5_d', 'm_s5_glu_w', 'm_s5_glu_b', 'm_rwkv_mu', 'm_rwkv_w0', 'm_rwkv_w2', 'm_rwkv_a0', 'm_rwkv_a2', 'm_rwkv_k_k', 'm_rwkv_k_a', 'm_rwkv_r_k', 'm_rwkv_ln_w', 'm_rwkv_ln_b', 'm_w_out', 'm_final_g', 'v_norm_g', 'v_w_in', 'v_s5_lam_re', 'v_s5_lam_im', 'v_s5_log_dt', 'v_s5_b_re', 'v_s5_b_im', 'v_s5_c_re', 'v_s5_c_im', 'v_s5_d', 'v_s5_glu_w', 'v_s5_glu_b', 'v_rwkv_mu', 'v_rwkv_w0', 'v_rwkv_w2', 'v_rwkv_a0', 'v_rwkv_a2', 'v_rwkv_k_k', 'v_rwkv_k_a', 'v_rwkv_r_k', 'v_rwkv_ln_w', 'v_rwkv_ln_b', 'v_w_out', 'v_final_g']
TWIN_OUTPUTS = ['loss', 'grad_x', 'grad_norm_g', 'grad_w_in', 'grad_s5_lam_re', 'grad_s5_lam_im', 'grad_s5_log_dt', 'grad_s5_b_re', 'grad_s5_b_im', 'grad_s5_c_re', 'grad_s5_c_im', 'grad_s5_d', 'grad_s5_glu_w', 'grad_s5_glu_b', 'grad_rwkv_mu', 'grad_rwkv_w0', 'grad_rwkv_w2', 'grad_rwkv_a0', 'grad_rwkv_a2', 'grad_rwkv_k_k', 'grad_rwkv_k_a', 'grad_rwkv_r_k', 'grad_rwkv_ln_w', 'grad_rwkv_ln_b', 'grad_w_out', 'grad_final_g', 'delta_norm_g', 'delta_w_in', 'delta_s5_lam_re', 'delta_s5_lam_im', 'delta_s5_log_dt', 'delta_s5_b_re', 'delta_s5_b_im', 'delta_s5_c_re', 'delta_s5_c_im', 'delta_s5_d', 'delta_s5_glu_w', 'delta_s5_glu_b', 'delta_rwkv_mu', 'delta_rwkv_w0', 'delta_rwkv_w2', 'delta_rwkv_a0', 'delta_rwkv_a2', 'delta_rwkv_k_k', 'delta_rwkv_k_a', 'delta_rwkv_r_k', 'delta_rwkv_ln_w', 'delta_rwkv_ln_b', 'delta_w_out', 'delta_final_g', 'new_m_norm_g', 'new_m_w_in', 'new_m_s5_lam_re', 'new_m_s5_lam_im', 'new_m_s5_log_dt', 'new_m_s5_b_re', 'new_m_s5_b_im', 'new_m_s5_c_re', 'new_m_s5_c_im', 'new_m_s5_d', 'new_m_s5_glu_w', 'new_m_s5_glu_b', 'new_m_rwkv_mu', 'new_m_rwkv_w0', 'new_m_rwkv_w2', 'new_m_rwkv_a0', 'new_m_rwkv_a2', 'new_m_rwkv_k_k', 'new_m_rwkv_k_a', 'new_m_rwkv_r_k', 'new_m_rwkv_ln_w', 'new_m_rwkv_ln_b', 'new_m_w_out', 'new_m_final_g', 'new_v_norm_g', 'new_v_w_in', 'new_v_s5_lam_re', 'new_v_s5_lam_im', 'new_v_s5_log_dt', 'new_v_s5_b_re', 'new_v_s5_b_im', 'new_v_s5_c_re', 'new_v_s5_c_im', 'new_v_s5_d', 'new_v_s5_glu_w', 'new_v_s5_glu_b', 'new_v_rwkv_mu', 'new_v_rwkv_w0', 'new_v_rwkv_w2', 'new_v_rwkv_a0', 'new_v_rwkv_a2', 'new_v_rwkv_k_k', 'new_v_rwkv_k_a', 'new_v_rwkv_r_k', 'new_v_rwkv_ln_w', 'new_v_rwkv_ln_b', 'new_v_w_out', 'new_v_final_g']
TWIN_LEAF_KINDS = {'loss': 'loss', 'grad_x': 'grad_x', 'grad_norm_g': 'grad_w', 'grad_w_in': 'grad_w', 'grad_s5_lam_re': 'grad_w', 'grad_s5_lam_im': 'grad_w', 'grad_s5_log_dt': 'grad_w', 'grad_s5_b_re': 'grad_w', 'grad_s5_b_im': 'grad_w', 'grad_s5_c_re': 'grad_w', 'grad_s5_c_im': 'grad_w', 'grad_s5_d': 'grad_w', 'grad_s5_glu_w': 'grad_w', 'grad_s5_glu_b': 'grad_w', 'grad_rwkv_mu': 'grad_w', 'grad_rwkv_w0': 'grad_w', 'grad_rwkv_w2': 'grad_w', 'grad_rwkv_a0': 'grad_w', 'grad_rwkv_a2': 'grad_w', 'grad_rwkv_k_k': 'grad_w', 'grad_rwkv_k_a': 'grad_w', 'grad_rwkv_r_k': 'grad_w', 'grad_rwkv_ln_w': 'grad_w', 'grad_rwkv_ln_b': 'grad_w', 'grad_w_out': 'grad_w', 'grad_final_g': 'grad_w', 'delta_norm_g': 'delta_w', 'delta_w_in': 'delta_w', 'delta_s5_lam_re': 'delta_w', 'delta_s5_lam_im': 'delta_w', 'delta_s5_log_dt': 'delta_w', 'delta_s5_b_re': 'delta_w', 'delta_s5_b_im': 'delta_w', 'delta_s5_c_re': 'delta_w', 'delta_s5_c_im': 'delta_w', 'delta_s5_d': 'delta_w', 'delta_s5_glu_w': 'delta_w', 'delta_s5_glu_b': 'delta_w', 'delta_rwkv_mu': 'delta_w', 'delta_rwkv_w0': 'delta_w', 'delta_rwkv_w2': 'delta_w', 'delta_rwkv_a0': 'delta_w', 'delta_rwkv_a2': 'delta_w', 'delta_rwkv_k_k': 'delta_w', 'delta_rwkv_k_a': 'delta_w', 'delta_rwkv_r_k': 'delta_w', 'delta_rwkv_ln_w': 'delta_w', 'delta_rwkv_ln_b': 'delta_w', 'delta_w_out': 'delta_w', 'delta_final_g': 'delta_w', 'new_m_norm_g': 'new_m', 'new_m_w_in': 'new_m', 'new_m_s5_lam_re': 'new_m', 'new_m_s5_lam_im': 'new_m', 'new_m_s5_log_dt': 'new_m', 'new_m_s5_b_re': 'new_m', 'new_m_s5_b_im': 'new_m', 'new_m_s5_c_re': 'new_m', 'new_m_s5_c_im': 'new_m', 'new_m_s5_d': 'new_m', 'new_m_s5_glu_w': 'new_m', 'new_m_s5_glu_b': 'new_m', 'new_m_rwkv_mu': 'new_m', 'new_m_rwkv_w0': 'new_m', 'new_m_rwkv_w2': 'new_m', 'new_m_rwkv_a0': 'new_m', 'new_m_rwkv_a2': 'new_m', 'new_m_rwkv_k_k': 'new_m', 'new_m_rwkv_k_a': 'new_m', 'new_m_rwkv_r_k': 'new_m', 'new_m_rwkv_ln_w': 'new_m', 'new_m_rwkv_ln_b': 'new_m', 'new_m_w_out': 'new_m', 'new_m_final_g': 'new_m', 'new_v_norm_g': 'new_v', 'new_v_w_in': 'new_v', 'new_v_s5_lam_re': 'new_v', 'new_v_s5_lam_im': 'new_v', 'new_v_s5_log_dt': 'new_v', 'new_v_s5_b_re': 'new_v', 'new_v_s5_b_im': 'new_v', 'new_v_s5_c_re': 'new_v', 'new_v_s5_c_im': 'new_v', 'new_v_s5_d': 'new_v', 'new_v_s5_glu_w': 'new_v', 'new_v_s5_glu_b': 'new_v', 'new_v_rwkv_mu': 'new_v', 'new_v_rwkv_w0': 'new_v', 'new_v_rwkv_w2': 'new_v', 'new_v_rwkv_a0': 'new_v', 'new_v_rwkv_a2': 'new_v', 'new_v_rwkv_k_k': 'new_v', 'new_v_rwkv_k_a': 'new_v', 'new_v_rwkv_r_k': 'new_v', 'new_v_rwkv_ln_w': 'new_v', 'new_v_rwkv_ln_b': 'new_v', 'new_v_w_out': 'new_v', 'new_v_final_g': 'new_v'}


def _forward(args):
    return _fwd_reference(*[args[k] for k in FWD_PARAMS])


def _output_shape():
    out = _jax.eval_shape(lambda: _forward(_fwd_setup_inputs(0)))
    return out.shape, out.dtype

N_MICROBATCH = 1
ADAM_LR = 0.001
ADAM_B1 = 0.9
ADAM_B2 = 0.999
ADAM_EPS = 1e-08
ADAM_WD = 0.01
ADAM_STEP = 10
PER_EXAMPLE_BATCH_AXIS = {'x': 0, 'loss_target': 0}
SHARED_INPUTS = []
_WEIGHT_DTYPES = {'norm_g': _jnp.float32, 'w_in': _jnp.float32, 's5_lam_re': _jnp.float32, 's5_lam_im': _jnp.float32, 's5_log_dt': _jnp.float32, 's5_b_re': _jnp.float32, 's5_b_im': _jnp.float32, 's5_c_re': _jnp.float32, 's5_c_im': _jnp.float32, 's5_d': _jnp.float32, 's5_glu_w': _jnp.float32, 's5_glu_b': _jnp.float32, 'rwkv_mu': _jnp.float32, 'rwkv_w0': _jnp.float32, 'rwkv_w2': _jnp.float32, 'rwkv_a0': _jnp.float32, 'rwkv_a2': _jnp.float32, 'rwkv_k_k': _jnp.float32, 'rwkv_k_a': _jnp.float32, 'rwkv_r_k': _jnp.float32, 'rwkv_ln_w': _jnp.float32, 'rwkv_ln_b': _jnp.float32, 'w_out': _jnp.float32, 'final_g': _jnp.float32}
MOMENT_SCALE = {'norm_g': 1.448046e-01, 'w_in': 8.435062e-02, 's5_lam_re': 1.505588e-03, 's5_lam_im': 1.866647e-03, 's5_log_dt': 9.009185e-01, 's5_b_re': 1.086980e-03, 's5_b_im': 1.098268e-03, 's5_c_re': 2.138009e-03, 's5_c_im': 2.165342e-03, 's5_d': 3.220479e-02, 's5_glu_w': 2.669645e-03, 's5_glu_b': 9.404116e-03, 'rwkv_mu': 1.640103e-01, 'rwkv_w0': 5.440358e-02, 'rwkv_w2': 9.592914e-03, 'rwkv_a0': 5.213576e-02, 'rwkv_a2': 4.131917e-02, 'rwkv_k_k': 7.451433e-02, 'rwkv_k_a': 1.256696e-01, 'rwkv_r_k': 2.479837e-01, 'rwkv_ln_w': 1.010531e-01, 'rwkv_ln_b': 1.119476e-01, 'w_out': 7.131676e-02, 'final_g': 3.201299e+01}


def _to_microbatches(a, axis):
    t = _jnp.moveaxis(a, axis, 0)
    t = t.reshape((N_MICROBATCH, t.shape[0] // N_MICROBATCH) + t.shape[1:])
    return _jnp.moveaxis(t, 1, axis + 1)


def setup_inputs(seed: int = 0) -> dict:
    inp = _fwd_setup_inputs(seed)
    key = _jax.random.fold_in(_jax.random.key(seed), 7919)
    shape, _ = _output_shape()
    out = dict(inp)
    out["loss_target"] = _jax.random.normal(_jax.random.fold_in(key, 0), shape, _jnp.float32)
    for i, name in enumerate(TWIN_WEIGHTS):
        w = inp[name].astype(_jnp.float32)
        if MOMENT_SCALE is None:
            s = _jnp.sqrt(_jnp.mean(_jnp.square(w)) + 1e-30)
        else:
            s = MOMENT_SCALE[name]
        km, kv = _jax.random.split(_jax.random.fold_in(key, i + 1))
        out[name] = w
        out["m_" + name] = s * _jax.random.normal(km, w.shape, _jnp.float32)
        out["v_" + name] = (s * s) * _jax.random.uniform(kv, w.shape, _jnp.float32, 0.5, 1.5)
    if N_MICROBATCH > 1:
        for name, axis in PER_EXAMPLE_BATCH_AXIS.items():
            out[name] = _to_microbatches(out[name], axis)
    return {'x': out['x'], 'norm_g': out['norm_g'], 'w_in': out['w_in'], 's5_lam_re': out['s5_lam_re'], 's5_lam_im': out['s5_lam_im'], 's5_log_dt': out['s5_log_dt'], 's5_b_re': out['s5_b_re'], 's5_b_im': out['s5_b_im'], 's5_c_re': out['s5_c_re'], 's5_c_im': out['s5_c_im'], 's5_d': out['s5_d'], 's5_glu_w': out['s5_glu_w'], 's5_glu_b': out['s5_glu_b'], 'rwkv_mu': out['rwkv_mu'], 'rwkv_w0': out['rwkv_w0'], 'rwkv_w2': out['rwkv_w2'], 'rwkv_a0': out['rwkv_a0'], 'rwkv_a2': out['rwkv_a2'], 'rwkv_k_k': out['rwkv_k_k'], 'rwkv_k_a': out['rwkv_k_a'], 'rwkv_r_k': out['rwkv_r_k'], 'rwkv_ln_w': out['rwkv_ln_w'], 'rwkv_ln_b': out['rwkv_ln_b'], 'w_out': out['w_out'], 'final_g': out['final_g'], 'loss_target': out['loss_target'], 'm_norm_g': out['m_norm_g'], 'm_w_in': out['m_w_in'], 'm_s5_lam_re': out['m_s5_lam_re'], 'm_s5_lam_im': out['m_s5_lam_im'], 'm_s5_log_dt': out['m_s5_log_dt'], 'm_s5_b_re': out['m_s5_b_re'], 'm_s5_b_im': out['m_s5_b_im'], 'm_s5_c_re': out['m_s5_c_re'], 'm_s5_c_im': out['m_s5_c_im'], 'm_s5_d': out['m_s5_d'], 'm_s5_glu_w': out['m_s5_glu_w'], 'm_s5_glu_b': out['m_s5_glu_b'], 'm_rwkv_mu': out['m_rwkv_mu'], 'm_rwkv_w0': out['m_rwkv_w0'], 'm_rwkv_w2': out['m_rwkv_w2'], 'm_rwkv_a0': out['m_rwkv_a0'], 'm_rwkv_a2': out['m_rwkv_a2'], 'm_rwkv_k_k': out['m_rwkv_k_k'], 'm_rwkv_k_a': out['m_rwkv_k_a'], 'm_rwkv_r_k': out['m_rwkv_r_k'], 'm_rwkv_ln_w': out['m_rwkv_ln_w'], 'm_rwkv_ln_b': out['m_rwkv_ln_b'], 'm_w_out': out['m_w_out'], 'm_final_g': out['m_final_g'], 'v_norm_g': out['v_norm_g'], 'v_w_in': out['v_w_in'], 'v_s5_lam_re': out['v_s5_lam_re'], 'v_s5_lam_im': out['v_s5_lam_im'], 'v_s5_log_dt': out['v_s5_log_dt'], 'v_s5_b_re': out['v_s5_b_re'], 'v_s5_b_im': out['v_s5_b_im'], 'v_s5_c_re': out['v_s5_c_re'], 'v_s5_c_im': out['v_s5_c_im'], 'v_s5_d': out['v_s5_d'], 'v_s5_glu_w': out['v_s5_glu_w'], 'v_s5_glu_b': out['v_s5_glu_b'], 'v_rwkv_mu': out['v_rwkv_mu'], 'v_rwkv_w0': out['v_rwkv_w0'], 'v_rwkv_w2': out['v_rwkv_w2'], 'v_rwkv_a0': out['v_rwkv_a0'], 'v_rwkv_a2': out['v_rwkv_a2'], 'v_rwkv_k_k': out['v_rwkv_k_k'], 'v_rwkv_k_a': out['v_rwkv_k_a'], 'v_rwkv_r_k': out['v_rwkv_r_k'], 'v_rwkv_ln_w': out['v_rwkv_ln_w'], 'v_rwkv_ln_b': out['v_rwkv_ln_b'], 'v_w_out': out['v_w_out'], 'v_final_g': out['v_final_g']}


def _loss(weights, diff, rest, loss_target):
    with _jax.named_scope("forward"):
        args = {**rest, TWIN_DIFF_INPUT: diff, **{k: w.astype(_WEIGHT_DTYPES[k]) for k, w in weights.items()}}
        y = _forward(args)
    with _jax.named_scope("loss_head"):
        err = _jnp.square(y.astype(_jnp.float32) - loss_target)
        return 0.5 * _jnp.sum(_jnp.mean(err, axis=-1)) if err.ndim else 0.5 * err


def _adamw(w, g, m, v):
    m = ADAM_B1 * m + (1.0 - ADAM_B1) * g
    v = ADAM_B2 * v + (1.0 - ADAM_B2) * _jnp.square(g)
    m_hat = m / (1.0 - ADAM_B1 ** ADAM_STEP)
    v_hat = v / (1.0 - ADAM_B2 ** ADAM_STEP)
    delta = -ADAM_LR * (m_hat / (_jnp.sqrt(v_hat) + ADAM_EPS) + ADAM_WD * w)
    return delta, m, v


def reference(x, norm_g, w_in, s5_lam_re, s5_lam_im, s5_log_dt, s5_b_re, s5_b_im, s5_c_re, s5_c_im, s5_d, s5_glu_w, s5_glu_b, rwkv_mu, rwkv_w0, rwkv_w2, rwkv_a0, rwkv_a2, rwkv_k_k, rwkv_k_a, rwkv_r_k, rwkv_ln_w, rwkv_ln_b, w_out, final_g, loss_target, m_norm_g, m_w_in, m_s5_lam_re, m_s5_lam_im, m_s5_log_dt, m_s5_b_re, m_s5_b_im, m_s5_c_re, m_s5_c_im, m_s5_d, m_s5_glu_w, m_s5_glu_b, m_rwkv_mu, m_rwkv_w0, m_rwkv_w2, m_rwkv_a0, m_rwkv_a2, m_rwkv_k_k, m_rwkv_k_a, m_rwkv_r_k, m_rwkv_ln_w, m_rwkv_ln_b, m_w_out, m_final_g, v_norm_g, v_w_in, v_s5_lam_re, v_s5_lam_im, v_s5_log_dt, v_s5_b_re, v_s5_b_im, v_s5_c_re, v_s5_c_im, v_s5_d, v_s5_glu_w, v_s5_glu_b, v_rwkv_mu, v_rwkv_w0, v_rwkv_w2, v_rwkv_a0, v_rwkv_a2, v_rwkv_k_k, v_rwkv_k_a, v_rwkv_r_k, v_rwkv_ln_w, v_rwkv_ln_b, v_w_out, v_final_g):
    given = dict(x=x, norm_g=norm_g, w_in=w_in, s5_lam_re=s5_lam_re, s5_lam_im=s5_lam_im, s5_log_dt=s5_log_dt, s5_b_re=s5_b_re, s5_b_im=s5_b_im, s5_c_re=s5_c_re, s5_c_im=s5_c_im, s5_d=s5_d, s5_glu_w=s5_glu_w, s5_glu_b=s5_glu_b, rwkv_mu=rwkv_mu, rwkv_w0=rwkv_w0, rwkv_w2=rwkv_w2, rwkv_a0=rwkv_a0, rwkv_a2=rwkv_a2, rwkv_k_k=rwkv_k_k, rwkv_k_a=rwkv_k_a, rwkv_r_k=rwkv_r_k, rwkv_ln_w=rwkv_ln_w, rwkv_ln_b=rwkv_ln_b, w_out=w_out, final_g=final_g, loss_target=loss_target, m_norm_g=m_norm_g, m_w_in=m_w_in, m_s5_lam_re=m_s5_lam_re, m_s5_lam_im=m_s5_lam_im, m_s5_log_dt=m_s5_log_dt, m_s5_b_re=m_s5_b_re, m_s5_b_im=m_s5_b_im, m_s5_c_re=m_s5_c_re, m_s5_c_im=m_s5_c_im, m_s5_d=m_s5_d, m_s5_glu_w=m_s5_glu_w, m_s5_glu_b=m_s5_glu_b, m_rwkv_mu=m_rwkv_mu, m_rwkv_w0=m_rwkv_w0, m_rwkv_w2=m_rwkv_w2, m_rwkv_a0=m_rwkv_a0, m_rwkv_a2=m_rwkv_a2, m_rwkv_k_k=m_rwkv_k_k, m_rwkv_k_a=m_rwkv_k_a, m_rwkv_r_k=m_rwkv_r_k, m_rwkv_ln_w=m_rwkv_ln_w, m_rwkv_ln_b=m_rwkv_ln_b, m_w_out=m_w_out, m_final_g=m_final_g, v_norm_g=v_norm_g, v_w_in=v_w_in, v_s5_lam_re=v_s5_lam_re, v_s5_lam_im=v_s5_lam_im, v_s5_log_dt=v_s5_log_dt, v_s5_b_re=v_s5_b_re, v_s5_b_im=v_s5_b_im, v_s5_c_re=v_s5_c_re, v_s5_c_im=v_s5_c_im, v_s5_d=v_s5_d, v_s5_glu_w=v_s5_glu_w, v_s5_glu_b=v_s5_glu_b, v_rwkv_mu=v_rwkv_mu, v_rwkv_w0=v_rwkv_w0, v_rwkv_w2=v_rwkv_w2, v_rwkv_a0=v_rwkv_a0, v_rwkv_a2=v_rwkv_a2, v_rwkv_k_k=v_rwkv_k_k, v_rwkv_k_a=v_rwkv_k_a, v_rwkv_r_k=v_rwkv_r_k, v_rwkv_ln_w=v_rwkv_ln_w, v_rwkv_ln_b=v_rwkv_ln_b, v_w_out=v_w_out, v_final_g=v_final_g)
    weights = {n: given[n] for n in TWIN_WEIGHTS}
    shared = {n: given[n] for n in SHARED_INPUTS}
    per_example = {n: given[n] for n in ['x']}
    grad_fn = _jax.value_and_grad(_loss, argnums=(0, 1))

    def one_microbatch(ex, loss_target):
        ex = dict(ex)
        diff = ex.pop(TWIN_DIFF_INPUT)
        return grad_fn(weights, diff, {**shared, **ex}, loss_target)

    if N_MICROBATCH == 1:
        loss, (grad_w, grad_x) = one_microbatch(per_example, given["loss_target"])
    else:
        def body(carry, xs):
            loss_sum, grad_sum = carry
            l_k, (gw_k, gx_k) = one_microbatch(xs[0], xs[1])
            with _jax.named_scope("update"):
                return (loss_sum + l_k, _jax.tree.map(_jnp.add, grad_sum, gw_k)), gx_k

        init = (_jnp.zeros((), _jnp.float32), _jax.tree.map(_jnp.zeros_like, weights))
        (loss, grad_w), grad_x = _jax.lax.scan(body, init, (per_example, given["loss_target"]))
    with _jax.named_scope("update"):
        delta_w, new_m, new_v = {}, {}, {}
        for n in TWIN_WEIGHTS:
            delta_w[n], new_m[n], new_v[n] = _adamw(weights[n], grad_w[n], given["m_" + n], given["v_" + n])
    return (loss, grad_x, *[grad_w[n] for n in TWIN_WEIGHTS], *[delta_w[n] for n in TWIN_WEIGHTS],
            *[new_m[n] for n in TWIN_WEIGHTS], *[new_v[n] for n in TWIN_WEIGHTS])
```

```python
import functools
import math

import jax
import jax.numpy as jnp
from jax import lax
from jax.experimental import pallas as pl
from jax.experimental.pallas import tpu as pltpu

F32 = jnp.float32
BF16 = jnp.bfloat16
HI = lax.Precision.HIGHEST

D_MODEL = 1024
D_S5 = 512
D_RWKV = 512
S5_GROUPS = 32
S5_GROUP = 16
S5_STATE = 64
N_STATE = S5_GROUPS * S5_STATE
N_HEADS = 8
HEAD = 64
D_SHIFT = 3 * D_RWKV + 128
D_IN = 2 * D_S5 + D_SHIFT + D_RWKV
NORM_EPS = 1e-6
GN_EPS = 64e-5
N_DEV = 8
LANES = 128
S5_BLOCKS = 4
S5_SEGMENTS = 8
RWKV_CHUNK = 64
VMEM_LIMIT = 56 * 1024 * 1024

ADAM_LR = 0.001
ADAM_B1 = 0.9
ADAM_B2 = 0.999
ADAM_EPS = 1e-08
ADAM_WD = 0.01
ADAM_STEP = 10


def _dot(a, b, dims, prec):
    return lax.dot_general(a, b, (dims, ((), ())), precision=prec, preferred_element_type=F32)


def _make_mm(cast, prec):
    @jax.custom_vjp
    def mm(a, b):
        return _dot(cast(a), cast(b), ((1,), (0,)), prec)

    def fwd(a, b):
        return mm(a, b), (a, b)

    def bwd(res, g):
        a, b = res
        return (_dot(cast(g), cast(b), ((1,), (1,)), prec), _dot(cast(a), cast(g), ((0,), (0,)), prec))

    mm.defvjp(fwd, bwd)
    return mm


mm_hi = _make_mm(lambda t: t, HI)
mm_bf = _make_mm(lambda t: t.astype(BF16), None)


def _sigmoid(x):
    return 1.0 / (1.0 + jnp.exp(-x))


def _silu(x):
    return x * _sigmoid(x)


def _softplus(x):
    return jnp.maximum(x, 0.0) + jnp.log(1.0 + jnp.exp(-jnp.abs(x)))


def _gelu(x):
    return 0.5 * x * (1.0 + jnp.tanh(math.sqrt(2.0 / math.pi) * (x + 0.044715 * x * x * x)))


def _rms(x, g):
    return x * lax.rsqrt(jnp.mean(x * x, axis=-1, keepdims=True) + NORM_EPS) * g


def _const_spec(shape):
    nd = len(shape)
    return pl.BlockSpec(shape, lambda *_: (0,) * nd, pipeline_mode=pl.Buffered(1))


def _acc_spec(shape):
    nd = len(shape)
    return pl.BlockSpec(shape, lambda *_: (0,) * nd)


def _params(sem):
    return pltpu.CompilerParams(dimension_semantics=(sem,), vmem_limit_bytes=VMEM_LIMIT)


_ANY = pl.BlockSpec(memory_space=pl.ANY)


def _sds(shape):
    return jax.ShapeDtypeStruct(shape, F32)


def _head_sum_matrix():
    i = jnp.arange(D_RWKV) // HEAD
    return (i[:, None] == i[None, :]).astype(F32)


def _s5_param_fn(lam_re, lam_im, logdt, b_re, b_im):
    dt = jnp.exp(logdt)
    mag = jnp.exp(lam_re * dt)
    ang = lam_im * dt
    lbr = mag * jnp.cos(ang)
    lbi = mag * jnp.sin(ang)
    nr = lbr - 1.0
    den = lam_re * lam_re + lam_im * lam_im
    cr = (nr * lam_re + lbi * lam_im) / den
    ci = (lbi * lam_re - nr * lam_im) / den
    return lbr, lbi, cr * b_re - ci * b_im, cr * b_im + ci * b_re


def _s5_param_fwd(lam_re, lam_im, logdt, b_re, b_im):
    def body(lr, li, ld, br, bi, o_lr, o_li, o_br, o_bi):
        outs = _s5_param_fn(lr[...], li[...], ld[...], br[...], bi[...])
        for o, v in zip((o_lr, o_li, o_br, o_bi), outs):
            o[...] = v

    return pl.pallas_call(
        body, name="s5_param_fwd",
        out_shape=[_sds((1, N_STATE))] * 2 + [_sds((S5_GROUP, N_STATE))] * 2,
    )(lam_re, lam_im, logdt, b_re, b_im)


def _s5_param_bwd(lam_re, lam_im, logdt, b_re, b_im, d_lbr, d_lbi, d_bbr, d_bbi, group_ind):
    def body(lr, li, ld, br, bi, g0, g1, g2, g3, ind, o_lr, o_li, o_ld, o_br, o_bi):
        _, vjp = jax.vjp(_s5_param_fn, lr[...], li[...], ld[...], br[...], bi[...])
        d_lr, d_li, d_ld, d_br, d_bi = vjp((g0[...], g1[...], g2[...], g3[...]))
        o_lr[...] = d_lr
        o_li[...] = d_li
        o_ld[...] = _dot(jnp.broadcast_to(d_ld, (8, N_STATE)), ind[...], ((1,), (0,)), HI)
        o_br[...] = d_br
        o_bi[...] = d_bi

    return pl.pallas_call(
        body, name="s5_param_bwd",
        out_shape=[_sds((1, N_STATE))] * 2 + [_sds((8, LANES))] + [_sds((S5_GROUP, N_STATE))] * 2,
    )(lam_re, lam_im, logdt, b_re, b_im, d_lbr, d_lbi, d_bbr, d_bbi, group_ind)


def _fwd_in(x, norm_g, w_in_bf, b4_re, b4_im, tt):
    L = x.shape[0]

    def body(x_ref, g_ref, w_ref, bre_ref, bim_ref, u_ref, zs_ref, rw_ref, zr_ref, bur_ref, bui_ref):
        h = _rms(x_ref[...], g_ref[...])
        proj = jnp.dot(h.astype(BF16), w_ref[...], preferred_element_type=F32)
        u = proj[:, 0:D_S5]
        u_ref[...] = u
        zs_ref[...] = proj[:, D_S5:2 * D_S5]
        rw_ref[...] = proj[:, 2 * D_S5:2 * D_S5 + D_SHIFT]
        zr_ref[...] = proj[:, 2 * D_S5 + D_SHIFT:D_IN]
        for q in range(S5_BLOCKS):
            uq = u[:, q * LANES:(q + 1) * LANES]
            cols = slice(q * 512, (q + 1) * 512)
            bur_ref[:, cols] = _dot(uq, bre_ref[q], ((1,), (0,)), HI)
            bui_ref[:, cols] = _dot(uq, bim_ref[q], ((1,), (0,)), HI)

    row = lambda n: pl.BlockSpec((tt, n), lambda i: (i, 0))
    return pl.pallas_call(
        body, name="fwd_in", grid=(L // tt,),
        in_specs=[row(D_MODEL), _const_spec((1, D_MODEL)), _const_spec((D_MODEL, D_IN)),
                  _const_spec((S5_BLOCKS, LANES, 512)), _const_spec((S5_BLOCKS, LANES, 512))],
        out_specs=[row(D_S5), row(D_S5), row(D_SHIFT), row(D_RWKV), row(N_STATE), row(N_STATE)],
        out_shape=[_sds((L, D_S5)), _sds((L, D_S5)), _sds((L, D_SHIFT)), _sds((L, D_RWKV)),
                   _sds((L, N_STATE)), _sds((L, N_STATE))],
        compiler_params=_params("parallel"),
    )(x, norm_g, w_in_bf, b4_re, b4_im)


def _cmul(ar, ai, br, bi):
    return ar * br - ai * bi, ar * bi + ai * br


def _s5_scan(xr, xi, lam_r, lam_i, reverse):
    L = xr.shape[0]
    nseg = S5_SEGMENTS
    seg = L // nseg
    cb = LANES

    def body(xr_ref, xi_ref, lr_ref, li_ref, or_ref, oi_ref):
        lr = jnp.broadcast_to(lr_ref[...], (nseg, cb))
        li = jnp.broadcast_to(li_ref[...], (nseg, cb))

        def rows(i):
            return pl.ds((seg - 1 - i) if reverse else i, nseg, stride=seg)

        def local_scan(i, c):
            mr, mi = _cmul(lr, li, c[0], c[1])
            sr = mr + xr_ref[rows(i), :]
            si = mi + xi_ref[rows(i), :]
            or_ref[rows(i), :] = sr
            oi_ref[rows(i), :] = si
            return sr, si

        zero = jnp.zeros((nseg, cb), F32)
        end_r, end_i = lax.fori_loop(0, seg, local_scan, (zero, zero))

        pr, pi_ = lr, li
        n = 1
        while n < seg:
            pr, pi_ = _cmul(pr, pi_, pr, pi_)
            n *= 2
        rid = lax.broadcasted_iota(jnp.int32, (nseg, cb), 0)
        cr, ci = zero, zero
        carry_r, carry_i = zero[0:1], zero[0:1]
        for j in (range(nseg - 2, -1, -1) if reverse else range(1, nseg)):
            src = j + 1 if reverse else j - 1
            mr, mi = _cmul(pr[0:1], pi_[0:1], carry_r, carry_i)
            carry_r = end_r[src:src + 1] + mr
            carry_i = end_i[src:src + 1] + mi
            cr = jnp.where(rid == j, jnp.broadcast_to(carry_r, (nseg, cb)), cr)
            ci = jnp.where(rid == j, jnp.broadcast_to(carry_i, (nseg, cb)), ci)

        def add_carry(i, c):
            cr, ci = _cmul(lr, li, c[0], c[1])
            or_ref[rows(i), :] = or_ref[rows(i), :] + cr
            oi_ref[rows(i), :] = oi_ref[rows(i), :] + ci
            return cr, ci

        lax.fori_loop(0, seg, add_carry, (cr, ci))

    assert seg & (seg - 1) == 0
    blk = pl.BlockSpec((L, cb), lambda j: (0, j))
    lam = pl.BlockSpec((1, cb), lambda j: (0, j))
    return pl.pallas_call(
        body, name="s5_scan_bwd" if reverse else "s5_scan_fwd", grid=(N_STATE // cb,),
        in_specs=[blk, blk, lam, lam], out_specs=[blk, blk],
        out_shape=[_sds((L, N_STATE))] * 2,
        input_output_aliases={0: 0, 1: 1},
        compiler_params=_params("parallel"),
    )(xr, xi, lam_r, lam_i)


def _rwkv_pre_fn(r, k, v, wa, w0, w2p, a0, a2p, k_k, k_a, ee):
    w = -_softplus(-(w0 + mm_hi(jnp.tanh(wa), w2p))) - 0.5
    logw = -jnp.exp(w)
    a = _sigmoid(a0 + mm_hi(wa, a2p))
    kkp = k * k_k
    kk = kkp / jnp.maximum(jnp.sqrt(mm_hi(kkp * kkp, ee)), 1e-12)
    k2 = k * (1.0 + (a - 1.0) * k_a)
    return r, logw, k2, v, -kk, kk * a


def _shifted(rw, prev_blk, first):
    rolled = pltpu.roll(rw, 1, axis=0)
    prev_row = jnp.where(first, 0.0, prev_blk[7:8, :])
    rid = lax.broadcasted_iota(jnp.int32, rw.shape, 0)
    return jnp.where(rid == 0, jnp.broadcast_to(prev_row, rw.shape), rolled)


def _split_rw(t):
    return t[:, 0:512], t[:, 512:1024], t[:, 1024:1536], t[:, 1536:1664]


def _rwkv_pre_specs(tt):
    row = pl.BlockSpec((tt, D_SHIFT), lambda i: (i, 0))
    prev = pl.BlockSpec((8, D_SHIFT), lambda i: (jnp.maximum(i * (tt // 8) - 1, 0), 0))
    consts = [_const_spec((1, D_SHIFT)), _const_spec((1, D_RWKV)), _const_spec((LANES, D_RWKV)),
              _const_spec((1, D_RWKV)), _const_spec((LANES, D_RWKV)), _const_spec((1, D_RWKV)),
              _const_spec((1, D_RWKV)), _const_spec((D_RWKV, D_RWKV))]
    return [row, prev] + consts


def _rwkv_pre_fwd(rw, mu, w0, w2p, a0, a2p, k_k, k_a, ee, tt):
    L = rw.shape[0]

    def body(rw_ref, prev_ref, mu_ref, w0_ref, w2_ref, a0_ref, a2_ref, kk_ref, ka_ref, ee_ref, *outs):
        rwv = rw_ref[...]
        rws = rwv + (_shifted(rwv, prev_ref[...], pl.program_id(0) == 0) - rwv) * mu_ref[...]
        res = _rwkv_pre_fn(*_split_rw(rws), w0_ref[...], w2_ref[...], a0_ref[...], a2_ref[...],
                           kk_ref[...], ka_ref[...], ee_ref[...])
        for o, val in zip(outs, res):
            o[...] = val

    row = pl.BlockSpec((tt, D_RWKV), lambda i: (i, 0))
    return pl.pallas_call(
        body, name="rwkv_pre_fwd", grid=(L // tt,),
        in_specs=_rwkv_pre_specs(tt), out_specs=[row] * 6, out_shape=[_sds((L, D_RWKV))] * 6,
        compiler_params=_params("parallel"),
    )(rw, rw, mu, w0, w2p, a0, a2p, k_k, k_a, ee)


def _rwkv_pre_bwd(rw, mu, w0, w2p, a0, a2p, k_k, k_a, ee, cots, tt):
    L = rw.shape[0]
    n_t = L // tt

    def body(rw_ref, prev_ref, mu_ref, w0_ref, w2_ref, a0_ref, a2_ref, kk_ref, ka_ref, ee_ref,
             c_r, c_w, c_k, c_v, c_a, c_b, cb_r, cb_k, cb_v,
             drws_ref, dmu_o, dw0_o, dw2_o, da0_o, da2_o, dkk_o, dka_o,
             dmu, dw0, dw2, da0, da2, dkk, dka):
        i = pl.program_id(0)
        accs = (dmu, dw0, dw2, da0, da2, dkk, dka)

        @pl.when(i == 0)
        def _():
            for acc in accs:
                acc[...] = jnp.zeros_like(acc)

        rwv = rw_ref[...]
        diff = _shifted(rwv, prev_ref[...], i == 0) - rwv
        rws = rwv + diff * mu_ref[...]
        consts = (w0_ref[...], w2_ref[...], a0_ref[...], a2_ref[...], kk_ref[...], ka_ref[...])
        _, vjp = jax.vjp(lambda *a: _rwkv_pre_fn(*a, ee_ref[...]), *_split_rw(rws), *consts)
        g = vjp((c_r[...] + cb_r[...], c_w[...], c_k[...] + cb_k[...], c_v[...] + cb_v[...], c_a[...], c_b[...]))
        drws = jnp.concatenate(g[0:4], axis=-1)
        drws_ref[...] = drws
        dmu[...] += jnp.sum(drws * diff, axis=0, keepdims=True)
        for acc, val in zip(accs[1:], g[4:]):
            acc[...] += val

        @pl.when(i == n_t - 1)
        def _():
            for acc, out in zip(accs, (dmu_o, dw0_o, dw2_o, da0_o, da2_o, dkk_o, dka_o)):
                out[...] = acc[...]

    row = pl.BlockSpec((tt, D_RWKV), lambda i: (i, 0))
    shapes = [(1, D_SHIFT), (1, D_RWKV), (LANES, D_RWKV), (1, D_RWKV), (LANES, D_RWKV), (1, D_RWKV), (1, D_RWKV)]
    return pl.pallas_call(
        body, name="rwkv_pre_bwd", grid=(n_t,),
        in_specs=_rwkv_pre_specs(tt) + [row] * 9,
        out_specs=[pl.BlockSpec((tt, D_SHIFT), lambda i: (i, 0))] + [_acc_spec(s) for s in shapes],
        out_shape=[_sds((L, D_SHIFT))] + [_sds(s) for s in shapes],
        scratch_shapes=[pltpu.VMEM(s, F32) for s in shapes],
        compiler_params=_params("arbitrary"),
    )(rw, rw, mu, w0, w2p, a0, a2p, k_k, k_a, ee, *cots)


def _bmm(a, b):
    return lax.dot_general(a, b, (((2,), (1,)), ((0,), (0,))), precision=HI, preferred_element_type=F32)


def _bmm_nt(a, b):
    return lax.dot_general(a, b, (((2,), (2,)), ((0,), (0,))), precision=HI, preferred_element_type=F32)


def _bmm_tn(a, b):
    return lax.dot_general(a, b, (((1,), (1,)), ((0,), (0,))), precision=HI, preferred_element_type=F32)


def _rwkv_chunk(st0, r, logw, k, v, a, b):
    n_h, t, _ = r.shape
    ti = lax.broadcasted_iota(jnp.int32, (t, t), 0)
    si = lax.broadcasted_iota(jnp.int32, (t, t), 1)
    incl = (ti >= si)[None]
    strict = (ti > si)[None]
    ones_tri = jnp.broadcast_to(jnp.where(ti >= si, 1.0, 0.0)[None], (n_h, t, t))
    log_p = _bmm(ones_tri, logw)
    p_in = jnp.exp(log_p)
    p_inv = jnp.exp(-log_p)
    at = a * jnp.exp(log_p - logw)
    rt = r * p_in
    bt = b * p_inv
    kt = k * p_inv
    a_ab = jnp.where(strict, _bmm_nt(at, bt), 0.0)
    a_ak = jnp.where(strict, _bmm_nt(at, kt), 0.0)
    inv = jnp.where(ti == si, 1.0, 0.0)[None] + a_ab
    pw = a_ab
    n = 1
    while 2 * n < t:
        pw = _bmm(pw, pw)
        inv = inv + _bmm(inv, pw)
        n *= 2
    u = _bmm(inv, _bmm(at, st0) + _bmm(a_ak, v))
    y = (_bmm(rt, st0) + _bmm(jnp.where(incl, _bmm_nt(rt, bt), 0.0), u)
         + _bmm(jnp.where(incl, _bmm_nt(rt, kt), 0.0), v))
    p_end = jnp.swapaxes(p_in[:, t - 1:t, :], 1, 2)
    st1 = (st0 + _bmm_tn(bt, u) + _bmm_tn(kt, v)) * p_end
    return y, st1


def _rwkv_scan_fwd(ops):
    n_h, L, n = ops[0].shape
    t = RWKV_CHUNK
    n_c = L // t

    def body(r_ref, w_ref, k_ref, v_ref, a_ref, b_ref, y_ref, st_ref, st):
        @pl.when(pl.program_id(0) == 0)
        def _():
            st[...] = jnp.zeros_like(st)

        st0 = st[...]
        st_ref[0] = st0
        y, st1 = _rwkv_chunk(st0, r_ref[...], w_ref[...], k_ref[...], v_ref[...], a_ref[...], b_ref[...])
        y_ref[...] = y
        st[...] = st1

    blk = pl.BlockSpec((n_h, t, n), lambda c: (0, c, 0))
    return pl.pallas_call(
        body, name="rwkv_scan_fwd", grid=(n_c,), in_specs=[blk] * 6,
        out_specs=[blk, pl.BlockSpec((1, n_h, n, n), lambda c: (c, 0, 0, 0))],
        out_shape=[_sds((n_h, L, n)), _sds((n_c, n_h, n, n))],
        scratch_shapes=[pltpu.VMEM((n_h, n, n), F32)],
        compiler_params=_params("arbitrary"),
    )(*ops)


def _rwkv_scan_bwd(ops, states, dy):
    n_h, L, n = ops[0].shape
    t = RWKV_CHUNK
    n_c = L // t

    def body(r_ref, w_ref, k_ref, v_ref, a_ref, b_ref, st_ref, dy_ref, dr, dw, dk, dv, da, db, dst):
        @pl.when(pl.program_id(0) == 0)
        def _():
            dst[...] = jnp.zeros_like(dst)

        _, vjp = jax.vjp(_rwkv_chunk, st_ref[0], r_ref[...], w_ref[...], k_ref[...], v_ref[...],
                         a_ref[...], b_ref[...])
        g = vjp((dy_ref[...], dst[...]))
        dst[...] = g[0]
        for out, val in zip((dr, dw, dk, dv, da, db), g[1:]):
            out[...] = val

    blk = pl.BlockSpec((n_h, t, n), lambda c: (0, n_c - 1 - c, 0))
    return pl.pallas_call(
        body, name="rwkv_scan_bwd", grid=(n_c,),
        in_specs=[blk] * 6 + [pl.BlockSpec((1, n_h, n, n), lambda c: (n_c - 1 - c, 0, 0, 0)), blk],
        out_specs=[blk] * 6, out_shape=[_sds((n_h, L, n))] * 6,
        scratch_shapes=[pltpu.VMEM((n_h, n, n), F32)],
        compiler_params=_params("arbitrary"),
    )(*ops, states, dy)


def _post_fn(x, u, zs, zr, ysc, r, k2, v, s_re, s_im, c_re, c_im, d, glu_w, glu_b, ln_w, ln_b, r_k,
             wo_s5, wo_rwkv, gf, tgt, ee):
    y_ssm = jnp.concatenate(
        [mm_hi(s_re[q], c_re[q]) - mm_hi(s_im[q], c_im[q]) for q in range(S5_BLOCKS)], axis=-1)
    y3 = _gelu(y_ssm + d * u)
    y_s5 = y3 * _sigmoid(mm_bf(y3, glu_w) + glu_b) * _silu(zs)
    mean = mm_hi(ysc, ee) * (1.0 / HEAD)
    yc = ysc - mean
    var = mm_hi(yc * yc, ee) * (1.0 / HEAD)
    gn = yc * lax.rsqrt(var + GN_EPS) * ln_w + ln_b
    bonus = mm_hi(r * k2 * r_k, ee) * v
    y_rwkv = (gn + bonus) * _silu(zr)
    x2 = x + mm_bf(y_s5, wo_s5) + mm_bf(y_rwkv, wo_rwkv)
    err = _rms(x2, gf) - tgt
    return 0.5 * jnp.mean(err * err, axis=-1, keepdims=True)


def _post(x, u, zs, zr, ysc, r, k2, v, s_re, s_im, c4_re, c4_im, d, glu_w, glu_b, ln_w, ln_b, r_k,
          w_out, gf, tgt, ee, tt):
    L = x.shape[0]
    n_t = L // tt
    acc_shapes = [(S5_BLOCKS, 512, LANES), (S5_BLOCKS, 512, LANES), (1, D_S5), (D_S5, D_S5), (1, D_S5),
                  (1, D_RWKV), (1, D_RWKV), (1, D_RWKV), (D_MODEL, D_MODEL), (1, D_MODEL), (8, LANES)]

    def body(x_ref, u_ref, zs_ref, zr_ref, ysc_ref, r_ref, k2_ref, v_ref, sre_ref, sim_ref,
             cre_ref, cim_ref, d_ref, gw_ref, gb_ref, lw_ref, lb_ref, rk_ref, wo_ref, gf_ref, tgt_ref, ee_ref,
             dx_o, du_o, dzs_o, dzr_o, dysc_o, dr_o, dk2_o, dv_o, gre_o, gim_o,
             dcre_o, dcim_o, dd_o, dgw_o, dgb_o, dlw_o, dlb_o, drk_o, dwo_o, dgf_o, loss_o,
             dcre, dcim, dd, dgw, dgb, dlw, dlb, drk, dwo, dgf, loss):
        i = pl.program_id(0)
        accs = (dcre, dcim, dd, dgw, dgb, dlw, dlb, drk, dwo, dgf, loss)

        @pl.when(i == 0)
        def _():
            for acc in accs:
                acc[...] = jnp.zeros_like(acc)

        blocks = [slice(q * 512, (q + 1) * 512) for q in range(S5_BLOCKS)]
        args = (x_ref[...], u_ref[...], zs_ref[...], zr_ref[...], ysc_ref[...], r_ref[...], k2_ref[...], v_ref[...],
                [sre_ref[:, c] for c in blocks], [sim_ref[:, c] for c in blocks],
                [cre_ref[q] for q in range(S5_BLOCKS)], [cim_ref[q] for q in range(S5_BLOCKS)],
                d_ref[...], gw_ref[...], gb_ref[...], lw_ref[...], lb_ref[...], rk_ref[...],
                wo_ref[0:D_S5, :], wo_ref[D_S5:D_MODEL, :], gf_ref[...])
        rows, vjp = jax.vjp(lambda *a: _post_fn(*a, tgt_ref[...], ee_ref[...]), *args)
        g = vjp(jnp.ones_like(rows))
        for out, val in zip((dx_o, du_o, dzs_o, dzr_o, dysc_o, dr_o, dk2_o, dv_o), g[0:8]):
            out[...] = val
        for q in range(S5_BLOCKS):
            gre_o[:, blocks[q]] = g[8][q]
            gim_o[:, blocks[q]] = g[9][q]
            dcre[q] += g[10][q]
            dcim[q] += g[11][q]
        for acc, val in zip((dd, dgw, dgb, dlw, dlb, drk), g[12:18]):
            acc[...] += val
        dwo[0:D_S5, :] += g[18]
        dwo[D_S5:D_MODEL, :] += g[19]
        dgf[...] += g[20]
        loss[...] += jnp.broadcast_to(jnp.sum(rows, axis=0, keepdims=True), loss.shape)

        @pl.when(i == n_t - 1)
        def _():
            for acc, out in zip(accs, (dcre_o, dcim_o, dd_o, dgw_o, dgb_o, dlw_o, dlb_o, drk_o, dwo_o, dgf_o, loss_o)):
                pltpu.sync_copy(acc, out)

    row = lambda n: pl.BlockSpec((tt, n), lambda i: (i, 0))
    in_specs = ([row(D_MODEL)] + [row(512)] * 7 + [row(N_STATE)] * 2
                + [_const_spec((S5_BLOCKS, 512, LANES))] * 2
                + [_const_spec(s) for s in [(1, D_S5), (D_S5, D_S5), (1, D_S5), (1, D_RWKV), (1, D_RWKV), (1, D_RWKV),
                                            (D_MODEL, D_MODEL), (1, D_MODEL)]]
                + [row(D_MODEL), _const_spec((D_RWKV, D_RWKV))])
    out_rows = [D_MODEL] + [512] * 7 + [N_STATE] * 2
    return pl.pallas_call(
        body, name="post_fwd_bwd", grid=(n_t,), in_specs=in_specs,
        out_specs=[row(n) for n in out_rows] + [_ANY] * len(acc_shapes),
        out_shape=[_sds((L, n)) for n in out_rows] + [_sds(s) for s in acc_shapes],
        scratch_shapes=[pltpu.VMEM(s, F32) for s in acc_shapes],
        compiler_params=_params("arbitrary"),
    )(x, u, zs, zr, ysc, r, k2, v, s_re, s_im, c4_re, c4_im, d, glu_w, glu_b, ln_w, ln_b, r_k, w_out, gf, tgt, ee)


def _s5_bwd(u, du_direct, s_re, s_im, g_re, g_im, b4_re, b4_im, tt):
    L = u.shape[0]
    n_t = L // tt
    acc_shapes = [(S5_BLOCKS, LANES, 512), (S5_BLOCKS, LANES, 512), (1, N_STATE), (1, N_STATE)]

    def body(u_ref, dud_ref, sre_ref, sim_ref, gre_ref, gim_ref, bre_ref, bim_ref,
             du_o, dbre_o, dbim_o, dlr_o, dli_o, dbre, dbim, dlr, dli, last_r, last_i):
        i = pl.program_id(0)

        @pl.when(i == 0)
        def _():
            for acc in (dbre, dbim, dlr, dli, last_r, last_i):
                acc[...] = jnp.zeros_like(acc)

        uv = u_ref[...]
        gr = gre_ref[...]
        gi = gim_ref[...]
        pieces = []
        for q in range(S5_BLOCKS):
            cols = slice(q * 512, (q + 1) * 512)
            uq = uv[:, q * LANES:(q + 1) * LANES]
            pieces.append(_dot(gr[:, cols], bre_ref[q], ((1,), (1,)), HI) + _dot(gi[:, cols], bim_ref[q], ((1,), (1,)), HI))
            dbre[q] += _dot(uq, gr[:, cols], ((0,), (0,)), HI)
            dbim[q] += _dot(uq, gi[:, cols], ((0,), (0,)), HI)
        du_o[...] = dud_ref[...] + jnp.concatenate(pieces, axis=-1)
        sr = sre_ref[...]
        si = sim_ref[...]
        rid = lax.broadcasted_iota(jnp.int32, sr.shape, 0)
        pr = jnp.where(rid == 0, jnp.broadcast_to(last_r[7:8, :], sr.shape), pltpu.roll(sr, 1, axis=0))
        pi_ = jnp.where(rid == 0, jnp.broadcast_to(last_i[7:8, :], si.shape), pltpu.roll(si, 1, axis=0))
        dlr[...] += jnp.sum(pr * gr + pi_ * gi, axis=0, keepdims=True)
        dli[...] += jnp.sum(pr * gi - pi_ * gr, axis=0, keepdims=True)
        last_r[...] = sr[tt - 8:tt, :]
        last_i[...] = si[tt - 8:tt, :]

        @pl.when(i == n_t - 1)
        def _():
            for acc, out in zip((dbre, dbim, dlr, dli), (dbre_o, dbim_o, dlr_o, dli_o)):
                out[...] = acc[...]

    row = lambda n: pl.BlockSpec((tt, n), lambda i: (i, 0))
    return pl.pallas_call(
        body, name="s5_bwd", grid=(n_t,),
        in_specs=[row(D_S5), row(D_S5)] + [row(N_STATE)] * 4 + [_const_spec((S5_BLOCKS, LANES, 512))] * 2,
        out_specs=[row(D_S5)] + [_acc_spec(s) for s in acc_shapes],
        out_shape=[_sds((L, D_S5))] + [_sds(s) for s in acc_shapes],
        scratch_shapes=[pltpu.VMEM(s, F32) for s in acc_shapes] + [pltpu.VMEM((8, N_STATE), F32)] * 2,
        compiler_params=_params("arbitrary"),
    )(u, du_direct, s_re, s_im, g_re, g_im, b4_re, b4_im)


def _bwd_in(x, norm_g, w_in_bf, mu, dx2, du, dzs, drws, dzr, tt):
    L = x.shape[0]
    n_t = L // tt

    def body(x_ref, g_ref, w_ref, mu_ref, dx2_ref, du_ref, dzs_ref, drws_ref, nxt_ref, dzr_ref,
             gx_o, dproj_o, dg_o, dg):
        i = pl.program_id(0)

        @pl.when(i == 0)
        def _():
            dg[...] = jnp.zeros_like(dg)

        drws_v = drws_ref[...]
        rid = lax.broadcasted_iota(jnp.int32, drws_v.shape, 0)
        nxt_row = jnp.where(i == n_t - 1, 0.0, nxt_ref[0:1, :])
        nxt = jnp.where(rid == tt - 1, jnp.broadcast_to(nxt_row, drws_v.shape), pltpu.roll(drws_v, tt - 1, axis=0))
        muv = mu_ref[...]
        drw = drws_v * (1.0 - muv) + nxt * muv
        dproj_o[:, 0:D_S5] = du_ref[...]
        dproj_o[:, D_S5:2 * D_S5] = dzs_ref[...]
        dproj_o[:, 2 * D_S5:2 * D_S5 + D_SHIFT] = drw
        dproj_o[:, 2 * D_S5 + D_SHIFT:D_IN] = dzr_ref[...]
        dh = _dot(dproj_o[...].astype(BF16), w_ref[...], ((1,), (1,)), None)
        _, vjp = jax.vjp(_rms, x_ref[...], g_ref[...])
        dxh, dgv = vjp(dh)
        gx_o[...] = dx2_ref[...] + dxh
        dg[...] += dgv

        @pl.when(i == n_t - 1)
        def _():
            dg_o[...] = dg[...]

    row = lambda n: pl.BlockSpec((tt, n), lambda i: (i, 0))
    nxt = pl.BlockSpec((8, D_SHIFT), lambda i: (jnp.minimum((i + 1) * (tt // 8), L // 8 - 1), 0))
    return pl.pallas_call(
        body, name="bwd_in", grid=(n_t,),
        in_specs=[row(D_MODEL), _const_spec((1, D_MODEL)), _const_spec((D_MODEL, D_IN)), _const_spec((1, D_SHIFT)),
                  row(D_MODEL), row(D_S5), row(D_S5), row(D_SHIFT), nxt, row(D_RWKV)],
        out_specs=[row(D_MODEL), row(D_IN), _acc_spec((1, D_MODEL))],
        out_shape=[_sds((L, D_MODEL)), _sds((L, D_IN)), _sds((1, D_MODEL))],
        scratch_shapes=[pltpu.VMEM((1, D_MODEL), F32)],
        compiler_params=_params("arbitrary"),
    )(x, norm_g, w_in_bf, mu, dx2, du, dzs, drws, drws, dzr)


def _grad_w_in(x, norm_g, dproj, tt, cn):
    L = x.shape[0]

    def body(x_ref, g_ref, dp_ref, out_ref):
        @pl.when(pl.program_id(1) == 0)
        def _():
            out_ref[...] = jnp.zeros_like(out_ref)

        h = _rms(x_ref[...], g_ref[...])
        out_ref[...] += _dot(h.astype(BF16), dp_ref[...].astype(BF16), ((0,), (0,)), None)

    return pl.pallas_call(
        body, name="grad_w_in", grid=(D_IN // cn, L // tt),
        in_specs=[pl.BlockSpec((tt, D_MODEL), lambda j, i: (i, 0)), pl.BlockSpec((1, D_MODEL), lambda j, i: (0, 0)),
                  pl.BlockSpec((tt, cn), lambda j, i: (i, j))],
        out_specs=pl.BlockSpec((D_MODEL, cn), lambda j, i: (0, j)),
        out_shape=_sds((D_MODEL, D_IN)),
        compiler_params=pltpu.CompilerParams(dimension_semantics=("parallel", "arbitrary"), vmem_limit_bytes=VMEM_LIMIT),
    )(x, norm_g, dproj)


def _to_heads(t):
    return t.reshape(t.shape[0], N_HEADS, HEAD).transpose(1, 0, 2)


def _from_heads(t):
    return t.transpose(1, 0, 2).reshape(t.shape[1], N_HEADS * HEAD)


def _block_diag_b(bbar):
    bb = bbar.reshape(S5_GROUP, S5_BLOCKS, 8, S5_STATE)
    return jnp.einsum('hqgp,Gg->qGhgp', bb, jnp.eye(8, dtype=F32)).reshape(S5_BLOCKS, LANES, 512)


def _block_diag_b_t(db4):
    d = db4.reshape(S5_BLOCKS, 8, S5_GROUP, 8, S5_STATE)
    return jnp.einsum('qGhgp,Gg->hqgp', d, jnp.eye(8, dtype=F32)).reshape(S5_GROUP, N_STATE)


def _block_diag_c(c):
    cc = c.reshape(S5_BLOCKS, 8, S5_GROUP, S5_STATE)
    return jnp.einsum('qghp,gG->qgpGh', cc, jnp.eye(8, dtype=F32)).reshape(S5_BLOCKS, 512, LANES)


def _block_diag_c_t(dc4):
    d = dc4.reshape(S5_BLOCKS, 8, S5_STATE, 8, S5_GROUP)
    return jnp.einsum('qgpGh,gG->qghp', d, jnp.eye(8, dtype=F32)).reshape(S5_GROUPS, S5_GROUP, S5_STATE)


def _local_step(x, tgt, w):
    L = x.shape[0]
    tt = min(256, L)
    tp = min(128, L)
    ee = _head_sum_matrix()

    lam_re = w['s5_lam_re'].reshape(1, N_STATE)
    lam_im = w['s5_lam_im'].reshape(1, N_STATE)
    logdt = jnp.repeat(w['s5_log_dt'], S5_STATE).reshape(1, N_STATE)
    b_re_t = w['s5_b_re'].transpose(2, 0, 1).reshape(S5_GROUP, N_STATE)
    b_im_t = w['s5_b_im'].transpose(2, 0, 1).reshape(S5_GROUP, N_STATE)
    lbr, lbi, bbr, bbi = _s5_param_fwd(lam_re, lam_im, logdt, b_re_t, b_im_t)
    b4_re, b4_im = _block_diag_b(bbr), _block_diag_b(bbi)
    c4_re, c4_im = _block_diag_c(w['s5_c_re']), _block_diag_c(w['s5_c_im'])

    norm_g = w['norm_g'].reshape(1, D_MODEL)
    w_in_bf = w['w_in'].astype(BF16)
    u, zs, rw, zr, bu_re, bu_im = _fwd_in(x, norm_g, w_in_bf, b4_re, b4_im, tt)
    s_re, s_im = _s5_scan(bu_re, bu_im, lbr, lbi, reverse=False)

    row = lambda t: t.reshape(1, -1)
    zpad = jnp.zeros((HEAD, D_RWKV), F32)
    w2p = jnp.concatenate([w['rwkv_w2'], zpad], axis=0)
    a2p = jnp.concatenate([zpad, w['rwkv_a2']], axis=0)
    pre_consts = (row(w['rwkv_mu']), row(w['rwkv_w0']), w2p, row(w['rwkv_a0']), a2p,
                  row(w['rwkv_k_k']), row(w['rwkv_k_a']), ee)
    ops = _rwkv_pre_fwd(rw, *pre_consts, tt)
    ops_h = [_to_heads(t) for t in ops]
    ysc_h, states = _rwkv_scan_fwd(ops_h)
    ysc = _from_heads(ysc_h)

    post = _post(x, u, zs, zr, ysc, ops[0], ops[2], ops[3], s_re, s_im, c4_re, c4_im,
                 row(w['s5_d']), w['s5_glu_w'], row(w['s5_glu_b']), row(w['rwkv_ln_w']), row(w['rwkv_ln_b']),
                 row(w['rwkv_r_k']), w['w_out'], row(w['final_g']), tgt, ee, tp)
    (dx2, du_d, dzs, dzr, dysc, dr_b, dk2_b, dv_b, g_re, g_im,
     dc4_re, dc4_im, dd, dglu_w, dglu_b, dln_w, dln_b, dr_k, dw_out, dgf, loss) = post

    gt_re, gt_im = _s5_scan(g_re, g_im, lbr, -lbi, reverse=True)
    du, db4_re, db4_im, dlbr, dlbi = _s5_bwd(u, du_d, s_re, s_im, gt_re, gt_im, b4_re, b4_im, tt)
    group_ind = (jnp.arange(N_STATE)[:, None] // S5_STATE == jnp.arange(LANES)[None, :]).astype(F32)
    dlam_re, dlam_im, dlogdt, db_re_t, db_im_t = _s5_param_bwd(
        lam_re, lam_im, logdt, b_re_t, b_im_t, dlbr, dlbi, _block_diag_b_t(db4_re), _block_diag_b_t(db4_im), group_ind)

    scan_g = _rwkv_scan_bwd(ops_h, states, _to_heads(dysc))
    cots = [_from_heads(t) for t in scan_g] + [dr_b, dk2_b, dv_b]
    drws, dmu, dw0, dw2p, da0, da2p, dk_k, dk_a = _rwkv_pre_bwd(rw, *pre_consts, cots, tt)

    grad_x, dproj, dnorm_g = _bwd_in(x, norm_g, w_in_bf, row(w['rwkv_mu']), dx2, du, dzs, drws, dzr, tt)
    dw_in = _grad_w_in(x, norm_g, dproj, tt, 640)

    unb = lambda t: t.reshape(S5_GROUP, S5_GROUPS, S5_STATE).transpose(1, 2, 0)
    grads = {
        'norm_g': dnorm_g.reshape(D_MODEL), 'w_in': dw_in,
        's5_lam_re': dlam_re.reshape(S5_GROUPS, S5_STATE), 's5_lam_im': dlam_im.reshape(S5_GROUPS, S5_STATE),
        's5_log_dt': dlogdt[0, :S5_GROUPS], 's5_b_re': unb(db_re_t), 's5_b_im': unb(db_im_t),
        's5_c_re': _block_diag_c_t(dc4_re), 's5_c_im': _block_diag_c_t(dc4_im),
        's5_d': dd.reshape(D_S5), 's5_glu_w': dglu_w, 's5_glu_b': dglu_b.reshape(D_S5),
        'rwkv_mu': dmu.reshape(-1), 'rwkv_w0': dw0.reshape(-1), 'rwkv_w2': dw2p[:HEAD], 'rwkv_a0': da0.reshape(-1),
        'rwkv_a2': da2p[HEAD:], 'rwkv_k_k': dk_k.reshape(-1), 'rwkv_k_a': dk_a.reshape(-1),
        'rwkv_r_k': dr_k.reshape(N_HEADS, HEAD), 'rwkv_ln_w': dln_w.reshape(-1), 'rwkv_ln_b': dln_b.reshape(-1),
        'w_out': dw_out, 'final_g': dgf.reshape(D_MODEL),
    }
    return loss, grad_x, grads


def _exchange(send, gather, name):
    rows = send.shape[-2]

    def body(send_ref, recv_ref, send_sems, recv_sems, local_sem):
        pos = (lax.axis_index("x"), lax.axis_index("y"), lax.axis_index("c"))
        me = 4 * pos[0] + 2 * pos[1] + pos[2]

        def block_for(dev):
            return send_ref if gather else send_ref.at[dev]

        own = pltpu.make_async_copy(block_for(me), recv_ref.at[me], local_sem)
        own.start()
        copies = []
        for k in range(1, N_DEV):
            peer = tuple(1 - p if (k >> (2 - axis)) & 1 else p for axis, p in enumerate(pos))
            peer_id = 4 * peer[0] + 2 * peer[1] + peer[2]
            out = pltpu.make_async_remote_copy(
                src_ref=block_for(peer_id), dst_ref=recv_ref.at[me],
                send_sem=send_sems.at[k - 1], recv_sem=recv_sems.at[k - 1],
                device_id=peer, device_id_type=pl.DeviceIdType.MESH)
            out.start()
            arrival = pltpu.make_async_remote_copy(
                src_ref=block_for(peer_id), dst_ref=recv_ref.at[peer_id],
                send_sem=send_sems.at[k - 1], recv_sem=recv_sems.at[k - 1],
                device_id=peer, device_id_type=pl.DeviceIdType.MESH)
            copies.append((out, arrival))
        for _, arrival in copies:
            arrival.wait_recv()
        for out, _ in copies:
            out.wait_send()
        own.wait()

    return pl.pallas_call(
        body, name=name, in_specs=[_ANY], out_specs=_ANY,
        out_shape=_sds((N_DEV, rows, LANES)),
        scratch_shapes=[pltpu.SemaphoreType.DMA((N_DEV - 1,)), pltpu.SemaphoreType.DMA((N_DEV - 1,)),
                        pltpu.SemaphoreType.DMA],
        compiler_params=pltpu.CompilerParams(has_side_effects=True),
    )(send)


def _reduce_adamw(recv, w, m, v, tr):
    rows = w.shape[0]

    def body(recv_ref, w_ref, m_ref, v_ref, g_o, d_o, m_o, v_o):
        g = recv_ref[0]
        for s in range(1, N_DEV):
            g = g + recv_ref[s]
        m_new = ADAM_B1 * m_ref[...] + (1.0 - ADAM_B1) * g
        v_new = ADAM_B2 * v_ref[...] + (1.0 - ADAM_B2) * (g * g)
        m_hat = m_new / (1.0 - ADAM_B1 ** ADAM_STEP)
        v_hat = v_new / (1.0 - ADAM_B2 ** ADAM_STEP)
        g_o[...] = g
        d_o[...] = -ADAM_LR * (m_hat / (jnp.sqrt(v_hat) + ADAM_EPS) + ADAM_WD * w_ref[...])
        m_o[...] = m_new
        v_o[...] = v_new

    blk = pl.BlockSpec((tr, LANES), lambda i: (i, 0))
    return pl.pallas_call(
        body, name="reduce_adamw", grid=(rows // tr,),
        in_specs=[pl.BlockSpec((N_DEV, tr, LANES), lambda i: (0, i, 0)), blk, blk, blk],
        out_specs=[blk] * 4, out_shape=[_sds((rows, LANES))] * 4,
        compiler_params=_params("parallel"),
    )(recv, w, m, v)


_WEIGHTS = [
    ('norm_g', (1, 1024), False), ('w_in', (1, 1024, 400), True), ('s5_lam_re', (1, 32, 64), False),
    ('s5_lam_im', (1, 32, 64), False), ('s5_log_dt', (1, 32), False), ('s5_b_re', (1, 32, 64, 16), False),
    ('s5_b_im', (1, 32, 64, 16), False), ('s5_c_re', (1, 32, 16, 64), False), ('s5_c_im', (1, 32, 16, 64), False),
    ('s5_d', (1, 512), False), ('s5_glu_w', (1, 64, 512), True), ('s5_glu_b', (1, 512), False),
    ('rwkv_mu', (1, 1664), False), ('rwkv_w0', (1, 512), False), ('rwkv_w2', (1, 64, 64), True),
    ('rwkv_a0', (1, 512), False), ('rwkv_a2', (1, 64, 64), True), ('rwkv_k_k', (1, 512), False),
    ('rwkv_k_a', (1, 512), False), ('rwkv_r_k', (1, 8, 64), False), ('rwkv_ln_w', (1, 512), False),
    ('rwkv_ln_b', (1, 512), False), ('w_out', (1, 128, 1024), True), ('final_g', (1024,), False),
]
_PACK_ORDER = [e for e in _WEIGHTS if e[2]] + [e for e in _WEIGHTS if not e[2]]
_SHARD_ROWS = sum(math.prod(s) for _, s, sh in _WEIGHTS if sh) // LANES
_SMALL_SIZE = sum(math.prod(s) for _, s, sh in _WEIGHTS if not sh)
_PACK_ROWS = 5760
_SMALL_PAD = _PACK_ROWS * LANES - _SHARD_ROWS * LANES - _SMALL_SIZE


def _pack(per_device):
    flat = [per_device[n].reshape(-1) for n, _, _ in _PACK_ORDER] + [jnp.zeros((_SMALL_PAD,), F32)]
    return jnp.concatenate(flat).reshape(_PACK_ROWS, LANES)


def _unpack(packed):
    flat = packed.reshape(-1)
    out, off = {}, 0
    for n, s, _ in _PACK_ORDER:
        size = math.prod(s)
        out[n] = flat[off:off + size].reshape(s)
        off += size
    return out


def _gathered_weights(gathered):
    flat = gathered.reshape(N_DEV, -1)
    out, off = {}, 0
    for n, s, _ in _PACK_ORDER[:5]:
        size = math.prod(s)
        blk = flat[:, off:off + size].reshape((N_DEV,) + s[1:])
        off += size
        if n in ('w_in', 'rwkv_w2', 'rwkv_a2'):
            out[n] = blk.transpose(1, 0, 2).reshape(s[1], N_DEV * s[2])
        else:
            out[n] = blk.reshape(N_DEV * s[1], s[2])
    return out


def _grad_blocks(grads):
    parts = []
    for n, s, sharded in _PACK_ORDER:
        g = grads[n]
        if not sharded:
            parts.append(jnp.broadcast_to(g.reshape(1, -1), (N_DEV, g.size)))
        elif n in ('w_in', 'rwkv_w2', 'rwkv_a2'):
            parts.append(g.reshape(g.shape[0], N_DEV, s[2]).transpose(1, 0, 2).reshape(N_DEV, -1))
        else:
            parts.append(g.reshape(N_DEV, -1))
    parts.append(jnp.zeros((N_DEV, _SMALL_PAD), F32))
    return jnp.concatenate(parts, axis=1).reshape(N_DEV, _PACK_ROWS, LANES)


def kernel(x, norm_g, w_in, s5_lam_re, s5_lam_im, s5_log_dt, s5_b_re, s5_b_im, s5_c_re, s5_c_im, s5_d, s5_glu_w, s5_glu_b, rwkv_mu, rwkv_w0, rwkv_w2, rwkv_a0, rwkv_a2, rwkv_k_k, rwkv_k_a, rwkv_r_k, rwkv_ln_w, rwkv_ln_b, w_out, final_g, loss_target, m_norm_g, m_w_in, m_s5_lam_re, m_s5_lam_im, m_s5_log_dt, m_s5_b_re, m_s5_b_im, m_s5_c_re, m_s5_c_im, m_s5_d, m_s5_glu_w, m_s5_glu_b, m_rwkv_mu, m_rwkv_w0, m_rwkv_w2, m_rwkv_a0, m_rwkv_a2, m_rwkv_k_k, m_rwkv_k_a, m_rwkv_r_k, m_rwkv_ln_w, m_rwkv_ln_b, m_w_out, m_final_g, v_norm_g, v_w_in, v_s5_lam_re, v_s5_lam_im, v_s5_log_dt, v_s5_b_re, v_s5_b_im, v_s5_c_re, v_s5_c_im, v_s5_d, v_s5_glu_w, v_s5_glu_b, v_rwkv_mu, v_rwkv_w0, v_rwkv_w2, v_rwkv_a0, v_rwkv_a2, v_rwkv_k_k, v_rwkv_k_a, v_rwkv_r_k, v_rwkv_ln_w, v_rwkv_ln_b, v_w_out, v_final_g):
    given = dict(locals())
    names = [n for n, _, _ in _WEIGHTS]
    weights = {n: given[n] for n in names}

    shard_rows = _pack(weights)[:_SHARD_ROWS]
    full = _gathered_weights(_exchange(shard_rows, True, "gather_weights"))
    local = {n: (full[n] if sharded else (weights[n][0] if n != 'final_g' else weights[n]))
             for n, _, sharded in _WEIGHTS}

    loss, grad_x, grads = _local_step(x[0], loss_target[0], local)

    recv = _exchange(_grad_blocks(grads), False, "exchange_grads")
    g, delta, m_new, v_new = _reduce_adamw(
        recv, _pack(weights), _pack({n: given['m_' + n] for n in names}), _pack({n: given['v_' + n] for n in names}), 640)
    total = lax.psum(loss[0, 0], ("x", "y", "c"))
    outs = [total, grad_x[None]]
    for packed in (g, delta, m_new, v_new):
        unpacked = _unpack(packed)
        outs += [unpacked[n] for n in names]
    return tuple(outs)
```

```python
import functools
import math

import jax
import jax.numpy as jnp
from jax import lax
from jax.experimental import pallas as pl
from jax.experimental.pallas import tpu as pltpu

F32 = jnp.float32
BF16 = jnp.bfloat16
HI = lax.Precision.HIGH

D_MODEL = 1024
D_S5 = 512
D_RWKV = 512
S5_GROUPS = 32
S5_GROUP = 16
S5_STATE = 64
N_STATE = S5_GROUPS * S5_STATE
N_HEADS = 8
HEAD = 64
D_SHIFT = 3 * D_RWKV + 128
D_IN = 2 * D_S5 + D_SHIFT + D_RWKV
NORM_EPS = 1e-6
GN_EPS = 64e-5
N_DEV = 8
LANES = 128
S5_BLOCKS = 4
S5_SEGMENTS = 32
RWKV_CHUNK = 64
VMEM_LIMIT = 56 * 1024 * 1024

ADAM_LR = 0.001
ADAM_B1 = 0.9
ADAM_B2 = 0.999
ADAM_EPS = 1e-08
ADAM_WD = 0.01
ADAM_STEP = 10


def _dot(a, b, dims, prec):
    return lax.dot_general(a, b, (dims, ((), ())), precision=prec, preferred_element_type=F32)


def _make_mm(cast, prec):
    @jax.custom_vjp
    def mm(a, b):
        return _dot(cast(a), cast(b), ((1,), (0,)), prec)

    def fwd(a, b):
        return mm(a, b), (a, b)

    def bwd(res, g):
        a, b = res
        return (_dot(cast(g), cast(b), ((1,), (1,)), prec), _dot(cast(a), cast(g), ((0,), (0,)), prec))

    mm.defvjp(fwd, bwd)
    return mm


mm_hi = _make_mm(lambda t: t, HI)
mm_bf = _make_mm(lambda t: t.astype(BF16), None)


def _sigmoid(x):
    return 1.0 / (1.0 + jnp.exp(-x))


def _silu(x):
    return x * _sigmoid(x)


def _softplus(x):
    return jnp.maximum(x, 0.0) + jnp.log(1.0 + jnp.exp(-jnp.abs(x)))


def _gelu(x):
    return 0.5 * x * (1.0 + jnp.tanh(math.sqrt(2.0 / math.pi) * (x + 0.044715 * x * x * x)))


def _rms(x, g):
    return x * lax.rsqrt(jnp.mean(x * x, axis=-1, keepdims=True) + NORM_EPS) * g


def _const_spec(shape):
    nd = len(shape)
    return pl.BlockSpec(shape, lambda *_: (0,) * nd, pipeline_mode=pl.Buffered(1))


def _acc_spec(shape):
    nd = len(shape)
    return pl.BlockSpec(shape, lambda *_: (0,) * nd)


def _params(sem):
    return pltpu.CompilerParams(dimension_semantics=(sem,), vmem_limit_bytes=VMEM_LIMIT)


_ANY = pl.BlockSpec(memory_space=pl.ANY)


def _sds(shape):
    return jax.ShapeDtypeStruct(shape, F32)


def _head_sum_matrix():
    i = jnp.arange(D_RWKV) // HEAD
    return (i[:, None] == i[None, :]).astype(F32)


def _s5_param_fn(lam_re, lam_im, logdt, b_re, b_im):
    dt = jnp.exp(logdt)
    mag = jnp.exp(lam_re * dt)
    ang = lam_im * dt
    lbr = mag * jnp.cos(ang)
    lbi = mag * jnp.sin(ang)
    nr = lbr - 1.0
    den = lam_re * lam_re + lam_im * lam_im
    cr = (nr * lam_re + lbi * lam_im) / den
    ci = (lbi * lam_re - nr * lam_im) / den
    return lbr, lbi, cr * b_re - ci * b_im, cr * b_im + ci * b_re


def _s5_param_fwd(lam_re, lam_im, logdt, b_re, b_im):
    def body(lr, li, ld, br, bi, o_lr, o_li, o_br, o_bi):
        outs = _s5_param_fn(lr[...], li[...], ld[...], br[...], bi[...])
        for o, v in zip((o_lr, o_li, o_br, o_bi), outs):
            o[...] = v

    return pl.pallas_call(
        body, name="s5_param_fwd",
        out_shape=[_sds((1, N_STATE))] * 2 + [_sds((S5_GROUP, N_STATE))] * 2,
    )(lam_re, lam_im, logdt, b_re, b_im)


def _s5_param_bwd(lam_re, lam_im, logdt, b_re, b_im, d_lbr, d_lbi, d_bbr, d_bbi, group_ind):
    def body(lr, li, ld, br, bi, g0, g1, g2, g3, ind, o_lr, o_li, o_ld, o_br, o_bi):
        _, vjp = jax.vjp(_s5_param_fn, lr[...], li[...], ld[...], br[...], bi[...])
        d_lr, d_li, d_ld, d_br, d_bi = vjp((g0[...], g1[...], g2[...], g3[...]))
        o_lr[...] = d_lr
        o_li[...] = d_li
        o_ld[...] = _dot(jnp.broadcast_to(d_ld, (8, N_STATE)), ind[...], ((1,), (0,)), HI)
        o_br[...] = d_br
        o_bi[...] = d_bi

    return pl.pallas_call(
        body, name="s5_param_bwd",
        out_shape=[_sds((1, N_STATE))] * 2 + [_sds((8, LANES))] + [_sds((S5_GROUP, N_STATE))] * 2,
    )(lam_re, lam_im, logdt, b_re, b_im, d_lbr, d_lbi, d_bbr, d_bbi, group_ind)


def _fwd_in(x, norm_g, w_in_bf, b4_re, b4_im, tt):
    L = x.shape[0]

    def body(x_ref, g_ref, w_ref, bre_ref, bim_ref, u_ref, zs_ref, rw_ref, zr_ref, bur_ref, bui_ref):
        h = _rms(x_ref[...], g_ref[...])
        proj = jnp.dot(h.astype(BF16), w_ref[...], preferred_element_type=F32)
        u = proj[:, 0:D_S5]
        u_ref[...] = u
        zs_ref[...] = proj[:, D_S5:2 * D_S5]
        rw_ref[...] = proj[:, 2 * D_S5:2 * D_S5 + D_SHIFT]
        zr_ref[...] = proj[:, 2 * D_S5 + D_SHIFT:D_IN]
        for q in range(S5_BLOCKS):
            uq = u[:, q * LANES:(q + 1) * LANES]
            cols = slice(q * 512, (q + 1) * 512)
            bur_ref[:, cols] = _dot(uq, bre_ref[q], ((1,), (0,)), HI)
            bui_ref[:, cols] = _dot(uq, bim_ref[q], ((1,), (0,)), HI)

    row = lambda n: pl.BlockSpec((tt, n), lambda i: (i, 0))
    return pl.pallas_call(
        body, name="fwd_in", grid=(L // tt,),
        in_specs=[row(D_MODEL), _const_spec((1, D_MODEL)), _const_spec((D_MODEL, D_IN)),
                  _const_spec((S5_BLOCKS, LANES, 512)), _const_spec((S5_BLOCKS, LANES, 512))],
        out_specs=[row(D_S5), row(D_S5), row(D_SHIFT), row(D_RWKV), row(N_STATE), row(N_STATE)],
        out_shape=[_sds((L, D_S5)), _sds((L, D_S5)), _sds((L, D_SHIFT)), _sds((L, D_RWKV)),
                   _sds((L, N_STATE)), _sds((L, N_STATE))],
        compiler_params=_params("parallel"),
    )(x, norm_g, w_in_bf, b4_re, b4_im)


def _cmul(ar, ai, br, bi):
    return ar * br - ai * bi, ar * bi + ai * br


def _s5_scan(xr, xi, lam_r, lam_i, reverse):
    L = xr.shape[0]
    nseg = S5_SEGMENTS
    n_g = nseg // 8
    seg = L // nseg
    cb = LANES

    def body(xr_ref, xi_ref, lr_ref, li_ref, or_ref, oi_ref):
        lr = jnp.broadcast_to(lr_ref[...], (8, cb))
        li = jnp.broadcast_to(li_ref[...], (8, cb))

        def rows(i, g):
            return pl.ds(g * 8 * seg + ((seg - 1 - i) if reverse else i), 8, stride=seg)

        def local_scan(i, c):
            out = []
            for g in range(n_g):
                mr, mi = _cmul(lr, li, c[2 * g], c[2 * g + 1])
                sr = mr + xr_ref[rows(i, g), :]
                si = mi + xi_ref[rows(i, g), :]
                or_ref[rows(i, g), :] = sr
                oi_ref[rows(i, g), :] = si
                out += [sr, si]
            return tuple(out)

        zero = jnp.zeros((8, cb), F32)
        ends = lax.fori_loop(0, seg, local_scan, (zero,) * (2 * n_g))

        pr, pi_ = lr[0:1], li[0:1]
        n = 1
        while n < seg:
            pr, pi_ = _cmul(pr, pi_, pr, pi_)
            n *= 2
        rid = lax.broadcasted_iota(jnp.int32, (8, cb), 0)
        carry_r, carry_i = zero[0:1], zero[0:1]
        corr = [zero] * (2 * n_g)
        for j in (range(nseg - 1, -1, -1) if reverse else range(nseg)):
            g, row = divmod(j, 8)
            corr[2 * g] = jnp.where(rid == row, jnp.broadcast_to(carry_r, (8, cb)), corr[2 * g])
            corr[2 * g + 1] = jnp.where(rid == row, jnp.broadcast_to(carry_i, (8, cb)), corr[2 * g + 1])
            mr, mi = _cmul(pr, pi_, carry_r, carry_i)
            carry_r = ends[2 * g][row:row + 1] + mr
            carry_i = ends[2 * g + 1][row:row + 1] + mi

        def add_carry(i, c):
            out = []
            for g in range(n_g):
                cr, ci = _cmul(lr, li, c[2 * g], c[2 * g + 1])
                or_ref[rows(i, g), :] = or_ref[rows(i, g), :] + cr
                oi_ref[rows(i, g), :] = oi_ref[rows(i, g), :] + ci
                out += [cr, ci]
            return tuple(out)

        lax.fori_loop(0, seg, add_carry, tuple(corr))

    assert seg & (seg - 1) == 0
    blk = pl.BlockSpec((L, cb), lambda j: (0, j))
    lam = pl.BlockSpec((1, cb), lambda j: (0, j))
    return pl.pallas_call(
        body, name="s5_scan_bwd" if reverse else "s5_scan_fwd", grid=(N_STATE // cb,),
        in_specs=[blk, blk, lam, lam], out_specs=[blk, blk],
        out_shape=[_sds((L, N_STATE))] * 2,
        input_output_aliases={0: 0, 1: 1},
        compiler_params=_params("parallel"),
    )(xr, xi, lam_r, lam_i)


def _rwkv_pre_fn(r, k, v, wa, w0, w2p, a0, a2p, k_k, k_a, ee):
    w = -_softplus(-(w0 + mm_hi(jnp.tanh(wa), w2p))) - 0.5
    logw = -jnp.exp(w)
    a = _sigmoid(a0 + mm_hi(wa, a2p))
    kkp = k * k_k
    kk = kkp / jnp.maximum(jnp.sqrt(mm_hi(kkp * kkp, ee)), 1e-12)
    k2 = k * (1.0 + (a - 1.0) * k_a)
    return r, logw, k2, v, -kk, kk * a


def _shifted(rw, prev_blk, first):
    rolled = pltpu.roll(rw, 1, axis=0)
    prev_row = jnp.where(first, 0.0, prev_blk[7:8, :])
    rid = lax.broadcasted_iota(jnp.int32, rw.shape, 0)
    return jnp.where(rid == 0, jnp.broadcast_to(prev_row, rw.shape), rolled)


def _split_rw(t):
    return t[:, 0:512], t[:, 512:1024], t[:, 1024:1536], t[:, 1536:1664]


def _rwkv_pre_specs(tt):
    row = pl.BlockSpec((tt, D_SHIFT), lambda i: (i, 0))
    prev = pl.BlockSpec((8, D_SHIFT), lambda i: (jnp.maximum(i * (tt // 8) - 1, 0), 0))
    consts = [_const_spec((1, D_SHIFT)), _const_spec((1, D_RWKV)), _const_spec((LANES, D_RWKV)),
              _const_spec((1, D_RWKV)), _const_spec((LANES, D_RWKV)), _const_spec((1, D_RWKV)),
              _const_spec((1, D_RWKV)), _const_spec((D_RWKV, D_RWKV))]
    return [row, prev] + consts


def _rwkv_pre_fwd(rw, mu, w0, w2p, a0, a2p, k_k, k_a, ee, tt):
    L = rw.shape[0]

    def body(rw_ref, prev_ref, mu_ref, w0_ref, w2_ref, a0_ref, a2_ref, kk_ref, ka_ref, ee_ref, *outs):
        rwv = rw_ref[...]
        rws = rwv + (_shifted(rwv, prev_ref[...], pl.program_id(0) == 0) - rwv) * mu_ref[...]
        res = _rwkv_pre_fn(*_split_rw(rws), w0_ref[...], w2_ref[...], a0_ref[...], a2_ref[...],
                           kk_ref[...], ka_ref[...], ee_ref[...])
        for o, val in zip(outs, res):
            o[...] = val

    row = pl.BlockSpec((tt, D_RWKV), lambda i: (i, 0))
    return pl.pallas_call(
        body, name="rwkv_pre_fwd", grid=(L // tt,),
        in_specs=_rwkv_pre_specs(tt), out_specs=[row] * 6, out_shape=[_sds((L, D_RWKV))] * 6,
        compiler_params=_params("parallel"),
    )(rw, rw, mu, w0, w2p, a0, a2p, k_k, k_a, ee)


def _rwkv_pre_bwd(rw, mu, w0, w2p, a0, a2p, k_k, k_a, ee, cots, tt):
    L = rw.shape[0]
    n_t = L // tt

    def body(rw_ref, prev_ref, mu_ref, w0_ref, w2_ref, a0_ref, a2_ref, kk_ref, ka_ref, ee_ref,
             c_r, c_w, c_k, c_v, c_a, c_b, cb_r, cb_k, cb_v,
             drws_ref, dmu_o, dw0_o, dw2_o, da0_o, da2_o, dkk_o, dka_o,
             dmu, dw0, dw2, da0, da2, dkk, dka):
        i = pl.program_id(0)
        accs = (dmu, dw0, dw2, da0, da2, dkk, dka)

        @pl.when(i == 0)
        def _():
            for acc in accs:
                acc[...] = jnp.zeros_like(acc)

        rwv = rw_ref[...]
        diff = _shifted(rwv, prev_ref[...], i == 0) - rwv
        rws = rwv + diff * mu_ref[...]
        consts = (w0_ref[...], w2_ref[...], a0_ref[...], a2_ref[...], kk_ref[...], ka_ref[...])
        _, vjp = jax.vjp(lambda *a: _rwkv_pre_fn(*a, ee_ref[...]), *_split_rw(rws), *consts)
        g = vjp((c_r[...] + cb_r[...], c_w[...], c_k[...] + cb_k[...], c_v[...] + cb_v[...], c_a[...], c_b[...]))
        drws = jnp.concatenate(g[0:4], axis=-1)
        drws_ref[...] = drws
        dmu[...] += jnp.sum(drws * diff, axis=0, keepdims=True)
        for acc, val in zip(accs[1:], g[4:]):
            acc[...] += val

        @pl.when(i == n_t - 1)
        def _():
            for acc, out in zip(accs, (dmu_o, dw0_o, dw2_o, da0_o, da2_o, dkk_o, dka_o)):
                out[...] = acc[...]

    row = pl.BlockSpec((tt, D_RWKV), lambda i: (i, 0))
    shapes = [(1, D_SHIFT), (1, D_RWKV), (LANES, D_RWKV), (1, D_RWKV), (LANES, D_RWKV), (1, D_RWKV), (1, D_RWKV)]
    return pl.pallas_call(
        body, name="rwkv_pre_bwd", grid=(n_t,),
        in_specs=_rwkv_pre_specs(tt) + [row] * 9,
        out_specs=[pl.BlockSpec((tt, D_SHIFT), lambda i: (i, 0))] + [_acc_spec(s) for s in shapes],
        out_shape=[_sds((L, D_SHIFT))] + [_sds(s) for s in shapes],
        scratch_shapes=[pltpu.VMEM(s, F32) for s in shapes],
        compiler_params=_params("arbitrary"),
    )(rw, rw, mu, w0, w2p, a0, a2p, k_k, k_a, ee, *cots)


def _bmm(a, b):
    return lax.dot_general(a, b, (((2,), (1,)), ((0,), (0,))), precision=HI, preferred_element_type=F32)


def _bmm_nt(a, b):
    return lax.dot_general(a, b, (((2,), (2,)), ((0,), (0,))), precision=HI, preferred_element_type=F32)


def _bmm_tn(a, b):
    return lax.dot_general(a, b, (((1,), (1,)), ((0,), (0,))), precision=HI, preferred_element_type=F32)


def _unit_lower_inverse(a):
    t = a.shape[-1]
    ti = lax.broadcasted_iota(jnp.int32, (t, t), 0)
    si = lax.broadcasted_iota(jnp.int32, (t, t), 1)
    inv = jnp.where(ti == si, 1.0, 0.0)[None] + a
    pw = a
    n = 1
    while 2 * n < t:
        pw = _bmm(pw, pw)
        inv = inv + _bmm(inv, pw)
        n *= 2
    return inv


@jax.custom_vjp
def _solve_unit_lower(a, rhs, inv):
    return _bmm(inv, rhs)


def _solve_fwd(a, rhs, inv):
    u = _bmm(inv, rhs)
    return u, (inv, u)


def _solve_bwd(res, du):
    inv, u = res
    d_rhs = _bmm_tn(inv, du)
    return _bmm_nt(d_rhs, u), d_rhs, jnp.zeros_like(inv)


_solve_unit_lower.defvjp(_solve_fwd, _solve_bwd)


def _rwkv_chunk(st0, r, logw, k, v, a, b, inv=None):
    n_h, t, _ = r.shape
    ti = lax.broadcasted_iota(jnp.int32, (t, t), 0)
    si = lax.broadcasted_iota(jnp.int32, (t, t), 1)
    incl = (ti >= si)[None]
    strict = (ti > si)[None]
    ones_tri = jnp.broadcast_to(jnp.where(ti >= si, 1.0, 0.0)[None], (n_h, t, t))
    log_p = _bmm(ones_tri, logw)
    p_in = jnp.exp(log_p)
    p_inv = jnp.exp(-log_p)
    at = a * jnp.exp(log_p - logw)
    rt = r * p_in
    bt = b * p_inv
    kt = k * p_inv
    a_ab = jnp.where(strict, _bmm_nt(at, bt), 0.0)
    a_ak = jnp.where(strict, _bmm_nt(at, kt), 0.0)
    if inv is None:
        inv = _unit_lower_inverse(a_ab)
    u = _solve_unit_lower(a_ab, _bmm(at, st0) + _bmm(a_ak, v), inv)
    y = (_bmm(rt, st0) + _bmm(jnp.where(incl, _bmm_nt(rt, bt), 0.0), u)
         + _bmm(jnp.where(incl, _bmm_nt(rt, kt), 0.0), v))
    p_end = jnp.swapaxes(p_in[:, t - 1:t, :], 1, 2)
    st1 = (st0 + _bmm_tn(bt, u) + _bmm_tn(kt, v)) * p_end
    return y, st1, inv


def _rwkv_scan_fwd(ops):
    n_h, L, n = ops[0].shape
    t = RWKV_CHUNK
    n_c = L // t

    def body(r_ref, w_ref, k_ref, v_ref, a_ref, b_ref, y_ref, st_ref, inv_ref, st):
        @pl.when(pl.program_id(0) == 0)
        def _():
            st[...] = jnp.zeros_like(st)

        st0 = st[...]
        st_ref[0] = st0
        y, st1, inv = _rwkv_chunk(st0, r_ref[...], w_ref[...], k_ref[...], v_ref[...], a_ref[...], b_ref[...])
        y_ref[...] = y
        inv_ref[0] = inv
        st[...] = st1

    blk = pl.BlockSpec((n_h, t, n), lambda c: (0, c, 0))
    return pl.pallas_call(
        body, name="rwkv_scan_fwd", grid=(n_c,), in_specs=[blk] * 6,
        out_specs=[blk, pl.BlockSpec((1, n_h, n, n), lambda c: (c, 0, 0, 0)),
                   pl.BlockSpec((1, n_h, t, t), lambda c: (c, 0, 0, 0))],
        out_shape=[_sds((n_h, L, n)), _sds((n_c, n_h, n, n)), _sds((n_c, n_h, t, t))],
        scratch_shapes=[pltpu.VMEM((n_h, n, n), F32)],
        compiler_params=_params("arbitrary"),
    )(*ops)


def _rwkv_scan_bwd(ops, states, invs, dy):
    n_h, L, n = ops[0].shape
    t = RWKV_CHUNK
    n_c = L // t

    def body(r_ref, w_ref, k_ref, v_ref, a_ref, b_ref, st_ref, inv_ref, dy_ref, dr, dw, dk, dv, da, db, dst):
        @pl.when(pl.program_id(0) == 0)
        def _():
            dst[...] = jnp.zeros_like(dst)

        inv = inv_ref[0]
        _, vjp = jax.vjp(lambda *a: _rwkv_chunk(*a, inv=inv)[:2], st_ref[0], r_ref[...], w_ref[...], k_ref[...],
                         v_ref[...], a_ref[...], b_ref[...])
        g = vjp((dy_ref[...], dst[...]))
        dst[...] = g[0]
        for out, val in zip((dr, dw, dk, dv, da, db), g[1:]):
            out[...] = val

    blk = pl.BlockSpec((n_h, t, n), lambda c: (0, n_c - 1 - c, 0))
    per_chunk = lambda m: pl.BlockSpec((1, n_h, m, m), lambda c: (n_c - 1 - c, 0, 0, 0))
    return pl.pallas_call(
        body, name="rwkv_scan_bwd", grid=(n_c,),
        in_specs=[blk] * 6 + [per_chunk(n), per_chunk(t), blk],
        out_specs=[blk] * 6, out_shape=[_sds((n_h, L, n))] * 6,
        scratch_shapes=[pltpu.VMEM((n_h, n, n), F32)],
        compiler_params=_params("arbitrary"),
    )(*ops, states, invs, dy)


def _post_fn(x, u, zs, zr, ysc, r, k2, v, s_re, s_im, c_re, c_im, d, glu_w, glu_b, ln_w, ln_b, r_k,
             wo_s5, wo_rwkv, gf, tgt, ee):
    y_ssm = jnp.concatenate(
        [mm_hi(s_re[q], c_re[q]) - mm_hi(s_im[q], c_im[q]) for q in range(S5_BLOCKS)], axis=-1)
    y3 = _gelu(y_ssm + d * u)
    y_s5 = y3 * _sigmoid(mm_bf(y3, glu_w) + glu_b) * _silu(zs)
    mean = mm_hi(ysc, ee) * (1.0 / HEAD)
    yc = ysc - mean
    var = mm_hi(yc * yc, ee) * (1.0 / HEAD)
    gn = yc * lax.rsqrt(var + GN_EPS) * ln_w + ln_b
    bonus = mm_hi(r * k2 * r_k, ee) * v
    y_rwkv = (gn + bonus) * _silu(zr)
    x2 = x + mm_bf(y_s5, wo_s5) + mm_bf(y_rwkv, wo_rwkv)
    err = _rms(x2, gf) - tgt
    return 0.5 * jnp.mean(err * err, axis=-1, keepdims=True)


def _post(x, u, zs, zr, ysc, r, k2, v, s_re, s_im, c4_re, c4_im, d, glu_w, glu_b, ln_w, ln_b, r_k,
          w_out, gf, tgt, ee, tt):
    L = x.shape[0]
    n_t = L // tt
    acc_shapes = [(S5_BLOCKS, 512, LANES), (S5_BLOCKS, 512, LANES), (1, D_S5), (D_S5, D_S5), (1, D_S5),
                  (1, D_RWKV), (1, D_RWKV), (1, D_RWKV), (D_MODEL, D_MODEL), (1, D_MODEL), (8, LANES)]

    def body(x_ref, u_ref, zs_ref, zr_ref, ysc_ref, r_ref, k2_ref, v_ref, sre_ref, sim_ref,
             cre_ref, cim_ref, d_ref, gw_ref, gb_ref, lw_ref, lb_ref, rk_ref, wo_ref, gf_ref, tgt_ref, ee_ref,
             dx_o, du_o, dzs_o, dzr_o, dysc_o, dr_o, dk2_o, dv_o, gre_o, gim_o,
             dcre_o, dcim_o, dd_o, dgw_o, dgb_o, dlw_o, dlb_o, drk_o, dwo_o, dgf_o, loss_o,
             dcre, dcim, dd, dgw, dgb, dlw, dlb, drk, dwo, dgf, loss):
        i = pl.program_id(0)
        accs = (dcre, dcim, dd, dgw, dgb, dlw, dlb, drk, dwo, dgf, loss)

        @pl.when(i == 0)
        def _():
            for acc in accs:
                acc[...] = jnp.zeros_like(acc)

        blocks = [slice(q * 512, (q + 1) * 512) for q in range(S5_BLOCKS)]
        args = (x_ref[...], u_ref[...], zs_ref[...], zr_ref[...], ysc_ref[...], r_ref[...], k2_ref[...], v_ref[...],
                [sre_ref[:, c] for c in blocks], [sim_ref[:, c] for c in blocks],
                [cre_ref[q] for q in range(S5_BLOCKS)], [cim_ref[q] for q in range(S5_BLOCKS)],
                d_ref[...], gw_ref[...], gb_ref[...], lw_ref[...], lb_ref[...], rk_ref[...],
                wo_ref[0:D_S5, :], wo_ref[D_S5:D_MODEL, :], gf_ref[...])
        rows, vjp = jax.vjp(lambda *a: _post_fn(*a, tgt_ref[...], ee_ref[...]), *args)
        g = vjp(jnp.ones_like(rows))
        for out, val in zip((dx_o, du_o, dzs_o, dzr_o, dysc_o, dr_o, dk2_o, dv_o), g[0:8]):
            out[...] = val
        for q in range(S5_BLOCKS):
            gre_o[:, blocks[q]] = g[8][q]
            gim_o[:, blocks[q]] = g[9][q]
            dcre[q] += g[10][q]
            dcim[q] += g[11][q]
        for acc, val in zip((dd, dgw, dgb, dlw, dlb, drk), g[12:18]):
            acc[...] += val
        dwo[0:D_S5, :] += g[18]
        dwo[D_S5:D_MODEL, :] += g[19]
        dgf[...] += g[20]
        loss[...] += jnp.broadcast_to(jnp.sum(rows, axis=0, keepdims=True), loss.shape)

        @pl.when(i == n_t - 1)
        def _():
            for acc, out in zip(accs, (dcre_o, dcim_o, dd_o, dgw_o, dgb_o, dlw_o, dlb_o, drk_o, dwo_o, dgf_o, loss_o)):
                pltpu.sync_copy(acc, out)

    row = lambda n: pl.BlockSpec((tt, n), lambda i: (i, 0))
    in_specs = ([row(D_MODEL)] + [row(512)] * 7 + [row(N_STATE)] * 2
                + [_const_spec((S5_BLOCKS, 512, LANES))] * 2
                + [_const_spec(s) for s in [(1, D_S5), (D_S5, D_S5), (1, D_S5), (1, D_RWKV), (1, D_RWKV), (1, D_RWKV),
                                            (D_MODEL, D_MODEL), (1, D_MODEL)]]
                + [row(D_MODEL), _const_spec((D_RWKV, D_RWKV))])
    out_rows = [D_MODEL] + [512] * 7 + [N_STATE] * 2
    return pl.pallas_call(
        body, name="post_fwd_bwd", grid=(n_t,), in_specs=in_specs,
        out_specs=[row(n) for n in out_rows] + [_ANY] * len(acc_shapes),
        out_shape=[_sds((L, n)) for n in out_rows] + [_sds(s) for s in acc_shapes],
        scratch_shapes=[pltpu.VMEM(s, F32) for s in acc_shapes],
        compiler_params=_params("arbitrary"),
    )(x, u, zs, zr, ysc, r, k2, v, s_re, s_im, c4_re, c4_im, d, glu_w, glu_b, ln_w, ln_b, r_k, w_out, gf, tgt, ee)


def _s5_bwd(u, du_direct, s_re, s_im, g_re, g_im, b4_re, b4_im, tt):
    L = u.shape[0]
    n_t = L // tt
    acc_shapes = [(S5_BLOCKS, LANES, 512), (S5_BLOCKS, LANES, 512), (1, N_STATE), (1, N_STATE)]

    def body(u_ref, dud_ref, sre_ref, sim_ref, gre_ref, gim_ref, bre_ref, bim_ref,
             du_o, dbre_o, dbim_o, dlr_o, dli_o, dbre, dbim, dlr, dli, last_r, last_i):
        i = pl.program_id(0)

        @pl.when(i == 0)
        def _():
            for acc in (dbre, dbim, dlr, dli, last_r, last_i):
                acc[...] = jnp.zeros_like(acc)

        uv = u_ref[...]
        gr = gre_ref[...]
        gi = gim_ref[...]
        pieces = []
        for q in range(S5_BLOCKS):
            cols = slice(q * 512, (q + 1) * 512)
            uq = uv[:, q * LANES:(q + 1) * LANES]
            pieces.append(_dot(gr[:, cols], bre_ref[q], ((1,), (1,)), HI) + _dot(gi[:, cols], bim_ref[q], ((1,), (1,)), HI))
            dbre[q] += _dot(uq, gr[:, cols], ((0,), (0,)), HI)
            dbim[q] += _dot(uq, gi[:, cols], ((0,), (0,)), HI)
        du_o[...] = dud_ref[...] + jnp.concatenate(pieces, axis=-1)
        sr = sre_ref[...]
        si = sim_ref[...]
        rid = lax.broadcasted_iota(jnp.int32, sr.shape, 0)
        pr = jnp.where(rid == 0, jnp.broadcast_to(last_r[7:8, :], sr.shape), pltpu.roll(sr, 1, axis=0))
        pi_ = jnp.where(rid == 0, jnp.broadcast_to(last_i[7:8, :], si.shape), pltpu.roll(si, 1, axis=0))
        dlr[...] += jnp.sum(pr * gr + pi_ * gi, axis=0, keepdims=True)
        dli[...] += jnp.sum(pr * gi - pi_ * gr, axis=0, keepdims=True)
        last_r[...] = sr[tt - 8:tt, :]
        last_i[...] = si[tt - 8:tt, :]

        @pl.when(i == n_t - 1)
        def _():
            for acc, out in zip((dbre, dbim, dlr, dli), (dbre_o, dbim_o, dlr_o, dli_o)):
                out[...] = acc[...]

    row = lambda n: pl.BlockSpec((tt, n), lambda i: (i, 0))
    return pl.pallas_call(
        body, name="s5_bwd", grid=(n_t,),
        in_specs=[row(D_S5), row(D_S5)] + [row(N_STATE)] * 4 + [_const_spec((S5_BLOCKS, LANES, 512))] * 2,
        out_specs=[row(D_S5)] + [_acc_spec(s) for s in acc_shapes],
        out_shape=[_sds((L, D_S5))] + [_sds(s) for s in acc_shapes],
        scratch_shapes=[pltpu.VMEM(s, F32) for s in acc_shapes] + [pltpu.VMEM((8, N_STATE), F32)] * 2,
        compiler_params=_params("arbitrary"),
    )(u, du_direct, s_re, s_im, g_re, g_im, b4_re, b4_im)


def _bwd_in(x, norm_g, w_in_bf, mu, dx2, du, dzs, drws, dzr, tt):
    L = x.shape[0]
    n_t = L // tt

    def body(x_ref, g_ref, w_ref, mu_ref, dx2_ref, du_ref, dzs_ref, drws_ref, nxt_ref, dzr_ref,
             gx_o, dproj_o, dg_o, dg):
        i = pl.program_id(0)

        @pl.when(i == 0)
        def _():
            dg[...] = jnp.zeros_like(dg)

        drws_v = drws_ref[...]
        rid = lax.broadcasted_iota(jnp.int32, drws_v.shape, 0)
        nxt_row = jnp.where(i == n_t - 1, 0.0, nxt_ref[0:1, :])
        nxt = jnp.where(rid == tt - 1, jnp.broadcast_to(nxt_row, drws_v.shape), pltpu.roll(drws_v, tt - 1, axis=0))
        muv = mu_ref[...]
        drw = drws_v * (1.0 - muv) + nxt * muv
        dproj_o[:, 0:D_S5] = du_ref[...]
        dproj_o[:, D_S5:2 * D_S5] = dzs_ref[...]
        dproj_o[:, 2 * D_S5:2 * D_S5 + D_SHIFT] = drw
        dproj_o[:, 2 * D_S5 + D_SHIFT:D_IN] = dzr_ref[...]
        dh = _dot(dproj_o[...].astype(BF16), w_ref[...], ((1,), (1,)), None)
        _, vjp = jax.vjp(_rms, x_ref[...], g_ref[...])
        dxh, dgv = vjp(dh)
        gx_o[...] = dx2_ref[...] + dxh
        dg[...] += dgv

        @pl.when(i == n_t - 1)
        def _():
            dg_o[...] = dg[...]

    row = lambda n: pl.BlockSpec((tt, n), lambda i: (i, 0))
    nxt = pl.BlockSpec((8, D_SHIFT), lambda i: (jnp.minimum((i + 1) * (tt // 8), L // 8 - 1), 0))
    return pl.pallas_call(
        body, name="bwd_in", grid=(n_t,),
        in_specs=[row(D_MODEL), _const_spec((1, D_MODEL)), _const_spec((D_MODEL, D_IN)), _const_spec((1, D_SHIFT)),
                  row(D_MODEL), row(D_S5), row(D_S5), row(D_SHIFT), nxt, row(D_RWKV)],
        out_specs=[row(D_MODEL), row(D_IN), _acc_spec((1, D_MODEL))],
        out_shape=[_sds((L, D_MODEL)), _sds((L, D_IN)), _sds((1, D_MODEL))],
        scratch_shapes=[pltpu.VMEM((1, D_MODEL), F32)],
        compiler_params=_params("arbitrary"),
    )(x, norm_g, w_in_bf, mu, dx2, du, dzs, drws, drws, dzr)


def _grad_w_in(x, norm_g, dproj, tt, cn):
    L = x.shape[0]

    def body(x_ref, g_ref, dp_ref, out_ref):
        @pl.when(pl.program_id(1) == 0)
        def _():
            out_ref[...] = jnp.zeros_like(out_ref)

        h = _rms(x_ref[...], g_ref[...])
        out_ref[...] += _dot(h.astype(BF16), dp_ref[...].astype(BF16), ((0,), (0,)), None)

    return pl.pallas_call(
        body, name="grad_w_in", grid=(D_IN // cn, L // tt),
        in_specs=[pl.BlockSpec((tt, D_MODEL), lambda j, i: (i, 0)), pl.BlockSpec((1, D_MODEL), lambda j, i: (0, 0)),
                  pl.BlockSpec((tt, cn), lambda j, i: (i, j))],
        out_specs=pl.BlockSpec((D_MODEL, cn), lambda j, i: (0, j)),
        out_shape=_sds((D_MODEL, D_IN)),
        compiler_params=pltpu.CompilerParams(dimension_semantics=("parallel", "arbitrary"), vmem_limit_bytes=VMEM_LIMIT),
    )(x, norm_g, dproj)


def _to_heads(t):
    return t.reshape(t.shape[0], N_HEADS, HEAD).transpose(1, 0, 2)


def _from_heads(t):
    return t.transpose(1, 0, 2).reshape(t.shape[1], N_HEADS * HEAD)


def _block_diag_b(bbar):
    bb = bbar.reshape(S5_GROUP, S5_BLOCKS, 8, S5_STATE)
    return jnp.einsum('hqgp,Gg->qGhgp', bb, jnp.eye(8, dtype=F32)).reshape(S5_BLOCKS, LANES, 512)


def _block_diag_b_t(db4):
    d = db4.reshape(S5_BLOCKS, 8, S5_GROUP, 8, S5_STATE)
    return jnp.einsum('qGhgp,Gg->hqgp', d, jnp.eye(8, dtype=F32)).reshape(S5_GROUP, N_STATE)


def _block_diag_c(c):
    cc = c.reshape(S5_BLOCKS, 8, S5_GROUP, S5_STATE)
    return jnp.einsum('qghp,gG->qgpGh', cc, jnp.eye(8, dtype=F32)).reshape(S5_BLOCKS, 512, LANES)


def _block_diag_c_t(dc4):
    d = dc4.reshape(S5_BLOCKS, 8, S5_STATE, 8, S5_GROUP)
    return jnp.einsum('qgpGh,gG->qghp', d, jnp.eye(8, dtype=F32)).reshape(S5_GROUPS, S5_GROUP, S5_STATE)


def _local_step(x, tgt, w):
    L = x.shape[0]
    tt = min(256, L)
    tp = min(128, L)
    ee = _head_sum_matrix()

    lam_re = w['s5_lam_re'].reshape(1, N_STATE)
    lam_im = w['s5_lam_im'].reshape(1, N_STATE)
    logdt = jnp.repeat(w['s5_log_dt'], S5_STATE).reshape(1, N_STATE)
    b_re_t = w['s5_b_re'].transpose(2, 0, 1).reshape(S5_GROUP, N_STATE)
    b_im_t = w['s5_b_im'].transpose(2, 0, 1).reshape(S5_GROUP, N_STATE)
    lbr, lbi, bbr, bbi = _s5_param_fwd(lam_re, lam_im, logdt, b_re_t, b_im_t)
    b4_re, b4_im = _block_diag_b(bbr), _block_diag_b(bbi)
    c4_re, c4_im = _block_diag_c(w['s5_c_re']), _block_diag_c(w['s5_c_im'])

    norm_g = w['norm_g'].reshape(1, D_MODEL)
    w_in_bf = w['w_in'].astype(BF16)
    u, zs, rw, zr, bu_re, bu_im = _fwd_in(x, norm_g, w_in_bf, b4_re, b4_im, tt)
    s_re, s_im = _s5_scan(bu_re, bu_im, lbr, lbi, reverse=False)

    row = lambda t: t.reshape(1, -1)
    zpad = jnp.zeros((HEAD, D_RWKV), F32)
    w2p = jnp.concatenate([w['rwkv_w2'], zpad], axis=0)
    a2p = jnp.concatenate([zpad, w['rwkv_a2']], axis=0)
    pre_consts = (row(w['rwkv_mu']), row(w['rwkv_w0']), w2p, row(w['rwkv_a0']), a2p,
                  row(w['rwkv_k_k']), row(w['rwkv_k_a']), ee)
    ops = _rwkv_pre_fwd(rw, *pre_consts, tt)
    ops_h = [_to_heads(t) for t in ops]
    ysc_h, states, invs = _rwkv_scan_fwd(ops_h)
    ysc = _from_heads(ysc_h)

    post = _post(x, u, zs, zr, ysc, ops[0], ops[2], ops[3], s_re, s_im, c4_re, c4_im,
                 row(w['s5_d']), w['s5_glu_w'], row(w['s5_glu_b']), row(w['rwkv_ln_w']), row(w['rwkv_ln_b']),
                 row(w['rwkv_r_k']), w['w_out'], row(w['final_g']), tgt, ee, tp)
    (dx2, du_d, dzs, dzr, dysc, dr_b, dk2_b, dv_b, g_re, g_im,
     dc4_re, dc4_im, dd, dglu_w, dglu_b, dln_w, dln_b, dr_k, dw_out, dgf, loss) = post

    gt_re, gt_im = _s5_scan(g_re, g_im, lbr, -lbi, reverse=True)
    du, db4_re, db4_im, dlbr, dlbi = _s5_bwd(u, du_d, s_re, s_im, gt_re, gt_im, b4_re, b4_im, tt)
    group_ind = (jnp.arange(N_STATE)[:, None] // S5_STATE == jnp.arange(LANES)[None, :]).astype(F32)
    dlam_re, dlam_im, dlogdt, db_re_t, db_im_t = _s5_param_bwd(
        lam_re, lam_im, logdt, b_re_t, b_im_t, dlbr, dlbi, _block_diag_b_t(db4_re), _block_diag_b_t(db4_im), group_ind)

    scan_g = _rwkv_scan_bwd(ops_h, states, invs, _to_heads(dysc))
    cots = [_from_heads(t) for t in scan_g] + [dr_b, dk2_b, dv_b]
    drws, dmu, dw0, dw2p, da0, da2p, dk_k, dk_a = _rwkv_pre_bwd(rw, *pre_consts, cots, tt)

    grad_x, dproj, dnorm_g = _bwd_in(x, norm_g, w_in_bf, row(w['rwkv_mu']), dx2, du, dzs, drws, dzr, tt)
    dw_in = _grad_w_in(x, norm_g, dproj, tt, 640)

    unb = lambda t: t.reshape(S5_GROUP, S5_GROUPS, S5_STATE).transpose(1, 2, 0)
    grads = {
        'norm_g': dnorm_g.reshape(D_MODEL), 'w_in': dw_in,
        's5_lam_re': dlam_re.reshape(S5_GROUPS, S5_STATE), 's5_lam_im': dlam_im.reshape(S5_GROUPS, S5_STATE),
        's5_log_dt': dlogdt[0, :S5_GROUPS], 's5_b_re': unb(db_re_t), 's5_b_im': unb(db_im_t),
        's5_c_re': _block_diag_c_t(dc4_re), 's5_c_im': _block_diag_c_t(dc4_im),
        's5_d': dd.reshape(D_S5), 's5_glu_w': dglu_w, 's5_glu_b': dglu_b.reshape(D_S5),
        'rwkv_mu': dmu.reshape(-1), 'rwkv_w0': dw0.reshape(-1), 'rwkv_w2': dw2p[:HEAD], 'rwkv_a0': da0.reshape(-1),
        'rwkv_a2': da2p[HEAD:], 'rwkv_k_k': dk_k.reshape(-1), 'rwkv_k_a': dk_a.reshape(-1),
        'rwkv_r_k': dr_k.reshape(N_HEADS, HEAD), 'rwkv_ln_w': dln_w.reshape(-1), 'rwkv_ln_b': dln_b.reshape(-1),
        'w_out': dw_out, 'final_g': dgf.reshape(D_MODEL),
    }
    return loss, grad_x, grads


def _exchange(send, gather, name):
    rows = send.shape[-2]

    def body(send_ref, recv_ref, send_sems, recv_sems, local_sem):
        pos = (lax.axis_index("x"), lax.axis_index("y"), lax.axis_index("c"))
        me = 4 * pos[0] + 2 * pos[1] + pos[2]

        def block_for(dev):
            return send_ref if gather else send_ref.at[dev]

        own = pltpu.make_async_copy(block_for(me), recv_ref.at[me], local_sem)
        own.start()
        copies = []
        for k in range(1, N_DEV):
            peer = tuple(1 - p if (k >> (2 - axis)) & 1 else p for axis, p in enumerate(pos))
            peer_id = 4 * peer[0] + 2 * peer[1] + peer[2]
            out = pltpu.make_async_remote_copy(
                src_ref=block_for(peer_id), dst_ref=recv_ref.at[me],
                send_sem=send_sems.at[k - 1], recv_sem=recv_sems.at[k - 1],
                device_id=peer, device_id_type=pl.DeviceIdType.MESH)
            out.start()
            arrival = pltpu.make_async_remote_copy(
                src_ref=block_for(peer_id), dst_ref=recv_ref.at[peer_id],
                send_sem=send_sems.at[k - 1], recv_sem=recv_sems.at[k - 1],
                device_id=peer, device_id_type=pl.DeviceIdType.MESH)
            copies.append((out, arrival))
        for _, arrival in copies:
            arrival.wait_recv()
        for out, _ in copies:
            out.wait_send()
        own.wait()

    return pl.pallas_call(
        body, name=name, in_specs=[_ANY], out_specs=_ANY,
        out_shape=_sds((N_DEV, rows, LANES)),
        scratch_shapes=[pltpu.SemaphoreType.DMA((N_DEV - 1,)), pltpu.SemaphoreType.DMA((N_DEV - 1,)),
                        pltpu.SemaphoreType.DMA],
        compiler_params=pltpu.CompilerParams(has_side_effects=True),
    )(send)


def _reduce_adamw(recv, w, m, v, tr):
    rows = w.shape[0]

    def body(recv_ref, w_ref, m_ref, v_ref, g_o, d_o, m_o, v_o):
        g = recv_ref[0]
        for s in range(1, N_DEV):
            g = g + recv_ref[s]
        m_new = ADAM_B1 * m_ref[...] + (1.0 - ADAM_B1) * g
        v_new = ADAM_B2 * v_ref[...] + (1.0 - ADAM_B2) * (g * g)
        m_hat = m_new / (1.0 - ADAM_B1 ** ADAM_STEP)
        v_hat = v_new / (1.0 - ADAM_B2 ** ADAM_STEP)
        g_o[...] = g
        d_o[...] = -ADAM_LR * (m_hat / (jnp.sqrt(v_hat) + ADAM_EPS) + ADAM_WD * w_ref[...])
        m_o[...] = m_new
        v_o[...] = v_new

    blk = pl.BlockSpec((tr, LANES), lambda i: (i, 0))
    return pl.pallas_call(
        body, name="reduce_adamw", grid=(rows // tr,),
        in_specs=[pl.BlockSpec((N_DEV, tr, LANES), lambda i: (0, i, 0)), blk, blk, blk],
        out_specs=[blk] * 4, out_shape=[_sds((rows, LANES))] * 4,
        compiler_params=_params("parallel"),
    )(recv, w, m, v)


_WEIGHTS = [
    ('norm_g', (1, 1024), False), ('w_in', (1, 1024, 400), True), ('s5_lam_re', (1, 32, 64), False),
    ('s5_lam_im', (1, 32, 64), False), ('s5_log_dt', (1, 32), False), ('s5_b_re', (1, 32, 64, 16), False),
    ('s5_b_im', (1, 32, 64, 16), False), ('s5_c_re', (1, 32, 16, 64), False), ('s5_c_im', (1, 32, 16, 64), False),
    ('s5_d', (1, 512), False), ('s5_glu_w', (1, 64, 512), True), ('s5_glu_b', (1, 512), False),
    ('rwkv_mu', (1, 1664), False), ('rwkv_w0', (1, 512), False), ('rwkv_w2', (1, 64, 64), True),
    ('rwkv_a0', (1, 512), False), ('rwkv_a2', (1, 64, 64), True), ('rwkv_k_k', (1, 512), False),
    ('rwkv_k_a', (1, 512), False), ('rwkv_r_k', (1, 8, 64), False), ('rwkv_ln_w', (1, 512), False),
    ('rwkv_ln_b', (1, 512), False), ('w_out', (1, 128, 1024), True), ('final_g', (1024,), False),
]
_PACK_ORDER = [e for e in _WEIGHTS if e[2]] + [e for e in _WEIGHTS if not e[2]]
_SHARD_ROWS = sum(math.prod(s) for _, s, sh in _WEIGHTS if sh) // LANES
_SMALL_SIZE = sum(math.prod(s) for _, s, sh in _WEIGHTS if not sh)
_PACK_ROWS = 5760
_SMALL_PAD = _PACK_ROWS * LANES - _SHARD_ROWS * LANES - _SMALL_SIZE


def _pack(per_device):
    flat = [per_device[n].reshape(-1) for n, _, _ in _PACK_ORDER] + [jnp.zeros((_SMALL_PAD,), F32)]
    return jnp.concatenate(flat).reshape(_PACK_ROWS, LANES)


def _unpack(packed):
    flat = packed.reshape(-1)
    out, off = {}, 0
    for n, s, _ in _PACK_ORDER:
        size = math.prod(s)
        out[n] = flat[off:off + size].reshape(s)
        off += size
    return out


def _gathered_weights(gathered):
    flat = gathered.reshape(N_DEV, -1)
    out, off = {}, 0
    for n, s, _ in _PACK_ORDER[:5]:
        size = math.prod(s)
        blk = flat[:, off:off + size].reshape((N_DEV,) + s[1:])
        off += size
        if n in ('w_in', 'rwkv_w2', 'rwkv_a2'):
            out[n] = blk.transpose(1, 0, 2).reshape(s[1], N_DEV * s[2])
        else:
            out[n] = blk.reshape(N_DEV * s[1], s[2])
    return out


def _grad_blocks(grads):
    parts = []
    for n, s, sharded in _PACK_ORDER:
        g = grads[n]
        if not sharded:
            parts.append(jnp.broadcast_to(g.reshape(1, -1), (N_DEV, g.size)))
        elif n in ('w_in', 'rwkv_w2', 'rwkv_a2'):
            parts.append(g.reshape(g.shape[0], N_DEV, s[2]).transpose(1, 0, 2).reshape(N_DEV, -1))
        else:
            parts.append(g.reshape(N_DEV, -1))
    parts.append(jnp.zeros((N_DEV, _SMALL_PAD), F32))
    return jnp.concatenate(parts, axis=1).reshape(N_DEV, _PACK_ROWS, LANES)


def kernel(x, norm_g, w_in, s5_lam_re, s5_lam_im, s5_log_dt, s5_b_re, s5_b_im, s5_c_re, s5_c_im, s5_d, s5_glu_w, s5_glu_b, rwkv_mu, rwkv_w0, rwkv_w2, rwkv_a0, rwkv_a2, rwkv_k_k, rwkv_k_a, rwkv_r_k, rwkv_ln_w, rwkv_ln_b, w_out, final_g, loss_target, m_norm_g, m_w_in, m_s5_lam_re, m_s5_lam_im, m_s5_log_dt, m_s5_b_re, m_s5_b_im, m_s5_c_re, m_s5_c_im, m_s5_d, m_s5_glu_w, m_s5_glu_b, m_rwkv_mu, m_rwkv_w0, m_rwkv_w2, m_rwkv_a0, m_rwkv_a2, m_rwkv_k_k, m_rwkv_k_a, m_rwkv_r_k, m_rwkv_ln_w, m_rwkv_ln_b, m_w_out, m_final_g, v_norm_g, v_w_in, v_s5_lam_re, v_s5_lam_im, v_s5_log_dt, v_s5_b_re, v_s5_b_im, v_s5_c_re, v_s5_c_im, v_s5_d, v_s5_glu_w, v_s5_glu_b, v_rwkv_mu, v_rwkv_w0, v_rwkv_w2, v_rwkv_a0, v_rwkv_a2, v_rwkv_k_k, v_rwkv_k_a, v_rwkv_r_k, v_rwkv_ln_w, v_rwkv_ln_b, v_w_out, v_final_g):
    given = dict(locals())
    names = [n for n, _, _ in _WEIGHTS]
    weights = {n: given[n] for n in names}

    shard_rows = _pack(weights)[:_SHARD_ROWS]
    full = _gathered_weights(_exchange(shard_rows, True, "gather_weights"))
    local = {n: (full[n] if sharded else (weights[n][0] if n != 'final_g' else weights[n]))
             for n, _, sharded in _WEIGHTS}

    loss, grad_x, grads = _local_step(x[0], loss_target[0], local)

    recv = _exchange(_grad_blocks(grads), False, "exchange_grads")
    g, delta, m_new, v_new = _reduce_adamw(
        recv, _pack(weights), _pack({n: given['m_' + n] for n in names}), _pack({n: given['v_' + n] for n in names}), 640)
    total = lax.psum(loss[0, 0], ("x", "y", "c"))
    outs = [total, grad_x[None]]
    for packed in (g, delta, m_new, v_new):
        unpacked = _unpack(packed)
        outs += [unpacked[n] for n in names]
    return tuple(outs)
```

```python
import functools
import math

import jax
import jax.numpy as jnp
from jax import lax
from jax.experimental import pallas as pl
from jax.experimental.pallas import tpu as pltpu

F32 = jnp.float32
BF16 = jnp.bfloat16
HI = lax.Precision.HIGH

D_MODEL = 1024
D_S5 = 512
D_RWKV = 512
S5_GROUPS = 32
S5_GROUP = 16
S5_STATE = 64
N_STATE = S5_GROUPS * S5_STATE
N_HEADS = 8
HEAD = 64
D_SHIFT = 3 * D_RWKV + 128
D_IN = 2 * D_S5 + D_SHIFT + D_RWKV
NORM_EPS = 1e-6
GN_EPS = 64e-5
N_DEV = 8
LANES = 128
S5_BLOCKS = 4
S5_SEGMENTS = 32
RWKV_CHUNK = 64
VMEM_LIMIT = 56 * 1024 * 1024

ADAM_LR = 0.001
ADAM_B1 = 0.9
ADAM_B2 = 0.999
ADAM_EPS = 1e-08
ADAM_WD = 0.01
ADAM_STEP = 10


def _dot(a, b, dims, prec):
    return lax.dot_general(a, b, (dims, ((), ())), precision=prec, preferred_element_type=F32)


def _make_mm(cast, prec):
    @jax.custom_vjp
    def mm(a, b):
        return _dot(cast(a), cast(b), ((1,), (0,)), prec)

    def fwd(a, b):
        return mm(a, b), (a, b)

    def bwd(res, g):
        a, b = res
        return (_dot(cast(g), cast(b), ((1,), (1,)), prec), _dot(cast(a), cast(g), ((0,), (0,)), prec))

    mm.defvjp(fwd, bwd)
    return mm


mm_hi = _make_mm(lambda t: t, HI)
mm_bf = _make_mm(lambda t: t.astype(BF16), None)


def _sigmoid(x):
    return 1.0 / (1.0 + jnp.exp(-x))


def _silu(x):
    return x * _sigmoid(x)


def _softplus(x):
    return jnp.maximum(x, 0.0) + jnp.log(1.0 + jnp.exp(-jnp.abs(x)))


def _gelu(x):
    return 0.5 * x * (1.0 + jnp.tanh(math.sqrt(2.0 / math.pi) * (x + 0.044715 * x * x * x)))


def _rms(x, g):
    return x * lax.rsqrt(jnp.mean(x * x, axis=-1, keepdims=True) + NORM_EPS) * g


def _const_spec(shape):
    nd = len(shape)
    return pl.BlockSpec(shape, lambda *_: (0,) * nd, pipeline_mode=pl.Buffered(1))


def _acc_spec(shape):
    nd = len(shape)
    return pl.BlockSpec(shape, lambda *_: (0,) * nd)


def _params(sem):
    return pltpu.CompilerParams(dimension_semantics=(sem,), vmem_limit_bytes=VMEM_LIMIT)


_ANY = pl.BlockSpec(memory_space=pl.ANY)


def _sds(shape):
    return jax.ShapeDtypeStruct(shape, F32)


def _head_sum_matrix():
    i = jnp.arange(D_RWKV) // HEAD
    return (i[:, None] == i[None, :]).astype(F32)


def _s5_param_fn(lam_re, lam_im, logdt, b_re, b_im):
    dt = jnp.exp(logdt)
    mag = jnp.exp(lam_re * dt)
    ang = lam_im * dt
    lbr = mag * jnp.cos(ang)
    lbi = mag * jnp.sin(ang)
    nr = lbr - 1.0
    den = lam_re * lam_re + lam_im * lam_im
    cr = (nr * lam_re + lbi * lam_im) / den
    ci = (lbi * lam_re - nr * lam_im) / den
    return lbr, lbi, cr * b_re - ci * b_im, cr * b_im + ci * b_re


def _s5_param_fwd(lam_re, lam_im, logdt, b_re, b_im):
    def body(lr, li, ld, br, bi, o_lr, o_li, o_br, o_bi):
        outs = _s5_param_fn(lr[...], li[...], ld[...], br[...], bi[...])
        for o, v in zip((o_lr, o_li, o_br, o_bi), outs):
            o[...] = v

    return pl.pallas_call(
        body, name="s5_param_fwd",
        out_shape=[_sds((1, N_STATE))] * 2 + [_sds((S5_GROUP, N_STATE))] * 2,
    )(lam_re, lam_im, logdt, b_re, b_im)


def _s5_param_bwd(lam_re, lam_im, logdt, b_re, b_im, d_lbr, d_lbi, d_bbr, d_bbi, group_ind):
    def body(lr, li, ld, br, bi, g0, g1, g2, g3, ind, o_lr, o_li, o_ld, o_br, o_bi):
        _, vjp = jax.vjp(_s5_param_fn, lr[...], li[...], ld[...], br[...], bi[...])
        d_lr, d_li, d_ld, d_br, d_bi = vjp((g0[...], g1[...], g2[...], g3[...]))
        o_lr[...] = d_lr
        o_li[...] = d_li
        o_ld[...] = _dot(jnp.broadcast_to(d_ld, (8, N_STATE)), ind[...], ((1,), (0,)), HI)
        o_br[...] = d_br
        o_bi[...] = d_bi

    return pl.pallas_call(
        body, name="s5_param_bwd",
        out_shape=[_sds((1, N_STATE))] * 2 + [_sds((8, LANES))] + [_sds((S5_GROUP, N_STATE))] * 2,
    )(lam_re, lam_im, logdt, b_re, b_im, d_lbr, d_lbi, d_bbr, d_bbi, group_ind)


def _fwd_in(x, norm_g, w_in_bf, b4_re, b4_im, tt):
    L = x.shape[0]

    def body(x_ref, g_ref, w_ref, bre_ref, bim_ref, u_ref, zs_ref, rw_ref, zr_ref, bur_ref, bui_ref):
        h = _rms(x_ref[...], g_ref[...])
        proj = jnp.dot(h.astype(BF16), w_ref[...], preferred_element_type=F32)
        u = proj[:, 0:D_S5]
        u_ref[...] = u
        zs_ref[...] = proj[:, D_S5:2 * D_S5]
        rw_ref[...] = proj[:, 2 * D_S5:2 * D_S5 + D_SHIFT]
        zr_ref[...] = proj[:, 2 * D_S5 + D_SHIFT:D_IN]
        for q in range(S5_BLOCKS):
            uq = u[:, q * LANES:(q + 1) * LANES]
            cols = slice(q * 512, (q + 1) * 512)
            bur_ref[:, cols] = _dot(uq, bre_ref[q], ((1,), (0,)), HI)
            bui_ref[:, cols] = _dot(uq, bim_ref[q], ((1,), (0,)), HI)

    row = lambda n: pl.BlockSpec((tt, n), lambda i: (i, 0))
    return pl.pallas_call(
        body, name="fwd_in", grid=(L // tt,),
        in_specs=[row(D_MODEL), _const_spec((1, D_MODEL)), _const_spec((D_MODEL, D_IN)),
                  _const_spec((S5_BLOCKS, LANES, 512)), _const_spec((S5_BLOCKS, LANES, 512))],
        out_specs=[row(D_S5), row(D_S5), row(D_SHIFT), row(D_RWKV), row(N_STATE), row(N_STATE)],
        out_shape=[_sds((L, D_S5)), _sds((L, D_S5)), _sds((L, D_SHIFT)), _sds((L, D_RWKV)),
                   _sds((L, N_STATE)), _sds((L, N_STATE))],
        compiler_params=_params("parallel"),
    )(x, norm_g, w_in_bf, b4_re, b4_im)


def _cmul(ar, ai, br, bi):
    return ar * br - ai * bi, ar * bi + ai * br


def _s5_scan(xr, xi, lam_r, lam_i, reverse):
    L = xr.shape[0]
    nseg = S5_SEGMENTS
    n_g = nseg // 8
    seg = L // nseg
    cb = LANES

    def body(xr_ref, xi_ref, lr_ref, li_ref, or_ref, oi_ref):
        lr = jnp.broadcast_to(lr_ref[...], (8, cb))
        li = jnp.broadcast_to(li_ref[...], (8, cb))

        def rows(i, g):
            return pl.ds(g * 8 * seg + ((seg - 1 - i) if reverse else i), 8, stride=seg)

        def local_scan(i, c):
            out = []
            for g in range(n_g):
                mr, mi = _cmul(lr, li, c[2 * g], c[2 * g + 1])
                sr = mr + xr_ref[rows(i, g), :]
                si = mi + xi_ref[rows(i, g), :]
                or_ref[rows(i, g), :] = sr
                oi_ref[rows(i, g), :] = si
                out += [sr, si]
            return tuple(out)

        zero = jnp.zeros((8, cb), F32)
        ends = lax.fori_loop(0, seg, local_scan, (zero,) * (2 * n_g))

        pr, pi_ = lr[0:1], li[0:1]
        n = 1
        while n < seg:
            pr, pi_ = _cmul(pr, pi_, pr, pi_)
            n *= 2
        rid = lax.broadcasted_iota(jnp.int32, (8, cb), 0)
        carry_r, carry_i = zero[0:1], zero[0:1]
        corr = [zero] * (2 * n_g)
        for j in (range(nseg - 1, -1, -1) if reverse else range(nseg)):
            g, row = divmod(j, 8)
            corr[2 * g] = jnp.where(rid == row, jnp.broadcast_to(carry_r, (8, cb)), corr[2 * g])
            corr[2 * g + 1] = jnp.where(rid == row, jnp.broadcast_to(carry_i, (8, cb)), corr[2 * g + 1])
            mr, mi = _cmul(pr, pi_, carry_r, carry_i)
            carry_r = ends[2 * g][row:row + 1] + mr
            carry_i = ends[2 * g + 1][row:row + 1] + mi

        def add_carry(i, c):
            out = []
            for g in range(n_g):
                cr, ci = _cmul(lr, li, c[2 * g], c[2 * g + 1])
                or_ref[rows(i, g), :] = or_ref[rows(i, g), :] + cr
                oi_ref[rows(i, g), :] = oi_ref[rows(i, g), :] + ci
                out += [cr, ci]
            return tuple(out)

        lax.fori_loop(0, seg, add_carry, tuple(corr))

    assert seg & (seg - 1) == 0
    blk = pl.BlockSpec((L, cb), lambda j: (0, j))
    lam = pl.BlockSpec((1, cb), lambda j: (0, j))
    return pl.pallas_call(
        body, name="s5_scan_bwd" if reverse else "s5_scan_fwd", grid=(N_STATE // cb,),
        in_specs=[blk, blk, lam, lam], out_specs=[blk, blk],
        out_shape=[_sds((L, N_STATE))] * 2,
        input_output_aliases={0: 0, 1: 1},
        compiler_params=_params("parallel"),
    )(xr, xi, lam_r, lam_i)


def _rwkv_pre_fn(r, k, v, wa, w0, w2p, a0, a2p, k_k, k_a, ee):
    w = -_softplus(-(w0 + mm_hi(jnp.tanh(wa), w2p))) - 0.5
    logw = -jnp.exp(w)
    a = _sigmoid(a0 + mm_hi(wa, a2p))
    kkp = k * k_k
    kk = kkp / jnp.maximum(jnp.sqrt(mm_hi(kkp * kkp, ee)), 1e-12)
    k2 = k * (1.0 + (a - 1.0) * k_a)
    return r, logw, k2, v, -kk, kk * a


def _head_spec(tt):
    return pl.BlockSpec((N_HEADS, tt, HEAD), lambda i: (0, i, 0))


def _load_heads(ref):
    return jnp.concatenate([ref[h] for h in range(N_HEADS)], axis=-1)


def _store_heads(ref, val):
    for h in range(N_HEADS):
        ref[h] = val[:, h * HEAD:(h + 1) * HEAD]


def _shifted(rw, prev_blk, first):
    rolled = pltpu.roll(rw, 1, axis=0)
    prev_row = jnp.where(first, 0.0, prev_blk[7:8, :])
    rid = lax.broadcasted_iota(jnp.int32, rw.shape, 0)
    return jnp.where(rid == 0, jnp.broadcast_to(prev_row, rw.shape), rolled)


def _split_rw(t):
    return t[:, 0:512], t[:, 512:1024], t[:, 1024:1536], t[:, 1536:1664]


def _rwkv_pre_specs(tt):
    row = pl.BlockSpec((tt, D_SHIFT), lambda i: (i, 0))
    prev = pl.BlockSpec((8, D_SHIFT), lambda i: (jnp.maximum(i * (tt // 8) - 1, 0), 0))
    consts = [_const_spec((1, D_SHIFT)), _const_spec((1, D_RWKV)), _const_spec((LANES, D_RWKV)),
              _const_spec((1, D_RWKV)), _const_spec((LANES, D_RWKV)), _const_spec((1, D_RWKV)),
              _const_spec((1, D_RWKV)), _const_spec((D_RWKV, D_RWKV))]
    return [row, prev] + consts


def _rwkv_pre_fwd(rw, mu, w0, w2p, a0, a2p, k_k, k_a, ee, tt):
    L = rw.shape[0]

    def body(rw_ref, prev_ref, mu_ref, w0_ref, w2_ref, a0_ref, a2_ref, kk_ref, ka_ref, ee_ref, *outs):
        rwv = rw_ref[...]
        rws = rwv + (_shifted(rwv, prev_ref[...], pl.program_id(0) == 0) - rwv) * mu_ref[...]
        res = _rwkv_pre_fn(*_split_rw(rws), w0_ref[...], w2_ref[...], a0_ref[...], a2_ref[...],
                           kk_ref[...], ka_ref[...], ee_ref[...])
        for o, val in zip(outs, res):
            _store_heads(o, val)

    return pl.pallas_call(
        body, name="rwkv_pre_fwd", grid=(L // tt,),
        in_specs=_rwkv_pre_specs(tt), out_specs=[_head_spec(tt)] * 6, out_shape=[_sds((N_HEADS, L, HEAD))] * 6,
        compiler_params=_params("parallel"),
    )(rw, rw, mu, w0, w2p, a0, a2p, k_k, k_a, ee)


def _rwkv_pre_bwd(rw, mu, w0, w2p, a0, a2p, k_k, k_a, ee, cots, tt):
    L = rw.shape[0]
    n_t = L // tt

    def body(rw_ref, prev_ref, mu_ref, w0_ref, w2_ref, a0_ref, a2_ref, kk_ref, ka_ref, ee_ref,
             c_r, c_w, c_k, c_v, c_a, c_b, cb_r, cb_k, cb_v,
             drws_ref, dmu_o, dw0_o, dw2_o, da0_o, da2_o, dkk_o, dka_o,
             dmu, dw0, dw2, da0, da2, dkk, dka):
        i = pl.program_id(0)
        accs = (dmu, dw0, dw2, da0, da2, dkk, dka)

        @pl.when(i == 0)
        def _():
            for acc in accs:
                acc[...] = jnp.zeros_like(acc)

        rwv = rw_ref[...]
        diff = _shifted(rwv, prev_ref[...], i == 0) - rwv
        rws = rwv + diff * mu_ref[...]
        consts = (w0_ref[...], w2_ref[...], a0_ref[...], a2_ref[...], kk_ref[...], ka_ref[...])
        _, vjp = jax.vjp(lambda *a: _rwkv_pre_fn(*a, ee_ref[...]), *_split_rw(rws), *consts)
        scan = [_load_heads(c) for c in (c_r, c_w, c_k, c_v, c_a, c_b)]
        g = vjp((scan[0] + cb_r[...], scan[1], scan[2] + cb_k[...], scan[3] + cb_v[...], scan[4], scan[5]))
        drws = jnp.concatenate(g[0:4], axis=-1)
        drws_ref[...] = drws
        dmu[...] += jnp.sum(drws * diff, axis=0, keepdims=True)
        for acc, val in zip(accs[1:], g[4:]):
            acc[...] += val

        @pl.when(i == n_t - 1)
        def _():
            for acc, out in zip(accs, (dmu_o, dw0_o, dw2_o, da0_o, da2_o, dkk_o, dka_o)):
                out[...] = acc[...]

    row = pl.BlockSpec((tt, D_RWKV), lambda i: (i, 0))
    shapes = [(1, D_SHIFT), (1, D_RWKV), (LANES, D_RWKV), (1, D_RWKV), (LANES, D_RWKV), (1, D_RWKV), (1, D_RWKV)]
    return pl.pallas_call(
        body, name="rwkv_pre_bwd", grid=(n_t,),
        in_specs=_rwkv_pre_specs(tt) + [_head_spec(tt)] * 6 + [row] * 3,
        out_specs=[pl.BlockSpec((tt, D_SHIFT), lambda i: (i, 0))] + [_acc_spec(s) for s in shapes],
        out_shape=[_sds((L, D_SHIFT))] + [_sds(s) for s in shapes],
        scratch_shapes=[pltpu.VMEM(s, F32) for s in shapes],
        compiler_params=_params("arbitrary"),
    )(rw, rw, mu, w0, w2p, a0, a2p, k_k, k_a, ee, *cots)


def _bmm(a, b):
    return lax.dot_general(a, b, (((2,), (1,)), ((0,), (0,))), precision=HI, preferred_element_type=F32)


def _bmm_nt(a, b):
    return lax.dot_general(a, b, (((2,), (2,)), ((0,), (0,))), precision=HI, preferred_element_type=F32)


def _bmm_tn(a, b):
    return lax.dot_general(a, b, (((1,), (1,)), ((0,), (0,))), precision=HI, preferred_element_type=F32)


def _unit_lower_inverse(a):
    t = a.shape[-1]
    ti = lax.broadcasted_iota(jnp.int32, (t, t), 0)
    si = lax.broadcasted_iota(jnp.int32, (t, t), 1)
    inv = jnp.where(ti == si, 1.0, 0.0)[None] + a
    pw = a
    n = 1
    while 2 * n < t:
        pw = _bmm(pw, pw)
        inv = inv + _bmm(inv, pw)
        n *= 2
    return inv


@jax.custom_vjp
def _solve_unit_lower(a, rhs, inv):
    return _bmm(inv, rhs)


def _solve_fwd(a, rhs, inv):
    u = _bmm(inv, rhs)
    return u, (inv, u)


def _solve_bwd(res, du):
    inv, u = res
    d_rhs = _bmm_tn(inv, du)
    return _bmm_nt(d_rhs, u), d_rhs, jnp.zeros_like(inv)


_solve_unit_lower.defvjp(_solve_fwd, _solve_bwd)


def _rwkv_chunk(st0, r, logw, k, v, a, b, inv=None):
    n_h, t, _ = r.shape
    ti = lax.broadcasted_iota(jnp.int32, (t, t), 0)
    si = lax.broadcasted_iota(jnp.int32, (t, t), 1)
    incl = (ti >= si)[None]
    strict = (ti > si)[None]
    ones_tri = jnp.broadcast_to(jnp.where(ti >= si, 1.0, 0.0)[None], (n_h, t, t))
    log_p = _bmm(ones_tri, logw)
    p_in = jnp.exp(log_p)
    p_inv = jnp.exp(-log_p)
    at = a * jnp.exp(log_p - logw)
    rt = r * p_in
    bt = b * p_inv
    kt = k * p_inv
    a_ab = jnp.where(strict, _bmm_nt(at, bt), 0.0)
    a_ak = jnp.where(strict, _bmm_nt(at, kt), 0.0)
    if inv is None:
        inv = _unit_lower_inverse(a_ab)
    u = _solve_unit_lower(a_ab, _bmm(at, st0) + _bmm(a_ak, v), inv)
    y = (_bmm(rt, st0) + _bmm(jnp.where(incl, _bmm_nt(rt, bt), 0.0), u)
         + _bmm(jnp.where(incl, _bmm_nt(rt, kt), 0.0), v))
    p_end = jnp.swapaxes(p_in[:, t - 1:t, :], 1, 2)
    st1 = (st0 + _bmm_tn(bt, u) + _bmm_tn(kt, v)) * p_end
    return y, st1, inv


def _rwkv_scan_fwd(ops):
    n_h, L, n = ops[0].shape
    t = RWKV_CHUNK
    n_c = L // t

    def body(r_ref, w_ref, k_ref, v_ref, a_ref, b_ref, y_ref, st_ref, inv_ref, st):
        @pl.when(pl.program_id(0) == 0)
        def _():
            st[...] = jnp.zeros_like(st)

        st0 = st[...]
        st_ref[0] = st0
        y, st1, inv = _rwkv_chunk(st0, r_ref[...], w_ref[...], k_ref[...], v_ref[...], a_ref[...], b_ref[...])
        y_ref[...] = y
        inv_ref[0] = inv
        st[...] = st1

    blk = pl.BlockSpec((n_h, t, n), lambda c: (0, c, 0))
    return pl.pallas_call(
        body, name="rwkv_scan_fwd", grid=(n_c,), in_specs=[blk] * 6,
        out_specs=[blk, pl.BlockSpec((1, n_h, n, n), lambda c: (c, 0, 0, 0)),
                   pl.BlockSpec((1, n_h, t, t), lambda c: (c, 0, 0, 0))],
        out_shape=[_sds((n_h, L, n)), _sds((n_c, n_h, n, n)), _sds((n_c, n_h, t, t))],
        scratch_shapes=[pltpu.VMEM((n_h, n, n), F32)],
        compiler_params=_params("arbitrary"),
    )(*ops)


def _rwkv_scan_bwd(ops, states, invs, dy):
    n_h, L, n = ops[0].shape
    t = RWKV_CHUNK
    n_c = L // t

    def body(r_ref, w_ref, k_ref, v_ref, a_ref, b_ref, st_ref, inv_ref, dy_ref, dr, dw, dk, dv, da, db, dst):
        @pl.when(pl.program_id(0) == 0)
        def _():
            dst[...] = jnp.zeros_like(dst)

        inv = inv_ref[0]
        _, vjp = jax.vjp(lambda *a: _rwkv_chunk(*a, inv=inv)[:2], st_ref[0], r_ref[...], w_ref[...], k_ref[...],
                         v_ref[...], a_ref[...], b_ref[...])
        g = vjp((dy_ref[...], dst[...]))
        dst[...] = g[0]
        for out, val in zip((dr, dw, dk, dv, da, db), g[1:]):
            out[...] = val

    blk = pl.BlockSpec((n_h, t, n), lambda c: (0, n_c - 1 - c, 0))
    per_chunk = lambda m: pl.BlockSpec((1, n_h, m, m), lambda c: (n_c - 1 - c, 0, 0, 0))
    return pl.pallas_call(
        body, name="rwkv_scan_bwd", grid=(n_c,),
        in_specs=[blk] * 6 + [per_chunk(n), per_chunk(t), blk],
        out_specs=[blk] * 6, out_shape=[_sds((n_h, L, n))] * 6,
        scratch_shapes=[pltpu.VMEM((n_h, n, n), F32)],
        compiler_params=_params("arbitrary"),
    )(*ops, states, invs, dy)


def _post_fn(x, u, zs, zr, ysc, r, k2, v, s_re, s_im, c_re, c_im, d, glu_w, glu_b, ln_w, ln_b, r_k,
             wo_s5, wo_rwkv, gf, tgt, ee):
    y_ssm = jnp.concatenate(
        [mm_hi(s_re[q], c_re[q]) - mm_hi(s_im[q], c_im[q]) for q in range(S5_BLOCKS)], axis=-1)
    y3 = _gelu(y_ssm + d * u)
    y_s5 = y3 * _sigmoid(mm_bf(y3, glu_w) + glu_b) * _silu(zs)
    mean = mm_hi(ysc, ee) * (1.0 / HEAD)
    yc = ysc - mean
    var = mm_hi(yc * yc, ee) * (1.0 / HEAD)
    gn = yc * lax.rsqrt(var + GN_EPS) * ln_w + ln_b
    bonus = mm_hi(r * k2 * r_k, ee) * v
    y_rwkv = (gn + bonus) * _silu(zr)
    x2 = x + mm_bf(y_s5, wo_s5) + mm_bf(y_rwkv, wo_rwkv)
    err = _rms(x2, gf) - tgt
    return 0.5 * jnp.mean(err * err, axis=-1, keepdims=True)


def _post(x, u, zs, zr, ysc, r, k2, v, s_re, s_im, c4_re, c4_im, d, glu_w, glu_b, ln_w, ln_b, r_k,
          w_out, gf, tgt, ee, tt):
    L = x.shape[0]
    n_t = L // tt
    acc_shapes = [(S5_BLOCKS, 512, LANES), (S5_BLOCKS, 512, LANES), (1, D_S5), (D_S5, D_S5), (1, D_S5),
                  (1, D_RWKV), (1, D_RWKV), (1, D_RWKV), (D_MODEL, D_MODEL), (1, D_MODEL), (8, LANES)]

    def body(x_ref, u_ref, zs_ref, zr_ref, ysc_ref, r_ref, k2_ref, v_ref, sre_ref, sim_ref,
             cre_ref, cim_ref, d_ref, gw_ref, gb_ref, lw_ref, lb_ref, rk_ref, wo_ref, gf_ref, tgt_ref, ee_ref,
             dx_o, du_o, dzs_o, dzr_o, dysc_o, dr_o, dk2_o, dv_o, gre_o, gim_o,
             dcre_o, dcim_o, dd_o, dgw_o, dgb_o, dlw_o, dlb_o, drk_o, dwo_o, dgf_o, loss_o,
             dcre, dcim, dd, dgw, dgb, dlw, dlb, drk, dwo, dgf, loss):
        i = pl.program_id(0)
        accs = (dcre, dcim, dd, dgw, dgb, dlw, dlb, drk, dwo, dgf, loss)

        @pl.when(i == 0)
        def _():
            for acc in accs:
                acc[...] = jnp.zeros_like(acc)

        blocks = [slice(q * 512, (q + 1) * 512) for q in range(S5_BLOCKS)]
        args = (x_ref[...], u_ref[...], zs_ref[...], zr_ref[...],
                _load_heads(ysc_ref), _load_heads(r_ref), _load_heads(k2_ref), _load_heads(v_ref),
                [sre_ref[:, c] for c in blocks], [sim_ref[:, c] for c in blocks],
                [cre_ref[q] for q in range(S5_BLOCKS)], [cim_ref[q] for q in range(S5_BLOCKS)],
                d_ref[...], gw_ref[...], gb_ref[...], lw_ref[...], lb_ref[...], rk_ref[...],
                wo_ref[0:D_S5, :], wo_ref[D_S5:D_MODEL, :], gf_ref[...])
        rows, vjp = jax.vjp(lambda *a: _post_fn(*a, tgt_ref[...], ee_ref[...]), *args)
        g = vjp(jnp.ones_like(rows))
        for out, val in zip((dx_o, du_o, dzs_o, dzr_o), g[0:4]):
            out[...] = val
        _store_heads(dysc_o, g[4])
        for out, val in zip((dr_o, dk2_o, dv_o), g[5:8]):
            out[...] = val
        for q in range(S5_BLOCKS):
            gre_o[:, blocks[q]] = g[8][q]
            gim_o[:, blocks[q]] = g[9][q]
            dcre[q] += g[10][q]
            dcim[q] += g[11][q]
        for acc, val in zip((dd, dgw, dgb, dlw, dlb, drk), g[12:18]):
            acc[...] += val
        dwo[0:D_S5, :] += g[18]
        dwo[D_S5:D_MODEL, :] += g[19]
        dgf[...] += g[20]
        loss[...] += jnp.broadcast_to(jnp.sum(rows, axis=0, keepdims=True), loss.shape)

        @pl.when(i == n_t - 1)
        def _():
            for acc, out in zip(accs, (dcre_o, dcim_o, dd_o, dgw_o, dgb_o, dlw_o, dlb_o, drk_o, dwo_o, dgf_o, loss_o)):
                pltpu.sync_copy(acc, out)

    row = lambda n: pl.BlockSpec((tt, n), lambda i: (i, 0))
    in_specs = ([row(D_MODEL)] + [row(512)] * 3 + [_head_spec(tt)] * 4 + [row(N_STATE)] * 2
                + [_const_spec((S5_BLOCKS, 512, LANES))] * 2
                + [_const_spec(s) for s in [(1, D_S5), (D_S5, D_S5), (1, D_S5), (1, D_RWKV), (1, D_RWKV), (1, D_RWKV),
                                            (D_MODEL, D_MODEL), (1, D_MODEL)]]
                + [row(D_MODEL), _const_spec((D_RWKV, D_RWKV))])
    out_rows = [D_MODEL] + [512] * 3 + [None] + [512] * 3 + [N_STATE] * 2
    return pl.pallas_call(
        body, name="post_fwd_bwd", grid=(n_t,), in_specs=in_specs,
        out_specs=[row(n) if n else _head_spec(tt) for n in out_rows] + [_ANY] * len(acc_shapes),
        out_shape=([_sds((L, n)) if n else _sds((N_HEADS, L, HEAD)) for n in out_rows]
                   + [_sds(s) for s in acc_shapes]),
        scratch_shapes=[pltpu.VMEM(s, F32) for s in acc_shapes],
        compiler_params=_params("arbitrary"),
    )(x, u, zs, zr, ysc, r, k2, v, s_re, s_im, c4_re, c4_im, d, glu_w, glu_b, ln_w, ln_b, r_k, w_out, gf, tgt, ee)


def _s5_bwd(u, du_direct, s_re, s_im, g_re, g_im, b4_re, b4_im, tt):
    L = u.shape[0]
    n_t = L // tt
    acc_shapes = [(S5_BLOCKS, LANES, 512), (S5_BLOCKS, LANES, 512), (1, N_STATE), (1, N_STATE)]

    def body(u_ref, dud_ref, sre_ref, sim_ref, gre_ref, gim_ref, bre_ref, bim_ref,
             du_o, dbre_o, dbim_o, dlr_o, dli_o, dbre, dbim, dlr, dli, last_r, last_i):
        i = pl.program_id(0)

        @pl.when(i == 0)
        def _():
            for acc in (dbre, dbim, dlr, dli, last_r, last_i):
                acc[...] = jnp.zeros_like(acc)

        uv = u_ref[...]
        gr = gre_ref[...]
        gi = gim_ref[...]
        pieces = []
        for q in range(S5_BLOCKS):
            cols = slice(q * 512, (q + 1) * 512)
            uq = uv[:, q * LANES:(q + 1) * LANES]
            pieces.append(_dot(gr[:, cols], bre_ref[q], ((1,), (1,)), HI) + _dot(gi[:, cols], bim_ref[q], ((1,), (1,)), HI))
            dbre[q] += _dot(uq, gr[:, cols], ((0,), (0,)), HI)
            dbim[q] += _dot(uq, gi[:, cols], ((0,), (0,)), HI)
        du_o[...] = dud_ref[...] + jnp.concatenate(pieces, axis=-1)
        sr = sre_ref[...]
        si = sim_ref[...]
        rid = lax.broadcasted_iota(jnp.int32, sr.shape, 0)
        pr = jnp.where(rid == 0, jnp.broadcast_to(last_r[7:8, :], sr.shape), pltpu.roll(sr, 1, axis=0))
        pi_ = jnp.where(rid == 0, jnp.broadcast_to(last_i[7:8, :], si.shape), pltpu.roll(si, 1, axis=0))
        dlr[...] += jnp.sum(pr * gr + pi_ * gi, axis=0, keepdims=True)
        dli[...] += jnp.sum(pr * gi - pi_ * gr, axis=0, keepdims=True)
        last_r[...] = sr[tt - 8:tt, :]
        last_i[...] = si[tt - 8:tt, :]

        @pl.when(i == n_t - 1)
        def _():
            for acc, out in zip((dbre, dbim, dlr, dli), (dbre_o, dbim_o, dlr_o, dli_o)):
                out[...] = acc[...]

    row = lambda n: pl.BlockSpec((tt, n), lambda i: (i, 0))
    return pl.pallas_call(
        body, name="s5_bwd", grid=(n_t,),
        in_specs=[row(D_S5), row(D_S5)] + [row(N_STATE)] * 4 + [_const_spec((S5_BLOCKS, LANES, 512))] * 2,
        out_specs=[row(D_S5)] + [_acc_spec(s) for s in acc_shapes],
        out_shape=[_sds((L, D_S5))] + [_sds(s) for s in acc_shapes],
        scratch_shapes=[pltpu.VMEM(s, F32) for s in acc_shapes] + [pltpu.VMEM((8, N_STATE), F32)] * 2,
        compiler_params=_params("arbitrary"),
    )(u, du_direct, s_re, s_im, g_re, g_im, b4_re, b4_im)


def _bwd_in(x, norm_g, w_in_bf, mu, dx2, du, dzs, drws, dzr, tt):
    L = x.shape[0]
    n_t = L // tt

    def body(x_ref, g_ref, w_ref, mu_ref, dx2_ref, du_ref, dzs_ref, drws_ref, nxt_ref, dzr_ref,
             gx_o, dproj_o, dg_o, dg):
        i = pl.program_id(0)

        @pl.when(i == 0)
        def _():
            dg[...] = jnp.zeros_like(dg)

        drws_v = drws_ref[...]
        rid = lax.broadcasted_iota(jnp.int32, drws_v.shape, 0)
        nxt_row = jnp.where(i == n_t - 1, 0.0, nxt_ref[0:1, :])
        nxt = jnp.where(rid == tt - 1, jnp.broadcast_to(nxt_row, drws_v.shape), pltpu.roll(drws_v, tt - 1, axis=0))
        muv = mu_ref[...]
        drw = drws_v * (1.0 - muv) + nxt * muv
        dproj_o[:, 0:D_S5] = du_ref[...]
        dproj_o[:, D_S5:2 * D_S5] = dzs_ref[...]
        dproj_o[:, 2 * D_S5:2 * D_S5 + D_SHIFT] = drw
        dproj_o[:, 2 * D_S5 + D_SHIFT:D_IN] = dzr_ref[...]
        dh = _dot(dproj_o[...].astype(BF16), w_ref[...], ((1,), (1,)), None)
        _, vjp = jax.vjp(_rms, x_ref[...], g_ref[...])
        dxh, dgv = vjp(dh)
        gx_o[...] = dx2_ref[...] + dxh
        dg[...] += dgv

        @pl.when(i == n_t - 1)
        def _():
            dg_o[...] = dg[...]

    row = lambda n: pl.BlockSpec((tt, n), lambda i: (i, 0))
    nxt = pl.BlockSpec((8, D_SHIFT), lambda i: (jnp.minimum((i + 1) * (tt // 8), L // 8 - 1), 0))
    return pl.pallas_call(
        body, name="bwd_in", grid=(n_t,),
        in_specs=[row(D_MODEL), _const_spec((1, D_MODEL)), _const_spec((D_MODEL, D_IN)), _const_spec((1, D_SHIFT)),
                  row(D_MODEL), row(D_S5), row(D_S5), row(D_SHIFT), nxt, row(D_RWKV)],
        out_specs=[row(D_MODEL), row(D_IN), _acc_spec((1, D_MODEL))],
        out_shape=[_sds((L, D_MODEL)), _sds((L, D_IN)), _sds((1, D_MODEL))],
        scratch_shapes=[pltpu.VMEM((1, D_MODEL), F32)],
        compiler_params=_params("arbitrary"),
    )(x, norm_g, w_in_bf, mu, dx2, du, dzs, drws, drws, dzr)


def _grad_w_in(x, norm_g, dproj, tt, cn):
    L = x.shape[0]

    def body(x_ref, g_ref, dp_ref, out_ref):
        @pl.when(pl.program_id(1) == 0)
        def _():
            out_ref[...] = jnp.zeros_like(out_ref)

        h = _rms(x_ref[...], g_ref[...])
        out_ref[...] += _dot(h.astype(BF16), dp_ref[...].astype(BF16), ((0,), (0,)), None)

    return pl.pallas_call(
        body, name="grad_w_in", grid=(D_IN // cn, L // tt),
        in_specs=[pl.BlockSpec((tt, D_MODEL), lambda j, i: (i, 0)), pl.BlockSpec((1, D_MODEL), lambda j, i: (0, 0)),
                  pl.BlockSpec((tt, cn), lambda j, i: (i, j))],
        out_specs=pl.BlockSpec((D_MODEL, cn), lambda j, i: (0, j)),
        out_shape=_sds((D_MODEL, D_IN)),
        compiler_params=pltpu.CompilerParams(dimension_semantics=("parallel", "arbitrary"), vmem_limit_bytes=VMEM_LIMIT),
    )(x, norm_g, dproj)


def _block_diag_b(bbar):
    bb = bbar.reshape(S5_GROUP, S5_BLOCKS, 8, S5_STATE)
    return jnp.einsum('hqgp,Gg->qGhgp', bb, jnp.eye(8, dtype=F32)).reshape(S5_BLOCKS, LANES, 512)


def _block_diag_b_t(db4):
    d = db4.reshape(S5_BLOCKS, 8, S5_GROUP, 8, S5_STATE)
    return jnp.einsum('qGhgp,Gg->hqgp', d, jnp.eye(8, dtype=F32)).reshape(S5_GROUP, N_STATE)


def _block_diag_c(c):
    cc = c.reshape(S5_BLOCKS, 8, S5_GROUP, S5_STATE)
    return jnp.einsum('qghp,gG->qgpGh', cc, jnp.eye(8, dtype=F32)).reshape(S5_BLOCKS, 512, LANES)


def _block_diag_c_t(dc4):
    d = dc4.reshape(S5_BLOCKS, 8, S5_STATE, 8, S5_GROUP)
    return jnp.einsum('qgpGh,gG->qghp', d, jnp.eye(8, dtype=F32)).reshape(S5_GROUPS, S5_GROUP, S5_STATE)


def _local_step(x, tgt, w):
    L = x.shape[0]
    tt = min(256, L)
    tp = min(128, L)
    ee = _head_sum_matrix()

    lam_re = w['s5_lam_re'].reshape(1, N_STATE)
    lam_im = w['s5_lam_im'].reshape(1, N_STATE)
    logdt = jnp.repeat(w['s5_log_dt'], S5_STATE).reshape(1, N_STATE)
    b_re_t = w['s5_b_re'].transpose(2, 0, 1).reshape(S5_GROUP, N_STATE)
    b_im_t = w['s5_b_im'].transpose(2, 0, 1).reshape(S5_GROUP, N_STATE)
    lbr, lbi, bbr, bbi = _s5_param_fwd(lam_re, lam_im, logdt, b_re_t, b_im_t)
    b4_re, b4_im = _block_diag_b(bbr), _block_diag_b(bbi)
    c4_re, c4_im = _block_diag_c(w['s5_c_re']), _block_diag_c(w['s5_c_im'])

    norm_g = w['norm_g'].reshape(1, D_MODEL)
    w_in_bf = w['w_in'].astype(BF16)
    u, zs, rw, zr, bu_re, bu_im = _fwd_in(x, norm_g, w_in_bf, b4_re, b4_im, tt)
    s_re, s_im = _s5_scan(bu_re, bu_im, lbr, lbi, reverse=False)

    row = lambda t: t.reshape(1, -1)
    zpad = jnp.zeros((HEAD, D_RWKV), F32)
    w2p = jnp.concatenate([w['rwkv_w2'], zpad], axis=0)
    a2p = jnp.concatenate([zpad, w['rwkv_a2']], axis=0)
    pre_consts = (row(w['rwkv_mu']), row(w['rwkv_w0']), w2p, row(w['rwkv_a0']), a2p,
                  row(w['rwkv_k_k']), row(w['rwkv_k_a']), ee)
    ops = _rwkv_pre_fwd(rw, *pre_consts, tt)
    ysc, states, invs = _rwkv_scan_fwd(ops)

    post = _post(x, u, zs, zr, ysc, ops[0], ops[2], ops[3], s_re, s_im, c4_re, c4_im,
                 row(w['s5_d']), w['s5_glu_w'], row(w['s5_glu_b']), row(w['rwkv_ln_w']), row(w['rwkv_ln_b']),
                 row(w['rwkv_r_k']), w['w_out'], row(w['final_g']), tgt, ee, tp)
    (dx2, du_d, dzs, dzr, dysc, dr_b, dk2_b, dv_b, g_re, g_im,
     dc4_re, dc4_im, dd, dglu_w, dglu_b, dln_w, dln_b, dr_k, dw_out, dgf, loss) = post

    gt_re, gt_im = _s5_scan(g_re, g_im, lbr, -lbi, reverse=True)
    du, db4_re, db4_im, dlbr, dlbi = _s5_bwd(u, du_d, s_re, s_im, gt_re, gt_im, b4_re, b4_im, tt)
    group_ind = (jnp.arange(N_STATE)[:, None] // S5_STATE == jnp.arange(LANES)[None, :]).astype(F32)
    dlam_re, dlam_im, dlogdt, db_re_t, db_im_t = _s5_param_bwd(
        lam_re, lam_im, logdt, b_re_t, b_im_t, dlbr, dlbi, _block_diag_b_t(db4_re), _block_diag_b_t(db4_im), group_ind)

    cots = list(_rwkv_scan_bwd(ops, states, invs, dysc)) + [dr_b, dk2_b, dv_b]
    drws, dmu, dw0, dw2p, da0, da2p, dk_k, dk_a = _rwkv_pre_bwd(rw, *pre_consts, cots, tt)

    grad_x, dproj, dnorm_g = _bwd_in(x, norm_g, w_in_bf, row(w['rwkv_mu']), dx2, du, dzs, drws, dzr, tt)
    dw_in = _grad_w_in(x, norm_g, dproj, tt, 640)

    unb = lambda t: t.reshape(S5_GROUP, S5_GROUPS, S5_STATE).transpose(1, 2, 0)
    grads = {
        'norm_g': dnorm_g.reshape(D_MODEL), 'w_in': dw_in,
        's5_lam_re': dlam_re.reshape(S5_GROUPS, S5_STATE), 's5_lam_im': dlam_im.reshape(S5_GROUPS, S5_STATE),
        's5_log_dt': dlogdt[0, :S5_GROUPS], 's5_b_re': unb(db_re_t), 's5_b_im': unb(db_im_t),
        's5_c_re': _block_diag_c_t(dc4_re), 's5_c_im': _block_diag_c_t(dc4_im),
        's5_d': dd.reshape(D_S5), 's5_glu_w': dglu_w, 's5_glu_b': dglu_b.reshape(D_S5),
        'rwkv_mu': dmu.reshape(-1), 'rwkv_w0': dw0.reshape(-1), 'rwkv_w2': dw2p[:HEAD], 'rwkv_a0': da0.reshape(-1),
        'rwkv_a2': da2p[HEAD:], 'rwkv_k_k': dk_k.reshape(-1), 'rwkv_k_a': dk_a.reshape(-1),
        'rwkv_r_k': dr_k.reshape(N_HEADS, HEAD), 'rwkv_ln_w': dln_w.reshape(-1), 'rwkv_ln_b': dln_b.reshape(-1),
        'w_out': dw_out, 'final_g': dgf.reshape(D_MODEL),
    }
    return loss, grad_x, grads


def _exchange(arrays, gather, name):
    n = len(arrays)

    def body(*refs):
        send_refs, recv_refs = refs[:n], refs[n:2 * n]
        send_sems, recv_sems, local_sems = refs[2 * n:]
        pos = (lax.axis_index("x"), lax.axis_index("y"), lax.axis_index("c"))
        me = 4 * pos[0] + 2 * pos[1] + pos[2]
        own, outs, arrivals = [], [], []
        for i, (send_ref, recv_ref) in enumerate(zip(send_refs, recv_refs)):
            def block_for(dev, send_ref=send_ref, whole=gather[i]):
                return send_ref if whole else send_ref.at[dev]

            own.append(pltpu.make_async_copy(block_for(me), recv_ref.at[me], local_sems.at[i]))
            own[-1].start()
            for k in range(1, N_DEV):
                peer = tuple(1 - p if (k >> (2 - axis)) & 1 else p for axis, p in enumerate(pos))
                peer_id = 4 * peer[0] + 2 * peer[1] + peer[2]
                sems = dict(send_sem=send_sems.at[i, k - 1], recv_sem=recv_sems.at[i, k - 1],
                            device_id=peer, device_id_type=pl.DeviceIdType.MESH)
                outs.append(pltpu.make_async_remote_copy(src_ref=block_for(peer_id), dst_ref=recv_ref.at[me], **sems))
                outs[-1].start()
                arrivals.append(
                    pltpu.make_async_remote_copy(src_ref=block_for(peer_id), dst_ref=recv_ref.at[peer_id], **sems))
        for copy in arrivals:
            copy.wait_recv()
        for copy in outs:
            copy.wait_send()
        for copy in own:
            copy.wait()

    return pl.pallas_call(
        body, name=name, in_specs=[_ANY] * n, out_specs=[_ANY] * n,
        out_shape=[_sds(((N_DEV,) + a.shape) if whole else a.shape) for a, whole in zip(arrays, gather)],
        scratch_shapes=[pltpu.SemaphoreType.DMA((n, N_DEV - 1)), pltpu.SemaphoreType.DMA((n, N_DEV - 1)),
                        pltpu.SemaphoreType.DMA((n,))],
        compiler_params=pltpu.CompilerParams(has_side_effects=True),
    )(*arrays)


def _sum_devices(ref):
    g = ref[0]
    for s in range(1, N_DEV):
        g = g + ref[s]
    return g


def _adamw_math(g, w, m, v):
    m_new = ADAM_B1 * m + (1.0 - ADAM_B1) * g
    v_new = ADAM_B2 * v + (1.0 - ADAM_B2) * (g * g)
    m_hat = m_new / (1.0 - ADAM_B1 ** ADAM_STEP)
    v_hat = v_new / (1.0 - ADAM_B2 ** ADAM_STEP)
    return -ADAM_LR * (m_hat / (jnp.sqrt(v_hat) + ADAM_EPS) + ADAM_WD * w), m_new, v_new


def _adamw(gs, ws, ms, vs, reduce, name):
    n = len(ws)

    def body(*refs):
        g_refs, w_refs, m_refs, v_refs = (refs[j * n:(j + 1) * n] for j in range(4))
        outs = refs[4 * n:]
        for i in range(n):
            g = _sum_devices(g_refs[i]) if reduce else g_refs[i][...]
            res = _adamw_math(g, w_refs[i][...], m_refs[i][...], v_refs[i][...])
            for j, val in enumerate(((g,) if reduce else ()) + res):
                outs[j * n + i][...] = val

    return pl.pallas_call(
        body, name=name, out_shape=[_sds(w.shape) for w in ws] * (4 if reduce else 3),
        compiler_params=pltpu.CompilerParams(vmem_limit_bytes=VMEM_LIMIT),
    )(*gs, *ws, *ms, *vs)


def _sum_blocks(recv):
    def body(recv_ref, out_ref):
        out_ref[...] = _sum_devices(recv_ref)

    return pl.pallas_call(body, name="sum_small_grads", out_shape=_sds(recv.shape[1:]))(recv)


_WEIGHTS = [
    ('norm_g', (1, 1024), False), ('w_in', (1, 1024, 400), True), ('s5_lam_re', (1, 32, 64), False),
    ('s5_lam_im', (1, 32, 64), False), ('s5_log_dt', (1, 32), False), ('s5_b_re', (1, 32, 64, 16), False),
    ('s5_b_im', (1, 32, 64, 16), False), ('s5_c_re', (1, 32, 16, 64), False), ('s5_c_im', (1, 32, 16, 64), False),
    ('s5_d', (1, 512), False), ('s5_glu_w', (1, 64, 512), True), ('s5_glu_b', (1, 512), False),
    ('rwkv_mu', (1, 1664), False), ('rwkv_w0', (1, 512), False), ('rwkv_w2', (1, 64, 64), True),
    ('rwkv_a0', (1, 512), False), ('rwkv_a2', (1, 64, 64), True), ('rwkv_k_k', (1, 512), False),
    ('rwkv_k_a', (1, 512), False), ('rwkv_r_k', (1, 8, 64), False), ('rwkv_ln_w', (1, 512), False),
    ('rwkv_ln_b', (1, 512), False), ('w_out', (1, 128, 1024), True), ('final_g', (1024,), False),
]
_SHARDED = [(n, s) for n, s, sharded in _WEIGHTS if sharded]
_SMALL = [(n, s) for n, s, sharded in _WEIGHTS if not sharded]
_COLUMN_SHARDED = ('w_in', 'rwkv_w2', 'rwkv_a2')
_SMALL_ROWS = -(-sum(math.prod(s) for _, s in _SMALL) // (8 * LANES)) * 8


def _pack_small(grads):
    flat = [grads[n].reshape(-1) for n, _ in _SMALL]
    pad = _SMALL_ROWS * LANES - sum(f.size for f in flat)
    return jnp.concatenate(flat + [jnp.zeros((pad,), F32)]).reshape(_SMALL_ROWS, LANES)


def _unpack_small(packed):
    flat = packed.reshape(-1)
    out, off = {}, 0
    for n, s in _SMALL:
        size = math.prod(s)
        out[n] = flat[off:off + size].reshape(s)
        off += size
    return out


def _join_shards(name, blocks):
    _, rows, cols = blocks.shape
    if name in _COLUMN_SHARDED:
        return blocks.transpose(1, 0, 2).reshape(rows, N_DEV * cols)
    return blocks.reshape(N_DEV * rows, cols)


def _split_shards(name, full, shard_shape):
    rows, cols = shard_shape
    if name in _COLUMN_SHARDED:
        return full.reshape(rows, N_DEV, cols).transpose(1, 0, 2)
    return full.reshape(N_DEV, rows, cols)


def kernel(x, norm_g, w_in, s5_lam_re, s5_lam_im, s5_log_dt, s5_b_re, s5_b_im, s5_c_re, s5_c_im, s5_d, s5_glu_w, s5_glu_b, rwkv_mu, rwkv_w0, rwkv_w2, rwkv_a0, rwkv_a2, rwkv_k_k, rwkv_k_a, rwkv_r_k, rwkv_ln_w, rwkv_ln_b, w_out, final_g, loss_target, m_norm_g, m_w_in, m_s5_lam_re, m_s5_lam_im, m_s5_log_dt, m_s5_b_re, m_s5_b_im, m_s5_c_re, m_s5_c_im, m_s5_d, m_s5_glu_w, m_s5_glu_b, m_rwkv_mu, m_rwkv_w0, m_rwkv_w2, m_rwkv_a0, m_rwkv_a2, m_rwkv_k_k, m_rwkv_k_a, m_rwkv_r_k, m_rwkv_ln_w, m_rwkv_ln_b, m_w_out, m_final_g, v_norm_g, v_w_in, v_s5_lam_re, v_s5_lam_im, v_s5_log_dt, v_s5_b_re, v_s5_b_im, v_s5_c_re, v_s5_c_im, v_s5_d, v_s5_glu_w, v_s5_glu_b, v_rwkv_mu, v_rwkv_w0, v_rwkv_w2, v_rwkv_a0, v_rwkv_a2, v_rwkv_k_k, v_rwkv_k_a, v_rwkv_r_k, v_rwkv_ln_w, v_rwkv_ln_b, v_w_out, v_final_g):
    given = dict(locals())

    gathered = _exchange([given[n][0] for n, _ in _SHARDED], (True,) * len(_SHARDED), "gather_weights")
    local = {n: _join_shards(n, blocks) for (n, _), blocks in zip(_SHARDED, gathered)}
    local.update({n: (given[n][0] if len(s) > 1 else given[n]) for n, s in _SMALL})

    loss, grad_x, grads = _local_step(x[0], loss_target[0], local)

    recv = _exchange([_split_shards(n, grads[n], s[1:]) for n, s in _SHARDED] + [_pack_small(grads)],
                     (False,) * len(_SHARDED) + (True,), "exchange_grads")

    result = {}
    for group, name in (([0], "adamw_w_in"), ([1, 2, 3, 4], "adamw_shards")):
        ns = [_SHARDED[i][0] for i in group]
        res = _adamw([recv[i] for i in group], [given[n][0] for n in ns], [given['m_' + n][0] for n in ns],
                     [given['v_' + n][0] for n in ns], True, name)
        for j, n in enumerate(ns):
            result[n] = [res[k * len(ns) + j][None] for k in range(4)]
    g_small = _unpack_small(_sum_blocks(recv[-1]))
    two_d = lambda t: t.reshape(1, -1) if t.ndim == 1 else t
    ns = [n for n, _ in _SMALL]
    res = _adamw([two_d(g_small[n]) for n in ns], [two_d(given[n]) for n in ns], [two_d(given['m_' + n]) for n in ns],
                 [two_d(given['v_' + n]) for n in ns], False, "adamw_small")
    for j, (n, s) in enumerate(_SMALL):
        result[n] = [g_small[n]] + [res[k * len(ns) + j].reshape(s) for k in range(3)]

    total = lax.psum(loss[0, 0], ("x", "y", "c"))
    outs = [total, grad_x[None]]
    for k in range(4):
        outs += [result[n][k] for n, _, _ in _WEIGHTS]
    return tuple(outs)
```

```python
import functools
import math

import jax
import jax.numpy as jnp
from jax import lax
from jax.experimental import pallas as pl
from jax.experimental.pallas import tpu as pltpu

F32 = jnp.float32
BF16 = jnp.bfloat16
HI = lax.Precision.HIGH

D_MODEL = 1024
D_S5 = 512
D_RWKV = 512
S5_GROUPS = 32
S5_GROUP = 16
S5_STATE = 64
N_STATE = S5_GROUPS * S5_STATE
N_HEADS = 8
HEAD = 64
D_SHIFT = 3 * D_RWKV + 128
D_IN = 2 * D_S5 + D_SHIFT + D_RWKV
NORM_EPS = 1e-6
GN_EPS = 64e-5
N_DEV = 8
LANES = 128
S5_BLOCKS = 4
S5_SEGMENTS = 32
RWKV_CHUNK = 64
VMEM_LIMIT = 56 * 1024 * 1024

ADAM_LR = 0.001
ADAM_B1 = 0.9
ADAM_B2 = 0.999
ADAM_EPS = 1e-08
ADAM_WD = 0.01
ADAM_STEP = 10


def _dot(a, b, dims, prec):
    return lax.dot_general(a, b, (dims, ((), ())), precision=prec, preferred_element_type=F32)


def _make_mm(cast, prec):
    @jax.custom_vjp
    def mm(a, b):
        return _dot(cast(a), cast(b), ((1,), (0,)), prec)

    def fwd(a, b):
        return mm(a, b), (a, b)

    def bwd(res, g):
        a, b = res
        return (_dot(cast(g), cast(b), ((1,), (1,)), prec), _dot(cast(a), cast(g), ((0,), (0,)), prec))

    mm.defvjp(fwd, bwd)
    return mm


mm_hi = _make_mm(lambda t: t, HI)
mm_bf = _make_mm(lambda t: t.astype(BF16), None)


def _sigmoid(x):
    return 1.0 / (1.0 + jnp.exp(-x))


def _silu(x):
    return x * _sigmoid(x)


def _softplus(x):
    return jnp.maximum(x, 0.0) + jnp.log(1.0 + jnp.exp(-jnp.abs(x)))


def _gelu(x):
    return 0.5 * x * (1.0 + jnp.tanh(math.sqrt(2.0 / math.pi) * (x + 0.044715 * x * x * x)))


def _rms(x, g):
    return x * lax.rsqrt(jnp.mean(x * x, axis=-1, keepdims=True) + NORM_EPS) * g


def _const_spec(shape):
    nd = len(shape)
    return pl.BlockSpec(shape, lambda *_: (0,) * nd, pipeline_mode=pl.Buffered(1))


def _acc_spec(shape):
    nd = len(shape)
    return pl.BlockSpec(shape, lambda *_: (0,) * nd)


def _params(sem):
    return pltpu.CompilerParams(dimension_semantics=(sem,), vmem_limit_bytes=VMEM_LIMIT)


_ANY = pl.BlockSpec(memory_space=pl.ANY)


def _sds(shape):
    return jax.ShapeDtypeStruct(shape, F32)


def _head_sum_matrix():
    i = jnp.arange(D_RWKV) // HEAD
    return (i[:, None] == i[None, :]).astype(F32)


def _s5_param_fn(lam_re, lam_im, logdt, b_re, b_im):
    dt = jnp.exp(logdt)
    mag = jnp.exp(lam_re * dt)
    ang = lam_im * dt
    lbr = mag * jnp.cos(ang)
    lbi = mag * jnp.sin(ang)
    nr = lbr - 1.0
    den = lam_re * lam_re + lam_im * lam_im
    cr = (nr * lam_re + lbi * lam_im) / den
    ci = (lbi * lam_re - nr * lam_im) / den
    return lbr, lbi, cr * b_re - ci * b_im, cr * b_im + ci * b_re


def _s5_param_fwd(lam_re, lam_im, logdt, b_re, b_im):
    def body(lr, li, ld, br, bi, o_lr, o_li, o_br, o_bi):
        outs = _s5_param_fn(lr[...], li[...], ld[...], br[...], bi[...])
        for o, v in zip((o_lr, o_li, o_br, o_bi), outs):
            o[...] = v

    return pl.pallas_call(
        body, name="s5_param_fwd",
        out_shape=[_sds((1, N_STATE))] * 2 + [_sds((S5_GROUP, N_STATE))] * 2,
    )(lam_re, lam_im, logdt, b_re, b_im)


def _s5_param_bwd(lam_re, lam_im, logdt, b_re, b_im, d_lbr, d_lbi, d_bbr, d_bbi, group_ind):
    def body(lr, li, ld, br, bi, g0, g1, g2, g3, ind, o_lr, o_li, o_ld, o_br, o_bi):
        _, vjp = jax.vjp(_s5_param_fn, lr[...], li[...], ld[...], br[...], bi[...])
        d_lr, d_li, d_ld, d_br, d_bi = vjp((g0[...], g1[...], g2[...], g3[...]))
        o_lr[...] = d_lr
        o_li[...] = d_li
        o_ld[...] = _dot(jnp.broadcast_to(d_ld, (8, N_STATE)), ind[...], ((1,), (0,)), HI)
        o_br[...] = d_br
        o_bi[...] = d_bi

    return pl.pallas_call(
        body, name="s5_param_bwd",
        out_shape=[_sds((1, N_STATE))] * 2 + [_sds((8, LANES))] + [_sds((S5_GROUP, N_STATE))] * 2,
    )(lam_re, lam_im, logdt, b_re, b_im, d_lbr, d_lbi, d_bbr, d_bbi, group_ind)


def _fwd_in(x, norm_g, w_in_bf, b4_re, b4_im, tt):
    L = x.shape[0]

    def body(x_ref, g_ref, w_ref, bre_ref, bim_ref, u_ref, zs_ref, rw_ref, zr_ref, bur_ref, bui_ref):
        h = _rms(x_ref[...], g_ref[...])
        proj = jnp.dot(h.astype(BF16), w_ref[...], preferred_element_type=F32)
        u = proj[:, 0:D_S5]
        u_ref[...] = u
        zs_ref[...] = proj[:, D_S5:2 * D_S5]
        rw_ref[...] = proj[:, 2 * D_S5:2 * D_S5 + D_SHIFT]
        zr_ref[...] = proj[:, 2 * D_S5 + D_SHIFT:D_IN]
        for q in range(S5_BLOCKS):
            uq = u[:, q * LANES:(q + 1) * LANES]
            cols = slice(q * 512, (q + 1) * 512)
            bur_ref[:, cols] = _dot(uq, bre_ref[q], ((1,), (0,)), HI)
            bui_ref[:, cols] = _dot(uq, bim_ref[q], ((1,), (0,)), HI)

    row = lambda n: pl.BlockSpec((tt, n), lambda i: (i, 0))
    return pl.pallas_call(
        body, name="fwd_in", grid=(L // tt,),
        in_specs=[row(D_MODEL), _const_spec((1, D_MODEL)), _const_spec((D_MODEL, D_IN)),
                  _const_spec((S5_BLOCKS, LANES, 512)), _const_spec((S5_BLOCKS, LANES, 512))],
        out_specs=[row(D_S5), row(D_S5), row(D_SHIFT), row(D_RWKV), row(N_STATE), row(N_STATE)],
        out_shape=[_sds((L, D_S5)), _sds((L, D_S5)), _sds((L, D_SHIFT)), _sds((L, D_RWKV)),
                   _sds((L, N_STATE)), _sds((L, N_STATE))],
        compiler_params=_params("parallel"),
    )(x, norm_g, w_in_bf, b4_re, b4_im)


def _cmul(ar, ai, br, bi):
    return ar * br - ai * bi, ar * bi + ai * br


def _s5_scan(xr, xi, lam_r, lam_i, reverse):
    L = xr.shape[0]
    nseg = S5_SEGMENTS
    n_g = nseg // 8
    seg = L // nseg
    cb = LANES

    def body(xr_ref, xi_ref, lr_ref, li_ref, or_ref, oi_ref):
        lr = jnp.broadcast_to(lr_ref[...], (8, cb))
        li = jnp.broadcast_to(li_ref[...], (8, cb))

        def rows(i, g):
            return pl.ds(g * 8 * seg + ((seg - 1 - i) if reverse else i), 8, stride=seg)

        def local_scan(i, c):
            out = []
            for g in range(n_g):
                mr, mi = _cmul(lr, li, c[2 * g], c[2 * g + 1])
                sr = mr + xr_ref[rows(i, g), :]
                si = mi + xi_ref[rows(i, g), :]
                or_ref[rows(i, g), :] = sr
                oi_ref[rows(i, g), :] = si
                out += [sr, si]
            return tuple(out)

        zero = jnp.zeros((8, cb), F32)
        ends = lax.fori_loop(0, seg, local_scan, (zero,) * (2 * n_g))

        pr, pi_ = lr[0:1], li[0:1]
        n = 1
        while n < seg:
            pr, pi_ = _cmul(pr, pi_, pr, pi_)
            n *= 2
        rid = lax.broadcasted_iota(jnp.int32, (8, cb), 0)
        carry_r, carry_i = zero[0:1], zero[0:1]
        corr = [zero] * (2 * n_g)
        for j in (range(nseg - 1, -1, -1) if reverse else range(nseg)):
            g, row = divmod(j, 8)
            corr[2 * g] = jnp.where(rid == row, jnp.broadcast_to(carry_r, (8, cb)), corr[2 * g])
            corr[2 * g + 1] = jnp.where(rid == row, jnp.broadcast_to(carry_i, (8, cb)), corr[2 * g + 1])
            mr, mi = _cmul(pr, pi_, carry_r, carry_i)
            carry_r = ends[2 * g][row:row + 1] + mr
            carry_i = ends[2 * g + 1][row:row + 1] + mi

        def add_carry(i, c):
            out = []
            for g in range(n_g):
                cr, ci = _cmul(lr, li, c[2 * g], c[2 * g + 1])
                or_ref[rows(i, g), :] = or_ref[rows(i, g), :] + cr
                oi_ref[rows(i, g), :] = oi_ref[rows(i, g), :] + ci
                out += [cr, ci]
            return tuple(out)

        lax.fori_loop(0, seg, add_carry, tuple(corr))

    assert seg & (seg - 1) == 0
    blk = pl.BlockSpec((L, cb), lambda j: (0, j))
    lam = pl.BlockSpec((1, cb), lambda j: (0, j))
    return pl.pallas_call(
        body, name="s5_scan_bwd" if reverse else "s5_scan_fwd", grid=(N_STATE // cb,),
        in_specs=[blk, blk, lam, lam], out_specs=[blk, blk],
        out_shape=[_sds((L, N_STATE))] * 2,
        input_output_aliases={0: 0, 1: 1},
        compiler_params=_params("parallel"),
    )(xr, xi, lam_r, lam_i)


def _rwkv_pre_fn(r, k, v, wa, w0, w2p, a0, a2p, k_k, k_a, ee):
    w = -_softplus(-(w0 + mm_hi(jnp.tanh(wa), w2p))) - 0.5
    logw = -jnp.exp(w)
    a = _sigmoid(a0 + mm_hi(wa, a2p))
    kkp = k * k_k
    kk = kkp / jnp.maximum(jnp.sqrt(mm_hi(kkp * kkp, ee)), 1e-12)
    k2 = k * (1.0 + (a - 1.0) * k_a)
    return r, logw, k2, v, -kk, kk * a


def _head_spec(tt):
    return pl.BlockSpec((N_HEADS, tt, HEAD), lambda i: (0, i, 0))


def _load_heads(ref):
    return jnp.concatenate([ref[h] for h in range(N_HEADS)], axis=-1)


def _store_heads(ref, val):
    for h in range(N_HEADS):
        ref[h] = val[:, h * HEAD:(h + 1) * HEAD]


def _shifted(rw, prev_blk, first):
    rolled = pltpu.roll(rw, 1, axis=0)
    prev_row = jnp.where(first, 0.0, prev_blk[7:8, :])
    rid = lax.broadcasted_iota(jnp.int32, rw.shape, 0)
    return jnp.where(rid == 0, jnp.broadcast_to(prev_row, rw.shape), rolled)


def _split_rw(t):
    return t[:, 0:512], t[:, 512:1024], t[:, 1024:1536], t[:, 1536:1664]


def _rwkv_pre_specs(tt):
    row = pl.BlockSpec((tt, D_SHIFT), lambda i: (i, 0))
    prev = pl.BlockSpec((8, D_SHIFT), lambda i: (jnp.maximum(i * (tt // 8) - 1, 0), 0))
    consts = [_const_spec((1, D_SHIFT)), _const_spec((1, D_RWKV)), _const_spec((LANES, D_RWKV)),
              _const_spec((1, D_RWKV)), _const_spec((LANES, D_RWKV)), _const_spec((1, D_RWKV)),
              _const_spec((1, D_RWKV)), _const_spec((D_RWKV, D_RWKV))]
    return [row, prev] + consts


def _rwkv_pre_fwd(rw, mu, w0, w2p, a0, a2p, k_k, k_a, ee, tt):
    L = rw.shape[0]

    def body(rw_ref, prev_ref, mu_ref, w0_ref, w2_ref, a0_ref, a2_ref, kk_ref, ka_ref, ee_ref, *outs):
        rwv = rw_ref[...]
        rws = rwv + (_shifted(rwv, prev_ref[...], pl.program_id(0) == 0) - rwv) * mu_ref[...]
        res = _rwkv_pre_fn(*_split_rw(rws), w0_ref[...], w2_ref[...], a0_ref[...], a2_ref[...],
                           kk_ref[...], ka_ref[...], ee_ref[...])
        for o, val in zip(outs, res):
            _store_heads(o, val)

    return pl.pallas_call(
        body, name="rwkv_pre_fwd", grid=(L // tt,),
        in_specs=_rwkv_pre_specs(tt), out_specs=[_head_spec(tt)] * 6, out_shape=[_sds((N_HEADS, L, HEAD))] * 6,
        compiler_params=_params("parallel"),
    )(rw, rw, mu, w0, w2p, a0, a2p, k_k, k_a, ee)


def _rwkv_pre_bwd(rw, mu, w0, w2p, a0, a2p, k_k, k_a, ee, cots, tt):
    L = rw.shape[0]
    n_t = L // tt

    def body(rw_ref, prev_ref, mu_ref, w0_ref, w2_ref, a0_ref, a2_ref, kk_ref, ka_ref, ee_ref,
             c_r, c_w, c_k, c_v, c_a, c_b, cb_r, cb_k, cb_v,
             drws_ref, dmu_o, dw0_o, dw2_o, da0_o, da2_o, dkk_o, dka_o,
             dmu, dw0, dw2, da0, da2, dkk, dka):
        i = pl.program_id(0)
        accs = (dmu, dw0, dw2, da0, da2, dkk, dka)

        @pl.when(i == 0)
        def _():
            for acc in accs:
                acc[...] = jnp.zeros_like(acc)

        rwv = rw_ref[...]
        diff = _shifted(rwv, prev_ref[...], i == 0) - rwv
        rws = rwv + diff * mu_ref[...]
        consts = (w0_ref[...], w2_ref[...], a0_ref[...], a2_ref[...], kk_ref[...], ka_ref[...])
        _, vjp = jax.vjp(lambda *a: _rwkv_pre_fn(*a, ee_ref[...]), *_split_rw(rws), *consts)
        scan = [_load_heads(c) for c in (c_r, c_w, c_k, c_v, c_a, c_b)]
        g = vjp((scan[0] + cb_r[...], scan[1], scan[2] + cb_k[...], scan[3] + cb_v[...], scan[4], scan[5]))
        drws = jnp.concatenate(g[0:4], axis=-1)
        drws_ref[...] = drws
        dmu[...] += jnp.sum(drws * diff, axis=0, keepdims=True)
        for acc, val in zip(accs[1:], g[4:]):
            acc[...] += val

        @pl.when(i == n_t - 1)
        def _():
            for acc, out in zip(accs, (dmu_o, dw0_o, dw2_o, da0_o, da2_o, dkk_o, dka_o)):
                out[...] = acc[...]

    row = pl.BlockSpec((tt, D_RWKV), lambda i: (i, 0))
    shapes = [(1, D_SHIFT), (1, D_RWKV), (LANES, D_RWKV), (1, D_RWKV), (LANES, D_RWKV), (1, D_RWKV), (1, D_RWKV)]
    return pl.pallas_call(
        body, name="rwkv_pre_bwd", grid=(n_t,),
        in_specs=_rwkv_pre_specs(tt) + [_head_spec(tt)] * 6 + [row] * 3,
        out_specs=[pl.BlockSpec((tt, D_SHIFT), lambda i: (i, 0))] + [_acc_spec(s) for s in shapes],
        out_shape=[_sds((L, D_SHIFT))] + [_sds(s) for s in shapes],
        scratch_shapes=[pltpu.VMEM(s, F32) for s in shapes],
        compiler_params=_params("arbitrary"),
    )(rw, rw, mu, w0, w2p, a0, a2p, k_k, k_a, ee, *cots)


def _bmm(a, b):
    return lax.dot_general(a, b, (((2,), (1,)), ((0,), (0,))), precision=HI, preferred_element_type=F32)


def _bmm_nt(a, b):
    return lax.dot_general(a, b, (((2,), (2,)), ((0,), (0,))), precision=HI, preferred_element_type=F32)


def _bmm_tn(a, b):
    return lax.dot_general(a, b, (((1,), (1,)), ((0,), (0,))), precision=HI, preferred_element_type=F32)


def _unit_lower_inverse(a):
    t = a.shape[-1]
    ti = lax.broadcasted_iota(jnp.int32, (t, t), 0)
    si = lax.broadcasted_iota(jnp.int32, (t, t), 1)
    inv = jnp.where(ti == si, 1.0, 0.0)[None] + a
    pw = a
    n = 1
    while 2 * n < t:
        pw = _bmm(pw, pw)
        inv = inv + _bmm(inv, pw)
        n *= 2
    return inv


@jax.custom_vjp
def _solve_unit_lower(a, rhs, inv):
    return _bmm(inv, rhs)


def _solve_fwd(a, rhs, inv):
    u = _bmm(inv, rhs)
    return u, (inv, u)


def _solve_bwd(res, du):
    inv, u = res
    d_rhs = _bmm_tn(inv, du)
    return _bmm_nt(d_rhs, u), d_rhs, jnp.zeros_like(inv)


_solve_unit_lower.defvjp(_solve_fwd, _solve_bwd)


def _rwkv_chunk(st0, r, logw, k, v, a, b, inv=None):
    n_h, t, _ = r.shape
    ti = lax.broadcasted_iota(jnp.int32, (t, t), 0)
    si = lax.broadcasted_iota(jnp.int32, (t, t), 1)
    incl = (ti >= si)[None]
    strict = (ti > si)[None]
    ones_tri = jnp.broadcast_to(jnp.where(ti >= si, 1.0, 0.0)[None], (n_h, t, t))
    log_p = _bmm(ones_tri, logw)
    p_in = jnp.exp(log_p)
    p_inv = jnp.exp(-log_p)
    at = a * jnp.exp(log_p - logw)
    rt = r * p_in
    bt = b * p_inv
    kt = k * p_inv
    a_ab = jnp.where(strict, _bmm_nt(at, bt), 0.0)
    a_ak = jnp.where(strict, _bmm_nt(at, kt), 0.0)
    if inv is None:
        inv = _unit_lower_inverse(a_ab)
    u = _solve_unit_lower(a_ab, _bmm(at, st0) + _bmm(a_ak, v), inv)
    y = (_bmm(rt, st0) + _bmm(jnp.where(incl, _bmm_nt(rt, bt), 0.0), u)
         + _bmm(jnp.where(incl, _bmm_nt(rt, kt), 0.0), v))
    p_end = jnp.swapaxes(p_in[:, t - 1:t, :], 1, 2)
    st1 = (st0 + _bmm_tn(bt, u) + _bmm_tn(kt, v)) * p_end
    return y, st1, inv


def _rwkv_scan_fwd(ops):
    n_h, L, n = ops[0].shape
    t = RWKV_CHUNK
    n_c = L // t

    def body(r_ref, w_ref, k_ref, v_ref, a_ref, b_ref, y_ref, st_ref, inv_ref, st):
        @pl.when(pl.program_id(0) == 0)
        def _():
            st[...] = jnp.zeros_like(st)

        st0 = st[...]
        st_ref[0] = st0
        y, st1, inv = _rwkv_chunk(st0, r_ref[...], w_ref[...], k_ref[...], v_ref[...], a_ref[...], b_ref[...])
        y_ref[...] = y
        inv_ref[0] = inv
        st[...] = st1

    blk = pl.BlockSpec((n_h, t, n), lambda c: (0, c, 0))
    return pl.pallas_call(
        body, name="rwkv_scan_fwd", grid=(n_c,), in_specs=[blk] * 6,
        out_specs=[blk, pl.BlockSpec((1, n_h, n, n), lambda c: (c, 0, 0, 0)),
                   pl.BlockSpec((1, n_h, t, t), lambda c: (c, 0, 0, 0))],
        out_shape=[_sds((n_h, L, n)), _sds((n_c, n_h, n, n)), _sds((n_c, n_h, t, t))],
        scratch_shapes=[pltpu.VMEM((n_h, n, n), F32)],
        compiler_params=_params("arbitrary"),
    )(*ops)


def _rwkv_scan_bwd(ops, states, invs, dy):
    n_h, L, n = ops[0].shape
    t = RWKV_CHUNK
    n_c = L // t

    def body(r_ref, w_ref, k_ref, v_ref, a_ref, b_ref, st_ref, inv_ref, dy_ref, dr, dw, dk, dv, da, db, dst):
        @pl.when(pl.program_id(0) == 0)
        def _():
            dst[...] = jnp.zeros_like(dst)

        inv = inv_ref[0]
        _, vjp = jax.vjp(lambda *a: _rwkv_chunk(*a, inv=inv)[:2], st_ref[0], r_ref[...], w_ref[...], k_ref[...],
                         v_ref[...], a_ref[...], b_ref[...])
        g = vjp((dy_ref[...], dst[...]))
        dst[...] = g[0]
        for out, val in zip((dr, dw, dk, dv, da, db), g[1:]):
            out[...] = val

    blk = pl.BlockSpec((n_h, t, n), lambda c: (0, n_c - 1 - c, 0))
    per_chunk = lambda m: pl.BlockSpec((1, n_h, m, m), lambda c: (n_c - 1 - c, 0, 0, 0))
    return pl.pallas_call(
        body, name="rwkv_scan_bwd", grid=(n_c,),
        in_specs=[blk] * 6 + [per_chunk(n), per_chunk(t), blk],
        out_specs=[blk] * 6, out_shape=[_sds((n_h, L, n))] * 6,
        scratch_shapes=[pltpu.VMEM((n_h, n, n), F32)],
        compiler_params=_params("arbitrary"),
    )(*ops, states, invs, dy)


def _post_fn(x, u, zs, zr, ysc, r, k2, v, s_re, s_im, c_re, c_im, d, glu_w, glu_b, ln_w, ln_b, r_k,
             wo_s5, wo_rwkv, gf, tgt, ee):
    y_ssm = jnp.concatenate(
        [mm_hi(s_re[q], c_re[q]) - mm_hi(s_im[q], c_im[q]) for q in range(S5_BLOCKS)], axis=-1)
    y3 = _gelu(y_ssm + d * u)
    y_s5 = y3 * _sigmoid(mm_bf(y3, glu_w) + glu_b) * _silu(zs)
    mean = mm_hi(ysc, ee) * (1.0 / HEAD)
    yc = ysc - mean
    var = mm_hi(yc * yc, ee) * (1.0 / HEAD)
    gn = yc * lax.rsqrt(var + GN_EPS) * ln_w + ln_b
    bonus = mm_hi(r * k2 * r_k, ee) * v
    y_rwkv = (gn + bonus) * _silu(zr)
    x2 = x + mm_bf(y_s5, wo_s5) + mm_bf(y_rwkv, wo_rwkv)
    err = _rms(x2, gf) - tgt
    return 0.5 * jnp.mean(err * err, axis=-1, keepdims=True)


def _post(x, u, zs, zr, ysc, r, k2, v, s_re, s_im, c4_re, c4_im, d, glu_w, glu_b, ln_w, ln_b, r_k,
          w_out, gf, tgt, ee, tt):
    L = x.shape[0]
    n_t = L // tt
    acc_shapes = [(S5_BLOCKS, 512, LANES), (S5_BLOCKS, 512, LANES), (1, D_S5), (D_S5, D_S5), (1, D_S5),
                  (1, D_RWKV), (1, D_RWKV), (1, D_RWKV), (D_MODEL, D_MODEL), (1, D_MODEL), (8, LANES)]

    def body(x_ref, u_ref, zs_ref, zr_ref, ysc_ref, r_ref, k2_ref, v_ref, sre_ref, sim_ref,
             cre_ref, cim_ref, d_ref, gw_ref, gb_ref, lw_ref, lb_ref, rk_ref, wo_ref, gf_ref, tgt_ref, ee_ref,
             dx_o, du_o, dzs_o, dzr_o, dysc_o, dr_o, dk2_o, dv_o, gre_o, gim_o,
             dcre_o, dcim_o, dd_o, dgw_o, dgb_o, dlw_o, dlb_o, drk_o, dwo_o, dgf_o, loss_o,
             dcre, dcim, dd, dgw, dgb, dlw, dlb, drk, dwo, dgf, loss):
        i = pl.program_id(0)
        accs = (dcre, dcim, dd, dgw, dgb, dlw, dlb, drk, dwo, dgf, loss)

        @pl.when(i == 0)
        def _():
            for acc in accs:
                acc[...] = jnp.zeros_like(acc)

        blocks = [slice(q * 512, (q + 1) * 512) for q in range(S5_BLOCKS)]
        args = (x_ref[...], u_ref[...], zs_ref[...], zr_ref[...],
                _load_heads(ysc_ref), _load_heads(r_ref), _load_heads(k2_ref), _load_heads(v_ref),
                [sre_ref[:, c] for c in blocks], [sim_ref[:, c] for c in blocks],
                [cre_ref[q] for q in range(S5_BLOCKS)], [cim_ref[q] for q in range(S5_BLOCKS)],
                d_ref[...], gw_ref[...], gb_ref[...], lw_ref[...], lb_ref[...], rk_ref[...],
                wo_ref[0:D_S5, :], wo_ref[D_S5:D_MODEL, :], gf_ref[...])
        rows, vjp = jax.vjp(lambda *a: _post_fn(*a, tgt_ref[...], ee_ref[...]), *args)
        g = vjp(jnp.ones_like(rows))
        for out, val in zip((dx_o, du_o, dzs_o, dzr_o), g[0:4]):
            out[...] = val
        _store_heads(dysc_o, g[4])
        for out, val in zip((dr_o, dk2_o, dv_o), g[5:8]):
            out[...] = val
        for q in range(S5_BLOCKS):
            gre_o[:, blocks[q]] = g[8][q]
            gim_o[:, blocks[q]] = g[9][q]
            dcre[q] += g[10][q]
            dcim[q] += g[11][q]
        for acc, val in zip((dd, dgw, dgb, dlw, dlb, drk), g[12:18]):
            acc[...] += val
        dwo[0:D_S5, :] += g[18]
        dwo[D_S5:D_MODEL, :] += g[19]
        dgf[...] += g[20]
        loss[...] += jnp.broadcast_to(jnp.sum(rows, axis=0, keepdims=True), loss.shape)

        @pl.when(i == n_t - 1)
        def _():
            for acc, out in zip(accs, (dcre_o, dcim_o, dd_o, dgw_o, dgb_o, dlw_o, dlb_o, drk_o, dwo_o, dgf_o, loss_o)):
                pltpu.sync_copy(acc, out)

    row = lambda n: pl.BlockSpec((tt, n), lambda i: (i, 0))
    in_specs = ([row(D_MODEL)] + [row(512)] * 3 + [_head_spec(tt)] * 4 + [row(N_STATE)] * 2
                + [_const_spec((S5_BLOCKS, 512, LANES))] * 2
                + [_const_spec(s) for s in [(1, D_S5), (D_S5, D_S5), (1, D_S5), (1, D_RWKV), (1, D_RWKV), (1, D_RWKV),
                                            (D_MODEL, D_MODEL), (1, D_MODEL)]]
                + [row(D_MODEL), _const_spec((D_RWKV, D_RWKV))])
    out_rows = [D_MODEL] + [512] * 3 + [None] + [512] * 3 + [N_STATE] * 2
    return pl.pallas_call(
        body, name="post_fwd_bwd", grid=(n_t,), in_specs=in_specs,
        out_specs=[row(n) if n else _head_spec(tt) for n in out_rows] + [_ANY] * len(acc_shapes),
        out_shape=([_sds((L, n)) if n else _sds((N_HEADS, L, HEAD)) for n in out_rows]
                   + [_sds(s) for s in acc_shapes]),
        scratch_shapes=[pltpu.VMEM(s, F32) for s in acc_shapes],
        compiler_params=_params("arbitrary"),
    )(x, u, zs, zr, ysc, r, k2, v, s_re, s_im, c4_re, c4_im, d, glu_w, glu_b, ln_w, ln_b, r_k, w_out, gf, tgt, ee)


def _s5_bwd(u, du_direct, s_re, s_im, g_re, g_im, b4_re, b4_im, tt):
    L = u.shape[0]
    n_t = L // tt
    acc_shapes = [(S5_BLOCKS, LANES, 512), (S5_BLOCKS, LANES, 512), (1, N_STATE), (1, N_STATE)]

    def body(u_ref, dud_ref, sre_ref, sim_ref, gre_ref, gim_ref, bre_ref, bim_ref,
             du_o, dbre_o, dbim_o, dlr_o, dli_o, dbre, dbim, dlr, dli, last_r, last_i):
        i = pl.program_id(0)

        @pl.when(i == 0)
        def _():
            for acc in (dbre, dbim, dlr, dli, last_r, last_i):
                acc[...] = jnp.zeros_like(acc)

        uv = u_ref[...]
        gr = gre_ref[...]
        gi = gim_ref[...]
        pieces = []
        for q in range(S5_BLOCKS):
            cols = slice(q * 512, (q + 1) * 512)
            uq = uv[:, q * LANES:(q + 1) * LANES]
            pieces.append(_dot(gr[:, cols], bre_ref[q], ((1,), (1,)), HI) + _dot(gi[:, cols], bim_ref[q], ((1,), (1,)), HI))
            dbre[q] += _dot(uq, gr[:, cols], ((0,), (0,)), HI)
            dbim[q] += _dot(uq, gi[:, cols], ((0,), (0,)), HI)
        du_o[...] = dud_ref[...] + jnp.concatenate(pieces, axis=-1)
        sr = sre_ref[...]
        si = sim_ref[...]
        rid = lax.broadcasted_iota(jnp.int32, sr.shape, 0)
        pr = jnp.where(rid == 0, jnp.broadcast_to(last_r[7:8, :], sr.shape), pltpu.roll(sr, 1, axis=0))
        pi_ = jnp.where(rid == 0, jnp.broadcast_to(last_i[7:8, :], si.shape), pltpu.roll(si, 1, axis=0))
        dlr[...] += jnp.sum(pr * gr + pi_ * gi, axis=0, keepdims=True)
        dli[...] += jnp.sum(pr * gi - pi_ * gr, axis=0, keepdims=True)
        last_r[...] = sr[tt - 8:tt, :]
        last_i[...] = si[tt - 8:tt, :]

        @pl.when(i == n_t - 1)
        def _():
            for acc, out in zip((dbre, dbim, dlr, dli), (dbre_o, dbim_o, dlr_o, dli_o)):
                out[...] = acc[...]

    row = lambda n: pl.BlockSpec((tt, n), lambda i: (i, 0))
    return pl.pallas_call(
        body, name="s5_bwd", grid=(n_t,),
        in_specs=[row(D_S5), row(D_S5)] + [row(N_STATE)] * 4 + [_const_spec((S5_BLOCKS, LANES, 512))] * 2,
        out_specs=[row(D_S5)] + [_acc_spec(s) for s in acc_shapes],
        out_shape=[_sds((L, D_S5))] + [_sds(s) for s in acc_shapes],
        scratch_shapes=[pltpu.VMEM(s, F32) for s in acc_shapes] + [pltpu.VMEM((8, N_STATE), F32)] * 2,
        compiler_params=_params("arbitrary"),
    )(u, du_direct, s_re, s_im, g_re, g_im, b4_re, b4_im)


def _bwd_in(x, norm_g, w_in_bf, mu, dx2, du, dzs, drws, dzr, tt):
    L = x.shape[0]
    n_t = L // tt

    def body(x_ref, g_ref, w_ref, mu_ref, dx2_ref, du_ref, dzs_ref, drws_ref, nxt_ref, dzr_ref,
             gx_o, dproj_o, dg_o, dg):
        i = pl.program_id(0)

        @pl.when(i == 0)
        def _():
            dg[...] = jnp.zeros_like(dg)

        drws_v = drws_ref[...]
        rid = lax.broadcasted_iota(jnp.int32, drws_v.shape, 0)
        nxt_row = jnp.where(i == n_t - 1, 0.0, nxt_ref[0:1, :])
        nxt = jnp.where(rid == tt - 1, jnp.broadcast_to(nxt_row, drws_v.shape), pltpu.roll(drws_v, tt - 1, axis=0))
        muv = mu_ref[...]
        drw = drws_v * (1.0 - muv) + nxt * muv
        dproj_o[:, 0:D_S5] = du_ref[...]
        dproj_o[:, D_S5:2 * D_S5] = dzs_ref[...]
        dproj_o[:, 2 * D_S5:2 * D_S5 + D_SHIFT] = drw
        dproj_o[:, 2 * D_S5 + D_SHIFT:D_IN] = dzr_ref[...]
        dh = _dot(dproj_o[...].astype(BF16), w_ref[...], ((1,), (1,)), None)
        _, vjp = jax.vjp(_rms, x_ref[...], g_ref[...])
        dxh, dgv = vjp(dh)
        gx_o[...] = dx2_ref[...] + dxh
        dg[...] += dgv

        @pl.when(i == n_t - 1)
        def _():
            dg_o[...] = dg[...]

    row = lambda n: pl.BlockSpec((tt, n), lambda i: (i, 0))
    nxt = pl.BlockSpec((8, D_SHIFT), lambda i: (jnp.minimum((i + 1) * (tt // 8), L // 8 - 1), 0))
    return pl.pallas_call(
        body, name="bwd_in", grid=(n_t,),
        in_specs=[row(D_MODEL), _const_spec((1, D_MODEL)), _const_spec((D_MODEL, D_IN)), _const_spec((1, D_SHIFT)),
                  row(D_MODEL), row(D_S5), row(D_S5), row(D_SHIFT), nxt, row(D_RWKV)],
        out_specs=[row(D_MODEL), row(D_IN), _acc_spec((1, D_MODEL))],
        out_shape=[_sds((L, D_MODEL)), _sds((L, D_IN)), _sds((1, D_MODEL))],
        scratch_shapes=[pltpu.VMEM((1, D_MODEL), F32)],
        compiler_params=_params("arbitrary"),
    )(x, norm_g, w_in_bf, mu, dx2, du, dzs, drws, drws, dzr)


def _grad_w_in(x, norm_g, dproj, tt, cn):
    L = x.shape[0]

    def body(x_ref, g_ref, dp_ref, out_ref):
        @pl.when(pl.program_id(1) == 0)
        def _():
            out_ref[...] = jnp.zeros_like(out_ref)

        h = _rms(x_ref[...], g_ref[...])
        out_ref[...] += _dot(h.astype(BF16), dp_ref[...].astype(BF16), ((0,), (0,)), None)

    return pl.pallas_call(
        body, name="grad_w_in", grid=(D_IN // cn, L // tt),
        in_specs=[pl.BlockSpec((tt, D_MODEL), lambda j, i: (i, 0)), pl.BlockSpec((1, D_MODEL), lambda j, i: (0, 0)),
                  pl.BlockSpec((tt, cn), lambda j, i: (i, j))],
        out_specs=pl.BlockSpec((D_MODEL, cn), lambda j, i: (0, j)),
        out_shape=_sds((D_MODEL, D_IN)),
        compiler_params=pltpu.CompilerParams(dimension_semantics=("parallel", "arbitrary"), vmem_limit_bytes=VMEM_LIMIT),
    )(x, norm_g, dproj)


def _block_diag_b(bbar):
    bb = bbar.reshape(S5_GROUP, S5_BLOCKS, 8, S5_STATE)
    return jnp.einsum('hqgp,Gg->qGhgp', bb, jnp.eye(8, dtype=F32)).reshape(S5_BLOCKS, LANES, 512)


def _block_diag_b_t(db4):
    d = db4.reshape(S5_BLOCKS, 8, S5_GROUP, 8, S5_STATE)
    return jnp.einsum('qGhgp,Gg->hqgp', d, jnp.eye(8, dtype=F32)).reshape(S5_GROUP, N_STATE)


def _block_diag_c(c):
    cc = c.reshape(S5_BLOCKS, 8, S5_GROUP, S5_STATE)
    return jnp.einsum('qghp,gG->qgpGh', cc, jnp.eye(8, dtype=F32)).reshape(S5_BLOCKS, 512, LANES)


def _block_diag_c_t(dc4):
    d = dc4.reshape(S5_BLOCKS, 8, S5_STATE, 8, S5_GROUP)
    return jnp.einsum('qgpGh,gG->qghp', d, jnp.eye(8, dtype=F32)).reshape(S5_GROUPS, S5_GROUP, S5_STATE)


def _local_step(x, tgt, w):
    L = x.shape[0]
    tt = min(256, L)
    tp = min(128, L)
    ee = _head_sum_matrix()

    lam_re = w['s5_lam_re'].reshape(1, N_STATE)
    lam_im = w['s5_lam_im'].reshape(1, N_STATE)
    logdt = jnp.repeat(w['s5_log_dt'], S5_STATE).reshape(1, N_STATE)
    b_re_t = w['s5_b_re'].transpose(2, 0, 1).reshape(S5_GROUP, N_STATE)
    b_im_t = w['s5_b_im'].transpose(2, 0, 1).reshape(S5_GROUP, N_STATE)
    lbr, lbi, bbr, bbi = _s5_param_fwd(lam_re, lam_im, logdt, b_re_t, b_im_t)
    b4_re, b4_im = _block_diag_b(bbr), _block_diag_b(bbi)
    c4_re, c4_im = _block_diag_c(w['s5_c_re']), _block_diag_c(w['s5_c_im'])

    norm_g = w['norm_g'].reshape(1, D_MODEL)
    w_in_bf = w['w_in'].astype(BF16)
    u, zs, rw, zr, bu_re, bu_im = _fwd_in(x, norm_g, w_in_bf, b4_re, b4_im, tt)
    s_re, s_im = _s5_scan(bu_re, bu_im, lbr, lbi, reverse=False)

    row = lambda t: t.reshape(1, -1)
    zpad = jnp.zeros((HEAD, D_RWKV), F32)
    w2p = jnp.concatenate([w['rwkv_w2'], zpad], axis=0)
    a2p = jnp.concatenate([zpad, w['rwkv_a2']], axis=0)
    pre_consts = (row(w['rwkv_mu']), row(w['rwkv_w0']), w2p, row(w['rwkv_a0']), a2p,
                  row(w['rwkv_k_k']), row(w['rwkv_k_a']), ee)
    ops = _rwkv_pre_fwd(rw, *pre_consts, tt)
    ysc, states, invs = _rwkv_scan_fwd(ops)

    post = _post(x, u, zs, zr, ysc, ops[0], ops[2], ops[3], s_re, s_im, c4_re, c4_im,
                 row(w['s5_d']), w['s5_glu_w'], row(w['s5_glu_b']), row(w['rwkv_ln_w']), row(w['rwkv_ln_b']),
                 row(w['rwkv_r_k']), w['w_out'], row(w['final_g']), tgt, ee, tp)
    (dx2, du_d, dzs, dzr, dysc, dr_b, dk2_b, dv_b, g_re, g_im,
     dc4_re, dc4_im, dd, dglu_w, dglu_b, dln_w, dln_b, dr_k, dw_out, dgf, loss) = post

    gt_re, gt_im = _s5_scan(g_re, g_im, lbr, -lbi, reverse=True)
    du, db4_re, db4_im, dlbr, dlbi = _s5_bwd(u, du_d, s_re, s_im, gt_re, gt_im, b4_re, b4_im, tt)
    group_ind = (jnp.arange(N_STATE)[:, None] // S5_STATE == jnp.arange(LANES)[None, :]).astype(F32)
    dlam_re, dlam_im, dlogdt, db_re_t, db_im_t = _s5_param_bwd(
        lam_re, lam_im, logdt, b_re_t, b_im_t, dlbr, dlbi, _block_diag_b_t(db4_re), _block_diag_b_t(db4_im), group_ind)

    cots = list(_rwkv_scan_bwd(ops, states, invs, dysc)) + [dr_b, dk2_b, dv_b]
    drws, dmu, dw0, dw2p, da0, da2p, dk_k, dk_a = _rwkv_pre_bwd(rw, *pre_consts, cots, tt)

    grad_x, dproj, dnorm_g = _bwd_in(x, norm_g, w_in_bf, row(w['rwkv_mu']), dx2, du, dzs, drws, dzr, tt)
    dw_in = _grad_w_in(x, norm_g, dproj, tt, 640)

    unb = lambda t: t.reshape(S5_GROUP, S5_GROUPS, S5_STATE).transpose(1, 2, 0)
    grads = {
        'norm_g': dnorm_g.reshape(D_MODEL), 'w_in': dw_in,
        's5_lam_re': dlam_re.reshape(S5_GROUPS, S5_STATE), 's5_lam_im': dlam_im.reshape(S5_GROUPS, S5_STATE),
        's5_log_dt': dlogdt[0, :S5_GROUPS], 's5_b_re': unb(db_re_t), 's5_b_im': unb(db_im_t),
        's5_c_re': _block_diag_c_t(dc4_re), 's5_c_im': _block_diag_c_t(dc4_im),
        's5_d': dd.reshape(D_S5), 's5_glu_w': dglu_w, 's5_glu_b': dglu_b.reshape(D_S5),
        'rwkv_mu': dmu.reshape(-1), 'rwkv_w0': dw0.reshape(-1), 'rwkv_w2': dw2p[:HEAD], 'rwkv_a0': da0.reshape(-1),
        'rwkv_a2': da2p[HEAD:], 'rwkv_k_k': dk_k.reshape(-1), 'rwkv_k_a': dk_a.reshape(-1),
        'rwkv_r_k': dr_k.reshape(N_HEADS, HEAD), 'rwkv_ln_w': dln_w.reshape(-1), 'rwkv_ln_b': dln_b.reshape(-1),
        'w_out': dw_out, 'final_g': dgf.reshape(D_MODEL),
    }
    return loss, grad_x, grads


def _exchange(arrays, gather, axes, name):
    n = len(arrays)
    group = 2 ** len(axes)

    def body(*refs):
        send_refs, recv_refs = refs[:n], refs[n:2 * n]
        send_sems, recv_sems, local_sems = refs[2 * n:]
        pos = {ax: lax.axis_index(ax) for ax in ("x", "y", "c")}

        def index_of(p):
            idx = 0
            for ax in axes:
                idx = 2 * idx + p[ax]
            return idx

        me = index_of(pos)
        own, outs, arrivals = [], [], []
        for i, (send_ref, recv_ref) in enumerate(zip(send_refs, recv_refs)):
            def block_for(dev, send_ref=send_ref, whole=gather[i]):
                return send_ref if whole else send_ref.at[dev]

            own.append(pltpu.make_async_copy(block_for(me), recv_ref.at[me], local_sems.at[i]))
            own[-1].start()
            for k in range(1, group):
                peer = dict(pos)
                for bit, ax in enumerate(axes):
                    if (k >> bit) & 1:
                        peer[ax] = 1 - pos[ax]
                peer_idx = index_of(peer)
                sems = dict(send_sem=send_sems.at[i, k - 1], recv_sem=recv_sems.at[i, k - 1],
                            device_id=(peer["x"], peer["y"], peer["c"]), device_id_type=pl.DeviceIdType.MESH)
                outs.append(pltpu.make_async_remote_copy(src_ref=block_for(peer_idx), dst_ref=recv_ref.at[me], **sems))
                outs[-1].start()
                arrivals.append(
                    pltpu.make_async_remote_copy(src_ref=block_for(peer_idx), dst_ref=recv_ref.at[peer_idx], **sems))
        for copy in arrivals:
            copy.wait_recv()
        for copy in outs:
            copy.wait_send()
        for copy in own:
            copy.wait()

    return pl.pallas_call(
        body, name=name, in_specs=[_ANY] * n, out_specs=[_ANY] * n,
        out_shape=[jax.ShapeDtypeStruct(((group,) + a.shape) if whole else a.shape, a.dtype)
                   for a, whole in zip(arrays, gather)],
        scratch_shapes=[pltpu.SemaphoreType.DMA((n, group - 1)), pltpu.SemaphoreType.DMA((n, group - 1)),
                        pltpu.SemaphoreType.DMA((n,))],
        compiler_params=pltpu.CompilerParams(has_side_effects=True),
    )(*arrays)


def _sum_devices(ref):
    g = ref[0].astype(F32)
    for s in range(1, ref.shape[0]):
        g = g + ref[s].astype(F32)
    return g


def _pair_sum(arrays, dtypes):
    n = len(arrays)

    def body(*refs):
        for src, dst in zip(refs[:n], refs[n:]):
            dst[...] = (src[0] + src[1]).astype(dst.dtype)

    tail = lambda a: (0,) * (a.ndim - 2)
    return pl.pallas_call(
        body, name="pair_sum", grid=(4,),
        in_specs=[pl.BlockSpec((2, 1) + a.shape[2:], lambda j, a=a: (0, j) + tail(a)) for a in arrays],
        out_specs=[pl.BlockSpec((1,) + a.shape[2:], lambda j, a=a: (j,) + tail(a)) for a in arrays],
        out_shape=[jax.ShapeDtypeStruct(a.shape[1:], dt) for a, dt in zip(arrays, dtypes)],
        compiler_params=_params("parallel"),
    )(*arrays)


def _adamw_math(g, w, m, v):
    m_new = ADAM_B1 * m + (1.0 - ADAM_B1) * g
    v_new = ADAM_B2 * v + (1.0 - ADAM_B2) * (g * g)
    m_hat = m_new / (1.0 - ADAM_B1 ** ADAM_STEP)
    v_hat = v_new / (1.0 - ADAM_B2 ** ADAM_STEP)
    return -ADAM_LR * (m_hat / (jnp.sqrt(v_hat) + ADAM_EPS) + ADAM_WD * w), m_new, v_new


def _adamw(gs, ws, ms, vs, reduce, name):
    n = len(ws)

    def body(*refs):
        g_refs, w_refs, m_refs, v_refs = (refs[j * n:(j + 1) * n] for j in range(4))
        outs = refs[4 * n:]
        for i in range(n):
            g = _sum_devices(g_refs[i]) if reduce else g_refs[i][...]
            res = _adamw_math(g, w_refs[i][...], m_refs[i][...], v_refs[i][...])
            for j, val in enumerate(((g,) if reduce else ()) + res):
                outs[j * n + i][...] = val

    return pl.pallas_call(
        body, name=name, out_shape=[_sds(w.shape) for w in ws] * (4 if reduce else 3),
        compiler_params=pltpu.CompilerParams(vmem_limit_bytes=VMEM_LIMIT),
    )(*gs, *ws, *ms, *vs)


def _sum_blocks(recv):
    def body(recv_ref, out_ref):
        out_ref[...] = _sum_devices(recv_ref)

    return pl.pallas_call(body, name="sum_small_grads", out_shape=_sds(recv.shape[1:]))(recv)


_WEIGHTS = [
    ('norm_g', (1, 1024), False), ('w_in', (1, 1024, 400), True), ('s5_lam_re', (1, 32, 64), False),
    ('s5_lam_im', (1, 32, 64), False), ('s5_log_dt', (1, 32), False), ('s5_b_re', (1, 32, 64, 16), False),
    ('s5_b_im', (1, 32, 64, 16), False), ('s5_c_re', (1, 32, 16, 64), False), ('s5_c_im', (1, 32, 16, 64), False),
    ('s5_d', (1, 512), False), ('s5_glu_w', (1, 64, 512), True), ('s5_glu_b', (1, 512), False),
    ('rwkv_mu', (1, 1664), False), ('rwkv_w0', (1, 512), False), ('rwkv_w2', (1, 64, 64), True),
    ('rwkv_a0', (1, 512), False), ('rwkv_a2', (1, 64, 64), True), ('rwkv_k_k', (1, 512), False),
    ('rwkv_k_a', (1, 512), False), ('rwkv_r_k', (1, 8, 64), False), ('rwkv_ln_w', (1, 512), False),
    ('rwkv_ln_b', (1, 512), False), ('w_out', (1, 128, 1024), True), ('final_g', (1024,), False),
]
_SHARDED = [(n, s) for n, s, sharded in _WEIGHTS if sharded]
_SMALL = [(n, s) for n, s, sharded in _WEIGHTS if not sharded]
_COLUMN_SHARDED = ('w_in', 'rwkv_w2', 'rwkv_a2')
_SMALL_ROWS = -(-sum(math.prod(s) for _, s in _SMALL) // (8 * LANES)) * 8


def _pack_small(grads):
    flat = [grads[n].reshape(-1) for n, _ in _SMALL]
    pad = _SMALL_ROWS * LANES - sum(f.size for f in flat)
    return jnp.concatenate(flat + [jnp.zeros((pad,), F32)]).reshape(_SMALL_ROWS, LANES)


def _unpack_small(packed):
    flat = packed.reshape(-1)
    out, off = {}, 0
    for n, s in _SMALL:
        size = math.prod(s)
        out[n] = flat[off:off + size].reshape(s)
        off += size
    return out


_BF16_OPERANDS = ('w_in', 's5_glu_w', 'w_out')


def _join_shards(name, blocks):
    _, _, rows, cols = blocks.shape
    if name in _COLUMN_SHARDED:
        return blocks.transpose(2, 1, 0, 3).reshape(rows, N_DEV * cols)
    return blocks.transpose(1, 0, 2, 3).reshape(N_DEV * rows, cols)


def _split_shards(name, full, shard_shape):
    rows, cols = shard_shape
    if name in _COLUMN_SHARDED:
        return full.reshape(rows, 4, 2, cols).transpose(2, 1, 0, 3)
    return full.reshape(4, 2, rows, cols).transpose(1, 0, 2, 3)


def kernel(x, norm_g, w_in, s5_lam_re, s5_lam_im, s5_log_dt, s5_b_re, s5_b_im, s5_c_re, s5_c_im, s5_d, s5_glu_w, s5_glu_b, rwkv_mu, rwkv_w0, rwkv_w2, rwkv_a0, rwkv_a2, rwkv_k_k, rwkv_k_a, rwkv_r_k, rwkv_ln_w, rwkv_ln_b, w_out, final_g, loss_target, m_norm_g, m_w_in, m_s5_lam_re, m_s5_lam_im, m_s5_log_dt, m_s5_b_re, m_s5_b_im, m_s5_c_re, m_s5_c_im, m_s5_d, m_s5_glu_w, m_s5_glu_b, m_rwkv_mu, m_rwkv_w0, m_rwkv_w2, m_rwkv_a0, m_rwkv_a2, m_rwkv_k_k, m_rwkv_k_a, m_rwkv_r_k, m_rwkv_ln_w, m_rwkv_ln_b, m_w_out, m_final_g, v_norm_g, v_w_in, v_s5_lam_re, v_s5_lam_im, v_s5_log_dt, v_s5_b_re, v_s5_b_im, v_s5_c_re, v_s5_c_im, v_s5_d, v_s5_glu_w, v_s5_glu_b, v_rwkv_mu, v_rwkv_w0, v_rwkv_w2, v_rwkv_a0, v_rwkv_a2, v_rwkv_k_k, v_rwkv_k_a, v_rwkv_r_k, v_rwkv_ln_w, v_rwkv_ln_b, v_w_out, v_final_g):
    given = dict(locals())

    n_sh = len(_SHARDED)
    shards = [given[n][0].astype(BF16 if n in _BF16_OPERANDS else F32) for n, _ in _SHARDED]
    gathered = _exchange(shards, (True,) * n_sh, ("x", "y"), "gather_weights_chips")
    gathered = _exchange(gathered, (True,) * n_sh, ("c",), "gather_weights_cores")
    local = {n: _join_shards(n, blocks).astype(F32 if n != 'w_in' else BF16)
             for (n, _), blocks in zip(_SHARDED, gathered)}
    local.update({n: (given[n][0] if len(s) > 1 else given[n]) for n, s in _SMALL})

    loss, grad_x, grads = _local_step(x[0], loss_target[0], local)

    blocks = [_split_shards(n, grads[n], s[1:]) for n, s in _SHARDED]
    small = _pack_small(grads).reshape(4, _SMALL_ROWS // 4, LANES)
    recv = _exchange(blocks + [small], (False,) * n_sh + (True,), ("c",), "exchange_grads_cores")
    sums = list(_pair_sum(recv, (BF16,) * n_sh + (F32,)))
    sums[-1] = sums[-1].reshape(_SMALL_ROWS, LANES)
    recv = _exchange(sums, (False,) * n_sh + (True,), ("x", "y"), "exchange_grads_chips")

    result = {}
    for group, name in (([0], "adamw_w_in"), ([1, 2, 3, 4], "adamw_shards")):
        ns = [_SHARDED[i][0] for i in group]
        res = _adamw([recv[i] for i in group], [given[n][0] for n in ns], [given['m_' + n][0] for n in ns],
                     [given['v_' + n][0] for n in ns], True, name)
        for j, n in enumerate(ns):
            result[n] = [res[k * len(ns) + j][None] for k in range(4)]
    g_small = _unpack_small(_sum_blocks(recv[-1]))
    two_d = lambda t: t.reshape(1, -1) if t.ndim == 1 else t
    ns = [n for n, _ in _SMALL]
    res = _adamw([two_d(g_small[n]) for n in ns], [two_d(given[n]) for n in ns], [two_d(given['m_' + n]) for n in ns],
                 [two_d(given['v_' + n]) for n in ns], False, "adamw_small")
    for j, (n, s) in enumerate(_SMALL):
        result[n] = [g_small[n]] + [res[k * len(ns) + j].reshape(s) for k in range(3)]

    total = lax.psum(loss[0, 0], ("x", "y", "c"))
    outs = [total, grad_x[None]]
    for k in range(4):
        outs += [result[n][k] for n, _, _ in _WEIGHTS]
    return tuple(outs)
```

```python
import functools
import math

import jax
import jax.numpy as jnp
from jax import lax
from jax.experimental import pallas as pl
from jax.experimental.pallas import tpu as pltpu

F32 = jnp.float32
BF16 = jnp.bfloat16
HI = lax.Precision.HIGH

D_MODEL = 1024
D_S5 = 512
D_RWKV = 512
S5_GROUPS = 32
S5_GROUP = 16
S5_STATE = 64
N_STATE = S5_GROUPS * S5_STATE
N_HEADS = 8
HEAD = 64
D_SHIFT = 3 * D_RWKV + 128
D_IN = 2 * D_S5 + D_SHIFT + D_RWKV
NORM_EPS = 1e-6
GN_EPS = 64e-5
N_DEV = 8
LANES = 128
S5_BLOCKS = 4
S5_SEGMENTS = 32
RWKV_CHUNK = 64
VMEM_LIMIT = 56 * 1024 * 1024

ADAM_LR = 0.001
ADAM_B1 = 0.9
ADAM_B2 = 0.999
ADAM_EPS = 1e-08
ADAM_WD = 0.01
ADAM_STEP = 10


def _dot(a, b, dims, prec):
    return lax.dot_general(a, b, (dims, ((), ())), precision=prec, preferred_element_type=F32)


def _make_mm(cast, prec):
    @jax.custom_vjp
    def mm(a, b):
        return _dot(cast(a), cast(b), ((1,), (0,)), prec)

    def fwd(a, b):
        return mm(a, b), (a, b)

    def bwd(res, g):
        a, b = res
        return (_dot(cast(g), cast(b), ((1,), (1,)), prec), _dot(cast(a), cast(g), ((0,), (0,)), prec))

    mm.defvjp(fwd, bwd)
    return mm


mm_hi = _make_mm(lambda t: t, HI)
mm_bf = _make_mm(lambda t: t.astype(BF16), None)


def _sigmoid(x):
    return 1.0 / (1.0 + jnp.exp(-x))


def _silu(x):
    return x * _sigmoid(x)


def _softplus(x):
    return jnp.maximum(x, 0.0) + jnp.log(1.0 + jnp.exp(-jnp.abs(x)))


def _gelu(x):
    return 0.5 * x * (1.0 + jnp.tanh(math.sqrt(2.0 / math.pi) * (x + 0.044715 * x * x * x)))


def _rms(x, g):
    return x * lax.rsqrt(jnp.mean(x * x, axis=-1, keepdims=True) + NORM_EPS) * g


def _const_spec(shape):
    nd = len(shape)
    return pl.BlockSpec(shape, lambda *_: (0,) * nd, pipeline_mode=pl.Buffered(1))


def _acc_spec(shape):
    nd = len(shape)
    return pl.BlockSpec(shape, lambda *_: (0,) * nd)


def _params(sem):
    return pltpu.CompilerParams(dimension_semantics=(sem,), vmem_limit_bytes=VMEM_LIMIT)


_ANY = pl.BlockSpec(memory_space=pl.ANY)


def _sds(shape):
    return jax.ShapeDtypeStruct(shape, F32)


def _head_sum_matrix():
    i = jnp.arange(D_RWKV) // HEAD
    return (i[:, None] == i[None, :]).astype(F32)


def _s5_param_fn(lam_re, lam_im, logdt, b_re, b_im):
    dt = jnp.exp(logdt)
    mag = jnp.exp(lam_re * dt)
    ang = lam_im * dt
    lbr = mag * jnp.cos(ang)
    lbi = mag * jnp.sin(ang)
    nr = lbr - 1.0
    den = lam_re * lam_re + lam_im * lam_im
    cr = (nr * lam_re + lbi * lam_im) / den
    ci = (lbi * lam_re - nr * lam_im) / den
    return lbr, lbi, cr * b_re - ci * b_im, cr * b_im + ci * b_re


def _s5_param_fwd(lam_re, lam_im, logdt, b_re, b_im):
    def body(lr, li, ld, br, bi, o_lr, o_li, o_br, o_bi):
        outs = _s5_param_fn(lr[...], li[...], ld[...], br[...], bi[...])
        for o, v in zip((o_lr, o_li, o_br, o_bi), outs):
            o[...] = v

    return pl.pallas_call(
        body, name="s5_param_fwd",
        out_shape=[_sds((1, N_STATE))] * 2 + [_sds((S5_GROUP, N_STATE))] * 2,
    )(lam_re, lam_im, logdt, b_re, b_im)


def _s5_param_bwd(lam_re, lam_im, logdt, b_re, b_im, d_lbr, d_lbi, d_bbr, d_bbi, group_ind):
    def body(lr, li, ld, br, bi, g0, g1, g2, g3, ind, o_lr, o_li, o_ld, o_br, o_bi):
        _, vjp = jax.vjp(_s5_param_fn, lr[...], li[...], ld[...], br[...], bi[...])
        d_lr, d_li, d_ld, d_br, d_bi = vjp((g0[...], g1[...], g2[...], g3[...]))
        o_lr[...] = d_lr
        o_li[...] = d_li
        o_ld[...] = _dot(jnp.broadcast_to(d_ld, (8, N_STATE)), ind[...], ((1,), (0,)), HI)
        o_br[...] = d_br
        o_bi[...] = d_bi

    return pl.pallas_call(
        body, name="s5_param_bwd",
        out_shape=[_sds((1, N_STATE))] * 2 + [_sds((8, LANES))] + [_sds((S5_GROUP, N_STATE))] * 2,
    )(lam_re, lam_im, logdt, b_re, b_im, d_lbr, d_lbi, d_bbr, d_bbi, group_ind)


def _fwd_in(x, norm_g, w_in_bf, b4_re, b4_im, tt):
    L = x.shape[0]

    def body(x_ref, g_ref, w_ref, bre_ref, bim_ref, u_ref, zs_ref, rw_ref, zr_ref, bur_ref, bui_ref):
        h = _rms(x_ref[...], g_ref[...])
        proj = jnp.dot(h.astype(BF16), w_ref[...], preferred_element_type=F32)
        u = proj[:, 0:D_S5]
        u_ref[...] = u
        zs_ref[...] = proj[:, D_S5:2 * D_S5]
        rw_ref[...] = proj[:, 2 * D_S5:2 * D_S5 + D_SHIFT]
        zr_ref[...] = proj[:, 2 * D_S5 + D_SHIFT:D_IN]
        for q in range(S5_BLOCKS):
            uq = u[:, q * LANES:(q + 1) * LANES]
            cols = slice(q * 512, (q + 1) * 512)
            bur_ref[:, cols] = _dot(uq, bre_ref[q], ((1,), (0,)), HI)
            bui_ref[:, cols] = _dot(uq, bim_ref[q], ((1,), (0,)), HI)

    row = lambda n: pl.BlockSpec((tt, n), lambda i: (i, 0))
    return pl.pallas_call(
        body, name="fwd_in", grid=(L // tt,),
        in_specs=[row(D_MODEL), _const_spec((1, D_MODEL)), _const_spec((D_MODEL, D_IN)),
                  _const_spec((S5_BLOCKS, LANES, 512)), _const_spec((S5_BLOCKS, LANES, 512))],
        out_specs=[row(D_S5), row(D_S5), row(D_SHIFT), row(D_RWKV), row(N_STATE), row(N_STATE)],
        out_shape=[_sds((L, D_S5)), _sds((L, D_S5)), _sds((L, D_SHIFT)), _sds((L, D_RWKV)),
                   _sds((L, N_STATE)), _sds((L, N_STATE))],
        compiler_params=_params("parallel"),
    )(x, norm_g, w_in_bf, b4_re, b4_im)


def _cmul(ar, ai, br, bi):
    return ar * br - ai * bi, ar * bi + ai * br


def _s5_scan(xr, xi, lam_r, lam_i, reverse):
    L = xr.shape[0]
    nseg = S5_SEGMENTS
    n_g = nseg // 8
    seg = L // nseg
    cb = LANES

    def body(xr_ref, xi_ref, lr_ref, li_ref, or_ref, oi_ref):
        lr = jnp.broadcast_to(lr_ref[...], (8, cb))
        li = jnp.broadcast_to(li_ref[...], (8, cb))

        def rows(i, g):
            return pl.ds(g * 8 * seg + ((seg - 1 - i) if reverse else i), 8, stride=seg)

        def local_scan(i, c):
            out = []
            for g in range(n_g):
                mr, mi = _cmul(lr, li, c[2 * g], c[2 * g + 1])
                sr = mr + xr_ref[rows(i, g), :]
                si = mi + xi_ref[rows(i, g), :]
                or_ref[rows(i, g), :] = sr
                oi_ref[rows(i, g), :] = si
                out += [sr, si]
            return tuple(out)

        zero = jnp.zeros((8, cb), F32)
        ends = lax.fori_loop(0, seg, local_scan, (zero,) * (2 * n_g))

        pr, pi_ = lr[0:1], li[0:1]
        n = 1
        while n < seg:
            pr, pi_ = _cmul(pr, pi_, pr, pi_)
            n *= 2
        rid = lax.broadcasted_iota(jnp.int32, (8, cb), 0)
        carry_r, carry_i = zero[0:1], zero[0:1]
        corr = [zero] * (2 * n_g)
        for j in (range(nseg - 1, -1, -1) if reverse else range(nseg)):
            g, row = divmod(j, 8)
            corr[2 * g] = jnp.where(rid == row, jnp.broadcast_to(carry_r, (8, cb)), corr[2 * g])
            corr[2 * g + 1] = jnp.where(rid == row, jnp.broadcast_to(carry_i, (8, cb)), corr[2 * g + 1])
            mr, mi = _cmul(pr, pi_, carry_r, carry_i)
            carry_r = ends[2 * g][row:row + 1] + mr
            carry_i = ends[2 * g + 1][row:row + 1] + mi

        def add_carry(i, c):
            out = []
            for g in range(n_g):
                cr, ci = _cmul(lr, li, c[2 * g], c[2 * g + 1])
                or_ref[rows(i, g), :] = or_ref[rows(i, g), :] + cr
                oi_ref[rows(i, g), :] = oi_ref[rows(i, g), :] + ci
                out += [cr, ci]
            return tuple(out)

        lax.fori_loop(0, seg, add_carry, tuple(corr))

    assert seg & (seg - 1) == 0
    blk = pl.BlockSpec((L, cb), lambda j: (0, j))
    lam = pl.BlockSpec((1, cb), lambda j: (0, j))
    return pl.pallas_call(
        body, name="s5_scan_bwd" if reverse else "s5_scan_fwd", grid=(N_STATE // cb,),
        in_specs=[blk, blk, lam, lam], out_specs=[blk, blk],
        out_shape=[_sds((L, N_STATE))] * 2,
        input_output_aliases={0: 0, 1: 1},
        compiler_params=_params("parallel"),
    )(xr, xi, lam_r, lam_i)


def _rwkv_pre_fn(r, k, v, wa, w0, w2p, a0, a2p, k_k, k_a, ee):
    w = -_softplus(-(w0 + mm_hi(jnp.tanh(wa), w2p))) - 0.5
    logw = -jnp.exp(w)
    a = _sigmoid(a0 + mm_hi(wa, a2p))
    kkp = k * k_k
    kk = kkp / jnp.maximum(jnp.sqrt(mm_hi(kkp * kkp, ee)), 1e-12)
    k2 = k * (1.0 + (a - 1.0) * k_a)
    return r, logw, k2, v, -kk, kk * a


def _head_spec(tt):
    return pl.BlockSpec((N_HEADS, tt, HEAD), lambda i: (0, i, 0))


def _load_heads(ref):
    return jnp.concatenate([ref[h] for h in range(N_HEADS)], axis=-1)


def _store_heads(ref, val):
    for h in range(N_HEADS):
        ref[h] = val[:, h * HEAD:(h + 1) * HEAD]


def _shifted(rw, prev_blk, first):
    rolled = pltpu.roll(rw, 1, axis=0)
    prev_row = jnp.where(first, 0.0, prev_blk[7:8, :])
    rid = lax.broadcasted_iota(jnp.int32, rw.shape, 0)
    return jnp.where(rid == 0, jnp.broadcast_to(prev_row, rw.shape), rolled)


def _split_rw(t):
    return t[:, 0:512], t[:, 512:1024], t[:, 1024:1536], t[:, 1536:1664]


def _rwkv_pre_specs(tt):
    row = pl.BlockSpec((tt, D_SHIFT), lambda i: (i, 0))
    prev = pl.BlockSpec((8, D_SHIFT), lambda i: (jnp.maximum(i * (tt // 8) - 1, 0), 0))
    consts = [_const_spec((1, D_SHIFT)), _const_spec((1, D_RWKV)), _const_spec((LANES, D_RWKV)),
              _const_spec((1, D_RWKV)), _const_spec((LANES, D_RWKV)), _const_spec((1, D_RWKV)),
              _const_spec((1, D_RWKV)), _const_spec((D_RWKV, D_RWKV))]
    return [row, prev] + consts


def _rwkv_pre_fwd(rw, mu, w0, w2p, a0, a2p, k_k, k_a, ee, tt):
    L = rw.shape[0]

    def body(rw_ref, prev_ref, mu_ref, w0_ref, w2_ref, a0_ref, a2_ref, kk_ref, ka_ref, ee_ref, *outs):
        rwv = rw_ref[...]
        rws = rwv + (_shifted(rwv, prev_ref[...], pl.program_id(0) == 0) - rwv) * mu_ref[...]
        res = _rwkv_pre_fn(*_split_rw(rws), w0_ref[...], w2_ref[...], a0_ref[...], a2_ref[...],
                           kk_ref[...], ka_ref[...], ee_ref[...])
        for o, val in zip(outs, res):
            _store_heads(o, val)

    return pl.pallas_call(
        body, name="rwkv_pre_fwd", grid=(L // tt,),
        in_specs=_rwkv_pre_specs(tt), out_specs=[_head_spec(tt)] * 6, out_shape=[_sds((N_HEADS, L, HEAD))] * 6,
        compiler_params=_params("parallel"),
    )(rw, rw, mu, w0, w2p, a0, a2p, k_k, k_a, ee)


def _rwkv_pre_bwd(rw, mu, w0, w2p, a0, a2p, k_k, k_a, ee, cots, tt):
    L = rw.shape[0]
    n_t = L // tt

    def body(rw_ref, prev_ref, mu_ref, w0_ref, w2_ref, a0_ref, a2_ref, kk_ref, ka_ref, ee_ref,
             c_r, c_w, c_k, c_v, c_a, c_b, cb_r, cb_k, cb_v,
             drws_ref, dmu_o, dw0_o, dw2_o, da0_o, da2_o, dkk_o, dka_o,
             dmu, dw0, dw2, da0, da2, dkk, dka):
        i = pl.program_id(0)
        accs = (dmu, dw0, dw2, da0, da2, dkk, dka)

        @pl.when(i == 0)
        def _():
            for acc in accs:
                acc[...] = jnp.zeros_like(acc)

        rwv = rw_ref[...]
        diff = _shifted(rwv, prev_ref[...], i == 0) - rwv
        rws = rwv + diff * mu_ref[...]
        consts = (w0_ref[...], w2_ref[...], a0_ref[...], a2_ref[...], kk_ref[...], ka_ref[...])
        _, vjp = jax.vjp(lambda *a: _rwkv_pre_fn(*a, ee_ref[...]), *_split_rw(rws), *consts)
        scan = [_load_heads(c) for c in (c_r, c_w, c_k, c_v, c_a, c_b)]
        g = vjp((scan[0] + cb_r[...], scan[1], scan[2] + cb_k[...], scan[3] + cb_v[...], scan[4], scan[5]))
        drws = jnp.concatenate(g[0:4], axis=-1)
        drws_ref[...] = drws
        dmu[...] += jnp.sum(drws * diff, axis=0, keepdims=True)
        for acc, val in zip(accs[1:], g[4:]):
            acc[...] += val

        @pl.when(i == n_t - 1)
        def _():
            for acc, out in zip(accs, (dmu_o, dw0_o, dw2_o, da0_o, da2_o, dkk_o, dka_o)):
                out[...] = acc[...]

    row = pl.BlockSpec((tt, D_RWKV), lambda i: (i, 0))
    shapes = [(1, D_SHIFT), (1, D_RWKV), (LANES, D_RWKV), (1, D_RWKV), (LANES, D_RWKV), (1, D_RWKV), (1, D_RWKV)]
    return pl.pallas_call(
        body, name="rwkv_pre_bwd", grid=(n_t,),
        in_specs=_rwkv_pre_specs(tt) + [_head_spec(tt)] * 6 + [row] * 3,
        out_specs=[pl.BlockSpec((tt, D_SHIFT), lambda i: (i, 0))] + [_acc_spec(s) for s in shapes],
        out_shape=[_sds((L, D_SHIFT))] + [_sds(s) for s in shapes],
        scratch_shapes=[pltpu.VMEM(s, F32) for s in shapes],
        compiler_params=_params("arbitrary"),
    )(rw, rw, mu, w0, w2p, a0, a2p, k_k, k_a, ee, *cots)


def _bmm(a, b):
    return lax.dot_general(a, b, (((2,), (1,)), ((0,), (0,))), precision=HI, preferred_element_type=F32)


def _bmm_nt(a, b):
    return lax.dot_general(a, b, (((2,), (2,)), ((0,), (0,))), precision=HI, preferred_element_type=F32)


def _bmm_tn(a, b):
    return lax.dot_general(a, b, (((1,), (1,)), ((0,), (0,))), precision=HI, preferred_element_type=F32)


def _unit_lower_inverse(a):
    t = a.shape[-1]
    ti = lax.broadcasted_iota(jnp.int32, (t, t), 0)
    si = lax.broadcasted_iota(jnp.int32, (t, t), 1)
    inv = jnp.where(ti == si, 1.0, 0.0)[None] + a
    pw = a
    n = 1
    while 2 * n < t:
        pw = _bmm(pw, pw)
        inv = inv + _bmm(inv, pw)
        n *= 2
    return inv


@jax.custom_vjp
def _solve_unit_lower(a, rhs, inv):
    return _bmm(inv, rhs)


def _solve_fwd(a, rhs, inv):
    u = _bmm(inv, rhs)
    return u, (inv, u)


def _solve_bwd(res, du):
    inv, u = res
    d_rhs = _bmm_tn(inv, du)
    return _bmm_nt(d_rhs, u), d_rhs, jnp.zeros_like(inv)


_solve_unit_lower.defvjp(_solve_fwd, _solve_bwd)


def _rwkv_chunk(st0, r, logw, k, v, a, b, inv=None):
    n_h, t, _ = r.shape
    ti = lax.broadcasted_iota(jnp.int32, (t, t), 0)
    si = lax.broadcasted_iota(jnp.int32, (t, t), 1)
    incl = (ti >= si)[None]
    strict = (ti > si)[None]
    ones_tri = jnp.broadcast_to(jnp.where(ti >= si, 1.0, 0.0)[None], (n_h, t, t))
    log_p = _bmm(ones_tri, logw)
    p_in = jnp.exp(log_p)
    p_inv = jnp.exp(-log_p)
    at = a * jnp.exp(log_p - logw)
    rt = r * p_in
    bt = b * p_inv
    kt = k * p_inv
    a_ab = jnp.where(strict, _bmm_nt(at, bt), 0.0)
    a_ak = jnp.where(strict, _bmm_nt(at, kt), 0.0)
    if inv is None:
        inv = _unit_lower_inverse(a_ab)
    u = _solve_unit_lower(a_ab, _bmm(at, st0) + _bmm(a_ak, v), inv)
    y = (_bmm(rt, st0) + _bmm(jnp.where(incl, _bmm_nt(rt, bt), 0.0), u)
         + _bmm(jnp.where(incl, _bmm_nt(rt, kt), 0.0), v))
    p_end = jnp.swapaxes(p_in[:, t - 1:t, :], 1, 2)
    st1 = (st0 + _bmm_tn(bt, u) + _bmm_tn(kt, v)) * p_end
    return y, st1, inv


def _rwkv_scan_fwd(ops):
    n_h, L, n = ops[0].shape
    t = RWKV_CHUNK
    n_c = L // t

    def body(r_ref, w_ref, k_ref, v_ref, a_ref, b_ref, y_ref, st_ref, inv_ref, st):
        @pl.when(pl.program_id(0) == 0)
        def _():
            st[...] = jnp.zeros_like(st)

        st0 = st[...]
        st_ref[0] = st0
        y, st1, inv = _rwkv_chunk(st0, r_ref[...], w_ref[...], k_ref[...], v_ref[...], a_ref[...], b_ref[...])
        y_ref[...] = y
        inv_ref[0] = inv
        st[...] = st1

    blk = pl.BlockSpec((n_h, t, n), lambda c: (0, c, 0))
    return pl.pallas_call(
        body, name="rwkv_scan_fwd", grid=(n_c,), in_specs=[blk] * 6,
        out_specs=[blk, pl.BlockSpec((1, n_h, n, n), lambda c: (c, 0, 0, 0)),
                   pl.BlockSpec((1, n_h, t, t), lambda c: (c, 0, 0, 0))],
        out_shape=[_sds((n_h, L, n)), _sds((n_c, n_h, n, n)), _sds((n_c, n_h, t, t))],
        scratch_shapes=[pltpu.VMEM((n_h, n, n), F32)],
        compiler_params=_params("arbitrary"),
    )(*ops)


def _rwkv_scan_bwd(ops, states, invs, dy):
    n_h, L, n = ops[0].shape
    t = RWKV_CHUNK
    n_c = L // t

    def body(r_ref, w_ref, k_ref, v_ref, a_ref, b_ref, st_ref, inv_ref, dy_ref, dr, dw, dk, dv, da, db, dst):
        @pl.when(pl.program_id(0) == 0)
        def _():
            dst[...] = jnp.zeros_like(dst)

        inv = inv_ref[0]
        _, vjp = jax.vjp(lambda *a: _rwkv_chunk(*a, inv=inv)[:2], st_ref[0], r_ref[...], w_ref[...], k_ref[...],
                         v_ref[...], a_ref[...], b_ref[...])
        g = vjp((dy_ref[...], dst[...]))
        dst[...] = g[0]
        for out, val in zip((dr, dw, dk, dv, da, db), g[1:]):
            out[...] = val

    blk = pl.BlockSpec((n_h, t, n), lambda c: (0, n_c - 1 - c, 0))
    per_chunk = lambda m: pl.BlockSpec((1, n_h, m, m), lambda c: (n_c - 1 - c, 0, 0, 0))
    return pl.pallas_call(
        body, name="rwkv_scan_bwd", grid=(n_c,),
        in_specs=[blk] * 6 + [per_chunk(n), per_chunk(t), blk],
        out_specs=[blk] * 6, out_shape=[_sds((n_h, L, n))] * 6,
        scratch_shapes=[pltpu.VMEM((n_h, n, n), F32)],
        compiler_params=_params("arbitrary"),
    )(*ops, states, invs, dy)


def _post_fn(x, u, zs, zr, ysc, r, k2, v, s_re, s_im, c_re, c_im, d, glu_w, glu_b, ln_w, ln_b, r_k,
             wo_s5, wo_rwkv, gf, tgt, ee):
    y_ssm = jnp.concatenate(
        [mm_hi(s_re[q], c_re[q]) - mm_hi(s_im[q], c_im[q]) for q in range(S5_BLOCKS)], axis=-1)
    y3 = _gelu(y_ssm + d * u)
    y_s5 = y3 * _sigmoid(mm_bf(y3, glu_w) + glu_b) * _silu(zs)
    mean = mm_hi(ysc, ee) * (1.0 / HEAD)
    yc = ysc - mean
    var = mm_hi(yc * yc, ee) * (1.0 / HEAD)
    gn = yc * lax.rsqrt(var + GN_EPS) * ln_w + ln_b
    bonus = mm_hi(r * k2 * r_k, ee) * v
    y_rwkv = (gn + bonus) * _silu(zr)
    x2 = x + mm_bf(y_s5, wo_s5) + mm_bf(y_rwkv, wo_rwkv)
    err = _rms(x2, gf) - tgt
    return 0.5 * jnp.mean(err * err, axis=-1, keepdims=True)


def _post(x, u, zs, zr, ysc, r, k2, v, s_re, s_im, c4_re, c4_im, d, glu_w, glu_b, ln_w, ln_b, r_k,
          w_out, gf, tgt, ee, tt):
    L = x.shape[0]
    n_t = L // tt
    acc_shapes = [(S5_BLOCKS, 512, LANES), (S5_BLOCKS, 512, LANES), (1, D_S5), (D_S5, D_S5), (1, D_S5),
                  (1, D_RWKV), (1, D_RWKV), (1, D_RWKV), (D_MODEL, D_MODEL), (1, D_MODEL), (8, LANES)]

    def body(x_ref, u_ref, zs_ref, zr_ref, ysc_ref, r_ref, k2_ref, v_ref, sre_ref, sim_ref,
             cre_ref, cim_ref, d_ref, gw_ref, gb_ref, lw_ref, lb_ref, rk_ref, wo_ref, gf_ref, tgt_ref, ee_ref,
             dx_o, du_o, dzs_o, dzr_o, dysc_o, dr_o, dk2_o, dv_o, gre_o, gim_o,
             dcre_o, dcim_o, dd_o, dgw_o, dgb_o, dlw_o, dlb_o, drk_o, dwo_o, dgf_o, loss_o,
             dcre, dcim, dd, dgw, dgb, dlw, dlb, drk, dwo, dgf, loss):
        i = pl.program_id(0)
        accs = (dcre, dcim, dd, dgw, dgb, dlw, dlb, drk, dwo, dgf, loss)

        @pl.when(i == 0)
        def _():
            for acc in accs:
                acc[...] = jnp.zeros_like(acc)

        blocks = [slice(q * 512, (q + 1) * 512) for q in range(S5_BLOCKS)]
        args = (x_ref[...], u_ref[...], zs_ref[...], zr_ref[...],
                _load_heads(ysc_ref), _load_heads(r_ref), _load_heads(k2_ref), _load_heads(v_ref),
                [sre_ref[:, c] for c in blocks], [sim_ref[:, c] for c in blocks],
                [cre_ref[q] for q in range(S5_BLOCKS)], [cim_ref[q] for q in range(S5_BLOCKS)],
                d_ref[...], gw_ref[...], gb_ref[...], lw_ref[...], lb_ref[...], rk_ref[...],
                wo_ref[0:D_S5, :], wo_ref[D_S5:D_MODEL, :], gf_ref[...])
        rows, vjp = jax.vjp(lambda *a: _post_fn(*a, tgt_ref[...], ee_ref[...]), *args)
        g = vjp(jnp.ones_like(rows))
        for out, val in zip((dx_o, du_o, dzs_o, dzr_o), g[0:4]):
            out[...] = val
        _store_heads(dysc_o, g[4])
        for out, val in zip((dr_o, dk2_o, dv_o), g[5:8]):
            out[...] = val
        for q in range(S5_BLOCKS):
            gre_o[:, blocks[q]] = g[8][q]
            gim_o[:, blocks[q]] = g[9][q]
            dcre[q] += g[10][q]
            dcim[q] += g[11][q]
        for acc, val in zip((dd, dgw, dgb, dlw, dlb, drk), g[12:18]):
            acc[...] += val
        dwo[0:D_S5, :] += g[18]
        dwo[D_S5:D_MODEL, :] += g[19]
        dgf[...] += g[20]
        loss[...] += jnp.broadcast_to(jnp.sum(rows, axis=0, keepdims=True), loss.shape)

        @pl.when(i == n_t - 1)
        def _():
            for acc, out in zip(accs, (dcre_o, dcim_o, dd_o, dgw_o, dgb_o, dlw_o, dlb_o, drk_o, dwo_o, dgf_o, loss_o)):
                pltpu.sync_copy(acc, out)

    row = lambda n: pl.BlockSpec((tt, n), lambda i: (i, 0))
    in_specs = ([row(D_MODEL)] + [row(512)] * 3 + [_head_spec(tt)] * 4 + [row(N_STATE)] * 2
                + [_const_spec((S5_BLOCKS, 512, LANES))] * 2
                + [_const_spec(s) for s in [(1, D_S5), (D_S5, D_S5), (1, D_S5), (1, D_RWKV), (1, D_RWKV), (1, D_RWKV),
                                            (D_MODEL, D_MODEL), (1, D_MODEL)]]
                + [row(D_MODEL), _const_spec((D_RWKV, D_RWKV))])
    out_rows = [D_MODEL] + [512] * 3 + [None] + [512] * 3 + [N_STATE] * 2
    return pl.pallas_call(
        body, name="post_fwd_bwd", grid=(n_t,), in_specs=in_specs,
        out_specs=[row(n) if n else _head_spec(tt) for n in out_rows] + [_ANY] * len(acc_shapes),
        out_shape=([_sds((L, n)) if n else _sds((N_HEADS, L, HEAD)) for n in out_rows]
                   + [_sds(s) for s in acc_shapes]),
        scratch_shapes=[pltpu.VMEM(s, F32) for s in acc_shapes],
        compiler_params=_params("arbitrary"),
    )(x, u, zs, zr, ysc, r, k2, v, s_re, s_im, c4_re, c4_im, d, glu_w, glu_b, ln_w, ln_b, r_k, w_out, gf, tgt, ee)


def _s5_bwd(u, du_direct, s_re, s_im, g_re, g_im, b4_re, b4_im, tt):
    L = u.shape[0]
    n_t = L // tt
    acc_shapes = [(S5_BLOCKS, LANES, 512), (S5_BLOCKS, LANES, 512), (1, N_STATE), (1, N_STATE)]

    def body(u_ref, dud_ref, sre_ref, sim_ref, gre_ref, gim_ref, bre_ref, bim_ref,
             du_o, dbre_o, dbim_o, dlr_o, dli_o, dbre, dbim, dlr, dli, last_r, last_i):
        i = pl.program_id(0)

        @pl.when(i == 0)
        def _():
            for acc in (dbre, dbim, dlr, dli, last_r, last_i):
                acc[...] = jnp.zeros_like(acc)

        uv = u_ref[...]
        gr = gre_ref[...]
        gi = gim_ref[...]
        pieces = []
        for q in range(S5_BLOCKS):
            cols = slice(q * 512, (q + 1) * 512)
            uq = uv[:, q * LANES:(q + 1) * LANES]
            pieces.append(_dot(gr[:, cols], bre_ref[q], ((1,), (1,)), HI) + _dot(gi[:, cols], bim_ref[q], ((1,), (1,)), HI))
            dbre[q] += _dot(uq, gr[:, cols], ((0,), (0,)), HI)
            dbim[q] += _dot(uq, gi[:, cols], ((0,), (0,)), HI)
        du_o[...] = dud_ref[...] + jnp.concatenate(pieces, axis=-1)
        sr = sre_ref[...]
        si = sim_ref[...]
        rid = lax.broadcasted_iota(jnp.int32, sr.shape, 0)
        pr = jnp.where(rid == 0, jnp.broadcast_to(last_r[7:8, :], sr.shape), pltpu.roll(sr, 1, axis=0))
        pi_ = jnp.where(rid == 0, jnp.broadcast_to(last_i[7:8, :], si.shape), pltpu.roll(si, 1, axis=0))
        dlr[...] += jnp.sum(pr * gr + pi_ * gi, axis=0, keepdims=True)
        dli[...] += jnp.sum(pr * gi - pi_ * gr, axis=0, keepdims=True)
        last_r[...] = sr[tt - 8:tt, :]
        last_i[...] = si[tt - 8:tt, :]

        @pl.when(i == n_t - 1)
        def _():
            for acc, out in zip((dbre, dbim, dlr, dli), (dbre_o, dbim_o, dlr_o, dli_o)):
                out[...] = acc[...]

    row = lambda n: pl.BlockSpec((tt, n), lambda i: (i, 0))
    return pl.pallas_call(
        body, name="s5_bwd", grid=(n_t,),
        in_specs=[row(D_S5), row(D_S5)] + [row(N_STATE)] * 4 + [_const_spec((S5_BLOCKS, LANES, 512))] * 2,
        out_specs=[row(D_S5)] + [_acc_spec(s) for s in acc_shapes],
        out_shape=[_sds((L, D_S5))] + [_sds(s) for s in acc_shapes],
        scratch_shapes=[pltpu.VMEM(s, F32) for s in acc_shapes] + [pltpu.VMEM((8, N_STATE), F32)] * 2,
        compiler_params=_params("arbitrary"),
    )(u, du_direct, s_re, s_im, g_re, g_im, b4_re, b4_im)


def _bwd_in(x, norm_g, w_in_bf, mu, dx2, du, dzs, drws, dzr, tt):
    L = x.shape[0]
    n_t = L // tt

    def body(x_ref, g_ref, w_ref, mu_ref, dx2_ref, du_ref, dzs_ref, drws_ref, nxt_ref, dzr_ref,
             gx_o, dproj_o, dg_o, dg):
        i = pl.program_id(0)

        @pl.when(i == 0)
        def _():
            dg[...] = jnp.zeros_like(dg)

        drws_v = drws_ref[...]
        rid = lax.broadcasted_iota(jnp.int32, drws_v.shape, 0)
        nxt_row = jnp.where(i == n_t - 1, 0.0, nxt_ref[0:1, :])
        nxt = jnp.where(rid == tt - 1, jnp.broadcast_to(nxt_row, drws_v.shape), pltpu.roll(drws_v, tt - 1, axis=0))
        muv = mu_ref[...]
        drw = drws_v * (1.0 - muv) + nxt * muv
        dproj_o[:, 0:D_S5] = du_ref[...]
        dproj_o[:, D_S5:2 * D_S5] = dzs_ref[...]
        dproj_o[:, 2 * D_S5:2 * D_S5 + D_SHIFT] = drw
        dproj_o[:, 2 * D_S5 + D_SHIFT:D_IN] = dzr_ref[...]
        dh = _dot(dproj_o[...].astype(BF16), w_ref[...], ((1,), (1,)), None)
        _, vjp = jax.vjp(_rms, x_ref[...], g_ref[...])
        dxh, dgv = vjp(dh)
        gx_o[...] = dx2_ref[...] + dxh
        dg[...] += dgv

        @pl.when(i == n_t - 1)
        def _():
            dg_o[...] = dg[...]

    row = lambda n: pl.BlockSpec((tt, n), lambda i: (i, 0))
    nxt = pl.BlockSpec((8, D_SHIFT), lambda i: (jnp.minimum((i + 1) * (tt // 8), L // 8 - 1), 0))
    return pl.pallas_call(
        body, name="bwd_in", grid=(n_t,),
        in_specs=[row(D_MODEL), _const_spec((1, D_MODEL)), _const_spec((D_MODEL, D_IN)), _const_spec((1, D_SHIFT)),
                  row(D_MODEL), row(D_S5), row(D_S5), row(D_SHIFT), nxt, row(D_RWKV)],
        out_specs=[row(D_MODEL), row(D_IN), _acc_spec((1, D_MODEL))],
        out_shape=[_sds((L, D_MODEL)), _sds((L, D_IN)), _sds((1, D_MODEL))],
        scratch_shapes=[pltpu.VMEM((1, D_MODEL), F32)],
        compiler_params=_params("arbitrary"),
    )(x, norm_g, w_in_bf, mu, dx2, du, dzs, drws, drws, dzr)


def _grad_w_in(x, norm_g, dproj, tt, cn):
    L = x.shape[0]

    def body(x_ref, g_ref, dp_ref, out_ref):
        @pl.when(pl.program_id(1) == 0)
        def _():
            out_ref[...] = jnp.zeros_like(out_ref)

        h = _rms(x_ref[...], g_ref[...])
        out_ref[...] += _dot(h.astype(BF16), dp_ref[...].astype(BF16), ((0,), (0,)), None)

    return pl.pallas_call(
        body, name="grad_w_in", grid=(D_IN // cn, L // tt),
        in_specs=[pl.BlockSpec((tt, D_MODEL), lambda j, i: (i, 0)), pl.BlockSpec((1, D_MODEL), lambda j, i: (0, 0)),
                  pl.BlockSpec((tt, cn), lambda j, i: (i, j))],
        out_specs=pl.BlockSpec((D_MODEL, cn), lambda j, i: (0, j)),
        out_shape=_sds((D_MODEL, D_IN)),
        compiler_params=pltpu.CompilerParams(dimension_semantics=("parallel", "arbitrary"), vmem_limit_bytes=VMEM_LIMIT),
    )(x, norm_g, dproj)


def _block_diag_b(bbar):
    bb = bbar.reshape(S5_GROUP, S5_BLOCKS, 8, S5_STATE)
    return jnp.einsum('hqgp,Gg->qGhgp', bb, jnp.eye(8, dtype=F32)).reshape(S5_BLOCKS, LANES, 512)


def _block_diag_b_t(db4):
    d = db4.reshape(S5_BLOCKS, 8, S5_GROUP, 8, S5_STATE)
    return jnp.einsum('qGhgp,Gg->hqgp', d, jnp.eye(8, dtype=F32)).reshape(S5_GROUP, N_STATE)


def _block_diag_c(c):
    cc = c.reshape(S5_BLOCKS, 8, S5_GROUP, S5_STATE)
    return jnp.einsum('qghp,gG->qgpGh', cc, jnp.eye(8, dtype=F32)).reshape(S5_BLOCKS, 512, LANES)


def _block_diag_c_t(dc4):
    d = dc4.reshape(S5_BLOCKS, 8, S5_STATE, 8, S5_GROUP)
    return jnp.einsum('qgpGh,gG->qghp', d, jnp.eye(8, dtype=F32)).reshape(S5_GROUPS, S5_GROUP, S5_STATE)


def _local_step(x, tgt, w):
    L = x.shape[0]
    tt = min(256, L)
    tp = min(128, L)
    ee = _head_sum_matrix()

    lam_re = w['s5_lam_re'].reshape(1, N_STATE)
    lam_im = w['s5_lam_im'].reshape(1, N_STATE)
    logdt = jnp.repeat(w['s5_log_dt'], S5_STATE).reshape(1, N_STATE)
    b_re_t = w['s5_b_re'].transpose(2, 0, 1).reshape(S5_GROUP, N_STATE)
    b_im_t = w['s5_b_im'].transpose(2, 0, 1).reshape(S5_GROUP, N_STATE)
    lbr, lbi, bbr, bbi = _s5_param_fwd(lam_re, lam_im, logdt, b_re_t, b_im_t)
    b4_re, b4_im = _block_diag_b(bbr), _block_diag_b(bbi)
    c4_re, c4_im = _block_diag_c(w['s5_c_re']), _block_diag_c(w['s5_c_im'])

    norm_g = w['norm_g'].reshape(1, D_MODEL)
    w_in_bf = w['w_in'].astype(BF16)
    u, zs, rw, zr, bu_re, bu_im = _fwd_in(x, norm_g, w_in_bf, b4_re, b4_im, tt)
    s_re, s_im = _s5_scan(bu_re, bu_im, lbr, lbi, reverse=False)

    row = lambda t: t.reshape(1, -1)
    zpad = jnp.zeros((HEAD, D_RWKV), F32)
    w2p = jnp.concatenate([w['rwkv_w2'], zpad], axis=0)
    a2p = jnp.concatenate([zpad, w['rwkv_a2']], axis=0)
    pre_consts = (row(w['rwkv_mu']), row(w['rwkv_w0']), w2p, row(w['rwkv_a0']), a2p,
                  row(w['rwkv_k_k']), row(w['rwkv_k_a']), ee)
    ops = _rwkv_pre_fwd(rw, *pre_consts, tt)
    ysc, states, invs = _rwkv_scan_fwd(ops)

    post = _post(x, u, zs, zr, ysc, ops[0], ops[2], ops[3], s_re, s_im, c4_re, c4_im,
                 row(w['s5_d']), w['s5_glu_w'], row(w['s5_glu_b']), row(w['rwkv_ln_w']), row(w['rwkv_ln_b']),
                 row(w['rwkv_r_k']), w['w_out'], row(w['final_g']), tgt, ee, tp)
    (dx2, du_d, dzs, dzr, dysc, dr_b, dk2_b, dv_b, g_re, g_im,
     dc4_re, dc4_im, dd, dglu_w, dglu_b, dln_w, dln_b, dr_k, dw_out, dgf, loss) = post

    gt_re, gt_im = _s5_scan(g_re, g_im, lbr, -lbi, reverse=True)
    du, db4_re, db4_im, dlbr, dlbi = _s5_bwd(u, du_d, s_re, s_im, gt_re, gt_im, b4_re, b4_im, tt)
    group_ind = (jnp.arange(N_STATE)[:, None] // S5_STATE == jnp.arange(LANES)[None, :]).astype(F32)
    dlam_re, dlam_im, dlogdt, db_re_t, db_im_t = _s5_param_bwd(
        lam_re, lam_im, logdt, b_re_t, b_im_t, dlbr, dlbi, _block_diag_b_t(db4_re), _block_diag_b_t(db4_im), group_ind)

    cots = list(_rwkv_scan_bwd(ops, states, invs, dysc)) + [dr_b, dk2_b, dv_b]
    drws, dmu, dw0, dw2p, da0, da2p, dk_k, dk_a = _rwkv_pre_bwd(rw, *pre_consts, cots, tt)

    grad_x, dproj, dnorm_g = _bwd_in(x, norm_g, w_in_bf, row(w['rwkv_mu']), dx2, du, dzs, drws, dzr, tt)
    dw_in = _grad_w_in(x, norm_g, dproj, tt, 640)

    unb = lambda t: t.reshape(S5_GROUP, S5_GROUPS, S5_STATE).transpose(1, 2, 0)
    grads = {
        'norm_g': dnorm_g.reshape(D_MODEL), 'w_in': dw_in,
        's5_lam_re': dlam_re.reshape(S5_GROUPS, S5_STATE), 's5_lam_im': dlam_im.reshape(S5_GROUPS, S5_STATE),
        's5_log_dt': dlogdt[0, :S5_GROUPS], 's5_b_re': unb(db_re_t), 's5_b_im': unb(db_im_t),
        's5_c_re': _block_diag_c_t(dc4_re), 's5_c_im': _block_diag_c_t(dc4_im),
        's5_d': dd.reshape(D_S5), 's5_glu_w': dglu_w, 's5_glu_b': dglu_b.reshape(D_S5),
        'rwkv_mu': dmu.reshape(-1), 'rwkv_w0': dw0.reshape(-1), 'rwkv_w2': dw2p[:HEAD], 'rwkv_a0': da0.reshape(-1),
        'rwkv_a2': da2p[HEAD:], 'rwkv_k_k': dk_k.reshape(-1), 'rwkv_k_a': dk_a.reshape(-1),
        'rwkv_r_k': dr_k.reshape(N_HEADS, HEAD), 'rwkv_ln_w': dln_w.reshape(-1), 'rwkv_ln_b': dln_b.reshape(-1),
        'w_out': dw_out, 'final_g': dgf.reshape(D_MODEL),
    }
    return loss, grad_x, grads


def _exchange(arrays, gather, axes, name):
    n = len(arrays)
    group = 2 ** len(axes)

    def body(*refs):
        send_refs, recv_refs = refs[:n], refs[n:2 * n]
        send_sems, recv_sems, local_sems = refs[2 * n:]
        pos = {ax: lax.axis_index(ax) for ax in ("x", "y", "c")}

        def index_of(p):
            idx = 0
            for ax in axes:
                idx = 2 * idx + p[ax]
            return idx

        me = index_of(pos)
        own, outs, arrivals = [], [], []
        for i, (send_ref, recv_ref) in enumerate(zip(send_refs, recv_refs)):
            def block_for(dev, send_ref=send_ref, whole=gather[i]):
                return send_ref if whole else send_ref.at[dev]

            own.append(pltpu.make_async_copy(block_for(me), recv_ref.at[me], local_sems.at[i]))
            own[-1].start()
            for k in range(1, group):
                peer = dict(pos)
                for bit, ax in enumerate(axes):
                    if (k >> bit) & 1:
                        peer[ax] = 1 - pos[ax]
                peer_idx = index_of(peer)
                sems = dict(send_sem=send_sems.at[i, k - 1], recv_sem=recv_sems.at[i, k - 1],
                            device_id=(peer["x"], peer["y"], peer["c"]), device_id_type=pl.DeviceIdType.MESH)
                outs.append(pltpu.make_async_remote_copy(src_ref=block_for(peer_idx), dst_ref=recv_ref.at[me], **sems))
                outs[-1].start()
                arrivals.append(
                    pltpu.make_async_remote_copy(src_ref=block_for(peer_idx), dst_ref=recv_ref.at[peer_idx], **sems))
        for copy in arrivals:
            copy.wait_recv()
        for copy in outs:
            copy.wait_send()
        for copy in own:
            copy.wait()

    return pl.pallas_call(
        body, name=name, in_specs=[_ANY] * n, out_specs=[_ANY] * n,
        out_shape=[jax.ShapeDtypeStruct(((group,) + a.shape) if whole else a.shape, a.dtype)
                   for a, whole in zip(arrays, gather)],
        scratch_shapes=[pltpu.SemaphoreType.DMA((n, group - 1)), pltpu.SemaphoreType.DMA((n, group - 1)),
                        pltpu.SemaphoreType.DMA((n,))],
        compiler_params=pltpu.CompilerParams(has_side_effects=True),
    )(*arrays)


def _sum_devices(ref):
    g = ref[0].astype(F32)
    for s in range(1, ref.shape[0]):
        g = g + ref[s].astype(F32)
    return g


def _adamw_math(g, w, m, v):
    m_new = ADAM_B1 * m + (1.0 - ADAM_B1) * g
    v_new = ADAM_B2 * v + (1.0 - ADAM_B2) * (g * g)
    m_hat = m_new / (1.0 - ADAM_B1 ** ADAM_STEP)
    v_hat = v_new / (1.0 - ADAM_B2 ** ADAM_STEP)
    return -ADAM_LR * (m_hat / (jnp.sqrt(v_hat) + ADAM_EPS) + ADAM_WD * w), m_new, v_new


def _adamw(gs, ws, ms, vs, reduce, name):
    n = len(ws)

    def body(*refs):
        g_refs, w_refs, m_refs, v_refs = (refs[j * n:(j + 1) * n] for j in range(4))
        outs = refs[4 * n:]
        for i in range(n):
            g = _sum_devices(g_refs[i]) if reduce else g_refs[i][...]
            res = _adamw_math(g, w_refs[i][...], m_refs[i][...], v_refs[i][...])
            for j, val in enumerate(((g,) if reduce else ()) + res):
                outs[j * n + i][...] = val

    return pl.pallas_call(
        body, name=name, out_shape=[_sds(w.shape) for w in ws] * (4 if reduce else 3),
        compiler_params=pltpu.CompilerParams(vmem_limit_bytes=VMEM_LIMIT),
    )(*gs, *ws, *ms, *vs)


def _sum_blocks(recv):
    def body(recv_ref, out_ref):
        out_ref[...] = _sum_devices(recv_ref)

    return pl.pallas_call(body, name="sum_small_grads", out_shape=_sds(recv.shape[1:]))(recv)


_WEIGHTS = [
    ('norm_g', (1, 1024), False), ('w_in', (1, 1024, 400), True), ('s5_lam_re', (1, 32, 64), False),
    ('s5_lam_im', (1, 32, 64), False), ('s5_log_dt', (1, 32), False), ('s5_b_re', (1, 32, 64, 16), False),
    ('s5_b_im', (1, 32, 64, 16), False), ('s5_c_re', (1, 32, 16, 64), False), ('s5_c_im', (1, 32, 16, 64), False),
    ('s5_d', (1, 512), False), ('s5_glu_w', (1, 64, 512), True), ('s5_glu_b', (1, 512), False),
    ('rwkv_mu', (1, 1664), False), ('rwkv_w0', (1, 512), False), ('rwkv_w2', (1, 64, 64), True),
    ('rwkv_a0', (1, 512), False), ('rwkv_a2', (1, 64, 64), True), ('rwkv_k_k', (1, 512), False),
    ('rwkv_k_a', (1, 512), False), ('rwkv_r_k', (1, 8, 64), False), ('rwkv_ln_w', (1, 512), False),
    ('rwkv_ln_b', (1, 512), False), ('w_out', (1, 128, 1024), True), ('final_g', (1024,), False),
]
_SHARDED = [(n, s) for n, s, sharded in _WEIGHTS if sharded]
_SMALL = [(n, s) for n, s, sharded in _WEIGHTS if not sharded]
_COLUMN_SHARDED = ('w_in', 'rwkv_w2', 'rwkv_a2')
_SMALL_ROWS = -(-sum(math.prod(s) for _, s in _SMALL) // (8 * LANES)) * 8


def _pack_small(grads):
    flat = [grads[n].reshape(-1) for n, _ in _SMALL]
    pad = _SMALL_ROWS * LANES - sum(f.size for f in flat)
    return jnp.concatenate(flat + [jnp.zeros((pad,), F32)]).reshape(_SMALL_ROWS, LANES)


def _unpack_small(packed):
    flat = packed.reshape(-1)
    out, off = {}, 0
    for n, s in _SMALL:
        size = math.prod(s)
        out[n] = flat[off:off + size].reshape(s)
        off += size
    return out


_BF16_OPERANDS = ('w_in', 's5_glu_w', 'w_out')


def _join_shards(name, blocks):
    _, rows, cols = blocks.shape
    if name in _COLUMN_SHARDED:
        return blocks.transpose(1, 0, 2).reshape(rows, N_DEV * cols)
    return blocks.reshape(N_DEV * rows, cols)


def _split_shards(name, full, shard_shape):
    rows, cols = shard_shape
    if name in _COLUMN_SHARDED:
        return full.reshape(rows, N_DEV, cols).transpose(1, 0, 2)
    return full.reshape(N_DEV, rows, cols)


def kernel(x, norm_g, w_in, s5_lam_re, s5_lam_im, s5_log_dt, s5_b_re, s5_b_im, s5_c_re, s5_c_im, s5_d, s5_glu_w, s5_glu_b, rwkv_mu, rwkv_w0, rwkv_w2, rwkv_a0, rwkv_a2, rwkv_k_k, rwkv_k_a, rwkv_r_k, rwkv_ln_w, rwkv_ln_b, w_out, final_g, loss_target, m_norm_g, m_w_in, m_s5_lam_re, m_s5_lam_im, m_s5_log_dt, m_s5_b_re, m_s5_b_im, m_s5_c_re, m_s5_c_im, m_s5_d, m_s5_glu_w, m_s5_glu_b, m_rwkv_mu, m_rwkv_w0, m_rwkv_w2, m_rwkv_a0, m_rwkv_a2, m_rwkv_k_k, m_rwkv_k_a, m_rwkv_r_k, m_rwkv_ln_w, m_rwkv_ln_b, m_w_out, m_final_g, v_norm_g, v_w_in, v_s5_lam_re, v_s5_lam_im, v_s5_log_dt, v_s5_b_re, v_s5_b_im, v_s5_c_re, v_s5_c_im, v_s5_d, v_s5_glu_w, v_s5_glu_b, v_rwkv_mu, v_rwkv_w0, v_rwkv_w2, v_rwkv_a0, v_rwkv_a2, v_rwkv_k_k, v_rwkv_k_a, v_rwkv_r_k, v_rwkv_ln_w, v_rwkv_ln_b, v_w_out, v_final_g):
    given = dict(locals())

    n_sh = len(_SHARDED)
    everyone = ("x", "y", "c")
    shards = [given[n][0].astype(BF16 if n in _BF16_OPERANDS else F32) for n, _ in _SHARDED]
    gathered = _exchange(shards, (True,) * n_sh, everyone, "gather_weights")
    local = {n: _join_shards(n, blocks).astype(F32 if n != 'w_in' else BF16)
             for (n, _), blocks in zip(_SHARDED, gathered)}
    local.update({n: (given[n][0] if len(s) > 1 else given[n]) for n, s in _SMALL})

    loss, grad_x, grads = _local_step(x[0], loss_target[0], local)

    blocks = [_split_shards(n, grads[n], s[1:]).astype(BF16) for n, s in _SHARDED]
    recv = _exchange(blocks + [_pack_small(grads)], (False,) * n_sh + (True,), everyone, "exchange_grads")

    result = {}
    for group, name in (([0], "adamw_w_in"), ([1, 2, 3, 4], "adamw_shards")):
        ns = [_SHARDED[i][0] for i in group]
        res = _adamw([recv[i] for i in group], [given[n][0] for n in ns], [given['m_' + n][0] for n in ns],
                     [given['v_' + n][0] for n in ns], True, name)
        for j, n in enumerate(ns):
            result[n] = [res[k * len(ns) + j][None] for k in range(4)]
    g_small = _unpack_small(_sum_blocks(recv[-1]))
    two_d = lambda t: t.reshape(1, -1) if t.ndim == 1 else t
    ns = [n for n, _ in _SMALL]
    res = _adamw([two_d(g_small[n]) for n in ns], [two_d(given[n]) for n in ns], [two_d(given['m_' + n]) for n in ns],
                 [two_d(given['v_' + n]) for n in ns], False, "adamw_small")
    for j, (n, s) in enumerate(_SMALL):
        result[n] = [g_small[n]] + [res[k * len(ns) + j].reshape(s) for k in range(3)]

    total = lax.psum(loss[0, 0], ("x", "y", "c"))
    outs = [total, grad_x[None]]
    for k in range(4):
        outs += [result[n][k] for n, _, _ in _WEIGHTS]
    return tuple(outs)
```

```python
import functools
import math

import jax
import jax.numpy as jnp
from jax import lax
from jax.experimental import pallas as pl
from jax.experimental.pallas import tpu as pltpu

F32 = jnp.float32
BF16 = jnp.bfloat16
HI = lax.Precision.HIGH

D_MODEL = 1024
D_S5 = 512
D_RWKV = 512
S5_GROUPS = 32
S5_GROUP = 16
S5_STATE = 64
N_STATE = S5_GROUPS * S5_STATE
N_HEADS = 8
HEAD = 64
D_SHIFT = 3 * D_RWKV + 128
D_IN = 2 * D_S5 + D_SHIFT + D_RWKV
NORM_EPS = 1e-6
GN_EPS = 64e-5
N_DEV = 8
LANES = 128
S5_BLOCKS = 4
RWKV_CHUNK = 64
VMEM_LIMIT = 56 * 1024 * 1024

ADAM_LR = 0.001
ADAM_B1 = 0.9
ADAM_B2 = 0.999
ADAM_EPS = 1e-08
ADAM_WD = 0.01
ADAM_STEP = 10


def _dot(a, b, dims, prec):
    return lax.dot_general(a, b, (dims, ((), ())), precision=prec, preferred_element_type=F32)


def _make_mm(cast, prec):
    @jax.custom_vjp
    def mm(a, b):
        return _dot(cast(a), cast(b), ((1,), (0,)), prec)

    def fwd(a, b):
        return mm(a, b), (a, b)

    def bwd(res, g):
        a, b = res
        return (_dot(cast(g), cast(b), ((1,), (1,)), prec), _dot(cast(a), cast(g), ((0,), (0,)), prec))

    mm.defvjp(fwd, bwd)
    return mm


mm_hi = _make_mm(lambda t: t, HI)
mm_bf = _make_mm(lambda t: t.astype(BF16), None)


def _sigmoid(x):
    return 1.0 / (1.0 + jnp.exp(-x))


def _silu(x):
    return x * _sigmoid(x)


def _softplus(x):
    return jnp.maximum(x, 0.0) + jnp.log(1.0 + jnp.exp(-jnp.abs(x)))


def _gelu(x):
    return 0.5 * x * (1.0 + jnp.tanh(math.sqrt(2.0 / math.pi) * (x + 0.044715 * x * x * x)))


def _rms(x, g):
    return x * lax.rsqrt(jnp.mean(x * x, axis=-1, keepdims=True) + NORM_EPS) * g


def _const_spec(shape):
    nd = len(shape)
    return pl.BlockSpec(shape, lambda *_: (0,) * nd, pipeline_mode=pl.Buffered(1))


def _acc_spec(shape):
    nd = len(shape)
    return pl.BlockSpec(shape, lambda *_: (0,) * nd)


def _params(sem):
    return pltpu.CompilerParams(dimension_semantics=(sem,), vmem_limit_bytes=VMEM_LIMIT)


_ANY = pl.BlockSpec(memory_space=pl.ANY)


def _sds(shape):
    return jax.ShapeDtypeStruct(shape, F32)


def _head_sum_matrix():
    i = jnp.arange(D_RWKV) // HEAD
    return (i[:, None] == i[None, :]).astype(F32)


def _s5_param_fn(lam_re, lam_im, logdt, b_re, b_im):
    dt = jnp.exp(logdt)
    mag = jnp.exp(lam_re * dt)
    ang = lam_im * dt
    lbr = mag * jnp.cos(ang)
    lbi = mag * jnp.sin(ang)
    nr = lbr - 1.0
    den = lam_re * lam_re + lam_im * lam_im
    cr = (nr * lam_re + lbi * lam_im) / den
    ci = (lbi * lam_re - nr * lam_im) / den
    return lbr, lbi, cr * b_re - ci * b_im, cr * b_im + ci * b_re


def _cmul(ar, ai, br, bi):
    return ar * br - ai * bi, ar * bi + ai * br


def _s5_param_fwd(lam_re, lam_im, logdt, b_re, b_im):
    def body(lr, li, ld, br, bi, o_br, o_bi, o_pr, o_pi):
        lbr, lbi, bbr, bbi = _s5_param_fn(lr[...], li[...], ld[...], br[...], bi[...])
        o_br[...] = bbr
        o_bi[...] = bbi
        rid = lax.broadcasted_iota(jnp.int32, (8, N_STATE), 0)
        pr, pi_ = lbr, lbi
        acc_r = jnp.broadcast_to(pr, (8, N_STATE))
        acc_i = jnp.broadcast_to(pi_, (8, N_STATE))
        for j in range(1, 8):
            pr, pi_ = _cmul(pr, pi_, lbr, lbi)
            acc_r = jnp.where(rid == j, jnp.broadcast_to(pr, (8, N_STATE)), acc_r)
            acc_i = jnp.where(rid == j, jnp.broadcast_to(pi_, (8, N_STATE)), acc_i)
        o_pr[...] = acc_r
        o_pi[...] = acc_i

    return pl.pallas_call(
        body, name="s5_param_fwd",
        out_shape=[_sds((S5_GROUP, N_STATE))] * 2 + [_sds((8, N_STATE))] * 2,
    )(lam_re, lam_im, logdt, b_re, b_im)


def _s5_param_bwd(lam_re, lam_im, logdt, b_re, b_im, d_lbr, d_lbi, d_bbr, d_bbi, group_ind):
    def body(lr, li, ld, br, bi, g0, g1, g2, g3, ind, o_lr, o_li, o_ld, o_br, o_bi):
        _, vjp = jax.vjp(_s5_param_fn, lr[...], li[...], ld[...], br[...], bi[...])
        d_lr, d_li, d_ld, d_br, d_bi = vjp((g0[...], g1[...], g2[...], g3[...]))
        o_lr[...] = d_lr
        o_li[...] = d_li
        o_ld[...] = _dot(jnp.broadcast_to(d_ld, (8, N_STATE)), ind[...], ((1,), (0,)), HI)
        o_br[...] = d_br
        o_bi[...] = d_bi

    return pl.pallas_call(
        body, name="s5_param_bwd",
        out_shape=[_sds((1, N_STATE))] * 2 + [_sds((8, LANES))] + [_sds((S5_GROUP, N_STATE))] * 2,
    )(lam_re, lam_im, logdt, b_re, b_im, d_lbr, d_lbi, d_bbr, d_bbi, group_ind)


def _fwd_in(x, norm_g, w_in_bf, tt):
    L = x.shape[0]

    def body(x_ref, g_ref, w_ref, u_ref, zs_ref, rw_ref, zr_ref):
        h = _rms(x_ref[...], g_ref[...])
        proj = jnp.dot(h.astype(BF16), w_ref[...], preferred_element_type=F32)
        u_ref[...] = proj[:, 0:D_S5]
        zs_ref[...] = proj[:, D_S5:2 * D_S5]
        rw_ref[...] = proj[:, 2 * D_S5:2 * D_S5 + D_SHIFT]
        zr_ref[...] = proj[:, 2 * D_S5 + D_SHIFT:D_IN]

    row = lambda n: pl.BlockSpec((tt, n), lambda i: (i, 0))
    return pl.pallas_call(
        body, name="fwd_in", grid=(L // tt,),
        in_specs=[row(D_MODEL), _const_spec((1, D_MODEL)), _const_spec((D_MODEL, D_IN))],
        out_specs=[row(D_S5), row(D_S5), row(D_SHIFT), row(D_RWKV)],
        out_shape=[_sds((L, D_S5)), _sds((L, D_S5)), _sds((L, D_SHIFT)), _sds((L, D_RWKV))],
        compiler_params=_params("parallel"),
    )(x, norm_g, w_in_bf)


S5_LANE_CHUNK = 512


def _tile_scan(re_ref, im_ref, pow_r_ref, pow_i_ref, carry_r_ref, carry_i_ref, reverse):
    t, n = re_ref.shape
    n_groups = t // 8
    ch = S5_LANE_CHUNK
    rid = lax.broadcasted_iota(jnp.int32, (8, ch), 0)
    for c in range(n // ch):
        cols = slice(c * ch, (c + 1) * ch)
        pow_r = pow_r_ref[:, cols]
        pow_i = pow_i_ref[:, cols]
        row = lambda tile, j: jnp.broadcast_to(tile[j:j + 1], (8, ch))
        levels = [(d, row(pow_r, 8 - d if reverse else d - 1), row(pow_i, 8 - d if reverse else d - 1))
                  for d in (1, 2, 4)]

        def group(g, carry):
            r0 = pl.multiple_of(((n_groups - 1 - g) if reverse else g) * 8, 8)
            xr = re_ref[pl.ds(r0, 8), cols]
            xi = im_ref[pl.ds(r0, 8), cols]
            for d, lr, li in levels:
                keep = (rid < 8 - d) if reverse else (rid >= d)
                shift = (8 - d) if reverse else d
                sr = jnp.where(keep, pltpu.roll(xr, shift, axis=0), 0.0)
                si = jnp.where(keep, pltpu.roll(xi, shift, axis=0), 0.0)
                mr, mi = _cmul(lr, li, sr, si)
                xr = xr + mr
                xi = xi + mi
            mr, mi = _cmul(pow_r, pow_i, carry[0], carry[1])
            xr = xr + mr
            xi = xi + mi
            re_ref[pl.ds(r0, 8), cols] = xr
            im_ref[pl.ds(r0, 8), cols] = xi
            last = 0 if reverse else 7
            return row(xr, last), row(xi, last)

        out = lax.fori_loop(0, n_groups, group, (carry_r_ref[:, cols], carry_i_ref[:, cols]))
        carry_r_ref[:, cols] = out[0]
        carry_i_ref[:, cols] = out[1]


def _s5_fwd(u, b4_re, b4_im, c4_re, c4_im, pow_r, pow_i, tt):
    L = u.shape[0]

    def body(u_ref, bre_ref, bim_ref, cre_ref, cim_ref, pr_ref, pi_ref, sre_o, sim_o, y_o, car_r, car_i):
        @pl.when(pl.program_id(0) == 0)
        def _():
            car_r[...] = jnp.zeros_like(car_r)
            car_i[...] = jnp.zeros_like(car_i)

        uv = u_ref[...]
        for q in range(S5_BLOCKS):
            uq = uv[:, q * LANES:(q + 1) * LANES]
            cols = slice(q * 512, (q + 1) * 512)
            sre_o[:, cols] = _dot(uq, bre_ref[q], ((1,), (0,)), HI)
            sim_o[:, cols] = _dot(uq, bim_ref[q], ((1,), (0,)), HI)
        _tile_scan(sre_o, sim_o, pr_ref, pi_ref, car_r, car_i, reverse=False)
        for q in range(S5_BLOCKS):
            cols = slice(q * 512, (q + 1) * 512)
            y_o[:, q * LANES:(q + 1) * LANES] = (_dot(sre_o[:, cols], cre_ref[q], ((1,), (0,)), HI)
                                                 - _dot(sim_o[:, cols], cim_ref[q], ((1,), (0,)), HI))

    row = lambda n: pl.BlockSpec((tt, n), lambda i: (i, 0))
    return pl.pallas_call(
        body, name="s5_fwd", grid=(L // tt,),
        in_specs=[row(D_S5)] + [_const_spec((S5_BLOCKS, LANES, 512))] * 2 + [_const_spec((S5_BLOCKS, 512, LANES))] * 2
        + [_const_spec((8, N_STATE))] * 2,
        out_specs=[row(N_STATE), row(N_STATE), row(D_S5)],
        out_shape=[_sds((L, N_STATE)), _sds((L, N_STATE)), _sds((L, D_S5))],
        scratch_shapes=[pltpu.VMEM((8, N_STATE), F32)] * 2,
        compiler_params=_params("arbitrary"),
    )(u, b4_re, b4_im, c4_re, c4_im, pow_r, pow_i)


def _rwkv_pre_fn(r, k, v, wa, w0, w2p, a0, a2p, k_k, k_a, ee):
    w = -_softplus(-(w0 + mm_hi(jnp.tanh(wa), w2p))) - 0.5
    logw = -jnp.exp(w)
    a = _sigmoid(a0 + mm_hi(wa, a2p))
    kkp = k * k_k
    kk = kkp / jnp.maximum(jnp.sqrt(mm_hi(kkp * kkp, ee)), 1e-12)
    k2 = k * (1.0 + (a - 1.0) * k_a)
    return r, logw, k2, v, -kk, kk * a


def _head_spec(tt):
    return pl.BlockSpec((N_HEADS, tt, HEAD), lambda i: (0, i, 0))


def _load_heads(ref):
    return jnp.concatenate([ref[h] for h in range(N_HEADS)], axis=-1)


def _store_heads(ref, val):
    for h in range(N_HEADS):
        ref[h] = val[:, h * HEAD:(h + 1) * HEAD]


def _shifted(rw, prev_blk, first):
    rolled = pltpu.roll(rw, 1, axis=0)
    prev_row = jnp.where(first, 0.0, prev_blk[7:8, :])
    rid = lax.broadcasted_iota(jnp.int32, rw.shape, 0)
    return jnp.where(rid == 0, jnp.broadcast_to(prev_row, rw.shape), rolled)


def _split_rw(t):
    return t[:, 0:512], t[:, 512:1024], t[:, 1024:1536], t[:, 1536:1664]


def _rwkv_pre_specs(tt):
    row = pl.BlockSpec((tt, D_SHIFT), lambda i: (i, 0))
    prev = pl.BlockSpec((8, D_SHIFT), lambda i: (jnp.maximum(i * (tt // 8) - 1, 0), 0))
    consts = [_const_spec((1, D_SHIFT)), _const_spec((1, D_RWKV)), _const_spec((LANES, D_RWKV)),
              _const_spec((1, D_RWKV)), _const_spec((LANES, D_RWKV)), _const_spec((1, D_RWKV)),
              _const_spec((1, D_RWKV)), _const_spec((D_RWKV, D_RWKV))]
    return [row, prev] + consts


def _rwkv_pre_fwd(rw, mu, w0, w2p, a0, a2p, k_k, k_a, ee, tt):
    L = rw.shape[0]

    def body(rw_ref, prev_ref, mu_ref, w0_ref, w2_ref, a0_ref, a2_ref, kk_ref, ka_ref, ee_ref, *outs):
        rwv = rw_ref[...]
        rws = rwv + (_shifted(rwv, prev_ref[...], pl.program_id(0) == 0) - rwv) * mu_ref[...]
        res = _rwkv_pre_fn(*_split_rw(rws), w0_ref[...], w2_ref[...], a0_ref[...], a2_ref[...],
                           kk_ref[...], ka_ref[...], ee_ref[...])
        for o, val in zip(outs, res):
            _store_heads(o, val)

    return pl.pallas_call(
        body, name="rwkv_pre_fwd", grid=(L // tt,),
        in_specs=_rwkv_pre_specs(tt), out_specs=[_head_spec(tt)] * 6, out_shape=[_sds((N_HEADS, L, HEAD))] * 6,
        compiler_params=_params("parallel"),
    )(rw, rw, mu, w0, w2p, a0, a2p, k_k, k_a, ee)


def _rwkv_pre_bwd(rw, mu, w0, w2p, a0, a2p, k_k, k_a, ee, cots, tt):
    L = rw.shape[0]
    n_t = L // tt

    def body(rw_ref, prev_ref, mu_ref, w0_ref, w2_ref, a0_ref, a2_ref, kk_ref, ka_ref, ee_ref,
             c_r, c_w, c_k, c_v, c_a, c_b, cb_r, cb_k, cb_v,
             drws_ref, dmu_o, dw0_o, dw2_o, da0_o, da2_o, dkk_o, dka_o,
             dmu, dw0, dw2, da0, da2, dkk, dka):
        i = pl.program_id(0)
        accs = (dmu, dw0, dw2, da0, da2, dkk, dka)

        @pl.when(i == 0)
        def _():
            for acc in accs:
                acc[...] = jnp.zeros_like(acc)

        rwv = rw_ref[...]
        diff = _shifted(rwv, prev_ref[...], i == 0) - rwv
        rws = rwv + diff * mu_ref[...]
        consts = (w0_ref[...], w2_ref[...], a0_ref[...], a2_ref[...], kk_ref[...], ka_ref[...])
        _, vjp = jax.vjp(lambda *a: _rwkv_pre_fn(*a, ee_ref[...]), *_split_rw(rws), *consts)
        scan = [_load_heads(c) for c in (c_r, c_w, c_k, c_v, c_a, c_b)]
        g = vjp((scan[0] + cb_r[...], scan[1], scan[2] + cb_k[...], scan[3] + cb_v[...], scan[4], scan[5]))
        drws = jnp.concatenate(g[0:4], axis=-1)
        drws_ref[...] = drws
        dmu[...] += jnp.sum(drws * diff, axis=0, keepdims=True)
        for acc, val in zip(accs[1:], g[4:]):
            acc[...] += val

        @pl.when(i == n_t - 1)
        def _():
            for acc, out in zip(accs, (dmu_o, dw0_o, dw2_o, da0_o, da2_o, dkk_o, dka_o)):
                out[...] = acc[...]

    row = pl.BlockSpec((tt, D_RWKV), lambda i: (i, 0))
    shapes = [(1, D_SHIFT), (1, D_RWKV), (LANES, D_RWKV), (1, D_RWKV), (LANES, D_RWKV), (1, D_RWKV), (1, D_RWKV)]
    return pl.pallas_call(
        body, name="rwkv_pre_bwd", grid=(n_t,),
        in_specs=_rwkv_pre_specs(tt) + [_head_spec(tt)] * 6 + [row] * 3,
        out_specs=[pl.BlockSpec((tt, D_SHIFT), lambda i: (i, 0))] + [_acc_spec(s) for s in shapes],
        out_shape=[_sds((L, D_SHIFT))] + [_sds(s) for s in shapes],
        scratch_shapes=[pltpu.VMEM(s, F32) for s in shapes],
        compiler_params=_params("arbitrary"),
    )(rw, rw, mu, w0, w2p, a0, a2p, k_k, k_a, ee, *cots)


def _bmm(a, b):
    return lax.dot_general(a, b, (((2,), (1,)), ((0,), (0,))), precision=HI, preferred_element_type=F32)


def _bmm_nt(a, b):
    return lax.dot_general(a, b, (((2,), (2,)), ((0,), (0,))), precision=HI, preferred_element_type=F32)


def _bmm_tn(a, b):
    return lax.dot_general(a, b, (((1,), (1,)), ((0,), (0,))), precision=HI, preferred_element_type=F32)


def _unit_lower_inverse(a):
    t = a.shape[-1]
    ti = lax.broadcasted_iota(jnp.int32, (t, t), 0)
    si = lax.broadcasted_iota(jnp.int32, (t, t), 1)
    inv = jnp.where(ti == si, 1.0, 0.0)[None] + a
    pw = a
    n = 1
    while 2 * n < t:
        pw = _bmm(pw, pw)
        inv = inv + _bmm(inv, pw)
        n *= 2
    return inv


@jax.custom_vjp
def _solve_unit_lower(a, rhs, inv):
    return _bmm(inv, rhs)


def _solve_fwd(a, rhs, inv):
    u = _bmm(inv, rhs)
    return u, (inv, u)


def _solve_bwd(res, du):
    inv, u = res
    d_rhs = _bmm_tn(inv, du)
    return _bmm_nt(d_rhs, u), d_rhs, jnp.zeros_like(inv)


_solve_unit_lower.defvjp(_solve_fwd, _solve_bwd)


def _rwkv_chunk(st0, r, logw, k, v, a, b, inv=None):
    n_h, t, _ = r.shape
    ti = lax.broadcasted_iota(jnp.int32, (t, t), 0)
    si = lax.broadcasted_iota(jnp.int32, (t, t), 1)
    incl = (ti >= si)[None]
    strict = (ti > si)[None]
    ones_tri = jnp.broadcast_to(jnp.where(ti >= si, 1.0, 0.0)[None], (n_h, t, t))
    log_p = _bmm(ones_tri, logw)
    p_in = jnp.exp(log_p)
    p_inv = jnp.exp(-log_p)
    at = a * jnp.exp(log_p - logw)
    rt = r * p_in
    bt = b * p_inv
    kt = k * p_inv
    a_ab = jnp.where(strict, _bmm_nt(at, bt), 0.0)
    a_ak = jnp.where(strict, _bmm_nt(at, kt), 0.0)
    if inv is None:
        inv = _unit_lower_inverse(a_ab)
    u = _solve_unit_lower(a_ab, _bmm(at, st0) + _bmm(a_ak, v), inv)
    y = (_bmm(rt, st0) + _bmm(jnp.where(incl, _bmm_nt(rt, bt), 0.0), u)
         + _bmm(jnp.where(incl, _bmm_nt(rt, kt), 0.0), v))
    p_end = jnp.swapaxes(p_in[:, t - 1:t, :], 1, 2)
    st1 = (st0 + _bmm_tn(bt, u) + _bmm_tn(kt, v)) * p_end
    return y, st1, inv


def _rwkv_scan_fwd(ops):
    n_h, L, n = ops[0].shape
    t = RWKV_CHUNK
    n_c = L // t

    def body(r_ref, w_ref, k_ref, v_ref, a_ref, b_ref, y_ref, st_ref, inv_ref, st):
        @pl.when(pl.program_id(0) == 0)
        def _():
            st[...] = jnp.zeros_like(st)

        st0 = st[...]
        st_ref[0] = st0
        y, st1, inv = _rwkv_chunk(st0, r_ref[...], w_ref[...], k_ref[...], v_ref[...], a_ref[...], b_ref[...])
        y_ref[...] = y
        inv_ref[0] = inv
        st[...] = st1

    blk = pl.BlockSpec((n_h, t, n), lambda c: (0, c, 0))
    return pl.pallas_call(
        body, name="rwkv_scan_fwd", grid=(n_c,), in_specs=[blk] * 6,
        out_specs=[blk, pl.BlockSpec((1, n_h, n, n), lambda c: (c, 0, 0, 0)),
                   pl.BlockSpec((1, n_h, t, t), lambda c: (c, 0, 0, 0))],
        out_shape=[_sds((n_h, L, n)), _sds((n_c, n_h, n, n)), _sds((n_c, n_h, t, t))],
        scratch_shapes=[pltpu.VMEM((n_h, n, n), F32)],
        compiler_params=_params("arbitrary"),
    )(*ops)


def _rwkv_scan_bwd(ops, states, invs, dy):
    n_h, L, n = ops[0].shape
    t = RWKV_CHUNK
    n_c = L // t

    def body(r_ref, w_ref, k_ref, v_ref, a_ref, b_ref, st_ref, inv_ref, dy_ref, dr, dw, dk, dv, da, db, dst):
        @pl.when(pl.program_id(0) == 0)
        def _():
            dst[...] = jnp.zeros_like(dst)

        inv = inv_ref[0]
        _, vjp = jax.vjp(lambda *a: _rwkv_chunk(*a, inv=inv)[:2], st_ref[0], r_ref[...], w_ref[...], k_ref[...],
                         v_ref[...], a_ref[...], b_ref[...])
        g = vjp((dy_ref[...], dst[...]))
        dst[...] = g[0]
        for out, val in zip((dr, dw, dk, dv, da, db), g[1:]):
            out[...] = val

    blk = pl.BlockSpec((n_h, t, n), lambda c: (0, n_c - 1 - c, 0))
    per_chunk = lambda m: pl.BlockSpec((1, n_h, m, m), lambda c: (n_c - 1 - c, 0, 0, 0))
    return pl.pallas_call(
        body, name="rwkv_scan_bwd", grid=(n_c,),
        in_specs=[blk] * 6 + [per_chunk(n), per_chunk(t), blk],
        out_specs=[blk] * 6, out_shape=[_sds((n_h, L, n))] * 6,
        scratch_shapes=[pltpu.VMEM((n_h, n, n), F32)],
        compiler_params=_params("arbitrary"),
    )(*ops, states, invs, dy)


def _post_fn(x, u, zs, zr, ysc, r, k2, v, y_ssm, d, glu_w, glu_b, ln_w, ln_b, r_k,
             wo_s5, wo_rwkv, gf, tgt, ee):
    y3 = _gelu(y_ssm + d * u)
    y_s5 = y3 * _sigmoid(mm_bf(y3, glu_w) + glu_b) * _silu(zs)
    mean = mm_hi(ysc, ee) * (1.0 / HEAD)
    yc = ysc - mean
    var = mm_hi(yc * yc, ee) * (1.0 / HEAD)
    gn = yc * lax.rsqrt(var + GN_EPS) * ln_w + ln_b
    bonus = mm_hi(r * k2 * r_k, ee) * v
    y_rwkv = (gn + bonus) * _silu(zr)
    x2 = x + mm_bf(y_s5, wo_s5) + mm_bf(y_rwkv, wo_rwkv)
    err = _rms(x2, gf) - tgt
    return 0.5 * jnp.mean(err * err, axis=-1, keepdims=True)


def _post(x, u, zs, zr, ysc, r, k2, v, y_ssm, d, glu_w, glu_b, ln_w, ln_b, r_k, w_out, gf, tgt, ee, tt):
    L = x.shape[0]
    n_t = L // tt
    acc_shapes = [(1, D_S5), (D_S5, D_S5), (1, D_S5), (1, D_RWKV), (1, D_RWKV), (1, D_RWKV),
                  (D_MODEL, D_MODEL), (1, D_MODEL), (8, LANES)]

    def body(x_ref, u_ref, zs_ref, zr_ref, ysc_ref, r_ref, k2_ref, v_ref, yssm_ref,
             d_ref, gw_ref, gb_ref, lw_ref, lb_ref, rk_ref, wo_ref, gf_ref, tgt_ref, ee_ref,
             dx_o, du_o, dzs_o, dzr_o, dysc_o, dr_o, dk2_o, dv_o, dyssm_o,
             dd_o, dgw_o, dgb_o, dlw_o, dlb_o, drk_o, dwo_o, dgf_o, loss_o,
             dd, dgw, dgb, dlw, dlb, drk, dwo, dgf, loss):
        i = pl.program_id(0)
        accs = (dd, dgw, dgb, dlw, dlb, drk, dwo, dgf, loss)

        @pl.when(i == 0)
        def _():
            for acc in accs:
                acc[...] = jnp.zeros_like(acc)

        args = (x_ref[...], u_ref[...], zs_ref[...], zr_ref[...],
                _load_heads(ysc_ref), _load_heads(r_ref), _load_heads(k2_ref), _load_heads(v_ref), yssm_ref[...],
                d_ref[...], gw_ref[...], gb_ref[...], lw_ref[...], lb_ref[...], rk_ref[...],
                wo_ref[0:D_S5, :], wo_ref[D_S5:D_MODEL, :], gf_ref[...])
        rows, vjp = jax.vjp(lambda *a: _post_fn(*a, tgt_ref[...], ee_ref[...]), *args)
        g = vjp(jnp.ones_like(rows))
        for out, val in zip((dx_o, du_o, dzs_o, dzr_o), g[0:4]):
            out[...] = val
        _store_heads(dysc_o, g[4])
        for out, val in zip((dr_o, dk2_o, dv_o, dyssm_o), g[5:9]):
            out[...] = val
        for acc, val in zip((dd, dgw, dgb, dlw, dlb, drk), g[9:15]):
            acc[...] += val
        dwo[0:D_S5, :] += g[15]
        dwo[D_S5:D_MODEL, :] += g[16]
        dgf[...] += g[17]
        loss[...] += jnp.broadcast_to(jnp.sum(rows, axis=0, keepdims=True), loss.shape)

        @pl.when(i == n_t - 1)
        def _():
            for acc, out in zip(accs, (dd_o, dgw_o, dgb_o, dlw_o, dlb_o, drk_o, dwo_o, dgf_o, loss_o)):
                pltpu.sync_copy(acc, out)

    row = lambda n: pl.BlockSpec((tt, n), lambda i: (i, 0))
    in_specs = ([row(D_MODEL)] + [row(512)] * 3 + [_head_spec(tt)] * 4 + [row(D_S5)]
                + [_const_spec(s) for s in [(1, D_S5), (D_S5, D_S5), (1, D_S5), (1, D_RWKV), (1, D_RWKV), (1, D_RWKV),
                                            (D_MODEL, D_MODEL), (1, D_MODEL)]]
                + [row(D_MODEL), _const_spec((D_RWKV, D_RWKV))])
    out_rows = [D_MODEL] + [512] * 3 + [None] + [512] * 4
    return pl.pallas_call(
        body, name="post_fwd_bwd", grid=(n_t,), in_specs=in_specs,
        out_specs=[row(n) if n else _head_spec(tt) for n in out_rows] + [_ANY] * len(acc_shapes),
        out_shape=([_sds((L, n)) if n else _sds((N_HEADS, L, HEAD)) for n in out_rows]
                   + [_sds(s) for s in acc_shapes]),
        scratch_shapes=[pltpu.VMEM(s, F32) for s in acc_shapes],
        compiler_params=_params("arbitrary"),
    )(x, u, zs, zr, ysc, r, k2, v, y_ssm, d, glu_w, glu_b, ln_w, ln_b, r_k, w_out, gf, tgt, ee)


def _s5_bwd(u, du_direct, dy, s_re, s_im, b4_re, b4_im, c4_re, c4_im, pow_r, pow_i, tt):
    L = u.shape[0]
    n_t = L // tt
    acc_shapes = ([(S5_BLOCKS, LANES, 512)] * 2 + [(S5_BLOCKS, 512, LANES)] * 2 + [(1, N_STATE)] * 2)

    def body(u_ref, dud_ref, dy_ref, sre_ref, sim_ref, pre_ref, pim_ref, bre_ref, bim_ref, cre_ref, cim_ref,
             pr_ref, pi_ref, du_o, dbre_o, dbim_o, dcre_o, dcim_o, dlr_o, dli_o,
             dbre, dbim, dcre, dcim, dlr, dli, gre, gim, car_r, car_i):
        i = pl.program_id(0)

        @pl.when(i == 0)
        def _():
            for acc in (dbre, dbim, dcre, dcim, dlr, dli, car_r, car_i):
                acc[...] = jnp.zeros_like(acc)

        uv = u_ref[...]
        dyv = dy_ref[...]
        blocks = [slice(q * 512, (q + 1) * 512) for q in range(S5_BLOCKS)]
        lanes = [slice(q * LANES, (q + 1) * LANES) for q in range(S5_BLOCKS)]
        for q in range(S5_BLOCKS):
            gre[:, blocks[q]] = _dot(dyv[:, lanes[q]], cre_ref[q], ((1,), (1,)), HI)
            gim[:, blocks[q]] = -_dot(dyv[:, lanes[q]], cim_ref[q], ((1,), (1,)), HI)
        _tile_scan(gre, gim, pr_ref, pi_ref, car_r, car_i, reverse=True)
        for q in range(S5_BLOCKS):
            gr = gre[:, blocks[q]]
            gi = gim[:, blocks[q]]
            sr = sre_ref[:, blocks[q]]
            si = sim_ref[:, blocks[q]]
            du_o[:, lanes[q]] = (dud_ref[:, lanes[q]] + _dot(gr, bre_ref[q], ((1,), (1,)), HI)
                                 + _dot(gi, bim_ref[q], ((1,), (1,)), HI))
            dbre[q] += _dot(uv[:, lanes[q]], gr, ((0,), (0,)), HI)
            dbim[q] += _dot(uv[:, lanes[q]], gi, ((0,), (0,)), HI)
            dcre[q] += _dot(sr, dyv[:, lanes[q]], ((0,), (0,)), HI)
            dcim[q] -= _dot(si, dyv[:, lanes[q]], ((0,), (0,)), HI)
            rid = lax.broadcasted_iota(jnp.int32, sr.shape, 0)
            first = i == n_t - 1
            prev_r = jnp.where(first, 0.0, pre_ref[7:8, blocks[q]])
            prev_i = jnp.where(first, 0.0, pim_ref[7:8, blocks[q]])
            pr = jnp.where(rid == 0, jnp.broadcast_to(prev_r, sr.shape), pltpu.roll(sr, 1, axis=0))
            pi_ = jnp.where(rid == 0, jnp.broadcast_to(prev_i, si.shape), pltpu.roll(si, 1, axis=0))
            dlr[:, blocks[q]] += jnp.sum(pr * gr + pi_ * gi, axis=0, keepdims=True)
            dli[:, blocks[q]] += jnp.sum(pr * gi - pi_ * gr, axis=0, keepdims=True)

        @pl.when(i == n_t - 1)
        def _():
            for acc, out in zip((dbre, dbim, dcre, dcim, dlr, dli), (dbre_o, dbim_o, dcre_o, dcim_o, dlr_o, dli_o)):
                out[...] = acc[...]

    row = lambda n: pl.BlockSpec((tt, n), lambda i: (n_t - 1 - i, 0))
    prev = pl.BlockSpec((8, N_STATE), lambda i: (jnp.maximum((n_t - 1 - i) * (tt // 8) - 1, 0), 0))
    return pl.pallas_call(
        body, name="s5_bwd", grid=(n_t,),
        in_specs=[row(D_S5)] * 3 + [row(N_STATE)] * 2 + [prev] * 2
        + [_const_spec((S5_BLOCKS, LANES, 512))] * 2 + [_const_spec((S5_BLOCKS, 512, LANES))] * 2
        + [_const_spec((8, N_STATE))] * 2,
        out_specs=[row(D_S5)] + [_acc_spec(s) for s in acc_shapes],
        out_shape=[_sds((L, D_S5))] + [_sds(s) for s in acc_shapes],
        scratch_shapes=[pltpu.VMEM(s, F32) for s in acc_shapes] + [pltpu.VMEM((tt, N_STATE), F32)] * 2
        + [pltpu.VMEM((8, N_STATE), F32)] * 2,
        compiler_params=_params("arbitrary"),
    )(u, du_direct, dy, s_re, s_im, s_re, s_im, b4_re, b4_im, c4_re, c4_im, pow_r, pow_i)


def _bwd_in(x, norm_g, w_in_bf, mu, dx2, du, dzs, drws, dzr, tt):
    L = x.shape[0]
    n_t = L // tt

    def body(x_ref, g_ref, w_ref, mu_ref, dx2_ref, du_ref, dzs_ref, drws_ref, nxt_ref, dzr_ref,
             gx_o, dproj_o, dg_o, dg):
        i = pl.program_id(0)

        @pl.when(i == 0)
        def _():
            dg[...] = jnp.zeros_like(dg)

        drws_v = drws_ref[...]
        rid = lax.broadcasted_iota(jnp.int32, drws_v.shape, 0)
        nxt_row = jnp.where(i == n_t - 1, 0.0, nxt_ref[0:1, :])
        nxt = jnp.where(rid == tt - 1, jnp.broadcast_to(nxt_row, drws_v.shape), pltpu.roll(drws_v, tt - 1, axis=0))
        muv = mu_ref[...]
        drw = drws_v * (1.0 - muv) + nxt * muv
        dproj_o[:, 0:D_S5] = du_ref[...]
        dproj_o[:, D_S5:2 * D_S5] = dzs_ref[...]
        dproj_o[:, 2 * D_S5:2 * D_S5 + D_SHIFT] = drw
        dproj_o[:, 2 * D_S5 + D_SHIFT:D_IN] = dzr_ref[...]
        dh = _dot(dproj_o[...].astype(BF16), w_ref[...], ((1,), (1,)), None)
        _, vjp = jax.vjp(_rms, x_ref[...], g_ref[...])
        dxh, dgv = vjp(dh)
        gx_o[...] = dx2_ref[...] + dxh
        dg[...] += dgv

        @pl.when(i == n_t - 1)
        def _():
            dg_o[...] = dg[...]

    row = lambda n: pl.BlockSpec((tt, n), lambda i: (i, 0))
    nxt = pl.BlockSpec((8, D_SHIFT), lambda i: (jnp.minimum((i + 1) * (tt // 8), L // 8 - 1), 0))
    return pl.pallas_call(
        body, name="bwd_in", grid=(n_t,),
        in_specs=[row(D_MODEL), _const_spec((1, D_MODEL)), _const_spec((D_MODEL, D_IN)), _const_spec((1, D_SHIFT)),
                  row(D_MODEL), row(D_S5), row(D_S5), row(D_SHIFT), nxt, row(D_RWKV)],
        out_specs=[row(D_MODEL), row(D_IN), _acc_spec((1, D_MODEL))],
        out_shape=[_sds((L, D_MODEL)), _sds((L, D_IN)), _sds((1, D_MODEL))],
        scratch_shapes=[pltpu.VMEM((1, D_MODEL), F32)],
        compiler_params=_params("arbitrary"),
    )(x, norm_g, w_in_bf, mu, dx2, du, dzs, drws, drws, dzr)


def _grad_w_in(x, norm_g, dproj, tt, cn):
    L = x.shape[0]

    def body(x_ref, g_ref, dp_ref, out_ref):
        @pl.when(pl.program_id(1) == 0)
        def _():
            out_ref[...] = jnp.zeros_like(out_ref)

        h = _rms(x_ref[...], g_ref[...])
        out_ref[...] += _dot(h.astype(BF16), dp_ref[...].astype(BF16), ((0,), (0,)), None)

    return pl.pallas_call(
        body, name="grad_w_in", grid=(D_IN // cn, L // tt),
        in_specs=[pl.BlockSpec((tt, D_MODEL), lambda j, i: (i, 0)), pl.BlockSpec((1, D_MODEL), lambda j, i: (0, 0)),
                  pl.BlockSpec((tt, cn), lambda j, i: (i, j))],
        out_specs=pl.BlockSpec((D_MODEL, cn), lambda j, i: (0, j)),
        out_shape=_sds((D_MODEL, D_IN)),
        compiler_params=pltpu.CompilerParams(dimension_semantics=("parallel", "arbitrary"), vmem_limit_bytes=VMEM_LIMIT),
    )(x, norm_g, dproj)


def _block_diag_b(bbar):
    bb = bbar.reshape(S5_GROUP, S5_BLOCKS, 8, S5_STATE)
    return jnp.einsum('hqgp,Gg->qGhgp', bb, jnp.eye(8, dtype=F32)).reshape(S5_BLOCKS, LANES, 512)


def _block_diag_b_t(db4):
    d = db4.reshape(S5_BLOCKS, 8, S5_GROUP, 8, S5_STATE)
    return jnp.einsum('qGhgp,Gg->hqgp', d, jnp.eye(8, dtype=F32)).reshape(S5_GROUP, N_STATE)


def _block_diag_c(c):
    cc = c.reshape(S5_BLOCKS, 8, S5_GROUP, S5_STATE)
    return jnp.einsum('qghp,gG->qgpGh', cc, jnp.eye(8, dtype=F32)).reshape(S5_BLOCKS, 512, LANES)


def _block_diag_c_t(dc4):
    d = dc4.reshape(S5_BLOCKS, 8, S5_STATE, 8, S5_GROUP)
    return jnp.einsum('qgpGh,gG->qghp', d, jnp.eye(8, dtype=F32)).reshape(S5_GROUPS, S5_GROUP, S5_STATE)


def _local_step(x, tgt, w):
    L = x.shape[0]
    tt = min(256, L)
    tp = min(128, L)
    ee = _head_sum_matrix()

    lam_re = w['s5_lam_re'].reshape(1, N_STATE)
    lam_im = w['s5_lam_im'].reshape(1, N_STATE)
    logdt = jnp.repeat(w['s5_log_dt'], S5_STATE).reshape(1, N_STATE)
    b_re_t = w['s5_b_re'].transpose(2, 0, 1).reshape(S5_GROUP, N_STATE)
    b_im_t = w['s5_b_im'].transpose(2, 0, 1).reshape(S5_GROUP, N_STATE)
    bbr, bbi, pow_r, pow_i = _s5_param_fwd(lam_re, lam_im, logdt, b_re_t, b_im_t)
    b4_re, b4_im = _block_diag_b(bbr), _block_diag_b(bbi)
    c4_re, c4_im = _block_diag_c(w['s5_c_re']), _block_diag_c(w['s5_c_im'])

    norm_g = w['norm_g'].reshape(1, D_MODEL)
    w_in_bf = w['w_in'].astype(BF16)
    u, zs, rw, zr = _fwd_in(x, norm_g, w_in_bf, tt)
    s_re, s_im, y_ssm = _s5_fwd(u, b4_re, b4_im, c4_re, c4_im, pow_r, pow_i, tt)

    row = lambda t: t.reshape(1, -1)
    zpad = jnp.zeros((HEAD, D_RWKV), F32)
    w2p = jnp.concatenate([w['rwkv_w2'], zpad], axis=0)
    a2p = jnp.concatenate([zpad, w['rwkv_a2']], axis=0)
    pre_consts = (row(w['rwkv_mu']), row(w['rwkv_w0']), w2p, row(w['rwkv_a0']), a2p,
                  row(w['rwkv_k_k']), row(w['rwkv_k_a']), ee)
    ops = _rwkv_pre_fwd(rw, *pre_consts, tt)
    ysc, states, invs = _rwkv_scan_fwd(ops)

    post = _post(x, u, zs, zr, ysc, ops[0], ops[2], ops[3], y_ssm,
                 row(w['s5_d']), w['s5_glu_w'], row(w['s5_glu_b']), row(w['rwkv_ln_w']), row(w['rwkv_ln_b']),
                 row(w['rwkv_r_k']), w['w_out'], row(w['final_g']), tgt, ee, tp)
    (dx2, du_d, dzs, dzr, dysc, dr_b, dk2_b, dv_b, dy_ssm,
     dd, dglu_w, dglu_b, dln_w, dln_b, dr_k, dw_out, dgf, loss) = post

    du, db4_re, db4_im, dc4_re, dc4_im, dlbr, dlbi = _s5_bwd(
        u, du_d, dy_ssm, s_re, s_im, b4_re, b4_im, c4_re, c4_im, pow_r[::-1], -pow_i[::-1], tt)
    group_ind = (jnp.arange(N_STATE)[:, None] // S5_STATE == jnp.arange(LANES)[None, :]).astype(F32)
    dlam_re, dlam_im, dlogdt, db_re_t, db_im_t = _s5_param_bwd(
        lam_re, lam_im, logdt, b_re_t, b_im_t, dlbr, dlbi, _block_diag_b_t(db4_re), _block_diag_b_t(db4_im), group_ind)

    cots = list(_rwkv_scan_bwd(ops, states, invs, dysc)) + [dr_b, dk2_b, dv_b]
    drws, dmu, dw0, dw2p, da0, da2p, dk_k, dk_a = _rwkv_pre_bwd(rw, *pre_consts, cots, tt)

    grad_x, dproj, dnorm_g = _bwd_in(x, norm_g, w_in_bf, row(w['rwkv_mu']), dx2, du, dzs, drws, dzr, tt)
    dw_in = _grad_w_in(x, norm_g, dproj, tt, 640)

    unb = lambda t: t.reshape(S5_GROUP, S5_GROUPS, S5_STATE).transpose(1, 2, 0)
    grads = {
        'norm_g': dnorm_g.reshape(D_MODEL), 'w_in': dw_in,
        's5_lam_re': dlam_re.reshape(S5_GROUPS, S5_STATE), 's5_lam_im': dlam_im.reshape(S5_GROUPS, S5_STATE),
        's5_log_dt': dlogdt[0, :S5_GROUPS], 's5_b_re': unb(db_re_t), 's5_b_im': unb(db_im_t),
        's5_c_re': _block_diag_c_t(dc4_re), 's5_c_im': _block_diag_c_t(dc4_im),
        's5_d': dd.reshape(D_S5), 's5_glu_w': dglu_w, 's5_glu_b': dglu_b.reshape(D_S5),
        'rwkv_mu': dmu.reshape(-1), 'rwkv_w0': dw0.reshape(-1), 'rwkv_w2': dw2p[:HEAD], 'rwkv_a0': da0.reshape(-1),
        'rwkv_a2': da2p[HEAD:], 'rwkv_k_k': dk_k.reshape(-1), 'rwkv_k_a': dk_a.reshape(-1),
        'rwkv_r_k': dr_k.reshape(N_HEADS, HEAD), 'rwkv_ln_w': dln_w.reshape(-1), 'rwkv_ln_b': dln_b.reshape(-1),
        'w_out': dw_out, 'final_g': dgf.reshape(D_MODEL),
    }
    return loss, grad_x, grads


def _exchange(arrays, gather, axes, name):
    n = len(arrays)
    group = 2 ** len(axes)

    def body(*refs):
        send_refs, recv_refs = refs[:n], refs[n:2 * n]
        send_sems, recv_sems, local_sems = refs[2 * n:]
        pos = {ax: lax.axis_index(ax) for ax in ("x", "y", "c")}

        def index_of(p):
            idx = 0
            for ax in axes:
                idx = 2 * idx + p[ax]
            return idx

        me = index_of(pos)
        own, outs, arrivals = [], [], []
        for i, (send_ref, recv_ref) in enumerate(zip(send_refs, recv_refs)):
            def block_for(dev, send_ref=send_ref, whole=gather[i]):
                return send_ref if whole else send_ref.at[dev]

            own.append(pltpu.make_async_copy(block_for(me), recv_ref.at[me], local_sems.at[i]))
            own[-1].start()
            for k in range(1, group):
                peer = dict(pos)
                for bit, ax in enumerate(axes):
                    if (k >> bit) & 1:
                        peer[ax] = 1 - pos[ax]
                peer_idx = index_of(peer)
                sems = dict(send_sem=send_sems.at[i, k - 1], recv_sem=recv_sems.at[i, k - 1],
                            device_id=(peer["x"], peer["y"], peer["c"]), device_id_type=pl.DeviceIdType.MESH)
                outs.append(pltpu.make_async_remote_copy(src_ref=block_for(peer_idx), dst_ref=recv_ref.at[me], **sems))
                outs[-1].start()
                arrivals.append(
                    pltpu.make_async_remote_copy(src_ref=block_for(peer_idx), dst_ref=recv_ref.at[peer_idx], **sems))
        for copy in arrivals:
            copy.wait_recv()
        for copy in outs:
            copy.wait_send()
        for copy in own:
            copy.wait()

    return pl.pallas_call(
        body, name=name, in_specs=[_ANY] * n, out_specs=[_ANY] * n,
        out_shape=[jax.ShapeDtypeStruct(((group,) + a.shape) if whole else a.shape, a.dtype)
                   for a, whole in zip(arrays, gather)],
        scratch_shapes=[pltpu.SemaphoreType.DMA((n, group - 1)), pltpu.SemaphoreType.DMA((n, group - 1)),
                        pltpu.SemaphoreType.DMA((n,))],
        compiler_params=pltpu.CompilerParams(has_side_effects=True),
    )(*arrays)


def _sum_devices(ref):
    g = ref[0].astype(F32)
    for s in range(1, ref.shape[0]):
        g = g + ref[s].astype(F32)
    return g


def _adamw_math(g, w, m, v):
    m_new = ADAM_B1 * m + (1.0 - ADAM_B1) * g
    v_new = ADAM_B2 * v + (1.0 - ADAM_B2) * (g * g)
    m_hat = m_new / (1.0 - ADAM_B1 ** ADAM_STEP)
    v_hat = v_new / (1.0 - ADAM_B2 ** ADAM_STEP)
    return -ADAM_LR * (m_hat / (jnp.sqrt(v_hat) + ADAM_EPS) + ADAM_WD * w), m_new, v_new


def _adamw(gs, ws, ms, vs, reduce, name):
    n = len(ws)

    def body(*refs):
        g_refs, w_refs, m_refs, v_refs = (refs[j * n:(j + 1) * n] for j in range(4))
        outs = refs[4 * n:]
        for i in range(n):
            g = _sum_devices(g_refs[i]) if reduce else g_refs[i][...]
            res = _adamw_math(g, w_refs[i][...], m_refs[i][...], v_refs[i][...])
            for j, val in enumerate(((g,) if reduce else ()) + res):
                outs[j * n + i][...] = val

    return pl.pallas_call(
        body, name=name, out_shape=[_sds(w.shape) for w in ws] * (4 if reduce else 3),
        compiler_params=pltpu.CompilerParams(vmem_limit_bytes=VMEM_LIMIT),
    )(*gs, *ws, *ms, *vs)


def _sum_blocks(recv):
    def body(recv_ref, out_ref):
        out_ref[...] = _sum_devices(recv_ref)

    return pl.pallas_call(body, name="sum_small_grads", out_shape=_sds(recv.shape[1:]))(recv)


_WEIGHTS = [
    ('norm_g', (1, 1024), False), ('w_in', (1, 1024, 400), True), ('s5_lam_re', (1, 32, 64), False),
    ('s5_lam_im', (1, 32, 64), False), ('s5_log_dt', (1, 32), False), ('s5_b_re', (1, 32, 64, 16), False),
    ('s5_b_im', (1, 32, 64, 16), False), ('s5_c_re', (1, 32, 16, 64), False), ('s5_c_im', (1, 32, 16, 64), False),
    ('s5_d', (1, 512), False), ('s5_glu_w', (1, 64, 512), True), ('s5_glu_b', (1, 512), False),
    ('rwkv_mu', (1, 1664), False), ('rwkv_w0', (1, 512), False), ('rwkv_w2', (1, 64, 64), True),
    ('rwkv_a0', (1, 512), False), ('rwkv_a2', (1, 64, 64), True), ('rwkv_k_k', (1, 512), False),
    ('rwkv_k_a', (1, 512), False), ('rwkv_r_k', (1, 8, 64), False), ('rwkv_ln_w', (1, 512), False),
    ('rwkv_ln_b', (1, 512), False), ('w_out', (1, 128, 1024), True), ('final_g', (1024,), False),
]
_SHARDED = [(n, s) for n, s, sharded in _WEIGHTS if sharded]
_SMALL = [(n, s) for n, s, sharded in _WEIGHTS if not sharded]
_COLUMN_SHARDED = ('w_in', 'rwkv_w2', 'rwkv_a2')
_SMALL_ROWS = -(-sum(math.prod(s) for _, s in _SMALL) // (8 * LANES)) * 8


def _pack_small(grads):
    flat = [grads[n].reshape(-1) for n, _ in _SMALL]
    pad = _SMALL_ROWS * LANES - sum(f.size for f in flat)
    return jnp.concatenate(flat + [jnp.zeros((pad,), F32)]).reshape(_SMALL_ROWS, LANES)


def _unpack_small(packed):
    flat = packed.reshape(-1)
    out, off = {}, 0
    for n, s in _SMALL:
        size = math.prod(s)
        out[n] = flat[off:off + size].reshape(s)
        off += size
    return out


_BF16_OPERANDS = ('w_in', 's5_glu_w', 'w_out')


def _join_shards(name, blocks):
    _, rows, cols = blocks.shape
    if name in _COLUMN_SHARDED:
        return blocks.transpose(1, 0, 2).reshape(rows, N_DEV * cols)
    return blocks.reshape(N_DEV * rows, cols)


def _split_shards(name, full, shard_shape):
    rows, cols = shard_shape
    if name in _COLUMN_SHARDED:
        return full.reshape(rows, N_DEV, cols).transpose(1, 0, 2)
    return full.reshape(N_DEV, rows, cols)


def kernel(x, norm_g, w_in, s5_lam_re, s5_lam_im, s5_log_dt, s5_b_re, s5_b_im, s5_c_re, s5_c_im, s5_d, s5_glu_w, s5_glu_b, rwkv_mu, rwkv_w0, rwkv_w2, rwkv_a0, rwkv_a2, rwkv_k_k, rwkv_k_a, rwkv_r_k, rwkv_ln_w, rwkv_ln_b, w_out, final_g, loss_target, m_norm_g, m_w_in, m_s5_lam_re, m_s5_lam_im, m_s5_log_dt, m_s5_b_re, m_s5_b_im, m_s5_c_re, m_s5_c_im, m_s5_d, m_s5_glu_w, m_s5_glu_b, m_rwkv_mu, m_rwkv_w0, m_rwkv_w2, m_rwkv_a0, m_rwkv_a2, m_rwkv_k_k, m_rwkv_k_a, m_rwkv_r_k, m_rwkv_ln_w, m_rwkv_ln_b, m_w_out, m_final_g, v_norm_g, v_w_in, v_s5_lam_re, v_s5_lam_im, v_s5_log_dt, v_s5_b_re, v_s5_b_im, v_s5_c_re, v_s5_c_im, v_s5_d, v_s5_glu_w, v_s5_glu_b, v_rwkv_mu, v_rwkv_w0, v_rwkv_w2, v_rwkv_a0, v_rwkv_a2, v_rwkv_k_k, v_rwkv_k_a, v_rwkv_r_k, v_rwkv_ln_w, v_rwkv_ln_b, v_w_out, v_final_g):
    given = dict(locals())

    n_sh = len(_SHARDED)
    everyone = ("x", "y", "c")
    shards = [given[n][0].astype(BF16 if n in _BF16_OPERANDS else F32) for n, _ in _SHARDED]
    gathered = _exchange(shards, (True,) * n_sh, everyone, "gather_weights")
    local = {n: _join_shards(n, blocks).astype(F32 if n != 'w_in' else BF16)
             for (n, _), blocks in zip(_SHARDED, gathered)}
    local.update({n: (given[n][0] if len(s) > 1 else given[n]) for n, s in _SMALL})

    loss, grad_x, grads = _local_step(x[0], loss_target[0], local)

    blocks = [_split_shards(n, grads[n], s[1:]).astype(BF16) for n, s in _SHARDED]
    recv = _exchange(blocks + [_pack_small(grads)], (False,) * n_sh + (True,), everyone, "exchange_grads")

    result = {}
    for group, name in (([0], "adamw_w_in"), ([1, 2, 3, 4], "adamw_shards")):
        ns = [_SHARDED[i][0] for i in group]
        res = _adamw([recv[i] for i in group], [given[n][0] for n in ns], [given['m_' + n][0] for n in ns],
                     [given['v_' + n][0] for n in ns], True, name)
        for j, n in enumerate(ns):
            result[n] = [res[k * len(ns) + j][None] for k in range(4)]
    g_small = _unpack_small(_sum_blocks(recv[-1]))
    two_d = lambda t: t.reshape(1, -1) if t.ndim == 1 else t
    ns = [n for n, _ in _SMALL]
    res = _adamw([two_d(g_small[n]) for n in ns], [two_d(given[n]) for n in ns], [two_d(given['m_' + n]) for n in ns],
                 [two_d(given['v_' + n]) for n in ns], False, "adamw_small")
    for j, (n, s) in enumerate(_SMALL):
        result[n] = [g_small[n]] + [res[k * len(ns) + j].reshape(s) for k in range(3)]

    total = lax.psum(loss[0, 0], ("x", "y", "c"))
    outs = [total, grad_x[None]]
    for k in range(4):
        outs += [result[n][k] for n, _, _ in _WEIGHTS]
    return tuple(outs)
```

```python
import functools
import math

import jax
import jax.numpy as jnp
from jax import lax
from jax.experimental import pallas as pl
from jax.experimental.pallas import tpu as pltpu

F32 = jnp.float32
BF16 = jnp.bfloat16
HI = lax.Precision.HIGH

D_MODEL = 1024
D_S5 = 512
D_RWKV = 512
S5_GROUPS = 32
S5_GROUP = 16
S5_STATE = 64
N_STATE = S5_GROUPS * S5_STATE
N_HEADS = 8
HEAD = 64
D_SHIFT = 3 * D_RWKV + 128
D_IN = 2 * D_S5 + D_SHIFT + D_RWKV
NORM_EPS = 1e-6
GN_EPS = 64e-5
N_DEV = 8
LANES = 128
S5_BLOCKS = 4
RWKV_CHUNK = 64
VMEM_LIMIT = 56 * 1024 * 1024

ADAM_LR = 0.001
ADAM_B1 = 0.9
ADAM_B2 = 0.999
ADAM_EPS = 1e-08
ADAM_WD = 0.01
ADAM_STEP = 10


def _dot(a, b, dims, prec):
    return lax.dot_general(a, b, (dims, ((), ())), precision=prec, preferred_element_type=F32)


def _make_mm(cast, prec):
    @jax.custom_vjp
    def mm(a, b):
        return _dot(cast(a), cast(b), ((1,), (0,)), prec)

    def fwd(a, b):
        return mm(a, b), (a, b)

    def bwd(res, g):
        a, b = res
        return (_dot(cast(g), cast(b), ((1,), (1,)), prec), _dot(cast(a), cast(g), ((0,), (0,)), prec))

    mm.defvjp(fwd, bwd)
    return mm


mm_hi = _make_mm(lambda t: t, HI)
mm_bf = _make_mm(lambda t: t.astype(BF16), None)


def _sigmoid(x):
    return 1.0 / (1.0 + jnp.exp(-x))


def _silu(x):
    return x * _sigmoid(x)


def _softplus(x):
    return jnp.maximum(x, 0.0) + jnp.log(1.0 + jnp.exp(-jnp.abs(x)))


def _gelu(x):
    return 0.5 * x * (1.0 + jnp.tanh(math.sqrt(2.0 / math.pi) * (x + 0.044715 * x * x * x)))


def _rms(x, g):
    return x * lax.rsqrt(jnp.mean(x * x, axis=-1, keepdims=True) + NORM_EPS) * g


def _const_spec(shape):
    nd = len(shape)
    return pl.BlockSpec(shape, lambda *_: (0,) * nd, pipeline_mode=pl.Buffered(1))


def _acc_spec(shape):
    nd = len(shape)
    return pl.BlockSpec(shape, lambda *_: (0,) * nd)


def _params(sem):
    return pltpu.CompilerParams(dimension_semantics=(sem,), vmem_limit_bytes=VMEM_LIMIT)


_ANY = pl.BlockSpec(memory_space=pl.ANY)


def _sds(shape):
    return jax.ShapeDtypeStruct(shape, F32)


def _head_sum_matrix():
    i = jnp.arange(D_RWKV) // HEAD
    return (i[:, None] == i[None, :]).astype(F32)


def _s5_param_fn(lam_re, lam_im, logdt, b_re, b_im):
    dt = jnp.exp(logdt)
    mag = jnp.exp(lam_re * dt)
    ang = lam_im * dt
    lbr = mag * jnp.cos(ang)
    lbi = mag * jnp.sin(ang)
    nr = lbr - 1.0
    den = lam_re * lam_re + lam_im * lam_im
    cr = (nr * lam_re + lbi * lam_im) / den
    ci = (lbi * lam_re - nr * lam_im) / den
    return lbr, lbi, cr * b_re - ci * b_im, cr * b_im + ci * b_re


def _cmul(ar, ai, br, bi):
    return ar * br - ai * bi, ar * bi + ai * br


def _s5_param_fwd(lam_re, lam_im, logdt, b_re, b_im):
    def body(lr, li, ld, br, bi, o_br, o_bi, o_pr, o_pi):
        lbr, lbi, bbr, bbi = _s5_param_fn(lr[...], li[...], ld[...], br[...], bi[...])
        o_br[...] = bbr
        o_bi[...] = bbi
        rid = lax.broadcasted_iota(jnp.int32, (8, N_STATE), 0)
        pr, pi_ = lbr, lbi
        acc_r = jnp.broadcast_to(pr, (8, N_STATE))
        acc_i = jnp.broadcast_to(pi_, (8, N_STATE))
        for j in range(1, 8):
            pr, pi_ = _cmul(pr, pi_, lbr, lbi)
            acc_r = jnp.where(rid == j, jnp.broadcast_to(pr, (8, N_STATE)), acc_r)
            acc_i = jnp.where(rid == j, jnp.broadcast_to(pi_, (8, N_STATE)), acc_i)
        o_pr[...] = acc_r
        o_pi[...] = acc_i

    return pl.pallas_call(
        body, name="s5_param_fwd",
        out_shape=[_sds((S5_GROUP, N_STATE))] * 2 + [_sds((8, N_STATE))] * 2,
    )(lam_re, lam_im, logdt, b_re, b_im)


def _s5_param_bwd(lam_re, lam_im, logdt, b_re, b_im, d_lbr, d_lbi, d_bbr, d_bbi, group_ind):
    def body(lr, li, ld, br, bi, g0, g1, g2, g3, ind, o_lr, o_li, o_ld, o_br, o_bi):
        _, vjp = jax.vjp(_s5_param_fn, lr[...], li[...], ld[...], br[...], bi[...])
        d_lr, d_li, d_ld, d_br, d_bi = vjp((g0[...], g1[...], g2[...], g3[...]))
        o_lr[...] = d_lr
        o_li[...] = d_li
        o_ld[...] = _dot(jnp.broadcast_to(d_ld, (8, N_STATE)), ind[...], ((1,), (0,)), HI)
        o_br[...] = d_br
        o_bi[...] = d_bi

    return pl.pallas_call(
        body, name="s5_param_bwd",
        out_shape=[_sds((1, N_STATE))] * 2 + [_sds((8, LANES))] + [_sds((S5_GROUP, N_STATE))] * 2,
    )(lam_re, lam_im, logdt, b_re, b_im, d_lbr, d_lbi, d_bbr, d_bbi, group_ind)


def _fwd_in(x, norm_g, w_in_bf, tt):
    L = x.shape[0]

    def body(x_ref, g_ref, w_ref, u_ref, zs_ref, rw_ref, zr_ref):
        h = _rms(x_ref[...], g_ref[...])
        proj = jnp.dot(h.astype(BF16), w_ref[...], preferred_element_type=F32)
        u_ref[...] = proj[:, 0:D_S5]
        zs_ref[...] = proj[:, D_S5:2 * D_S5]
        rw_ref[...] = proj[:, 2 * D_S5:2 * D_S5 + D_SHIFT]
        zr_ref[...] = proj[:, 2 * D_S5 + D_SHIFT:D_IN]

    row = lambda n: pl.BlockSpec((tt, n), lambda i: (i, 0))
    return pl.pallas_call(
        body, name="fwd_in", grid=(L // tt,),
        in_specs=[row(D_MODEL), _const_spec((1, D_MODEL)), _const_spec((D_MODEL, D_IN))],
        out_specs=[row(D_S5), row(D_S5), row(D_SHIFT), row(D_RWKV)],
        out_shape=[_sds((L, D_S5)), _sds((L, D_S5)), _sds((L, D_SHIFT)), _sds((L, D_RWKV))],
        compiler_params=_params("parallel"),
    )(x, norm_g, w_in_bf)


S5_LANE_CHUNK = 512


def _tile_scan(re_ref, im_ref, pow_r_ref, pow_i_ref, carry_r_ref, carry_i_ref, reverse):
    t, n = re_ref.shape
    n_groups = t // 8
    ch = S5_LANE_CHUNK
    rid = lax.broadcasted_iota(jnp.int32, (8, ch), 0)
    for c in range(n // ch):
        cols = slice(c * ch, (c + 1) * ch)
        pow_r = pow_r_ref[:, cols]
        pow_i = pow_i_ref[:, cols]
        row = lambda tile, j: jnp.broadcast_to(tile[j:j + 1], (8, ch))
        levels = [(d, row(pow_r, 8 - d if reverse else d - 1), row(pow_i, 8 - d if reverse else d - 1))
                  for d in (1, 2, 4)]

        def group(g, carry):
            r0 = pl.multiple_of(((n_groups - 1 - g) if reverse else g) * 8, 8)
            xr = re_ref[pl.ds(r0, 8), cols]
            xi = im_ref[pl.ds(r0, 8), cols]
            for d, lr, li in levels:
                keep = (rid < 8 - d) if reverse else (rid >= d)
                shift = (8 - d) if reverse else d
                sr = jnp.where(keep, pltpu.roll(xr, shift, axis=0), 0.0)
                si = jnp.where(keep, pltpu.roll(xi, shift, axis=0), 0.0)
                mr, mi = _cmul(lr, li, sr, si)
                xr = xr + mr
                xi = xi + mi
            mr, mi = _cmul(pow_r, pow_i, carry[0], carry[1])
            xr = xr + mr
            xi = xi + mi
            re_ref[pl.ds(r0, 8), cols] = xr
            im_ref[pl.ds(r0, 8), cols] = xi
            last = 0 if reverse else 7
            return row(xr, last), row(xi, last)

        out = lax.fori_loop(0, n_groups, group, (carry_r_ref[:, cols], carry_i_ref[:, cols]))
        carry_r_ref[:, cols] = out[0]
        carry_i_ref[:, cols] = out[1]


def _s5_fwd(u, b4_re, b4_im, c4_re, c4_im, pow_r, pow_i, tt):
    L = u.shape[0]

    def body(u_ref, bre_ref, bim_ref, cre_ref, cim_ref, pr_ref, pi_ref, sre_o, sim_o, y_o, car_r, car_i):
        @pl.when(pl.program_id(0) == 0)
        def _():
            car_r[...] = jnp.zeros_like(car_r)
            car_i[...] = jnp.zeros_like(car_i)

        uv = u_ref[...]
        for q in range(S5_BLOCKS):
            uq = uv[:, q * LANES:(q + 1) * LANES]
            cols = slice(q * 512, (q + 1) * 512)
            sre_o[:, cols] = _dot(uq, bre_ref[q], ((1,), (0,)), HI)
            sim_o[:, cols] = _dot(uq, bim_ref[q], ((1,), (0,)), HI)
        _tile_scan(sre_o, sim_o, pr_ref, pi_ref, car_r, car_i, reverse=False)
        for q in range(S5_BLOCKS):
            cols = slice(q * 512, (q + 1) * 512)
            y_o[:, q * LANES:(q + 1) * LANES] = (_dot(sre_o[:, cols], cre_ref[q], ((1,), (0,)), HI)
                                                 - _dot(sim_o[:, cols], cim_ref[q], ((1,), (0,)), HI))

    row = lambda n: pl.BlockSpec((tt, n), lambda i: (i, 0))
    return pl.pallas_call(
        body, name="s5_fwd", grid=(L // tt,),
        in_specs=[row(D_S5)] + [_const_spec((S5_BLOCKS, LANES, 512))] * 2 + [_const_spec((S5_BLOCKS, 512, LANES))] * 2
        + [_const_spec((8, N_STATE))] * 2,
        out_specs=[row(N_STATE), row(N_STATE), row(D_S5)],
        out_shape=[_sds((L, N_STATE)), _sds((L, N_STATE)), _sds((L, D_S5))],
        scratch_shapes=[pltpu.VMEM((8, N_STATE), F32)] * 2,
        compiler_params=_params("arbitrary"),
    )(u, b4_re, b4_im, c4_re, c4_im, pow_r, pow_i)


def _rwkv_pre_fn(r, k, v, wa, w0, w2p, a0, a2p, k_k, k_a, ee):
    w = -_softplus(-(w0 + mm_hi(jnp.tanh(wa), w2p))) - 0.5
    logw = -jnp.exp(w)
    a = _sigmoid(a0 + mm_hi(wa, a2p))
    kkp = k * k_k
    kk = kkp / jnp.maximum(jnp.sqrt(mm_hi(kkp * kkp, ee)), 1e-12)
    k2 = k * (1.0 + (a - 1.0) * k_a)
    return r, logw, k2, v, -kk, kk * a


def _head_spec(tt):
    return pl.BlockSpec((N_HEADS, tt, HEAD), lambda i: (0, i, 0))


def _load_heads(ref):
    return jnp.concatenate([ref[h] for h in range(N_HEADS)], axis=-1)


def _store_heads(ref, val):
    for h in range(N_HEADS):
        ref[h] = val[:, h * HEAD:(h + 1) * HEAD]


def _shifted(rw, prev_blk, first):
    rolled = pltpu.roll(rw, 1, axis=0)
    prev_row = jnp.where(first, 0.0, prev_blk[7:8, :])
    rid = lax.broadcasted_iota(jnp.int32, rw.shape, 0)
    return jnp.where(rid == 0, jnp.broadcast_to(prev_row, rw.shape), rolled)


def _split_rw(t):
    return t[:, 0:512], t[:, 512:1024], t[:, 1024:1536], t[:, 1536:1664]


def _rwkv_pre_specs(tt):
    row = pl.BlockSpec((tt, D_SHIFT), lambda i: (i, 0))
    prev = pl.BlockSpec((8, D_SHIFT), lambda i: (jnp.maximum(i * (tt // 8) - 1, 0), 0))
    consts = [_const_spec((1, D_SHIFT)), _const_spec((1, D_RWKV)), _const_spec((LANES, D_RWKV)),
              _const_spec((1, D_RWKV)), _const_spec((LANES, D_RWKV)), _const_spec((1, D_RWKV)),
              _const_spec((1, D_RWKV)), _const_spec((D_RWKV, D_RWKV))]
    return [row, prev] + consts


def _rwkv_pre_fwd(rw, mu, w0, w2p, a0, a2p, k_k, k_a, ee, tt):
    L = rw.shape[0]

    def body(rw_ref, prev_ref, mu_ref, w0_ref, w2_ref, a0_ref, a2_ref, kk_ref, ka_ref, ee_ref, *outs):
        rwv = rw_ref[...]
        rws = rwv + (_shifted(rwv, prev_ref[...], pl.program_id(0) == 0) - rwv) * mu_ref[...]
        res = _rwkv_pre_fn(*_split_rw(rws), w0_ref[...], w2_ref[...], a0_ref[...], a2_ref[...],
                           kk_ref[...], ka_ref[...], ee_ref[...])
        for o, val in zip(outs, res):
            _store_heads(o, val)

    return pl.pallas_call(
        body, name="rwkv_pre_fwd", grid=(L // tt,),
        in_specs=_rwkv_pre_specs(tt), out_specs=[_head_spec(tt)] * 6, out_shape=[_sds((N_HEADS, L, HEAD))] * 6,
        compiler_params=_params("parallel"),
    )(rw, rw, mu, w0, w2p, a0, a2p, k_k, k_a, ee)


def _rwkv_pre_bwd(rw, mu, w0, w2p, a0, a2p, k_k, k_a, ee, cots, tt):
    L = rw.shape[0]
    n_t = L // tt

    def body(rw_ref, prev_ref, mu_ref, w0_ref, w2_ref, a0_ref, a2_ref, kk_ref, ka_ref, ee_ref,
             c_r, c_w, c_k, c_v, c_a, c_b, cb_r, cb_k, cb_v,
             drws_ref, dmu_o, dw0_o, dw2_o, da0_o, da2_o, dkk_o, dka_o,
             dmu, dw0, dw2, da0, da2, dkk, dka):
        i = pl.program_id(0)
        accs = (dmu, dw0, dw2, da0, da2, dkk, dka)

        @pl.when(i == 0)
        def _():
            for acc in accs:
                acc[...] = jnp.zeros_like(acc)

        rwv = rw_ref[...]
        diff = _shifted(rwv, prev_ref[...], i == 0) - rwv
        rws = rwv + diff * mu_ref[...]
        consts = (w0_ref[...], w2_ref[...], a0_ref[...], a2_ref[...], kk_ref[...], ka_ref[...])
        _, vjp = jax.vjp(lambda *a: _rwkv_pre_fn(*a, ee_ref[...]), *_split_rw(rws), *consts)
        scan = [_load_heads(c) for c in (c_r, c_w, c_k, c_v, c_a, c_b)]
        g = vjp((scan[0] + cb_r[...], scan[1], scan[2] + cb_k[...], scan[3] + cb_v[...], scan[4], scan[5]))
        drws = jnp.concatenate(g[0:4], axis=-1)
        drws_ref[...] = drws
        dmu[...] += jnp.sum(drws * diff, axis=0, keepdims=True)
        for acc, val in zip(accs[1:], g[4:]):
            acc[...] += val

        @pl.when(i == n_t - 1)
        def _():
            for acc, out in zip(accs, (dmu_o, dw0_o, dw2_o, da0_o, da2_o, dkk_o, dka_o)):
                out[...] = acc[...]

    row = pl.BlockSpec((tt, D_RWKV), lambda i: (i, 0))
    shapes = [(1, D_SHIFT), (1, D_RWKV), (LANES, D_RWKV), (1, D_RWKV), (LANES, D_RWKV), (1, D_RWKV), (1, D_RWKV)]
    return pl.pallas_call(
        body, name="rwkv_pre_bwd", grid=(n_t,),
        in_specs=_rwkv_pre_specs(tt) + [_head_spec(tt)] * 6 + [row] * 3,
        out_specs=[pl.BlockSpec((tt, D_SHIFT), lambda i: (i, 0))] + [_acc_spec(s) for s in shapes],
        out_shape=[_sds((L, D_SHIFT))] + [_sds(s) for s in shapes],
        scratch_shapes=[pltpu.VMEM(s, F32) for s in shapes],
        compiler_params=_params("arbitrary"),
    )(rw, rw, mu, w0, w2p, a0, a2p, k_k, k_a, ee, *cots)


def _bmm(a, b):
    return lax.dot_general(a, b, (((2,), (1,)), ((0,), (0,))), precision=HI, preferred_element_type=F32)


def _bmm_nt(a, b):
    return lax.dot_general(a, b, (((2,), (2,)), ((0,), (0,))), precision=HI, preferred_element_type=F32)


def _bmm_tn(a, b):
    return lax.dot_general(a, b, (((1,), (1,)), ((0,), (0,))), precision=HI, preferred_element_type=F32)


def _unit_lower_inverse(a):
    t = a.shape[-1]
    ti = lax.broadcasted_iota(jnp.int32, (t, t), 0)
    si = lax.broadcasted_iota(jnp.int32, (t, t), 1)
    inv = jnp.where(ti == si, 1.0, 0.0)[None] + a
    pw = _bmm(a, a)
    n = 2
    while n < t:
        both = _bmm(jnp.concatenate([inv, pw], axis=1), pw)
        inv = inv + both[:, :t]
        pw = both[:, t:]
        n *= 2
    return inv


@jax.custom_vjp
def _solve_unit_lower(a, rhs, inv):
    return _bmm(inv, rhs)


def _solve_fwd(a, rhs, inv):
    u = _bmm(inv, rhs)
    return u, (inv, u)


def _solve_bwd(res, du):
    inv, u = res
    d_rhs = _bmm_tn(inv, du)
    return _bmm_nt(d_rhs, u), d_rhs, jnp.zeros_like(inv)


_solve_unit_lower.defvjp(_solve_fwd, _solve_bwd)


def _rwkv_chunk(st0, r, logw, k, v, a, b, inv=None):
    n_h, t, _ = r.shape
    ti = lax.broadcasted_iota(jnp.int32, (t, t), 0)
    si = lax.broadcasted_iota(jnp.int32, (t, t), 1)
    ones_tri = jnp.broadcast_to(jnp.where(ti >= si, 1.0, 0.0)[None], (n_h, t, t))
    log_p = _bmm(ones_tri, logw)
    p_in = jnp.exp(log_p)
    p_inv = jnp.exp(-log_p)
    at = a * jnp.exp(log_p - logw)
    rt = r * p_in
    bk = jnp.concatenate([b * p_inv, k * p_inv], axis=1)
    ri = lax.broadcasted_iota(jnp.int32, (2 * t, 2 * t), 0)
    ci = lax.broadcasted_iota(jnp.int32, (2 * t, 2 * t), 1)
    top_rows = ri < t
    diff = jnp.where(top_rows, ri, ri - t) - jnp.where(ci < t, ci, ci - t)
    mask = (diff >= jnp.where(top_rows, 1, 0))[None]
    m = jnp.where(mask, _bmm_nt(jnp.concatenate([at, rt], axis=1), bk), 0.0)
    top, bottom = m[:, :t], m[:, t:]
    a_ab = top[:, :, :t]
    if inv is None:
        inv = _unit_lower_inverse(a_ab)
    rhs = _bmm(jnp.concatenate([at, top[:, :, t:]], axis=2), jnp.concatenate([st0, v], axis=1))
    u = _solve_unit_lower(a_ab, rhs, inv)
    y = _bmm(jnp.concatenate([rt, bottom], axis=2), jnp.concatenate([st0, u, v], axis=1))
    p_end = jnp.swapaxes(p_in[:, t - 1:t, :], 1, 2)
    st1 = (st0 + _bmm_tn(bk, jnp.concatenate([u, v], axis=1))) * p_end
    return y, st1, inv


def _rwkv_scan_fwd(ops):
    n_h, L, n = ops[0].shape
    t = RWKV_CHUNK
    n_c = L // t

    def body(r_ref, w_ref, k_ref, v_ref, a_ref, b_ref, y_ref, st_ref, inv_ref, st):
        @pl.when(pl.program_id(0) == 0)
        def _():
            st[...] = jnp.zeros_like(st)

        st0 = st[...]
        st_ref[0] = st0
        y, st1, inv = _rwkv_chunk(st0, r_ref[...], w_ref[...], k_ref[...], v_ref[...], a_ref[...], b_ref[...])
        y_ref[...] = y
        inv_ref[0] = inv
        st[...] = st1

    blk = pl.BlockSpec((n_h, t, n), lambda c: (0, c, 0))
    return pl.pallas_call(
        body, name="rwkv_scan_fwd", grid=(n_c,), in_specs=[blk] * 6,
        out_specs=[blk, pl.BlockSpec((1, n_h, n, n), lambda c: (c, 0, 0, 0)),
                   pl.BlockSpec((1, n_h, t, t), lambda c: (c, 0, 0, 0))],
        out_shape=[_sds((n_h, L, n)), _sds((n_c, n_h, n, n)), _sds((n_c, n_h, t, t))],
        scratch_shapes=[pltpu.VMEM((n_h, n, n), F32)],
        compiler_params=_params("arbitrary"),
    )(*ops)


def _rwkv_scan_bwd(ops, states, invs, dy):
    n_h, L, n = ops[0].shape
    t = RWKV_CHUNK
    n_c = L // t

    def body(r_ref, w_ref, k_ref, v_ref, a_ref, b_ref, st_ref, inv_ref, dy_ref, dr, dw, dk, dv, da, db, dst):
        @pl.when(pl.program_id(0) == 0)
        def _():
            dst[...] = jnp.zeros_like(dst)

        inv = inv_ref[0]
        _, vjp = jax.vjp(lambda *a: _rwkv_chunk(*a, inv=inv)[:2], st_ref[0], r_ref[...], w_ref[...], k_ref[...],
                         v_ref[...], a_ref[...], b_ref[...])
        g = vjp((dy_ref[...], dst[...]))
        dst[...] = g[0]
        for out, val in zip((dr, dw, dk, dv, da, db), g[1:]):
            out[...] = val

    blk = pl.BlockSpec((n_h, t, n), lambda c: (0, n_c - 1 - c, 0))
    per_chunk = lambda m: pl.BlockSpec((1, n_h, m, m), lambda c: (n_c - 1 - c, 0, 0, 0))
    return pl.pallas_call(
        body, name="rwkv_scan_bwd", grid=(n_c,),
        in_specs=[blk] * 6 + [per_chunk(n), per_chunk(t), blk],
        out_specs=[blk] * 6, out_shape=[_sds((n_h, L, n))] * 6,
        scratch_shapes=[pltpu.VMEM((n_h, n, n), F32)],
        compiler_params=_params("arbitrary"),
    )(*ops, states, invs, dy)


def _post_fn(x, u, zs, zr, ysc, r, k2, v, y_ssm, d, glu_w, glu_b, ln_w, ln_b, r_k,
             wo_s5, wo_rwkv, gf, tgt, ee):
    y3 = _gelu(y_ssm + d * u)
    y_s5 = y3 * _sigmoid(mm_bf(y3, glu_w) + glu_b) * _silu(zs)
    mean = mm_hi(ysc, ee) * (1.0 / HEAD)
    yc = ysc - mean
    var = mm_hi(yc * yc, ee) * (1.0 / HEAD)
    gn = yc * lax.rsqrt(var + GN_EPS) * ln_w + ln_b
    bonus = mm_hi(r * k2 * r_k, ee) * v
    y_rwkv = (gn + bonus) * _silu(zr)
    x2 = x + mm_bf(y_s5, wo_s5) + mm_bf(y_rwkv, wo_rwkv)
    err = _rms(x2, gf) - tgt
    return 0.5 * jnp.mean(err * err, axis=-1, keepdims=True)


def _post(x, u, zs, zr, ysc, r, k2, v, y_ssm, d, glu_w, glu_b, ln_w, ln_b, r_k, w_out, gf, tgt, ee, tt):
    L = x.shape[0]
    n_t = L // tt
    acc_shapes = [(1, D_S5), (D_S5, D_S5), (1, D_S5), (1, D_RWKV), (1, D_RWKV), (1, D_RWKV),
                  (D_MODEL, D_MODEL), (1, D_MODEL), (8, LANES)]

    def body(x_ref, u_ref, zs_ref, zr_ref, ysc_ref, r_ref, k2_ref, v_ref, yssm_ref,
             d_ref, gw_ref, gb_ref, lw_ref, lb_ref, rk_ref, wo_ref, gf_ref, tgt_ref, ee_ref,
             dx_o, du_o, dzs_o, dzr_o, dysc_o, dr_o, dk2_o, dv_o, dyssm_o,
             dd_o, dgw_o, dgb_o, dlw_o, dlb_o, drk_o, dwo_o, dgf_o, loss_o,
             dd, dgw, dgb, dlw, dlb, drk, dwo, dgf, loss):
        i = pl.program_id(0)
        accs = (dd, dgw, dgb, dlw, dlb, drk, dwo, dgf, loss)

        @pl.when(i == 0)
        def _():
            for acc in accs:
                acc[...] = jnp.zeros_like(acc)

        args = (x_ref[...], u_ref[...], zs_ref[...], zr_ref[...],
                _load_heads(ysc_ref), _load_heads(r_ref), _load_heads(k2_ref), _load_heads(v_ref), yssm_ref[...],
                d_ref[...], gw_ref[...], gb_ref[...], lw_ref[...], lb_ref[...], rk_ref[...],
                wo_ref[0:D_S5, :], wo_ref[D_S5:D_MODEL, :], gf_ref[...])
        rows, vjp = jax.vjp(lambda *a: _post_fn(*a, tgt_ref[...], ee_ref[...]), *args)
        g = vjp(jnp.ones_like(rows))
        for out, val in zip((dx_o, du_o, dzs_o, dzr_o), g[0:4]):
            out[...] = val
        _store_heads(dysc_o, g[4])
        for out, val in zip((dr_o, dk2_o, dv_o, dyssm_o), g[5:9]):
            out[...] = val
        for acc, val in zip((dd, dgw, dgb, dlw, dlb, drk), g[9:15]):
            acc[...] += val
        dwo[0:D_S5, :] += g[15]
        dwo[D_S5:D_MODEL, :] += g[16]
        dgf[...] += g[17]
        loss[...] += jnp.broadcast_to(jnp.sum(rows, axis=0, keepdims=True), loss.shape)

        @pl.when(i == n_t - 1)
        def _():
            for acc, out in zip(accs, (dd_o, dgw_o, dgb_o, dlw_o, dlb_o, drk_o, dwo_o, dgf_o, loss_o)):
                pltpu.sync_copy(acc, out)

    row = lambda n: pl.BlockSpec((tt, n), lambda i: (i, 0))
    in_specs = ([row(D_MODEL)] + [row(512)] * 3 + [_head_spec(tt)] * 4 + [row(D_S5)]
                + [_const_spec(s) for s in [(1, D_S5), (D_S5, D_S5), (1, D_S5), (1, D_RWKV), (1, D_RWKV), (1, D_RWKV),
                                            (D_MODEL, D_MODEL), (1, D_MODEL)]]
                + [row(D_MODEL), _const_spec((D_RWKV, D_RWKV))])
    out_rows = [D_MODEL] + [512] * 3 + [None] + [512] * 4
    return pl.pallas_call(
        body, name="post_fwd_bwd", grid=(n_t,), in_specs=in_specs,
        out_specs=[row(n) if n else _head_spec(tt) for n in out_rows] + [_ANY] * len(acc_shapes),
        out_shape=([_sds((L, n)) if n else _sds((N_HEADS, L, HEAD)) for n in out_rows]
                   + [_sds(s) for s in acc_shapes]),
        scratch_shapes=[pltpu.VMEM(s, F32) for s in acc_shapes],
        compiler_params=_params("arbitrary"),
    )(x, u, zs, zr, ysc, r, k2, v, y_ssm, d, glu_w, glu_b, ln_w, ln_b, r_k, w_out, gf, tgt, ee)


def _s5_bwd(u, du_direct, dy, s_re, s_im, b4_re, b4_im, c4_re, c4_im, pow_r, pow_i, tt):
    L = u.shape[0]
    n_t = L // tt
    acc_shapes = ([(S5_BLOCKS, LANES, 512)] * 2 + [(S5_BLOCKS, 512, LANES)] * 2 + [(1, N_STATE)] * 2)

    def body(u_ref, dud_ref, dy_ref, sre_ref, sim_ref, pre_ref, pim_ref, bre_ref, bim_ref, cre_ref, cim_ref,
             pr_ref, pi_ref, du_o, dbre_o, dbim_o, dcre_o, dcim_o, dlr_o, dli_o,
             dbre, dbim, dcre, dcim, dlr, dli, gre, gim, car_r, car_i):
        i = pl.program_id(0)

        @pl.when(i == 0)
        def _():
            for acc in (dbre, dbim, dcre, dcim, dlr, dli, car_r, car_i):
                acc[...] = jnp.zeros_like(acc)

        uv = u_ref[...]
        dyv = dy_ref[...]
        blocks = [slice(q * 512, (q + 1) * 512) for q in range(S5_BLOCKS)]
        lanes = [slice(q * LANES, (q + 1) * LANES) for q in range(S5_BLOCKS)]
        for q in range(S5_BLOCKS):
            gre[:, blocks[q]] = _dot(dyv[:, lanes[q]], cre_ref[q], ((1,), (1,)), HI)
            gim[:, blocks[q]] = -_dot(dyv[:, lanes[q]], cim_ref[q], ((1,), (1,)), HI)
        _tile_scan(gre, gim, pr_ref, pi_ref, car_r, car_i, reverse=True)
        for q in range(S5_BLOCKS):
            gr = gre[:, blocks[q]]
            gi = gim[:, blocks[q]]
            sr = sre_ref[:, blocks[q]]
            si = sim_ref[:, blocks[q]]
            du_o[:, lanes[q]] = (dud_ref[:, lanes[q]] + _dot(gr, bre_ref[q], ((1,), (1,)), HI)
                                 + _dot(gi, bim_ref[q], ((1,), (1,)), HI))
            dbre[q] += _dot(uv[:, lanes[q]], gr, ((0,), (0,)), HI)
            dbim[q] += _dot(uv[:, lanes[q]], gi, ((0,), (0,)), HI)
            dcre[q] += _dot(sr, dyv[:, lanes[q]], ((0,), (0,)), HI)
            dcim[q] -= _dot(si, dyv[:, lanes[q]], ((0,), (0,)), HI)
            rid = lax.broadcasted_iota(jnp.int32, sr.shape, 0)
            first = i == n_t - 1
            prev_r = jnp.where(first, 0.0, pre_ref[7:8, blocks[q]])
            prev_i = jnp.where(first, 0.0, pim_ref[7:8, blocks[q]])
            pr = jnp.where(rid == 0, jnp.broadcast_to(prev_r, sr.shape), pltpu.roll(sr, 1, axis=0))
            pi_ = jnp.where(rid == 0, jnp.broadcast_to(prev_i, si.shape), pltpu.roll(si, 1, axis=0))
            dlr[:, blocks[q]] += jnp.sum(pr * gr + pi_ * gi, axis=0, keepdims=True)
            dli[:, blocks[q]] += jnp.sum(pr * gi - pi_ * gr, axis=0, keepdims=True)

        @pl.when(i == n_t - 1)
        def _():
            for acc, out in zip((dbre, dbim, dcre, dcim, dlr, dli), (dbre_o, dbim_o, dcre_o, dcim_o, dlr_o, dli_o)):
                out[...] = acc[...]

    row = lambda n: pl.BlockSpec((tt, n), lambda i: (n_t - 1 - i, 0))
    prev = pl.BlockSpec((8, N_STATE), lambda i: (jnp.maximum((n_t - 1 - i) * (tt // 8) - 1, 0), 0))
    return pl.pallas_call(
        body, name="s5_bwd", grid=(n_t,),
        in_specs=[row(D_S5)] * 3 + [row(N_STATE)] * 2 + [prev] * 2
        + [_const_spec((S5_BLOCKS, LANES, 512))] * 2 + [_const_spec((S5_BLOCKS, 512, LANES))] * 2
        + [_const_spec((8, N_STATE))] * 2,
        out_specs=[row(D_S5)] + [_acc_spec(s) for s in acc_shapes],
        out_shape=[_sds((L, D_S5))] + [_sds(s) for s in acc_shapes],
        scratch_shapes=[pltpu.VMEM(s, F32) for s in acc_shapes] + [pltpu.VMEM((tt, N_STATE), F32)] * 2
        + [pltpu.VMEM((8, N_STATE), F32)] * 2,
        compiler_params=_params("arbitrary"),
    )(u, du_direct, dy, s_re, s_im, s_re, s_im, b4_re, b4_im, c4_re, c4_im, pow_r, pow_i)


def _bwd_in(x, norm_g, w_in_bf, mu, dx2, du, dzs, drws, dzr, tt):
    L = x.shape[0]
    n_t = L // tt

    def body(x_ref, g_ref, w_ref, mu_ref, dx2_ref, du_ref, dzs_ref, drws_ref, nxt_ref, dzr_ref,
             gx_o, dproj_o, dg_o, dg):
        i = pl.program_id(0)

        @pl.when(i == 0)
        def _():
            dg[...] = jnp.zeros_like(dg)

        drws_v = drws_ref[...]
        rid = lax.broadcasted_iota(jnp.int32, drws_v.shape, 0)
        nxt_row = jnp.where(i == n_t - 1, 0.0, nxt_ref[0:1, :])
        nxt = jnp.where(rid == tt - 1, jnp.broadcast_to(nxt_row, drws_v.shape), pltpu.roll(drws_v, tt - 1, axis=0))
        muv = mu_ref[...]
        drw = drws_v * (1.0 - muv) + nxt * muv
        dproj_o[:, 0:D_S5] = du_ref[...]
        dproj_o[:, D_S5:2 * D_S5] = dzs_ref[...]
        dproj_o[:, 2 * D_S5:2 * D_S5 + D_SHIFT] = drw
        dproj_o[:, 2 * D_S5 + D_SHIFT:D_IN] = dzr_ref[...]
        dh = _dot(dproj_o[...].astype(BF16), w_ref[...], ((1,), (1,)), None)
        _, vjp = jax.vjp(_rms, x_ref[...], g_ref[...])
        dxh, dgv = vjp(dh)
        gx_o[...] = dx2_ref[...] + dxh
        dg[...] += dgv

        @pl.when(i == n_t - 1)
        def _():
            dg_o[...] = dg[...]

    row = lambda n: pl.BlockSpec((tt, n), lambda i: (i, 0))
    nxt = pl.BlockSpec((8, D_SHIFT), lambda i: (jnp.minimum((i + 1) * (tt // 8), L // 8 - 1), 0))
    return pl.pallas_call(
        body, name="bwd_in", grid=(n_t,),
        in_specs=[row(D_MODEL), _const_spec((1, D_MODEL)), _const_spec((D_MODEL, D_IN)), _const_spec((1, D_SHIFT)),
                  row(D_MODEL), row(D_S5), row(D_S5), row(D_SHIFT), nxt, row(D_RWKV)],
        out_specs=[row(D_MODEL), row(D_IN), _acc_spec((1, D_MODEL))],
        out_shape=[_sds((L, D_MODEL)), _sds((L, D_IN)), _sds((1, D_MODEL))],
        scratch_shapes=[pltpu.VMEM((1, D_MODEL), F32)],
        compiler_params=_params("arbitrary"),
    )(x, norm_g, w_in_bf, mu, dx2, du, dzs, drws, drws, dzr)


def _grad_w_in(x, norm_g, dproj, tt, cn):
    L = x.shape[0]

    def body(x_ref, g_ref, dp_ref, out_ref):
        @pl.when(pl.program_id(1) == 0)
        def _():
            out_ref[...] = jnp.zeros_like(out_ref)

        h = _rms(x_ref[...], g_ref[...])
        out_ref[...] += _dot(h.astype(BF16), dp_ref[...].astype(BF16), ((0,), (0,)), None)

    return pl.pallas_call(
        body, name="grad_w_in", grid=(D_IN // cn, L // tt),
        in_specs=[pl.BlockSpec((tt, D_MODEL), lambda j, i: (i, 0)), pl.BlockSpec((1, D_MODEL), lambda j, i: (0, 0)),
                  pl.BlockSpec((tt, cn), lambda j, i: (i, j))],
        out_specs=pl.BlockSpec((D_MODEL, cn), lambda j, i: (0, j)),
        out_shape=_sds((D_MODEL, D_IN)),
        compiler_params=pltpu.CompilerParams(dimension_semantics=("parallel", "arbitrary"), vmem_limit_bytes=VMEM_LIMIT),
    )(x, norm_g, dproj)


def _block_diag_b(bbar):
    bb = bbar.reshape(S5_GROUP, S5_BLOCKS, 8, S5_STATE)
    return jnp.einsum('hqgp,Gg->qGhgp', bb, jnp.eye(8, dtype=F32)).reshape(S5_BLOCKS, LANES, 512)


def _block_diag_b_t(db4):
    d = db4.reshape(S5_BLOCKS, 8, S5_GROUP, 8, S5_STATE)
    return jnp.einsum('qGhgp,Gg->hqgp', d, jnp.eye(8, dtype=F32)).reshape(S5_GROUP, N_STATE)


def _block_diag_c(c):
    cc = c.reshape(S5_BLOCKS, 8, S5_GROUP, S5_STATE)
    return jnp.einsum('qghp,gG->qgpGh', cc, jnp.eye(8, dtype=F32)).reshape(S5_BLOCKS, 512, LANES)


def _block_diag_c_t(dc4):
    d = dc4.reshape(S5_BLOCKS, 8, S5_STATE, 8, S5_GROUP)
    return jnp.einsum('qgpGh,gG->qghp', d, jnp.eye(8, dtype=F32)).reshape(S5_GROUPS, S5_GROUP, S5_STATE)


def _local_step(x, tgt, w):
    L = x.shape[0]
    tt = min(256, L)
    tp = min(128, L)
    ee = _head_sum_matrix()

    lam_re = w['s5_lam_re'].reshape(1, N_STATE)
    lam_im = w['s5_lam_im'].reshape(1, N_STATE)
    logdt = jnp.repeat(w['s5_log_dt'], S5_STATE).reshape(1, N_STATE)
    b_re_t = w['s5_b_re'].transpose(2, 0, 1).reshape(S5_GROUP, N_STATE)
    b_im_t = w['s5_b_im'].transpose(2, 0, 1).reshape(S5_GROUP, N_STATE)
    bbr, bbi, pow_r, pow_i = _s5_param_fwd(lam_re, lam_im, logdt, b_re_t, b_im_t)
    b4_re, b4_im = _block_diag_b(bbr), _block_diag_b(bbi)
    c4_re, c4_im = _block_diag_c(w['s5_c_re']), _block_diag_c(w['s5_c_im'])

    norm_g = w['norm_g'].reshape(1, D_MODEL)
    w_in_bf = w['w_in'].astype(BF16)
    u, zs, rw, zr = _fwd_in(x, norm_g, w_in_bf, tt)
    s_re, s_im, y_ssm = _s5_fwd(u, b4_re, b4_im, c4_re, c4_im, pow_r, pow_i, tt)

    row = lambda t: t.reshape(1, -1)
    zpad = jnp.zeros((HEAD, D_RWKV), F32)
    w2p = jnp.concatenate([w['rwkv_w2'], zpad], axis=0)
    a2p = jnp.concatenate([zpad, w['rwkv_a2']], axis=0)
    pre_consts = (row(w['rwkv_mu']), row(w['rwkv_w0']), w2p, row(w['rwkv_a0']), a2p,
                  row(w['rwkv_k_k']), row(w['rwkv_k_a']), ee)
    ops = _rwkv_pre_fwd(rw, *pre_consts, tt)
    ysc, states, invs = _rwkv_scan_fwd(ops)

    post = _post(x, u, zs, zr, ysc, ops[0], ops[2], ops[3], y_ssm,
                 row(w['s5_d']), w['s5_glu_w'], row(w['s5_glu_b']), row(w['rwkv_ln_w']), row(w['rwkv_ln_b']),
                 row(w['rwkv_r_k']), w['w_out'], row(w['final_g']), tgt, ee, tp)
    (dx2, du_d, dzs, dzr, dysc, dr_b, dk2_b, dv_b, dy_ssm,
     dd, dglu_w, dglu_b, dln_w, dln_b, dr_k, dw_out, dgf, loss) = post

    du, db4_re, db4_im, dc4_re, dc4_im, dlbr, dlbi = _s5_bwd(
        u, du_d, dy_ssm, s_re, s_im, b4_re, b4_im, c4_re, c4_im, pow_r[::-1], -pow_i[::-1], tt)
    group_ind = (jnp.arange(N_STATE)[:, None] // S5_STATE == jnp.arange(LANES)[None, :]).astype(F32)
    dlam_re, dlam_im, dlogdt, db_re_t, db_im_t = _s5_param_bwd(
        lam_re, lam_im, logdt, b_re_t, b_im_t, dlbr, dlbi, _block_diag_b_t(db4_re), _block_diag_b_t(db4_im), group_ind)

    cots = list(_rwkv_scan_bwd(ops, states, invs, dysc)) + [dr_b, dk2_b, dv_b]
    drws, dmu, dw0, dw2p, da0, da2p, dk_k, dk_a = _rwkv_pre_bwd(rw, *pre_consts, cots, tt)

    grad_x, dproj, dnorm_g = _bwd_in(x, norm_g, w_in_bf, row(w['rwkv_mu']), dx2, du, dzs, drws, dzr, tt)
    dw_in = _grad_w_in(x, norm_g, dproj, tt, 640)

    unb = lambda t: t.reshape(S5_GROUP, S5_GROUPS, S5_STATE).transpose(1, 2, 0)
    grads = {
        'norm_g': dnorm_g.reshape(D_MODEL), 'w_in': dw_in,
        's5_lam_re': dlam_re.reshape(S5_GROUPS, S5_STATE), 's5_lam_im': dlam_im.reshape(S5_GROUPS, S5_STATE),
        's5_log_dt': dlogdt[0, :S5_GROUPS], 's5_b_re': unb(db_re_t), 's5_b_im': unb(db_im_t),
        's5_c_re': _block_diag_c_t(dc4_re), 's5_c_im': _block_diag_c_t(dc4_im),
        's5_d': dd.reshape(D_S5), 's5_glu_w': dglu_w, 's5_glu_b': dglu_b.reshape(D_S5),
        'rwkv_mu': dmu.reshape(-1), 'rwkv_w0': dw0.reshape(-1), 'rwkv_w2': dw2p[:HEAD], 'rwkv_a0': da0.reshape(-1),
        'rwkv_a2': da2p[HEAD:], 'rwkv_k_k': dk_k.reshape(-1), 'rwkv_k_a': dk_a.reshape(-1),
        'rwkv_r_k': dr_k.reshape(N_HEADS, HEAD), 'rwkv_ln_w': dln_w.reshape(-1), 'rwkv_ln_b': dln_b.reshape(-1),
        'w_out': dw_out, 'final_g': dgf.reshape(D_MODEL),
    }
    return loss, grad_x, grads


def _exchange(arrays, gather, axes, name):
    n = len(arrays)
    group = 2 ** len(axes)

    def body(*refs):
        send_refs, recv_refs = refs[:n], refs[n:2 * n]
        send_sems, recv_sems, local_sems = refs[2 * n:]
        pos = {ax: lax.axis_index(ax) for ax in ("x", "y", "c")}

        def index_of(p):
            idx = 0
            for ax in axes:
                idx = 2 * idx + p[ax]
            return idx

        me = index_of(pos)
        own, outs, arrivals = [], [], []
        for i, (send_ref, recv_ref) in enumerate(zip(send_refs, recv_refs)):
            def block_for(dev, send_ref=send_ref, whole=gather[i]):
                return send_ref if whole else send_ref.at[dev]

            own.append(pltpu.make_async_copy(block_for(me), recv_ref.at[me], local_sems.at[i]))
            own[-1].start()
            for k in range(1, group):
                peer = dict(pos)
                for bit, ax in enumerate(axes):
                    if (k >> bit) & 1:
                        peer[ax] = 1 - pos[ax]
                peer_idx = index_of(peer)
                sems = dict(send_sem=send_sems.at[i, k - 1], recv_sem=recv_sems.at[i, k - 1],
                            device_id=(peer["x"], peer["y"], peer["c"]), device_id_type=pl.DeviceIdType.MESH)
                outs.append(pltpu.make_async_remote_copy(src_ref=block_for(peer_idx), dst_ref=recv_ref.at[me], **sems))
                outs[-1].start()
                arrivals.append(
                    pltpu.make_async_remote_copy(src_ref=block_for(peer_idx), dst_ref=recv_ref.at[peer_idx], **sems))
        for copy in arrivals:
            copy.wait_recv()
        for copy in outs:
            copy.wait_send()
        for copy in own:
            copy.wait()

    return pl.pallas_call(
        body, name=name, in_specs=[_ANY] * n, out_specs=[_ANY] * n,
        out_shape=[jax.ShapeDtypeStruct(((group,) + a.shape) if whole else a.shape, a.dtype)
                   for a, whole in zip(arrays, gather)],
        scratch_shapes=[pltpu.SemaphoreType.DMA((n, group - 1)), pltpu.SemaphoreType.DMA((n, group - 1)),
                        pltpu.SemaphoreType.DMA((n,))],
        compiler_params=pltpu.CompilerParams(has_side_effects=True),
    )(*arrays)


def _sum_devices(ref):
    g = ref[0].astype(F32)
    for s in range(1, ref.shape[0]):
        g = g + ref[s].astype(F32)
    return g


def _adamw_math(g, w, m, v):
    m_new = ADAM_B1 * m + (1.0 - ADAM_B1) * g
    v_new = ADAM_B2 * v + (1.0 - ADAM_B2) * (g * g)
    m_hat = m_new / (1.0 - ADAM_B1 ** ADAM_STEP)
    v_hat = v_new / (1.0 - ADAM_B2 ** ADAM_STEP)
    return -ADAM_LR * (m_hat / (jnp.sqrt(v_hat) + ADAM_EPS) + ADAM_WD * w), m_new, v_new


def _adamw(gs, ws, ms, vs, reduce, name):
    n = len(ws)

    def body(*refs):
        g_refs, w_refs, m_refs, v_refs = (refs[j * n:(j + 1) * n] for j in range(4))
        outs = refs[4 * n:]
        for i in range(n):
            g = _sum_devices(g_refs[i]) if reduce else g_refs[i][...]
            res = _adamw_math(g, w_refs[i][...], m_refs[i][...], v_refs[i][...])
            for j, val in enumerate(((g,) if reduce else ()) + res):
                outs[j * n + i][...] = val

    return pl.pallas_call(
        body, name=name, out_shape=[_sds(w.shape) for w in ws] * (4 if reduce else 3),
        compiler_params=pltpu.CompilerParams(vmem_limit_bytes=VMEM_LIMIT),
    )(*gs, *ws, *ms, *vs)


def _sum_blocks(recv):
    def body(recv_ref, out_ref):
        out_ref[...] = _sum_devices(recv_ref)

    return pl.pallas_call(body, name="sum_small_grads", out_shape=_sds(recv.shape[1:]))(recv)


_WEIGHTS = [
    ('norm_g', (1, 1024), False), ('w_in', (1, 1024, 400), True), ('s5_lam_re', (1, 32, 64), False),
    ('s5_lam_im', (1, 32, 64), False), ('s5_log_dt', (1, 32), False), ('s5_b_re', (1, 32, 64, 16), False),
    ('s5_b_im', (1, 32, 64, 16), False), ('s5_c_re', (1, 32, 16, 64), False), ('s5_c_im', (1, 32, 16, 64), False),
    ('s5_d', (1, 512), False), ('s5_glu_w', (1, 64, 512), True), ('s5_glu_b', (1, 512), False),
    ('rwkv_mu', (1, 1664), False), ('rwkv_w0', (1, 512), False), ('rwkv_w2', (1, 64, 64), True),
    ('rwkv_a0', (1, 512), False), ('rwkv_a2', (1, 64, 64), True), ('rwkv_k_k', (1, 512), False),
    ('rwkv_k_a', (1, 512), False), ('rwkv_r_k', (1, 8, 64), False), ('rwkv_ln_w', (1, 512), False),
    ('rwkv_ln_b', (1, 512), False), ('w_out', (1, 128, 1024), True), ('final_g', (1024,), False),
]
_SHARDED = [(n, s) for n, s, sharded in _WEIGHTS if sharded]
_SMALL = [(n, s) for n, s, sharded in _WEIGHTS if not sharded]
_COLUMN_SHARDED = ('w_in', 'rwkv_w2', 'rwkv_a2')
_SMALL_ROWS = -(-sum(math.prod(s) for _, s in _SMALL) // (8 * LANES)) * 8


def _pack_small(grads):
    flat = [grads[n].reshape(-1) for n, _ in _SMALL]
    pad = _SMALL_ROWS * LANES - sum(f.size for f in flat)
    return jnp.concatenate(flat + [jnp.zeros((pad,), F32)]).reshape(_SMALL_ROWS, LANES)


def _unpack_small(packed):
    flat = packed.reshape(-1)
    out, off = {}, 0
    for n, s in _SMALL:
        size = math.prod(s)
        out[n] = flat[off:off + size].reshape(s)
        off += size
    return out


_BF16_OPERANDS = ('w_in', 's5_glu_w', 'w_out')


def _join_shards(name, blocks):
    _, rows, cols = blocks.shape
    if name in _COLUMN_SHARDED:
        return blocks.transpose(1, 0, 2).reshape(rows, N_DEV * cols)
    return blocks.reshape(N_DEV * rows, cols)


def _split_shards(name, full, shard_shape):
    rows, cols = shard_shape
    if name in _COLUMN_SHARDED:
        return full.reshape(rows, N_DEV, cols).transpose(1, 0, 2)
    return full.reshape(N_DEV, rows, cols)


def kernel(x, norm_g, w_in, s5_lam_re, s5_lam_im, s5_log_dt, s5_b_re, s5_b_im, s5_c_re, s5_c_im, s5_d, s5_glu_w, s5_glu_b, rwkv_mu, rwkv_w0, rwkv_w2, rwkv_a0, rwkv_a2, rwkv_k_k, rwkv_k_a, rwkv_r_k, rwkv_ln_w, rwkv_ln_b, w_out, final_g, loss_target, m_norm_g, m_w_in, m_s5_lam_re, m_s5_lam_im, m_s5_log_dt, m_s5_b_re, m_s5_b_im, m_s5_c_re, m_s5_c_im, m_s5_d, m_s5_glu_w, m_s5_glu_b, m_rwkv_mu, m_rwkv_w0, m_rwkv_w2, m_rwkv_a0, m_rwkv_a2, m_rwkv_k_k, m_rwkv_k_a, m_rwkv_r_k, m_rwkv_ln_w, m_rwkv_ln_b, m_w_out, m_final_g, v_norm_g, v_w_in, v_s5_lam_re, v_s5_lam_im, v_s5_log_dt, v_s5_b_re, v_s5_b_im, v_s5_c_re, v_s5_c_im, v_s5_d, v_s5_glu_w, v_s5_glu_b, v_rwkv_mu, v_rwkv_w0, v_rwkv_w2, v_rwkv_a0, v_rwkv_a2, v_rwkv_k_k, v_rwkv_k_a, v_rwkv_r_k, v_rwkv_ln_w, v_rwkv_ln_b, v_w_out, v_final_g):
    given = dict(locals())

    n_sh = len(_SHARDED)
    everyone = ("x", "y", "c")
    shards = [given[n][0].astype(BF16 if n in _BF16_OPERANDS else F32) for n, _ in _SHARDED]
    gathered = _exchange(shards, (True,) * n_sh, everyone, "gather_weights")
    local = {n: _join_shards(n, blocks).astype(F32 if n != 'w_in' else BF16)
             for (n, _), blocks in zip(_SHARDED, gathered)}
    local.update({n: (given[n][0] if len(s) > 1 else given[n]) for n, s in _SMALL})

    loss, grad_x, grads = _local_step(x[0], loss_target[0], local)

    blocks = [_split_shards(n, grads[n], s[1:]).astype(BF16) for n, s in _SHARDED]
    recv = _exchange(blocks + [_pack_small(grads)], (False,) * n_sh + (True,), everyone, "exchange_grads")

    result = {}
    for group, name in (([0], "adamw_w_in"), ([1, 2, 3, 4], "adamw_shards")):
        ns = [_SHARDED[i][0] for i in group]
        res = _adamw([recv[i] for i in group], [given[n][0] for n in ns], [given['m_' + n][0] for n in ns],
                     [given['v_' + n][0] for n in ns], True, name)
        for j, n in enumerate(ns):
            result[n] = [res[k * len(ns) + j][None] for k in range(4)]
    g_small = _unpack_small(_sum_blocks(recv[-1]))
    two_d = lambda t: t.reshape(1, -1) if t.ndim == 1 else t
    ns = [n for n, _ in _SMALL]
    res = _adamw([two_d(g_small[n]) for n in ns], [two_d(given[n]) for n in ns], [two_d(given['m_' + n]) for n in ns],
                 [two_d(given['v_' + n]) for n in ns], False, "adamw_small")
    for j, (n, s) in enumerate(_SMALL):
        result[n] = [g_small[n]] + [res[k * len(ns) + j].reshape(s) for k in range(3)]

    total = lax.psum(loss[0, 0], ("x", "y", "c"))
    outs = [total, grad_x[None]]
    for k in range(4):
        outs += [result[n][k] for n, _, _ in _WEIGHTS]
    return tuple(outs)
```

```python
import functools
import math

import jax
import jax.numpy as jnp
from jax import lax
from jax.experimental import pallas as pl
from jax.experimental.pallas import tpu as pltpu

F32 = jnp.float32
BF16 = jnp.bfloat16
HI = lax.Precision.HIGH

D_MODEL = 1024
D_S5 = 512
D_RWKV = 512
S5_GROUPS = 32
S5_GROUP = 16
S5_STATE = 64
N_STATE = S5_GROUPS * S5_STATE
N_HEADS = 8
HEAD = 64
D_SHIFT = 3 * D_RWKV + 128
D_IN = 2 * D_S5 + D_SHIFT + D_RWKV
NORM_EPS = 1e-6
GN_EPS = 64e-5
N_DEV = 8
LANES = 128
S5_BLOCKS = 4
RWKV_CHUNK = 64
VMEM_LIMIT = 56 * 1024 * 1024

ADAM_LR = 0.001
ADAM_B1 = 0.9
ADAM_B2 = 0.999
ADAM_EPS = 1e-08
ADAM_WD = 0.01
ADAM_STEP = 10


def _dot(a, b, dims, prec):
    return lax.dot_general(a, b, (dims, ((), ())), precision=prec, preferred_element_type=F32)


def _dot_bf(a, b, dims):
    return _dot(a.astype(BF16), b.astype(BF16), dims, None)


def _make_mm(cast, prec):
    @jax.custom_vjp
    def mm(a, b):
        return _dot(cast(a), cast(b), ((1,), (0,)), prec)

    def fwd(a, b):
        return mm(a, b), (a, b)

    def bwd(res, g):
        a, b = res
        return (_dot(cast(g), cast(b), ((1,), (1,)), prec), _dot(cast(a), cast(g), ((0,), (0,)), prec))

    mm.defvjp(fwd, bwd)
    return mm


mm_hi = _make_mm(lambda t: t, HI)
mm_bf = _make_mm(lambda t: t.astype(BF16), None)


def _sigmoid(x):
    return 1.0 / (1.0 + jnp.exp(-x))


def _silu(x):
    return x * _sigmoid(x)


def _softplus(x):
    return jnp.maximum(x, 0.0) + jnp.log(1.0 + jnp.exp(-jnp.abs(x)))


def _gelu(x):
    return 0.5 * x * (1.0 + jnp.tanh(math.sqrt(2.0 / math.pi) * (x + 0.044715 * x * x * x)))


def _rms(x, g):
    return x * lax.rsqrt(jnp.mean(x * x, axis=-1, keepdims=True) + NORM_EPS) * g


def _const_spec(shape):
    nd = len(shape)
    return pl.BlockSpec(shape, lambda *_: (0,) * nd, pipeline_mode=pl.Buffered(1))


def _acc_spec(shape):
    nd = len(shape)
    return pl.BlockSpec(shape, lambda *_: (0,) * nd)


def _params(sem):
    return pltpu.CompilerParams(dimension_semantics=(sem,), vmem_limit_bytes=VMEM_LIMIT)


_ANY = pl.BlockSpec(memory_space=pl.ANY)


def _sds(shape):
    return jax.ShapeDtypeStruct(shape, F32)


def _head_sum_matrix():
    i = jnp.arange(D_RWKV) // HEAD
    return (i[:, None] == i[None, :]).astype(F32)


def _s5_param_fn(lam_re, lam_im, logdt, b_re, b_im):
    dt = jnp.exp(logdt)
    mag = jnp.exp(lam_re * dt)
    ang = lam_im * dt
    lbr = mag * jnp.cos(ang)
    lbi = mag * jnp.sin(ang)
    nr = lbr - 1.0
    den = lam_re * lam_re + lam_im * lam_im
    cr = (nr * lam_re + lbi * lam_im) / den
    ci = (lbi * lam_re - nr * lam_im) / den
    return lbr, lbi, cr * b_re - ci * b_im, cr * b_im + ci * b_re


def _cmul(ar, ai, br, bi):
    return ar * br - ai * bi, ar * bi + ai * br


def _s5_param_fwd(lam_re, lam_im, logdt, b_re, b_im):
    def body(lr, li, ld, br, bi, o_br, o_bi, o_pr, o_pi):
        lbr, lbi, bbr, bbi = _s5_param_fn(lr[...], li[...], ld[...], br[...], bi[...])
        o_br[...] = bbr
        o_bi[...] = bbi
        rid = lax.broadcasted_iota(jnp.int32, (8, N_STATE), 0)
        pr, pi_ = lbr, lbi
        acc_r = jnp.broadcast_to(pr, (8, N_STATE))
        acc_i = jnp.broadcast_to(pi_, (8, N_STATE))
        for j in range(1, 8):
            pr, pi_ = _cmul(pr, pi_, lbr, lbi)
            acc_r = jnp.where(rid == j, jnp.broadcast_to(pr, (8, N_STATE)), acc_r)
            acc_i = jnp.where(rid == j, jnp.broadcast_to(pi_, (8, N_STATE)), acc_i)
        o_pr[...] = acc_r
        o_pi[...] = acc_i

    return pl.pallas_call(
        body, name="s5_param_fwd",
        out_shape=[_sds((S5_GROUP, N_STATE))] * 2 + [_sds((8, N_STATE))] * 2,
    )(lam_re, lam_im, logdt, b_re, b_im)


def _s5_param_bwd(lam_re, lam_im, logdt, b_re, b_im, d_lbr, d_lbi, d_bbr, d_bbi, group_ind):
    def body(lr, li, ld, br, bi, g0, g1, g2, g3, ind, o_lr, o_li, o_ld, o_br, o_bi):
        _, vjp = jax.vjp(_s5_param_fn, lr[...], li[...], ld[...], br[...], bi[...])
        d_lr, d_li, d_ld, d_br, d_bi = vjp((g0[...], g1[...], g2[...], g3[...]))
        o_lr[...] = d_lr
        o_li[...] = d_li
        o_ld[...] = _dot(jnp.broadcast_to(d_ld, (8, N_STATE)), ind[...], ((1,), (0,)), HI)
        o_br[...] = d_br
        o_bi[...] = d_bi

    return pl.pallas_call(
        body, name="s5_param_bwd",
        out_shape=[_sds((1, N_STATE))] * 2 + [_sds((8, LANES))] + [_sds((S5_GROUP, N_STATE))] * 2,
    )(lam_re, lam_im, logdt, b_re, b_im, d_lbr, d_lbi, d_bbr, d_bbi, group_ind)


def _fwd_in(x, norm_g, w_in_bf, tt):
    L = x.shape[0]

    def body(x_ref, g_ref, w_ref, u_ref, zs_ref, rw_ref, zr_ref):
        h = _rms(x_ref[...], g_ref[...])
        proj = jnp.dot(h.astype(BF16), w_ref[...], preferred_element_type=F32)
        u_ref[...] = proj[:, 0:D_S5]
        zs_ref[...] = proj[:, D_S5:2 * D_S5]
        rw_ref[...] = proj[:, 2 * D_S5:2 * D_S5 + D_SHIFT]
        zr_ref[...] = proj[:, 2 * D_S5 + D_SHIFT:D_IN]

    row = lambda n: pl.BlockSpec((tt, n), lambda i: (i, 0))
    return pl.pallas_call(
        body, name="fwd_in", grid=(L // tt,),
        in_specs=[row(D_MODEL), _const_spec((1, D_MODEL)), _const_spec((D_MODEL, D_IN))],
        out_specs=[row(D_S5), row(D_S5), row(D_SHIFT), row(D_RWKV)],
        out_shape=[_sds((L, D_S5)), _sds((L, D_S5)), _sds((L, D_SHIFT)), _sds((L, D_RWKV))],
        compiler_params=_params("parallel"),
    )(x, norm_g, w_in_bf)


S5_LANE_CHUNK = 512


def _tile_scan(re_ref, im_ref, pow_r_ref, pow_i_ref, carry_r_ref, carry_i_ref, reverse):
    t, n = re_ref.shape
    n_groups = t // 8
    ch = S5_LANE_CHUNK
    rid = lax.broadcasted_iota(jnp.int32, (8, ch), 0)
    for c in range(n // ch):
        cols = slice(c * ch, (c + 1) * ch)
        pow_r = pow_r_ref[:, cols]
        pow_i = pow_i_ref[:, cols]
        row = lambda tile, j: jnp.broadcast_to(tile[j:j + 1], (8, ch))
        levels = [(d, row(pow_r, 8 - d if reverse else d - 1), row(pow_i, 8 - d if reverse else d - 1))
                  for d in (1, 2, 4)]

        def group(g, carry):
            r0 = pl.multiple_of(((n_groups - 1 - g) if reverse else g) * 8, 8)
            xr = re_ref[pl.ds(r0, 8), cols]
            xi = im_ref[pl.ds(r0, 8), cols]
            for d, lr, li in levels:
                keep = (rid < 8 - d) if reverse else (rid >= d)
                shift = (8 - d) if reverse else d
                sr = jnp.where(keep, pltpu.roll(xr, shift, axis=0), 0.0)
                si = jnp.where(keep, pltpu.roll(xi, shift, axis=0), 0.0)
                mr, mi = _cmul(lr, li, sr, si)
                xr = xr + mr
                xi = xi + mi
            mr, mi = _cmul(pow_r, pow_i, carry[0], carry[1])
            xr = xr + mr
            xi = xi + mi
            re_ref[pl.ds(r0, 8), cols] = xr
            im_ref[pl.ds(r0, 8), cols] = xi
            last = 0 if reverse else 7
            return row(xr, last), row(xi, last)

        out = lax.fori_loop(0, n_groups, group, (carry_r_ref[:, cols], carry_i_ref[:, cols]))
        carry_r_ref[:, cols] = out[0]
        carry_i_ref[:, cols] = out[1]


def _s5_fwd(u, b4_re, b4_im, c4_re, c4_im, pow_r, pow_i, tt):
    L = u.shape[0]

    def body(u_ref, bre_ref, bim_ref, cre_ref, cim_ref, pr_ref, pi_ref, sre_o, sim_o, y_o, car_r, car_i):
        @pl.when(pl.program_id(0) == 0)
        def _():
            car_r[...] = jnp.zeros_like(car_r)
            car_i[...] = jnp.zeros_like(car_i)

        uv = u_ref[...]
        for q in range(S5_BLOCKS):
            uq = uv[:, q * LANES:(q + 1) * LANES]
            cols = slice(q * 512, (q + 1) * 512)
            sre_o[:, cols] = _dot_bf(uq, bre_ref[q], ((1,), (0,)))
            sim_o[:, cols] = _dot_bf(uq, bim_ref[q], ((1,), (0,)))
        _tile_scan(sre_o, sim_o, pr_ref, pi_ref, car_r, car_i, reverse=False)
        for q in range(S5_BLOCKS):
            cols = slice(q * 512, (q + 1) * 512)
            y_o[:, q * LANES:(q + 1) * LANES] = (_dot_bf(sre_o[:, cols], cre_ref[q], ((1,), (0,)))
                                                 - _dot_bf(sim_o[:, cols], cim_ref[q], ((1,), (0,))))

    row = lambda n: pl.BlockSpec((tt, n), lambda i: (i, 0))
    return pl.pallas_call(
        body, name="s5_fwd", grid=(L // tt,),
        in_specs=[row(D_S5)] + [_const_spec((S5_BLOCKS, LANES, 512))] * 2 + [_const_spec((S5_BLOCKS, 512, LANES))] * 2
        + [_const_spec((8, N_STATE))] * 2,
        out_specs=[row(N_STATE), row(N_STATE), row(D_S5)],
        out_shape=[_sds((L, N_STATE)), _sds((L, N_STATE)), _sds((L, D_S5))],
        scratch_shapes=[pltpu.VMEM((8, N_STATE), F32)] * 2,
        compiler_params=_params("arbitrary"),
    )(u, b4_re, b4_im, c4_re, c4_im, pow_r, pow_i)


def _rwkv_pre_fn(r, k, v, wa, w0, w2p, a0, a2p, k_k, k_a, ee):
    w = -_softplus(-(w0 + mm_hi(jnp.tanh(wa), w2p))) - 0.5
    logw = -jnp.exp(w)
    a = _sigmoid(a0 + mm_hi(wa, a2p))
    kkp = k * k_k
    kk = kkp / jnp.maximum(jnp.sqrt(mm_hi(kkp * kkp, ee)), 1e-12)
    k2 = k * (1.0 + (a - 1.0) * k_a)
    return r, logw, k2, v, -kk, kk * a


def _head_spec(tt):
    return pl.BlockSpec((N_HEADS, tt, HEAD), lambda i: (0, i, 0))


def _load_heads(ref):
    return jnp.concatenate([ref[h] for h in range(N_HEADS)], axis=-1)


def _store_heads(ref, val):
    for h in range(N_HEADS):
        ref[h] = val[:, h * HEAD:(h + 1) * HEAD]


def _shifted(rw, prev_blk, first):
    rolled = pltpu.roll(rw, 1, axis=0)
    prev_row = jnp.where(first, 0.0, prev_blk[7:8, :])
    rid = lax.broadcasted_iota(jnp.int32, rw.shape, 0)
    return jnp.where(rid == 0, jnp.broadcast_to(prev_row, rw.shape), rolled)


def _split_rw(t):
    return t[:, 0:512], t[:, 512:1024], t[:, 1024:1536], t[:, 1536:1664]


def _rwkv_pre_specs(tt):
    row = pl.BlockSpec((tt, D_SHIFT), lambda i: (i, 0))
    prev = pl.BlockSpec((8, D_SHIFT), lambda i: (jnp.maximum(i * (tt // 8) - 1, 0), 0))
    consts = [_const_spec((1, D_SHIFT)), _const_spec((1, D_RWKV)), _const_spec((LANES, D_RWKV)),
              _const_spec((1, D_RWKV)), _const_spec((LANES, D_RWKV)), _const_spec((1, D_RWKV)),
              _const_spec((1, D_RWKV)), _const_spec((D_RWKV, D_RWKV))]
    return [row, prev] + consts


def _rwkv_pre_fwd(rw, mu, w0, w2p, a0, a2p, k_k, k_a, ee, tt):
    L = rw.shape[0]

    def body(rw_ref, prev_ref, mu_ref, w0_ref, w2_ref, a0_ref, a2_ref, kk_ref, ka_ref, ee_ref, *outs):
        rwv = rw_ref[...]
        rws = rwv + (_shifted(rwv, prev_ref[...], pl.program_id(0) == 0) - rwv) * mu_ref[...]
        res = _rwkv_pre_fn(*_split_rw(rws), w0_ref[...], w2_ref[...], a0_ref[...], a2_ref[...],
                           kk_ref[...], ka_ref[...], ee_ref[...])
        for o, val in zip(outs, res):
            _store_heads(o, val)

    return pl.pallas_call(
        body, name="rwkv_pre_fwd", grid=(L // tt,),
        in_specs=_rwkv_pre_specs(tt), out_specs=[_head_spec(tt)] * 6, out_shape=[_sds((N_HEADS, L, HEAD))] * 6,
        compiler_params=_params("parallel"),
    )(rw, rw, mu, w0, w2p, a0, a2p, k_k, k_a, ee)


def _rwkv_pre_bwd(rw, mu, w0, w2p, a0, a2p, k_k, k_a, ee, cots, tt):
    L = rw.shape[0]
    n_t = L // tt

    def body(rw_ref, prev_ref, mu_ref, w0_ref, w2_ref, a0_ref, a2_ref, kk_ref, ka_ref, ee_ref,
             c_r, c_w, c_k, c_v, c_a, c_b, cb_r, cb_k, cb_v,
             drws_ref, dmu_o, dw0_o, dw2_o, da0_o, da2_o, dkk_o, dka_o,
             dmu, dw0, dw2, da0, da2, dkk, dka):
        i = pl.program_id(0)
        accs = (dmu, dw0, dw2, da0, da2, dkk, dka)

        @pl.when(i == 0)
        def _():
            for acc in accs:
                acc[...] = jnp.zeros_like(acc)

        rwv = rw_ref[...]
        diff = _shifted(rwv, prev_ref[...], i == 0) - rwv
        rws = rwv + diff * mu_ref[...]
        consts = (w0_ref[...], w2_ref[...], a0_ref[...], a2_ref[...], kk_ref[...], ka_ref[...])
        _, vjp = jax.vjp(lambda *a: _rwkv_pre_fn(*a, ee_ref[...]), *_split_rw(rws), *consts)
        scan = [_load_heads(c) for c in (c_r, c_w, c_k, c_v, c_a, c_b)]
        g = vjp((scan[0] + cb_r[...], scan[1], scan[2] + cb_k[...], scan[3] + cb_v[...], scan[4], scan[5]))
        drws = jnp.concatenate(g[0:4], axis=-1)
        drws_ref[...] = drws
        dmu[...] += jnp.sum(drws * diff, axis=0, keepdims=True)
        for acc, val in zip(accs[1:], g[4:]):
            acc[...] += val

        @pl.when(i == n_t - 1)
        def _():
            for acc, out in zip(accs, (dmu_o, dw0_o, dw2_o, da0_o, da2_o, dkk_o, dka_o)):
                out[...] = acc[...]

    row = pl.BlockSpec((tt, D_RWKV), lambda i: (i, 0))
    shapes = [(1, D_SHIFT), (1, D_RWKV), (LANES, D_RWKV), (1, D_RWKV), (LANES, D_RWKV), (1, D_RWKV), (1, D_RWKV)]
    return pl.pallas_call(
        body, name="rwkv_pre_bwd", grid=(n_t,),
        in_specs=_rwkv_pre_specs(tt) + [_head_spec(tt)] * 6 + [row] * 3,
        out_specs=[pl.BlockSpec((tt, D_SHIFT), lambda i: (i, 0))] + [_acc_spec(s) for s in shapes],
        out_shape=[_sds((L, D_SHIFT))] + [_sds(s) for s in shapes],
        scratch_shapes=[pltpu.VMEM(s, F32) for s in shapes],
        compiler_params=_params("arbitrary"),
    )(rw, rw, mu, w0, w2p, a0, a2p, k_k, k_a, ee, *cots)


def _bmm(a, b):
    return lax.dot_general(a, b, (((2,), (1,)), ((0,), (0,))), precision=HI, preferred_element_type=F32)


def _bmm_nt(a, b):
    return lax.dot_general(a, b, (((2,), (2,)), ((0,), (0,))), precision=HI, preferred_element_type=F32)


def _bmm_tn(a, b):
    return lax.dot_general(a, b, (((1,), (1,)), ((0,), (0,))), precision=HI, preferred_element_type=F32)


def _bdot_bf(a, b, lhs_dim, rhs_dim):
    return lax.dot_general(a.astype(BF16), b.astype(BF16), (((lhs_dim,), (rhs_dim,)), ((0,), (0,))),
                           preferred_element_type=F32)


@jax.custom_vjp
def _bmm_bf(a, b):
    return _bdot_bf(a, b, 2, 1)


def _bmm_bf_fwd(a, b):
    return _bmm_bf(a, b), (a, b)


def _bmm_bf_bwd(res, g):
    a, b = res
    return _bdot_bf(g, b, 2, 2), _bdot_bf(a, g, 1, 1)


_bmm_bf.defvjp(_bmm_bf_fwd, _bmm_bf_bwd)


@jax.custom_vjp
def _bmm_tn_bf(a, b):
    return _bdot_bf(a, b, 1, 1)


def _bmm_tn_bf_fwd(a, b):
    return _bmm_tn_bf(a, b), (a, b)


def _bmm_tn_bf_bwd(res, g):
    a, b = res
    return _bdot_bf(b, g, 2, 2), _bdot_bf(a, g, 2, 1)


_bmm_tn_bf.defvjp(_bmm_tn_bf_fwd, _bmm_tn_bf_bwd)


def _unit_lower_inverse(a):
    t = a.shape[-1]
    ti = lax.broadcasted_iota(jnp.int32, (t, t), 0)
    si = lax.broadcasted_iota(jnp.int32, (t, t), 1)
    inv = jnp.where(ti == si, 1.0, 0.0)[None] + a
    pw = _bmm(a, a)
    n = 2
    while n < t:
        both = _bmm(jnp.concatenate([inv, pw], axis=1), pw)
        inv = inv + both[:, :t]
        pw = both[:, t:]
        n *= 2
    return inv


@jax.custom_vjp
def _solve_unit_lower(a, rhs, inv):
    return _bmm(inv, rhs)


def _solve_fwd(a, rhs, inv):
    u = _bmm(inv, rhs)
    return u, (inv, u)


def _solve_bwd(res, du):
    inv, u = res
    d_rhs = _bmm_tn(inv, du)
    return _bmm_nt(d_rhs, u), d_rhs, jnp.zeros_like(inv)


_solve_unit_lower.defvjp(_solve_fwd, _solve_bwd)


def _rwkv_chunk(st0, r, logw, k, v, a, b, inv=None):
    n_h, t, _ = r.shape
    ti = lax.broadcasted_iota(jnp.int32, (t, t), 0)
    si = lax.broadcasted_iota(jnp.int32, (t, t), 1)
    ones_tri = jnp.broadcast_to(jnp.where(ti >= si, 1.0, 0.0)[None], (n_h, t, t))
    log_p = _bmm(ones_tri, logw)
    p_in = jnp.exp(log_p)
    p_inv = jnp.exp(-log_p)
    at = a * jnp.exp(log_p - logw)
    rt = r * p_in
    bk = jnp.concatenate([b * p_inv, k * p_inv], axis=1)
    ri = lax.broadcasted_iota(jnp.int32, (2 * t, 2 * t), 0)
    ci = lax.broadcasted_iota(jnp.int32, (2 * t, 2 * t), 1)
    top_rows = ri < t
    diff = jnp.where(top_rows, ri, ri - t) - jnp.where(ci < t, ci, ci - t)
    mask = (diff >= jnp.where(top_rows, 1, 0))[None]
    m = jnp.where(mask, _bmm_nt(jnp.concatenate([at, rt], axis=1), bk), 0.0)
    top, bottom = m[:, :t], m[:, t:]
    a_ab = top[:, :, :t]
    if inv is None:
        inv = _unit_lower_inverse(a_ab)
    rhs = _bmm_bf(jnp.concatenate([at, top[:, :, t:]], axis=2), jnp.concatenate([st0, v], axis=1))
    u = _solve_unit_lower(a_ab, rhs, inv)
    y = _bmm_bf(jnp.concatenate([rt, bottom], axis=2), jnp.concatenate([st0, u, v], axis=1))
    p_end = jnp.swapaxes(p_in[:, t - 1:t, :], 1, 2)
    st1 = (st0 + _bmm_tn_bf(bk, jnp.concatenate([u, v], axis=1))) * p_end
    return y, st1, inv


def _rwkv_scan_fwd(ops):
    n_h, L, n = ops[0].shape
    t = RWKV_CHUNK
    n_c = L // t

    def body(r_ref, w_ref, k_ref, v_ref, a_ref, b_ref, y_ref, st_ref, inv_ref, st):
        @pl.when(pl.program_id(0) == 0)
        def _():
            st[...] = jnp.zeros_like(st)

        st0 = st[...]
        st_ref[0] = st0
        y, st1, inv = _rwkv_chunk(st0, r_ref[...], w_ref[...], k_ref[...], v_ref[...], a_ref[...], b_ref[...])
        y_ref[...] = y
        inv_ref[0] = inv
        st[...] = st1

    blk = pl.BlockSpec((n_h, t, n), lambda c: (0, c, 0))
    return pl.pallas_call(
        body, name="rwkv_scan_fwd", grid=(n_c,), in_specs=[blk] * 6,
        out_specs=[blk, pl.BlockSpec((1, n_h, n, n), lambda c: (c, 0, 0, 0)),
                   pl.BlockSpec((1, n_h, t, t), lambda c: (c, 0, 0, 0))],
        out_shape=[_sds((n_h, L, n)), _sds((n_c, n_h, n, n)), _sds((n_c, n_h, t, t))],
        scratch_shapes=[pltpu.VMEM((n_h, n, n), F32)],
        compiler_params=_params("arbitrary"),
    )(*ops)


def _rwkv_scan_bwd(ops, states, invs, dy):
    n_h, L, n = ops[0].shape
    t = RWKV_CHUNK
    n_c = L // t

    def body(r_ref, w_ref, k_ref, v_ref, a_ref, b_ref, st_ref, inv_ref, dy_ref, dr, dw, dk, dv, da, db, dst):
        @pl.when(pl.program_id(0) == 0)
        def _():
            dst[...] = jnp.zeros_like(dst)

        inv = inv_ref[0]
        _, vjp = jax.vjp(lambda *a: _rwkv_chunk(*a, inv=inv)[:2], st_ref[0], r_ref[...], w_ref[...], k_ref[...],
                         v_ref[...], a_ref[...], b_ref[...])
        g = vjp((dy_ref[...], dst[...]))
        dst[...] = g[0]
        for out, val in zip((dr, dw, dk, dv, da, db), g[1:]):
            out[...] = val

    blk = pl.BlockSpec((n_h, t, n), lambda c: (0, n_c - 1 - c, 0))
    per_chunk = lambda m: pl.BlockSpec((1, n_h, m, m), lambda c: (n_c - 1 - c, 0, 0, 0))
    return pl.pallas_call(
        body, name="rwkv_scan_bwd", grid=(n_c,),
        in_specs=[blk] * 6 + [per_chunk(n), per_chunk(t), blk],
        out_specs=[blk] * 6, out_shape=[_sds((n_h, L, n))] * 6,
        scratch_shapes=[pltpu.VMEM((n_h, n, n), F32)],
        compiler_params=_params("arbitrary"),
    )(*ops, states, invs, dy)


def _post_fn(x, u, zs, zr, ysc, r, k2, v, y_ssm, d, glu_w, glu_b, ln_w, ln_b, r_k,
             wo_s5, wo_rwkv, gf, tgt, ee):
    y3 = _gelu(y_ssm + d * u)
    y_s5 = y3 * _sigmoid(mm_bf(y3, glu_w) + glu_b) * _silu(zs)
    mean = mm_hi(ysc, ee) * (1.0 / HEAD)
    yc = ysc - mean
    var = mm_hi(yc * yc, ee) * (1.0 / HEAD)
    gn = yc * lax.rsqrt(var + GN_EPS) * ln_w + ln_b
    bonus = mm_hi(r * k2 * r_k, ee) * v
    y_rwkv = (gn + bonus) * _silu(zr)
    x2 = x + mm_bf(y_s5, wo_s5) + mm_bf(y_rwkv, wo_rwkv)
    err = _rms(x2, gf) - tgt
    return 0.5 * jnp.mean(err * err, axis=-1, keepdims=True)


def _post(x, u, zs, zr, ysc, r, k2, v, y_ssm, d, glu_w, glu_b, ln_w, ln_b, r_k, w_out, gf, tgt, ee, tt):
    L = x.shape[0]
    n_t = L // tt
    acc_shapes = [(1, D_S5), (D_S5, D_S5), (1, D_S5), (1, D_RWKV), (1, D_RWKV), (1, D_RWKV),
                  (D_MODEL, D_MODEL), (1, D_MODEL), (8, LANES)]

    def body(x_ref, u_ref, zs_ref, zr_ref, ysc_ref, r_ref, k2_ref, v_ref, yssm_ref,
             d_ref, gw_ref, gb_ref, lw_ref, lb_ref, rk_ref, wo_ref, gf_ref, tgt_ref, ee_ref,
             dx_o, du_o, dzs_o, dzr_o, dysc_o, dr_o, dk2_o, dv_o, dyssm_o,
             dd_o, dgw_o, dgb_o, dlw_o, dlb_o, drk_o, dwo_o, dgf_o, loss_o,
             dd, dgw, dgb, dlw, dlb, drk, dwo, dgf, loss):
        i = pl.program_id(0)
        accs = (dd, dgw, dgb, dlw, dlb, drk, dwo, dgf, loss)

        @pl.when(i == 0)
        def _():
            for acc in accs:
                acc[...] = jnp.zeros_like(acc)

        args = (x_ref[...], u_ref[...], zs_ref[...], zr_ref[...],
                _load_heads(ysc_ref), _load_heads(r_ref), _load_heads(k2_ref), _load_heads(v_ref), yssm_ref[...],
                d_ref[...], gw_ref[...], gb_ref[...], lw_ref[...], lb_ref[...], rk_ref[...],
                wo_ref[0:D_S5, :], wo_ref[D_S5:D_MODEL, :], gf_ref[...])
        rows, vjp = jax.vjp(lambda *a: _post_fn(*a, tgt_ref[...], ee_ref[...]), *args)
        g = vjp(jnp.ones_like(rows))
        for out, val in zip((dx_o, du_o, dzs_o, dzr_o), g[0:4]):
            out[...] = val
        _store_heads(dysc_o, g[4])
        for out, val in zip((dr_o, dk2_o, dv_o, dyssm_o), g[5:9]):
            out[...] = val
        for acc, val in zip((dd, dgw, dgb, dlw, dlb, drk), g[9:15]):
            acc[...] += val
        dwo[0:D_S5, :] += g[15]
        dwo[D_S5:D_MODEL, :] += g[16]
        dgf[...] += g[17]
        loss[...] += jnp.broadcast_to(jnp.sum(rows, axis=0, keepdims=True), loss.shape)

        @pl.when(i == n_t - 1)
        def _():
            for acc, out in zip(accs, (dd_o, dgw_o, dgb_o, dlw_o, dlb_o, drk_o, dwo_o, dgf_o, loss_o)):
                pltpu.sync_copy(acc, out)

    row = lambda n: pl.BlockSpec((tt, n), lambda i: (i, 0))
    in_specs = ([row(D_MODEL)] + [row(512)] * 3 + [_head_spec(tt)] * 4 + [row(D_S5)]
                + [_const_spec(s) for s in [(1, D_S5), (D_S5, D_S5), (1, D_S5), (1, D_RWKV), (1, D_RWKV), (1, D_RWKV),
                                            (D_MODEL, D_MODEL), (1, D_MODEL)]]
                + [row(D_MODEL), _const_spec((D_RWKV, D_RWKV))])
    out_rows = [D_MODEL] + [512] * 3 + [None] + [512] * 4
    return pl.pallas_call(
        body, name="post_fwd_bwd", grid=(n_t,), in_specs=in_specs,
        out_specs=[row(n) if n else _head_spec(tt) for n in out_rows] + [_ANY] * len(acc_shapes),
        out_shape=([_sds((L, n)) if n else _sds((N_HEADS, L, HEAD)) for n in out_rows]
                   + [_sds(s) for s in acc_shapes]),
        scratch_shapes=[pltpu.VMEM(s, F32) for s in acc_shapes],
        compiler_params=_params("arbitrary"),
    )(x, u, zs, zr, ysc, r, k2, v, y_ssm, d, glu_w, glu_b, ln_w, ln_b, r_k, w_out, gf, tgt, ee)


def _s5_bwd(u, du_direct, dy, s_re, s_im, b4_re, b4_im, c4_re, c4_im, pow_r, pow_i, tt):
    L = u.shape[0]
    n_t = L // tt
    acc_shapes = ([(S5_BLOCKS, LANES, 512)] * 2 + [(S5_BLOCKS, 512, LANES)] * 2 + [(1, N_STATE)] * 2)

    def body(u_ref, dud_ref, dy_ref, sre_ref, sim_ref, pre_ref, pim_ref, bre_ref, bim_ref, cre_ref, cim_ref,
             pr_ref, pi_ref, du_o, dbre_o, dbim_o, dcre_o, dcim_o, dlr_o, dli_o,
             dbre, dbim, dcre, dcim, dlr, dli, gre, gim, car_r, car_i):
        i = pl.program_id(0)

        @pl.when(i == 0)
        def _():
            for acc in (dbre, dbim, dcre, dcim, dlr, dli, car_r, car_i):
                acc[...] = jnp.zeros_like(acc)

        uv = u_ref[...]
        dyv = dy_ref[...]
        blocks = [slice(q * 512, (q + 1) * 512) for q in range(S5_BLOCKS)]
        lanes = [slice(q * LANES, (q + 1) * LANES) for q in range(S5_BLOCKS)]
        for q in range(S5_BLOCKS):
            gre[:, blocks[q]] = _dot_bf(dyv[:, lanes[q]], cre_ref[q], ((1,), (1,)))
            gim[:, blocks[q]] = -_dot_bf(dyv[:, lanes[q]], cim_ref[q], ((1,), (1,)))
        _tile_scan(gre, gim, pr_ref, pi_ref, car_r, car_i, reverse=True)
        for q in range(S5_BLOCKS):
            gr = gre[:, blocks[q]]
            gi = gim[:, blocks[q]]
            sr = sre_ref[:, blocks[q]]
            si = sim_ref[:, blocks[q]]
            du_o[:, lanes[q]] = (dud_ref[:, lanes[q]] + _dot_bf(gr, bre_ref[q], ((1,), (1,)))
                                 + _dot_bf(gi, bim_ref[q], ((1,), (1,))))
            dbre[q] += _dot_bf(uv[:, lanes[q]], gr, ((0,), (0,)))
            dbim[q] += _dot_bf(uv[:, lanes[q]], gi, ((0,), (0,)))
            dcre[q] += _dot_bf(sr, dyv[:, lanes[q]], ((0,), (0,)))
            dcim[q] -= _dot_bf(si, dyv[:, lanes[q]], ((0,), (0,)))
            rid = lax.broadcasted_iota(jnp.int32, sr.shape, 0)
            first = i == n_t - 1
            prev_r = jnp.where(first, 0.0, pre_ref[7:8, blocks[q]])
            prev_i = jnp.where(first, 0.0, pim_ref[7:8, blocks[q]])
            pr = jnp.where(rid == 0, jnp.broadcast_to(prev_r, sr.shape), pltpu.roll(sr, 1, axis=0))
            pi_ = jnp.where(rid == 0, jnp.broadcast_to(prev_i, si.shape), pltpu.roll(si, 1, axis=0))
            dlr[:, blocks[q]] += jnp.sum(pr * gr + pi_ * gi, axis=0, keepdims=True)
            dli[:, blocks[q]] += jnp.sum(pr * gi - pi_ * gr, axis=0, keepdims=True)

        @pl.when(i == n_t - 1)
        def _():
            for acc, out in zip((dbre, dbim, dcre, dcim, dlr, dli), (dbre_o, dbim_o, dcre_o, dcim_o, dlr_o, dli_o)):
                out[...] = acc[...]

    row = lambda n: pl.BlockSpec((tt, n), lambda i: (n_t - 1 - i, 0))
    prev = pl.BlockSpec((8, N_STATE), lambda i: (jnp.maximum((n_t - 1 - i) * (tt // 8) - 1, 0), 0))
    return pl.pallas_call(
        body, name="s5_bwd", grid=(n_t,),
        in_specs=[row(D_S5)] * 3 + [row(N_STATE)] * 2 + [prev] * 2
        + [_const_spec((S5_BLOCKS, LANES, 512))] * 2 + [_const_spec((S5_BLOCKS, 512, LANES))] * 2
        + [_const_spec((8, N_STATE))] * 2,
        out_specs=[row(D_S5)] + [_acc_spec(s) for s in acc_shapes],
        out_shape=[_sds((L, D_S5))] + [_sds(s) for s in acc_shapes],
        scratch_shapes=[pltpu.VMEM(s, F32) for s in acc_shapes] + [pltpu.VMEM((tt, N_STATE), F32)] * 2
        + [pltpu.VMEM((8, N_STATE), F32)] * 2,
        compiler_params=_params("arbitrary"),
    )(u, du_direct, dy, s_re, s_im, s_re, s_im, b4_re, b4_im, c4_re, c4_im, pow_r, pow_i)


def _bwd_in(x, norm_g, w_in_bf, mu, dx2, du, dzs, drws, dzr, tt):
    L = x.shape[0]
    n_t = L // tt

    def body(x_ref, g_ref, w_ref, mu_ref, dx2_ref, du_ref, dzs_ref, drws_ref, nxt_ref, dzr_ref,
             gx_o, dproj_o, dg_o, dg):
        i = pl.program_id(0)

        @pl.when(i == 0)
        def _():
            dg[...] = jnp.zeros_like(dg)

        drws_v = drws_ref[...]
        rid = lax.broadcasted_iota(jnp.int32, drws_v.shape, 0)
        nxt_row = jnp.where(i == n_t - 1, 0.0, nxt_ref[0:1, :])
        nxt = jnp.where(rid == tt - 1, jnp.broadcast_to(nxt_row, drws_v.shape), pltpu.roll(drws_v, tt - 1, axis=0))
        muv = mu_ref[...]
        drw = drws_v * (1.0 - muv) + nxt * muv
        dproj_o[:, 0:D_S5] = du_ref[...]
        dproj_o[:, D_S5:2 * D_S5] = dzs_ref[...]
        dproj_o[:, 2 * D_S5:2 * D_S5 + D_SHIFT] = drw
        dproj_o[:, 2 * D_S5 + D_SHIFT:D_IN] = dzr_ref[...]
        dh = _dot(dproj_o[...].astype(BF16), w_ref[...], ((1,), (1,)), None)
        _, vjp = jax.vjp(_rms, x_ref[...], g_ref[...])
        dxh, dgv = vjp(dh)
        gx_o[...] = dx2_ref[...] + dxh
        dg[...] += dgv

        @pl.when(i == n_t - 1)
        def _():
            dg_o[...] = dg[...]

    row = lambda n: pl.BlockSpec((tt, n), lambda i: (i, 0))
    nxt = pl.BlockSpec((8, D_SHIFT), lambda i: (jnp.minimum((i + 1) * (tt // 8), L // 8 - 1), 0))
    return pl.pallas_call(
        body, name="bwd_in", grid=(n_t,),
        in_specs=[row(D_MODEL), _const_spec((1, D_MODEL)), _const_spec((D_MODEL, D_IN)), _const_spec((1, D_SHIFT)),
                  row(D_MODEL), row(D_S5), row(D_S5), row(D_SHIFT), nxt, row(D_RWKV)],
        out_specs=[row(D_MODEL), row(D_IN), _acc_spec((1, D_MODEL))],
        out_shape=[_sds((L, D_MODEL)), _sds((L, D_IN)), _sds((1, D_MODEL))],
        scratch_shapes=[pltpu.VMEM((1, D_MODEL), F32)],
        compiler_params=_params("arbitrary"),
    )(x, norm_g, w_in_bf, mu, dx2, du, dzs, drws, drws, dzr)


def _grad_w_in(x, norm_g, dproj, tt, cn):
    L = x.shape[0]

    def body(x_ref, g_ref, dp_ref, out_ref):
        @pl.when(pl.program_id(1) == 0)
        def _():
            out_ref[...] = jnp.zeros_like(out_ref)

        h = _rms(x_ref[...], g_ref[...])
        out_ref[...] += _dot(h.astype(BF16), dp_ref[...].astype(BF16), ((0,), (0,)), None)

    return pl.pallas_call(
        body, name="grad_w_in", grid=(D_IN // cn, L // tt),
        in_specs=[pl.BlockSpec((tt, D_MODEL), lambda j, i: (i, 0)), pl.BlockSpec((1, D_MODEL), lambda j, i: (0, 0)),
                  pl.BlockSpec((tt, cn), lambda j, i: (i, j))],
        out_specs=pl.BlockSpec((D_MODEL, cn), lambda j, i: (0, j)),
        out_shape=_sds((D_MODEL, D_IN)),
        compiler_params=pltpu.CompilerParams(dimension_semantics=("parallel", "arbitrary"), vmem_limit_bytes=VMEM_LIMIT),
    )(x, norm_g, dproj)


def _block_diag_b(bbar):
    bb = bbar.reshape(S5_GROUP, S5_BLOCKS, 8, S5_STATE)
    return jnp.einsum('hqgp,Gg->qGhgp', bb, jnp.eye(8, dtype=F32)).reshape(S5_BLOCKS, LANES, 512)


def _block_diag_b_t(db4):
    d = db4.reshape(S5_BLOCKS, 8, S5_GROUP, 8, S5_STATE)
    return jnp.einsum('qGhgp,Gg->hqgp', d, jnp.eye(8, dtype=F32)).reshape(S5_GROUP, N_STATE)


def _block_diag_c(c):
    cc = c.reshape(S5_BLOCKS, 8, S5_GROUP, S5_STATE)
    return jnp.einsum('qghp,gG->qgpGh', cc, jnp.eye(8, dtype=F32)).reshape(S5_BLOCKS, 512, LANES)


def _block_diag_c_t(dc4):
    d = dc4.reshape(S5_BLOCKS, 8, S5_STATE, 8, S5_GROUP)
    return jnp.einsum('qgpGh,gG->qghp', d, jnp.eye(8, dtype=F32)).reshape(S5_GROUPS, S5_GROUP, S5_STATE)


def _local_step(x, tgt, w):
    L = x.shape[0]
    tt = min(256, L)
    tp = min(128, L)
    ee = _head_sum_matrix()

    lam_re = w['s5_lam_re'].reshape(1, N_STATE)
    lam_im = w['s5_lam_im'].reshape(1, N_STATE)
    logdt = jnp.repeat(w['s5_log_dt'], S5_STATE).reshape(1, N_STATE)
    b_re_t = w['s5_b_re'].transpose(2, 0, 1).reshape(S5_GROUP, N_STATE)
    b_im_t = w['s5_b_im'].transpose(2, 0, 1).reshape(S5_GROUP, N_STATE)
    bbr, bbi, pow_r, pow_i = _s5_param_fwd(lam_re, lam_im, logdt, b_re_t, b_im_t)
    b4_re, b4_im = _block_diag_b(bbr), _block_diag_b(bbi)
    c4_re, c4_im = _block_diag_c(w['s5_c_re']), _block_diag_c(w['s5_c_im'])

    norm_g = w['norm_g'].reshape(1, D_MODEL)
    w_in_bf = w['w_in'].astype(BF16)
    u, zs, rw, zr = _fwd_in(x, norm_g, w_in_bf, tt)
    s_re, s_im, y_ssm = _s5_fwd(u, b4_re, b4_im, c4_re, c4_im, pow_r, pow_i, tt)

    row = lambda t: t.reshape(1, -1)
    zpad = jnp.zeros((HEAD, D_RWKV), F32)
    w2p = jnp.concatenate([w['rwkv_w2'], zpad], axis=0)
    a2p = jnp.concatenate([zpad, w['rwkv_a2']], axis=0)
    pre_consts = (row(w['rwkv_mu']), row(w['rwkv_w0']), w2p, row(w['rwkv_a0']), a2p,
                  row(w['rwkv_k_k']), row(w['rwkv_k_a']), ee)
    ops = _rwkv_pre_fwd(rw, *pre_consts, tt)
    ysc, states, invs = _rwkv_scan_fwd(ops)

    post = _post(x, u, zs, zr, ysc, ops[0], ops[2], ops[3], y_ssm,
                 row(w['s5_d']), w['s5_glu_w'], row(w['s5_glu_b']), row(w['rwkv_ln_w']), row(w['rwkv_ln_b']),
                 row(w['rwkv_r_k']), w['w_out'], row(w['final_g']), tgt, ee, tp)
    (dx2, du_d, dzs, dzr, dysc, dr_b, dk2_b, dv_b, dy_ssm,
     dd, dglu_w, dglu_b, dln_w, dln_b, dr_k, dw_out, dgf, loss) = post

    du, db4_re, db4_im, dc4_re, dc4_im, dlbr, dlbi = _s5_bwd(
        u, du_d, dy_ssm, s_re, s_im, b4_re, b4_im, c4_re, c4_im, pow_r[::-1], -pow_i[::-1], tt)
    group_ind = (jnp.arange(N_STATE)[:, None] // S5_STATE == jnp.arange(LANES)[None, :]).astype(F32)
    dlam_re, dlam_im, dlogdt, db_re_t, db_im_t = _s5_param_bwd(
        lam_re, lam_im, logdt, b_re_t, b_im_t, dlbr, dlbi, _block_diag_b_t(db4_re), _block_diag_b_t(db4_im), group_ind)

    cots = list(_rwkv_scan_bwd(ops, states, invs, dysc)) + [dr_b, dk2_b, dv_b]
    drws, dmu, dw0, dw2p, da0, da2p, dk_k, dk_a = _rwkv_pre_bwd(rw, *pre_consts, cots, tt)

    grad_x, dproj, dnorm_g = _bwd_in(x, norm_g, w_in_bf, row(w['rwkv_mu']), dx2, du, dzs, drws, dzr, tt)
    dw_in = _grad_w_in(x, norm_g, dproj, tt, 640)

    unb = lambda t: t.reshape(S5_GROUP, S5_GROUPS, S5_STATE).transpose(1, 2, 0)
    grads = {
        'norm_g': dnorm_g.reshape(D_MODEL), 'w_in': dw_in,
        's5_lam_re': dlam_re.reshape(S5_GROUPS, S5_STATE), 's5_lam_im': dlam_im.reshape(S5_GROUPS, S5_STATE),
        's5_log_dt': dlogdt[0, :S5_GROUPS], 's5_b_re': unb(db_re_t), 's5_b_im': unb(db_im_t),
        's5_c_re': _block_diag_c_t(dc4_re), 's5_c_im': _block_diag_c_t(dc4_im),
        's5_d': dd.reshape(D_S5), 's5_glu_w': dglu_w, 's5_glu_b': dglu_b.reshape(D_S5),
        'rwkv_mu': dmu.reshape(-1), 'rwkv_w0': dw0.reshape(-1), 'rwkv_w2': dw2p[:HEAD], 'rwkv_a0': da0.reshape(-1),
        'rwkv_a2': da2p[HEAD:], 'rwkv_k_k': dk_k.reshape(-1), 'rwkv_k_a': dk_a.reshape(-1),
        'rwkv_r_k': dr_k.reshape(N_HEADS, HEAD), 'rwkv_ln_w': dln_w.reshape(-1), 'rwkv_ln_b': dln_b.reshape(-1),
        'w_out': dw_out, 'final_g': dgf.reshape(D_MODEL),
    }
    return loss, grad_x, grads


def _exchange(arrays, gather, axes, name):
    n = len(arrays)
    group = 2 ** len(axes)

    def body(*refs):
        send_refs, recv_refs = refs[:n], refs[n:2 * n]
        send_sems, recv_sems, local_sems = refs[2 * n:]
        pos = {ax: lax.axis_index(ax) for ax in ("x", "y", "c")}

        def index_of(p):
            idx = 0
            for ax in axes:
                idx = 2 * idx + p[ax]
            return idx

        me = index_of(pos)
        own, outs, arrivals = [], [], []
        for i, (send_ref, recv_ref) in enumerate(zip(send_refs, recv_refs)):
            def block_for(dev, send_ref=send_ref, whole=gather[i]):
                return send_ref if whole else send_ref.at[dev]

            own.append(pltpu.make_async_copy(block_for(me), recv_ref.at[me], local_sems.at[i]))
            own[-1].start()
            for k in range(1, group):
                peer = dict(pos)
                for bit, ax in enumerate(axes):
                    if (k >> bit) & 1:
                        peer[ax] = 1 - pos[ax]
                peer_idx = index_of(peer)
                sems = dict(send_sem=send_sems.at[i, k - 1], recv_sem=recv_sems.at[i, k - 1],
                            device_id=(peer["x"], peer["y"], peer["c"]), device_id_type=pl.DeviceIdType.MESH)
                outs.append(pltpu.make_async_remote_copy(src_ref=block_for(peer_idx), dst_ref=recv_ref.at[me], **sems))
                outs[-1].start()
                arrivals.append(
                    pltpu.make_async_remote_copy(src_ref=block_for(peer_idx), dst_ref=recv_ref.at[peer_idx], **sems))
        for copy in arrivals:
            copy.wait_recv()
        for copy in outs:
            copy.wait_send()
        for copy in own:
            copy.wait()

    return pl.pallas_call(
        body, name=name, in_specs=[_ANY] * n, out_specs=[_ANY] * n,
        out_shape=[jax.ShapeDtypeStruct(((group,) + a.shape) if whole else a.shape, a.dtype)
                   for a, whole in zip(arrays, gather)],
        scratch_shapes=[pltpu.SemaphoreType.DMA((n, group - 1)), pltpu.SemaphoreType.DMA((n, group - 1)),
                        pltpu.SemaphoreType.DMA((n,))],
        compiler_params=pltpu.CompilerParams(has_side_effects=True),
    )(*arrays)


def _sum_devices(ref):
    g = ref[0].astype(F32)
    for s in range(1, ref.shape[0]):
        g = g + ref[s].astype(F32)
    return g


def _adamw_math(g, w, m, v):
    m_new = ADAM_B1 * m + (1.0 - ADAM_B1) * g
    v_new = ADAM_B2 * v + (1.0 - ADAM_B2) * (g * g)
    m_hat = m_new / (1.0 - ADAM_B1 ** ADAM_STEP)
    v_hat = v_new / (1.0 - ADAM_B2 ** ADAM_STEP)
    return -ADAM_LR * (m_hat / (jnp.sqrt(v_hat) + ADAM_EPS) + ADAM_WD * w), m_new, v_new


def _adamw(gs, ws, ms, vs, reduce, name):
    n = len(ws)

    def body(*refs):
        g_refs, w_refs, m_refs, v_refs = (refs[j * n:(j + 1) * n] for j in range(4))
        outs = refs[4 * n:]
        for i in range(n):
            g = _sum_devices(g_refs[i]) if reduce else g_refs[i][...]
            res = _adamw_math(g, w_refs[i][...], m_refs[i][...], v_refs[i][...])
            for j, val in enumerate(((g,) if reduce else ()) + res):
                outs[j * n + i][...] = val

    return pl.pallas_call(
        body, name=name, out_shape=[_sds(w.shape) for w in ws] * (4 if reduce else 3),
        compiler_params=pltpu.CompilerParams(vmem_limit_bytes=VMEM_LIMIT),
    )(*gs, *ws, *ms, *vs)


def _sum_blocks(recv):
    def body(recv_ref, out_ref):
        out_ref[...] = _sum_devices(recv_ref)

    return pl.pallas_call(body, name="sum_small_grads", out_shape=_sds(recv.shape[1:]))(recv)


_WEIGHTS = [
    ('norm_g', (1, 1024), False), ('w_in', (1, 1024, 400), True), ('s5_lam_re', (1, 32, 64), False),
    ('s5_lam_im', (1, 32, 64), False), ('s5_log_dt', (1, 32), False), ('s5_b_re', (1, 32, 64, 16), False),
    ('s5_b_im', (1, 32, 64, 16), False), ('s5_c_re', (1, 32, 16, 64), False), ('s5_c_im', (1, 32, 16, 64), False),
    ('s5_d', (1, 512), False), ('s5_glu_w', (1, 64, 512), True), ('s5_glu_b', (1, 512), False),
    ('rwkv_mu', (1, 1664), False), ('rwkv_w0', (1, 512), False), ('rwkv_w2', (1, 64, 64), True),
    ('rwkv_a0', (1, 512), False), ('rwkv_a2', (1, 64, 64), True), ('rwkv_k_k', (1, 512), False),
    ('rwkv_k_a', (1, 512), False), ('rwkv_r_k', (1, 8, 64), False), ('rwkv_ln_w', (1, 512), False),
    ('rwkv_ln_b', (1, 512), False), ('w_out', (1, 128, 1024), True), ('final_g', (1024,), False),
]
_SHARDED = [(n, s) for n, s, sharded in _WEIGHTS if sharded]
_SMALL = [(n, s) for n, s, sharded in _WEIGHTS if not sharded]
_COLUMN_SHARDED = ('w_in', 'rwkv_w2', 'rwkv_a2')
_SMALL_ROWS = -(-sum(math.prod(s) for _, s in _SMALL) // (8 * LANES)) * 8


def _pack_small(grads):
    flat = [grads[n].reshape(-1) for n, _ in _SMALL]
    pad = _SMALL_ROWS * LANES - sum(f.size for f in flat)
    return jnp.concatenate(flat + [jnp.zeros((pad,), F32)]).reshape(_SMALL_ROWS, LANES)


def _unpack_small(packed):
    flat = packed.reshape(-1)
    out, off = {}, 0
    for n, s in _SMALL:
        size = math.prod(s)
        out[n] = flat[off:off + size].reshape(s)
        off += size
    return out


_BF16_OPERANDS = ('w_in', 's5_glu_w', 'w_out')


def _join_shards(name, blocks):
    _, rows, cols = blocks.shape
    if name in _COLUMN_SHARDED:
        return blocks.transpose(1, 0, 2).reshape(rows, N_DEV * cols)
    return blocks.reshape(N_DEV * rows, cols)


def _split_shards(name, full, shard_shape):
    rows, cols = shard_shape
    if name in _COLUMN_SHARDED:
        return full.reshape(rows, N_DEV, cols).transpose(1, 0, 2)
    return full.reshape(N_DEV, rows, cols)


def kernel(x, norm_g, w_in, s5_lam_re, s5_lam_im, s5_log_dt, s5_b_re, s5_b_im, s5_c_re, s5_c_im, s5_d, s5_glu_w, s5_glu_b, rwkv_mu, rwkv_w0, rwkv_w2, rwkv_a0, rwkv_a2, rwkv_k_k, rwkv_k_a, rwkv_r_k, rwkv_ln_w, rwkv_ln_b, w_out, final_g, loss_target, m_norm_g, m_w_in, m_s5_lam_re, m_s5_lam_im, m_s5_log_dt, m_s5_b_re, m_s5_b_im, m_s5_c_re, m_s5_c_im, m_s5_d, m_s5_glu_w, m_s5_glu_b, m_rwkv_mu, m_rwkv_w0, m_rwkv_w2, m_rwkv_a0, m_rwkv_a2, m_rwkv_k_k, m_rwkv_k_a, m_rwkv_r_k, m_rwkv_ln_w, m_rwkv_ln_b, m_w_out, m_final_g, v_norm_g, v_w_in, v_s5_lam_re, v_s5_lam_im, v_s5_log_dt, v_s5_b_re, v_s5_b_im, v_s5_c_re, v_s5_c_im, v_s5_d, v_s5_glu_w, v_s5_glu_b, v_rwkv_mu, v_rwkv_w0, v_rwkv_w2, v_rwkv_a0, v_rwkv_a2, v_rwkv_k_k, v_rwkv_k_a, v_rwkv_r_k, v_rwkv_ln_w, v_rwkv_ln_b, v_w_out, v_final_g):
    given = dict(locals())

    n_sh = len(_SHARDED)
    everyone = ("x", "y", "c")
    shards = [given[n][0].astype(BF16 if n in _BF16_OPERANDS else F32) for n, _ in _SHARDED]
    gathered = _exchange(shards, (True,) * n_sh, everyone, "gather_weights")
    local = {n: _join_shards(n, blocks).astype(F32 if n != 'w_in' else BF16)
             for (n, _), blocks in zip(_SHARDED, gathered)}
    local.update({n: (given[n][0] if len(s) > 1 else given[n]) for n, s in _SMALL})

    loss, grad_x, grads = _local_step(x[0], loss_target[0], local)

    blocks = [_split_shards(n, grads[n], s[1:]).astype(BF16) for n, s in _SHARDED]
    recv = _exchange(blocks + [_pack_small(grads)], (False,) * n_sh + (True,), everyone, "exchange_grads")

    result = {}
    for group, name in (([0], "adamw_w_in"), ([1, 2, 3, 4], "adamw_shards")):
        ns = [_SHARDED[i][0] for i in group]
        res = _adamw([recv[i] for i in group], [given[n][0] for n in ns], [given['m_' + n][0] for n in ns],
                     [given['v_' + n][0] for n in ns], True, name)
        for j, n in enumerate(ns):
            result[n] = [res[k * len(ns) + j][None] for k in range(4)]
    g_small = _unpack_small(_sum_blocks(recv[-1]))
    two_d = lambda t: t.reshape(1, -1) if t.ndim == 1 else t
    ns = [n for n, _ in _SMALL]
    res = _adamw([two_d(g_small[n]) for n in ns], [two_d(given[n]) for n in ns], [two_d(given['m_' + n]) for n in ns],
                 [two_d(given['v_' + n]) for n in ns], False, "adamw_small")
    for j, (n, s) in enumerate(_SMALL):
        result[n] = [g_small[n]] + [res[k * len(ns) + j].reshape(s) for k in range(3)]

    total = lax.psum(loss[0, 0], ("x", "y", "c"))
    outs = [total, grad_x[None]]
    for k in range(4):
        outs += [result[n][k] for n, _, _ in _WEIGHTS]
    return tuple(outs)
```

```python
import functools
import math

import jax
import jax.numpy as jnp
from jax import lax
from jax.experimental import pallas as pl
from jax.experimental.pallas import tpu as pltpu

F32 = jnp.float32
BF16 = jnp.bfloat16
HI = lax.Precision.HIGH

D_MODEL = 1024
D_S5 = 512
D_RWKV = 512
S5_GROUPS = 32
S5_GROUP = 16
S5_STATE = 64
N_STATE = S5_GROUPS * S5_STATE
N_HEADS = 8
HEAD = 64
D_SHIFT = 3 * D_RWKV + 128
D_IN = 2 * D_S5 + D_SHIFT + D_RWKV
NORM_EPS = 1e-6
GN_EPS = 64e-5
N_DEV = 8
LANES = 128
S5_BLOCKS = 4
RWKV_CHUNK = 64
VMEM_LIMIT = 56 * 1024 * 1024

ADAM_LR = 0.001
ADAM_B1 = 0.9
ADAM_B2 = 0.999
ADAM_EPS = 1e-08
ADAM_WD = 0.01
ADAM_STEP = 10


def _dot(a, b, dims, prec):
    return lax.dot_general(a, b, (dims, ((), ())), precision=prec, preferred_element_type=F32)


def _dot_bf(a, b, dims):
    return _dot(a.astype(BF16), b.astype(BF16), dims, None)


def _make_mm(cast, prec):
    @jax.custom_vjp
    def mm(a, b):
        return _dot(cast(a), cast(b), ((1,), (0,)), prec)

    def fwd(a, b):
        return mm(a, b), (a, b)

    def bwd(res, g):
        a, b = res
        return (_dot(cast(g), cast(b), ((1,), (1,)), prec), _dot(cast(a), cast(g), ((0,), (0,)), prec))

    mm.defvjp(fwd, bwd)
    return mm


mm_bf = _make_mm(lambda t: t.astype(BF16), None)


def _make_head_sum(split):
    def product(x, ee):
        hi = x.astype(BF16)
        out = _dot(hi, ee, ((1,), (0,)), None)
        if split:
            out = out + _dot((x - hi.astype(F32)).astype(BF16), ee, ((1,), (0,)), None)
        return out

    @jax.custom_vjp
    def head_sum(x, ee):
        return product(x, ee)

    def fwd(x, ee):
        return product(x, ee), ee

    def bwd(ee, g):
        return product(g, ee), jnp.zeros_like(ee)

    head_sum.defvjp(fwd, bwd)
    return head_sum


head_sum = _make_head_sum(False)
head_sum_split = _make_head_sum(True)


def _sigmoid(x):
    return 1.0 / (1.0 + jnp.exp(-x))


def _silu(x):
    return x * _sigmoid(x)


def _softplus(x):
    return jnp.maximum(x, 0.0) + jnp.log(1.0 + jnp.exp(-jnp.abs(x)))


def _gelu(x):
    return 0.5 * x * (1.0 + jnp.tanh(math.sqrt(2.0 / math.pi) * (x + 0.044715 * x * x * x)))


def _rms(x, g):
    return x * lax.rsqrt(jnp.mean(x * x, axis=-1, keepdims=True) + NORM_EPS) * g


def _const_spec(shape):
    nd = len(shape)
    return pl.BlockSpec(shape, lambda *_: (0,) * nd, pipeline_mode=pl.Buffered(1))


def _acc_spec(shape):
    nd = len(shape)
    return pl.BlockSpec(shape, lambda *_: (0,) * nd)


def _params(sem):
    return pltpu.CompilerParams(dimension_semantics=(sem,), vmem_limit_bytes=VMEM_LIMIT)


_ANY = pl.BlockSpec(memory_space=pl.ANY)


def _sds(shape):
    return jax.ShapeDtypeStruct(shape, F32)


def _head_sum_matrix():
    i = jnp.arange(D_RWKV) // HEAD
    return (i[:, None] == i[None, :]).astype(BF16)


def _s5_param_fn(lam_re, lam_im, logdt, b_re, b_im):
    dt = jnp.exp(logdt)
    mag = jnp.exp(lam_re * dt)
    ang = lam_im * dt
    lbr = mag * jnp.cos(ang)
    lbi = mag * jnp.sin(ang)
    nr = lbr - 1.0
    den = lam_re * lam_re + lam_im * lam_im
    cr = (nr * lam_re + lbi * lam_im) / den
    ci = (lbi * lam_re - nr * lam_im) / den
    return lbr, lbi, cr * b_re - ci * b_im, cr * b_im + ci * b_re


def _cmul(ar, ai, br, bi):
    return ar * br - ai * bi, ar * bi + ai * br


def _s5_param_fwd(lam_re, lam_im, logdt, b_re, b_im):
    def body(lr, li, ld, br, bi, o_br, o_bi, o_pr, o_pi):
        lbr, lbi, bbr, bbi = _s5_param_fn(lr[...], li[...], ld[...], br[...], bi[...])
        o_br[...] = bbr
        o_bi[...] = bbi
        rid = lax.broadcasted_iota(jnp.int32, (8, N_STATE), 0)
        pr, pi_ = lbr, lbi
        acc_r = jnp.broadcast_to(pr, (8, N_STATE))
        acc_i = jnp.broadcast_to(pi_, (8, N_STATE))
        for j in range(1, 8):
            pr, pi_ = _cmul(pr, pi_, lbr, lbi)
            acc_r = jnp.where(rid == j, jnp.broadcast_to(pr, (8, N_STATE)), acc_r)
            acc_i = jnp.where(rid == j, jnp.broadcast_to(pi_, (8, N_STATE)), acc_i)
        o_pr[...] = acc_r
        o_pi[...] = acc_i

    return pl.pallas_call(
        body, name="s5_param_fwd",
        out_shape=[_sds((S5_GROUP, N_STATE))] * 2 + [_sds((8, N_STATE))] * 2,
    )(lam_re, lam_im, logdt, b_re, b_im)


def _s5_param_bwd(lam_re, lam_im, logdt, b_re, b_im, d_lbr, d_lbi, d_bbr, d_bbi, group_ind):
    def body(lr, li, ld, br, bi, g0, g1, g2, g3, ind, o_lr, o_li, o_ld, o_br, o_bi):
        _, vjp = jax.vjp(_s5_param_fn, lr[...], li[...], ld[...], br[...], bi[...])
        d_lr, d_li, d_ld, d_br, d_bi = vjp((g0[...], g1[...], g2[...], g3[...]))
        o_lr[...] = d_lr
        o_li[...] = d_li
        o_ld[...] = _dot(jnp.broadcast_to(d_ld, (8, N_STATE)), ind[...], ((1,), (0,)), HI)
        o_br[...] = d_br
        o_bi[...] = d_bi

    return pl.pallas_call(
        body, name="s5_param_bwd",
        out_shape=[_sds((1, N_STATE))] * 2 + [_sds((8, LANES))] + [_sds((S5_GROUP, N_STATE))] * 2,
    )(lam_re, lam_im, logdt, b_re, b_im, d_lbr, d_lbi, d_bbr, d_bbi, group_ind)


def _fwd_in(x, norm_g, w_in_bf, tt):
    L = x.shape[0]

    def body(x_ref, g_ref, w_ref, u_ref, zs_ref, rw_ref, zr_ref):
        h = _rms(x_ref[...], g_ref[...])
        proj = jnp.dot(h.astype(BF16), w_ref[...], preferred_element_type=F32)
        u_ref[...] = proj[:, 0:D_S5]
        zs_ref[...] = proj[:, D_S5:2 * D_S5]
        rw_ref[...] = proj[:, 2 * D_S5:2 * D_S5 + D_SHIFT]
        zr_ref[...] = proj[:, 2 * D_S5 + D_SHIFT:D_IN]

    row = lambda n: pl.BlockSpec((tt, n), lambda i: (i, 0))
    return pl.pallas_call(
        body, name="fwd_in", grid=(L // tt,),
        in_specs=[row(D_MODEL), _const_spec((1, D_MODEL)), _const_spec((D_MODEL, D_IN))],
        out_specs=[row(D_S5), row(D_S5), row(D_SHIFT), row(D_RWKV)],
        out_shape=[_sds((L, D_S5)), _sds((L, D_S5)), _sds((L, D_SHIFT)), _sds((L, D_RWKV))],
        compiler_params=_params("parallel"),
    )(x, norm_g, w_in_bf)


S5_LANE_CHUNK = 512


def _tile_scan(re_ref, im_ref, pow_r_ref, pow_i_ref, carry_r_ref, carry_i_ref, reverse):
    t, n = re_ref.shape
    n_groups = t // 8
    ch = S5_LANE_CHUNK
    rid = lax.broadcasted_iota(jnp.int32, (8, ch), 0)
    for c in range(n // ch):
        cols = slice(c * ch, (c + 1) * ch)
        pow_r = pow_r_ref[:, cols]
        pow_i = pow_i_ref[:, cols]
        row = lambda tile, j: jnp.broadcast_to(tile[j:j + 1], (8, ch))
        levels = [(d, row(pow_r, 8 - d if reverse else d - 1), row(pow_i, 8 - d if reverse else d - 1))
                  for d in (1, 2, 4)]

        def group(g, carry):
            r0 = pl.multiple_of(((n_groups - 1 - g) if reverse else g) * 8, 8)
            xr = re_ref[pl.ds(r0, 8), cols]
            xi = im_ref[pl.ds(r0, 8), cols]
            for d, lr, li in levels:
                keep = (rid < 8 - d) if reverse else (rid >= d)
                shift = (8 - d) if reverse else d
                sr = jnp.where(keep, pltpu.roll(xr, shift, axis=0), 0.0)
                si = jnp.where(keep, pltpu.roll(xi, shift, axis=0), 0.0)
                mr, mi = _cmul(lr, li, sr, si)
                xr = xr + mr
                xi = xi + mi
            mr, mi = _cmul(pow_r, pow_i, carry[0], carry[1])
            xr = xr + mr
            xi = xi + mi
            re_ref[pl.ds(r0, 8), cols] = xr
            im_ref[pl.ds(r0, 8), cols] = xi
            last = 0 if reverse else 7
            return row(xr, last), row(xi, last)

        out = lax.fori_loop(0, n_groups, group, (carry_r_ref[:, cols], carry_i_ref[:, cols]))
        carry_r_ref[:, cols] = out[0]
        carry_i_ref[:, cols] = out[1]


def _s5_fwd(u, b4_re, b4_im, c4_re, c4_im, pow_r, pow_i, tt):
    L = u.shape[0]

    def body(u_ref, bre_ref, bim_ref, cre_ref, cim_ref, pr_ref, pi_ref, sre_o, sim_o, y_o, car_r, car_i):
        @pl.when(pl.program_id(0) == 0)
        def _():
            car_r[...] = jnp.zeros_like(car_r)
            car_i[...] = jnp.zeros_like(car_i)

        uv = u_ref[...]
        for q in range(S5_BLOCKS):
            uq = uv[:, q * LANES:(q + 1) * LANES]
            cols = slice(q * 512, (q + 1) * 512)
            sre_o[:, cols] = _dot_bf(uq, bre_ref[q], ((1,), (0,)))
            sim_o[:, cols] = _dot_bf(uq, bim_ref[q], ((1,), (0,)))
        _tile_scan(sre_o, sim_o, pr_ref, pi_ref, car_r, car_i, reverse=False)
        for q in range(S5_BLOCKS):
            cols = slice(q * 512, (q + 1) * 512)
            y_o[:, q * LANES:(q + 1) * LANES] = (_dot_bf(sre_o[:, cols], cre_ref[q], ((1,), (0,)))
                                                 - _dot_bf(sim_o[:, cols], cim_ref[q], ((1,), (0,))))

    row = lambda n: pl.BlockSpec((tt, n), lambda i: (i, 0))
    return pl.pallas_call(
        body, name="s5_fwd", grid=(L // tt,),
        in_specs=[row(D_S5)] + [_const_spec((S5_BLOCKS, LANES, 512))] * 2 + [_const_spec((S5_BLOCKS, 512, LANES))] * 2
        + [_const_spec((8, N_STATE))] * 2,
        out_specs=[row(N_STATE), row(N_STATE), row(D_S5)],
        out_shape=[_sds((L, N_STATE)), _sds((L, N_STATE)), _sds((L, D_S5))],
        scratch_shapes=[pltpu.VMEM((8, N_STATE), F32)] * 2,
        compiler_params=_params("arbitrary"),
    )(u, b4_re, b4_im, c4_re, c4_im, pow_r, pow_i)


def _rwkv_pre_fn(r, k, v, wa, w0, w2p, a0, a2p, k_k, k_a, ee):
    w = -_softplus(-(w0 + mm_bf(jnp.tanh(wa), w2p))) - 0.5
    logw = -jnp.exp(w)
    a = _sigmoid(a0 + mm_bf(wa, a2p))
    kkp = k * k_k
    kk = kkp / jnp.maximum(jnp.sqrt(head_sum(kkp * kkp, ee)), 1e-12)
    k2 = k * (1.0 + (a - 1.0) * k_a)
    return r, logw, k2, v, -kk, kk * a


def _head_spec(tt):
    return pl.BlockSpec((N_HEADS, tt, HEAD), lambda i: (0, i, 0))


def _load_heads(ref):
    return jnp.concatenate([ref[h] for h in range(N_HEADS)], axis=-1)


def _store_heads(ref, val):
    for h in range(N_HEADS):
        ref[h] = val[:, h * HEAD:(h + 1) * HEAD]


def _shifted(rw, prev_blk, first):
    rolled = pltpu.roll(rw, 1, axis=0)
    prev_row = jnp.where(first, 0.0, prev_blk[7:8, :])
    rid = lax.broadcasted_iota(jnp.int32, rw.shape, 0)
    return jnp.where(rid == 0, jnp.broadcast_to(prev_row, rw.shape), rolled)


def _split_rw(t):
    return t[:, 0:512], t[:, 512:1024], t[:, 1024:1536], t[:, 1536:1664]


def _rwkv_pre_specs(tt):
    row = pl.BlockSpec((tt, D_SHIFT), lambda i: (i, 0))
    prev = pl.BlockSpec((8, D_SHIFT), lambda i: (jnp.maximum(i * (tt // 8) - 1, 0), 0))
    consts = [_const_spec((1, D_SHIFT)), _const_spec((1, D_RWKV)), _const_spec((LANES, D_RWKV)),
              _const_spec((1, D_RWKV)), _const_spec((LANES, D_RWKV)), _const_spec((1, D_RWKV)),
              _const_spec((1, D_RWKV)), _const_spec((D_RWKV, D_RWKV))]
    return [row, prev] + consts


def _rwkv_pre_fwd(rw, mu, w0, w2p, a0, a2p, k_k, k_a, ee, tt):
    L = rw.shape[0]

    def body(rw_ref, prev_ref, mu_ref, w0_ref, w2_ref, a0_ref, a2_ref, kk_ref, ka_ref, ee_ref, *outs):
        rwv = rw_ref[...]
        rws = rwv + (_shifted(rwv, prev_ref[...], pl.program_id(0) == 0) - rwv) * mu_ref[...]
        res = _rwkv_pre_fn(*_split_rw(rws), w0_ref[...], w2_ref[...], a0_ref[...], a2_ref[...],
                           kk_ref[...], ka_ref[...], ee_ref[...])
        for o, val in zip(outs, res):
            _store_heads(o, val)

    return pl.pallas_call(
        body, name="rwkv_pre_fwd", grid=(L // tt,),
        in_specs=_rwkv_pre_specs(tt), out_specs=[_head_spec(tt)] * 6, out_shape=[_sds((N_HEADS, L, HEAD))] * 6,
        compiler_params=_params("parallel"),
    )(rw, rw, mu, w0, w2p, a0, a2p, k_k, k_a, ee)


def _rwkv_pre_bwd(rw, mu, w0, w2p, a0, a2p, k_k, k_a, ee, cots, tt):
    L = rw.shape[0]
    n_t = L // tt

    def body(rw_ref, prev_ref, mu_ref, w0_ref, w2_ref, a0_ref, a2_ref, kk_ref, ka_ref, ee_ref,
             c_r, c_w, c_k, c_v, c_a, c_b, cb_r, cb_k, cb_v,
             drws_ref, dmu_o, dw0_o, dw2_o, da0_o, da2_o, dkk_o, dka_o,
             dmu, dw0, dw2, da0, da2, dkk, dka):
        i = pl.program_id(0)
        accs = (dmu, dw0, dw2, da0, da2, dkk, dka)

        @pl.when(i == 0)
        def _():
            for acc in accs:
                acc[...] = jnp.zeros_like(acc)

        rwv = rw_ref[...]
        diff = _shifted(rwv, prev_ref[...], i == 0) - rwv
        rws = rwv + diff * mu_ref[...]
        consts = (w0_ref[...], w2_ref[...], a0_ref[...], a2_ref[...], kk_ref[...], ka_ref[...])
        _, vjp = jax.vjp(lambda *a: _rwkv_pre_fn(*a, ee_ref[...]), *_split_rw(rws), *consts)
        scan = [_load_heads(c) for c in (c_r, c_w, c_k, c_v, c_a, c_b)]
        g = vjp((scan[0] + cb_r[...], scan[1], scan[2] + cb_k[...], scan[3] + cb_v[...], scan[4], scan[5]))
        drws = jnp.concatenate(g[0:4], axis=-1)
        drws_ref[...] = drws
        dmu[...] += jnp.sum(drws * diff, axis=0, keepdims=True)
        for acc, val in zip(accs[1:], g[4:]):
            acc[...] += val

        @pl.when(i == n_t - 1)
        def _():
            for acc, out in zip(accs, (dmu_o, dw0_o, dw2_o, da0_o, da2_o, dkk_o, dka_o)):
                out[...] = acc[...]

    row = pl.BlockSpec((tt, D_RWKV), lambda i: (i, 0))
    shapes = [(1, D_SHIFT), (1, D_RWKV), (LANES, D_RWKV), (1, D_RWKV), (LANES, D_RWKV), (1, D_RWKV), (1, D_RWKV)]
    return pl.pallas_call(
        body, name="rwkv_pre_bwd", grid=(n_t,),
        in_specs=_rwkv_pre_specs(tt) + [_head_spec(tt)] * 6 + [row] * 3,
        out_specs=[pl.BlockSpec((tt, D_SHIFT), lambda i: (i, 0))] + [_acc_spec(s) for s in shapes],
        out_shape=[_sds((L, D_SHIFT))] + [_sds(s) for s in shapes],
        scratch_shapes=[pltpu.VMEM(s, F32) for s in shapes],
        compiler_params=_params("arbitrary"),
    )(rw, rw, mu, w0, w2p, a0, a2p, k_k, k_a, ee, *cots)


def _bmm(a, b):
    return lax.dot_general(a, b, (((2,), (1,)), ((0,), (0,))), precision=HI, preferred_element_type=F32)


def _bmm_nt(a, b):
    return lax.dot_general(a, b, (((2,), (2,)), ((0,), (0,))), precision=HI, preferred_element_type=F32)


def _bmm_tn(a, b):
    return lax.dot_general(a, b, (((1,), (1,)), ((0,), (0,))), precision=HI, preferred_element_type=F32)


def _bdot_bf(a, b, lhs_dim, rhs_dim):
    return lax.dot_general(a.astype(BF16), b.astype(BF16), (((lhs_dim,), (rhs_dim,)), ((0,), (0,))),
                           preferred_element_type=F32)


@jax.custom_vjp
def _bmm_bf(a, b):
    return _bdot_bf(a, b, 2, 1)


def _bmm_bf_fwd(a, b):
    return _bmm_bf(a, b), (a, b)


def _bmm_bf_bwd(res, g):
    a, b = res
    return _bdot_bf(g, b, 2, 2), _bdot_bf(a, g, 1, 1)


_bmm_bf.defvjp(_bmm_bf_fwd, _bmm_bf_bwd)


@jax.custom_vjp
def _bmm_tn_bf(a, b):
    return _bdot_bf(a, b, 1, 1)


def _bmm_tn_bf_fwd(a, b):
    return _bmm_tn_bf(a, b), (a, b)


def _bmm_tn_bf_bwd(res, g):
    a, b = res
    return _bdot_bf(b, g, 2, 2), _bdot_bf(a, g, 2, 1)


_bmm_tn_bf.defvjp(_bmm_tn_bf_fwd, _bmm_tn_bf_bwd)


def _unit_lower_inverse(a):
    t = a.shape[-1]
    ti = lax.broadcasted_iota(jnp.int32, (t, t), 0)
    si = lax.broadcasted_iota(jnp.int32, (t, t), 1)
    inv = jnp.where(ti == si, 1.0, 0.0)[None] + a
    pw = _bmm(a, a)
    n = 2
    while n < t:
        both = _bmm(jnp.concatenate([inv, pw], axis=1), pw)
        inv = inv + both[:, :t]
        pw = both[:, t:]
        n *= 2
    return inv


@jax.custom_vjp
def _solve_unit_lower(a, rhs, inv):
    return _bmm(inv, rhs)


def _solve_fwd(a, rhs, inv):
    u = _bmm(inv, rhs)
    return u, (inv, u)


def _solve_bwd(res, du):
    inv, u = res
    d_rhs = _bmm_tn(inv, du)
    return _bmm_nt(d_rhs, u), d_rhs, jnp.zeros_like(inv)


_solve_unit_lower.defvjp(_solve_fwd, _solve_bwd)


def _rwkv_chunk(st0, r, logw, k, v, a, b, inv=None):
    n_h, t, _ = r.shape
    ti = lax.broadcasted_iota(jnp.int32, (t, t), 0)
    si = lax.broadcasted_iota(jnp.int32, (t, t), 1)
    ones_tri = jnp.broadcast_to(jnp.where(ti >= si, 1.0, 0.0)[None], (n_h, t, t))
    log_p = _bmm(ones_tri, logw)
    p_in = jnp.exp(log_p)
    p_inv = jnp.exp(-log_p)
    at = a * jnp.exp(log_p - logw)
    rt = r * p_in
    bk = jnp.concatenate([b * p_inv, k * p_inv], axis=1)
    ri = lax.broadcasted_iota(jnp.int32, (2 * t, 2 * t), 0)
    ci = lax.broadcasted_iota(jnp.int32, (2 * t, 2 * t), 1)
    top_rows = ri < t
    diff = jnp.where(top_rows, ri, ri - t) - jnp.where(ci < t, ci, ci - t)
    mask = (diff >= jnp.where(top_rows, 1, 0))[None]
    m = jnp.where(mask, _bmm_nt(jnp.concatenate([at, rt], axis=1), bk), 0.0)
    top, bottom = m[:, :t], m[:, t:]
    a_ab = top[:, :, :t]
    if inv is None:
        inv = _unit_lower_inverse(a_ab)
    rhs = _bmm_bf(jnp.concatenate([at, top[:, :, t:]], axis=2), jnp.concatenate([st0, v], axis=1))
    u = _solve_unit_lower(a_ab, rhs, inv)
    y = _bmm_bf(jnp.concatenate([rt, bottom], axis=2), jnp.concatenate([st0, u, v], axis=1))
    p_end = jnp.swapaxes(p_in[:, t - 1:t, :], 1, 2)
    st1 = (st0 + _bmm_tn_bf(bk, jnp.concatenate([u, v], axis=1))) * p_end
    return y, st1, inv


def _rwkv_scan_fwd(ops):
    n_h, L, n = ops[0].shape
    t = RWKV_CHUNK
    n_c = L // t

    def body(r_ref, w_ref, k_ref, v_ref, a_ref, b_ref, y_ref, st_ref, inv_ref, st):
        @pl.when(pl.program_id(0) == 0)
        def _():
            st[...] = jnp.zeros_like(st)

        st0 = st[...]
        st_ref[0] = st0
        y, st1, inv = _rwkv_chunk(st0, r_ref[...], w_ref[...], k_ref[...], v_ref[...], a_ref[...], b_ref[...])
        y_ref[...] = y
        inv_ref[0] = inv
        st[...] = st1

    blk = pl.BlockSpec((n_h, t, n), lambda c: (0, c, 0))
    return pl.pallas_call(
        body, name="rwkv_scan_fwd", grid=(n_c,), in_specs=[blk] * 6,
        out_specs=[blk, pl.BlockSpec((1, n_h, n, n), lambda c: (c, 0, 0, 0)),
                   pl.BlockSpec((1, n_h, t, t), lambda c: (c, 0, 0, 0))],
        out_shape=[_sds((n_h, L, n)), _sds((n_c, n_h, n, n)), _sds((n_c, n_h, t, t))],
        scratch_shapes=[pltpu.VMEM((n_h, n, n), F32)],
        compiler_params=_params("arbitrary"),
    )(*ops)


def _rwkv_scan_bwd(ops, states, invs, dy):
    n_h, L, n = ops[0].shape
    t = RWKV_CHUNK
    n_c = L // t

    def body(r_ref, w_ref, k_ref, v_ref, a_ref, b_ref, st_ref, inv_ref, dy_ref, dr, dw, dk, dv, da, db, dst):
        @pl.when(pl.program_id(0) == 0)
        def _():
            dst[...] = jnp.zeros_like(dst)

        inv = inv_ref[0]
        _, vjp = jax.vjp(lambda *a: _rwkv_chunk(*a, inv=inv)[:2], st_ref[0], r_ref[...], w_ref[...], k_ref[...],
                         v_ref[...], a_ref[...], b_ref[...])
        g = vjp((dy_ref[...], dst[...]))
        dst[...] = g[0]
        for out, val in zip((dr, dw, dk, dv, da, db), g[1:]):
            out[...] = val

    blk = pl.BlockSpec((n_h, t, n), lambda c: (0, n_c - 1 - c, 0))
    per_chunk = lambda m: pl.BlockSpec((1, n_h, m, m), lambda c: (n_c - 1 - c, 0, 0, 0))
    return pl.pallas_call(
        body, name="rwkv_scan_bwd", grid=(n_c,),
        in_specs=[blk] * 6 + [per_chunk(n), per_chunk(t), blk],
        out_specs=[blk] * 6, out_shape=[_sds((n_h, L, n))] * 6,
        scratch_shapes=[pltpu.VMEM((n_h, n, n), F32)],
        compiler_params=_params("arbitrary"),
    )(*ops, states, invs, dy)


def _post_fn(x, u, zs, zr, ysc, r, k2, v, y_ssm, d, glu_w, glu_b, ln_w, ln_b, r_k,
             wo_s5, wo_rwkv, gf, tgt, ee):
    y3 = _gelu(y_ssm + d * u)
    y_s5 = y3 * _sigmoid(mm_bf(y3, glu_w) + glu_b) * _silu(zs)
    mean = head_sum_split(ysc, ee) * (1.0 / HEAD)
    yc = ysc - mean
    var = head_sum(yc * yc, ee) * (1.0 / HEAD)
    gn = yc * lax.rsqrt(var + GN_EPS) * ln_w + ln_b
    bonus = head_sum(r * k2 * r_k, ee) * v
    y_rwkv = (gn + bonus) * _silu(zr)
    x2 = x + mm_bf(y_s5, wo_s5) + mm_bf(y_rwkv, wo_rwkv)
    err = _rms(x2, gf) - tgt
    return 0.5 * jnp.mean(err * err, axis=-1, keepdims=True)


def _post(x, u, zs, zr, ysc, r, k2, v, y_ssm, d, glu_w, glu_b, ln_w, ln_b, r_k, w_out, gf, tgt, ee, tt):
    L = x.shape[0]
    n_t = L // tt
    acc_shapes = [(1, D_S5), (D_S5, D_S5), (1, D_S5), (1, D_RWKV), (1, D_RWKV), (1, D_RWKV),
                  (D_MODEL, D_MODEL), (1, D_MODEL), (8, LANES)]

    def body(x_ref, u_ref, zs_ref, zr_ref, ysc_ref, r_ref, k2_ref, v_ref, yssm_ref,
             d_ref, gw_ref, gb_ref, lw_ref, lb_ref, rk_ref, wo_ref, gf_ref, tgt_ref, ee_ref,
             dx_o, du_o, dzs_o, dzr_o, dysc_o, dr_o, dk2_o, dv_o, dyssm_o,
             dd_o, dgw_o, dgb_o, dlw_o, dlb_o, drk_o, dwo_o, dgf_o, loss_o,
             dd, dgw, dgb, dlw, dlb, drk, dwo, dgf, loss):
        i = pl.program_id(0)
        accs = (dd, dgw, dgb, dlw, dlb, drk, dwo, dgf, loss)

        @pl.when(i == 0)
        def _():
            for acc in accs:
                acc[...] = jnp.zeros_like(acc)

        args = (x_ref[...], u_ref[...], zs_ref[...], zr_ref[...],
                _load_heads(ysc_ref), _load_heads(r_ref), _load_heads(k2_ref), _load_heads(v_ref), yssm_ref[...],
                d_ref[...], gw_ref[...], gb_ref[...], lw_ref[...], lb_ref[...], rk_ref[...],
                wo_ref[0:D_S5, :], wo_ref[D_S5:D_MODEL, :], gf_ref[...])
        rows, vjp = jax.vjp(lambda *a: _post_fn(*a, tgt_ref[...], ee_ref[...]), *args)
        g = vjp(jnp.ones_like(rows))
        for out, val in zip((dx_o, du_o, dzs_o, dzr_o), g[0:4]):
            out[...] = val
        _store_heads(dysc_o, g[4])
        for out, val in zip((dr_o, dk2_o, dv_o, dyssm_o), g[5:9]):
            out[...] = val
        for acc, val in zip((dd, dgw, dgb, dlw, dlb, drk), g[9:15]):
            acc[...] += val
        dwo[0:D_S5, :] += g[15]
        dwo[D_S5:D_MODEL, :] += g[16]
        dgf[...] += g[17]
        loss[...] += jnp.broadcast_to(jnp.sum(rows, axis=0, keepdims=True), loss.shape)

        @pl.when(i == n_t - 1)
        def _():
            for acc, out in zip(accs, (dd_o, dgw_o, dgb_o, dlw_o, dlb_o, drk_o, dwo_o, dgf_o, loss_o)):
                pltpu.sync_copy(acc, out)

    row = lambda n: pl.BlockSpec((tt, n), lambda i: (i, 0))
    in_specs = ([row(D_MODEL)] + [row(512)] * 3 + [_head_spec(tt)] * 4 + [row(D_S5)]
                + [_const_spec(s) for s in [(1, D_S5), (D_S5, D_S5), (1, D_S5), (1, D_RWKV), (1, D_RWKV), (1, D_RWKV),
                                            (D_MODEL, D_MODEL), (1, D_MODEL)]]
                + [row(D_MODEL), _const_spec((D_RWKV, D_RWKV))])
    out_rows = [D_MODEL] + [512] * 3 + [None] + [512] * 4
    return pl.pallas_call(
        body, name="post_fwd_bwd", grid=(n_t,), in_specs=in_specs,
        out_specs=[row(n) if n else _head_spec(tt) for n in out_rows] + [_ANY] * len(acc_shapes),
        out_shape=([_sds((L, n)) if n else _sds((N_HEADS, L, HEAD)) for n in out_rows]
                   + [_sds(s) for s in acc_shapes]),
        scratch_shapes=[pltpu.VMEM(s, F32) for s in acc_shapes],
        compiler_params=_params("arbitrary"),
    )(x, u, zs, zr, ysc, r, k2, v, y_ssm, d, glu_w, glu_b, ln_w, ln_b, r_k, w_out, gf, tgt, ee)


def _s5_bwd(u, du_direct, dy, s_re, s_im, b4_re, b4_im, c4_re, c4_im, pow_r, pow_i, tt):
    L = u.shape[0]
    n_t = L // tt
    acc_shapes = ([(S5_BLOCKS, LANES, 512)] * 2 + [(S5_BLOCKS, 512, LANES)] * 2 + [(1, N_STATE)] * 2)

    def body(u_ref, dud_ref, dy_ref, sre_ref, sim_ref, pre_ref, pim_ref, bre_ref, bim_ref, cre_ref, cim_ref,
             pr_ref, pi_ref, du_o, dbre_o, dbim_o, dcre_o, dcim_o, dlr_o, dli_o,
             dbre, dbim, dcre, dcim, dlr, dli, gre, gim, car_r, car_i):
        i = pl.program_id(0)

        @pl.when(i == 0)
        def _():
            for acc in (dbre, dbim, dcre, dcim, dlr, dli, car_r, car_i):
                acc[...] = jnp.zeros_like(acc)

        uv = u_ref[...]
        dyv = dy_ref[...]
        blocks = [slice(q * 512, (q + 1) * 512) for q in range(S5_BLOCKS)]
        lanes = [slice(q * LANES, (q + 1) * LANES) for q in range(S5_BLOCKS)]
        for q in range(S5_BLOCKS):
            gre[:, blocks[q]] = _dot_bf(dyv[:, lanes[q]], cre_ref[q], ((1,), (1,)))
            gim[:, blocks[q]] = -_dot_bf(dyv[:, lanes[q]], cim_ref[q], ((1,), (1,)))
        _tile_scan(gre, gim, pr_ref, pi_ref, car_r, car_i, reverse=True)
        for q in range(S5_BLOCKS):
            gr = gre[:, blocks[q]]
            gi = gim[:, blocks[q]]
            sr = sre_ref[:, blocks[q]]
            si = sim_ref[:, blocks[q]]
            du_o[:, lanes[q]] = (dud_ref[:, lanes[q]] + _dot_bf(gr, bre_ref[q], ((1,), (1,)))
                                 + _dot_bf(gi, bim_ref[q], ((1,), (1,))))
            dbre[q] += _dot_bf(uv[:, lanes[q]], gr, ((0,), (0,)))
            dbim[q] += _dot_bf(uv[:, lanes[q]], gi, ((0,), (0,)))
            dcre[q] += _dot_bf(sr, dyv[:, lanes[q]], ((0,), (0,)))
            dcim[q] -= _dot_bf(si, dyv[:, lanes[q]], ((0,), (0,)))
            rid = lax.broadcasted_iota(jnp.int32, sr.shape, 0)
            first = i == n_t - 1
            prev_r = jnp.where(first, 0.0, pre_ref[7:8, blocks[q]])
            prev_i = jnp.where(first, 0.0, pim_ref[7:8, blocks[q]])
            pr = jnp.where(rid == 0, jnp.broadcast_to(prev_r, sr.shape), pltpu.roll(sr, 1, axis=0))
            pi_ = jnp.where(rid == 0, jnp.broadcast_to(prev_i, si.shape), pltpu.roll(si, 1, axis=0))
            dlr[:, blocks[q]] += jnp.sum(pr * gr + pi_ * gi, axis=0, keepdims=True)
            dli[:, blocks[q]] += jnp.sum(pr * gi - pi_ * gr, axis=0, keepdims=True)

        @pl.when(i == n_t - 1)
        def _():
            for acc, out in zip((dbre, dbim, dcre, dcim, dlr, dli), (dbre_o, dbim_o, dcre_o, dcim_o, dlr_o, dli_o)):
                out[...] = acc[...]

    row = lambda n: pl.BlockSpec((tt, n), lambda i: (n_t - 1 - i, 0))
    prev = pl.BlockSpec((8, N_STATE), lambda i: (jnp.maximum((n_t - 1 - i) * (tt // 8) - 1, 0), 0))
    return pl.pallas_call(
        body, name="s5_bwd", grid=(n_t,),
        in_specs=[row(D_S5)] * 3 + [row(N_STATE)] * 2 + [prev] * 2
        + [_const_spec((S5_BLOCKS, LANES, 512))] * 2 + [_const_spec((S5_BLOCKS, 512, LANES))] * 2
        + [_const_spec((8, N_STATE))] * 2,
        out_specs=[row(D_S5)] + [_acc_spec(s) for s in acc_shapes],
        out_shape=[_sds((L, D_S5))] + [_sds(s) for s in acc_shapes],
        scratch_shapes=[pltpu.VMEM(s, F32) for s in acc_shapes] + [pltpu.VMEM((tt, N_STATE), F32)] * 2
        + [pltpu.VMEM((8, N_STATE), F32)] * 2,
        compiler_params=_params("arbitrary"),
    )(u, du_direct, dy, s_re, s_im, s_re, s_im, b4_re, b4_im, c4_re, c4_im, pow_r, pow_i)


def _bwd_in(x, norm_g, w_in_bf, mu, dx2, du, dzs, drws, dzr, tt):
    L = x.shape[0]
    n_t = L // tt

    def body(x_ref, g_ref, w_ref, mu_ref, dx2_ref, du_ref, dzs_ref, drws_ref, nxt_ref, dzr_ref,
             gx_o, dw_o, dg_o, dproj, dw, dg):
        i = pl.program_id(0)

        @pl.when(i == 0)
        def _():
            dw[...] = jnp.zeros_like(dw)
            dg[...] = jnp.zeros_like(dg)

        drws_v = drws_ref[...]
        rid = lax.broadcasted_iota(jnp.int32, drws_v.shape, 0)
        nxt_row = jnp.where(i == n_t - 1, 0.0, nxt_ref[0:1, :])
        nxt = jnp.where(rid == tt - 1, jnp.broadcast_to(nxt_row, drws_v.shape), pltpu.roll(drws_v, tt - 1, axis=0))
        muv = mu_ref[...]
        drw = drws_v * (1.0 - muv) + nxt * muv
        dproj[:, 0:D_S5] = du_ref[...].astype(BF16)
        dproj[:, D_S5:2 * D_S5] = dzs_ref[...].astype(BF16)
        dproj[:, 2 * D_S5:2 * D_S5 + D_SHIFT] = drw.astype(BF16)
        dproj[:, 2 * D_S5 + D_SHIFT:D_IN] = dzr_ref[...].astype(BF16)
        dh = _dot(dproj[...], w_ref[...], ((1,), (1,)), None)
        h, vjp = jax.vjp(_rms, x_ref[...], g_ref[...])
        dxh, dgv = vjp(dh)
        gx_o[...] = dx2_ref[...] + dxh
        dg[...] += dgv
        dw[...] += _dot(h.astype(BF16), dproj[...], ((0,), (0,)), None)

        @pl.when(i == n_t - 1)
        def _():
            dg_o[...] = dg[...]
            pltpu.sync_copy(dw, dw_o)

    row = lambda n: pl.BlockSpec((tt, n), lambda i: (i, 0))
    nxt = pl.BlockSpec((8, D_SHIFT), lambda i: (jnp.minimum((i + 1) * (tt // 8), L // 8 - 1), 0))
    return pl.pallas_call(
        body, name="bwd_in", grid=(n_t,),
        in_specs=[row(D_MODEL), _const_spec((1, D_MODEL)), _const_spec((D_MODEL, D_IN)), _const_spec((1, D_SHIFT)),
                  row(D_MODEL), row(D_S5), row(D_S5), row(D_SHIFT), nxt, row(D_RWKV)],
        out_specs=[row(D_MODEL), _ANY, _acc_spec((1, D_MODEL))],
        out_shape=[_sds((L, D_MODEL)), _sds((D_MODEL, D_IN)), _sds((1, D_MODEL))],
        scratch_shapes=[pltpu.VMEM((tt, D_IN), BF16), pltpu.VMEM((D_MODEL, D_IN), F32), pltpu.VMEM((1, D_MODEL), F32)],
        compiler_params=_params("arbitrary"),
    )(x, norm_g, w_in_bf, mu, dx2, du, dzs, drws, drws, dzr)


def _block_diag_b(bbar):
    bb = bbar.reshape(S5_GROUP, S5_BLOCKS, 8, S5_STATE)
    return jnp.einsum('hqgp,Gg->qGhgp', bb, jnp.eye(8, dtype=F32)).reshape(S5_BLOCKS, LANES, 512)


def _block_diag_b_t(db4):
    d = db4.reshape(S5_BLOCKS, 8, S5_GROUP, 8, S5_STATE)
    return jnp.einsum('qGhgp,Gg->hqgp', d, jnp.eye(8, dtype=F32)).reshape(S5_GROUP, N_STATE)


def _block_diag_c(c):
    cc = c.reshape(S5_BLOCKS, 8, S5_GROUP, S5_STATE)
    return jnp.einsum('qghp,gG->qgpGh', cc, jnp.eye(8, dtype=F32)).reshape(S5_BLOCKS, 512, LANES)


def _block_diag_c_t(dc4):
    d = dc4.reshape(S5_BLOCKS, 8, S5_STATE, 8, S5_GROUP)
    return jnp.einsum('qgpGh,gG->qghp', d, jnp.eye(8, dtype=F32)).reshape(S5_GROUPS, S5_GROUP, S5_STATE)


def _local_step(x, tgt, w):
    L = x.shape[0]
    tt = min(256, L)
    tp = min(128, L)
    ee = _head_sum_matrix()

    lam_re = w['s5_lam_re'].reshape(1, N_STATE)
    lam_im = w['s5_lam_im'].reshape(1, N_STATE)
    logdt = jnp.repeat(w['s5_log_dt'], S5_STATE).reshape(1, N_STATE)
    b_re_t = w['s5_b_re'].transpose(2, 0, 1).reshape(S5_GROUP, N_STATE)
    b_im_t = w['s5_b_im'].transpose(2, 0, 1).reshape(S5_GROUP, N_STATE)
    bbr, bbi, pow_r, pow_i = _s5_param_fwd(lam_re, lam_im, logdt, b_re_t, b_im_t)
    b4_re, b4_im = _block_diag_b(bbr), _block_diag_b(bbi)
    c4_re, c4_im = _block_diag_c(w['s5_c_re']), _block_diag_c(w['s5_c_im'])

    norm_g = w['norm_g'].reshape(1, D_MODEL)
    w_in_bf = w['w_in'].astype(BF16)
    u, zs, rw, zr = _fwd_in(x, norm_g, w_in_bf, tt)
    s_re, s_im, y_ssm = _s5_fwd(u, b4_re, b4_im, c4_re, c4_im, pow_r, pow_i, tt)

    row = lambda t: t.reshape(1, -1)
    zpad = jnp.zeros((HEAD, D_RWKV), F32)
    w2p = jnp.concatenate([w['rwkv_w2'], zpad], axis=0)
    a2p = jnp.concatenate([zpad, w['rwkv_a2']], axis=0)
    pre_consts = (row(w['rwkv_mu']), row(w['rwkv_w0']), w2p, row(w['rwkv_a0']), a2p,
                  row(w['rwkv_k_k']), row(w['rwkv_k_a']), ee)
    ops = _rwkv_pre_fwd(rw, *pre_consts, tt)
    ysc, states, invs = _rwkv_scan_fwd(ops)

    post = _post(x, u, zs, zr, ysc, ops[0], ops[2], ops[3], y_ssm,
                 row(w['s5_d']), w['s5_glu_w'], row(w['s5_glu_b']), row(w['rwkv_ln_w']), row(w['rwkv_ln_b']),
                 row(w['rwkv_r_k']), w['w_out'], row(w['final_g']), tgt, ee, tp)
    (dx2, du_d, dzs, dzr, dysc, dr_b, dk2_b, dv_b, dy_ssm,
     dd, dglu_w, dglu_b, dln_w, dln_b, dr_k, dw_out, dgf, loss) = post

    du, db4_re, db4_im, dc4_re, dc4_im, dlbr, dlbi = _s5_bwd(
        u, du_d, dy_ssm, s_re, s_im, b4_re, b4_im, c4_re, c4_im, pow_r[::-1], -pow_i[::-1], tt)
    group_ind = (jnp.arange(N_STATE)[:, None] // S5_STATE == jnp.arange(LANES)[None, :]).astype(F32)
    dlam_re, dlam_im, dlogdt, db_re_t, db_im_t = _s5_param_bwd(
        lam_re, lam_im, logdt, b_re_t, b_im_t, dlbr, dlbi, _block_diag_b_t(db4_re), _block_diag_b_t(db4_im), group_ind)

    cots = list(_rwkv_scan_bwd(ops, states, invs, dysc)) + [dr_b, dk2_b, dv_b]
    drws, dmu, dw0, dw2p, da0, da2p, dk_k, dk_a = _rwkv_pre_bwd(rw, *pre_consts, cots, tt)

    grad_x, dw_in, dnorm_g = _bwd_in(x, norm_g, w_in_bf, row(w['rwkv_mu']), dx2, du, dzs, drws, dzr, tt)

    unb = lambda t: t.reshape(S5_GROUP, S5_GROUPS, S5_STATE).transpose(1, 2, 0)
    grads = {
        'norm_g': dnorm_g.reshape(D_MODEL), 'w_in': dw_in,
        's5_lam_re': dlam_re.reshape(S5_GROUPS, S5_STATE), 's5_lam_im': dlam_im.reshape(S5_GROUPS, S5_STATE),
        's5_log_dt': dlogdt[0, :S5_GROUPS], 's5_b_re': unb(db_re_t), 's5_b_im': unb(db_im_t),
        's5_c_re': _block_diag_c_t(dc4_re), 's5_c_im': _block_diag_c_t(dc4_im),
        's5_d': dd.reshape(D_S5), 's5_glu_w': dglu_w, 's5_glu_b': dglu_b.reshape(D_S5),
        'rwkv_mu': dmu.reshape(-1), 'rwkv_w0': dw0.reshape(-1), 'rwkv_w2': dw2p[:HEAD], 'rwkv_a0': da0.reshape(-1),
        'rwkv_a2': da2p[HEAD:], 'rwkv_k_k': dk_k.reshape(-1), 'rwkv_k_a': dk_a.reshape(-1),
        'rwkv_r_k': dr_k.reshape(N_HEADS, HEAD), 'rwkv_ln_w': dln_w.reshape(-1), 'rwkv_ln_b': dln_b.reshape(-1),
        'w_out': dw_out, 'final_g': dgf.reshape(D_MODEL),
    }
    return loss, grad_x, grads


def _exchange(arrays, gather, axes, name):
    n = len(arrays)
    group = 2 ** len(axes)

    def body(*refs):
        send_refs, recv_refs = refs[:n], refs[n:2 * n]
        send_sems, recv_sems, local_sems = refs[2 * n:]
        pos = {ax: lax.axis_index(ax) for ax in ("x", "y", "c")}

        def index_of(p):
            idx = 0
            for ax in axes:
                idx = 2 * idx + p[ax]
            return idx

        me = index_of(pos)
        own, outs, arrivals = [], [], []
        for i, (send_ref, recv_ref) in enumerate(zip(send_refs, recv_refs)):
            def block_for(dev, send_ref=send_ref, whole=gather[i]):
                return send_ref if whole else send_ref.at[dev]

            own.append(pltpu.make_async_copy(block_for(me), recv_ref.at[me], local_sems.at[i]))
            own[-1].start()
            for k in range(1, group):
                peer = dict(pos)
                for bit, ax in enumerate(axes):
                    if (k >> bit) & 1:
                        peer[ax] = 1 - pos[ax]
                peer_idx = index_of(peer)
                sems = dict(send_sem=send_sems.at[i, k - 1], recv_sem=recv_sems.at[i, k - 1],
                            device_id=(peer["x"], peer["y"], peer["c"]), device_id_type=pl.DeviceIdType.MESH)
                outs.append(pltpu.make_async_remote_copy(src_ref=block_for(peer_idx), dst_ref=recv_ref.at[me], **sems))
                outs[-1].start()
                arrivals.append(
                    pltpu.make_async_remote_copy(src_ref=block_for(peer_idx), dst_ref=recv_ref.at[peer_idx], **sems))
        for copy in arrivals:
            copy.wait_recv()
        for copy in outs:
            copy.wait_send()
        for copy in own:
            copy.wait()

    return pl.pallas_call(
        body, name=name, in_specs=[_ANY] * n, out_specs=[_ANY] * n,
        out_shape=[jax.ShapeDtypeStruct(((group,) + a.shape) if whole else a.shape, a.dtype)
                   for a, whole in zip(arrays, gather)],
        scratch_shapes=[pltpu.SemaphoreType.DMA((n, group - 1)), pltpu.SemaphoreType.DMA((n, group - 1)),
                        pltpu.SemaphoreType.DMA((n,))],
        compiler_params=pltpu.CompilerParams(has_side_effects=True),
    )(*arrays)


def _sum_devices(ref):
    g = ref[0].astype(F32)
    for s in range(1, ref.shape[0]):
        g = g + ref[s].astype(F32)
    return g


def _adamw_math(g, w, m, v):
    m_new = ADAM_B1 * m + (1.0 - ADAM_B1) * g
    v_new = ADAM_B2 * v + (1.0 - ADAM_B2) * (g * g)
    m_hat = m_new / (1.0 - ADAM_B1 ** ADAM_STEP)
    v_hat = v_new / (1.0 - ADAM_B2 ** ADAM_STEP)
    return -ADAM_LR * (m_hat / (jnp.sqrt(v_hat) + ADAM_EPS) + ADAM_WD * w), m_new, v_new


def _adamw(gs, ws, ms, vs, reduce, name):
    n = len(ws)

    def body(*refs):
        g_refs, w_refs, m_refs, v_refs = (refs[j * n:(j + 1) * n] for j in range(4))
        outs = refs[4 * n:]
        for i in range(n):
            g = _sum_devices(g_refs[i]) if reduce else g_refs[i][...]
            res = _adamw_math(g, w_refs[i][...], m_refs[i][...], v_refs[i][...])
            for j, val in enumerate(((g,) if reduce else ()) + res):
                outs[j * n + i][...] = val

    return pl.pallas_call(
        body, name=name, out_shape=[_sds(w.shape) for w in ws] * (4 if reduce else 3),
        compiler_params=pltpu.CompilerParams(vmem_limit_bytes=VMEM_LIMIT),
    )(*gs, *ws, *ms, *vs)


def _sum_blocks(recv):
    def body(recv_ref, out_ref):
        out_ref[...] = _sum_devices(recv_ref)

    return pl.pallas_call(body, name="sum_small_grads", out_shape=_sds(recv.shape[1:]))(recv)


_WEIGHTS = [
    ('norm_g', (1, 1024), False), ('w_in', (1, 1024, 400), True), ('s5_lam_re', (1, 32, 64), False),
    ('s5_lam_im', (1, 32, 64), False), ('s5_log_dt', (1, 32), False), ('s5_b_re', (1, 32, 64, 16), False),
    ('s5_b_im', (1, 32, 64, 16), False), ('s5_c_re', (1, 32, 16, 64), False), ('s5_c_im', (1, 32, 16, 64), False),
    ('s5_d', (1, 512), False), ('s5_glu_w', (1, 64, 512), True), ('s5_glu_b', (1, 512), False),
    ('rwkv_mu', (1, 1664), False), ('rwkv_w0', (1, 512), False), ('rwkv_w2', (1, 64, 64), True),
    ('rwkv_a0', (1, 512), False), ('rwkv_a2', (1, 64, 64), True), ('rwkv_k_k', (1, 512), False),
    ('rwkv_k_a', (1, 512), False), ('rwkv_r_k', (1, 8, 64), False), ('rwkv_ln_w', (1, 512), False),
    ('rwkv_ln_b', (1, 512), False), ('w_out', (1, 128, 1024), True), ('final_g', (1024,), False),
]
_SHARDED = [(n, s) for n, s, sharded in _WEIGHTS if sharded]
_SMALL = [(n, s) for n, s, sharded in _WEIGHTS if not sharded]
_COLUMN_SHARDED = ('w_in', 'rwkv_w2', 'rwkv_a2')
_SMALL_ROWS = -(-sum(math.prod(s) for _, s in _SMALL) // (8 * LANES)) * 8


def _pack_small(grads):
    flat = [grads[n].reshape(-1) for n, _ in _SMALL]
    pad = _SMALL_ROWS * LANES - sum(f.size for f in flat)
    return jnp.concatenate(flat + [jnp.zeros((pad,), F32)]).reshape(_SMALL_ROWS, LANES)


def _unpack_small(packed):
    flat = packed.reshape(-1)
    out, off = {}, 0
    for n, s in _SMALL:
        size = math.prod(s)
        out[n] = flat[off:off + size].reshape(s)
        off += size
    return out


_BF16_OPERANDS = ('w_in', 's5_glu_w', 'w_out')


def _join_shards(name, blocks):
    _, rows, cols = blocks.shape
    if name in _COLUMN_SHARDED:
        return blocks.transpose(1, 0, 2).reshape(rows, N_DEV * cols)
    return blocks.reshape(N_DEV * rows, cols)


def _split_shards(name, full, shard_shape):
    rows, cols = shard_shape
    if name in _COLUMN_SHARDED:
        return full.reshape(rows, N_DEV, cols).transpose(1, 0, 2)
    return full.reshape(N_DEV, rows, cols)


def kernel(x, norm_g, w_in, s5_lam_re, s5_lam_im, s5_log_dt, s5_b_re, s5_b_im, s5_c_re, s5_c_im, s5_d, s5_glu_w, s5_glu_b, rwkv_mu, rwkv_w0, rwkv_w2, rwkv_a0, rwkv_a2, rwkv_k_k, rwkv_k_a, rwkv_r_k, rwkv_ln_w, rwkv_ln_b, w_out, final_g, loss_target, m_norm_g, m_w_in, m_s5_lam_re, m_s5_lam_im, m_s5_log_dt, m_s5_b_re, m_s5_b_im, m_s5_c_re, m_s5_c_im, m_s5_d, m_s5_glu_w, m_s5_glu_b, m_rwkv_mu, m_rwkv_w0, m_rwkv_w2, m_rwkv_a0, m_rwkv_a2, m_rwkv_k_k, m_rwkv_k_a, m_rwkv_r_k, m_rwkv_ln_w, m_rwkv_ln_b, m_w_out, m_final_g, v_norm_g, v_w_in, v_s5_lam_re, v_s5_lam_im, v_s5_log_dt, v_s5_b_re, v_s5_b_im, v_s5_c_re, v_s5_c_im, v_s5_d, v_s5_glu_w, v_s5_glu_b, v_rwkv_mu, v_rwkv_w0, v_rwkv_w2, v_rwkv_a0, v_rwkv_a2, v_rwkv_k_k, v_rwkv_k_a, v_rwkv_r_k, v_rwkv_ln_w, v_rwkv_ln_b, v_w_out, v_final_g):
    given = dict(locals())

    n_sh = len(_SHARDED)
    everyone = ("x", "y", "c")
    shards = [given[n][0].astype(BF16 if n in _BF16_OPERANDS else F32) for n, _ in _SHARDED]
    gathered = _exchange(shards, (True,) * n_sh, everyone, "gather_weights")
    local = {n: _join_shards(n, blocks).astype(F32 if n != 'w_in' else BF16)
             for (n, _), blocks in zip(_SHARDED, gathered)}
    local.update({n: (given[n][0] if len(s) > 1 else given[n]) for n, s in _SMALL})

    loss, grad_x, grads = _local_step(x[0], loss_target[0], local)

    blocks = [_split_shards(n, grads[n], s[1:]).astype(BF16) for n, s in _SHARDED]
    recv = _exchange(blocks + [_pack_small(grads)], (False,) * n_sh + (True,), everyone, "exchange_grads")

    result = {}
    for group, name in (([0], "adamw_w_in"), ([1, 2, 3, 4], "adamw_shards")):
        ns = [_SHARDED[i][0] for i in group]
        res = _adamw([recv[i] for i in group], [given[n][0] for n in ns], [given['m_' + n][0] for n in ns],
                     [given['v_' + n][0] for n in ns], True, name)
        for j, n in enumerate(ns):
            result[n] = [res[k * len(ns) + j][None] for k in range(4)]
    g_small = _unpack_small(_sum_blocks(recv[-1]))
    two_d = lambda t: t.reshape(1, -1) if t.ndim == 1 else t
    ns = [n for n, _ in _SMALL]
    res = _adamw([two_d(g_small[n]) for n in ns], [two_d(given[n]) for n in ns], [two_d(given['m_' + n]) for n in ns],
                 [two_d(given['v_' + n]) for n in ns], False, "adamw_small")
    for j, (n, s) in enumerate(_SMALL):
        result[n] = [g_small[n]] + [res[k * len(ns) + j].reshape(s) for k in range(3)]

    total = lax.psum(loss[0, 0], ("x", "y", "c"))
    outs = [total, grad_x[None]]
    for k in range(4):
        outs += [result[n][k] for n, _, _ in _WEIGHTS]
    return tuple(outs)
```

```python
import functools
import math

import jax
import jax.numpy as jnp
from jax import lax
from jax.experimental import pallas as pl
from jax.experimental.pallas import tpu as pltpu

F32 = jnp.float32
BF16 = jnp.bfloat16
HI = lax.Precision.HIGH

D_MODEL = 1024
D_S5 = 512
D_RWKV = 512
S5_GROUPS = 32
S5_GROUP = 16
S5_STATE = 64
N_STATE = S5_GROUPS * S5_STATE
N_HEADS = 8
HEAD = 64
D_SHIFT = 3 * D_RWKV + 128
D_IN = 2 * D_S5 + D_SHIFT + D_RWKV
NORM_EPS = 1e-6
GN_EPS = 64e-5
N_DEV = 8
LANES = 128
S5_BLOCKS = 4
RWKV_CHUNK = 64
VMEM_LIMIT = 56 * 1024 * 1024

ADAM_LR = 0.001
ADAM_B1 = 0.9
ADAM_B2 = 0.999
ADAM_EPS = 1e-08
ADAM_WD = 0.01
ADAM_STEP = 10


def _dot(a, b, dims, prec):
    return lax.dot_general(a, b, (dims, ((), ())), precision=prec, preferred_element_type=F32)


def _dot_bf(a, b, dims):
    return _dot(a.astype(BF16), b.astype(BF16), dims, None)


def _make_mm(cast, prec):
    @jax.custom_vjp
    def mm(a, b):
        return _dot(cast(a), cast(b), ((1,), (0,)), prec)

    def fwd(a, b):
        return mm(a, b), (a, b)

    def bwd(res, g):
        a, b = res
        return (_dot(cast(g), cast(b), ((1,), (1,)), prec), _dot(cast(a), cast(g), ((0,), (0,)), prec))

    mm.defvjp(fwd, bwd)
    return mm


mm_bf = _make_mm(lambda t: t.astype(BF16), None)


def _make_head_sum(split):
    def product(x, ee):
        hi = x.astype(BF16)
        out = _dot(hi, ee, ((1,), (0,)), None)
        if split:
            out = out + _dot((x - hi.astype(F32)).astype(BF16), ee, ((1,), (0,)), None)
        return out

    @jax.custom_vjp
    def head_sum(x, ee):
        return product(x, ee)

    def fwd(x, ee):
        return product(x, ee), ee

    def bwd(ee, g):
        return product(g, ee), jnp.zeros_like(ee)

    head_sum.defvjp(fwd, bwd)
    return head_sum


head_sum = _make_head_sum(False)
head_sum_split = _make_head_sum(True)


def _sigmoid(x):
    return 1.0 / (1.0 + jnp.exp(-x))


def _silu(x):
    return x * _sigmoid(x)


def _softplus(x):
    return jnp.maximum(x, 0.0) + jnp.log(1.0 + jnp.exp(-jnp.abs(x)))


def _gelu(x):
    return 0.5 * x * (1.0 + jnp.tanh(math.sqrt(2.0 / math.pi) * (x + 0.044715 * x * x * x)))


def _rms(x, g):
    return x * lax.rsqrt(jnp.mean(x * x, axis=-1, keepdims=True) + NORM_EPS) * g


def _const_spec(shape):
    nd = len(shape)
    return pl.BlockSpec(shape, lambda *_: (0,) * nd, pipeline_mode=pl.Buffered(1))


def _acc_spec(shape):
    nd = len(shape)
    return pl.BlockSpec(shape, lambda *_: (0,) * nd)


def _params(sem):
    return pltpu.CompilerParams(dimension_semantics=(sem,), vmem_limit_bytes=VMEM_LIMIT)


_ANY = pl.BlockSpec(memory_space=pl.ANY)


def _sds(shape):
    return jax.ShapeDtypeStruct(shape, F32)


def _head_sum_matrix():
    i = jnp.arange(D_RWKV) // HEAD
    return (i[:, None] == i[None, :]).astype(BF16)


def _s5_param_fn(lam_re, lam_im, logdt, b_re, b_im):
    dt = jnp.exp(logdt)
    mag = jnp.exp(lam_re * dt)
    ang = lam_im * dt
    lbr = mag * jnp.cos(ang)
    lbi = mag * jnp.sin(ang)
    nr = lbr - 1.0
    den = lam_re * lam_re + lam_im * lam_im
    cr = (nr * lam_re + lbi * lam_im) / den
    ci = (lbi * lam_re - nr * lam_im) / den
    return lbr, lbi, cr * b_re - ci * b_im, cr * b_im + ci * b_re


def _cmul(ar, ai, br, bi):
    return ar * br - ai * bi, ar * bi + ai * br


def _s5_param_fwd(lam_re, lam_im, logdt, b_re, b_im):
    def body(lr, li, ld, br, bi, o_br, o_bi, o_pr, o_pi):
        lbr, lbi, bbr, bbi = _s5_param_fn(lr[...], li[...], ld[...], br[...], bi[...])
        o_br[...] = bbr
        o_bi[...] = bbi
        rid = lax.broadcasted_iota(jnp.int32, (8, N_STATE), 0)
        pr, pi_ = lbr, lbi
        acc_r = jnp.broadcast_to(pr, (8, N_STATE))
        acc_i = jnp.broadcast_to(pi_, (8, N_STATE))
        for j in range(1, 8):
            pr, pi_ = _cmul(pr, pi_, lbr, lbi)
            acc_r = jnp.where(rid == j, jnp.broadcast_to(pr, (8, N_STATE)), acc_r)
            acc_i = jnp.where(rid == j, jnp.broadcast_to(pi_, (8, N_STATE)), acc_i)
        o_pr[...] = acc_r
        o_pi[...] = acc_i

    return pl.pallas_call(
        body, name="s5_param_fwd",
        out_shape=[_sds((S5_GROUP, N_STATE))] * 2 + [_sds((8, N_STATE))] * 2,
    )(lam_re, lam_im, logdt, b_re, b_im)


def _s5_param_bwd(lam_re, lam_im, logdt, b_re, b_im, d_lbr, d_lbi, d_bbr, d_bbi, group_ind):
    def body(lr, li, ld, br, bi, g0, g1, g2, g3, ind, o_lr, o_li, o_ld, o_br, o_bi):
        _, vjp = jax.vjp(_s5_param_fn, lr[...], li[...], ld[...], br[...], bi[...])
        d_lr, d_li, d_ld, d_br, d_bi = vjp((g0[...], g1[...], g2[...], g3[...]))
        o_lr[...] = d_lr
        o_li[...] = d_li
        o_ld[...] = _dot(jnp.broadcast_to(d_ld, (8, N_STATE)), ind[...], ((1,), (0,)), HI)
        o_br[...] = d_br
        o_bi[...] = d_bi

    return pl.pallas_call(
        body, name="s5_param_bwd",
        out_shape=[_sds((1, N_STATE))] * 2 + [_sds((8, LANES))] + [_sds((S5_GROUP, N_STATE))] * 2,
    )(lam_re, lam_im, logdt, b_re, b_im, d_lbr, d_lbi, d_bbr, d_bbi, group_ind)


def _fwd_in(x, norm_g, w_in_bf, tt):
    L = x.shape[0]

    def body(x_ref, g_ref, w_ref, u_ref, zs_ref, rw_ref, zr_ref):
        h = _rms(x_ref[...], g_ref[...])
        proj = jnp.dot(h.astype(BF16), w_ref[...], preferred_element_type=F32)
        u_ref[...] = proj[:, 0:D_S5]
        zs_ref[...] = proj[:, D_S5:2 * D_S5]
        rw_ref[...] = proj[:, 2 * D_S5:2 * D_S5 + D_SHIFT]
        zr_ref[...] = proj[:, 2 * D_S5 + D_SHIFT:D_IN]

    row = lambda n: pl.BlockSpec((tt, n), lambda i: (i, 0))
    return pl.pallas_call(
        body, name="fwd_in", grid=(L // tt,),
        in_specs=[row(D_MODEL), _const_spec((1, D_MODEL)), _const_spec((D_MODEL, D_IN))],
        out_specs=[row(D_S5), row(D_S5), row(D_SHIFT), row(D_RWKV)],
        out_shape=[_sds((L, D_S5)), _sds((L, D_S5)), _sds((L, D_SHIFT)), _sds((L, D_RWKV))],
        compiler_params=_params("parallel"),
    )(x, norm_g, w_in_bf)


S5_LANE_CHUNK = 512


def _tile_scan(re_ref, im_ref, pow_r_ref, pow_i_ref, carry_r_ref, carry_i_ref, reverse):
    t, n = re_ref.shape
    n_groups = t // 8
    ch = S5_LANE_CHUNK
    rid = lax.broadcasted_iota(jnp.int32, (8, ch), 0)
    for c in range(n // ch):
        cols = slice(c * ch, (c + 1) * ch)
        pow_r = pow_r_ref[:, cols]
        pow_i = pow_i_ref[:, cols]
        row = lambda tile, j: jnp.broadcast_to(tile[j:j + 1], (8, ch))
        levels = [(d, row(pow_r, 8 - d if reverse else d - 1), row(pow_i, 8 - d if reverse else d - 1))
                  for d in (1, 2, 4)]

        def group(g, carry):
            r0 = pl.multiple_of(((n_groups - 1 - g) if reverse else g) * 8, 8)
            xr = re_ref[pl.ds(r0, 8), cols]
            xi = im_ref[pl.ds(r0, 8), cols]
            for d, lr, li in levels:
                keep = (rid < 8 - d) if reverse else (rid >= d)
                shift = (8 - d) if reverse else d
                sr = jnp.where(keep, pltpu.roll(xr, shift, axis=0), 0.0)
                si = jnp.where(keep, pltpu.roll(xi, shift, axis=0), 0.0)
                mr, mi = _cmul(lr, li, sr, si)
                xr = xr + mr
                xi = xi + mi
            mr, mi = _cmul(pow_r, pow_i, carry[0], carry[1])
            xr = xr + mr
            xi = xi + mi
            re_ref[pl.ds(r0, 8), cols] = xr
            im_ref[pl.ds(r0, 8), cols] = xi
            last = 0 if reverse else 7
            return row(xr, last), row(xi, last)

        out = lax.fori_loop(0, n_groups, group, (carry_r_ref[:, cols], carry_i_ref[:, cols]))
        carry_r_ref[:, cols] = out[0]
        carry_i_ref[:, cols] = out[1]


def _s5_fwd(u, b4_re, b4_im, c4_re, c4_im, pow_r, pow_i, tt):
    L = u.shape[0]

    def body(u_ref, bre_ref, bim_ref, cre_ref, cim_ref, pr_ref, pi_ref, sre_o, sim_o, y_o, car_r, car_i):
        @pl.when(pl.program_id(0) == 0)
        def _():
            car_r[...] = jnp.zeros_like(car_r)
            car_i[...] = jnp.zeros_like(car_i)

        uv = u_ref[...]
        for q in range(S5_BLOCKS):
            uq = uv[:, q * LANES:(q + 1) * LANES]
            cols = slice(q * 512, (q + 1) * 512)
            sre_o[:, cols] = _dot_bf(uq, bre_ref[q], ((1,), (0,)))
            sim_o[:, cols] = _dot_bf(uq, bim_ref[q], ((1,), (0,)))
        _tile_scan(sre_o, sim_o, pr_ref, pi_ref, car_r, car_i, reverse=False)
        for q in range(S5_BLOCKS):
            cols = slice(q * 512, (q + 1) * 512)
            y_o[:, q * LANES:(q + 1) * LANES] = (_dot_bf(sre_o[:, cols], cre_ref[q], ((1,), (0,)))
                                                 - _dot_bf(sim_o[:, cols], cim_ref[q], ((1,), (0,))))

    row = lambda n: pl.BlockSpec((tt, n), lambda i: (i, 0))
    return pl.pallas_call(
        body, name="s5_fwd", grid=(L // tt,),
        in_specs=[row(D_S5)] + [_const_spec((S5_BLOCKS, LANES, 512))] * 2 + [_const_spec((S5_BLOCKS, 512, LANES))] * 2
        + [_const_spec((8, N_STATE))] * 2,
        out_specs=[row(N_STATE), row(N_STATE), row(D_S5)],
        out_shape=[_sds((L, N_STATE)), _sds((L, N_STATE)), _sds((L, D_S5))],
        scratch_shapes=[pltpu.VMEM((8, N_STATE), F32)] * 2,
        compiler_params=_params("arbitrary"),
    )(u, b4_re, b4_im, c4_re, c4_im, pow_r, pow_i)


def _rwkv_pre_fn(r, k, v, wa, w0, w2p, a0, a2p, k_k, k_a, ee):
    w = -_softplus(-(w0 + mm_bf(jnp.tanh(wa), w2p))) - 0.5
    logw = -jnp.exp(w)
    a = _sigmoid(a0 + mm_bf(wa, a2p))
    kkp = k * k_k
    kk = kkp / jnp.maximum(jnp.sqrt(head_sum(kkp * kkp, ee)), 1e-12)
    k2 = k * (1.0 + (a - 1.0) * k_a)
    return r, logw, k2, v, -kk, kk * a


def _head_spec(tt):
    return pl.BlockSpec((N_HEADS, tt, HEAD), lambda i: (0, i, 0))


def _load_heads(ref):
    return jnp.concatenate([ref[h] for h in range(N_HEADS)], axis=-1)


def _store_heads(ref, val):
    for h in range(N_HEADS):
        ref[h] = val[:, h * HEAD:(h + 1) * HEAD]


def _shifted(rw, prev_blk, first):
    rolled = pltpu.roll(rw, 1, axis=0)
    prev_row = jnp.where(first, 0.0, prev_blk[7:8, :])
    rid = lax.broadcasted_iota(jnp.int32, rw.shape, 0)
    return jnp.where(rid == 0, jnp.broadcast_to(prev_row, rw.shape), rolled)


def _split_rw(t):
    return t[:, 0:512], t[:, 512:1024], t[:, 1024:1536], t[:, 1536:1664]


def _rwkv_pre_specs(tt):
    row = pl.BlockSpec((tt, D_SHIFT), lambda i: (i, 0))
    prev = pl.BlockSpec((8, D_SHIFT), lambda i: (jnp.maximum(i * (tt // 8) - 1, 0), 0))
    consts = [_const_spec((1, D_SHIFT)), _const_spec((1, D_RWKV)), _const_spec((LANES, D_RWKV)),
              _const_spec((1, D_RWKV)), _const_spec((LANES, D_RWKV)), _const_spec((1, D_RWKV)),
              _const_spec((1, D_RWKV)), _const_spec((D_RWKV, D_RWKV))]
    return [row, prev] + consts


def _rwkv_pre_fwd(rw, mu, w0, w2p, a0, a2p, k_k, k_a, ee, tt):
    L = rw.shape[0]

    def body(rw_ref, prev_ref, mu_ref, w0_ref, w2_ref, a0_ref, a2_ref, kk_ref, ka_ref, ee_ref, *outs):
        rwv = rw_ref[...]
        rws = rwv + (_shifted(rwv, prev_ref[...], pl.program_id(0) == 0) - rwv) * mu_ref[...]
        res = _rwkv_pre_fn(*_split_rw(rws), w0_ref[...], w2_ref[...], a0_ref[...], a2_ref[...],
                           kk_ref[...], ka_ref[...], ee_ref[...])
        for o, val in zip(outs, res):
            _store_heads(o, val)

    return pl.pallas_call(
        body, name="rwkv_pre_fwd", grid=(L // tt,),
        in_specs=_rwkv_pre_specs(tt), out_specs=[_head_spec(tt)] * 6, out_shape=[_sds((N_HEADS, L, HEAD))] * 6,
        compiler_params=_params("parallel"),
    )(rw, rw, mu, w0, w2p, a0, a2p, k_k, k_a, ee)


def _rwkv_pre_bwd(rw, mu, w0, w2p, a0, a2p, k_k, k_a, ee, cots, tt):
    L = rw.shape[0]
    n_t = L // tt

    def body(rw_ref, prev_ref, mu_ref, w0_ref, w2_ref, a0_ref, a2_ref, kk_ref, ka_ref, ee_ref,
             c_r, c_w, c_k, c_v, c_a, c_b, cb_r, cb_k, cb_v,
             drws_ref, dmu_o, dw0_o, dw2_o, da0_o, da2_o, dkk_o, dka_o,
             dmu, dw0, dw2, da0, da2, dkk, dka):
        i = pl.program_id(0)
        accs = (dmu, dw0, dw2, da0, da2, dkk, dka)

        @pl.when(i == 0)
        def _():
            for acc in accs:
                acc[...] = jnp.zeros_like(acc)

        rwv = rw_ref[...]
        diff = _shifted(rwv, prev_ref[...], i == 0) - rwv
        rws = rwv + diff * mu_ref[...]
        consts = (w0_ref[...], w2_ref[...], a0_ref[...], a2_ref[...], kk_ref[...], ka_ref[...])
        _, vjp = jax.vjp(lambda *a: _rwkv_pre_fn(*a, ee_ref[...]), *_split_rw(rws), *consts)
        scan = [_load_heads(c) for c in (c_r, c_w, c_k, c_v, c_a, c_b)]
        g = vjp((scan[0] + cb_r[...], scan[1], scan[2] + cb_k[...], scan[3] + cb_v[...], scan[4], scan[5]))
        drws = jnp.concatenate(g[0:4], axis=-1)
        drws_ref[...] = drws
        dmu[...] += jnp.sum(drws * diff, axis=0, keepdims=True)
        for acc, val in zip(accs[1:], g[4:]):
            acc[...] += val

        @pl.when(i == n_t - 1)
        def _():
            for acc, out in zip(accs, (dmu_o, dw0_o, dw2_o, da0_o, da2_o, dkk_o, dka_o)):
                out[...] = acc[...]

    row = pl.BlockSpec((tt, D_RWKV), lambda i: (i, 0))
    shapes = [(1, D_SHIFT), (1, D_RWKV), (LANES, D_RWKV), (1, D_RWKV), (LANES, D_RWKV), (1, D_RWKV), (1, D_RWKV)]
    return pl.pallas_call(
        body, name="rwkv_pre_bwd", grid=(n_t,),
        in_specs=_rwkv_pre_specs(tt) + [_head_spec(tt)] * 6 + [row] * 3,
        out_specs=[pl.BlockSpec((tt, D_SHIFT), lambda i: (i, 0))] + [_acc_spec(s) for s in shapes],
        out_shape=[_sds((L, D_SHIFT))] + [_sds(s) for s in shapes],
        scratch_shapes=[pltpu.VMEM(s, F32) for s in shapes],
        compiler_params=_params("arbitrary"),
    )(rw, rw, mu, w0, w2p, a0, a2p, k_k, k_a, ee, *cots)


def _bmm(a, b):
    return lax.dot_general(a, b, (((2,), (1,)), ((0,), (0,))), precision=HI, preferred_element_type=F32)


def _bmm_nt(a, b):
    return lax.dot_general(a, b, (((2,), (2,)), ((0,), (0,))), precision=HI, preferred_element_type=F32)


def _bmm_tn(a, b):
    return lax.dot_general(a, b, (((1,), (1,)), ((0,), (0,))), precision=HI, preferred_element_type=F32)


def _bdot_bf(a, b, lhs_dim, rhs_dim):
    return lax.dot_general(a.astype(BF16), b.astype(BF16), (((lhs_dim,), (rhs_dim,)), ((0,), (0,))),
                           preferred_element_type=F32)


@jax.custom_vjp
def _bmm_bf(a, b):
    return _bdot_bf(a, b, 2, 1)


def _bmm_bf_fwd(a, b):
    return _bmm_bf(a, b), (a, b)


def _bmm_bf_bwd(res, g):
    a, b = res
    return _bdot_bf(g, b, 2, 2), _bdot_bf(a, g, 1, 1)


_bmm_bf.defvjp(_bmm_bf_fwd, _bmm_bf_bwd)


@jax.custom_vjp
def _bmm_tn_bf(a, b):
    return _bdot_bf(a, b, 1, 1)


def _bmm_tn_bf_fwd(a, b):
    return _bmm_tn_bf(a, b), (a, b)


def _bmm_tn_bf_bwd(res, g):
    a, b = res
    return _bdot_bf(b, g, 2, 2), _bdot_bf(a, g, 2, 1)


_bmm_tn_bf.defvjp(_bmm_tn_bf_fwd, _bmm_tn_bf_bwd)


def _unit_lower_inverse(a):
    t = a.shape[-1]
    ti = lax.broadcasted_iota(jnp.int32, (t, t), 0)
    si = lax.broadcasted_iota(jnp.int32, (t, t), 1)
    inv = jnp.where(ti == si, 1.0, 0.0)[None] + a
    pw = _bmm(a, a)
    n = 2
    while n < t:
        both = _bmm(jnp.concatenate([inv, pw], axis=1), pw)
        inv = inv + both[:, :t]
        pw = both[:, t:]
        n *= 2
    return inv


@jax.custom_vjp
def _solve_unit_lower(a, rhs, inv):
    return _bmm(inv, rhs)


def _solve_fwd(a, rhs, inv):
    u = _bmm(inv, rhs)
    return u, (inv, u)


def _solve_bwd(res, du):
    inv, u = res
    d_rhs = _bmm_tn(inv, du)
    return _bmm_nt(d_rhs, u), d_rhs, jnp.zeros_like(inv)


_solve_unit_lower.defvjp(_solve_fwd, _solve_bwd)


def _rwkv_chunk(st0, r, logw, k, v, a, b, inv=None):
    n_h, t, _ = r.shape
    ti = lax.broadcasted_iota(jnp.int32, (t, t), 0)
    si = lax.broadcasted_iota(jnp.int32, (t, t), 1)
    ones_tri = jnp.broadcast_to(jnp.where(ti >= si, 1.0, 0.0)[None], (n_h, t, t))
    log_p = _bmm(ones_tri, logw)
    p_in = jnp.exp(log_p)
    p_inv = jnp.exp(-log_p)
    at = a * jnp.exp(log_p - logw)
    rt = r * p_in
    bk = jnp.concatenate([b * p_inv, k * p_inv], axis=1)
    ri = lax.broadcasted_iota(jnp.int32, (2 * t, 2 * t), 0)
    ci = lax.broadcasted_iota(jnp.int32, (2 * t, 2 * t), 1)
    top_rows = ri < t
    diff = jnp.where(top_rows, ri, ri - t) - jnp.where(ci < t, ci, ci - t)
    mask = (diff >= jnp.where(top_rows, 1, 0))[None]
    m = jnp.where(mask, _bmm_nt(jnp.concatenate([at, rt], axis=1), bk), 0.0)
    top, bottom = m[:, :t], m[:, t:]
    a_ab = top[:, :, :t]
    if inv is None:
        inv = _unit_lower_inverse(a_ab)
    rhs = _bmm_bf(jnp.concatenate([at, top[:, :, t:]], axis=2), jnp.concatenate([st0, v], axis=1))
    u = _solve_unit_lower(a_ab, rhs, inv)
    y = _bmm_bf(jnp.concatenate([rt, bottom], axis=2), jnp.concatenate([st0, u, v], axis=1))
    p_end = jnp.swapaxes(p_in[:, t - 1:t, :], 1, 2)
    st1 = (st0 + _bmm_tn_bf(bk, jnp.concatenate([u, v], axis=1))) * p_end
    return y, st1, inv


def _rwkv_scan_fwd(ops):
    n_h, L, n = ops[0].shape
    t = RWKV_CHUNK
    n_c = L // t

    def body(r_ref, w_ref, k_ref, v_ref, a_ref, b_ref, y_ref, st_ref, inv_ref, st):
        @pl.when(pl.program_id(0) == 0)
        def _():
            st[...] = jnp.zeros_like(st)

        st0 = st[...]
        st_ref[0] = st0
        y, st1, inv = _rwkv_chunk(st0, r_ref[...], w_ref[...], k_ref[...], v_ref[...], a_ref[...], b_ref[...])
        y_ref[...] = y
        inv_ref[0] = inv
        st[...] = st1

    blk = pl.BlockSpec((n_h, t, n), lambda c: (0, c, 0))
    return pl.pallas_call(
        body, name="rwkv_scan_fwd", grid=(n_c,), in_specs=[blk] * 6,
        out_specs=[blk, pl.BlockSpec((1, n_h, n, n), lambda c: (c, 0, 0, 0)),
                   pl.BlockSpec((1, n_h, t, t), lambda c: (c, 0, 0, 0))],
        out_shape=[_sds((n_h, L, n)), _sds((n_c, n_h, n, n)), _sds((n_c, n_h, t, t))],
        scratch_shapes=[pltpu.VMEM((n_h, n, n), F32)],
        compiler_params=_params("arbitrary"),
    )(*ops)


def _rwkv_scan_bwd(ops, states, invs, dy):
    n_h, L, n = ops[0].shape
    t = RWKV_CHUNK
    n_c = L // t

    def body(r_ref, w_ref, k_ref, v_ref, a_ref, b_ref, st_ref, inv_ref, dy_ref, dr, dw, dk, dv, da, db, dst):
        @pl.when(pl.program_id(0) == 0)
        def _():
            dst[...] = jnp.zeros_like(dst)

        inv = inv_ref[0]
        _, vjp = jax.vjp(lambda *a: _rwkv_chunk(*a, inv=inv)[:2], st_ref[0], r_ref[...], w_ref[...], k_ref[...],
                         v_ref[...], a_ref[...], b_ref[...])
        g = vjp((dy_ref[...], dst[...]))
        dst[...] = g[0]
        for out, val in zip((dr, dw, dk, dv, da, db), g[1:]):
            out[...] = val

    blk = pl.BlockSpec((n_h, t, n), lambda c: (0, n_c - 1 - c, 0))
    per_chunk = lambda m: pl.BlockSpec((1, n_h, m, m), lambda c: (n_c - 1 - c, 0, 0, 0))
    return pl.pallas_call(
        body, name="rwkv_scan_bwd", grid=(n_c,),
        in_specs=[blk] * 6 + [per_chunk(n), per_chunk(t), blk],
        out_specs=[blk] * 6, out_shape=[_sds((n_h, L, n))] * 6,
        scratch_shapes=[pltpu.VMEM((n_h, n, n), F32)],
        compiler_params=_params("arbitrary"),
    )(*ops, states, invs, dy)


def _post_fn(x, u, zs, zr, ysc, r, k2, v, y_ssm, d, glu_w, glu_b, ln_w, ln_b, r_k,
             wo_s5, wo_rwkv, gf, tgt, ee):
    y3 = _gelu(y_ssm + d * u)
    y_s5 = y3 * _sigmoid(mm_bf(y3, glu_w) + glu_b) * _silu(zs)
    mean = head_sum_split(ysc, ee) * (1.0 / HEAD)
    yc = ysc - mean
    var = head_sum(yc * yc, ee) * (1.0 / HEAD)
    gn = yc * lax.rsqrt(var + GN_EPS) * ln_w + ln_b
    bonus = head_sum(r * k2 * r_k, ee) * v
    y_rwkv = (gn + bonus) * _silu(zr)
    x2 = x + mm_bf(y_s5, wo_s5) + mm_bf(y_rwkv, wo_rwkv)
    err = _rms(x2, gf) - tgt
    return 0.5 * jnp.mean(err * err, axis=-1, keepdims=True)


def _post(x, u, zs, zr, ysc, r, k2, v, y_ssm, d, glu_w, glu_b, ln_w, ln_b, r_k, w_out, gf, tgt, ee, tt):
    L = x.shape[0]
    n_t = L // tt
    acc_shapes = [(1, D_S5), (D_S5, D_S5), (1, D_S5), (1, D_RWKV), (1, D_RWKV), (1, D_RWKV),
                  (D_MODEL, D_MODEL), (1, D_MODEL), (8, LANES)]

    def body(x_ref, u_ref, zs_ref, zr_ref, ysc_ref, r_ref, k2_ref, v_ref, yssm_ref,
             d_ref, gw_ref, gb_ref, lw_ref, lb_ref, rk_ref, wo_ref, gf_ref, tgt_ref, ee_ref,
             dx_o, du_o, dzs_o, dzr_o, dysc_o, dr_o, dk2_o, dv_o, dyssm_o,
             dd_o, dgw_o, dgb_o, dlw_o, dlb_o, drk_o, dwo_o, dgf_o, loss_o,
             dd, dgw, dgb, dlw, dlb, drk, dwo, dgf, loss):
        i = pl.program_id(0)
        accs = (dd, dgw, dgb, dlw, dlb, drk, dwo, dgf, loss)

        @pl.when(i == 0)
        def _():
            for acc in accs:
                acc[...] = jnp.zeros_like(acc)

        args = (x_ref[...], u_ref[...], zs_ref[...], zr_ref[...],
                _load_heads(ysc_ref), _load_heads(r_ref), _load_heads(k2_ref), _load_heads(v_ref), yssm_ref[...],
                d_ref[...], gw_ref[...], gb_ref[...], lw_ref[...], lb_ref[...], rk_ref[...],
                wo_ref[0:D_S5, :], wo_ref[D_S5:D_MODEL, :], gf_ref[...])
        rows, vjp = jax.vjp(lambda *a: _post_fn(*a, tgt_ref[...], ee_ref[...]), *args)
        g = vjp(jnp.ones_like(rows))
        for out, val in zip((dx_o, du_o, dzs_o, dzr_o), g[0:4]):
            out[...] = val
        _store_heads(dysc_o, g[4])
        for out, val in zip((dr_o, dk2_o, dv_o, dyssm_o), g[5:9]):
            out[...] = val
        for acc, val in zip((dd, dgw, dgb, dlw, dlb, drk), g[9:15]):
            acc[...] += val
        dwo[0:D_S5, :] += g[15]
        dwo[D_S5:D_MODEL, :] += g[16]
        dgf[...] += g[17]
        loss[...] += jnp.broadcast_to(jnp.sum(rows, axis=0, keepdims=True), loss.shape)

        @pl.when(i == n_t - 1)
        def _():
            for acc, out in zip(accs, (dd_o, dgw_o, dgb_o, dlw_o, dlb_o, drk_o, dwo_o, dgf_o, loss_o)):
                pltpu.sync_copy(acc, out)

    row = lambda n: pl.BlockSpec((tt, n), lambda i: (i, 0))
    in_specs = ([row(D_MODEL)] + [row(512)] * 3 + [_head_spec(tt)] * 4 + [row(D_S5)]
                + [_const_spec(s) for s in [(1, D_S5), (D_S5, D_S5), (1, D_S5), (1, D_RWKV), (1, D_RWKV), (1, D_RWKV),
                                            (D_MODEL, D_MODEL), (1, D_MODEL)]]
                + [row(D_MODEL), _const_spec((D_RWKV, D_RWKV))])
    out_rows = [D_MODEL] + [512] * 3 + [None] + [512] * 4
    return pl.pallas_call(
        body, name="post_fwd_bwd", grid=(n_t,), in_specs=in_specs,
        out_specs=[row(n) if n else _head_spec(tt) for n in out_rows] + [_ANY] * len(acc_shapes),
        out_shape=([_sds((L, n)) if n else _sds((N_HEADS, L, HEAD)) for n in out_rows]
                   + [_sds(s) for s in acc_shapes]),
        scratch_shapes=[pltpu.VMEM(s, F32) for s in acc_shapes],
        compiler_params=_params("arbitrary"),
    )(x, u, zs, zr, ysc, r, k2, v, y_ssm, d, glu_w, glu_b, ln_w, ln_b, r_k, w_out, gf, tgt, ee)


def _s5_bwd(u, du_direct, dy, s_re, s_im, b4_re, b4_im, c4_re, c4_im, pow_r, pow_i, tt):
    L = u.shape[0]
    n_t = L // tt
    acc_shapes = ([(S5_BLOCKS, LANES, 512)] * 2 + [(S5_BLOCKS, 512, LANES)] * 2 + [(1, N_STATE)] * 2)

    def body(u_ref, dud_ref, dy_ref, sre_ref, sim_ref, pre_ref, pim_ref, bre_ref, bim_ref, cre_ref, cim_ref,
             pr_ref, pi_ref, du_o, dbre_o, dbim_o, dcre_o, dcim_o, dlr_o, dli_o,
             dbre, dbim, dcre, dcim, dlr, dli, gre, gim, car_r, car_i):
        i = pl.program_id(0)

        @pl.when(i == 0)
        def _():
            for acc in (dbre, dbim, dcre, dcim, dlr, dli, car_r, car_i):
                acc[...] = jnp.zeros_like(acc)

        uv = u_ref[...]
        dyv = dy_ref[...]
        blocks = [slice(q * 512, (q + 1) * 512) for q in range(S5_BLOCKS)]
        lanes = [slice(q * LANES, (q + 1) * LANES) for q in range(S5_BLOCKS)]
        for q in range(S5_BLOCKS):
            gre[:, blocks[q]] = _dot_bf(dyv[:, lanes[q]], cre_ref[q], ((1,), (1,)))
            gim[:, blocks[q]] = -_dot_bf(dyv[:, lanes[q]], cim_ref[q], ((1,), (1,)))
        _tile_scan(gre, gim, pr_ref, pi_ref, car_r, car_i, reverse=True)
        for q in range(S5_BLOCKS):
            gr = gre[:, blocks[q]]
            gi = gim[:, blocks[q]]
            sr = sre_ref[:, blocks[q]]
            si = sim_ref[:, blocks[q]]
            du_o[:, lanes[q]] = (dud_ref[:, lanes[q]] + _dot_bf(gr, bre_ref[q], ((1,), (1,)))
                                 + _dot_bf(gi, bim_ref[q], ((1,), (1,))))
            dbre[q] += _dot_bf(uv[:, lanes[q]], gr, ((0,), (0,)))
            dbim[q] += _dot_bf(uv[:, lanes[q]], gi, ((0,), (0,)))
            dcre[q] += _dot_bf(sr, dyv[:, lanes[q]], ((0,), (0,)))
            dcim[q] -= _dot_bf(si, dyv[:, lanes[q]], ((0,), (0,)))
            rid = lax.broadcasted_iota(jnp.int32, sr.shape, 0)
            first = i == n_t - 1
            prev_r = jnp.where(first, 0.0, pre_ref[7:8, blocks[q]])
            prev_i = jnp.where(first, 0.0, pim_ref[7:8, blocks[q]])
            pr = jnp.where(rid == 0, jnp.broadcast_to(prev_r, sr.shape), pltpu.roll(sr, 1, axis=0))
            pi_ = jnp.where(rid == 0, jnp.broadcast_to(prev_i, si.shape), pltpu.roll(si, 1, axis=0))
            dlr[:, blocks[q]] += jnp.sum(pr * gr + pi_ * gi, axis=0, keepdims=True)
            dli[:, blocks[q]] += jnp.sum(pr * gi - pi_ * gr, axis=0, keepdims=True)

        @pl.when(i == n_t - 1)
        def _():
            for acc, out in zip((dbre, dbim, dcre, dcim, dlr, dli), (dbre_o, dbim_o, dcre_o, dcim_o, dlr_o, dli_o)):
                out[...] = acc[...]

    row = lambda n: pl.BlockSpec((tt, n), lambda i: (n_t - 1 - i, 0))
    prev = pl.BlockSpec((8, N_STATE), lambda i: (jnp.maximum((n_t - 1 - i) * (tt // 8) - 1, 0), 0))
    return pl.pallas_call(
        body, name="s5_bwd", grid=(n_t,),
        in_specs=[row(D_S5)] * 3 + [row(N_STATE)] * 2 + [prev] * 2
        + [_const_spec((S5_BLOCKS, LANES, 512))] * 2 + [_const_spec((S5_BLOCKS, 512, LANES))] * 2
        + [_const_spec((8, N_STATE))] * 2,
        out_specs=[row(D_S5)] + [_acc_spec(s) for s in acc_shapes],
        out_shape=[_sds((L, D_S5))] + [_sds(s) for s in acc_shapes],
        scratch_shapes=[pltpu.VMEM(s, F32) for s in acc_shapes] + [pltpu.VMEM((tt, N_STATE), F32)] * 2
        + [pltpu.VMEM((8, N_STATE), F32)] * 2,
        compiler_params=_params("arbitrary"),
    )(u, du_direct, dy, s_re, s_im, s_re, s_im, b4_re, b4_im, c4_re, c4_im, pow_r, pow_i)


def _bwd_in(x, norm_g, w_in_bf, mu, dx2, du, dzs, drws, dzr, tt):
    L = x.shape[0]
    n_t = L // tt

    def body(x_ref, g_ref, w_ref, mu_ref, dx2_ref, du_ref, dzs_ref, drws_ref, nxt_ref, dzr_ref,
             gx_o, dw_o, dg_o, dproj, dw, dg):
        i = pl.program_id(0)

        @pl.when(i == 0)
        def _():
            dw[...] = jnp.zeros_like(dw)
            dg[...] = jnp.zeros_like(dg)

        drws_v = drws_ref[...]
        rid = lax.broadcasted_iota(jnp.int32, drws_v.shape, 0)
        nxt_row = jnp.where(i == n_t - 1, 0.0, nxt_ref[0:1, :])
        nxt = jnp.where(rid == tt - 1, jnp.broadcast_to(nxt_row, drws_v.shape), pltpu.roll(drws_v, tt - 1, axis=0))
        muv = mu_ref[...]
        drw = drws_v * (1.0 - muv) + nxt * muv
        dproj[:, 0:D_S5] = du_ref[...].astype(BF16)
        dproj[:, D_S5:2 * D_S5] = dzs_ref[...].astype(BF16)
        dproj[:, 2 * D_S5:2 * D_S5 + D_SHIFT] = drw.astype(BF16)
        dproj[:, 2 * D_S5 + D_SHIFT:D_IN] = dzr_ref[...].astype(BF16)
        dh = _dot(dproj[...], w_ref[...], ((1,), (1,)), None)
        h, vjp = jax.vjp(_rms, x_ref[...], g_ref[...])
        dxh, dgv = vjp(dh)
        gx_o[...] = dx2_ref[...] + dxh
        dg[...] += dgv
        dw[...] += _dot(h.astype(BF16), dproj[...], ((0,), (0,)), None)

        @pl.when(i == n_t - 1)
        def _():
            dg_o[...] = dg[...]
            pltpu.sync_copy(dw, dw_o)

    row = lambda n: pl.BlockSpec((tt, n), lambda i: (i, 0))
    nxt = pl.BlockSpec((8, D_SHIFT), lambda i: (jnp.minimum((i + 1) * (tt // 8), L // 8 - 1), 0))
    return pl.pallas_call(
        body, name="bwd_in", grid=(n_t,),
        in_specs=[row(D_MODEL), _const_spec((1, D_MODEL)), _const_spec((D_MODEL, D_IN)), _const_spec((1, D_SHIFT)),
                  row(D_MODEL), row(D_S5), row(D_S5), row(D_SHIFT), nxt, row(D_RWKV)],
        out_specs=[row(D_MODEL), _ANY, _acc_spec((1, D_MODEL))],
        out_shape=[_sds((L, D_MODEL)), _sds((D_MODEL, D_IN)), _sds((1, D_MODEL))],
        scratch_shapes=[pltpu.VMEM((tt, D_IN), BF16), pltpu.VMEM((D_MODEL, D_IN), F32), pltpu.VMEM((1, D_MODEL), F32)],
        compiler_params=_params("arbitrary"),
    )(x, norm_g, w_in_bf, mu, dx2, du, dzs, drws, drws, dzr)


def _block_diag_b(bbar):
    bb = bbar.reshape(S5_GROUP, S5_BLOCKS, 8, S5_STATE)
    return jnp.einsum('hqgp,Gg->qGhgp', bb, jnp.eye(8, dtype=F32)).reshape(S5_BLOCKS, LANES, 512)


def _block_diag_b_t(db4):
    d = db4.reshape(S5_BLOCKS, 8, S5_GROUP, 8, S5_STATE)
    return jnp.einsum('qGhgp,Gg->hqgp', d, jnp.eye(8, dtype=F32)).reshape(S5_GROUP, N_STATE)


def _block_diag_c(c):
    cc = c.reshape(S5_BLOCKS, 8, S5_GROUP, S5_STATE)
    return jnp.einsum('qghp,gG->qgpGh', cc, jnp.eye(8, dtype=F32)).reshape(S5_BLOCKS, 512, LANES)


def _block_diag_c_t(dc4):
    d = dc4.reshape(S5_BLOCKS, 8, S5_STATE, 8, S5_GROUP)
    return jnp.einsum('qgpGh,gG->qghp', d, jnp.eye(8, dtype=F32)).reshape(S5_GROUPS, S5_GROUP, S5_STATE)


def _local_step(x, tgt, w):
    L = x.shape[0]
    tt = min(256, L)
    tp = min(128, L)
    ee = _head_sum_matrix()

    lam_re = w['s5_lam_re'].reshape(1, N_STATE)
    lam_im = w['s5_lam_im'].reshape(1, N_STATE)
    logdt = jnp.repeat(w['s5_log_dt'], S5_STATE).reshape(1, N_STATE)
    b_re_t = w['s5_b_re'].transpose(2, 0, 1).reshape(S5_GROUP, N_STATE)
    b_im_t = w['s5_b_im'].transpose(2, 0, 1).reshape(S5_GROUP, N_STATE)
    bbr, bbi, pow_r, pow_i = _s5_param_fwd(lam_re, lam_im, logdt, b_re_t, b_im_t)
    b4_re, b4_im = _block_diag_b(bbr), _block_diag_b(bbi)
    c4_re, c4_im = _block_diag_c(w['s5_c_re']), _block_diag_c(w['s5_c_im'])

    norm_g = w['norm_g'].reshape(1, D_MODEL)
    w_in_bf = w['w_in'].astype(BF16)
    u, zs, rw, zr = _fwd_in(x, norm_g, w_in_bf, tt)
    s_re, s_im, y_ssm = _s5_fwd(u, b4_re, b4_im, c4_re, c4_im, pow_r, pow_i, tt)

    row = lambda t: t.reshape(1, -1)
    zpad = jnp.zeros((HEAD, D_RWKV), F32)
    w2p = jnp.concatenate([w['rwkv_w2'], zpad], axis=0)
    a2p = jnp.concatenate([zpad, w['rwkv_a2']], axis=0)
    pre_consts = (row(w['rwkv_mu']), row(w['rwkv_w0']), w2p, row(w['rwkv_a0']), a2p,
                  row(w['rwkv_k_k']), row(w['rwkv_k_a']), ee)
    ops = _rwkv_pre_fwd(rw, *pre_consts, tt)
    ysc, states, invs = _rwkv_scan_fwd(ops)

    post = _post(x, u, zs, zr, ysc, ops[0], ops[2], ops[3], y_ssm,
                 row(w['s5_d']), w['s5_glu_w'], row(w['s5_glu_b']), row(w['rwkv_ln_w']), row(w['rwkv_ln_b']),
                 row(w['rwkv_r_k']), w['w_out'], row(w['final_g']), tgt, ee, tp)
    (dx2, du_d, dzs, dzr, dysc, dr_b, dk2_b, dv_b, dy_ssm,
     dd, dglu_w, dglu_b, dln_w, dln_b, dr_k, dw_out, dgf, loss) = post

    du, db4_re, db4_im, dc4_re, dc4_im, dlbr, dlbi = _s5_bwd(
        u, du_d, dy_ssm, s_re, s_im, b4_re, b4_im, c4_re, c4_im, pow_r[::-1], -pow_i[::-1], tt)
    group_ind = (jnp.arange(N_STATE)[:, None] // S5_STATE == jnp.arange(LANES)[None, :]).astype(F32)
    dlam_re, dlam_im, dlogdt, db_re_t, db_im_t = _s5_param_bwd(
        lam_re, lam_im, logdt, b_re_t, b_im_t, dlbr, dlbi, _block_diag_b_t(db4_re), _block_diag_b_t(db4_im), group_ind)

    cots = list(_rwkv_scan_bwd(ops, states, invs, dysc)) + [dr_b, dk2_b, dv_b]
    drws, dmu, dw0, dw2p, da0, da2p, dk_k, dk_a = _rwkv_pre_bwd(rw, *pre_consts, cots, tt)

    grad_x, dw_in, dnorm_g = _bwd_in(x, norm_g, w_in_bf, row(w['rwkv_mu']), dx2, du, dzs, drws, dzr, tt)

    unb = lambda t: t.reshape(S5_GROUP, S5_GROUPS, S5_STATE).transpose(1, 2, 0)
    grads = {
        'norm_g': dnorm_g.reshape(D_MODEL), 'w_in': dw_in,
        's5_lam_re': dlam_re.reshape(S5_GROUPS, S5_STATE), 's5_lam_im': dlam_im.reshape(S5_GROUPS, S5_STATE),
        's5_log_dt': dlogdt[0, :S5_GROUPS], 's5_b_re': unb(db_re_t), 's5_b_im': unb(db_im_t),
        's5_c_re': _block_diag_c_t(dc4_re), 's5_c_im': _block_diag_c_t(dc4_im),
        's5_d': dd.reshape(D_S5), 's5_glu_w': dglu_w, 's5_glu_b': dglu_b.reshape(D_S5),
        'rwkv_mu': dmu.reshape(-1), 'rwkv_w0': dw0.reshape(-1), 'rwkv_w2': dw2p[:HEAD], 'rwkv_a0': da0.reshape(-1),
        'rwkv_a2': da2p[HEAD:], 'rwkv_k_k': dk_k.reshape(-1), 'rwkv_k_a': dk_a.reshape(-1),
        'rwkv_r_k': dr_k.reshape(N_HEADS, HEAD), 'rwkv_ln_w': dln_w.reshape(-1), 'rwkv_ln_b': dln_b.reshape(-1),
        'w_out': dw_out, 'final_g': dgf.reshape(D_MODEL),
    }
    return loss, grad_x, grads


def _pair_gather(arrays, name):
    n = len(arrays)

    def body(*refs):
        srcs, outs, sends, lands = (refs[j * n:(j + 1) * n] for j in range(4))
        send_sems, recv_sems, local_sems = refs[4 * n:]
        pos = (lax.axis_index("x"), lax.axis_index("y"), lax.axis_index("c"))
        me = pos[2]
        sibling = (pos[0], pos[1], 1 - me)
        stage = [pltpu.make_async_copy(srcs[i], sends[i], local_sems.at[i, 0]) for i in range(n)]
        own = [pltpu.make_async_copy(srcs[i], outs[i].at[me], local_sems.at[i, 1]) for i in range(n)]
        for copy in stage + own:
            copy.start()
        remote = []
        for i in range(n):
            stage[i].wait()
            remote.append(pltpu.make_async_remote_copy(
                src_ref=sends[i], dst_ref=lands[i], send_sem=send_sems.at[i], recv_sem=recv_sems.at[i],
                device_id=sibling, device_id_type=pl.DeviceIdType.MESH))
            remote[-1].start()
        unload = []
        for i in range(n):
            remote[i].wait_recv()
            unload.append(pltpu.make_async_copy(lands[i], outs[i].at[1 - me], local_sems.at[i, 2]))
            unload[-1].start()
        for i in range(n):
            remote[i].wait_send()
            own[i].wait()
            unload[i].wait()

    return pl.pallas_call(
        body, name=name, in_specs=[_ANY] * n, out_specs=[_ANY] * n,
        out_shape=[jax.ShapeDtypeStruct((2,) + a.shape, a.dtype) for a in arrays],
        scratch_shapes=[pltpu.VMEM(a.shape, a.dtype) for a in arrays] * 2
        + [pltpu.SemaphoreType.DMA((n,)), pltpu.SemaphoreType.DMA((n,)), pltpu.SemaphoreType.DMA((n, 3))],
        compiler_params=pltpu.CompilerParams(has_side_effects=True, vmem_limit_bytes=VMEM_LIMIT),
    )(*arrays)


def _exchange(arrays, gather, axes, name):
    n = len(arrays)
    group = 2 ** len(axes)

    def body(*refs):
        send_refs, recv_refs = refs[:n], refs[n:2 * n]
        send_sems, recv_sems, local_sems = refs[2 * n:]
        pos = {ax: lax.axis_index(ax) for ax in ("x", "y", "c")}

        def index_of(p):
            idx = 0
            for ax in axes:
                idx = 2 * idx + p[ax]
            return idx

        me = index_of(pos)
        own, outs, arrivals = [], [], []
        for i, (send_ref, recv_ref) in enumerate(zip(send_refs, recv_refs)):
            def block_for(dev, send_ref=send_ref, whole=gather[i]):
                return send_ref if whole else send_ref.at[dev]

            own.append(pltpu.make_async_copy(block_for(me), recv_ref.at[me], local_sems.at[i]))
            own[-1].start()
            for k in range(1, group):
                peer = dict(pos)
                for bit, ax in enumerate(axes):
                    if (k >> bit) & 1:
                        peer[ax] = 1 - pos[ax]
                peer_idx = index_of(peer)
                sems = dict(send_sem=send_sems.at[i, k - 1], recv_sem=recv_sems.at[i, k - 1],
                            device_id=(peer["x"], peer["y"], peer["c"]), device_id_type=pl.DeviceIdType.MESH)
                outs.append(pltpu.make_async_remote_copy(src_ref=block_for(peer_idx), dst_ref=recv_ref.at[me], **sems))
                outs[-1].start()
                arrivals.append(
                    pltpu.make_async_remote_copy(src_ref=block_for(peer_idx), dst_ref=recv_ref.at[peer_idx], **sems))
        for copy in arrivals:
            copy.wait_recv()
        for copy in outs:
            copy.wait_send()
        for copy in own:
            copy.wait()

    return pl.pallas_call(
        body, name=name, in_specs=[_ANY] * n, out_specs=[_ANY] * n,
        out_shape=[jax.ShapeDtypeStruct(((group,) + a.shape) if whole else a.shape, a.dtype)
                   for a, whole in zip(arrays, gather)],
        scratch_shapes=[pltpu.SemaphoreType.DMA((n, group - 1)), pltpu.SemaphoreType.DMA((n, group - 1)),
                        pltpu.SemaphoreType.DMA((n,))],
        compiler_params=pltpu.CompilerParams(has_side_effects=True),
    )(*arrays)


def _sum_devices(ref):
    g = ref[0].astype(F32)
    for s in range(1, ref.shape[0]):
        g = g + ref[s].astype(F32)
    return g


def _adamw_math(g, w, m, v):
    m_new = ADAM_B1 * m + (1.0 - ADAM_B1) * g
    v_new = ADAM_B2 * v + (1.0 - ADAM_B2) * (g * g)
    m_hat = m_new / (1.0 - ADAM_B1 ** ADAM_STEP)
    v_hat = v_new / (1.0 - ADAM_B2 ** ADAM_STEP)
    return -ADAM_LR * (m_hat / (jnp.sqrt(v_hat) + ADAM_EPS) + ADAM_WD * w), m_new, v_new


def _adamw(gs, ws, ms, vs, reduce, name):
    n = len(ws)

    def body(*refs):
        g_refs, w_refs, m_refs, v_refs = (refs[j * n:(j + 1) * n] for j in range(4))
        outs = refs[4 * n:]
        for i in range(n):
            g = _sum_devices(g_refs[i]) if reduce else g_refs[i][...]
            res = _adamw_math(g, w_refs[i][...], m_refs[i][...], v_refs[i][...])
            for j, val in enumerate(((g,) if reduce else ()) + res):
                outs[j * n + i][...] = val

    return pl.pallas_call(
        body, name=name, out_shape=[_sds(w.shape) for w in ws] * (4 if reduce else 3),
        compiler_params=pltpu.CompilerParams(vmem_limit_bytes=VMEM_LIMIT),
    )(*gs, *ws, *ms, *vs)


def _sum_blocks(recv):
    def body(recv_ref, out_ref):
        out_ref[...] = _sum_devices(recv_ref)

    return pl.pallas_call(body, name="sum_small_grads", out_shape=_sds(recv.shape[1:]))(recv)


_WEIGHTS = [
    ('norm_g', (1, 1024), False), ('w_in', (1, 1024, 400), True), ('s5_lam_re', (1, 32, 64), False),
    ('s5_lam_im', (1, 32, 64), False), ('s5_log_dt', (1, 32), False), ('s5_b_re', (1, 32, 64, 16), False),
    ('s5_b_im', (1, 32, 64, 16), False), ('s5_c_re', (1, 32, 16, 64), False), ('s5_c_im', (1, 32, 16, 64), False),
    ('s5_d', (1, 512), False), ('s5_glu_w', (1, 64, 512), True), ('s5_glu_b', (1, 512), False),
    ('rwkv_mu', (1, 1664), False), ('rwkv_w0', (1, 512), False), ('rwkv_w2', (1, 64, 64), True),
    ('rwkv_a0', (1, 512), False), ('rwkv_a2', (1, 64, 64), True), ('rwkv_k_k', (1, 512), False),
    ('rwkv_k_a', (1, 512), False), ('rwkv_r_k', (1, 8, 64), False), ('rwkv_ln_w', (1, 512), False),
    ('rwkv_ln_b', (1, 512), False), ('w_out', (1, 128, 1024), True), ('final_g', (1024,), False),
]
_SHARDED = [(n, s) for n, s, sharded in _WEIGHTS if sharded]
_SMALL = [(n, s) for n, s, sharded in _WEIGHTS if not sharded]
_COLUMN_SHARDED = ('w_in', 'rwkv_w2', 'rwkv_a2')
_SMALL_ROWS = -(-sum(math.prod(s) for _, s in _SMALL) // (8 * LANES)) * 8


def _pack_small(grads):
    flat = [grads[n].reshape(-1) for n, _ in _SMALL]
    pad = _SMALL_ROWS * LANES - sum(f.size for f in flat)
    return jnp.concatenate(flat + [jnp.zeros((pad,), F32)]).reshape(_SMALL_ROWS, LANES)


def _unpack_small(packed):
    flat = packed.reshape(-1)
    out, off = {}, 0
    for n, s in _SMALL:
        size = math.prod(s)
        out[n] = flat[off:off + size].reshape(s)
        off += size
    return out


_BF16_OPERANDS = ('w_in', 's5_glu_w', 'w_out')


def _join_shards(name, blocks):
    _, rows, cols = blocks.shape
    if name in _COLUMN_SHARDED:
        return blocks.transpose(1, 0, 2).reshape(rows, N_DEV * cols)
    return blocks.reshape(N_DEV * rows, cols)


def _split_shards(name, full, shard_shape):
    rows, cols = shard_shape
    if name in _COLUMN_SHARDED:
        return full.reshape(rows, N_DEV, cols).transpose(1, 0, 2)
    return full.reshape(N_DEV, rows, cols)


def kernel(x, norm_g, w_in, s5_lam_re, s5_lam_im, s5_log_dt, s5_b_re, s5_b_im, s5_c_re, s5_c_im, s5_d, s5_glu_w, s5_glu_b, rwkv_mu, rwkv_w0, rwkv_w2, rwkv_a0, rwkv_a2, rwkv_k_k, rwkv_k_a, rwkv_r_k, rwkv_ln_w, rwkv_ln_b, w_out, final_g, loss_target, m_norm_g, m_w_in, m_s5_lam_re, m_s5_lam_im, m_s5_log_dt, m_s5_b_re, m_s5_b_im, m_s5_c_re, m_s5_c_im, m_s5_d, m_s5_glu_w, m_s5_glu_b, m_rwkv_mu, m_rwkv_w0, m_rwkv_w2, m_rwkv_a0, m_rwkv_a2, m_rwkv_k_k, m_rwkv_k_a, m_rwkv_r_k, m_rwkv_ln_w, m_rwkv_ln_b, m_w_out, m_final_g, v_norm_g, v_w_in, v_s5_lam_re, v_s5_lam_im, v_s5_log_dt, v_s5_b_re, v_s5_b_im, v_s5_c_re, v_s5_c_im, v_s5_d, v_s5_glu_w, v_s5_glu_b, v_rwkv_mu, v_rwkv_w0, v_rwkv_w2, v_rwkv_a0, v_rwkv_a2, v_rwkv_k_k, v_rwkv_k_a, v_rwkv_r_k, v_rwkv_ln_w, v_rwkv_ln_b, v_w_out, v_final_g):
    given = dict(locals())

    n_sh = len(_SHARDED)
    everyone = ("x", "y", "c")
    shards = [given[n][0].astype(BF16 if n in _BF16_OPERANDS else F32) for n, _ in _SHARDED]
    gathered = _exchange(shards, (True,) * n_sh, ("x", "y"), "gather_weights_chips")
    gathered = _pair_gather(gathered, "gather_weights_cores")
    by_device = [g.transpose(1, 0, 2, 3).reshape((N_DEV,) + g.shape[2:]) for g in gathered]
    local = {n: _join_shards(n, blocks).astype(F32 if n != 'w_in' else BF16)
             for (n, _), blocks in zip(_SHARDED, by_device)}
    local.update({n: (given[n][0] if len(s) > 1 else given[n]) for n, s in _SMALL})

    loss, grad_x, grads = _local_step(x[0], loss_target[0], local)

    blocks = [_split_shards(n, grads[n], s[1:]).astype(BF16) for n, s in _SHARDED]
    recv = _exchange(blocks + [_pack_small(grads)], (False,) * n_sh + (True,), everyone, "exchange_grads")

    result = {}
    for group, name in (([0], "adamw_w_in"), ([1, 2, 3, 4], "adamw_shards")):
        ns = [_SHARDED[i][0] for i in group]
        res = _adamw([recv[i] for i in group], [given[n][0] for n in ns], [given['m_' + n][0] for n in ns],
                     [given['v_' + n][0] for n in ns], True, name)
        for j, n in enumerate(ns):
            result[n] = [res[k * len(ns) + j][None] for k in range(4)]
    g_small = _unpack_small(_sum_blocks(recv[-1]))
    two_d = lambda t: t.reshape(1, -1) if t.ndim == 1 else t
    ns = [n for n, _ in _SMALL]
    res = _adamw([two_d(g_small[n]) for n in ns], [two_d(given[n]) for n in ns], [two_d(given['m_' + n]) for n in ns],
                 [two_d(given['v_' + n]) for n in ns], False, "adamw_small")
    for j, (n, s) in enumerate(_SMALL):
        result[n] = [g_small[n]] + [res[k * len(ns) + j].reshape(s) for k in range(3)]

    total = lax.psum(loss[0, 0], ("x", "y", "c"))
    outs = [total, grad_x[None]]
    for k in range(4):
        outs += [result[n][k] for n, _, _ in _WEIGHTS]
    return tuple(outs)
```

```python
import functools
import math

import jax
import jax.numpy as jnp
from jax import lax
from jax.experimental import pallas as pl
from jax.experimental.pallas import tpu as pltpu

F32 = jnp.float32
BF16 = jnp.bfloat16
HI = lax.Precision.HIGH

D_MODEL = 1024
D_S5 = 512
D_RWKV = 512
S5_GROUPS = 32
S5_GROUP = 16
S5_STATE = 64
N_STATE = S5_GROUPS * S5_STATE
N_HEADS = 8
HEAD = 64
D_SHIFT = 3 * D_RWKV + 128
D_IN = 2 * D_S5 + D_SHIFT + D_RWKV
NORM_EPS = 1e-6
GN_EPS = 64e-5
N_DEV = 8
LANES = 128
S5_BLOCKS = 4
RWKV_CHUNK = 64
VMEM_LIMIT = 56 * 1024 * 1024

ADAM_LR = 0.001
ADAM_B1 = 0.9
ADAM_B2 = 0.999
ADAM_EPS = 1e-08
ADAM_WD = 0.01
ADAM_STEP = 10


def _dot(a, b, dims, prec):
    return lax.dot_general(a, b, (dims, ((), ())), precision=prec, preferred_element_type=F32)


def _dot_bf(a, b, dims):
    return _dot(a.astype(BF16), b.astype(BF16), dims, None)


def _make_mm(cast, prec):
    @jax.custom_vjp
    def mm(a, b):
        return _dot(cast(a), cast(b), ((1,), (0,)), prec)

    def fwd(a, b):
        return mm(a, b), (a, b)

    def bwd(res, g):
        a, b = res
        return (_dot(cast(g), cast(b), ((1,), (1,)), prec), _dot(cast(a), cast(g), ((0,), (0,)), prec))

    mm.defvjp(fwd, bwd)
    return mm


mm_bf = _make_mm(lambda t: t.astype(BF16), None)


def _make_head_sum(split):
    def product(x, ee):
        hi = x.astype(BF16)
        out = _dot(hi, ee, ((1,), (0,)), None)
        if split:
            out = out + _dot((x - hi.astype(F32)).astype(BF16), ee, ((1,), (0,)), None)
        return out

    @jax.custom_vjp
    def head_sum(x, ee):
        return product(x, ee)

    def fwd(x, ee):
        return product(x, ee), ee

    def bwd(ee, g):
        return product(g, ee), jnp.zeros_like(ee)

    head_sum.defvjp(fwd, bwd)
    return head_sum


head_sum = _make_head_sum(False)
head_sum_split = _make_head_sum(True)


def _sigmoid(x):
    return 1.0 / (1.0 + jnp.exp(-x))


def _silu(x):
    return x * _sigmoid(x)


def _softplus(x):
    return jnp.maximum(x, 0.0) + jnp.log(1.0 + jnp.exp(-jnp.abs(x)))


def _gelu(x):
    return 0.5 * x * (1.0 + jnp.tanh(math.sqrt(2.0 / math.pi) * (x + 0.044715 * x * x * x)))


def _rms(x, g):
    return x * lax.rsqrt(jnp.mean(x * x, axis=-1, keepdims=True) + NORM_EPS) * g


def _const_spec(shape):
    nd = len(shape)
    return pl.BlockSpec(shape, lambda *_: (0,) * nd, pipeline_mode=pl.Buffered(1))


def _acc_spec(shape):
    nd = len(shape)
    return pl.BlockSpec(shape, lambda *_: (0,) * nd)


def _params(sem):
    return pltpu.CompilerParams(dimension_semantics=(sem,), vmem_limit_bytes=VMEM_LIMIT)


_ANY = pl.BlockSpec(memory_space=pl.ANY)


def _sds(shape):
    return jax.ShapeDtypeStruct(shape, F32)


def _head_sum_matrix():
    i = jnp.arange(D_RWKV) // HEAD
    return (i[:, None] == i[None, :]).astype(BF16)


def _s5_param_fn(lam_re, lam_im, logdt, b_re, b_im):
    dt = jnp.exp(logdt)
    mag = jnp.exp(lam_re * dt)
    ang = lam_im * dt
    lbr = mag * jnp.cos(ang)
    lbi = mag * jnp.sin(ang)
    nr = lbr - 1.0
    den = lam_re * lam_re + lam_im * lam_im
    cr = (nr * lam_re + lbi * lam_im) / den
    ci = (lbi * lam_re - nr * lam_im) / den
    return lbr, lbi, cr * b_re - ci * b_im, cr * b_im + ci * b_re


def _cmul(ar, ai, br, bi):
    return ar * br - ai * bi, ar * bi + ai * br


def _s5_param_fwd(lam_re, lam_im, logdt, b_re, b_im):
    def body(lr, li, ld, br, bi, o_br, o_bi, o_pr, o_pi, o_qr, o_qi):
        lbr, lbi, bbr, bbi = _s5_param_fn(lr[...], li[...], ld[...], br[...], bi[...])
        o_br[...] = bbr
        o_bi[...] = bbi
        rid = lax.broadcasted_iota(jnp.int32, (8, N_STATE), 0)
        pr, pi_ = lbr, lbi
        fwd_r = rev_r = jnp.broadcast_to(pr, (8, N_STATE))
        fwd_i = rev_i = jnp.broadcast_to(pi_, (8, N_STATE))
        for j in range(1, 8):
            pr, pi_ = _cmul(pr, pi_, lbr, lbi)
            fwd_r = jnp.where(rid == j, jnp.broadcast_to(pr, (8, N_STATE)), fwd_r)
            fwd_i = jnp.where(rid == j, jnp.broadcast_to(pi_, (8, N_STATE)), fwd_i)
            rev_r = jnp.where(rid == 7 - j, jnp.broadcast_to(pr, (8, N_STATE)), rev_r)
            rev_i = jnp.where(rid == 7 - j, jnp.broadcast_to(pi_, (8, N_STATE)), rev_i)
        o_pr[...] = fwd_r
        o_pi[...] = fwd_i
        o_qr[...] = rev_r
        o_qi[...] = -rev_i

    return pl.pallas_call(
        body, name="s5_param_fwd",
        out_shape=[_sds((S5_GROUP, N_STATE))] * 2 + [_sds((8, N_STATE))] * 4,
    )(lam_re, lam_im, logdt, b_re, b_im)


def _s5_param_bwd(lam_re, lam_im, logdt, b_re, b_im, d_lbr, d_lbi, d_bbr, d_bbi, group_ind):
    def body(lr, li, ld, br, bi, g0, g1, g2, g3, ind, o_lr, o_li, o_ld, o_br, o_bi):
        _, vjp = jax.vjp(_s5_param_fn, lr[...], li[...], ld[...], br[...], bi[...])
        d_lr, d_li, d_ld, d_br, d_bi = vjp((g0[...], g1[...], g2[...], g3[...]))
        o_lr[...] = d_lr
        o_li[...] = d_li
        o_ld[...] = _dot(jnp.broadcast_to(d_ld, (8, N_STATE)), ind[...], ((1,), (0,)), HI)
        o_br[...] = d_br
        o_bi[...] = d_bi

    return pl.pallas_call(
        body, name="s5_param_bwd",
        out_shape=[_sds((1, N_STATE))] * 2 + [_sds((8, LANES))] + [_sds((S5_GROUP, N_STATE))] * 2,
    )(lam_re, lam_im, logdt, b_re, b_im, d_lbr, d_lbi, d_bbr, d_bbi, group_ind)


def _fwd_in(x, norm_g, w_in_bf, tt):
    L = x.shape[0]

    def body(x_ref, g_ref, w_ref, u_ref, zs_ref, rw_ref, zr_ref):
        h = _rms(x_ref[...], g_ref[...])
        proj = jnp.dot(h.astype(BF16), w_ref[...], preferred_element_type=F32)
        u_ref[...] = proj[:, 0:D_S5]
        zs_ref[...] = proj[:, D_S5:2 * D_S5]
        rw_ref[...] = proj[:, 2 * D_S5:2 * D_S5 + D_SHIFT]
        zr_ref[...] = proj[:, 2 * D_S5 + D_SHIFT:D_IN]

    row = lambda n: pl.BlockSpec((tt, n), lambda i: (i, 0))
    return pl.pallas_call(
        body, name="fwd_in", grid=(L // tt,),
        in_specs=[row(D_MODEL), _const_spec((1, D_MODEL)), _const_spec((D_MODEL, D_IN))],
        out_specs=[row(D_S5), row(D_S5), row(D_SHIFT), row(D_RWKV)],
        out_shape=[_sds((L, D_S5)), _sds((L, D_S5)), _sds((L, D_SHIFT)), _sds((L, D_RWKV))],
        compiler_params=_params("parallel"),
    )(x, norm_g, w_in_bf)


S5_LANE_CHUNK = 512


def _tile_scan(re_ref, im_ref, pow_r_ref, pow_i_ref, carry_r_ref, carry_i_ref, reverse):
    t, n = re_ref.shape
    n_groups = t // 8
    ch = S5_LANE_CHUNK
    rid = lax.broadcasted_iota(jnp.int32, (8, ch), 0)
    for c in range(n // ch):
        cols = slice(c * ch, (c + 1) * ch)
        pow_r = pow_r_ref[:, cols]
        pow_i = pow_i_ref[:, cols]
        row = lambda tile, j: jnp.broadcast_to(tile[j:j + 1], (8, ch))
        levels = [(d, row(pow_r, 8 - d if reverse else d - 1), row(pow_i, 8 - d if reverse else d - 1))
                  for d in (1, 2, 4)]

        def group(g, carry):
            r0 = pl.multiple_of(((n_groups - 1 - g) if reverse else g) * 8, 8)
            xr = re_ref[pl.ds(r0, 8), cols]
            xi = im_ref[pl.ds(r0, 8), cols]
            for d, lr, li in levels:
                keep = (rid < 8 - d) if reverse else (rid >= d)
                shift = (8 - d) if reverse else d
                sr = jnp.where(keep, pltpu.roll(xr, shift, axis=0), 0.0)
                si = jnp.where(keep, pltpu.roll(xi, shift, axis=0), 0.0)
                mr, mi = _cmul(lr, li, sr, si)
                xr = xr + mr
                xi = xi + mi
            mr, mi = _cmul(pow_r, pow_i, carry[0], carry[1])
            xr = xr + mr
            xi = xi + mi
            re_ref[pl.ds(r0, 8), cols] = xr
            im_ref[pl.ds(r0, 8), cols] = xi
            last = 0 if reverse else 7
            return row(xr, last), row(xi, last)

        out = lax.fori_loop(0, n_groups, group, (carry_r_ref[:, cols], carry_i_ref[:, cols]))
        carry_r_ref[:, cols] = out[0]
        carry_i_ref[:, cols] = out[1]


def _s5_fwd(u, b4_re, b4_im, c4_re, c4_im, pow_r, pow_i, tt):
    L = u.shape[0]

    def body(u_ref, bre_ref, bim_ref, cre_ref, cim_ref, pr_ref, pi_ref, sre_o, sim_o, y_o, car_r, car_i):
        @pl.when(pl.program_id(0) == 0)
        def _():
            car_r[...] = jnp.zeros_like(car_r)
            car_i[...] = jnp.zeros_like(car_i)

        uv = u_ref[...]
        for q in range(S5_BLOCKS):
            uq = uv[:, q * LANES:(q + 1) * LANES]
            cols = slice(q * 512, (q + 1) * 512)
            sre_o[:, cols] = _dot_bf(uq, bre_ref[q], ((1,), (0,)))
            sim_o[:, cols] = _dot_bf(uq, bim_ref[q], ((1,), (0,)))
        _tile_scan(sre_o, sim_o, pr_ref, pi_ref, car_r, car_i, reverse=False)
        for q in range(S5_BLOCKS):
            cols = slice(q * 512, (q + 1) * 512)
            y_o[:, q * LANES:(q + 1) * LANES] = (_dot_bf(sre_o[:, cols], cre_ref[q], ((1,), (0,)))
                                                 - _dot_bf(sim_o[:, cols], cim_ref[q], ((1,), (0,))))

    row = lambda n: pl.BlockSpec((tt, n), lambda i: (i, 0))
    return pl.pallas_call(
        body, name="s5_fwd", grid=(L // tt,),
        in_specs=[row(D_S5)] + [_const_spec((S5_BLOCKS, LANES, 512))] * 2 + [_const_spec((S5_BLOCKS, 512, LANES))] * 2
        + [_const_spec((8, N_STATE))] * 2,
        out_specs=[row(N_STATE), row(N_STATE), row(D_S5)],
        out_shape=[_sds((L, N_STATE)), _sds((L, N_STATE)), _sds((L, D_S5))],
        scratch_shapes=[pltpu.VMEM((8, N_STATE), F32)] * 2,
        compiler_params=_params("arbitrary"),
    )(u, b4_re, b4_im, c4_re, c4_im, pow_r, pow_i)


def _rwkv_pre_fn(r, k, v, wa, w0, w2p, a0, a2p, k_k, k_a, ee):
    w = -_softplus(-(w0 + mm_bf(jnp.tanh(wa), w2p))) - 0.5
    logw = -jnp.exp(w)
    a = _sigmoid(a0 + mm_bf(wa, a2p))
    kkp = k * k_k
    kk = kkp / jnp.maximum(jnp.sqrt(head_sum(kkp * kkp, ee)), 1e-12)
    k2 = k * (1.0 + (a - 1.0) * k_a)
    return r, logw, k2, v, -kk, kk * a


def _head_spec(tt):
    return pl.BlockSpec((N_HEADS, tt, HEAD), lambda i: (0, i, 0))


def _load_heads(ref):
    return jnp.concatenate([ref[h] for h in range(N_HEADS)], axis=-1)


def _store_heads(ref, val):
    for h in range(N_HEADS):
        ref[h] = val[:, h * HEAD:(h + 1) * HEAD]


def _shifted(rw, prev_blk, first):
    rolled = pltpu.roll(rw, 1, axis=0)
    prev_row = jnp.where(first, 0.0, prev_blk[7:8, :])
    rid = lax.broadcasted_iota(jnp.int32, rw.shape, 0)
    return jnp.where(rid == 0, jnp.broadcast_to(prev_row, rw.shape), rolled)


def _split_rw(t):
    return t[:, 0:512], t[:, 512:1024], t[:, 1024:1536], t[:, 1536:1664]


def _rwkv_pre_specs(tt):
    row = pl.BlockSpec((tt, D_SHIFT), lambda i: (i, 0))
    prev = pl.BlockSpec((8, D_SHIFT), lambda i: (jnp.maximum(i * (tt // 8) - 1, 0), 0))
    consts = [_const_spec((1, D_SHIFT)), _const_spec((1, D_RWKV)), _const_spec((LANES, D_RWKV)),
              _const_spec((1, D_RWKV)), _const_spec((LANES, D_RWKV)), _const_spec((1, D_RWKV)),
              _const_spec((1, D_RWKV)), _const_spec((D_RWKV, D_RWKV))]
    return [row, prev] + consts


def _rwkv_pre_fwd(rw, mu, w0, w2p, a0, a2p, k_k, k_a, ee, tt):
    L = rw.shape[0]

    def body(rw_ref, prev_ref, mu_ref, w0_ref, w2_ref, a0_ref, a2_ref, kk_ref, ka_ref, ee_ref, *outs):
        rwv = rw_ref[...]
        rws = rwv + (_shifted(rwv, prev_ref[...], pl.program_id(0) == 0) - rwv) * mu_ref[...]
        res = _rwkv_pre_fn(*_split_rw(rws), w0_ref[...], w2_ref[...], a0_ref[...], a2_ref[...],
                           kk_ref[...], ka_ref[...], ee_ref[...])
        for o, val in zip(outs, res):
            _store_heads(o, val)

    return pl.pallas_call(
        body, name="rwkv_pre_fwd", grid=(L // tt,),
        in_specs=_rwkv_pre_specs(tt), out_specs=[_head_spec(tt)] * 6, out_shape=[_sds((N_HEADS, L, HEAD))] * 6,
        compiler_params=_params("parallel"),
    )(rw, rw, mu, w0, w2p, a0, a2p, k_k, k_a, ee)


def _rwkv_pre_bwd(rw, mu, w0, w2p, a0, a2p, k_k, k_a, ee, cots, tt):
    L = rw.shape[0]
    n_t = L // tt

    def body(rw_ref, prev_ref, mu_ref, w0_ref, w2_ref, a0_ref, a2_ref, kk_ref, ka_ref, ee_ref,
             c_r, c_w, c_k, c_v, c_a, c_b, cb_r, cb_k, cb_v,
             drws_ref, dmu_o, dw0_o, dw2_o, da0_o, da2_o, dkk_o, dka_o,
             dmu, dw0, dw2, da0, da2, dkk, dka):
        i = pl.program_id(0)
        accs = (dmu, dw0, dw2, da0, da2, dkk, dka)

        @pl.when(i == 0)
        def _():
            for acc in accs:
                acc[...] = jnp.zeros_like(acc)

        rwv = rw_ref[...]
        diff = _shifted(rwv, prev_ref[...], i == 0) - rwv
        rws = rwv + diff * mu_ref[...]
        consts = (w0_ref[...], w2_ref[...], a0_ref[...], a2_ref[...], kk_ref[...], ka_ref[...])
        _, vjp = jax.vjp(lambda *a: _rwkv_pre_fn(*a, ee_ref[...]), *_split_rw(rws), *consts)
        scan = [_load_heads(c) for c in (c_r, c_w, c_k, c_v, c_a, c_b)]
        g = vjp((scan[0] + cb_r[...], scan[1], scan[2] + cb_k[...], scan[3] + cb_v[...], scan[4], scan[5]))
        drws = jnp.concatenate(g[0:4], axis=-1)
        drws_ref[...] = drws
        dmu[...] += jnp.sum(drws * diff, axis=0, keepdims=True)
        for acc, val in zip(accs[1:], g[4:]):
            acc[...] += val

        @pl.when(i == n_t - 1)
        def _():
            for acc, out in zip(accs, (dmu_o, dw0_o, dw2_o, da0_o, da2_o, dkk_o, dka_o)):
                out[...] = acc[...]

    row = pl.BlockSpec((tt, D_RWKV), lambda i: (i, 0))
    shapes = [(1, D_SHIFT), (1, D_RWKV), (LANES, D_RWKV), (1, D_RWKV), (LANES, D_RWKV), (1, D_RWKV), (1, D_RWKV)]
    return pl.pallas_call(
        body, name="rwkv_pre_bwd", grid=(n_t,),
        in_specs=_rwkv_pre_specs(tt) + [_head_spec(tt)] * 6 + [row] * 3,
        out_specs=[pl.BlockSpec((tt, D_SHIFT), lambda i: (i, 0))] + [_acc_spec(s) for s in shapes],
        out_shape=[_sds((L, D_SHIFT))] + [_sds(s) for s in shapes],
        scratch_shapes=[pltpu.VMEM(s, F32) for s in shapes],
        compiler_params=_params("arbitrary"),
    )(rw, rw, mu, w0, w2p, a0, a2p, k_k, k_a, ee, *cots)


def _bmm(a, b):
    return lax.dot_general(a, b, (((2,), (1,)), ((0,), (0,))), precision=HI, preferred_element_type=F32)


def _bmm_nt(a, b):
    return lax.dot_general(a, b, (((2,), (2,)), ((0,), (0,))), precision=HI, preferred_element_type=F32)


def _bmm_tn(a, b):
    return lax.dot_general(a, b, (((1,), (1,)), ((0,), (0,))), precision=HI, preferred_element_type=F32)


def _bdot_bf(a, b, lhs_dim, rhs_dim):
    return lax.dot_general(a.astype(BF16), b.astype(BF16), (((lhs_dim,), (rhs_dim,)), ((0,), (0,))),
                           preferred_element_type=F32)


@jax.custom_vjp
def _bmm_bf(a, b):
    return _bdot_bf(a, b, 2, 1)


def _bmm_bf_fwd(a, b):
    return _bmm_bf(a, b), (a, b)


def _bmm_bf_bwd(res, g):
    a, b = res
    return _bdot_bf(g, b, 2, 2), _bdot_bf(a, g, 1, 1)


_bmm_bf.defvjp(_bmm_bf_fwd, _bmm_bf_bwd)


@jax.custom_vjp
def _bmm_tn_bf(a, b):
    return _bdot_bf(a, b, 1, 1)


def _bmm_tn_bf_fwd(a, b):
    return _bmm_tn_bf(a, b), (a, b)


def _bmm_tn_bf_bwd(res, g):
    a, b = res
    return _bdot_bf(b, g, 2, 2), _bdot_bf(a, g, 2, 1)


_bmm_tn_bf.defvjp(_bmm_tn_bf_fwd, _bmm_tn_bf_bwd)


def _unit_lower_inverse(a):
    t = a.shape[-1]
    ti = lax.broadcasted_iota(jnp.int32, (t, t), 0)
    si = lax.broadcasted_iota(jnp.int32, (t, t), 1)
    inv = jnp.where(ti == si, 1.0, 0.0)[None] + a
    pw = _bmm(a, a)
    n = 2
    while n < t:
        both = _bmm(jnp.concatenate([inv, pw], axis=1), pw)
        inv = inv + both[:, :t]
        pw = both[:, t:]
        n *= 2
    return inv


@jax.custom_vjp
def _solve_unit_lower(a, rhs, inv):
    return _bmm(inv, rhs)


def _solve_fwd(a, rhs, inv):
    u = _bmm(inv, rhs)
    return u, (inv, u)


def _solve_bwd(res, du):
    inv, u = res
    d_rhs = _bmm_tn(inv, du)
    return _bmm_nt(d_rhs, u), d_rhs, jnp.zeros_like(inv)


_solve_unit_lower.defvjp(_solve_fwd, _solve_bwd)


def _rwkv_chunk(st0, r, logw, k, v, a, b, inv=None):
    n_h, t, _ = r.shape
    ti = lax.broadcasted_iota(jnp.int32, (t, t), 0)
    si = lax.broadcasted_iota(jnp.int32, (t, t), 1)
    ones_tri = jnp.broadcast_to(jnp.where(ti >= si, 1.0, 0.0)[None], (n_h, t, t))
    log_p = _bmm(ones_tri, logw)
    p_in = jnp.exp(log_p)
    p_inv = jnp.exp(-log_p)
    at = a * jnp.exp(log_p - logw)
    rt = r * p_in
    bk = jnp.concatenate([b * p_inv, k * p_inv], axis=1)
    ri = lax.broadcasted_iota(jnp.int32, (2 * t, 2 * t), 0)
    ci = lax.broadcasted_iota(jnp.int32, (2 * t, 2 * t), 1)
    top_rows = ri < t
    diff = jnp.where(top_rows, ri, ri - t) - jnp.where(ci < t, ci, ci - t)
    mask = (diff >= jnp.where(top_rows, 1, 0))[None]
    m = jnp.where(mask, _bmm_nt(jnp.concatenate([at, rt], axis=1), bk), 0.0)
    top, bottom = m[:, :t], m[:, t:]
    a_ab = top[:, :, :t]
    if inv is None:
        inv = _unit_lower_inverse(a_ab)
    rhs = _bmm_bf(jnp.concatenate([at, top[:, :, t:]], axis=2), jnp.concatenate([st0, v], axis=1))
    u = _solve_unit_lower(a_ab, rhs, inv)
    y = _bmm_bf(jnp.concatenate([rt, bottom], axis=2), jnp.concatenate([st0, u, v], axis=1))
    p_end = jnp.swapaxes(p_in[:, t - 1:t, :], 1, 2)
    st1 = (st0 + _bmm_tn_bf(bk, jnp.concatenate([u, v], axis=1))) * p_end
    return y, st1, inv


def _rwkv_scan_fwd(ops):
    n_h, L, n = ops[0].shape
    t = RWKV_CHUNK
    n_c = L // t

    def body(r_ref, w_ref, k_ref, v_ref, a_ref, b_ref, y_ref, st_ref, inv_ref, st):
        @pl.when(pl.program_id(0) == 0)
        def _():
            st[...] = jnp.zeros_like(st)

        st0 = st[...]
        st_ref[0] = st0
        y, st1, inv = _rwkv_chunk(st0, r_ref[...], w_ref[...], k_ref[...], v_ref[...], a_ref[...], b_ref[...])
        y_ref[...] = y
        inv_ref[0] = inv
        st[...] = st1

    blk = pl.BlockSpec((n_h, t, n), lambda c: (0, c, 0))
    return pl.pallas_call(
        body, name="rwkv_scan_fwd", grid=(n_c,), in_specs=[blk] * 6,
        out_specs=[blk, pl.BlockSpec((1, n_h, n, n), lambda c: (c, 0, 0, 0)),
                   pl.BlockSpec((1, n_h, t, t), lambda c: (c, 0, 0, 0))],
        out_shape=[_sds((n_h, L, n)), _sds((n_c, n_h, n, n)), _sds((n_c, n_h, t, t))],
        scratch_shapes=[pltpu.VMEM((n_h, n, n), F32)],
        compiler_params=_params("arbitrary"),
    )(*ops)


def _rwkv_scan_bwd(ops, states, invs, dy):
    n_h, L, n = ops[0].shape
    t = RWKV_CHUNK
    n_c = L // t

    def body(r_ref, w_ref, k_ref, v_ref, a_ref, b_ref, st_ref, inv_ref, dy_ref, dr, dw, dk, dv, da, db, dst):
        @pl.when(pl.program_id(0) == 0)
        def _():
            dst[...] = jnp.zeros_like(dst)

        inv = inv_ref[0]
        _, vjp = jax.vjp(lambda *a: _rwkv_chunk(*a, inv=inv)[:2], st_ref[0], r_ref[...], w_ref[...], k_ref[...],
                         v_ref[...], a_ref[...], b_ref[...])
        g = vjp((dy_ref[...], dst[...]))
        dst[...] = g[0]
        for out, val in zip((dr, dw, dk, dv, da, db), g[1:]):
            out[...] = val

    blk = pl.BlockSpec((n_h, t, n), lambda c: (0, n_c - 1 - c, 0))
    per_chunk = lambda m: pl.BlockSpec((1, n_h, m, m), lambda c: (n_c - 1 - c, 0, 0, 0))
    return pl.pallas_call(
        body, name="rwkv_scan_bwd", grid=(n_c,),
        in_specs=[blk] * 6 + [per_chunk(n), per_chunk(t), blk],
        out_specs=[blk] * 6, out_shape=[_sds((n_h, L, n))] * 6,
        scratch_shapes=[pltpu.VMEM((n_h, n, n), F32)],
        compiler_params=_params("arbitrary"),
    )(*ops, states, invs, dy)


def _post_fn(x, u, zs, zr, ysc, r, k2, v, y_ssm, d, glu_w, glu_b, ln_w, ln_b, r_k,
             wo_s5, wo_rwkv, gf, tgt, ee):
    y3 = _gelu(y_ssm + d * u)
    y_s5 = y3 * _sigmoid(mm_bf(y3, glu_w) + glu_b) * _silu(zs)
    mean = head_sum_split(ysc, ee) * (1.0 / HEAD)
    yc = ysc - mean
    var = head_sum(yc * yc, ee) * (1.0 / HEAD)
    gn = yc * lax.rsqrt(var + GN_EPS) * ln_w + ln_b
    bonus = head_sum(r * k2 * r_k, ee) * v
    y_rwkv = (gn + bonus) * _silu(zr)
    x2 = x + mm_bf(y_s5, wo_s5) + mm_bf(y_rwkv, wo_rwkv)
    err = _rms(x2, gf) - tgt
    return 0.5 * jnp.mean(err * err, axis=-1, keepdims=True)


def _post(x, u, zs, zr, ysc, r, k2, v, y_ssm, d, glu_w, glu_b, ln_w, ln_b, r_k, w_out, gf, tgt, ee, tt):
    L = x.shape[0]
    n_t = L // tt
    acc_shapes = [(1, D_S5), (D_S5, D_S5), (1, D_S5), (1, D_RWKV), (1, D_RWKV), (1, D_RWKV),
                  (D_MODEL, D_MODEL), (1, D_MODEL), (8, LANES)]

    def body(x_ref, u_ref, zs_ref, zr_ref, ysc_ref, r_ref, k2_ref, v_ref, yssm_ref,
             d_ref, gw_ref, gb_ref, lw_ref, lb_ref, rk_ref, wo_ref, gf_ref, tgt_ref, ee_ref,
             dx_o, du_o, dzs_o, dzr_o, dysc_o, dr_o, dk2_o, dv_o, dyssm_o,
             dd_o, dgw_o, dgb_o, dlw_o, dlb_o, drk_o, dwo_o, dgf_o, loss_o,
             dd, dgw, dgb, dlw, dlb, drk, dwo, dgf, loss):
        i = pl.program_id(0)
        accs = (dd, dgw, dgb, dlw, dlb, drk, dwo, dgf, loss)

        @pl.when(i == 0)
        def _():
            for acc in accs:
                acc[...] = jnp.zeros_like(acc)

        args = (x_ref[...], u_ref[...], zs_ref[...], zr_ref[...],
                _load_heads(ysc_ref), _load_heads(r_ref), _load_heads(k2_ref), _load_heads(v_ref), yssm_ref[...],
                d_ref[...], gw_ref[...], gb_ref[...], lw_ref[...], lb_ref[...], rk_ref[...],
                wo_ref[0:D_S5, :], wo_ref[D_S5:D_MODEL, :], gf_ref[...])
        rows, vjp = jax.vjp(lambda *a: _post_fn(*a, tgt_ref[...], ee_ref[...]), *args)
        g = vjp(jnp.ones_like(rows))
        for out, val in zip((dx_o, du_o, dzs_o, dzr_o), g[0:4]):
            out[...] = val
        _store_heads(dysc_o, g[4])
        for out, val in zip((dr_o, dk2_o, dv_o, dyssm_o), g[5:9]):
            out[...] = val
        for acc, val in zip((dd, dgw, dgb, dlw, dlb, drk), g[9:15]):
            acc[...] += val
        dwo[0:D_S5, :] += g[15]
        dwo[D_S5:D_MODEL, :] += g[16]
        dgf[...] += g[17]
        loss[...] += jnp.broadcast_to(jnp.sum(rows, axis=0, keepdims=True), loss.shape)

        @pl.when(i == n_t - 1)
        def _():
            for acc, out in zip(accs, (dd_o, dgw_o, dgb_o, dlw_o, dlb_o, drk_o, dwo_o, dgf_o, loss_o)):
                pltpu.sync_copy(acc, out)

    row = lambda n: pl.BlockSpec((tt, n), lambda i: (i, 0))
    in_specs = ([row(D_MODEL)] + [row(512)] * 3 + [_head_spec(tt)] * 4 + [row(D_S5)]
                + [_const_spec(s) for s in [(1, D_S5), (D_S5, D_S5), (1, D_S5), (1, D_RWKV), (1, D_RWKV), (1, D_RWKV),
                                            (D_MODEL, D_MODEL), (1, D_MODEL)]]
                + [row(D_MODEL), _const_spec((D_RWKV, D_RWKV))])
    out_rows = [D_MODEL] + [512] * 3 + [None] + [512] * 4
    return pl.pallas_call(
        body, name="post_fwd_bwd", grid=(n_t,), in_specs=in_specs,
        out_specs=[row(n) if n else _head_spec(tt) for n in out_rows] + [_ANY] * len(acc_shapes),
        out_shape=([_sds((L, n)) if n else _sds((N_HEADS, L, HEAD)) for n in out_rows]
                   + [_sds(s) for s in acc_shapes]),
        scratch_shapes=[pltpu.VMEM(s, F32) for s in acc_shapes],
        compiler_params=_params("arbitrary"),
    )(x, u, zs, zr, ysc, r, k2, v, y_ssm, d, glu_w, glu_b, ln_w, ln_b, r_k, w_out, gf, tgt, ee)


def _s5_bwd(u, du_direct, dy, s_re, s_im, b4_re, b4_im, c4_re, c4_im, pow_r, pow_i, tt):
    L = u.shape[0]
    n_t = L // tt
    acc_shapes = ([(S5_BLOCKS, LANES, 512)] * 2 + [(S5_BLOCKS, 512, LANES)] * 2 + [(1, N_STATE)] * 2)

    def body(u_ref, dud_ref, dy_ref, sre_ref, sim_ref, pre_ref, pim_ref, bre_ref, bim_ref, cre_ref, cim_ref,
             pr_ref, pi_ref, du_o, dbre_o, dbim_o, dcre_o, dcim_o, dlr_o, dli_o,
             dbre, dbim, dcre, dcim, dlr, dli, gre, gim, car_r, car_i):
        i = pl.program_id(0)

        @pl.when(i == 0)
        def _():
            for acc in (dbre, dbim, dcre, dcim, dlr, dli, car_r, car_i):
                acc[...] = jnp.zeros_like(acc)

        uv = u_ref[...]
        dyv = dy_ref[...]
        blocks = [slice(q * 512, (q + 1) * 512) for q in range(S5_BLOCKS)]
        lanes = [slice(q * LANES, (q + 1) * LANES) for q in range(S5_BLOCKS)]
        for q in range(S5_BLOCKS):
            gre[:, blocks[q]] = _dot_bf(dyv[:, lanes[q]], cre_ref[q], ((1,), (1,)))
            gim[:, blocks[q]] = -_dot_bf(dyv[:, lanes[q]], cim_ref[q], ((1,), (1,)))
        _tile_scan(gre, gim, pr_ref, pi_ref, car_r, car_i, reverse=True)
        for q in range(S5_BLOCKS):
            gr = gre[:, blocks[q]]
            gi = gim[:, blocks[q]]
            sr = sre_ref[:, blocks[q]]
            si = sim_ref[:, blocks[q]]
            du_o[:, lanes[q]] = (dud_ref[:, lanes[q]] + _dot_bf(gr, bre_ref[q], ((1,), (1,)))
                                 + _dot_bf(gi, bim_ref[q], ((1,), (1,))))
            dbre[q] += _dot_bf(uv[:, lanes[q]], gr, ((0,), (0,)))
            dbim[q] += _dot_bf(uv[:, lanes[q]], gi, ((0,), (0,)))
            dcre[q] += _dot_bf(sr, dyv[:, lanes[q]], ((0,), (0,)))
            dcim[q] -= _dot_bf(si, dyv[:, lanes[q]], ((0,), (0,)))
            rid = lax.broadcasted_iota(jnp.int32, sr.shape, 0)
            first = i == n_t - 1
            prev_r = jnp.where(first, 0.0, pre_ref[7:8, blocks[q]])
            prev_i = jnp.where(first, 0.0, pim_ref[7:8, blocks[q]])
            pr = jnp.where(rid == 0, jnp.broadcast_to(prev_r, sr.shape), pltpu.roll(sr, 1, axis=0))
            pi_ = jnp.where(rid == 0, jnp.broadcast_to(prev_i, si.shape), pltpu.roll(si, 1, axis=0))
            dlr[:, blocks[q]] += jnp.sum(pr * gr + pi_ * gi, axis=0, keepdims=True)
            dli[:, blocks[q]] += jnp.sum(pr * gi - pi_ * gr, axis=0, keepdims=True)

        @pl.when(i == n_t - 1)
        def _():
            for acc, out in zip((dbre, dbim, dcre, dcim, dlr, dli), (dbre_o, dbim_o, dcre_o, dcim_o, dlr_o, dli_o)):
                out[...] = acc[...]

    row = lambda n: pl.BlockSpec((tt, n), lambda i: (n_t - 1 - i, 0))
    prev = pl.BlockSpec((8, N_STATE), lambda i: (jnp.maximum((n_t - 1 - i) * (tt // 8) - 1, 0), 0))
    return pl.pallas_call(
        body, name="s5_bwd", grid=(n_t,),
        in_specs=[row(D_S5)] * 3 + [row(N_STATE)] * 2 + [prev] * 2
        + [_const_spec((S5_BLOCKS, LANES, 512))] * 2 + [_const_spec((S5_BLOCKS, 512, LANES))] * 2
        + [_const_spec((8, N_STATE))] * 2,
        out_specs=[row(D_S5)] + [_acc_spec(s) for s in acc_shapes],
        out_shape=[_sds((L, D_S5))] + [_sds(s) for s in acc_shapes],
        scratch_shapes=[pltpu.VMEM(s, F32) for s in acc_shapes] + [pltpu.VMEM((tt, N_STATE), F32)] * 2
        + [pltpu.VMEM((8, N_STATE), F32)] * 2,
        compiler_params=_params("arbitrary"),
    )(u, du_direct, dy, s_re, s_im, s_re, s_im, b4_re, b4_im, c4_re, c4_im, pow_r, pow_i)


def _bwd_in(x, norm_g, w_in_bf, mu, dx2, du, dzs, drws, dzr, tt):
    L = x.shape[0]
    n_t = L // tt

    def body(x_ref, g_ref, w_ref, mu_ref, dx2_ref, du_ref, dzs_ref, drws_ref, nxt_ref, dzr_ref,
             gx_o, dw_o, dg_o, dproj, dw, dg):
        i = pl.program_id(0)

        @pl.when(i == 0)
        def _():
            dw[...] = jnp.zeros_like(dw)
            dg[...] = jnp.zeros_like(dg)

        drws_v = drws_ref[...]
        rid = lax.broadcasted_iota(jnp.int32, drws_v.shape, 0)
        nxt_row = jnp.where(i == n_t - 1, 0.0, nxt_ref[0:1, :])
        nxt = jnp.where(rid == tt - 1, jnp.broadcast_to(nxt_row, drws_v.shape), pltpu.roll(drws_v, tt - 1, axis=0))
        muv = mu_ref[...]
        drw = drws_v * (1.0 - muv) + nxt * muv
        dproj[:, 0:D_S5] = du_ref[...].astype(BF16)
        dproj[:, D_S5:2 * D_S5] = dzs_ref[...].astype(BF16)
        dproj[:, 2 * D_S5:2 * D_S5 + D_SHIFT] = drw.astype(BF16)
        dproj[:, 2 * D_S5 + D_SHIFT:D_IN] = dzr_ref[...].astype(BF16)
        dh = _dot(dproj[...], w_ref[...], ((1,), (1,)), None)
        h, vjp = jax.vjp(_rms, x_ref[...], g_ref[...])
        dxh, dgv = vjp(dh)
        gx_o[...] = dx2_ref[...] + dxh
        dg[...] += dgv
        dw[...] += _dot(h.astype(BF16), dproj[...], ((0,), (0,)), None)

        @pl.when(i == n_t - 1)
        def _():
            dg_o[...] = dg[...]
            pltpu.sync_copy(dw, dw_o)

    row = lambda n: pl.BlockSpec((tt, n), lambda i: (i, 0))
    nxt = pl.BlockSpec((8, D_SHIFT), lambda i: (jnp.minimum((i + 1) * (tt // 8), L // 8 - 1), 0))
    return pl.pallas_call(
        body, name="bwd_in", grid=(n_t,),
        in_specs=[row(D_MODEL), _const_spec((1, D_MODEL)), _const_spec((D_MODEL, D_IN)), _const_spec((1, D_SHIFT)),
                  row(D_MODEL), row(D_S5), row(D_S5), row(D_SHIFT), nxt, row(D_RWKV)],
        out_specs=[row(D_MODEL), _ANY, _acc_spec((1, D_MODEL))],
        out_shape=[_sds((L, D_MODEL)), _sds((D_MODEL, D_IN)), _sds((1, D_MODEL))],
        scratch_shapes=[pltpu.VMEM((tt, D_IN), BF16), pltpu.VMEM((D_MODEL, D_IN), F32), pltpu.VMEM((1, D_MODEL), F32)],
        compiler_params=_params("arbitrary"),
    )(x, norm_g, w_in_bf, mu, dx2, du, dzs, drws, drws, dzr)


def _block_diag_b(bbar):
    bb = bbar.reshape(S5_GROUP, S5_BLOCKS, 8, S5_STATE)
    return jnp.einsum('hqgp,Gg->qGhgp', bb, jnp.eye(8, dtype=F32)).reshape(S5_BLOCKS, LANES, 512)


def _block_diag_b_t(db4):
    d = db4.reshape(S5_BLOCKS, 8, S5_GROUP, 8, S5_STATE)
    return jnp.einsum('qGhgp,Gg->hqgp', d, jnp.eye(8, dtype=F32)).reshape(S5_GROUP, N_STATE)


def _block_diag_c(c):
    cc = c.reshape(S5_BLOCKS, 8, S5_GROUP, S5_STATE)
    return jnp.einsum('qghp,gG->qgpGh', cc, jnp.eye(8, dtype=F32)).reshape(S5_BLOCKS, 512, LANES)


def _block_diag_c_t(dc4):
    d = dc4.reshape(S5_BLOCKS, 8, S5_STATE, 8, S5_GROUP)
    return jnp.einsum('qgpGh,gG->qghp', d, jnp.eye(8, dtype=F32)).reshape(S5_GROUPS, S5_GROUP, S5_STATE)


def _local_step(x, tgt, w):
    L = x.shape[0]
    tt = min(512, L)
    tp = min(256, L)
    ee = _head_sum_matrix()

    lam_re = w['s5_lam_re'].reshape(1, N_STATE)
    lam_im = w['s5_lam_im'].reshape(1, N_STATE)
    logdt = jnp.repeat(w['s5_log_dt'], S5_STATE).reshape(1, N_STATE)
    b_re_t = w['s5_b_re'].transpose(2, 0, 1).reshape(S5_GROUP, N_STATE)
    b_im_t = w['s5_b_im'].transpose(2, 0, 1).reshape(S5_GROUP, N_STATE)
    bbr, bbi, pow_r, pow_i, rpow_r, rpow_i = _s5_param_fwd(lam_re, lam_im, logdt, b_re_t, b_im_t)
    b4_re, b4_im = _block_diag_b(bbr), _block_diag_b(bbi)
    c4_re, c4_im = _block_diag_c(w['s5_c_re']), _block_diag_c(w['s5_c_im'])

    norm_g = w['norm_g'].reshape(1, D_MODEL)
    w_in_bf = w['w_in'].astype(BF16)
    u, zs, rw, zr = _fwd_in(x, norm_g, w_in_bf, tt)
    s_re, s_im, y_ssm = _s5_fwd(u, b4_re, b4_im, c4_re, c4_im, pow_r, pow_i, tt)

    row = lambda t: t.reshape(1, -1)
    zpad = jnp.zeros((HEAD, D_RWKV), F32)
    w2p = jnp.concatenate([w['rwkv_w2'], zpad], axis=0)
    a2p = jnp.concatenate([zpad, w['rwkv_a2']], axis=0)
    pre_consts = (row(w['rwkv_mu']), row(w['rwkv_w0']), w2p, row(w['rwkv_a0']), a2p,
                  row(w['rwkv_k_k']), row(w['rwkv_k_a']), ee)
    ops = _rwkv_pre_fwd(rw, *pre_consts, tt)
    ysc, states, invs = _rwkv_scan_fwd(ops)

    post = _post(x, u, zs, zr, ysc, ops[0], ops[2], ops[3], y_ssm,
                 row(w['s5_d']), w['s5_glu_w'], row(w['s5_glu_b']), row(w['rwkv_ln_w']), row(w['rwkv_ln_b']),
                 row(w['rwkv_r_k']), w['w_out'], row(w['final_g']), tgt, ee, tp)
    (dx2, du_d, dzs, dzr, dysc, dr_b, dk2_b, dv_b, dy_ssm,
     dd, dglu_w, dglu_b, dln_w, dln_b, dr_k, dw_out, dgf, loss) = post

    du, db4_re, db4_im, dc4_re, dc4_im, dlbr, dlbi = _s5_bwd(
        u, du_d, dy_ssm, s_re, s_im, b4_re, b4_im, c4_re, c4_im, rpow_r, rpow_i, tt)
    group_ind = (jnp.arange(N_STATE)[:, None] // S5_STATE == jnp.arange(LANES)[None, :]).astype(F32)
    dlam_re, dlam_im, dlogdt, db_re_t, db_im_t = _s5_param_bwd(
        lam_re, lam_im, logdt, b_re_t, b_im_t, dlbr, dlbi, _block_diag_b_t(db4_re), _block_diag_b_t(db4_im), group_ind)

    cots = list(_rwkv_scan_bwd(ops, states, invs, dysc)) + [dr_b, dk2_b, dv_b]
    drws, dmu, dw0, dw2p, da0, da2p, dk_k, dk_a = _rwkv_pre_bwd(rw, *pre_consts, cots, tt)

    grad_x, dw_in, dnorm_g = _bwd_in(x, norm_g, w_in_bf, row(w['rwkv_mu']), dx2, du, dzs, drws, dzr, tt)

    unb = lambda t: t.reshape(S5_GROUP, S5_GROUPS, S5_STATE).transpose(1, 2, 0)
    grads = {
        'norm_g': dnorm_g.reshape(D_MODEL), 'w_in': dw_in,
        's5_lam_re': dlam_re.reshape(S5_GROUPS, S5_STATE), 's5_lam_im': dlam_im.reshape(S5_GROUPS, S5_STATE),
        's5_log_dt': dlogdt[0, :S5_GROUPS], 's5_b_re': unb(db_re_t), 's5_b_im': unb(db_im_t),
        's5_c_re': _block_diag_c_t(dc4_re), 's5_c_im': _block_diag_c_t(dc4_im),
        's5_d': dd.reshape(D_S5), 's5_glu_w': dglu_w, 's5_glu_b': dglu_b.reshape(D_S5),
        'rwkv_mu': dmu.reshape(-1), 'rwkv_w0': dw0.reshape(-1), 'rwkv_w2': dw2p[:HEAD], 'rwkv_a0': da0.reshape(-1),
        'rwkv_a2': da2p[HEAD:], 'rwkv_k_k': dk_k.reshape(-1), 'rwkv_k_a': dk_a.reshape(-1),
        'rwkv_r_k': dr_k.reshape(N_HEADS, HEAD), 'rwkv_ln_w': dln_w.reshape(-1), 'rwkv_ln_b': dln_b.reshape(-1),
        'w_out': dw_out, 'final_g': dgf.reshape(D_MODEL),
    }
    return loss, grad_x, grads


def _exchange(arrays, gather, axes, name):
    n = len(arrays)
    group = 2 ** len(axes)

    def body(*refs):
        send_refs, recv_refs = refs[:n], refs[n:2 * n]
        send_sems, recv_sems, local_sems = refs[2 * n:]
        pos = {ax: lax.axis_index(ax) for ax in ("x", "y", "c")}

        def index_of(p):
            idx = 0
            for ax in axes:
                idx = 2 * idx + p[ax]
            return idx

        me = index_of(pos)
        own, outs, arrivals = [], [], []
        for i, (send_ref, recv_ref) in enumerate(zip(send_refs, recv_refs)):
            def block_for(dev, send_ref=send_ref, whole=gather[i]):
                return send_ref if whole else send_ref.at[dev]

            own.append(pltpu.make_async_copy(block_for(me), recv_ref.at[me], local_sems.at[i]))
            own[-1].start()
            for k in range(1, group):
                peer = dict(pos)
                for bit, ax in enumerate(axes):
                    if (k >> bit) & 1:
                        peer[ax] = 1 - pos[ax]
                peer_idx = index_of(peer)
                sems = dict(send_sem=send_sems.at[i, k - 1], recv_sem=recv_sems.at[i, k - 1],
                            device_id=(peer["x"], peer["y"], peer["c"]), device_id_type=pl.DeviceIdType.MESH)
                outs.append(pltpu.make_async_remote_copy(src_ref=block_for(peer_idx), dst_ref=recv_ref.at[me], **sems))
                outs[-1].start()
                arrivals.append(
                    pltpu.make_async_remote_copy(src_ref=block_for(peer_idx), dst_ref=recv_ref.at[peer_idx], **sems))
        for copy in arrivals:
            copy.wait_recv()
        for copy in outs:
            copy.wait_send()
        for copy in own:
            copy.wait()

    return pl.pallas_call(
        body, name=name, in_specs=[_ANY] * n, out_specs=[_ANY] * n,
        out_shape=[jax.ShapeDtypeStruct(((group,) + a.shape) if whole else a.shape, a.dtype)
                   for a, whole in zip(arrays, gather)],
        scratch_shapes=[pltpu.SemaphoreType.DMA((n, group - 1)), pltpu.SemaphoreType.DMA((n, group - 1)),
                        pltpu.SemaphoreType.DMA((n,))],
        compiler_params=pltpu.CompilerParams(has_side_effects=True),
    )(*arrays)


def _sum_devices(ref):
    g = ref[0].astype(F32)
    for s in range(1, ref.shape[0]):
        g = g + ref[s].astype(F32)
    return g


def _adamw_math(g, w, m, v):
    m_new = ADAM_B1 * m + (1.0 - ADAM_B1) * g
    v_new = ADAM_B2 * v + (1.0 - ADAM_B2) * (g * g)
    m_hat = m_new / (1.0 - ADAM_B1 ** ADAM_STEP)
    v_hat = v_new / (1.0 - ADAM_B2 ** ADAM_STEP)
    return -ADAM_LR * (m_hat / (jnp.sqrt(v_hat) + ADAM_EPS) + ADAM_WD * w), m_new, v_new


def _adamw(gs, ws, ms, vs, reduce, name):
    n = len(ws)

    def body(*refs):
        g_refs, w_refs, m_refs, v_refs = (refs[j * n:(j + 1) * n] for j in range(4))
        outs = refs[4 * n:]
        for i in range(n):
            g = _sum_devices(g_refs[i]) if reduce else g_refs[i][...]
            res = _adamw_math(g, w_refs[i][...], m_refs[i][...], v_refs[i][...])
            for j, val in enumerate(((g,) if reduce else ()) + res):
                outs[j * n + i][...] = val

    return pl.pallas_call(
        body, name=name, out_shape=[_sds(w.shape) for w in ws] * (4 if reduce else 3),
        compiler_params=pltpu.CompilerParams(vmem_limit_bytes=VMEM_LIMIT),
    )(*gs, *ws, *ms, *vs)


def _sum_blocks(recv):
    def body(recv_ref, out_ref):
        out_ref[...] = _sum_devices(recv_ref)

    return pl.pallas_call(body, name="sum_small_grads", out_shape=_sds(recv.shape[1:]))(recv)


_WEIGHTS = [
    ('norm_g', (1, 1024), False), ('w_in', (1, 1024, 400), True), ('s5_lam_re', (1, 32, 64), False),
    ('s5_lam_im', (1, 32, 64), False), ('s5_log_dt', (1, 32), False), ('s5_b_re', (1, 32, 64, 16), False),
    ('s5_b_im', (1, 32, 64, 16), False), ('s5_c_re', (1, 32, 16, 64), False), ('s5_c_im', (1, 32, 16, 64), False),
    ('s5_d', (1, 512), False), ('s5_glu_w', (1, 64, 512), True), ('s5_glu_b', (1, 512), False),
    ('rwkv_mu', (1, 1664), False), ('rwkv_w0', (1, 512), False), ('rwkv_w2', (1, 64, 64), True),
    ('rwkv_a0', (1, 512), False), ('rwkv_a2', (1, 64, 64), True), ('rwkv_k_k', (1, 512), False),
    ('rwkv_k_a', (1, 512), False), ('rwkv_r_k', (1, 8, 64), False), ('rwkv_ln_w', (1, 512), False),
    ('rwkv_ln_b', (1, 512), False), ('w_out', (1, 128, 1024), True), ('final_g', (1024,), False),
]
_SHARDED = [(n, s) for n, s, sharded in _WEIGHTS if sharded]
_SMALL = [(n, s) for n, s, sharded in _WEIGHTS if not sharded]
_COLUMN_SHARDED = ('w_in', 'rwkv_w2', 'rwkv_a2')
_SMALL_ROWS = -(-sum(math.prod(s) for _, s in _SMALL) // (8 * LANES)) * 8


def _pack_small(grads):
    flat = [grads[n].reshape(-1) for n, _ in _SMALL]
    pad = _SMALL_ROWS * LANES - sum(f.size for f in flat)
    return jnp.concatenate(flat + [jnp.zeros((pad,), F32)]).reshape(_SMALL_ROWS, LANES)


def _unpack_small(packed):
    flat = packed.reshape(-1)
    out, off = {}, 0
    for n, s in _SMALL:
        size = math.prod(s)
        out[n] = flat[off:off + size].reshape(s)
        off += size
    return out


_BF16_OPERANDS = ('w_in', 's5_glu_w', 'w_out')


def _join_shards(name, blocks):
    _, rows, cols = blocks.shape
    if name in _COLUMN_SHARDED:
        return blocks.transpose(1, 0, 2).reshape(rows, N_DEV * cols)
    return blocks.reshape(N_DEV * rows, cols)


def _split_shards(name, full, shard_shape):
    rows, cols = shard_shape
    if name in _COLUMN_SHARDED:
        return full.reshape(rows, N_DEV, cols).transpose(1, 0, 2)
    return full.reshape(N_DEV, rows, cols)


def kernel(x, norm_g, w_in, s5_lam_re, s5_lam_im, s5_log_dt, s5_b_re, s5_b_im, s5_c_re, s5_c_im, s5_d, s5_glu_w, s5_glu_b, rwkv_mu, rwkv_w0, rwkv_w2, rwkv_a0, rwkv_a2, rwkv_k_k, rwkv_k_a, rwkv_r_k, rwkv_ln_w, rwkv_ln_b, w_out, final_g, loss_target, m_norm_g, m_w_in, m_s5_lam_re, m_s5_lam_im, m_s5_log_dt, m_s5_b_re, m_s5_b_im, m_s5_c_re, m_s5_c_im, m_s5_d, m_s5_glu_w, m_s5_glu_b, m_rwkv_mu, m_rwkv_w0, m_rwkv_w2, m_rwkv_a0, m_rwkv_a2, m_rwkv_k_k, m_rwkv_k_a, m_rwkv_r_k, m_rwkv_ln_w, m_rwkv_ln_b, m_w_out, m_final_g, v_norm_g, v_w_in, v_s5_lam_re, v_s5_lam_im, v_s5_log_dt, v_s5_b_re, v_s5_b_im, v_s5_c_re, v_s5_c_im, v_s5_d, v_s5_glu_w, v_s5_glu_b, v_rwkv_mu, v_rwkv_w0, v_rwkv_w2, v_rwkv_a0, v_rwkv_a2, v_rwkv_k_k, v_rwkv_k_a, v_rwkv_r_k, v_rwkv_ln_w, v_rwkv_ln_b, v_w_out, v_final_g):
    given = dict(locals())

    n_sh = len(_SHARDED)
    everyone = ("x", "y", "c")
    shards = [given[n][0].astype(BF16 if n in _BF16_OPERANDS else F32) for n, _ in _SHARDED]
    gathered = _exchange(shards, (True,) * n_sh, everyone, "gather_weights")
    local = {n: _join_shards(n, blocks).astype(F32 if n != 'w_in' else BF16)
             for (n, _), blocks in zip(_SHARDED, gathered)}
    local.update({n: (given[n][0] if len(s) > 1 else given[n]) for n, s in _SMALL})

    loss, grad_x, grads = _local_step(x[0], loss_target[0], local)

    blocks = [_split_shards(n, grads[n], s[1:]).astype(BF16) for n, s in _SHARDED]
    small = _pack_small(grads).reshape(N_DEV, _SMALL_ROWS // N_DEV, LANES)
    recv = _exchange(blocks + [small], (False,) * (n_sh + 1), everyone, "exchange_grads")
    small_sum = _exchange([_sum_blocks(recv[-1])], (True,), everyone, "gather_small_grads")[0]

    result = {}
    for group, name in (([0], "adamw_w_in"), ([1, 2, 3, 4], "adamw_shards")):
        ns = [_SHARDED[i][0] for i in group]
        res = _adamw([recv[i] for i in group], [given[n][0] for n in ns], [given['m_' + n][0] for n in ns],
                     [given['v_' + n][0] for n in ns], True, name)
        for j, n in enumerate(ns):
            result[n] = [res[k * len(ns) + j][None] for k in range(4)]
    g_small = _unpack_small(small_sum)
    two_d = lambda t: t.reshape(1, -1) if t.ndim == 1 else t
    ns = [n for n, _ in _SMALL]
    res = _adamw([two_d(g_small[n]) for n in ns], [two_d(given[n]) for n in ns], [two_d(given['m_' + n]) for n in ns],
                 [two_d(given['v_' + n]) for n in ns], False, "adamw_small")
    for j, (n, s) in enumerate(_SMALL):
        result[n] = [g_small[n]] + [res[k * len(ns) + j].reshape(s) for k in range(3)]

    total = lax.psum(loss[0, 0], ("x", "y", "c"))
    outs = [total, grad_x[None]]
    for k in range(4):
        outs += [result[n][k] for n, _, _ in _WEIGHTS]
    return tuple(outs)
```

```python
import functools
import math

import jax
import jax.numpy as jnp
from jax import lax
from jax.experimental import pallas as pl
from jax.experimental.pallas import tpu as pltpu

F32 = jnp.float32
BF16 = jnp.bfloat16
HI = lax.Precision.HIGH

D_MODEL = 1024
D_S5 = 512
D_RWKV = 512
S5_GROUPS = 32
S5_GROUP = 16
S5_STATE = 64
N_STATE = S5_GROUPS * S5_STATE
N_HEADS = 8
HEAD = 64
D_SHIFT = 3 * D_RWKV + 128
D_IN = 2 * D_S5 + D_SHIFT + D_RWKV
NORM_EPS = 1e-6
GN_EPS = 64e-5
N_DEV = 8
LANES = 128
S5_BLOCKS = 4
RWKV_CHUNK = 64
RWKV_CHUNKS_PER_STEP = 4
VMEM_LIMIT = 56 * 1024 * 1024

ADAM_LR = 0.001
ADAM_B1 = 0.9
ADAM_B2 = 0.999
ADAM_EPS = 1e-08
ADAM_WD = 0.01
ADAM_STEP = 10


def _dot(a, b, dims, prec):
    return lax.dot_general(a, b, (dims, ((), ())), precision=prec, preferred_element_type=F32)


def _dot_bf(a, b, dims):
    return _dot(a.astype(BF16), b.astype(BF16), dims, None)


def _make_mm(cast, prec):
    @jax.custom_vjp
    def mm(a, b):
        return _dot(cast(a), cast(b), ((1,), (0,)), prec)

    def fwd(a, b):
        return mm(a, b), (a, b)

    def bwd(res, g):
        a, b = res
        return (_dot(cast(g), cast(b), ((1,), (1,)), prec), _dot(cast(a), cast(g), ((0,), (0,)), prec))

    mm.defvjp(fwd, bwd)
    return mm


mm_bf = _make_mm(lambda t: t.astype(BF16), None)


def _make_head_sum(split):
    def product(x, ee):
        hi = x.astype(BF16)
        out = _dot(hi, ee, ((1,), (0,)), None)
        if split:
            out = out + _dot((x - hi.astype(F32)).astype(BF16), ee, ((1,), (0,)), None)
        return out

    @jax.custom_vjp
    def head_sum(x, ee):
        return product(x, ee)

    def fwd(x, ee):
        return product(x, ee), ee

    def bwd(ee, g):
        return product(g, ee), jnp.zeros_like(ee)

    head_sum.defvjp(fwd, bwd)
    return head_sum


head_sum = _make_head_sum(False)
head_sum_split = _make_head_sum(True)


def _sigmoid(x):
    return 1.0 / (1.0 + jnp.exp(-x))


def _silu(x):
    return x * _sigmoid(x)


def _softplus(x):
    return jnp.maximum(x, 0.0) + jnp.log(1.0 + jnp.exp(-jnp.abs(x)))


def _gelu(x):
    return 0.5 * x * (1.0 + jnp.tanh(math.sqrt(2.0 / math.pi) * (x + 0.044715 * x * x * x)))


def _rms(x, g):
    return x * lax.rsqrt(jnp.mean(x * x, axis=-1, keepdims=True) + NORM_EPS) * g


def _const_spec(shape):
    nd = len(shape)
    return pl.BlockSpec(shape, lambda *_: (0,) * nd, pipeline_mode=pl.Buffered(1))


def _acc_spec(shape):
    nd = len(shape)
    return pl.BlockSpec(shape, lambda *_: (0,) * nd)


def _params(sem):
    return pltpu.CompilerParams(dimension_semantics=(sem,), vmem_limit_bytes=VMEM_LIMIT)


_ANY = pl.BlockSpec(memory_space=pl.ANY)


def _sds(shape):
    return jax.ShapeDtypeStruct(shape, F32)


def _head_sum_matrix():
    i = jnp.arange(D_RWKV) // HEAD
    return (i[:, None] == i[None, :]).astype(BF16)


def _s5_param_fn(lam_re, lam_im, logdt, b_re, b_im):
    dt = jnp.exp(logdt)
    mag = jnp.exp(lam_re * dt)
    ang = lam_im * dt
    lbr = mag * jnp.cos(ang)
    lbi = mag * jnp.sin(ang)
    nr = lbr - 1.0
    den = lam_re * lam_re + lam_im * lam_im
    cr = (nr * lam_re + lbi * lam_im) / den
    ci = (lbi * lam_re - nr * lam_im) / den
    return lbr, lbi, cr * b_re - ci * b_im, cr * b_im + ci * b_re


def _cmul(ar, ai, br, bi):
    return ar * br - ai * bi, ar * bi + ai * br


def _s5_param_fwd(lam_re, lam_im, logdt, b_re, b_im):
    def body(lr, li, ld, br, bi, o_br, o_bi, o_pr, o_pi, o_qr, o_qi):
        lbr, lbi, bbr, bbi = _s5_param_fn(lr[...], li[...], ld[...], br[...], bi[...])
        o_br[...] = bbr
        o_bi[...] = bbi
        rid = lax.broadcasted_iota(jnp.int32, (8, N_STATE), 0)
        pr, pi_ = lbr, lbi
        fwd_r = rev_r = jnp.broadcast_to(pr, (8, N_STATE))
        fwd_i = rev_i = jnp.broadcast_to(pi_, (8, N_STATE))
        for j in range(1, 8):
            pr, pi_ = _cmul(pr, pi_, lbr, lbi)
            fwd_r = jnp.where(rid == j, jnp.broadcast_to(pr, (8, N_STATE)), fwd_r)
            fwd_i = jnp.where(rid == j, jnp.broadcast_to(pi_, (8, N_STATE)), fwd_i)
            rev_r = jnp.where(rid == 7 - j, jnp.broadcast_to(pr, (8, N_STATE)), rev_r)
            rev_i = jnp.where(rid == 7 - j, jnp.broadcast_to(pi_, (8, N_STATE)), rev_i)
        o_pr[...] = fwd_r
        o_pi[...] = fwd_i
        o_qr[...] = rev_r
        o_qi[...] = -rev_i

    return pl.pallas_call(
        body, name="s5_param_fwd",
        out_shape=[_sds((S5_GROUP, N_STATE))] * 2 + [_sds((8, N_STATE))] * 4,
    )(lam_re, lam_im, logdt, b_re, b_im)


def _s5_param_bwd(lam_re, lam_im, logdt, b_re, b_im, d_lbr, d_lbi, d_bbr, d_bbi, group_ind):
    def body(lr, li, ld, br, bi, g0, g1, g2, g3, ind, o_lr, o_li, o_ld, o_br, o_bi):
        _, vjp = jax.vjp(_s5_param_fn, lr[...], li[...], ld[...], br[...], bi[...])
        d_lr, d_li, d_ld, d_br, d_bi = vjp((g0[...], g1[...], g2[...], g3[...]))
        o_lr[...] = d_lr
        o_li[...] = d_li
        o_ld[...] = _dot(jnp.broadcast_to(d_ld, (8, N_STATE)), ind[...], ((1,), (0,)), HI)
        o_br[...] = d_br
        o_bi[...] = d_bi

    return pl.pallas_call(
        body, name="s5_param_bwd",
        out_shape=[_sds((1, N_STATE))] * 2 + [_sds((8, LANES))] + [_sds((S5_GROUP, N_STATE))] * 2,
    )(lam_re, lam_im, logdt, b_re, b_im, d_lbr, d_lbi, d_bbr, d_bbi, group_ind)


def _fwd_in(x, norm_g, w_in_bf, tt):
    L = x.shape[0]

    def body(x_ref, g_ref, w_ref, u_ref, zs_ref, rw_ref, zr_ref):
        h = _rms(x_ref[...], g_ref[...])
        proj = jnp.dot(h.astype(BF16), w_ref[...], preferred_element_type=F32)
        u_ref[...] = proj[:, 0:D_S5]
        zs_ref[...] = proj[:, D_S5:2 * D_S5]
        rw_ref[...] = proj[:, 2 * D_S5:2 * D_S5 + D_SHIFT]
        zr_ref[...] = proj[:, 2 * D_S5 + D_SHIFT:D_IN]

    row = lambda n: pl.BlockSpec((tt, n), lambda i: (i, 0))
    return pl.pallas_call(
        body, name="fwd_in", grid=(L // tt,),
        in_specs=[row(D_MODEL), _const_spec((1, D_MODEL)), _const_spec((D_MODEL, D_IN))],
        out_specs=[row(D_S5), row(D_S5), row(D_SHIFT), row(D_RWKV)],
        out_shape=[_sds((L, D_S5)), _sds((L, D_S5)), _sds((L, D_SHIFT)), _sds((L, D_RWKV))],
        compiler_params=_params("parallel"),
    )(x, norm_g, w_in_bf)


S5_LANE_CHUNK = 512


def _tile_scan(re_ref, im_ref, pow_r_ref, pow_i_ref, carry_r_ref, carry_i_ref, reverse):
    t, n = re_ref.shape
    n_groups = t // 8
    ch = S5_LANE_CHUNK
    rid = lax.broadcasted_iota(jnp.int32, (8, ch), 0)
    for c in range(n // ch):
        cols = slice(c * ch, (c + 1) * ch)
        pow_r = pow_r_ref[:, cols]
        pow_i = pow_i_ref[:, cols]
        row = lambda tile, j: jnp.broadcast_to(tile[j:j + 1], (8, ch))
        levels = [(d, row(pow_r, 8 - d if reverse else d - 1), row(pow_i, 8 - d if reverse else d - 1))
                  for d in (1, 2, 4)]

        def group(g, carry):
            r0 = pl.multiple_of(((n_groups - 1 - g) if reverse else g) * 8, 8)
            xr = re_ref[pl.ds(r0, 8), cols]
            xi = im_ref[pl.ds(r0, 8), cols]
            for d, lr, li in levels:
                keep = (rid < 8 - d) if reverse else (rid >= d)
                shift = (8 - d) if reverse else d
                sr = jnp.where(keep, pltpu.roll(xr, shift, axis=0), 0.0)
                si = jnp.where(keep, pltpu.roll(xi, shift, axis=0), 0.0)
                mr, mi = _cmul(lr, li, sr, si)
                xr = xr + mr
                xi = xi + mi
            mr, mi = _cmul(pow_r, pow_i, carry[0], carry[1])
            xr = xr + mr
            xi = xi + mi
            re_ref[pl.ds(r0, 8), cols] = xr
            im_ref[pl.ds(r0, 8), cols] = xi
            last = 0 if reverse else 7
            return row(xr, last), row(xi, last)

        out = lax.fori_loop(0, n_groups, group, (carry_r_ref[:, cols], carry_i_ref[:, cols]))
        carry_r_ref[:, cols] = out[0]
        carry_i_ref[:, cols] = out[1]


def _s5_fwd(u, b4_re, b4_im, c4_re, c4_im, pow_r, pow_i, tt):
    L = u.shape[0]

    def body(u_ref, bre_ref, bim_ref, cre_ref, cim_ref, pr_ref, pi_ref, sre_o, sim_o, y_o, car_r, car_i):
        @pl.when(pl.program_id(0) == 0)
        def _():
            car_r[...] = jnp.zeros_like(car_r)
            car_i[...] = jnp.zeros_like(car_i)

        uv = u_ref[...]
        for q in range(S5_BLOCKS):
            uq = uv[:, q * LANES:(q + 1) * LANES]
            cols = slice(q * 512, (q + 1) * 512)
            sre_o[:, cols] = _dot_bf(uq, bre_ref[q], ((1,), (0,)))
            sim_o[:, cols] = _dot_bf(uq, bim_ref[q], ((1,), (0,)))
        _tile_scan(sre_o, sim_o, pr_ref, pi_ref, car_r, car_i, reverse=False)
        for q in range(S5_BLOCKS):
            cols = slice(q * 512, (q + 1) * 512)
            y_o[:, q * LANES:(q + 1) * LANES] = (_dot_bf(sre_o[:, cols], cre_ref[q], ((1,), (0,)))
                                                 - _dot_bf(sim_o[:, cols], cim_ref[q], ((1,), (0,))))

    row = lambda n: pl.BlockSpec((tt, n), lambda i: (i, 0))
    return pl.pallas_call(
        body, name="s5_fwd", grid=(L // tt,),
        in_specs=[row(D_S5)] + [_const_spec((S5_BLOCKS, LANES, 512))] * 2 + [_const_spec((S5_BLOCKS, 512, LANES))] * 2
        + [_const_spec((8, N_STATE))] * 2,
        out_specs=[row(N_STATE), row(N_STATE), row(D_S5)],
        out_shape=[_sds((L, N_STATE)), _sds((L, N_STATE)), _sds((L, D_S5))],
        scratch_shapes=[pltpu.VMEM((8, N_STATE), F32)] * 2,
        compiler_params=_params("arbitrary"),
    )(u, b4_re, b4_im, c4_re, c4_im, pow_r, pow_i)


def _rwkv_pre_fn(r, k, v, wa, w0, w2p, a0, a2p, k_k, k_a, ee):
    w = -_softplus(-(w0 + mm_bf(jnp.tanh(wa), w2p))) - 0.5
    logw = -jnp.exp(w)
    a = _sigmoid(a0 + mm_bf(wa, a2p))
    kkp = k * k_k
    kk = kkp / jnp.maximum(jnp.sqrt(head_sum(kkp * kkp, ee)), 1e-12)
    k2 = k * (1.0 + (a - 1.0) * k_a)
    return r, logw, k2, v, -kk, kk * a


def _head_spec(tt):
    return pl.BlockSpec((N_HEADS, tt, HEAD), lambda i: (0, i, 0))


def _load_heads(ref):
    return jnp.concatenate([ref[h] for h in range(N_HEADS)], axis=-1)


def _store_heads(ref, val):
    for h in range(N_HEADS):
        ref[h] = val[:, h * HEAD:(h + 1) * HEAD]


def _shifted(rw, prev_blk, first):
    rolled = pltpu.roll(rw, 1, axis=0)
    prev_row = jnp.where(first, 0.0, prev_blk[7:8, :])
    rid = lax.broadcasted_iota(jnp.int32, rw.shape, 0)
    return jnp.where(rid == 0, jnp.broadcast_to(prev_row, rw.shape), rolled)


def _split_rw(t):
    return t[:, 0:512], t[:, 512:1024], t[:, 1024:1536], t[:, 1536:1664]


def _rwkv_pre_specs(tt):
    row = pl.BlockSpec((tt, D_SHIFT), lambda i: (i, 0))
    prev = pl.BlockSpec((8, D_SHIFT), lambda i: (jnp.maximum(i * (tt // 8) - 1, 0), 0))
    consts = [_const_spec((1, D_SHIFT)), _const_spec((1, D_RWKV)), _const_spec((LANES, D_RWKV)),
              _const_spec((1, D_RWKV)), _const_spec((LANES, D_RWKV)), _const_spec((1, D_RWKV)),
              _const_spec((1, D_RWKV)), _const_spec((D_RWKV, D_RWKV))]
    return [row, prev] + consts


def _rwkv_pre_fwd(rw, mu, w0, w2p, a0, a2p, k_k, k_a, ee, tt):
    L = rw.shape[0]

    def body(rw_ref, prev_ref, mu_ref, w0_ref, w2_ref, a0_ref, a2_ref, kk_ref, ka_ref, ee_ref, *outs):
        rwv = rw_ref[...]
        rws = rwv + (_shifted(rwv, prev_ref[...], pl.program_id(0) == 0) - rwv) * mu_ref[...]
        res = _rwkv_pre_fn(*_split_rw(rws), w0_ref[...], w2_ref[...], a0_ref[...], a2_ref[...],
                           kk_ref[...], ka_ref[...], ee_ref[...])
        for o, val in zip(outs, res):
            _store_heads(o, val)

    return pl.pallas_call(
        body, name="rwkv_pre_fwd", grid=(L // tt,),
        in_specs=_rwkv_pre_specs(tt), out_specs=[_head_spec(tt)] * 6, out_shape=[_sds((N_HEADS, L, HEAD))] * 6,
        compiler_params=_params("parallel"),
    )(rw, rw, mu, w0, w2p, a0, a2p, k_k, k_a, ee)


def _rwkv_pre_bwd(rw, mu, w0, w2p, a0, a2p, k_k, k_a, ee, cots, tt):
    L = rw.shape[0]
    n_t = L // tt

    def body(rw_ref, prev_ref, mu_ref, w0_ref, w2_ref, a0_ref, a2_ref, kk_ref, ka_ref, ee_ref,
             c_r, c_w, c_k, c_v, c_a, c_b, cb_r, cb_k, cb_v,
             drws_ref, dmu_o, dw0_o, dw2_o, da0_o, da2_o, dkk_o, dka_o,
             dmu, dw0, dw2, da0, da2, dkk, dka):
        i = pl.program_id(0)
        accs = (dmu, dw0, dw2, da0, da2, dkk, dka)

        @pl.when(i == 0)
        def _():
            for acc in accs:
                acc[...] = jnp.zeros_like(acc)

        rwv = rw_ref[...]
        diff = _shifted(rwv, prev_ref[...], i == 0) - rwv
        rws = rwv + diff * mu_ref[...]
        consts = (w0_ref[...], w2_ref[...], a0_ref[...], a2_ref[...], kk_ref[...], ka_ref[...])
        _, vjp = jax.vjp(lambda *a: _rwkv_pre_fn(*a, ee_ref[...]), *_split_rw(rws), *consts)
        scan = [_load_heads(c) for c in (c_r, c_w, c_k, c_v, c_a, c_b)]
        g = vjp((scan[0] + cb_r[...], scan[1], scan[2] + cb_k[...], scan[3] + cb_v[...], scan[4], scan[5]))
        drws = jnp.concatenate(g[0:4], axis=-1)
        drws_ref[...] = drws
        dmu[...] += jnp.sum(drws * diff, axis=0, keepdims=True)
        for acc, val in zip(accs[1:], g[4:]):
            acc[...] += val

        @pl.when(i == n_t - 1)
        def _():
            for acc, out in zip(accs, (dmu_o, dw0_o, dw2_o, da0_o, da2_o, dkk_o, dka_o)):
                out[...] = acc[...]

    row = pl.BlockSpec((tt, D_RWKV), lambda i: (i, 0))
    shapes = [(1, D_SHIFT), (1, D_RWKV), (LANES, D_RWKV), (1, D_RWKV), (LANES, D_RWKV), (1, D_RWKV), (1, D_RWKV)]
    return pl.pallas_call(
        body, name="rwkv_pre_bwd", grid=(n_t,),
        in_specs=_rwkv_pre_specs(tt) + [_head_spec(tt)] * 6 + [row] * 3,
        out_specs=[pl.BlockSpec((tt, D_SHIFT), lambda i: (i, 0))] + [_acc_spec(s) for s in shapes],
        out_shape=[_sds((L, D_SHIFT))] + [_sds(s) for s in shapes],
        scratch_shapes=[pltpu.VMEM(s, F32) for s in shapes],
        compiler_params=_params("arbitrary"),
    )(rw, rw, mu, w0, w2p, a0, a2p, k_k, k_a, ee, *cots)


def _bmm(a, b):
    return lax.dot_general(a, b, (((2,), (1,)), ((0,), (0,))), precision=HI, preferred_element_type=F32)


def _bmm_nt(a, b):
    return lax.dot_general(a, b, (((2,), (2,)), ((0,), (0,))), precision=HI, preferred_element_type=F32)


def _bmm_tn(a, b):
    return lax.dot_general(a, b, (((1,), (1,)), ((0,), (0,))), precision=HI, preferred_element_type=F32)


def _bdot_bf(a, b, lhs_dim, rhs_dim):
    return lax.dot_general(a.astype(BF16), b.astype(BF16), (((lhs_dim,), (rhs_dim,)), ((0,), (0,))),
                           preferred_element_type=F32)


@jax.custom_vjp
def _bmm_bf(a, b):
    return _bdot_bf(a, b, 2, 1)


def _bmm_bf_fwd(a, b):
    return _bmm_bf(a, b), (a, b)


def _bmm_bf_bwd(res, g):
    a, b = res
    return _bdot_bf(g, b, 2, 2), _bdot_bf(a, g, 1, 1)


_bmm_bf.defvjp(_bmm_bf_fwd, _bmm_bf_bwd)


@jax.custom_vjp
def _bmm_tn_bf(a, b):
    return _bdot_bf(a, b, 1, 1)


def _bmm_tn_bf_fwd(a, b):
    return _bmm_tn_bf(a, b), (a, b)


def _bmm_tn_bf_bwd(res, g):
    a, b = res
    return _bdot_bf(b, g, 2, 2), _bdot_bf(a, g, 2, 1)


_bmm_tn_bf.defvjp(_bmm_tn_bf_fwd, _bmm_tn_bf_bwd)


def _unit_lower_inverse(a):
    t = a.shape[-1]
    ti = lax.broadcasted_iota(jnp.int32, (t, t), 0)
    si = lax.broadcasted_iota(jnp.int32, (t, t), 1)
    inv = jnp.where(ti == si, 1.0, 0.0)[None] + a
    pw = _bmm(a, a)
    n = 2
    while n < t:
        both = _bmm(jnp.concatenate([inv, pw], axis=1), pw)
        inv = inv + both[:, :t]
        pw = both[:, t:]
        n *= 2
    return inv


@jax.custom_vjp
def _solve_unit_lower(a, rhs, inv):
    return _bmm(inv, rhs)


def _solve_fwd(a, rhs, inv):
    u = _bmm(inv, rhs)
    return u, (inv, u)


def _solve_bwd(res, du):
    inv, u = res
    d_rhs = _bmm_tn(inv, du)
    return _bmm_nt(d_rhs, u), d_rhs, jnp.zeros_like(inv)


_solve_unit_lower.defvjp(_solve_fwd, _solve_bwd)


def _rwkv_chunk(st0, r, logw, k, v, a, b, inv=None):
    n_h, t, _ = r.shape
    ti = lax.broadcasted_iota(jnp.int32, (t, t), 0)
    si = lax.broadcasted_iota(jnp.int32, (t, t), 1)
    ones_tri = jnp.broadcast_to(jnp.where(ti >= si, 1.0, 0.0)[None], (n_h, t, t))
    log_p = _bmm(ones_tri, logw)
    p_in = jnp.exp(log_p)
    p_inv = jnp.exp(-log_p)
    at = a * jnp.exp(log_p - logw)
    rt = r * p_in
    bk = jnp.concatenate([b * p_inv, k * p_inv], axis=1)
    ri = lax.broadcasted_iota(jnp.int32, (2 * t, 2 * t), 0)
    ci = lax.broadcasted_iota(jnp.int32, (2 * t, 2 * t), 1)
    top_rows = ri < t
    diff = jnp.where(top_rows, ri, ri - t) - jnp.where(ci < t, ci, ci - t)
    mask = (diff >= jnp.where(top_rows, 1, 0))[None]
    m = jnp.where(mask, _bmm_nt(jnp.concatenate([at, rt], axis=1), bk), 0.0)
    top, bottom = m[:, :t], m[:, t:]
    a_ab = top[:, :, :t]
    if inv is None:
        inv = _unit_lower_inverse(a_ab)
    rhs = _bmm_bf(jnp.concatenate([at, top[:, :, t:]], axis=2), jnp.concatenate([st0, v], axis=1))
    u = _solve_unit_lower(a_ab, rhs, inv)
    y = _bmm_bf(jnp.concatenate([rt, bottom], axis=2), jnp.concatenate([st0, u, v], axis=1))
    p_end = jnp.swapaxes(p_in[:, t - 1:t, :], 1, 2)
    st1 = (st0 + _bmm_tn_bf(bk, jnp.concatenate([u, v], axis=1))) * p_end
    return y, st1, inv


def _rwkv_scan_fwd(ops):
    n_h, L, n = ops[0].shape
    t = RWKV_CHUNK
    per = min(RWKV_CHUNKS_PER_STEP, L // t)
    n_c = L // t
    n_s = n_c // per

    def body(r_ref, w_ref, k_ref, v_ref, a_ref, b_ref, y_ref, st_ref, inv_ref, st):
        @pl.when(pl.program_id(0) == 0)
        def _():
            st[...] = jnp.zeros_like(st)

        st0 = st[...]
        for j in range(per):
            rows = slice(j * t, (j + 1) * t)
            st_ref[j] = st0
            y, st0, inv = _rwkv_chunk(st0, *(ref[:, rows, :] for ref in (r_ref, w_ref, k_ref, v_ref, a_ref, b_ref)))
            y_ref[:, rows, :] = y
            inv_ref[j] = inv
        st[...] = st0

    blk = pl.BlockSpec((n_h, per * t, n), lambda c: (0, c, 0))
    return pl.pallas_call(
        body, name="rwkv_scan_fwd", grid=(n_s,), in_specs=[blk] * 6,
        out_specs=[blk, pl.BlockSpec((per, n_h, n, n), lambda c: (c, 0, 0, 0)),
                   pl.BlockSpec((per, n_h, t, t), lambda c: (c, 0, 0, 0))],
        out_shape=[_sds((n_h, L, n)), _sds((n_c, n_h, n, n)), _sds((n_c, n_h, t, t))],
        scratch_shapes=[pltpu.VMEM((n_h, n, n), F32)],
        compiler_params=_params("arbitrary"),
    )(*ops)


def _rwkv_scan_bwd(ops, states, invs, dy):
    n_h, L, n = ops[0].shape
    t = RWKV_CHUNK
    per = min(RWKV_CHUNKS_PER_STEP, L // t)
    n_s = L // t // per

    def body(r_ref, w_ref, k_ref, v_ref, a_ref, b_ref, st_ref, inv_ref, dy_ref, dr, dw, dk, dv, da, db, dst):
        @pl.when(pl.program_id(0) == 0)
        def _():
            dst[...] = jnp.zeros_like(dst)

        vjps = []
        for j in range(per):
            rows = slice(j * t, (j + 1) * t)
            inv = inv_ref[j]
            args = [ref[:, rows, :] for ref in (r_ref, w_ref, k_ref, v_ref, a_ref, b_ref)]
            vjps.append(jax.vjp(lambda *a, inv=inv: _rwkv_chunk(*a, inv=inv)[:2], st_ref[j], *args)[1])
        d_state = dst[...]
        for j in reversed(range(per)):
            rows = slice(j * t, (j + 1) * t)
            g = vjps[j]((dy_ref[:, rows, :], d_state))
            d_state = g[0]
            for out, val in zip((dr, dw, dk, dv, da, db), g[1:]):
                out[:, rows, :] = val
        dst[...] = d_state

    blk = pl.BlockSpec((n_h, per * t, n), lambda c: (0, n_s - 1 - c, 0))
    per_chunk = lambda m: pl.BlockSpec((per, n_h, m, m), lambda c: (n_s - 1 - c, 0, 0, 0))
    return pl.pallas_call(
        body, name="rwkv_scan_bwd", grid=(n_s,),
        in_specs=[blk] * 6 + [per_chunk(n), per_chunk(t), blk],
        out_specs=[blk] * 6, out_shape=[_sds((n_h, L, n))] * 6,
        scratch_shapes=[pltpu.VMEM((n_h, n, n), F32)],
        compiler_params=_params("arbitrary"),
    )(*ops, states, invs, dy)


def _post_fn(x, u, zs, zr, ysc, r, k2, v, y_ssm, d, glu_w, glu_b, ln_w, ln_b, r_k,
             wo_s5, wo_rwkv, gf, tgt, ee):
    y3 = _gelu(y_ssm + d * u)
    y_s5 = y3 * _sigmoid(mm_bf(y3, glu_w) + glu_b) * _silu(zs)
    mean = head_sum_split(ysc, ee) * (1.0 / HEAD)
    yc = ysc - mean
    var = head_sum(yc * yc, ee) * (1.0 / HEAD)
    gn = yc * lax.rsqrt(var + GN_EPS) * ln_w + ln_b
    bonus = head_sum(r * k2 * r_k, ee) * v
    y_rwkv = (gn + bonus) * _silu(zr)
    x2 = x + mm_bf(y_s5, wo_s5) + mm_bf(y_rwkv, wo_rwkv)
    err = _rms(x2, gf) - tgt
    return 0.5 * jnp.mean(err * err, axis=-1, keepdims=True)


def _post(x, u, zs, zr, ysc, r, k2, v, y_ssm, d, glu_w, glu_b, ln_w, ln_b, r_k, w_out, gf, tgt, ee, tt):
    L = x.shape[0]
    n_t = L // tt
    acc_shapes = [(1, D_S5), (D_S5, D_S5), (1, D_S5), (1, D_RWKV), (1, D_RWKV), (1, D_RWKV),
                  (D_MODEL, D_MODEL), (1, D_MODEL), (8, LANES)]

    def body(x_ref, u_ref, zs_ref, zr_ref, ysc_ref, r_ref, k2_ref, v_ref, yssm_ref,
             d_ref, gw_ref, gb_ref, lw_ref, lb_ref, rk_ref, wo_ref, gf_ref, tgt_ref, ee_ref,
             dx_o, du_o, dzs_o, dzr_o, dysc_o, dr_o, dk2_o, dv_o, dyssm_o,
             dd_o, dgw_o, dgb_o, dlw_o, dlb_o, drk_o, dwo_o, dgf_o, loss_o,
             dd, dgw, dgb, dlw, dlb, drk, dwo, dgf, loss):
        i = pl.program_id(0)
        accs = (dd, dgw, dgb, dlw, dlb, drk, dwo, dgf, loss)

        @pl.when(i == 0)
        def _():
            for acc in accs:
                acc[...] = jnp.zeros_like(acc)

        args = (x_ref[...], u_ref[...], zs_ref[...], zr_ref[...],
                _load_heads(ysc_ref), _load_heads(r_ref), _load_heads(k2_ref), _load_heads(v_ref), yssm_ref[...],
                d_ref[...], gw_ref[...], gb_ref[...], lw_ref[...], lb_ref[...], rk_ref[...],
                wo_ref[0:D_S5, :], wo_ref[D_S5:D_MODEL, :], gf_ref[...])
        rows, vjp = jax.vjp(lambda *a: _post_fn(*a, tgt_ref[...], ee_ref[...]), *args)
        g = vjp(jnp.ones_like(rows))
        for out, val in zip((dx_o, du_o, dzs_o, dzr_o), g[0:4]):
            out[...] = val
        _store_heads(dysc_o, g[4])
        for out, val in zip((dr_o, dk2_o, dv_o, dyssm_o), g[5:9]):
            out[...] = val
        for acc, val in zip((dd, dgw, dgb, dlw, dlb, drk), g[9:15]):
            acc[...] += val
        dwo[0:D_S5, :] += g[15]
        dwo[D_S5:D_MODEL, :] += g[16]
        dgf[...] += g[17]
        loss[...] += jnp.broadcast_to(jnp.sum(rows, axis=0, keepdims=True), loss.shape)

        @pl.when(i == n_t - 1)
        def _():
            for acc, out in zip(accs, (dd_o, dgw_o, dgb_o, dlw_o, dlb_o, drk_o, dwo_o, dgf_o, loss_o)):
                pltpu.sync_copy(acc, out)

    row = lambda n: pl.BlockSpec((tt, n), lambda i: (i, 0))
    in_specs = ([row(D_MODEL)] + [row(512)] * 3 + [_head_spec(tt)] * 4 + [row(D_S5)]
                + [_const_spec(s) for s in [(1, D_S5), (D_S5, D_S5), (1, D_S5), (1, D_RWKV), (1, D_RWKV), (1, D_RWKV),
                                            (D_MODEL, D_MODEL), (1, D_MODEL)]]
                + [row(D_MODEL), _const_spec((D_RWKV, D_RWKV))])
    out_rows = [D_MODEL] + [512] * 3 + [None] + [512] * 4
    return pl.pallas_call(
        body, name="post_fwd_bwd", grid=(n_t,), in_specs=in_specs,
        out_specs=[row(n) if n else _head_spec(tt) for n in out_rows] + [_ANY] * len(acc_shapes),
        out_shape=([_sds((L, n)) if n else _sds((N_HEADS, L, HEAD)) for n in out_rows]
                   + [_sds(s) for s in acc_shapes]),
        scratch_shapes=[pltpu.VMEM(s, F32) for s in acc_shapes],
        compiler_params=_params("arbitrary"),
    )(x, u, zs, zr, ysc, r, k2, v, y_ssm, d, glu_w, glu_b, ln_w, ln_b, r_k, w_out, gf, tgt, ee)


def _s5_bwd(u, du_direct, dy, s_re, s_im, b4_re, b4_im, c4_re, c4_im, pow_r, pow_i, tt):
    L = u.shape[0]
    n_t = L // tt
    acc_shapes = ([(S5_BLOCKS, LANES, 512)] * 2 + [(S5_BLOCKS, 512, LANES)] * 2 + [(1, N_STATE)] * 2)

    def body(u_ref, dud_ref, dy_ref, sre_ref, sim_ref, pre_ref, pim_ref, bre_ref, bim_ref, cre_ref, cim_ref,
             pr_ref, pi_ref, du_o, dbre_o, dbim_o, dcre_o, dcim_o, dlr_o, dli_o,
             dbre, dbim, dcre, dcim, dlr, dli, gre, gim, car_r, car_i):
        i = pl.program_id(0)

        @pl.when(i == 0)
        def _():
            for acc in (dbre, dbim, dcre, dcim, dlr, dli, car_r, car_i):
                acc[...] = jnp.zeros_like(acc)

        uv = u_ref[...]
        dyv = dy_ref[...]
        blocks = [slice(q * 512, (q + 1) * 512) for q in range(S5_BLOCKS)]
        lanes = [slice(q * LANES, (q + 1) * LANES) for q in range(S5_BLOCKS)]
        for q in range(S5_BLOCKS):
            gre[:, blocks[q]] = _dot_bf(dyv[:, lanes[q]], cre_ref[q], ((1,), (1,)))
            gim[:, blocks[q]] = -_dot_bf(dyv[:, lanes[q]], cim_ref[q], ((1,), (1,)))
        _tile_scan(gre, gim, pr_ref, pi_ref, car_r, car_i, reverse=True)
        for q in range(S5_BLOCKS):
            gr = gre[:, blocks[q]]
            gi = gim[:, blocks[q]]
            sr = sre_ref[:, blocks[q]]
            si = sim_ref[:, blocks[q]]
            du_o[:, lanes[q]] = (dud_ref[:, lanes[q]] + _dot_bf(gr, bre_ref[q], ((1,), (1,)))
                                 + _dot_bf(gi, bim_ref[q], ((1,), (1,))))
            dbre[q] += _dot_bf(uv[:, lanes[q]], gr, ((0,), (0,)))
            dbim[q] += _dot_bf(uv[:, lanes[q]], gi, ((0,), (0,)))
            dcre[q] += _dot_bf(sr, dyv[:, lanes[q]], ((0,), (0,)))
            dcim[q] -= _dot_bf(si, dyv[:, lanes[q]], ((0,), (0,)))
            rid = lax.broadcasted_iota(jnp.int32, sr.shape, 0)
            first = i == n_t - 1
            prev_r = jnp.where(first, 0.0, pre_ref[7:8, blocks[q]])
            prev_i = jnp.where(first, 0.0, pim_ref[7:8, blocks[q]])
            pr = jnp.where(rid == 0, jnp.broadcast_to(prev_r, sr.shape), pltpu.roll(sr, 1, axis=0))
            pi_ = jnp.where(rid == 0, jnp.broadcast_to(prev_i, si.shape), pltpu.roll(si, 1, axis=0))
            dlr[:, blocks[q]] += jnp.sum(pr * gr + pi_ * gi, axis=0, keepdims=True)
            dli[:, blocks[q]] += jnp.sum(pr * gi - pi_ * gr, axis=0, keepdims=True)

        @pl.when(i == n_t - 1)
        def _():
            for acc, out in zip((dbre, dbim, dcre, dcim, dlr, dli), (dbre_o, dbim_o, dcre_o, dcim_o, dlr_o, dli_o)):
                out[...] = acc[...]

    row = lambda n: pl.BlockSpec((tt, n), lambda i: (n_t - 1 - i, 0))
    prev = pl.BlockSpec((8, N_STATE), lambda i: (jnp.maximum((n_t - 1 - i) * (tt // 8) - 1, 0), 0))
    return pl.pallas_call(
        body, name="s5_bwd", grid=(n_t,),
        in_specs=[row(D_S5)] * 3 + [row(N_STATE)] * 2 + [prev] * 2
        + [_const_spec((S5_BLOCKS, LANES, 512))] * 2 + [_const_spec((S5_BLOCKS, 512, LANES))] * 2
        + [_const_spec((8, N_STATE))] * 2,
        out_specs=[row(D_S5)] + [_acc_spec(s) for s in acc_shapes],
        out_shape=[_sds((L, D_S5))] + [_sds(s) for s in acc_shapes],
        scratch_shapes=[pltpu.VMEM(s, F32) for s in acc_shapes] + [pltpu.VMEM((tt, N_STATE), F32)] * 2
        + [pltpu.VMEM((8, N_STATE), F32)] * 2,
        compiler_params=_params("arbitrary"),
    )(u, du_direct, dy, s_re, s_im, s_re, s_im, b4_re, b4_im, c4_re, c4_im, pow_r, pow_i)


def _bwd_in(x, norm_g, w_in_bf, mu, dx2, du, dzs, drws, dzr, tt):
    L = x.shape[0]
    n_t = L // tt

    def body(x_ref, g_ref, w_ref, mu_ref, dx2_ref, du_ref, dzs_ref, drws_ref, nxt_ref, dzr_ref,
             gx_o, dw_o, dg_o, dproj, dw, dg):
        i = pl.program_id(0)

        @pl.when(i == 0)
        def _():
            dw[...] = jnp.zeros_like(dw)
            dg[...] = jnp.zeros_like(dg)

        drws_v = drws_ref[...]
        rid = lax.broadcasted_iota(jnp.int32, drws_v.shape, 0)
        nxt_row = jnp.where(i == n_t - 1, 0.0, nxt_ref[0:1, :])
        nxt = jnp.where(rid == tt - 1, jnp.broadcast_to(nxt_row, drws_v.shape), pltpu.roll(drws_v, tt - 1, axis=0))
        muv = mu_ref[...]
        drw = drws_v * (1.0 - muv) + nxt * muv
        dproj[:, 0:D_S5] = du_ref[...].astype(BF16)
        dproj[:, D_S5:2 * D_S5] = dzs_ref[...].astype(BF16)
        dproj[:, 2 * D_S5:2 * D_S5 + D_SHIFT] = drw.astype(BF16)
        dproj[:, 2 * D_S5 + D_SHIFT:D_IN] = dzr_ref[...].astype(BF16)
        dh = _dot(dproj[...], w_ref[...], ((1,), (1,)), None)
        h, vjp = jax.vjp(_rms, x_ref[...], g_ref[...])
        dxh, dgv = vjp(dh)
        gx_o[...] = dx2_ref[...] + dxh
        dg[...] += dgv
        dw[...] += _dot(h.astype(BF16), dproj[...], ((0,), (0,)), None)

        @pl.when(i == n_t - 1)
        def _():
            dg_o[...] = dg[...]
            pltpu.sync_copy(dw, dw_o)

    row = lambda n: pl.BlockSpec((tt, n), lambda i: (i, 0))
    nxt = pl.BlockSpec((8, D_SHIFT), lambda i: (jnp.minimum((i + 1) * (tt // 8), L // 8 - 1), 0))
    return pl.pallas_call(
        body, name="bwd_in", grid=(n_t,),
        in_specs=[row(D_MODEL), _const_spec((1, D_MODEL)), _const_spec((D_MODEL, D_IN)), _const_spec((1, D_SHIFT)),
                  row(D_MODEL), row(D_S5), row(D_S5), row(D_SHIFT), nxt, row(D_RWKV)],
        out_specs=[row(D_MODEL), _ANY, _acc_spec((1, D_MODEL))],
        out_shape=[_sds((L, D_MODEL)), _sds((D_MODEL, D_IN)), _sds((1, D_MODEL))],
        scratch_shapes=[pltpu.VMEM((tt, D_IN), BF16), pltpu.VMEM((D_MODEL, D_IN), F32), pltpu.VMEM((1, D_MODEL), F32)],
        compiler_params=_params("arbitrary"),
    )(x, norm_g, w_in_bf, mu, dx2, du, dzs, drws, drws, dzr)


def _block_diag_b(bbar):
    bb = bbar.reshape(S5_GROUP, S5_BLOCKS, 8, S5_STATE)
    return jnp.einsum('hqgp,Gg->qGhgp', bb, jnp.eye(8, dtype=F32)).reshape(S5_BLOCKS, LANES, 512)


def _block_diag_b_t(db4):
    d = db4.reshape(S5_BLOCKS, 8, S5_GROUP, 8, S5_STATE)
    return jnp.einsum('qGhgp,Gg->hqgp', d, jnp.eye(8, dtype=F32)).reshape(S5_GROUP, N_STATE)


def _block_diag_c(c):
    cc = c.reshape(S5_BLOCKS, 8, S5_GROUP, S5_STATE)
    return jnp.einsum('qghp,gG->qgpGh', cc, jnp.eye(8, dtype=F32)).reshape(S5_BLOCKS, 512, LANES)


def _block_diag_c_t(dc4):
    d = dc4.reshape(S5_BLOCKS, 8, S5_STATE, 8, S5_GROUP)
    return jnp.einsum('qgpGh,gG->qghp', d, jnp.eye(8, dtype=F32)).reshape(S5_GROUPS, S5_GROUP, S5_STATE)


def _local_step(x, tgt, w):
    L = x.shape[0]
    tt = min(512, L)
    tp = min(256, L)
    ee = _head_sum_matrix()

    lam_re = w['s5_lam_re'].reshape(1, N_STATE)
    lam_im = w['s5_lam_im'].reshape(1, N_STATE)
    logdt = jnp.repeat(w['s5_log_dt'], S5_STATE).reshape(1, N_STATE)
    b_re_t = w['s5_b_re'].transpose(2, 0, 1).reshape(S5_GROUP, N_STATE)
    b_im_t = w['s5_b_im'].transpose(2, 0, 1).reshape(S5_GROUP, N_STATE)
    bbr, bbi, pow_r, pow_i, rpow_r, rpow_i = _s5_param_fwd(lam_re, lam_im, logdt, b_re_t, b_im_t)
    b4_re, b4_im = _block_diag_b(bbr), _block_diag_b(bbi)
    c4_re, c4_im = _block_diag_c(w['s5_c_re']), _block_diag_c(w['s5_c_im'])

    norm_g = w['norm_g'].reshape(1, D_MODEL)
    w_in_bf = w['w_in'].astype(BF16)
    u, zs, rw, zr = _fwd_in(x, norm_g, w_in_bf, tt)
    s_re, s_im, y_ssm = _s5_fwd(u, b4_re, b4_im, c4_re, c4_im, pow_r, pow_i, tt)

    row = lambda t: t.reshape(1, -1)
    zpad = jnp.zeros((HEAD, D_RWKV), F32)
    w2p = jnp.concatenate([w['rwkv_w2'], zpad], axis=0)
    a2p = jnp.concatenate([zpad, w['rwkv_a2']], axis=0)
    pre_consts = (row(w['rwkv_mu']), row(w['rwkv_w0']), w2p, row(w['rwkv_a0']), a2p,
                  row(w['rwkv_k_k']), row(w['rwkv_k_a']), ee)
    ops = _rwkv_pre_fwd(rw, *pre_consts, tt)
    ysc, states, invs = _rwkv_scan_fwd(ops)

    post = _post(x, u, zs, zr, ysc, ops[0], ops[2], ops[3], y_ssm,
                 row(w['s5_d']), w['s5_glu_w'], row(w['s5_glu_b']), row(w['rwkv_ln_w']), row(w['rwkv_ln_b']),
                 row(w['rwkv_r_k']), w['w_out'], row(w['final_g']), tgt, ee, tp)
    (dx2, du_d, dzs, dzr, dysc, dr_b, dk2_b, dv_b, dy_ssm,
     dd, dglu_w, dglu_b, dln_w, dln_b, dr_k, dw_out, dgf, loss) = post

    du, db4_re, db4_im, dc4_re, dc4_im, dlbr, dlbi = _s5_bwd(
        u, du_d, dy_ssm, s_re, s_im, b4_re, b4_im, c4_re, c4_im, rpow_r, rpow_i, tt)
    group_ind = (jnp.arange(N_STATE)[:, None] // S5_STATE == jnp.arange(LANES)[None, :]).astype(F32)
    dlam_re, dlam_im, dlogdt, db_re_t, db_im_t = _s5_param_bwd(
        lam_re, lam_im, logdt, b_re_t, b_im_t, dlbr, dlbi, _block_diag_b_t(db4_re), _block_diag_b_t(db4_im), group_ind)

    cots = list(_rwkv_scan_bwd(ops, states, invs, dysc)) + [dr_b, dk2_b, dv_b]
    drws, dmu, dw0, dw2p, da0, da2p, dk_k, dk_a = _rwkv_pre_bwd(rw, *pre_consts, cots, tt)

    grad_x, dw_in, dnorm_g = _bwd_in(x, norm_g, w_in_bf, row(w['rwkv_mu']), dx2, du, dzs, drws, dzr, tt)

    unb = lambda t: t.reshape(S5_GROUP, S5_GROUPS, S5_STATE).transpose(1, 2, 0)
    grads = {
        'norm_g': dnorm_g.reshape(D_MODEL), 'w_in': dw_in,
        's5_lam_re': dlam_re.reshape(S5_GROUPS, S5_STATE), 's5_lam_im': dlam_im.reshape(S5_GROUPS, S5_STATE),
        's5_log_dt': dlogdt[0, :S5_GROUPS], 's5_b_re': unb(db_re_t), 's5_b_im': unb(db_im_t),
        's5_c_re': _block_diag_c_t(dc4_re), 's5_c_im': _block_diag_c_t(dc4_im),
        's5_d': dd.reshape(D_S5), 's5_glu_w': dglu_w, 's5_glu_b': dglu_b.reshape(D_S5),
        'rwkv_mu': dmu.reshape(-1), 'rwkv_w0': dw0.reshape(-1), 'rwkv_w2': dw2p[:HEAD], 'rwkv_a0': da0.reshape(-1),
        'rwkv_a2': da2p[HEAD:], 'rwkv_k_k': dk_k.reshape(-1), 'rwkv_k_a': dk_a.reshape(-1),
        'rwkv_r_k': dr_k.reshape(N_HEADS, HEAD), 'rwkv_ln_w': dln_w.reshape(-1), 'rwkv_ln_b': dln_b.reshape(-1),
        'w_out': dw_out, 'final_g': dgf.reshape(D_MODEL),
    }
    return loss, grad_x, grads


def _exchange(arrays, gather, axes, name):
    n = len(arrays)
    group = 2 ** len(axes)

    def body(*refs):
        send_refs, recv_refs = refs[:n], refs[n:2 * n]
        send_sems, recv_sems, local_sems = refs[2 * n:]
        pos = {ax: lax.axis_index(ax) for ax in ("x", "y", "c")}

        def index_of(p):
            idx = 0
            for ax in axes:
                idx = 2 * idx + p[ax]
            return idx

        me = index_of(pos)
        own, outs, arrivals = [], [], []
        for i, (send_ref, recv_ref) in enumerate(zip(send_refs, recv_refs)):
            def block_for(dev, send_ref=send_ref, whole=gather[i]):
                return send_ref if whole else send_ref.at[dev]

            own.append(pltpu.make_async_copy(block_for(me), recv_ref.at[me], local_sems.at[i]))
            own[-1].start()
            for k in range(1, group):
                peer = dict(pos)
                for bit, ax in enumerate(axes):
                    if (k >> bit) & 1:
                        peer[ax] = 1 - pos[ax]
                peer_idx = index_of(peer)
                sems = dict(send_sem=send_sems.at[i, k - 1], recv_sem=recv_sems.at[i, k - 1],
                            device_id=(peer["x"], peer["y"], peer["c"]), device_id_type=pl.DeviceIdType.MESH)
                outs.append(pltpu.make_async_remote_copy(src_ref=block_for(peer_idx), dst_ref=recv_ref.at[me], **sems))
                outs[-1].start()
                arrivals.append(
                    pltpu.make_async_remote_copy(src_ref=block_for(peer_idx), dst_ref=recv_ref.at[peer_idx], **sems))
        for copy in arrivals:
            copy.wait_recv()
        for copy in outs:
            copy.wait_send()
        for copy in own:
            copy.wait()

    return pl.pallas_call(
        body, name=name, in_specs=[_ANY] * n, out_specs=[_ANY] * n,
        out_shape=[jax.ShapeDtypeStruct(((group,) + a.shape) if whole else a.shape, a.dtype)
                   for a, whole in zip(arrays, gather)],
        scratch_shapes=[pltpu.SemaphoreType.DMA((n, group - 1)), pltpu.SemaphoreType.DMA((n, group - 1)),
                        pltpu.SemaphoreType.DMA((n,))],
        compiler_params=pltpu.CompilerParams(has_side_effects=True),
    )(*arrays)


def _sum_devices(ref):
    g = ref[0].astype(F32)
    for s in range(1, ref.shape[0]):
        g = g + ref[s].astype(F32)
    return g


def _adamw_math(g, w, m, v):
    m_new = ADAM_B1 * m + (1.0 - ADAM_B1) * g
    v_new = ADAM_B2 * v + (1.0 - ADAM_B2) * (g * g)
    m_hat = m_new / (1.0 - ADAM_B1 ** ADAM_STEP)
    v_hat = v_new / (1.0 - ADAM_B2 ** ADAM_STEP)
    return -ADAM_LR * (m_hat / (jnp.sqrt(v_hat) + ADAM_EPS) + ADAM_WD * w), m_new, v_new


def _adamw(gs, ws, ms, vs, reduce, name):
    n = len(ws)

    def body(*refs):
        g_refs, w_refs, m_refs, v_refs = (refs[j * n:(j + 1) * n] for j in range(4))
        outs = refs[4 * n:]
        for i in range(n):
            g = _sum_devices(g_refs[i]) if reduce else g_refs[i][...]
            res = _adamw_math(g, w_refs[i][...], m_refs[i][...], v_refs[i][...])
            for j, val in enumerate(((g,) if reduce else ()) + res):
                outs[j * n + i][...] = val

    return pl.pallas_call(
        body, name=name, out_shape=[_sds(w.shape) for w in ws] * (4 if reduce else 3),
        compiler_params=pltpu.CompilerParams(vmem_limit_bytes=VMEM_LIMIT),
    )(*gs, *ws, *ms, *vs)


def _sum_blocks(recv):
    def body(recv_ref, out_ref):
        out_ref[...] = _sum_devices(recv_ref)

    return pl.pallas_call(body, name="sum_small_grads", out_shape=_sds(recv.shape[1:]))(recv)


_WEIGHTS = [
    ('norm_g', (1, 1024), False), ('w_in', (1, 1024, 400), True), ('s5_lam_re', (1, 32, 64), False),
    ('s5_lam_im', (1, 32, 64), False), ('s5_log_dt', (1, 32), False), ('s5_b_re', (1, 32, 64, 16), False),
    ('s5_b_im', (1, 32, 64, 16), False), ('s5_c_re', (1, 32, 16, 64), False), ('s5_c_im', (1, 32, 16, 64), False),
    ('s5_d', (1, 512), False), ('s5_glu_w', (1, 64, 512), True), ('s5_glu_b', (1, 512), False),
    ('rwkv_mu', (1, 1664), False), ('rwkv_w0', (1, 512), False), ('rwkv_w2', (1, 64, 64), True),
    ('rwkv_a0', (1, 512), False), ('rwkv_a2', (1, 64, 64), True), ('rwkv_k_k', (1, 512), False),
    ('rwkv_k_a', (1, 512), False), ('rwkv_r_k', (1, 8, 64), False), ('rwkv_ln_w', (1, 512), False),
    ('rwkv_ln_b', (1, 512), False), ('w_out', (1, 128, 1024), True), ('final_g', (1024,), False),
]
_SHARDED = [(n, s) for n, s, sharded in _WEIGHTS if sharded]
_SMALL = [(n, s) for n, s, sharded in _WEIGHTS if not sharded]
_COLUMN_SHARDED = ('w_in', 'rwkv_w2', 'rwkv_a2')
_SMALL_ROWS = -(-sum(math.prod(s) for _, s in _SMALL) // (8 * LANES)) * 8


def _pack_small(grads):
    flat = [grads[n].reshape(-1) for n, _ in _SMALL]
    pad = _SMALL_ROWS * LANES - sum(f.size for f in flat)
    return jnp.concatenate(flat + [jnp.zeros((pad,), F32)]).reshape(_SMALL_ROWS, LANES)


def _unpack_small(packed):
    flat = packed.reshape(-1)
    out, off = {}, 0
    for n, s in _SMALL:
        size = math.prod(s)
        out[n] = flat[off:off + size].reshape(s)
        off += size
    return out


_BF16_OPERANDS = ('w_in', 's5_glu_w', 'w_out')


def _join_shards(name, blocks):
    _, rows, cols = blocks.shape
    if name in _COLUMN_SHARDED:
        return blocks.transpose(1, 0, 2).reshape(rows, N_DEV * cols)
    return blocks.reshape(N_DEV * rows, cols)


def _split_shards(name, full, shard_shape):
    rows, cols = shard_shape
    if name in _COLUMN_SHARDED:
        return full.reshape(rows, N_DEV, cols).transpose(1, 0, 2)
    return full.reshape(N_DEV, rows, cols)


def kernel(x, norm_g, w_in, s5_lam_re, s5_lam_im, s5_log_dt, s5_b_re, s5_b_im, s5_c_re, s5_c_im, s5_d, s5_glu_w, s5_glu_b, rwkv_mu, rwkv_w0, rwkv_w2, rwkv_a0, rwkv_a2, rwkv_k_k, rwkv_k_a, rwkv_r_k, rwkv_ln_w, rwkv_ln_b, w_out, final_g, loss_target, m_norm_g, m_w_in, m_s5_lam_re, m_s5_lam_im, m_s5_log_dt, m_s5_b_re, m_s5_b_im, m_s5_c_re, m_s5_c_im, m_s5_d, m_s5_glu_w, m_s5_glu_b, m_rwkv_mu, m_rwkv_w0, m_rwkv_w2, m_rwkv_a0, m_rwkv_a2, m_rwkv_k_k, m_rwkv_k_a, m_rwkv_r_k, m_rwkv_ln_w, m_rwkv_ln_b, m_w_out, m_final_g, v_norm_g, v_w_in, v_s5_lam_re, v_s5_lam_im, v_s5_log_dt, v_s5_b_re, v_s5_b_im, v_s5_c_re, v_s5_c_im, v_s5_d, v_s5_glu_w, v_s5_glu_b, v_rwkv_mu, v_rwkv_w0, v_rwkv_w2, v_rwkv_a0, v_rwkv_a2, v_rwkv_k_k, v_rwkv_k_a, v_rwkv_r_k, v_rwkv_ln_w, v_rwkv_ln_b, v_w_out, v_final_g):
    given = dict(locals())

    n_sh = len(_SHARDED)
    everyone = ("x", "y", "c")
    shards = [given[n][0].astype(BF16 if n in _BF16_OPERANDS else F32) for n, _ in _SHARDED]
    gathered = _exchange(shards, (True,) * n_sh, everyone, "gather_weights")
    local = {n: _join_shards(n, blocks).astype(F32 if n != 'w_in' else BF16)
             for (n, _), blocks in zip(_SHARDED, gathered)}
    local.update({n: (given[n][0] if len(s) > 1 else given[n]) for n, s in _SMALL})

    loss, grad_x, grads = _local_step(x[0], loss_target[0], local)

    blocks = [_split_shards(n, grads[n], s[1:]).astype(BF16) for n, s in _SHARDED]
    small = _pack_small(grads).reshape(N_DEV, _SMALL_ROWS // N_DEV, LANES)
    recv = _exchange(blocks + [small], (False,) * (n_sh + 1), everyone, "exchange_grads")
    small_sum = _exchange([_sum_blocks(recv[-1])], (True,), everyone, "gather_small_grads")[0]

    result = {}
    for group, name in (([0], "adamw_w_in"), ([1, 2, 3, 4], "adamw_shards")):
        ns = [_SHARDED[i][0] for i in group]
        res = _adamw([recv[i] for i in group], [given[n][0] for n in ns], [given['m_' + n][0] for n in ns],
                     [given['v_' + n][0] for n in ns], True, name)
        for j, n in enumerate(ns):
            result[n] = [res[k * len(ns) + j][None] for k in range(4)]
    g_small = _unpack_small(small_sum)
    two_d = lambda t: t.reshape(1, -1) if t.ndim == 1 else t
    ns = [n for n, _ in _SMALL]
    res = _adamw([two_d(g_small[n]) for n in ns], [two_d(given[n]) for n in ns], [two_d(given['m_' + n]) for n in ns],
                 [two_d(given['v_' + n]) for n in ns], False, "adamw_small")
    for j, (n, s) in enumerate(_SMALL):
        result[n] = [g_small[n]] + [res[k * len(ns) + j].reshape(s) for k in range(3)]

    total = lax.psum(loss[0, 0], ("x", "y", "c"))
    outs = [total, grad_x[None]]
    for k in range(4):
        outs += [result[n][k] for n, _, _ in _WEIGHTS]
    return tuple(outs)
```

```python
import functools
import math

import jax
import jax.numpy as jnp
from jax import lax
from jax.experimental import pallas as pl
from jax.experimental.pallas import tpu as pltpu

F32 = jnp.float32
BF16 = jnp.bfloat16
HI = lax.Precision.HIGH

D_MODEL = 1024
D_S5 = 512
D_RWKV = 512
S5_GROUPS = 32
S5_GROUP = 16
S5_STATE = 64
N_STATE = S5_GROUPS * S5_STATE
N_HEADS = 8
HEAD = 64
D_SHIFT = 3 * D_RWKV + 128
D_IN = 2 * D_S5 + D_SHIFT + D_RWKV
NORM_EPS = 1e-6
GN_EPS = 64e-5
N_DEV = 8
LANES = 128
S5_BLOCKS = 4
RWKV_CHUNK = 64
RWKV_CHUNKS_PER_STEP = 4
VMEM_LIMIT = 56 * 1024 * 1024

ADAM_LR = 0.001
ADAM_B1 = 0.9
ADAM_B2 = 0.999
ADAM_EPS = 1e-08
ADAM_WD = 0.01
ADAM_STEP = 10


def _dot(a, b, dims, prec):
    return lax.dot_general(a, b, (dims, ((), ())), precision=prec, preferred_element_type=F32)


def _dot_bf(a, b, dims):
    return _dot(a.astype(BF16), b.astype(BF16), dims, None)


def _make_mm(cast, prec):
    @jax.custom_vjp
    def mm(a, b):
        return _dot(cast(a), cast(b), ((1,), (0,)), prec)

    def fwd(a, b):
        return mm(a, b), (a, b)

    def bwd(res, g):
        a, b = res
        return (_dot(cast(g), cast(b), ((1,), (1,)), prec), _dot(cast(a), cast(g), ((0,), (0,)), prec))

    mm.defvjp(fwd, bwd)
    return mm


mm_bf = _make_mm(lambda t: t.astype(BF16), None)


def _make_head_sum(split):
    def product(x, ee):
        hi = x.astype(BF16)
        out = _dot(hi, ee, ((1,), (0,)), None)
        if split:
            out = out + _dot((x - hi.astype(F32)).astype(BF16), ee, ((1,), (0,)), None)
        return out

    @jax.custom_vjp
    def head_sum(x, ee):
        return product(x, ee)

    def fwd(x, ee):
        return product(x, ee), ee

    def bwd(ee, g):
        return product(g, ee), jnp.zeros_like(ee)

    head_sum.defvjp(fwd, bwd)
    return head_sum


head_sum = _make_head_sum(False)
head_sum_split = _make_head_sum(True)


def _sigmoid(x):
    return 1.0 / (1.0 + jnp.exp(-x))


def _silu(x):
    return x * _sigmoid(x)


def _softplus(x):
    return jnp.maximum(x, 0.0) + jnp.log(1.0 + jnp.exp(-jnp.abs(x)))


def _gelu(x):
    return 0.5 * x * (1.0 + jnp.tanh(math.sqrt(2.0 / math.pi) * (x + 0.044715 * x * x * x)))


def _rms(x, g):
    return x * lax.rsqrt(jnp.mean(x * x, axis=-1, keepdims=True) + NORM_EPS) * g


def _const_spec(shape):
    nd = len(shape)
    return pl.BlockSpec(shape, lambda *_: (0,) * nd, pipeline_mode=pl.Buffered(1))


def _acc_spec(shape):
    nd = len(shape)
    return pl.BlockSpec(shape, lambda *_: (0,) * nd)


def _params(sem):
    return pltpu.CompilerParams(dimension_semantics=(sem,), vmem_limit_bytes=VMEM_LIMIT)


_ANY = pl.BlockSpec(memory_space=pl.ANY)


def _sds(shape):
    return jax.ShapeDtypeStruct(shape, F32)


def _head_sum_matrix():
    i = jnp.arange(D_RWKV) // HEAD
    return (i[:, None] == i[None, :]).astype(BF16)


def _s5_param_fn(lam_re, lam_im, logdt, b_re, b_im):
    dt = jnp.exp(logdt)
    mag = jnp.exp(lam_re * dt)
    ang = lam_im * dt
    lbr = mag * jnp.cos(ang)
    lbi = mag * jnp.sin(ang)
    nr = lbr - 1.0
    den = lam_re * lam_re + lam_im * lam_im
    cr = (nr * lam_re + lbi * lam_im) / den
    ci = (lbi * lam_re - nr * lam_im) / den
    return lbr, lbi, cr * b_re - ci * b_im, cr * b_im + ci * b_re


def _cmul(ar, ai, br, bi):
    return ar * br - ai * bi, ar * bi + ai * br


def _s5_param_fwd(lam_re, lam_im, logdt, b_re, b_im):
    def body(lr, li, ld, br, bi, o_br, o_bi, o_pr, o_pi, o_qr, o_qi):
        lbr, lbi, bbr, bbi = _s5_param_fn(lr[...], li[...], ld[...], br[...], bi[...])
        o_br[...] = bbr
        o_bi[...] = bbi
        rid = lax.broadcasted_iota(jnp.int32, (8, N_STATE), 0)
        pr, pi_ = lbr, lbi
        fwd_r = rev_r = jnp.broadcast_to(pr, (8, N_STATE))
        fwd_i = rev_i = jnp.broadcast_to(pi_, (8, N_STATE))
        for j in range(1, 8):
            pr, pi_ = _cmul(pr, pi_, lbr, lbi)
            fwd_r = jnp.where(rid == j, jnp.broadcast_to(pr, (8, N_STATE)), fwd_r)
            fwd_i = jnp.where(rid == j, jnp.broadcast_to(pi_, (8, N_STATE)), fwd_i)
            rev_r = jnp.where(rid == 7 - j, jnp.broadcast_to(pr, (8, N_STATE)), rev_r)
            rev_i = jnp.where(rid == 7 - j, jnp.broadcast_to(pi_, (8, N_STATE)), rev_i)
        o_pr[...] = fwd_r
        o_pi[...] = fwd_i
        o_qr[...] = rev_r
        o_qi[...] = -rev_i

    return pl.pallas_call(
        body, name="s5_param_fwd",
        out_shape=[_sds((S5_GROUP, N_STATE))] * 2 + [_sds((8, N_STATE))] * 4,
    )(lam_re, lam_im, logdt, b_re, b_im)


def _s5_param_bwd(lam_re, lam_im, logdt, b_re, b_im, d_lbr, d_lbi, d_bbr, d_bbi, group_ind):
    def body(lr, li, ld, br, bi, g0, g1, g2, g3, ind, o_lr, o_li, o_ld, o_br, o_bi):
        _, vjp = jax.vjp(_s5_param_fn, lr[...], li[...], ld[...], br[...], bi[...])
        d_lr, d_li, d_ld, d_br, d_bi = vjp((g0[...], g1[...], g2[...], g3[...]))
        o_lr[...] = d_lr
        o_li[...] = d_li
        o_ld[...] = _dot(jnp.broadcast_to(d_ld, (8, N_STATE)), ind[...], ((1,), (0,)), HI)
        o_br[...] = d_br
        o_bi[...] = d_bi

    return pl.pallas_call(
        body, name="s5_param_bwd",
        out_shape=[_sds((1, N_STATE))] * 2 + [_sds((8, LANES))] + [_sds((S5_GROUP, N_STATE))] * 2,
    )(lam_re, lam_im, logdt, b_re, b_im, d_lbr, d_lbi, d_bbr, d_bbi, group_ind)


def _fwd_in(x, norm_g, w_in_bf, tt):
    L = x.shape[0]

    def body(x_ref, g_ref, w_ref, u_ref, zs_ref, rw_ref, zr_ref):
        h = _rms(x_ref[...], g_ref[...])
        proj = jnp.dot(h.astype(BF16), w_ref[...], preferred_element_type=F32)
        u_ref[...] = proj[:, 0:D_S5]
        zs_ref[...] = proj[:, D_S5:2 * D_S5]
        rw_ref[...] = proj[:, 2 * D_S5:2 * D_S5 + D_SHIFT]
        zr_ref[...] = proj[:, 2 * D_S5 + D_SHIFT:D_IN]

    row = lambda n: pl.BlockSpec((tt, n), lambda i: (i, 0))
    return pl.pallas_call(
        body, name="fwd_in", grid=(L // tt,),
        in_specs=[row(D_MODEL), _const_spec((1, D_MODEL)), _const_spec((D_MODEL, D_IN))],
        out_specs=[row(D_S5), row(D_S5), row(D_SHIFT), row(D_RWKV)],
        out_shape=[_sds((L, D_S5)), _sds((L, D_S5)), _sds((L, D_SHIFT)), _sds((L, D_RWKV))],
        compiler_params=_params("parallel"),
    )(x, norm_g, w_in_bf)


S5_LANE_CHUNK = 512


def _tile_scan(re_ref, im_ref, pow_r_ref, pow_i_ref, carry_r_ref, carry_i_ref, reverse):
    t, n = re_ref.shape
    n_groups = t // 8
    ch = S5_LANE_CHUNK
    rid = lax.broadcasted_iota(jnp.int32, (8, ch), 0)
    for c in range(n // ch):
        cols = slice(c * ch, (c + 1) * ch)
        pow_r = pow_r_ref[:, cols]
        pow_i = pow_i_ref[:, cols]
        row = lambda tile, j: jnp.broadcast_to(tile[j:j + 1], (8, ch))
        levels = [(d, row(pow_r, 8 - d if reverse else d - 1), row(pow_i, 8 - d if reverse else d - 1))
                  for d in (1, 2, 4)]

        def group(g, carry):
            r0 = pl.multiple_of(((n_groups - 1 - g) if reverse else g) * 8, 8)
            xr = re_ref[pl.ds(r0, 8), cols]
            xi = im_ref[pl.ds(r0, 8), cols]
            for d, lr, li in levels:
                keep = (rid < 8 - d) if reverse else (rid >= d)
                shift = (8 - d) if reverse else d
                sr = jnp.where(keep, pltpu.roll(xr, shift, axis=0), 0.0)
                si = jnp.where(keep, pltpu.roll(xi, shift, axis=0), 0.0)
                mr, mi = _cmul(lr, li, sr, si)
                xr = xr + mr
                xi = xi + mi
            mr, mi = _cmul(pow_r, pow_i, carry[0], carry[1])
            xr = xr + mr
            xi = xi + mi
            re_ref[pl.ds(r0, 8), cols] = xr
            im_ref[pl.ds(r0, 8), cols] = xi
            last = 0 if reverse else 7
            return row(xr, last), row(xi, last)

        out = lax.fori_loop(0, n_groups, group, (carry_r_ref[:, cols], carry_i_ref[:, cols]))
        carry_r_ref[:, cols] = out[0]
        carry_i_ref[:, cols] = out[1]


def _s5_fwd(u, b4_re, b4_im, c4_re, c4_im, pow_r, pow_i, tt):
    L = u.shape[0]

    def body(u_ref, bre_ref, bim_ref, cre_ref, cim_ref, pr_ref, pi_ref, sre_o, sim_o, y_o, car_r, car_i):
        @pl.when(pl.program_id(0) == 0)
        def _():
            car_r[...] = jnp.zeros_like(car_r)
            car_i[...] = jnp.zeros_like(car_i)

        uv = u_ref[...]
        for q in range(S5_BLOCKS):
            uq = uv[:, q * LANES:(q + 1) * LANES]
            cols = slice(q * 512, (q + 1) * 512)
            sre_o[:, cols] = _dot_bf(uq, bre_ref[q], ((1,), (0,)))
            sim_o[:, cols] = _dot_bf(uq, bim_ref[q], ((1,), (0,)))
        _tile_scan(sre_o, sim_o, pr_ref, pi_ref, car_r, car_i, reverse=False)
        for q in range(S5_BLOCKS):
            cols = slice(q * 512, (q + 1) * 512)
            y_o[:, q * LANES:(q + 1) * LANES] = (_dot_bf(sre_o[:, cols], cre_ref[q], ((1,), (0,)))
                                                 - _dot_bf(sim_o[:, cols], cim_ref[q], ((1,), (0,))))

    row = lambda n: pl.BlockSpec((tt, n), lambda i: (i, 0))
    return pl.pallas_call(
        body, name="s5_fwd", grid=(L // tt,),
        in_specs=[row(D_S5)] + [_const_spec((S5_BLOCKS, LANES, 512))] * 2 + [_const_spec((S5_BLOCKS, 512, LANES))] * 2
        + [_const_spec((8, N_STATE))] * 2,
        out_specs=[row(N_STATE), row(N_STATE), row(D_S5)],
        out_shape=[_sds((L, N_STATE)), _sds((L, N_STATE)), _sds((L, D_S5))],
        scratch_shapes=[pltpu.VMEM((8, N_STATE), F32)] * 2,
        compiler_params=_params("arbitrary"),
    )(u, b4_re, b4_im, c4_re, c4_im, pow_r, pow_i)


def _rwkv_pre_fn(r, k, v, wa, w0, w2p, a0, a2p, k_k, k_a, ee):
    w = -_softplus(-(w0 + mm_bf(jnp.tanh(wa), w2p))) - 0.5
    logw = -jnp.exp(w)
    a = _sigmoid(a0 + mm_bf(wa, a2p))
    kkp = k * k_k
    kk = kkp / jnp.maximum(jnp.sqrt(head_sum(kkp * kkp, ee)), 1e-12)
    k2 = k * (1.0 + (a - 1.0) * k_a)
    return r, logw, k2, v, -kk, kk * a


def _head_spec(tt):
    return pl.BlockSpec((N_HEADS, tt, HEAD), lambda i: (0, i, 0))


def _load_heads(ref):
    return jnp.concatenate([ref[h] for h in range(N_HEADS)], axis=-1)


def _store_heads(ref, val):
    for h in range(N_HEADS):
        ref[h] = val[:, h * HEAD:(h + 1) * HEAD]


def _shifted(rw, prev_blk, first):
    rolled = pltpu.roll(rw, 1, axis=0)
    prev_row = jnp.where(first, 0.0, prev_blk[7:8, :])
    rid = lax.broadcasted_iota(jnp.int32, rw.shape, 0)
    return jnp.where(rid == 0, jnp.broadcast_to(prev_row, rw.shape), rolled)


def _split_rw(t):
    return t[:, 0:512], t[:, 512:1024], t[:, 1024:1536], t[:, 1536:1664]


def _rwkv_pre_specs(tt):
    row = pl.BlockSpec((tt, D_SHIFT), lambda i: (i, 0))
    prev = pl.BlockSpec((8, D_SHIFT), lambda i: (jnp.maximum(i * (tt // 8) - 1, 0), 0))
    consts = [_const_spec((1, D_SHIFT)), _const_spec((1, D_RWKV)), _const_spec((LANES, D_RWKV)),
              _const_spec((1, D_RWKV)), _const_spec((LANES, D_RWKV)), _const_spec((1, D_RWKV)),
              _const_spec((1, D_RWKV)), _const_spec((D_RWKV, D_RWKV))]
    return [row, prev] + consts


def _rwkv_pre_fwd(rw, mu, w0, w2p, a0, a2p, k_k, k_a, ee, tt):
    L = rw.shape[0]

    def body(rw_ref, prev_ref, mu_ref, w0_ref, w2_ref, a0_ref, a2_ref, kk_ref, ka_ref, ee_ref, *outs):
        rwv = rw_ref[...]
        rws = rwv + (_shifted(rwv, prev_ref[...], pl.program_id(0) == 0) - rwv) * mu_ref[...]
        res = _rwkv_pre_fn(*_split_rw(rws), w0_ref[...], w2_ref[...], a0_ref[...], a2_ref[...],
                           kk_ref[...], ka_ref[...], ee_ref[...])
        for o, val in zip(outs, res):
            _store_heads(o, val)

    return pl.pallas_call(
        body, name="rwkv_pre_fwd", grid=(L // tt,),
        in_specs=_rwkv_pre_specs(tt), out_specs=[_head_spec(tt)] * 6, out_shape=[_sds((N_HEADS, L, HEAD))] * 6,
        compiler_params=_params("parallel"),
    )(rw, rw, mu, w0, w2p, a0, a2p, k_k, k_a, ee)


def _rwkv_pre_bwd(rw, mu, w0, w2p, a0, a2p, k_k, k_a, ee, cots, tt):
    L = rw.shape[0]
    n_t = L // tt

    def body(rw_ref, prev_ref, mu_ref, w0_ref, w2_ref, a0_ref, a2_ref, kk_ref, ka_ref, ee_ref,
             c_r, c_w, c_k, c_v, c_a, c_b, cb_r, cb_k, cb_v,
             drws_ref, dmu_o, dw0_o, dw2_o, da0_o, da2_o, dkk_o, dka_o,
             dmu, dw0, dw2, da0, da2, dkk, dka):
        i = pl.program_id(0)
        accs = (dmu, dw0, dw2, da0, da2, dkk, dka)

        @pl.when(i == 0)
        def _():
            for acc in accs:
                acc[...] = jnp.zeros_like(acc)

        rwv = rw_ref[...]
        diff = _shifted(rwv, prev_ref[...], i == 0) - rwv
        rws = rwv + diff * mu_ref[...]
        consts = (w0_ref[...], w2_ref[...], a0_ref[...], a2_ref[...], kk_ref[...], ka_ref[...])
        _, vjp = jax.vjp(lambda *a: _rwkv_pre_fn(*a, ee_ref[...]), *_split_rw(rws), *consts)
        scan = [_load_heads(c) for c in (c_r, c_w, c_k, c_v, c_a, c_b)]
        g = vjp((scan[0] + cb_r[...], scan[1], scan[2] + cb_k[...], scan[3] + cb_v[...], scan[4], scan[5]))
        drws = jnp.concatenate(g[0:4], axis=-1)
        drws_ref[...] = drws
        dmu[...] += jnp.sum(drws * diff, axis=0, keepdims=True)
        for acc, val in zip(accs[1:], g[4:]):
            acc[...] += val

        @pl.when(i == n_t - 1)
        def _():
            for acc, out in zip(accs, (dmu_o, dw0_o, dw2_o, da0_o, da2_o, dkk_o, dka_o)):
                out[...] = acc[...]

    row = pl.BlockSpec((tt, D_RWKV), lambda i: (i, 0))
    shapes = [(1, D_SHIFT), (1, D_RWKV), (LANES, D_RWKV), (1, D_RWKV), (LANES, D_RWKV), (1, D_RWKV), (1, D_RWKV)]
    return pl.pallas_call(
        body, name="rwkv_pre_bwd", grid=(n_t,),
        in_specs=_rwkv_pre_specs(tt) + [_head_spec(tt)] * 6 + [row] * 3,
        out_specs=[pl.BlockSpec((tt, D_SHIFT), lambda i: (i, 0))] + [_acc_spec(s) for s in shapes],
        out_shape=[_sds((L, D_SHIFT))] + [_sds(s) for s in shapes],
        scratch_shapes=[pltpu.VMEM(s, F32) for s in shapes],
        compiler_params=_params("arbitrary"),
    )(rw, rw, mu, w0, w2p, a0, a2p, k_k, k_a, ee, *cots)


def _bmm(a, b):
    return lax.dot_general(a, b, (((2,), (1,)), ((0,), (0,))), precision=HI, preferred_element_type=F32)


def _bmm_nt(a, b):
    return lax.dot_general(a, b, (((2,), (2,)), ((0,), (0,))), precision=HI, preferred_element_type=F32)


def _bmm_tn(a, b):
    return lax.dot_general(a, b, (((1,), (1,)), ((0,), (0,))), precision=HI, preferred_element_type=F32)


def _bdot_bf(a, b, lhs_dim, rhs_dim):
    return lax.dot_general(a.astype(BF16), b.astype(BF16), (((lhs_dim,), (rhs_dim,)), ((0,), (0,))),
                           preferred_element_type=F32)


@jax.custom_vjp
def _bmm_bf(a, b):
    return _bdot_bf(a, b, 2, 1)


def _bmm_bf_fwd(a, b):
    return _bmm_bf(a, b), (a, b)


def _bmm_bf_bwd(res, g):
    a, b = res
    return _bdot_bf(g, b, 2, 2), _bdot_bf(a, g, 1, 1)


_bmm_bf.defvjp(_bmm_bf_fwd, _bmm_bf_bwd)


@jax.custom_vjp
def _bmm_tn_bf(a, b):
    return _bdot_bf(a, b, 1, 1)


def _bmm_tn_bf_fwd(a, b):
    return _bmm_tn_bf(a, b), (a, b)


def _bmm_tn_bf_bwd(res, g):
    a, b = res
    return _bdot_bf(b, g, 2, 2), _bdot_bf(a, g, 2, 1)


_bmm_tn_bf.defvjp(_bmm_tn_bf_fwd, _bmm_tn_bf_bwd)


def _unit_lower_inverse(a):
    t = a.shape[-1]
    ti = lax.broadcasted_iota(jnp.int32, (t, t), 0)
    si = lax.broadcasted_iota(jnp.int32, (t, t), 1)

    def same_block(bits):
        shift = jnp.int32(bits)
        return (lax.shift_right_logical(ti, shift) == lax.shift_right_logical(si, shift))[None]

    def mm(x, y):
        return _bdot_bf(x, y, 2, 1)

    d = jnp.where(same_block(3), a, 0.0)
    inv = jnp.where(ti == si, 1.0, 0.0)[None] + d
    pw = mm(d, d)
    both = mm(jnp.concatenate([inv, pw], axis=1), pw)
    inv = inv + both[:, :t]
    inv = inv + mm(inv, both[:, t:])
    bits = 3
    while (1 << bits) < t:
        e = jnp.where(same_block(bits), 0.0, jnp.where(same_block(bits + 1), a, 0.0))
        inv = inv + mm(mm(inv, e), inv)
        bits += 1
    return inv


@jax.custom_vjp
def _solve_unit_lower(a, rhs, inv):
    return _bmm(inv, rhs)


def _solve_fwd(a, rhs, inv):
    u = _bmm(inv, rhs)
    return u, (inv, u)


def _solve_bwd(res, du):
    inv, u = res
    d_rhs = _bmm_tn(inv, du)
    return _bmm_nt(d_rhs, u), d_rhs, jnp.zeros_like(inv)


_solve_unit_lower.defvjp(_solve_fwd, _solve_bwd)


def _rwkv_chunk(st0, r, logw, k, v, a, b, inv=None):
    n_h, t, _ = r.shape
    ti = lax.broadcasted_iota(jnp.int32, (t, t), 0)
    si = lax.broadcasted_iota(jnp.int32, (t, t), 1)
    ones_tri = jnp.broadcast_to(jnp.where(ti >= si, 1.0, 0.0)[None], (n_h, t, t))
    log_p = _bmm(ones_tri, logw)
    p_in = jnp.exp(log_p)
    p_inv = jnp.exp(-log_p)
    at = a * jnp.exp(log_p - logw)
    rt = r * p_in
    bk = jnp.concatenate([b * p_inv, k * p_inv], axis=1)
    ri = lax.broadcasted_iota(jnp.int32, (2 * t, 2 * t), 0)
    ci = lax.broadcasted_iota(jnp.int32, (2 * t, 2 * t), 1)
    top_rows = ri < t
    diff = jnp.where(top_rows, ri, ri - t) - jnp.where(ci < t, ci, ci - t)
    mask = (diff >= jnp.where(top_rows, 1, 0))[None]
    m = jnp.where(mask, _bmm_nt(jnp.concatenate([at, rt], axis=1), bk), 0.0)
    top, bottom = m[:, :t], m[:, t:]
    a_ab = top[:, :, :t]
    if inv is None:
        inv = _unit_lower_inverse(a_ab)
    rhs = _bmm_bf(jnp.concatenate([at, top[:, :, t:]], axis=2), jnp.concatenate([st0, v], axis=1))
    u = _solve_unit_lower(a_ab, rhs, inv)
    y = _bmm_bf(jnp.concatenate([rt, bottom], axis=2), jnp.concatenate([st0, u, v], axis=1))
    p_end = jnp.swapaxes(p_in[:, t - 1:t, :], 1, 2)
    st1 = (st0 + _bmm_tn_bf(bk, jnp.concatenate([u, v], axis=1))) * p_end
    return y, st1, inv


def _rwkv_scan_fwd(ops):
    n_h, L, n = ops[0].shape
    t = RWKV_CHUNK
    per = min(RWKV_CHUNKS_PER_STEP, L // t)
    n_c = L // t
    n_s = n_c // per

    def body(r_ref, w_ref, k_ref, v_ref, a_ref, b_ref, y_ref, st_ref, inv_ref, st):
        @pl.when(pl.program_id(0) == 0)
        def _():
            st[...] = jnp.zeros_like(st)

        st0 = st[...]
        for j in range(per):
            rows = slice(j * t, (j + 1) * t)
            st_ref[j] = st0
            y, st0, inv = _rwkv_chunk(st0, *(ref[:, rows, :] for ref in (r_ref, w_ref, k_ref, v_ref, a_ref, b_ref)))
            y_ref[:, rows, :] = y
            inv_ref[j] = inv
        st[...] = st0

    blk = pl.BlockSpec((n_h, per * t, n), lambda c: (0, c, 0))
    return pl.pallas_call(
        body, name="rwkv_scan_fwd", grid=(n_s,), in_specs=[blk] * 6,
        out_specs=[blk, pl.BlockSpec((per, n_h, n, n), lambda c: (c, 0, 0, 0)),
                   pl.BlockSpec((per, n_h, t, t), lambda c: (c, 0, 0, 0))],
        out_shape=[_sds((n_h, L, n)), _sds((n_c, n_h, n, n)), _sds((n_c, n_h, t, t))],
        scratch_shapes=[pltpu.VMEM((n_h, n, n), F32)],
        compiler_params=_params("arbitrary"),
    )(*ops)


def _rwkv_scan_bwd(ops, states, invs, dy):
    n_h, L, n = ops[0].shape
    t = RWKV_CHUNK
    per = min(RWKV_CHUNKS_PER_STEP, L // t)
    n_s = L // t // per

    def body(r_ref, w_ref, k_ref, v_ref, a_ref, b_ref, st_ref, inv_ref, dy_ref, dr, dw, dk, dv, da, db, dst):
        @pl.when(pl.program_id(0) == 0)
        def _():
            dst[...] = jnp.zeros_like(dst)

        vjps = []
        for j in range(per):
            rows = slice(j * t, (j + 1) * t)
            inv = inv_ref[j]
            args = [ref[:, rows, :] for ref in (r_ref, w_ref, k_ref, v_ref, a_ref, b_ref)]
            vjps.append(jax.vjp(lambda *a, inv=inv: _rwkv_chunk(*a, inv=inv)[:2], st_ref[j], *args)[1])
        d_state = dst[...]
        for j in reversed(range(per)):
            rows = slice(j * t, (j + 1) * t)
            g = vjps[j]((dy_ref[:, rows, :], d_state))
            d_state = g[0]
            for out, val in zip((dr, dw, dk, dv, da, db), g[1:]):
                out[:, rows, :] = val
        dst[...] = d_state

    blk = pl.BlockSpec((n_h, per * t, n), lambda c: (0, n_s - 1 - c, 0))
    per_chunk = lambda m: pl.BlockSpec((per, n_h, m, m), lambda c: (n_s - 1 - c, 0, 0, 0))
    return pl.pallas_call(
        body, name="rwkv_scan_bwd", grid=(n_s,),
        in_specs=[blk] * 6 + [per_chunk(n), per_chunk(t), blk],
        out_specs=[blk] * 6, out_shape=[_sds((n_h, L, n))] * 6,
        scratch_shapes=[pltpu.VMEM((n_h, n, n), F32)],
        compiler_params=_params("arbitrary"),
    )(*ops, states, invs, dy)


def _post_fn(x, u, zs, zr, ysc, r, k2, v, y_ssm, d, glu_w, glu_b, ln_w, ln_b, r_k,
             wo_s5, wo_rwkv, gf, tgt, ee):
    y3 = _gelu(y_ssm + d * u)
    y_s5 = y3 * _sigmoid(mm_bf(y3, glu_w) + glu_b) * _silu(zs)
    mean = head_sum_split(ysc, ee) * (1.0 / HEAD)
    yc = ysc - mean
    var = head_sum(yc * yc, ee) * (1.0 / HEAD)
    gn = yc * lax.rsqrt(var + GN_EPS) * ln_w + ln_b
    bonus = head_sum(r * k2 * r_k, ee) * v
    y_rwkv = (gn + bonus) * _silu(zr)
    x2 = x + mm_bf(y_s5, wo_s5) + mm_bf(y_rwkv, wo_rwkv)
    err = _rms(x2, gf) - tgt
    return 0.5 * jnp.mean(err * err, axis=-1, keepdims=True)


def _post(x, u, zs, zr, ysc, r, k2, v, y_ssm, d, glu_w, glu_b, ln_w, ln_b, r_k, w_out, gf, tgt, ee, tt):
    L = x.shape[0]
    n_t = L // tt
    acc_shapes = [(1, D_S5), (D_S5, D_S5), (1, D_S5), (1, D_RWKV), (1, D_RWKV), (1, D_RWKV),
                  (D_MODEL, D_MODEL), (1, D_MODEL), (8, LANES)]

    def body(x_ref, u_ref, zs_ref, zr_ref, ysc_ref, r_ref, k2_ref, v_ref, yssm_ref,
             d_ref, gw_ref, gb_ref, lw_ref, lb_ref, rk_ref, wo_ref, gf_ref, tgt_ref, ee_ref,
             dx_o, du_o, dzs_o, dzr_o, dysc_o, dr_o, dk2_o, dv_o, dyssm_o,
             dd_o, dgw_o, dgb_o, dlw_o, dlb_o, drk_o, dwo_o, dgf_o, loss_o,
             dd, dgw, dgb, dlw, dlb, drk, dwo, dgf, loss):
        i = pl.program_id(0)
        accs = (dd, dgw, dgb, dlw, dlb, drk, dwo, dgf, loss)

        @pl.when(i == 0)
        def _():
            for acc in accs:
                acc[...] = jnp.zeros_like(acc)

        args = (x_ref[...], u_ref[...], zs_ref[...], zr_ref[...],
                _load_heads(ysc_ref), _load_heads(r_ref), _load_heads(k2_ref), _load_heads(v_ref), yssm_ref[...],
                d_ref[...], gw_ref[...], gb_ref[...], lw_ref[...], lb_ref[...], rk_ref[...],
                wo_ref[0:D_S5, :], wo_ref[D_S5:D_MODEL, :], gf_ref[...])
        rows, vjp = jax.vjp(lambda *a: _post_fn(*a, tgt_ref[...], ee_ref[...]), *args)
        g = vjp(jnp.ones_like(rows))
        for out, val in zip((dx_o, du_o, dzs_o, dzr_o), g[0:4]):
            out[...] = val
        _store_heads(dysc_o, g[4])
        for out, val in zip((dr_o, dk2_o, dv_o, dyssm_o), g[5:9]):
            out[...] = val
        for acc, val in zip((dd, dgw, dgb, dlw, dlb, drk), g[9:15]):
            acc[...] += val
        dwo[0:D_S5, :] += g[15]
        dwo[D_S5:D_MODEL, :] += g[16]
        dgf[...] += g[17]
        loss[...] += jnp.broadcast_to(jnp.sum(rows, axis=0, keepdims=True), loss.shape)

        @pl.when(i == n_t - 1)
        def _():
            for acc, out in zip(accs, (dd_o, dgw_o, dgb_o, dlw_o, dlb_o, drk_o, dwo_o, dgf_o, loss_o)):
                pltpu.sync_copy(acc, out)

    row = lambda n: pl.BlockSpec((tt, n), lambda i: (i, 0))
    in_specs = ([row(D_MODEL)] + [row(512)] * 3 + [_head_spec(tt)] * 4 + [row(D_S5)]
                + [_const_spec(s) for s in [(1, D_S5), (D_S5, D_S5), (1, D_S5), (1, D_RWKV), (1, D_RWKV), (1, D_RWKV),
                                            (D_MODEL, D_MODEL), (1, D_MODEL)]]
                + [row(D_MODEL), _const_spec((D_RWKV, D_RWKV))])
    out_rows = [D_MODEL] + [512] * 3 + [None] + [512] * 4
    return pl.pallas_call(
        body, name="post_fwd_bwd", grid=(n_t,), in_specs=in_specs,
        out_specs=[row(n) if n else _head_spec(tt) for n in out_rows] + [_ANY] * len(acc_shapes),
        out_shape=([_sds((L, n)) if n else _sds((N_HEADS, L, HEAD)) for n in out_rows]
                   + [_sds(s) for s in acc_shapes]),
        scratch_shapes=[pltpu.VMEM(s, F32) for s in acc_shapes],
        compiler_params=_params("arbitrary"),
    )(x, u, zs, zr, ysc, r, k2, v, y_ssm, d, glu_w, glu_b, ln_w, ln_b, r_k, w_out, gf, tgt, ee)


def _s5_bwd(u, du_direct, dy, s_re, s_im, b4_re, b4_im, c4_re, c4_im, pow_r, pow_i, tt):
    L = u.shape[0]
    n_t = L // tt
    acc_shapes = ([(S5_BLOCKS, LANES, 512)] * 2 + [(S5_BLOCKS, 512, LANES)] * 2 + [(1, N_STATE)] * 2)

    def body(u_ref, dud_ref, dy_ref, sre_ref, sim_ref, pre_ref, pim_ref, bre_ref, bim_ref, cre_ref, cim_ref,
             pr_ref, pi_ref, du_o, dbre_o, dbim_o, dcre_o, dcim_o, dlr_o, dli_o,
             dbre, dbim, dcre, dcim, dlr, dli, gre, gim, car_r, car_i):
        i = pl.program_id(0)

        @pl.when(i == 0)
        def _():
            for acc in (dbre, dbim, dcre, dcim, dlr, dli, car_r, car_i):
                acc[...] = jnp.zeros_like(acc)

        uv = u_ref[...]
        dyv = dy_ref[...]
        blocks = [slice(q * 512, (q + 1) * 512) for q in range(S5_BLOCKS)]
        lanes = [slice(q * LANES, (q + 1) * LANES) for q in range(S5_BLOCKS)]
        for q in range(S5_BLOCKS):
            gre[:, blocks[q]] = _dot_bf(dyv[:, lanes[q]], cre_ref[q], ((1,), (1,)))
            gim[:, blocks[q]] = -_dot_bf(dyv[:, lanes[q]], cim_ref[q], ((1,), (1,)))
        _tile_scan(gre, gim, pr_ref, pi_ref, car_r, car_i, reverse=True)
        for q in range(S5_BLOCKS):
            gr = gre[:, blocks[q]]
            gi = gim[:, blocks[q]]
            sr = sre_ref[:, blocks[q]]
            si = sim_ref[:, blocks[q]]
            du_o[:, lanes[q]] = (dud_ref[:, lanes[q]] + _dot_bf(gr, bre_ref[q], ((1,), (1,)))
                                 + _dot_bf(gi, bim_ref[q], ((1,), (1,))))
            dbre[q] += _dot_bf(uv[:, lanes[q]], gr, ((0,), (0,)))
            dbim[q] += _dot_bf(uv[:, lanes[q]], gi, ((0,), (0,)))
            dcre[q] += _dot_bf(sr, dyv[:, lanes[q]], ((0,), (0,)))
            dcim[q] -= _dot_bf(si, dyv[:, lanes[q]], ((0,), (0,)))
            rid = lax.broadcasted_iota(jnp.int32, sr.shape, 0)
            first = i == n_t - 1
            prev_r = jnp.where(first, 0.0, pre_ref[7:8, blocks[q]])
            prev_i = jnp.where(first, 0.0, pim_ref[7:8, blocks[q]])
            pr = jnp.where(rid == 0, jnp.broadcast_to(prev_r, sr.shape), pltpu.roll(sr, 1, axis=0))
            pi_ = jnp.where(rid == 0, jnp.broadcast_to(prev_i, si.shape), pltpu.roll(si, 1, axis=0))
            dlr[:, blocks[q]] += jnp.sum(pr * gr + pi_ * gi, axis=0, keepdims=True)
            dli[:, blocks[q]] += jnp.sum(pr * gi - pi_ * gr, axis=0, keepdims=True)

        @pl.when(i == n_t - 1)
        def _():
            for acc, out in zip((dbre, dbim, dcre, dcim, dlr, dli), (dbre_o, dbim_o, dcre_o, dcim_o, dlr_o, dli_o)):
                out[...] = acc[...]

    row = lambda n: pl.BlockSpec((tt, n), lambda i: (n_t - 1 - i, 0))
    prev = pl.BlockSpec((8, N_STATE), lambda i: (jnp.maximum((n_t - 1 - i) * (tt // 8) - 1, 0), 0))
    return pl.pallas_call(
        body, name="s5_bwd", grid=(n_t,),
        in_specs=[row(D_S5)] * 3 + [row(N_STATE)] * 2 + [prev] * 2
        + [_const_spec((S5_BLOCKS, LANES, 512))] * 2 + [_const_spec((S5_BLOCKS, 512, LANES))] * 2
        + [_const_spec((8, N_STATE))] * 2,
        out_specs=[row(D_S5)] + [_acc_spec(s) for s in acc_shapes],
        out_shape=[_sds((L, D_S5))] + [_sds(s) for s in acc_shapes],
        scratch_shapes=[pltpu.VMEM(s, F32) for s in acc_shapes] + [pltpu.VMEM((tt, N_STATE), F32)] * 2
        + [pltpu.VMEM((8, N_STATE), F32)] * 2,
        compiler_params=_params("arbitrary"),
    )(u, du_direct, dy, s_re, s_im, s_re, s_im, b4_re, b4_im, c4_re, c4_im, pow_r, pow_i)


def _bwd_in(x, norm_g, w_in_bf, mu, dx2, du, dzs, drws, dzr, tt):
    L = x.shape[0]
    n_t = L // tt

    def body(x_ref, g_ref, w_ref, mu_ref, dx2_ref, du_ref, dzs_ref, drws_ref, nxt_ref, dzr_ref,
             gx_o, dw_o, dg_o, dproj, dw, dg):
        i = pl.program_id(0)

        @pl.when(i == 0)
        def _():
            dw[...] = jnp.zeros_like(dw)
            dg[...] = jnp.zeros_like(dg)

        drws_v = drws_ref[...]
        rid = lax.broadcasted_iota(jnp.int32, drws_v.shape, 0)
        nxt_row = jnp.where(i == n_t - 1, 0.0, nxt_ref[0:1, :])
        nxt = jnp.where(rid == tt - 1, jnp.broadcast_to(nxt_row, drws_v.shape), pltpu.roll(drws_v, tt - 1, axis=0))
        muv = mu_ref[...]
        drw = drws_v * (1.0 - muv) + nxt * muv
        dproj[:, 0:D_S5] = du_ref[...].astype(BF16)
        dproj[:, D_S5:2 * D_S5] = dzs_ref[...].astype(BF16)
        dproj[:, 2 * D_S5:2 * D_S5 + D_SHIFT] = drw.astype(BF16)
        dproj[:, 2 * D_S5 + D_SHIFT:D_IN] = dzr_ref[...].astype(BF16)
        dh = _dot(dproj[...], w_ref[...], ((1,), (1,)), None)
        h, vjp = jax.vjp(_rms, x_ref[...], g_ref[...])
        dxh, dgv = vjp(dh)
        gx_o[...] = dx2_ref[...] + dxh
        dg[...] += dgv
        dw[...] += _dot(h.astype(BF16), dproj[...], ((0,), (0,)), None)

        @pl.when(i == n_t - 1)
        def _():
            dg_o[...] = dg[...]
            pltpu.sync_copy(dw, dw_o)

    row = lambda n: pl.BlockSpec((tt, n), lambda i: (i, 0))
    nxt = pl.BlockSpec((8, D_SHIFT), lambda i: (jnp.minimum((i + 1) * (tt // 8), L // 8 - 1), 0))
    return pl.pallas_call(
        body, name="bwd_in", grid=(n_t,),
        in_specs=[row(D_MODEL), _const_spec((1, D_MODEL)), _const_spec((D_MODEL, D_IN)), _const_spec((1, D_SHIFT)),
                  row(D_MODEL), row(D_S5), row(D_S5), row(D_SHIFT), nxt, row(D_RWKV)],
        out_specs=[row(D_MODEL), _ANY, _acc_spec((1, D_MODEL))],
        out_shape=[_sds((L, D_MODEL)), _sds((D_MODEL, D_IN)), _sds((1, D_MODEL))],
        scratch_shapes=[pltpu.VMEM((tt, D_IN), BF16), pltpu.VMEM((D_MODEL, D_IN), F32), pltpu.VMEM((1, D_MODEL), F32)],
        compiler_params=_params("arbitrary"),
    )(x, norm_g, w_in_bf, mu, dx2, du, dzs, drws, drws, dzr)


def _block_diag_b(bbar):
    bb = bbar.reshape(S5_GROUP, S5_BLOCKS, 8, S5_STATE)
    return jnp.einsum('hqgp,Gg->qGhgp', bb, jnp.eye(8, dtype=F32)).reshape(S5_BLOCKS, LANES, 512)


def _block_diag_b_t(db4):
    d = db4.reshape(S5_BLOCKS, 8, S5_GROUP, 8, S5_STATE)
    return jnp.einsum('qGhgp,Gg->hqgp', d, jnp.eye(8, dtype=F32)).reshape(S5_GROUP, N_STATE)


def _block_diag_c(c):
    cc = c.reshape(S5_BLOCKS, 8, S5_GROUP, S5_STATE)
    return jnp.einsum('qghp,gG->qgpGh', cc, jnp.eye(8, dtype=F32)).reshape(S5_BLOCKS, 512, LANES)


def _block_diag_c_t(dc4):
    d = dc4.reshape(S5_BLOCKS, 8, S5_STATE, 8, S5_GROUP)
    return jnp.einsum('qgpGh,gG->qghp', d, jnp.eye(8, dtype=F32)).reshape(S5_GROUPS, S5_GROUP, S5_STATE)


def _local_step(x, tgt, w):
    L = x.shape[0]
    tt = min(512, L)
    tp = min(256, L)
    ee = _head_sum_matrix()

    lam_re = w['s5_lam_re'].reshape(1, N_STATE)
    lam_im = w['s5_lam_im'].reshape(1, N_STATE)
    logdt = jnp.repeat(w['s5_log_dt'], S5_STATE).reshape(1, N_STATE)
    b_re_t = w['s5_b_re'].transpose(2, 0, 1).reshape(S5_GROUP, N_STATE)
    b_im_t = w['s5_b_im'].transpose(2, 0, 1).reshape(S5_GROUP, N_STATE)
    bbr, bbi, pow_r, pow_i, rpow_r, rpow_i = _s5_param_fwd(lam_re, lam_im, logdt, b_re_t, b_im_t)
    b4_re, b4_im = _block_diag_b(bbr), _block_diag_b(bbi)
    c4_re, c4_im = _block_diag_c(w['s5_c_re']), _block_diag_c(w['s5_c_im'])

    norm_g = w['norm_g'].reshape(1, D_MODEL)
    w_in_bf = w['w_in'].astype(BF16)
    u, zs, rw, zr = _fwd_in(x, norm_g, w_in_bf, tt)
    s_re, s_im, y_ssm = _s5_fwd(u, b4_re, b4_im, c4_re, c4_im, pow_r, pow_i, tt)

    row = lambda t: t.reshape(1, -1)
    zpad = jnp.zeros((HEAD, D_RWKV), F32)
    w2p = jnp.concatenate([w['rwkv_w2'], zpad], axis=0)
    a2p = jnp.concatenate([zpad, w['rwkv_a2']], axis=0)
    pre_consts = (row(w['rwkv_mu']), row(w['rwkv_w0']), w2p, row(w['rwkv_a0']), a2p,
                  row(w['rwkv_k_k']), row(w['rwkv_k_a']), ee)
    ops = _rwkv_pre_fwd(rw, *pre_consts, tt)
    ysc, states, invs = _rwkv_scan_fwd(ops)

    post = _post(x, u, zs, zr, ysc, ops[0], ops[2], ops[3], y_ssm,
                 row(w['s5_d']), w['s5_glu_w'], row(w['s5_glu_b']), row(w['rwkv_ln_w']), row(w['rwkv_ln_b']),
                 row(w['rwkv_r_k']), w['w_out'], row(w['final_g']), tgt, ee, tp)
    (dx2, du_d, dzs, dzr, dysc, dr_b, dk2_b, dv_b, dy_ssm,
     dd, dglu_w, dglu_b, dln_w, dln_b, dr_k, dw_out, dgf, loss) = post

    du, db4_re, db4_im, dc4_re, dc4_im, dlbr, dlbi = _s5_bwd(
        u, du_d, dy_ssm, s_re, s_im, b4_re, b4_im, c4_re, c4_im, rpow_r, rpow_i, tt)
    group_ind = (jnp.arange(N_STATE)[:, None] // S5_STATE == jnp.arange(LANES)[None, :]).astype(F32)
    dlam_re, dlam_im, dlogdt, db_re_t, db_im_t = _s5_param_bwd(
        lam_re, lam_im, logdt, b_re_t, b_im_t, dlbr, dlbi, _block_diag_b_t(db4_re), _block_diag_b_t(db4_im), group_ind)

    cots = list(_rwkv_scan_bwd(ops, states, invs, dysc)) + [dr_b, dk2_b, dv_b]
    drws, dmu, dw0, dw2p, da0, da2p, dk_k, dk_a = _rwkv_pre_bwd(rw, *pre_consts, cots, tt)

    grad_x, dw_in, dnorm_g = _bwd_in(x, norm_g, w_in_bf, row(w['rwkv_mu']), dx2, du, dzs, drws, dzr, tt)

    unb = lambda t: t.reshape(S5_GROUP, S5_GROUPS, S5_STATE).transpose(1, 2, 0)
    grads = {
        'norm_g': dnorm_g.reshape(D_MODEL), 'w_in': dw_in,
        's5_lam_re': dlam_re.reshape(S5_GROUPS, S5_STATE), 's5_lam_im': dlam_im.reshape(S5_GROUPS, S5_STATE),
        's5_log_dt': dlogdt[0, :S5_GROUPS], 's5_b_re': unb(db_re_t), 's5_b_im': unb(db_im_t),
        's5_c_re': _block_diag_c_t(dc4_re), 's5_c_im': _block_diag_c_t(dc4_im),
        's5_d': dd.reshape(D_S5), 's5_glu_w': dglu_w, 's5_glu_b': dglu_b.reshape(D_S5),
        'rwkv_mu': dmu.reshape(-1), 'rwkv_w0': dw0.reshape(-1), 'rwkv_w2': dw2p[:HEAD], 'rwkv_a0': da0.reshape(-1),
        'rwkv_a2': da2p[HEAD:], 'rwkv_k_k': dk_k.reshape(-1), 'rwkv_k_a': dk_a.reshape(-1),
        'rwkv_r_k': dr_k.reshape(N_HEADS, HEAD), 'rwkv_ln_w': dln_w.reshape(-1), 'rwkv_ln_b': dln_b.reshape(-1),
        'w_out': dw_out, 'final_g': dgf.reshape(D_MODEL),
    }
    return loss, grad_x, grads


def _exchange(arrays, gather, axes, name):
    n = len(arrays)
    group = 2 ** len(axes)

    def body(*refs):
        send_refs, recv_refs = refs[:n], refs[n:2 * n]
        send_sems, recv_sems, local_sems = refs[2 * n:]
        pos = {ax: lax.axis_index(ax) for ax in ("x", "y", "c")}

        def index_of(p):
            idx = 0
            for ax in axes:
                idx = 2 * idx + p[ax]
            return idx

        me = index_of(pos)
        own, outs, arrivals = [], [], []
        for i, (send_ref, recv_ref) in enumerate(zip(send_refs, recv_refs)):
            def block_for(dev, send_ref=send_ref, whole=gather[i]):
                return send_ref if whole else send_ref.at[dev]

            own.append(pltpu.make_async_copy(block_for(me), recv_ref.at[me], local_sems.at[i]))
            own[-1].start()
            for k in range(1, group):
                peer = dict(pos)
                for bit, ax in enumerate(axes):
                    if (k >> bit) & 1:
                        peer[ax] = 1 - pos[ax]
                peer_idx = index_of(peer)
                sems = dict(send_sem=send_sems.at[i, k - 1], recv_sem=recv_sems.at[i, k - 1],
                            device_id=(peer["x"], peer["y"], peer["c"]), device_id_type=pl.DeviceIdType.MESH)
                outs.append(pltpu.make_async_remote_copy(src_ref=block_for(peer_idx), dst_ref=recv_ref.at[me], **sems))
                outs[-1].start()
                arrivals.append(
                    pltpu.make_async_remote_copy(src_ref=block_for(peer_idx), dst_ref=recv_ref.at[peer_idx], **sems))
        for copy in arrivals:
            copy.wait_recv()
        for copy in outs:
            copy.wait_send()
        for copy in own:
            copy.wait()

    return pl.pallas_call(
        body, name=name, in_specs=[_ANY] * n, out_specs=[_ANY] * n,
        out_shape=[jax.ShapeDtypeStruct(((group,) + a.shape) if whole else a.shape, a.dtype)
                   for a, whole in zip(arrays, gather)],
        scratch_shapes=[pltpu.SemaphoreType.DMA((n, group - 1)), pltpu.SemaphoreType.DMA((n, group - 1)),
                        pltpu.SemaphoreType.DMA((n,))],
        compiler_params=pltpu.CompilerParams(has_side_effects=True),
    )(*arrays)


def _sum_devices(ref):
    g = ref[0].astype(F32)
    for s in range(1, ref.shape[0]):
        g = g + ref[s].astype(F32)
    return g


def _adamw_math(g, w, m, v):
    m_new = ADAM_B1 * m + (1.0 - ADAM_B1) * g
    v_new = ADAM_B2 * v + (1.0 - ADAM_B2) * (g * g)
    m_hat = m_new / (1.0 - ADAM_B1 ** ADAM_STEP)
    v_hat = v_new / (1.0 - ADAM_B2 ** ADAM_STEP)
    return -ADAM_LR * (m_hat / (jnp.sqrt(v_hat) + ADAM_EPS) + ADAM_WD * w), m_new, v_new


def _adamw(gs, ws, ms, vs, reduce, name):
    n = len(ws)

    def body(*refs):
        g_refs, w_refs, m_refs, v_refs = (refs[j * n:(j + 1) * n] for j in range(4))
        outs = refs[4 * n:]
        for i in range(n):
            g = _sum_devices(g_refs[i]) if reduce else g_refs[i][...]
            res = _adamw_math(g, w_refs[i][...], m_refs[i][...], v_refs[i][...])
            for j, val in enumerate(((g,) if reduce else ()) + res):
                outs[j * n + i][...] = val

    return pl.pallas_call(
        body, name=name, out_shape=[_sds(w.shape) for w in ws] * (4 if reduce else 3),
        compiler_params=pltpu.CompilerParams(vmem_limit_bytes=VMEM_LIMIT),
    )(*gs, *ws, *ms, *vs)


def _sum_blocks(recv):
    def body(recv_ref, out_ref):
        out_ref[...] = _sum_devices(recv_ref)

    return pl.pallas_call(body, name="sum_small_grads", out_shape=_sds(recv.shape[1:]))(recv)


_WEIGHTS = [
    ('norm_g', (1, 1024), False), ('w_in', (1, 1024, 400), True), ('s5_lam_re', (1, 32, 64), False),
    ('s5_lam_im', (1, 32, 64), False), ('s5_log_dt', (1, 32), False), ('s5_b_re', (1, 32, 64, 16), False),
    ('s5_b_im', (1, 32, 64, 16), False), ('s5_c_re', (1, 32, 16, 64), False), ('s5_c_im', (1, 32, 16, 64), False),
    ('s5_d', (1, 512), False), ('s5_glu_w', (1, 64, 512), True), ('s5_glu_b', (1, 512), False),
    ('rwkv_mu', (1, 1664), False), ('rwkv_w0', (1, 512), False), ('rwkv_w2', (1, 64, 64), True),
    ('rwkv_a0', (1, 512), False), ('rwkv_a2', (1, 64, 64), True), ('rwkv_k_k', (1, 512), False),
    ('rwkv_k_a', (1, 512), False), ('rwkv_r_k', (1, 8, 64), False), ('rwkv_ln_w', (1, 512), False),
    ('rwkv_ln_b', (1, 512), False), ('w_out', (1, 128, 1024), True), ('final_g', (1024,), False),
]
_SHARDED = [(n, s) for n, s, sharded in _WEIGHTS if sharded]
_SMALL = [(n, s) for n, s, sharded in _WEIGHTS if not sharded]
_COLUMN_SHARDED = ('w_in', 'rwkv_w2', 'rwkv_a2')
_SMALL_ROWS = -(-sum(math.prod(s) for _, s in _SMALL) // (8 * LANES)) * 8


def _pack_small(grads):
    flat = [grads[n].reshape(-1) for n, _ in _SMALL]
    pad = _SMALL_ROWS * LANES - sum(f.size for f in flat)
    return jnp.concatenate(flat + [jnp.zeros((pad,), F32)]).reshape(_SMALL_ROWS, LANES)


def _unpack_small(packed):
    flat = packed.reshape(-1)
    out, off = {}, 0
    for n, s in _SMALL:
        size = math.prod(s)
        out[n] = flat[off:off + size].reshape(s)
        off += size
    return out


_BF16_OPERANDS = ('w_in', 's5_glu_w', 'w_out')


def _join_shards(name, blocks):
    _, rows, cols = blocks.shape
    if name in _COLUMN_SHARDED:
        return blocks.transpose(1, 0, 2).reshape(rows, N_DEV * cols)
    return blocks.reshape(N_DEV * rows, cols)


def _split_shards(name, full, shard_shape):
    rows, cols = shard_shape
    if name in _COLUMN_SHARDED:
        return full.reshape(rows, N_DEV, cols).transpose(1, 0, 2)
    return full.reshape(N_DEV, rows, cols)


def kernel(x, norm_g, w_in, s5_lam_re, s5_lam_im, s5_log_dt, s5_b_re, s5_b_im, s5_c_re, s5_c_im, s5_d, s5_glu_w, s5_glu_b, rwkv_mu, rwkv_w0, rwkv_w2, rwkv_a0, rwkv_a2, rwkv_k_k, rwkv_k_a, rwkv_r_k, rwkv_ln_w, rwkv_ln_b, w_out, final_g, loss_target, m_norm_g, m_w_in, m_s5_lam_re, m_s5_lam_im, m_s5_log_dt, m_s5_b_re, m_s5_b_im, m_s5_c_re, m_s5_c_im, m_s5_d, m_s5_glu_w, m_s5_glu_b, m_rwkv_mu, m_rwkv_w0, m_rwkv_w2, m_rwkv_a0, m_rwkv_a2, m_rwkv_k_k, m_rwkv_k_a, m_rwkv_r_k, m_rwkv_ln_w, m_rwkv_ln_b, m_w_out, m_final_g, v_norm_g, v_w_in, v_s5_lam_re, v_s5_lam_im, v_s5_log_dt, v_s5_b_re, v_s5_b_im, v_s5_c_re, v_s5_c_im, v_s5_d, v_s5_glu_w, v_s5_glu_b, v_rwkv_mu, v_rwkv_w0, v_rwkv_w2, v_rwkv_a0, v_rwkv_a2, v_rwkv_k_k, v_rwkv_k_a, v_rwkv_r_k, v_rwkv_ln_w, v_rwkv_ln_b, v_w_out, v_final_g):
    given = dict(locals())

    n_sh = len(_SHARDED)
    everyone = ("x", "y", "c")
    shards = [given[n][0].astype(BF16 if n in _BF16_OPERANDS else F32) for n, _ in _SHARDED]
    gathered = _exchange(shards, (True,) * n_sh, everyone, "gather_weights")
    local = {n: _join_shards(n, blocks).astype(F32 if n != 'w_in' else BF16)
             for (n, _), blocks in zip(_SHARDED, gathered)}
    local.update({n: (given[n][0] if len(s) > 1 else given[n]) for n, s in _SMALL})

    loss, grad_x, grads = _local_step(x[0], loss_target[0], local)

    blocks = [_split_shards(n, grads[n], s[1:]).astype(BF16) for n, s in _SHARDED]
    small = _pack_small(grads).reshape(N_DEV, _SMALL_ROWS // N_DEV, LANES)
    recv = _exchange(blocks + [small], (False,) * (n_sh + 1), everyone, "exchange_grads")
    small_sum = _exchange([_sum_blocks(recv[-1])], (True,), everyone, "gather_small_grads")[0]

    result = {}
    for group, name in (([0], "adamw_w_in"), ([1, 2, 3, 4], "adamw_shards")):
        ns = [_SHARDED[i][0] for i in group]
        res = _adamw([recv[i] for i in group], [given[n][0] for n in ns], [given['m_' + n][0] for n in ns],
                     [given['v_' + n][0] for n in ns], True, name)
        for j, n in enumerate(ns):
            result[n] = [res[k * len(ns) + j][None] for k in range(4)]
    g_small = _unpack_small(small_sum)
    two_d = lambda t: t.reshape(1, -1) if t.ndim == 1 else t
    ns = [n for n, _ in _SMALL]
    res = _adamw([two_d(g_small[n]) for n in ns], [two_d(given[n]) for n in ns], [two_d(given['m_' + n]) for n in ns],
                 [two_d(given['v_' + n]) for n in ns], False, "adamw_small")
    for j, (n, s) in enumerate(_SMALL):
        result[n] = [g_small[n]] + [res[k * len(ns) + j].reshape(s) for k in range(3)]

    total = lax.psum(loss[0, 0], ("x", "y", "c"))
    outs = [total, grad_x[None]]
    for k in range(4):
        outs += [result[n][k] for n, _, _ in _WEIGHTS]
    return tuple(outs)
```

```python
import functools
import math

import jax
import jax.numpy as jnp
from jax import lax
from jax.experimental import pallas as pl
from jax.experimental.pallas import tpu as pltpu

F32 = jnp.float32
BF16 = jnp.bfloat16
HI = lax.Precision.HIGH

D_MODEL = 1024
D_S5 = 512
D_RWKV = 512
S5_GROUPS = 32
S5_GROUP = 16
S5_STATE = 64
N_STATE = S5_GROUPS * S5_STATE
N_HEADS = 8
HEAD = 64
D_SHIFT = 3 * D_RWKV + 128
D_IN = 2 * D_S5 + D_SHIFT + D_RWKV
NORM_EPS = 1e-6
GN_EPS = 64e-5
N_DEV = 8
LANES = 128
S5_BLOCKS = 4
RWKV_CHUNK = 64
RWKV_CHUNKS_PER_STEP = 4
VMEM_LIMIT = 56 * 1024 * 1024

ADAM_LR = 0.001
ADAM_B1 = 0.9
ADAM_B2 = 0.999
ADAM_EPS = 1e-08
ADAM_WD = 0.01
ADAM_STEP = 10


def _dot(a, b, dims, prec):
    return lax.dot_general(a, b, (dims, ((), ())), precision=prec, preferred_element_type=F32)


def _dot_bf(a, b, dims):
    return _dot(a.astype(BF16), b.astype(BF16), dims, None)


def _make_mm(cast, prec):
    @jax.custom_vjp
    def mm(a, b):
        return _dot(cast(a), cast(b), ((1,), (0,)), prec)

    def fwd(a, b):
        return mm(a, b), (a, b)

    def bwd(res, g):
        a, b = res
        return (_dot(cast(g), cast(b), ((1,), (1,)), prec), _dot(cast(a), cast(g), ((0,), (0,)), prec))

    mm.defvjp(fwd, bwd)
    return mm


mm_bf = _make_mm(lambda t: t.astype(BF16), None)


def _make_head_sum(split):
    def product(x, ee):
        hi = x.astype(BF16)
        out = _dot(hi, ee, ((1,), (0,)), None)
        if split:
            out = out + _dot((x - hi.astype(F32)).astype(BF16), ee, ((1,), (0,)), None)
        return out

    @jax.custom_vjp
    def head_sum(x, ee):
        return product(x, ee)

    def fwd(x, ee):
        return product(x, ee), ee

    def bwd(ee, g):
        return product(g, ee), jnp.zeros_like(ee)

    head_sum.defvjp(fwd, bwd)
    return head_sum


head_sum = _make_head_sum(False)
head_sum_split = _make_head_sum(True)


def _sigmoid(x):
    return 1.0 / (1.0 + jnp.exp(-x))


def _silu(x):
    return x * _sigmoid(x)


def _softplus(x):
    return jnp.maximum(x, 0.0) + jnp.log(1.0 + jnp.exp(-jnp.abs(x)))


def _gelu(x):
    return 0.5 * x * (1.0 + jnp.tanh(math.sqrt(2.0 / math.pi) * (x + 0.044715 * x * x * x)))


def _rms(x, g):
    return x * lax.rsqrt(jnp.mean(x * x, axis=-1, keepdims=True) + NORM_EPS) * g


def _const_spec(shape):
    nd = len(shape)
    return pl.BlockSpec(shape, lambda *_: (0,) * nd, pipeline_mode=pl.Buffered(1))


def _acc_spec(shape):
    nd = len(shape)
    return pl.BlockSpec(shape, lambda *_: (0,) * nd)


def _params(sem):
    return pltpu.CompilerParams(dimension_semantics=(sem,), vmem_limit_bytes=VMEM_LIMIT)


_ANY = pl.BlockSpec(memory_space=pl.ANY)


def _sds(shape):
    return jax.ShapeDtypeStruct(shape, F32)


def _head_sum_matrix():
    i = jnp.arange(D_RWKV) // HEAD
    return (i[:, None] == i[None, :]).astype(BF16)


def _s5_param_fn(lam_re, lam_im, logdt, b_re, b_im):
    dt = jnp.exp(logdt)
    mag = jnp.exp(lam_re * dt)
    ang = lam_im * dt
    lbr = mag * jnp.cos(ang)
    lbi = mag * jnp.sin(ang)
    nr = lbr - 1.0
    den = lam_re * lam_re + lam_im * lam_im
    cr = (nr * lam_re + lbi * lam_im) / den
    ci = (lbi * lam_re - nr * lam_im) / den
    return lbr, lbi, cr * b_re - ci * b_im, cr * b_im + ci * b_re


def _cmul(ar, ai, br, bi):
    return ar * br - ai * bi, ar * bi + ai * br


def _s5_param_fwd(lam_re, lam_im, logdt, b_re, b_im):
    def body(lr, li, ld, br, bi, o_br, o_bi, o_pr, o_pi, o_qr, o_qi):
        lbr, lbi, bbr, bbi = _s5_param_fn(lr[...], li[...], ld[...], br[...], bi[...])
        o_br[...] = bbr
        o_bi[...] = bbi
        rid = lax.broadcasted_iota(jnp.int32, (8, N_STATE), 0)
        pr, pi_ = lbr, lbi
        fwd_r = rev_r = jnp.broadcast_to(pr, (8, N_STATE))
        fwd_i = rev_i = jnp.broadcast_to(pi_, (8, N_STATE))
        for j in range(1, 8):
            pr, pi_ = _cmul(pr, pi_, lbr, lbi)
            fwd_r = jnp.where(rid == j, jnp.broadcast_to(pr, (8, N_STATE)), fwd_r)
            fwd_i = jnp.where(rid == j, jnp.broadcast_to(pi_, (8, N_STATE)), fwd_i)
            rev_r = jnp.where(rid == 7 - j, jnp.broadcast_to(pr, (8, N_STATE)), rev_r)
            rev_i = jnp.where(rid == 7 - j, jnp.broadcast_to(pi_, (8, N_STATE)), rev_i)
        o_pr[...] = fwd_r
        o_pi[...] = fwd_i
        o_qr[...] = rev_r
        o_qi[...] = -rev_i

    return pl.pallas_call(
        body, name="s5_param_fwd",
        out_shape=[_sds((S5_GROUP, N_STATE))] * 2 + [_sds((8, N_STATE))] * 4,
    )(lam_re, lam_im, logdt, b_re, b_im)


def _s5_param_bwd(lam_re, lam_im, logdt, b_re, b_im, d_lbr, d_lbi, d_bbr, d_bbi, group_ind):
    def body(lr, li, ld, br, bi, g0, g1, g2, g3, ind, o_lr, o_li, o_ld, o_br, o_bi):
        _, vjp = jax.vjp(_s5_param_fn, lr[...], li[...], ld[...], br[...], bi[...])
        d_lr, d_li, d_ld, d_br, d_bi = vjp((g0[...], g1[...], g2[...], g3[...]))
        o_lr[...] = d_lr
        o_li[...] = d_li
        o_ld[...] = _dot(jnp.broadcast_to(d_ld, (8, N_STATE)), ind[...], ((1,), (0,)), HI)
        o_br[...] = d_br
        o_bi[...] = d_bi

    return pl.pallas_call(
        body, name="s5_param_bwd",
        out_shape=[_sds((1, N_STATE))] * 2 + [_sds((8, LANES))] + [_sds((S5_GROUP, N_STATE))] * 2,
    )(lam_re, lam_im, logdt, b_re, b_im, d_lbr, d_lbi, d_bbr, d_bbi, group_ind)


def _fwd_in(x, norm_g, w_in_bf, tt):
    L = x.shape[0]

    def body(x_ref, g_ref, w_ref, u_ref, zs_ref, rw_ref, zr_ref):
        h = _rms(x_ref[...], g_ref[...])
        proj = jnp.dot(h.astype(BF16), w_ref[...], preferred_element_type=F32)
        u_ref[...] = proj[:, 0:D_S5]
        zs_ref[...] = proj[:, D_S5:2 * D_S5]
        rw_ref[...] = proj[:, 2 * D_S5:2 * D_S5 + D_SHIFT]
        zr_ref[...] = proj[:, 2 * D_S5 + D_SHIFT:D_IN]

    row = lambda n: pl.BlockSpec((tt, n), lambda i: (i, 0))
    return pl.pallas_call(
        body, name="fwd_in", grid=(L // tt,),
        in_specs=[row(D_MODEL), _const_spec((1, D_MODEL)), _const_spec((D_MODEL, D_IN))],
        out_specs=[row(D_S5), row(D_S5), row(D_SHIFT), row(D_RWKV)],
        out_shape=[_sds((L, D_S5)), _sds((L, D_S5)), _sds((L, D_SHIFT)), _sds((L, D_RWKV))],
        compiler_params=_params("parallel"),
    )(x, norm_g, w_in_bf)


S5_LANE_CHUNK = 512


def _tile_scan(re_ref, im_ref, pow_r_ref, pow_i_ref, carry_r_ref, carry_i_ref, reverse):
    t, n = re_ref.shape
    n_groups = t // 8
    ch = S5_LANE_CHUNK
    rid = lax.broadcasted_iota(jnp.int32, (8, ch), 0)
    for c in range(n // ch):
        cols = slice(c * ch, (c + 1) * ch)
        pow_r = pow_r_ref[:, cols]
        pow_i = pow_i_ref[:, cols]
        row = lambda tile, j: jnp.broadcast_to(tile[j:j + 1], (8, ch))
        levels = [(d, row(pow_r, 8 - d if reverse else d - 1), row(pow_i, 8 - d if reverse else d - 1))
                  for d in (1, 2, 4)]

        def group(g, carry):
            r0 = pl.multiple_of(((n_groups - 1 - g) if reverse else g) * 8, 8)
            xr = re_ref[pl.ds(r0, 8), cols]
            xi = im_ref[pl.ds(r0, 8), cols]
            for d, lr, li in levels:
                keep = (rid < 8 - d) if reverse else (rid >= d)
                shift = (8 - d) if reverse else d
                sr = jnp.where(keep, pltpu.roll(xr, shift, axis=0), 0.0)
                si = jnp.where(keep, pltpu.roll(xi, shift, axis=0), 0.0)
                mr, mi = _cmul(lr, li, sr, si)
                xr = xr + mr
                xi = xi + mi
            mr, mi = _cmul(pow_r, pow_i, carry[0], carry[1])
            xr = xr + mr
            xi = xi + mi
            re_ref[pl.ds(r0, 8), cols] = xr
            im_ref[pl.ds(r0, 8), cols] = xi
            last = 0 if reverse else 7
            return row(xr, last), row(xi, last)

        out = lax.fori_loop(0, n_groups, group, (carry_r_ref[:, cols], carry_i_ref[:, cols]))
        carry_r_ref[:, cols] = out[0]
        carry_i_ref[:, cols] = out[1]


def _s5_fwd(u, b4_re, b4_im, c4_re, c4_im, pow_r, pow_i, tt):
    L = u.shape[0]

    def body(u_ref, bre_ref, bim_ref, cre_ref, cim_ref, pr_ref, pi_ref, sre_o, sim_o, y_o, car_r, car_i):
        @pl.when(pl.program_id(0) == 0)
        def _():
            car_r[...] = jnp.zeros_like(car_r)
            car_i[...] = jnp.zeros_like(car_i)

        uv = u_ref[...]
        for q in range(S5_BLOCKS):
            uq = uv[:, q * LANES:(q + 1) * LANES]
            cols = slice(q * 512, (q + 1) * 512)
            sre_o[:, cols] = _dot_bf(uq, bre_ref[q], ((1,), (0,)))
            sim_o[:, cols] = _dot_bf(uq, bim_ref[q], ((1,), (0,)))
        _tile_scan(sre_o, sim_o, pr_ref, pi_ref, car_r, car_i, reverse=False)
        for q in range(S5_BLOCKS):
            cols = slice(q * 512, (q + 1) * 512)
            y_o[:, q * LANES:(q + 1) * LANES] = (_dot_bf(sre_o[:, cols], cre_ref[q], ((1,), (0,)))
                                                 - _dot_bf(sim_o[:, cols], cim_ref[q], ((1,), (0,))))

    row = lambda n: pl.BlockSpec((tt, n), lambda i: (i, 0))
    return pl.pallas_call(
        body, name="s5_fwd", grid=(L // tt,),
        in_specs=[row(D_S5)] + [_const_spec((S5_BLOCKS, LANES, 512))] * 2 + [_const_spec((S5_BLOCKS, 512, LANES))] * 2
        + [_const_spec((8, N_STATE))] * 2,
        out_specs=[row(N_STATE), row(N_STATE), row(D_S5)],
        out_shape=[_sds((L, N_STATE)), _sds((L, N_STATE)), _sds((L, D_S5))],
        scratch_shapes=[pltpu.VMEM((8, N_STATE), F32)] * 2,
        compiler_params=_params("arbitrary"),
    )(u, b4_re, b4_im, c4_re, c4_im, pow_r, pow_i)


def _rwkv_pre_fn(r, k, v, wa, w0, w2p, a0, a2p, k_k, k_a, ee):
    w = -_softplus(-(w0 + mm_bf(jnp.tanh(wa), w2p))) - 0.5
    logw = -jnp.exp(w)
    a = _sigmoid(a0 + mm_bf(wa, a2p))
    kkp = k * k_k
    kk = kkp / jnp.maximum(jnp.sqrt(head_sum(kkp * kkp, ee)), 1e-12)
    k2 = k * (1.0 + (a - 1.0) * k_a)
    return r, logw, k2, v, -kk, kk * a


def _head_spec(tt):
    return pl.BlockSpec((N_HEADS, tt, HEAD), lambda i: (0, i, 0))


def _load_heads(ref):
    return jnp.concatenate([ref[h] for h in range(N_HEADS)], axis=-1)


def _store_heads(ref, val):
    for h in range(N_HEADS):
        ref[h] = val[:, h * HEAD:(h + 1) * HEAD]


def _shifted(rw, prev_blk, first):
    rolled = pltpu.roll(rw, 1, axis=0)
    prev_row = jnp.where(first, 0.0, prev_blk[7:8, :])
    rid = lax.broadcasted_iota(jnp.int32, rw.shape, 0)
    return jnp.where(rid == 0, jnp.broadcast_to(prev_row, rw.shape), rolled)


def _split_rw(t):
    return t[:, 0:512], t[:, 512:1024], t[:, 1024:1536], t[:, 1536:1664]


def _rwkv_pre_specs(tt):
    row = pl.BlockSpec((tt, D_SHIFT), lambda i: (i, 0))
    prev = pl.BlockSpec((8, D_SHIFT), lambda i: (jnp.maximum(i * (tt // 8) - 1, 0), 0))
    consts = [_const_spec((1, D_SHIFT)), _const_spec((1, D_RWKV)), _const_spec((LANES, D_RWKV)),
              _const_spec((1, D_RWKV)), _const_spec((LANES, D_RWKV)), _const_spec((1, D_RWKV)),
              _const_spec((1, D_RWKV)), _const_spec((D_RWKV, D_RWKV))]
    return [row, prev] + consts


def _rwkv_pre_fwd(rw, mu, w0, w2p, a0, a2p, k_k, k_a, ee, tt):
    L = rw.shape[0]

    def body(rw_ref, prev_ref, mu_ref, w0_ref, w2_ref, a0_ref, a2_ref, kk_ref, ka_ref, ee_ref, *outs):
        rwv = rw_ref[...]
        rws = rwv + (_shifted(rwv, prev_ref[...], pl.program_id(0) == 0) - rwv) * mu_ref[...]
        res = _rwkv_pre_fn(*_split_rw(rws), w0_ref[...], w2_ref[...], a0_ref[...], a2_ref[...],
                           kk_ref[...], ka_ref[...], ee_ref[...])
        for o, val in zip(outs, res):
            _store_heads(o, val)

    return pl.pallas_call(
        body, name="rwkv_pre_fwd", grid=(L // tt,),
        in_specs=_rwkv_pre_specs(tt), out_specs=[_head_spec(tt)] * 6, out_shape=[_sds((N_HEADS, L, HEAD))] * 6,
        compiler_params=_params("parallel"),
    )(rw, rw, mu, w0, w2p, a0, a2p, k_k, k_a, ee)


def _rwkv_pre_bwd(rw, mu, w0, w2p, a0, a2p, k_k, k_a, ee, cots, tt):
    L = rw.shape[0]
    n_t = L // tt

    def body(rw_ref, prev_ref, mu_ref, w0_ref, w2_ref, a0_ref, a2_ref, kk_ref, ka_ref, ee_ref,
             c_r, c_w, c_k, c_v, c_a, c_b, cb_r, cb_k, cb_v,
             drws_ref, dmu_o, dw0_o, dw2_o, da0_o, da2_o, dkk_o, dka_o,
             dmu, dw0, dw2, da0, da2, dkk, dka):
        i = pl.program_id(0)
        accs = (dmu, dw0, dw2, da0, da2, dkk, dka)

        @pl.when(i == 0)
        def _():
            for acc in accs:
                acc[...] = jnp.zeros_like(acc)

        rwv = rw_ref[...]
        diff = _shifted(rwv, prev_ref[...], i == 0) - rwv
        rws = rwv + diff * mu_ref[...]
        consts = (w0_ref[...], w2_ref[...], a0_ref[...], a2_ref[...], kk_ref[...], ka_ref[...])
        _, vjp = jax.vjp(lambda *a: _rwkv_pre_fn(*a, ee_ref[...]), *_split_rw(rws), *consts)
        scan = [_load_heads(c) for c in (c_r, c_w, c_k, c_v, c_a, c_b)]
        g = vjp((scan[0] + cb_r[...], scan[1], scan[2] + cb_k[...], scan[3] + cb_v[...], scan[4], scan[5]))
        drws = jnp.concatenate(g[0:4], axis=-1)
        drws_ref[...] = drws
        dmu[...] += jnp.sum(drws * diff, axis=0, keepdims=True)
        for acc, val in zip(accs[1:], g[4:]):
            acc[...] += val

        @pl.when(i == n_t - 1)
        def _():
            for acc, out in zip(accs, (dmu_o, dw0_o, dw2_o, da0_o, da2_o, dkk_o, dka_o)):
                out[...] = acc[...]

    row = pl.BlockSpec((tt, D_RWKV), lambda i: (i, 0))
    shapes = [(1, D_SHIFT), (1, D_RWKV), (LANES, D_RWKV), (1, D_RWKV), (LANES, D_RWKV), (1, D_RWKV), (1, D_RWKV)]
    return pl.pallas_call(
        body, name="rwkv_pre_bwd", grid=(n_t,),
        in_specs=_rwkv_pre_specs(tt) + [_head_spec(tt)] * 6 + [row] * 3,
        out_specs=[pl.BlockSpec((tt, D_SHIFT), lambda i: (i, 0))] + [_acc_spec(s) for s in shapes],
        out_shape=[_sds((L, D_SHIFT))] + [_sds(s) for s in shapes],
        scratch_shapes=[pltpu.VMEM(s, F32) for s in shapes],
        compiler_params=_params("arbitrary"),
    )(rw, rw, mu, w0, w2p, a0, a2p, k_k, k_a, ee, *cots)


def _bmm(a, b):
    return lax.dot_general(a, b, (((2,), (1,)), ((0,), (0,))), precision=HI, preferred_element_type=F32)


def _bmm_nt(a, b):
    return lax.dot_general(a, b, (((2,), (2,)), ((0,), (0,))), precision=HI, preferred_element_type=F32)


def _bmm_tn(a, b):
    return lax.dot_general(a, b, (((1,), (1,)), ((0,), (0,))), precision=HI, preferred_element_type=F32)


def _bdot_bf(a, b, lhs_dim, rhs_dim):
    return lax.dot_general(a.astype(BF16), b.astype(BF16), (((lhs_dim,), (rhs_dim,)), ((0,), (0,))),
                           preferred_element_type=F32)


@jax.custom_vjp
def _bmm_bf(a, b):
    return _bdot_bf(a, b, 2, 1)


def _bmm_bf_fwd(a, b):
    return _bmm_bf(a, b), (a, b)


def _bmm_bf_bwd(res, g):
    a, b = res
    return _bdot_bf(g, b, 2, 2), _bdot_bf(a, g, 1, 1)


_bmm_bf.defvjp(_bmm_bf_fwd, _bmm_bf_bwd)


@jax.custom_vjp
def _bmm_tn_bf(a, b):
    return _bdot_bf(a, b, 1, 1)


def _bmm_tn_bf_fwd(a, b):
    return _bmm_tn_bf(a, b), (a, b)


def _bmm_tn_bf_bwd(res, g):
    a, b = res
    return _bdot_bf(b, g, 2, 2), _bdot_bf(a, g, 2, 1)


_bmm_tn_bf.defvjp(_bmm_tn_bf_fwd, _bmm_tn_bf_bwd)


def _unit_lower_inverse(a):
    t = a.shape[-1]
    ti = lax.broadcasted_iota(jnp.int32, (t, t), 0)
    si = lax.broadcasted_iota(jnp.int32, (t, t), 1)

    def same_block(bits):
        shift = jnp.int32(bits)
        return (lax.shift_right_logical(ti, shift) == lax.shift_right_logical(si, shift))[None]

    def mm(x, y):
        return _bdot_bf(x, y, 2, 1)

    d = jnp.where(same_block(3), a, 0.0)
    inv = jnp.where(ti == si, 1.0, 0.0)[None] + d
    pw = mm(d, d)
    both = mm(jnp.concatenate([inv, pw], axis=1), pw)
    inv = inv + both[:, :t]
    inv = inv + mm(inv, both[:, t:])
    bits = 3
    while (1 << bits) < t:
        e = jnp.where(same_block(bits), 0.0, jnp.where(same_block(bits + 1), a, 0.0))
        inv = inv + mm(mm(inv, e), inv)
        bits += 1
    return inv


def _tri_mask(t):
    ri = lax.broadcasted_iota(jnp.int32, (2 * t, 2 * t), 0)
    ci = lax.broadcasted_iota(jnp.int32, (2 * t, 2 * t), 1)
    top_rows = ri < t
    diff = jnp.where(top_rows, ri, ri - t) - jnp.where(ci < t, ci, ci - t)
    return (diff >= jnp.where(top_rows, 1, 0))[None]


def _ones_tri(n_h, t):
    ti = lax.broadcasted_iota(jnp.int32, (t, t), 0)
    si = lax.broadcasted_iota(jnp.int32, (t, t), 1)
    return jnp.broadcast_to(jnp.where(ti >= si, 1.0, 0.0)[None], (n_h, t, t))


@jax.custom_vjp
def _running_sum_kept(logw, kept):
    return kept


def _running_sum_kept_bwd(shape, g):
    return _bmm_tn(_ones_tri(shape[0], shape[1]), g), jnp.zeros_like(g)


_running_sum_kept.defvjp(lambda logw, kept: (kept, logw.shape), _running_sum_kept_bwd)


@jax.custom_vjp
def _tri_products_kept(ar, bk, kept):
    return kept


def _tri_products_kept_bwd(res, g):
    ar, bk = res
    g = jnp.where(_tri_mask(ar.shape[1] // 2), g, 0.0)
    return _bmm(g, bk), _bmm_tn(g, ar), jnp.zeros_like(g)


_tri_products_kept.defvjp(lambda ar, bk, kept: (kept, (ar, bk)), _tri_products_kept_bwd)


@jax.custom_vjp
def _solve_unit_lower(a, rhs, inv, kept=None):
    return _bmm(inv, rhs) if kept is None else kept


def _solve_fwd(a, rhs, inv, kept=None):
    u = _bmm(inv, rhs) if kept is None else kept
    return u, (inv, u, kept is not None)


def _solve_bwd(res, du):
    inv, u, had_kept = res
    d_rhs = _bmm_tn(inv, du)
    return _bmm_nt(d_rhs, u), d_rhs, jnp.zeros_like(inv), (jnp.zeros_like(u) if had_kept else None)


_solve_unit_lower.defvjp(_solve_fwd, _solve_bwd)


def _rwkv_chunk(st0, r, logw, k, v, a, b, kept=None):
    n_h, t, _ = r.shape
    log_p = _bmm(_ones_tri(n_h, t), logw) if kept is None else _running_sum_kept(logw, kept[0])
    p_in = jnp.exp(log_p)
    p_inv = jnp.exp(-log_p)
    at = a * jnp.exp(log_p - logw)
    rt = r * p_in
    ar = jnp.concatenate([at, rt], axis=1)
    bk = jnp.concatenate([b * p_inv, k * p_inv], axis=1)
    if kept is None:
        m = jnp.where(_tri_mask(t), _bmm_nt(ar, bk), 0.0)
        inv = _unit_lower_inverse(m[:, :t, :t])
    else:
        m = _tri_products_kept(ar, bk, kept[1])
        inv = kept[2]
    top, bottom = m[:, :t], m[:, t:]
    rhs = _bmm_bf(jnp.concatenate([at, top[:, :, t:]], axis=2), jnp.concatenate([st0, v], axis=1))
    u = _solve_unit_lower(top[:, :, :t], rhs, inv, None if kept is None else kept[3])
    y = _bmm_bf(jnp.concatenate([rt, bottom], axis=2), jnp.concatenate([st0, u, v], axis=1))
    p_end = jnp.swapaxes(p_in[:, t - 1:t, :], 1, 2)
    st1 = (st0 + _bmm_tn_bf(bk, jnp.concatenate([u, v], axis=1))) * p_end
    return y, st1, (log_p, m, inv, u)


def _rwkv_scan_fwd(ops):
    n_h, L, n = ops[0].shape
    t = RWKV_CHUNK
    per = min(RWKV_CHUNKS_PER_STEP, L // t)
    n_c = L // t
    n_s = n_c // per

    def body(r_ref, w_ref, k_ref, v_ref, a_ref, b_ref, y_ref, st_ref, logp_ref, m_ref, inv_ref, u_ref, st):
        @pl.when(pl.program_id(0) == 0)
        def _():
            st[...] = jnp.zeros_like(st)

        st0 = st[...]
        for j in range(per):
            rows = slice(j * t, (j + 1) * t)
            st_ref[j] = st0
            y, st0, (log_p, m, inv, u) = _rwkv_chunk(
                st0, *(ref[:, rows, :] for ref in (r_ref, w_ref, k_ref, v_ref, a_ref, b_ref)))
            y_ref[:, rows, :] = y
            logp_ref[:, rows, :] = log_p
            u_ref[:, rows, :] = u
            m_ref[j] = m
            inv_ref[j] = inv
        st[...] = st0

    blk = pl.BlockSpec((n_h, per * t, n), lambda c: (0, c, 0))
    per_chunk = lambda m: pl.BlockSpec((per, n_h, m, m), lambda c: (c, 0, 0, 0))
    return pl.pallas_call(
        body, name="rwkv_scan_fwd", grid=(n_s,), in_specs=[blk] * 6,
        out_specs=[blk, per_chunk(n), blk, per_chunk(2 * t), per_chunk(t), blk],
        out_shape=[_sds((n_h, L, n)), _sds((n_c, n_h, n, n)), _sds((n_h, L, n)), _sds((n_c, n_h, 2 * t, 2 * t)),
                   _sds((n_c, n_h, t, t)), _sds((n_h, L, n))],
        scratch_shapes=[pltpu.VMEM((n_h, n, n), F32)],
        compiler_params=_params("arbitrary"),
    )(*ops)


def _rwkv_scan_bwd(ops, states, kept, dy):
    n_h, L, n = ops[0].shape
    t = RWKV_CHUNK
    per = min(RWKV_CHUNKS_PER_STEP, L // t)
    n_s = L // t // per

    def body(r_ref, w_ref, k_ref, v_ref, a_ref, b_ref, st_ref, logp_ref, m_ref, inv_ref, u_ref, dy_ref,
             dr, dw, dk, dv, da, db, dst):
        @pl.when(pl.program_id(0) == 0)
        def _():
            dst[...] = jnp.zeros_like(dst)

        vjps = []
        for j in range(per):
            rows = slice(j * t, (j + 1) * t)
            have = (logp_ref[:, rows, :], m_ref[j], inv_ref[j], u_ref[:, rows, :])
            args = [ref[:, rows, :] for ref in (r_ref, w_ref, k_ref, v_ref, a_ref, b_ref)]
            vjps.append(jax.vjp(lambda *a, have=have: _rwkv_chunk(*a, kept=have)[:2], st_ref[j], *args)[1])
        d_state = dst[...]
        for j in reversed(range(per)):
            rows = slice(j * t, (j + 1) * t)
            g = vjps[j]((dy_ref[:, rows, :], d_state))
            d_state = g[0]
            for out, val in zip((dr, dw, dk, dv, da, db), g[1:]):
                out[:, rows, :] = val
        dst[...] = d_state

    blk = pl.BlockSpec((n_h, per * t, n), lambda c: (0, n_s - 1 - c, 0))
    per_chunk = lambda m: pl.BlockSpec((per, n_h, m, m), lambda c: (n_s - 1 - c, 0, 0, 0))
    return pl.pallas_call(
        body, name="rwkv_scan_bwd", grid=(n_s,),
        in_specs=[blk] * 6 + [per_chunk(n), blk, per_chunk(2 * t), per_chunk(t), blk, blk],
        out_specs=[blk] * 6, out_shape=[_sds((n_h, L, n))] * 6,
        scratch_shapes=[pltpu.VMEM((n_h, n, n), F32)],
        compiler_params=_params("arbitrary"),
    )(*ops, states, *kept, dy)


def _post_fn(x, u, zs, zr, ysc, r, k2, v, y_ssm, d, glu_w, glu_b, ln_w, ln_b, r_k,
             wo_s5, wo_rwkv, gf, tgt, ee):
    y3 = _gelu(y_ssm + d * u)
    y_s5 = y3 * _sigmoid(mm_bf(y3, glu_w) + glu_b) * _silu(zs)
    mean = head_sum_split(ysc, ee) * (1.0 / HEAD)
    yc = ysc - mean
    var = head_sum(yc * yc, ee) * (1.0 / HEAD)
    gn = yc * lax.rsqrt(var + GN_EPS) * ln_w + ln_b
    bonus = head_sum(r * k2 * r_k, ee) * v
    y_rwkv = (gn + bonus) * _silu(zr)
    x2 = x + mm_bf(y_s5, wo_s5) + mm_bf(y_rwkv, wo_rwkv)
    err = _rms(x2, gf) - tgt
    return 0.5 * jnp.mean(err * err, axis=-1, keepdims=True)


def _post(x, u, zs, zr, ysc, r, k2, v, y_ssm, d, glu_w, glu_b, ln_w, ln_b, r_k, w_out, gf, tgt, ee, tt):
    L = x.shape[0]
    n_t = L // tt
    acc_shapes = [(1, D_S5), (D_S5, D_S5), (1, D_S5), (1, D_RWKV), (1, D_RWKV), (1, D_RWKV),
                  (D_MODEL, D_MODEL), (1, D_MODEL), (8, LANES)]

    def body(x_ref, u_ref, zs_ref, zr_ref, ysc_ref, r_ref, k2_ref, v_ref, yssm_ref,
             d_ref, gw_ref, gb_ref, lw_ref, lb_ref, rk_ref, wo_ref, gf_ref, tgt_ref, ee_ref,
             dx_o, du_o, dzs_o, dzr_o, dysc_o, dr_o, dk2_o, dv_o, dyssm_o,
             dd_o, dgw_o, dgb_o, dlw_o, dlb_o, drk_o, dwo_o, dgf_o, loss_o,
             dd, dgw, dgb, dlw, dlb, drk, dwo, dgf, loss):
        i = pl.program_id(0)
        accs = (dd, dgw, dgb, dlw, dlb, drk, dwo, dgf, loss)

        @pl.when(i == 0)
        def _():
            for acc in accs:
                acc[...] = jnp.zeros_like(acc)

        args = (x_ref[...], u_ref[...], zs_ref[...], zr_ref[...],
                _load_heads(ysc_ref), _load_heads(r_ref), _load_heads(k2_ref), _load_heads(v_ref), yssm_ref[...],
                d_ref[...], gw_ref[...], gb_ref[...], lw_ref[...], lb_ref[...], rk_ref[...],
                wo_ref[0:D_S5, :], wo_ref[D_S5:D_MODEL, :], gf_ref[...])
        rows, vjp = jax.vjp(lambda *a: _post_fn(*a, tgt_ref[...], ee_ref[...]), *args)
        g = vjp(jnp.ones_like(rows))
        for out, val in zip((dx_o, du_o, dzs_o, dzr_o), g[0:4]):
            out[...] = val
        _store_heads(dysc_o, g[4])
        for out, val in zip((dr_o, dk2_o, dv_o, dyssm_o), g[5:9]):
            out[...] = val
        for acc, val in zip((dd, dgw, dgb, dlw, dlb, drk), g[9:15]):
            acc[...] += val
        dwo[0:D_S5, :] += g[15]
        dwo[D_S5:D_MODEL, :] += g[16]
        dgf[...] += g[17]
        loss[...] += jnp.broadcast_to(jnp.sum(rows, axis=0, keepdims=True), loss.shape)

        @pl.when(i == n_t - 1)
        def _():
            for acc, out in zip(accs, (dd_o, dgw_o, dgb_o, dlw_o, dlb_o, drk_o, dwo_o, dgf_o, loss_o)):
                pltpu.sync_copy(acc, out)

    row = lambda n: pl.BlockSpec((tt, n), lambda i: (i, 0))
    in_specs = ([row(D_MODEL)] + [row(512)] * 3 + [_head_spec(tt)] * 4 + [row(D_S5)]
                + [_const_spec(s) for s in [(1, D_S5), (D_S5, D_S5), (1, D_S5), (1, D_RWKV), (1, D_RWKV), (1, D_RWKV),
                                            (D_MODEL, D_MODEL), (1, D_MODEL)]]
                + [row(D_MODEL), _const_spec((D_RWKV, D_RWKV))])
    out_rows = [D_MODEL] + [512] * 3 + [None] + [512] * 4
    return pl.pallas_call(
        body, name="post_fwd_bwd", grid=(n_t,), in_specs=in_specs,
        out_specs=[row(n) if n else _head_spec(tt) for n in out_rows] + [_ANY] * len(acc_shapes),
        out_shape=([_sds((L, n)) if n else _sds((N_HEADS, L, HEAD)) for n in out_rows]
                   + [_sds(s) for s in acc_shapes]),
        scratch_shapes=[pltpu.VMEM(s, F32) for s in acc_shapes],
        compiler_params=_params("arbitrary"),
    )(x, u, zs, zr, ysc, r, k2, v, y_ssm, d, glu_w, glu_b, ln_w, ln_b, r_k, w_out, gf, tgt, ee)


def _s5_bwd(u, du_direct, dy, s_re, s_im, b4_re, b4_im, c4_re, c4_im, pow_r, pow_i, tt):
    L = u.shape[0]
    n_t = L // tt
    acc_shapes = ([(S5_BLOCKS, LANES, 512)] * 2 + [(S5_BLOCKS, 512, LANES)] * 2 + [(1, N_STATE)] * 2)

    def body(u_ref, dud_ref, dy_ref, sre_ref, sim_ref, pre_ref, pim_ref, bre_ref, bim_ref, cre_ref, cim_ref,
             pr_ref, pi_ref, du_o, dbre_o, dbim_o, dcre_o, dcim_o, dlr_o, dli_o,
             dbre, dbim, dcre, dcim, dlr, dli, gre, gim, car_r, car_i):
        i = pl.program_id(0)

        @pl.when(i == 0)
        def _():
            for acc in (dbre, dbim, dcre, dcim, dlr, dli, car_r, car_i):
                acc[...] = jnp.zeros_like(acc)

        uv = u_ref[...]
        dyv = dy_ref[...]
        blocks = [slice(q * 512, (q + 1) * 512) for q in range(S5_BLOCKS)]
        lanes = [slice(q * LANES, (q + 1) * LANES) for q in range(S5_BLOCKS)]
        for q in range(S5_BLOCKS):
            gre[:, blocks[q]] = _dot_bf(dyv[:, lanes[q]], cre_ref[q], ((1,), (1,)))
            gim[:, blocks[q]] = -_dot_bf(dyv[:, lanes[q]], cim_ref[q], ((1,), (1,)))
        _tile_scan(gre, gim, pr_ref, pi_ref, car_r, car_i, reverse=True)
        for q in range(S5_BLOCKS):
            gr = gre[:, blocks[q]]
            gi = gim[:, blocks[q]]
            sr = sre_ref[:, blocks[q]]
            si = sim_ref[:, blocks[q]]
            du_o[:, lanes[q]] = (dud_ref[:, lanes[q]] + _dot_bf(gr, bre_ref[q], ((1,), (1,)))
                                 + _dot_bf(gi, bim_ref[q], ((1,), (1,))))
            dbre[q] += _dot_bf(uv[:, lanes[q]], gr, ((0,), (0,)))
            dbim[q] += _dot_bf(uv[:, lanes[q]], gi, ((0,), (0,)))
            dcre[q] += _dot_bf(sr, dyv[:, lanes[q]], ((0,), (0,)))
            dcim[q] -= _dot_bf(si, dyv[:, lanes[q]], ((0,), (0,)))
            rid = lax.broadcasted_iota(jnp.int32, sr.shape, 0)
            first = i == n_t - 1
            prev_r = jnp.where(first, 0.0, pre_ref[7:8, blocks[q]])
            prev_i = jnp.where(first, 0.0, pim_ref[7:8, blocks[q]])
            pr = jnp.where(rid == 0, jnp.broadcast_to(prev_r, sr.shape), pltpu.roll(sr, 1, axis=0))
            pi_ = jnp.where(rid == 0, jnp.broadcast_to(prev_i, si.shape), pltpu.roll(si, 1, axis=0))
            dlr[:, blocks[q]] += jnp.sum(pr * gr + pi_ * gi, axis=0, keepdims=True)
            dli[:, blocks[q]] += jnp.sum(pr * gi - pi_ * gr, axis=0, keepdims=True)

        @pl.when(i == n_t - 1)
        def _():
            for acc, out in zip((dbre, dbim, dcre, dcim, dlr, dli), (dbre_o, dbim_o, dcre_o, dcim_o, dlr_o, dli_o)):
                out[...] = acc[...]

    row = lambda n: pl.BlockSpec((tt, n), lambda i: (n_t - 1 - i, 0))
    prev = pl.BlockSpec((8, N_STATE), lambda i: (jnp.maximum((n_t - 1 - i) * (tt // 8) - 1, 0), 0))
    return pl.pallas_call(
        body, name="s5_bwd", grid=(n_t,),
        in_specs=[row(D_S5)] * 3 + [row(N_STATE)] * 2 + [prev] * 2
        + [_const_spec((S5_BLOCKS, LANES, 512))] * 2 + [_const_spec((S5_BLOCKS, 512, LANES))] * 2
        + [_const_spec((8, N_STATE))] * 2,
        out_specs=[row(D_S5)] + [_acc_spec(s) for s in acc_shapes],
        out_shape=[_sds((L, D_S5))] + [_sds(s) for s in acc_shapes],
        scratch_shapes=[pltpu.VMEM(s, F32) for s in acc_shapes] + [pltpu.VMEM((tt, N_STATE), F32)] * 2
        + [pltpu.VMEM((8, N_STATE), F32)] * 2,
        compiler_params=_params("arbitrary"),
    )(u, du_direct, dy, s_re, s_im, s_re, s_im, b4_re, b4_im, c4_re, c4_im, pow_r, pow_i)


def _bwd_in(x, norm_g, w_in_bf, mu, dx2, du, dzs, drws, dzr, tt):
    L = x.shape[0]
    n_t = L // tt

    def body(x_ref, g_ref, w_ref, mu_ref, dx2_ref, du_ref, dzs_ref, drws_ref, nxt_ref, dzr_ref,
             gx_o, dw_o, dg_o, dproj, dw, dg):
        i = pl.program_id(0)

        @pl.when(i == 0)
        def _():
            dw[...] = jnp.zeros_like(dw)
            dg[...] = jnp.zeros_like(dg)

        drws_v = drws_ref[...]
        rid = lax.broadcasted_iota(jnp.int32, drws_v.shape, 0)
        nxt_row = jnp.where(i == n_t - 1, 0.0, nxt_ref[0:1, :])
        nxt = jnp.where(rid == tt - 1, jnp.broadcast_to(nxt_row, drws_v.shape), pltpu.roll(drws_v, tt - 1, axis=0))
        muv = mu_ref[...]
        drw = drws_v * (1.0 - muv) + nxt * muv
        dproj[:, 0:D_S5] = du_ref[...].astype(BF16)
        dproj[:, D_S5:2 * D_S5] = dzs_ref[...].astype(BF16)
        dproj[:, 2 * D_S5:2 * D_S5 + D_SHIFT] = drw.astype(BF16)
        dproj[:, 2 * D_S5 + D_SHIFT:D_IN] = dzr_ref[...].astype(BF16)
        dh = _dot(dproj[...], w_ref[...], ((1,), (1,)), None)
        h, vjp = jax.vjp(_rms, x_ref[...], g_ref[...])
        dxh, dgv = vjp(dh)
        gx_o[...] = dx2_ref[...] + dxh
        dg[...] += dgv
        dw[...] += _dot(h.astype(BF16), dproj[...], ((0,), (0,)), None)

        @pl.when(i == n_t - 1)
        def _():
            dg_o[...] = dg[...]
            pltpu.sync_copy(dw, dw_o)

    row = lambda n: pl.BlockSpec((tt, n), lambda i: (i, 0))
    nxt = pl.BlockSpec((8, D_SHIFT), lambda i: (jnp.minimum((i + 1) * (tt // 8), L // 8 - 1), 0))
    return pl.pallas_call(
        body, name="bwd_in", grid=(n_t,),
        in_specs=[row(D_MODEL), _const_spec((1, D_MODEL)), _const_spec((D_MODEL, D_IN)), _const_spec((1, D_SHIFT)),
                  row(D_MODEL), row(D_S5), row(D_S5), row(D_SHIFT), nxt, row(D_RWKV)],
        out_specs=[row(D_MODEL), _ANY, _acc_spec((1, D_MODEL))],
        out_shape=[_sds((L, D_MODEL)), _sds((D_MODEL, D_IN)), _sds((1, D_MODEL))],
        scratch_shapes=[pltpu.VMEM((tt, D_IN), BF16), pltpu.VMEM((D_MODEL, D_IN), F32), pltpu.VMEM((1, D_MODEL), F32)],
        compiler_params=_params("arbitrary"),
    )(x, norm_g, w_in_bf, mu, dx2, du, dzs, drws, drws, dzr)


def _block_diag_b(bbar):
    bb = bbar.reshape(S5_GROUP, S5_BLOCKS, 8, S5_STATE)
    return jnp.einsum('hqgp,Gg->qGhgp', bb, jnp.eye(8, dtype=F32)).reshape(S5_BLOCKS, LANES, 512)


def _block_diag_b_t(db4):
    d = db4.reshape(S5_BLOCKS, 8, S5_GROUP, 8, S5_STATE)
    return jnp.einsum('qGhgp,Gg->hqgp', d, jnp.eye(8, dtype=F32)).reshape(S5_GROUP, N_STATE)


def _block_diag_c(c):
    cc = c.reshape(S5_BLOCKS, 8, S5_GROUP, S5_STATE)
    return jnp.einsum('qghp,gG->qgpGh', cc, jnp.eye(8, dtype=F32)).reshape(S5_BLOCKS, 512, LANES)


def _block_diag_c_t(dc4):
    d = dc4.reshape(S5_BLOCKS, 8, S5_STATE, 8, S5_GROUP)
    return jnp.einsum('qgpGh,gG->qghp', d, jnp.eye(8, dtype=F32)).reshape(S5_GROUPS, S5_GROUP, S5_STATE)


def _local_step(x, tgt, w):
    L = x.shape[0]
    tt = min(512, L)
    tp = min(256, L)
    ee = _head_sum_matrix()

    lam_re = w['s5_lam_re'].reshape(1, N_STATE)
    lam_im = w['s5_lam_im'].reshape(1, N_STATE)
    logdt = jnp.repeat(w['s5_log_dt'], S5_STATE).reshape(1, N_STATE)
    b_re_t = w['s5_b_re'].transpose(2, 0, 1).reshape(S5_GROUP, N_STATE)
    b_im_t = w['s5_b_im'].transpose(2, 0, 1).reshape(S5_GROUP, N_STATE)
    bbr, bbi, pow_r, pow_i, rpow_r, rpow_i = _s5_param_fwd(lam_re, lam_im, logdt, b_re_t, b_im_t)
    b4_re, b4_im = _block_diag_b(bbr), _block_diag_b(bbi)
    c4_re, c4_im = _block_diag_c(w['s5_c_re']), _block_diag_c(w['s5_c_im'])

    norm_g = w['norm_g'].reshape(1, D_MODEL)
    w_in_bf = w['w_in'].astype(BF16)
    u, zs, rw, zr = _fwd_in(x, norm_g, w_in_bf, tt)
    s_re, s_im, y_ssm = _s5_fwd(u, b4_re, b4_im, c4_re, c4_im, pow_r, pow_i, tt)

    row = lambda t: t.reshape(1, -1)
    zpad = jnp.zeros((HEAD, D_RWKV), F32)
    w2p = jnp.concatenate([w['rwkv_w2'], zpad], axis=0)
    a2p = jnp.concatenate([zpad, w['rwkv_a2']], axis=0)
    pre_consts = (row(w['rwkv_mu']), row(w['rwkv_w0']), w2p, row(w['rwkv_a0']), a2p,
                  row(w['rwkv_k_k']), row(w['rwkv_k_a']), ee)
    ops = _rwkv_pre_fwd(rw, *pre_consts, tt)
    ysc, states, *kept = _rwkv_scan_fwd(ops)

    post = _post(x, u, zs, zr, ysc, ops[0], ops[2], ops[3], y_ssm,
                 row(w['s5_d']), w['s5_glu_w'], row(w['s5_glu_b']), row(w['rwkv_ln_w']), row(w['rwkv_ln_b']),
                 row(w['rwkv_r_k']), w['w_out'], row(w['final_g']), tgt, ee, tp)
    (dx2, du_d, dzs, dzr, dysc, dr_b, dk2_b, dv_b, dy_ssm,
     dd, dglu_w, dglu_b, dln_w, dln_b, dr_k, dw_out, dgf, loss) = post

    du, db4_re, db4_im, dc4_re, dc4_im, dlbr, dlbi = _s5_bwd(
        u, du_d, dy_ssm, s_re, s_im, b4_re, b4_im, c4_re, c4_im, rpow_r, rpow_i, tt)
    group_ind = (jnp.arange(N_STATE)[:, None] // S5_STATE == jnp.arange(LANES)[None, :]).astype(F32)
    dlam_re, dlam_im, dlogdt, db_re_t, db_im_t = _s5_param_bwd(
        lam_re, lam_im, logdt, b_re_t, b_im_t, dlbr, dlbi, _block_diag_b_t(db4_re), _block_diag_b_t(db4_im), group_ind)

    cots = list(_rwkv_scan_bwd(ops, states, kept, dysc)) + [dr_b, dk2_b, dv_b]
    drws, dmu, dw0, dw2p, da0, da2p, dk_k, dk_a = _rwkv_pre_bwd(rw, *pre_consts, cots, tt)

    grad_x, dw_in, dnorm_g = _bwd_in(x, norm_g, w_in_bf, row(w['rwkv_mu']), dx2, du, dzs, drws, dzr, tt)

    unb = lambda t: t.reshape(S5_GROUP, S5_GROUPS, S5_STATE).transpose(1, 2, 0)
    grads = {
        'norm_g': dnorm_g.reshape(D_MODEL), 'w_in': dw_in,
        's5_lam_re': dlam_re.reshape(S5_GROUPS, S5_STATE), 's5_lam_im': dlam_im.reshape(S5_GROUPS, S5_STATE),
        's5_log_dt': dlogdt[0, :S5_GROUPS], 's5_b_re': unb(db_re_t), 's5_b_im': unb(db_im_t),
        's5_c_re': _block_diag_c_t(dc4_re), 's5_c_im': _block_diag_c_t(dc4_im),
        's5_d': dd.reshape(D_S5), 's5_glu_w': dglu_w, 's5_glu_b': dglu_b.reshape(D_S5),
        'rwkv_mu': dmu.reshape(-1), 'rwkv_w0': dw0.reshape(-1), 'rwkv_w2': dw2p[:HEAD], 'rwkv_a0': da0.reshape(-1),
        'rwkv_a2': da2p[HEAD:], 'rwkv_k_k': dk_k.reshape(-1), 'rwkv_k_a': dk_a.reshape(-1),
        'rwkv_r_k': dr_k.reshape(N_HEADS, HEAD), 'rwkv_ln_w': dln_w.reshape(-1), 'rwkv_ln_b': dln_b.reshape(-1),
        'w_out': dw_out, 'final_g': dgf.reshape(D_MODEL),
    }
    return loss, grad_x, grads


def _exchange(arrays, gather, axes, name):
    n = len(arrays)
    group = 2 ** len(axes)

    def body(*refs):
        send_refs, recv_refs = refs[:n], refs[n:2 * n]
        send_sems, recv_sems, local_sems = refs[2 * n:]
        pos = {ax: lax.axis_index(ax) for ax in ("x", "y", "c")}

        def index_of(p):
            idx = 0
            for ax in axes:
                idx = 2 * idx + p[ax]
            return idx

        me = index_of(pos)
        own, outs, arrivals = [], [], []
        for i, (send_ref, recv_ref) in enumerate(zip(send_refs, recv_refs)):
            def block_for(dev, send_ref=send_ref, whole=gather[i]):
                return send_ref if whole else send_ref.at[dev]

            own.append(pltpu.make_async_copy(block_for(me), recv_ref.at[me], local_sems.at[i]))
            own[-1].start()
            for k in range(1, group):
                peer = dict(pos)
                for bit, ax in enumerate(axes):
                    if (k >> bit) & 1:
                        peer[ax] = 1 - pos[ax]
                peer_idx = index_of(peer)
                sems = dict(send_sem=send_sems.at[i, k - 1], recv_sem=recv_sems.at[i, k - 1],
                            device_id=(peer["x"], peer["y"], peer["c"]), device_id_type=pl.DeviceIdType.MESH)
                outs.append(pltpu.make_async_remote_copy(src_ref=block_for(peer_idx), dst_ref=recv_ref.at[me], **sems))
                outs[-1].start()
                arrivals.append(
                    pltpu.make_async_remote_copy(src_ref=block_for(peer_idx), dst_ref=recv_ref.at[peer_idx], **sems))
        for copy in arrivals:
            copy.wait_recv()
        for copy in outs:
            copy.wait_send()
        for copy in own:
            copy.wait()

    return pl.pallas_call(
        body, name=name, in_specs=[_ANY] * n, out_specs=[_ANY] * n,
        out_shape=[jax.ShapeDtypeStruct(((group,) + a.shape) if whole else a.shape, a.dtype)
                   for a, whole in zip(arrays, gather)],
        scratch_shapes=[pltpu.SemaphoreType.DMA((n, group - 1)), pltpu.SemaphoreType.DMA((n, group - 1)),
                        pltpu.SemaphoreType.DMA((n,))],
        compiler_params=pltpu.CompilerParams(has_side_effects=True),
    )(*arrays)


def _sum_devices(ref):
    g = ref[0].astype(F32)
    for s in range(1, ref.shape[0]):
        g = g + ref[s].astype(F32)
    return g


def _adamw_math(g, w, m, v):
    m_new = ADAM_B1 * m + (1.0 - ADAM_B1) * g
    v_new = ADAM_B2 * v + (1.0 - ADAM_B2) * (g * g)
    m_hat = m_new / (1.0 - ADAM_B1 ** ADAM_STEP)
    v_hat = v_new / (1.0 - ADAM_B2 ** ADAM_STEP)
    return -ADAM_LR * (m_hat / (jnp.sqrt(v_hat) + ADAM_EPS) + ADAM_WD * w), m_new, v_new


def _adamw(gs, ws, ms, vs, reduce, name):
    n = len(ws)

    def body(*refs):
        g_refs, w_refs, m_refs, v_refs = (refs[j * n:(j + 1) * n] for j in range(4))
        outs = refs[4 * n:]
        for i in range(n):
            g = _sum_devices(g_refs[i]) if reduce else g_refs[i][...]
            res = _adamw_math(g, w_refs[i][...], m_refs[i][...], v_refs[i][...])
            for j, val in enumerate(((g,) if reduce else ()) + res):
                outs[j * n + i][...] = val

    return pl.pallas_call(
        body, name=name, out_shape=[_sds(w.shape) for w in ws] * (4 if reduce else 3),
        compiler_params=pltpu.CompilerParams(vmem_limit_bytes=VMEM_LIMIT),
    )(*gs, *ws, *ms, *vs)


def _sum_blocks(recv):
    def body(recv_ref, out_ref):
        out_ref[...] = _sum_devices(recv_ref)

    return pl.pallas_call(body, name="sum_small_grads", out_shape=_sds(recv.shape[1:]))(recv)


_WEIGHTS = [
    ('norm_g', (1, 1024), False), ('w_in', (1, 1024, 400), True), ('s5_lam_re', (1, 32, 64), False),
    ('s5_lam_im', (1, 32, 64), False), ('s5_log_dt', (1, 32), False), ('s5_b_re', (1, 32, 64, 16), False),
    ('s5_b_im', (1, 32, 64, 16), False), ('s5_c_re', (1, 32, 16, 64), False), ('s5_c_im', (1, 32, 16, 64), False),
    ('s5_d', (1, 512), False), ('s5_glu_w', (1, 64, 512), True), ('s5_glu_b', (1, 512), False),
    ('rwkv_mu', (1, 1664), False), ('rwkv_w0', (1, 512), False), ('rwkv_w2', (1, 64, 64), True),
    ('rwkv_a0', (1, 512), False), ('rwkv_a2', (1, 64, 64), True), ('rwkv_k_k', (1, 512), False),
    ('rwkv_k_a', (1, 512), False), ('rwkv_r_k', (1, 8, 64), False), ('rwkv_ln_w', (1, 512), False),
    ('rwkv_ln_b', (1, 512), False), ('w_out', (1, 128, 1024), True), ('final_g', (1024,), False),
]
_SHARDED = [(n, s) for n, s, sharded in _WEIGHTS if sharded]
_SMALL = [(n, s) for n, s, sharded in _WEIGHTS if not sharded]
_COLUMN_SHARDED = ('w_in', 'rwkv_w2', 'rwkv_a2')
_SMALL_ROWS = -(-sum(math.prod(s) for _, s in _SMALL) // (8 * LANES)) * 8


def _pack_small(grads):
    flat = [grads[n].reshape(-1) for n, _ in _SMALL]
    pad = _SMALL_ROWS * LANES - sum(f.size for f in flat)
    return jnp.concatenate(flat + [jnp.zeros((pad,), F32)]).reshape(_SMALL_ROWS, LANES)


def _unpack_small(packed):
    flat = packed.reshape(-1)
    out, off = {}, 0
    for n, s in _SMALL:
        size = math.prod(s)
        out[n] = flat[off:off + size].reshape(s)
        off += size
    return out


_BF16_OPERANDS = ('w_in', 's5_glu_w', 'w_out')


def _join_shards(name, blocks):
    _, rows, cols = blocks.shape
    if name in _COLUMN_SHARDED:
        return blocks.transpose(1, 0, 2).reshape(rows, N_DEV * cols)
    return blocks.reshape(N_DEV * rows, cols)


def _split_shards(name, full, shard_shape):
    rows, cols = shard_shape
    if name in _COLUMN_SHARDED:
        return full.reshape(rows, N_DEV, cols).transpose(1, 0, 2)
    return full.reshape(N_DEV, rows, cols)


def kernel(x, norm_g, w_in, s5_lam_re, s5_lam_im, s5_log_dt, s5_b_re, s5_b_im, s5_c_re, s5_c_im, s5_d, s5_glu_w, s5_glu_b, rwkv_mu, rwkv_w0, rwkv_w2, rwkv_a0, rwkv_a2, rwkv_k_k, rwkv_k_a, rwkv_r_k, rwkv_ln_w, rwkv_ln_b, w_out, final_g, loss_target, m_norm_g, m_w_in, m_s5_lam_re, m_s5_lam_im, m_s5_log_dt, m_s5_b_re, m_s5_b_im, m_s5_c_re, m_s5_c_im, m_s5_d, m_s5_glu_w, m_s5_glu_b, m_rwkv_mu, m_rwkv_w0, m_rwkv_w2, m_rwkv_a0, m_rwkv_a2, m_rwkv_k_k, m_rwkv_k_a, m_rwkv_r_k, m_rwkv_ln_w, m_rwkv_ln_b, m_w_out, m_final_g, v_norm_g, v_w_in, v_s5_lam_re, v_s5_lam_im, v_s5_log_dt, v_s5_b_re, v_s5_b_im, v_s5_c_re, v_s5_c_im, v_s5_d, v_s5_glu_w, v_s5_glu_b, v_rwkv_mu, v_rwkv_w0, v_rwkv_w2, v_rwkv_a0, v_rwkv_a2, v_rwkv_k_k, v_rwkv_k_a, v_rwkv_r_k, v_rwkv_ln_w, v_rwkv_ln_b, v_w_out, v_final_g):
    given = dict(locals())

    n_sh = len(_SHARDED)
    everyone = ("x", "y", "c")
    shards = [given[n][0].astype(BF16 if n in _BF16_OPERANDS else F32) for n, _ in _SHARDED]
    gathered = _exchange(shards, (True,) * n_sh, everyone, "gather_weights")
    local = {n: _join_shards(n, blocks).astype(F32 if n != 'w_in' else BF16)
             for (n, _), blocks in zip(_SHARDED, gathered)}
    local.update({n: (given[n][0] if len(s) > 1 else given[n]) for n, s in _SMALL})

    loss, grad_x, grads = _local_step(x[0], loss_target[0], local)

    blocks = [_split_shards(n, grads[n], s[1:]).astype(BF16) for n, s in _SHARDED]
    small = _pack_small(grads).reshape(N_DEV, _SMALL_ROWS // N_DEV, LANES)
    recv = _exchange(blocks + [small], (False,) * (n_sh + 1), everyone, "exchange_grads")
    small_sum = _exchange([_sum_blocks(recv[-1])], (True,), everyone, "gather_small_grads")[0]

    result = {}
    for group, name in (([0], "adamw_w_in"), ([1, 2, 3, 4], "adamw_shards")):
        ns = [_SHARDED[i][0] for i in group]
        res = _adamw([recv[i] for i in group], [given[n][0] for n in ns], [given['m_' + n][0] for n in ns],
                     [given['v_' + n][0] for n in ns], True, name)
        for j, n in enumerate(ns):
            result[n] = [res[k * len(ns) + j][None] for k in range(4)]
    g_small = _unpack_small(small_sum)
    two_d = lambda t: t.reshape(1, -1) if t.ndim == 1 else t
    ns = [n for n, _ in _SMALL]
    res = _adamw([two_d(g_small[n]) for n in ns], [two_d(given[n]) for n in ns], [two_d(given['m_' + n]) for n in ns],
                 [two_d(given['v_' + n]) for n in ns], False, "adamw_small")
    for j, (n, s) in enumerate(_SMALL):
        result[n] = [g_small[n]] + [res[k * len(ns) + j].reshape(s) for k in range(3)]

    total = lax.psum(loss[0, 0], ("x", "y", "c"))
    outs = [total, grad_x[None]]
    for k in range(4):
        outs += [result[n][k] for n, _, _ in _WEIGHTS]
    return tuple(outs)
```

```python
import math

import jax
import jax.numpy as jnp
from jax import lax
from jax.experimental import pallas as pl
from jax.experimental.pallas import tpu as pltpu

F32 = jnp.float32
BF16 = jnp.bfloat16
HI = lax.Precision.HIGH

D_MODEL = 1024
D_S5 = 512
D_RWKV = 512
S5_GROUPS = 32
S5_GROUP = 16
S5_STATE = 64
N_STATE = S5_GROUPS * S5_STATE
N_HEADS = 8
HEAD = 64
D_SHIFT = 3 * D_RWKV + 128
D_IN = 2 * D_S5 + D_SHIFT + D_RWKV
NORM_EPS = 1e-6
GN_EPS = 64e-5
N_DEV = 8
LANES = 128
S5_BLOCKS = 4
RWKV_CHUNK = 64
RWKV_CHUNKS_PER_STEP = 4
VMEM_LIMIT = 56 * 1024 * 1024

ADAM_LR = 0.001
ADAM_B1 = 0.9
ADAM_B2 = 0.999
ADAM_EPS = 1e-08
ADAM_WD = 0.01
ADAM_STEP = 10


def _dot(a, b, dims, prec):
    return lax.dot_general(a, b, (dims, ((), ())), precision=prec, preferred_element_type=F32)


def _dot_bf(a, b, dims):
    return _dot(a.astype(BF16), b.astype(BF16), dims, None)


def _make_mm(cast, prec):
    @jax.custom_vjp
    def mm(a, b):
        return _dot(cast(a), cast(b), ((1,), (0,)), prec)

    def fwd(a, b):
        return mm(a, b), (a, b)

    def bwd(res, g):
        a, b = res
        return (_dot(cast(g), cast(b), ((1,), (1,)), prec), _dot(cast(a), cast(g), ((0,), (0,)), prec))

    mm.defvjp(fwd, bwd)
    return mm


mm_bf = _make_mm(lambda t: t.astype(BF16), None)


def _make_head_sum(split):
    def product(x, ee):
        hi = x.astype(BF16)
        out = _dot(hi, ee, ((1,), (0,)), None)
        if split:
            out = out + _dot((x - hi.astype(F32)).astype(BF16), ee, ((1,), (0,)), None)
        return out

    @jax.custom_vjp
    def head_sum(x, ee):
        return product(x, ee)

    def fwd(x, ee):
        return product(x, ee), ee

    def bwd(ee, g):
        return product(g, ee), jnp.zeros_like(ee)

    head_sum.defvjp(fwd, bwd)
    return head_sum


head_sum = _make_head_sum(False)
head_sum_split = _make_head_sum(True)


@jax.custom_vjp
def _sigmoid(x):
    return 1.0 / (1.0 + jnp.exp(-x))


def _sigmoid_fwd(x):
    s = _sigmoid(x)
    return s, s


_sigmoid.defvjp(_sigmoid_fwd, lambda s, g: (g * s * (1.0 - s),))


@jax.custom_vjp
def _silu(x):
    return x * _sigmoid(x)


def _silu_fwd(x):
    s = _sigmoid(x)
    return x * s, (x, s)


def _silu_bwd(res, g):
    x, s = res
    return (g * s * (1.0 + x * (1.0 - s)),)


_silu.defvjp(_silu_fwd, _silu_bwd)


def _softplus(x):
    return jnp.maximum(x, 0.0) + jnp.log(1.0 + jnp.exp(-jnp.abs(x)))


_GELU_C = 2.0 * math.sqrt(2.0 / math.pi)


def _gelu_gate(x):
    return 1.0 / (1.0 + jnp.exp(-_GELU_C * x * (1.0 + 0.044715 * (x * x))))


@jax.custom_vjp
def _gelu(x):
    return x * _gelu_gate(x)


def _gelu_fwd(x):
    s = _gelu_gate(x)
    return x * s, (x, s)


def _gelu_bwd(res, g):
    x, s = res
    return (g * (s + x * s * (1.0 - s) * (_GELU_C * (1.0 + 3.0 * 0.044715 * (x * x)))),)


_gelu.defvjp(_gelu_fwd, _gelu_bwd)


def _rms(x, g):
    return x * lax.rsqrt(jnp.mean(x * x, axis=-1, keepdims=True) + NORM_EPS) * g


def _const_spec(shape):
    nd = len(shape)
    return pl.BlockSpec(shape, lambda *_: (0,) * nd, pipeline_mode=pl.Buffered(1))


def _acc_spec(shape):
    nd = len(shape)
    return pl.BlockSpec(shape, lambda *_: (0,) * nd)


def _params(sem):
    return pltpu.CompilerParams(dimension_semantics=(sem,), vmem_limit_bytes=VMEM_LIMIT)


_ANY = pl.BlockSpec(memory_space=pl.ANY)


def _sds(shape):
    return jax.ShapeDtypeStruct(shape, F32)


def _head_sum_matrix():
    i = jnp.arange(D_RWKV) // HEAD
    return (i[:, None] == i[None, :]).astype(BF16)


def _s5_param_fn(lam_re, lam_im, logdt, b_re, b_im):
    dt = jnp.exp(logdt)
    mag = jnp.exp(lam_re * dt)
    ang = lam_im * dt
    lbr = mag * jnp.cos(ang)
    lbi = mag * jnp.sin(ang)
    nr = lbr - 1.0
    den = lam_re * lam_re + lam_im * lam_im
    cr = (nr * lam_re + lbi * lam_im) / den
    ci = (lbi * lam_re - nr * lam_im) / den
    return lbr, lbi, cr * b_re - ci * b_im, cr * b_im + ci * b_re


def _cmul(ar, ai, br, bi):
    return ar * br - ai * bi, ar * bi + ai * br


def _s5_param_fwd(lam_re, lam_im, logdt, b_re, b_im):
    def body(lr, li, ld, br, bi, o_br, o_bi, o_pr, o_pi, o_qr, o_qi):
        lbr, lbi, bbr, bbi = _s5_param_fn(lr[...], li[...], ld[...], br[...], bi[...])
        o_br[...] = bbr
        o_bi[...] = bbi
        rid = lax.broadcasted_iota(jnp.int32, (8, N_STATE), 0)
        pr, pi_ = lbr, lbi
        fwd_r = rev_r = jnp.broadcast_to(pr, (8, N_STATE))
        fwd_i = rev_i = jnp.broadcast_to(pi_, (8, N_STATE))
        for j in range(1, 8):
            pr, pi_ = _cmul(pr, pi_, lbr, lbi)
            fwd_r = jnp.where(rid == j, jnp.broadcast_to(pr, (8, N_STATE)), fwd_r)
            fwd_i = jnp.where(rid == j, jnp.broadcast_to(pi_, (8, N_STATE)), fwd_i)
            rev_r = jnp.where(rid == 7 - j, jnp.broadcast_to(pr, (8, N_STATE)), rev_r)
            rev_i = jnp.where(rid == 7 - j, jnp.broadcast_to(pi_, (8, N_STATE)), rev_i)
        o_pr[...] = fwd_r
        o_pi[...] = fwd_i
        o_qr[...] = rev_r
        o_qi[...] = -rev_i

    return pl.pallas_call(
        body, name="s5_param_fwd",
        out_shape=[_sds((S5_GROUP, N_STATE))] * 2 + [_sds((8, N_STATE))] * 4,
    )(lam_re, lam_im, logdt, b_re, b_im)


def _s5_param_bwd(lam_re, lam_im, logdt, b_re, b_im, d_lbr, d_lbi, d_bbr, d_bbi, group_ind):
    def body(lr, li, ld, br, bi, g0, g1, g2, g3, ind, o_lr, o_li, o_ld, o_br, o_bi):
        _, vjp = jax.vjp(_s5_param_fn, lr[...], li[...], ld[...], br[...], bi[...])
        d_lr, d_li, d_ld, d_br, d_bi = vjp((g0[...], g1[...], g2[...], g3[...]))
        o_lr[...] = d_lr
        o_li[...] = d_li
        o_ld[...] = _dot(jnp.broadcast_to(d_ld, (8, N_STATE)), ind[...], ((1,), (0,)), HI)
        o_br[...] = d_br
        o_bi[...] = d_bi

    return pl.pallas_call(
        body, name="s5_param_bwd",
        out_shape=[_sds((1, N_STATE))] * 2 + [_sds((8, LANES))] + [_sds((S5_GROUP, N_STATE))] * 2,
    )(lam_re, lam_im, logdt, b_re, b_im, d_lbr, d_lbi, d_bbr, d_bbi, group_ind)


def _fwd_in(x, norm_g, w_in_bf, tt):
    L = x.shape[0]

    def body(x_ref, g_ref, w_ref, u_ref, zs_ref, rw_ref, zr_ref):
        h = _rms(x_ref[...], g_ref[...])
        proj = jnp.dot(h.astype(BF16), w_ref[...], preferred_element_type=F32)
        u_ref[...] = proj[:, 0:D_S5]
        zs_ref[...] = proj[:, D_S5:2 * D_S5]
        rw_ref[...] = proj[:, 2 * D_S5:2 * D_S5 + D_SHIFT]
        zr_ref[...] = proj[:, 2 * D_S5 + D_SHIFT:D_IN]

    row = lambda n: pl.BlockSpec((tt, n), lambda i: (i, 0))
    return pl.pallas_call(
        body, name="fwd_in", grid=(L // tt,),
        in_specs=[row(D_MODEL), _const_spec((1, D_MODEL)), _const_spec((D_MODEL, D_IN))],
        out_specs=[row(D_S5), row(D_S5), row(D_SHIFT), row(D_RWKV)],
        out_shape=[_sds((L, D_S5)), _sds((L, D_S5)), _sds((L, D_SHIFT)), _sds((L, D_RWKV))],
        compiler_params=_params("parallel"),
    )(x, norm_g, w_in_bf)


S5_LANE_CHUNK = 512


def _tile_scan(re_ref, im_ref, pow_r_ref, pow_i_ref, carry_r_ref, carry_i_ref, reverse):
    t, n = re_ref.shape
    n_groups = t // 8
    ch = S5_LANE_CHUNK
    rid = lax.broadcasted_iota(jnp.int32, (8, ch), 0)
    for c in range(n // ch):
        cols = slice(c * ch, (c + 1) * ch)
        pow_r = pow_r_ref[:, cols]
        pow_i = pow_i_ref[:, cols]
        row = lambda tile, j: jnp.broadcast_to(tile[j:j + 1], (8, ch))
        levels = [(d, row(pow_r, 8 - d if reverse else d - 1), row(pow_i, 8 - d if reverse else d - 1))
                  for d in (1, 2, 4)]

        def group(g, carry):
            r0 = pl.multiple_of(((n_groups - 1 - g) if reverse else g) * 8, 8)
            xr = re_ref[pl.ds(r0, 8), cols]
            xi = im_ref[pl.ds(r0, 8), cols]
            for d, lr, li in levels:
                keep = (rid < 8 - d) if reverse else (rid >= d)
                shift = (8 - d) if reverse else d
                sr = jnp.where(keep, pltpu.roll(xr, shift, axis=0), 0.0)
                si = jnp.where(keep, pltpu.roll(xi, shift, axis=0), 0.0)
                mr, mi = _cmul(lr, li, sr, si)
                xr = xr + mr
                xi = xi + mi
            mr, mi = _cmul(pow_r, pow_i, carry[0], carry[1])
            xr = xr + mr
            xi = xi + mi
            re_ref[pl.ds(r0, 8), cols] = xr
            im_ref[pl.ds(r0, 8), cols] = xi
            last = 0 if reverse else 7
            return row(xr, last), row(xi, last)

        out = lax.fori_loop(0, n_groups, group, (carry_r_ref[:, cols], carry_i_ref[:, cols]))
        carry_r_ref[:, cols] = out[0]
        carry_i_ref[:, cols] = out[1]


def _s5_fwd(u, b4_re, b4_im, c4_re, c4_im, pow_r, pow_i, tt):
    L = u.shape[0]

    def body(u_ref, bre_ref, bim_ref, cre_ref, cim_ref, pr_ref, pi_ref, sre_o, sim_o, y_o, car_r, car_i):
        @pl.when(pl.program_id(0) == 0)
        def _():
            car_r[...] = jnp.zeros_like(car_r)
            car_i[...] = jnp.zeros_like(car_i)

        uv = u_ref[...]
        for q in range(S5_BLOCKS):
            uq = uv[:, q * LANES:(q + 1) * LANES]
            cols = slice(q * 512, (q + 1) * 512)
            sre_o[:, cols] = _dot_bf(uq, bre_ref[q], ((1,), (0,)))
            sim_o[:, cols] = _dot_bf(uq, bim_ref[q], ((1,), (0,)))
        _tile_scan(sre_o, sim_o, pr_ref, pi_ref, car_r, car_i, reverse=False)
        for q in range(S5_BLOCKS):
            cols = slice(q * 512, (q + 1) * 512)
            y_o[:, q * LANES:(q + 1) * LANES] = (_dot_bf(sre_o[:, cols], cre_ref[q], ((1,), (0,)))
                                                 - _dot_bf(sim_o[:, cols], cim_ref[q], ((1,), (0,))))

    row = lambda n: pl.BlockSpec((tt, n), lambda i: (i, 0))
    return pl.pallas_call(
        body, name="s5_fwd", grid=(L // tt,),
        in_specs=[row(D_S5)] + [_const_spec((S5_BLOCKS, LANES, 512))] * 2 + [_const_spec((S5_BLOCKS, 512, LANES))] * 2
        + [_const_spec((8, N_STATE))] * 2,
        out_specs=[row(N_STATE), row(N_STATE), row(D_S5)],
        out_shape=[_sds((L, N_STATE)), _sds((L, N_STATE)), _sds((L, D_S5))],
        scratch_shapes=[pltpu.VMEM((8, N_STATE), F32)] * 2,
        compiler_params=_params("arbitrary"),
    )(u, b4_re, b4_im, c4_re, c4_im, pow_r, pow_i)


def _rwkv_pre_fn(r, k, v, wa, w0, w2p, a0, a2p, k_k, k_a, ee):
    w = -_softplus(-(w0 + mm_bf(jnp.tanh(wa), w2p))) - 0.5
    logw = -jnp.exp(w)
    a = _sigmoid(a0 + mm_bf(wa, a2p))
    kkp = k * k_k
    kk = kkp / jnp.maximum(jnp.sqrt(head_sum(kkp * kkp, ee)), 1e-12)
    k2 = k * (1.0 + (a - 1.0) * k_a)
    return r, logw, k2, v, -kk, kk * a


def _head_spec(tt):
    return pl.BlockSpec((N_HEADS, tt, HEAD), lambda i: (0, i, 0))


def _load_heads(ref):
    return jnp.concatenate([ref[h] for h in range(N_HEADS)], axis=-1)


def _store_heads(ref, val):
    for h in range(N_HEADS):
        ref[h] = val[:, h * HEAD:(h + 1) * HEAD]


def _shifted(rw, prev_blk, first):
    rolled = pltpu.roll(rw, 1, axis=0)
    prev_row = jnp.where(first, 0.0, prev_blk[7:8, :])
    rid = lax.broadcasted_iota(jnp.int32, rw.shape, 0)
    return jnp.where(rid == 0, jnp.broadcast_to(prev_row, rw.shape), rolled)


def _split_rw(t):
    return t[:, 0:512], t[:, 512:1024], t[:, 1024:1536], t[:, 1536:1664]


def _rwkv_pre_specs(tt):
    row = pl.BlockSpec((tt, D_SHIFT), lambda i: (i, 0))
    prev = pl.BlockSpec((8, D_SHIFT), lambda i: (jnp.maximum(i * (tt // 8) - 1, 0), 0))
    consts = [_const_spec((1, D_SHIFT)), _const_spec((1, D_RWKV)), _const_spec((LANES, D_RWKV)),
              _const_spec((1, D_RWKV)), _const_spec((LANES, D_RWKV)), _const_spec((1, D_RWKV)),
              _const_spec((1, D_RWKV)), _const_spec((D_RWKV, D_RWKV))]
    return [row, prev] + consts


def _rwkv_pre_fwd(rw, mu, w0, w2p, a0, a2p, k_k, k_a, ee, tt):
    L = rw.shape[0]

    def body(rw_ref, prev_ref, mu_ref, w0_ref, w2_ref, a0_ref, a2_ref, kk_ref, ka_ref, ee_ref, *outs):
        rwv = rw_ref[...]
        rws = rwv + (_shifted(rwv, prev_ref[...], pl.program_id(0) == 0) - rwv) * mu_ref[...]
        res = _rwkv_pre_fn(*_split_rw(rws), w0_ref[...], w2_ref[...], a0_ref[...], a2_ref[...],
                           kk_ref[...], ka_ref[...], ee_ref[...])
        for o, val in zip(outs, res):
            _store_heads(o, val)

    return pl.pallas_call(
        body, name="rwkv_pre_fwd", grid=(L // tt,),
        in_specs=_rwkv_pre_specs(tt), out_specs=[_head_spec(tt)] * 6, out_shape=[_sds((N_HEADS, L, HEAD))] * 6,
        compiler_params=_params("parallel"),
    )(rw, rw, mu, w0, w2p, a0, a2p, k_k, k_a, ee)


def _rwkv_pre_bwd(rw, mu, w0, w2p, a0, a2p, k_k, k_a, ee, cots, tt):
    L = rw.shape[0]
    n_t = L // tt

    def body(rw_ref, prev_ref, mu_ref, w0_ref, w2_ref, a0_ref, a2_ref, kk_ref, ka_ref, ee_ref,
             c_r, c_w, c_k, c_v, c_a, c_b, cb_r, cb_k, cb_v,
             drws_ref, dmu_o, dw0_o, dw2_o, da0_o, da2_o, dkk_o, dka_o,
             dmu, dw0, dw2, da0, da2, dkk, dka):
        i = pl.program_id(0)
        accs = (dmu, dw0, dw2, da0, da2, dkk, dka)

        @pl.when(i == 0)
        def _():
            for acc in accs:
                acc[...] = jnp.zeros_like(acc)

        rwv = rw_ref[...]
        diff = _shifted(rwv, prev_ref[...], i == 0) - rwv
        rws = rwv + diff * mu_ref[...]
        consts = (w0_ref[...], w2_ref[...], a0_ref[...], a2_ref[...], kk_ref[...], ka_ref[...])
        _, vjp = jax.vjp(lambda *a: _rwkv_pre_fn(*a, ee_ref[...]), *_split_rw(rws), *consts)
        scan = [_load_heads(c) for c in (c_r, c_w, c_k, c_v, c_a, c_b)]
        g = vjp((scan[0] + cb_r[...], scan[1], scan[2] + cb_k[...], scan[3] + cb_v[...], scan[4], scan[5]))
        drws = jnp.concatenate(g[0:4], axis=-1)
        drws_ref[...] = drws
        dmu[...] += jnp.sum(drws * diff, axis=0, keepdims=True)
        for acc, val in zip(accs[1:], g[4:]):
            acc[...] += val

        @pl.when(i == n_t - 1)
        def _():
            for acc, out in zip(accs, (dmu_o, dw0_o, dw2_o, da0_o, da2_o, dkk_o, dka_o)):
                out[...] = acc[...]

    row = pl.BlockSpec((tt, D_RWKV), lambda i: (i, 0))
    shapes = [(1, D_SHIFT), (1, D_RWKV), (LANES, D_RWKV), (1, D_RWKV), (LANES, D_RWKV), (1, D_RWKV), (1, D_RWKV)]
    return pl.pallas_call(
        body, name="rwkv_pre_bwd", grid=(n_t,),
        in_specs=_rwkv_pre_specs(tt) + [_head_spec(tt)] * 6 + [row] * 3,
        out_specs=[pl.BlockSpec((tt, D_SHIFT), lambda i: (i, 0))] + [_acc_spec(s) for s in shapes],
        out_shape=[_sds((L, D_SHIFT))] + [_sds(s) for s in shapes],
        scratch_shapes=[pltpu.VMEM(s, F32) for s in shapes],
        compiler_params=_params("arbitrary"),
    )(rw, rw, mu, w0, w2p, a0, a2p, k_k, k_a, ee, *cots)


def _bmm(a, b):
    return lax.dot_general(a, b, (((2,), (1,)), ((0,), (0,))), precision=HI, preferred_element_type=F32)


def _bmm_nt(a, b):
    return lax.dot_general(a, b, (((2,), (2,)), ((0,), (0,))), precision=HI, preferred_element_type=F32)


def _bmm_tn(a, b):
    return lax.dot_general(a, b, (((1,), (1,)), ((0,), (0,))), precision=HI, preferred_element_type=F32)


def _bdot_bf(a, b, lhs_dim, rhs_dim):
    return lax.dot_general(a.astype(BF16), b.astype(BF16), (((lhs_dim,), (rhs_dim,)), ((0,), (0,))),
                           preferred_element_type=F32)


@jax.custom_vjp
def _bmm_bf(a, b):
    return _bdot_bf(a, b, 2, 1)


def _bmm_bf_fwd(a, b):
    return _bmm_bf(a, b), (a, b)


def _bmm_bf_bwd(res, g):
    a, b = res
    return _bdot_bf(g, b, 2, 2), _bdot_bf(a, g, 1, 1)


_bmm_bf.defvjp(_bmm_bf_fwd, _bmm_bf_bwd)


@jax.custom_vjp
def _bmm_tn_bf(a, b):
    return _bdot_bf(a, b, 1, 1)


def _bmm_tn_bf_fwd(a, b):
    return _bmm_tn_bf(a, b), (a, b)


def _bmm_tn_bf_bwd(res, g):
    a, b = res
    return _bdot_bf(b, g, 2, 2), _bdot_bf(a, g, 2, 1)


_bmm_tn_bf.defvjp(_bmm_tn_bf_fwd, _bmm_tn_bf_bwd)


def _unit_lower_inverse(a):
    t = a.shape[-1]
    ti = lax.broadcasted_iota(jnp.int32, (t, t), 0)
    si = lax.broadcasted_iota(jnp.int32, (t, t), 1)

    def same_block(bits):
        shift = jnp.int32(bits)
        return (lax.shift_right_logical(ti, shift) == lax.shift_right_logical(si, shift))[None]

    def mm(x, y):
        return _bdot_bf(x, y, 2, 1)

    d = jnp.where(same_block(3), a, 0.0)
    inv = jnp.where(ti == si, 1.0, 0.0)[None] + d
    pw = mm(d, d)
    both = mm(jnp.concatenate([inv, pw], axis=1), pw)
    inv = inv + both[:, :t]
    inv = inv + mm(inv, both[:, t:])
    bits = 3
    while (1 << bits) < t:
        e = jnp.where(same_block(bits), 0.0, jnp.where(same_block(bits + 1), a, 0.0))
        inv = inv + mm(mm(inv, e), inv)
        bits += 1
    return inv


def _tri_mask(t):
    ri = lax.broadcasted_iota(jnp.int32, (2 * t, 2 * t), 0)
    ci = lax.broadcasted_iota(jnp.int32, (2 * t, 2 * t), 1)
    top_rows = ri < t
    diff = jnp.where(top_rows, ri, ri - t) - jnp.where(ci < t, ci, ci - t)
    return (diff >= jnp.where(top_rows, 1, 0))[None]


def _ones_tri(n_h, t):
    ti = lax.broadcasted_iota(jnp.int32, (t, t), 0)
    si = lax.broadcasted_iota(jnp.int32, (t, t), 1)
    return jnp.broadcast_to(jnp.where(ti >= si, 1.0, 0.0)[None], (n_h, t, t))


@jax.custom_vjp
def _running_sum_kept(logw, kept):
    return kept


def _running_sum_kept_bwd(shape, g):
    return _bmm_tn(_ones_tri(shape[0], shape[1]), g), jnp.zeros_like(g)


_running_sum_kept.defvjp(lambda logw, kept: (kept, logw.shape), _running_sum_kept_bwd)


@jax.custom_vjp
def _tri_products_kept(ar, bk, kept):
    return kept


def _tri_products_kept_bwd(res, g):
    ar, bk = res
    g = jnp.where(_tri_mask(ar.shape[1] // 2), g, 0.0)
    return _bmm(g, bk), _bmm_tn(g, ar), jnp.zeros_like(g)


_tri_products_kept.defvjp(lambda ar, bk, kept: (kept, (ar, bk)), _tri_products_kept_bwd)


@jax.custom_vjp
def _solve_unit_lower(a, rhs, inv, kept=None):
    return _bmm(inv, rhs) if kept is None else kept


def _solve_fwd(a, rhs, inv, kept=None):
    u = _bmm(inv, rhs) if kept is None else kept
    return u, (inv, u, kept is not None)


def _solve_bwd(res, du):
    inv, u, had_kept = res
    d_rhs = _bmm_tn(inv, du)
    return _bmm_nt(d_rhs, u), d_rhs, jnp.zeros_like(inv), (jnp.zeros_like(u) if had_kept else None)


_solve_unit_lower.defvjp(_solve_fwd, _solve_bwd)


def _rwkv_chunk(st0, r, logw, k, v, a, b, kept=None):
    n_h, t, _ = r.shape
    log_p = _bmm(_ones_tri(n_h, t), logw) if kept is None else _running_sum_kept(logw, kept[0])
    p_in = jnp.exp(log_p)
    p_inv = jnp.exp(-log_p)
    at = a * jnp.exp(log_p - logw)
    rt = r * p_in
    ar = jnp.concatenate([at, rt], axis=1)
    bk = jnp.concatenate([b * p_inv, k * p_inv], axis=1)
    if kept is None:
        m = jnp.where(_tri_mask(t), _bmm_nt(ar, bk), 0.0)
        inv = _unit_lower_inverse(m[:, :t, :t])
    else:
        m = _tri_products_kept(ar, bk, kept[1])
        inv = kept[2]
    top, bottom = m[:, :t], m[:, t:]
    rhs = _bmm_bf(jnp.concatenate([at, top[:, :, t:]], axis=2), jnp.concatenate([st0, v], axis=1))
    u = _solve_unit_lower(top[:, :, :t], rhs, inv, None if kept is None else kept[3])
    y = _bmm_bf(jnp.concatenate([rt, bottom], axis=2), jnp.concatenate([st0, u, v], axis=1))
    p_end = jnp.swapaxes(p_in[:, t - 1:t, :], 1, 2)
    st1 = (st0 + _bmm_tn_bf(bk, jnp.concatenate([u, v], axis=1))) * p_end
    return y, st1, (log_p, m, inv, u)


def _rwkv_scan_fwd(ops):
    n_h, L, n = ops[0].shape
    t = RWKV_CHUNK
    per = min(RWKV_CHUNKS_PER_STEP, L // t)
    n_c = L // t
    n_s = n_c // per

    def body(r_ref, w_ref, k_ref, v_ref, a_ref, b_ref, y_ref, st_ref, logp_ref, m_ref, inv_ref, u_ref, st):
        @pl.when(pl.program_id(0) == 0)
        def _():
            st[...] = jnp.zeros_like(st)

        st0 = st[...]
        for j in range(per):
            rows = slice(j * t, (j + 1) * t)
            st_ref[j] = st0
            y, st0, (log_p, m, inv, u) = _rwkv_chunk(
                st0, *(ref[:, rows, :] for ref in (r_ref, w_ref, k_ref, v_ref, a_ref, b_ref)))
            y_ref[:, rows, :] = y
            logp_ref[:, rows, :] = log_p
            u_ref[:, rows, :] = u
            m_ref[j] = m
            inv_ref[j] = inv
        st[...] = st0

    blk = pl.BlockSpec((n_h, per * t, n), lambda c: (0, c, 0))
    per_chunk = lambda m: pl.BlockSpec((per, n_h, m, m), lambda c: (c, 0, 0, 0))
    return pl.pallas_call(
        body, name="rwkv_scan_fwd", grid=(n_s,), in_specs=[blk] * 6,
        out_specs=[blk, per_chunk(n), blk, per_chunk(2 * t), per_chunk(t), blk],
        out_shape=[_sds((n_h, L, n)), _sds((n_c, n_h, n, n)), _sds((n_h, L, n)), _sds((n_c, n_h, 2 * t, 2 * t)),
                   _sds((n_c, n_h, t, t)), _sds((n_h, L, n))],
        scratch_shapes=[pltpu.VMEM((n_h, n, n), F32)],
        compiler_params=_params("arbitrary"),
    )(*ops)


def _rwkv_scan_bwd(ops, states, kept, dy):
    n_h, L, n = ops[0].shape
    t = RWKV_CHUNK
    per = min(RWKV_CHUNKS_PER_STEP, L // t)
    n_s = L // t // per

    def body(r_ref, w_ref, k_ref, v_ref, a_ref, b_ref, st_ref, logp_ref, m_ref, inv_ref, u_ref, dy_ref,
             dr, dw, dk, dv, da, db, dst):
        @pl.when(pl.program_id(0) == 0)
        def _():
            dst[...] = jnp.zeros_like(dst)

        vjps = []
        for j in range(per):
            rows = slice(j * t, (j + 1) * t)
            have = (logp_ref[:, rows, :], m_ref[j], inv_ref[j], u_ref[:, rows, :])
            args = [ref[:, rows, :] for ref in (r_ref, w_ref, k_ref, v_ref, a_ref, b_ref)]
            vjps.append(jax.vjp(lambda *a, have=have: _rwkv_chunk(*a, kept=have)[:2], st_ref[j], *args)[1])
        d_state = dst[...]
        for j in reversed(range(per)):
            rows = slice(j * t, (j + 1) * t)
            g = vjps[j]((dy_ref[:, rows, :], d_state))
            d_state = g[0]
            for out, val in zip((dr, dw, dk, dv, da, db), g[1:]):
                out[:, rows, :] = val
        dst[...] = d_state

    blk = pl.BlockSpec((n_h, per * t, n), lambda c: (0, n_s - 1 - c, 0))
    per_chunk = lambda m: pl.BlockSpec((per, n_h, m, m), lambda c: (n_s - 1 - c, 0, 0, 0))
    return pl.pallas_call(
        body, name="rwkv_scan_bwd", grid=(n_s,),
        in_specs=[blk] * 6 + [per_chunk(n), blk, per_chunk(2 * t), per_chunk(t), blk, blk],
        out_specs=[blk] * 6, out_shape=[_sds((n_h, L, n))] * 6,
        scratch_shapes=[pltpu.VMEM((n_h, n, n), F32)],
        compiler_params=_params("arbitrary"),
    )(*ops, states, *kept, dy)


def _post_fn(x, u, zs, zr, ysc, r, k2, v, y_ssm, d, glu_w, glu_b, ln_w, ln_b, r_k,
             wo_s5, wo_rwkv, gf, tgt, ee):
    y3 = _gelu(y_ssm + d * u)
    y_s5 = y3 * _sigmoid(mm_bf(y3, glu_w) + glu_b) * _silu(zs)
    mean = head_sum_split(ysc, ee) * (1.0 / HEAD)
    yc = ysc - mean
    var = head_sum(yc * yc, ee) * (1.0 / HEAD)
    gn = yc * lax.rsqrt(var + GN_EPS) * ln_w + ln_b
    bonus = head_sum(r * k2 * r_k, ee) * v
    y_rwkv = (gn + bonus) * _silu(zr)
    x2 = x + mm_bf(y_s5, wo_s5) + mm_bf(y_rwkv, wo_rwkv)
    err = _rms(x2, gf) - tgt
    return 0.5 * jnp.mean(err * err, axis=-1, keepdims=True)


def _post(x, u, zs, zr, ysc, r, k2, v, y_ssm, d, glu_w, glu_b, ln_w, ln_b, r_k, w_out, gf, tgt, ee, tt):
    L = x.shape[0]
    n_t = L // tt
    acc_shapes = [(1, D_S5), (D_S5, D_S5), (1, D_S5), (1, D_RWKV), (1, D_RWKV), (1, D_RWKV),
                  (D_MODEL, D_MODEL), (1, D_MODEL), (8, LANES)]

    def body(x_ref, u_ref, zs_ref, zr_ref, ysc_ref, r_ref, k2_ref, v_ref, yssm_ref,
             d_ref, gw_ref, gb_ref, lw_ref, lb_ref, rk_ref, wo_ref, gf_ref, tgt_ref, ee_ref,
             dx_o, du_o, dzs_o, dzr_o, dysc_o, dr_o, dk2_o, dv_o, dyssm_o,
             dd_o, dgw_o, dgb_o, dlw_o, dlb_o, drk_o, dwo_o, dgf_o, loss_o,
             dd, dgw, dgb, dlw, dlb, drk, dwo, dgf, loss):
        i = pl.program_id(0)
        accs = (dd, dgw, dgb, dlw, dlb, drk, dwo, dgf, loss)

        @pl.when(i == 0)
        def _():
            for acc in accs:
                acc[...] = jnp.zeros_like(acc)

        args = (x_ref[...], u_ref[...], zs_ref[...], zr_ref[...],
                _load_heads(ysc_ref), _load_heads(r_ref), _load_heads(k2_ref), _load_heads(v_ref), yssm_ref[...],
                d_ref[...], gw_ref[...], gb_ref[...], lw_ref[...], lb_ref[...], rk_ref[...],
                wo_ref[0:D_S5, :], wo_ref[D_S5:D_MODEL, :], gf_ref[...])
        rows, vjp = jax.vjp(lambda *a: _post_fn(*a, tgt_ref[...], ee_ref[...]), *args)
        g = vjp(jnp.ones_like(rows))
        for out, val in zip((dx_o, du_o, dzs_o, dzr_o), g[0:4]):
            out[...] = val
        _store_heads(dysc_o, g[4])
        for out, val in zip((dr_o, dk2_o, dv_o, dyssm_o), g[5:9]):
            out[...] = val
        for acc, val in zip((dd, dgw, dgb, dlw, dlb, drk), g[9:15]):
            acc[...] += val
        dwo[0:D_S5, :] += g[15]
        dwo[D_S5:D_MODEL, :] += g[16]
        dgf[...] += g[17]
        loss[...] += jnp.broadcast_to(jnp.sum(rows, axis=0, keepdims=True), loss.shape)

        @pl.when(i == n_t - 1)
        def _():
            for acc, out in zip(accs, (dd_o, dgw_o, dgb_o, dlw_o, dlb_o, drk_o, dwo_o, dgf_o, loss_o)):
                pltpu.sync_copy(acc, out)

    row = lambda n: pl.BlockSpec((tt, n), lambda i: (i, 0))
    in_specs = ([row(D_MODEL)] + [row(512)] * 3 + [_head_spec(tt)] * 4 + [row(D_S5)]
                + [_const_spec(s) for s in [(1, D_S5), (D_S5, D_S5), (1, D_S5), (1, D_RWKV), (1, D_RWKV), (1, D_RWKV),
                                            (D_MODEL, D_MODEL), (1, D_MODEL)]]
                + [row(D_MODEL), _const_spec((D_RWKV, D_RWKV))])
    out_rows = [D_MODEL] + [512] * 3 + [None] + [512] * 4
    return pl.pallas_call(
        body, name="post_fwd_bwd", grid=(n_t,), in_specs=in_specs,
        out_specs=[row(n) if n else _head_spec(tt) for n in out_rows] + [_ANY] * len(acc_shapes),
        out_shape=([_sds((L, n)) if n else _sds((N_HEADS, L, HEAD)) for n in out_rows]
                   + [_sds(s) for s in acc_shapes]),
        scratch_shapes=[pltpu.VMEM(s, F32) for s in acc_shapes],
        compiler_params=_params("arbitrary"),
    )(x, u, zs, zr, ysc, r, k2, v, y_ssm, d, glu_w, glu_b, ln_w, ln_b, r_k, w_out, gf, tgt, ee)


def _s5_bwd(u, du_direct, dy, s_re, s_im, b4_re, b4_im, c4_re, c4_im, pow_r, pow_i, tt):
    L = u.shape[0]
    n_t = L // tt
    acc_shapes = ([(S5_BLOCKS, LANES, 512)] * 2 + [(S5_BLOCKS, 512, LANES)] * 2 + [(1, N_STATE)] * 2)

    def body(u_ref, dud_ref, dy_ref, sre_ref, sim_ref, pre_ref, pim_ref, bre_ref, bim_ref, cre_ref, cim_ref,
             pr_ref, pi_ref, du_o, dbre_o, dbim_o, dcre_o, dcim_o, dlr_o, dli_o,
             dbre, dbim, dcre, dcim, dlr, dli, gre, gim, car_r, car_i):
        i = pl.program_id(0)

        @pl.when(i == 0)
        def _():
            for acc in (dbre, dbim, dcre, dcim, dlr, dli, car_r, car_i):
                acc[...] = jnp.zeros_like(acc)

        uv = u_ref[...]
        dyv = dy_ref[...]
        blocks = [slice(q * 512, (q + 1) * 512) for q in range(S5_BLOCKS)]
        lanes = [slice(q * LANES, (q + 1) * LANES) for q in range(S5_BLOCKS)]
        for q in range(S5_BLOCKS):
            gre[:, blocks[q]] = _dot_bf(dyv[:, lanes[q]], cre_ref[q], ((1,), (1,)))
            gim[:, blocks[q]] = -_dot_bf(dyv[:, lanes[q]], cim_ref[q], ((1,), (1,)))
        _tile_scan(gre, gim, pr_ref, pi_ref, car_r, car_i, reverse=True)
        for q in range(S5_BLOCKS):
            gr = gre[:, blocks[q]]
            gi = gim[:, blocks[q]]
            sr = sre_ref[:, blocks[q]]
            si = sim_ref[:, blocks[q]]
            du_o[:, lanes[q]] = (dud_ref[:, lanes[q]] + _dot_bf(gr, bre_ref[q], ((1,), (1,)))
                                 + _dot_bf(gi, bim_ref[q], ((1,), (1,))))
            dbre[q] += _dot_bf(uv[:, lanes[q]], gr, ((0,), (0,)))
            dbim[q] += _dot_bf(uv[:, lanes[q]], gi, ((0,), (0,)))
            dcre[q] += _dot_bf(sr, dyv[:, lanes[q]], ((0,), (0,)))
            dcim[q] -= _dot_bf(si, dyv[:, lanes[q]], ((0,), (0,)))
            rid = lax.broadcasted_iota(jnp.int32, sr.shape, 0)
            first = i == n_t - 1
            prev_r = jnp.where(first, 0.0, pre_ref[7:8, blocks[q]])
            prev_i = jnp.where(first, 0.0, pim_ref[7:8, blocks[q]])
            pr = jnp.where(rid == 0, jnp.broadcast_to(prev_r, sr.shape), pltpu.roll(sr, 1, axis=0))
            pi_ = jnp.where(rid == 0, jnp.broadcast_to(prev_i, si.shape), pltpu.roll(si, 1, axis=0))
            dlr[:, blocks[q]] += jnp.sum(pr * gr + pi_ * gi, axis=0, keepdims=True)
            dli[:, blocks[q]] += jnp.sum(pr * gi - pi_ * gr, axis=0, keepdims=True)

        @pl.when(i == n_t - 1)
        def _():
            for acc, out in zip((dbre, dbim, dcre, dcim, dlr, dli), (dbre_o, dbim_o, dcre_o, dcim_o, dlr_o, dli_o)):
                out[...] = acc[...]

    row = lambda n: pl.BlockSpec((tt, n), lambda i: (n_t - 1 - i, 0))
    prev = pl.BlockSpec((8, N_STATE), lambda i: (jnp.maximum((n_t - 1 - i) * (tt // 8) - 1, 0), 0))
    return pl.pallas_call(
        body, name="s5_bwd", grid=(n_t,),
        in_specs=[row(D_S5)] * 3 + [row(N_STATE)] * 2 + [prev] * 2
        + [_const_spec((S5_BLOCKS, LANES, 512))] * 2 + [_const_spec((S5_BLOCKS, 512, LANES))] * 2
        + [_const_spec((8, N_STATE))] * 2,
        out_specs=[row(D_S5)] + [_acc_spec(s) for s in acc_shapes],
        out_shape=[_sds((L, D_S5))] + [_sds(s) for s in acc_shapes],
        scratch_shapes=[pltpu.VMEM(s, F32) for s in acc_shapes] + [pltpu.VMEM((tt, N_STATE), F32)] * 2
        + [pltpu.VMEM((8, N_STATE), F32)] * 2,
        compiler_params=_params("arbitrary"),
    )(u, du_direct, dy, s_re, s_im, s_re, s_im, b4_re, b4_im, c4_re, c4_im, pow_r, pow_i)


def _bwd_in(x, norm_g, w_in_bf, mu, dx2, du, dzs, drws, dzr, tt):
    L = x.shape[0]
    n_t = L // tt

    def body(x_ref, g_ref, w_ref, mu_ref, dx2_ref, du_ref, dzs_ref, drws_ref, nxt_ref, dzr_ref,
             gx_o, dw_o, dg_o, dproj, dw, dg):
        i = pl.program_id(0)

        @pl.when(i == 0)
        def _():
            dw[...] = jnp.zeros_like(dw)
            dg[...] = jnp.zeros_like(dg)

        drws_v = drws_ref[...]
        rid = lax.broadcasted_iota(jnp.int32, drws_v.shape, 0)
        nxt_row = jnp.where(i == n_t - 1, 0.0, nxt_ref[0:1, :])
        nxt = jnp.where(rid == tt - 1, jnp.broadcast_to(nxt_row, drws_v.shape), pltpu.roll(drws_v, tt - 1, axis=0))
        muv = mu_ref[...]
        drw = drws_v * (1.0 - muv) + nxt * muv
        dproj[:, 0:D_S5] = du_ref[...].astype(BF16)
        dproj[:, D_S5:2 * D_S5] = dzs_ref[...].astype(BF16)
        dproj[:, 2 * D_S5:2 * D_S5 + D_SHIFT] = drw.astype(BF16)
        dproj[:, 2 * D_S5 + D_SHIFT:D_IN] = dzr_ref[...].astype(BF16)
        dh = _dot(dproj[...], w_ref[...], ((1,), (1,)), None)
        h, vjp = jax.vjp(_rms, x_ref[...], g_ref[...])
        dxh, dgv = vjp(dh)
        gx_o[...] = dx2_ref[...] + dxh
        dg[...] += dgv
        dw[...] += _dot(h.astype(BF16), dproj[...], ((0,), (0,)), None)

        @pl.when(i == n_t - 1)
        def _():
            dg_o[...] = dg[...]
            pltpu.sync_copy(dw, dw_o)

    row = lambda n: pl.BlockSpec((tt, n), lambda i: (i, 0))
    nxt = pl.BlockSpec((8, D_SHIFT), lambda i: (jnp.minimum((i + 1) * (tt // 8), L // 8 - 1), 0))
    return pl.pallas_call(
        body, name="bwd_in", grid=(n_t,),
        in_specs=[row(D_MODEL), _const_spec((1, D_MODEL)), _const_spec((D_MODEL, D_IN)), _const_spec((1, D_SHIFT)),
                  row(D_MODEL), row(D_S5), row(D_S5), row(D_SHIFT), nxt, row(D_RWKV)],
        out_specs=[row(D_MODEL), _ANY, _acc_spec((1, D_MODEL))],
        out_shape=[_sds((L, D_MODEL)), _sds((D_MODEL, D_IN)), _sds((1, D_MODEL))],
        scratch_shapes=[pltpu.VMEM((tt, D_IN), BF16), pltpu.VMEM((D_MODEL, D_IN), F32), pltpu.VMEM((1, D_MODEL), F32)],
        compiler_params=_params("arbitrary"),
    )(x, norm_g, w_in_bf, mu, dx2, du, dzs, drws, drws, dzr)


def _block_diag_b(bbar):
    bb = bbar.reshape(S5_GROUP, S5_BLOCKS, 8, S5_STATE)
    return jnp.einsum('hqgp,Gg->qGhgp', bb, jnp.eye(8, dtype=F32)).reshape(S5_BLOCKS, LANES, 512)


def _block_diag_b_t(db4):
    d = db4.reshape(S5_BLOCKS, 8, S5_GROUP, 8, S5_STATE)
    return jnp.einsum('qGhgp,Gg->hqgp', d, jnp.eye(8, dtype=F32)).reshape(S5_GROUP, N_STATE)


def _block_diag_c(c):
    cc = c.reshape(S5_BLOCKS, 8, S5_GROUP, S5_STATE)
    return jnp.einsum('qghp,gG->qgpGh', cc, jnp.eye(8, dtype=F32)).reshape(S5_BLOCKS, 512, LANES)


def _block_diag_c_t(dc4):
    d = dc4.reshape(S5_BLOCKS, 8, S5_STATE, 8, S5_GROUP)
    return jnp.einsum('qgpGh,gG->qghp', d, jnp.eye(8, dtype=F32)).reshape(S5_GROUPS, S5_GROUP, S5_STATE)


def _local_step(x, tgt, w):
    L = x.shape[0]
    tt = min(512, L)
    tp = min(256, L)
    ee = _head_sum_matrix()

    lam_re = w['s5_lam_re'].reshape(1, N_STATE)
    lam_im = w['s5_lam_im'].reshape(1, N_STATE)
    logdt = jnp.repeat(w['s5_log_dt'], S5_STATE).reshape(1, N_STATE)
    b_re_t = w['s5_b_re'].transpose(2, 0, 1).reshape(S5_GROUP, N_STATE)
    b_im_t = w['s5_b_im'].transpose(2, 0, 1).reshape(S5_GROUP, N_STATE)
    bbr, bbi, pow_r, pow_i, rpow_r, rpow_i = _s5_param_fwd(lam_re, lam_im, logdt, b_re_t, b_im_t)
    b4_re, b4_im = _block_diag_b(bbr), _block_diag_b(bbi)
    c4_re, c4_im = _block_diag_c(w['s5_c_re']), _block_diag_c(w['s5_c_im'])

    norm_g = w['norm_g'].reshape(1, D_MODEL)
    w_in_bf = w['w_in'].astype(BF16)
    u, zs, rw, zr = _fwd_in(x, norm_g, w_in_bf, tt)
    s_re, s_im, y_ssm = _s5_fwd(u, b4_re, b4_im, c4_re, c4_im, pow_r, pow_i, tt)

    row = lambda t: t.reshape(1, -1)
    zpad = jnp.zeros((HEAD, D_RWKV), F32)
    w2p = jnp.concatenate([w['rwkv_w2'], zpad], axis=0)
    a2p = jnp.concatenate([zpad, w['rwkv_a2']], axis=0)
    pre_consts = (row(w['rwkv_mu']), row(w['rwkv_w0']), w2p, row(w['rwkv_a0']), a2p,
                  row(w['rwkv_k_k']), row(w['rwkv_k_a']), ee)
    ops = _rwkv_pre_fwd(rw, *pre_consts, tt)
    ysc, states, *kept = _rwkv_scan_fwd(ops)

    post = _post(x, u, zs, zr, ysc, ops[0], ops[2], ops[3], y_ssm,
                 row(w['s5_d']), w['s5_glu_w'], row(w['s5_glu_b']), row(w['rwkv_ln_w']), row(w['rwkv_ln_b']),
                 row(w['rwkv_r_k']), w['w_out'], row(w['final_g']), tgt, ee, tp)
    (dx2, du_d, dzs, dzr, dysc, dr_b, dk2_b, dv_b, dy_ssm,
     dd, dglu_w, dglu_b, dln_w, dln_b, dr_k, dw_out, dgf, loss) = post

    du, db4_re, db4_im, dc4_re, dc4_im, dlbr, dlbi = _s5_bwd(
        u, du_d, dy_ssm, s_re, s_im, b4_re, b4_im, c4_re, c4_im, rpow_r, rpow_i, tt)
    group_ind = (jnp.arange(N_STATE)[:, None] // S5_STATE == jnp.arange(LANES)[None, :]).astype(F32)
    dlam_re, dlam_im, dlogdt, db_re_t, db_im_t = _s5_param_bwd(
        lam_re, lam_im, logdt, b_re_t, b_im_t, dlbr, dlbi, _block_diag_b_t(db4_re), _block_diag_b_t(db4_im), group_ind)

    cots = list(_rwkv_scan_bwd(ops, states, kept, dysc)) + [dr_b, dk2_b, dv_b]
    drws, dmu, dw0, dw2p, da0, da2p, dk_k, dk_a = _rwkv_pre_bwd(rw, *pre_consts, cots, tt)

    grad_x, dw_in, dnorm_g = _bwd_in(x, norm_g, w_in_bf, row(w['rwkv_mu']), dx2, du, dzs, drws, dzr, tt)

    unb = lambda t: t.reshape(S5_GROUP, S5_GROUPS, S5_STATE).transpose(1, 2, 0)
    grads = {
        'norm_g': dnorm_g.reshape(D_MODEL), 'w_in': dw_in,
        's5_lam_re': dlam_re.reshape(S5_GROUPS, S5_STATE), 's5_lam_im': dlam_im.reshape(S5_GROUPS, S5_STATE),
        's5_log_dt': dlogdt[0, :S5_GROUPS], 's5_b_re': unb(db_re_t), 's5_b_im': unb(db_im_t),
        's5_c_re': _block_diag_c_t(dc4_re), 's5_c_im': _block_diag_c_t(dc4_im),
        's5_d': dd.reshape(D_S5), 's5_glu_w': dglu_w, 's5_glu_b': dglu_b.reshape(D_S5),
        'rwkv_mu': dmu.reshape(-1), 'rwkv_w0': dw0.reshape(-1), 'rwkv_w2': dw2p[:HEAD], 'rwkv_a0': da0.reshape(-1),
        'rwkv_a2': da2p[HEAD:], 'rwkv_k_k': dk_k.reshape(-1), 'rwkv_k_a': dk_a.reshape(-1),
        'rwkv_r_k': dr_k.reshape(N_HEADS, HEAD), 'rwkv_ln_w': dln_w.reshape(-1), 'rwkv_ln_b': dln_b.reshape(-1),
        'w_out': dw_out, 'final_g': dgf.reshape(D_MODEL),
    }
    return loss, grad_x, grads


def _exchange(arrays, gather, axes, name):
    n = len(arrays)
    group = 2 ** len(axes)

    def body(*refs):
        send_refs, recv_refs = refs[:n], refs[n:2 * n]
        send_sems, recv_sems, local_sems = refs[2 * n:]
        pos = {ax: lax.axis_index(ax) for ax in ("x", "y", "c")}

        def index_of(p):
            idx = 0
            for ax in axes:
                idx = 2 * idx + p[ax]
            return idx

        me = index_of(pos)
        own, outs, arrivals = [], [], []
        for i, (send_ref, recv_ref) in enumerate(zip(send_refs, recv_refs)):
            def block_for(dev, send_ref=send_ref, whole=gather[i]):
                return send_ref if whole else send_ref.at[dev]

            own.append(pltpu.make_async_copy(block_for(me), recv_ref.at[me], local_sems.at[i]))
            own[-1].start()
            for k in range(1, group):
                peer = dict(pos)
                for bit, ax in enumerate(axes):
                    if (k >> bit) & 1:
                        peer[ax] = 1 - pos[ax]
                peer_idx = index_of(peer)
                sems = dict(send_sem=send_sems.at[i, k - 1], recv_sem=recv_sems.at[i, k - 1],
                            device_id=(peer["x"], peer["y"], peer["c"]), device_id_type=pl.DeviceIdType.MESH)
                outs.append(pltpu.make_async_remote_copy(src_ref=block_for(peer_idx), dst_ref=recv_ref.at[me], **sems))
                outs[-1].start()
                arrivals.append(
                    pltpu.make_async_remote_copy(src_ref=block_for(peer_idx), dst_ref=recv_ref.at[peer_idx], **sems))
        for copy in arrivals:
            copy.wait_recv()
        for copy in outs:
            copy.wait_send()
        for copy in own:
            copy.wait()

    return pl.pallas_call(
        body, name=name, in_specs=[_ANY] * n, out_specs=[_ANY] * n,
        out_shape=[jax.ShapeDtypeStruct(((group,) + a.shape) if whole else a.shape, a.dtype)
                   for a, whole in zip(arrays, gather)],
        scratch_shapes=[pltpu.SemaphoreType.DMA((n, group - 1)), pltpu.SemaphoreType.DMA((n, group - 1)),
                        pltpu.SemaphoreType.DMA((n,))],
        compiler_params=pltpu.CompilerParams(has_side_effects=True),
    )(*arrays)


def _sum_devices(ref):
    g = ref[0].astype(F32)
    for s in range(1, ref.shape[0]):
        g = g + ref[s].astype(F32)
    return g


def _adamw_math(g, w, m, v):
    m_new = ADAM_B1 * m + (1.0 - ADAM_B1) * g
    v_new = ADAM_B2 * v + (1.0 - ADAM_B2) * (g * g)
    m_hat = m_new / (1.0 - ADAM_B1 ** ADAM_STEP)
    v_hat = v_new / (1.0 - ADAM_B2 ** ADAM_STEP)
    return -ADAM_LR * (m_hat / (jnp.sqrt(v_hat) + ADAM_EPS) + ADAM_WD * w), m_new, v_new


def _adamw(gs, ws, ms, vs, reduce, name):
    n = len(ws)

    def body(*refs):
        g_refs, w_refs, m_refs, v_refs = (refs[j * n:(j + 1) * n] for j in range(4))
        outs = refs[4 * n:]
        for i in range(n):
            g = _sum_devices(g_refs[i]) if reduce else g_refs[i][...]
            res = _adamw_math(g, w_refs[i][...], m_refs[i][...], v_refs[i][...])
            for j, val in enumerate(((g,) if reduce else ()) + res):
                outs[j * n + i][...] = val

    return pl.pallas_call(
        body, name=name, out_shape=[_sds(w.shape) for w in ws] * (4 if reduce else 3),
        compiler_params=pltpu.CompilerParams(vmem_limit_bytes=VMEM_LIMIT),
    )(*gs, *ws, *ms, *vs)


def _sum_blocks(recv):
    def body(recv_ref, out_ref):
        out_ref[...] = _sum_devices(recv_ref)

    return pl.pallas_call(body, name="sum_small_grads", out_shape=_sds(recv.shape[1:]))(recv)


_WEIGHTS = [
    ('norm_g', (1, 1024), False), ('w_in', (1, 1024, 400), True), ('s5_lam_re', (1, 32, 64), False),
    ('s5_lam_im', (1, 32, 64), False), ('s5_log_dt', (1, 32), False), ('s5_b_re', (1, 32, 64, 16), False),
    ('s5_b_im', (1, 32, 64, 16), False), ('s5_c_re', (1, 32, 16, 64), False), ('s5_c_im', (1, 32, 16, 64), False),
    ('s5_d', (1, 512), False), ('s5_glu_w', (1, 64, 512), True), ('s5_glu_b', (1, 512), False),
    ('rwkv_mu', (1, 1664), False), ('rwkv_w0', (1, 512), False), ('rwkv_w2', (1, 64, 64), True),
    ('rwkv_a0', (1, 512), False), ('rwkv_a2', (1, 64, 64), True), ('rwkv_k_k', (1, 512), False),
    ('rwkv_k_a', (1, 512), False), ('rwkv_r_k', (1, 8, 64), False), ('rwkv_ln_w', (1, 512), False),
    ('rwkv_ln_b', (1, 512), False), ('w_out', (1, 128, 1024), True), ('final_g', (1024,), False),
]
_SHARDED = [(n, s) for n, s, sharded in _WEIGHTS if sharded]
_SMALL = [(n, s) for n, s, sharded in _WEIGHTS if not sharded]
_COLUMN_SHARDED = ('w_in', 'rwkv_w2', 'rwkv_a2')
_SMALL_SIZE = sum(math.prod(s) for _, s in _SMALL) + 1
_SMALL_ROWS = -(-_SMALL_SIZE // (8 * LANES)) * 8


def _pack_small(grads, loss):
    flat = [grads[n].reshape(-1) for n, _ in _SMALL] + [loss.reshape(1)]
    pad = _SMALL_ROWS * LANES - _SMALL_SIZE
    return jnp.concatenate(flat + [jnp.zeros((pad,), F32)]).reshape(_SMALL_ROWS, LANES)


def _unpack_small(packed):
    flat = packed.reshape(-1)
    out, off = {}, 0
    for n, s in _SMALL:
        size = math.prod(s)
        out[n] = flat[off:off + size].reshape(s)
        off += size
    return out, flat[off]


_BF16_OPERANDS = ('w_in', 's5_glu_w', 'w_out')


def _join_shards(name, blocks):
    _, rows, cols = blocks.shape
    if name in _COLUMN_SHARDED:
        return blocks.transpose(1, 0, 2).reshape(rows, N_DEV * cols)
    return blocks.reshape(N_DEV * rows, cols)


def _split_shards(name, full, shard_shape):
    rows, cols = shard_shape
    if name in _COLUMN_SHARDED:
        return full.reshape(rows, N_DEV, cols).transpose(1, 0, 2)
    return full.reshape(N_DEV, rows, cols)


def kernel(x, norm_g, w_in, s5_lam_re, s5_lam_im, s5_log_dt, s5_b_re, s5_b_im, s5_c_re, s5_c_im, s5_d, s5_glu_w, s5_glu_b, rwkv_mu, rwkv_w0, rwkv_w2, rwkv_a0, rwkv_a2, rwkv_k_k, rwkv_k_a, rwkv_r_k, rwkv_ln_w, rwkv_ln_b, w_out, final_g, loss_target, m_norm_g, m_w_in, m_s5_lam_re, m_s5_lam_im, m_s5_log_dt, m_s5_b_re, m_s5_b_im, m_s5_c_re, m_s5_c_im, m_s5_d, m_s5_glu_w, m_s5_glu_b, m_rwkv_mu, m_rwkv_w0, m_rwkv_w2, m_rwkv_a0, m_rwkv_a2, m_rwkv_k_k, m_rwkv_k_a, m_rwkv_r_k, m_rwkv_ln_w, m_rwkv_ln_b, m_w_out, m_final_g, v_norm_g, v_w_in, v_s5_lam_re, v_s5_lam_im, v_s5_log_dt, v_s5_b_re, v_s5_b_im, v_s5_c_re, v_s5_c_im, v_s5_d, v_s5_glu_w, v_s5_glu_b, v_rwkv_mu, v_rwkv_w0, v_rwkv_w2, v_rwkv_a0, v_rwkv_a2, v_rwkv_k_k, v_rwkv_k_a, v_rwkv_r_k, v_rwkv_ln_w, v_rwkv_ln_b, v_w_out, v_final_g):
    given = dict(locals())

    n_sh = len(_SHARDED)
    everyone = ("x", "y", "c")
    shards = [given[n][0].astype(BF16 if n in _BF16_OPERANDS else F32) for n, _ in _SHARDED]
    gathered = _exchange(shards, (True,) * n_sh, everyone, "gather_weights")
    local = {n: _join_shards(n, blocks).astype(F32 if n != 'w_in' else BF16)
             for (n, _), blocks in zip(_SHARDED, gathered)}
    local.update({n: (given[n][0] if len(s) > 1 else given[n]) for n, s in _SMALL})

    loss, grad_x, grads = _local_step(x[0], loss_target[0], local)

    blocks = [_split_shards(n, grads[n], s[1:]).astype(BF16) for n, s in _SHARDED]
    small = _pack_small(grads, loss[0, 0]).reshape(N_DEV, _SMALL_ROWS // N_DEV, LANES)
    recv = _exchange(blocks + [small], (False,) * (n_sh + 1), everyone, "exchange_grads")
    small_sum = _exchange([_sum_blocks(recv[-1])], (True,), everyone, "gather_small_grads")[0]

    result = {}
    for group, name in (([0], "adamw_w_in"), ([1, 2, 3, 4], "adamw_shards")):
        ns = [_SHARDED[i][0] for i in group]
        res = _adamw([recv[i] for i in group], [given[n][0] for n in ns], [given['m_' + n][0] for n in ns],
                     [given['v_' + n][0] for n in ns], True, name)
        for j, n in enumerate(ns):
            result[n] = [res[k * len(ns) + j][None] for k in range(4)]
    g_small, total = _unpack_small(small_sum)
    two_d = lambda t: t.reshape(1, -1) if t.ndim == 1 else t
    ns = [n for n, _ in _SMALL]
    res = _adamw([two_d(g_small[n]) for n in ns], [two_d(given[n]) for n in ns], [two_d(given['m_' + n]) for n in ns],
                 [two_d(given['v_' + n]) for n in ns], False, "adamw_small")
    for j, (n, s) in enumerate(_SMALL):
        result[n] = [g_small[n]] + [res[k * len(ns) + j].reshape(s) for k in range(3)]

    outs = [total, grad_x[None]]
    for k in range(4):
        outs += [result[n][k] for n, _, _ in _WEIGHTS]
    return tuple(outs)
```

```python
import math

import jax
import jax.numpy as jnp
from jax import lax
from jax.experimental import pallas as pl
from jax.experimental.pallas import tpu as pltpu

F32 = jnp.float32
BF16 = jnp.bfloat16
HI = lax.Precision.HIGH

D_MODEL = 1024
D_S5 = 512
D_RWKV = 512
S5_GROUPS = 32
S5_GROUP = 16
S5_STATE = 64
N_STATE = S5_GROUPS * S5_STATE
N_HEADS = 8
HEAD = 64
D_SHIFT = 3 * D_RWKV + 128
D_IN = 2 * D_S5 + D_SHIFT + D_RWKV
NORM_EPS = 1e-6
GN_EPS = 64e-5
N_DEV = 8
LANES = 128
S5_BLOCKS = 4
RWKV_CHUNK = 64
RWKV_CHUNKS_PER_STEP = 4
VMEM_LIMIT = 56 * 1024 * 1024

ADAM_LR = 0.001
ADAM_B1 = 0.9
ADAM_B2 = 0.999
ADAM_EPS = 1e-08
ADAM_WD = 0.01
ADAM_STEP = 10


def _dot(a, b, dims, prec):
    return lax.dot_general(a, b, (dims, ((), ())), precision=prec, preferred_element_type=F32)


def _dot_bf(a, b, dims):
    return _dot(a.astype(BF16), b.astype(BF16), dims, None)


def _make_mm(cast, prec):
    @jax.custom_vjp
    def mm(a, b):
        return _dot(cast(a), cast(b), ((1,), (0,)), prec)

    def fwd(a, b):
        return mm(a, b), (a, b)

    def bwd(res, g):
        a, b = res
        return (_dot(cast(g), cast(b), ((1,), (1,)), prec), _dot(cast(a), cast(g), ((0,), (0,)), prec))

    mm.defvjp(fwd, bwd)
    return mm


mm_bf = _make_mm(lambda t: t.astype(BF16), None)


def _make_head_sum(split):
    def product(x, ee):
        hi = x.astype(BF16)
        out = _dot(hi, ee, ((1,), (0,)), None)
        if split:
            out = out + _dot((x - hi.astype(F32)).astype(BF16), ee, ((1,), (0,)), None)
        return out

    @jax.custom_vjp
    def head_sum(x, ee):
        return product(x, ee)

    def fwd(x, ee):
        return product(x, ee), ee

    def bwd(ee, g):
        return product(g, ee), jnp.zeros_like(ee)

    head_sum.defvjp(fwd, bwd)
    return head_sum


head_sum = _make_head_sum(False)
head_sum_split = _make_head_sum(True)


@jax.custom_vjp
def _sigmoid(x):
    return 1.0 / (1.0 + jnp.exp(-x))


def _sigmoid_fwd(x):
    s = _sigmoid(x)
    return s, s


_sigmoid.defvjp(_sigmoid_fwd, lambda s, g: (g * s * (1.0 - s),))


@jax.custom_vjp
def _silu(x):
    return x * _sigmoid(x)


def _silu_fwd(x):
    s = _sigmoid(x)
    return x * s, (x, s)


def _silu_bwd(res, g):
    x, s = res
    return (g * s * (1.0 + x * (1.0 - s)),)


_silu.defvjp(_silu_fwd, _silu_bwd)


@jax.custom_vjp
def _softplus(x):
    return jnp.maximum(x, 0.0) + jnp.log(1.0 + jnp.exp(-jnp.abs(x)))


def _softplus_fwd(x):
    e = jnp.exp(-jnp.abs(x))
    return jnp.maximum(x, 0.0) + jnp.log(1.0 + e), (x, e)


def _softplus_bwd(res, g):
    x, e = res
    return (g * jnp.where(x >= 0.0, 1.0, e) / (1.0 + e),)


_softplus.defvjp(_softplus_fwd, _softplus_bwd)


@jax.custom_vjp
def _normalize_heads(x, ee):
    return x / jnp.maximum(jnp.sqrt(head_sum(x * x, ee)), 1e-12)


def _normalize_heads_fwd(x, ee):
    norm = jnp.sqrt(head_sum(x * x, ee))
    inv = 1.0 / jnp.maximum(norm, 1e-12)
    y = x * inv
    return y, (y, inv, norm, ee)


def _normalize_heads_bwd(res, g):
    y, inv, norm, ee = res
    along = jnp.where(norm > 1e-12, head_sum(g * y, ee), 0.0)
    return inv * (g - y * along), jnp.zeros_like(ee)


_normalize_heads.defvjp(_normalize_heads_fwd, _normalize_heads_bwd)


_GELU_C = 2.0 * math.sqrt(2.0 / math.pi)


def _gelu_gate(x):
    return 1.0 / (1.0 + jnp.exp(-_GELU_C * x * (1.0 + 0.044715 * (x * x))))


@jax.custom_vjp
def _gelu(x):
    return x * _gelu_gate(x)


def _gelu_fwd(x):
    s = _gelu_gate(x)
    return x * s, (x, s)


def _gelu_bwd(res, g):
    x, s = res
    return (g * (s + x * s * (1.0 - s) * (_GELU_C * (1.0 + 3.0 * 0.044715 * (x * x)))),)


_gelu.defvjp(_gelu_fwd, _gelu_bwd)


def _rms(x, g):
    return x * lax.rsqrt(jnp.mean(x * x, axis=-1, keepdims=True) + NORM_EPS) * g


def _const_spec(shape):
    nd = len(shape)
    return pl.BlockSpec(shape, lambda *_: (0,) * nd, pipeline_mode=pl.Buffered(1))


def _acc_spec(shape):
    nd = len(shape)
    return pl.BlockSpec(shape, lambda *_: (0,) * nd)


def _params(sem):
    return pltpu.CompilerParams(dimension_semantics=(sem,), vmem_limit_bytes=VMEM_LIMIT)


_ANY = pl.BlockSpec(memory_space=pl.ANY)


def _sds(shape):
    return jax.ShapeDtypeStruct(shape, F32)


def _head_sum_matrix():
    i = jnp.arange(D_RWKV) // HEAD
    return (i[:, None] == i[None, :]).astype(BF16)


def _s5_param_fn(lam_re, lam_im, logdt, b_re, b_im):
    dt = jnp.exp(logdt)
    mag = jnp.exp(lam_re * dt)
    ang = lam_im * dt
    lbr = mag * jnp.cos(ang)
    lbi = mag * jnp.sin(ang)
    nr = lbr - 1.0
    den = lam_re * lam_re + lam_im * lam_im
    cr = (nr * lam_re + lbi * lam_im) / den
    ci = (lbi * lam_re - nr * lam_im) / den
    return lbr, lbi, cr * b_re - ci * b_im, cr * b_im + ci * b_re


def _cmul(ar, ai, br, bi):
    return ar * br - ai * bi, ar * bi + ai * br


def _s5_param_fwd(lam_re, lam_im, logdt, b_re, b_im):
    def body(lr, li, ld, br, bi, o_br, o_bi, o_pr, o_pi, o_qr, o_qi):
        lbr, lbi, bbr, bbi = _s5_param_fn(lr[...], li[...], ld[...], br[...], bi[...])
        o_br[...] = bbr
        o_bi[...] = bbi
        rid = lax.broadcasted_iota(jnp.int32, (8, N_STATE), 0)
        pr, pi_ = lbr, lbi
        fwd_r = rev_r = jnp.broadcast_to(pr, (8, N_STATE))
        fwd_i = rev_i = jnp.broadcast_to(pi_, (8, N_STATE))
        for j in range(1, 8):
            pr, pi_ = _cmul(pr, pi_, lbr, lbi)
            fwd_r = jnp.where(rid == j, jnp.broadcast_to(pr, (8, N_STATE)), fwd_r)
            fwd_i = jnp.where(rid == j, jnp.broadcast_to(pi_, (8, N_STATE)), fwd_i)
            rev_r = jnp.where(rid == 7 - j, jnp.broadcast_to(pr, (8, N_STATE)), rev_r)
            rev_i = jnp.where(rid == 7 - j, jnp.broadcast_to(pi_, (8, N_STATE)), rev_i)
        o_pr[...] = fwd_r
        o_pi[...] = fwd_i
        o_qr[...] = rev_r
        o_qi[...] = -rev_i

    return pl.pallas_call(
        body, name="s5_param_fwd",
        out_shape=[_sds((S5_GROUP, N_STATE))] * 2 + [_sds((8, N_STATE))] * 4,
    )(lam_re, lam_im, logdt, b_re, b_im)


def _s5_param_bwd(lam_re, lam_im, logdt, b_re, b_im, d_lbr, d_lbi, d_bbr, d_bbi, group_ind):
    def body(lr, li, ld, br, bi, g0, g1, g2, g3, ind, o_lr, o_li, o_ld, o_br, o_bi):
        _, vjp = jax.vjp(_s5_param_fn, lr[...], li[...], ld[...], br[...], bi[...])
        d_lr, d_li, d_ld, d_br, d_bi = vjp((g0[...], g1[...], g2[...], g3[...]))
        o_lr[...] = d_lr
        o_li[...] = d_li
        o_ld[...] = _dot(jnp.broadcast_to(d_ld, (8, N_STATE)), ind[...], ((1,), (0,)), HI)
        o_br[...] = d_br
        o_bi[...] = d_bi

    return pl.pallas_call(
        body, name="s5_param_bwd",
        out_shape=[_sds((1, N_STATE))] * 2 + [_sds((8, LANES))] + [_sds((S5_GROUP, N_STATE))] * 2,
    )(lam_re, lam_im, logdt, b_re, b_im, d_lbr, d_lbi, d_bbr, d_bbi, group_ind)


def _fwd_in(x, norm_g, w_in_bf, tt):
    L = x.shape[0]

    def body(x_ref, g_ref, w_ref, u_ref, zs_ref, rw_ref, zr_ref):
        h = _rms(x_ref[...], g_ref[...])
        proj = jnp.dot(h.astype(BF16), w_ref[...], preferred_element_type=F32)
        u_ref[...] = proj[:, 0:D_S5]
        zs_ref[...] = proj[:, D_S5:2 * D_S5]
        rw_ref[...] = proj[:, 2 * D_S5:2 * D_S5 + D_SHIFT]
        zr_ref[...] = proj[:, 2 * D_S5 + D_SHIFT:D_IN]

    row = lambda n: pl.BlockSpec((tt, n), lambda i: (i, 0))
    return pl.pallas_call(
        body, name="fwd_in", grid=(L // tt,),
        in_specs=[row(D_MODEL), _const_spec((1, D_MODEL)), _const_spec((D_MODEL, D_IN))],
        out_specs=[row(D_S5), row(D_S5), row(D_SHIFT), row(D_RWKV)],
        out_shape=[_sds((L, D_S5)), _sds((L, D_S5)), _sds((L, D_SHIFT)), _sds((L, D_RWKV))],
        compiler_params=_params("parallel"),
    )(x, norm_g, w_in_bf)


S5_LANE_CHUNK = 512


def _tile_scan(re_ref, im_ref, pow_r_ref, pow_i_ref, carry_r_ref, carry_i_ref, reverse):
    t, n = re_ref.shape
    n_groups = t // 8
    ch = S5_LANE_CHUNK
    rid = lax.broadcasted_iota(jnp.int32, (8, ch), 0)
    for c in range(n // ch):
        cols = slice(c * ch, (c + 1) * ch)
        pow_r = pow_r_ref[:, cols]
        pow_i = pow_i_ref[:, cols]
        row = lambda tile, j: jnp.broadcast_to(tile[j:j + 1], (8, ch))
        levels = [(d, row(pow_r, 8 - d if reverse else d - 1), row(pow_i, 8 - d if reverse else d - 1))
                  for d in (1, 2, 4)]

        def group(g, carry):
            r0 = pl.multiple_of(((n_groups - 1 - g) if reverse else g) * 8, 8)
            xr = re_ref[pl.ds(r0, 8), cols]
            xi = im_ref[pl.ds(r0, 8), cols]
            for d, lr, li in levels:
                keep = (rid < 8 - d) if reverse else (rid >= d)
                shift = (8 - d) if reverse else d
                sr = jnp.where(keep, pltpu.roll(xr, shift, axis=0), 0.0)
                si = jnp.where(keep, pltpu.roll(xi, shift, axis=0), 0.0)
                mr, mi = _cmul(lr, li, sr, si)
                xr = xr + mr
                xi = xi + mi
            mr, mi = _cmul(pow_r, pow_i, carry[0], carry[1])
            xr = xr + mr
            xi = xi + mi
            re_ref[pl.ds(r0, 8), cols] = xr
            im_ref[pl.ds(r0, 8), cols] = xi
            last = 0 if reverse else 7
            return row(xr, last), row(xi, last)

        out = lax.fori_loop(0, n_groups, group, (carry_r_ref[:, cols], carry_i_ref[:, cols]))
        carry_r_ref[:, cols] = out[0]
        carry_i_ref[:, cols] = out[1]


def _s5_fwd(u, b4_re, b4_im, c4_re, c4_im, pow_r, pow_i, tt):
    L = u.shape[0]

    def body(u_ref, bre_ref, bim_ref, cre_ref, cim_ref, pr_ref, pi_ref, sre_o, sim_o, y_o, car_r, car_i):
        @pl.when(pl.program_id(0) == 0)
        def _():
            car_r[...] = jnp.zeros_like(car_r)
            car_i[...] = jnp.zeros_like(car_i)

        uv = u_ref[...]
        for q in range(S5_BLOCKS):
            uq = uv[:, q * LANES:(q + 1) * LANES]
            cols = slice(q * 512, (q + 1) * 512)
            sre_o[:, cols] = _dot_bf(uq, bre_ref[q], ((1,), (0,)))
            sim_o[:, cols] = _dot_bf(uq, bim_ref[q], ((1,), (0,)))
        _tile_scan(sre_o, sim_o, pr_ref, pi_ref, car_r, car_i, reverse=False)
        for q in range(S5_BLOCKS):
            cols = slice(q * 512, (q + 1) * 512)
            y_o[:, q * LANES:(q + 1) * LANES] = (_dot_bf(sre_o[:, cols], cre_ref[q], ((1,), (0,)))
                                                 - _dot_bf(sim_o[:, cols], cim_ref[q], ((1,), (0,))))

    row = lambda n: pl.BlockSpec((tt, n), lambda i: (i, 0))
    return pl.pallas_call(
        body, name="s5_fwd", grid=(L // tt,),
        in_specs=[row(D_S5)] + [_const_spec((S5_BLOCKS, LANES, 512))] * 2 + [_const_spec((S5_BLOCKS, 512, LANES))] * 2
        + [_const_spec((8, N_STATE))] * 2,
        out_specs=[row(N_STATE), row(N_STATE), row(D_S5)],
        out_shape=[_sds((L, N_STATE)), _sds((L, N_STATE)), _sds((L, D_S5))],
        scratch_shapes=[pltpu.VMEM((8, N_STATE), F32)] * 2,
        compiler_params=_params("arbitrary"),
    )(u, b4_re, b4_im, c4_re, c4_im, pow_r, pow_i)


def _rwkv_pre_fn(r, k, v, wa, w0, w2p, a0, a2p, k_k, k_a, ee):
    w = -_softplus(-(w0 + mm_bf(jnp.tanh(wa), w2p))) - 0.5
    logw = -jnp.exp(w)
    a = _sigmoid(a0 + mm_bf(wa, a2p))
    kk = _normalize_heads(k * k_k, ee)
    k2 = k * (1.0 + (a - 1.0) * k_a)
    return r, logw, k2, v, -kk, kk * a


def _head_spec(tt):
    return pl.BlockSpec((N_HEADS, tt, HEAD), lambda i: (0, i, 0))


def _load_heads(ref):
    return jnp.concatenate([ref[h] for h in range(N_HEADS)], axis=-1)


def _store_heads(ref, val):
    for h in range(N_HEADS):
        ref[h] = val[:, h * HEAD:(h + 1) * HEAD]


def _shifted(rw, prev_blk, first):
    rolled = pltpu.roll(rw, 1, axis=0)
    prev_row = jnp.where(first, 0.0, prev_blk[7:8, :])
    rid = lax.broadcasted_iota(jnp.int32, rw.shape, 0)
    return jnp.where(rid == 0, jnp.broadcast_to(prev_row, rw.shape), rolled)


def _split_rw(t):
    return t[:, 0:512], t[:, 512:1024], t[:, 1024:1536], t[:, 1536:1664]


def _rwkv_pre_specs(tt):
    row = pl.BlockSpec((tt, D_SHIFT), lambda i: (i, 0))
    prev = pl.BlockSpec((8, D_SHIFT), lambda i: (jnp.maximum(i * (tt // 8) - 1, 0), 0))
    consts = [_const_spec((1, D_SHIFT)), _const_spec((1, D_RWKV)), _const_spec((LANES, D_RWKV)),
              _const_spec((1, D_RWKV)), _const_spec((LANES, D_RWKV)), _const_spec((1, D_RWKV)),
              _const_spec((1, D_RWKV)), _const_spec((D_RWKV, D_RWKV))]
    return [row, prev] + consts


def _rwkv_pre_fwd(rw, mu, w0, w2p, a0, a2p, k_k, k_a, ee, tt):
    L = rw.shape[0]

    def body(rw_ref, prev_ref, mu_ref, w0_ref, w2_ref, a0_ref, a2_ref, kk_ref, ka_ref, ee_ref, *outs):
        rwv = rw_ref[...]
        rws = rwv + (_shifted(rwv, prev_ref[...], pl.program_id(0) == 0) - rwv) * mu_ref[...]
        res = _rwkv_pre_fn(*_split_rw(rws), w0_ref[...], w2_ref[...], a0_ref[...], a2_ref[...],
                           kk_ref[...], ka_ref[...], ee_ref[...])
        for o, val in zip(outs, res):
            _store_heads(o, val)

    return pl.pallas_call(
        body, name="rwkv_pre_fwd", grid=(L // tt,),
        in_specs=_rwkv_pre_specs(tt), out_specs=[_head_spec(tt)] * 6, out_shape=[_sds((N_HEADS, L, HEAD))] * 6,
        compiler_params=_params("parallel"),
    )(rw, rw, mu, w0, w2p, a0, a2p, k_k, k_a, ee)


def _rwkv_pre_bwd(rw, mu, w0, w2p, a0, a2p, k_k, k_a, ee, cots, tt):
    L = rw.shape[0]
    n_t = L // tt

    def body(rw_ref, prev_ref, mu_ref, w0_ref, w2_ref, a0_ref, a2_ref, kk_ref, ka_ref, ee_ref,
             c_r, c_w, c_k, c_v, c_a, c_b, cb_r, cb_k, cb_v,
             drws_ref, dmu_o, dw0_o, dw2_o, da0_o, da2_o, dkk_o, dka_o,
             dmu, dw0, dw2, da0, da2, dkk, dka):
        i = pl.program_id(0)
        accs = (dmu, dw0, dw2, da0, da2, dkk, dka)

        @pl.when(i == 0)
        def _():
            for acc in accs:
                acc[...] = jnp.zeros_like(acc)

        rwv = rw_ref[...]
        diff = _shifted(rwv, prev_ref[...], i == 0) - rwv
        rws = rwv + diff * mu_ref[...]
        consts = (w0_ref[...], w2_ref[...], a0_ref[...], a2_ref[...], kk_ref[...], ka_ref[...])
        _, vjp = jax.vjp(lambda *a: _rwkv_pre_fn(*a, ee_ref[...]), *_split_rw(rws), *consts)
        scan = [_load_heads(c) for c in (c_r, c_w, c_k, c_v, c_a, c_b)]
        g = vjp((scan[0] + cb_r[...], scan[1], scan[2] + cb_k[...], scan[3] + cb_v[...], scan[4], scan[5]))
        drws = jnp.concatenate(g[0:4], axis=-1)
        drws_ref[...] = drws
        dmu[...] += jnp.sum(drws * diff, axis=0, keepdims=True)
        for acc, val in zip(accs[1:], g[4:]):
            acc[...] += val

        @pl.when(i == n_t - 1)
        def _():
            for acc, out in zip(accs, (dmu_o, dw0_o, dw2_o, da0_o, da2_o, dkk_o, dka_o)):
                out[...] = acc[...]

    row = pl.BlockSpec((tt, D_RWKV), lambda i: (i, 0))
    shapes = [(1, D_SHIFT), (1, D_RWKV), (LANES, D_RWKV), (1, D_RWKV), (LANES, D_RWKV), (1, D_RWKV), (1, D_RWKV)]
    return pl.pallas_call(
        body, name="rwkv_pre_bwd", grid=(n_t,),
        in_specs=_rwkv_pre_specs(tt) + [_head_spec(tt)] * 6 + [row] * 3,
        out_specs=[pl.BlockSpec((tt, D_SHIFT), lambda i: (i, 0))] + [_acc_spec(s) for s in shapes],
        out_shape=[_sds((L, D_SHIFT))] + [_sds(s) for s in shapes],
        scratch_shapes=[pltpu.VMEM(s, F32) for s in shapes],
        compiler_params=_params("arbitrary"),
    )(rw, rw, mu, w0, w2p, a0, a2p, k_k, k_a, ee, *cots)


def _bmm(a, b):
    return lax.dot_general(a, b, (((2,), (1,)), ((0,), (0,))), precision=HI, preferred_element_type=F32)


def _bmm_nt(a, b):
    return lax.dot_general(a, b, (((2,), (2,)), ((0,), (0,))), precision=HI, preferred_element_type=F32)


def _bmm_tn(a, b):
    return lax.dot_general(a, b, (((1,), (1,)), ((0,), (0,))), precision=HI, preferred_element_type=F32)


def _bdot_bf(a, b, lhs_dim, rhs_dim):
    return lax.dot_general(a.astype(BF16), b.astype(BF16), (((lhs_dim,), (rhs_dim,)), ((0,), (0,))),
                           preferred_element_type=F32)


@jax.custom_vjp
def _bmm_bf(a, b):
    return _bdot_bf(a, b, 2, 1)


def _bmm_bf_fwd(a, b):
    return _bmm_bf(a, b), (a, b)


def _bmm_bf_bwd(res, g):
    a, b = res
    return _bdot_bf(g, b, 2, 2), _bdot_bf(a, g, 1, 1)


_bmm_bf.defvjp(_bmm_bf_fwd, _bmm_bf_bwd)


@jax.custom_vjp
def _bmm_tn_bf(a, b):
    return _bdot_bf(a, b, 1, 1)


def _bmm_tn_bf_fwd(a, b):
    return _bmm_tn_bf(a, b), (a, b)


def _bmm_tn_bf_bwd(res, g):
    a, b = res
    return _bdot_bf(b, g, 2, 2), _bdot_bf(a, g, 2, 1)


_bmm_tn_bf.defvjp(_bmm_tn_bf_fwd, _bmm_tn_bf_bwd)


def _unit_lower_inverse(a):
    t = a.shape[-1]
    ti = lax.broadcasted_iota(jnp.int32, (t, t), 0)
    si = lax.broadcasted_iota(jnp.int32, (t, t), 1)

    def same_block(bits):
        shift = jnp.int32(bits)
        return (lax.shift_right_logical(ti, shift) == lax.shift_right_logical(si, shift))[None]

    def mm(x, y):
        return _bdot_bf(x, y, 2, 1)

    d = jnp.where(same_block(3), a, 0.0)
    inv = jnp.where(ti == si, 1.0, 0.0)[None] + d
    pw = mm(d, d)
    both = mm(jnp.concatenate([inv, pw], axis=1), pw)
    inv = inv + both[:, :t]
    inv = inv + mm(inv, both[:, t:])
    bits = 3
    while (1 << bits) < t:
        e = jnp.where(same_block(bits), 0.0, jnp.where(same_block(bits + 1), a, 0.0))
        inv = inv + mm(mm(inv, e), inv)
        bits += 1
    return inv


def _tri_mask(t):
    ri = lax.broadcasted_iota(jnp.int32, (2 * t, 2 * t), 0)
    ci = lax.broadcasted_iota(jnp.int32, (2 * t, 2 * t), 1)
    top_rows = ri < t
    diff = jnp.where(top_rows, ri, ri - t) - jnp.where(ci < t, ci, ci - t)
    return (diff >= jnp.where(top_rows, 1, 0))[None]


def _ones_tri(n_h, t):
    ti = lax.broadcasted_iota(jnp.int32, (t, t), 0)
    si = lax.broadcasted_iota(jnp.int32, (t, t), 1)
    return jnp.broadcast_to(jnp.where(ti >= si, 1.0, 0.0)[None], (n_h, t, t))


@jax.custom_vjp
def _running_sum_kept(logw, kept):
    return kept


def _running_sum_kept_bwd(shape, g):
    return _bmm_tn(_ones_tri(shape[0], shape[1]), g), jnp.zeros_like(g)


_running_sum_kept.defvjp(lambda logw, kept: (kept, logw.shape), _running_sum_kept_bwd)


@jax.custom_vjp
def _tri_products_kept(ar, bk, kept):
    return kept


def _tri_products_kept_bwd(res, g):
    ar, bk = res
    g = jnp.where(_tri_mask(ar.shape[1] // 2), g, 0.0)
    return _bmm(g, bk), _bmm_tn(g, ar), jnp.zeros_like(g)


_tri_products_kept.defvjp(lambda ar, bk, kept: (kept, (ar, bk)), _tri_products_kept_bwd)


@jax.custom_vjp
def _solve_unit_lower(a, rhs, inv, kept=None):
    return _bmm(inv, rhs) if kept is None else kept


def _solve_fwd(a, rhs, inv, kept=None):
    u = _bmm(inv, rhs) if kept is None else kept
    return u, (inv, u, kept is not None)


def _solve_bwd(res, du):
    inv, u, had_kept = res
    d_rhs = _bmm_tn(inv, du)
    return _bmm_nt(d_rhs, u), d_rhs, jnp.zeros_like(inv), (jnp.zeros_like(u) if had_kept else None)


_solve_unit_lower.defvjp(_solve_fwd, _solve_bwd)


def _rwkv_chunk(st0, r, logw, k, v, a, b, kept=None):
    n_h, t, _ = r.shape
    log_p = _bmm(_ones_tri(n_h, t), logw) if kept is None else _running_sum_kept(logw, kept[0])
    p_in = jnp.exp(log_p)
    p_inv = jnp.exp(-log_p)
    at = a * jnp.exp(log_p - logw)
    rt = r * p_in
    ar = jnp.concatenate([at, rt], axis=1)
    bk = jnp.concatenate([b * p_inv, k * p_inv], axis=1)
    if kept is None:
        m = jnp.where(_tri_mask(t), _bmm_nt(ar, bk), 0.0)
        inv = _unit_lower_inverse(m[:, :t, :t])
    else:
        m = _tri_products_kept(ar, bk, kept[1])
        inv = kept[2]
    top, bottom = m[:, :t], m[:, t:]
    rhs = _bmm_bf(jnp.concatenate([at, top[:, :, t:]], axis=2), jnp.concatenate([st0, v], axis=1))
    u = _solve_unit_lower(top[:, :, :t], rhs, inv, None if kept is None else kept[3])
    y = _bmm_bf(jnp.concatenate([rt, bottom], axis=2), jnp.concatenate([st0, u, v], axis=1))
    p_end = jnp.swapaxes(p_in[:, t - 1:t, :], 1, 2)
    st1 = (st0 + _bmm_tn_bf(bk, jnp.concatenate([u, v], axis=1))) * p_end
    return y, st1, (log_p, m, inv, u)


def _rwkv_scan_fwd(ops):
    n_h, L, n = ops[0].shape
    t = RWKV_CHUNK
    per = min(RWKV_CHUNKS_PER_STEP, L // t)
    n_c = L // t
    n_s = n_c // per

    def body(r_ref, w_ref, k_ref, v_ref, a_ref, b_ref, y_ref, st_ref, logp_ref, m_ref, inv_ref, u_ref, st):
        @pl.when(pl.program_id(0) == 0)
        def _():
            st[...] = jnp.zeros_like(st)

        st0 = st[...]
        for j in range(per):
            rows = slice(j * t, (j + 1) * t)
            st_ref[j] = st0
            y, st0, (log_p, m, inv, u) = _rwkv_chunk(
                st0, *(ref[:, rows, :] for ref in (r_ref, w_ref, k_ref, v_ref, a_ref, b_ref)))
            y_ref[:, rows, :] = y
            logp_ref[:, rows, :] = log_p
            u_ref[:, rows, :] = u
            m_ref[j] = m
            inv_ref[j] = inv
        st[...] = st0

    blk = pl.BlockSpec((n_h, per * t, n), lambda c: (0, c, 0))
    per_chunk = lambda m: pl.BlockSpec((per, n_h, m, m), lambda c: (c, 0, 0, 0))
    return pl.pallas_call(
        body, name="rwkv_scan_fwd", grid=(n_s,), in_specs=[blk] * 6,
        out_specs=[blk, per_chunk(n), blk, per_chunk(2 * t), per_chunk(t), blk],
        out_shape=[_sds((n_h, L, n)), _sds((n_c, n_h, n, n)), _sds((n_h, L, n)), _sds((n_c, n_h, 2 * t, 2 * t)),
                   _sds((n_c, n_h, t, t)), _sds((n_h, L, n))],
        scratch_shapes=[pltpu.VMEM((n_h, n, n), F32)],
        compiler_params=_params("arbitrary"),
    )(*ops)


def _rwkv_scan_bwd(ops, states, kept, dy):
    n_h, L, n = ops[0].shape
    t = RWKV_CHUNK
    per = min(RWKV_CHUNKS_PER_STEP, L // t)
    n_s = L // t // per

    def body(r_ref, w_ref, k_ref, v_ref, a_ref, b_ref, st_ref, logp_ref, m_ref, inv_ref, u_ref, dy_ref,
             dr, dw, dk, dv, da, db, dst):
        @pl.when(pl.program_id(0) == 0)
        def _():
            dst[...] = jnp.zeros_like(dst)

        vjps = []
        for j in range(per):
            rows = slice(j * t, (j + 1) * t)
            have = (logp_ref[:, rows, :], m_ref[j], inv_ref[j], u_ref[:, rows, :])
            args = [ref[:, rows, :] for ref in (r_ref, w_ref, k_ref, v_ref, a_ref, b_ref)]
            vjps.append(jax.vjp(lambda *a, have=have: _rwkv_chunk(*a, kept=have)[:2], st_ref[j], *args)[1])
        d_state = dst[...]
        for j in reversed(range(per)):
            rows = slice(j * t, (j + 1) * t)
            g = vjps[j]((dy_ref[:, rows, :], d_state))
            d_state = g[0]
            for out, val in zip((dr, dw, dk, dv, da, db), g[1:]):
                out[:, rows, :] = val
        dst[...] = d_state

    blk = pl.BlockSpec((n_h, per * t, n), lambda c: (0, n_s - 1 - c, 0))
    per_chunk = lambda m: pl.BlockSpec((per, n_h, m, m), lambda c: (n_s - 1 - c, 0, 0, 0))
    return pl.pallas_call(
        body, name="rwkv_scan_bwd", grid=(n_s,),
        in_specs=[blk] * 6 + [per_chunk(n), blk, per_chunk(2 * t), per_chunk(t), blk, blk],
        out_specs=[blk] * 6, out_shape=[_sds((n_h, L, n))] * 6,
        scratch_shapes=[pltpu.VMEM((n_h, n, n), F32)],
        compiler_params=_params("arbitrary"),
    )(*ops, states, *kept, dy)


def _post_fn(x, u, zs, zr, ysc, r, k2, v, y_ssm, d, glu_w, glu_b, ln_w, ln_b, r_k,
             wo_s5, wo_rwkv, gf, tgt, ee):
    y3 = _gelu(y_ssm + d * u)
    y_s5 = y3 * _sigmoid(mm_bf(y3, glu_w) + glu_b) * _silu(zs)
    mean = head_sum_split(ysc, ee) * (1.0 / HEAD)
    yc = ysc - mean
    var = head_sum(yc * yc, ee) * (1.0 / HEAD)
    gn = yc * lax.rsqrt(var + GN_EPS) * ln_w + ln_b
    bonus = head_sum(r * k2 * r_k, ee) * v
    y_rwkv = (gn + bonus) * _silu(zr)
    x2 = x + mm_bf(y_s5, wo_s5) + mm_bf(y_rwkv, wo_rwkv)
    err = _rms(x2, gf) - tgt
    return 0.5 * jnp.mean(err * err, axis=-1, keepdims=True)


def _post(x, u, zs, zr, ysc, r, k2, v, y_ssm, d, glu_w, glu_b, ln_w, ln_b, r_k, w_out, gf, tgt, ee, tt):
    L = x.shape[0]
    n_t = L // tt
    acc_shapes = [(1, D_S5), (D_S5, D_S5), (1, D_S5), (1, D_RWKV), (1, D_RWKV), (1, D_RWKV),
                  (D_MODEL, D_MODEL), (1, D_MODEL), (8, LANES)]

    def body(x_ref, u_ref, zs_ref, zr_ref, ysc_ref, r_ref, k2_ref, v_ref, yssm_ref,
             d_ref, gw_ref, gb_ref, lw_ref, lb_ref, rk_ref, wo_ref, gf_ref, tgt_ref, ee_ref,
             dx_o, du_o, dzs_o, dzr_o, dysc_o, dr_o, dk2_o, dv_o, dyssm_o,
             dd_o, dgw_o, dgb_o, dlw_o, dlb_o, drk_o, dwo_o, dgf_o, loss_o,
             dd, dgw, dgb, dlw, dlb, drk, dwo, dgf, loss):
        i = pl.program_id(0)
        accs = (dd, dgw, dgb, dlw, dlb, drk, dwo, dgf, loss)

        @pl.when(i == 0)
        def _():
            for acc in accs:
                acc[...] = jnp.zeros_like(acc)

        args = (x_ref[...], u_ref[...], zs_ref[...], zr_ref[...],
                _load_heads(ysc_ref), _load_heads(r_ref), _load_heads(k2_ref), _load_heads(v_ref), yssm_ref[...],
                d_ref[...], gw_ref[...], gb_ref[...], lw_ref[...], lb_ref[...], rk_ref[...],
                wo_ref[0:D_S5, :], wo_ref[D_S5:D_MODEL, :], gf_ref[...])
        rows, vjp = jax.vjp(lambda *a: _post_fn(*a, tgt_ref[...], ee_ref[...]), *args)
        g = vjp(jnp.ones_like(rows))
        for out, val in zip((dx_o, du_o, dzs_o, dzr_o), g[0:4]):
            out[...] = val
        _store_heads(dysc_o, g[4])
        for out, val in zip((dr_o, dk2_o, dv_o, dyssm_o), g[5:9]):
            out[...] = val
        for acc, val in zip((dd, dgw, dgb, dlw, dlb, drk), g[9:15]):
            acc[...] += val
        dwo[0:D_S5, :] += g[15]
        dwo[D_S5:D_MODEL, :] += g[16]
        dgf[...] += g[17]
        loss[...] += jnp.broadcast_to(jnp.sum(rows, axis=0, keepdims=True), loss.shape)

        @pl.when(i == n_t - 1)
        def _():
            for acc, out in zip(accs, (dd_o, dgw_o, dgb_o, dlw_o, dlb_o, drk_o, dwo_o, dgf_o, loss_o)):
                pltpu.sync_copy(acc, out)

    row = lambda n: pl.BlockSpec((tt, n), lambda i: (i, 0))
    in_specs = ([row(D_MODEL)] + [row(512)] * 3 + [_head_spec(tt)] * 4 + [row(D_S5)]
                + [_const_spec(s) for s in [(1, D_S5), (D_S5, D_S5), (1, D_S5), (1, D_RWKV), (1, D_RWKV), (1, D_RWKV),
                                            (D_MODEL, D_MODEL), (1, D_MODEL)]]
                + [row(D_MODEL), _const_spec((D_RWKV, D_RWKV))])
    out_rows = [D_MODEL] + [512] * 3 + [None] + [512] * 4
    return pl.pallas_call(
        body, name="post_fwd_bwd", grid=(n_t,), in_specs=in_specs,
        out_specs=[row(n) if n else _head_spec(tt) for n in out_rows] + [_ANY] * len(acc_shapes),
        out_shape=([_sds((L, n)) if n else _sds((N_HEADS, L, HEAD)) for n in out_rows]
                   + [_sds(s) for s in acc_shapes]),
        scratch_shapes=[pltpu.VMEM(s, F32) for s in acc_shapes],
        compiler_params=_params("arbitrary"),
    )(x, u, zs, zr, ysc, r, k2, v, y_ssm, d, glu_w, glu_b, ln_w, ln_b, r_k, w_out, gf, tgt, ee)


def _s5_bwd(u, du_direct, dy, s_re, s_im, b4_re, b4_im, c4_re, c4_im, pow_r, pow_i, tt):
    L = u.shape[0]
    n_t = L // tt
    acc_shapes = ([(S5_BLOCKS, LANES, 512)] * 2 + [(S5_BLOCKS, 512, LANES)] * 2 + [(1, N_STATE)] * 2)

    def body(u_ref, dud_ref, dy_ref, sre_ref, sim_ref, pre_ref, pim_ref, bre_ref, bim_ref, cre_ref, cim_ref,
             pr_ref, pi_ref, du_o, dbre_o, dbim_o, dcre_o, dcim_o, dlr_o, dli_o,
             dbre, dbim, dcre, dcim, dlr, dli, gre, gim, car_r, car_i):
        i = pl.program_id(0)

        @pl.when(i == 0)
        def _():
            for acc in (dbre, dbim, dcre, dcim, dlr, dli, car_r, car_i):
                acc[...] = jnp.zeros_like(acc)

        uv = u_ref[...]
        dyv = dy_ref[...]
        blocks = [slice(q * 512, (q + 1) * 512) for q in range(S5_BLOCKS)]
        lanes = [slice(q * LANES, (q + 1) * LANES) for q in range(S5_BLOCKS)]
        for q in range(S5_BLOCKS):
            gre[:, blocks[q]] = _dot_bf(dyv[:, lanes[q]], cre_ref[q], ((1,), (1,)))
            gim[:, blocks[q]] = -_dot_bf(dyv[:, lanes[q]], cim_ref[q], ((1,), (1,)))
        _tile_scan(gre, gim, pr_ref, pi_ref, car_r, car_i, reverse=True)
        for q in range(S5_BLOCKS):
            gr = gre[:, blocks[q]]
            gi = gim[:, blocks[q]]
            sr = sre_ref[:, blocks[q]]
            si = sim_ref[:, blocks[q]]
            du_o[:, lanes[q]] = (dud_ref[:, lanes[q]] + _dot_bf(gr, bre_ref[q], ((1,), (1,)))
                                 + _dot_bf(gi, bim_ref[q], ((1,), (1,))))
            dbre[q] += _dot_bf(uv[:, lanes[q]], gr, ((0,), (0,)))
            dbim[q] += _dot_bf(uv[:, lanes[q]], gi, ((0,), (0,)))
            dcre[q] += _dot_bf(sr, dyv[:, lanes[q]], ((0,), (0,)))
            dcim[q] -= _dot_bf(si, dyv[:, lanes[q]], ((0,), (0,)))
            rid = lax.broadcasted_iota(jnp.int32, sr.shape, 0)
            first = i == n_t - 1
            prev_r = jnp.where(first, 0.0, pre_ref[7:8, blocks[q]])
            prev_i = jnp.where(first, 0.0, pim_ref[7:8, blocks[q]])
            pr = jnp.where(rid == 0, jnp.broadcast_to(prev_r, sr.shape), pltpu.roll(sr, 1, axis=0))
            pi_ = jnp.where(rid == 0, jnp.broadcast_to(prev_i, si.shape), pltpu.roll(si, 1, axis=0))
            dlr[:, blocks[q]] += jnp.sum(pr * gr + pi_ * gi, axis=0, keepdims=True)
            dli[:, blocks[q]] += jnp.sum(pr * gi - pi_ * gr, axis=0, keepdims=True)

        @pl.when(i == n_t - 1)
        def _():
            for acc, out in zip((dbre, dbim, dcre, dcim, dlr, dli), (dbre_o, dbim_o, dcre_o, dcim_o, dlr_o, dli_o)):
                out[...] = acc[...]

    row = lambda n: pl.BlockSpec((tt, n), lambda i: (n_t - 1 - i, 0))
    prev = pl.BlockSpec((8, N_STATE), lambda i: (jnp.maximum((n_t - 1 - i) * (tt // 8) - 1, 0), 0))
    return pl.pallas_call(
        body, name="s5_bwd", grid=(n_t,),
        in_specs=[row(D_S5)] * 3 + [row(N_STATE)] * 2 + [prev] * 2
        + [_const_spec((S5_BLOCKS, LANES, 512))] * 2 + [_const_spec((S5_BLOCKS, 512, LANES))] * 2
        + [_const_spec((8, N_STATE))] * 2,
        out_specs=[row(D_S5)] + [_acc_spec(s) for s in acc_shapes],
        out_shape=[_sds((L, D_S5))] + [_sds(s) for s in acc_shapes],
        scratch_shapes=[pltpu.VMEM(s, F32) for s in acc_shapes] + [pltpu.VMEM((tt, N_STATE), F32)] * 2
        + [pltpu.VMEM((8, N_STATE), F32)] * 2,
        compiler_params=_params("arbitrary"),
    )(u, du_direct, dy, s_re, s_im, s_re, s_im, b4_re, b4_im, c4_re, c4_im, pow_r, pow_i)


def _bwd_in(x, norm_g, w_in_bf, mu, dx2, du, dzs, drws, dzr, tt):
    L = x.shape[0]
    n_t = L // tt

    def body(x_ref, g_ref, w_ref, mu_ref, dx2_ref, du_ref, dzs_ref, drws_ref, nxt_ref, dzr_ref,
             gx_o, dw_o, dg_o, dproj, dw, dg):
        i = pl.program_id(0)

        @pl.when(i == 0)
        def _():
            dw[...] = jnp.zeros_like(dw)
            dg[...] = jnp.zeros_like(dg)

        drws_v = drws_ref[...]
        rid = lax.broadcasted_iota(jnp.int32, drws_v.shape, 0)
        nxt_row = jnp.where(i == n_t - 1, 0.0, nxt_ref[0:1, :])
        nxt = jnp.where(rid == tt - 1, jnp.broadcast_to(nxt_row, drws_v.shape), pltpu.roll(drws_v, tt - 1, axis=0))
        muv = mu_ref[...]
        drw = drws_v * (1.0 - muv) + nxt * muv
        dproj[:, 0:D_S5] = du_ref[...].astype(BF16)
        dproj[:, D_S5:2 * D_S5] = dzs_ref[...].astype(BF16)
        dproj[:, 2 * D_S5:2 * D_S5 + D_SHIFT] = drw.astype(BF16)
        dproj[:, 2 * D_S5 + D_SHIFT:D_IN] = dzr_ref[...].astype(BF16)
        dh = _dot(dproj[...], w_ref[...], ((1,), (1,)), None)
        h, vjp = jax.vjp(_rms, x_ref[...], g_ref[...])
        dxh, dgv = vjp(dh)
        gx_o[...] = dx2_ref[...] + dxh
        dg[...] += dgv
        dw[...] += _dot(h.astype(BF16), dproj[...], ((0,), (0,)), None)

        @pl.when(i == n_t - 1)
        def _():
            dg_o[...] = dg[...]
            pltpu.sync_copy(dw, dw_o)

    row = lambda n: pl.BlockSpec((tt, n), lambda i: (i, 0))
    nxt = pl.BlockSpec((8, D_SHIFT), lambda i: (jnp.minimum((i + 1) * (tt // 8), L // 8 - 1), 0))
    return pl.pallas_call(
        body, name="bwd_in", grid=(n_t,),
        in_specs=[row(D_MODEL), _const_spec((1, D_MODEL)), _const_spec((D_MODEL, D_IN)), _const_spec((1, D_SHIFT)),
                  row(D_MODEL), row(D_S5), row(D_S5), row(D_SHIFT), nxt, row(D_RWKV)],
        out_specs=[row(D_MODEL), _ANY, _acc_spec((1, D_MODEL))],
        out_shape=[_sds((L, D_MODEL)), _sds((D_MODEL, D_IN)), _sds((1, D_MODEL))],
        scratch_shapes=[pltpu.VMEM((tt, D_IN), BF16), pltpu.VMEM((D_MODEL, D_IN), F32), pltpu.VMEM((1, D_MODEL), F32)],
        compiler_params=_params("arbitrary"),
    )(x, norm_g, w_in_bf, mu, dx2, du, dzs, drws, drws, dzr)


def _block_diag_b(bbar):
    bb = bbar.reshape(S5_GROUP, S5_BLOCKS, 8, S5_STATE)
    return jnp.einsum('hqgp,Gg->qGhgp', bb, jnp.eye(8, dtype=F32)).reshape(S5_BLOCKS, LANES, 512)


def _block_diag_b_t(db4):
    d = db4.reshape(S5_BLOCKS, 8, S5_GROUP, 8, S5_STATE)
    return jnp.einsum('qGhgp,Gg->hqgp', d, jnp.eye(8, dtype=F32)).reshape(S5_GROUP, N_STATE)


def _block_diag_c(c):
    cc = c.reshape(S5_BLOCKS, 8, S5_GROUP, S5_STATE)
    return jnp.einsum('qghp,gG->qgpGh', cc, jnp.eye(8, dtype=F32)).reshape(S5_BLOCKS, 512, LANES)


def _block_diag_c_t(dc4):
    d = dc4.reshape(S5_BLOCKS, 8, S5_STATE, 8, S5_GROUP)
    return jnp.einsum('qgpGh,gG->qghp', d, jnp.eye(8, dtype=F32)).reshape(S5_GROUPS, S5_GROUP, S5_STATE)


def _local_step(x, tgt, w):
    L = x.shape[0]
    tt = min(512, L)
    tp = min(256, L)
    ee = _head_sum_matrix()

    lam_re = w['s5_lam_re'].reshape(1, N_STATE)
    lam_im = w['s5_lam_im'].reshape(1, N_STATE)
    logdt = jnp.repeat(w['s5_log_dt'], S5_STATE).reshape(1, N_STATE)
    b_re_t = w['s5_b_re'].transpose(2, 0, 1).reshape(S5_GROUP, N_STATE)
    b_im_t = w['s5_b_im'].transpose(2, 0, 1).reshape(S5_GROUP, N_STATE)
    bbr, bbi, pow_r, pow_i, rpow_r, rpow_i = _s5_param_fwd(lam_re, lam_im, logdt, b_re_t, b_im_t)
    b4_re, b4_im = _block_diag_b(bbr), _block_diag_b(bbi)
    c4_re, c4_im = _block_diag_c(w['s5_c_re']), _block_diag_c(w['s5_c_im'])

    norm_g = w['norm_g'].reshape(1, D_MODEL)
    w_in_bf = w['w_in'].astype(BF16)
    u, zs, rw, zr = _fwd_in(x, norm_g, w_in_bf, tt)
    s_re, s_im, y_ssm = _s5_fwd(u, b4_re, b4_im, c4_re, c4_im, pow_r, pow_i, tt)

    row = lambda t: t.reshape(1, -1)
    zpad = jnp.zeros((HEAD, D_RWKV), F32)
    w2p = jnp.concatenate([w['rwkv_w2'], zpad], axis=0)
    a2p = jnp.concatenate([zpad, w['rwkv_a2']], axis=0)
    pre_consts = (row(w['rwkv_mu']), row(w['rwkv_w0']), w2p, row(w['rwkv_a0']), a2p,
                  row(w['rwkv_k_k']), row(w['rwkv_k_a']), ee)
    ops = _rwkv_pre_fwd(rw, *pre_consts, tt)
    ysc, states, *kept = _rwkv_scan_fwd(ops)

    post = _post(x, u, zs, zr, ysc, ops[0], ops[2], ops[3], y_ssm,
                 row(w['s5_d']), w['s5_glu_w'], row(w['s5_glu_b']), row(w['rwkv_ln_w']), row(w['rwkv_ln_b']),
                 row(w['rwkv_r_k']), w['w_out'], row(w['final_g']), tgt, ee, tp)
    (dx2, du_d, dzs, dzr, dysc, dr_b, dk2_b, dv_b, dy_ssm,
     dd, dglu_w, dglu_b, dln_w, dln_b, dr_k, dw_out, dgf, loss) = post

    du, db4_re, db4_im, dc4_re, dc4_im, dlbr, dlbi = _s5_bwd(
        u, du_d, dy_ssm, s_re, s_im, b4_re, b4_im, c4_re, c4_im, rpow_r, rpow_i, tt)
    group_ind = (jnp.arange(N_STATE)[:, None] // S5_STATE == jnp.arange(LANES)[None, :]).astype(F32)
    dlam_re, dlam_im, dlogdt, db_re_t, db_im_t = _s5_param_bwd(
        lam_re, lam_im, logdt, b_re_t, b_im_t, dlbr, dlbi, _block_diag_b_t(db4_re), _block_diag_b_t(db4_im), group_ind)

    cots = list(_rwkv_scan_bwd(ops, states, kept, dysc)) + [dr_b, dk2_b, dv_b]
    drws, dmu, dw0, dw2p, da0, da2p, dk_k, dk_a = _rwkv_pre_bwd(rw, *pre_consts, cots, tt)

    grad_x, dw_in, dnorm_g = _bwd_in(x, norm_g, w_in_bf, row(w['rwkv_mu']), dx2, du, dzs, drws, dzr, tt)

    unb = lambda t: t.reshape(S5_GROUP, S5_GROUPS, S5_STATE).transpose(1, 2, 0)
    grads = {
        'norm_g': dnorm_g.reshape(D_MODEL), 'w_in': dw_in,
        's5_lam_re': dlam_re.reshape(S5_GROUPS, S5_STATE), 's5_lam_im': dlam_im.reshape(S5_GROUPS, S5_STATE),
        's5_log_dt': dlogdt[0, :S5_GROUPS], 's5_b_re': unb(db_re_t), 's5_b_im': unb(db_im_t),
        's5_c_re': _block_diag_c_t(dc4_re), 's5_c_im': _block_diag_c_t(dc4_im),
        's5_d': dd.reshape(D_S5), 's5_glu_w': dglu_w, 's5_glu_b': dglu_b.reshape(D_S5),
        'rwkv_mu': dmu.reshape(-1), 'rwkv_w0': dw0.reshape(-1), 'rwkv_w2': dw2p[:HEAD], 'rwkv_a0': da0.reshape(-1),
        'rwkv_a2': da2p[HEAD:], 'rwkv_k_k': dk_k.reshape(-1), 'rwkv_k_a': dk_a.reshape(-1),
        'rwkv_r_k': dr_k.reshape(N_HEADS, HEAD), 'rwkv_ln_w': dln_w.reshape(-1), 'rwkv_ln_b': dln_b.reshape(-1),
        'w_out': dw_out, 'final_g': dgf.reshape(D_MODEL),
    }
    return loss, grad_x, grads


def _exchange(arrays, gather, axes, name):
    n = len(arrays)
    group = 2 ** len(axes)

    def body(*refs):
        send_refs, recv_refs = refs[:n], refs[n:2 * n]
        send_sems, recv_sems, local_sems = refs[2 * n:]
        pos = {ax: lax.axis_index(ax) for ax in ("x", "y", "c")}

        def index_of(p):
            idx = 0
            for ax in axes:
                idx = 2 * idx + p[ax]
            return idx

        me = index_of(pos)
        own, outs, arrivals = [], [], []
        for i, (send_ref, recv_ref) in enumerate(zip(send_refs, recv_refs)):
            def block_for(dev, send_ref=send_ref, whole=gather[i]):
                return send_ref if whole else send_ref.at[dev]

            own.append(pltpu.make_async_copy(block_for(me), recv_ref.at[me], local_sems.at[i]))
            own[-1].start()
            for k in range(1, group):
                peer = dict(pos)
                for bit, ax in enumerate(axes):
                    if (k >> bit) & 1:
                        peer[ax] = 1 - pos[ax]
                peer_idx = index_of(peer)
                sems = dict(send_sem=send_sems.at[i, k - 1], recv_sem=recv_sems.at[i, k - 1],
                            device_id=(peer["x"], peer["y"], peer["c"]), device_id_type=pl.DeviceIdType.MESH)
                outs.append(pltpu.make_async_remote_copy(src_ref=block_for(peer_idx), dst_ref=recv_ref.at[me], **sems))
                outs[-1].start()
                arrivals.append(
                    pltpu.make_async_remote_copy(src_ref=block_for(peer_idx), dst_ref=recv_ref.at[peer_idx], **sems))
        for copy in arrivals:
            copy.wait_recv()
        for copy in outs:
            copy.wait_send()
        for copy in own:
            copy.wait()

    return pl.pallas_call(
        body, name=name, in_specs=[_ANY] * n, out_specs=[_ANY] * n,
        out_shape=[jax.ShapeDtypeStruct(((group,) + a.shape) if whole else a.shape, a.dtype)
                   for a, whole in zip(arrays, gather)],
        scratch_shapes=[pltpu.SemaphoreType.DMA((n, group - 1)), pltpu.SemaphoreType.DMA((n, group - 1)),
                        pltpu.SemaphoreType.DMA((n,))],
        compiler_params=pltpu.CompilerParams(has_side_effects=True),
    )(*arrays)


def _sum_devices(ref):
    g = ref[0].astype(F32)
    for s in range(1, ref.shape[0]):
        g = g + ref[s].astype(F32)
    return g


def _adamw_math(g, w, m, v):
    m_new = ADAM_B1 * m + (1.0 - ADAM_B1) * g
    v_new = ADAM_B2 * v + (1.0 - ADAM_B2) * (g * g)
    m_hat = m_new / (1.0 - ADAM_B1 ** ADAM_STEP)
    v_hat = v_new / (1.0 - ADAM_B2 ** ADAM_STEP)
    return -ADAM_LR * (m_hat / (jnp.sqrt(v_hat) + ADAM_EPS) + ADAM_WD * w), m_new, v_new


def _adamw(gs, ws, ms, vs, reduce, name):
    n = len(ws)

    def body(*refs):
        g_refs, w_refs, m_refs, v_refs = (refs[j * n:(j + 1) * n] for j in range(4))
        outs = refs[4 * n:]
        for i in range(n):
            g = _sum_devices(g_refs[i]) if reduce else g_refs[i][...]
            res = _adamw_math(g, w_refs[i][...], m_refs[i][...], v_refs[i][...])
            for j, val in enumerate(((g,) if reduce else ()) + res):
                outs[j * n + i][...] = val

    return pl.pallas_call(
        body, name=name, out_shape=[_sds(w.shape) for w in ws] * (4 if reduce else 3),
        compiler_params=pltpu.CompilerParams(vmem_limit_bytes=VMEM_LIMIT),
    )(*gs, *ws, *ms, *vs)


def _sum_blocks(recv):
    def body(recv_ref, out_ref):
        out_ref[...] = _sum_devices(recv_ref)

    return pl.pallas_call(body, name="sum_small_grads", out_shape=_sds(recv.shape[1:]))(recv)


_WEIGHTS = [
    ('norm_g', (1, 1024), False), ('w_in', (1, 1024, 400), True), ('s5_lam_re', (1, 32, 64), False),
    ('s5_lam_im', (1, 32, 64), False), ('s5_log_dt', (1, 32), False), ('s5_b_re', (1, 32, 64, 16), False),
    ('s5_b_im', (1, 32, 64, 16), False), ('s5_c_re', (1, 32, 16, 64), False), ('s5_c_im', (1, 32, 16, 64), False),
    ('s5_d', (1, 512), False), ('s5_glu_w', (1, 64, 512), True), ('s5_glu_b', (1, 512), False),
    ('rwkv_mu', (1, 1664), False), ('rwkv_w0', (1, 512), False), ('rwkv_w2', (1, 64, 64), True),
    ('rwkv_a0', (1, 512), False), ('rwkv_a2', (1, 64, 64), True), ('rwkv_k_k', (1, 512), False),
    ('rwkv_k_a', (1, 512), False), ('rwkv_r_k', (1, 8, 64), False), ('rwkv_ln_w', (1, 512), False),
    ('rwkv_ln_b', (1, 512), False), ('w_out', (1, 128, 1024), True), ('final_g', (1024,), False),
]
_SHARDED = [(n, s) for n, s, sharded in _WEIGHTS if sharded]
_SMALL = [(n, s) for n, s, sharded in _WEIGHTS if not sharded]
_COLUMN_SHARDED = ('w_in', 'rwkv_w2', 'rwkv_a2')
_SMALL_SIZE = sum(math.prod(s) for _, s in _SMALL) + 1
_SMALL_ROWS = -(-_SMALL_SIZE // (8 * LANES)) * 8


def _pack_small(grads, loss):
    flat = [grads[n].reshape(-1) for n, _ in _SMALL] + [loss.reshape(1)]
    pad = _SMALL_ROWS * LANES - _SMALL_SIZE
    return jnp.concatenate(flat + [jnp.zeros((pad,), F32)]).reshape(_SMALL_ROWS, LANES)


def _unpack_small(packed):
    flat = packed.reshape(-1)
    out, off = {}, 0
    for n, s in _SMALL:
        size = math.prod(s)
        out[n] = flat[off:off + size].reshape(s)
        off += size
    return out, flat[off]


_BF16_OPERANDS = ('w_in', 's5_glu_w', 'w_out')


def _join_shards(name, blocks):
    _, rows, cols = blocks.shape
    if name in _COLUMN_SHARDED:
        return blocks.transpose(1, 0, 2).reshape(rows, N_DEV * cols)
    return blocks.reshape(N_DEV * rows, cols)


def _split_shards(name, full, shard_shape):
    rows, cols = shard_shape
    if name in _COLUMN_SHARDED:
        return full.reshape(rows, N_DEV, cols).transpose(1, 0, 2)
    return full.reshape(N_DEV, rows, cols)


def kernel(x, norm_g, w_in, s5_lam_re, s5_lam_im, s5_log_dt, s5_b_re, s5_b_im, s5_c_re, s5_c_im, s5_d, s5_glu_w, s5_glu_b, rwkv_mu, rwkv_w0, rwkv_w2, rwkv_a0, rwkv_a2, rwkv_k_k, rwkv_k_a, rwkv_r_k, rwkv_ln_w, rwkv_ln_b, w_out, final_g, loss_target, m_norm_g, m_w_in, m_s5_lam_re, m_s5_lam_im, m_s5_log_dt, m_s5_b_re, m_s5_b_im, m_s5_c_re, m_s5_c_im, m_s5_d, m_s5_glu_w, m_s5_glu_b, m_rwkv_mu, m_rwkv_w0, m_rwkv_w2, m_rwkv_a0, m_rwkv_a2, m_rwkv_k_k, m_rwkv_k_a, m_rwkv_r_k, m_rwkv_ln_w, m_rwkv_ln_b, m_w_out, m_final_g, v_norm_g, v_w_in, v_s5_lam_re, v_s5_lam_im, v_s5_log_dt, v_s5_b_re, v_s5_b_im, v_s5_c_re, v_s5_c_im, v_s5_d, v_s5_glu_w, v_s5_glu_b, v_rwkv_mu, v_rwkv_w0, v_rwkv_w2, v_rwkv_a0, v_rwkv_a2, v_rwkv_k_k, v_rwkv_k_a, v_rwkv_r_k, v_rwkv_ln_w, v_rwkv_ln_b, v_w_out, v_final_g):
    given = dict(locals())

    n_sh = len(_SHARDED)
    everyone = ("x", "y", "c")
    shards = [given[n][0].astype(BF16 if n in _BF16_OPERANDS else F32) for n, _ in _SHARDED]
    gathered = _exchange(shards, (True,) * n_sh, everyone, "gather_weights")
    local = {n: _join_shards(n, blocks).astype(F32 if n != 'w_in' else BF16)
             for (n, _), blocks in zip(_SHARDED, gathered)}
    local.update({n: (given[n][0] if len(s) > 1 else given[n]) for n, s in _SMALL})

    loss, grad_x, grads = _local_step(x[0], loss_target[0], local)

    blocks = [_split_shards(n, grads[n], s[1:]).astype(BF16) for n, s in _SHARDED]
    small = _pack_small(grads, loss[0, 0]).reshape(N_DEV, _SMALL_ROWS // N_DEV, LANES)
    recv = _exchange(blocks + [small], (False,) * (n_sh + 1), everyone, "exchange_grads")
    small_sum = _exchange([_sum_blocks(recv[-1])], (True,), everyone, "gather_small_grads")[0]

    result = {}
    for group, name in (([0], "adamw_w_in"), ([1, 2, 3, 4], "adamw_shards")):
        ns = [_SHARDED[i][0] for i in group]
        res = _adamw([recv[i] for i in group], [given[n][0] for n in ns], [given['m_' + n][0] for n in ns],
                     [given['v_' + n][0] for n in ns], True, name)
        for j, n in enumerate(ns):
            result[n] = [res[k * len(ns) + j][None] for k in range(4)]
    g_small, total = _unpack_small(small_sum)
    two_d = lambda t: t.reshape(1, -1) if t.ndim == 1 else t
    ns = [n for n, _ in _SMALL]
    res = _adamw([two_d(g_small[n]) for n in ns], [two_d(given[n]) for n in ns], [two_d(given['m_' + n]) for n in ns],
                 [two_d(given['v_' + n]) for n in ns], False, "adamw_small")
    for j, (n, s) in enumerate(_SMALL):
        result[n] = [g_small[n]] + [res[k * len(ns) + j].reshape(s) for k in range(3)]

    outs = [total, grad_x[None]]
    for k in range(4):
        outs += [result[n][k] for n, _, _ in _WEIGHTS]
    return tuple(outs)
```

```python
import math

import jax
import jax.numpy as jnp
from jax import lax
from jax.experimental import pallas as pl
from jax.experimental.pallas import tpu as pltpu

F32 = jnp.float32
BF16 = jnp.bfloat16
HI = lax.Precision.HIGH

D_MODEL = 1024
D_S5 = 512
D_RWKV = 512
S5_GROUPS = 32
S5_GROUP = 16
S5_STATE = 64
N_STATE = S5_GROUPS * S5_STATE
N_HEADS = 8
HEAD = 64
D_SHIFT = 3 * D_RWKV + 128
D_IN = 2 * D_S5 + D_SHIFT + D_RWKV
NORM_EPS = 1e-6
GN_EPS = 64e-5
N_DEV = 8
LANES = 128
S5_BLOCKS = 4
RWKV_CHUNK = 64
RWKV_CHUNKS_PER_STEP = 4
VMEM_LIMIT = 56 * 1024 * 1024

ADAM_LR = 0.001
ADAM_B1 = 0.9
ADAM_B2 = 0.999
ADAM_EPS = 1e-08
ADAM_WD = 0.01
ADAM_STEP = 10


def _dot(a, b, dims, prec):
    return lax.dot_general(a, b, (dims, ((), ())), precision=prec, preferred_element_type=F32)


def _dot_bf(a, b, dims):
    return _dot(a.astype(BF16), b.astype(BF16), dims, None)


def _make_mm(cast, prec):
    @jax.custom_vjp
    def mm(a, b):
        return _dot(cast(a), cast(b), ((1,), (0,)), prec)

    def fwd(a, b):
        return mm(a, b), (a, b)

    def bwd(res, g):
        a, b = res
        return (_dot(cast(g), cast(b), ((1,), (1,)), prec), _dot(cast(a), cast(g), ((0,), (0,)), prec))

    mm.defvjp(fwd, bwd)
    return mm


mm_bf = _make_mm(lambda t: t.astype(BF16), None)


def _make_head_sum(split):
    def product(x, ee):
        hi = x.astype(BF16)
        out = _dot(hi, ee, ((1,), (0,)), None)
        if split:
            out = out + _dot((x - hi.astype(F32)).astype(BF16), ee, ((1,), (0,)), None)
        return out

    @jax.custom_vjp
    def head_sum(x, ee):
        return product(x, ee)

    def fwd(x, ee):
        return product(x, ee), ee

    def bwd(ee, g):
        return product(g, ee), jnp.zeros_like(ee)

    head_sum.defvjp(fwd, bwd)
    return head_sum


head_sum = _make_head_sum(False)
head_sum_split = _make_head_sum(True)


@jax.custom_vjp
def _sigmoid(x):
    return 1.0 / (1.0 + jnp.exp(-x))


def _sigmoid_fwd(x):
    s = _sigmoid(x)
    return s, s


_sigmoid.defvjp(_sigmoid_fwd, lambda s, g: (g * s * (1.0 - s),))


@jax.custom_vjp
def _silu(x):
    return x * _sigmoid(x)


def _silu_fwd(x):
    s = _sigmoid(x)
    return x * s, (x, s)


def _silu_bwd(res, g):
    x, s = res
    return (g * s * (1.0 + x * (1.0 - s)),)


_silu.defvjp(_silu_fwd, _silu_bwd)


@jax.custom_vjp
def _softplus(x):
    return jnp.maximum(x, 0.0) + jnp.log(1.0 + jnp.exp(-jnp.abs(x)))


def _softplus_fwd(x):
    e = jnp.exp(-jnp.abs(x))
    return jnp.maximum(x, 0.0) + jnp.log(1.0 + e), (x, e)


def _softplus_bwd(res, g):
    x, e = res
    return (g * jnp.where(x >= 0.0, 1.0, e) / (1.0 + e),)


_softplus.defvjp(_softplus_fwd, _softplus_bwd)


@jax.custom_vjp
def _normalize_heads(x, ee):
    return x / jnp.maximum(jnp.sqrt(head_sum(x * x, ee)), 1e-12)


def _normalize_heads_fwd(x, ee):
    norm = jnp.sqrt(head_sum(x * x, ee))
    inv = 1.0 / jnp.maximum(norm, 1e-12)
    y = x * inv
    return y, (y, inv, norm, ee)


def _normalize_heads_bwd(res, g):
    y, inv, norm, ee = res
    along = jnp.where(norm > 1e-12, head_sum(g * y, ee), 0.0)
    return inv * (g - y * along), jnp.zeros_like(ee)


_normalize_heads.defvjp(_normalize_heads_fwd, _normalize_heads_bwd)


_GELU_C = 2.0 * math.sqrt(2.0 / math.pi)


def _gelu_gate(x):
    return 1.0 / (1.0 + jnp.exp(-_GELU_C * x * (1.0 + 0.044715 * (x * x))))


@jax.custom_vjp
def _gelu(x):
    return x * _gelu_gate(x)


def _gelu_fwd(x):
    s = _gelu_gate(x)
    return x * s, (x, s)


def _gelu_bwd(res, g):
    x, s = res
    return (g * (s + x * s * (1.0 - s) * (_GELU_C * (1.0 + 3.0 * 0.044715 * (x * x)))),)


_gelu.defvjp(_gelu_fwd, _gelu_bwd)


def _rms(x, g):
    return x * lax.rsqrt(jnp.mean(x * x, axis=-1, keepdims=True) + NORM_EPS) * g


def _const_spec(shape):
    nd = len(shape)
    return pl.BlockSpec(shape, lambda *_: (0,) * nd, pipeline_mode=pl.Buffered(1))


def _acc_spec(shape):
    nd = len(shape)
    return pl.BlockSpec(shape, lambda *_: (0,) * nd)


def _params(sem):
    return pltpu.CompilerParams(dimension_semantics=(sem,), vmem_limit_bytes=VMEM_LIMIT)


_ANY = pl.BlockSpec(memory_space=pl.ANY)


def _sds(shape):
    return jax.ShapeDtypeStruct(shape, F32)


def _head_sum_matrix():
    i = jnp.arange(D_RWKV) // HEAD
    return (i[:, None] == i[None, :]).astype(BF16)


def _s5_param_fn(lam_re, lam_im, logdt, b_re, b_im):
    dt = jnp.exp(logdt)
    mag = jnp.exp(lam_re * dt)
    ang = lam_im * dt
    lbr = mag * jnp.cos(ang)
    lbi = mag * jnp.sin(ang)
    nr = lbr - 1.0
    den = lam_re * lam_re + lam_im * lam_im
    cr = (nr * lam_re + lbi * lam_im) / den
    ci = (lbi * lam_re - nr * lam_im) / den
    return lbr, lbi, cr * b_re - ci * b_im, cr * b_im + ci * b_re


def _cmul(ar, ai, br, bi):
    return ar * br - ai * bi, ar * bi + ai * br


def _s5_param_fwd(lam_re, lam_im, logdt, b_re, b_im):
    def body(lr, li, ld, br, bi, o_br, o_bi, o_pr, o_pi, o_qr, o_qi):
        lbr, lbi, bbr, bbi = _s5_param_fn(lr[...], li[...], ld[...], br[...], bi[...])
        o_br[...] = bbr
        o_bi[...] = bbi
        rid = lax.broadcasted_iota(jnp.int32, (8, N_STATE), 0)
        pr, pi_ = lbr, lbi
        fwd_r = rev_r = jnp.broadcast_to(pr, (8, N_STATE))
        fwd_i = rev_i = jnp.broadcast_to(pi_, (8, N_STATE))
        for j in range(1, 8):
            pr, pi_ = _cmul(pr, pi_, lbr, lbi)
            fwd_r = jnp.where(rid == j, jnp.broadcast_to(pr, (8, N_STATE)), fwd_r)
            fwd_i = jnp.where(rid == j, jnp.broadcast_to(pi_, (8, N_STATE)), fwd_i)
            rev_r = jnp.where(rid == 7 - j, jnp.broadcast_to(pr, (8, N_STATE)), rev_r)
            rev_i = jnp.where(rid == 7 - j, jnp.broadcast_to(pi_, (8, N_STATE)), rev_i)
        o_pr[...] = fwd_r
        o_pi[...] = fwd_i
        o_qr[...] = rev_r
        o_qi[...] = -rev_i

    return pl.pallas_call(
        body, name="s5_param_fwd",
        out_shape=[_sds((S5_GROUP, N_STATE))] * 2 + [_sds((8, N_STATE))] * 4,
    )(lam_re, lam_im, logdt, b_re, b_im)


def _s5_param_bwd(lam_re, lam_im, logdt, b_re, b_im, d_lbr, d_lbi, d_bbr, d_bbi, group_ind):
    def body(lr, li, ld, br, bi, g0, g1, g2, g3, ind, o_lr, o_li, o_ld, o_br, o_bi):
        _, vjp = jax.vjp(_s5_param_fn, lr[...], li[...], ld[...], br[...], bi[...])
        d_lr, d_li, d_ld, d_br, d_bi = vjp((g0[...], g1[...], g2[...], g3[...]))
        o_lr[...] = d_lr
        o_li[...] = d_li
        o_ld[...] = _dot(jnp.broadcast_to(d_ld, (8, N_STATE)), ind[...], ((1,), (0,)), HI)
        o_br[...] = d_br
        o_bi[...] = d_bi

    return pl.pallas_call(
        body, name="s5_param_bwd",
        out_shape=[_sds((1, N_STATE))] * 2 + [_sds((8, LANES))] + [_sds((S5_GROUP, N_STATE))] * 2,
    )(lam_re, lam_im, logdt, b_re, b_im, d_lbr, d_lbi, d_bbr, d_bbi, group_ind)


def _fwd_in(x, norm_g, w_in_bf, tt):
    L = x.shape[0]

    def body(x_ref, g_ref, w_ref, u_ref, zs_ref, rw_ref, zr_ref):
        h = _rms(x_ref[...], g_ref[...])
        proj = jnp.dot(h.astype(BF16), w_ref[...], preferred_element_type=F32)
        u_ref[...] = proj[:, 0:D_S5]
        zs_ref[...] = proj[:, D_S5:2 * D_S5]
        rw_ref[...] = proj[:, 2 * D_S5:2 * D_S5 + D_SHIFT]
        zr_ref[...] = proj[:, 2 * D_S5 + D_SHIFT:D_IN]

    row = lambda n: pl.BlockSpec((tt, n), lambda i: (i, 0))
    return pl.pallas_call(
        body, name="fwd_in", grid=(L // tt,),
        in_specs=[row(D_MODEL), _const_spec((1, D_MODEL)), _const_spec((D_MODEL, D_IN))],
        out_specs=[row(D_S5), row(D_S5), row(D_SHIFT), row(D_RWKV)],
        out_shape=[_sds((L, D_S5)), _sds((L, D_S5)), _sds((L, D_SHIFT)), _sds((L, D_RWKV))],
        compiler_params=_params("parallel"),
    )(x, norm_g, w_in_bf)


S5_LANE_CHUNK = 512


def _tile_scan(re_ref, im_ref, pow_r_ref, pow_i_ref, carry_r_ref, carry_i_ref, reverse):
    t, n = re_ref.shape
    n_groups = t // 8
    ch = S5_LANE_CHUNK
    rid = lax.broadcasted_iota(jnp.int32, (8, ch), 0)
    for c in range(n // ch):
        cols = slice(c * ch, (c + 1) * ch)
        pow_r = pow_r_ref[:, cols]
        pow_i = pow_i_ref[:, cols]
        row = lambda tile, j: jnp.broadcast_to(tile[j:j + 1], (8, ch))
        levels = [(d, row(pow_r, 8 - d if reverse else d - 1), row(pow_i, 8 - d if reverse else d - 1))
                  for d in (1, 2, 4)]

        def group(g, carry):
            r0 = pl.multiple_of(((n_groups - 1 - g) if reverse else g) * 8, 8)
            xr = re_ref[pl.ds(r0, 8), cols]
            xi = im_ref[pl.ds(r0, 8), cols]
            for d, lr, li in levels:
                keep = (rid < 8 - d) if reverse else (rid >= d)
                shift = (8 - d) if reverse else d
                sr = jnp.where(keep, pltpu.roll(xr, shift, axis=0), 0.0)
                si = jnp.where(keep, pltpu.roll(xi, shift, axis=0), 0.0)
                mr, mi = _cmul(lr, li, sr, si)
                xr = xr + mr
                xi = xi + mi
            mr, mi = _cmul(pow_r, pow_i, carry[0], carry[1])
            xr = xr + mr
            xi = xi + mi
            re_ref[pl.ds(r0, 8), cols] = xr
            im_ref[pl.ds(r0, 8), cols] = xi
            last = 0 if reverse else 7
            return row(xr, last), row(xi, last)

        out = lax.fori_loop(0, n_groups, group, (carry_r_ref[:, cols], carry_i_ref[:, cols]))
        carry_r_ref[:, cols] = out[0]
        carry_i_ref[:, cols] = out[1]


def _s5_fwd(u, b4_re, b4_im, c4_re, c4_im, pow_r, pow_i, tt):
    L = u.shape[0]

    def body(u_ref, bre_ref, bim_ref, cre_ref, cim_ref, pr_ref, pi_ref, sre_o, sim_o, y_o, car_r, car_i):
        @pl.when(pl.program_id(0) == 0)
        def _():
            car_r[...] = jnp.zeros_like(car_r)
            car_i[...] = jnp.zeros_like(car_i)

        uv = u_ref[...]
        for q in range(S5_BLOCKS):
            uq = uv[:, q * LANES:(q + 1) * LANES]
            cols = slice(q * 512, (q + 1) * 512)
            sre_o[:, cols] = _dot_bf(uq, bre_ref[q], ((1,), (0,)))
            sim_o[:, cols] = _dot_bf(uq, bim_ref[q], ((1,), (0,)))
        _tile_scan(sre_o, sim_o, pr_ref, pi_ref, car_r, car_i, reverse=False)
        for q in range(S5_BLOCKS):
            cols = slice(q * 512, (q + 1) * 512)
            y_o[:, q * LANES:(q + 1) * LANES] = (_dot_bf(sre_o[:, cols], cre_ref[q], ((1,), (0,)))
                                                 - _dot_bf(sim_o[:, cols], cim_ref[q], ((1,), (0,))))

    row = lambda n: pl.BlockSpec((tt, n), lambda i: (i, 0))
    return pl.pallas_call(
        body, name="s5_fwd", grid=(L // tt,),
        in_specs=[row(D_S5)] + [_const_spec((S5_BLOCKS, LANES, 512))] * 2 + [_const_spec((S5_BLOCKS, 512, LANES))] * 2
        + [_const_spec((8, N_STATE))] * 2,
        out_specs=[row(N_STATE), row(N_STATE), row(D_S5)],
        out_shape=[_sds((L, N_STATE)), _sds((L, N_STATE)), _sds((L, D_S5))],
        scratch_shapes=[pltpu.VMEM((8, N_STATE), F32)] * 2,
        compiler_params=_params("arbitrary"),
    )(u, b4_re, b4_im, c4_re, c4_im, pow_r, pow_i)


def _rwkv_pre_fn(r, k, v, wa, w0, w2p, a0, a2p, k_k, k_a, ee):
    w = -_softplus(-(w0 + mm_bf(jnp.tanh(wa), w2p))) - 0.5
    logw = -jnp.exp(w)
    a = _sigmoid(a0 + mm_bf(wa, a2p))
    kk = _normalize_heads(k * k_k, ee)
    k2 = k * (1.0 + (a - 1.0) * k_a)
    return r, logw, k2, v, -kk, kk * a


N_PAIRS = N_HEADS // 2


def _head_spec(tt):
    return pl.BlockSpec((N_PAIRS, tt, LANES), lambda i: (0, i, 0))


def _load_heads(ref):
    return jnp.concatenate([ref[p] for p in range(N_PAIRS)], axis=-1)


def _store_heads(ref, val):
    for p in range(N_PAIRS):
        ref[p] = val[:, p * LANES:(p + 1) * LANES]


def _split_pairs(x):
    return jnp.concatenate([x[:, :, :HEAD], x[:, :, HEAD:]], axis=0)


def _join_pairs(x):
    return jnp.concatenate([x[:N_PAIRS], x[N_PAIRS:]], axis=-1)


def _shifted(rw, prev_blk, first):
    rolled = pltpu.roll(rw, 1, axis=0)
    prev_row = jnp.where(first, 0.0, prev_blk[7:8, :])
    rid = lax.broadcasted_iota(jnp.int32, rw.shape, 0)
    return jnp.where(rid == 0, jnp.broadcast_to(prev_row, rw.shape), rolled)


def _split_rw(t):
    return t[:, 0:512], t[:, 512:1024], t[:, 1024:1536], t[:, 1536:1664]


def _rwkv_pre_specs(tt):
    row = pl.BlockSpec((tt, D_SHIFT), lambda i: (i, 0))
    prev = pl.BlockSpec((8, D_SHIFT), lambda i: (jnp.maximum(i * (tt // 8) - 1, 0), 0))
    consts = [_const_spec((1, D_SHIFT)), _const_spec((1, D_RWKV)), _const_spec((LANES, D_RWKV)),
              _const_spec((1, D_RWKV)), _const_spec((LANES, D_RWKV)), _const_spec((1, D_RWKV)),
              _const_spec((1, D_RWKV)), _const_spec((D_RWKV, D_RWKV))]
    return [row, prev] + consts


def _rwkv_pre_fwd(rw, mu, w0, w2p, a0, a2p, k_k, k_a, ee, tt):
    L = rw.shape[0]

    def body(rw_ref, prev_ref, mu_ref, w0_ref, w2_ref, a0_ref, a2_ref, kk_ref, ka_ref, ee_ref, *outs):
        rwv = rw_ref[...]
        rws = rwv + (_shifted(rwv, prev_ref[...], pl.program_id(0) == 0) - rwv) * mu_ref[...]
        res = _rwkv_pre_fn(*_split_rw(rws), w0_ref[...], w2_ref[...], a0_ref[...], a2_ref[...],
                           kk_ref[...], ka_ref[...], ee_ref[...])
        for o, val in zip(outs, res):
            _store_heads(o, val)

    return pl.pallas_call(
        body, name="rwkv_pre_fwd", grid=(L // tt,),
        in_specs=_rwkv_pre_specs(tt), out_specs=[_head_spec(tt)] * 6, out_shape=[_sds((N_PAIRS, L, LANES))] * 6,
        compiler_params=_params("parallel"),
    )(rw, rw, mu, w0, w2p, a0, a2p, k_k, k_a, ee)


def _rwkv_pre_bwd(rw, mu, w0, w2p, a0, a2p, k_k, k_a, ee, cots, tt):
    L = rw.shape[0]
    n_t = L // tt

    def body(rw_ref, prev_ref, mu_ref, w0_ref, w2_ref, a0_ref, a2_ref, kk_ref, ka_ref, ee_ref,
             c_r, c_w, c_k, c_v, c_a, c_b, cb_r, cb_k, cb_v,
             drws_ref, dmu_o, dw0_o, dw2_o, da0_o, da2_o, dkk_o, dka_o,
             dmu, dw0, dw2, da0, da2, dkk, dka):
        i = pl.program_id(0)
        accs = (dmu, dw0, dw2, da0, da2, dkk, dka)

        @pl.when(i == 0)
        def _():
            for acc in accs:
                acc[...] = jnp.zeros_like(acc)

        rwv = rw_ref[...]
        diff = _shifted(rwv, prev_ref[...], i == 0) - rwv
        rws = rwv + diff * mu_ref[...]
        consts = (w0_ref[...], w2_ref[...], a0_ref[...], a2_ref[...], kk_ref[...], ka_ref[...])
        _, vjp = jax.vjp(lambda *a: _rwkv_pre_fn(*a, ee_ref[...]), *_split_rw(rws), *consts)
        scan = [_load_heads(c) for c in (c_r, c_w, c_k, c_v, c_a, c_b)]
        g = vjp((scan[0] + cb_r[...], scan[1], scan[2] + cb_k[...], scan[3] + cb_v[...], scan[4], scan[5]))
        drws = jnp.concatenate(g[0:4], axis=-1)
        drws_ref[...] = drws
        dmu[...] += jnp.sum(drws * diff, axis=0, keepdims=True)
        for acc, val in zip(accs[1:], g[4:]):
            acc[...] += val

        @pl.when(i == n_t - 1)
        def _():
            for acc, out in zip(accs, (dmu_o, dw0_o, dw2_o, da0_o, da2_o, dkk_o, dka_o)):
                out[...] = acc[...]

    row = pl.BlockSpec((tt, D_RWKV), lambda i: (i, 0))
    shapes = [(1, D_SHIFT), (1, D_RWKV), (LANES, D_RWKV), (1, D_RWKV), (LANES, D_RWKV), (1, D_RWKV), (1, D_RWKV)]
    return pl.pallas_call(
        body, name="rwkv_pre_bwd", grid=(n_t,),
        in_specs=_rwkv_pre_specs(tt) + [_head_spec(tt)] * 6 + [row] * 3,
        out_specs=[pl.BlockSpec((tt, D_SHIFT), lambda i: (i, 0))] + [_acc_spec(s) for s in shapes],
        out_shape=[_sds((L, D_SHIFT))] + [_sds(s) for s in shapes],
        scratch_shapes=[pltpu.VMEM(s, F32) for s in shapes],
        compiler_params=_params("arbitrary"),
    )(rw, rw, mu, w0, w2p, a0, a2p, k_k, k_a, ee, *cots)


def _bmm(a, b):
    return lax.dot_general(a, b, (((2,), (1,)), ((0,), (0,))), precision=HI, preferred_element_type=F32)


def _bmm_nt(a, b):
    return lax.dot_general(a, b, (((2,), (2,)), ((0,), (0,))), precision=HI, preferred_element_type=F32)


def _bmm_tn(a, b):
    return lax.dot_general(a, b, (((1,), (1,)), ((0,), (0,))), precision=HI, preferred_element_type=F32)


def _bdot_bf(a, b, lhs_dim, rhs_dim):
    return lax.dot_general(a.astype(BF16), b.astype(BF16), (((lhs_dim,), (rhs_dim,)), ((0,), (0,))),
                           preferred_element_type=F32)


@jax.custom_vjp
def _bmm_bf(a, b):
    return _bdot_bf(a, b, 2, 1)


def _bmm_bf_fwd(a, b):
    return _bmm_bf(a, b), (a, b)


def _bmm_bf_bwd(res, g):
    a, b = res
    return _bdot_bf(g, b, 2, 2), _bdot_bf(a, g, 1, 1)


_bmm_bf.defvjp(_bmm_bf_fwd, _bmm_bf_bwd)


@jax.custom_vjp
def _bmm_tn_bf(a, b):
    return _bdot_bf(a, b, 1, 1)


def _bmm_tn_bf_fwd(a, b):
    return _bmm_tn_bf(a, b), (a, b)


def _bmm_tn_bf_bwd(res, g):
    a, b = res
    return _bdot_bf(b, g, 2, 2), _bdot_bf(a, g, 2, 1)


_bmm_tn_bf.defvjp(_bmm_tn_bf_fwd, _bmm_tn_bf_bwd)


def _unit_lower_inverse(a):
    t = a.shape[-1]
    ti = lax.broadcasted_iota(jnp.int32, (t, t), 0)
    si = lax.broadcasted_iota(jnp.int32, (t, t), 1)

    def same_block(bits):
        shift = jnp.int32(bits)
        return (lax.shift_right_logical(ti, shift) == lax.shift_right_logical(si, shift))[None]

    def mm(x, y):
        return _bdot_bf(x, y, 2, 1)

    d = jnp.where(same_block(3), a, 0.0)
    inv = jnp.where(ti == si, 1.0, 0.0)[None] + d
    pw = mm(d, d)
    both = mm(jnp.concatenate([inv, pw], axis=1), pw)
    inv = inv + both[:, :t]
    inv = inv + mm(inv, both[:, t:])
    bits = 3
    while (1 << bits) < t:
        e = jnp.where(same_block(bits), 0.0, jnp.where(same_block(bits + 1), a, 0.0))
        inv = inv + mm(mm(inv, e), inv)
        bits += 1
    return inv


def _tri_mask(t):
    ri = lax.broadcasted_iota(jnp.int32, (2 * t, 2 * t), 0)
    ci = lax.broadcasted_iota(jnp.int32, (2 * t, 2 * t), 1)
    top_rows = ri < t
    diff = jnp.where(top_rows, ri, ri - t) - jnp.where(ci < t, ci, ci - t)
    return (diff >= jnp.where(top_rows, 1, 0))[None]


def _ones_tri(n_h, t):
    ti = lax.broadcasted_iota(jnp.int32, (t, t), 0)
    si = lax.broadcasted_iota(jnp.int32, (t, t), 1)
    return jnp.broadcast_to(jnp.where(ti >= si, 1.0, 0.0)[None], (n_h, t, t))


@jax.custom_vjp
def _running_sum_kept(logw, kept):
    return kept


def _running_sum_kept_bwd(shape, g):
    return _bmm_tn(_ones_tri(shape[0], shape[1]), g), jnp.zeros_like(g)


_running_sum_kept.defvjp(lambda logw, kept: (kept, logw.shape), _running_sum_kept_bwd)


@jax.custom_vjp
def _tri_products_kept(ar, bk, kept):
    return kept


def _tri_products_kept_bwd(res, g):
    ar, bk = res
    g = jnp.where(_tri_mask(ar.shape[1] // 2), g, 0.0)
    return _bmm(g, bk), _bmm_tn(g, ar), jnp.zeros_like(g)


_tri_products_kept.defvjp(lambda ar, bk, kept: (kept, (ar, bk)), _tri_products_kept_bwd)


@jax.custom_vjp
def _solve_unit_lower(a, rhs, inv, kept=None):
    return _bmm(inv, rhs) if kept is None else kept


def _solve_fwd(a, rhs, inv, kept=None):
    u = _bmm(inv, rhs) if kept is None else kept
    return u, (inv, u, kept is not None)


def _solve_bwd(res, du):
    inv, u, had_kept = res
    d_rhs = _bmm_tn(inv, du)
    return _bmm_nt(d_rhs, u), d_rhs, jnp.zeros_like(inv), (jnp.zeros_like(u) if had_kept else None)


_solve_unit_lower.defvjp(_solve_fwd, _solve_bwd)


def _rwkv_chunk(st0, r, logw, k, v, a, b, kept=None):
    n_h, t, _ = r.shape
    log_p = _bmm(_ones_tri(n_h, t), logw) if kept is None else _running_sum_kept(logw, kept[0])
    p_in = jnp.exp(log_p)
    p_inv = jnp.exp(-log_p)
    at = a * jnp.exp(log_p - logw)
    rt = r * p_in
    ar = jnp.concatenate([at, rt], axis=1)
    bk = jnp.concatenate([b * p_inv, k * p_inv], axis=1)
    if kept is None:
        m = jnp.where(_tri_mask(t), _bmm_nt(ar, bk), 0.0)
        inv = _unit_lower_inverse(m[:, :t, :t])
    else:
        m = _tri_products_kept(ar, bk, kept[1])
        inv = kept[2]
    top, bottom = m[:, :t], m[:, t:]
    rhs = _bmm_bf(jnp.concatenate([at, top[:, :, t:]], axis=2), jnp.concatenate([st0, v], axis=1))
    u = _solve_unit_lower(top[:, :, :t], rhs, inv, None if kept is None else kept[3])
    y = _bmm_bf(jnp.concatenate([rt, bottom], axis=2), jnp.concatenate([st0, u, v], axis=1))
    p_end = jnp.swapaxes(p_in[:, t - 1:t, :], 1, 2)
    st1 = (st0 + _bmm_tn_bf(bk, jnp.concatenate([u, v], axis=1))) * p_end
    return y, st1, (log_p, m, inv, u)


def _rwkv_scan_fwd(ops):
    n_h, L, n = N_HEADS, ops[0].shape[1], HEAD
    t = RWKV_CHUNK
    per = min(RWKV_CHUNKS_PER_STEP, L // t)
    n_c = L // t
    n_s = n_c // per

    def body(r_ref, w_ref, k_ref, v_ref, a_ref, b_ref, y_ref, st_ref, logp_ref, m_ref, inv_ref, u_ref, st):
        @pl.when(pl.program_id(0) == 0)
        def _():
            st[...] = jnp.zeros_like(st)

        st0 = st[...]
        for j in range(per):
            rows = slice(j * t, (j + 1) * t)
            st_ref[j] = st0
            y, st0, (log_p, m, inv, u) = _rwkv_chunk(
                st0, *(_split_pairs(ref[:, rows, :]) for ref in (r_ref, w_ref, k_ref, v_ref, a_ref, b_ref)))
            y_ref[:, rows, :] = _join_pairs(y)
            logp_ref[:, rows, :] = log_p
            u_ref[:, rows, :] = u
            m_ref[j] = m
            inv_ref[j] = inv
        st[...] = st0

    pairs = pl.BlockSpec((N_PAIRS, per * t, LANES), lambda c: (0, c, 0))
    blk = pl.BlockSpec((n_h, per * t, n), lambda c: (0, c, 0))
    per_chunk = lambda m: pl.BlockSpec((per, n_h, m, m), lambda c: (c, 0, 0, 0))
    return pl.pallas_call(
        body, name="rwkv_scan_fwd", grid=(n_s,), in_specs=[pairs] * 6,
        out_specs=[pairs, per_chunk(n), blk, per_chunk(2 * t), per_chunk(t), blk],
        out_shape=[_sds((N_PAIRS, L, LANES)), _sds((n_c, n_h, n, n)), _sds((n_h, L, n)),
                   _sds((n_c, n_h, 2 * t, 2 * t)), _sds((n_c, n_h, t, t)), _sds((n_h, L, n))],
        scratch_shapes=[pltpu.VMEM((n_h, n, n), F32)],
        compiler_params=_params("arbitrary"),
    )(*ops)


def _rwkv_scan_bwd(ops, states, kept, dy):
    n_h, L, n = N_HEADS, ops[0].shape[1], HEAD
    t = RWKV_CHUNK
    per = min(RWKV_CHUNKS_PER_STEP, L // t)
    n_s = L // t // per

    def body(r_ref, w_ref, k_ref, v_ref, a_ref, b_ref, st_ref, logp_ref, m_ref, inv_ref, u_ref, dy_ref,
             dr, dw, dk, dv, da, db, dst):
        @pl.when(pl.program_id(0) == 0)
        def _():
            dst[...] = jnp.zeros_like(dst)

        vjps = []
        for j in range(per):
            rows = slice(j * t, (j + 1) * t)
            have = (logp_ref[:, rows, :], m_ref[j], inv_ref[j], u_ref[:, rows, :])
            args = [_split_pairs(ref[:, rows, :]) for ref in (r_ref, w_ref, k_ref, v_ref, a_ref, b_ref)]
            vjps.append(jax.vjp(lambda *a, have=have: _rwkv_chunk(*a, kept=have)[:2], st_ref[j], *args)[1])
        d_state = dst[...]
        for j in reversed(range(per)):
            rows = slice(j * t, (j + 1) * t)
            g = vjps[j]((_split_pairs(dy_ref[:, rows, :]), d_state))
            d_state = g[0]
            for out, val in zip((dr, dw, dk, dv, da, db), g[1:]):
                out[:, rows, :] = _join_pairs(val)
        dst[...] = d_state

    pairs = pl.BlockSpec((N_PAIRS, per * t, LANES), lambda c: (0, n_s - 1 - c, 0))
    blk = pl.BlockSpec((n_h, per * t, n), lambda c: (0, n_s - 1 - c, 0))
    per_chunk = lambda m: pl.BlockSpec((per, n_h, m, m), lambda c: (n_s - 1 - c, 0, 0, 0))
    return pl.pallas_call(
        body, name="rwkv_scan_bwd", grid=(n_s,),
        in_specs=[pairs] * 6 + [per_chunk(n), blk, per_chunk(2 * t), per_chunk(t), blk, pairs],
        out_specs=[pairs] * 6, out_shape=[_sds((N_PAIRS, L, LANES))] * 6,
        scratch_shapes=[pltpu.VMEM((n_h, n, n), F32)],
        compiler_params=_params("arbitrary"),
    )(*ops, states, *kept, dy)


def _post_fn(x, u, zs, zr, ysc, r, k2, v, y_ssm, d, glu_w, glu_b, ln_w, ln_b, r_k,
             wo_s5, wo_rwkv, gf, tgt, ee):
    y3 = _gelu(y_ssm + d * u)
    y_s5 = y3 * _sigmoid(mm_bf(y3, glu_w) + glu_b) * _silu(zs)
    mean = head_sum_split(ysc, ee) * (1.0 / HEAD)
    yc = ysc - mean
    var = head_sum(yc * yc, ee) * (1.0 / HEAD)
    gn = yc * lax.rsqrt(var + GN_EPS) * ln_w + ln_b
    bonus = head_sum(r * k2 * r_k, ee) * v
    y_rwkv = (gn + bonus) * _silu(zr)
    x2 = x + mm_bf(y_s5, wo_s5) + mm_bf(y_rwkv, wo_rwkv)
    err = _rms(x2, gf) - tgt
    return 0.5 * jnp.mean(err * err, axis=-1, keepdims=True)


def _post(x, u, zs, zr, ysc, r, k2, v, y_ssm, d, glu_w, glu_b, ln_w, ln_b, r_k, w_out, gf, tgt, ee, tt):
    L = x.shape[0]
    n_t = L // tt
    acc_shapes = [(1, D_S5), (D_S5, D_S5), (1, D_S5), (1, D_RWKV), (1, D_RWKV), (1, D_RWKV),
                  (D_MODEL, D_MODEL), (1, D_MODEL), (8, LANES)]

    def body(x_ref, u_ref, zs_ref, zr_ref, ysc_ref, r_ref, k2_ref, v_ref, yssm_ref,
             d_ref, gw_ref, gb_ref, lw_ref, lb_ref, rk_ref, wo_ref, gf_ref, tgt_ref, ee_ref,
             dx_o, du_o, dzs_o, dzr_o, dysc_o, dr_o, dk2_o, dv_o, dyssm_o,
             dd_o, dgw_o, dgb_o, dlw_o, dlb_o, drk_o, dwo_o, dgf_o, loss_o,
             dd, dgw, dgb, dlw, dlb, drk, dwo, dgf, loss):
        i = pl.program_id(0)
        accs = (dd, dgw, dgb, dlw, dlb, drk, dwo, dgf, loss)

        @pl.when(i == 0)
        def _():
            for acc in accs:
                acc[...] = jnp.zeros_like(acc)

        args = (x_ref[...], u_ref[...], zs_ref[...], zr_ref[...],
                _load_heads(ysc_ref), _load_heads(r_ref), _load_heads(k2_ref), _load_heads(v_ref), yssm_ref[...],
                d_ref[...], gw_ref[...], gb_ref[...], lw_ref[...], lb_ref[...], rk_ref[...],
                wo_ref[0:D_S5, :], wo_ref[D_S5:D_MODEL, :], gf_ref[...])
        rows, vjp = jax.vjp(lambda *a: _post_fn(*a, tgt_ref[...], ee_ref[...]), *args)
        g = vjp(jnp.ones_like(rows))
        for out, val in zip((dx_o, du_o, dzs_o, dzr_o), g[0:4]):
            out[...] = val
        _store_heads(dysc_o, g[4])
        for out, val in zip((dr_o, dk2_o, dv_o, dyssm_o), g[5:9]):
            out[...] = val
        for acc, val in zip((dd, dgw, dgb, dlw, dlb, drk), g[9:15]):
            acc[...] += val
        dwo[0:D_S5, :] += g[15]
        dwo[D_S5:D_MODEL, :] += g[16]
        dgf[...] += g[17]
        loss[...] += jnp.broadcast_to(jnp.sum(rows, axis=0, keepdims=True), loss.shape)

        @pl.when(i == n_t - 1)
        def _():
            for acc, out in zip(accs, (dd_o, dgw_o, dgb_o, dlw_o, dlb_o, drk_o, dwo_o, dgf_o, loss_o)):
                pltpu.sync_copy(acc, out)

    row = lambda n: pl.BlockSpec((tt, n), lambda i: (i, 0))
    in_specs = ([row(D_MODEL)] + [row(512)] * 3 + [_head_spec(tt)] * 4 + [row(D_S5)]
                + [_const_spec(s) for s in [(1, D_S5), (D_S5, D_S5), (1, D_S5), (1, D_RWKV), (1, D_RWKV), (1, D_RWKV),
                                            (D_MODEL, D_MODEL), (1, D_MODEL)]]
                + [row(D_MODEL), _const_spec((D_RWKV, D_RWKV))])
    out_rows = [D_MODEL] + [512] * 3 + [None] + [512] * 4
    return pl.pallas_call(
        body, name="post_fwd_bwd", grid=(n_t,), in_specs=in_specs,
        out_specs=[row(n) if n else _head_spec(tt) for n in out_rows] + [_ANY] * len(acc_shapes),
        out_shape=([_sds((L, n)) if n else _sds((N_PAIRS, L, LANES)) for n in out_rows]
                   + [_sds(s) for s in acc_shapes]),
        scratch_shapes=[pltpu.VMEM(s, F32) for s in acc_shapes],
        compiler_params=_params("arbitrary"),
    )(x, u, zs, zr, ysc, r, k2, v, y_ssm, d, glu_w, glu_b, ln_w, ln_b, r_k, w_out, gf, tgt, ee)


def _s5_bwd(u, du_direct, dy, s_re, s_im, b4_re, b4_im, c4_re, c4_im, pow_r, pow_i, tt):
    L = u.shape[0]
    n_t = L // tt
    acc_shapes = ([(S5_BLOCKS, LANES, 512)] * 2 + [(S5_BLOCKS, 512, LANES)] * 2 + [(1, N_STATE)] * 2)

    def body(u_ref, dud_ref, dy_ref, sre_ref, sim_ref, pre_ref, pim_ref, bre_ref, bim_ref, cre_ref, cim_ref,
             pr_ref, pi_ref, du_o, dbre_o, dbim_o, dcre_o, dcim_o, dlr_o, dli_o,
             dbre, dbim, dcre, dcim, dlr, dli, gre, gim, car_r, car_i):
        i = pl.program_id(0)

        @pl.when(i == 0)
        def _():
            for acc in (dbre, dbim, dcre, dcim, dlr, dli, car_r, car_i):
                acc[...] = jnp.zeros_like(acc)

        uv = u_ref[...]
        dyv = dy_ref[...]
        blocks = [slice(q * 512, (q + 1) * 512) for q in range(S5_BLOCKS)]
        lanes = [slice(q * LANES, (q + 1) * LANES) for q in range(S5_BLOCKS)]
        for q in range(S5_BLOCKS):
            gre[:, blocks[q]] = _dot_bf(dyv[:, lanes[q]], cre_ref[q], ((1,), (1,)))
            gim[:, blocks[q]] = -_dot_bf(dyv[:, lanes[q]], cim_ref[q], ((1,), (1,)))
        _tile_scan(gre, gim, pr_ref, pi_ref, car_r, car_i, reverse=True)
        for q in range(S5_BLOCKS):
            gr = gre[:, blocks[q]]
            gi = gim[:, blocks[q]]
            sr = sre_ref[:, blocks[q]]
            si = sim_ref[:, blocks[q]]
            du_o[:, lanes[q]] = (dud_ref[:, lanes[q]] + _dot_bf(gr, bre_ref[q], ((1,), (1,)))
                                 + _dot_bf(gi, bim_ref[q], ((1,), (1,))))
            dbre[q] += _dot_bf(uv[:, lanes[q]], gr, ((0,), (0,)))
            dbim[q] += _dot_bf(uv[:, lanes[q]], gi, ((0,), (0,)))
            dcre[q] += _dot_bf(sr, dyv[:, lanes[q]], ((0,), (0,)))
            dcim[q] -= _dot_bf(si, dyv[:, lanes[q]], ((0,), (0,)))
            rid = lax.broadcasted_iota(jnp.int32, sr.shape, 0)
            first = i == n_t - 1
            prev_r = jnp.where(first, 0.0, pre_ref[7:8, blocks[q]])
            prev_i = jnp.where(first, 0.0, pim_ref[7:8, blocks[q]])
            pr = jnp.where(rid == 0, jnp.broadcast_to(prev_r, sr.shape), pltpu.roll(sr, 1, axis=0))
            pi_ = jnp.where(rid == 0, jnp.broadcast_to(prev_i, si.shape), pltpu.roll(si, 1, axis=0))
            dlr[:, blocks[q]] += jnp.sum(pr * gr + pi_ * gi, axis=0, keepdims=True)
            dli[:, blocks[q]] += jnp.sum(pr * gi - pi_ * gr, axis=0, keepdims=True)

        @pl.when(i == n_t - 1)
        def _():
            for acc, out in zip((dbre, dbim, dcre, dcim, dlr, dli), (dbre_o, dbim_o, dcre_o, dcim_o, dlr_o, dli_o)):
                out[...] = acc[...]

    row = lambda n: pl.BlockSpec((tt, n), lambda i: (n_t - 1 - i, 0))
    prev = pl.BlockSpec((8, N_STATE), lambda i: (jnp.maximum((n_t - 1 - i) * (tt // 8) - 1, 0), 0))
    return pl.pallas_call(
        body, name="s5_bwd", grid=(n_t,),
        in_specs=[row(D_S5)] * 3 + [row(N_STATE)] * 2 + [prev] * 2
        + [_const_spec((S5_BLOCKS, LANES, 512))] * 2 + [_const_spec((S5_BLOCKS, 512, LANES))] * 2
        + [_const_spec((8, N_STATE))] * 2,
        out_specs=[row(D_S5)] + [_acc_spec(s) for s in acc_shapes],
        out_shape=[_sds((L, D_S5))] + [_sds(s) for s in acc_shapes],
        scratch_shapes=[pltpu.VMEM(s, F32) for s in acc_shapes] + [pltpu.VMEM((tt, N_STATE), F32)] * 2
        + [pltpu.VMEM((8, N_STATE), F32)] * 2,
        compiler_params=_params("arbitrary"),
    )(u, du_direct, dy, s_re, s_im, s_re, s_im, b4_re, b4_im, c4_re, c4_im, pow_r, pow_i)


def _bwd_in(x, norm_g, w_in_bf, mu, dx2, du, dzs, drws, dzr, tt):
    L = x.shape[0]
    n_t = L // tt

    def body(x_ref, g_ref, w_ref, mu_ref, dx2_ref, du_ref, dzs_ref, drws_ref, nxt_ref, dzr_ref,
             gx_o, dw_o, dg_o, dproj, dw, dg):
        i = pl.program_id(0)

        @pl.when(i == 0)
        def _():
            dw[...] = jnp.zeros_like(dw)
            dg[...] = jnp.zeros_like(dg)

        drws_v = drws_ref[...]
        rid = lax.broadcasted_iota(jnp.int32, drws_v.shape, 0)
        nxt_row = jnp.where(i == n_t - 1, 0.0, nxt_ref[0:1, :])
        nxt = jnp.where(rid == tt - 1, jnp.broadcast_to(nxt_row, drws_v.shape), pltpu.roll(drws_v, tt - 1, axis=0))
        muv = mu_ref[...]
        drw = drws_v * (1.0 - muv) + nxt * muv
        dproj[:, 0:D_S5] = du_ref[...].astype(BF16)
        dproj[:, D_S5:2 * D_S5] = dzs_ref[...].astype(BF16)
        dproj[:, 2 * D_S5:2 * D_S5 + D_SHIFT] = drw.astype(BF16)
        dproj[:, 2 * D_S5 + D_SHIFT:D_IN] = dzr_ref[...].astype(BF16)
        dh = _dot(dproj[...], w_ref[...], ((1,), (1,)), None)
        h, vjp = jax.vjp(_rms, x_ref[...], g_ref[...])
        dxh, dgv = vjp(dh)
        gx_o[...] = dx2_ref[...] + dxh
        dg[...] += dgv
        dw[...] += _dot(h.astype(BF16), dproj[...], ((0,), (0,)), None)

        @pl.when(i == n_t - 1)
        def _():
            dg_o[...] = dg[...]
            pltpu.sync_copy(dw, dw_o)

    row = lambda n: pl.BlockSpec((tt, n), lambda i: (i, 0))
    nxt = pl.BlockSpec((8, D_SHIFT), lambda i: (jnp.minimum((i + 1) * (tt // 8), L // 8 - 1), 0))
    return pl.pallas_call(
        body, name="bwd_in", grid=(n_t,),
        in_specs=[row(D_MODEL), _const_spec((1, D_MODEL)), _const_spec((D_MODEL, D_IN)), _const_spec((1, D_SHIFT)),
                  row(D_MODEL), row(D_S5), row(D_S5), row(D_SHIFT), nxt, row(D_RWKV)],
        out_specs=[row(D_MODEL), _ANY, _acc_spec((1, D_MODEL))],
        out_shape=[_sds((L, D_MODEL)), _sds((D_MODEL, D_IN)), _sds((1, D_MODEL))],
        scratch_shapes=[pltpu.VMEM((tt, D_IN), BF16), pltpu.VMEM((D_MODEL, D_IN), F32), pltpu.VMEM((1, D_MODEL), F32)],
        compiler_params=_params("arbitrary"),
    )(x, norm_g, w_in_bf, mu, dx2, du, dzs, drws, drws, dzr)


def _block_diag_b(bbar):
    bb = bbar.reshape(S5_GROUP, S5_BLOCKS, 8, S5_STATE)
    return jnp.einsum('hqgp,Gg->qGhgp', bb, jnp.eye(8, dtype=F32)).reshape(S5_BLOCKS, LANES, 512)


def _block_diag_b_t(db4):
    d = db4.reshape(S5_BLOCKS, 8, S5_GROUP, 8, S5_STATE)
    return jnp.einsum('qGhgp,Gg->hqgp', d, jnp.eye(8, dtype=F32)).reshape(S5_GROUP, N_STATE)


def _block_diag_c(c):
    cc = c.reshape(S5_BLOCKS, 8, S5_GROUP, S5_STATE)
    return jnp.einsum('qghp,gG->qgpGh', cc, jnp.eye(8, dtype=F32)).reshape(S5_BLOCKS, 512, LANES)


def _block_diag_c_t(dc4):
    d = dc4.reshape(S5_BLOCKS, 8, S5_STATE, 8, S5_GROUP)
    return jnp.einsum('qgpGh,gG->qghp', d, jnp.eye(8, dtype=F32)).reshape(S5_GROUPS, S5_GROUP, S5_STATE)


def _local_step(x, tgt, w):
    L = x.shape[0]
    tt = min(512, L)
    tp = min(256, L)
    ee = _head_sum_matrix()

    lam_re = w['s5_lam_re'].reshape(1, N_STATE)
    lam_im = w['s5_lam_im'].reshape(1, N_STATE)
    logdt = jnp.repeat(w['s5_log_dt'], S5_STATE).reshape(1, N_STATE)
    b_re_t = w['s5_b_re'].transpose(2, 0, 1).reshape(S5_GROUP, N_STATE)
    b_im_t = w['s5_b_im'].transpose(2, 0, 1).reshape(S5_GROUP, N_STATE)
    bbr, bbi, pow_r, pow_i, rpow_r, rpow_i = _s5_param_fwd(lam_re, lam_im, logdt, b_re_t, b_im_t)
    b4_re, b4_im = _block_diag_b(bbr), _block_diag_b(bbi)
    c4_re, c4_im = _block_diag_c(w['s5_c_re']), _block_diag_c(w['s5_c_im'])

    norm_g = w['norm_g'].reshape(1, D_MODEL)
    w_in_bf = w['w_in'].astype(BF16)
    u, zs, rw, zr = _fwd_in(x, norm_g, w_in_bf, tt)
    s_re, s_im, y_ssm = _s5_fwd(u, b4_re, b4_im, c4_re, c4_im, pow_r, pow_i, tt)

    row = lambda t: t.reshape(1, -1)
    zpad = jnp.zeros((HEAD, D_RWKV), F32)
    w2p = jnp.concatenate([w['rwkv_w2'], zpad], axis=0)
    a2p = jnp.concatenate([zpad, w['rwkv_a2']], axis=0)
    pre_consts = (row(w['rwkv_mu']), row(w['rwkv_w0']), w2p, row(w['rwkv_a0']), a2p,
                  row(w['rwkv_k_k']), row(w['rwkv_k_a']), ee)
    ops = _rwkv_pre_fwd(rw, *pre_consts, tt)
    ysc, states, *kept = _rwkv_scan_fwd(ops)

    post = _post(x, u, zs, zr, ysc, ops[0], ops[2], ops[3], y_ssm,
                 row(w['s5_d']), w['s5_glu_w'], row(w['s5_glu_b']), row(w['rwkv_ln_w']), row(w['rwkv_ln_b']),
                 row(w['rwkv_r_k']), w['w_out'], row(w['final_g']), tgt, ee, tp)
    (dx2, du_d, dzs, dzr, dysc, dr_b, dk2_b, dv_b, dy_ssm,
     dd, dglu_w, dglu_b, dln_w, dln_b, dr_k, dw_out, dgf, loss) = post

    du, db4_re, db4_im, dc4_re, dc4_im, dlbr, dlbi = _s5_bwd(
        u, du_d, dy_ssm, s_re, s_im, b4_re, b4_im, c4_re, c4_im, rpow_r, rpow_i, tt)
    group_ind = (jnp.arange(N_STATE)[:, None] // S5_STATE == jnp.arange(LANES)[None, :]).astype(F32)
    dlam_re, dlam_im, dlogdt, db_re_t, db_im_t = _s5_param_bwd(
        lam_re, lam_im, logdt, b_re_t, b_im_t, dlbr, dlbi, _block_diag_b_t(db4_re), _block_diag_b_t(db4_im), group_ind)

    cots = list(_rwkv_scan_bwd(ops, states, kept, dysc)) + [dr_b, dk2_b, dv_b]
    drws, dmu, dw0, dw2p, da0, da2p, dk_k, dk_a = _rwkv_pre_bwd(rw, *pre_consts, cots, tt)

    grad_x, dw_in, dnorm_g = _bwd_in(x, norm_g, w_in_bf, row(w['rwkv_mu']), dx2, du, dzs, drws, dzr, tt)

    unb = lambda t: t.reshape(S5_GROUP, S5_GROUPS, S5_STATE).transpose(1, 2, 0)
    grads = {
        'norm_g': dnorm_g.reshape(D_MODEL), 'w_in': dw_in,
        's5_lam_re': dlam_re.reshape(S5_GROUPS, S5_STATE), 's5_lam_im': dlam_im.reshape(S5_GROUPS, S5_STATE),
        's5_log_dt': dlogdt[0, :S5_GROUPS], 's5_b_re': unb(db_re_t), 's5_b_im': unb(db_im_t),
        's5_c_re': _block_diag_c_t(dc4_re), 's5_c_im': _block_diag_c_t(dc4_im),
        's5_d': dd.reshape(D_S5), 's5_glu_w': dglu_w, 's5_glu_b': dglu_b.reshape(D_S5),
        'rwkv_mu': dmu.reshape(-1), 'rwkv_w0': dw0.reshape(-1), 'rwkv_w2': dw2p[:HEAD], 'rwkv_a0': da0.reshape(-1),
        'rwkv_a2': da2p[HEAD:], 'rwkv_k_k': dk_k.reshape(-1), 'rwkv_k_a': dk_a.reshape(-1),
        'rwkv_r_k': dr_k.reshape(N_HEADS, HEAD), 'rwkv_ln_w': dln_w.reshape(-1), 'rwkv_ln_b': dln_b.reshape(-1),
        'w_out': dw_out, 'final_g': dgf.reshape(D_MODEL),
    }
    return loss, grad_x, grads


def _exchange(arrays, gather, axes, name):
    n = len(arrays)
    group = 2 ** len(axes)

    def body(*refs):
        send_refs, recv_refs = refs[:n], refs[n:2 * n]
        send_sems, recv_sems, local_sems = refs[2 * n:]
        pos = {ax: lax.axis_index(ax) for ax in ("x", "y", "c")}

        def index_of(p):
            idx = 0
            for ax in axes:
                idx = 2 * idx + p[ax]
            return idx

        me = index_of(pos)
        own, outs, arrivals = [], [], []
        for i, (send_ref, recv_ref) in enumerate(zip(send_refs, recv_refs)):
            def block_for(dev, send_ref=send_ref, whole=gather[i]):
                return send_ref if whole else send_ref.at[dev]

            own.append(pltpu.make_async_copy(block_for(me), recv_ref.at[me], local_sems.at[i]))
            own[-1].start()
            for k in range(1, group):
                peer = dict(pos)
                for bit, ax in enumerate(axes):
                    if (k >> bit) & 1:
                        peer[ax] = 1 - pos[ax]
                peer_idx = index_of(peer)
                sems = dict(send_sem=send_sems.at[i, k - 1], recv_sem=recv_sems.at[i, k - 1],
                            device_id=(peer["x"], peer["y"], peer["c"]), device_id_type=pl.DeviceIdType.MESH)
                outs.append(pltpu.make_async_remote_copy(src_ref=block_for(peer_idx), dst_ref=recv_ref.at[me], **sems))
                outs[-1].start()
                arrivals.append(
                    pltpu.make_async_remote_copy(src_ref=block_for(peer_idx), dst_ref=recv_ref.at[peer_idx], **sems))
        for copy in arrivals:
            copy.wait_recv()
        for copy in outs:
            copy.wait_send()
        for copy in own:
            copy.wait()

    return pl.pallas_call(
        body, name=name, in_specs=[_ANY] * n, out_specs=[_ANY] * n,
        out_shape=[jax.ShapeDtypeStruct(((group,) + a.shape) if whole else a.shape, a.dtype)
                   for a, whole in zip(arrays, gather)],
        scratch_shapes=[pltpu.SemaphoreType.DMA((n, group - 1)), pltpu.SemaphoreType.DMA((n, group - 1)),
                        pltpu.SemaphoreType.DMA((n,))],
        compiler_params=pltpu.CompilerParams(has_side_effects=True),
    )(*arrays)


def _sum_devices(ref):
    g = ref[0].astype(F32)
    for s in range(1, ref.shape[0]):
        g = g + ref[s].astype(F32)
    return g


def _adamw_math(g, w, m, v):
    m_new = ADAM_B1 * m + (1.0 - ADAM_B1) * g
    v_new = ADAM_B2 * v + (1.0 - ADAM_B2) * (g * g)
    m_hat = m_new / (1.0 - ADAM_B1 ** ADAM_STEP)
    v_hat = v_new / (1.0 - ADAM_B2 ** ADAM_STEP)
    return -ADAM_LR * (m_hat / (jnp.sqrt(v_hat) + ADAM_EPS) + ADAM_WD * w), m_new, v_new


def _adamw(gs, ws, ms, vs, reduce, name):
    n = len(ws)

    def body(*refs):
        g_refs, w_refs, m_refs, v_refs = (refs[j * n:(j + 1) * n] for j in range(4))
        outs = refs[4 * n:]
        for i in range(n):
            g = _sum_devices(g_refs[i]) if reduce else g_refs[i][...]
            res = _adamw_math(g, w_refs[i][...], m_refs[i][...], v_refs[i][...])
            for j, val in enumerate(((g,) if reduce else ()) + res):
                outs[j * n + i][...] = val

    return pl.pallas_call(
        body, name=name, out_shape=[_sds(w.shape) for w in ws] * (4 if reduce else 3),
        compiler_params=pltpu.CompilerParams(vmem_limit_bytes=VMEM_LIMIT),
    )(*gs, *ws, *ms, *vs)


def _sum_blocks(recv):
    def body(recv_ref, out_ref):
        out_ref[...] = _sum_devices(recv_ref)

    return pl.pallas_call(body, name="sum_small_grads", out_shape=_sds(recv.shape[1:]))(recv)


_WEIGHTS = [
    ('norm_g', (1, 1024), False), ('w_in', (1, 1024, 400), True), ('s5_lam_re', (1, 32, 64), False),
    ('s5_lam_im', (1, 32, 64), False), ('s5_log_dt', (1, 32), False), ('s5_b_re', (1, 32, 64, 16), False),
    ('s5_b_im', (1, 32, 64, 16), False), ('s5_c_re', (1, 32, 16, 64), False), ('s5_c_im', (1, 32, 16, 64), False),
    ('s5_d', (1, 512), False), ('s5_glu_w', (1, 64, 512), True), ('s5_glu_b', (1, 512), False),
    ('rwkv_mu', (1, 1664), False), ('rwkv_w0', (1, 512), False), ('rwkv_w2', (1, 64, 64), True),
    ('rwkv_a0', (1, 512), False), ('rwkv_a2', (1, 64, 64), True), ('rwkv_k_k', (1, 512), False),
    ('rwkv_k_a', (1, 512), False), ('rwkv_r_k', (1, 8, 64), False), ('rwkv_ln_w', (1, 512), False),
    ('rwkv_ln_b', (1, 512), False), ('w_out', (1, 128, 1024), True), ('final_g', (1024,), False),
]
_SHARDED = [(n, s) for n, s, sharded in _WEIGHTS if sharded]
_SMALL = [(n, s) for n, s, sharded in _WEIGHTS if not sharded]
_COLUMN_SHARDED = ('w_in', 'rwkv_w2', 'rwkv_a2')
_SMALL_SIZE = sum(math.prod(s) for _, s in _SMALL) + 1
_SMALL_ROWS = -(-_SMALL_SIZE // (8 * LANES)) * 8


def _pack_small(grads, loss):
    flat = [grads[n].reshape(-1) for n, _ in _SMALL] + [loss.reshape(1)]
    pad = _SMALL_ROWS * LANES - _SMALL_SIZE
    return jnp.concatenate(flat + [jnp.zeros((pad,), F32)]).reshape(_SMALL_ROWS, LANES)


def _unpack_small(packed):
    flat = packed.reshape(-1)
    out, off = {}, 0
    for n, s in _SMALL:
        size = math.prod(s)
        out[n] = flat[off:off + size].reshape(s)
        off += size
    return out, flat[off]


_BF16_OPERANDS = ('w_in', 's5_glu_w', 'w_out')


def _join_shards(name, blocks):
    _, rows, cols = blocks.shape
    if name in _COLUMN_SHARDED:
        return blocks.transpose(1, 0, 2).reshape(rows, N_DEV * cols)
    return blocks.reshape(N_DEV * rows, cols)


def _split_shards(name, full, shard_shape):
    rows, cols = shard_shape
    if name in _COLUMN_SHARDED:
        return full.reshape(rows, N_DEV, cols).transpose(1, 0, 2)
    return full.reshape(N_DEV, rows, cols)


def kernel(x, norm_g, w_in, s5_lam_re, s5_lam_im, s5_log_dt, s5_b_re, s5_b_im, s5_c_re, s5_c_im, s5_d, s5_glu_w, s5_glu_b, rwkv_mu, rwkv_w0, rwkv_w2, rwkv_a0, rwkv_a2, rwkv_k_k, rwkv_k_a, rwkv_r_k, rwkv_ln_w, rwkv_ln_b, w_out, final_g, loss_target, m_norm_g, m_w_in, m_s5_lam_re, m_s5_lam_im, m_s5_log_dt, m_s5_b_re, m_s5_b_im, m_s5_c_re, m_s5_c_im, m_s5_d, m_s5_glu_w, m_s5_glu_b, m_rwkv_mu, m_rwkv_w0, m_rwkv_w2, m_rwkv_a0, m_rwkv_a2, m_rwkv_k_k, m_rwkv_k_a, m_rwkv_r_k, m_rwkv_ln_w, m_rwkv_ln_b, m_w_out, m_final_g, v_norm_g, v_w_in, v_s5_lam_re, v_s5_lam_im, v_s5_log_dt, v_s5_b_re, v_s5_b_im, v_s5_c_re, v_s5_c_im, v_s5_d, v_s5_glu_w, v_s5_glu_b, v_rwkv_mu, v_rwkv_w0, v_rwkv_w2, v_rwkv_a0, v_rwkv_a2, v_rwkv_k_k, v_rwkv_k_a, v_rwkv_r_k, v_rwkv_ln_w, v_rwkv_ln_b, v_w_out, v_final_g):
    given = dict(locals())

    n_sh = len(_SHARDED)
    everyone = ("x", "y", "c")
    shards = [given[n][0].astype(BF16 if n in _BF16_OPERANDS else F32) for n, _ in _SHARDED]
    gathered = _exchange(shards, (True,) * n_sh, everyone, "gather_weights")
    local = {n: _join_shards(n, blocks).astype(F32 if n != 'w_in' else BF16)
             for (n, _), blocks in zip(_SHARDED, gathered)}
    local.update({n: (given[n][0] if len(s) > 1 else given[n]) for n, s in _SMALL})

    loss, grad_x, grads = _local_step(x[0], loss_target[0], local)

    blocks = [_split_shards(n, grads[n], s[1:]).astype(BF16) for n, s in _SHARDED]
    small = _pack_small(grads, loss[0, 0]).reshape(N_DEV, _SMALL_ROWS // N_DEV, LANES)
    recv = _exchange(blocks + [small], (False,) * (n_sh + 1), everyone, "exchange_grads")
    small_sum = _exchange([_sum_blocks(recv[-1])], (True,), everyone, "gather_small_grads")[0]

    result = {}
    for group, name in (([0], "adamw_w_in"), ([1, 2, 3, 4], "adamw_shards")):
        ns = [_SHARDED[i][0] for i in group]
        res = _adamw([recv[i] for i in group], [given[n][0] for n in ns], [given['m_' + n][0] for n in ns],
                     [given['v_' + n][0] for n in ns], True, name)
        for j, n in enumerate(ns):
            result[n] = [res[k * len(ns) + j][None] for k in range(4)]
    g_small, total = _unpack_small(small_sum)
    two_d = lambda t: t.reshape(1, -1) if t.ndim == 1 else t
    ns = [n for n, _ in _SMALL]
    res = _adamw([two_d(g_small[n]) for n in ns], [two_d(given[n]) for n in ns], [two_d(given['m_' + n]) for n in ns],
                 [two_d(given['v_' + n]) for n in ns], False, "adamw_small")
    for j, (n, s) in enumerate(_SMALL):
        result[n] = [g_small[n]] + [res[k * len(ns) + j].reshape(s) for k in range(3)]

    outs = [total, grad_x[None]]
    for k in range(4):
        outs += [result[n][k] for n, _, _ in _WEIGHTS]
    return tuple(outs)
```

```python
import math

import jax
import jax.numpy as jnp
from jax import lax
from jax.experimental import pallas as pl
from jax.experimental.pallas import tpu as pltpu

F32 = jnp.float32
BF16 = jnp.bfloat16
HI = lax.Precision.HIGH

D_MODEL = 1024
D_S5 = 512
D_RWKV = 512
S5_GROUPS = 32
S5_GROUP = 16
S5_STATE = 64
N_STATE = S5_GROUPS * S5_STATE
N_HEADS = 8
HEAD = 64
D_SHIFT = 3 * D_RWKV + 128
D_IN = 2 * D_S5 + D_SHIFT + D_RWKV
NORM_EPS = 1e-6
GN_EPS = 64e-5
N_DEV = 8
LANES = 128
S5_BLOCKS = 4
RWKV_CHUNK = 64
RWKV_CHUNKS_PER_STEP = 4
VMEM_LIMIT = 56 * 1024 * 1024

ADAM_LR = 0.001
ADAM_B1 = 0.9
ADAM_B2 = 0.999
ADAM_EPS = 1e-08
ADAM_WD = 0.01
ADAM_STEP = 10


def _dot(a, b, dims, prec):
    return lax.dot_general(a, b, (dims, ((), ())), precision=prec, preferred_element_type=F32)


def _dot_bf(a, b, dims):
    return _dot(a.astype(BF16), b.astype(BF16), dims, None)


def _make_mm(cast, prec):
    @jax.custom_vjp
    def mm(a, b):
        return _dot(cast(a), cast(b), ((1,), (0,)), prec)

    def fwd(a, b):
        return mm(a, b), (a, b)

    def bwd(res, g):
        a, b = res
        return (_dot(cast(g), cast(b), ((1,), (1,)), prec), _dot(cast(a), cast(g), ((0,), (0,)), prec))

    mm.defvjp(fwd, bwd)
    return mm


mm_bf = _make_mm(lambda t: t.astype(BF16), None)


def _make_head_sum(split):
    def product(x, ee):
        hi = x.astype(BF16)
        out = _dot(hi, ee, ((1,), (0,)), None)
        if split:
            out = out + _dot((x - hi.astype(F32)).astype(BF16), ee, ((1,), (0,)), None)
        return out

    @jax.custom_vjp
    def head_sum(x, ee):
        return product(x, ee)

    def fwd(x, ee):
        return product(x, ee), ee

    def bwd(ee, g):
        return product(g, ee), jnp.zeros_like(ee)

    head_sum.defvjp(fwd, bwd)
    return head_sum


head_sum = _make_head_sum(False)
head_sum_split = _make_head_sum(True)


@jax.custom_vjp
def _sigmoid(x):
    return 1.0 / (1.0 + jnp.exp(-x))


def _sigmoid_fwd(x):
    s = _sigmoid(x)
    return s, s


_sigmoid.defvjp(_sigmoid_fwd, lambda s, g: (g * s * (1.0 - s),))


@jax.custom_vjp
def _silu(x):
    return x * _sigmoid(x)


def _silu_fwd(x):
    s = _sigmoid(x)
    return x * s, (x, s)


def _silu_bwd(res, g):
    x, s = res
    return (g * s * (1.0 + x * (1.0 - s)),)


_silu.defvjp(_silu_fwd, _silu_bwd)


@jax.custom_vjp
def _softplus(x):
    return jnp.maximum(x, 0.0) + jnp.log(1.0 + jnp.exp(-jnp.abs(x)))


def _softplus_fwd(x):
    e = jnp.exp(-jnp.abs(x))
    return jnp.maximum(x, 0.0) + jnp.log(1.0 + e), (x, e)


def _softplus_bwd(res, g):
    x, e = res
    return (g * jnp.where(x >= 0.0, 1.0, e) / (1.0 + e),)


_softplus.defvjp(_softplus_fwd, _softplus_bwd)


@jax.custom_vjp
def _normalize_heads(x, ee):
    return x / jnp.maximum(jnp.sqrt(head_sum(x * x, ee)), 1e-12)


def _normalize_heads_fwd(x, ee):
    norm = jnp.sqrt(head_sum(x * x, ee))
    inv = 1.0 / jnp.maximum(norm, 1e-12)
    y = x * inv
    return y, (y, inv, norm, ee)


def _normalize_heads_bwd(res, g):
    y, inv, norm, ee = res
    along = jnp.where(norm > 1e-12, head_sum(g * y, ee), 0.0)
    return inv * (g - y * along), jnp.zeros_like(ee)


_normalize_heads.defvjp(_normalize_heads_fwd, _normalize_heads_bwd)


_GELU_C = 2.0 * math.sqrt(2.0 / math.pi)


def _gelu_gate(x):
    return 1.0 / (1.0 + jnp.exp(-_GELU_C * x * (1.0 + 0.044715 * (x * x))))


@jax.custom_vjp
def _gelu(x):
    return x * _gelu_gate(x)


def _gelu_fwd(x):
    s = _gelu_gate(x)
    return x * s, (x, s)


def _gelu_bwd(res, g):
    x, s = res
    return (g * (s + x * s * (1.0 - s) * (_GELU_C * (1.0 + 3.0 * 0.044715 * (x * x)))),)


_gelu.defvjp(_gelu_fwd, _gelu_bwd)


def _rms(x, g):
    return x * lax.rsqrt(jnp.mean(x * x, axis=-1, keepdims=True) + NORM_EPS) * g


def _const_spec(shape):
    nd = len(shape)
    return pl.BlockSpec(shape, lambda *_: (0,) * nd, pipeline_mode=pl.Buffered(1))


def _acc_spec(shape):
    nd = len(shape)
    return pl.BlockSpec(shape, lambda *_: (0,) * nd)


def _params(sem):
    return pltpu.CompilerParams(dimension_semantics=(sem,), vmem_limit_bytes=VMEM_LIMIT)


_ANY = pl.BlockSpec(memory_space=pl.ANY)


def _sds(shape):
    return jax.ShapeDtypeStruct(shape, F32)


def _head_sum_matrix():
    i = jnp.arange(D_RWKV) // HEAD
    return (i[:, None] == i[None, :]).astype(BF16)


def _s5_param_fn(lam_re, lam_im, logdt, b_re, b_im):
    dt = jnp.exp(logdt)
    mag = jnp.exp(lam_re * dt)
    ang = lam_im * dt
    lbr = mag * jnp.cos(ang)
    lbi = mag * jnp.sin(ang)
    nr = lbr - 1.0
    den = lam_re * lam_re + lam_im * lam_im
    cr = (nr * lam_re + lbi * lam_im) / den
    ci = (lbi * lam_re - nr * lam_im) / den
    return lbr, lbi, cr * b_re - ci * b_im, cr * b_im + ci * b_re


def _cmul(ar, ai, br, bi):
    return ar * br - ai * bi, ar * bi + ai * br


def _s5_param_fwd(lam_re, lam_im, logdt, b_re, b_im):
    def body(lr, li, ld, br, bi, o_br, o_bi, o_pr, o_pi, o_qr, o_qi):
        lbr, lbi, bbr, bbi = _s5_param_fn(lr[...], li[...], ld[...], br[...], bi[...])
        o_br[...] = bbr
        o_bi[...] = bbi
        rid = lax.broadcasted_iota(jnp.int32, (8, N_STATE), 0)
        pr, pi_ = lbr, lbi
        fwd_r = rev_r = jnp.broadcast_to(pr, (8, N_STATE))
        fwd_i = rev_i = jnp.broadcast_to(pi_, (8, N_STATE))
        for j in range(1, 8):
            pr, pi_ = _cmul(pr, pi_, lbr, lbi)
            fwd_r = jnp.where(rid == j, jnp.broadcast_to(pr, (8, N_STATE)), fwd_r)
            fwd_i = jnp.where(rid == j, jnp.broadcast_to(pi_, (8, N_STATE)), fwd_i)
            rev_r = jnp.where(rid == 7 - j, jnp.broadcast_to(pr, (8, N_STATE)), rev_r)
            rev_i = jnp.where(rid == 7 - j, jnp.broadcast_to(pi_, (8, N_STATE)), rev_i)
        o_pr[...] = fwd_r
        o_pi[...] = fwd_i
        o_qr[...] = rev_r
        o_qi[...] = -rev_i

    return pl.pallas_call(
        body, name="s5_param_fwd",
        out_shape=[_sds((S5_GROUP, N_STATE))] * 2 + [_sds((8, N_STATE))] * 4,
    )(lam_re, lam_im, logdt, b_re, b_im)


def _s5_param_bwd(lam_re, lam_im, logdt, b_re, b_im, d_lbr, d_lbi, d_bbr, d_bbi, group_ind):
    def body(lr, li, ld, br, bi, g0, g1, g2, g3, ind, o_lr, o_li, o_ld, o_br, o_bi):
        _, vjp = jax.vjp(_s5_param_fn, lr[...], li[...], ld[...], br[...], bi[...])
        d_lr, d_li, d_ld, d_br, d_bi = vjp((g0[...], g1[...], g2[...], g3[...]))
        o_lr[...] = d_lr
        o_li[...] = d_li
        o_ld[...] = _dot(jnp.broadcast_to(d_ld, (8, N_STATE)), ind[...], ((1,), (0,)), HI)
        o_br[...] = d_br
        o_bi[...] = d_bi

    return pl.pallas_call(
        body, name="s5_param_bwd",
        out_shape=[_sds((1, N_STATE))] * 2 + [_sds((8, LANES))] + [_sds((S5_GROUP, N_STATE))] * 2,
    )(lam_re, lam_im, logdt, b_re, b_im, d_lbr, d_lbi, d_bbr, d_bbi, group_ind)


def _fwd_in(x, norm_g, w_in_bf, tt):
    L = x.shape[0]

    def body(x_ref, g_ref, w_ref, u_ref, zs_ref, rw_ref, zr_ref):
        h = _rms(x_ref[...], g_ref[...])
        proj = jnp.dot(h.astype(BF16), w_ref[...], preferred_element_type=F32)
        u_ref[...] = proj[:, 0:D_S5]
        zs_ref[...] = proj[:, D_S5:2 * D_S5]
        rw_ref[...] = proj[:, 2 * D_S5:2 * D_S5 + D_SHIFT]
        zr_ref[...] = proj[:, 2 * D_S5 + D_SHIFT:D_IN]

    row = lambda n: pl.BlockSpec((tt, n), lambda i: (i, 0))
    return pl.pallas_call(
        body, name="fwd_in", grid=(L // tt,),
        in_specs=[row(D_MODEL), _const_spec((1, D_MODEL)), _const_spec((D_MODEL, D_IN))],
        out_specs=[row(D_S5), row(D_S5), row(D_SHIFT), row(D_RWKV)],
        out_shape=[_sds((L, D_S5)), _sds((L, D_S5)), _sds((L, D_SHIFT)), _sds((L, D_RWKV))],
        compiler_params=_params("parallel"),
    )(x, norm_g, w_in_bf)


S5_LANE_CHUNK = 512


def _tile_scan(re_ref, im_ref, pow_r_ref, pow_i_ref, carry_r_ref, carry_i_ref, reverse):
    t, n = re_ref.shape
    n_groups = t // 8
    ch = S5_LANE_CHUNK
    rid = lax.broadcasted_iota(jnp.int32, (8, ch), 0)
    for c in range(n // ch):
        cols = slice(c * ch, (c + 1) * ch)
        pow_r = pow_r_ref[:, cols]
        pow_i = pow_i_ref[:, cols]
        row = lambda tile, j: jnp.broadcast_to(tile[j:j + 1], (8, ch))
        levels = []
        for d in (1, 2, 4):
            keep = (rid < 8 - d) if reverse else (rid >= d)
            j = 8 - d if reverse else d - 1
            levels.append(((8 - d) if reverse else d,
                           jnp.where(keep, row(pow_r, j), 0.0), jnp.where(keep, row(pow_i, j), 0.0)))

        def group(g, carry):
            r0 = pl.multiple_of(((n_groups - 1 - g) if reverse else g) * 8, 8)
            xr = re_ref[pl.ds(r0, 8), cols]
            xi = im_ref[pl.ds(r0, 8), cols]
            for shift, lr, li in levels:
                mr, mi = _cmul(lr, li, pltpu.roll(xr, shift, axis=0), pltpu.roll(xi, shift, axis=0))
                xr = xr + mr
                xi = xi + mi
            mr, mi = _cmul(pow_r, pow_i, carry[0], carry[1])
            xr = xr + mr
            xi = xi + mi
            re_ref[pl.ds(r0, 8), cols] = xr
            im_ref[pl.ds(r0, 8), cols] = xi
            last = 0 if reverse else 7
            return row(xr, last), row(xi, last)

        out = lax.fori_loop(0, n_groups, group, (carry_r_ref[:, cols], carry_i_ref[:, cols]))
        carry_r_ref[:, cols] = out[0]
        carry_i_ref[:, cols] = out[1]


def _s5_fwd(u, b4_re, b4_im, c4_re, c4_im, pow_r, pow_i, tt):
    L = u.shape[0]

    def body(u_ref, bre_ref, bim_ref, cre_ref, cim_ref, pr_ref, pi_ref, sre_o, sim_o, y_o, car_r, car_i):
        @pl.when(pl.program_id(0) == 0)
        def _():
            car_r[...] = jnp.zeros_like(car_r)
            car_i[...] = jnp.zeros_like(car_i)

        uv = u_ref[...]
        for q in range(S5_BLOCKS):
            uq = uv[:, q * LANES:(q + 1) * LANES]
            cols = slice(q * 512, (q + 1) * 512)
            sre_o[:, cols] = _dot_bf(uq, bre_ref[q], ((1,), (0,)))
            sim_o[:, cols] = _dot_bf(uq, bim_ref[q], ((1,), (0,)))
        _tile_scan(sre_o, sim_o, pr_ref, pi_ref, car_r, car_i, reverse=False)
        for q in range(S5_BLOCKS):
            cols = slice(q * 512, (q + 1) * 512)
            y_o[:, q * LANES:(q + 1) * LANES] = (_dot_bf(sre_o[:, cols], cre_ref[q], ((1,), (0,)))
                                                 - _dot_bf(sim_o[:, cols], cim_ref[q], ((1,), (0,))))

    row = lambda n: pl.BlockSpec((tt, n), lambda i: (i, 0))
    return pl.pallas_call(
        body, name="s5_fwd", grid=(L // tt,),
        in_specs=[row(D_S5)] + [_const_spec((S5_BLOCKS, LANES, 512))] * 2 + [_const_spec((S5_BLOCKS, 512, LANES))] * 2
        + [_const_spec((8, N_STATE))] * 2,
        out_specs=[row(N_STATE), row(N_STATE), row(D_S5)],
        out_shape=[_sds((L, N_STATE)), _sds((L, N_STATE)), _sds((L, D_S5))],
        scratch_shapes=[pltpu.VMEM((8, N_STATE), F32)] * 2,
        compiler_params=_params("arbitrary"),
    )(u, b4_re, b4_im, c4_re, c4_im, pow_r, pow_i)


def _rwkv_pre_fn(r, k, v, wa, w0, w2p, a0, a2p, k_k, k_a, ee):
    w = -_softplus(-(w0 + mm_bf(jnp.tanh(wa), w2p))) - 0.5
    logw = -jnp.exp(w)
    a = _sigmoid(a0 + mm_bf(wa, a2p))
    kk = _normalize_heads(k * k_k, ee)
    k2 = k * (1.0 + (a - 1.0) * k_a)
    return r, logw, k2, v, -kk, kk * a


N_PAIRS = N_HEADS // 2


def _head_spec(tt):
    return pl.BlockSpec((N_PAIRS, tt, LANES), lambda i: (0, i, 0))


def _load_heads(ref):
    return jnp.concatenate([ref[p] for p in range(N_PAIRS)], axis=-1)


def _store_heads(ref, val):
    for p in range(N_PAIRS):
        ref[p] = val[:, p * LANES:(p + 1) * LANES]


def _split_pairs(x):
    return jnp.concatenate([x[:, :, :HEAD], x[:, :, HEAD:]], axis=0)


def _join_pairs(x):
    return jnp.concatenate([x[:N_PAIRS], x[N_PAIRS:]], axis=-1)


def _shifted(rw, prev_blk, first):
    rolled = pltpu.roll(rw, 1, axis=0)
    prev_row = jnp.where(first, 0.0, prev_blk[7:8, :])
    rid = lax.broadcasted_iota(jnp.int32, rw.shape, 0)
    return jnp.where(rid == 0, jnp.broadcast_to(prev_row, rw.shape), rolled)


def _split_rw(t):
    return t[:, 0:512], t[:, 512:1024], t[:, 1024:1536], t[:, 1536:1664]


def _rwkv_pre_specs(tt):
    row = pl.BlockSpec((tt, D_SHIFT), lambda i: (i, 0))
    prev = pl.BlockSpec((8, D_SHIFT), lambda i: (jnp.maximum(i * (tt // 8) - 1, 0), 0))
    consts = [_const_spec((1, D_SHIFT)), _const_spec((1, D_RWKV)), _const_spec((LANES, D_RWKV)),
              _const_spec((1, D_RWKV)), _const_spec((LANES, D_RWKV)), _const_spec((1, D_RWKV)),
              _const_spec((1, D_RWKV)), _const_spec((D_RWKV, D_RWKV))]
    return [row, prev] + consts


def _rwkv_pre_fwd(rw, mu, w0, w2p, a0, a2p, k_k, k_a, ee, tt):
    L = rw.shape[0]

    def body(rw_ref, prev_ref, mu_ref, w0_ref, w2_ref, a0_ref, a2_ref, kk_ref, ka_ref, ee_ref, *outs):
        rwv = rw_ref[...]
        rws = rwv + (_shifted(rwv, prev_ref[...], pl.program_id(0) == 0) - rwv) * mu_ref[...]
        res = _rwkv_pre_fn(*_split_rw(rws), w0_ref[...], w2_ref[...], a0_ref[...], a2_ref[...],
                           kk_ref[...], ka_ref[...], ee_ref[...])
        for o, val in zip(outs, res):
            _store_heads(o, val)

    return pl.pallas_call(
        body, name="rwkv_pre_fwd", grid=(L // tt,),
        in_specs=_rwkv_pre_specs(tt), out_specs=[_head_spec(tt)] * 6, out_shape=[_sds((N_PAIRS, L, LANES))] * 6,
        compiler_params=_params("parallel"),
    )(rw, rw, mu, w0, w2p, a0, a2p, k_k, k_a, ee)


def _rwkv_pre_bwd(rw, mu, w0, w2p, a0, a2p, k_k, k_a, ee, cots, tt):
    L = rw.shape[0]
    n_t = L // tt

    def body(rw_ref, prev_ref, mu_ref, w0_ref, w2_ref, a0_ref, a2_ref, kk_ref, ka_ref, ee_ref,
             c_r, c_w, c_k, c_v, c_a, c_b, cb_r, cb_k, cb_v,
             drws_ref, dmu_o, dw0_o, dw2_o, da0_o, da2_o, dkk_o, dka_o,
             dmu, dw0, dw2, da0, da2, dkk, dka):
        i = pl.program_id(0)
        accs = (dmu, dw0, dw2, da0, da2, dkk, dka)

        @pl.when(i == 0)
        def _():
            for acc in accs:
                acc[...] = jnp.zeros_like(acc)

        rwv = rw_ref[...]
        diff = _shifted(rwv, prev_ref[...], i == 0) - rwv
        rws = rwv + diff * mu_ref[...]
        consts = (w0_ref[...], w2_ref[...], a0_ref[...], a2_ref[...], kk_ref[...], ka_ref[...])
        _, vjp = jax.vjp(lambda *a: _rwkv_pre_fn(*a, ee_ref[...]), *_split_rw(rws), *consts)
        scan = [_load_heads(c) for c in (c_r, c_w, c_k, c_v, c_a, c_b)]
        g = vjp((scan[0] + cb_r[...], scan[1], scan[2] + cb_k[...], scan[3] + cb_v[...], scan[4], scan[5]))
        drws = jnp.concatenate(g[0:4], axis=-1)
        drws_ref[...] = drws
        dmu[...] += jnp.sum(drws * diff, axis=0, keepdims=True)
        for acc, val in zip(accs[1:], g[4:]):
            acc[...] += val

        @pl.when(i == n_t - 1)
        def _():
            for acc, out in zip(accs, (dmu_o, dw0_o, dw2_o, da0_o, da2_o, dkk_o, dka_o)):
                out[...] = acc[...]

    row = pl.BlockSpec((tt, D_RWKV), lambda i: (i, 0))
    shapes = [(1, D_SHIFT), (1, D_RWKV), (LANES, D_RWKV), (1, D_RWKV), (LANES, D_RWKV), (1, D_RWKV), (1, D_RWKV)]
    return pl.pallas_call(
        body, name="rwkv_pre_bwd", grid=(n_t,),
        in_specs=_rwkv_pre_specs(tt) + [_head_spec(tt)] * 6 + [row] * 3,
        out_specs=[pl.BlockSpec((tt, D_SHIFT), lambda i: (i, 0))] + [_acc_spec(s) for s in shapes],
        out_shape=[_sds((L, D_SHIFT))] + [_sds(s) for s in shapes],
        scratch_shapes=[pltpu.VMEM(s, F32) for s in shapes],
        compiler_params=_params("arbitrary"),
    )(rw, rw, mu, w0, w2p, a0, a2p, k_k, k_a, ee, *cots)


def _bmm(a, b):
    return lax.dot_general(a, b, (((2,), (1,)), ((0,), (0,))), precision=HI, preferred_element_type=F32)


def _bmm_nt(a, b):
    return lax.dot_general(a, b, (((2,), (2,)), ((0,), (0,))), precision=HI, preferred_element_type=F32)


def _bmm_tn(a, b):
    return lax.dot_general(a, b, (((1,), (1,)), ((0,), (0,))), precision=HI, preferred_element_type=F32)


def _bdot_bf(a, b, lhs_dim, rhs_dim):
    return lax.dot_general(a.astype(BF16), b.astype(BF16), (((lhs_dim,), (rhs_dim,)), ((0,), (0,))),
                           preferred_element_type=F32)


@jax.custom_vjp
def _bmm_bf(a, b):
    return _bdot_bf(a, b, 2, 1)


def _bmm_bf_fwd(a, b):
    return _bmm_bf(a, b), (a, b)


def _bmm_bf_bwd(res, g):
    a, b = res
    return _bdot_bf(g, b, 2, 2), _bdot_bf(a, g, 1, 1)


_bmm_bf.defvjp(_bmm_bf_fwd, _bmm_bf_bwd)


@jax.custom_vjp
def _bmm_tn_bf(a, b):
    return _bdot_bf(a, b, 1, 1)


def _bmm_tn_bf_fwd(a, b):
    return _bmm_tn_bf(a, b), (a, b)


def _bmm_tn_bf_bwd(res, g):
    a, b = res
    return _bdot_bf(b, g, 2, 2), _bdot_bf(a, g, 2, 1)


_bmm_tn_bf.defvjp(_bmm_tn_bf_fwd, _bmm_tn_bf_bwd)


def _unit_lower_inverse(a):
    t = a.shape[-1]
    ti = lax.broadcasted_iota(jnp.int32, (t, t), 0)
    si = lax.broadcasted_iota(jnp.int32, (t, t), 1)

    def same_block(bits):
        shift = jnp.int32(bits)
        return (lax.shift_right_logical(ti, shift) == lax.shift_right_logical(si, shift))[None]

    def mm(x, y):
        return _bdot_bf(x, y, 2, 1)

    d = jnp.where(same_block(3), a, 0.0)
    inv = jnp.where(ti == si, 1.0, 0.0)[None] + d
    pw = mm(d, d)
    both = mm(jnp.concatenate([inv, pw], axis=1), pw)
    inv = inv + both[:, :t]
    inv = inv + mm(inv, both[:, t:])
    bits = 3
    while (1 << bits) < t:
        e = jnp.where(same_block(bits), 0.0, jnp.where(same_block(bits + 1), a, 0.0))
        inv = inv + mm(mm(inv, e), inv)
        bits += 1
    return inv


def _tri_mask(t):
    ri = lax.broadcasted_iota(jnp.int32, (2 * t, 2 * t), 0)
    ci = lax.broadcasted_iota(jnp.int32, (2 * t, 2 * t), 1)
    top_rows = ri < t
    diff = jnp.where(top_rows, ri, ri - t) - jnp.where(ci < t, ci, ci - t)
    return (diff >= jnp.where(top_rows, 1, 0))[None]


def _ones_tri(n_h, t):
    ti = lax.broadcasted_iota(jnp.int32, (t, t), 0)
    si = lax.broadcasted_iota(jnp.int32, (t, t), 1)
    return jnp.broadcast_to(jnp.where(ti >= si, 1.0, 0.0)[None], (n_h, t, t))


@jax.custom_vjp
def _running_sum_kept(logw, kept):
    return kept


def _running_sum_kept_bwd(shape, g):
    return _bmm_tn(_ones_tri(shape[0], shape[1]), g), jnp.zeros_like(g)


_running_sum_kept.defvjp(lambda logw, kept: (kept, logw.shape), _running_sum_kept_bwd)


@jax.custom_vjp
def _tri_products_kept(ar, bk, kept):
    return kept


def _tri_products_kept_bwd(res, g):
    ar, bk = res
    g = jnp.where(_tri_mask(ar.shape[1] // 2), g, 0.0)
    return _bmm(g, bk), _bmm_tn(g, ar), jnp.zeros_like(g)


_tri_products_kept.defvjp(lambda ar, bk, kept: (kept, (ar, bk)), _tri_products_kept_bwd)


@jax.custom_vjp
def _solve_unit_lower(a, rhs, inv, kept=None):
    return _bmm(inv, rhs) if kept is None else kept


def _solve_fwd(a, rhs, inv, kept=None):
    u = _bmm(inv, rhs) if kept is None else kept
    return u, (inv, u, kept is not None)


def _solve_bwd(res, du):
    inv, u, had_kept = res
    d_rhs = _bmm_tn(inv, du)
    return _bmm_nt(d_rhs, u), d_rhs, jnp.zeros_like(inv), (jnp.zeros_like(u) if had_kept else None)


_solve_unit_lower.defvjp(_solve_fwd, _solve_bwd)


def _rwkv_chunk(st0, r, logw, k, v, a, b, kept=None):
    n_h, t, _ = r.shape
    log_p = _bmm(_ones_tri(n_h, t), logw) if kept is None else _running_sum_kept(logw, kept[0])
    p_in = jnp.exp(log_p)
    p_inv = jnp.exp(-log_p)
    at = a * jnp.exp(log_p - logw)
    rt = r * p_in
    ar = jnp.concatenate([at, rt], axis=1)
    bk = jnp.concatenate([b * p_inv, k * p_inv], axis=1)
    if kept is None:
        m = jnp.where(_tri_mask(t), _bmm_nt(ar, bk), 0.0)
        inv = _unit_lower_inverse(m[:, :t, :t])
    else:
        m = _tri_products_kept(ar, bk, kept[1])
        inv = kept[2]
    top, bottom = m[:, :t], m[:, t:]
    rhs = _bmm_bf(jnp.concatenate([at, top[:, :, t:]], axis=2), jnp.concatenate([st0, v], axis=1))
    u = _solve_unit_lower(top[:, :, :t], rhs, inv, None if kept is None else kept[3])
    y = _bmm_bf(jnp.concatenate([rt, bottom], axis=2), jnp.concatenate([st0, u, v], axis=1))
    p_end = jnp.swapaxes(p_in[:, t - 1:t, :], 1, 2)
    st1 = (st0 + _bmm_tn_bf(bk, jnp.concatenate([u, v], axis=1))) * p_end
    return y, st1, (log_p, m, inv, u)


def _rwkv_scan_fwd(ops):
    n_h, L, n = N_HEADS, ops[0].shape[1], HEAD
    t = RWKV_CHUNK
    per = min(RWKV_CHUNKS_PER_STEP, L // t)
    n_c = L // t
    n_s = n_c // per

    def body(r_ref, w_ref, k_ref, v_ref, a_ref, b_ref, y_ref, st_ref, logp_ref, m_ref, inv_ref, u_ref, st):
        @pl.when(pl.program_id(0) == 0)
        def _():
            st[...] = jnp.zeros_like(st)

        st0 = st[...]
        for j in range(per):
            rows = slice(j * t, (j + 1) * t)
            st_ref[j] = st0
            y, st0, (log_p, m, inv, u) = _rwkv_chunk(
                st0, *(_split_pairs(ref[:, rows, :]) for ref in (r_ref, w_ref, k_ref, v_ref, a_ref, b_ref)))
            y_ref[:, rows, :] = _join_pairs(y)
            logp_ref[:, rows, :] = log_p
            u_ref[:, rows, :] = u
            m_ref[j] = m
            inv_ref[j] = inv
        st[...] = st0

    pairs = pl.BlockSpec((N_PAIRS, per * t, LANES), lambda c: (0, c, 0))
    blk = pl.BlockSpec((n_h, per * t, n), lambda c: (0, c, 0))
    per_chunk = lambda m: pl.BlockSpec((per, n_h, m, m), lambda c: (c, 0, 0, 0))
    return pl.pallas_call(
        body, name="rwkv_scan_fwd", grid=(n_s,), in_specs=[pairs] * 6,
        out_specs=[pairs, per_chunk(n), blk, per_chunk(2 * t), per_chunk(t), blk],
        out_shape=[_sds((N_PAIRS, L, LANES)), _sds((n_c, n_h, n, n)), _sds((n_h, L, n)),
                   _sds((n_c, n_h, 2 * t, 2 * t)), _sds((n_c, n_h, t, t)), _sds((n_h, L, n))],
        scratch_shapes=[pltpu.VMEM((n_h, n, n), F32)],
        compiler_params=_params("arbitrary"),
    )(*ops)


def _rwkv_scan_bwd(ops, states, kept, dy):
    n_h, L, n = N_HEADS, ops[0].shape[1], HEAD
    t = RWKV_CHUNK
    per = min(RWKV_CHUNKS_PER_STEP, L // t)
    n_s = L // t // per

    def body(r_ref, w_ref, k_ref, v_ref, a_ref, b_ref, st_ref, logp_ref, m_ref, inv_ref, u_ref, dy_ref,
             dr, dw, dk, dv, da, db, dst):
        @pl.when(pl.program_id(0) == 0)
        def _():
            dst[...] = jnp.zeros_like(dst)

        vjps = []
        for j in range(per):
            rows = slice(j * t, (j + 1) * t)
            have = (logp_ref[:, rows, :], m_ref[j], inv_ref[j], u_ref[:, rows, :])
            args = [_split_pairs(ref[:, rows, :]) for ref in (r_ref, w_ref, k_ref, v_ref, a_ref, b_ref)]
            vjps.append(jax.vjp(lambda *a, have=have: _rwkv_chunk(*a, kept=have)[:2], st_ref[j], *args)[1])
        d_state = dst[...]
        for j in reversed(range(per)):
            rows = slice(j * t, (j + 1) * t)
            g = vjps[j]((_split_pairs(dy_ref[:, rows, :]), d_state))
            d_state = g[0]
            for out, val in zip((dr, dw, dk, dv, da, db), g[1:]):
                out[:, rows, :] = _join_pairs(val)
        dst[...] = d_state

    pairs = pl.BlockSpec((N_PAIRS, per * t, LANES), lambda c: (0, n_s - 1 - c, 0))
    blk = pl.BlockSpec((n_h, per * t, n), lambda c: (0, n_s - 1 - c, 0))
    per_chunk = lambda m: pl.BlockSpec((per, n_h, m, m), lambda c: (n_s - 1 - c, 0, 0, 0))
    return pl.pallas_call(
        body, name="rwkv_scan_bwd", grid=(n_s,),
        in_specs=[pairs] * 6 + [per_chunk(n), blk, per_chunk(2 * t), per_chunk(t), blk, pairs],
        out_specs=[pairs] * 6, out_shape=[_sds((N_PAIRS, L, LANES))] * 6,
        scratch_shapes=[pltpu.VMEM((n_h, n, n), F32)],
        compiler_params=_params("arbitrary"),
    )(*ops, states, *kept, dy)


def _post_fn(x, u, zs, zr, ysc, r, k2, v, y_ssm, d, glu_w, glu_b, ln_w, ln_b, r_k,
             wo_s5, wo_rwkv, gf, tgt, ee):
    y3 = _gelu(y_ssm + d * u)
    y_s5 = y3 * _sigmoid(mm_bf(y3, glu_w) + glu_b) * _silu(zs)
    mean = head_sum_split(ysc, ee) * (1.0 / HEAD)
    yc = ysc - mean
    var = head_sum(yc * yc, ee) * (1.0 / HEAD)
    gn = yc * lax.rsqrt(var + GN_EPS) * ln_w + ln_b
    bonus = head_sum(r * k2 * r_k, ee) * v
    y_rwkv = (gn + bonus) * _silu(zr)
    x2 = x + mm_bf(y_s5, wo_s5) + mm_bf(y_rwkv, wo_rwkv)
    err = _rms(x2, gf) - tgt
    return 0.5 * jnp.mean(err * err, axis=-1, keepdims=True)


def _post(x, u, zs, zr, ysc, r, k2, v, y_ssm, d, glu_w, glu_b, ln_w, ln_b, r_k, w_out, gf, tgt, ee, tt):
    L = x.shape[0]
    n_t = L // tt
    acc_shapes = [(1, D_S5), (D_S5, D_S5), (1, D_S5), (1, D_RWKV), (1, D_RWKV), (1, D_RWKV),
                  (D_MODEL, D_MODEL), (1, D_MODEL), (8, LANES)]

    def body(x_ref, u_ref, zs_ref, zr_ref, ysc_ref, r_ref, k2_ref, v_ref, yssm_ref,
             d_ref, gw_ref, gb_ref, lw_ref, lb_ref, rk_ref, wo_ref, gf_ref, tgt_ref, ee_ref,
             dx_o, du_o, dzs_o, dzr_o, dysc_o, dr_o, dk2_o, dv_o, dyssm_o,
             dd_o, dgw_o, dgb_o, dlw_o, dlb_o, drk_o, dwo_o, dgf_o, loss_o,
             dd, dgw, dgb, dlw, dlb, drk, dwo, dgf, loss):
        i = pl.program_id(0)
        accs = (dd, dgw, dgb, dlw, dlb, drk, dwo, dgf, loss)

        @pl.when(i == 0)
        def _():
            for acc in accs:
                acc[...] = jnp.zeros_like(acc)

        args = (x_ref[...], u_ref[...], zs_ref[...], zr_ref[...],
                _load_heads(ysc_ref), _load_heads(r_ref), _load_heads(k2_ref), _load_heads(v_ref), yssm_ref[...],
                d_ref[...], gw_ref[...], gb_ref[...], lw_ref[...], lb_ref[...], rk_ref[...],
                wo_ref[0:D_S5, :], wo_ref[D_S5:D_MODEL, :], gf_ref[...])
        rows, vjp = jax.vjp(lambda *a: _post_fn(*a, tgt_ref[...], ee_ref[...]), *args)
        g = vjp(jnp.ones_like(rows))
        for out, val in zip((dx_o, du_o, dzs_o, dzr_o), g[0:4]):
            out[...] = val
        _store_heads(dysc_o, g[4])
        for out, val in zip((dr_o, dk2_o, dv_o, dyssm_o), g[5:9]):
            out[...] = val
        for acc, val in zip((dd, dgw, dgb, dlw, dlb, drk), g[9:15]):
            acc[...] += val
        dwo[0:D_S5, :] += g[15]
        dwo[D_S5:D_MODEL, :] += g[16]
        dgf[...] += g[17]
        loss[...] += jnp.broadcast_to(jnp.sum(rows, axis=0, keepdims=True), loss.shape)

        @pl.when(i == n_t - 1)
        def _():
            for acc, out in zip(accs, (dd_o, dgw_o, dgb_o, dlw_o, dlb_o, drk_o, dwo_o, dgf_o, loss_o)):
                pltpu.sync_copy(acc, out)

    row = lambda n: pl.BlockSpec((tt, n), lambda i: (i, 0))
    in_specs = ([row(D_MODEL)] + [row(512)] * 3 + [_head_spec(tt)] * 4 + [row(D_S5)]
                + [_const_spec(s) for s in [(1, D_S5), (D_S5, D_S5), (1, D_S5), (1, D_RWKV), (1, D_RWKV), (1, D_RWKV),
                                            (D_MODEL, D_MODEL), (1, D_MODEL)]]
                + [row(D_MODEL), _const_spec((D_RWKV, D_RWKV))])
    out_rows = [D_MODEL] + [512] * 3 + [None] + [512] * 4
    return pl.pallas_call(
        body, name="post_fwd_bwd", grid=(n_t,), in_specs=in_specs,
        out_specs=[row(n) if n else _head_spec(tt) for n in out_rows] + [_ANY] * len(acc_shapes),
        out_shape=([_sds((L, n)) if n else _sds((N_PAIRS, L, LANES)) for n in out_rows]
                   + [_sds(s) for s in acc_shapes]),
        scratch_shapes=[pltpu.VMEM(s, F32) for s in acc_shapes],
        compiler_params=_params("arbitrary"),
    )(x, u, zs, zr, ysc, r, k2, v, y_ssm, d, glu_w, glu_b, ln_w, ln_b, r_k, w_out, gf, tgt, ee)


def _s5_bwd(u, du_direct, dy, s_re, s_im, b4_re, b4_im, c4_re, c4_im, pow_r, pow_i, tt):
    L = u.shape[0]
    n_t = L // tt
    acc_shapes = ([(S5_BLOCKS, LANES, 512)] * 2 + [(S5_BLOCKS, 512, LANES)] * 2 + [(1, N_STATE)] * 2)

    def body(u_ref, dud_ref, dy_ref, sre_ref, sim_ref, pre_ref, pim_ref, bre_ref, bim_ref, cre_ref, cim_ref,
             pr_ref, pi_ref, du_o, dbre_o, dbim_o, dcre_o, dcim_o, dlr_o, dli_o,
             dbre, dbim, dcre, dcim, dlr, dli, gre, gim, car_r, car_i):
        i = pl.program_id(0)

        @pl.when(i == 0)
        def _():
            for acc in (dbre, dbim, dcre, dcim, dlr, dli, car_r, car_i):
                acc[...] = jnp.zeros_like(acc)

        uv = u_ref[...]
        dyv = dy_ref[...]
        blocks = [slice(q * 512, (q + 1) * 512) for q in range(S5_BLOCKS)]
        lanes = [slice(q * LANES, (q + 1) * LANES) for q in range(S5_BLOCKS)]
        for q in range(S5_BLOCKS):
            gre[:, blocks[q]] = _dot_bf(dyv[:, lanes[q]], cre_ref[q], ((1,), (1,)))
            gim[:, blocks[q]] = -_dot_bf(dyv[:, lanes[q]], cim_ref[q], ((1,), (1,)))
        _tile_scan(gre, gim, pr_ref, pi_ref, car_r, car_i, reverse=True)
        for q in range(S5_BLOCKS):
            gr = gre[:, blocks[q]]
            gi = gim[:, blocks[q]]
            sr = sre_ref[:, blocks[q]]
            si = sim_ref[:, blocks[q]]
            du_o[:, lanes[q]] = (dud_ref[:, lanes[q]] + _dot_bf(gr, bre_ref[q], ((1,), (1,)))
                                 + _dot_bf(gi, bim_ref[q], ((1,), (1,))))
            dbre[q] += _dot_bf(uv[:, lanes[q]], gr, ((0,), (0,)))
            dbim[q] += _dot_bf(uv[:, lanes[q]], gi, ((0,), (0,)))
            dcre[q] += _dot_bf(sr, dyv[:, lanes[q]], ((0,), (0,)))
            dcim[q] -= _dot_bf(si, dyv[:, lanes[q]], ((0,), (0,)))
            rid = lax.broadcasted_iota(jnp.int32, sr.shape, 0)
            first = i == n_t - 1
            prev_r = jnp.where(first, 0.0, pre_ref[7:8, blocks[q]])
            prev_i = jnp.where(first, 0.0, pim_ref[7:8, blocks[q]])
            pr = jnp.where(rid == 0, jnp.broadcast_to(prev_r, sr.shape), pltpu.roll(sr, 1, axis=0))
            pi_ = jnp.where(rid == 0, jnp.broadcast_to(prev_i, si.shape), pltpu.roll(si, 1, axis=0))
            dlr[:, blocks[q]] += jnp.sum(pr * gr + pi_ * gi, axis=0, keepdims=True)
            dli[:, blocks[q]] += jnp.sum(pr * gi - pi_ * gr, axis=0, keepdims=True)

        @pl.when(i == n_t - 1)
        def _():
            for acc, out in zip((dbre, dbim, dcre, dcim, dlr, dli), (dbre_o, dbim_o, dcre_o, dcim_o, dlr_o, dli_o)):
                out[...] = acc[...]

    row = lambda n: pl.BlockSpec((tt, n), lambda i: (n_t - 1 - i, 0))
    prev = pl.BlockSpec((8, N_STATE), lambda i: (jnp.maximum((n_t - 1 - i) * (tt // 8) - 1, 0), 0))
    return pl.pallas_call(
        body, name="s5_bwd", grid=(n_t,),
        in_specs=[row(D_S5)] * 3 + [row(N_STATE)] * 2 + [prev] * 2
        + [_const_spec((S5_BLOCKS, LANES, 512))] * 2 + [_const_spec((S5_BLOCKS, 512, LANES))] * 2
        + [_const_spec((8, N_STATE))] * 2,
        out_specs=[row(D_S5)] + [_acc_spec(s) for s in acc_shapes],
        out_shape=[_sds((L, D_S5))] + [_sds(s) for s in acc_shapes],
        scratch_shapes=[pltpu.VMEM(s, F32) for s in acc_shapes] + [pltpu.VMEM((tt, N_STATE), F32)] * 2
        + [pltpu.VMEM((8, N_STATE), F32)] * 2,
        compiler_params=_params("arbitrary"),
    )(u, du_direct, dy, s_re, s_im, s_re, s_im, b4_re, b4_im, c4_re, c4_im, pow_r, pow_i)


def _bwd_in(x, norm_g, w_in_bf, mu, dx2, du, dzs, drws, dzr, tt):
    L = x.shape[0]
    n_t = L // tt

    def body(x_ref, g_ref, w_ref, mu_ref, dx2_ref, du_ref, dzs_ref, drws_ref, nxt_ref, dzr_ref,
             gx_o, dw_o, dg_o, dproj, dw, dg):
        i = pl.program_id(0)

        @pl.when(i == 0)
        def _():
            dw[...] = jnp.zeros_like(dw)
            dg[...] = jnp.zeros_like(dg)

        drws_v = drws_ref[...]
        rid = lax.broadcasted_iota(jnp.int32, drws_v.shape, 0)
        nxt_row = jnp.where(i == n_t - 1, 0.0, nxt_ref[0:1, :])
        nxt = jnp.where(rid == tt - 1, jnp.broadcast_to(nxt_row, drws_v.shape), pltpu.roll(drws_v, tt - 1, axis=0))
        muv = mu_ref[...]
        drw = drws_v * (1.0 - muv) + nxt * muv
        dproj[:, 0:D_S5] = du_ref[...].astype(BF16)
        dproj[:, D_S5:2 * D_S5] = dzs_ref[...].astype(BF16)
        dproj[:, 2 * D_S5:2 * D_S5 + D_SHIFT] = drw.astype(BF16)
        dproj[:, 2 * D_S5 + D_SHIFT:D_IN] = dzr_ref[...].astype(BF16)
        dh = _dot(dproj[...], w_ref[...], ((1,), (1,)), None)
        h, vjp = jax.vjp(_rms, x_ref[...], g_ref[...])
        dxh, dgv = vjp(dh)
        gx_o[...] = dx2_ref[...] + dxh
        dg[...] += dgv
        dw[...] += _dot(h.astype(BF16), dproj[...], ((0,), (0,)), None)

        @pl.when(i == n_t - 1)
        def _():
            dg_o[...] = dg[...]
            pltpu.sync_copy(dw, dw_o)

    row = lambda n: pl.BlockSpec((tt, n), lambda i: (i, 0))
    nxt = pl.BlockSpec((8, D_SHIFT), lambda i: (jnp.minimum((i + 1) * (tt // 8), L // 8 - 1), 0))
    return pl.pallas_call(
        body, name="bwd_in", grid=(n_t,),
        in_specs=[row(D_MODEL), _const_spec((1, D_MODEL)), _const_spec((D_MODEL, D_IN)), _const_spec((1, D_SHIFT)),
                  row(D_MODEL), row(D_S5), row(D_S5), row(D_SHIFT), nxt, row(D_RWKV)],
        out_specs=[row(D_MODEL), _ANY, _acc_spec((1, D_MODEL))],
        out_shape=[_sds((L, D_MODEL)), _sds((D_MODEL, D_IN)), _sds((1, D_MODEL))],
        scratch_shapes=[pltpu.VMEM((tt, D_IN), BF16), pltpu.VMEM((D_MODEL, D_IN), F32), pltpu.VMEM((1, D_MODEL), F32)],
        compiler_params=_params("arbitrary"),
    )(x, norm_g, w_in_bf, mu, dx2, du, dzs, drws, drws, dzr)


def _block_diag_b(bbar):
    bb = bbar.reshape(S5_GROUP, S5_BLOCKS, 8, S5_STATE)
    return jnp.einsum('hqgp,Gg->qGhgp', bb, jnp.eye(8, dtype=F32)).reshape(S5_BLOCKS, LANES, 512)


def _block_diag_b_t(db4):
    d = db4.reshape(S5_BLOCKS, 8, S5_GROUP, 8, S5_STATE)
    return jnp.einsum('qGhgp,Gg->hqgp', d, jnp.eye(8, dtype=F32)).reshape(S5_GROUP, N_STATE)


def _block_diag_c(c):
    cc = c.reshape(S5_BLOCKS, 8, S5_GROUP, S5_STATE)
    return jnp.einsum('qghp,gG->qgpGh', cc, jnp.eye(8, dtype=F32)).reshape(S5_BLOCKS, 512, LANES)


def _block_diag_c_t(dc4):
    d = dc4.reshape(S5_BLOCKS, 8, S5_STATE, 8, S5_GROUP)
    return jnp.einsum('qgpGh,gG->qghp', d, jnp.eye(8, dtype=F32)).reshape(S5_GROUPS, S5_GROUP, S5_STATE)


def _local_step(x, tgt, w):
    L = x.shape[0]
    tt = min(512, L)
    tp = min(256, L)
    ee = _head_sum_matrix()

    lam_re = w['s5_lam_re'].reshape(1, N_STATE)
    lam_im = w['s5_lam_im'].reshape(1, N_STATE)
    logdt = jnp.repeat(w['s5_log_dt'], S5_STATE).reshape(1, N_STATE)
    b_re_t = w['s5_b_re'].transpose(2, 0, 1).reshape(S5_GROUP, N_STATE)
    b_im_t = w['s5_b_im'].transpose(2, 0, 1).reshape(S5_GROUP, N_STATE)
    bbr, bbi, pow_r, pow_i, rpow_r, rpow_i = _s5_param_fwd(lam_re, lam_im, logdt, b_re_t, b_im_t)
    b4_re, b4_im = _block_diag_b(bbr), _block_diag_b(bbi)
    c4_re, c4_im = _block_diag_c(w['s5_c_re']), _block_diag_c(w['s5_c_im'])

    norm_g = w['norm_g'].reshape(1, D_MODEL)
    w_in_bf = w['w_in'].astype(BF16)
    u, zs, rw, zr = _fwd_in(x, norm_g, w_in_bf, tt)
    s_re, s_im, y_ssm = _s5_fwd(u, b4_re, b4_im, c4_re, c4_im, pow_r, pow_i, tt)

    row = lambda t: t.reshape(1, -1)
    zpad = jnp.zeros((HEAD, D_RWKV), F32)
    w2p = jnp.concatenate([w['rwkv_w2'], zpad], axis=0)
    a2p = jnp.concatenate([zpad, w['rwkv_a2']], axis=0)
    pre_consts = (row(w['rwkv_mu']), row(w['rwkv_w0']), w2p, row(w['rwkv_a0']), a2p,
                  row(w['rwkv_k_k']), row(w['rwkv_k_a']), ee)
    ops = _rwkv_pre_fwd(rw, *pre_consts, tt)
    ysc, states, *kept = _rwkv_scan_fwd(ops)

    post = _post(x, u, zs, zr, ysc, ops[0], ops[2], ops[3], y_ssm,
                 row(w['s5_d']), w['s5_glu_w'], row(w['s5_glu_b']), row(w['rwkv_ln_w']), row(w['rwkv_ln_b']),
                 row(w['rwkv_r_k']), w['w_out'], row(w['final_g']), tgt, ee, tp)
    (dx2, du_d, dzs, dzr, dysc, dr_b, dk2_b, dv_b, dy_ssm,
     dd, dglu_w, dglu_b, dln_w, dln_b, dr_k, dw_out, dgf, loss) = post

    du, db4_re, db4_im, dc4_re, dc4_im, dlbr, dlbi = _s5_bwd(
        u, du_d, dy_ssm, s_re, s_im, b4_re, b4_im, c4_re, c4_im, rpow_r, rpow_i, tt)
    group_ind = (jnp.arange(N_STATE)[:, None] // S5_STATE == jnp.arange(LANES)[None, :]).astype(F32)
    dlam_re, dlam_im, dlogdt, db_re_t, db_im_t = _s5_param_bwd(
        lam_re, lam_im, logdt, b_re_t, b_im_t, dlbr, dlbi, _block_diag_b_t(db4_re), _block_diag_b_t(db4_im), group_ind)

    cots = list(_rwkv_scan_bwd(ops, states, kept, dysc)) + [dr_b, dk2_b, dv_b]
    drws, dmu, dw0, dw2p, da0, da2p, dk_k, dk_a = _rwkv_pre_bwd(rw, *pre_consts, cots, tt)

    grad_x, dw_in, dnorm_g = _bwd_in(x, norm_g, w_in_bf, row(w['rwkv_mu']), dx2, du, dzs, drws, dzr, tt)

    unb = lambda t: t.reshape(S5_GROUP, S5_GROUPS, S5_STATE).transpose(1, 2, 0)
    grads = {
        'norm_g': dnorm_g.reshape(D_MODEL), 'w_in': dw_in,
        's5_lam_re': dlam_re.reshape(S5_GROUPS, S5_STATE), 's5_lam_im': dlam_im.reshape(S5_GROUPS, S5_STATE),
        's5_log_dt': dlogdt[0, :S5_GROUPS], 's5_b_re': unb(db_re_t), 's5_b_im': unb(db_im_t),
        's5_c_re': _block_diag_c_t(dc4_re), 's5_c_im': _block_diag_c_t(dc4_im),
        's5_d': dd.reshape(D_S5), 's5_glu_w': dglu_w, 's5_glu_b': dglu_b.reshape(D_S5),
        'rwkv_mu': dmu.reshape(-1), 'rwkv_w0': dw0.reshape(-1), 'rwkv_w2': dw2p[:HEAD], 'rwkv_a0': da0.reshape(-1),
        'rwkv_a2': da2p[HEAD:], 'rwkv_k_k': dk_k.reshape(-1), 'rwkv_k_a': dk_a.reshape(-1),
        'rwkv_r_k': dr_k.reshape(N_HEADS, HEAD), 'rwkv_ln_w': dln_w.reshape(-1), 'rwkv_ln_b': dln_b.reshape(-1),
        'w_out': dw_out, 'final_g': dgf.reshape(D_MODEL),
    }
    return loss, grad_x, grads


def _exchange(arrays, gather, axes, name):
    n = len(arrays)
    group = 2 ** len(axes)

    def body(*refs):
        send_refs, recv_refs = refs[:n], refs[n:2 * n]
        send_sems, recv_sems, local_sems = refs[2 * n:]
        pos = {ax: lax.axis_index(ax) for ax in ("x", "y", "c")}

        def index_of(p):
            idx = 0
            for ax in axes:
                idx = 2 * idx + p[ax]
            return idx

        me = index_of(pos)
        own, outs, arrivals = [], [], []
        for i, (send_ref, recv_ref) in enumerate(zip(send_refs, recv_refs)):
            def block_for(dev, send_ref=send_ref, whole=gather[i]):
                return send_ref if whole else send_ref.at[dev]

            own.append(pltpu.make_async_copy(block_for(me), recv_ref.at[me], local_sems.at[i]))
            own[-1].start()
            for k in range(1, group):
                peer = dict(pos)
                for bit, ax in enumerate(axes):
                    if (k >> bit) & 1:
                        peer[ax] = 1 - pos[ax]
                peer_idx = index_of(peer)
                sems = dict(send_sem=send_sems.at[i, k - 1], recv_sem=recv_sems.at[i, k - 1],
                            device_id=(peer["x"], peer["y"], peer["c"]), device_id_type=pl.DeviceIdType.MESH)
                outs.append(pltpu.make_async_remote_copy(src_ref=block_for(peer_idx), dst_ref=recv_ref.at[me], **sems))
                outs[-1].start()
                arrivals.append(
                    pltpu.make_async_remote_copy(src_ref=block_for(peer_idx), dst_ref=recv_ref.at[peer_idx], **sems))
        for copy in arrivals:
            copy.wait_recv()
        for copy in outs:
            copy.wait_send()
        for copy in own:
            copy.wait()

    return pl.pallas_call(
        body, name=name, in_specs=[_ANY] * n, out_specs=[_ANY] * n,
        out_shape=[jax.ShapeDtypeStruct(((group,) + a.shape) if whole else a.shape, a.dtype)
                   for a, whole in zip(arrays, gather)],
        scratch_shapes=[pltpu.SemaphoreType.DMA((n, group - 1)), pltpu.SemaphoreType.DMA((n, group - 1)),
                        pltpu.SemaphoreType.DMA((n,))],
        compiler_params=pltpu.CompilerParams(has_side_effects=True),
    )(*arrays)


def _sum_devices(ref):
    g = ref[0].astype(F32)
    for s in range(1, ref.shape[0]):
        g = g + ref[s].astype(F32)
    return g


def _adamw_math(g, w, m, v):
    m_new = ADAM_B1 * m + (1.0 - ADAM_B1) * g
    v_new = ADAM_B2 * v + (1.0 - ADAM_B2) * (g * g)
    m_hat = m_new / (1.0 - ADAM_B1 ** ADAM_STEP)
    v_hat = v_new / (1.0 - ADAM_B2 ** ADAM_STEP)
    return -ADAM_LR * (m_hat / (jnp.sqrt(v_hat) + ADAM_EPS) + ADAM_WD * w), m_new, v_new


def _adamw(gs, ws, ms, vs, reduce, name):
    n = len(ws)

    def body(*refs):
        g_refs, w_refs, m_refs, v_refs = (refs[j * n:(j + 1) * n] for j in range(4))
        outs = refs[4 * n:]
        for i in range(n):
            g = _sum_devices(g_refs[i]) if reduce else g_refs[i][...]
            res = _adamw_math(g, w_refs[i][...], m_refs[i][...], v_refs[i][...])
            for j, val in enumerate(((g,) if reduce else ()) + res):
                outs[j * n + i][...] = val

    return pl.pallas_call(
        body, name=name, out_shape=[_sds(w.shape) for w in ws] * (4 if reduce else 3),
        compiler_params=pltpu.CompilerParams(vmem_limit_bytes=VMEM_LIMIT),
    )(*gs, *ws, *ms, *vs)


def _sum_blocks(recv):
    def body(recv_ref, out_ref):
        out_ref[...] = _sum_devices(recv_ref)

    return pl.pallas_call(body, name="sum_small_grads", out_shape=_sds(recv.shape[1:]))(recv)


_WEIGHTS = [
    ('norm_g', (1, 1024), False), ('w_in', (1, 1024, 400), True), ('s5_lam_re', (1, 32, 64), False),
    ('s5_lam_im', (1, 32, 64), False), ('s5_log_dt', (1, 32), False), ('s5_b_re', (1, 32, 64, 16), False),
    ('s5_b_im', (1, 32, 64, 16), False), ('s5_c_re', (1, 32, 16, 64), False), ('s5_c_im', (1, 32, 16, 64), False),
    ('s5_d', (1, 512), False), ('s5_glu_w', (1, 64, 512), True), ('s5_glu_b', (1, 512), False),
    ('rwkv_mu', (1, 1664), False), ('rwkv_w0', (1, 512), False), ('rwkv_w2', (1, 64, 64), True),
    ('rwkv_a0', (1, 512), False), ('rwkv_a2', (1, 64, 64), True), ('rwkv_k_k', (1, 512), False),
    ('rwkv_k_a', (1, 512), False), ('rwkv_r_k', (1, 8, 64), False), ('rwkv_ln_w', (1, 512), False),
    ('rwkv_ln_b', (1, 512), False), ('w_out', (1, 128, 1024), True), ('final_g', (1024,), False),
]
_SHARDED = [(n, s) for n, s, sharded in _WEIGHTS if sharded]
_SMALL = [(n, s) for n, s, sharded in _WEIGHTS if not sharded]
_COLUMN_SHARDED = ('w_in', 'rwkv_w2', 'rwkv_a2')
_SMALL_SIZE = sum(math.prod(s) for _, s in _SMALL) + 1
_SMALL_ROWS = -(-_SMALL_SIZE // (8 * LANES)) * 8


def _pack_small(grads, loss):
    flat = [grads[n].reshape(-1) for n, _ in _SMALL] + [loss.reshape(1)]
    pad = _SMALL_ROWS * LANES - _SMALL_SIZE
    return jnp.concatenate(flat + [jnp.zeros((pad,), F32)]).reshape(_SMALL_ROWS, LANES)


def _unpack_small(packed):
    flat = packed.reshape(-1)
    out, off = {}, 0
    for n, s in _SMALL:
        size = math.prod(s)
        out[n] = flat[off:off + size].reshape(s)
        off += size
    return out, flat[off]


_BF16_OPERANDS = ('w_in', 's5_glu_w', 'w_out')


def _join_shards(name, blocks):
    _, rows, cols = blocks.shape
    if name in _COLUMN_SHARDED:
        return blocks.transpose(1, 0, 2).reshape(rows, N_DEV * cols)
    return blocks.reshape(N_DEV * rows, cols)


def _split_shards(name, full, shard_shape):
    rows, cols = shard_shape
    if name in _COLUMN_SHARDED:
        return full.reshape(rows, N_DEV, cols).transpose(1, 0, 2)
    return full.reshape(N_DEV, rows, cols)


def kernel(x, norm_g, w_in, s5_lam_re, s5_lam_im, s5_log_dt, s5_b_re, s5_b_im, s5_c_re, s5_c_im, s5_d, s5_glu_w, s5_glu_b, rwkv_mu, rwkv_w0, rwkv_w2, rwkv_a0, rwkv_a2, rwkv_k_k, rwkv_k_a, rwkv_r_k, rwkv_ln_w, rwkv_ln_b, w_out, final_g, loss_target, m_norm_g, m_w_in, m_s5_lam_re, m_s5_lam_im, m_s5_log_dt, m_s5_b_re, m_s5_b_im, m_s5_c_re, m_s5_c_im, m_s5_d, m_s5_glu_w, m_s5_glu_b, m_rwkv_mu, m_rwkv_w0, m_rwkv_w2, m_rwkv_a0, m_rwkv_a2, m_rwkv_k_k, m_rwkv_k_a, m_rwkv_r_k, m_rwkv_ln_w, m_rwkv_ln_b, m_w_out, m_final_g, v_norm_g, v_w_in, v_s5_lam_re, v_s5_lam_im, v_s5_log_dt, v_s5_b_re, v_s5_b_im, v_s5_c_re, v_s5_c_im, v_s5_d, v_s5_glu_w, v_s5_glu_b, v_rwkv_mu, v_rwkv_w0, v_rwkv_w2, v_rwkv_a0, v_rwkv_a2, v_rwkv_k_k, v_rwkv_k_a, v_rwkv_r_k, v_rwkv_ln_w, v_rwkv_ln_b, v_w_out, v_final_g):
    given = dict(locals())

    n_sh = len(_SHARDED)
    everyone = ("x", "y", "c")
    shards = [given[n][0].astype(BF16 if n in _BF16_OPERANDS else F32) for n, _ in _SHARDED]
    gathered = _exchange(shards, (True,) * n_sh, everyone, "gather_weights")
    local = {n: _join_shards(n, blocks).astype(F32 if n != 'w_in' else BF16)
             for (n, _), blocks in zip(_SHARDED, gathered)}
    local.update({n: (given[n][0] if len(s) > 1 else given[n]) for n, s in _SMALL})

    loss, grad_x, grads = _local_step(x[0], loss_target[0], local)

    blocks = [_split_shards(n, grads[n], s[1:]).astype(BF16) for n, s in _SHARDED]
    small = _pack_small(grads, loss[0, 0]).reshape(N_DEV, _SMALL_ROWS // N_DEV, LANES)
    recv = _exchange(blocks + [small], (False,) * (n_sh + 1), everyone, "exchange_grads")
    small_sum = _exchange([_sum_blocks(recv[-1])], (True,), everyone, "gather_small_grads")[0]

    result = {}
    for group, name in (([0], "adamw_w_in"), ([1, 2, 3, 4], "adamw_shards")):
        ns = [_SHARDED[i][0] for i in group]
        res = _adamw([recv[i] for i in group], [given[n][0] for n in ns], [given['m_' + n][0] for n in ns],
                     [given['v_' + n][0] for n in ns], True, name)
        for j, n in enumerate(ns):
            result[n] = [res[k * len(ns) + j][None] for k in range(4)]
    g_small, total = _unpack_small(small_sum)
    two_d = lambda t: t.reshape(1, -1) if t.ndim == 1 else t
    ns = [n for n, _ in _SMALL]
    res = _adamw([two_d(g_small[n]) for n in ns], [two_d(given[n]) for n in ns], [two_d(given['m_' + n]) for n in ns],
                 [two_d(given['v_' + n]) for n in ns], False, "adamw_small")
    for j, (n, s) in enumerate(_SMALL):
        result[n] = [g_small[n]] + [res[k * len(ns) + j].reshape(s) for k in range(3)]

    outs = [total, grad_x[None]]
    for k in range(4):
        outs += [result[n][k] for n, _, _ in _WEIGHTS]
    return tuple(outs)
```

```python
import math

import jax
import jax.numpy as jnp
from jax import lax
from jax.experimental import pallas as pl
from jax.experimental.pallas import tpu as pltpu

F32 = jnp.float32
BF16 = jnp.bfloat16
HI = lax.Precision.HIGH

D_MODEL = 1024
D_S5 = 512
D_RWKV = 512
S5_GROUPS = 32
S5_GROUP = 16
S5_STATE = 64
N_STATE = S5_GROUPS * S5_STATE
N_HEADS = 8
HEAD = 64
D_SHIFT = 3 * D_RWKV + 128
D_IN = 2 * D_S5 + D_SHIFT + D_RWKV
NORM_EPS = 1e-6
GN_EPS = 64e-5
N_DEV = 8
LANES = 128
S5_BLOCKS = 4
RWKV_CHUNK = 64
RWKV_CHUNKS_PER_STEP = 4
VMEM_LIMIT = 56 * 1024 * 1024

ADAM_LR = 0.001
ADAM_B1 = 0.9
ADAM_B2 = 0.999
ADAM_EPS = 1e-08
ADAM_WD = 0.01
ADAM_STEP = 10


def _dot(a, b, dims, prec):
    return lax.dot_general(a, b, (dims, ((), ())), precision=prec, preferred_element_type=F32)


def _dot_bf(a, b, dims):
    return _dot(a.astype(BF16), b.astype(BF16), dims, None)


def _make_mm(cast, prec):
    @jax.custom_vjp
    def mm(a, b):
        return _dot(cast(a), cast(b), ((1,), (0,)), prec)

    def fwd(a, b):
        return mm(a, b), (a, b)

    def bwd(res, g):
        a, b = res
        return (_dot(cast(g), cast(b), ((1,), (1,)), prec), _dot(cast(a), cast(g), ((0,), (0,)), prec))

    mm.defvjp(fwd, bwd)
    return mm


mm_bf = _make_mm(lambda t: t.astype(BF16), None)


def _make_head_sum(split):
    def product(x, ee):
        hi = x.astype(BF16)
        out = _dot(hi, ee, ((1,), (0,)), None)
        if split:
            out = out + _dot((x - hi.astype(F32)).astype(BF16), ee, ((1,), (0,)), None)
        return out

    @jax.custom_vjp
    def head_sum(x, ee):
        return product(x, ee)

    def fwd(x, ee):
        return product(x, ee), ee

    def bwd(ee, g):
        return product(g, ee), jnp.zeros_like(ee)

    head_sum.defvjp(fwd, bwd)
    return head_sum


head_sum = _make_head_sum(False)
head_sum_split = _make_head_sum(True)


@jax.custom_vjp
def _sigmoid(x):
    return 1.0 / (1.0 + jnp.exp(-x))


def _sigmoid_fwd(x):
    s = _sigmoid(x)
    return s, s


_sigmoid.defvjp(_sigmoid_fwd, lambda s, g: (g * s * (1.0 - s),))


@jax.custom_vjp
def _silu(x):
    return x * _sigmoid(x)


def _silu_fwd(x):
    s = _sigmoid(x)
    return x * s, (x, s)


def _silu_bwd(res, g):
    x, s = res
    return (g * s * (1.0 + x * (1.0 - s)),)


_silu.defvjp(_silu_fwd, _silu_bwd)


@jax.custom_vjp
def _softplus(x):
    return jnp.maximum(x, 0.0) + jnp.log(1.0 + jnp.exp(-jnp.abs(x)))


def _softplus_fwd(x):
    e = jnp.exp(-jnp.abs(x))
    return jnp.maximum(x, 0.0) + jnp.log(1.0 + e), (x, e)


def _softplus_bwd(res, g):
    x, e = res
    return (g * jnp.where(x >= 0.0, 1.0, e) / (1.0 + e),)


_softplus.defvjp(_softplus_fwd, _softplus_bwd)


@jax.custom_vjp
def _normalize_heads(x, ee):
    return x / jnp.maximum(jnp.sqrt(head_sum(x * x, ee)), 1e-12)


def _normalize_heads_fwd(x, ee):
    norm = jnp.sqrt(head_sum(x * x, ee))
    inv = 1.0 / jnp.maximum(norm, 1e-12)
    y = x * inv
    return y, (y, inv, norm, ee)


def _normalize_heads_bwd(res, g):
    y, inv, norm, ee = res
    along = jnp.where(norm > 1e-12, head_sum(g * y, ee), 0.0)
    return inv * (g - y * along), jnp.zeros_like(ee)


_normalize_heads.defvjp(_normalize_heads_fwd, _normalize_heads_bwd)


_GELU_C = 2.0 * math.sqrt(2.0 / math.pi)


def _gelu_gate(x):
    return 1.0 / (1.0 + jnp.exp(-_GELU_C * x * (1.0 + 0.044715 * (x * x))))


@jax.custom_vjp
def _gelu(x):
    return x * _gelu_gate(x)


def _gelu_fwd(x):
    s = _gelu_gate(x)
    return x * s, (x, s)


def _gelu_bwd(res, g):
    x, s = res
    return (g * (s + x * s * (1.0 - s) * (_GELU_C * (1.0 + 3.0 * 0.044715 * (x * x)))),)


_gelu.defvjp(_gelu_fwd, _gelu_bwd)


def _rms(x, g):
    return x * lax.rsqrt(jnp.mean(x * x, axis=-1, keepdims=True) + NORM_EPS) * g


def _const_spec(shape):
    nd = len(shape)
    return pl.BlockSpec(shape, lambda *_: (0,) * nd, pipeline_mode=pl.Buffered(1))


def _acc_spec(shape):
    nd = len(shape)
    return pl.BlockSpec(shape, lambda *_: (0,) * nd)


def _params(sem):
    return pltpu.CompilerParams(dimension_semantics=(sem,), vmem_limit_bytes=VMEM_LIMIT)


_ANY = pl.BlockSpec(memory_space=pl.ANY)


def _sds(shape):
    return jax.ShapeDtypeStruct(shape, F32)


def _head_sum_matrix():
    i = jnp.arange(D_RWKV) // HEAD
    return (i[:, None] == i[None, :]).astype(BF16)


def _s5_param_fn(lam_re, lam_im, logdt, b_re, b_im):
    dt = jnp.exp(logdt)
    mag = jnp.exp(lam_re * dt)
    ang = lam_im * dt
    lbr = mag * jnp.cos(ang)
    lbi = mag * jnp.sin(ang)
    nr = lbr - 1.0
    den = lam_re * lam_re + lam_im * lam_im
    cr = (nr * lam_re + lbi * lam_im) / den
    ci = (lbi * lam_re - nr * lam_im) / den
    return lbr, lbi, cr * b_re - ci * b_im, cr * b_im + ci * b_re


def _cmul(ar, ai, br, bi):
    return ar * br - ai * bi, ar * bi + ai * br


def _s5_param_fwd(lam_re, lam_im, logdt, b_re, b_im):
    def body(lr, li, ld, br, bi, o_br, o_bi, o_pr, o_pi, o_qr, o_qi):
        lbr, lbi, bbr, bbi = _s5_param_fn(lr[...], li[...], ld[...], br[...], bi[...])
        o_br[...] = bbr
        o_bi[...] = bbi
        rid = lax.broadcasted_iota(jnp.int32, (8, N_STATE), 0)
        pr, pi_ = lbr, lbi
        fwd_r = rev_r = jnp.broadcast_to(pr, (8, N_STATE))
        fwd_i = rev_i = jnp.broadcast_to(pi_, (8, N_STATE))
        for j in range(1, 8):
            pr, pi_ = _cmul(pr, pi_, lbr, lbi)
            fwd_r = jnp.where(rid == j, jnp.broadcast_to(pr, (8, N_STATE)), fwd_r)
            fwd_i = jnp.where(rid == j, jnp.broadcast_to(pi_, (8, N_STATE)), fwd_i)
            rev_r = jnp.where(rid == 7 - j, jnp.broadcast_to(pr, (8, N_STATE)), rev_r)
            rev_i = jnp.where(rid == 7 - j, jnp.broadcast_to(pi_, (8, N_STATE)), rev_i)
        o_pr[...] = fwd_r
        o_pi[...] = fwd_i
        o_qr[...] = rev_r
        o_qi[...] = -rev_i

    return pl.pallas_call(
        body, name="s5_param_fwd",
        out_shape=[_sds((S5_GROUP, N_STATE))] * 2 + [_sds((8, N_STATE))] * 4,
    )(lam_re, lam_im, logdt, b_re, b_im)


def _s5_param_bwd(lam_re, lam_im, logdt, b_re, b_im, d_lbr, d_lbi, d_bbr, d_bbi, group_ind):
    def body(lr, li, ld, br, bi, g0, g1, g2, g3, ind, o_lr, o_li, o_ld, o_br, o_bi):
        _, vjp = jax.vjp(_s5_param_fn, lr[...], li[...], ld[...], br[...], bi[...])
        d_lr, d_li, d_ld, d_br, d_bi = vjp((g0[...], g1[...], g2[...], g3[...]))
        o_lr[...] = d_lr
        o_li[...] = d_li
        o_ld[...] = _dot(jnp.broadcast_to(d_ld, (8, N_STATE)), ind[...], ((1,), (0,)), HI)
        o_br[...] = d_br
        o_bi[...] = d_bi

    return pl.pallas_call(
        body, name="s5_param_bwd",
        out_shape=[_sds((1, N_STATE))] * 2 + [_sds((8, LANES))] + [_sds((S5_GROUP, N_STATE))] * 2,
    )(lam_re, lam_im, logdt, b_re, b_im, d_lbr, d_lbi, d_bbr, d_bbi, group_ind)


def _fwd_in(x, norm_g, w_in_bf, tt):
    L = x.shape[0]

    def body(x_ref, g_ref, w_ref, u_ref, zs_ref, rw_ref, zr_ref):
        h = _rms(x_ref[...], g_ref[...])
        proj = jnp.dot(h.astype(BF16), w_ref[...], preferred_element_type=F32)
        u_ref[...] = proj[:, 0:D_S5]
        zs_ref[...] = proj[:, D_S5:2 * D_S5]
        rw_ref[...] = proj[:, 2 * D_S5:2 * D_S5 + D_SHIFT]
        zr_ref[...] = proj[:, 2 * D_S5 + D_SHIFT:D_IN]

    row = lambda n: pl.BlockSpec((tt, n), lambda i: (i, 0))
    return pl.pallas_call(
        body, name="fwd_in", grid=(L // tt,),
        in_specs=[row(D_MODEL), _const_spec((1, D_MODEL)), _const_spec((D_MODEL, D_IN))],
        out_specs=[row(D_S5), row(D_S5), row(D_SHIFT), row(D_RWKV)],
        out_shape=[_sds((L, D_S5)), _sds((L, D_S5)), _sds((L, D_SHIFT)), _sds((L, D_RWKV))],
        compiler_params=_params("parallel"),
    )(x, norm_g, w_in_bf)


S5_LANE_CHUNK = 512


def _tile_scan(re_ref, im_ref, pow_r_ref, pow_i_ref, carry_r_ref, carry_i_ref, reverse):
    t, n = re_ref.shape
    n_groups = t // 8
    ch = S5_LANE_CHUNK
    rid = lax.broadcasted_iota(jnp.int32, (8, ch), 0)
    for c in range(n // ch):
        cols = slice(c * ch, (c + 1) * ch)
        pow_r = pow_r_ref[:, cols]
        pow_i = pow_i_ref[:, cols]
        row = lambda tile, j: jnp.broadcast_to(tile[j:j + 1], (8, ch))
        levels = []
        for d in (1, 2, 4):
            keep = (rid < 8 - d) if reverse else (rid >= d)
            j = 8 - d if reverse else d - 1
            levels.append(((8 - d) if reverse else d,
                           jnp.where(keep, row(pow_r, j), 0.0), jnp.where(keep, row(pow_i, j), 0.0)))

        def group(g, carry):
            r0 = pl.multiple_of(((n_groups - 1 - g) if reverse else g) * 8, 8)
            xr = re_ref[pl.ds(r0, 8), cols]
            xi = im_ref[pl.ds(r0, 8), cols]
            for shift, lr, li in levels:
                mr, mi = _cmul(lr, li, pltpu.roll(xr, shift, axis=0), pltpu.roll(xi, shift, axis=0))
                xr = xr + mr
                xi = xi + mi
            mr, mi = _cmul(pow_r, pow_i, carry[0], carry[1])
            xr = xr + mr
            xi = xi + mi
            re_ref[pl.ds(r0, 8), cols] = xr
            im_ref[pl.ds(r0, 8), cols] = xi
            last = 0 if reverse else 7
            return row(xr, last), row(xi, last)

        out = lax.fori_loop(0, n_groups, group, (carry_r_ref[:, cols], carry_i_ref[:, cols]))
        carry_r_ref[:, cols] = out[0]
        carry_i_ref[:, cols] = out[1]


def _s5_fwd(u, b4_re, b4_im, c4_re, c4_im, pow_r, pow_i, tt):
    L = u.shape[0]

    def body(u_ref, bre_ref, bim_ref, cre_ref, cim_ref, pr_ref, pi_ref, sre_o, sim_o, y_o, car_r, car_i):
        @pl.when(pl.program_id(0) == 0)
        def _():
            car_r[...] = jnp.zeros_like(car_r)
            car_i[...] = jnp.zeros_like(car_i)

        uv = u_ref[...]
        for q in range(S5_BLOCKS):
            uq = uv[:, q * LANES:(q + 1) * LANES]
            cols = slice(q * 512, (q + 1) * 512)
            sre_o[:, cols] = _dot_bf(uq, bre_ref[q], ((1,), (0,)))
            sim_o[:, cols] = _dot_bf(uq, bim_ref[q], ((1,), (0,)))
        _tile_scan(sre_o, sim_o, pr_ref, pi_ref, car_r, car_i, reverse=False)
        for q in range(S5_BLOCKS):
            cols = slice(q * 512, (q + 1) * 512)
            y_o[:, q * LANES:(q + 1) * LANES] = (_dot_bf(sre_o[:, cols], cre_ref[q], ((1,), (0,)))
                                                 - _dot_bf(sim_o[:, cols], cim_ref[q], ((1,), (0,))))

    row = lambda n: pl.BlockSpec((tt, n), lambda i: (i, 0))
    return pl.pallas_call(
        body, name="s5_fwd", grid=(L // tt,),
        in_specs=[row(D_S5)] + [_const_spec((S5_BLOCKS, LANES, 512))] * 2 + [_const_spec((S5_BLOCKS, 512, LANES))] * 2
        + [_const_spec((8, N_STATE))] * 2,
        out_specs=[row(N_STATE), row(N_STATE), row(D_S5)],
        out_shape=[_sds((L, N_STATE)), _sds((L, N_STATE)), _sds((L, D_S5))],
        scratch_shapes=[pltpu.VMEM((8, N_STATE), F32)] * 2,
        compiler_params=_params("arbitrary"),
    )(u, b4_re, b4_im, c4_re, c4_im, pow_r, pow_i)


def _rwkv_pre_fn(r, k, v, wa, w0, w2p, a0, a2p, k_k, k_a, ee):
    w = -_softplus(-(w0 + mm_bf(jnp.tanh(wa), w2p))) - 0.5
    logw = -jnp.exp(w)
    a = _sigmoid(a0 + mm_bf(wa, a2p))
    kk = _normalize_heads(k * k_k, ee)
    k2 = k * (1.0 + (a - 1.0) * k_a)
    return r, logw, k2, v, -kk, kk * a


N_PAIRS = N_HEADS // 2


def _head_spec(tt):
    return pl.BlockSpec((N_PAIRS, tt, LANES), lambda i: (0, i, 0))


def _load_heads(ref):
    return jnp.concatenate([ref[p] for p in range(N_PAIRS)], axis=-1)


def _store_heads(ref, val):
    for p in range(N_PAIRS):
        ref[p] = val[:, p * LANES:(p + 1) * LANES]


def _split_pairs(x):
    return jnp.concatenate([x[:, :, :HEAD], x[:, :, HEAD:]], axis=0)


def _join_pairs(x):
    return jnp.concatenate([x[:N_PAIRS], x[N_PAIRS:]], axis=-1)


def _shifted(rw, prev_blk, first):
    rolled = pltpu.roll(rw, 1, axis=0)
    prev_row = jnp.where(first, 0.0, prev_blk[7:8, :])
    rid = lax.broadcasted_iota(jnp.int32, rw.shape, 0)
    return jnp.where(rid == 0, jnp.broadcast_to(prev_row, rw.shape), rolled)


def _split_rw(t):
    return t[:, 0:512], t[:, 512:1024], t[:, 1024:1536], t[:, 1536:1664]


def _rwkv_pre_specs(tt):
    row = pl.BlockSpec((tt, D_SHIFT), lambda i: (i, 0))
    prev = pl.BlockSpec((8, D_SHIFT), lambda i: (jnp.maximum(i * (tt // 8) - 1, 0), 0))
    consts = [_const_spec((1, D_SHIFT)), _const_spec((1, D_RWKV)), _const_spec((LANES, D_RWKV)),
              _const_spec((1, D_RWKV)), _const_spec((LANES, D_RWKV)), _const_spec((1, D_RWKV)),
              _const_spec((1, D_RWKV)), _const_spec((D_RWKV, D_RWKV))]
    return [row, prev] + consts


def _rwkv_pre_fwd(rw, mu, w0, w2p, a0, a2p, k_k, k_a, ee, tt):
    L = rw.shape[0]

    def body(rw_ref, prev_ref, mu_ref, w0_ref, w2_ref, a0_ref, a2_ref, kk_ref, ka_ref, ee_ref, *outs):
        rwv = rw_ref[...]
        rws = rwv + (_shifted(rwv, prev_ref[...], pl.program_id(0) == 0) - rwv) * mu_ref[...]
        res = _rwkv_pre_fn(*_split_rw(rws), w0_ref[...], w2_ref[...], a0_ref[...], a2_ref[...],
                           kk_ref[...], ka_ref[...], ee_ref[...])
        for o, val in zip(outs, res):
            _store_heads(o, val)

    return pl.pallas_call(
        body, name="rwkv_pre_fwd", grid=(L // tt,),
        in_specs=_rwkv_pre_specs(tt), out_specs=[_head_spec(tt)] * 6, out_shape=[_sds((N_PAIRS, L, LANES))] * 6,
        compiler_params=_params("parallel"),
    )(rw, rw, mu, w0, w2p, a0, a2p, k_k, k_a, ee)


def _rwkv_pre_bwd(rw, mu, w0, w2p, a0, a2p, k_k, k_a, ee, cots, tt):
    L = rw.shape[0]
    n_t = L // tt

    def body(rw_ref, prev_ref, mu_ref, w0_ref, w2_ref, a0_ref, a2_ref, kk_ref, ka_ref, ee_ref,
             c_r, c_w, c_k, c_v, c_a, c_b, cb_r, cb_k, cb_v,
             drws_ref, dmu_o, dw0_o, dw2_o, da0_o, da2_o, dkk_o, dka_o,
             dmu, dw0, dw2, da0, da2, dkk, dka):
        i = pl.program_id(0)
        accs = (dmu, dw0, dw2, da0, da2, dkk, dka)

        @pl.when(i == 0)
        def _():
            for acc in accs:
                acc[...] = jnp.zeros_like(acc)

        rwv = rw_ref[...]
        diff = _shifted(rwv, prev_ref[...], i == 0) - rwv
        rws = rwv + diff * mu_ref[...]
        consts = (w0_ref[...], w2_ref[...], a0_ref[...], a2_ref[...], kk_ref[...], ka_ref[...])
        _, vjp = jax.vjp(lambda *a: _rwkv_pre_fn(*a, ee_ref[...]), *_split_rw(rws), *consts)
        scan = [_load_heads(c) for c in (c_r, c_w, c_k, c_v, c_a, c_b)]
        g = vjp((scan[0] + cb_r[...], scan[1], scan[2] + cb_k[...], scan[3] + cb_v[...], scan[4], scan[5]))
        drws = jnp.concatenate(g[0:4], axis=-1)
        drws_ref[...] = drws
        dmu[...] += jnp.sum(drws * diff, axis=0, keepdims=True)
        for acc, val in zip(accs[1:], g[4:]):
            acc[...] += val

        @pl.when(i == n_t - 1)
        def _():
            for acc, out in zip(accs, (dmu_o, dw0_o, dw2_o, da0_o, da2_o, dkk_o, dka_o)):
                out[...] = acc[...]

    row = pl.BlockSpec((tt, D_RWKV), lambda i: (i, 0))
    shapes = [(1, D_SHIFT), (1, D_RWKV), (LANES, D_RWKV), (1, D_RWKV), (LANES, D_RWKV), (1, D_RWKV), (1, D_RWKV)]
    return pl.pallas_call(
        body, name="rwkv_pre_bwd", grid=(n_t,),
        in_specs=_rwkv_pre_specs(tt) + [_head_spec(tt)] * 6 + [row] * 3,
        out_specs=[pl.BlockSpec((tt, D_SHIFT), lambda i: (i, 0))] + [_acc_spec(s) for s in shapes],
        out_shape=[_sds((L, D_SHIFT))] + [_sds(s) for s in shapes],
        scratch_shapes=[pltpu.VMEM(s, F32) for s in shapes],
        compiler_params=_params("arbitrary"),
    )(rw, rw, mu, w0, w2p, a0, a2p, k_k, k_a, ee, *cots)


def _bmm(a, b):
    return lax.dot_general(a, b, (((2,), (1,)), ((0,), (0,))), precision=HI, preferred_element_type=F32)


def _bmm_nt(a, b):
    return lax.dot_general(a, b, (((2,), (2,)), ((0,), (0,))), precision=HI, preferred_element_type=F32)


def _bmm_tn(a, b):
    return lax.dot_general(a, b, (((1,), (1,)), ((0,), (0,))), precision=HI, preferred_element_type=F32)


def _bdot_bf(a, b, lhs_dim, rhs_dim):
    return lax.dot_general(a.astype(BF16), b.astype(BF16), (((lhs_dim,), (rhs_dim,)), ((0,), (0,))),
                           preferred_element_type=F32)


@jax.custom_vjp
def _bmm_bf(a, b):
    return _bdot_bf(a, b, 2, 1)


def _bmm_bf_fwd(a, b):
    return _bmm_bf(a, b), (a, b)


def _bmm_bf_bwd(res, g):
    a, b = res
    return _bdot_bf(g, b, 2, 2), _bdot_bf(a, g, 1, 1)


_bmm_bf.defvjp(_bmm_bf_fwd, _bmm_bf_bwd)


@jax.custom_vjp
def _bmm_tn_bf(a, b):
    return _bdot_bf(a, b, 1, 1)


def _bmm_tn_bf_fwd(a, b):
    return _bmm_tn_bf(a, b), (a, b)


def _bmm_tn_bf_bwd(res, g):
    a, b = res
    return _bdot_bf(b, g, 2, 2), _bdot_bf(a, g, 2, 1)


_bmm_tn_bf.defvjp(_bmm_tn_bf_fwd, _bmm_tn_bf_bwd)


def _unit_lower_inverse(a):
    t = a.shape[-1]
    ti = lax.broadcasted_iota(jnp.int32, (t, t), 0)
    si = lax.broadcasted_iota(jnp.int32, (t, t), 1)

    def same_block(bits):
        shift = jnp.int32(bits)
        return (lax.shift_right_logical(ti, shift) == lax.shift_right_logical(si, shift))[None]

    def mm(x, y):
        return _bdot_bf(x, y, 2, 1)

    d = jnp.where(same_block(3), a, 0.0)
    inv = jnp.where(ti == si, 1.0, 0.0)[None] + d
    pw = mm(d, d)
    both = mm(jnp.concatenate([inv, pw], axis=1), pw)
    inv = inv + both[:, :t]
    inv = inv + mm(inv, both[:, t:])
    bits = 3
    while (1 << bits) < t:
        e = jnp.where(same_block(bits), 0.0, jnp.where(same_block(bits + 1), a, 0.0))
        inv = inv + mm(mm(inv, e), inv)
        bits += 1
    return inv


def _tri_mask(t):
    ri = lax.broadcasted_iota(jnp.int32, (2 * t, 2 * t), 0)
    ci = lax.broadcasted_iota(jnp.int32, (2 * t, 2 * t), 1)
    top_rows = ri < t
    diff = jnp.where(top_rows, ri, ri - t) - jnp.where(ci < t, ci, ci - t)
    return (diff >= jnp.where(top_rows, 1, 0))[None]


def _ones_tri(n_h, t):
    ti = lax.broadcasted_iota(jnp.int32, (t, t), 0)
    si = lax.broadcasted_iota(jnp.int32, (t, t), 1)
    return jnp.broadcast_to(jnp.where(ti >= si, 1.0, 0.0)[None], (n_h, t, t))


@jax.custom_vjp
def _running_sum_kept(logw, kept):
    return kept


def _running_sum_kept_bwd(shape, g):
    return _bmm_tn(_ones_tri(shape[0], shape[1]), g), jnp.zeros_like(g)


_running_sum_kept.defvjp(lambda logw, kept: (kept, logw.shape), _running_sum_kept_bwd)


@jax.custom_vjp
def _tri_products_kept(ar, bk, kept):
    return kept


def _tri_products_kept_bwd(res, g):
    ar, bk = res
    g = jnp.where(_tri_mask(ar.shape[1] // 2), g, 0.0)
    return _bmm(g, bk), _bmm_tn(g, ar), jnp.zeros_like(g)


_tri_products_kept.defvjp(lambda ar, bk, kept: (kept, (ar, bk)), _tri_products_kept_bwd)


@jax.custom_vjp
def _solve_unit_lower(a, rhs, inv, kept=None):
    return _bmm(inv, rhs) if kept is None else kept


def _solve_fwd(a, rhs, inv, kept=None):
    u = _bmm(inv, rhs) if kept is None else kept
    return u, (inv, u, kept is not None)


def _solve_bwd(res, du):
    inv, u, had_kept = res
    d_rhs = _bmm_tn(inv, du)
    return _bmm_nt(d_rhs, u), d_rhs, jnp.zeros_like(inv), (jnp.zeros_like(u) if had_kept else None)


_solve_unit_lower.defvjp(_solve_fwd, _solve_bwd)


def _rwkv_chunk(st0, r, logw, k, v, a, b, kept=None):
    n_h, t, _ = r.shape
    log_p = _bmm(_ones_tri(n_h, t), logw) if kept is None else _running_sum_kept(logw, kept[0])
    p_in = jnp.exp(log_p)
    p_inv = jnp.exp(-log_p)
    at = a * jnp.exp(log_p - logw)
    rt = r * p_in
    ar = jnp.concatenate([at, rt], axis=1)
    bk = jnp.concatenate([b * p_inv, k * p_inv], axis=1)
    if kept is None:
        m = jnp.where(_tri_mask(t), _bmm_nt(ar, bk), 0.0)
        inv = _unit_lower_inverse(m[:, :t, :t])
    else:
        m = _tri_products_kept(ar, bk, kept[1])
        inv = kept[2]
    top, bottom = m[:, :t], m[:, t:]
    rhs = _bmm_bf(jnp.concatenate([at, top[:, :, t:]], axis=2), jnp.concatenate([st0, v], axis=1))
    u = _solve_unit_lower(top[:, :, :t], rhs, inv, None if kept is None else kept[3])
    y = _bmm_bf(jnp.concatenate([rt, bottom], axis=2), jnp.concatenate([st0, u, v], axis=1))
    p_end = jnp.swapaxes(p_in[:, t - 1:t, :], 1, 2)
    st1 = (st0 + _bmm_tn_bf(bk, jnp.concatenate([u, v], axis=1))) * p_end
    return y, st1, (log_p, m, inv, u)


def _rwkv_scan_fwd(ops):
    n_h, L, n = N_HEADS, ops[0].shape[1], HEAD
    t = RWKV_CHUNK
    per = min(RWKV_CHUNKS_PER_STEP, L // t)
    n_c = L // t
    n_s = n_c // per

    def body(r_ref, w_ref, k_ref, v_ref, a_ref, b_ref, y_ref, st_ref, logp_ref, m_ref, inv_ref, u_ref, st):
        @pl.when(pl.program_id(0) == 0)
        def _():
            st[...] = jnp.zeros_like(st)

        st0 = st[...]
        for j in range(per):
            rows = slice(j * t, (j + 1) * t)
            st_ref[j] = st0
            y, st0, (log_p, m, inv, u) = _rwkv_chunk(
                st0, *(_split_pairs(ref[:, rows, :]) for ref in (r_ref, w_ref, k_ref, v_ref, a_ref, b_ref)))
            y_ref[:, rows, :] = _join_pairs(y)
            logp_ref[:, rows, :] = log_p
            u_ref[:, rows, :] = u
            m_ref[j] = m
            inv_ref[j] = inv
        st[...] = st0

    pairs = pl.BlockSpec((N_PAIRS, per * t, LANES), lambda c: (0, c, 0))
    blk = pl.BlockSpec((n_h, per * t, n), lambda c: (0, c, 0))
    per_chunk = lambda m: pl.BlockSpec((per, n_h, m, m), lambda c: (c, 0, 0, 0))
    return pl.pallas_call(
        body, name="rwkv_scan_fwd", grid=(n_s,), in_specs=[pairs] * 6,
        out_specs=[pairs, per_chunk(n), blk, per_chunk(2 * t), per_chunk(t), blk],
        out_shape=[_sds((N_PAIRS, L, LANES)), _sds((n_c, n_h, n, n)), _sds((n_h, L, n)),
                   _sds((n_c, n_h, 2 * t, 2 * t)), _sds((n_c, n_h, t, t)), _sds((n_h, L, n))],
        scratch_shapes=[pltpu.VMEM((n_h, n, n), F32)],
        compiler_params=_params("arbitrary"),
    )(*ops)


def _rwkv_scan_bwd(ops, states, kept, dy):
    n_h, L, n = N_HEADS, ops[0].shape[1], HEAD
    t = RWKV_CHUNK
    per = min(RWKV_CHUNKS_PER_STEP, L // t)
    n_s = L // t // per

    def body(r_ref, w_ref, k_ref, v_ref, a_ref, b_ref, st_ref, logp_ref, m_ref, inv_ref, u_ref, dy_ref,
             dr, dw, dk, dv, da, db, dst):
        @pl.when(pl.program_id(0) == 0)
        def _():
            dst[...] = jnp.zeros_like(dst)

        vjps = []
        for j in range(per):
            rows = slice(j * t, (j + 1) * t)
            have = (logp_ref[:, rows, :], m_ref[j], inv_ref[j], u_ref[:, rows, :])
            args = [_split_pairs(ref[:, rows, :]) for ref in (r_ref, w_ref, k_ref, v_ref, a_ref, b_ref)]
            vjps.append(jax.vjp(lambda *a, have=have: _rwkv_chunk(*a, kept=have)[:2], st_ref[j], *args)[1])
        d_state = dst[...]
        for j in reversed(range(per)):
            rows = slice(j * t, (j + 1) * t)
            g = vjps[j]((_split_pairs(dy_ref[:, rows, :]), d_state))
            d_state = g[0]
            for out, val in zip((dr, dw, dk, dv, da, db), g[1:]):
                out[:, rows, :] = _join_pairs(val)
        dst[...] = d_state

    pairs = pl.BlockSpec((N_PAIRS, per * t, LANES), lambda c: (0, n_s - 1 - c, 0))
    blk = pl.BlockSpec((n_h, per * t, n), lambda c: (0, n_s - 1 - c, 0))
    per_chunk = lambda m: pl.BlockSpec((per, n_h, m, m), lambda c: (n_s - 1 - c, 0, 0, 0))
    return pl.pallas_call(
        body, name="rwkv_scan_bwd", grid=(n_s,),
        in_specs=[pairs] * 6 + [per_chunk(n), blk, per_chunk(2 * t), per_chunk(t), blk, pairs],
        out_specs=[pairs] * 6, out_shape=[_sds((N_PAIRS, L, LANES))] * 6,
        scratch_shapes=[pltpu.VMEM((n_h, n, n), F32)],
        compiler_params=_params("arbitrary"),
    )(*ops, states, *kept, dy)


def _post_fn(x, u, zs, zr, ysc, r, k2, v, y_ssm, d, glu_w, glu_b, ln_w, ln_b, r_k,
             wo_s5, wo_rwkv, gf, tgt, ee):
    y3 = _gelu(y_ssm + d * u)
    y_s5 = y3 * _sigmoid(mm_bf(y3, glu_w) + glu_b) * _silu(zs)
    mean = head_sum_split(ysc, ee) * (1.0 / HEAD)
    yc = ysc - mean
    var = head_sum(yc * yc, ee) * (1.0 / HEAD)
    gn = yc * lax.rsqrt(var + GN_EPS) * ln_w + ln_b
    bonus = head_sum(r * k2 * r_k, ee) * v
    y_rwkv = (gn + bonus) * _silu(zr)
    x2 = x + mm_bf(y_s5, wo_s5) + mm_bf(y_rwkv, wo_rwkv)
    err = _rms(x2, gf) - tgt
    return 0.5 * jnp.mean(err * err, axis=-1, keepdims=True)


def _post(x, u, zs, zr, ysc, r, k2, v, y_ssm, d, glu_w, glu_b, ln_w, ln_b, r_k, w_out, gf, tgt, ee, tt):
    L = x.shape[0]
    n_t = L // tt
    acc_shapes = [(1, D_S5), (D_S5, D_S5), (1, D_S5), (1, D_RWKV), (1, D_RWKV), (1, D_RWKV),
                  (D_MODEL, D_MODEL), (1, D_MODEL), (8, LANES)]

    def body(x_ref, u_ref, zs_ref, zr_ref, ysc_ref, r_ref, k2_ref, v_ref, yssm_ref,
             d_ref, gw_ref, gb_ref, lw_ref, lb_ref, rk_ref, wo_ref, gf_ref, tgt_ref, ee_ref,
             dx_o, du_o, dzs_o, dzr_o, dysc_o, dr_o, dk2_o, dv_o, dyssm_o,
             dd_o, dgw_o, dgb_o, dlw_o, dlb_o, drk_o, dwo_o, dgf_o, loss_o,
             dd, dgw, dgb, dlw, dlb, drk, dwo, dgf, loss):
        i = pl.program_id(0)
        accs = (dd, dgw, dgb, dlw, dlb, drk, dwo, dgf, loss)

        @pl.when(i == 0)
        def _():
            for acc in accs:
                acc[...] = jnp.zeros_like(acc)

        args = (x_ref[...], u_ref[...], zs_ref[...], zr_ref[...],
                _load_heads(ysc_ref), _load_heads(r_ref), _load_heads(k2_ref), _load_heads(v_ref), yssm_ref[...],
                d_ref[...], gw_ref[...], gb_ref[...], lw_ref[...], lb_ref[...], rk_ref[...],
                wo_ref[0:D_S5, :], wo_ref[D_S5:D_MODEL, :], gf_ref[...])
        rows, vjp = jax.vjp(lambda *a: _post_fn(*a, tgt_ref[...], ee_ref[...]), *args)
        g = vjp(jnp.ones_like(rows))
        for out, val in zip((dx_o, du_o, dzs_o, dzr_o), g[0:4]):
            out[...] = val
        _store_heads(dysc_o, g[4])
        for out, val in zip((dr_o, dk2_o, dv_o, dyssm_o), g[5:9]):
            out[...] = val
        for acc, val in zip((dd, dgw, dgb, dlw, dlb, drk), g[9:15]):
            acc[...] += val
        dwo[0:D_S5, :] += g[15]
        dwo[D_S5:D_MODEL, :] += g[16]
        dgf[...] += g[17]
        loss[...] += jnp.broadcast_to(jnp.sum(rows, axis=0, keepdims=True), loss.shape)

        @pl.when(i == n_t - 1)
        def _():
            for acc, out in zip(accs, (dd_o, dgw_o, dgb_o, dlw_o, dlb_o, drk_o, dwo_o, dgf_o, loss_o)):
                pltpu.sync_copy(acc, out)

    row = lambda n: pl.BlockSpec((tt, n), lambda i: (i, 0))
    in_specs = ([row(D_MODEL)] + [row(512)] * 3 + [_head_spec(tt)] * 4 + [row(D_S5)]
                + [_const_spec(s) for s in [(1, D_S5), (D_S5, D_S5), (1, D_S5), (1, D_RWKV), (1, D_RWKV), (1, D_RWKV),
                                            (D_MODEL, D_MODEL), (1, D_MODEL)]]
                + [row(D_MODEL), _const_spec((D_RWKV, D_RWKV))])
    out_rows = [D_MODEL] + [512] * 3 + [None] + [512] * 4
    return pl.pallas_call(
        body, name="post_fwd_bwd", grid=(n_t,), in_specs=in_specs,
        out_specs=[row(n) if n else _head_spec(tt) for n in out_rows] + [_ANY] * len(acc_shapes),
        out_shape=([_sds((L, n)) if n else _sds((N_PAIRS, L, LANES)) for n in out_rows]
                   + [_sds(s) for s in acc_shapes]),
        scratch_shapes=[pltpu.VMEM(s, F32) for s in acc_shapes],
        compiler_params=_params("arbitrary"),
    )(x, u, zs, zr, ysc, r, k2, v, y_ssm, d, glu_w, glu_b, ln_w, ln_b, r_k, w_out, gf, tgt, ee)


def _s5_bwd(u, du_direct, dy, s_re, s_im, b4_re, b4_im, c4_re, c4_im, pow_r, pow_i, tt):
    L = u.shape[0]
    n_t = L // tt
    acc_shapes = ([(S5_BLOCKS, LANES, 512)] * 2 + [(S5_BLOCKS, 512, LANES)] * 2 + [(1, N_STATE)] * 2)

    def body(u_ref, dud_ref, dy_ref, sre_ref, sim_ref, pre_ref, pim_ref, bre_ref, bim_ref, cre_ref, cim_ref,
             pr_ref, pi_ref, du_o, dbre_o, dbim_o, dcre_o, dcim_o, dlr_o, dli_o,
             dbre, dbim, dcre, dcim, dlr, dli, gre, gim, car_r, car_i):
        i = pl.program_id(0)

        @pl.when(i == 0)
        def _():
            for acc in (dbre, dbim, dcre, dcim, dlr, dli, car_r, car_i):
                acc[...] = jnp.zeros_like(acc)

        uv = u_ref[...]
        dyv = dy_ref[...]
        blocks = [slice(q * 512, (q + 1) * 512) for q in range(S5_BLOCKS)]
        lanes = [slice(q * LANES, (q + 1) * LANES) for q in range(S5_BLOCKS)]
        for q in range(S5_BLOCKS):
            gre[:, blocks[q]] = _dot_bf(dyv[:, lanes[q]], cre_ref[q], ((1,), (1,)))
            gim[:, blocks[q]] = -_dot_bf(dyv[:, lanes[q]], cim_ref[q], ((1,), (1,)))
        _tile_scan(gre, gim, pr_ref, pi_ref, car_r, car_i, reverse=True)
        for q in range(S5_BLOCKS):
            gr = gre[:, blocks[q]]
            gi = gim[:, blocks[q]]
            sr = sre_ref[:, blocks[q]]
            si = sim_ref[:, blocks[q]]
            du_o[:, lanes[q]] = (dud_ref[:, lanes[q]] + _dot_bf(gr, bre_ref[q], ((1,), (1,)))
                                 + _dot_bf(gi, bim_ref[q], ((1,), (1,))))
            dbre[q] += _dot_bf(uv[:, lanes[q]], gr, ((0,), (0,)))
            dbim[q] += _dot_bf(uv[:, lanes[q]], gi, ((0,), (0,)))
            dcre[q] += _dot_bf(sr, dyv[:, lanes[q]], ((0,), (0,)))
            dcim[q] -= _dot_bf(si, dyv[:, lanes[q]], ((0,), (0,)))
            rid = lax.broadcasted_iota(jnp.int32, sr.shape, 0)
            first = i == n_t - 1
            prev_r = jnp.where(first, 0.0, pre_ref[7:8, blocks[q]])
            prev_i = jnp.where(first, 0.0, pim_ref[7:8, blocks[q]])
            pr = jnp.where(rid == 0, jnp.broadcast_to(prev_r, sr.shape), pltpu.roll(sr, 1, axis=0))
            pi_ = jnp.where(rid == 0, jnp.broadcast_to(prev_i, si.shape), pltpu.roll(si, 1, axis=0))
            dlr[:, blocks[q]] += jnp.sum(pr * gr + pi_ * gi, axis=0, keepdims=True)
            dli[:, blocks[q]] += jnp.sum(pr * gi - pi_ * gr, axis=0, keepdims=True)

        @pl.when(i == n_t - 1)
        def _():
            for acc, out in zip((dbre, dbim, dcre, dcim, dlr, dli), (dbre_o, dbim_o, dcre_o, dcim_o, dlr_o, dli_o)):
                out[...] = acc[...]

    row = lambda n: pl.BlockSpec((tt, n), lambda i: (n_t - 1 - i, 0))
    prev = pl.BlockSpec((8, N_STATE), lambda i: (jnp.maximum((n_t - 1 - i) * (tt // 8) - 1, 0), 0))
    return pl.pallas_call(
        body, name="s5_bwd", grid=(n_t,),
        in_specs=[row(D_S5)] * 3 + [row(N_STATE)] * 2 + [prev] * 2
        + [_const_spec((S5_BLOCKS, LANES, 512))] * 2 + [_const_spec((S5_BLOCKS, 512, LANES))] * 2
        + [_const_spec((8, N_STATE))] * 2,
        out_specs=[row(D_S5)] + [_acc_spec(s) for s in acc_shapes],
        out_shape=[_sds((L, D_S5))] + [_sds(s) for s in acc_shapes],
        scratch_shapes=[pltpu.VMEM(s, F32) for s in acc_shapes] + [pltpu.VMEM((tt, N_STATE), F32)] * 2
        + [pltpu.VMEM((8, N_STATE), F32)] * 2,
        compiler_params=_params("arbitrary"),
    )(u, du_direct, dy, s_re, s_im, s_re, s_im, b4_re, b4_im, c4_re, c4_im, pow_r, pow_i)


def _bwd_in(x, norm_g, w_in_bf, mu, dx2, du, dzs, drws, dzr, tt):
    L = x.shape[0]
    n_t = L // tt

    def body(x_ref, g_ref, w_ref, mu_ref, dx2_ref, du_ref, dzs_ref, drws_ref, nxt_ref, dzr_ref,
             gx_o, dw_o, dg_o, dproj, dw, dg):
        i = pl.program_id(0)

        @pl.when(i == 0)
        def _():
            dw[...] = jnp.zeros_like(dw)
            dg[...] = jnp.zeros_like(dg)

        drws_v = drws_ref[...]
        rid = lax.broadcasted_iota(jnp.int32, drws_v.shape, 0)
        nxt_row = jnp.where(i == n_t - 1, 0.0, nxt_ref[0:1, :])
        nxt = jnp.where(rid == tt - 1, jnp.broadcast_to(nxt_row, drws_v.shape), pltpu.roll(drws_v, tt - 1, axis=0))
        muv = mu_ref[...]
        drw = drws_v * (1.0 - muv) + nxt * muv
        dproj[:, 0:D_S5] = du_ref[...].astype(BF16)
        dproj[:, D_S5:2 * D_S5] = dzs_ref[...].astype(BF16)
        dproj[:, 2 * D_S5:2 * D_S5 + D_SHIFT] = drw.astype(BF16)
        dproj[:, 2 * D_S5 + D_SHIFT:D_IN] = dzr_ref[...].astype(BF16)
        dh = _dot(dproj[...], w_ref[...], ((1,), (1,)), None)
        h, vjp = jax.vjp(_rms, x_ref[...], g_ref[...])
        dxh, dgv = vjp(dh)
        gx_o[...] = dx2_ref[...] + dxh
        dg[...] += dgv
        dw[...] += _dot(h.astype(BF16), dproj[...], ((0,), (0,)), None)

        @pl.when(i == n_t - 1)
        def _():
            dg_o[...] = dg[...]
            pltpu.sync_copy(dw, dw_o)

    row = lambda n: pl.BlockSpec((tt, n), lambda i: (i, 0))
    nxt = pl.BlockSpec((8, D_SHIFT), lambda i: (jnp.minimum((i + 1) * (tt // 8), L // 8 - 1), 0))
    return pl.pallas_call(
        body, name="bwd_in", grid=(n_t,),
        in_specs=[row(D_MODEL), _const_spec((1, D_MODEL)), _const_spec((D_MODEL, D_IN)), _const_spec((1, D_SHIFT)),
                  row(D_MODEL), row(D_S5), row(D_S5), row(D_SHIFT), nxt, row(D_RWKV)],
        out_specs=[row(D_MODEL), _ANY, _acc_spec((1, D_MODEL))],
        out_shape=[_sds((L, D_MODEL)), _sds((D_MODEL, D_IN)), _sds((1, D_MODEL))],
        scratch_shapes=[pltpu.VMEM((tt, D_IN), BF16), pltpu.VMEM((D_MODEL, D_IN), F32), pltpu.VMEM((1, D_MODEL), F32)],
        compiler_params=_params("arbitrary"),
    )(x, norm_g, w_in_bf, mu, dx2, du, dzs, drws, drws, dzr)


def _block_diag_b(bbar):
    bb = bbar.reshape(S5_GROUP, S5_BLOCKS, 8, S5_STATE)
    return jnp.einsum('hqgp,Gg->qGhgp', bb, jnp.eye(8, dtype=F32)).reshape(S5_BLOCKS, LANES, 512)


def _block_diag_b_t(db4):
    d = db4.reshape(S5_BLOCKS, 8, S5_GROUP, 8, S5_STATE)
    return jnp.einsum('qGhgp,Gg->hqgp', d, jnp.eye(8, dtype=F32)).reshape(S5_GROUP, N_STATE)


def _block_diag_c(c):
    cc = c.reshape(S5_BLOCKS, 8, S5_GROUP, S5_STATE)
    return jnp.einsum('qghp,gG->qgpGh', cc, jnp.eye(8, dtype=F32)).reshape(S5_BLOCKS, 512, LANES)


def _block_diag_c_t(dc4):
    d = dc4.reshape(S5_BLOCKS, 8, S5_STATE, 8, S5_GROUP)
    return jnp.einsum('qgpGh,gG->qghp', d, jnp.eye(8, dtype=F32)).reshape(S5_GROUPS, S5_GROUP, S5_STATE)


def _local_step(x, tgt, w):
    L = x.shape[0]
    tt = min(512, L)
    tp = min(256, L)
    ee = _head_sum_matrix()

    lam_re = w['s5_lam_re'].reshape(1, N_STATE)
    lam_im = w['s5_lam_im'].reshape(1, N_STATE)
    logdt = jnp.repeat(w['s5_log_dt'], S5_STATE).reshape(1, N_STATE)
    b_re_t = w['s5_b_re'].transpose(2, 0, 1).reshape(S5_GROUP, N_STATE)
    b_im_t = w['s5_b_im'].transpose(2, 0, 1).reshape(S5_GROUP, N_STATE)
    bbr, bbi, pow_r, pow_i, rpow_r, rpow_i = _s5_param_fwd(lam_re, lam_im, logdt, b_re_t, b_im_t)
    b4_re, b4_im = _block_diag_b(bbr), _block_diag_b(bbi)
    c4_re, c4_im = _block_diag_c(w['s5_c_re']), _block_diag_c(w['s5_c_im'])

    norm_g = w['norm_g'].reshape(1, D_MODEL)
    w_in_bf = w['w_in'].astype(BF16)
    u, zs, rw, zr = _fwd_in(x, norm_g, w_in_bf, tt)
    s_re, s_im, y_ssm = _s5_fwd(u, b4_re, b4_im, c4_re, c4_im, pow_r, pow_i, tt)

    row = lambda t: t.reshape(1, -1)
    zpad = jnp.zeros((HEAD, D_RWKV), F32)
    w2p = jnp.concatenate([w['rwkv_w2'], zpad], axis=0)
    a2p = jnp.concatenate([zpad, w['rwkv_a2']], axis=0)
    pre_consts = (row(w['rwkv_mu']), row(w['rwkv_w0']), w2p, row(w['rwkv_a0']), a2p,
                  row(w['rwkv_k_k']), row(w['rwkv_k_a']), ee)
    ops = _rwkv_pre_fwd(rw, *pre_consts, tt)
    ysc, states, *kept = _rwkv_scan_fwd(ops)

    post = _post(x, u, zs, zr, ysc, ops[0], ops[2], ops[3], y_ssm,
                 row(w['s5_d']), w['s5_glu_w'], row(w['s5_glu_b']), row(w['rwkv_ln_w']), row(w['rwkv_ln_b']),
                 row(w['rwkv_r_k']), w['w_out'], row(w['final_g']), tgt, ee, tp)
    (dx2, du_d, dzs, dzr, dysc, dr_b, dk2_b, dv_b, dy_ssm,
     dd, dglu_w, dglu_b, dln_w, dln_b, dr_k, dw_out, dgf, loss) = post

    du, db4_re, db4_im, dc4_re, dc4_im, dlbr, dlbi = _s5_bwd(
        u, du_d, dy_ssm, s_re, s_im, b4_re, b4_im, c4_re, c4_im, rpow_r, rpow_i, tt)
    group_ind = (jnp.arange(N_STATE)[:, None] // S5_STATE == jnp.arange(LANES)[None, :]).astype(F32)
    dlam_re, dlam_im, dlogdt, db_re_t, db_im_t = _s5_param_bwd(
        lam_re, lam_im, logdt, b_re_t, b_im_t, dlbr, dlbi, _block_diag_b_t(db4_re), _block_diag_b_t(db4_im), group_ind)

    cots = list(_rwkv_scan_bwd(ops, states, kept, dysc)) + [dr_b, dk2_b, dv_b]
    drws, dmu, dw0, dw2p, da0, da2p, dk_k, dk_a = _rwkv_pre_bwd(rw, *pre_consts, cots, tt)

    grad_x, dw_in, dnorm_g = _bwd_in(x, norm_g, w_in_bf, row(w['rwkv_mu']), dx2, du, dzs, drws, dzr, tt)

    unb = lambda t: t.reshape(S5_GROUP, S5_GROUPS, S5_STATE).transpose(1, 2, 0)
    grads = {
        'norm_g': dnorm_g.reshape(D_MODEL), 'w_in': dw_in,
        's5_lam_re': dlam_re.reshape(S5_GROUPS, S5_STATE), 's5_lam_im': dlam_im.reshape(S5_GROUPS, S5_STATE),
        's5_log_dt': dlogdt[0, :S5_GROUPS], 's5_b_re': unb(db_re_t), 's5_b_im': unb(db_im_t),
        's5_c_re': _block_diag_c_t(dc4_re), 's5_c_im': _block_diag_c_t(dc4_im),
        's5_d': dd.reshape(D_S5), 's5_glu_w': dglu_w, 's5_glu_b': dglu_b.reshape(D_S5),
        'rwkv_mu': dmu.reshape(-1), 'rwkv_w0': dw0.reshape(-1), 'rwkv_w2': dw2p[:HEAD], 'rwkv_a0': da0.reshape(-1),
        'rwkv_a2': da2p[HEAD:], 'rwkv_k_k': dk_k.reshape(-1), 'rwkv_k_a': dk_a.reshape(-1),
        'rwkv_r_k': dr_k.reshape(N_HEADS, HEAD), 'rwkv_ln_w': dln_w.reshape(-1), 'rwkv_ln_b': dln_b.reshape(-1),
        'w_out': dw_out, 'final_g': dgf.reshape(D_MODEL),
    }
    return loss, grad_x, grads


def _exchange(arrays, gather, axes, name):
    n = len(arrays)
    group = 2 ** len(axes)

    def body(*refs):
        send_refs, recv_refs = refs[:n], refs[n:2 * n]
        send_sems, recv_sems, local_sems = refs[2 * n:]
        pos = {ax: lax.axis_index(ax) for ax in ("x", "y", "c")}

        def index_of(p):
            idx = 0
            for ax in axes:
                idx = 2 * idx + p[ax]
            return idx

        me = index_of(pos)
        own, outs, arrivals = [], [], []
        for i, (send_ref, recv_ref) in enumerate(zip(send_refs, recv_refs)):
            def block_for(dev, send_ref=send_ref, whole=gather[i]):
                return send_ref if whole else send_ref.at[dev]

            own.append(pltpu.make_async_copy(block_for(me), recv_ref.at[me], local_sems.at[i]))
            own[-1].start()
            for k in range(1, group):
                peer = dict(pos)
                for bit, ax in enumerate(axes):
                    if (k >> bit) & 1:
                        peer[ax] = 1 - pos[ax]
                peer_idx = index_of(peer)
                sems = dict(send_sem=send_sems.at[i, k - 1], recv_sem=recv_sems.at[i, k - 1],
                            device_id=(peer["x"], peer["y"], peer["c"]), device_id_type=pl.DeviceIdType.MESH)
                outs.append(pltpu.make_async_remote_copy(src_ref=block_for(peer_idx), dst_ref=recv_ref.at[me], **sems))
                outs[-1].start()
                arrivals.append(
                    pltpu.make_async_remote_copy(src_ref=block_for(peer_idx), dst_ref=recv_ref.at[peer_idx], **sems))
        for copy in arrivals:
            copy.wait_recv()
        for copy in outs:
            copy.wait_send()
        for copy in own:
            copy.wait()

    return pl.pallas_call(
        body, name=name, in_specs=[_ANY] * n, out_specs=[_ANY] * n,
        out_shape=[jax.ShapeDtypeStruct(((group,) + a.shape) if whole else a.shape, a.dtype)
                   for a, whole in zip(arrays, gather)],
        scratch_shapes=[pltpu.SemaphoreType.DMA((n, group - 1)), pltpu.SemaphoreType.DMA((n, group - 1)),
                        pltpu.SemaphoreType.DMA((n,))],
        compiler_params=pltpu.CompilerParams(has_side_effects=True),
    )(*arrays)


def _sum_devices(ref):
    g = ref[0].astype(F32)
    for s in range(1, ref.shape[0]):
        g = g + ref[s].astype(F32)
    return g


def _adamw_math(g, w, m, v):
    m_new = ADAM_B1 * m + (1.0 - ADAM_B1) * g
    v_new = ADAM_B2 * v + (1.0 - ADAM_B2) * (g * g)
    m_hat = m_new / (1.0 - ADAM_B1 ** ADAM_STEP)
    v_hat = v_new / (1.0 - ADAM_B2 ** ADAM_STEP)
    return -ADAM_LR * (m_hat / (jnp.sqrt(v_hat) + ADAM_EPS) + ADAM_WD * w), m_new, v_new


def _adamw(gs, ws, ms, vs, reduce, name):
    n = len(ws)

    def body(*refs):
        g_refs, w_refs, m_refs, v_refs = (refs[j * n:(j + 1) * n] for j in range(4))
        outs = refs[4 * n:]
        for i in range(n):
            g = _sum_devices(g_refs[i]) if reduce else g_refs[i][...]
            res = _adamw_math(g, w_refs[i][...], m_refs[i][...], v_refs[i][...])
            for j, val in enumerate(((g,) if reduce else ()) + res):
                outs[j * n + i][...] = val

    return pl.pallas_call(
        body, name=name, out_shape=[_sds(w.shape) for w in ws] * (4 if reduce else 3),
        compiler_params=pltpu.CompilerParams(vmem_limit_bytes=VMEM_LIMIT),
    )(*gs, *ws, *ms, *vs)


def _sum_blocks(recv):
    def body(recv_ref, out_ref):
        out_ref[...] = _sum_devices(recv_ref)

    return pl.pallas_call(body, name="sum_small_grads", out_shape=_sds(recv.shape[1:]))(recv)


_WEIGHTS = [
    ('norm_g', (1, 1024), False), ('w_in', (1, 1024, 400), True), ('s5_lam_re', (1, 32, 64), False),
    ('s5_lam_im', (1, 32, 64), False), ('s5_log_dt', (1, 32), False), ('s5_b_re', (1, 32, 64, 16), False),
    ('s5_b_im', (1, 32, 64, 16), False), ('s5_c_re', (1, 32, 16, 64), False), ('s5_c_im', (1, 32, 16, 64), False),
    ('s5_d', (1, 512), False), ('s5_glu_w', (1, 64, 512), True), ('s5_glu_b', (1, 512), False),
    ('rwkv_mu', (1, 1664), False), ('rwkv_w0', (1, 512), False), ('rwkv_w2', (1, 64, 64), True),
    ('rwkv_a0', (1, 512), False), ('rwkv_a2', (1, 64, 64), True), ('rwkv_k_k', (1, 512), False),
    ('rwkv_k_a', (1, 512), False), ('rwkv_r_k', (1, 8, 64), False), ('rwkv_ln_w', (1, 512), False),
    ('rwkv_ln_b', (1, 512), False), ('w_out', (1, 128, 1024), True), ('final_g', (1024,), False),
]
_SHARDED = [(n, s) for n, s, sharded in _WEIGHTS if sharded]
_SMALL = [(n, s) for n, s, sharded in _WEIGHTS if not sharded]
_COLUMN_SHARDED = ('w_in', 'rwkv_w2', 'rwkv_a2')
_SMALL_SIZE = sum(math.prod(s) for _, s in _SMALL) + 1
_SMALL_ROWS = -(-_SMALL_SIZE // (8 * LANES)) * 8


def _pack_small(grads, loss):
    flat = [grads[n].reshape(-1) for n, _ in _SMALL] + [loss.reshape(1)]
    pad = _SMALL_ROWS * LANES - _SMALL_SIZE
    return jnp.concatenate(flat + [jnp.zeros((pad,), F32)]).reshape(_SMALL_ROWS, LANES)


def _unpack_small(packed):
    flat = packed.reshape(-1)
    out, off = {}, 0
    for n, s in _SMALL:
        size = math.prod(s)
        out[n] = flat[off:off + size].reshape(s)
        off += size
    return out, flat[off]


_BF16_OPERANDS = ('w_in', 's5_glu_w', 'w_out')


def _lane_dense(a):
    rows, cols = a.shape[-2:]
    return a if cols % LANES == 0 else a.reshape(a.shape[:-2] + (rows * cols // LANES, LANES))


def _join_shards(name, blocks):
    _, rows, cols = blocks.shape
    if name in _COLUMN_SHARDED:
        return blocks.transpose(1, 0, 2).reshape(rows, N_DEV * cols)
    return blocks.reshape(N_DEV * rows, cols)


def _split_shards(name, full, shard_shape):
    rows, cols = shard_shape
    if name in _COLUMN_SHARDED:
        return full.reshape(rows, N_DEV, cols).transpose(1, 0, 2)
    return full.reshape(N_DEV, rows, cols)


def kernel(x, norm_g, w_in, s5_lam_re, s5_lam_im, s5_log_dt, s5_b_re, s5_b_im, s5_c_re, s5_c_im, s5_d, s5_glu_w, s5_glu_b, rwkv_mu, rwkv_w0, rwkv_w2, rwkv_a0, rwkv_a2, rwkv_k_k, rwkv_k_a, rwkv_r_k, rwkv_ln_w, rwkv_ln_b, w_out, final_g, loss_target, m_norm_g, m_w_in, m_s5_lam_re, m_s5_lam_im, m_s5_log_dt, m_s5_b_re, m_s5_b_im, m_s5_c_re, m_s5_c_im, m_s5_d, m_s5_glu_w, m_s5_glu_b, m_rwkv_mu, m_rwkv_w0, m_rwkv_w2, m_rwkv_a0, m_rwkv_a2, m_rwkv_k_k, m_rwkv_k_a, m_rwkv_r_k, m_rwkv_ln_w, m_rwkv_ln_b, m_w_out, m_final_g, v_norm_g, v_w_in, v_s5_lam_re, v_s5_lam_im, v_s5_log_dt, v_s5_b_re, v_s5_b_im, v_s5_c_re, v_s5_c_im, v_s5_d, v_s5_glu_w, v_s5_glu_b, v_rwkv_mu, v_rwkv_w0, v_rwkv_w2, v_rwkv_a0, v_rwkv_a2, v_rwkv_k_k, v_rwkv_k_a, v_rwkv_r_k, v_rwkv_ln_w, v_rwkv_ln_b, v_w_out, v_final_g):
    given = dict(locals())

    n_sh = len(_SHARDED)
    everyone = ("x", "y", "c")
    shards = [_lane_dense(given[n][0].astype(BF16 if n in _BF16_OPERANDS else F32)) for n, _ in _SHARDED]
    gathered = _exchange(shards, (True,) * n_sh, everyone, "gather_weights")
    local = {n: _join_shards(n, blocks.reshape((N_DEV,) + s[1:])).astype(F32 if n != 'w_in' else BF16)
             for (n, s), blocks in zip(_SHARDED, gathered)}
    local.update({n: (given[n][0] if len(s) > 1 else given[n]) for n, s in _SMALL})

    loss, grad_x, grads = _local_step(x[0], loss_target[0], local)

    blocks = [_lane_dense(_split_shards(n, grads[n], s[1:]).astype(BF16)) for n, s in _SHARDED]
    small = _pack_small(grads, loss[0, 0]).reshape(N_DEV, _SMALL_ROWS // N_DEV, LANES)
    recv = _exchange(blocks + [small], (False,) * (n_sh + 1), everyone, "exchange_grads")
    recv = [r.reshape((N_DEV,) + s[1:]) for r, (_, s) in zip(recv, _SHARDED)] + [recv[-1]]
    small_sum = _exchange([_sum_blocks(recv[-1])], (True,), everyone, "gather_small_grads")[0]

    result = {}
    for group, name in (([0], "adamw_w_in"), ([1, 2, 3, 4], "adamw_shards")):
        ns = [_SHARDED[i][0] for i in group]
        res = _adamw([recv[i] for i in group], [given[n][0] for n in ns], [given['m_' + n][0] for n in ns],
                     [given['v_' + n][0] for n in ns], True, name)
        for j, n in enumerate(ns):
            result[n] = [res[k * len(ns) + j][None] for k in range(4)]
    g_small, total = _unpack_small(small_sum)
    two_d = lambda t: t.reshape(1, -1) if t.ndim == 1 else t
    ns = [n for n, _ in _SMALL]
    res = _adamw([two_d(g_small[n]) for n in ns], [two_d(given[n]) for n in ns], [two_d(given['m_' + n]) for n in ns],
                 [two_d(given['v_' + n]) for n in ns], False, "adamw_small")
    for j, (n, s) in enumerate(_SMALL):
        result[n] = [g_small[n]] + [res[k * len(ns) + j].reshape(s) for k in range(3)]

    outs = [total, grad_x[None]]
    for k in range(4):
        outs += [result[n][k] for n, _, _ in _WEIGHTS]
    return tuple(outs)
```

```python
import math

import jax
import jax.numpy as jnp
from jax import lax
from jax.experimental import pallas as pl
from jax.experimental.pallas import tpu as pltpu

F32 = jnp.float32
BF16 = jnp.bfloat16
HI = lax.Precision.HIGH

D_MODEL = 1024
D_S5 = 512
D_RWKV = 512
S5_GROUPS = 32
S5_GROUP = 16
S5_STATE = 64
N_STATE = S5_GROUPS * S5_STATE
N_HEADS = 8
HEAD = 64
D_SHIFT = 3 * D_RWKV + 128
D_IN = 2 * D_S5 + D_SHIFT + D_RWKV
NORM_EPS = 1e-6
GN_EPS = 64e-5
N_DEV = 8
LANES = 128
S5_BLOCKS = 4
RWKV_CHUNK = 64
RWKV_CHUNKS_PER_STEP = 4
VMEM_LIMIT = 56 * 1024 * 1024

ADAM_LR = 0.001
ADAM_B1 = 0.9
ADAM_B2 = 0.999
ADAM_EPS = 1e-08
ADAM_WD = 0.01
ADAM_STEP = 10


def _dot(a, b, dims, prec):
    return lax.dot_general(a, b, (dims, ((), ())), precision=prec, preferred_element_type=F32)


def _dot_bf(a, b, dims):
    return _dot(a.astype(BF16), b.astype(BF16), dims, None)


def _make_mm(cast, prec):
    @jax.custom_vjp
    def mm(a, b):
        return _dot(cast(a), cast(b), ((1,), (0,)), prec)

    def fwd(a, b):
        return mm(a, b), (a, b)

    def bwd(res, g):
        a, b = res
        return (_dot(cast(g), cast(b), ((1,), (1,)), prec), _dot(cast(a), cast(g), ((0,), (0,)), prec))

    mm.defvjp(fwd, bwd)
    return mm


mm_bf = _make_mm(lambda t: t.astype(BF16), None)


def _make_head_sum(split):
    def product(x, ee):
        hi = x.astype(BF16)
        out = _dot(hi, ee, ((1,), (0,)), None)
        if split:
            out = out + _dot((x - hi.astype(F32)).astype(BF16), ee, ((1,), (0,)), None)
        return out

    @jax.custom_vjp
    def head_sum(x, ee):
        return product(x, ee)

    def fwd(x, ee):
        return product(x, ee), ee

    def bwd(ee, g):
        return product(g, ee), jnp.zeros_like(ee)

    head_sum.defvjp(fwd, bwd)
    return head_sum


head_sum = _make_head_sum(False)
head_sum_split = _make_head_sum(True)


@jax.custom_vjp
def _sigmoid(x):
    return 1.0 / (1.0 + jnp.exp(-x))


def _sigmoid_fwd(x):
    s = _sigmoid(x)
    return s, s


_sigmoid.defvjp(_sigmoid_fwd, lambda s, g: (g * s * (1.0 - s),))


@jax.custom_vjp
def _silu(x):
    return x * _sigmoid(x)


def _silu_fwd(x):
    s = _sigmoid(x)
    return x * s, (x, s)


def _silu_bwd(res, g):
    x, s = res
    return (g * s * (1.0 + x * (1.0 - s)),)


_silu.defvjp(_silu_fwd, _silu_bwd)


@jax.custom_vjp
def _softplus(x):
    return jnp.maximum(x, 0.0) + jnp.log(1.0 + jnp.exp(-jnp.abs(x)))


def _softplus_fwd(x):
    e = jnp.exp(-jnp.abs(x))
    return jnp.maximum(x, 0.0) + jnp.log(1.0 + e), (x, e)


def _softplus_bwd(res, g):
    x, e = res
    return (g * jnp.where(x >= 0.0, 1.0, e) / (1.0 + e),)


_softplus.defvjp(_softplus_fwd, _softplus_bwd)


@jax.custom_vjp
def _normalize_heads(x, ee):
    return x / jnp.maximum(jnp.sqrt(head_sum(x * x, ee)), 1e-12)


def _normalize_heads_fwd(x, ee):
    norm = jnp.sqrt(head_sum(x * x, ee))
    inv = 1.0 / jnp.maximum(norm, 1e-12)
    y = x * inv
    return y, (y, inv, norm, ee)


def _normalize_heads_bwd(res, g):
    y, inv, norm, ee = res
    along = jnp.where(norm > 1e-12, head_sum(g * y, ee), 0.0)
    return inv * (g - y * along), jnp.zeros_like(ee)


_normalize_heads.defvjp(_normalize_heads_fwd, _normalize_heads_bwd)


_GELU_C = 2.0 * math.sqrt(2.0 / math.pi)


def _gelu_gate(x):
    return 1.0 / (1.0 + jnp.exp(-_GELU_C * x * (1.0 + 0.044715 * (x * x))))


@jax.custom_vjp
def _gelu(x):
    return x * _gelu_gate(x)


def _gelu_fwd(x):
    s = _gelu_gate(x)
    return x * s, (x, s)


def _gelu_bwd(res, g):
    x, s = res
    return (g * (s + x * s * (1.0 - s) * (_GELU_C * (1.0 + 3.0 * 0.044715 * (x * x)))),)


_gelu.defvjp(_gelu_fwd, _gelu_bwd)


def _rms(x, g):
    return x * lax.rsqrt(jnp.mean(x * x, axis=-1, keepdims=True) + NORM_EPS) * g


def _const_spec(shape):
    nd = len(shape)
    return pl.BlockSpec(shape, lambda *_: (0,) * nd, pipeline_mode=pl.Buffered(1))


def _acc_spec(shape):
    nd = len(shape)
    return pl.BlockSpec(shape, lambda *_: (0,) * nd)


def _params(sem):
    return pltpu.CompilerParams(dimension_semantics=(sem,), vmem_limit_bytes=VMEM_LIMIT)


_ANY = pl.BlockSpec(memory_space=pl.ANY)


def _sds(shape):
    return jax.ShapeDtypeStruct(shape, F32)


def _head_sum_matrix():
    i = jnp.arange(D_RWKV) // HEAD
    return (i[:, None] == i[None, :]).astype(BF16)


def _s5_param_fn(lam_re, lam_im, logdt, b_re, b_im):
    dt = jnp.exp(logdt)
    mag = jnp.exp(lam_re * dt)
    ang = lam_im * dt
    lbr = mag * jnp.cos(ang)
    lbi = mag * jnp.sin(ang)
    nr = lbr - 1.0
    den = lam_re * lam_re + lam_im * lam_im
    cr = (nr * lam_re + lbi * lam_im) / den
    ci = (lbi * lam_re - nr * lam_im) / den
    return lbr, lbi, cr * b_re - ci * b_im, cr * b_im + ci * b_re


def _cmul(ar, ai, br, bi):
    return ar * br - ai * bi, ar * bi + ai * br


def _s5_param_fwd(lam_re, lam_im, logdt, b_re, b_im):
    def body(lr, li, ld, br, bi, o_br, o_bi, o_pr, o_pi, o_qr, o_qi):
        lbr, lbi, bbr, bbi = _s5_param_fn(lr[...], li[...], ld[...], br[...], bi[...])
        o_br[...] = bbr
        o_bi[...] = bbi
        rid = lax.broadcasted_iota(jnp.int32, (8, N_STATE), 0)
        pr, pi_ = lbr, lbi
        fwd_r = rev_r = jnp.broadcast_to(pr, (8, N_STATE))
        fwd_i = rev_i = jnp.broadcast_to(pi_, (8, N_STATE))
        for j in range(1, 8):
            pr, pi_ = _cmul(pr, pi_, lbr, lbi)
            fwd_r = jnp.where(rid == j, jnp.broadcast_to(pr, (8, N_STATE)), fwd_r)
            fwd_i = jnp.where(rid == j, jnp.broadcast_to(pi_, (8, N_STATE)), fwd_i)
            rev_r = jnp.where(rid == 7 - j, jnp.broadcast_to(pr, (8, N_STATE)), rev_r)
            rev_i = jnp.where(rid == 7 - j, jnp.broadcast_to(pi_, (8, N_STATE)), rev_i)
        o_pr[...] = fwd_r
        o_pi[...] = fwd_i
        o_qr[...] = rev_r
        o_qi[...] = -rev_i

    return pl.pallas_call(
        body, name="s5_param_fwd",
        out_shape=[_sds((S5_GROUP, N_STATE))] * 2 + [_sds((8, N_STATE))] * 4,
    )(lam_re, lam_im, logdt, b_re, b_im)


def _s5_param_bwd(lam_re, lam_im, logdt, b_re, b_im, d_lbr, d_lbi, d_bbr, d_bbi, group_ind):
    def body(lr, li, ld, br, bi, g0, g1, g2, g3, ind, o_lr, o_li, o_ld, o_br, o_bi):
        _, vjp = jax.vjp(_s5_param_fn, lr[...], li[...], ld[...], br[...], bi[...])
        d_lr, d_li, d_ld, d_br, d_bi = vjp((g0[...], g1[...], g2[...], g3[...]))
        o_lr[...] = d_lr
        o_li[...] = d_li
        o_ld[...] = _dot(jnp.broadcast_to(d_ld, (8, N_STATE)), ind[...], ((1,), (0,)), HI)
        o_br[...] = d_br
        o_bi[...] = d_bi

    return pl.pallas_call(
        body, name="s5_param_bwd",
        out_shape=[_sds((1, N_STATE))] * 2 + [_sds((8, LANES))] + [_sds((S5_GROUP, N_STATE))] * 2,
    )(lam_re, lam_im, logdt, b_re, b_im, d_lbr, d_lbi, d_bbr, d_bbi, group_ind)


def _fwd_in(x, norm_g, w_in_bf, tt):
    L = x.shape[0]

    def body(x_ref, g_ref, w_ref, u_ref, zs_ref, rw_ref, zr_ref):
        h = _rms(x_ref[...], g_ref[...])
        proj = _dot(h.astype(BF16), w_ref[...], ((1,), (1,)), None)
        u_ref[...] = proj[:, 0:D_S5]
        zs_ref[...] = proj[:, D_S5:2 * D_S5]
        rw_ref[...] = proj[:, 2 * D_S5:2 * D_S5 + D_SHIFT]
        zr_ref[...] = proj[:, 2 * D_S5 + D_SHIFT:D_IN]

    row = lambda n: pl.BlockSpec((tt, n), lambda i: (i, 0))
    return pl.pallas_call(
        body, name="fwd_in", grid=(L // tt,),
        in_specs=[row(D_MODEL), _const_spec((1, D_MODEL)), _const_spec((D_IN, D_MODEL))],
        out_specs=[row(D_S5), row(D_S5), row(D_SHIFT), row(D_RWKV)],
        out_shape=[_sds((L, D_S5)), _sds((L, D_S5)), _sds((L, D_SHIFT)), _sds((L, D_RWKV))],
        compiler_params=_params("parallel"),
    )(x, norm_g, w_in_bf)


S5_LANE_CHUNK = 512


def _tile_scan(re_ref, im_ref, pow_r_ref, pow_i_ref, carry_r_ref, carry_i_ref, reverse):
    t, n = re_ref.shape
    n_groups = t // 8
    ch = S5_LANE_CHUNK
    rid = lax.broadcasted_iota(jnp.int32, (8, ch), 0)
    for c in range(n // ch):
        cols = slice(c * ch, (c + 1) * ch)
        pow_r = pow_r_ref[:, cols]
        pow_i = pow_i_ref[:, cols]
        row = lambda tile, j: jnp.broadcast_to(tile[j:j + 1], (8, ch))
        levels = []
        for d in (1, 2, 4):
            keep = (rid < 8 - d) if reverse else (rid >= d)
            j = 8 - d if reverse else d - 1
            levels.append(((8 - d) if reverse else d,
                           jnp.where(keep, row(pow_r, j), 0.0), jnp.where(keep, row(pow_i, j), 0.0)))

        def group(g, carry):
            r0 = pl.multiple_of(((n_groups - 1 - g) if reverse else g) * 8, 8)
            xr = re_ref[pl.ds(r0, 8), cols]
            xi = im_ref[pl.ds(r0, 8), cols]
            for shift, lr, li in levels:
                mr, mi = _cmul(lr, li, pltpu.roll(xr, shift, axis=0), pltpu.roll(xi, shift, axis=0))
                xr = xr + mr
                xi = xi + mi
            mr, mi = _cmul(pow_r, pow_i, carry[0], carry[1])
            xr = xr + mr
            xi = xi + mi
            re_ref[pl.ds(r0, 8), cols] = xr
            im_ref[pl.ds(r0, 8), cols] = xi
            last = 0 if reverse else 7
            return row(xr, last), row(xi, last)

        out = lax.fori_loop(0, n_groups, group, (carry_r_ref[:, cols], carry_i_ref[:, cols]))
        carry_r_ref[:, cols] = out[0]
        carry_i_ref[:, cols] = out[1]


def _s5_fwd(u, b4_re, b4_im, c4_re, c4_im, pow_r, pow_i, tt):
    L = u.shape[0]

    def body(u_ref, bre_ref, bim_ref, cre_ref, cim_ref, pr_ref, pi_ref, sre_o, sim_o, y_o, car_r, car_i):
        @pl.when(pl.program_id(0) == 0)
        def _():
            car_r[...] = jnp.zeros_like(car_r)
            car_i[...] = jnp.zeros_like(car_i)

        uv = u_ref[...]
        for q in range(S5_BLOCKS):
            uq = uv[:, q * LANES:(q + 1) * LANES]
            cols = slice(q * 512, (q + 1) * 512)
            sre_o[:, cols] = _dot_bf(uq, bre_ref[q], ((1,), (0,)))
            sim_o[:, cols] = _dot_bf(uq, bim_ref[q], ((1,), (0,)))
        _tile_scan(sre_o, sim_o, pr_ref, pi_ref, car_r, car_i, reverse=False)
        for q in range(S5_BLOCKS):
            cols = slice(q * 512, (q + 1) * 512)
            y_o[:, q * LANES:(q + 1) * LANES] = (_dot_bf(sre_o[:, cols], cre_ref[q], ((1,), (0,)))
                                                 - _dot_bf(sim_o[:, cols], cim_ref[q], ((1,), (0,))))

    row = lambda n: pl.BlockSpec((tt, n), lambda i: (i, 0))
    return pl.pallas_call(
        body, name="s5_fwd", grid=(L // tt,),
        in_specs=[row(D_S5)] + [_const_spec((S5_BLOCKS, LANES, 512))] * 2 + [_const_spec((S5_BLOCKS, 512, LANES))] * 2
        + [_const_spec((8, N_STATE))] * 2,
        out_specs=[row(N_STATE), row(N_STATE), row(D_S5)],
        out_shape=[_sds((L, N_STATE)), _sds((L, N_STATE)), _sds((L, D_S5))],
        scratch_shapes=[pltpu.VMEM((8, N_STATE), F32)] * 2,
        compiler_params=_params("arbitrary"),
    )(u, b4_re, b4_im, c4_re, c4_im, pow_r, pow_i)


def _rwkv_pre_fn(r, k, v, wa, w0, w2p, a0, a2p, k_k, k_a, ee):
    w = -_softplus(-(w0 + mm_bf(jnp.tanh(wa), w2p))) - 0.5
    logw = -jnp.exp(w)
    a = _sigmoid(a0 + mm_bf(wa, a2p))
    kk = _normalize_heads(k * k_k, ee)
    k2 = k * (1.0 + (a - 1.0) * k_a)
    return r, logw, k2, v, -kk, kk * a


N_PAIRS = N_HEADS // 2


def _head_spec(tt):
    return pl.BlockSpec((N_PAIRS, tt, LANES), lambda i: (0, i, 0))


def _load_heads(ref):
    return jnp.concatenate([ref[p] for p in range(N_PAIRS)], axis=-1)


def _store_heads(ref, val):
    for p in range(N_PAIRS):
        ref[p] = val[:, p * LANES:(p + 1) * LANES]


def _split_pairs(x):
    return jnp.concatenate([x[:, :, :HEAD], x[:, :, HEAD:]], axis=0)


def _join_pairs(x):
    return jnp.concatenate([x[:N_PAIRS], x[N_PAIRS:]], axis=-1)


def _shifted(rw, prev_blk, first):
    rolled = pltpu.roll(rw, 1, axis=0)
    prev_row = jnp.where(first, 0.0, prev_blk[7:8, :])
    rid = lax.broadcasted_iota(jnp.int32, rw.shape, 0)
    return jnp.where(rid == 0, jnp.broadcast_to(prev_row, rw.shape), rolled)


def _split_rw(t):
    return t[:, 0:512], t[:, 512:1024], t[:, 1024:1536], t[:, 1536:1664]


def _rwkv_pre_specs(tt):
    row = pl.BlockSpec((tt, D_SHIFT), lambda i: (i, 0))
    prev = pl.BlockSpec((8, D_SHIFT), lambda i: (jnp.maximum(i * (tt // 8) - 1, 0), 0))
    consts = [_const_spec((1, D_SHIFT)), _const_spec((1, D_RWKV)), _const_spec((LANES, D_RWKV)),
              _const_spec((1, D_RWKV)), _const_spec((LANES, D_RWKV)), _const_spec((1, D_RWKV)),
              _const_spec((1, D_RWKV)), _const_spec((D_RWKV, D_RWKV))]
    return [row, prev] + consts


def _rwkv_pre_fwd(rw, mu, w0, w2p, a0, a2p, k_k, k_a, ee, tt):
    L = rw.shape[0]

    def body(rw_ref, prev_ref, mu_ref, w0_ref, w2_ref, a0_ref, a2_ref, kk_ref, ka_ref, ee_ref, *outs):
        rwv = rw_ref[...]
        rws = rwv + (_shifted(rwv, prev_ref[...], pl.program_id(0) == 0) - rwv) * mu_ref[...]
        res = _rwkv_pre_fn(*_split_rw(rws), w0_ref[...], w2_ref[...], a0_ref[...], a2_ref[...],
                           kk_ref[...], ka_ref[...], ee_ref[...])
        for o, val in zip(outs, res):
            _store_heads(o, val)

    return pl.pallas_call(
        body, name="rwkv_pre_fwd", grid=(L // tt,),
        in_specs=_rwkv_pre_specs(tt), out_specs=[_head_spec(tt)] * 6, out_shape=[_sds((N_PAIRS, L, LANES))] * 6,
        compiler_params=_params("parallel"),
    )(rw, rw, mu, w0, w2p, a0, a2p, k_k, k_a, ee)


def _rwkv_pre_bwd(rw, mu, w0, w2p, a0, a2p, k_k, k_a, ee, cots, tt):
    L = rw.shape[0]
    n_t = L // tt

    def body(rw_ref, prev_ref, mu_ref, w0_ref, w2_ref, a0_ref, a2_ref, kk_ref, ka_ref, ee_ref,
             c_r, c_w, c_k, c_v, c_a, c_b, cb_r, cb_k, cb_v,
             drws_ref, dmu_o, dw0_o, dw2_o, da0_o, da2_o, dkk_o, dka_o,
             dmu, dw0, dw2, da0, da2, dkk, dka):
        i = pl.program_id(0)
        accs = (dmu, dw0, dw2, da0, da2, dkk, dka)

        @pl.when(i == 0)
        def _():
            for acc in accs:
                acc[...] = jnp.zeros_like(acc)

        rwv = rw_ref[...]
        diff = _shifted(rwv, prev_ref[...], i == 0) - rwv
        rws = rwv + diff * mu_ref[...]
        consts = (w0_ref[...], w2_ref[...], a0_ref[...], a2_ref[...], kk_ref[...], ka_ref[...])
        _, vjp = jax.vjp(lambda *a: _rwkv_pre_fn(*a, ee_ref[...]), *_split_rw(rws), *consts)
        scan = [_load_heads(c) for c in (c_r, c_w, c_k, c_v, c_a, c_b)]
        g = vjp((scan[0] + cb_r[...], scan[1], scan[2] + cb_k[...], scan[3] + cb_v[...], scan[4], scan[5]))
        drws = jnp.concatenate(g[0:4], axis=-1)
        drws_ref[...] = drws
        dmu[...] += jnp.sum(drws * diff, axis=0, keepdims=True)
        for acc, val in zip(accs[1:], g[4:]):
            acc[...] += val

        @pl.when(i == n_t - 1)
        def _():
            for acc, out in zip(accs, (dmu_o, dw0_o, dw2_o, da0_o, da2_o, dkk_o, dka_o)):
                out[...] = acc[...]

    row = pl.BlockSpec((tt, D_RWKV), lambda i: (i, 0))
    shapes = [(1, D_SHIFT), (1, D_RWKV), (LANES, D_RWKV), (1, D_RWKV), (LANES, D_RWKV), (1, D_RWKV), (1, D_RWKV)]
    return pl.pallas_call(
        body, name="rwkv_pre_bwd", grid=(n_t,),
        in_specs=_rwkv_pre_specs(tt) + [_head_spec(tt)] * 6 + [row] * 3,
        out_specs=[pl.BlockSpec((tt, D_SHIFT), lambda i: (i, 0))] + [_acc_spec(s) for s in shapes],
        out_shape=[_sds((L, D_SHIFT))] + [_sds(s) for s in shapes],
        scratch_shapes=[pltpu.VMEM(s, F32) for s in shapes],
        compiler_params=_params("arbitrary"),
    )(rw, rw, mu, w0, w2p, a0, a2p, k_k, k_a, ee, *cots)


def _bmm(a, b):
    return lax.dot_general(a, b, (((2,), (1,)), ((0,), (0,))), precision=HI, preferred_element_type=F32)


def _bmm_nt(a, b):
    return lax.dot_general(a, b, (((2,), (2,)), ((0,), (0,))), precision=HI, preferred_element_type=F32)


def _bmm_tn(a, b):
    return lax.dot_general(a, b, (((1,), (1,)), ((0,), (0,))), precision=HI, preferred_element_type=F32)


def _bdot_bf(a, b, lhs_dim, rhs_dim):
    return lax.dot_general(a.astype(BF16), b.astype(BF16), (((lhs_dim,), (rhs_dim,)), ((0,), (0,))),
                           preferred_element_type=F32)


@jax.custom_vjp
def _bmm_bf(a, b):
    return _bdot_bf(a, b, 2, 1)


def _bmm_bf_fwd(a, b):
    return _bmm_bf(a, b), (a, b)


def _bmm_bf_bwd(res, g):
    a, b = res
    return _bdot_bf(g, b, 2, 2), _bdot_bf(a, g, 1, 1)


_bmm_bf.defvjp(_bmm_bf_fwd, _bmm_bf_bwd)


@jax.custom_vjp
def _bmm_tn_bf(a, b):
    return _bdot_bf(a, b, 1, 1)


def _bmm_tn_bf_fwd(a, b):
    return _bmm_tn_bf(a, b), (a, b)


def _bmm_tn_bf_bwd(res, g):
    a, b = res
    return _bdot_bf(b, g, 2, 2), _bdot_bf(a, g, 2, 1)


_bmm_tn_bf.defvjp(_bmm_tn_bf_fwd, _bmm_tn_bf_bwd)


def _unit_lower_inverse(a):
    t = a.shape[-1]
    ti = lax.broadcasted_iota(jnp.int32, (t, t), 0)
    si = lax.broadcasted_iota(jnp.int32, (t, t), 1)

    def same_block(bits):
        shift = jnp.int32(bits)
        return (lax.shift_right_logical(ti, shift) == lax.shift_right_logical(si, shift))[None]

    def mm(x, y):
        return _bdot_bf(x, y, 2, 1)

    d = jnp.where(same_block(3), a, 0.0)
    inv = jnp.where(ti == si, 1.0, 0.0)[None] + d
    pw = mm(d, d)
    both = mm(jnp.concatenate([inv, pw], axis=1), pw)
    inv = inv + both[:, :t]
    inv = inv + mm(inv, both[:, t:])
    bits = 3
    while (1 << bits) < t:
        e = jnp.where(same_block(bits), 0.0, jnp.where(same_block(bits + 1), a, 0.0))
        inv = inv + mm(mm(inv, e), inv)
        bits += 1
    return inv


def _tri_mask(t):
    ri = lax.broadcasted_iota(jnp.int32, (2 * t, 2 * t), 0)
    ci = lax.broadcasted_iota(jnp.int32, (2 * t, 2 * t), 1)
    top_rows = ri < t
    diff = jnp.where(top_rows, ri, ri - t) - jnp.where(ci < t, ci, ci - t)
    return (diff >= jnp.where(top_rows, 1, 0))[None]


def _ones_tri(n_h, t):
    ti = lax.broadcasted_iota(jnp.int32, (t, t), 0)
    si = lax.broadcasted_iota(jnp.int32, (t, t), 1)
    return jnp.broadcast_to(jnp.where(ti >= si, 1.0, 0.0)[None], (n_h, t, t))


@jax.custom_vjp
def _running_sum_kept(logw, kept):
    return kept


def _running_sum_kept_bwd(shape, g):
    return _bmm_tn(_ones_tri(shape[0], shape[1]), g), jnp.zeros_like(g)


_running_sum_kept.defvjp(lambda logw, kept: (kept, logw.shape), _running_sum_kept_bwd)


@jax.custom_vjp
def _tri_products_kept(ar, bk, kept):
    return kept


def _tri_products_kept_bwd(res, g):
    ar, bk = res
    g = jnp.where(_tri_mask(ar.shape[1] // 2), g, 0.0)
    return _bmm(g, bk), _bmm_tn(g, ar), jnp.zeros_like(g)


_tri_products_kept.defvjp(lambda ar, bk, kept: (kept, (ar, bk)), _tri_products_kept_bwd)


@jax.custom_vjp
def _solve_unit_lower(a, rhs, inv, kept=None):
    return _bmm(inv, rhs) if kept is None else kept


def _solve_fwd(a, rhs, inv, kept=None):
    u = _bmm(inv, rhs) if kept is None else kept
    return u, (inv, u, kept is not None)


def _solve_bwd(res, du):
    inv, u, had_kept = res
    d_rhs = _bmm_tn(inv, du)
    return _bmm_nt(d_rhs, u), d_rhs, jnp.zeros_like(inv), (jnp.zeros_like(u) if had_kept else None)


_solve_unit_lower.defvjp(_solve_fwd, _solve_bwd)


def _rwkv_chunk(st0, r, logw, k, v, a, b, kept=None):
    n_h, t, _ = r.shape
    log_p = _bmm(_ones_tri(n_h, t), logw) if kept is None else _running_sum_kept(logw, kept[0])
    p_in = jnp.exp(log_p)
    p_inv = jnp.exp(-log_p)
    at = a * jnp.exp(log_p - logw)
    rt = r * p_in
    ar = jnp.concatenate([at, rt], axis=1)
    bk = jnp.concatenate([b * p_inv, k * p_inv], axis=1)
    if kept is None:
        m = jnp.where(_tri_mask(t), _bmm_nt(ar, bk), 0.0)
        inv = _unit_lower_inverse(m[:, :t, :t])
    else:
        m = _tri_products_kept(ar, bk, kept[1])
        inv = kept[2]
    top, bottom = m[:, :t], m[:, t:]
    rhs = _bmm_bf(jnp.concatenate([at, top[:, :, t:]], axis=2), jnp.concatenate([st0, v], axis=1))
    u = _solve_unit_lower(top[:, :, :t], rhs, inv, None if kept is None else kept[3])
    y = _bmm_bf(jnp.concatenate([rt, bottom], axis=2), jnp.concatenate([st0, u, v], axis=1))
    p_end = jnp.swapaxes(p_in[:, t - 1:t, :], 1, 2)
    st1 = (st0 + _bmm_tn_bf(bk, jnp.concatenate([u, v], axis=1))) * p_end
    return y, st1, (log_p, m, inv, u)


def _rwkv_scan_fwd(ops):
    n_h, L, n = N_HEADS, ops[0].shape[1], HEAD
    t = RWKV_CHUNK
    per = min(RWKV_CHUNKS_PER_STEP, L // t)
    n_c = L // t
    n_s = n_c // per

    def body(r_ref, w_ref, k_ref, v_ref, a_ref, b_ref, y_ref, st_ref, logp_ref, m_ref, inv_ref, u_ref, st):
        @pl.when(pl.program_id(0) == 0)
        def _():
            st[...] = jnp.zeros_like(st)

        st0 = st[...]
        for j in range(per):
            rows = slice(j * t, (j + 1) * t)
            st_ref[j] = st0
            y, st0, (log_p, m, inv, u) = _rwkv_chunk(
                st0, *(_split_pairs(ref[:, rows, :]) for ref in (r_ref, w_ref, k_ref, v_ref, a_ref, b_ref)))
            y_ref[:, rows, :] = _join_pairs(y)
            logp_ref[:, rows, :] = log_p
            u_ref[:, rows, :] = u
            m_ref[j] = m
            inv_ref[j] = inv
        st[...] = st0

    pairs = pl.BlockSpec((N_PAIRS, per * t, LANES), lambda c: (0, c, 0))
    blk = pl.BlockSpec((n_h, per * t, n), lambda c: (0, c, 0))
    per_chunk = lambda m: pl.BlockSpec((per, n_h, m, m), lambda c: (c, 0, 0, 0))
    return pl.pallas_call(
        body, name="rwkv_scan_fwd", grid=(n_s,), in_specs=[pairs] * 6,
        out_specs=[pairs, per_chunk(n), blk, per_chunk(2 * t), per_chunk(t), blk],
        out_shape=[_sds((N_PAIRS, L, LANES)), _sds((n_c, n_h, n, n)), _sds((n_h, L, n)),
                   _sds((n_c, n_h, 2 * t, 2 * t)), _sds((n_c, n_h, t, t)), _sds((n_h, L, n))],
        scratch_shapes=[pltpu.VMEM((n_h, n, n), F32)],
        compiler_params=_params("arbitrary"),
    )(*ops)


def _rwkv_scan_bwd(ops, states, kept, dy):
    n_h, L, n = N_HEADS, ops[0].shape[1], HEAD
    t = RWKV_CHUNK
    per = min(RWKV_CHUNKS_PER_STEP, L // t)
    n_s = L // t // per

    def body(r_ref, w_ref, k_ref, v_ref, a_ref, b_ref, st_ref, logp_ref, m_ref, inv_ref, u_ref, dy_ref,
             dr, dw, dk, dv, da, db, dst):
        @pl.when(pl.program_id(0) == 0)
        def _():
            dst[...] = jnp.zeros_like(dst)

        vjps = []
        for j in range(per):
            rows = slice(j * t, (j + 1) * t)
            have = (logp_ref[:, rows, :], m_ref[j], inv_ref[j], u_ref[:, rows, :])
            args = [_split_pairs(ref[:, rows, :]) for ref in (r_ref, w_ref, k_ref, v_ref, a_ref, b_ref)]
            vjps.append(jax.vjp(lambda *a, have=have: _rwkv_chunk(*a, kept=have)[:2], st_ref[j], *args)[1])
        d_state = dst[...]
        for j in reversed(range(per)):
            rows = slice(j * t, (j + 1) * t)
            g = vjps[j]((_split_pairs(dy_ref[:, rows, :]), d_state))
            d_state = g[0]
            for out, val in zip((dr, dw, dk, dv, da, db), g[1:]):
                out[:, rows, :] = _join_pairs(val)
        dst[...] = d_state

    pairs = pl.BlockSpec((N_PAIRS, per * t, LANES), lambda c: (0, n_s - 1 - c, 0))
    blk = pl.BlockSpec((n_h, per * t, n), lambda c: (0, n_s - 1 - c, 0))
    per_chunk = lambda m: pl.BlockSpec((per, n_h, m, m), lambda c: (n_s - 1 - c, 0, 0, 0))
    return pl.pallas_call(
        body, name="rwkv_scan_bwd", grid=(n_s,),
        in_specs=[pairs] * 6 + [per_chunk(n), blk, per_chunk(2 * t), per_chunk(t), blk, pairs],
        out_specs=[pairs] * 6, out_shape=[_sds((N_PAIRS, L, LANES))] * 6,
        scratch_shapes=[pltpu.VMEM((n_h, n, n), F32)],
        compiler_params=_params("arbitrary"),
    )(*ops, states, *kept, dy)


def _post_fn(x, u, zs, zr, ysc, r, k2, v, y_ssm, d, glu_w, glu_b, ln_w, ln_b, r_k,
             wo_s5, wo_rwkv, gf, tgt, ee):
    y3 = _gelu(y_ssm + d * u)
    y_s5 = y3 * _sigmoid(mm_bf(y3, glu_w) + glu_b) * _silu(zs)
    mean = head_sum_split(ysc, ee) * (1.0 / HEAD)
    yc = ysc - mean
    var = head_sum(yc * yc, ee) * (1.0 / HEAD)
    gn = yc * lax.rsqrt(var + GN_EPS) * ln_w + ln_b
    bonus = head_sum(r * k2 * r_k, ee) * v
    y_rwkv = (gn + bonus) * _silu(zr)
    x2 = x + mm_bf(y_s5, wo_s5) + mm_bf(y_rwkv, wo_rwkv)
    err = _rms(x2, gf) - tgt
    return 0.5 * jnp.mean(err * err, axis=-1, keepdims=True)


def _post(x, u, zs, zr, ysc, r, k2, v, y_ssm, d, glu_w, glu_b, ln_w, ln_b, r_k, w_out, gf, tgt, ee, tt):
    L = x.shape[0]
    n_t = L // tt
    acc_shapes = [(1, D_S5), (D_S5, D_S5), (1, D_S5), (1, D_RWKV), (1, D_RWKV), (1, D_RWKV),
                  (D_MODEL, D_MODEL), (1, D_MODEL), (8, LANES)]

    def body(x_ref, u_ref, zs_ref, zr_ref, ysc_ref, r_ref, k2_ref, v_ref, yssm_ref,
             d_ref, gw_ref, gb_ref, lw_ref, lb_ref, rk_ref, wo_ref, gf_ref, tgt_ref, ee_ref,
             dx_o, du_o, dzs_o, dzr_o, dysc_o, dr_o, dk2_o, dv_o, dyssm_o,
             dd_o, dgw_o, dgb_o, dlw_o, dlb_o, drk_o, dwo_o, dgf_o, loss_o,
             dd, dgw, dgb, dlw, dlb, drk, dwo, dgf, loss):
        i = pl.program_id(0)
        accs = (dd, dgw, dgb, dlw, dlb, drk, dwo, dgf, loss)

        @pl.when(i == 0)
        def _():
            for acc in accs:
                acc[...] = jnp.zeros_like(acc)

        args = (x_ref[...], u_ref[...], zs_ref[...], zr_ref[...],
                _load_heads(ysc_ref), _load_heads(r_ref), _load_heads(k2_ref), _load_heads(v_ref), yssm_ref[...],
                d_ref[...], gw_ref[...], gb_ref[...], lw_ref[...], lb_ref[...], rk_ref[...],
                wo_ref[0:D_S5, :], wo_ref[D_S5:D_MODEL, :], gf_ref[...])
        rows, vjp = jax.vjp(lambda *a: _post_fn(*a, tgt_ref[...], ee_ref[...]), *args)
        g = vjp(jnp.ones_like(rows))
        for out, val in zip((dx_o, du_o, dzs_o, dzr_o), g[0:4]):
            out[...] = val
        _store_heads(dysc_o, g[4])
        for out, val in zip((dr_o, dk2_o, dv_o, dyssm_o), g[5:9]):
            out[...] = val
        for acc, val in zip((dd, dgw, dgb, dlw, dlb, drk), g[9:15]):
            acc[...] += val
        dwo[0:D_S5, :] += g[15]
        dwo[D_S5:D_MODEL, :] += g[16]
        dgf[...] += g[17]
        loss[...] += jnp.broadcast_to(jnp.sum(rows, axis=0, keepdims=True), loss.shape)

        @pl.when(i == n_t - 1)
        def _():
            for acc, out in zip(accs, (dd_o, dgw_o, dgb_o, dlw_o, dlb_o, drk_o, dwo_o, dgf_o, loss_o)):
                pltpu.sync_copy(acc, out)

    row = lambda n: pl.BlockSpec((tt, n), lambda i: (i, 0))
    in_specs = ([row(D_MODEL)] + [row(512)] * 3 + [_head_spec(tt)] * 4 + [row(D_S5)]
                + [_const_spec(s) for s in [(1, D_S5), (D_S5, D_S5), (1, D_S5), (1, D_RWKV), (1, D_RWKV), (1, D_RWKV),
                                            (D_MODEL, D_MODEL), (1, D_MODEL)]]
                + [row(D_MODEL), _const_spec((D_RWKV, D_RWKV))])
    out_rows = [D_MODEL] + [512] * 3 + [None] + [512] * 4
    return pl.pallas_call(
        body, name="post_fwd_bwd", grid=(n_t,), in_specs=in_specs,
        out_specs=[row(n) if n else _head_spec(tt) for n in out_rows] + [_ANY] * len(acc_shapes),
        out_shape=([_sds((L, n)) if n else _sds((N_PAIRS, L, LANES)) for n in out_rows]
                   + [_sds(s) for s in acc_shapes]),
        scratch_shapes=[pltpu.VMEM(s, F32) for s in acc_shapes],
        compiler_params=_params("arbitrary"),
    )(x, u, zs, zr, ysc, r, k2, v, y_ssm, d, glu_w, glu_b, ln_w, ln_b, r_k, w_out, gf, tgt, ee)


def _s5_bwd(u, du_direct, dy, s_re, s_im, b4_re, b4_im, c4_re, c4_im, pow_r, pow_i, tt):
    L = u.shape[0]
    n_t = L // tt
    acc_shapes = ([(S5_BLOCKS, LANES, 512)] * 2 + [(S5_BLOCKS, 512, LANES)] * 2 + [(1, N_STATE)] * 2)

    def body(u_ref, dud_ref, dy_ref, sre_ref, sim_ref, pre_ref, pim_ref, bre_ref, bim_ref, cre_ref, cim_ref,
             pr_ref, pi_ref, du_o, dbre_o, dbim_o, dcre_o, dcim_o, dlr_o, dli_o,
             dbre, dbim, dcre, dcim, dlr, dli, gre, gim, car_r, car_i):
        i = pl.program_id(0)

        @pl.when(i == 0)
        def _():
            for acc in (dbre, dbim, dcre, dcim, dlr, dli, car_r, car_i):
                acc[...] = jnp.zeros_like(acc)

        uv = u_ref[...]
        dyv = dy_ref[...]
        blocks = [slice(q * 512, (q + 1) * 512) for q in range(S5_BLOCKS)]
        lanes = [slice(q * LANES, (q + 1) * LANES) for q in range(S5_BLOCKS)]
        for q in range(S5_BLOCKS):
            gre[:, blocks[q]] = _dot_bf(dyv[:, lanes[q]], cre_ref[q], ((1,), (1,)))
            gim[:, blocks[q]] = -_dot_bf(dyv[:, lanes[q]], cim_ref[q], ((1,), (1,)))
        _tile_scan(gre, gim, pr_ref, pi_ref, car_r, car_i, reverse=True)
        for q in range(S5_BLOCKS):
            gr = gre[:, blocks[q]]
            gi = gim[:, blocks[q]]
            sr = sre_ref[:, blocks[q]]
            si = sim_ref[:, blocks[q]]
            du_o[:, lanes[q]] = (dud_ref[:, lanes[q]] + _dot_bf(gr, bre_ref[q], ((1,), (1,)))
                                 + _dot_bf(gi, bim_ref[q], ((1,), (1,))))
            dbre[q] += _dot_bf(uv[:, lanes[q]], gr, ((0,), (0,)))
            dbim[q] += _dot_bf(uv[:, lanes[q]], gi, ((0,), (0,)))
            dcre[q] += _dot_bf(sr, dyv[:, lanes[q]], ((0,), (0,)))
            dcim[q] -= _dot_bf(si, dyv[:, lanes[q]], ((0,), (0,)))
            rid = lax.broadcasted_iota(jnp.int32, sr.shape, 0)
            first = i == n_t - 1
            prev_r = jnp.where(first, 0.0, pre_ref[7:8, blocks[q]])
            prev_i = jnp.where(first, 0.0, pim_ref[7:8, blocks[q]])
            pr = jnp.where(rid == 0, jnp.broadcast_to(prev_r, sr.shape), pltpu.roll(sr, 1, axis=0))
            pi_ = jnp.where(rid == 0, jnp.broadcast_to(prev_i, si.shape), pltpu.roll(si, 1, axis=0))
            dlr[:, blocks[q]] += jnp.sum(pr * gr + pi_ * gi, axis=0, keepdims=True)
            dli[:, blocks[q]] += jnp.sum(pr * gi - pi_ * gr, axis=0, keepdims=True)

        @pl.when(i == n_t - 1)
        def _():
            for acc, out in zip((dbre, dbim, dcre, dcim, dlr, dli), (dbre_o, dbim_o, dcre_o, dcim_o, dlr_o, dli_o)):
                out[...] = acc[...]

    row = lambda n: pl.BlockSpec((tt, n), lambda i: (n_t - 1 - i, 0))
    prev = pl.BlockSpec((8, N_STATE), lambda i: (jnp.maximum((n_t - 1 - i) * (tt // 8) - 1, 0), 0))
    return pl.pallas_call(
        body, name="s5_bwd", grid=(n_t,),
        in_specs=[row(D_S5)] * 3 + [row(N_STATE)] * 2 + [prev] * 2
        + [_const_spec((S5_BLOCKS, LANES, 512))] * 2 + [_const_spec((S5_BLOCKS, 512, LANES))] * 2
        + [_const_spec((8, N_STATE))] * 2,
        out_specs=[row(D_S5)] + [_acc_spec(s) for s in acc_shapes],
        out_shape=[_sds((L, D_S5))] + [_sds(s) for s in acc_shapes],
        scratch_shapes=[pltpu.VMEM(s, F32) for s in acc_shapes] + [pltpu.VMEM((tt, N_STATE), F32)] * 2
        + [pltpu.VMEM((8, N_STATE), F32)] * 2,
        compiler_params=_params("arbitrary"),
    )(u, du_direct, dy, s_re, s_im, s_re, s_im, b4_re, b4_im, c4_re, c4_im, pow_r, pow_i)


def _bwd_in(x, norm_g, w_in_bf, mu, dx2, du, dzs, drws, dzr, tt):
    L = x.shape[0]
    n_t = L // tt

    def body(x_ref, g_ref, w_ref, mu_ref, dx2_ref, du_ref, dzs_ref, drws_ref, nxt_ref, dzr_ref,
             gx_o, dw_o, dg_o, dproj, dw, dg):
        i = pl.program_id(0)

        @pl.when(i == 0)
        def _():
            dw[...] = jnp.zeros_like(dw)
            dg[...] = jnp.zeros_like(dg)

        drws_v = drws_ref[...]
        rid = lax.broadcasted_iota(jnp.int32, drws_v.shape, 0)
        nxt_row = jnp.where(i == n_t - 1, 0.0, nxt_ref[0:1, :])
        nxt = jnp.where(rid == tt - 1, jnp.broadcast_to(nxt_row, drws_v.shape), pltpu.roll(drws_v, tt - 1, axis=0))
        muv = mu_ref[...]
        drw = drws_v * (1.0 - muv) + nxt * muv
        dproj[:, 0:D_S5] = du_ref[...].astype(BF16)
        dproj[:, D_S5:2 * D_S5] = dzs_ref[...].astype(BF16)
        dproj[:, 2 * D_S5:2 * D_S5 + D_SHIFT] = drw.astype(BF16)
        dproj[:, 2 * D_S5 + D_SHIFT:D_IN] = dzr_ref[...].astype(BF16)
        dh = _dot(dproj[...], w_ref[...], ((1,), (0,)), None)
        h, vjp = jax.vjp(_rms, x_ref[...], g_ref[...])
        dxh, dgv = vjp(dh)
        gx_o[...] = dx2_ref[...] + dxh
        dg[...] += dgv
        dw[...] += _dot(h.astype(BF16), dproj[...], ((0,), (0,)), None)

        @pl.when(i == n_t - 1)
        def _():
            dg_o[...] = dg[...]
            pltpu.sync_copy(dw, dw_o)

    row = lambda n: pl.BlockSpec((tt, n), lambda i: (i, 0))
    nxt = pl.BlockSpec((8, D_SHIFT), lambda i: (jnp.minimum((i + 1) * (tt // 8), L // 8 - 1), 0))
    return pl.pallas_call(
        body, name="bwd_in", grid=(n_t,),
        in_specs=[row(D_MODEL), _const_spec((1, D_MODEL)), _const_spec((D_IN, D_MODEL)), _const_spec((1, D_SHIFT)),
                  row(D_MODEL), row(D_S5), row(D_S5), row(D_SHIFT), nxt, row(D_RWKV)],
        out_specs=[row(D_MODEL), _ANY, _acc_spec((1, D_MODEL))],
        out_shape=[_sds((L, D_MODEL)), _sds((D_MODEL, D_IN)), _sds((1, D_MODEL))],
        scratch_shapes=[pltpu.VMEM((tt, D_IN), BF16), pltpu.VMEM((D_MODEL, D_IN), F32), pltpu.VMEM((1, D_MODEL), F32)],
        compiler_params=_params("arbitrary"),
    )(x, norm_g, w_in_bf, mu, dx2, du, dzs, drws, drws, dzr)


def _block_diag_b(bbar):
    bb = bbar.reshape(S5_GROUP, S5_BLOCKS, 8, S5_STATE)
    return jnp.einsum('hqgp,Gg->qGhgp', bb, jnp.eye(8, dtype=F32)).reshape(S5_BLOCKS, LANES, 512)


def _block_diag_b_t(db4):
    d = db4.reshape(S5_BLOCKS, 8, S5_GROUP, 8, S5_STATE)
    return jnp.einsum('qGhgp,Gg->hqgp', d, jnp.eye(8, dtype=F32)).reshape(S5_GROUP, N_STATE)


def _block_diag_c(c):
    cc = c.reshape(S5_BLOCKS, 8, S5_GROUP, S5_STATE)
    return jnp.einsum('qghp,gG->qgpGh', cc, jnp.eye(8, dtype=F32)).reshape(S5_BLOCKS, 512, LANES)


def _block_diag_c_t(dc4):
    d = dc4.reshape(S5_BLOCKS, 8, S5_STATE, 8, S5_GROUP)
    return jnp.einsum('qgpGh,gG->qghp', d, jnp.eye(8, dtype=F32)).reshape(S5_GROUPS, S5_GROUP, S5_STATE)


def _local_step(x, tgt, w):
    L = x.shape[0]
    tt = min(512, L)
    tp = min(256, L)
    ee = _head_sum_matrix()

    lam_re = w['s5_lam_re'].reshape(1, N_STATE)
    lam_im = w['s5_lam_im'].reshape(1, N_STATE)
    logdt = jnp.repeat(w['s5_log_dt'], S5_STATE).reshape(1, N_STATE)
    b_re_t = w['s5_b_re'].transpose(2, 0, 1).reshape(S5_GROUP, N_STATE)
    b_im_t = w['s5_b_im'].transpose(2, 0, 1).reshape(S5_GROUP, N_STATE)
    bbr, bbi, pow_r, pow_i, rpow_r, rpow_i = _s5_param_fwd(lam_re, lam_im, logdt, b_re_t, b_im_t)
    b4_re, b4_im = _block_diag_b(bbr), _block_diag_b(bbi)
    c4_re, c4_im = _block_diag_c(w['s5_c_re']), _block_diag_c(w['s5_c_im'])

    norm_g = w['norm_g'].reshape(1, D_MODEL)
    w_in_bf = (w['w_in_t'] if 'w_in_t' in w else w['w_in'].T).astype(BF16)
    u, zs, rw, zr = _fwd_in(x, norm_g, w_in_bf, tt)
    s_re, s_im, y_ssm = _s5_fwd(u, b4_re, b4_im, c4_re, c4_im, pow_r, pow_i, tt)

    row = lambda t: t.reshape(1, -1)
    zpad = jnp.zeros((HEAD, D_RWKV), F32)
    w2p = jnp.concatenate([w['rwkv_w2'], zpad], axis=0)
    a2p = jnp.concatenate([zpad, w['rwkv_a2']], axis=0)
    pre_consts = (row(w['rwkv_mu']), row(w['rwkv_w0']), w2p, row(w['rwkv_a0']), a2p,
                  row(w['rwkv_k_k']), row(w['rwkv_k_a']), ee)
    ops = _rwkv_pre_fwd(rw, *pre_consts, tt)
    ysc, states, *kept = _rwkv_scan_fwd(ops)

    post = _post(x, u, zs, zr, ysc, ops[0], ops[2], ops[3], y_ssm,
                 row(w['s5_d']), w['s5_glu_w'], row(w['s5_glu_b']), row(w['rwkv_ln_w']), row(w['rwkv_ln_b']),
                 row(w['rwkv_r_k']), w['w_out'], row(w['final_g']), tgt, ee, tp)
    (dx2, du_d, dzs, dzr, dysc, dr_b, dk2_b, dv_b, dy_ssm,
     dd, dglu_w, dglu_b, dln_w, dln_b, dr_k, dw_out, dgf, loss) = post

    du, db4_re, db4_im, dc4_re, dc4_im, dlbr, dlbi = _s5_bwd(
        u, du_d, dy_ssm, s_re, s_im, b4_re, b4_im, c4_re, c4_im, rpow_r, rpow_i, tt)
    group_ind = (jnp.arange(N_STATE)[:, None] // S5_STATE == jnp.arange(LANES)[None, :]).astype(F32)
    dlam_re, dlam_im, dlogdt, db_re_t, db_im_t = _s5_param_bwd(
        lam_re, lam_im, logdt, b_re_t, b_im_t, dlbr, dlbi, _block_diag_b_t(db4_re), _block_diag_b_t(db4_im), group_ind)

    cots = list(_rwkv_scan_bwd(ops, states, kept, dysc)) + [dr_b, dk2_b, dv_b]
    drws, dmu, dw0, dw2p, da0, da2p, dk_k, dk_a = _rwkv_pre_bwd(rw, *pre_consts, cots, tt)

    grad_x, dw_in, dnorm_g = _bwd_in(x, norm_g, w_in_bf, row(w['rwkv_mu']), dx2, du, dzs, drws, dzr, tt)

    unb = lambda t: t.reshape(S5_GROUP, S5_GROUPS, S5_STATE).transpose(1, 2, 0)
    grads = {
        'norm_g': dnorm_g.reshape(D_MODEL), 'w_in': dw_in,
        's5_lam_re': dlam_re.reshape(S5_GROUPS, S5_STATE), 's5_lam_im': dlam_im.reshape(S5_GROUPS, S5_STATE),
        's5_log_dt': dlogdt[0, :S5_GROUPS], 's5_b_re': unb(db_re_t), 's5_b_im': unb(db_im_t),
        's5_c_re': _block_diag_c_t(dc4_re), 's5_c_im': _block_diag_c_t(dc4_im),
        's5_d': dd.reshape(D_S5), 's5_glu_w': dglu_w, 's5_glu_b': dglu_b.reshape(D_S5),
        'rwkv_mu': dmu.reshape(-1), 'rwkv_w0': dw0.reshape(-1), 'rwkv_w2': dw2p[:HEAD], 'rwkv_a0': da0.reshape(-1),
        'rwkv_a2': da2p[HEAD:], 'rwkv_k_k': dk_k.reshape(-1), 'rwkv_k_a': dk_a.reshape(-1),
        'rwkv_r_k': dr_k.reshape(N_HEADS, HEAD), 'rwkv_ln_w': dln_w.reshape(-1), 'rwkv_ln_b': dln_b.reshape(-1),
        'w_out': dw_out, 'final_g': dgf.reshape(D_MODEL),
    }
    return loss, grad_x, grads


def _exchange(arrays, gather, axes, name):
    n = len(arrays)
    group = 2 ** len(axes)

    def body(*refs):
        send_refs, recv_refs = refs[:n], refs[n:2 * n]
        send_sems, recv_sems, local_sems = refs[2 * n:]
        pos = {ax: lax.axis_index(ax) for ax in ("x", "y", "c")}

        def index_of(p):
            idx = 0
            for ax in axes:
                idx = 2 * idx + p[ax]
            return idx

        me = index_of(pos)
        own, outs, arrivals = [], [], []
        for i, (send_ref, recv_ref) in enumerate(zip(send_refs, recv_refs)):
            def block_for(dev, send_ref=send_ref, whole=gather[i]):
                return send_ref if whole else send_ref.at[dev]

            own.append(pltpu.make_async_copy(block_for(me), recv_ref.at[me], local_sems.at[i]))
            own[-1].start()
            for k in range(1, group):
                peer = dict(pos)
                for bit, ax in enumerate(axes):
                    if (k >> bit) & 1:
                        peer[ax] = 1 - pos[ax]
                peer_idx = index_of(peer)
                sems = dict(send_sem=send_sems.at[i, k - 1], recv_sem=recv_sems.at[i, k - 1],
                            device_id=(peer["x"], peer["y"], peer["c"]), device_id_type=pl.DeviceIdType.MESH)
                outs.append(pltpu.make_async_remote_copy(src_ref=block_for(peer_idx), dst_ref=recv_ref.at[me], **sems))
                outs[-1].start()
                arrivals.append(
                    pltpu.make_async_remote_copy(src_ref=block_for(peer_idx), dst_ref=recv_ref.at[peer_idx], **sems))
        for copy in arrivals:
            copy.wait_recv()
        for copy in outs:
            copy.wait_send()
        for copy in own:
            copy.wait()

    return pl.pallas_call(
        body, name=name, in_specs=[_ANY] * n, out_specs=[_ANY] * n,
        out_shape=[jax.ShapeDtypeStruct(((group,) + a.shape) if whole else a.shape, a.dtype)
                   for a, whole in zip(arrays, gather)],
        scratch_shapes=[pltpu.SemaphoreType.DMA((n, group - 1)), pltpu.SemaphoreType.DMA((n, group - 1)),
                        pltpu.SemaphoreType.DMA((n,))],
        compiler_params=pltpu.CompilerParams(has_side_effects=True),
    )(*arrays)


def _sum_devices(ref):
    g = ref[0].astype(F32)
    for s in range(1, ref.shape[0]):
        g = g + ref[s].astype(F32)
    return g


def _adamw_math(g, w, m, v):
    m_new = ADAM_B1 * m + (1.0 - ADAM_B1) * g
    v_new = ADAM_B2 * v + (1.0 - ADAM_B2) * (g * g)
    m_hat = m_new / (1.0 - ADAM_B1 ** ADAM_STEP)
    v_hat = v_new / (1.0 - ADAM_B2 ** ADAM_STEP)
    return -ADAM_LR * (m_hat / (jnp.sqrt(v_hat) + ADAM_EPS) + ADAM_WD * w), m_new, v_new


def _adamw(gs, ws, ms, vs, reduce, name):
    n = len(ws)

    def body(*refs):
        g_refs, w_refs, m_refs, v_refs = (refs[j * n:(j + 1) * n] for j in range(4))
        outs = refs[4 * n:]
        for i in range(n):
            g = _sum_devices(g_refs[i]) if reduce else g_refs[i][...]
            res = _adamw_math(g, w_refs[i][...], m_refs[i][...], v_refs[i][...])
            for j, val in enumerate(((g,) if reduce else ()) + res):
                outs[j * n + i][...] = val

    return pl.pallas_call(
        body, name=name, out_shape=[_sds(w.shape) for w in ws] * (4 if reduce else 3),
        compiler_params=pltpu.CompilerParams(vmem_limit_bytes=VMEM_LIMIT),
    )(*gs, *ws, *ms, *vs)


def _sum_blocks(recv):
    def body(recv_ref, out_ref):
        out_ref[...] = _sum_devices(recv_ref)

    return pl.pallas_call(body, name="sum_small_grads", out_shape=_sds(recv.shape[1:]))(recv)


_WEIGHTS = [
    ('norm_g', (1, 1024), False), ('w_in', (1, 1024, 400), True), ('s5_lam_re', (1, 32, 64), False),
    ('s5_lam_im', (1, 32, 64), False), ('s5_log_dt', (1, 32), False), ('s5_b_re', (1, 32, 64, 16), False),
    ('s5_b_im', (1, 32, 64, 16), False), ('s5_c_re', (1, 32, 16, 64), False), ('s5_c_im', (1, 32, 16, 64), False),
    ('s5_d', (1, 512), False), ('s5_glu_w', (1, 64, 512), True), ('s5_glu_b', (1, 512), False),
    ('rwkv_mu', (1, 1664), False), ('rwkv_w0', (1, 512), False), ('rwkv_w2', (1, 64, 64), True),
    ('rwkv_a0', (1, 512), False), ('rwkv_a2', (1, 64, 64), True), ('rwkv_k_k', (1, 512), False),
    ('rwkv_k_a', (1, 512), False), ('rwkv_r_k', (1, 8, 64), False), ('rwkv_ln_w', (1, 512), False),
    ('rwkv_ln_b', (1, 512), False), ('w_out', (1, 128, 1024), True), ('final_g', (1024,), False),
]
_SHARDED = [(n, s) for n, s, sharded in _WEIGHTS if sharded]
_SMALL = [(n, s) for n, s, sharded in _WEIGHTS if not sharded]
_COLUMN_SHARDED = ('w_in', 'rwkv_w2', 'rwkv_a2')
_SMALL_SIZE = sum(math.prod(s) for _, s in _SMALL) + 1
_SMALL_ROWS = -(-_SMALL_SIZE // (8 * LANES)) * 8


def _pack_small(grads, loss):
    flat = [grads[n].reshape(-1) for n, _ in _SMALL] + [loss.reshape(1)]
    pad = _SMALL_ROWS * LANES - _SMALL_SIZE
    return jnp.concatenate(flat + [jnp.zeros((pad,), F32)]).reshape(_SMALL_ROWS, LANES)


def _unpack_small(packed):
    flat = packed.reshape(-1)
    out, off = {}, 0
    for n, s in _SMALL:
        size = math.prod(s)
        out[n] = flat[off:off + size].reshape(s)
        off += size
    return out, flat[off]


_BF16_OPERANDS = ('w_in', 's5_glu_w', 'w_out')


def _join_shards(name, blocks):
    _, rows, cols = blocks.shape
    if name in _COLUMN_SHARDED:
        return blocks.transpose(1, 0, 2).reshape(rows, N_DEV * cols)
    return blocks.reshape(N_DEV * rows, cols)


def _split_shards(name, full, shard_shape):
    rows, cols = shard_shape
    if name in _COLUMN_SHARDED:
        return full.reshape(rows, N_DEV, cols).transpose(1, 0, 2)
    return full.reshape(N_DEV, rows, cols)


def kernel(x, norm_g, w_in, s5_lam_re, s5_lam_im, s5_log_dt, s5_b_re, s5_b_im, s5_c_re, s5_c_im, s5_d, s5_glu_w, s5_glu_b, rwkv_mu, rwkv_w0, rwkv_w2, rwkv_a0, rwkv_a2, rwkv_k_k, rwkv_k_a, rwkv_r_k, rwkv_ln_w, rwkv_ln_b, w_out, final_g, loss_target, m_norm_g, m_w_in, m_s5_lam_re, m_s5_lam_im, m_s5_log_dt, m_s5_b_re, m_s5_b_im, m_s5_c_re, m_s5_c_im, m_s5_d, m_s5_glu_w, m_s5_glu_b, m_rwkv_mu, m_rwkv_w0, m_rwkv_w2, m_rwkv_a0, m_rwkv_a2, m_rwkv_k_k, m_rwkv_k_a, m_rwkv_r_k, m_rwkv_ln_w, m_rwkv_ln_b, m_w_out, m_final_g, v_norm_g, v_w_in, v_s5_lam_re, v_s5_lam_im, v_s5_log_dt, v_s5_b_re, v_s5_b_im, v_s5_c_re, v_s5_c_im, v_s5_d, v_s5_glu_w, v_s5_glu_b, v_rwkv_mu, v_rwkv_w0, v_rwkv_w2, v_rwkv_a0, v_rwkv_a2, v_rwkv_k_k, v_rwkv_k_a, v_rwkv_r_k, v_rwkv_ln_w, v_rwkv_ln_b, v_w_out, v_final_g):
    given = dict(locals())

    n_sh = len(_SHARDED)
    everyone = ("x", "y", "c")
    shards = [given[n][0].astype(BF16 if n in _BF16_OPERANDS else F32) for n, _ in _SHARDED]
    shards[0] = shards[0].T
    gathered = _exchange(shards, (True,) * n_sh, everyone, "gather_weights")
    local = {n: _join_shards(n, blocks).astype(F32) for (n, _), blocks in list(zip(_SHARDED, gathered))[1:]}
    local['w_in_t'] = gathered[0].reshape(D_IN, D_MODEL)
    local.update({n: (given[n][0] if len(s) > 1 else given[n]) for n, s in _SMALL})

    loss, grad_x, grads = _local_step(x[0], loss_target[0], local)

    blocks = [_split_shards(n, grads[n], s[1:]).astype(BF16) for n, s in _SHARDED]
    small = _pack_small(grads, loss[0, 0]).reshape(N_DEV, _SMALL_ROWS // N_DEV, LANES)
    recv = _exchange(blocks + [small], (False,) * (n_sh + 1), everyone, "exchange_grads")
    small_sum = _exchange([_sum_blocks(recv[-1])], (True,), everyone, "gather_small_grads")[0]

    result = {}
    for group, name in (([0], "adamw_w_in"), ([1, 2, 3, 4], "adamw_shards")):
        ns = [_SHARDED[i][0] for i in group]
        res = _adamw([recv[i] for i in group], [given[n][0] for n in ns], [given['m_' + n][0] for n in ns],
                     [given['v_' + n][0] for n in ns], True, name)
        for j, n in enumerate(ns):
            result[n] = [res[k * len(ns) + j][None] for k in range(4)]
    g_small, total = _unpack_small(small_sum)
    two_d = lambda t: t.reshape(1, -1) if t.ndim == 1 else t
    ns = [n for n, _ in _SMALL]
    res = _adamw([two_d(g_small[n]) for n in ns], [two_d(given[n]) for n in ns], [two_d(given['m_' + n]) for n in ns],
                 [two_d(given['v_' + n]) for n in ns], False, "adamw_small")
    for j, (n, s) in enumerate(_SMALL):
        result[n] = [g_small[n]] + [res[k * len(ns) + j].reshape(s) for k in range(3)]

    outs = [total, grad_x[None]]
    for k in range(4):
        outs += [result[n][k] for n, _, _ in _WEIGHTS]
    return tuple(outs)
```

```python
import math

import jax
import jax.numpy as jnp
from jax import lax
from jax.experimental import pallas as pl
from jax.experimental.pallas import tpu as pltpu

F32 = jnp.float32
BF16 = jnp.bfloat16
HI = lax.Precision.HIGH

D_MODEL = 1024
D_S5 = 512
D_RWKV = 512
S5_GROUPS = 32
S5_GROUP = 16
S5_STATE = 64
N_STATE = S5_GROUPS * S5_STATE
N_HEADS = 8
HEAD = 64
D_SHIFT = 3 * D_RWKV + 128
D_IN = 2 * D_S5 + D_SHIFT + D_RWKV
NORM_EPS = 1e-6
GN_EPS = 64e-5
N_DEV = 8
LANES = 128
S5_BLOCKS = 4
RWKV_CHUNK = 64
RWKV_CHUNKS_PER_STEP = 4
VMEM_LIMIT = 56 * 1024 * 1024

ADAM_LR = 0.001
ADAM_B1 = 0.9
ADAM_B2 = 0.999
ADAM_EPS = 1e-08
ADAM_WD = 0.01
ADAM_STEP = 10


def _dot(a, b, dims, prec):
    return lax.dot_general(a, b, (dims, ((), ())), precision=prec, preferred_element_type=F32)


def _dot_bf(a, b, dims):
    return _dot(a.astype(BF16), b.astype(BF16), dims, None)


def _make_mm(cast, prec):
    @jax.custom_vjp
    def mm(a, b):
        return _dot(cast(a), cast(b), ((1,), (0,)), prec)

    def fwd(a, b):
        return mm(a, b), (a, b)

    def bwd(res, g):
        a, b = res
        return (_dot(cast(g), cast(b), ((1,), (1,)), prec), _dot(cast(a), cast(g), ((0,), (0,)), prec))

    mm.defvjp(fwd, bwd)
    return mm


mm_bf = _make_mm(lambda t: t.astype(BF16), None)


@jax.custom_vjp
def mm_w(a, w):
    return _dot(a.astype(BF16), w, ((1,), (0,)), None)


def _mm_w_fwd(a, w):
    return mm_w(a, w), w


def _mm_w_bwd(w, g):
    return _dot(g.astype(BF16), w, ((1,), (1,)), None), jnp.zeros_like(w)


mm_w.defvjp(_mm_w_fwd, _mm_w_bwd)


def _make_head_sum(split):
    def product(x, ee):
        hi = x.astype(BF16)
        out = _dot(hi, ee, ((1,), (0,)), None)
        if split:
            out = out + _dot((x - hi.astype(F32)).astype(BF16), ee, ((1,), (0,)), None)
        return out

    @jax.custom_vjp
    def head_sum(x, ee):
        return product(x, ee)

    def fwd(x, ee):
        return product(x, ee), ee

    def bwd(ee, g):
        return product(g, ee), jnp.zeros_like(ee)

    head_sum.defvjp(fwd, bwd)
    return head_sum


head_sum = _make_head_sum(False)
head_sum_split = _make_head_sum(True)


@jax.custom_vjp
def _sigmoid(x):
    return 1.0 / (1.0 + jnp.exp(-x))


def _sigmoid_fwd(x):
    s = _sigmoid(x)
    return s, s


_sigmoid.defvjp(_sigmoid_fwd, lambda s, g: (g * s * (1.0 - s),))


@jax.custom_vjp
def _silu(x):
    return x * _sigmoid(x)


def _silu_fwd(x):
    s = _sigmoid(x)
    return x * s, (x, s)


def _silu_bwd(res, g):
    x, s = res
    return (g * s * (1.0 + x * (1.0 - s)),)


_silu.defvjp(_silu_fwd, _silu_bwd)


@jax.custom_vjp
def _softplus(x):
    return jnp.maximum(x, 0.0) + jnp.log(1.0 + jnp.exp(-jnp.abs(x)))


def _softplus_fwd(x):
    e = jnp.exp(-jnp.abs(x))
    return jnp.maximum(x, 0.0) + jnp.log(1.0 + e), (x, e)


def _softplus_bwd(res, g):
    x, e = res
    return (g * jnp.where(x >= 0.0, 1.0, e) / (1.0 + e),)


_softplus.defvjp(_softplus_fwd, _softplus_bwd)


@jax.custom_vjp
def _normalize_heads(x, ee):
    return x / jnp.maximum(jnp.sqrt(head_sum(x * x, ee)), 1e-12)


def _normalize_heads_fwd(x, ee):
    norm = jnp.sqrt(head_sum(x * x, ee))
    inv = 1.0 / jnp.maximum(norm, 1e-12)
    y = x * inv
    return y, (y, inv, norm, ee)


def _normalize_heads_bwd(res, g):
    y, inv, norm, ee = res
    along = jnp.where(norm > 1e-12, head_sum(g * y, ee), 0.0)
    return inv * (g - y * along), jnp.zeros_like(ee)


_normalize_heads.defvjp(_normalize_heads_fwd, _normalize_heads_bwd)


_GELU_C = 2.0 * math.sqrt(2.0 / math.pi)


def _gelu_gate(x):
    return 1.0 / (1.0 + jnp.exp(-_GELU_C * x * (1.0 + 0.044715 * (x * x))))


@jax.custom_vjp
def _gelu(x):
    return x * _gelu_gate(x)


def _gelu_fwd(x):
    s = _gelu_gate(x)
    return x * s, (x, s)


def _gelu_bwd(res, g):
    x, s = res
    return (g * (s + x * s * (1.0 - s) * (_GELU_C * (1.0 + 3.0 * 0.044715 * (x * x)))),)


_gelu.defvjp(_gelu_fwd, _gelu_bwd)


def _rms(x, g):
    return x * lax.rsqrt(jnp.mean(x * x, axis=-1, keepdims=True) + NORM_EPS) * g


def _const_spec(shape):
    nd = len(shape)
    return pl.BlockSpec(shape, lambda *_: (0,) * nd, pipeline_mode=pl.Buffered(1))


def _acc_spec(shape):
    nd = len(shape)
    return pl.BlockSpec(shape, lambda *_: (0,) * nd)


def _params(sem):
    return pltpu.CompilerParams(dimension_semantics=(sem,), vmem_limit_bytes=VMEM_LIMIT)


_ANY = pl.BlockSpec(memory_space=pl.ANY)


def _sds(shape):
    return jax.ShapeDtypeStruct(shape, F32)


def _head_sum_matrix():
    i = jnp.arange(D_RWKV) // HEAD
    return (i[:, None] == i[None, :]).astype(BF16)


def _s5_param_fn(lam_re, lam_im, logdt, b_re, b_im):
    dt = jnp.exp(logdt)
    mag = jnp.exp(lam_re * dt)
    ang = lam_im * dt
    lbr = mag * jnp.cos(ang)
    lbi = mag * jnp.sin(ang)
    nr = lbr - 1.0
    den = lam_re * lam_re + lam_im * lam_im
    cr = (nr * lam_re + lbi * lam_im) / den
    ci = (lbi * lam_re - nr * lam_im) / den
    return lbr, lbi, cr * b_re - ci * b_im, cr * b_im + ci * b_re


def _cmul(ar, ai, br, bi):
    return ar * br - ai * bi, ar * bi + ai * br


def _s5_param_fwd(lam_re, lam_im, logdt, b_re, b_im):
    def body(lr, li, ld, br, bi, o_br, o_bi, o_pr, o_pi, o_qr, o_qi):
        lbr, lbi, bbr, bbi = _s5_param_fn(lr[...], li[...], ld[...], br[...], bi[...])
        o_br[...] = bbr
        o_bi[...] = bbi
        rid = lax.broadcasted_iota(jnp.int32, (8, N_STATE), 0)
        pr, pi_ = lbr, lbi
        fwd_r = rev_r = jnp.broadcast_to(pr, (8, N_STATE))
        fwd_i = rev_i = jnp.broadcast_to(pi_, (8, N_STATE))
        for j in range(1, 8):
            pr, pi_ = _cmul(pr, pi_, lbr, lbi)
            fwd_r = jnp.where(rid == j, jnp.broadcast_to(pr, (8, N_STATE)), fwd_r)
            fwd_i = jnp.where(rid == j, jnp.broadcast_to(pi_, (8, N_STATE)), fwd_i)
            rev_r = jnp.where(rid == 7 - j, jnp.broadcast_to(pr, (8, N_STATE)), rev_r)
            rev_i = jnp.where(rid == 7 - j, jnp.broadcast_to(pi_, (8, N_STATE)), rev_i)
        o_pr[...] = fwd_r
        o_pi[...] = fwd_i
        o_qr[...] = rev_r
        o_qi[...] = -rev_i

    return pl.pallas_call(
        body, name="s5_param_fwd",
        out_shape=[_sds((S5_GROUP, N_STATE))] * 2 + [_sds((8, N_STATE))] * 4,
    )(lam_re, lam_im, logdt, b_re, b_im)


def _s5_param_bwd(lam_re, lam_im, logdt, b_re, b_im, d_lbr, d_lbi, d_bbr, d_bbi, group_ind):
    def body(lr, li, ld, br, bi, g0, g1, g2, g3, ind, o_lr, o_li, o_ld, o_br, o_bi):
        _, vjp = jax.vjp(_s5_param_fn, lr[...], li[...], ld[...], br[...], bi[...])
        d_lr, d_li, d_ld, d_br, d_bi = vjp((g0[...], g1[...], g2[...], g3[...]))
        o_lr[...] = d_lr
        o_li[...] = d_li
        o_ld[...] = _dot(jnp.broadcast_to(d_ld, (8, N_STATE)), ind[...], ((1,), (0,)), HI)
        o_br[...] = d_br
        o_bi[...] = d_bi

    return pl.pallas_call(
        body, name="s5_param_bwd",
        out_shape=[_sds((1, N_STATE))] * 2 + [_sds((8, LANES))] + [_sds((S5_GROUP, N_STATE))] * 2,
    )(lam_re, lam_im, logdt, b_re, b_im, d_lbr, d_lbi, d_bbr, d_bbi, group_ind)


def _fwd_in(x, norm_g, w_in_bf, tt):
    L = x.shape[0]

    def body(x_ref, g_ref, w_ref, u_ref, zs_ref, rw_ref, zr_ref):
        h = _rms(x_ref[...], g_ref[...])
        proj = _dot(h.astype(BF16), w_ref[...], ((1,), (1,)), None)
        u_ref[...] = proj[:, 0:D_S5]
        zs_ref[...] = proj[:, D_S5:2 * D_S5]
        rw_ref[...] = proj[:, 2 * D_S5:2 * D_S5 + D_SHIFT]
        zr_ref[...] = proj[:, 2 * D_S5 + D_SHIFT:D_IN]

    row = lambda n: pl.BlockSpec((tt, n), lambda i: (i, 0))
    return pl.pallas_call(
        body, name="fwd_in", grid=(L // tt,),
        in_specs=[row(D_MODEL), _const_spec((1, D_MODEL)), _const_spec((D_IN, D_MODEL))],
        out_specs=[row(D_S5), row(D_S5), row(D_SHIFT), row(D_RWKV)],
        out_shape=[_sds((L, D_S5)), _sds((L, D_S5)), _sds((L, D_SHIFT)), _sds((L, D_RWKV))],
        compiler_params=_params("parallel"),
    )(x, norm_g, w_in_bf)


S5_LANE_CHUNK = 512


def _tile_scan(re_ref, im_ref, pow_r_ref, pow_i_ref, carry_r_ref, carry_i_ref, reverse):
    t, n = re_ref.shape
    n_groups = t // 8
    ch = S5_LANE_CHUNK
    rid = lax.broadcasted_iota(jnp.int32, (8, ch), 0)
    for c in range(n // ch):
        cols = slice(c * ch, (c + 1) * ch)
        pow_r = pow_r_ref[:, cols]
        pow_i = pow_i_ref[:, cols]
        row = lambda tile, j: jnp.broadcast_to(tile[j:j + 1], (8, ch))
        levels = []
        for d in (1, 2, 4):
            keep = (rid < 8 - d) if reverse else (rid >= d)
            j = 8 - d if reverse else d - 1
            levels.append(((8 - d) if reverse else d,
                           jnp.where(keep, row(pow_r, j), 0.0), jnp.where(keep, row(pow_i, j), 0.0)))

        def group(g, carry):
            r0 = pl.multiple_of(((n_groups - 1 - g) if reverse else g) * 8, 8)
            xr = re_ref[pl.ds(r0, 8), cols]
            xi = im_ref[pl.ds(r0, 8), cols]
            for shift, lr, li in levels:
                mr, mi = _cmul(lr, li, pltpu.roll(xr, shift, axis=0), pltpu.roll(xi, shift, axis=0))
                xr = xr + mr
                xi = xi + mi
            mr, mi = _cmul(pow_r, pow_i, carry[0], carry[1])
            xr = xr + mr
            xi = xi + mi
            re_ref[pl.ds(r0, 8), cols] = xr
            im_ref[pl.ds(r0, 8), cols] = xi
            last = 0 if reverse else 7
            return row(xr, last), row(xi, last)

        out = lax.fori_loop(0, n_groups, group, (carry_r_ref[:, cols], carry_i_ref[:, cols]))
        carry_r_ref[:, cols] = out[0]
        carry_i_ref[:, cols] = out[1]


def _s5_fwd(u, b4_re, b4_im, c4_re, c4_im, pow_r, pow_i, tt):
    L = u.shape[0]

    def body(u_ref, bre_ref, bim_ref, cre_ref, cim_ref, pr_ref, pi_ref, sre_o, sim_o, y_o, car_r, car_i):
        @pl.when(pl.program_id(0) == 0)
        def _():
            car_r[...] = jnp.zeros_like(car_r)
            car_i[...] = jnp.zeros_like(car_i)

        uv = u_ref[...]
        for q in range(S5_BLOCKS):
            uq = uv[:, q * LANES:(q + 1) * LANES]
            cols = slice(q * 512, (q + 1) * 512)
            sre_o[:, cols] = _dot_bf(uq, bre_ref[q], ((1,), (0,)))
            sim_o[:, cols] = _dot_bf(uq, bim_ref[q], ((1,), (0,)))
        _tile_scan(sre_o, sim_o, pr_ref, pi_ref, car_r, car_i, reverse=False)
        for q in range(S5_BLOCKS):
            cols = slice(q * 512, (q + 1) * 512)
            y_o[:, q * LANES:(q + 1) * LANES] = (_dot_bf(sre_o[:, cols], cre_ref[q], ((1,), (0,)))
                                                 - _dot_bf(sim_o[:, cols], cim_ref[q], ((1,), (0,))))

    row = lambda n: pl.BlockSpec((tt, n), lambda i: (i, 0))
    return pl.pallas_call(
        body, name="s5_fwd", grid=(L // tt,),
        in_specs=[row(D_S5)] + [_const_spec((S5_BLOCKS, LANES, 512))] * 2 + [_const_spec((S5_BLOCKS, 512, LANES))] * 2
        + [_const_spec((8, N_STATE))] * 2,
        out_specs=[row(N_STATE), row(N_STATE), row(D_S5)],
        out_shape=[_sds((L, N_STATE)), _sds((L, N_STATE)), _sds((L, D_S5))],
        scratch_shapes=[pltpu.VMEM((8, N_STATE), F32)] * 2,
        compiler_params=_params("arbitrary"),
    )(u, b4_re, b4_im, c4_re, c4_im, pow_r, pow_i)


def _rwkv_pre_fn(r, k, v, wa, w0, w2p, a0, a2p, k_k, k_a, ee):
    w = -_softplus(-(w0 + mm_bf(jnp.tanh(wa), w2p))) - 0.5
    logw = -jnp.exp(w)
    a = _sigmoid(a0 + mm_bf(wa, a2p))
    kk = _normalize_heads(k * k_k, ee)
    k2 = k * (1.0 + (a - 1.0) * k_a)
    return r, logw, k2, v, -kk, kk * a


N_PAIRS = N_HEADS // 2


def _head_spec(tt):
    return pl.BlockSpec((N_PAIRS, tt, LANES), lambda i: (0, i, 0))


def _load_heads(ref):
    return jnp.concatenate([ref[p] for p in range(N_PAIRS)], axis=-1)


def _store_heads(ref, val):
    for p in range(N_PAIRS):
        ref[p] = val[:, p * LANES:(p + 1) * LANES]


def _split_pairs(x):
    return jnp.concatenate([x[:, :, :HEAD], x[:, :, HEAD:]], axis=0)


def _join_pairs(x):
    return jnp.concatenate([x[:N_PAIRS], x[N_PAIRS:]], axis=-1)


def _shifted(rw, prev_blk, first):
    rolled = pltpu.roll(rw, 1, axis=0)
    prev_row = jnp.where(first, 0.0, prev_blk[7:8, :])
    rid = lax.broadcasted_iota(jnp.int32, rw.shape, 0)
    return jnp.where(rid == 0, jnp.broadcast_to(prev_row, rw.shape), rolled)


def _split_rw(t):
    return t[:, 0:512], t[:, 512:1024], t[:, 1024:1536], t[:, 1536:1664]


def _rwkv_pre_specs(tt):
    row = pl.BlockSpec((tt, D_SHIFT), lambda i: (i, 0))
    prev = pl.BlockSpec((8, D_SHIFT), lambda i: (jnp.maximum(i * (tt // 8) - 1, 0), 0))
    consts = [_const_spec((1, D_SHIFT)), _const_spec((1, D_RWKV)), _const_spec((LANES, D_RWKV)),
              _const_spec((1, D_RWKV)), _const_spec((LANES, D_RWKV)), _const_spec((1, D_RWKV)),
              _const_spec((1, D_RWKV)), _const_spec((D_RWKV, D_RWKV))]
    return [row, prev] + consts


def _rwkv_pre_fwd(rw, mu, w0, w2p, a0, a2p, k_k, k_a, ee, tt):
    L = rw.shape[0]

    def body(rw_ref, prev_ref, mu_ref, w0_ref, w2_ref, a0_ref, a2_ref, kk_ref, ka_ref, ee_ref, *outs):
        rwv = rw_ref[...]
        rws = rwv + (_shifted(rwv, prev_ref[...], pl.program_id(0) == 0) - rwv) * mu_ref[...]
        res = _rwkv_pre_fn(*_split_rw(rws), w0_ref[...], w2_ref[...], a0_ref[...], a2_ref[...],
                           kk_ref[...], ka_ref[...], ee_ref[...])
        for o, val in zip(outs, res):
            _store_heads(o, val)

    return pl.pallas_call(
        body, name="rwkv_pre_fwd", grid=(L // tt,),
        in_specs=_rwkv_pre_specs(tt), out_specs=[_head_spec(tt)] * 6, out_shape=[_sds((N_PAIRS, L, LANES))] * 6,
        compiler_params=_params("parallel"),
    )(rw, rw, mu, w0, w2p, a0, a2p, k_k, k_a, ee)


def _rwkv_pre_bwd(rw, mu, w0, w2p, a0, a2p, k_k, k_a, ee, cots, tt):
    L = rw.shape[0]
    n_t = L // tt

    def body(rw_ref, prev_ref, mu_ref, w0_ref, w2_ref, a0_ref, a2_ref, kk_ref, ka_ref, ee_ref,
             c_r, c_w, c_k, c_v, c_a, c_b, cb_r, cb_k, cb_v,
             drws_ref, dmu_o, dw0_o, dw2_o, da0_o, da2_o, dkk_o, dka_o,
             dmu, dw0, dw2, da0, da2, dkk, dka):
        i = pl.program_id(0)
        accs = (dmu, dw0, dw2, da0, da2, dkk, dka)

        @pl.when(i == 0)
        def _():
            for acc in accs:
                acc[...] = jnp.zeros_like(acc)

        rwv = rw_ref[...]
        diff = _shifted(rwv, prev_ref[...], i == 0) - rwv
        rws = rwv + diff * mu_ref[...]
        consts = (w0_ref[...], w2_ref[...], a0_ref[...], a2_ref[...], kk_ref[...], ka_ref[...])
        _, vjp = jax.vjp(lambda *a: _rwkv_pre_fn(*a, ee_ref[...]), *_split_rw(rws), *consts)
        scan = [_load_heads(c) for c in (c_r, c_w, c_k, c_v, c_a, c_b)]
        g = vjp((scan[0] + cb_r[...], scan[1], scan[2] + cb_k[...], scan[3] + cb_v[...], scan[4], scan[5]))
        drws = jnp.concatenate(g[0:4], axis=-1)
        drws_ref[...] = drws
        dmu[...] += jnp.sum(drws * diff, axis=0, keepdims=True)
        for acc, val in zip(accs[1:], g[4:]):
            acc[...] += val

        @pl.when(i == n_t - 1)
        def _():
            for acc, out in zip(accs, (dmu_o, dw0_o, dw2_o, da0_o, da2_o, dkk_o, dka_o)):
                out[...] = acc[...]

    row = pl.BlockSpec((tt, D_RWKV), lambda i: (i, 0))
    shapes = [(1, D_SHIFT), (1, D_RWKV), (LANES, D_RWKV), (1, D_RWKV), (LANES, D_RWKV), (1, D_RWKV), (1, D_RWKV)]
    return pl.pallas_call(
        body, name="rwkv_pre_bwd", grid=(n_t,),
        in_specs=_rwkv_pre_specs(tt) + [_head_spec(tt)] * 6 + [row] * 3,
        out_specs=[pl.BlockSpec((tt, D_SHIFT), lambda i: (i, 0))] + [_acc_spec(s) for s in shapes],
        out_shape=[_sds((L, D_SHIFT))] + [_sds(s) for s in shapes],
        scratch_shapes=[pltpu.VMEM(s, F32) for s in shapes],
        compiler_params=_params("arbitrary"),
    )(rw, rw, mu, w0, w2p, a0, a2p, k_k, k_a, ee, *cots)


def _bmm(a, b):
    return lax.dot_general(a, b, (((2,), (1,)), ((0,), (0,))), precision=HI, preferred_element_type=F32)


def _bmm_nt(a, b):
    return lax.dot_general(a, b, (((2,), (2,)), ((0,), (0,))), precision=HI, preferred_element_type=F32)


def _bmm_tn(a, b):
    return lax.dot_general(a, b, (((1,), (1,)), ((0,), (0,))), precision=HI, preferred_element_type=F32)


def _bdot_bf(a, b, lhs_dim, rhs_dim):
    return lax.dot_general(a.astype(BF16), b.astype(BF16), (((lhs_dim,), (rhs_dim,)), ((0,), (0,))),
                           preferred_element_type=F32)


@jax.custom_vjp
def _bmm_bf(a, b):
    return _bdot_bf(a, b, 2, 1)


def _bmm_bf_fwd(a, b):
    return _bmm_bf(a, b), (a, b)


def _bmm_bf_bwd(res, g):
    a, b = res
    return _bdot_bf(g, b, 2, 2), _bdot_bf(a, g, 1, 1)


_bmm_bf.defvjp(_bmm_bf_fwd, _bmm_bf_bwd)


@jax.custom_vjp
def _bmm_tn_bf(a, b):
    return _bdot_bf(a, b, 1, 1)


def _bmm_tn_bf_fwd(a, b):
    return _bmm_tn_bf(a, b), (a, b)


def _bmm_tn_bf_bwd(res, g):
    a, b = res
    return _bdot_bf(b, g, 2, 2), _bdot_bf(a, g, 2, 1)


_bmm_tn_bf.defvjp(_bmm_tn_bf_fwd, _bmm_tn_bf_bwd)


def _unit_lower_inverse(a):
    t = a.shape[-1]
    ti = lax.broadcasted_iota(jnp.int32, (t, t), 0)
    si = lax.broadcasted_iota(jnp.int32, (t, t), 1)

    def same_block(bits):
        shift = jnp.int32(bits)
        return (lax.shift_right_logical(ti, shift) == lax.shift_right_logical(si, shift))[None]

    def mm(x, y):
        return _bdot_bf(x, y, 2, 1)

    d = jnp.where(same_block(3), a, 0.0)
    inv = jnp.where(ti == si, 1.0, 0.0)[None] + d
    pw = mm(d, d)
    both = mm(jnp.concatenate([inv, pw], axis=1), pw)
    inv = inv + both[:, :t]
    inv = inv + mm(inv, both[:, t:])
    bits = 3
    while (1 << bits) < t:
        e = jnp.where(same_block(bits), 0.0, jnp.where(same_block(bits + 1), a, 0.0))
        inv = inv + mm(mm(inv, e), inv)
        bits += 1
    return inv


def _tri_mask(t):
    ri = lax.broadcasted_iota(jnp.int32, (2 * t, 2 * t), 0)
    ci = lax.broadcasted_iota(jnp.int32, (2 * t, 2 * t), 1)
    top_rows = ri < t
    diff = jnp.where(top_rows, ri, ri - t) - jnp.where(ci < t, ci, ci - t)
    return (diff >= jnp.where(top_rows, 1, 0))[None]


def _ones_tri(n_h, t):
    ti = lax.broadcasted_iota(jnp.int32, (t, t), 0)
    si = lax.broadcasted_iota(jnp.int32, (t, t), 1)
    return jnp.broadcast_to(jnp.where(ti >= si, 1.0, 0.0)[None], (n_h, t, t))


@jax.custom_vjp
def _running_sum_kept(logw, kept):
    return kept


def _running_sum_kept_bwd(shape, g):
    return _bmm_tn(_ones_tri(shape[0], shape[1]), g), jnp.zeros_like(g)


_running_sum_kept.defvjp(lambda logw, kept: (kept, logw.shape), _running_sum_kept_bwd)


@jax.custom_vjp
def _tri_products_kept(ar, bk, kept):
    return kept


def _tri_products_kept_bwd(res, g):
    ar, bk = res
    g = jnp.where(_tri_mask(ar.shape[1] // 2), g, 0.0)
    return _bmm(g, bk), _bmm_tn(g, ar), jnp.zeros_like(g)


_tri_products_kept.defvjp(lambda ar, bk, kept: (kept, (ar, bk)), _tri_products_kept_bwd)


@jax.custom_vjp
def _solve_unit_lower(a, rhs, inv, kept=None):
    return _bmm(inv, rhs) if kept is None else kept


def _solve_fwd(a, rhs, inv, kept=None):
    u = _bmm(inv, rhs) if kept is None else kept
    return u, (inv, u, kept is not None)


def _solve_bwd(res, du):
    inv, u, had_kept = res
    d_rhs = _bmm_tn(inv, du)
    return _bmm_nt(d_rhs, u), d_rhs, jnp.zeros_like(inv), (jnp.zeros_like(u) if had_kept else None)


_solve_unit_lower.defvjp(_solve_fwd, _solve_bwd)


def _rwkv_chunk(st0, r, logw, k, v, a, b, kept=None):
    n_h, t, _ = r.shape
    log_p = _bmm(_ones_tri(n_h, t), logw) if kept is None else _running_sum_kept(logw, kept[0])
    p_in = jnp.exp(log_p)
    p_inv = jnp.exp(-log_p)
    at = a * jnp.exp(log_p - logw)
    rt = r * p_in
    ar = jnp.concatenate([at, rt], axis=1)
    bk = jnp.concatenate([b * p_inv, k * p_inv], axis=1)
    if kept is None:
        m = jnp.where(_tri_mask(t), _bmm_nt(ar, bk), 0.0)
        inv = _unit_lower_inverse(m[:, :t, :t])
    else:
        m = _tri_products_kept(ar, bk, kept[1])
        inv = kept[2]
    top, bottom = m[:, :t], m[:, t:]
    rhs = _bmm_bf(jnp.concatenate([at, top[:, :, t:]], axis=2), jnp.concatenate([st0, v], axis=1))
    u = _solve_unit_lower(top[:, :, :t], rhs, inv, None if kept is None else kept[3])
    y = _bmm_bf(jnp.concatenate([rt, bottom], axis=2), jnp.concatenate([st0, u, v], axis=1))
    p_end = jnp.swapaxes(p_in[:, t - 1:t, :], 1, 2)
    st1 = (st0 + _bmm_tn_bf(bk, jnp.concatenate([u, v], axis=1))) * p_end
    return y, st1, (log_p, m, inv, u)


def _rwkv_scan_fwd(ops):
    n_h, L, n = N_HEADS, ops[0].shape[1], HEAD
    t = RWKV_CHUNK
    per = min(RWKV_CHUNKS_PER_STEP, L // t)
    n_c = L // t
    n_s = n_c // per

    def body(r_ref, w_ref, k_ref, v_ref, a_ref, b_ref, y_ref, st_ref, logp_ref, m_ref, inv_ref, u_ref, st):
        @pl.when(pl.program_id(0) == 0)
        def _():
            st[...] = jnp.zeros_like(st)

        st0 = st[...]
        for j in range(per):
            rows = slice(j * t, (j + 1) * t)
            st_ref[j] = st0
            y, st0, (log_p, m, inv, u) = _rwkv_chunk(
                st0, *(_split_pairs(ref[:, rows, :]) for ref in (r_ref, w_ref, k_ref, v_ref, a_ref, b_ref)))
            y_ref[:, rows, :] = _join_pairs(y)
            logp_ref[:, rows, :] = log_p
            u_ref[:, rows, :] = u
            m_ref[j] = m
            inv_ref[j] = inv
        st[...] = st0

    pairs = pl.BlockSpec((N_PAIRS, per * t, LANES), lambda c: (0, c, 0))
    blk = pl.BlockSpec((n_h, per * t, n), lambda c: (0, c, 0))
    per_chunk = lambda m: pl.BlockSpec((per, n_h, m, m), lambda c: (c, 0, 0, 0))
    return pl.pallas_call(
        body, name="rwkv_scan_fwd", grid=(n_s,), in_specs=[pairs] * 6,
        out_specs=[pairs, per_chunk(n), blk, per_chunk(2 * t), per_chunk(t), blk],
        out_shape=[_sds((N_PAIRS, L, LANES)), _sds((n_c, n_h, n, n)), _sds((n_h, L, n)),
                   _sds((n_c, n_h, 2 * t, 2 * t)), _sds((n_c, n_h, t, t)), _sds((n_h, L, n))],
        scratch_shapes=[pltpu.VMEM((n_h, n, n), F32)],
        compiler_params=_params("arbitrary"),
    )(*ops)


def _rwkv_scan_bwd(ops, states, kept, dy):
    n_h, L, n = N_HEADS, ops[0].shape[1], HEAD
    t = RWKV_CHUNK
    per = min(RWKV_CHUNKS_PER_STEP, L // t)
    n_s = L // t // per

    def body(r_ref, w_ref, k_ref, v_ref, a_ref, b_ref, st_ref, logp_ref, m_ref, inv_ref, u_ref, dy_ref,
             dr, dw, dk, dv, da, db, dst):
        @pl.when(pl.program_id(0) == 0)
        def _():
            dst[...] = jnp.zeros_like(dst)

        vjps = []
        for j in range(per):
            rows = slice(j * t, (j + 1) * t)
            have = (logp_ref[:, rows, :], m_ref[j], inv_ref[j], u_ref[:, rows, :])
            args = [_split_pairs(ref[:, rows, :]) for ref in (r_ref, w_ref, k_ref, v_ref, a_ref, b_ref)]
            vjps.append(jax.vjp(lambda *a, have=have: _rwkv_chunk(*a, kept=have)[:2], st_ref[j], *args)[1])
        d_state = dst[...]
        for j in reversed(range(per)):
            rows = slice(j * t, (j + 1) * t)
            g = vjps[j]((_split_pairs(dy_ref[:, rows, :]), d_state))
            d_state = g[0]
            for out, val in zip((dr, dw, dk, dv, da, db), g[1:]):
                out[:, rows, :] = _join_pairs(val)
        dst[...] = d_state

    pairs = pl.BlockSpec((N_PAIRS, per * t, LANES), lambda c: (0, n_s - 1 - c, 0))
    blk = pl.BlockSpec((n_h, per * t, n), lambda c: (0, n_s - 1 - c, 0))
    per_chunk = lambda m: pl.BlockSpec((per, n_h, m, m), lambda c: (n_s - 1 - c, 0, 0, 0))
    return pl.pallas_call(
        body, name="rwkv_scan_bwd", grid=(n_s,),
        in_specs=[pairs] * 6 + [per_chunk(n), blk, per_chunk(2 * t), per_chunk(t), blk, pairs],
        out_specs=[pairs] * 6, out_shape=[_sds((N_PAIRS, L, LANES))] * 6,
        scratch_shapes=[pltpu.VMEM((n_h, n, n), F32)],
        compiler_params=_params("arbitrary"),
    )(*ops, states, *kept, dy)


def _post_fn(x, u, zs, zr, ysc, r, k2, v, y_ssm, gate_in, d, glu_b, ln_w, ln_b, r_k, gf,
              glu_w, wo_s5, wo_rwkv, tgt, ee):
    y3 = _gelu(y_ssm + d * u)
    y_s5 = y3 * _sigmoid(mm_w(y3, glu_w) + glu_b + gate_in) * _silu(zs)
    mean = head_sum_split(ysc, ee) * (1.0 / HEAD)
    yc = ysc - mean
    var = head_sum(yc * yc, ee) * (1.0 / HEAD)
    gn = yc * lax.rsqrt(var + GN_EPS) * ln_w + ln_b
    bonus = head_sum(r * k2 * r_k, ee) * v
    y_rwkv = (gn + bonus) * _silu(zr)
    x2 = x + mm_w(y_s5, wo_s5) + mm_w(y_rwkv, wo_rwkv)
    err = _rms(x2, gf) - tgt
    return 0.5 * jnp.mean(err * err, axis=-1, keepdims=True), (y3, y_s5, y_rwkv)


def _post(x, u, zs, zr, ysc, r, k2, v, y_ssm, d, glu_w, glu_b, ln_w, ln_b, r_k, w_out, gf, tgt, ee, tt):
    L = x.shape[0]
    n_t = L // tt
    acc_shapes = [(1, D_S5), (D_S5, D_S5), (1, D_S5), (1, D_RWKV), (1, D_RWKV), (1, D_RWKV),
                  (D_MODEL, D_MODEL), (1, D_MODEL), (8, LANES)]

    def body(x_ref, u_ref, zs_ref, zr_ref, ysc_ref, r_ref, k2_ref, v_ref, yssm_ref,
             d_ref, gw_ref, gb_ref, lw_ref, lb_ref, rk_ref, wo_ref, gf_ref, tgt_ref, ee_ref,
             dx_o, du_o, dzs_o, dzr_o, dysc_o, dr_o, dk2_o, dv_o, dyssm_o,
             dd_o, dgw_o, dgb_o, dlw_o, dlb_o, drk_o, dwo_o, dgf_o, loss_o,
             dd, dgw, dgb, dlw, dlb, drk, dwo, dgf, loss):
        i = pl.program_id(0)
        accs = (dd, dgw, dgb, dlw, dlb, drk, dwo, dgf, loss)

        @pl.when(i == 0)
        def _():
            for acc in accs:
                acc[...] = jnp.zeros_like(acc)

        args = (x_ref[...], u_ref[...], zs_ref[...], zr_ref[...],
                _load_heads(ysc_ref), _load_heads(r_ref), _load_heads(k2_ref), _load_heads(v_ref), yssm_ref[...],
                jnp.zeros((tt, D_S5), F32), d_ref[...], gb_ref[...], lw_ref[...], lb_ref[...], rk_ref[...], gf_ref[...])
        rows, vjp, (y3, y_s5, y_rwkv) = jax.vjp(
            lambda *a: _post_fn(*a, gw_ref[...], wo_ref[0:D_S5, :], wo_ref[D_S5:D_MODEL, :], tgt_ref[...],
                                ee_ref[...]), *args, has_aux=True)
        g = vjp(jnp.ones_like(rows))
        for out, val in zip((dx_o, du_o, dzs_o, dzr_o), g[0:4]):
            out[...] = val
        _store_heads(dysc_o, g[4])
        for out, val in zip((dr_o, dk2_o, dv_o, dyssm_o), g[5:9]):
            out[...] = val
        for acc, val in zip((dd, dgb, dlw, dlb, drk, dgf), g[10:16]):
            acc[...] += val
        dgw[...] += _dot_bf(y3, g[9], ((0,), (0,)))
        dwo[0:D_S5, :] += _dot_bf(y_s5, g[0], ((0,), (0,)))
        dwo[D_S5:D_MODEL, :] += _dot_bf(y_rwkv, g[0], ((0,), (0,)))
        loss[...] += jnp.broadcast_to(jnp.sum(rows, axis=0, keepdims=True), loss.shape)

        @pl.when(i == n_t - 1)
        def _():
            for acc, out in zip(accs, (dd_o, dgw_o, dgb_o, dlw_o, dlb_o, drk_o, dwo_o, dgf_o, loss_o)):
                pltpu.sync_copy(acc, out)

    row = lambda n: pl.BlockSpec((tt, n), lambda i: (i, 0))
    in_specs = ([row(D_MODEL)] + [row(512)] * 3 + [_head_spec(tt)] * 4 + [row(D_S5)]
                + [_const_spec(s) for s in [(1, D_S5), (D_S5, D_S5), (1, D_S5), (1, D_RWKV), (1, D_RWKV), (1, D_RWKV),
                                            (D_MODEL, D_MODEL), (1, D_MODEL)]]
                + [row(D_MODEL), _const_spec((D_RWKV, D_RWKV))])
    out_rows = [D_MODEL] + [512] * 3 + [None] + [512] * 4
    return pl.pallas_call(
        body, name="post_fwd_bwd", grid=(n_t,), in_specs=in_specs,
        out_specs=[row(n) if n else _head_spec(tt) for n in out_rows] + [_ANY] * len(acc_shapes),
        out_shape=([_sds((L, n)) if n else _sds((N_PAIRS, L, LANES)) for n in out_rows]
                   + [_sds(s) for s in acc_shapes]),
        scratch_shapes=[pltpu.VMEM(s, F32) for s in acc_shapes],
        compiler_params=_params("arbitrary"),
    )(x, u, zs, zr, ysc, r, k2, v, y_ssm, d, glu_w, glu_b, ln_w, ln_b, r_k, w_out, gf, tgt, ee)


def _s5_bwd(u, du_direct, dy, s_re, s_im, b4_re, b4_im, c4_re, c4_im, pow_r, pow_i, tt):
    L = u.shape[0]
    n_t = L // tt
    acc_shapes = ([(S5_BLOCKS, LANES, 512)] * 2 + [(S5_BLOCKS, 512, LANES)] * 2 + [(1, N_STATE)] * 2)

    def body(u_ref, dud_ref, dy_ref, sre_ref, sim_ref, pre_ref, pim_ref, bre_ref, bim_ref, cre_ref, cim_ref,
             pr_ref, pi_ref, du_o, dbre_o, dbim_o, dcre_o, dcim_o, dlr_o, dli_o,
             dbre, dbim, dcre, dcim, dlr, dli, gre, gim, car_r, car_i):
        i = pl.program_id(0)

        @pl.when(i == 0)
        def _():
            for acc in (dbre, dbim, dcre, dcim, dlr, dli, car_r, car_i):
                acc[...] = jnp.zeros_like(acc)

        uv = u_ref[...]
        dyv = dy_ref[...]
        blocks = [slice(q * 512, (q + 1) * 512) for q in range(S5_BLOCKS)]
        lanes = [slice(q * LANES, (q + 1) * LANES) for q in range(S5_BLOCKS)]
        for q in range(S5_BLOCKS):
            gre[:, blocks[q]] = _dot_bf(dyv[:, lanes[q]], cre_ref[q], ((1,), (1,)))
            gim[:, blocks[q]] = -_dot_bf(dyv[:, lanes[q]], cim_ref[q], ((1,), (1,)))
        _tile_scan(gre, gim, pr_ref, pi_ref, car_r, car_i, reverse=True)
        for q in range(S5_BLOCKS):
            gr = gre[:, blocks[q]]
            gi = gim[:, blocks[q]]
            sr = sre_ref[:, blocks[q]]
            si = sim_ref[:, blocks[q]]
            du_o[:, lanes[q]] = (dud_ref[:, lanes[q]] + _dot_bf(gr, bre_ref[q], ((1,), (1,)))
                                 + _dot_bf(gi, bim_ref[q], ((1,), (1,))))
            dbre[q] += _dot_bf(uv[:, lanes[q]], gr, ((0,), (0,)))
            dbim[q] += _dot_bf(uv[:, lanes[q]], gi, ((0,), (0,)))
            dcre[q] += _dot_bf(sr, dyv[:, lanes[q]], ((0,), (0,)))
            dcim[q] -= _dot_bf(si, dyv[:, lanes[q]], ((0,), (0,)))
            rid = lax.broadcasted_iota(jnp.int32, sr.shape, 0)
            first = i == n_t - 1
            prev_r = jnp.where(first, 0.0, pre_ref[7:8, blocks[q]])
            prev_i = jnp.where(first, 0.0, pim_ref[7:8, blocks[q]])
            pr = jnp.where(rid == 0, jnp.broadcast_to(prev_r, sr.shape), pltpu.roll(sr, 1, axis=0))
            pi_ = jnp.where(rid == 0, jnp.broadcast_to(prev_i, si.shape), pltpu.roll(si, 1, axis=0))
            dlr[:, blocks[q]] += jnp.sum(pr * gr + pi_ * gi, axis=0, keepdims=True)
            dli[:, blocks[q]] += jnp.sum(pr * gi - pi_ * gr, axis=0, keepdims=True)

        @pl.when(i == n_t - 1)
        def _():
            for acc, out in zip((dbre, dbim, dcre, dcim, dlr, dli), (dbre_o, dbim_o, dcre_o, dcim_o, dlr_o, dli_o)):
                out[...] = acc[...]

    row = lambda n: pl.BlockSpec((tt, n), lambda i: (n_t - 1 - i, 0))
    prev = pl.BlockSpec((8, N_STATE), lambda i: (jnp.maximum((n_t - 1 - i) * (tt // 8) - 1, 0), 0))
    return pl.pallas_call(
        body, name="s5_bwd", grid=(n_t,),
        in_specs=[row(D_S5)] * 3 + [row(N_STATE)] * 2 + [prev] * 2
        + [_const_spec((S5_BLOCKS, LANES, 512))] * 2 + [_const_spec((S5_BLOCKS, 512, LANES))] * 2
        + [_const_spec((8, N_STATE))] * 2,
        out_specs=[row(D_S5)] + [_acc_spec(s) for s in acc_shapes],
        out_shape=[_sds((L, D_S5))] + [_sds(s) for s in acc_shapes],
        scratch_shapes=[pltpu.VMEM(s, F32) for s in acc_shapes] + [pltpu.VMEM((tt, N_STATE), F32)] * 2
        + [pltpu.VMEM((8, N_STATE), F32)] * 2,
        compiler_params=_params("arbitrary"),
    )(u, du_direct, dy, s_re, s_im, s_re, s_im, b4_re, b4_im, c4_re, c4_im, pow_r, pow_i)


def _bwd_in(x, norm_g, w_in_bf, mu, dx2, du, dzs, drws, dzr, tt):
    L = x.shape[0]
    n_t = L // tt

    def body(x_ref, g_ref, w_ref, mu_ref, dx2_ref, du_ref, dzs_ref, drws_ref, nxt_ref, dzr_ref,
             gx_o, dw_o, dg_o, dproj, dw, dg):
        i = pl.program_id(0)

        @pl.when(i == 0)
        def _():
            dw[...] = jnp.zeros_like(dw)
            dg[...] = jnp.zeros_like(dg)

        drws_v = drws_ref[...]
        rid = lax.broadcasted_iota(jnp.int32, drws_v.shape, 0)
        nxt_row = jnp.where(i == n_t - 1, 0.0, nxt_ref[0:1, :])
        nxt = jnp.where(rid == tt - 1, jnp.broadcast_to(nxt_row, drws_v.shape), pltpu.roll(drws_v, tt - 1, axis=0))
        muv = mu_ref[...]
        drw = drws_v * (1.0 - muv) + nxt * muv
        dproj[:, 0:D_S5] = du_ref[...].astype(BF16)
        dproj[:, D_S5:2 * D_S5] = dzs_ref[...].astype(BF16)
        dproj[:, 2 * D_S5:2 * D_S5 + D_SHIFT] = drw.astype(BF16)
        dproj[:, 2 * D_S5 + D_SHIFT:D_IN] = dzr_ref[...].astype(BF16)
        dh = _dot(dproj[...], w_ref[...], ((1,), (0,)), None)
        h, vjp = jax.vjp(_rms, x_ref[...], g_ref[...])
        dxh, dgv = vjp(dh)
        gx_o[...] = dx2_ref[...] + dxh
        dg[...] += dgv
        dw[...] += _dot(h.astype(BF16), dproj[...], ((0,), (0,)), None)

        @pl.when(i == n_t - 1)
        def _():
            dg_o[...] = dg[...]
            pltpu.sync_copy(dw, dw_o)

    row = lambda n: pl.BlockSpec((tt, n), lambda i: (i, 0))
    nxt = pl.BlockSpec((8, D_SHIFT), lambda i: (jnp.minimum((i + 1) * (tt // 8), L // 8 - 1), 0))
    return pl.pallas_call(
        body, name="bwd_in", grid=(n_t,),
        in_specs=[row(D_MODEL), _const_spec((1, D_MODEL)), _const_spec((D_IN, D_MODEL)), _const_spec((1, D_SHIFT)),
                  row(D_MODEL), row(D_S5), row(D_S5), row(D_SHIFT), nxt, row(D_RWKV)],
        out_specs=[row(D_MODEL), _ANY, _acc_spec((1, D_MODEL))],
        out_shape=[_sds((L, D_MODEL)), _sds((D_MODEL, D_IN)), _sds((1, D_MODEL))],
        scratch_shapes=[pltpu.VMEM((tt, D_IN), BF16), pltpu.VMEM((D_MODEL, D_IN), F32), pltpu.VMEM((1, D_MODEL), F32)],
        compiler_params=_params("arbitrary"),
    )(x, norm_g, w_in_bf, mu, dx2, du, dzs, drws, drws, dzr)


def _block_diag_b(bbar):
    bb = bbar.reshape(S5_GROUP, S5_BLOCKS, 8, S5_STATE)
    return jnp.einsum('hqgp,Gg->qGhgp', bb, jnp.eye(8, dtype=F32)).reshape(S5_BLOCKS, LANES, 512)


def _block_diag_b_t(db4):
    d = db4.reshape(S5_BLOCKS, 8, S5_GROUP, 8, S5_STATE)
    return jnp.einsum('qGhgp,Gg->hqgp', d, jnp.eye(8, dtype=F32)).reshape(S5_GROUP, N_STATE)


def _block_diag_c(c):
    cc = c.reshape(S5_BLOCKS, 8, S5_GROUP, S5_STATE)
    return jnp.einsum('qghp,gG->qgpGh', cc, jnp.eye(8, dtype=F32)).reshape(S5_BLOCKS, 512, LANES)


def _block_diag_c_t(dc4):
    d = dc4.reshape(S5_BLOCKS, 8, S5_STATE, 8, S5_GROUP)
    return jnp.einsum('qgpGh,gG->qghp', d, jnp.eye(8, dtype=F32)).reshape(S5_GROUPS, S5_GROUP, S5_STATE)


def _local_step(x, tgt, w):
    L = x.shape[0]
    tt = min(512, L)
    tp = min(256, L)
    ee = _head_sum_matrix()

    lam_re = w['s5_lam_re'].reshape(1, N_STATE)
    lam_im = w['s5_lam_im'].reshape(1, N_STATE)
    logdt = jnp.repeat(w['s5_log_dt'], S5_STATE).reshape(1, N_STATE)
    b_re_t = w['s5_b_re'].transpose(2, 0, 1).reshape(S5_GROUP, N_STATE)
    b_im_t = w['s5_b_im'].transpose(2, 0, 1).reshape(S5_GROUP, N_STATE)
    bbr, bbi, pow_r, pow_i, rpow_r, rpow_i = _s5_param_fwd(lam_re, lam_im, logdt, b_re_t, b_im_t)
    b4_re, b4_im = _block_diag_b(bbr), _block_diag_b(bbi)
    c4_re, c4_im = _block_diag_c(w['s5_c_re']), _block_diag_c(w['s5_c_im'])

    norm_g = w['norm_g'].reshape(1, D_MODEL)
    w_in_bf = (w['w_in_t'] if 'w_in_t' in w else w['w_in'].T).astype(BF16)
    u, zs, rw, zr = _fwd_in(x, norm_g, w_in_bf, tt)
    s_re, s_im, y_ssm = _s5_fwd(u, b4_re, b4_im, c4_re, c4_im, pow_r, pow_i, tt)

    row = lambda t: t.reshape(1, -1)
    zpad = jnp.zeros((HEAD, D_RWKV), F32)
    w2p = jnp.concatenate([w['rwkv_w2'], zpad], axis=0)
    a2p = jnp.concatenate([zpad, w['rwkv_a2']], axis=0)
    pre_consts = (row(w['rwkv_mu']), row(w['rwkv_w0']), w2p, row(w['rwkv_a0']), a2p,
                  row(w['rwkv_k_k']), row(w['rwkv_k_a']), ee)
    ops = _rwkv_pre_fwd(rw, *pre_consts, tt)
    ysc, states, *kept = _rwkv_scan_fwd(ops)

    post = _post(x, u, zs, zr, ysc, ops[0], ops[2], ops[3], y_ssm,
                 row(w['s5_d']), w['s5_glu_w'].astype(BF16), row(w['s5_glu_b']), row(w['rwkv_ln_w']), row(w['rwkv_ln_b']),
                 row(w['rwkv_r_k']), w['w_out'].astype(BF16), row(w['final_g']), tgt, ee, tp)
    (dx2, du_d, dzs, dzr, dysc, dr_b, dk2_b, dv_b, dy_ssm,
     dd, dglu_w, dglu_b, dln_w, dln_b, dr_k, dw_out, dgf, loss) = post

    du, db4_re, db4_im, dc4_re, dc4_im, dlbr, dlbi = _s5_bwd(
        u, du_d, dy_ssm, s_re, s_im, b4_re, b4_im, c4_re, c4_im, rpow_r, rpow_i, tt)
    group_ind = (jnp.arange(N_STATE)[:, None] // S5_STATE == jnp.arange(LANES)[None, :]).astype(F32)
    dlam_re, dlam_im, dlogdt, db_re_t, db_im_t = _s5_param_bwd(
        lam_re, lam_im, logdt, b_re_t, b_im_t, dlbr, dlbi, _block_diag_b_t(db4_re), _block_diag_b_t(db4_im), group_ind)

    cots = list(_rwkv_scan_bwd(ops, states, kept, dysc)) + [dr_b, dk2_b, dv_b]
    drws, dmu, dw0, dw2p, da0, da2p, dk_k, dk_a = _rwkv_pre_bwd(rw, *pre_consts, cots, tt)

    grad_x, dw_in, dnorm_g = _bwd_in(x, norm_g, w_in_bf, row(w['rwkv_mu']), dx2, du, dzs, drws, dzr, tt)

    unb = lambda t: t.reshape(S5_GROUP, S5_GROUPS, S5_STATE).transpose(1, 2, 0)
    grads = {
        'norm_g': dnorm_g.reshape(D_MODEL), 'w_in': dw_in,
        's5_lam_re': dlam_re.reshape(S5_GROUPS, S5_STATE), 's5_lam_im': dlam_im.reshape(S5_GROUPS, S5_STATE),
        's5_log_dt': dlogdt[0, :S5_GROUPS], 's5_b_re': unb(db_re_t), 's5_b_im': unb(db_im_t),
        's5_c_re': _block_diag_c_t(dc4_re), 's5_c_im': _block_diag_c_t(dc4_im),
        's5_d': dd.reshape(D_S5), 's5_glu_w': dglu_w, 's5_glu_b': dglu_b.reshape(D_S5),
        'rwkv_mu': dmu.reshape(-1), 'rwkv_w0': dw0.reshape(-1), 'rwkv_w2': dw2p[:HEAD], 'rwkv_a0': da0.reshape(-1),
        'rwkv_a2': da2p[HEAD:], 'rwkv_k_k': dk_k.reshape(-1), 'rwkv_k_a': dk_a.reshape(-1),
        'rwkv_r_k': dr_k.reshape(N_HEADS, HEAD), 'rwkv_ln_w': dln_w.reshape(-1), 'rwkv_ln_b': dln_b.reshape(-1),
        'w_out': dw_out, 'final_g': dgf.reshape(D_MODEL),
    }
    return loss, grad_x, grads


def _exchange(arrays, gather, axes, name):
    n = len(arrays)
    group = 2 ** len(axes)

    def body(*refs):
        send_refs, recv_refs = refs[:n], refs[n:2 * n]
        send_sems, recv_sems, local_sems = refs[2 * n:]
        pos = {ax: lax.axis_index(ax) for ax in ("x", "y", "c")}

        def index_of(p):
            idx = 0
            for ax in axes:
                idx = 2 * idx + p[ax]
            return idx

        me = index_of(pos)
        own, outs, arrivals = [], [], []
        for i, (send_ref, recv_ref) in enumerate(zip(send_refs, recv_refs)):
            def block_for(dev, send_ref=send_ref, whole=gather[i]):
                return send_ref if whole else send_ref.at[dev]

            own.append(pltpu.make_async_copy(block_for(me), recv_ref.at[me], local_sems.at[i]))
            own[-1].start()
            for k in range(1, group):
                peer = dict(pos)
                for bit, ax in enumerate(axes):
                    if (k >> bit) & 1:
                        peer[ax] = 1 - pos[ax]
                peer_idx = index_of(peer)
                sems = dict(send_sem=send_sems.at[i, k - 1], recv_sem=recv_sems.at[i, k - 1],
                            device_id=(peer["x"], peer["y"], peer["c"]), device_id_type=pl.DeviceIdType.MESH)
                outs.append(pltpu.make_async_remote_copy(src_ref=block_for(peer_idx), dst_ref=recv_ref.at[me], **sems))
                outs[-1].start()
                arrivals.append(
                    pltpu.make_async_remote_copy(src_ref=block_for(peer_idx), dst_ref=recv_ref.at[peer_idx], **sems))
        for copy in arrivals:
            copy.wait_recv()
        for copy in outs:
            copy.wait_send()
        for copy in own:
            copy.wait()

    return pl.pallas_call(
        body, name=name, in_specs=[_ANY] * n, out_specs=[_ANY] * n,
        out_shape=[jax.ShapeDtypeStruct(((group,) + a.shape) if whole else a.shape, a.dtype)
                   for a, whole in zip(arrays, gather)],
        scratch_shapes=[pltpu.SemaphoreType.DMA((n, group - 1)), pltpu.SemaphoreType.DMA((n, group - 1)),
                        pltpu.SemaphoreType.DMA((n,))],
        compiler_params=pltpu.CompilerParams(has_side_effects=True),
    )(*arrays)


def _sum_devices(ref):
    g = ref[0].astype(F32)
    for s in range(1, ref.shape[0]):
        g = g + ref[s].astype(F32)
    return g


def _adamw_math(g, w, m, v):
    m_new = ADAM_B1 * m + (1.0 - ADAM_B1) * g
    v_new = ADAM_B2 * v + (1.0 - ADAM_B2) * (g * g)
    m_hat = m_new / (1.0 - ADAM_B1 ** ADAM_STEP)
    v_hat = v_new / (1.0 - ADAM_B2 ** ADAM_STEP)
    return -ADAM_LR * (m_hat / (jnp.sqrt(v_hat) + ADAM_EPS) + ADAM_WD * w), m_new, v_new


def _adamw(gs, ws, ms, vs, reduce, name):
    n = len(ws)

    def body(*refs):
        g_refs, w_refs, m_refs, v_refs = (refs[j * n:(j + 1) * n] for j in range(4))
        outs = refs[4 * n:]
        for i in range(n):
            g = _sum_devices(g_refs[i]) if reduce else g_refs[i][...]
            res = _adamw_math(g, w_refs[i][...], m_refs[i][...], v_refs[i][...])
            for j, val in enumerate(((g,) if reduce else ()) + res):
                outs[j * n + i][...] = val

    return pl.pallas_call(
        body, name=name, out_shape=[_sds(w.shape) for w in ws] * (4 if reduce else 3),
        compiler_params=pltpu.CompilerParams(vmem_limit_bytes=VMEM_LIMIT),
    )(*gs, *ws, *ms, *vs)


def _sum_blocks(recv):
    def body(recv_ref, out_ref):
        out_ref[...] = _sum_devices(recv_ref)

    return pl.pallas_call(body, name="sum_small_grads", out_shape=_sds(recv.shape[1:]))(recv)


_WEIGHTS = [
    ('norm_g', (1, 1024), False), ('w_in', (1, 1024, 400), True), ('s5_lam_re', (1, 32, 64), False),
    ('s5_lam_im', (1, 32, 64), False), ('s5_log_dt', (1, 32), False), ('s5_b_re', (1, 32, 64, 16), False),
    ('s5_b_im', (1, 32, 64, 16), False), ('s5_c_re', (1, 32, 16, 64), False), ('s5_c_im', (1, 32, 16, 64), False),
    ('s5_d', (1, 512), False), ('s5_glu_w', (1, 64, 512), True), ('s5_glu_b', (1, 512), False),
    ('rwkv_mu', (1, 1664), False), ('rwkv_w0', (1, 512), False), ('rwkv_w2', (1, 64, 64), True),
    ('rwkv_a0', (1, 512), False), ('rwkv_a2', (1, 64, 64), True), ('rwkv_k_k', (1, 512), False),
    ('rwkv_k_a', (1, 512), False), ('rwkv_r_k', (1, 8, 64), False), ('rwkv_ln_w', (1, 512), False),
    ('rwkv_ln_b', (1, 512), False), ('w_out', (1, 128, 1024), True), ('final_g', (1024,), False),
]
_SHARDED = [(n, s) for n, s, sharded in _WEIGHTS if sharded]
_SMALL = [(n, s) for n, s, sharded in _WEIGHTS if not sharded]
_COLUMN_SHARDED = ('w_in', 'rwkv_w2', 'rwkv_a2')
_SMALL_SIZE = sum(math.prod(s) for _, s in _SMALL) + 1
_SMALL_ROWS = -(-_SMALL_SIZE // (8 * LANES)) * 8


def _pack_small(grads, loss):
    flat = [grads[n].reshape(-1) for n, _ in _SMALL] + [loss.reshape(1)]
    pad = _SMALL_ROWS * LANES - _SMALL_SIZE
    return jnp.concatenate(flat + [jnp.zeros((pad,), F32)]).reshape(_SMALL_ROWS, LANES)


def _unpack_small(packed):
    flat = packed.reshape(-1)
    out, off = {}, 0
    for n, s in _SMALL:
        size = math.prod(s)
        out[n] = flat[off:off + size].reshape(s)
        off += size
    return out, flat[off]


_BF16_OPERANDS = ('w_in', 's5_glu_w', 'w_out')


def _join_shards(name, blocks):
    _, rows, cols = blocks.shape
    if name in _COLUMN_SHARDED:
        return blocks.transpose(1, 0, 2).reshape(rows, N_DEV * cols)
    return blocks.reshape(N_DEV * rows, cols)


def _split_shards(name, full, shard_shape):
    rows, cols = shard_shape
    if name in _COLUMN_SHARDED:
        return full.reshape(rows, N_DEV, cols).transpose(1, 0, 2)
    return full.reshape(N_DEV, rows, cols)


def kernel(x, norm_g, w_in, s5_lam_re, s5_lam_im, s5_log_dt, s5_b_re, s5_b_im, s5_c_re, s5_c_im, s5_d, s5_glu_w, s5_glu_b, rwkv_mu, rwkv_w0, rwkv_w2, rwkv_a0, rwkv_a2, rwkv_k_k, rwkv_k_a, rwkv_r_k, rwkv_ln_w, rwkv_ln_b, w_out, final_g, loss_target, m_norm_g, m_w_in, m_s5_lam_re, m_s5_lam_im, m_s5_log_dt, m_s5_b_re, m_s5_b_im, m_s5_c_re, m_s5_c_im, m_s5_d, m_s5_glu_w, m_s5_glu_b, m_rwkv_mu, m_rwkv_w0, m_rwkv_w2, m_rwkv_a0, m_rwkv_a2, m_rwkv_k_k, m_rwkv_k_a, m_rwkv_r_k, m_rwkv_ln_w, m_rwkv_ln_b, m_w_out, m_final_g, v_norm_g, v_w_in, v_s5_lam_re, v_s5_lam_im, v_s5_log_dt, v_s5_b_re, v_s5_b_im, v_s5_c_re, v_s5_c_im, v_s5_d, v_s5_glu_w, v_s5_glu_b, v_rwkv_mu, v_rwkv_w0, v_rwkv_w2, v_rwkv_a0, v_rwkv_a2, v_rwkv_k_k, v_rwkv_k_a, v_rwkv_r_k, v_rwkv_ln_w, v_rwkv_ln_b, v_w_out, v_final_g):
    given = dict(locals())

    n_sh = len(_SHARDED)
    everyone = ("x", "y", "c")
    shards = [given[n][0].astype(BF16 if n in _BF16_OPERANDS else F32) for n, _ in _SHARDED]
    shards[0] = shards[0].T
    gathered = _exchange(shards, (True,) * n_sh, everyone, "gather_weights")
    local = {n: _join_shards(n, blocks) for (n, _), blocks in list(zip(_SHARDED, gathered))[1:]}
    local['w_in_t'] = gathered[0].reshape(D_IN, D_MODEL)
    local.update({n: (given[n][0] if len(s) > 1 else given[n]) for n, s in _SMALL})

    loss, grad_x, grads = _local_step(x[0], loss_target[0], local)

    blocks = [_split_shards(n, grads[n], s[1:]).astype(BF16) for n, s in _SHARDED]
    small = _pack_small(grads, loss[0, 0]).reshape(N_DEV, _SMALL_ROWS // N_DEV, LANES)
    recv = _exchange(blocks + [small], (False,) * (n_sh + 1), everyone, "exchange_grads")
    small_sum = _exchange([_sum_blocks(recv[-1])], (True,), everyone, "gather_small_grads")[0]

    result = {}
    for group, name in (([0], "adamw_w_in"), ([1, 2, 3, 4], "adamw_shards")):
        ns = [_SHARDED[i][0] for i in group]
        res = _adamw([recv[i] for i in group], [given[n][0] for n in ns], [given['m_' + n][0] for n in ns],
                     [given['v_' + n][0] for n in ns], True, name)
        for j, n in enumerate(ns):
            result[n] = [res[k * len(ns) + j][None] for k in range(4)]
    g_small, total = _unpack_small(small_sum)
    two_d = lambda t: t.reshape(1, -1) if t.ndim == 1 else t
    ns = [n for n, _ in _SMALL]
    res = _adamw([two_d(g_small[n]) for n in ns], [two_d(given[n]) for n in ns], [two_d(given['m_' + n]) for n in ns],
                 [two_d(given['v_' + n]) for n in ns], False, "adamw_small")
    for j, (n, s) in enumerate(_SMALL):
        result[n] = [g_small[n]] + [res[k * len(ns) + j].reshape(s) for k in range(3)]

    outs = [total, grad_x[None]]
    for k in range(4):
        outs += [result[n][k] for n, _, _ in _WEIGHTS]
    return tuple(outs)
```

```python
import math

import jax
import jax.numpy as jnp
from jax import lax
from jax.experimental import pallas as pl
from jax.experimental.pallas import tpu as pltpu

F32 = jnp.float32
BF16 = jnp.bfloat16
HI = lax.Precision.HIGH

D_MODEL = 1024
D_S5 = 512
D_RWKV = 512
S5_GROUPS = 32
S5_GROUP = 16
S5_STATE = 64
N_STATE = S5_GROUPS * S5_STATE
N_HEADS = 8
HEAD = 64
D_SHIFT = 3 * D_RWKV + 128
D_IN = 2 * D_S5 + D_SHIFT + D_RWKV
NORM_EPS = 1e-6
GN_EPS = 64e-5
N_DEV = 8
LANES = 128
S5_BLOCKS = 4
RWKV_CHUNK = 64
RWKV_CHUNKS_PER_STEP = 4
VMEM_LIMIT = 56 * 1024 * 1024

ADAM_LR = 0.001
ADAM_B1 = 0.9
ADAM_B2 = 0.999
ADAM_EPS = 1e-08
ADAM_WD = 0.01
ADAM_STEP = 10


def _dot(a, b, dims, prec):
    return lax.dot_general(a, b, (dims, ((), ())), precision=prec, preferred_element_type=F32)


def _dot_bf(a, b, dims):
    return _dot(a.astype(BF16), b.astype(BF16), dims, None)


def _make_mm(cast, prec):
    @jax.custom_vjp
    def mm(a, b):
        return _dot(cast(a), cast(b), ((1,), (0,)), prec)

    def fwd(a, b):
        return mm(a, b), (a, b)

    def bwd(res, g):
        a, b = res
        return (_dot(cast(g), cast(b), ((1,), (1,)), prec), _dot(cast(a), cast(g), ((0,), (0,)), prec))

    mm.defvjp(fwd, bwd)
    return mm


mm_bf = _make_mm(lambda t: t.astype(BF16), None)


@jax.custom_vjp
def mm_w(a, w):
    return _dot(a.astype(BF16), w, ((1,), (0,)), None)


def _mm_w_fwd(a, w):
    return mm_w(a, w), w


def _mm_w_bwd(w, g):
    return _dot(g.astype(BF16), w, ((1,), (1,)), None), jnp.zeros_like(w)


mm_w.defvjp(_mm_w_fwd, _mm_w_bwd)


def _make_head_sum(split):
    def product(x, ee):
        hi = x.astype(BF16)
        out = _dot(hi, ee, ((1,), (0,)), None)
        if split:
            out = out + _dot((x - hi.astype(F32)).astype(BF16), ee, ((1,), (0,)), None)
        return out

    @jax.custom_vjp
    def head_sum(x, ee):
        return product(x, ee)

    def fwd(x, ee):
        return product(x, ee), ee

    def bwd(ee, g):
        return product(g, ee), jnp.zeros_like(ee)

    head_sum.defvjp(fwd, bwd)
    return head_sum


head_sum = _make_head_sum(False)
head_sum_split = _make_head_sum(True)


@jax.custom_vjp
def _sigmoid(x):
    return 1.0 / (1.0 + jnp.exp(-x))


def _sigmoid_fwd(x):
    s = _sigmoid(x)
    return s, s


_sigmoid.defvjp(_sigmoid_fwd, lambda s, g: (g * s * (1.0 - s),))


@jax.custom_vjp
def _silu(x):
    return x * _sigmoid(x)


def _silu_fwd(x):
    s = _sigmoid(x)
    return x * s, (x, s)


def _silu_bwd(res, g):
    x, s = res
    return (g * s * (1.0 + x * (1.0 - s)),)


_silu.defvjp(_silu_fwd, _silu_bwd)


@jax.custom_vjp
def _softplus(x):
    return jnp.maximum(x, 0.0) + jnp.log(1.0 + jnp.exp(-jnp.abs(x)))


def _softplus_fwd(x):
    e = jnp.exp(-jnp.abs(x))
    return jnp.maximum(x, 0.0) + jnp.log(1.0 + e), (x, e)


def _softplus_bwd(res, g):
    x, e = res
    return (g * jnp.where(x >= 0.0, 1.0, e) / (1.0 + e),)


_softplus.defvjp(_softplus_fwd, _softplus_bwd)


@jax.custom_vjp
def _normalize_heads(x, ee):
    return x / jnp.maximum(jnp.sqrt(head_sum(x * x, ee)), 1e-12)


def _normalize_heads_fwd(x, ee):
    norm = jnp.sqrt(head_sum(x * x, ee))
    inv = 1.0 / jnp.maximum(norm, 1e-12)
    y = x * inv
    return y, (y, inv, norm, ee)


def _normalize_heads_bwd(res, g):
    y, inv, norm, ee = res
    along = jnp.where(norm > 1e-12, head_sum(g * y, ee), 0.0)
    return inv * (g - y * along), jnp.zeros_like(ee)


_normalize_heads.defvjp(_normalize_heads_fwd, _normalize_heads_bwd)


_GELU_C = 2.0 * math.sqrt(2.0 / math.pi)


def _gelu_gate(x):
    return 1.0 / (1.0 + jnp.exp(-_GELU_C * x * (1.0 + 0.044715 * (x * x))))


@jax.custom_vjp
def _gelu(x):
    return x * _gelu_gate(x)


def _gelu_fwd(x):
    s = _gelu_gate(x)
    return x * s, (x, s)


def _gelu_bwd(res, g):
    x, s = res
    return (g * (s + x * s * (1.0 - s) * (_GELU_C * (1.0 + 3.0 * 0.044715 * (x * x)))),)


_gelu.defvjp(_gelu_fwd, _gelu_bwd)


def _rms(x, g):
    return x * lax.rsqrt(jnp.mean(x * x, axis=-1, keepdims=True) + NORM_EPS) * g


def _const_spec(shape):
    nd = len(shape)
    return pl.BlockSpec(shape, lambda *_: (0,) * nd, pipeline_mode=pl.Buffered(1))


def _acc_spec(shape):
    nd = len(shape)
    return pl.BlockSpec(shape, lambda *_: (0,) * nd)


def _params(sem):
    return pltpu.CompilerParams(dimension_semantics=(sem,), vmem_limit_bytes=VMEM_LIMIT)


_ANY = pl.BlockSpec(memory_space=pl.ANY)


def _sds(shape):
    return jax.ShapeDtypeStruct(shape, F32)


def _head_sum_matrix():
    i = jnp.arange(D_RWKV) // HEAD
    return (i[:, None] == i[None, :]).astype(BF16)


def _s5_param_fn(lam_re, lam_im, logdt, b_re, b_im):
    dt = jnp.exp(logdt)
    mag = jnp.exp(lam_re * dt)
    ang = lam_im * dt
    lbr = mag * jnp.cos(ang)
    lbi = mag * jnp.sin(ang)
    nr = lbr - 1.0
    den = lam_re * lam_re + lam_im * lam_im
    cr = (nr * lam_re + lbi * lam_im) / den
    ci = (lbi * lam_re - nr * lam_im) / den
    return lbr, lbi, cr * b_re - ci * b_im, cr * b_im + ci * b_re


def _cmul(ar, ai, br, bi):
    return ar * br - ai * bi, ar * bi + ai * br


def _s5_param_fwd(lam_re, lam_im, logdt, b_re, b_im):
    def body(lr, li, ld, br, bi, o_br, o_bi, o_pr, o_pi, o_qr, o_qi):
        lbr, lbi, bbr, bbi = _s5_param_fn(lr[...], li[...], ld[...], br[...], bi[...])
        o_br[...] = bbr
        o_bi[...] = bbi
        rid = lax.broadcasted_iota(jnp.int32, (8, N_STATE), 0)
        pr, pi_ = lbr, lbi
        fwd_r = rev_r = jnp.broadcast_to(pr, (8, N_STATE))
        fwd_i = rev_i = jnp.broadcast_to(pi_, (8, N_STATE))
        for j in range(1, 8):
            pr, pi_ = _cmul(pr, pi_, lbr, lbi)
            fwd_r = jnp.where(rid == j, jnp.broadcast_to(pr, (8, N_STATE)), fwd_r)
            fwd_i = jnp.where(rid == j, jnp.broadcast_to(pi_, (8, N_STATE)), fwd_i)
            rev_r = jnp.where(rid == 7 - j, jnp.broadcast_to(pr, (8, N_STATE)), rev_r)
            rev_i = jnp.where(rid == 7 - j, jnp.broadcast_to(pi_, (8, N_STATE)), rev_i)
        o_pr[...] = fwd_r
        o_pi[...] = fwd_i
        o_qr[...] = rev_r
        o_qi[...] = -rev_i

    return pl.pallas_call(
        body, name="s5_param_fwd",
        out_shape=[_sds((S5_GROUP, N_STATE))] * 2 + [_sds((8, N_STATE))] * 4,
    )(lam_re, lam_im, logdt, b_re, b_im)


def _s5_param_bwd(lam_re, lam_im, logdt, b_re, b_im, d_lbr, d_lbi, d_bbr, d_bbi, group_ind):
    def body(lr, li, ld, br, bi, g0, g1, g2, g3, ind, o_lr, o_li, o_ld, o_br, o_bi):
        _, vjp = jax.vjp(_s5_param_fn, lr[...], li[...], ld[...], br[...], bi[...])
        d_lr, d_li, d_ld, d_br, d_bi = vjp((g0[...], g1[...], g2[...], g3[...]))
        o_lr[...] = d_lr
        o_li[...] = d_li
        o_ld[...] = _dot(jnp.broadcast_to(d_ld, (8, N_STATE)), ind[...], ((1,), (0,)), HI)
        o_br[...] = d_br
        o_bi[...] = d_bi

    return pl.pallas_call(
        body, name="s5_param_bwd",
        out_shape=[_sds((1, N_STATE))] * 2 + [_sds((8, LANES))] + [_sds((S5_GROUP, N_STATE))] * 2,
    )(lam_re, lam_im, logdt, b_re, b_im, d_lbr, d_lbi, d_bbr, d_bbi, group_ind)


def _fwd_in(x, norm_g, w_in_bf, tt):
    L = x.shape[0]

    def body(x_ref, g_ref, w_ref, u_ref, zs_ref, rw_ref, zr_ref):
        h = _rms(x_ref[...], g_ref[...])
        proj = _dot(h.astype(BF16), w_ref[...], ((1,), (1,)), None)
        u_ref[...] = proj[:, 0:D_S5]
        zs_ref[...] = proj[:, D_S5:2 * D_S5]
        rw_ref[...] = proj[:, 2 * D_S5:2 * D_S5 + D_SHIFT]
        zr_ref[...] = proj[:, 2 * D_S5 + D_SHIFT:D_IN]

    row = lambda n: pl.BlockSpec((tt, n), lambda i: (i, 0))
    return pl.pallas_call(
        body, name="fwd_in", grid=(L // tt,),
        in_specs=[row(D_MODEL), _const_spec((1, D_MODEL)), _const_spec((D_IN, D_MODEL))],
        out_specs=[row(D_S5), row(D_S5), row(D_SHIFT), row(D_RWKV)],
        out_shape=[_sds((L, D_S5)), _sds((L, D_S5)), _sds((L, D_SHIFT)), _sds((L, D_RWKV))],
        compiler_params=_params("parallel"),
    )(x, norm_g, w_in_bf)


S5_LANE_CHUNK = 512


def _tile_scan(re_ref, im_ref, pow_r_ref, pow_i_ref, carry_r_ref, carry_i_ref, reverse):
    t, n = re_ref.shape
    n_groups = t // 8
    ch = S5_LANE_CHUNK
    rid = lax.broadcasted_iota(jnp.int32, (8, ch), 0)
    for c in range(n // ch):
        cols = slice(c * ch, (c + 1) * ch)
        pow_r = pow_r_ref[:, cols]
        pow_i = pow_i_ref[:, cols]
        row = lambda tile, j: jnp.broadcast_to(tile[j:j + 1], (8, ch))
        levels = []
        for d in (1, 2, 4):
            keep = (rid < 8 - d) if reverse else (rid >= d)
            j = 8 - d if reverse else d - 1
            levels.append(((8 - d) if reverse else d,
                           jnp.where(keep, row(pow_r, j), 0.0), jnp.where(keep, row(pow_i, j), 0.0)))

        def group(g, carry):
            r0 = pl.multiple_of(((n_groups - 1 - g) if reverse else g) * 8, 8)
            xr = re_ref[pl.ds(r0, 8), cols]
            xi = im_ref[pl.ds(r0, 8), cols]
            for shift, lr, li in levels:
                mr, mi = _cmul(lr, li, pltpu.roll(xr, shift, axis=0), pltpu.roll(xi, shift, axis=0))
                xr = xr + mr
                xi = xi + mi
            mr, mi = _cmul(pow_r, pow_i, carry[0], carry[1])
            xr = xr + mr
            xi = xi + mi
            re_ref[pl.ds(r0, 8), cols] = xr
            im_ref[pl.ds(r0, 8), cols] = xi
            last = 0 if reverse else 7
            return row(xr, last), row(xi, last)

        out = lax.fori_loop(0, n_groups, group, (carry_r_ref[:, cols], carry_i_ref[:, cols]))
        carry_r_ref[:, cols] = out[0]
        carry_i_ref[:, cols] = out[1]


def _s5_fwd(u, b4_re, b4_im, c4_re, c4_im, pow_r, pow_i, tt):
    L = u.shape[0]

    def body(u_ref, bre_ref, bim_ref, cre_ref, cim_ref, pr_ref, pi_ref, sre_o, sim_o, y_o, car_r, car_i):
        @pl.when(pl.program_id(0) == 0)
        def _():
            car_r[...] = jnp.zeros_like(car_r)
            car_i[...] = jnp.zeros_like(car_i)

        uv = u_ref[...]
        for q in range(S5_BLOCKS):
            uq = uv[:, q * LANES:(q + 1) * LANES]
            cols = slice(q * 512, (q + 1) * 512)
            sre_o[:, cols] = _dot_bf(uq, bre_ref[q], ((1,), (0,)))
            sim_o[:, cols] = _dot_bf(uq, bim_ref[q], ((1,), (0,)))
        _tile_scan(sre_o, sim_o, pr_ref, pi_ref, car_r, car_i, reverse=False)
        for q in range(S5_BLOCKS):
            cols = slice(q * 512, (q + 1) * 512)
            y_o[:, q * LANES:(q + 1) * LANES] = (_dot_bf(sre_o[:, cols], cre_ref[q], ((1,), (0,)))
                                                 - _dot_bf(sim_o[:, cols], cim_ref[q], ((1,), (0,))))

    row = lambda n: pl.BlockSpec((tt, n), lambda i: (i, 0))
    return pl.pallas_call(
        body, name="s5_fwd", grid=(L // tt,),
        in_specs=[row(D_S5)] + [_const_spec((S5_BLOCKS, LANES, 512))] * 2 + [_const_spec((S5_BLOCKS, 512, LANES))] * 2
        + [_const_spec((8, N_STATE))] * 2,
        out_specs=[row(N_STATE), row(N_STATE), row(D_S5)],
        out_shape=[_sds((L, N_STATE)), _sds((L, N_STATE)), _sds((L, D_S5))],
        scratch_shapes=[pltpu.VMEM((8, N_STATE), F32)] * 2,
        compiler_params=_params("arbitrary"),
    )(u, b4_re, b4_im, c4_re, c4_im, pow_r, pow_i)


def _rwkv_pre_fn(r, k, v, wa, w0, w2p, a0, a2p, k_k, k_a, ee):
    w = -_softplus(-(w0 + mm_bf(jnp.tanh(wa), w2p))) - 0.5
    logw = -jnp.exp(w)
    a = _sigmoid(a0 + mm_bf(wa, a2p))
    kk = _normalize_heads(k * k_k, ee)
    k2 = k * (1.0 + (a - 1.0) * k_a)
    return r, logw, k2, v, -kk, kk * a


N_PAIRS = N_HEADS // 2


def _head_spec(tt):
    return pl.BlockSpec((N_PAIRS, tt, LANES), lambda i: (0, i, 0))


def _load_heads(ref):
    return jnp.concatenate([ref[p] for p in range(N_PAIRS)], axis=-1)


def _store_heads(ref, val):
    for p in range(N_PAIRS):
        ref[p] = val[:, p * LANES:(p + 1) * LANES]


def _split_pairs(x):
    return jnp.concatenate([x[:, :, :HEAD], x[:, :, HEAD:]], axis=0)


def _join_pairs(x):
    return jnp.concatenate([x[:N_PAIRS], x[N_PAIRS:]], axis=-1)


def _shifted(rw, prev_blk, first):
    rolled = pltpu.roll(rw, 1, axis=0)
    prev_row = jnp.where(first, 0.0, prev_blk[7:8, :])
    rid = lax.broadcasted_iota(jnp.int32, rw.shape, 0)
    return jnp.where(rid == 0, jnp.broadcast_to(prev_row, rw.shape), rolled)


def _split_rw(t):
    return t[:, 0:512], t[:, 512:1024], t[:, 1024:1536], t[:, 1536:1664]


def _rwkv_pre_specs(tt):
    row = pl.BlockSpec((tt, D_SHIFT), lambda i: (i, 0))
    prev = pl.BlockSpec((8, D_SHIFT), lambda i: (jnp.maximum(i * (tt // 8) - 1, 0), 0))
    consts = [_const_spec((1, D_SHIFT)), _const_spec((1, D_RWKV)), _const_spec((LANES, D_RWKV)),
              _const_spec((1, D_RWKV)), _const_spec((LANES, D_RWKV)), _const_spec((1, D_RWKV)),
              _const_spec((1, D_RWKV)), _const_spec((D_RWKV, D_RWKV))]
    return [row, prev] + consts


def _rwkv_pre_fwd(rw, mu, w0, w2p, a0, a2p, k_k, k_a, ee, tt):
    L = rw.shape[0]

    def body(rw_ref, prev_ref, mu_ref, w0_ref, w2_ref, a0_ref, a2_ref, kk_ref, ka_ref, ee_ref, *outs):
        rwv = rw_ref[...]
        rws = rwv + (_shifted(rwv, prev_ref[...], pl.program_id(0) == 0) - rwv) * mu_ref[...]
        res = _rwkv_pre_fn(*_split_rw(rws), w0_ref[...], w2_ref[...], a0_ref[...], a2_ref[...],
                           kk_ref[...], ka_ref[...], ee_ref[...])
        for o, val in zip(outs, res):
            _store_heads(o, val)

    return pl.pallas_call(
        body, name="rwkv_pre_fwd", grid=(L // tt,),
        in_specs=_rwkv_pre_specs(tt), out_specs=[_head_spec(tt)] * 6, out_shape=[_sds((N_PAIRS, L, LANES))] * 6,
        compiler_params=_params("parallel"),
    )(rw, rw, mu, w0, w2p, a0, a2p, k_k, k_a, ee)


def _rwkv_pre_bwd(rw, mu, w0, w2p, a0, a2p, k_k, k_a, ee, cots, tt):
    L = rw.shape[0]
    n_t = L // tt

    def body(rw_ref, prev_ref, mu_ref, w0_ref, w2_ref, a0_ref, a2_ref, kk_ref, ka_ref, ee_ref,
             c_r, c_w, c_k, c_v, c_a, c_b, cb_r, cb_k, cb_v,
             drws_ref, dmu_o, dw0_o, dw2_o, da0_o, da2_o, dkk_o, dka_o,
             dmu, dw0, dw2, da0, da2, dkk, dka):
        i = pl.program_id(0)
        accs = (dmu, dw0, dw2, da0, da2, dkk, dka)

        @pl.when(i == 0)
        def _():
            for acc in accs:
                acc[...] = jnp.zeros_like(acc)

        rwv = rw_ref[...]
        diff = _shifted(rwv, prev_ref[...], i == 0) - rwv
        rws = rwv + diff * mu_ref[...]
        consts = (w0_ref[...], w2_ref[...], a0_ref[...], a2_ref[...], kk_ref[...], ka_ref[...])
        _, vjp = jax.vjp(lambda *a: _rwkv_pre_fn(*a, ee_ref[...]), *_split_rw(rws), *consts)
        scan = [_load_heads(c) for c in (c_r, c_w, c_k, c_v, c_a, c_b)]
        g = vjp((scan[0] + cb_r[...], scan[1], scan[2] + cb_k[...], scan[3] + cb_v[...], scan[4], scan[5]))
        drws = jnp.concatenate(g[0:4], axis=-1)
        drws_ref[...] = drws
        dmu[...] += jnp.sum(drws * diff, axis=0, keepdims=True)
        for acc, val in zip(accs[1:], g[4:]):
            acc[...] += val

        @pl.when(i == n_t - 1)
        def _():
            for acc, out in zip(accs, (dmu_o, dw0_o, dw2_o, da0_o, da2_o, dkk_o, dka_o)):
                out[...] = acc[...]

    row = pl.BlockSpec((tt, D_RWKV), lambda i: (i, 0))
    shapes = [(1, D_SHIFT), (1, D_RWKV), (LANES, D_RWKV), (1, D_RWKV), (LANES, D_RWKV), (1, D_RWKV), (1, D_RWKV)]
    return pl.pallas_call(
        body, name="rwkv_pre_bwd", grid=(n_t,),
        in_specs=_rwkv_pre_specs(tt) + [_head_spec(tt)] * 6 + [row] * 3,
        out_specs=[pl.BlockSpec((tt, D_SHIFT), lambda i: (i, 0))] + [_acc_spec(s) for s in shapes],
        out_shape=[_sds((L, D_SHIFT))] + [_sds(s) for s in shapes],
        scratch_shapes=[pltpu.VMEM(s, F32) for s in shapes],
        compiler_params=_params("arbitrary"),
    )(rw, rw, mu, w0, w2p, a0, a2p, k_k, k_a, ee, *cots)


def _bmm(a, b):
    return lax.dot_general(a, b, (((2,), (1,)), ((0,), (0,))), precision=HI, preferred_element_type=F32)


def _bmm_nt(a, b):
    return lax.dot_general(a, b, (((2,), (2,)), ((0,), (0,))), precision=HI, preferred_element_type=F32)


def _bmm_tn(a, b):
    return lax.dot_general(a, b, (((1,), (1,)), ((0,), (0,))), precision=HI, preferred_element_type=F32)


def _bdot_bf(a, b, lhs_dim, rhs_dim):
    return lax.dot_general(a.astype(BF16), b.astype(BF16), (((lhs_dim,), (rhs_dim,)), ((0,), (0,))),
                           preferred_element_type=F32)


@jax.custom_vjp
def _bmm_bf(a, b):
    return _bdot_bf(a, b, 2, 1)


def _bmm_bf_fwd(a, b):
    return _bmm_bf(a, b), (a, b)


def _bmm_bf_bwd(res, g):
    a, b = res
    return _bdot_bf(g, b, 2, 2), _bdot_bf(a, g, 1, 1)


_bmm_bf.defvjp(_bmm_bf_fwd, _bmm_bf_bwd)


@jax.custom_vjp
def _bmm_tn_bf(a, b):
    return _bdot_bf(a, b, 1, 1)


def _bmm_tn_bf_fwd(a, b):
    return _bmm_tn_bf(a, b), (a, b)


def _bmm_tn_bf_bwd(res, g):
    a, b = res
    return _bdot_bf(b, g, 2, 2), _bdot_bf(a, g, 2, 1)


_bmm_tn_bf.defvjp(_bmm_tn_bf_fwd, _bmm_tn_bf_bwd)


def _unit_lower_inverse(a):
    t = a.shape[-1]
    ti = lax.broadcasted_iota(jnp.int32, (t, t), 0)
    si = lax.broadcasted_iota(jnp.int32, (t, t), 1)

    def same_block(bits):
        shift = jnp.int32(bits)
        return (lax.shift_right_logical(ti, shift) == lax.shift_right_logical(si, shift))[None]

    def mm(x, y):
        return _bdot_bf(x, y, 2, 1)

    d = jnp.where(same_block(3), a, 0.0)
    inv = jnp.where(ti == si, 1.0, 0.0)[None] + d
    pw = mm(d, d)
    both = mm(jnp.concatenate([inv, pw], axis=1), pw)
    inv = inv + both[:, :t]
    inv = inv + mm(inv, both[:, t:])
    bits = 3
    while (1 << bits) < t:
        e = jnp.where(same_block(bits), 0.0, jnp.where(same_block(bits + 1), a, 0.0))
        inv = inv + mm(mm(inv, e), inv)
        bits += 1
    return inv


def _tri_mask(t):
    ri = lax.broadcasted_iota(jnp.int32, (2 * t, 2 * t), 0)
    ci = lax.broadcasted_iota(jnp.int32, (2 * t, 2 * t), 1)
    top_rows = ri < t
    diff = jnp.where(top_rows, ri, ri - t) - jnp.where(ci < t, ci, ci - t)
    return (diff >= jnp.where(top_rows, 1, 0))[None]


def _ones_tri(n_h, t):
    ti = lax.broadcasted_iota(jnp.int32, (t, t), 0)
    si = lax.broadcasted_iota(jnp.int32, (t, t), 1)
    return jnp.broadcast_to(jnp.where(ti >= si, 1.0, 0.0)[None], (n_h, t, t))


@jax.custom_vjp
def _running_sum_kept(logw, kept):
    return kept


def _running_sum_kept_bwd(shape, g):
    return _bmm_tn(_ones_tri(shape[0], shape[1]), g), jnp.zeros_like(g)


_running_sum_kept.defvjp(lambda logw, kept: (kept, logw.shape), _running_sum_kept_bwd)


@jax.custom_vjp
def _tri_products_kept(ar, bk, kept):
    return kept


def _tri_products_kept_bwd(res, g):
    ar, bk = res
    g = jnp.where(_tri_mask(ar.shape[1] // 2), g, 0.0)
    return _bmm(g, bk), _bmm_tn(g, ar), jnp.zeros_like(g)


_tri_products_kept.defvjp(lambda ar, bk, kept: (kept, (ar, bk)), _tri_products_kept_bwd)


@jax.custom_vjp
def _solve_unit_lower(a, rhs, inv, kept=None):
    return _bmm(inv, rhs) if kept is None else kept


def _solve_fwd(a, rhs, inv, kept=None):
    u = _bmm(inv, rhs) if kept is None else kept
    return u, (inv, u, kept is not None)


def _solve_bwd(res, du):
    inv, u, had_kept = res
    d_rhs = _bmm_tn(inv, du)
    return _bmm_nt(d_rhs, u), d_rhs, jnp.zeros_like(inv), (jnp.zeros_like(u) if had_kept else None)


_solve_unit_lower.defvjp(_solve_fwd, _solve_bwd)


def _rwkv_chunk(st0, r, logw, k, v, a, b, kept=None):
    n_h, t, _ = r.shape
    log_p = _bmm(_ones_tri(n_h, t), logw) if kept is None else _running_sum_kept(logw, kept[0])
    p_in = jnp.exp(log_p)
    p_inv = jnp.exp(-log_p)
    at = a * jnp.exp(log_p - logw)
    rt = r * p_in
    ar = jnp.concatenate([at, rt], axis=1)
    bk = jnp.concatenate([b * p_inv, k * p_inv], axis=1)
    if kept is None:
        m = jnp.where(_tri_mask(t), _bmm_nt(ar, bk), 0.0)
        inv = _unit_lower_inverse(m[:, :t, :t])
    else:
        m = _tri_products_kept(ar, bk, kept[1])
        inv = kept[2]
    top, bottom = m[:, :t], m[:, t:]
    rhs = _bmm_bf(jnp.concatenate([at, top[:, :, t:]], axis=2), jnp.concatenate([st0, v], axis=1))
    u = _solve_unit_lower(top[:, :, :t], rhs, inv, None if kept is None else kept[3])
    y = _bmm_bf(jnp.concatenate([rt, bottom], axis=2), jnp.concatenate([st0, u, v], axis=1))
    p_end = jnp.swapaxes(p_in[:, t - 1:t, :], 1, 2)
    st1 = (st0 + _bmm_tn_bf(bk, jnp.concatenate([u, v], axis=1))) * p_end
    return y, st1, (log_p, m, inv, u)


def _rwkv_scan_fwd(ops):
    n_h, L, n = N_HEADS, ops[0].shape[1], HEAD
    t = RWKV_CHUNK
    per = min(RWKV_CHUNKS_PER_STEP, L // t)
    n_c = L // t
    n_s = n_c // per

    def body(r_ref, w_ref, k_ref, v_ref, a_ref, b_ref, y_ref, st_ref, logp_ref, m_ref, inv_ref, u_ref, st):
        @pl.when(pl.program_id(0) == 0)
        def _():
            st[...] = jnp.zeros_like(st)

        st0 = st[...]
        for j in range(per):
            rows = slice(j * t, (j + 1) * t)
            st_ref[j] = st0
            y, st0, (log_p, m, inv, u) = _rwkv_chunk(
                st0, *(_split_pairs(ref[:, rows, :]) for ref in (r_ref, w_ref, k_ref, v_ref, a_ref, b_ref)))
            y_ref[:, rows, :] = _join_pairs(y)
            logp_ref[:, rows, :] = log_p
            u_ref[:, rows, :] = u
            m_ref[j] = m
            inv_ref[j] = inv
        st[...] = st0

    pairs = pl.BlockSpec((N_PAIRS, per * t, LANES), lambda c: (0, c, 0))
    blk = pl.BlockSpec((n_h, per * t, n), lambda c: (0, c, 0))
    per_chunk = lambda m: pl.BlockSpec((per, n_h, m, m), lambda c: (c, 0, 0, 0))
    return pl.pallas_call(
        body, name="rwkv_scan_fwd", grid=(n_s,), in_specs=[pairs] * 6,
        out_specs=[pairs, per_chunk(n), blk, per_chunk(2 * t), per_chunk(t), blk],
        out_shape=[_sds((N_PAIRS, L, LANES)), _sds((n_c, n_h, n, n)), _sds((n_h, L, n)),
                   _sds((n_c, n_h, 2 * t, 2 * t)), _sds((n_c, n_h, t, t)), _sds((n_h, L, n))],
        scratch_shapes=[pltpu.VMEM((n_h, n, n), F32)],
        compiler_params=_params("arbitrary"),
    )(*ops)


def _rwkv_scan_bwd(ops, states, kept, dy):
    n_h, L, n = N_HEADS, ops[0].shape[1], HEAD
    t = RWKV_CHUNK
    per = min(RWKV_CHUNKS_PER_STEP, L // t)
    n_s = L // t // per

    def body(r_ref, w_ref, k_ref, v_ref, a_ref, b_ref, st_ref, logp_ref, m_ref, inv_ref, u_ref, dy_ref,
             dr, dw, dk, dv, da, db, dst):
        @pl.when(pl.program_id(0) == 0)
        def _():
            dst[...] = jnp.zeros_like(dst)

        vjps = []
        for j in range(per):
            rows = slice(j * t, (j + 1) * t)
            have = (logp_ref[:, rows, :], m_ref[j], inv_ref[j], u_ref[:, rows, :])
            args = [_split_pairs(ref[:, rows, :]) for ref in (r_ref, w_ref, k_ref, v_ref, a_ref, b_ref)]
            vjps.append(jax.vjp(lambda *a, have=have: _rwkv_chunk(*a, kept=have)[:2], st_ref[j], *args)[1])
        d_state = dst[...]
        for j in reversed(range(per)):
            rows = slice(j * t, (j + 1) * t)
            g = vjps[j]((_split_pairs(dy_ref[:, rows, :]), d_state))
            d_state = g[0]
            for out, val in zip((dr, dw, dk, dv, da, db), g[1:]):
                out[:, rows, :] = _join_pairs(val)
        dst[...] = d_state

    pairs = pl.BlockSpec((N_PAIRS, per * t, LANES), lambda c: (0, n_s - 1 - c, 0))
    blk = pl.BlockSpec((n_h, per * t, n), lambda c: (0, n_s - 1 - c, 0))
    per_chunk = lambda m: pl.BlockSpec((per, n_h, m, m), lambda c: (n_s - 1 - c, 0, 0, 0))
    return pl.pallas_call(
        body, name="rwkv_scan_bwd", grid=(n_s,),
        in_specs=[pairs] * 6 + [per_chunk(n), blk, per_chunk(2 * t), per_chunk(t), blk, pairs],
        out_specs=[pairs] * 6, out_shape=[_sds((N_PAIRS, L, LANES))] * 6,
        scratch_shapes=[pltpu.VMEM((n_h, n, n), F32)],
        compiler_params=_params("arbitrary"),
    )(*ops, states, *kept, dy)


def _post_fn(x, u, zs, zr, ysc, r, k2, v, y_ssm, gate_in, d, glu_b, ln_w, ln_b, r_k, gf,
              glu_w, wo_s5, wo_rwkv, tgt, ee):
    y3 = _gelu(y_ssm + d * u)
    y_s5 = y3 * _sigmoid(mm_w(y3, glu_w) + glu_b + gate_in) * _silu(zs)
    mean = head_sum_split(ysc, ee) * (1.0 / HEAD)
    yc = ysc - mean
    var = head_sum(yc * yc, ee) * (1.0 / HEAD)
    gn = yc * lax.rsqrt(var + GN_EPS) * ln_w + ln_b
    bonus = head_sum(r * k2 * r_k, ee) * v
    y_rwkv = (gn + bonus) * _silu(zr)
    x2 = x + mm_w(y_s5, wo_s5) + mm_w(y_rwkv, wo_rwkv)
    err = _rms(x2, gf) - tgt
    return 0.5 * jnp.mean(err * err, axis=-1, keepdims=True), (y3, y_s5, y_rwkv)


def _post(x, u, zs, zr, ysc, r, k2, v, y_ssm, d, glu_w, glu_b, ln_w, ln_b, r_k, w_out, gf, tgt, ee, tt):
    L = x.shape[0]
    n_t = L // tt
    acc_shapes = [(1, D_S5), (D_S5, D_S5), (1, D_S5), (1, D_RWKV), (1, D_RWKV), (1, D_RWKV),
                  (D_MODEL, D_MODEL), (1, D_MODEL), (8, LANES)]

    def body(x_ref, u_ref, zs_ref, zr_ref, ysc_ref, r_ref, k2_ref, v_ref, yssm_ref,
             d_ref, gw_ref, gb_ref, lw_ref, lb_ref, rk_ref, wo_ref, gf_ref, tgt_ref, ee_ref,
             dx_o, du_o, dzs_o, dzr_o, dysc_o, dr_o, dk2_o, dv_o, dyssm_o,
             dd_o, dgw_o, dgb_o, dlw_o, dlb_o, drk_o, dwo_o, dgf_o, loss_o,
             dd, dgw, dgb, dlw, dlb, drk, dwo, dgf, loss):
        i = pl.program_id(0)
        accs = (dd, dgw, dgb, dlw, dlb, drk, dwo, dgf, loss)

        @pl.when(i == 0)
        def _():
            for acc in accs:
                acc[...] = jnp.zeros_like(acc)

        args = (x_ref[...], u_ref[...], zs_ref[...], zr_ref[...],
                _load_heads(ysc_ref), _load_heads(r_ref), _load_heads(k2_ref), _load_heads(v_ref), yssm_ref[...],
                jnp.zeros((tt, D_S5), F32), d_ref[...], gb_ref[...], lw_ref[...], lb_ref[...], rk_ref[...], gf_ref[...])
        rows, vjp, (y3, y_s5, y_rwkv) = jax.vjp(
            lambda *a: _post_fn(*a, gw_ref[...], wo_ref[0:D_S5, :], wo_ref[D_S5:D_MODEL, :], tgt_ref[...],
                                ee_ref[...]), *args, has_aux=True)
        g = vjp(jnp.ones_like(rows))
        for out, val in zip((dx_o, du_o, dzs_o, dzr_o), g[0:4]):
            out[...] = val
        _store_heads(dysc_o, g[4])
        for out, val in zip((dr_o, dk2_o, dv_o, dyssm_o), g[5:9]):
            out[...] = val
        for acc, val in zip((dd, dgb, dlw, dlb, drk, dgf), g[10:16]):
            acc[...] += val
        dgw[...] += _dot_bf(y3, g[9], ((0,), (0,)))
        dwo[0:D_S5, :] += _dot_bf(y_s5, g[0], ((0,), (0,)))
        dwo[D_S5:D_MODEL, :] += _dot_bf(y_rwkv, g[0], ((0,), (0,)))
        loss[...] += jnp.broadcast_to(jnp.sum(rows, axis=0, keepdims=True), loss.shape)

        @pl.when(i == n_t - 1)
        def _():
            for acc, out in zip(accs, (dd_o, dgw_o, dgb_o, dlw_o, dlb_o, drk_o, dwo_o, dgf_o, loss_o)):
                pltpu.sync_copy(acc, out)

    row = lambda n: pl.BlockSpec((tt, n), lambda i: (i, 0))
    in_specs = ([row(D_MODEL)] + [row(512)] * 3 + [_head_spec(tt)] * 4 + [row(D_S5)]
                + [_const_spec(s) for s in [(1, D_S5), (D_S5, D_S5), (1, D_S5), (1, D_RWKV), (1, D_RWKV), (1, D_RWKV),
                                            (D_MODEL, D_MODEL), (1, D_MODEL)]]
                + [row(D_MODEL), _const_spec((D_RWKV, D_RWKV))])
    out_rows = [D_MODEL] + [512] * 3 + [None] + [512] * 4
    return pl.pallas_call(
        body, name="post_fwd_bwd", grid=(n_t,), in_specs=in_specs,
        out_specs=[row(n) if n else _head_spec(tt) for n in out_rows] + [_ANY] * len(acc_shapes),
        out_shape=([_sds((L, n)) if n else _sds((N_PAIRS, L, LANES)) for n in out_rows]
                   + [_sds(s) for s in acc_shapes]),
        scratch_shapes=[pltpu.VMEM(s, F32) for s in acc_shapes],
        compiler_params=_params("arbitrary"),
    )(x, u, zs, zr, ysc, r, k2, v, y_ssm, d, glu_w, glu_b, ln_w, ln_b, r_k, w_out, gf, tgt, ee)


def _s5_bwd(u, du_direct, dy, s_re, s_im, b4_re, b4_im, c4_re, c4_im, pow_r, pow_i, tt):
    L = u.shape[0]
    n_t = L // tt
    acc_shapes = ([(S5_BLOCKS, LANES, 512)] * 2 + [(S5_BLOCKS, 512, LANES)] * 2 + [(1, N_STATE)] * 2)

    def body(u_ref, dud_ref, dy_ref, sre_ref, sim_ref, pre_ref, pim_ref, bre_ref, bim_ref, cre_ref, cim_ref,
             pr_ref, pi_ref, du_o, dbre_o, dbim_o, dcre_o, dcim_o, dlr_o, dli_o,
             dbre, dbim, dcre, dcim, dlr, dli, gre, gim, car_r, car_i):
        i = pl.program_id(0)

        @pl.when(i == 0)
        def _():
            for acc in (dbre, dbim, dcre, dcim, dlr, dli, car_r, car_i):
                acc[...] = jnp.zeros_like(acc)

        uv = u_ref[...]
        dyv = dy_ref[...]
        blocks = [slice(q * 512, (q + 1) * 512) for q in range(S5_BLOCKS)]
        lanes = [slice(q * LANES, (q + 1) * LANES) for q in range(S5_BLOCKS)]
        for q in range(S5_BLOCKS):
            gre[:, blocks[q]] = _dot_bf(dyv[:, lanes[q]], cre_ref[q], ((1,), (1,)))
            gim[:, blocks[q]] = -_dot_bf(dyv[:, lanes[q]], cim_ref[q], ((1,), (1,)))
        _tile_scan(gre, gim, pr_ref, pi_ref, car_r, car_i, reverse=True)
        for q in range(S5_BLOCKS):
            gr = gre[:, blocks[q]]
            gi = gim[:, blocks[q]]
            sr = sre_ref[:, blocks[q]]
            si = sim_ref[:, blocks[q]]
            du_o[:, lanes[q]] = (dud_ref[:, lanes[q]] + _dot_bf(gr, bre_ref[q], ((1,), (1,)))
                                 + _dot_bf(gi, bim_ref[q], ((1,), (1,))))
            dbre[q] += _dot_bf(uv[:, lanes[q]], gr, ((0,), (0,)))
            dbim[q] += _dot_bf(uv[:, lanes[q]], gi, ((0,), (0,)))
            dcre[q] += _dot_bf(sr, dyv[:, lanes[q]], ((0,), (0,)))
            dcim[q] -= _dot_bf(si, dyv[:, lanes[q]], ((0,), (0,)))
            rid = lax.broadcasted_iota(jnp.int32, sr.shape, 0)
            first = i == n_t - 1
            prev_r = jnp.where(first, 0.0, pre_ref[7:8, blocks[q]])
            prev_i = jnp.where(first, 0.0, pim_ref[7:8, blocks[q]])
            pr = jnp.where(rid == 0, jnp.broadcast_to(prev_r, sr.shape), pltpu.roll(sr, 1, axis=0))
            pi_ = jnp.where(rid == 0, jnp.broadcast_to(prev_i, si.shape), pltpu.roll(si, 1, axis=0))
            dlr[:, blocks[q]] += jnp.sum(pr * gr + pi_ * gi, axis=0, keepdims=True)
            dli[:, blocks[q]] += jnp.sum(pr * gi - pi_ * gr, axis=0, keepdims=True)

        @pl.when(i == n_t - 1)
        def _():
            for acc, out in zip((dbre, dbim, dcre, dcim, dlr, dli), (dbre_o, dbim_o, dcre_o, dcim_o, dlr_o, dli_o)):
                out[...] = acc[...]

    row = lambda n: pl.BlockSpec((tt, n), lambda i: (n_t - 1 - i, 0))
    prev = pl.BlockSpec((8, N_STATE), lambda i: (jnp.maximum((n_t - 1 - i) * (tt // 8) - 1, 0), 0))
    return pl.pallas_call(
        body, name="s5_bwd", grid=(n_t,),
        in_specs=[row(D_S5)] * 3 + [row(N_STATE)] * 2 + [prev] * 2
        + [_const_spec((S5_BLOCKS, LANES, 512))] * 2 + [_const_spec((S5_BLOCKS, 512, LANES))] * 2
        + [_const_spec((8, N_STATE))] * 2,
        out_specs=[row(D_S5)] + [_acc_spec(s) for s in acc_shapes],
        out_shape=[_sds((L, D_S5))] + [_sds(s) for s in acc_shapes],
        scratch_shapes=[pltpu.VMEM(s, F32) for s in acc_shapes] + [pltpu.VMEM((tt, N_STATE), F32)] * 2
        + [pltpu.VMEM((8, N_STATE), F32)] * 2,
        compiler_params=_params("arbitrary"),
    )(u, du_direct, dy, s_re, s_im, s_re, s_im, b4_re, b4_im, c4_re, c4_im, pow_r, pow_i)


def _bwd_in(x, norm_g, w_in_bf, mu, dx2, du, dzs, drws, dzr, tt):
    L = x.shape[0]
    n_t = L // tt

    def body(x_ref, g_ref, w_ref, mu_ref, dx2_ref, du_ref, dzs_ref, drws_ref, nxt_ref, dzr_ref,
             gx_o, dw_o, dg_o, dproj, dw, dg):
        i = pl.program_id(0)

        @pl.when(i == 0)
        def _():
            dw[...] = jnp.zeros_like(dw)
            dg[...] = jnp.zeros_like(dg)

        drws_v = drws_ref[...]
        rid = lax.broadcasted_iota(jnp.int32, drws_v.shape, 0)
        nxt_row = jnp.where(i == n_t - 1, 0.0, nxt_ref[0:1, :])
        nxt = jnp.where(rid == tt - 1, jnp.broadcast_to(nxt_row, drws_v.shape), pltpu.roll(drws_v, tt - 1, axis=0))
        muv = mu_ref[...]
        drw = drws_v * (1.0 - muv) + nxt * muv
        dproj[:, 0:D_S5] = du_ref[...].astype(BF16)
        dproj[:, D_S5:2 * D_S5] = dzs_ref[...].astype(BF16)
        dproj[:, 2 * D_S5:2 * D_S5 + D_SHIFT] = drw.astype(BF16)
        dproj[:, 2 * D_S5 + D_SHIFT:D_IN] = dzr_ref[...].astype(BF16)
        dh = _dot(dproj[...], w_ref[...], ((1,), (0,)), None)
        h, vjp = jax.vjp(_rms, x_ref[...], g_ref[...])
        dxh, dgv = vjp(dh)
        gx_o[...] = dx2_ref[...] + dxh
        dg[...] += dgv
        dw[...] += _dot(h.astype(BF16), dproj[...], ((0,), (0,)), None)

        @pl.when(i == n_t - 1)
        def _():
            dg_o[...] = dg[...]
            pltpu.sync_copy(dw, dw_o)

    row = lambda n: pl.BlockSpec((tt, n), lambda i: (i, 0))
    nxt = pl.BlockSpec((8, D_SHIFT), lambda i: (jnp.minimum((i + 1) * (tt // 8), L // 8 - 1), 0))
    return pl.pallas_call(
        body, name="bwd_in", grid=(n_t,),
        in_specs=[row(D_MODEL), _const_spec((1, D_MODEL)), _const_spec((D_IN, D_MODEL)), _const_spec((1, D_SHIFT)),
                  row(D_MODEL), row(D_S5), row(D_S5), row(D_SHIFT), nxt, row(D_RWKV)],
        out_specs=[row(D_MODEL), _ANY, _acc_spec((1, D_MODEL))],
        out_shape=[_sds((L, D_MODEL)), _sds((D_MODEL, D_IN)), _sds((1, D_MODEL))],
        scratch_shapes=[pltpu.VMEM((tt, D_IN), BF16), pltpu.VMEM((D_MODEL, D_IN), F32), pltpu.VMEM((1, D_MODEL), F32)],
        compiler_params=_params("arbitrary"),
    )(x, norm_g, w_in_bf, mu, dx2, du, dzs, drws, drws, dzr)


def _block_diag_b(bbar):
    bb = bbar.reshape(S5_GROUP, S5_BLOCKS, 8, S5_STATE)
    return jnp.einsum('hqgp,Gg->qGhgp', bb, jnp.eye(8, dtype=F32)).reshape(S5_BLOCKS, LANES, 512)


def _block_diag_b_t(db4):
    d = db4.reshape(S5_BLOCKS, 8, S5_GROUP, 8, S5_STATE)
    return jnp.einsum('qGhgp,Gg->hqgp', d, jnp.eye(8, dtype=F32)).reshape(S5_GROUP, N_STATE)


def _block_diag_c(c):
    cc = c.reshape(S5_BLOCKS, 8, S5_GROUP, S5_STATE)
    return jnp.einsum('qghp,gG->qgpGh', cc, jnp.eye(8, dtype=F32)).reshape(S5_BLOCKS, 512, LANES)


def _block_diag_c_t(dc4):
    d = dc4.reshape(S5_BLOCKS, 8, S5_STATE, 8, S5_GROUP)
    return jnp.einsum('qgpGh,gG->qghp', d, jnp.eye(8, dtype=F32)).reshape(S5_GROUPS, S5_GROUP, S5_STATE)


def _local_step(x, tgt, w):
    L = x.shape[0]
    tt = min(512, L)
    tp = min(256, L)
    ee = _head_sum_matrix()

    lam_re = w['s5_lam_re'].reshape(1, N_STATE)
    lam_im = w['s5_lam_im'].reshape(1, N_STATE)
    logdt = jnp.repeat(w['s5_log_dt'], S5_STATE).reshape(1, N_STATE)
    b_re_t = w['s5_b_re'].transpose(2, 0, 1).reshape(S5_GROUP, N_STATE)
    b_im_t = w['s5_b_im'].transpose(2, 0, 1).reshape(S5_GROUP, N_STATE)
    bbr, bbi, pow_r, pow_i, rpow_r, rpow_i = _s5_param_fwd(lam_re, lam_im, logdt, b_re_t, b_im_t)
    b4_re, b4_im = _block_diag_b(bbr), _block_diag_b(bbi)
    c4_re, c4_im = _block_diag_c(w['s5_c_re']), _block_diag_c(w['s5_c_im'])

    norm_g = w['norm_g'].reshape(1, D_MODEL)
    w_in_bf = (w['w_in_t'] if 'w_in_t' in w else w['w_in'].T).astype(BF16)
    u, zs, rw, zr = _fwd_in(x, norm_g, w_in_bf, tt)
    s_re, s_im, y_ssm = _s5_fwd(u, b4_re, b4_im, c4_re, c4_im, pow_r, pow_i, tt)

    row = lambda t: t.reshape(1, -1)
    zpad = jnp.zeros((HEAD, D_RWKV), F32)
    w2p = jnp.concatenate([w['rwkv_w2'], zpad], axis=0)
    a2p = jnp.concatenate([zpad, w['rwkv_a2']], axis=0)
    pre_consts = (row(w['rwkv_mu']), row(w['rwkv_w0']), w2p, row(w['rwkv_a0']), a2p,
                  row(w['rwkv_k_k']), row(w['rwkv_k_a']), ee)
    ops = _rwkv_pre_fwd(rw, *pre_consts, tt)
    ysc, states, *kept = _rwkv_scan_fwd(ops)

    post = _post(x, u, zs, zr, ysc, ops[0], ops[2], ops[3], y_ssm,
                 row(w['s5_d']), w['s5_glu_w'].astype(BF16), row(w['s5_glu_b']), row(w['rwkv_ln_w']), row(w['rwkv_ln_b']),
                 row(w['rwkv_r_k']), w['w_out'].astype(BF16), row(w['final_g']), tgt, ee, tp)
    (dx2, du_d, dzs, dzr, dysc, dr_b, dk2_b, dv_b, dy_ssm,
     dd, dglu_w, dglu_b, dln_w, dln_b, dr_k, dw_out, dgf, loss) = post

    du, db4_re, db4_im, dc4_re, dc4_im, dlbr, dlbi = _s5_bwd(
        u, du_d, dy_ssm, s_re, s_im, b4_re, b4_im, c4_re, c4_im, rpow_r, rpow_i, tt)
    group_ind = (jnp.arange(N_STATE)[:, None] // S5_STATE == jnp.arange(LANES)[None, :]).astype(F32)
    dlam_re, dlam_im, dlogdt, db_re_t, db_im_t = _s5_param_bwd(
        lam_re, lam_im, logdt, b_re_t, b_im_t, dlbr, dlbi, _block_diag_b_t(db4_re), _block_diag_b_t(db4_im), group_ind)

    cots = list(_rwkv_scan_bwd(ops, states, kept, dysc)) + [dr_b, dk2_b, dv_b]
    drws, dmu, dw0, dw2p, da0, da2p, dk_k, dk_a = _rwkv_pre_bwd(rw, *pre_consts, cots, tt)

    grad_x, dw_in, dnorm_g = _bwd_in(x, norm_g, w_in_bf, row(w['rwkv_mu']), dx2, du, dzs, drws, dzr, tt)

    unb = lambda t: t.reshape(S5_GROUP, S5_GROUPS, S5_STATE).transpose(1, 2, 0)
    grads = {
        'norm_g': dnorm_g.reshape(D_MODEL), 'w_in': dw_in,
        's5_lam_re': dlam_re.reshape(S5_GROUPS, S5_STATE), 's5_lam_im': dlam_im.reshape(S5_GROUPS, S5_STATE),
        's5_log_dt': dlogdt[0, :S5_GROUPS], 's5_b_re': unb(db_re_t), 's5_b_im': unb(db_im_t),
        's5_c_re': _block_diag_c_t(dc4_re), 's5_c_im': _block_diag_c_t(dc4_im),
        's5_d': dd.reshape(D_S5), 's5_glu_w': dglu_w, 's5_glu_b': dglu_b.reshape(D_S5),
        'rwkv_mu': dmu.reshape(-1), 'rwkv_w0': dw0.reshape(-1), 'rwkv_w2': dw2p[:HEAD], 'rwkv_a0': da0.reshape(-1),
        'rwkv_a2': da2p[HEAD:], 'rwkv_k_k': dk_k.reshape(-1), 'rwkv_k_a': dk_a.reshape(-1),
        'rwkv_r_k': dr_k.reshape(N_HEADS, HEAD), 'rwkv_ln_w': dln_w.reshape(-1), 'rwkv_ln_b': dln_b.reshape(-1),
        'w_out': dw_out, 'final_g': dgf.reshape(D_MODEL),
    }
    return loss, grad_x, grads


def _exchange(arrays, gather, axes, name):
    n = len(arrays)
    group = 2 ** len(axes)

    def body(*refs):
        send_refs, recv_refs = refs[:n], refs[n:2 * n]
        send_sems, recv_sems, local_sems = refs[2 * n:]
        pos = {ax: lax.axis_index(ax) for ax in ("x", "y", "c")}

        def index_of(p):
            idx = 0
            for ax in axes:
                idx = 2 * idx + p[ax]
            return idx

        me = index_of(pos)
        own, outs, arrivals = [], [], []
        for i, (send_ref, recv_ref) in enumerate(zip(send_refs, recv_refs)):
            def block_for(dev, send_ref=send_ref, whole=gather[i]):
                return send_ref if whole else send_ref.at[dev]

            own.append(pltpu.make_async_copy(block_for(me), recv_ref.at[me], local_sems.at[i]))
            own[-1].start()
            for k in range(1, group):
                peer = dict(pos)
                for bit, ax in enumerate(axes):
                    if (k >> bit) & 1:
                        peer[ax] = 1 - pos[ax]
                peer_idx = index_of(peer)
                sems = dict(send_sem=send_sems.at[i, k - 1], recv_sem=recv_sems.at[i, k - 1],
                            device_id=(peer["x"], peer["y"], peer["c"]), device_id_type=pl.DeviceIdType.MESH)
                outs.append(pltpu.make_async_remote_copy(src_ref=block_for(peer_idx), dst_ref=recv_ref.at[me], **sems))
                outs[-1].start()
                arrivals.append(
                    pltpu.make_async_remote_copy(src_ref=block_for(peer_idx), dst_ref=recv_ref.at[peer_idx], **sems))
        for copy in arrivals:
            copy.wait_recv()
        for copy in outs:
            copy.wait_send()
        for copy in own:
            copy.wait()

    return pl.pallas_call(
        body, name=name, in_specs=[_ANY] * n, out_specs=[_ANY] * n,
        out_shape=[jax.ShapeDtypeStruct(((group,) + a.shape) if whole else a.shape, a.dtype)
                   for a, whole in zip(arrays, gather)],
        scratch_shapes=[pltpu.SemaphoreType.DMA((n, group - 1)), pltpu.SemaphoreType.DMA((n, group - 1)),
                        pltpu.SemaphoreType.DMA((n,))],
        compiler_params=pltpu.CompilerParams(has_side_effects=True),
    )(*arrays)


def _sum_devices(ref):
    g = ref[0].astype(F32)
    for s in range(1, ref.shape[0]):
        g = g + ref[s].astype(F32)
    return g


def _adamw_math(g, w, m, v):
    m_new = ADAM_B1 * m + (1.0 - ADAM_B1) * g
    v_new = ADAM_B2 * v + (1.0 - ADAM_B2) * (g * g)
    m_hat = m_new / (1.0 - ADAM_B1 ** ADAM_STEP)
    v_hat = v_new / (1.0 - ADAM_B2 ** ADAM_STEP)
    return -ADAM_LR * (m_hat / (jnp.sqrt(v_hat) + ADAM_EPS) + ADAM_WD * w), m_new, v_new


def _adamw(gs, ws, ms, vs, reduce, name):
    n = len(ws)

    def body(*refs):
        g_refs, w_refs, m_refs, v_refs = (refs[j * n:(j + 1) * n] for j in range(4))
        outs = refs[4 * n:]
        for i in range(n):
            g = _sum_devices(g_refs[i]) if reduce else g_refs[i][...]
            res = _adamw_math(g, w_refs[i][...], m_refs[i][...], v_refs[i][...])
            for j, val in enumerate(((g,) if reduce else ()) + res):
                outs[j * n + i][...] = val

    return pl.pallas_call(
        body, name=name, out_shape=[_sds(w.shape) for w in ws] * (4 if reduce else 3),
        compiler_params=pltpu.CompilerParams(vmem_limit_bytes=VMEM_LIMIT),
    )(*gs, *ws, *ms, *vs)


def _sum_blocks(recv):
    def body(recv_ref, out_ref):
        out_ref[...] = _sum_devices(recv_ref)

    return pl.pallas_call(body, name="sum_small_grads", out_shape=_sds(recv.shape[1:]))(recv)


_WEIGHTS = [
    ('norm_g', (1, 1024), False), ('w_in', (1, 1024, 400), True), ('s5_lam_re', (1, 32, 64), False),
    ('s5_lam_im', (1, 32, 64), False), ('s5_log_dt', (1, 32), False), ('s5_b_re', (1, 32, 64, 16), False),
    ('s5_b_im', (1, 32, 64, 16), False), ('s5_c_re', (1, 32, 16, 64), False), ('s5_c_im', (1, 32, 16, 64), False),
    ('s5_d', (1, 512), False), ('s5_glu_w', (1, 64, 512), True), ('s5_glu_b', (1, 512), False),
    ('rwkv_mu', (1, 1664), False), ('rwkv_w0', (1, 512), False), ('rwkv_w2', (1, 64, 64), True),
    ('rwkv_a0', (1, 512), False), ('rwkv_a2', (1, 64, 64), True), ('rwkv_k_k', (1, 512), False),
    ('rwkv_k_a', (1, 512), False), ('rwkv_r_k', (1, 8, 64), False), ('rwkv_ln_w', (1, 512), False),
    ('rwkv_ln_b', (1, 512), False), ('w_out', (1, 128, 1024), True), ('final_g', (1024,), False),
]
_SHARDED = [(n, s) for n, s, sharded in _WEIGHTS if sharded]
_SMALL = [(n, s) for n, s, sharded in _WEIGHTS if not sharded]
_COLUMN_SHARDED = ('w_in', 'rwkv_w2', 'rwkv_a2')
_SMALL_SIZE = sum(math.prod(s) for _, s in _SMALL) + 1
_SMALL_ROWS = -(-_SMALL_SIZE // (8 * LANES)) * 8


_MINOR_SWAPPED = ('s5_b_re', 's5_b_im')


def _stored(name, t):
    return jnp.swapaxes(t, -1, -2) if name in _MINOR_SWAPPED else t


def _pack_small(grads, loss):
    flat = [_stored(n, grads[n]).reshape(-1) for n, _ in _SMALL] + [loss.reshape(1)]
    pad = _SMALL_ROWS * LANES - _SMALL_SIZE
    return jnp.concatenate(flat + [jnp.zeros((pad,), F32)]).reshape(_SMALL_ROWS, LANES)


def _unpack_small(packed):
    flat = packed.reshape(-1)
    out, off = {}, 0
    for n, s in _SMALL:
        size = math.prod(s)
        out[n] = flat[off:off + size].reshape(s[:-2] + (s[-1], s[-2]) if n in _MINOR_SWAPPED else s)
        off += size
    return out, flat[off]


_BF16_OPERANDS = ('w_in', 's5_glu_w', 'w_out')


def _join_shards(name, blocks):
    _, rows, cols = blocks.shape
    if name in _COLUMN_SHARDED:
        return blocks.transpose(1, 0, 2).reshape(rows, N_DEV * cols)
    return blocks.reshape(N_DEV * rows, cols)


def _split_shards(name, full, shard_shape):
    rows, cols = shard_shape
    if name in _COLUMN_SHARDED:
        return full.reshape(rows, N_DEV, cols).transpose(1, 0, 2)
    return full.reshape(N_DEV, rows, cols)


def kernel(x, norm_g, w_in, s5_lam_re, s5_lam_im, s5_log_dt, s5_b_re, s5_b_im, s5_c_re, s5_c_im, s5_d, s5_glu_w, s5_glu_b, rwkv_mu, rwkv_w0, rwkv_w2, rwkv_a0, rwkv_a2, rwkv_k_k, rwkv_k_a, rwkv_r_k, rwkv_ln_w, rwkv_ln_b, w_out, final_g, loss_target, m_norm_g, m_w_in, m_s5_lam_re, m_s5_lam_im, m_s5_log_dt, m_s5_b_re, m_s5_b_im, m_s5_c_re, m_s5_c_im, m_s5_d, m_s5_glu_w, m_s5_glu_b, m_rwkv_mu, m_rwkv_w0, m_rwkv_w2, m_rwkv_a0, m_rwkv_a2, m_rwkv_k_k, m_rwkv_k_a, m_rwkv_r_k, m_rwkv_ln_w, m_rwkv_ln_b, m_w_out, m_final_g, v_norm_g, v_w_in, v_s5_lam_re, v_s5_lam_im, v_s5_log_dt, v_s5_b_re, v_s5_b_im, v_s5_c_re, v_s5_c_im, v_s5_d, v_s5_glu_w, v_s5_glu_b, v_rwkv_mu, v_rwkv_w0, v_rwkv_w2, v_rwkv_a0, v_rwkv_a2, v_rwkv_k_k, v_rwkv_k_a, v_rwkv_r_k, v_rwkv_ln_w, v_rwkv_ln_b, v_w_out, v_final_g):
    given = dict(locals())

    n_sh = len(_SHARDED)
    everyone = ("x", "y", "c")
    shards = [given[n][0].astype(BF16 if n in _BF16_OPERANDS else F32) for n, _ in _SHARDED]
    shards[0] = shards[0].T
    gathered = _exchange(shards, (True,) * n_sh, everyone, "gather_weights")
    local = {n: _join_shards(n, blocks) for (n, _), blocks in list(zip(_SHARDED, gathered))[1:]}
    local['w_in_t'] = gathered[0].reshape(D_IN, D_MODEL)
    local.update({n: (given[n][0] if len(s) > 1 else given[n]) for n, s in _SMALL})

    loss, grad_x, grads = _local_step(x[0], loss_target[0], local)

    blocks = [_split_shards(n, grads[n], s[1:]).astype(BF16) for n, s in _SHARDED]
    blocks[0] = grads['w_in'].T.reshape(N_DEV, D_IN // N_DEV, D_MODEL).astype(BF16)
    small = _pack_small(grads, loss[0, 0]).reshape(N_DEV, _SMALL_ROWS // N_DEV, LANES)
    recv = _exchange(blocks + [small], (False,) * (n_sh + 1), everyone, "exchange_grads")
    small_sum = _exchange([_sum_blocks(recv[-1])], (True,), everyone, "gather_small_grads")[0]

    result = {}
    for group, name in (([0], "adamw_w_in"), ([1, 2, 3, 4], "adamw_shards")):
        ns = [_SHARDED[i][0] for i in group]
        own = (lambda t: t[0].T) if group == [0] else (lambda t: t[0])
        back = (lambda t: t.T[None]) if group == [0] else (lambda t: t[None])
        res = _adamw([recv[i] for i in group], [own(given[n]) for n in ns], [own(given['m_' + n]) for n in ns],
                     [own(given['v_' + n]) for n in ns], True, name)
        for j, n in enumerate(ns):
            result[n] = [back(res[k * len(ns) + j]) for k in range(4)]
    g_small, total = _unpack_small(small_sum)
    two_d = lambda t: t.reshape(1, -1) if t.ndim == 1 else t
    ns = [n for n, _ in _SMALL]
    res = _adamw([two_d(g_small[n]) for n in ns], [two_d(_stored(n, given[n])) for n in ns],
                 [two_d(_stored(n, given['m_' + n])) for n in ns], [two_d(_stored(n, given['v_' + n])) for n in ns],
                 False, "adamw_small")
    for j, (n, s) in enumerate(_SMALL):
        result[n] = [_stored(n, g_small[n])] + [_stored(n, res[k * len(ns) + j]).reshape(s) for k in range(3)]

    outs = [total, grad_x[None]]
    for k in range(4):
        outs += [result[n][k] for n, _, _ in _WEIGHTS]
    return tuple(outs)
```

```python
import math

import jax
import jax.numpy as jnp
from jax import lax
from jax.experimental import pallas as pl
from jax.experimental.pallas import tpu as pltpu

F32 = jnp.float32
BF16 = jnp.bfloat16
HI = lax.Precision.HIGH

D_MODEL = 1024
D_S5 = 512
D_RWKV = 512
S5_GROUPS = 32
S5_GROUP = 16
S5_STATE = 64
N_STATE = S5_GROUPS * S5_STATE
N_HEADS = 8
HEAD = 64
D_SHIFT = 3 * D_RWKV + 128
D_IN = 2 * D_S5 + D_SHIFT + D_RWKV
NORM_EPS = 1e-6
GN_EPS = 64e-5
N_DEV = 8
LANES = 128
S5_BLOCKS = 4
RWKV_CHUNK = 64
RWKV_CHUNKS_PER_STEP = 4
VMEM_LIMIT = 56 * 1024 * 1024

ADAM_LR = 0.001
ADAM_B1 = 0.9
ADAM_B2 = 0.999
ADAM_EPS = 1e-08
ADAM_WD = 0.01
ADAM_STEP = 10


def _dot(a, b, dims, prec):
    return lax.dot_general(a, b, (dims, ((), ())), precision=prec, preferred_element_type=F32)


def _dot_bf(a, b, dims):
    return _dot(a.astype(BF16), b.astype(BF16), dims, None)


def _make_mm(cast, prec):
    @jax.custom_vjp
    def mm(a, b):
        return _dot(cast(a), cast(b), ((1,), (0,)), prec)

    def fwd(a, b):
        return mm(a, b), (a, b)

    def bwd(res, g):
        a, b = res
        return (_dot(cast(g), cast(b), ((1,), (1,)), prec), _dot(cast(a), cast(g), ((0,), (0,)), prec))

    mm.defvjp(fwd, bwd)
    return mm


mm_bf = _make_mm(lambda t: t.astype(BF16), None)


@jax.custom_vjp
def mm_w(a, w):
    return _dot(a.astype(BF16), w, ((1,), (0,)), None)


def _mm_w_fwd(a, w):
    return mm_w(a, w), w


def _mm_w_bwd(w, g):
    return _dot(g.astype(BF16), w, ((1,), (1,)), None), jnp.zeros_like(w)


mm_w.defvjp(_mm_w_fwd, _mm_w_bwd)


def _make_head_sum(split):
    def product(x, ee):
        hi = x.astype(BF16)
        out = _dot(hi, ee, ((1,), (0,)), None)
        if split:
            out = out + _dot((x - hi.astype(F32)).astype(BF16), ee, ((1,), (0,)), None)
        return out

    @jax.custom_vjp
    def head_sum(x, ee):
        return product(x, ee)

    def fwd(x, ee):
        return product(x, ee), ee

    def bwd(ee, g):
        return product(g, ee), jnp.zeros_like(ee)

    head_sum.defvjp(fwd, bwd)
    return head_sum


head_sum = _make_head_sum(False)
head_sum_split = _make_head_sum(True)


@jax.custom_vjp
def _sigmoid(x):
    return 1.0 / (1.0 + jnp.exp(-x))


def _sigmoid_fwd(x):
    s = _sigmoid(x)
    return s, s


_sigmoid.defvjp(_sigmoid_fwd, lambda s, g: (g * s * (1.0 - s),))


@jax.custom_vjp
def _silu(x):
    return x * _sigmoid(x)


def _silu_fwd(x):
    s = _sigmoid(x)
    return x * s, (x, s)


def _silu_bwd(res, g):
    x, s = res
    return (g * s * (1.0 + x * (1.0 - s)),)


_silu.defvjp(_silu_fwd, _silu_bwd)


@jax.custom_vjp
def _softplus(x):
    return jnp.maximum(x, 0.0) + jnp.log(1.0 + jnp.exp(-jnp.abs(x)))


def _softplus_fwd(x):
    e = jnp.exp(-jnp.abs(x))
    return jnp.maximum(x, 0.0) + jnp.log(1.0 + e), (x, e)


def _softplus_bwd(res, g):
    x, e = res
    return (g * jnp.where(x >= 0.0, 1.0, e) / (1.0 + e),)


_softplus.defvjp(_softplus_fwd, _softplus_bwd)


@jax.custom_vjp
def _normalize_heads(x, ee):
    return x / jnp.maximum(jnp.sqrt(head_sum(x * x, ee)), 1e-12)


def _normalize_heads_fwd(x, ee):
    norm = jnp.sqrt(head_sum(x * x, ee))
    inv = 1.0 / jnp.maximum(norm, 1e-12)
    y = x * inv
    return y, (y, inv, norm, ee)


def _normalize_heads_bwd(res, g):
    y, inv, norm, ee = res
    along = jnp.where(norm > 1e-12, head_sum(g * y, ee), 0.0)
    return inv * (g - y * along), jnp.zeros_like(ee)


_normalize_heads.defvjp(_normalize_heads_fwd, _normalize_heads_bwd)


_GELU_C = 2.0 * math.sqrt(2.0 / math.pi)


def _gelu_gate(x):
    return 1.0 / (1.0 + jnp.exp(-_GELU_C * x * (1.0 + 0.044715 * (x * x))))


@jax.custom_vjp
def _gelu(x):
    return x * _gelu_gate(x)


def _gelu_fwd(x):
    s = _gelu_gate(x)
    return x * s, (x, s)


def _gelu_bwd(res, g):
    x, s = res
    return (g * (s + x * s * (1.0 - s) * (_GELU_C * (1.0 + 3.0 * 0.044715 * (x * x)))),)


_gelu.defvjp(_gelu_fwd, _gelu_bwd)


def _rms(x, g):
    return x * lax.rsqrt(jnp.mean(x * x, axis=-1, keepdims=True) + NORM_EPS) * g


def _const_spec(shape):
    nd = len(shape)
    return pl.BlockSpec(shape, lambda *_: (0,) * nd, pipeline_mode=pl.Buffered(1))


def _acc_spec(shape):
    nd = len(shape)
    return pl.BlockSpec(shape, lambda *_: (0,) * nd)


def _params(sem):
    return pltpu.CompilerParams(dimension_semantics=(sem,), vmem_limit_bytes=VMEM_LIMIT)


_ANY = pl.BlockSpec(memory_space=pl.ANY)


def _sds(shape):
    return jax.ShapeDtypeStruct(shape, F32)


def _head_sum_matrix():
    i = jnp.arange(D_RWKV) // HEAD
    return (i[:, None] == i[None, :]).astype(BF16)


def _s5_param_fn(lam_re, lam_im, logdt, b_re, b_im):
    dt = jnp.exp(logdt)
    mag = jnp.exp(lam_re * dt)
    ang = lam_im * dt
    lbr = mag * jnp.cos(ang)
    lbi = mag * jnp.sin(ang)
    nr = lbr - 1.0
    den = lam_re * lam_re + lam_im * lam_im
    cr = (nr * lam_re + lbi * lam_im) / den
    ci = (lbi * lam_re - nr * lam_im) / den
    return lbr, lbi, cr * b_re - ci * b_im, cr * b_im + ci * b_re


def _cmul(ar, ai, br, bi):
    return ar * br - ai * bi, ar * bi + ai * br


def _s5_param_fwd(lam_re, lam_im, logdt, b_re, b_im):
    def body(lr, li, ld, br, bi, o_br, o_bi, o_pr, o_pi, o_qr, o_qi):
        lbr, lbi, bbr, bbi = _s5_param_fn(lr[...], li[...], ld[...], br[...], bi[...])
        o_br[...] = bbr
        o_bi[...] = bbi
        rid = lax.broadcasted_iota(jnp.int32, (8, N_STATE), 0)
        pr, pi_ = lbr, lbi
        fwd_r = rev_r = jnp.broadcast_to(pr, (8, N_STATE))
        fwd_i = rev_i = jnp.broadcast_to(pi_, (8, N_STATE))
        for j in range(1, 8):
            pr, pi_ = _cmul(pr, pi_, lbr, lbi)
            fwd_r = jnp.where(rid == j, jnp.broadcast_to(pr, (8, N_STATE)), fwd_r)
            fwd_i = jnp.where(rid == j, jnp.broadcast_to(pi_, (8, N_STATE)), fwd_i)
            rev_r = jnp.where(rid == 7 - j, jnp.broadcast_to(pr, (8, N_STATE)), rev_r)
            rev_i = jnp.where(rid == 7 - j, jnp.broadcast_to(pi_, (8, N_STATE)), rev_i)
        o_pr[...] = fwd_r
        o_pi[...] = fwd_i
        o_qr[...] = rev_r
        o_qi[...] = -rev_i

    return pl.pallas_call(
        body, name="s5_param_fwd",
        out_shape=[_sds((S5_GROUP, N_STATE))] * 2 + [_sds((8, N_STATE))] * 4,
    )(lam_re, lam_im, logdt, b_re, b_im)


def _s5_param_bwd(lam_re, lam_im, logdt, b_re, b_im, d_lbr, d_lbi, d_bbr, d_bbi, group_ind):
    def body(lr, li, ld, br, bi, g0, g1, g2, g3, ind, o_lr, o_li, o_ld, o_br, o_bi):
        _, vjp = jax.vjp(_s5_param_fn, lr[...], li[...], ld[...], br[...], bi[...])
        d_lr, d_li, d_ld, d_br, d_bi = vjp((g0[...], g1[...], g2[...], g3[...]))
        o_lr[...] = d_lr
        o_li[...] = d_li
        o_ld[...] = _dot(jnp.broadcast_to(d_ld, (8, N_STATE)), ind[...], ((1,), (0,)), HI)
        o_br[...] = d_br
        o_bi[...] = d_bi

    return pl.pallas_call(
        body, name="s5_param_bwd",
        out_shape=[_sds((1, N_STATE))] * 2 + [_sds((8, LANES))] + [_sds((S5_GROUP, N_STATE))] * 2,
    )(lam_re, lam_im, logdt, b_re, b_im, d_lbr, d_lbi, d_bbr, d_bbi, group_ind)


def _fwd_in(x, norm_g, w_in_bf, tt):
    L = x.shape[0]

    def body(x_ref, g_ref, w_ref, u_ref, zs_ref, rw_ref, zr_ref):
        h = _rms(x_ref[...], g_ref[...])
        proj = _dot(h.astype(BF16), w_ref[...], ((1,), (1,)), None)
        u_ref[...] = proj[:, 0:D_S5]
        zs_ref[...] = proj[:, D_S5:2 * D_S5]
        rw_ref[...] = proj[:, 2 * D_S5:2 * D_S5 + D_SHIFT]
        zr_ref[...] = proj[:, 2 * D_S5 + D_SHIFT:D_IN]

    row = lambda n: pl.BlockSpec((tt, n), lambda i: (i, 0))
    return pl.pallas_call(
        body, name="fwd_in", grid=(L // tt,),
        in_specs=[row(D_MODEL), _const_spec((1, D_MODEL)), _const_spec((D_IN, D_MODEL))],
        out_specs=[row(D_S5), row(D_S5), row(D_SHIFT), row(D_RWKV)],
        out_shape=[_sds((L, D_S5)), _sds((L, D_S5)), _sds((L, D_SHIFT)), _sds((L, D_RWKV))],
        compiler_params=_params("parallel"),
    )(x, norm_g, w_in_bf)


S5_LANE_CHUNK = 512


def _tile_scan(re_ref, im_ref, pow_r_ref, pow_i_ref, carry_r_ref, carry_i_ref, reverse):
    t, n = re_ref.shape
    n_groups = t // 8
    ch = S5_LANE_CHUNK
    rid = lax.broadcasted_iota(jnp.int32, (8, ch), 0)
    for c in range(n // ch):
        cols = slice(c * ch, (c + 1) * ch)
        pow_r = pow_r_ref[:, cols]
        pow_i = pow_i_ref[:, cols]
        row = lambda tile, j: jnp.broadcast_to(tile[j:j + 1], (8, ch))
        levels = []
        for d in (1, 2, 4):
            keep = (rid < 8 - d) if reverse else (rid >= d)
            j = 8 - d if reverse else d - 1
            levels.append(((8 - d) if reverse else d,
                           jnp.where(keep, row(pow_r, j), 0.0), jnp.where(keep, row(pow_i, j), 0.0)))

        def group(g, carry):
            r0 = pl.multiple_of(((n_groups - 1 - g) if reverse else g) * 8, 8)
            xr = re_ref[pl.ds(r0, 8), cols]
            xi = im_ref[pl.ds(r0, 8), cols]
            for shift, lr, li in levels:
                mr, mi = _cmul(lr, li, pltpu.roll(xr, shift, axis=0), pltpu.roll(xi, shift, axis=0))
                xr = xr + mr
                xi = xi + mi
            mr, mi = _cmul(pow_r, pow_i, carry[0], carry[1])
            xr = xr + mr
            xi = xi + mi
            re_ref[pl.ds(r0, 8), cols] = xr
            im_ref[pl.ds(r0, 8), cols] = xi
            last = 0 if reverse else 7
            return row(xr, last), row(xi, last)

        out = lax.fori_loop(0, n_groups, group, (carry_r_ref[:, cols], carry_i_ref[:, cols]))
        carry_r_ref[:, cols] = out[0]
        carry_i_ref[:, cols] = out[1]


def _s5_fwd(u, b4_re, b4_im, c4_re, c4_im, pow_r, pow_i, tt):
    L = u.shape[0]

    def body(u_ref, bre_ref, bim_ref, cre_ref, cim_ref, pr_ref, pi_ref, sre_o, sim_o, y_o, car_r, car_i):
        @pl.when(pl.program_id(0) == 0)
        def _():
            car_r[...] = jnp.zeros_like(car_r)
            car_i[...] = jnp.zeros_like(car_i)

        uv = u_ref[...]
        for q in range(S5_BLOCKS):
            uq = uv[:, q * LANES:(q + 1) * LANES]
            cols = slice(q * 512, (q + 1) * 512)
            sre_o[:, cols] = _dot_bf(uq, bre_ref[q], ((1,), (0,)))
            sim_o[:, cols] = _dot_bf(uq, bim_ref[q], ((1,), (0,)))
        _tile_scan(sre_o, sim_o, pr_ref, pi_ref, car_r, car_i, reverse=False)
        for q in range(S5_BLOCKS):
            cols = slice(q * 512, (q + 1) * 512)
            y_o[:, q * LANES:(q + 1) * LANES] = (_dot_bf(sre_o[:, cols], cre_ref[q], ((1,), (0,)))
                                                 - _dot_bf(sim_o[:, cols], cim_ref[q], ((1,), (0,))))

    row = lambda n: pl.BlockSpec((tt, n), lambda i: (i, 0))
    return pl.pallas_call(
        body, name="s5_fwd", grid=(L // tt,),
        in_specs=[row(D_S5)] + [_const_spec((S5_BLOCKS, LANES, 512))] * 2 + [_const_spec((S5_BLOCKS, 512, LANES))] * 2
        + [_const_spec((8, N_STATE))] * 2,
        out_specs=[row(N_STATE), row(N_STATE), row(D_S5)],
        out_shape=[_sds((L, N_STATE)), _sds((L, N_STATE)), _sds((L, D_S5))],
        scratch_shapes=[pltpu.VMEM((8, N_STATE), F32)] * 2,
        compiler_params=_params("arbitrary"),
    )(u, b4_re, b4_im, c4_re, c4_im, pow_r, pow_i)


def _rwkv_pre_fn(r, k, v, wa, w0, w2p, a0, a2p, k_k, k_a, ee):
    w = -_softplus(-(w0 + mm_bf(jnp.tanh(wa), w2p))) - 0.5
    logw = -jnp.exp(w)
    a = _sigmoid(a0 + mm_bf(wa, a2p))
    kk = _normalize_heads(k * k_k, ee)
    k2 = k * (1.0 + (a - 1.0) * k_a)
    return r, logw, k2, v, -kk, kk * a


N_PAIRS = N_HEADS // 2


def _head_spec(tt):
    return pl.BlockSpec((N_PAIRS, tt, LANES), lambda i: (0, i, 0))


def _load_heads(ref):
    return jnp.concatenate([ref[p] for p in range(N_PAIRS)], axis=-1)


def _store_heads(ref, val):
    for p in range(N_PAIRS):
        ref[p] = val[:, p * LANES:(p + 1) * LANES]


def _split_pairs(x):
    return jnp.concatenate([x[:, :, :HEAD], x[:, :, HEAD:]], axis=0)


def _join_pairs(x):
    return jnp.concatenate([x[:N_PAIRS], x[N_PAIRS:]], axis=-1)


def _shifted(rw, prev_blk, first):
    rolled = pltpu.roll(rw, 1, axis=0)
    prev_row = jnp.where(first, 0.0, prev_blk[7:8, :])
    rid = lax.broadcasted_iota(jnp.int32, rw.shape, 0)
    return jnp.where(rid == 0, jnp.broadcast_to(prev_row, rw.shape), rolled)


def _split_rw(t):
    return t[:, 0:512], t[:, 512:1024], t[:, 1024:1536], t[:, 1536:1664]


def _rwkv_pre_specs(tt):
    row = pl.BlockSpec((tt, D_SHIFT), lambda i: (i, 0))
    prev = pl.BlockSpec((8, D_SHIFT), lambda i: (jnp.maximum(i * (tt // 8) - 1, 0), 0))
    consts = [_const_spec((1, D_SHIFT)), _const_spec((1, D_RWKV)), _const_spec((LANES, D_RWKV)),
              _const_spec((1, D_RWKV)), _const_spec((LANES, D_RWKV)), _const_spec((1, D_RWKV)),
              _const_spec((1, D_RWKV)), _const_spec((D_RWKV, D_RWKV))]
    return [row, prev] + consts


def _rwkv_pre_fwd(rw, mu, w0, w2p, a0, a2p, k_k, k_a, ee, tt):
    L = rw.shape[0]

    def body(rw_ref, prev_ref, mu_ref, w0_ref, w2_ref, a0_ref, a2_ref, kk_ref, ka_ref, ee_ref, *outs):
        rwv = rw_ref[...]
        rws = rwv + (_shifted(rwv, prev_ref[...], pl.program_id(0) == 0) - rwv) * mu_ref[...]
        res = _rwkv_pre_fn(*_split_rw(rws), w0_ref[...], w2_ref[...], a0_ref[...], a2_ref[...],
                           kk_ref[...], ka_ref[...], ee_ref[...])
        for o, val in zip(outs, res):
            _store_heads(o, val)

    return pl.pallas_call(
        body, name="rwkv_pre_fwd", grid=(L // tt,),
        in_specs=_rwkv_pre_specs(tt), out_specs=[_head_spec(tt)] * 6, out_shape=[_sds((N_PAIRS, L, LANES))] * 6,
        compiler_params=_params("parallel"),
    )(rw, rw, mu, w0, w2p, a0, a2p, k_k, k_a, ee)


def _rwkv_pre_bwd(rw, mu, w0, w2p, a0, a2p, k_k, k_a, ee, cots, tt):
    L = rw.shape[0]
    n_t = L // tt

    def body(rw_ref, prev_ref, mu_ref, w0_ref, w2_ref, a0_ref, a2_ref, kk_ref, ka_ref, ee_ref,
             c_r, c_w, c_k, c_v, c_a, c_b, cb_r, cb_k, cb_v,
             drws_ref, dmu_o, dw0_o, dw2_o, da0_o, da2_o, dkk_o, dka_o,
             dmu, dw0, dw2, da0, da2, dkk, dka):
        i = pl.program_id(0)
        accs = (dmu, dw0, dw2, da0, da2, dkk, dka)

        @pl.when(i == 0)
        def _():
            for acc in accs:
                acc[...] = jnp.zeros_like(acc)

        rwv = rw_ref[...]
        diff = _shifted(rwv, prev_ref[...], i == 0) - rwv
        rws = rwv + diff * mu_ref[...]
        consts = (w0_ref[...], w2_ref[...], a0_ref[...], a2_ref[...], kk_ref[...], ka_ref[...])
        _, vjp = jax.vjp(lambda *a: _rwkv_pre_fn(*a, ee_ref[...]), *_split_rw(rws), *consts)
        scan = [_load_heads(c) for c in (c_r, c_w, c_k, c_v, c_a, c_b)]
        g = vjp((scan[0] + cb_r[...], scan[1], scan[2] + cb_k[...], scan[3] + cb_v[...], scan[4], scan[5]))
        drws = jnp.concatenate(g[0:4], axis=-1)
        drws_ref[...] = drws
        dmu[...] += jnp.sum(drws * diff, axis=0, keepdims=True)
        for acc, val in zip(accs[1:], g[4:]):
            acc[...] += val

        @pl.when(i == n_t - 1)
        def _():
            for acc, out in zip(accs, (dmu_o, dw0_o, dw2_o, da0_o, da2_o, dkk_o, dka_o)):
                out[...] = acc[...]

    row = pl.BlockSpec((tt, D_RWKV), lambda i: (i, 0))
    shapes = [(1, D_SHIFT), (1, D_RWKV), (LANES, D_RWKV), (1, D_RWKV), (LANES, D_RWKV), (1, D_RWKV), (1, D_RWKV)]
    return pl.pallas_call(
        body, name="rwkv_pre_bwd", grid=(n_t,),
        in_specs=_rwkv_pre_specs(tt) + [_head_spec(tt)] * 6 + [row] * 3,
        out_specs=[pl.BlockSpec((tt, D_SHIFT), lambda i: (i, 0))] + [_acc_spec(s) for s in shapes],
        out_shape=[_sds((L, D_SHIFT))] + [_sds(s) for s in shapes],
        scratch_shapes=[pltpu.VMEM(s, F32) for s in shapes],
        compiler_params=_params("arbitrary"),
    )(rw, rw, mu, w0, w2p, a0, a2p, k_k, k_a, ee, *cots)


def _bmm(a, b):
    return lax.dot_general(a, b, (((2,), (1,)), ((0,), (0,))), precision=HI, preferred_element_type=F32)


def _bmm_nt(a, b):
    return lax.dot_general(a, b, (((2,), (2,)), ((0,), (0,))), precision=HI, preferred_element_type=F32)


def _bmm_tn(a, b):
    return lax.dot_general(a, b, (((1,), (1,)), ((0,), (0,))), precision=HI, preferred_element_type=F32)


def _bdot_bf(a, b, lhs_dim, rhs_dim):
    return lax.dot_general(a.astype(BF16), b.astype(BF16), (((lhs_dim,), (rhs_dim,)), ((0,), (0,))),
                           preferred_element_type=F32)


@jax.custom_vjp
def _bmm_bf(a, b):
    return _bdot_bf(a, b, 2, 1)


def _bmm_bf_fwd(a, b):
    return _bmm_bf(a, b), (a, b)


def _bmm_bf_bwd(res, g):
    a, b = res
    return _bdot_bf(g, b, 2, 2), _bdot_bf(a, g, 1, 1)


_bmm_bf.defvjp(_bmm_bf_fwd, _bmm_bf_bwd)


@jax.custom_vjp
def _bmm_tn_bf(a, b):
    return _bdot_bf(a, b, 1, 1)


def _bmm_tn_bf_fwd(a, b):
    return _bmm_tn_bf(a, b), (a, b)


def _bmm_tn_bf_bwd(res, g):
    a, b = res
    return _bdot_bf(b, g, 2, 2), _bdot_bf(a, g, 2, 1)


_bmm_tn_bf.defvjp(_bmm_tn_bf_fwd, _bmm_tn_bf_bwd)


def _unit_lower_inverse(a):
    t = a.shape[-1]
    ti = lax.broadcasted_iota(jnp.int32, (t, t), 0)
    si = lax.broadcasted_iota(jnp.int32, (t, t), 1)

    def same_block(bits):
        shift = jnp.int32(bits)
        return (lax.shift_right_logical(ti, shift) == lax.shift_right_logical(si, shift))[None]

    def mm(x, y):
        return _bdot_bf(x, y, 2, 1)

    d = jnp.where(same_block(3), a, 0.0)
    inv = jnp.where(ti == si, 1.0, 0.0)[None] + d
    pw = mm(d, d)
    both = mm(jnp.concatenate([inv, pw], axis=1), pw)
    inv = inv + both[:, :t]
    inv = inv + mm(inv, both[:, t:])
    bits = 3
    while (1 << bits) < t:
        e = jnp.where(same_block(bits), 0.0, jnp.where(same_block(bits + 1), a, 0.0))
        inv = inv + mm(mm(inv, e), inv)
        bits += 1
    return inv


def _tri_mask(t):
    ri = lax.broadcasted_iota(jnp.int32, (2 * t, 2 * t), 0)
    ci = lax.broadcasted_iota(jnp.int32, (2 * t, 2 * t), 1)
    top_rows = ri < t
    diff = jnp.where(top_rows, ri, ri - t) - jnp.where(ci < t, ci, ci - t)
    return (diff >= jnp.where(top_rows, 1, 0))[None]


def _ones_tri(n_h, t):
    ti = lax.broadcasted_iota(jnp.int32, (t, t), 0)
    si = lax.broadcasted_iota(jnp.int32, (t, t), 1)
    return jnp.broadcast_to(jnp.where(ti >= si, 1.0, 0.0)[None], (n_h, t, t))


@jax.custom_vjp
def _running_sum_kept(logw, kept):
    return kept


def _running_sum_kept_bwd(shape, g):
    return _bmm_tn(_ones_tri(shape[0], shape[1]), g), jnp.zeros_like(g)


_running_sum_kept.defvjp(lambda logw, kept: (kept, logw.shape), _running_sum_kept_bwd)


@jax.custom_vjp
def _tri_products_kept(ar, bk, kept):
    return kept


def _tri_products_kept_bwd(res, g):
    ar, bk = res
    g = jnp.where(_tri_mask(ar.shape[1] // 2), g, 0.0)
    return _bmm(g, bk), _bmm_tn(g, ar), jnp.zeros_like(g)


_tri_products_kept.defvjp(lambda ar, bk, kept: (kept, (ar, bk)), _tri_products_kept_bwd)


@jax.custom_vjp
def _solve_unit_lower(a, rhs, inv, kept=None):
    return _bmm(inv, rhs) if kept is None else kept


def _solve_fwd(a, rhs, inv, kept=None):
    u = _bmm(inv, rhs) if kept is None else kept
    return u, (inv, u, kept is not None)


def _solve_bwd(res, du):
    inv, u, had_kept = res
    d_rhs = _bmm_tn(inv, du)
    return _bmm_nt(d_rhs, u), d_rhs, jnp.zeros_like(inv), (jnp.zeros_like(u) if had_kept else None)


_solve_unit_lower.defvjp(_solve_fwd, _solve_bwd)


def _rwkv_chunk(st0, r, logw, k, v, a, b, kept=None):
    n_h, t, _ = r.shape
    log_p = _bmm(_ones_tri(n_h, t), logw) if kept is None else _running_sum_kept(logw, kept[0])
    p_in = jnp.exp(log_p)
    p_inv = jnp.exp(-log_p)
    at = a * jnp.exp(log_p - logw)
    rt = r * p_in
    ar = jnp.concatenate([at, rt], axis=1)
    bk = jnp.concatenate([b * p_inv, k * p_inv], axis=1)
    if kept is None:
        m = jnp.where(_tri_mask(t), _bmm_nt(ar, bk), 0.0)
        inv = _unit_lower_inverse(m[:, :t, :t])
    else:
        m = _tri_products_kept(ar, bk, kept[1])
        inv = kept[2]
    top, bottom = m[:, :t], m[:, t:]
    rhs = _bmm_bf(jnp.concatenate([at, top[:, :, t:]], axis=2), jnp.concatenate([st0, v], axis=1))
    u = _solve_unit_lower(top[:, :, :t], rhs, inv, None if kept is None else kept[3])
    y = _bmm_bf(jnp.concatenate([rt, bottom], axis=2), jnp.concatenate([st0, u, v], axis=1))
    p_end = jnp.swapaxes(p_in[:, t - 1:t, :], 1, 2)
    st1 = (st0 + _bmm_tn_bf(bk, jnp.concatenate([u, v], axis=1))) * p_end
    return y, st1, (log_p, m, inv, u)


def _rwkv_scan_fwd(ops):
    n_h, L, n = N_HEADS, ops[0].shape[1], HEAD
    t = RWKV_CHUNK
    per = min(RWKV_CHUNKS_PER_STEP, L // t)
    n_c = L // t
    n_s = n_c // per

    def body(r_ref, w_ref, k_ref, v_ref, a_ref, b_ref, y_ref, st_ref, logp_ref, m_ref, inv_ref, u_ref, st):
        @pl.when(pl.program_id(0) == 0)
        def _():
            st[...] = jnp.zeros_like(st)

        st0 = st[...]
        for j in range(per):
            rows = slice(j * t, (j + 1) * t)
            st_ref[j] = st0
            y, st0, (log_p, m, inv, u) = _rwkv_chunk(
                st0, *(_split_pairs(ref[:, rows, :]) for ref in (r_ref, w_ref, k_ref, v_ref, a_ref, b_ref)))
            y_ref[:, rows, :] = _join_pairs(y)
            logp_ref[:, rows, :] = log_p
            u_ref[:, rows, :] = u
            m_ref[j] = m
            inv_ref[j] = inv
        st[...] = st0

    pairs = pl.BlockSpec((N_PAIRS, per * t, LANES), lambda c: (0, c, 0))
    blk = pl.BlockSpec((n_h, per * t, n), lambda c: (0, c, 0))
    per_chunk = lambda m: pl.BlockSpec((per, n_h, m, m), lambda c: (c, 0, 0, 0))
    return pl.pallas_call(
        body, name="rwkv_scan_fwd", grid=(n_s,), in_specs=[pairs] * 6,
        out_specs=[pairs, per_chunk(n), blk, per_chunk(2 * t), per_chunk(t), blk],
        out_shape=[_sds((N_PAIRS, L, LANES)), _sds((n_c, n_h, n, n)), _sds((n_h, L, n)),
                   _sds((n_c, n_h, 2 * t, 2 * t)), _sds((n_c, n_h, t, t)), _sds((n_h, L, n))],
        scratch_shapes=[pltpu.VMEM((n_h, n, n), F32)],
        compiler_params=_params("arbitrary"),
    )(*ops)


def _rwkv_scan_bwd(ops, states, kept, dy):
    n_h, L, n = N_HEADS, ops[0].shape[1], HEAD
    t = RWKV_CHUNK
    per = min(RWKV_CHUNKS_PER_STEP, L // t)
    n_s = L // t // per

    def body(r_ref, w_ref, k_ref, v_ref, a_ref, b_ref, st_ref, logp_ref, m_ref, inv_ref, u_ref, dy_ref,
             dr, dw, dk, dv, da, db, dst):
        @pl.when(pl.program_id(0) == 0)
        def _():
            dst[...] = jnp.zeros_like(dst)

        vjps = []
        for j in range(per):
            rows = slice(j * t, (j + 1) * t)
            have = (logp_ref[:, rows, :], m_ref[j], inv_ref[j], u_ref[:, rows, :])
            args = [_split_pairs(ref[:, rows, :]) for ref in (r_ref, w_ref, k_ref, v_ref, a_ref, b_ref)]
            vjps.append(jax.vjp(lambda *a, have=have: _rwkv_chunk(*a, kept=have)[:2], st_ref[j], *args)[1])
        d_state = dst[...]
        for j in reversed(range(per)):
            rows = slice(j * t, (j + 1) * t)
            g = vjps[j]((_split_pairs(dy_ref[:, rows, :]), d_state))
            d_state = g[0]
            for out, val in zip((dr, dw, dk, dv, da, db), g[1:]):
                out[:, rows, :] = _join_pairs(val)
        dst[...] = d_state

    pairs = pl.BlockSpec((N_PAIRS, per * t, LANES), lambda c: (0, n_s - 1 - c, 0))
    blk = pl.BlockSpec((n_h, per * t, n), lambda c: (0, n_s - 1 - c, 0))
    per_chunk = lambda m: pl.BlockSpec((per, n_h, m, m), lambda c: (n_s - 1 - c, 0, 0, 0))
    return pl.pallas_call(
        body, name="rwkv_scan_bwd", grid=(n_s,),
        in_specs=[pairs] * 6 + [per_chunk(n), blk, per_chunk(2 * t), per_chunk(t), blk, pairs],
        out_specs=[pairs] * 6, out_shape=[_sds((N_PAIRS, L, LANES))] * 6,
        scratch_shapes=[pltpu.VMEM((n_h, n, n), F32)],
        compiler_params=_params("arbitrary"),
    )(*ops, states, *kept, dy)


def _post_fn(x, u, zs, zr, ysc, r, k2, v, y_ssm, gate_in, d, glu_b, ln_w, ln_b, r_k, gf,
              glu_w, wo_s5, wo_rwkv, tgt, ee):
    y3 = _gelu(y_ssm + d * u)
    y_s5 = y3 * _sigmoid(mm_w(y3, glu_w) + glu_b + gate_in) * _silu(zs)
    mean = head_sum_split(ysc, ee) * (1.0 / HEAD)
    yc = ysc - mean
    var = head_sum(yc * yc, ee) * (1.0 / HEAD)
    gn = yc * lax.rsqrt(var + GN_EPS) * ln_w + ln_b
    bonus = head_sum(r * k2 * r_k, ee) * v
    y_rwkv = (gn + bonus) * _silu(zr)
    x2 = x + mm_w(y_s5, wo_s5) + mm_w(y_rwkv, wo_rwkv)
    err = _rms(x2, gf) - tgt
    return 0.5 * jnp.mean(err * err, axis=-1, keepdims=True), (y3, y_s5, y_rwkv)


def _post(x, u, zs, zr, ysc, r, k2, v, y_ssm, d, glu_w, glu_b, ln_w, ln_b, r_k, w_out, gf, tgt, ee, tt):
    L = x.shape[0]
    n_t = L // tt
    acc_shapes = [(1, D_S5), (D_S5, D_S5), (1, D_S5), (1, D_RWKV), (1, D_RWKV), (1, D_RWKV),
                  (D_MODEL, D_MODEL), (1, D_MODEL), (8, LANES)]

    def body(x_ref, u_ref, zs_ref, zr_ref, ysc_ref, r_ref, k2_ref, v_ref, yssm_ref,
             d_ref, gw_ref, gb_ref, lw_ref, lb_ref, rk_ref, wo_ref, gf_ref, tgt_ref, ee_ref,
             dx_o, du_o, dzs_o, dzr_o, dysc_o, dr_o, dk2_o, dv_o, dyssm_o,
             dd_o, dgw_o, dgb_o, dlw_o, dlb_o, drk_o, dwo_o, dgf_o, loss_o,
             dd, dgw, dgb, dlw, dlb, drk, dwo, dgf, loss):
        i = pl.program_id(0)
        accs = (dd, dgw, dgb, dlw, dlb, drk, dwo, dgf, loss)

        @pl.when(i == 0)
        def _():
            for acc in accs:
                acc[...] = jnp.zeros_like(acc)

        args = (x_ref[...], u_ref[...], zs_ref[...], zr_ref[...],
                _load_heads(ysc_ref), _load_heads(r_ref), _load_heads(k2_ref), _load_heads(v_ref), yssm_ref[...],
                jnp.zeros((tt, D_S5), F32), d_ref[...], gb_ref[...], lw_ref[...], lb_ref[...], rk_ref[...], gf_ref[...])
        rows, vjp, (y3, y_s5, y_rwkv) = jax.vjp(
            lambda *a: _post_fn(*a, gw_ref[...], wo_ref[0:D_S5, :], wo_ref[D_S5:D_MODEL, :], tgt_ref[...],
                                ee_ref[...]), *args, has_aux=True)
        g = vjp(jnp.ones_like(rows))
        for out, val in zip((dx_o, du_o, dzs_o, dzr_o), g[0:4]):
            out[...] = val
        _store_heads(dysc_o, g[4])
        for out, val in zip((dr_o, dk2_o, dv_o, dyssm_o), g[5:9]):
            out[...] = val
        for acc, val in zip((dd, dgb, dlw, dlb, drk, dgf), g[10:16]):
            acc[...] += val
        dgw[...] += _dot_bf(y3, g[9], ((0,), (0,)))
        dwo[0:D_S5, :] += _dot_bf(y_s5, g[0], ((0,), (0,)))
        dwo[D_S5:D_MODEL, :] += _dot_bf(y_rwkv, g[0], ((0,), (0,)))
        loss[...] += jnp.broadcast_to(jnp.sum(rows, axis=0, keepdims=True), loss.shape)

        @pl.when(i == n_t - 1)
        def _():
            for acc, out in zip(accs, (dd_o, dgw_o, dgb_o, dlw_o, dlb_o, drk_o, dwo_o, dgf_o, loss_o)):
                pltpu.sync_copy(acc, out)

    row = lambda n: pl.BlockSpec((tt, n), lambda i: (i, 0))
    in_specs = ([row(D_MODEL)] + [row(512)] * 3 + [_head_spec(tt)] * 4 + [row(D_S5)]
                + [_const_spec(s) for s in [(1, D_S5), (D_S5, D_S5), (1, D_S5), (1, D_RWKV), (1, D_RWKV), (1, D_RWKV),
                                            (D_MODEL, D_MODEL), (1, D_MODEL)]]
                + [row(D_MODEL), _const_spec((D_RWKV, D_RWKV))])
    out_rows = [D_MODEL] + [512] * 3 + [None] + [512] * 4
    return pl.pallas_call(
        body, name="post_fwd_bwd", grid=(n_t,), in_specs=in_specs,
        out_specs=[row(n) if n else _head_spec(tt) for n in out_rows] + [_ANY] * len(acc_shapes),
        out_shape=([_sds((L, n)) if n else _sds((N_PAIRS, L, LANES)) for n in out_rows]
                   + [_sds(s) for s in acc_shapes]),
        scratch_shapes=[pltpu.VMEM(s, F32) for s in acc_shapes],
        compiler_params=_params("arbitrary"),
    )(x, u, zs, zr, ysc, r, k2, v, y_ssm, d, glu_w, glu_b, ln_w, ln_b, r_k, w_out, gf, tgt, ee)


def _s5_bwd(u, du_direct, dy, s_re, s_im, b4_re, b4_im, c4_re, c4_im, pow_r, pow_i, tt):
    L = u.shape[0]
    n_t = L // tt
    acc_shapes = ([(S5_BLOCKS, LANES, 512)] * 2 + [(S5_BLOCKS, 512, LANES)] * 2 + [(1, N_STATE)] * 2)

    def body(u_ref, dud_ref, dy_ref, sre_ref, sim_ref, pre_ref, pim_ref, bre_ref, bim_ref, cre_ref, cim_ref,
             pr_ref, pi_ref, du_o, dbre_o, dbim_o, dcre_o, dcim_o, dlr_o, dli_o,
             dbre, dbim, dcre, dcim, dlr, dli, gre, gim, car_r, car_i):
        i = pl.program_id(0)

        @pl.when(i == 0)
        def _():
            for acc in (dbre, dbim, dcre, dcim, dlr, dli, car_r, car_i):
                acc[...] = jnp.zeros_like(acc)

        uv = u_ref[...]
        dyv = dy_ref[...]
        blocks = [slice(q * 512, (q + 1) * 512) for q in range(S5_BLOCKS)]
        lanes = [slice(q * LANES, (q + 1) * LANES) for q in range(S5_BLOCKS)]
        for q in range(S5_BLOCKS):
            gre[:, blocks[q]] = _dot_bf(dyv[:, lanes[q]], cre_ref[q], ((1,), (1,)))
            gim[:, blocks[q]] = -_dot_bf(dyv[:, lanes[q]], cim_ref[q], ((1,), (1,)))
        _tile_scan(gre, gim, pr_ref, pi_ref, car_r, car_i, reverse=True)
        for q in range(S5_BLOCKS):
            gr = gre[:, blocks[q]]
            gi = gim[:, blocks[q]]
            sr = sre_ref[:, blocks[q]]
            si = sim_ref[:, blocks[q]]
            du_o[:, lanes[q]] = (dud_ref[:, lanes[q]] + _dot_bf(gr, bre_ref[q], ((1,), (1,)))
                                 + _dot_bf(gi, bim_ref[q], ((1,), (1,))))
            dbre[q] += _dot_bf(uv[:, lanes[q]], gr, ((0,), (0,)))
            dbim[q] += _dot_bf(uv[:, lanes[q]], gi, ((0,), (0,)))
            dcre[q] += _dot_bf(sr, dyv[:, lanes[q]], ((0,), (0,)))
            dcim[q] -= _dot_bf(si, dyv[:, lanes[q]], ((0,), (0,)))
            rid = lax.broadcasted_iota(jnp.int32, sr.shape, 0)
            first = i == n_t - 1
            prev_r = jnp.where(first, 0.0, pre_ref[7:8, blocks[q]])
            prev_i = jnp.where(first, 0.0, pim_ref[7:8, blocks[q]])
            pr = jnp.where(rid == 0, jnp.broadcast_to(prev_r, sr.shape), pltpu.roll(sr, 1, axis=0))
            pi_ = jnp.where(rid == 0, jnp.broadcast_to(prev_i, si.shape), pltpu.roll(si, 1, axis=0))
            dlr[:, blocks[q]] += jnp.sum(pr * gr + pi_ * gi, axis=0, keepdims=True)
            dli[:, blocks[q]] += jnp.sum(pr * gi - pi_ * gr, axis=0, keepdims=True)

        @pl.when(i == n_t - 1)
        def _():
            for acc, out in zip((dbre, dbim, dcre, dcim, dlr, dli), (dbre_o, dbim_o, dcre_o, dcim_o, dlr_o, dli_o)):
                out[...] = acc[...]

    row = lambda n: pl.BlockSpec((tt, n), lambda i: (n_t - 1 - i, 0))
    prev = pl.BlockSpec((8, N_STATE), lambda i: (jnp.maximum((n_t - 1 - i) * (tt // 8) - 1, 0), 0))
    return pl.pallas_call(
        body, name="s5_bwd", grid=(n_t,),
        in_specs=[row(D_S5)] * 3 + [row(N_STATE)] * 2 + [prev] * 2
        + [_const_spec((S5_BLOCKS, LANES, 512))] * 2 + [_const_spec((S5_BLOCKS, 512, LANES))] * 2
        + [_const_spec((8, N_STATE))] * 2,
        out_specs=[row(D_S5)] + [_acc_spec(s) for s in acc_shapes],
        out_shape=[_sds((L, D_S5))] + [_sds(s) for s in acc_shapes],
        scratch_shapes=[pltpu.VMEM(s, F32) for s in acc_shapes] + [pltpu.VMEM((tt, N_STATE), F32)] * 2
        + [pltpu.VMEM((8, N_STATE), F32)] * 2,
        compiler_params=_params("arbitrary"),
    )(u, du_direct, dy, s_re, s_im, s_re, s_im, b4_re, b4_im, c4_re, c4_im, pow_r, pow_i)


def _bwd_in(x, norm_g, w_in_bf, mu, dx2, du, dzs, drws, dzr, tt):
    L = x.shape[0]
    n_t = L // tt

    def body(x_ref, g_ref, w_ref, mu_ref, dx2_ref, du_ref, dzs_ref, drws_ref, nxt_ref, dzr_ref,
             gx_o, dw_o, dg_o, dproj, dw, dg, stage):
        i = pl.program_id(0)

        @pl.when(i == 0)
        def _():
            dw[...] = jnp.zeros_like(dw)
            dg[...] = jnp.zeros_like(dg)

        drws_v = drws_ref[...]
        rid = lax.broadcasted_iota(jnp.int32, drws_v.shape, 0)
        nxt_row = jnp.where(i == n_t - 1, 0.0, nxt_ref[0:1, :])
        nxt = jnp.where(rid == tt - 1, jnp.broadcast_to(nxt_row, drws_v.shape), pltpu.roll(drws_v, tt - 1, axis=0))
        muv = mu_ref[...]
        drw = drws_v * (1.0 - muv) + nxt * muv
        dproj[:, 0:D_S5] = du_ref[...].astype(BF16)
        dproj[:, D_S5:2 * D_S5] = dzs_ref[...].astype(BF16)
        dproj[:, 2 * D_S5:2 * D_S5 + D_SHIFT] = drw.astype(BF16)
        dproj[:, 2 * D_S5 + D_SHIFT:D_IN] = dzr_ref[...].astype(BF16)
        dh = _dot(dproj[...], w_ref[...], ((1,), (0,)), None)
        h, vjp = jax.vjp(_rms, x_ref[...], g_ref[...])
        dxh, dgv = vjp(dh)
        gx_o[...] = dx2_ref[...] + dxh
        dg[...] += dgv
        dw[...] += _dot(dproj[...], h.astype(BF16), ((0,), (0,)), None)

        @pl.when(i == n_t - 1)
        def _():
            dg_o[...] = dg[...]
            for j in range(D_IN // stage.shape[0]):
                rows = pl.ds(j * stage.shape[0], stage.shape[0])
                stage[...] = dw[rows, :].astype(BF16)
                pltpu.sync_copy(stage, dw_o.at[rows])

    row = lambda n: pl.BlockSpec((tt, n), lambda i: (i, 0))
    nxt = pl.BlockSpec((8, D_SHIFT), lambda i: (jnp.minimum((i + 1) * (tt // 8), L // 8 - 1), 0))
    return pl.pallas_call(
        body, name="bwd_in", grid=(n_t,),
        in_specs=[row(D_MODEL), _const_spec((1, D_MODEL)), _const_spec((D_IN, D_MODEL)), _const_spec((1, D_SHIFT)),
                  row(D_MODEL), row(D_S5), row(D_S5), row(D_SHIFT), nxt, row(D_RWKV)],
        out_specs=[row(D_MODEL), _ANY, _acc_spec((1, D_MODEL))],
        out_shape=[_sds((L, D_MODEL)), jax.ShapeDtypeStruct((D_IN, D_MODEL), BF16), _sds((1, D_MODEL))],
        scratch_shapes=[pltpu.VMEM((tt, D_IN), BF16), pltpu.VMEM((D_IN, D_MODEL), F32), pltpu.VMEM((1, D_MODEL), F32),
                        pltpu.VMEM((D_IN // N_DEV, D_MODEL), BF16)],
        compiler_params=_params("arbitrary"),
    )(x, norm_g, w_in_bf, mu, dx2, du, dzs, drws, drws, dzr)


def _block_diag_b(bbar):
    bb = bbar.reshape(S5_GROUP, S5_BLOCKS, 8, S5_STATE)
    return jnp.einsum('hqgp,Gg->qGhgp', bb, jnp.eye(8, dtype=F32)).reshape(S5_BLOCKS, LANES, 512)


def _block_diag_b_t(db4):
    d = db4.reshape(S5_BLOCKS, 8, S5_GROUP, 8, S5_STATE)
    return jnp.einsum('qGhgp,Gg->hqgp', d, jnp.eye(8, dtype=F32)).reshape(S5_GROUP, N_STATE)


def _block_diag_c(c):
    cc = c.reshape(S5_BLOCKS, 8, S5_GROUP, S5_STATE)
    return jnp.einsum('qghp,gG->qgpGh', cc, jnp.eye(8, dtype=F32)).reshape(S5_BLOCKS, 512, LANES)


def _block_diag_c_t(dc4):
    d = dc4.reshape(S5_BLOCKS, 8, S5_STATE, 8, S5_GROUP)
    return jnp.einsum('qgpGh,gG->qghp', d, jnp.eye(8, dtype=F32)).reshape(S5_GROUPS, S5_GROUP, S5_STATE)


def _local_step(x, tgt, w):
    L = x.shape[0]
    tt = min(512, L)
    tp = min(256, L)
    ee = _head_sum_matrix()

    lam_re = w['s5_lam_re'].reshape(1, N_STATE)
    lam_im = w['s5_lam_im'].reshape(1, N_STATE)
    logdt = jnp.repeat(w['s5_log_dt'], S5_STATE).reshape(1, N_STATE)
    b_re_t = w['s5_b_re'].transpose(2, 0, 1).reshape(S5_GROUP, N_STATE)
    b_im_t = w['s5_b_im'].transpose(2, 0, 1).reshape(S5_GROUP, N_STATE)
    bbr, bbi, pow_r, pow_i, rpow_r, rpow_i = _s5_param_fwd(lam_re, lam_im, logdt, b_re_t, b_im_t)
    b4_re, b4_im = _block_diag_b(bbr), _block_diag_b(bbi)
    c4_re, c4_im = _block_diag_c(w['s5_c_re']), _block_diag_c(w['s5_c_im'])

    norm_g = w['norm_g'].reshape(1, D_MODEL)
    w_in_bf = (w['w_in_t'] if 'w_in_t' in w else w['w_in'].T).astype(BF16)
    u, zs, rw, zr = _fwd_in(x, norm_g, w_in_bf, tt)
    s_re, s_im, y_ssm = _s5_fwd(u, b4_re, b4_im, c4_re, c4_im, pow_r, pow_i, tt)

    row = lambda t: t.reshape(1, -1)
    zpad = jnp.zeros((HEAD, D_RWKV), F32)
    w2p = jnp.concatenate([w['rwkv_w2'], zpad], axis=0)
    a2p = jnp.concatenate([zpad, w['rwkv_a2']], axis=0)
    pre_consts = (row(w['rwkv_mu']), row(w['rwkv_w0']), w2p, row(w['rwkv_a0']), a2p,
                  row(w['rwkv_k_k']), row(w['rwkv_k_a']), ee)
    ops = _rwkv_pre_fwd(rw, *pre_consts, tt)
    ysc, states, *kept = _rwkv_scan_fwd(ops)

    post = _post(x, u, zs, zr, ysc, ops[0], ops[2], ops[3], y_ssm,
                 row(w['s5_d']), w['s5_glu_w'].astype(BF16), row(w['s5_glu_b']), row(w['rwkv_ln_w']), row(w['rwkv_ln_b']),
                 row(w['rwkv_r_k']), w['w_out'].astype(BF16), row(w['final_g']), tgt, ee, tp)
    (dx2, du_d, dzs, dzr, dysc, dr_b, dk2_b, dv_b, dy_ssm,
     dd, dglu_w, dglu_b, dln_w, dln_b, dr_k, dw_out, dgf, loss) = post

    du, db4_re, db4_im, dc4_re, dc4_im, dlbr, dlbi = _s5_bwd(
        u, du_d, dy_ssm, s_re, s_im, b4_re, b4_im, c4_re, c4_im, rpow_r, rpow_i, tt)
    group_ind = (jnp.arange(N_STATE)[:, None] // S5_STATE == jnp.arange(LANES)[None, :]).astype(F32)
    dlam_re, dlam_im, dlogdt, db_re_t, db_im_t = _s5_param_bwd(
        lam_re, lam_im, logdt, b_re_t, b_im_t, dlbr, dlbi, _block_diag_b_t(db4_re), _block_diag_b_t(db4_im), group_ind)

    cots = list(_rwkv_scan_bwd(ops, states, kept, dysc)) + [dr_b, dk2_b, dv_b]
    drws, dmu, dw0, dw2p, da0, da2p, dk_k, dk_a = _rwkv_pre_bwd(rw, *pre_consts, cots, tt)

    grad_x, dw_in, dnorm_g = _bwd_in(x, norm_g, w_in_bf, row(w['rwkv_mu']), dx2, du, dzs, drws, dzr, tt)

    unb = lambda t: t.reshape(S5_GROUP, S5_GROUPS, S5_STATE).transpose(1, 2, 0)
    grads = {
        'norm_g': dnorm_g.reshape(D_MODEL), 'w_in_t': dw_in,
        's5_lam_re': dlam_re.reshape(S5_GROUPS, S5_STATE), 's5_lam_im': dlam_im.reshape(S5_GROUPS, S5_STATE),
        's5_log_dt': dlogdt[0, :S5_GROUPS], 's5_b_re': unb(db_re_t), 's5_b_im': unb(db_im_t),
        's5_c_re': _block_diag_c_t(dc4_re), 's5_c_im': _block_diag_c_t(dc4_im),
        's5_d': dd.reshape(D_S5), 's5_glu_w': dglu_w, 's5_glu_b': dglu_b.reshape(D_S5),
        'rwkv_mu': dmu.reshape(-1), 'rwkv_w0': dw0.reshape(-1), 'rwkv_w2': dw2p[:HEAD], 'rwkv_a0': da0.reshape(-1),
        'rwkv_a2': da2p[HEAD:], 'rwkv_k_k': dk_k.reshape(-1), 'rwkv_k_a': dk_a.reshape(-1),
        'rwkv_r_k': dr_k.reshape(N_HEADS, HEAD), 'rwkv_ln_w': dln_w.reshape(-1), 'rwkv_ln_b': dln_b.reshape(-1),
        'w_out': dw_out, 'final_g': dgf.reshape(D_MODEL),
    }
    return loss, grad_x, grads


def _exchange(arrays, gather, axes, name):
    n = len(arrays)
    group = 2 ** len(axes)

    def body(*refs):
        send_refs, recv_refs = refs[:n], refs[n:2 * n]
        send_sems, recv_sems, local_sems = refs[2 * n:]
        pos = {ax: lax.axis_index(ax) for ax in ("x", "y", "c")}

        def index_of(p):
            idx = 0
            for ax in axes:
                idx = 2 * idx + p[ax]
            return idx

        me = index_of(pos)
        own, outs, arrivals = [], [], []
        for i, (send_ref, recv_ref) in enumerate(zip(send_refs, recv_refs)):
            def block_for(dev, send_ref=send_ref, whole=gather[i]):
                return send_ref if whole else send_ref.at[dev]

            own.append(pltpu.make_async_copy(block_for(me), recv_ref.at[me], local_sems.at[i]))
            own[-1].start()
            for k in range(1, group):
                peer = dict(pos)
                for bit, ax in enumerate(axes):
                    if (k >> bit) & 1:
                        peer[ax] = 1 - pos[ax]
                peer_idx = index_of(peer)
                sems = dict(send_sem=send_sems.at[i, k - 1], recv_sem=recv_sems.at[i, k - 1],
                            device_id=(peer["x"], peer["y"], peer["c"]), device_id_type=pl.DeviceIdType.MESH)
                outs.append(pltpu.make_async_remote_copy(src_ref=block_for(peer_idx), dst_ref=recv_ref.at[me], **sems))
                outs[-1].start()
                arrivals.append(
                    pltpu.make_async_remote_copy(src_ref=block_for(peer_idx), dst_ref=recv_ref.at[peer_idx], **sems))
        for copy in arrivals:
            copy.wait_recv()
        for copy in outs:
            copy.wait_send()
        for copy in own:
            copy.wait()

    return pl.pallas_call(
        body, name=name, in_specs=[_ANY] * n, out_specs=[_ANY] * n,
        out_shape=[jax.ShapeDtypeStruct(((group,) + a.shape) if whole else a.shape, a.dtype)
                   for a, whole in zip(arrays, gather)],
        scratch_shapes=[pltpu.SemaphoreType.DMA((n, group - 1)), pltpu.SemaphoreType.DMA((n, group - 1)),
                        pltpu.SemaphoreType.DMA((n,))],
        compiler_params=pltpu.CompilerParams(has_side_effects=True),
    )(*arrays)


def _sum_devices(ref):
    g = ref[0].astype(F32)
    for s in range(1, ref.shape[0]):
        g = g + ref[s].astype(F32)
    return g


def _adamw_math(g, w, m, v):
    m_new = ADAM_B1 * m + (1.0 - ADAM_B1) * g
    v_new = ADAM_B2 * v + (1.0 - ADAM_B2) * (g * g)
    m_hat = m_new / (1.0 - ADAM_B1 ** ADAM_STEP)
    v_hat = v_new / (1.0 - ADAM_B2 ** ADAM_STEP)
    return -ADAM_LR * (m_hat / (jnp.sqrt(v_hat) + ADAM_EPS) + ADAM_WD * w), m_new, v_new


def _adamw(gs, ws, ms, vs, reduce, name):
    n = len(ws)

    def body(*refs):
        g_refs, w_refs, m_refs, v_refs = (refs[j * n:(j + 1) * n] for j in range(4))
        outs = refs[4 * n:]
        for i in range(n):
            g = _sum_devices(g_refs[i]) if reduce else g_refs[i][...]
            res = _adamw_math(g, w_refs[i][...], m_refs[i][...], v_refs[i][...])
            for j, val in enumerate(((g,) if reduce else ()) + res):
                outs[j * n + i][...] = val

    return pl.pallas_call(
        body, name=name, out_shape=[_sds(w.shape) for w in ws] * (4 if reduce else 3),
        compiler_params=pltpu.CompilerParams(vmem_limit_bytes=VMEM_LIMIT),
    )(*gs, *ws, *ms, *vs)


def _sum_blocks(recv):
    def body(recv_ref, out_ref):
        out_ref[...] = _sum_devices(recv_ref)

    return pl.pallas_call(body, name="sum_small_grads", out_shape=_sds(recv.shape[1:]))(recv)


_WEIGHTS = [
    ('norm_g', (1, 1024), False), ('w_in', (1, 1024, 400), True), ('s5_lam_re', (1, 32, 64), False),
    ('s5_lam_im', (1, 32, 64), False), ('s5_log_dt', (1, 32), False), ('s5_b_re', (1, 32, 64, 16), False),
    ('s5_b_im', (1, 32, 64, 16), False), ('s5_c_re', (1, 32, 16, 64), False), ('s5_c_im', (1, 32, 16, 64), False),
    ('s5_d', (1, 512), False), ('s5_glu_w', (1, 64, 512), True), ('s5_glu_b', (1, 512), False),
    ('rwkv_mu', (1, 1664), False), ('rwkv_w0', (1, 512), False), ('rwkv_w2', (1, 64, 64), True),
    ('rwkv_a0', (1, 512), False), ('rwkv_a2', (1, 64, 64), True), ('rwkv_k_k', (1, 512), False),
    ('rwkv_k_a', (1, 512), False), ('rwkv_r_k', (1, 8, 64), False), ('rwkv_ln_w', (1, 512), False),
    ('rwkv_ln_b', (1, 512), False), ('w_out', (1, 128, 1024), True), ('final_g', (1024,), False),
]
_SHARDED = [(n, s) for n, s, sharded in _WEIGHTS if sharded]
_SMALL = [(n, s) for n, s, sharded in _WEIGHTS if not sharded]
_COLUMN_SHARDED = ('w_in', 'rwkv_w2', 'rwkv_a2')
_SMALL_SIZE = sum(math.prod(s) for _, s in _SMALL) + 1
_SMALL_ROWS = -(-_SMALL_SIZE // (8 * LANES)) * 8


_MINOR_SWAPPED = ('s5_b_re', 's5_b_im')


def _stored(name, t):
    return jnp.swapaxes(t, -1, -2) if name in _MINOR_SWAPPED else t


def _pack_small(grads, loss):
    flat = [_stored(n, grads[n]).reshape(-1) for n, _ in _SMALL] + [loss.reshape(1)]
    pad = _SMALL_ROWS * LANES - _SMALL_SIZE
    return jnp.concatenate(flat + [jnp.zeros((pad,), F32)]).reshape(_SMALL_ROWS, LANES)


def _unpack_small(packed):
    flat = packed.reshape(-1)
    out, off = {}, 0
    for n, s in _SMALL:
        size = math.prod(s)
        out[n] = flat[off:off + size].reshape(s[:-2] + (s[-1], s[-2]) if n in _MINOR_SWAPPED else s)
        off += size
    return out, flat[off]


_BF16_OPERANDS = ('w_in', 's5_glu_w', 'w_out')


def _join_shards(name, blocks):
    _, rows, cols = blocks.shape
    if name in _COLUMN_SHARDED:
        return blocks.transpose(1, 0, 2).reshape(rows, N_DEV * cols)
    return blocks.reshape(N_DEV * rows, cols)


def _split_shards(name, full, shard_shape):
    rows, cols = shard_shape
    if name in _COLUMN_SHARDED:
        return full.reshape(rows, N_DEV, cols).transpose(1, 0, 2)
    return full.reshape(N_DEV, rows, cols)


def kernel(x, norm_g, w_in, s5_lam_re, s5_lam_im, s5_log_dt, s5_b_re, s5_b_im, s5_c_re, s5_c_im, s5_d, s5_glu_w, s5_glu_b, rwkv_mu, rwkv_w0, rwkv_w2, rwkv_a0, rwkv_a2, rwkv_k_k, rwkv_k_a, rwkv_r_k, rwkv_ln_w, rwkv_ln_b, w_out, final_g, loss_target, m_norm_g, m_w_in, m_s5_lam_re, m_s5_lam_im, m_s5_log_dt, m_s5_b_re, m_s5_b_im, m_s5_c_re, m_s5_c_im, m_s5_d, m_s5_glu_w, m_s5_glu_b, m_rwkv_mu, m_rwkv_w0, m_rwkv_w2, m_rwkv_a0, m_rwkv_a2, m_rwkv_k_k, m_rwkv_k_a, m_rwkv_r_k, m_rwkv_ln_w, m_rwkv_ln_b, m_w_out, m_final_g, v_norm_g, v_w_in, v_s5_lam_re, v_s5_lam_im, v_s5_log_dt, v_s5_b_re, v_s5_b_im, v_s5_c_re, v_s5_c_im, v_s5_d, v_s5_glu_w, v_s5_glu_b, v_rwkv_mu, v_rwkv_w0, v_rwkv_w2, v_rwkv_a0, v_rwkv_a2, v_rwkv_k_k, v_rwkv_k_a, v_rwkv_r_k, v_rwkv_ln_w, v_rwkv_ln_b, v_w_out, v_final_g):
    given = dict(locals())

    n_sh = len(_SHARDED)
    everyone = ("x", "y", "c")
    shards = [given[n][0].astype(BF16 if n in _BF16_OPERANDS else F32) for n, _ in _SHARDED]
    shards[0] = shards[0].T
    gathered = _exchange(shards, (True,) * n_sh, everyone, "gather_weights")
    local = {n: _join_shards(n, blocks) for (n, _), blocks in list(zip(_SHARDED, gathered))[1:]}
    local['w_in_t'] = gathered[0].reshape(D_IN, D_MODEL)
    local.update({n: (given[n][0] if len(s) > 1 else given[n]) for n, s in _SMALL})

    loss, grad_x, grads = _local_step(x[0], loss_target[0], local)

    blocks = [grads['w_in_t'].reshape(N_DEV, D_IN // N_DEV, D_MODEL)]
    blocks += [_split_shards(n, grads[n], s[1:]).astype(BF16) for n, s in _SHARDED[1:]]
    small = _pack_small(grads, loss[0, 0]).reshape(N_DEV, _SMALL_ROWS // N_DEV, LANES)
    recv = _exchange(blocks + [small], (False,) * (n_sh + 1), everyone, "exchange_grads")
    small_sum = _exchange([_sum_blocks(recv[-1])], (True,), everyone, "gather_small_grads")[0]

    result = {}
    for group, name in (([0], "adamw_w_in"), ([1, 2, 3, 4], "adamw_shards")):
        ns = [_SHARDED[i][0] for i in group]
        own = (lambda t: t[0].T) if group == [0] else (lambda t: t[0])
        back = (lambda t: t.T[None]) if group == [0] else (lambda t: t[None])
        res = _adamw([recv[i] for i in group], [own(given[n]) for n in ns], [own(given['m_' + n]) for n in ns],
                     [own(given['v_' + n]) for n in ns], True, name)
        for j, n in enumerate(ns):
            result[n] = [back(res[k * len(ns) + j]) for k in range(4)]
    g_small, total = _unpack_small(small_sum)
    two_d = lambda t: t.reshape(1, -1) if t.ndim == 1 else t
    ns = [n for n, _ in _SMALL]
    res = _adamw([two_d(g_small[n]) for n in ns], [two_d(_stored(n, given[n])) for n in ns],
                 [two_d(_stored(n, given['m_' + n])) for n in ns], [two_d(_stored(n, given['v_' + n])) for n in ns],
                 False, "adamw_small")
    for j, (n, s) in enumerate(_SMALL):
        result[n] = [_stored(n, g_small[n])] + [_stored(n, res[k * len(ns) + j]).reshape(s) for k in range(3)]

    outs = [total, grad_x[None]]
    for k in range(4):
        outs += [result[n][k] for n, _, _ in _WEIGHTS]
    return tuple(outs)
```

```python
import math

import jax
import jax.numpy as jnp
from jax import lax
from jax.experimental import pallas as pl
from jax.experimental.pallas import tpu as pltpu

F32 = jnp.float32
BF16 = jnp.bfloat16
HI = lax.Precision.HIGH

D_MODEL = 1024
D_S5 = 512
D_RWKV = 512
S5_GROUPS = 32
S5_GROUP = 16
S5_STATE = 64
N_STATE = S5_GROUPS * S5_STATE
N_HEADS = 8
HEAD = 64
D_SHIFT = 3 * D_RWKV + 128
D_IN = 2 * D_S5 + D_SHIFT + D_RWKV
NORM_EPS = 1e-6
GN_EPS = 64e-5
N_DEV = 8
LANES = 128
S5_BLOCKS = 4
RWKV_CHUNK = 64
RWKV_CHUNKS_PER_STEP = 4
VMEM_LIMIT = 56 * 1024 * 1024

ADAM_LR = 0.001
ADAM_B1 = 0.9
ADAM_B2 = 0.999
ADAM_EPS = 1e-08
ADAM_WD = 0.01
ADAM_STEP = 10


def _dot(a, b, dims, prec):
    return lax.dot_general(a, b, (dims, ((), ())), precision=prec, preferred_element_type=F32)


def _dot_bf(a, b, dims):
    return _dot(a.astype(BF16), b.astype(BF16), dims, None)


def _make_mm(cast, prec):
    @jax.custom_vjp
    def mm(a, b):
        return _dot(cast(a), cast(b), ((1,), (0,)), prec)

    def fwd(a, b):
        return mm(a, b), (a, b)

    def bwd(res, g):
        a, b = res
        return (_dot(cast(g), cast(b), ((1,), (1,)), prec), _dot(cast(a), cast(g), ((0,), (0,)), prec))

    mm.defvjp(fwd, bwd)
    return mm


mm_bf = _make_mm(lambda t: t.astype(BF16), None)


@jax.custom_vjp
def mm_w(a, w):
    return _dot(a.astype(BF16), w, ((1,), (0,)), None)


def _mm_w_fwd(a, w):
    return mm_w(a, w), w


def _mm_w_bwd(w, g):
    return _dot(g.astype(BF16), w, ((1,), (1,)), None), jnp.zeros_like(w)


mm_w.defvjp(_mm_w_fwd, _mm_w_bwd)


def _make_head_sum(split):
    def product(x, ee):
        hi = x.astype(BF16)
        out = _dot(hi, ee, ((1,), (0,)), None)
        if split:
            out = out + _dot((x - hi.astype(F32)).astype(BF16), ee, ((1,), (0,)), None)
        return out

    @jax.custom_vjp
    def head_sum(x, ee):
        return product(x, ee)

    def fwd(x, ee):
        return product(x, ee), ee

    def bwd(ee, g):
        return product(g, ee), jnp.zeros_like(ee)

    head_sum.defvjp(fwd, bwd)
    return head_sum


head_sum = _make_head_sum(False)
head_sum_split = _make_head_sum(True)


@jax.custom_vjp
def _sigmoid(x):
    return 1.0 / (1.0 + jnp.exp(-x))


def _sigmoid_fwd(x):
    s = _sigmoid(x)
    return s, s


_sigmoid.defvjp(_sigmoid_fwd, lambda s, g: (g * s * (1.0 - s),))


@jax.custom_vjp
def _silu(x):
    return x * _sigmoid(x)


def _silu_fwd(x):
    s = _sigmoid(x)
    return x * s, (x, s)


def _silu_bwd(res, g):
    x, s = res
    return (g * s * (1.0 + x * (1.0 - s)),)


_silu.defvjp(_silu_fwd, _silu_bwd)


@jax.custom_vjp
def _softplus(x):
    return jnp.maximum(x, 0.0) + jnp.log(1.0 + jnp.exp(-jnp.abs(x)))


def _softplus_fwd(x):
    e = jnp.exp(-jnp.abs(x))
    return jnp.maximum(x, 0.0) + jnp.log(1.0 + e), (x, e)


def _softplus_bwd(res, g):
    x, e = res
    return (g * jnp.where(x >= 0.0, 1.0, e) / (1.0 + e),)


_softplus.defvjp(_softplus_fwd, _softplus_bwd)


@jax.custom_vjp
def _normalize_heads(x, ee):
    return x / jnp.maximum(jnp.sqrt(head_sum(x * x, ee)), 1e-12)


def _normalize_heads_fwd(x, ee):
    norm = jnp.sqrt(head_sum(x * x, ee))
    inv = 1.0 / jnp.maximum(norm, 1e-12)
    y = x * inv
    return y, (y, inv, norm, ee)


def _normalize_heads_bwd(res, g):
    y, inv, norm, ee = res
    along = jnp.where(norm > 1e-12, head_sum(g * y, ee), 0.0)
    return inv * (g - y * along), jnp.zeros_like(ee)


_normalize_heads.defvjp(_normalize_heads_fwd, _normalize_heads_bwd)


_GELU_C = 2.0 * math.sqrt(2.0 / math.pi)


def _gelu_gate(x):
    return 1.0 / (1.0 + jnp.exp(-_GELU_C * x * (1.0 + 0.044715 * (x * x))))


@jax.custom_vjp
def _gelu(x):
    return x * _gelu_gate(x)


def _gelu_fwd(x):
    s = _gelu_gate(x)
    return x * s, (x, s)


def _gelu_bwd(res, g):
    x, s = res
    return (g * (s + x * s * (1.0 - s) * (_GELU_C * (1.0 + 3.0 * 0.044715 * (x * x)))),)


_gelu.defvjp(_gelu_fwd, _gelu_bwd)


def _rms(x, g):
    return x * lax.rsqrt(jnp.mean(x * x, axis=-1, keepdims=True) + NORM_EPS) * g


def _const_spec(shape):
    nd = len(shape)
    return pl.BlockSpec(shape, lambda *_: (0,) * nd, pipeline_mode=pl.Buffered(1))


def _acc_spec(shape):
    nd = len(shape)
    return pl.BlockSpec(shape, lambda *_: (0,) * nd)


def _params(sem):
    return pltpu.CompilerParams(dimension_semantics=(sem,), vmem_limit_bytes=VMEM_LIMIT)


_ANY = pl.BlockSpec(memory_space=pl.ANY)


def _sds(shape):
    return jax.ShapeDtypeStruct(shape, F32)


def _head_sum_matrix():
    i = jnp.arange(D_RWKV) // HEAD
    return (i[:, None] == i[None, :]).astype(BF16)


def _s5_param_fn(lam_re, lam_im, logdt, b_re, b_im):
    dt = jnp.exp(logdt)
    mag = jnp.exp(lam_re * dt)
    ang = lam_im * dt
    lbr = mag * jnp.cos(ang)
    lbi = mag * jnp.sin(ang)
    nr = lbr - 1.0
    den = lam_re * lam_re + lam_im * lam_im
    cr = (nr * lam_re + lbi * lam_im) / den
    ci = (lbi * lam_re - nr * lam_im) / den
    return lbr, lbi, cr * b_re - ci * b_im, cr * b_im + ci * b_re


def _cmul(ar, ai, br, bi):
    return ar * br - ai * bi, ar * bi + ai * br


def _s5_param_fwd(lam_re, lam_im, logdt, b_re, b_im):
    def body(lr, li, ld, br, bi, o_br, o_bi, o_pr, o_pi, o_qr, o_qi):
        lbr, lbi, bbr, bbi = _s5_param_fn(lr[...], li[...], ld[...], br[...], bi[...])
        o_br[...] = bbr
        o_bi[...] = bbi
        rid = lax.broadcasted_iota(jnp.int32, (8, N_STATE), 0)
        pr, pi_ = lbr, lbi
        fwd_r = rev_r = jnp.broadcast_to(pr, (8, N_STATE))
        fwd_i = rev_i = jnp.broadcast_to(pi_, (8, N_STATE))
        for j in range(1, 8):
            pr, pi_ = _cmul(pr, pi_, lbr, lbi)
            fwd_r = jnp.where(rid == j, jnp.broadcast_to(pr, (8, N_STATE)), fwd_r)
            fwd_i = jnp.where(rid == j, jnp.broadcast_to(pi_, (8, N_STATE)), fwd_i)
            rev_r = jnp.where(rid == 7 - j, jnp.broadcast_to(pr, (8, N_STATE)), rev_r)
            rev_i = jnp.where(rid == 7 - j, jnp.broadcast_to(pi_, (8, N_STATE)), rev_i)
        o_pr[...] = fwd_r
        o_pi[...] = fwd_i
        o_qr[...] = rev_r
        o_qi[...] = -rev_i

    return pl.pallas_call(
        body, name="s5_param_fwd",
        out_shape=[_sds((S5_GROUP, N_STATE))] * 2 + [_sds((8, N_STATE))] * 4,
    )(lam_re, lam_im, logdt, b_re, b_im)


def _s5_param_bwd(lam_re, lam_im, logdt, b_re, b_im, d_lbr, d_lbi, d_bbr, d_bbi, group_ind):
    def body(lr, li, ld, br, bi, g0, g1, g2, g3, ind, o_lr, o_li, o_ld, o_br, o_bi):
        _, vjp = jax.vjp(_s5_param_fn, lr[...], li[...], ld[...], br[...], bi[...])
        d_lr, d_li, d_ld, d_br, d_bi = vjp((g0[...], g1[...], g2[...], g3[...]))
        o_lr[...] = d_lr
        o_li[...] = d_li
        o_ld[...] = _dot(jnp.broadcast_to(d_ld, (8, N_STATE)), ind[...], ((1,), (0,)), HI)
        o_br[...] = d_br
        o_bi[...] = d_bi

    return pl.pallas_call(
        body, name="s5_param_bwd",
        out_shape=[_sds((1, N_STATE))] * 2 + [_sds((8, LANES))] + [_sds((S5_GROUP, N_STATE))] * 2,
    )(lam_re, lam_im, logdt, b_re, b_im, d_lbr, d_lbi, d_bbr, d_bbi, group_ind)


def _fwd_in(x, norm_g, w_in_bf, tt, late=()):
    L = x.shape[0]
    n_t = L // tt
    n_late = len(late)
    everyone = ("x", "y", "c")

    def body(x_ref, g_ref, w_ref, *refs):
        u_ref, zs_ref, rw_ref, zr_ref = refs[n_late:n_late + 4]
        if n_late:
            i = pl.program_id(0)
            copies = lambda: _exchange_copies(refs[:n_late], refs[n_late + 4:2 * n_late + 4], *refs[2 * n_late + 4:],
                                              (True,) * n_late, everyone)
            pl.when(i == 0)(lambda: _start_copies(copies()))
            pl.when(i == n_t - 1)(lambda: _wait_copies(copies()))
        h = _rms(x_ref[...], g_ref[...])
        proj = _dot(h.astype(BF16), w_ref[...], ((1,), (1,)), None)
        u_ref[...] = proj[:, 0:D_S5]
        zs_ref[...] = proj[:, D_S5:2 * D_S5]
        rw_ref[...] = proj[:, 2 * D_S5:2 * D_S5 + D_SHIFT]
        zr_ref[...] = proj[:, 2 * D_S5 + D_SHIFT:D_IN]

    row = lambda n: pl.BlockSpec((tt, n), lambda i: (i, 0))
    return pl.pallas_call(
        body, name="fwd_in", grid=(n_t,),
        in_specs=[row(D_MODEL), _const_spec((1, D_MODEL)), _const_spec((D_IN, D_MODEL))] + [_ANY] * n_late,
        out_specs=[row(D_S5), row(D_S5), row(D_SHIFT), row(D_RWKV)] + [_ANY] * n_late,
        out_shape=[_sds((L, D_S5)), _sds((L, D_S5)), _sds((L, D_SHIFT)), _sds((L, D_RWKV))]
        + _exchange_results(late, (True,) * n_late, everyone),
        scratch_shapes=_exchange_scratch(n_late, everyone) if n_late else [],
        compiler_params=pltpu.CompilerParams(dimension_semantics=("arbitrary",), vmem_limit_bytes=VMEM_LIMIT,
                                             has_side_effects=bool(n_late)),
    )(x, norm_g, w_in_bf, *late)


S5_LANE_CHUNK = 512


def _tile_scan(re_ref, im_ref, pow_r_ref, pow_i_ref, carry_r_ref, carry_i_ref, reverse):
    t, n = re_ref.shape
    n_groups = t // 8
    ch = S5_LANE_CHUNK
    rid = lax.broadcasted_iota(jnp.int32, (8, ch), 0)
    for c in range(n // ch):
        cols = slice(c * ch, (c + 1) * ch)
        pow_r = pow_r_ref[:, cols]
        pow_i = pow_i_ref[:, cols]
        row = lambda tile, j: jnp.broadcast_to(tile[j:j + 1], (8, ch))
        levels = []
        for d in (1, 2, 4):
            keep = (rid < 8 - d) if reverse else (rid >= d)
            j = 8 - d if reverse else d - 1
            levels.append(((8 - d) if reverse else d,
                           jnp.where(keep, row(pow_r, j), 0.0), jnp.where(keep, row(pow_i, j), 0.0)))

        def group(g, carry):
            r0 = pl.multiple_of(((n_groups - 1 - g) if reverse else g) * 8, 8)
            xr = re_ref[pl.ds(r0, 8), cols]
            xi = im_ref[pl.ds(r0, 8), cols]
            for shift, lr, li in levels:
                mr, mi = _cmul(lr, li, pltpu.roll(xr, shift, axis=0), pltpu.roll(xi, shift, axis=0))
                xr = xr + mr
                xi = xi + mi
            mr, mi = _cmul(pow_r, pow_i, carry[0], carry[1])
            xr = xr + mr
            xi = xi + mi
            re_ref[pl.ds(r0, 8), cols] = xr
            im_ref[pl.ds(r0, 8), cols] = xi
            last = 0 if reverse else 7
            return row(xr, last), row(xi, last)

        out = lax.fori_loop(0, n_groups, group, (carry_r_ref[:, cols], carry_i_ref[:, cols]))
        carry_r_ref[:, cols] = out[0]
        carry_i_ref[:, cols] = out[1]


def _s5_fwd(u, b4_re, b4_im, c4_re, c4_im, pow_r, pow_i, tt):
    L = u.shape[0]

    def body(u_ref, bre_ref, bim_ref, cre_ref, cim_ref, pr_ref, pi_ref, sre_o, sim_o, y_o, car_r, car_i):
        @pl.when(pl.program_id(0) == 0)
        def _():
            car_r[...] = jnp.zeros_like(car_r)
            car_i[...] = jnp.zeros_like(car_i)

        uv = u_ref[...]
        for q in range(S5_BLOCKS):
            uq = uv[:, q * LANES:(q + 1) * LANES]
            cols = slice(q * 512, (q + 1) * 512)
            sre_o[:, cols] = _dot_bf(uq, bre_ref[q], ((1,), (0,)))
            sim_o[:, cols] = _dot_bf(uq, bim_ref[q], ((1,), (0,)))
        _tile_scan(sre_o, sim_o, pr_ref, pi_ref, car_r, car_i, reverse=False)
        for q in range(S5_BLOCKS):
            cols = slice(q * 512, (q + 1) * 512)
            y_o[:, q * LANES:(q + 1) * LANES] = (_dot_bf(sre_o[:, cols], cre_ref[q], ((1,), (0,)))
                                                 - _dot_bf(sim_o[:, cols], cim_ref[q], ((1,), (0,))))

    row = lambda n: pl.BlockSpec((tt, n), lambda i: (i, 0))
    return pl.pallas_call(
        body, name="s5_fwd", grid=(L // tt,),
        in_specs=[row(D_S5)] + [_const_spec((S5_BLOCKS, LANES, 512))] * 2 + [_const_spec((S5_BLOCKS, 512, LANES))] * 2
        + [_const_spec((8, N_STATE))] * 2,
        out_specs=[row(N_STATE), row(N_STATE), row(D_S5)],
        out_shape=[_sds((L, N_STATE)), _sds((L, N_STATE)), _sds((L, D_S5))],
        scratch_shapes=[pltpu.VMEM((8, N_STATE), F32)] * 2,
        compiler_params=_params("arbitrary"),
    )(u, b4_re, b4_im, c4_re, c4_im, pow_r, pow_i)


def _rwkv_pre_fn(r, k, v, wa, w0, w2p, a0, a2p, k_k, k_a, ee):
    w = -_softplus(-(w0 + mm_bf(jnp.tanh(wa), w2p))) - 0.5
    logw = -jnp.exp(w)
    a = _sigmoid(a0 + mm_bf(wa, a2p))
    kk = _normalize_heads(k * k_k, ee)
    k2 = k * (1.0 + (a - 1.0) * k_a)
    return r, logw, k2, v, -kk, kk * a


N_PAIRS = N_HEADS // 2


def _head_spec(tt):
    return pl.BlockSpec((N_PAIRS, tt, LANES), lambda i: (0, i, 0))


def _load_heads(ref):
    return jnp.concatenate([ref[p] for p in range(N_PAIRS)], axis=-1)


def _store_heads(ref, val):
    for p in range(N_PAIRS):
        ref[p] = val[:, p * LANES:(p + 1) * LANES]


def _split_pairs(x):
    return jnp.concatenate([x[:, :, :HEAD], x[:, :, HEAD:]], axis=0)


def _join_pairs(x):
    return jnp.concatenate([x[:N_PAIRS], x[N_PAIRS:]], axis=-1)


def _shifted(rw, prev_blk, first):
    rolled = pltpu.roll(rw, 1, axis=0)
    prev_row = jnp.where(first, 0.0, prev_blk[7:8, :])
    rid = lax.broadcasted_iota(jnp.int32, rw.shape, 0)
    return jnp.where(rid == 0, jnp.broadcast_to(prev_row, rw.shape), rolled)


def _split_rw(t):
    return t[:, 0:512], t[:, 512:1024], t[:, 1024:1536], t[:, 1536:1664]


def _rwkv_pre_specs(tt):
    row = pl.BlockSpec((tt, D_SHIFT), lambda i: (i, 0))
    prev = pl.BlockSpec((8, D_SHIFT), lambda i: (jnp.maximum(i * (tt // 8) - 1, 0), 0))
    consts = [_const_spec((1, D_SHIFT)), _const_spec((1, D_RWKV)), _const_spec((LANES, D_RWKV)),
              _const_spec((1, D_RWKV)), _const_spec((LANES, D_RWKV)), _const_spec((1, D_RWKV)),
              _const_spec((1, D_RWKV)), _const_spec((D_RWKV, D_RWKV))]
    return [row, prev] + consts


def _rwkv_pre_fwd(rw, mu, w0, w2p, a0, a2p, k_k, k_a, ee, tt):
    L = rw.shape[0]

    def body(rw_ref, prev_ref, mu_ref, w0_ref, w2_ref, a0_ref, a2_ref, kk_ref, ka_ref, ee_ref, *outs):
        rwv = rw_ref[...]
        rws = rwv + (_shifted(rwv, prev_ref[...], pl.program_id(0) == 0) - rwv) * mu_ref[...]
        res = _rwkv_pre_fn(*_split_rw(rws), w0_ref[...], w2_ref[...], a0_ref[...], a2_ref[...],
                           kk_ref[...], ka_ref[...], ee_ref[...])
        for o, val in zip(outs, res):
            _store_heads(o, val)

    return pl.pallas_call(
        body, name="rwkv_pre_fwd", grid=(L // tt,),
        in_specs=_rwkv_pre_specs(tt), out_specs=[_head_spec(tt)] * 6, out_shape=[_sds((N_PAIRS, L, LANES))] * 6,
        compiler_params=_params("parallel"),
    )(rw, rw, mu, w0, w2p, a0, a2p, k_k, k_a, ee)


def _rwkv_pre_bwd(rw, mu, w0, w2p, a0, a2p, k_k, k_a, ee, cots, tt):
    L = rw.shape[0]
    n_t = L // tt

    def body(rw_ref, prev_ref, mu_ref, w0_ref, w2_ref, a0_ref, a2_ref, kk_ref, ka_ref, ee_ref,
             c_r, c_w, c_k, c_v, c_a, c_b, cb_r, cb_k, cb_v,
             drws_ref, dmu_o, dw0_o, dw2_o, da0_o, da2_o, dkk_o, dka_o,
             dmu, dw0, dw2, da0, da2, dkk, dka):
        i = pl.program_id(0)
        accs = (dmu, dw0, dw2, da0, da2, dkk, dka)

        @pl.when(i == 0)
        def _():
            for acc in accs:
                acc[...] = jnp.zeros_like(acc)

        rwv = rw_ref[...]
        diff = _shifted(rwv, prev_ref[...], i == 0) - rwv
        rws = rwv + diff * mu_ref[...]
        consts = (w0_ref[...], w2_ref[...], a0_ref[...], a2_ref[...], kk_ref[...], ka_ref[...])
        _, vjp = jax.vjp(lambda *a: _rwkv_pre_fn(*a, ee_ref[...]), *_split_rw(rws), *consts)
        scan = [_load_heads(c) for c in (c_r, c_w, c_k, c_v, c_a, c_b)]
        g = vjp((scan[0] + cb_r[...], scan[1], scan[2] + cb_k[...], scan[3] + cb_v[...], scan[4], scan[5]))
        drws = jnp.concatenate(g[0:4], axis=-1)
        drws_ref[...] = drws
        dmu[...] += jnp.sum(drws * diff, axis=0, keepdims=True)
        for acc, val in zip(accs[1:], g[4:]):
            acc[...] += val

        @pl.when(i == n_t - 1)
        def _():
            for acc, out in zip(accs, (dmu_o, dw0_o, dw2_o, da0_o, da2_o, dkk_o, dka_o)):
                out[...] = acc[...]

    row = pl.BlockSpec((tt, D_RWKV), lambda i: (i, 0))
    shapes = [(1, D_SHIFT), (1, D_RWKV), (LANES, D_RWKV), (1, D_RWKV), (LANES, D_RWKV), (1, D_RWKV), (1, D_RWKV)]
    return pl.pallas_call(
        body, name="rwkv_pre_bwd", grid=(n_t,),
        in_specs=_rwkv_pre_specs(tt) + [_head_spec(tt)] * 6 + [row] * 3,
        out_specs=[pl.BlockSpec((tt, D_SHIFT), lambda i: (i, 0))] + [_acc_spec(s) for s in shapes],
        out_shape=[_sds((L, D_SHIFT))] + [_sds(s) for s in shapes],
        scratch_shapes=[pltpu.VMEM(s, F32) for s in shapes],
        compiler_params=_params("arbitrary"),
    )(rw, rw, mu, w0, w2p, a0, a2p, k_k, k_a, ee, *cots)


def _bmm(a, b):
    return lax.dot_general(a, b, (((2,), (1,)), ((0,), (0,))), precision=HI, preferred_element_type=F32)


def _bmm_nt(a, b):
    return lax.dot_general(a, b, (((2,), (2,)), ((0,), (0,))), precision=HI, preferred_element_type=F32)


def _bmm_tn(a, b):
    return lax.dot_general(a, b, (((1,), (1,)), ((0,), (0,))), precision=HI, preferred_element_type=F32)


def _bdot_bf(a, b, lhs_dim, rhs_dim):
    return lax.dot_general(a.astype(BF16), b.astype(BF16), (((lhs_dim,), (rhs_dim,)), ((0,), (0,))),
                           preferred_element_type=F32)


@jax.custom_vjp
def _bmm_bf(a, b):
    return _bdot_bf(a, b, 2, 1)


def _bmm_bf_fwd(a, b):
    return _bmm_bf(a, b), (a, b)


def _bmm_bf_bwd(res, g):
    a, b = res
    return _bdot_bf(g, b, 2, 2), _bdot_bf(a, g, 1, 1)


_bmm_bf.defvjp(_bmm_bf_fwd, _bmm_bf_bwd)


@jax.custom_vjp
def _bmm_tn_bf(a, b):
    return _bdot_bf(a, b, 1, 1)


def _bmm_tn_bf_fwd(a, b):
    return _bmm_tn_bf(a, b), (a, b)


def _bmm_tn_bf_bwd(res, g):
    a, b = res
    return _bdot_bf(b, g, 2, 2), _bdot_bf(a, g, 2, 1)


_bmm_tn_bf.defvjp(_bmm_tn_bf_fwd, _bmm_tn_bf_bwd)


def _unit_lower_inverse(a):
    t = a.shape[-1]
    ti = lax.broadcasted_iota(jnp.int32, (t, t), 0)
    si = lax.broadcasted_iota(jnp.int32, (t, t), 1)

    def same_block(bits):
        shift = jnp.int32(bits)
        return (lax.shift_right_logical(ti, shift) == lax.shift_right_logical(si, shift))[None]

    def mm(x, y):
        return _bdot_bf(x, y, 2, 1)

    d = jnp.where(same_block(3), a, 0.0)
    inv = jnp.where(ti == si, 1.0, 0.0)[None] + d
    pw = mm(d, d)
    both = mm(jnp.concatenate([inv, pw], axis=1), pw)
    inv = inv + both[:, :t]
    inv = inv + mm(inv, both[:, t:])
    bits = 3
    while (1 << bits) < t:
        e = jnp.where(same_block(bits), 0.0, jnp.where(same_block(bits + 1), a, 0.0))
        inv = inv + mm(mm(inv, e), inv)
        bits += 1
    return inv


def _tri_mask(t):
    ri = lax.broadcasted_iota(jnp.int32, (2 * t, 2 * t), 0)
    ci = lax.broadcasted_iota(jnp.int32, (2 * t, 2 * t), 1)
    top_rows = ri < t
    diff = jnp.where(top_rows, ri, ri - t) - jnp.where(ci < t, ci, ci - t)
    return (diff >= jnp.where(top_rows, 1, 0))[None]


def _ones_tri(n_h, t):
    ti = lax.broadcasted_iota(jnp.int32, (t, t), 0)
    si = lax.broadcasted_iota(jnp.int32, (t, t), 1)
    return jnp.broadcast_to(jnp.where(ti >= si, 1.0, 0.0)[None], (n_h, t, t))


@jax.custom_vjp
def _running_sum_kept(logw, kept):
    return kept


def _running_sum_kept_bwd(shape, g):
    return _bmm_tn(_ones_tri(shape[0], shape[1]), g), jnp.zeros_like(g)


_running_sum_kept.defvjp(lambda logw, kept: (kept, logw.shape), _running_sum_kept_bwd)


@jax.custom_vjp
def _tri_products_kept(ar, bk, kept):
    return kept


def _tri_products_kept_bwd(res, g):
    ar, bk = res
    g = jnp.where(_tri_mask(ar.shape[1] // 2), g, 0.0)
    return _bmm(g, bk), _bmm_tn(g, ar), jnp.zeros_like(g)


_tri_products_kept.defvjp(lambda ar, bk, kept: (kept, (ar, bk)), _tri_products_kept_bwd)


@jax.custom_vjp
def _solve_unit_lower(a, rhs, inv, kept=None):
    return _bmm(inv, rhs) if kept is None else kept


def _solve_fwd(a, rhs, inv, kept=None):
    u = _bmm(inv, rhs) if kept is None else kept
    return u, (inv, u, kept is not None)


def _solve_bwd(res, du):
    inv, u, had_kept = res
    d_rhs = _bmm_tn(inv, du)
    return _bmm_nt(d_rhs, u), d_rhs, jnp.zeros_like(inv), (jnp.zeros_like(u) if had_kept else None)


_solve_unit_lower.defvjp(_solve_fwd, _solve_bwd)


def _rwkv_chunk(st0, r, logw, k, v, a, b, kept=None):
    n_h, t, _ = r.shape
    log_p = _bmm(_ones_tri(n_h, t), logw) if kept is None else _running_sum_kept(logw, kept[0])
    p_in = jnp.exp(log_p)
    p_inv = jnp.exp(-log_p)
    at = a * jnp.exp(log_p - logw)
    rt = r * p_in
    ar = jnp.concatenate([at, rt], axis=1)
    bk = jnp.concatenate([b * p_inv, k * p_inv], axis=1)
    if kept is None:
        m = jnp.where(_tri_mask(t), _bmm_nt(ar, bk), 0.0)
        inv = _unit_lower_inverse(m[:, :t, :t])
    else:
        m = _tri_products_kept(ar, bk, kept[1])
        inv = kept[2]
    top, bottom = m[:, :t], m[:, t:]
    rhs = _bmm_bf(jnp.concatenate([at, top[:, :, t:]], axis=2), jnp.concatenate([st0, v], axis=1))
    u = _solve_unit_lower(top[:, :, :t], rhs, inv, None if kept is None else kept[3])
    y = _bmm_bf(jnp.concatenate([rt, bottom], axis=2), jnp.concatenate([st0, u, v], axis=1))
    p_end = jnp.swapaxes(p_in[:, t - 1:t, :], 1, 2)
    st1 = (st0 + _bmm_tn_bf(bk, jnp.concatenate([u, v], axis=1))) * p_end
    return y, st1, (log_p, m, inv, u)


def _rwkv_scan_fwd(ops):
    n_h, L, n = N_HEADS, ops[0].shape[1], HEAD
    t = RWKV_CHUNK
    per = min(RWKV_CHUNKS_PER_STEP, L // t)
    n_c = L // t
    n_s = n_c // per

    def body(r_ref, w_ref, k_ref, v_ref, a_ref, b_ref, y_ref, st_ref, logp_ref, m_ref, inv_ref, u_ref, st):
        @pl.when(pl.program_id(0) == 0)
        def _():
            st[...] = jnp.zeros_like(st)

        st0 = st[...]
        for j in range(per):
            rows = slice(j * t, (j + 1) * t)
            st_ref[j] = st0
            y, st0, (log_p, m, inv, u) = _rwkv_chunk(
                st0, *(_split_pairs(ref[:, rows, :]) for ref in (r_ref, w_ref, k_ref, v_ref, a_ref, b_ref)))
            y_ref[:, rows, :] = _join_pairs(y)
            logp_ref[:, rows, :] = log_p
            u_ref[:, rows, :] = u
            m_ref[j] = m
            inv_ref[j] = inv
        st[...] = st0

    pairs = pl.BlockSpec((N_PAIRS, per * t, LANES), lambda c: (0, c, 0))
    blk = pl.BlockSpec((n_h, per * t, n), lambda c: (0, c, 0))
    per_chunk = lambda m: pl.BlockSpec((per, n_h, m, m), lambda c: (c, 0, 0, 0))
    return pl.pallas_call(
        body, name="rwkv_scan_fwd", grid=(n_s,), in_specs=[pairs] * 6,
        out_specs=[pairs, per_chunk(n), blk, per_chunk(2 * t), per_chunk(t), blk],
        out_shape=[_sds((N_PAIRS, L, LANES)), _sds((n_c, n_h, n, n)), _sds((n_h, L, n)),
                   _sds((n_c, n_h, 2 * t, 2 * t)), _sds((n_c, n_h, t, t)), _sds((n_h, L, n))],
        scratch_shapes=[pltpu.VMEM((n_h, n, n), F32)],
        compiler_params=_params("arbitrary"),
    )(*ops)


def _rwkv_scan_bwd(ops, states, kept, dy):
    n_h, L, n = N_HEADS, ops[0].shape[1], HEAD
    t = RWKV_CHUNK
    per = min(RWKV_CHUNKS_PER_STEP, L // t)
    n_s = L // t // per

    def body(r_ref, w_ref, k_ref, v_ref, a_ref, b_ref, st_ref, logp_ref, m_ref, inv_ref, u_ref, dy_ref,
             dr, dw, dk, dv, da, db, dst):
        @pl.when(pl.program_id(0) == 0)
        def _():
            dst[...] = jnp.zeros_like(dst)

        vjps = []
        for j in range(per):
            rows = slice(j * t, (j + 1) * t)
            have = (logp_ref[:, rows, :], m_ref[j], inv_ref[j], u_ref[:, rows, :])
            args = [_split_pairs(ref[:, rows, :]) for ref in (r_ref, w_ref, k_ref, v_ref, a_ref, b_ref)]
            vjps.append(jax.vjp(lambda *a, have=have: _rwkv_chunk(*a, kept=have)[:2], st_ref[j], *args)[1])
        d_state = dst[...]
        for j in reversed(range(per)):
            rows = slice(j * t, (j + 1) * t)
            g = vjps[j]((_split_pairs(dy_ref[:, rows, :]), d_state))
            d_state = g[0]
            for out, val in zip((dr, dw, dk, dv, da, db), g[1:]):
                out[:, rows, :] = _join_pairs(val)
        dst[...] = d_state

    pairs = pl.BlockSpec((N_PAIRS, per * t, LANES), lambda c: (0, n_s - 1 - c, 0))
    blk = pl.BlockSpec((n_h, per * t, n), lambda c: (0, n_s - 1 - c, 0))
    per_chunk = lambda m: pl.BlockSpec((per, n_h, m, m), lambda c: (n_s - 1 - c, 0, 0, 0))
    return pl.pallas_call(
        body, name="rwkv_scan_bwd", grid=(n_s,),
        in_specs=[pairs] * 6 + [per_chunk(n), blk, per_chunk(2 * t), per_chunk(t), blk, pairs],
        out_specs=[pairs] * 6, out_shape=[_sds((N_PAIRS, L, LANES))] * 6,
        scratch_shapes=[pltpu.VMEM((n_h, n, n), F32)],
        compiler_params=_params("arbitrary"),
    )(*ops, states, *kept, dy)


def _post_fn(x, u, zs, zr, ysc, r, k2, v, y_ssm, gate_in, d, glu_b, ln_w, ln_b, r_k, gf,
              glu_w, wo_s5, wo_rwkv, tgt, ee):
    y3 = _gelu(y_ssm + d * u)
    y_s5 = y3 * _sigmoid(mm_w(y3, glu_w) + glu_b + gate_in) * _silu(zs)
    mean = head_sum_split(ysc, ee) * (1.0 / HEAD)
    yc = ysc - mean
    var = head_sum(yc * yc, ee) * (1.0 / HEAD)
    gn = yc * lax.rsqrt(var + GN_EPS) * ln_w + ln_b
    bonus = head_sum(r * k2 * r_k, ee) * v
    y_rwkv = (gn + bonus) * _silu(zr)
    x2 = x + mm_w(y_s5, wo_s5) + mm_w(y_rwkv, wo_rwkv)
    err = _rms(x2, gf) - tgt
    return 0.5 * jnp.mean(err * err, axis=-1, keepdims=True), (y3, y_s5, y_rwkv)


def _post(x, u, zs, zr, ysc, r, k2, v, y_ssm, d, glu_w, glu_b, ln_w, ln_b, r_k, w_out, gf, tgt, ee, tt):
    L = x.shape[0]
    n_t = L // tt
    acc_shapes = [(1, D_S5), (D_S5, D_S5), (1, D_S5), (1, D_RWKV), (1, D_RWKV), (1, D_RWKV),
                  (D_MODEL, D_MODEL), (1, D_MODEL), (8, LANES)]

    def body(x_ref, u_ref, zs_ref, zr_ref, ysc_ref, r_ref, k2_ref, v_ref, yssm_ref,
             d_ref, gw_ref, gb_ref, lw_ref, lb_ref, rk_ref, wo_ref, gf_ref, tgt_ref, ee_ref,
             dx_o, du_o, dzs_o, dzr_o, dysc_o, dr_o, dk2_o, dv_o, dyssm_o,
             dd_o, dgw_o, dgb_o, dlw_o, dlb_o, drk_o, dwo_o, dgf_o, loss_o,
             dd, dgw, dgb, dlw, dlb, drk, dwo, dgf, loss):
        i = pl.program_id(0)
        accs = (dd, dgw, dgb, dlw, dlb, drk, dwo, dgf, loss)

        @pl.when(i == 0)
        def _():
            for acc in accs:
                acc[...] = jnp.zeros_like(acc)

        args = (x_ref[...], u_ref[...], zs_ref[...], zr_ref[...],
                _load_heads(ysc_ref), _load_heads(r_ref), _load_heads(k2_ref), _load_heads(v_ref), yssm_ref[...],
                jnp.zeros((tt, D_S5), F32), d_ref[...], gb_ref[...], lw_ref[...], lb_ref[...], rk_ref[...], gf_ref[...])
        rows, vjp, (y3, y_s5, y_rwkv) = jax.vjp(
            lambda *a: _post_fn(*a, gw_ref[...], wo_ref[0:D_S5, :], wo_ref[D_S5:D_MODEL, :], tgt_ref[...],
                                ee_ref[...]), *args, has_aux=True)
        g = vjp(jnp.ones_like(rows))
        for out, val in zip((dx_o, du_o, dzs_o, dzr_o), g[0:4]):
            out[...] = val
        _store_heads(dysc_o, g[4])
        for out, val in zip((dr_o, dk2_o, dv_o, dyssm_o), g[5:9]):
            out[...] = val
        for acc, val in zip((dd, dgb, dlw, dlb, drk, dgf), g[10:16]):
            acc[...] += val
        dgw[...] += _dot_bf(y3, g[9], ((0,), (0,)))
        dwo[0:D_S5, :] += _dot_bf(y_s5, g[0], ((0,), (0,)))
        dwo[D_S5:D_MODEL, :] += _dot_bf(y_rwkv, g[0], ((0,), (0,)))
        loss[...] += jnp.broadcast_to(jnp.sum(rows, axis=0, keepdims=True), loss.shape)

        @pl.when(i == n_t - 1)
        def _():
            for acc, out in zip(accs, (dd_o, dgw_o, dgb_o, dlw_o, dlb_o, drk_o, dwo_o, dgf_o, loss_o)):
                pltpu.sync_copy(acc, out)

    row = lambda n: pl.BlockSpec((tt, n), lambda i: (i, 0))
    in_specs = ([row(D_MODEL)] + [row(512)] * 3 + [_head_spec(tt)] * 4 + [row(D_S5)]
                + [_const_spec(s) for s in [(1, D_S5), (D_S5, D_S5), (1, D_S5), (1, D_RWKV), (1, D_RWKV), (1, D_RWKV),
                                            (D_MODEL, D_MODEL), (1, D_MODEL)]]
                + [row(D_MODEL), _const_spec((D_RWKV, D_RWKV))])
    out_rows = [D_MODEL] + [512] * 3 + [None] + [512] * 4
    return pl.pallas_call(
        body, name="post_fwd_bwd", grid=(n_t,), in_specs=in_specs,
        out_specs=[row(n) if n else _head_spec(tt) for n in out_rows] + [_ANY] * len(acc_shapes),
        out_shape=([_sds((L, n)) if n else _sds((N_PAIRS, L, LANES)) for n in out_rows]
                   + [_sds(s) for s in acc_shapes]),
        scratch_shapes=[pltpu.VMEM(s, F32) for s in acc_shapes],
        compiler_params=_params("arbitrary"),
    )(x, u, zs, zr, ysc, r, k2, v, y_ssm, d, glu_w, glu_b, ln_w, ln_b, r_k, w_out, gf, tgt, ee)


def _s5_bwd(u, du_direct, dy, s_re, s_im, b4_re, b4_im, c4_re, c4_im, pow_r, pow_i, tt):
    L = u.shape[0]
    n_t = L // tt
    acc_shapes = ([(S5_BLOCKS, LANES, 512)] * 2 + [(S5_BLOCKS, 512, LANES)] * 2 + [(1, N_STATE)] * 2)

    def body(u_ref, dud_ref, dy_ref, sre_ref, sim_ref, pre_ref, pim_ref, bre_ref, bim_ref, cre_ref, cim_ref,
             pr_ref, pi_ref, du_o, dbre_o, dbim_o, dcre_o, dcim_o, dlr_o, dli_o,
             dbre, dbim, dcre, dcim, dlr, dli, gre, gim, car_r, car_i):
        i = pl.program_id(0)

        @pl.when(i == 0)
        def _():
            for acc in (dbre, dbim, dcre, dcim, dlr, dli, car_r, car_i):
                acc[...] = jnp.zeros_like(acc)

        uv = u_ref[...]
        dyv = dy_ref[...]
        blocks = [slice(q * 512, (q + 1) * 512) for q in range(S5_BLOCKS)]
        lanes = [slice(q * LANES, (q + 1) * LANES) for q in range(S5_BLOCKS)]
        for q in range(S5_BLOCKS):
            gre[:, blocks[q]] = _dot_bf(dyv[:, lanes[q]], cre_ref[q], ((1,), (1,)))
            gim[:, blocks[q]] = -_dot_bf(dyv[:, lanes[q]], cim_ref[q], ((1,), (1,)))
        _tile_scan(gre, gim, pr_ref, pi_ref, car_r, car_i, reverse=True)
        for q in range(S5_BLOCKS):
            gr = gre[:, blocks[q]]
            gi = gim[:, blocks[q]]
            sr = sre_ref[:, blocks[q]]
            si = sim_ref[:, blocks[q]]
            du_o[:, lanes[q]] = (dud_ref[:, lanes[q]] + _dot_bf(gr, bre_ref[q], ((1,), (1,)))
                                 + _dot_bf(gi, bim_ref[q], ((1,), (1,))))
            dbre[q] += _dot_bf(uv[:, lanes[q]], gr, ((0,), (0,)))
            dbim[q] += _dot_bf(uv[:, lanes[q]], gi, ((0,), (0,)))
            dcre[q] += _dot_bf(sr, dyv[:, lanes[q]], ((0,), (0,)))
            dcim[q] -= _dot_bf(si, dyv[:, lanes[q]], ((0,), (0,)))
            rid = lax.broadcasted_iota(jnp.int32, sr.shape, 0)
            first = i == n_t - 1
            prev_r = jnp.where(first, 0.0, pre_ref[7:8, blocks[q]])
            prev_i = jnp.where(first, 0.0, pim_ref[7:8, blocks[q]])
            pr = jnp.where(rid == 0, jnp.broadcast_to(prev_r, sr.shape), pltpu.roll(sr, 1, axis=0))
            pi_ = jnp.where(rid == 0, jnp.broadcast_to(prev_i, si.shape), pltpu.roll(si, 1, axis=0))
            dlr[:, blocks[q]] += jnp.sum(pr * gr + pi_ * gi, axis=0, keepdims=True)
            dli[:, blocks[q]] += jnp.sum(pr * gi - pi_ * gr, axis=0, keepdims=True)

        @pl.when(i == n_t - 1)
        def _():
            for acc, out in zip((dbre, dbim, dcre, dcim, dlr, dli), (dbre_o, dbim_o, dcre_o, dcim_o, dlr_o, dli_o)):
                out[...] = acc[...]

    row = lambda n: pl.BlockSpec((tt, n), lambda i: (n_t - 1 - i, 0))
    prev = pl.BlockSpec((8, N_STATE), lambda i: (jnp.maximum((n_t - 1 - i) * (tt // 8) - 1, 0), 0))
    return pl.pallas_call(
        body, name="s5_bwd", grid=(n_t,),
        in_specs=[row(D_S5)] * 3 + [row(N_STATE)] * 2 + [prev] * 2
        + [_const_spec((S5_BLOCKS, LANES, 512))] * 2 + [_const_spec((S5_BLOCKS, 512, LANES))] * 2
        + [_const_spec((8, N_STATE))] * 2,
        out_specs=[row(D_S5)] + [_acc_spec(s) for s in acc_shapes],
        out_shape=[_sds((L, D_S5))] + [_sds(s) for s in acc_shapes],
        scratch_shapes=[pltpu.VMEM(s, F32) for s in acc_shapes] + [pltpu.VMEM((tt, N_STATE), F32)] * 2
        + [pltpu.VMEM((8, N_STATE), F32)] * 2,
        compiler_params=_params("arbitrary"),
    )(u, du_direct, dy, s_re, s_im, s_re, s_im, b4_re, b4_im, c4_re, c4_im, pow_r, pow_i)


def _bwd_in(x, norm_g, w_in_bf, mu, dx2, du, dzs, drws, dzr, tt):
    L = x.shape[0]
    n_t = L // tt

    def body(x_ref, g_ref, w_ref, mu_ref, dx2_ref, du_ref, dzs_ref, drws_ref, nxt_ref, dzr_ref,
             gx_o, dw_o, dg_o, dproj, dw, dg, stage):
        i = pl.program_id(0)

        @pl.when(i == 0)
        def _():
            dw[...] = jnp.zeros_like(dw)
            dg[...] = jnp.zeros_like(dg)

        drws_v = drws_ref[...]
        rid = lax.broadcasted_iota(jnp.int32, drws_v.shape, 0)
        nxt_row = jnp.where(i == n_t - 1, 0.0, nxt_ref[0:1, :])
        nxt = jnp.where(rid == tt - 1, jnp.broadcast_to(nxt_row, drws_v.shape), pltpu.roll(drws_v, tt - 1, axis=0))
        muv = mu_ref[...]
        drw = drws_v * (1.0 - muv) + nxt * muv
        dproj[:, 0:D_S5] = du_ref[...].astype(BF16)
        dproj[:, D_S5:2 * D_S5] = dzs_ref[...].astype(BF16)
        dproj[:, 2 * D_S5:2 * D_S5 + D_SHIFT] = drw.astype(BF16)
        dproj[:, 2 * D_S5 + D_SHIFT:D_IN] = dzr_ref[...].astype(BF16)
        dh = _dot(dproj[...], w_ref[...], ((1,), (0,)), None)
        h, vjp = jax.vjp(_rms, x_ref[...], g_ref[...])
        dxh, dgv = vjp(dh)
        gx_o[...] = dx2_ref[...] + dxh
        dg[...] += dgv
        dw[...] += _dot(dproj[...], h.astype(BF16), ((0,), (0,)), None)

        @pl.when(i == n_t - 1)
        def _():
            dg_o[...] = dg[...]
            for j in range(D_IN // stage.shape[0]):
                rows = pl.ds(j * stage.shape[0], stage.shape[0])
                stage[...] = dw[rows, :].astype(BF16)
                pltpu.sync_copy(stage, dw_o.at[rows])

    row = lambda n: pl.BlockSpec((tt, n), lambda i: (i, 0))
    nxt = pl.BlockSpec((8, D_SHIFT), lambda i: (jnp.minimum((i + 1) * (tt // 8), L // 8 - 1), 0))
    return pl.pallas_call(
        body, name="bwd_in", grid=(n_t,),
        in_specs=[row(D_MODEL), _const_spec((1, D_MODEL)), _const_spec((D_IN, D_MODEL)), _const_spec((1, D_SHIFT)),
                  row(D_MODEL), row(D_S5), row(D_S5), row(D_SHIFT), nxt, row(D_RWKV)],
        out_specs=[row(D_MODEL), _ANY, _acc_spec((1, D_MODEL))],
        out_shape=[_sds((L, D_MODEL)), jax.ShapeDtypeStruct((D_IN, D_MODEL), BF16), _sds((1, D_MODEL))],
        scratch_shapes=[pltpu.VMEM((tt, D_IN), BF16), pltpu.VMEM((D_IN, D_MODEL), F32), pltpu.VMEM((1, D_MODEL), F32),
                        pltpu.VMEM((D_IN // N_DEV, D_MODEL), BF16)],
        compiler_params=_params("arbitrary"),
    )(x, norm_g, w_in_bf, mu, dx2, du, dzs, drws, drws, dzr)


def _block_diag_b(bbar):
    bb = bbar.reshape(S5_GROUP, S5_BLOCKS, 8, S5_STATE)
    return jnp.einsum('hqgp,Gg->qGhgp', bb, jnp.eye(8, dtype=F32)).reshape(S5_BLOCKS, LANES, 512)


def _block_diag_b_t(db4):
    d = db4.reshape(S5_BLOCKS, 8, S5_GROUP, 8, S5_STATE)
    return jnp.einsum('qGhgp,Gg->hqgp', d, jnp.eye(8, dtype=F32)).reshape(S5_GROUP, N_STATE)


def _block_diag_c(c):
    cc = c.reshape(S5_BLOCKS, 8, S5_GROUP, S5_STATE)
    return jnp.einsum('qghp,gG->qgpGh', cc, jnp.eye(8, dtype=F32)).reshape(S5_BLOCKS, 512, LANES)


def _block_diag_c_t(dc4):
    d = dc4.reshape(S5_BLOCKS, 8, S5_STATE, 8, S5_GROUP)
    return jnp.einsum('qgpGh,gG->qghp', d, jnp.eye(8, dtype=F32)).reshape(S5_GROUPS, S5_GROUP, S5_STATE)


def _local_step(x, tgt, w, late=()):
    L = x.shape[0]
    tt = min(512, L)
    tp = min(256, L)
    ee = _head_sum_matrix()

    lam_re = w['s5_lam_re'].reshape(1, N_STATE)
    lam_im = w['s5_lam_im'].reshape(1, N_STATE)
    logdt = jnp.repeat(w['s5_log_dt'], S5_STATE).reshape(1, N_STATE)
    b_re_t = w['s5_b_re'].transpose(2, 0, 1).reshape(S5_GROUP, N_STATE)
    b_im_t = w['s5_b_im'].transpose(2, 0, 1).reshape(S5_GROUP, N_STATE)
    bbr, bbi, pow_r, pow_i, rpow_r, rpow_i = _s5_param_fwd(lam_re, lam_im, logdt, b_re_t, b_im_t)
    b4_re, b4_im = _block_diag_b(bbr), _block_diag_b(bbi)
    c4_re, c4_im = _block_diag_c(w['s5_c_re']), _block_diag_c(w['s5_c_im'])

    norm_g = w['norm_g'].reshape(1, D_MODEL)
    w_in_bf = (w['w_in_t'] if 'w_in_t' in w else w['w_in'].T).astype(BF16)
    u, zs, rw, zr, *gathered = _fwd_in(x, norm_g, w_in_bf, tt, [shard for _, shard in late])
    w = dict(w, **{n: _join_shards(n, blocks) for (n, _), blocks in zip(late, gathered)})
    s_re, s_im, y_ssm = _s5_fwd(u, b4_re, b4_im, c4_re, c4_im, pow_r, pow_i, tt)

    row = lambda t: t.reshape(1, -1)
    zpad = jnp.zeros((HEAD, D_RWKV), F32)
    w2p = jnp.concatenate([w['rwkv_w2'], zpad], axis=0)
    a2p = jnp.concatenate([zpad, w['rwkv_a2']], axis=0)
    pre_consts = (row(w['rwkv_mu']), row(w['rwkv_w0']), w2p, row(w['rwkv_a0']), a2p,
                  row(w['rwkv_k_k']), row(w['rwkv_k_a']), ee)
    ops = _rwkv_pre_fwd(rw, *pre_consts, tt)
    ysc, states, *kept = _rwkv_scan_fwd(ops)

    post = _post(x, u, zs, zr, ysc, ops[0], ops[2], ops[3], y_ssm,
                 row(w['s5_d']), w['s5_glu_w'].astype(BF16), row(w['s5_glu_b']), row(w['rwkv_ln_w']), row(w['rwkv_ln_b']),
                 row(w['rwkv_r_k']), w['w_out'].astype(BF16), row(w['final_g']), tgt, ee, tp)
    (dx2, du_d, dzs, dzr, dysc, dr_b, dk2_b, dv_b, dy_ssm,
     dd, dglu_w, dglu_b, dln_w, dln_b, dr_k, dw_out, dgf, loss) = post

    du, db4_re, db4_im, dc4_re, dc4_im, dlbr, dlbi = _s5_bwd(
        u, du_d, dy_ssm, s_re, s_im, b4_re, b4_im, c4_re, c4_im, rpow_r, rpow_i, tt)
    group_ind = (jnp.arange(N_STATE)[:, None] // S5_STATE == jnp.arange(LANES)[None, :]).astype(F32)
    dlam_re, dlam_im, dlogdt, db_re_t, db_im_t = _s5_param_bwd(
        lam_re, lam_im, logdt, b_re_t, b_im_t, dlbr, dlbi, _block_diag_b_t(db4_re), _block_diag_b_t(db4_im), group_ind)

    cots = list(_rwkv_scan_bwd(ops, states, kept, dysc)) + [dr_b, dk2_b, dv_b]
    drws, dmu, dw0, dw2p, da0, da2p, dk_k, dk_a = _rwkv_pre_bwd(rw, *pre_consts, cots, tt)

    grad_x, dw_in, dnorm_g = _bwd_in(x, norm_g, w_in_bf, row(w['rwkv_mu']), dx2, du, dzs, drws, dzr, tt)

    unb = lambda t: t.reshape(S5_GROUP, S5_GROUPS, S5_STATE).transpose(1, 2, 0)
    grads = {
        'norm_g': dnorm_g.reshape(D_MODEL), 'w_in_t': dw_in,
        's5_lam_re': dlam_re.reshape(S5_GROUPS, S5_STATE), 's5_lam_im': dlam_im.reshape(S5_GROUPS, S5_STATE),
        's5_log_dt': dlogdt[0, :S5_GROUPS], 's5_b_re': unb(db_re_t), 's5_b_im': unb(db_im_t),
        's5_c_re': _block_diag_c_t(dc4_re), 's5_c_im': _block_diag_c_t(dc4_im),
        's5_d': dd.reshape(D_S5), 's5_glu_w': dglu_w, 's5_glu_b': dglu_b.reshape(D_S5),
        'rwkv_mu': dmu.reshape(-1), 'rwkv_w0': dw0.reshape(-1), 'rwkv_w2': dw2p[:HEAD], 'rwkv_a0': da0.reshape(-1),
        'rwkv_a2': da2p[HEAD:], 'rwkv_k_k': dk_k.reshape(-1), 'rwkv_k_a': dk_a.reshape(-1),
        'rwkv_r_k': dr_k.reshape(N_HEADS, HEAD), 'rwkv_ln_w': dln_w.reshape(-1), 'rwkv_ln_b': dln_b.reshape(-1),
        'w_out': dw_out, 'final_g': dgf.reshape(D_MODEL),
    }
    return loss, grad_x, grads


def _exchange(arrays, gather, axes, name):
    n = len(arrays)

    def body(*refs):
        copies = _exchange_copies(refs[:n], refs[n:2 * n], *refs[2 * n:], gather, axes)
        _start_copies(copies)
        _wait_copies(copies)

    return pl.pallas_call(
        body, name=name, in_specs=[_ANY] * n, out_specs=[_ANY] * n,
        out_shape=_exchange_results(arrays, gather, axes), scratch_shapes=_exchange_scratch(n, axes),
        compiler_params=pltpu.CompilerParams(has_side_effects=True),
    )(*arrays)


def _exchange_results(arrays, gather, axes):
    return [jax.ShapeDtypeStruct(((2 ** len(axes),) + a.shape) if whole else a.shape, a.dtype)
            for a, whole in zip(arrays, gather)]


def _exchange_scratch(n, axes):
    peers = 2 ** len(axes) - 1
    return [pltpu.SemaphoreType.DMA((n, peers)), pltpu.SemaphoreType.DMA((n, peers)), pltpu.SemaphoreType.DMA((n,))]


def _exchange_copies(send_refs, recv_refs, send_sems, recv_sems, local_sems, gather, axes):
    pos = {ax: lax.axis_index(ax) for ax in ("x", "y", "c")}

    def index_of(p):
        idx = 0
        for ax in axes:
            idx = 2 * idx + p[ax]
        return idx

    me = index_of(pos)
    own, outs, arrivals = [], [], []
    for i, (send_ref, recv_ref) in enumerate(zip(send_refs, recv_refs)):
        def block_for(dev, send_ref=send_ref, whole=gather[i]):
            return send_ref if whole else send_ref.at[dev]

        own.append(pltpu.make_async_copy(block_for(me), recv_ref.at[me], local_sems.at[i]))
        for k in range(1, 2 ** len(axes)):
            peer = dict(pos)
            for bit, ax in enumerate(axes):
                if (k >> bit) & 1:
                    peer[ax] = 1 - pos[ax]
            peer_idx = index_of(peer)
            sems = dict(send_sem=send_sems.at[i, k - 1], recv_sem=recv_sems.at[i, k - 1],
                        device_id=(peer["x"], peer["y"], peer["c"]), device_id_type=pl.DeviceIdType.MESH)
            outs.append(pltpu.make_async_remote_copy(src_ref=block_for(peer_idx), dst_ref=recv_ref.at[me], **sems))
            arrivals.append(
                pltpu.make_async_remote_copy(src_ref=block_for(peer_idx), dst_ref=recv_ref.at[peer_idx], **sems))
    return own, outs, arrivals


def _start_copies(copies):
    own, outs, _ = copies
    for copy in own + outs:
        copy.start()


def _wait_copies(copies):
    own, outs, arrivals = copies
    for copy in arrivals:
        copy.wait_recv()
    for copy in outs:
        copy.wait_send()
    for copy in own:
        copy.wait()


def _sum_devices(ref):
    g = ref[0].astype(F32)
    for s in range(1, ref.shape[0]):
        g = g + ref[s].astype(F32)
    return g


def _adamw_math(g, w, m, v):
    m_new = ADAM_B1 * m + (1.0 - ADAM_B1) * g
    v_new = ADAM_B2 * v + (1.0 - ADAM_B2) * (g * g)
    m_hat = m_new / (1.0 - ADAM_B1 ** ADAM_STEP)
    v_hat = v_new / (1.0 - ADAM_B2 ** ADAM_STEP)
    return -ADAM_LR * (m_hat / (jnp.sqrt(v_hat) + ADAM_EPS) + ADAM_WD * w), m_new, v_new


def _adamw(gs, ws, ms, vs, reduce, name):
    n = len(ws)

    def body(*refs):
        g_refs, w_refs, m_refs, v_refs = (refs[j * n:(j + 1) * n] for j in range(4))
        outs = refs[4 * n:]
        for i in range(n):
            g = _sum_devices(g_refs[i]) if reduce else g_refs[i][...]
            res = _adamw_math(g, w_refs[i][...], m_refs[i][...], v_refs[i][...])
            for j, val in enumerate(((g,) if reduce else ()) + res):
                outs[j * n + i][...] = val

    return pl.pallas_call(
        body, name=name, out_shape=[_sds(w.shape) for w in ws] * (4 if reduce else 3),
        compiler_params=pltpu.CompilerParams(vmem_limit_bytes=VMEM_LIMIT),
    )(*gs, *ws, *ms, *vs)


def _sum_blocks(recv):
    def body(recv_ref, out_ref):
        out_ref[...] = _sum_devices(recv_ref)

    return pl.pallas_call(body, name="sum_small_grads", out_shape=_sds(recv.shape[1:]))(recv)


_WEIGHTS = [
    ('norm_g', (1, 1024), False), ('w_in', (1, 1024, 400), True), ('s5_lam_re', (1, 32, 64), False),
    ('s5_lam_im', (1, 32, 64), False), ('s5_log_dt', (1, 32), False), ('s5_b_re', (1, 32, 64, 16), False),
    ('s5_b_im', (1, 32, 64, 16), False), ('s5_c_re', (1, 32, 16, 64), False), ('s5_c_im', (1, 32, 16, 64), False),
    ('s5_d', (1, 512), False), ('s5_glu_w', (1, 64, 512), True), ('s5_glu_b', (1, 512), False),
    ('rwkv_mu', (1, 1664), False), ('rwkv_w0', (1, 512), False), ('rwkv_w2', (1, 64, 64), True),
    ('rwkv_a0', (1, 512), False), ('rwkv_a2', (1, 64, 64), True), ('rwkv_k_k', (1, 512), False),
    ('rwkv_k_a', (1, 512), False), ('rwkv_r_k', (1, 8, 64), False), ('rwkv_ln_w', (1, 512), False),
    ('rwkv_ln_b', (1, 512), False), ('w_out', (1, 128, 1024), True), ('final_g', (1024,), False),
]
_SHARDED = [(n, s) for n, s, sharded in _WEIGHTS if sharded]
_SMALL = [(n, s) for n, s, sharded in _WEIGHTS if not sharded]
_COLUMN_SHARDED = ('w_in', 'rwkv_w2', 'rwkv_a2')
_SMALL_SIZE = sum(math.prod(s) for _, s in _SMALL) + 1
_SMALL_ROWS = -(-_SMALL_SIZE // (8 * LANES)) * 8


_MINOR_SWAPPED = ('s5_b_re', 's5_b_im')


def _stored(name, t):
    return jnp.swapaxes(t, -1, -2) if name in _MINOR_SWAPPED else t


def _pack_small(grads, loss):
    flat = [_stored(n, grads[n]).reshape(-1) for n, _ in _SMALL] + [loss.reshape(1)]
    pad = _SMALL_ROWS * LANES - _SMALL_SIZE
    return jnp.concatenate(flat + [jnp.zeros((pad,), F32)]).reshape(_SMALL_ROWS, LANES)


def _unpack_small(packed):
    flat = packed.reshape(-1)
    out, off = {}, 0
    for n, s in _SMALL:
        size = math.prod(s)
        out[n] = flat[off:off + size].reshape(s[:-2] + (s[-1], s[-2]) if n in _MINOR_SWAPPED else s)
        off += size
    return out, flat[off]


_BF16_OPERANDS = ('w_in', 's5_glu_w', 'w_out')
_LATE_WEIGHTS = ('s5_glu_w', 'w_out')


def _join_shards(name, blocks):
    _, rows, cols = blocks.shape
    if name in _COLUMN_SHARDED:
        return blocks.transpose(1, 0, 2).reshape(rows, N_DEV * cols)
    return blocks.reshape(N_DEV * rows, cols)


def _split_shards(name, full, shard_shape):
    rows, cols = shard_shape
    if name in _COLUMN_SHARDED:
        return full.reshape(rows, N_DEV, cols).transpose(1, 0, 2)
    return full.reshape(N_DEV, rows, cols)


def kernel(x, norm_g, w_in, s5_lam_re, s5_lam_im, s5_log_dt, s5_b_re, s5_b_im, s5_c_re, s5_c_im, s5_d, s5_glu_w, s5_glu_b, rwkv_mu, rwkv_w0, rwkv_w2, rwkv_a0, rwkv_a2, rwkv_k_k, rwkv_k_a, rwkv_r_k, rwkv_ln_w, rwkv_ln_b, w_out, final_g, loss_target, m_norm_g, m_w_in, m_s5_lam_re, m_s5_lam_im, m_s5_log_dt, m_s5_b_re, m_s5_b_im, m_s5_c_re, m_s5_c_im, m_s5_d, m_s5_glu_w, m_s5_glu_b, m_rwkv_mu, m_rwkv_w0, m_rwkv_w2, m_rwkv_a0, m_rwkv_a2, m_rwkv_k_k, m_rwkv_k_a, m_rwkv_r_k, m_rwkv_ln_w, m_rwkv_ln_b, m_w_out, m_final_g, v_norm_g, v_w_in, v_s5_lam_re, v_s5_lam_im, v_s5_log_dt, v_s5_b_re, v_s5_b_im, v_s5_c_re, v_s5_c_im, v_s5_d, v_s5_glu_w, v_s5_glu_b, v_rwkv_mu, v_rwkv_w0, v_rwkv_w2, v_rwkv_a0, v_rwkv_a2, v_rwkv_k_k, v_rwkv_k_a, v_rwkv_r_k, v_rwkv_ln_w, v_rwkv_ln_b, v_w_out, v_final_g):
    given = dict(locals())

    n_sh = len(_SHARDED)
    everyone = ("x", "y", "c")
    shards = [given[n][0].astype(BF16 if n in _BF16_OPERANDS else F32) for n, _ in _SHARDED]
    shards[0] = shards[0].T
    names = [n for n, _ in _SHARDED]
    first = [i for i, n in enumerate(names) if n not in _LATE_WEIGHTS]
    gathered = _exchange([shards[i] for i in first], (True,) * len(first), everyone, "gather_weights")
    local = {names[i]: _join_shards(names[i], blocks) for i, blocks in list(zip(first, gathered))[1:]}
    local['w_in_t'] = gathered[0].reshape(D_IN, D_MODEL)
    local.update({n: (given[n][0] if len(s) > 1 else given[n]) for n, s in _SMALL})
    late = [(n, shards[names.index(n)]) for n in _LATE_WEIGHTS]

    loss, grad_x, grads = _local_step(x[0], loss_target[0], local, late)

    blocks = [grads['w_in_t'].reshape(N_DEV, D_IN // N_DEV, D_MODEL)]
    blocks += [_split_shards(n, grads[n], s[1:]).astype(BF16) for n, s in _SHARDED[1:]]
    small = _pack_small(grads, loss[0, 0]).reshape(N_DEV, _SMALL_ROWS // N_DEV, LANES)
    recv = _exchange(blocks + [small], (False,) * (n_sh + 1), everyone, "exchange_grads")
    small_sum = _exchange([_sum_blocks(recv[-1])], (True,), everyone, "gather_small_grads")[0]

    result = {}
    for group, name in (([0], "adamw_w_in"), ([1, 2, 3, 4], "adamw_shards")):
        ns = [_SHARDED[i][0] for i in group]
        own = (lambda t: t[0].T) if group == [0] else (lambda t: t[0])
        back = (lambda t: t.T[None]) if group == [0] else (lambda t: t[None])
        res = _adamw([recv[i] for i in group], [own(given[n]) for n in ns], [own(given['m_' + n]) for n in ns],
                     [own(given['v_' + n]) for n in ns], True, name)
        for j, n in enumerate(ns):
            result[n] = [back(res[k * len(ns) + j]) for k in range(4)]
    g_small, total = _unpack_small(small_sum)
    two_d = lambda t: t.reshape(1, -1) if t.ndim == 1 else t
    ns = [n for n, _ in _SMALL]
    res = _adamw([two_d(g_small[n]) for n in ns], [two_d(_stored(n, given[n])) for n in ns],
                 [two_d(_stored(n, given['m_' + n])) for n in ns], [two_d(_stored(n, given['v_' + n])) for n in ns],
                 False, "adamw_small")
    for j, (n, s) in enumerate(_SMALL):
        result[n] = [_stored(n, g_small[n])] + [_stored(n, res[k * len(ns) + j]).reshape(s) for k in range(3)]

    outs = [total, grad_x[None]]
    for k in range(4):
        outs += [result[n][k] for n, _, _ in _WEIGHTS]
    return tuple(outs)
```

```python
import math

import jax
import jax.numpy as jnp
from jax import lax
from jax.experimental import pallas as pl
from jax.experimental.pallas import tpu as pltpu

F32 = jnp.float32
BF16 = jnp.bfloat16
HI = lax.Precision.HIGH

D_MODEL = 1024
D_S5 = 512
D_RWKV = 512
S5_GROUPS = 32
S5_GROUP = 16
S5_STATE = 64
N_STATE = S5_GROUPS * S5_STATE
N_HEADS = 8
HEAD = 64
D_SHIFT = 3 * D_RWKV + 128
D_IN = 2 * D_S5 + D_SHIFT + D_RWKV
NORM_EPS = 1e-6
GN_EPS = 64e-5
N_DEV = 8
LANES = 128
S5_BLOCKS = 4
RWKV_CHUNK = 64
RWKV_CHUNKS_PER_STEP = 4
VMEM_LIMIT = 56 * 1024 * 1024

ADAM_LR = 0.001
ADAM_B1 = 0.9
ADAM_B2 = 0.999
ADAM_EPS = 1e-08
ADAM_WD = 0.01
ADAM_STEP = 10


def _dot(a, b, dims, prec):
    return lax.dot_general(a, b, (dims, ((), ())), precision=prec, preferred_element_type=F32)


def _dot_bf(a, b, dims):
    return _dot(a.astype(BF16), b.astype(BF16), dims, None)


def _make_mm(cast, prec):
    @jax.custom_vjp
    def mm(a, b):
        return _dot(cast(a), cast(b), ((1,), (0,)), prec)

    def fwd(a, b):
        return mm(a, b), (a, b)

    def bwd(res, g):
        a, b = res
        return (_dot(cast(g), cast(b), ((1,), (1,)), prec), _dot(cast(a), cast(g), ((0,), (0,)), prec))

    mm.defvjp(fwd, bwd)
    return mm


mm_bf = _make_mm(lambda t: t.astype(BF16), None)


@jax.custom_vjp
def mm_w(a, w):
    return _dot(a.astype(BF16), w, ((1,), (0,)), None)


def _mm_w_fwd(a, w):
    return mm_w(a, w), w


def _mm_w_bwd(w, g):
    return _dot(g.astype(BF16), w, ((1,), (1,)), None), jnp.zeros_like(w)


mm_w.defvjp(_mm_w_fwd, _mm_w_bwd)


def _make_head_sum(split):
    def product(x, ee):
        hi = x.astype(BF16)
        out = _dot(hi, ee, ((1,), (0,)), None)
        if split:
            out = out + _dot((x - hi.astype(F32)).astype(BF16), ee, ((1,), (0,)), None)
        return out

    @jax.custom_vjp
    def head_sum(x, ee):
        return product(x, ee)

    def fwd(x, ee):
        return product(x, ee), ee

    def bwd(ee, g):
        return product(g, ee), jnp.zeros_like(ee)

    head_sum.defvjp(fwd, bwd)
    return head_sum


head_sum = _make_head_sum(False)
head_sum_split = _make_head_sum(True)


@jax.custom_vjp
def _sigmoid(x):
    return 1.0 / (1.0 + jnp.exp(-x))


def _sigmoid_fwd(x):
    s = _sigmoid(x)
    return s, s


_sigmoid.defvjp(_sigmoid_fwd, lambda s, g: (g * s * (1.0 - s),))


@jax.custom_vjp
def _silu(x):
    return x * _sigmoid(x)


def _silu_fwd(x):
    s = _sigmoid(x)
    return x * s, (x, s)


def _silu_bwd(res, g):
    x, s = res
    return (g * s * (1.0 + x * (1.0 - s)),)


_silu.defvjp(_silu_fwd, _silu_bwd)


@jax.custom_vjp
def _softplus(x):
    return jnp.maximum(x, 0.0) + jnp.log(1.0 + jnp.exp(-jnp.abs(x)))


def _softplus_fwd(x):
    e = jnp.exp(-jnp.abs(x))
    return jnp.maximum(x, 0.0) + jnp.log(1.0 + e), (x, e)


def _softplus_bwd(res, g):
    x, e = res
    return (g * jnp.where(x >= 0.0, 1.0, e) / (1.0 + e),)


_softplus.defvjp(_softplus_fwd, _softplus_bwd)


@jax.custom_vjp
def _normalize_heads(x, ee):
    return x / jnp.maximum(jnp.sqrt(head_sum(x * x, ee)), 1e-12)


def _normalize_heads_fwd(x, ee):
    norm = jnp.sqrt(head_sum(x * x, ee))
    inv = 1.0 / jnp.maximum(norm, 1e-12)
    y = x * inv
    return y, (y, inv, norm, ee)


def _normalize_heads_bwd(res, g):
    y, inv, norm, ee = res
    along = jnp.where(norm > 1e-12, head_sum(g * y, ee), 0.0)
    return inv * (g - y * along), jnp.zeros_like(ee)


_normalize_heads.defvjp(_normalize_heads_fwd, _normalize_heads_bwd)


_GELU_C = 2.0 * math.sqrt(2.0 / math.pi)


def _gelu_gate(x):
    return 1.0 / (1.0 + jnp.exp(-_GELU_C * x * (1.0 + 0.044715 * (x * x))))


@jax.custom_vjp
def _gelu(x):
    return x * _gelu_gate(x)


def _gelu_fwd(x):
    s = _gelu_gate(x)
    return x * s, (x, s)


def _gelu_bwd(res, g):
    x, s = res
    return (g * (s + x * s * (1.0 - s) * (_GELU_C * (1.0 + 3.0 * 0.044715 * (x * x)))),)


_gelu.defvjp(_gelu_fwd, _gelu_bwd)


def _rms(x, g):
    return x * lax.rsqrt(jnp.mean(x * x, axis=-1, keepdims=True) + NORM_EPS) * g


def _const_spec(shape):
    nd = len(shape)
    return pl.BlockSpec(shape, lambda *_: (0,) * nd, pipeline_mode=pl.Buffered(1))


def _acc_spec(shape):
    nd = len(shape)
    return pl.BlockSpec(shape, lambda *_: (0,) * nd)


def _params(sem):
    return pltpu.CompilerParams(dimension_semantics=(sem,), vmem_limit_bytes=VMEM_LIMIT)


_ANY = pl.BlockSpec(memory_space=pl.ANY)


def _sds(shape):
    return jax.ShapeDtypeStruct(shape, F32)


def _head_sum_matrix():
    i = jnp.arange(D_RWKV) // HEAD
    return (i[:, None] == i[None, :]).astype(BF16)


def _s5_param_fn(lam_re, lam_im, logdt, b_re, b_im):
    dt = jnp.exp(logdt)
    mag = jnp.exp(lam_re * dt)
    ang = lam_im * dt
    lbr = mag * jnp.cos(ang)
    lbi = mag * jnp.sin(ang)
    nr = lbr - 1.0
    den = lam_re * lam_re + lam_im * lam_im
    cr = (nr * lam_re + lbi * lam_im) / den
    ci = (lbi * lam_re - nr * lam_im) / den
    return lbr, lbi, cr * b_re - ci * b_im, cr * b_im + ci * b_re


def _cmul(ar, ai, br, bi):
    return ar * br - ai * bi, ar * bi + ai * br


def _s5_param_fwd(lam_re, lam_im, logdt, b_re, b_im):
    def body(lr, li, ld, br, bi, o_br, o_bi, o_pr, o_pi, o_qr, o_qi):
        lbr, lbi, bbr, bbi = _s5_param_fn(lr[...], li[...], ld[...], br[...], bi[...])
        o_br[...] = bbr
        o_bi[...] = bbi
        rid = lax.broadcasted_iota(jnp.int32, (8, N_STATE), 0)
        pr, pi_ = lbr, lbi
        fwd_r = rev_r = jnp.broadcast_to(pr, (8, N_STATE))
        fwd_i = rev_i = jnp.broadcast_to(pi_, (8, N_STATE))
        for j in range(1, 8):
            pr, pi_ = _cmul(pr, pi_, lbr, lbi)
            fwd_r = jnp.where(rid == j, jnp.broadcast_to(pr, (8, N_STATE)), fwd_r)
            fwd_i = jnp.where(rid == j, jnp.broadcast_to(pi_, (8, N_STATE)), fwd_i)
            rev_r = jnp.where(rid == 7 - j, jnp.broadcast_to(pr, (8, N_STATE)), rev_r)
            rev_i = jnp.where(rid == 7 - j, jnp.broadcast_to(pi_, (8, N_STATE)), rev_i)
        o_pr[...] = fwd_r
        o_pi[...] = fwd_i
        o_qr[...] = rev_r
        o_qi[...] = -rev_i

    return pl.pallas_call(
        body, name="s5_param_fwd",
        out_shape=[_sds((S5_GROUP, N_STATE))] * 2 + [_sds((8, N_STATE))] * 4,
    )(lam_re, lam_im, logdt, b_re, b_im)


def _s5_param_bwd(lam_re, lam_im, logdt, b_re, b_im, d_lbr, d_lbi, d_bbr, d_bbi, group_ind):
    def body(lr, li, ld, br, bi, g0, g1, g2, g3, ind, o_lr, o_li, o_ld, o_br, o_bi):
        _, vjp = jax.vjp(_s5_param_fn, lr[...], li[...], ld[...], br[...], bi[...])
        d_lr, d_li, d_ld, d_br, d_bi = vjp((g0[...], g1[...], g2[...], g3[...]))
        o_lr[...] = d_lr
        o_li[...] = d_li
        o_ld[...] = _dot(jnp.broadcast_to(d_ld, (8, N_STATE)), ind[...], ((1,), (0,)), HI)
        o_br[...] = d_br
        o_bi[...] = d_bi

    return pl.pallas_call(
        body, name="s5_param_bwd",
        out_shape=[_sds((1, N_STATE))] * 2 + [_sds((8, LANES))] + [_sds((S5_GROUP, N_STATE))] * 2,
    )(lam_re, lam_im, logdt, b_re, b_im, d_lbr, d_lbi, d_bbr, d_bbi, group_ind)


def _fwd_in(x, norm_g, w_in_bf, tt, late=()):
    L = x.shape[0]
    n_t = L // tt
    n_late = len(late)
    everyone = ("x", "y", "c")

    def body(x_ref, g_ref, w_ref, *refs):
        u_ref, zs_ref, rw_ref, zr_ref = refs[n_late:n_late + 4]
        if n_late:
            i = pl.program_id(0)
            copies = lambda: _exchange_copies(refs[:n_late], refs[n_late + 4:2 * n_late + 4], *refs[2 * n_late + 4:],
                                              (True,) * n_late, everyone)
            pl.when(i == 0)(lambda: _start_copies(copies()))
            pl.when(i == n_t - 1)(lambda: _wait_copies(copies()))
        h = _rms(x_ref[...], g_ref[...])
        proj = _dot(h.astype(BF16), w_ref[...], ((1,), (1,)), None)
        u_ref[...] = proj[:, 0:D_S5]
        zs_ref[...] = proj[:, D_S5:2 * D_S5]
        rw_ref[...] = proj[:, 2 * D_S5:2 * D_S5 + D_SHIFT]
        zr_ref[...] = proj[:, 2 * D_S5 + D_SHIFT:D_IN]

    row = lambda n: pl.BlockSpec((tt, n), lambda i: (i, 0))
    return pl.pallas_call(
        body, name="fwd_in", grid=(n_t,),
        in_specs=[row(D_MODEL), _const_spec((1, D_MODEL)), _const_spec((D_IN, D_MODEL))] + [_ANY] * n_late,
        out_specs=[row(D_S5), row(D_S5), row(D_SHIFT), row(D_RWKV)] + [_ANY] * n_late,
        out_shape=[_sds((L, D_S5)), _sds((L, D_S5)), _sds((L, D_SHIFT)), _sds((L, D_RWKV))]
        + _exchange_results(late, (True,) * n_late, everyone),
        scratch_shapes=_exchange_scratch(n_late, everyone) if n_late else [],
        compiler_params=pltpu.CompilerParams(dimension_semantics=("arbitrary",), vmem_limit_bytes=VMEM_LIMIT,
                                             has_side_effects=bool(n_late)),
    )(x, norm_g, w_in_bf, *late)


S5_LANE_CHUNK = 512


def _tile_scan(re_ref, im_ref, pow_r_ref, pow_i_ref, carry_r_ref, carry_i_ref, reverse):
    t, n = re_ref.shape
    n_groups = t // 8
    ch = S5_LANE_CHUNK
    rid = lax.broadcasted_iota(jnp.int32, (8, ch), 0)
    for c in range(n // ch):
        cols = slice(c * ch, (c + 1) * ch)
        pow_r = pow_r_ref[:, cols]
        pow_i = pow_i_ref[:, cols]
        row = lambda tile, j: jnp.broadcast_to(tile[j:j + 1], (8, ch))
        levels = []
        for d in (1, 2, 4):
            keep = (rid < 8 - d) if reverse else (rid >= d)
            j = 8 - d if reverse else d - 1
            levels.append(((8 - d) if reverse else d,
                           jnp.where(keep, row(pow_r, j), 0.0), jnp.where(keep, row(pow_i, j), 0.0)))

        def group(g, carry):
            r0 = pl.multiple_of(((n_groups - 1 - g) if reverse else g) * 8, 8)
            xr = re_ref[pl.ds(r0, 8), cols]
            xi = im_ref[pl.ds(r0, 8), cols]
            for shift, lr, li in levels:
                mr, mi = _cmul(lr, li, pltpu.roll(xr, shift, axis=0), pltpu.roll(xi, shift, axis=0))
                xr = xr + mr
                xi = xi + mi
            mr, mi = _cmul(pow_r, pow_i, carry[0], carry[1])
            xr = xr + mr
            xi = xi + mi
            re_ref[pl.ds(r0, 8), cols] = xr
            im_ref[pl.ds(r0, 8), cols] = xi
            last = 0 if reverse else 7
            return row(xr, last), row(xi, last)

        out = lax.fori_loop(0, n_groups, group, (carry_r_ref[:, cols], carry_i_ref[:, cols]))
        carry_r_ref[:, cols] = out[0]
        carry_i_ref[:, cols] = out[1]


def _s5_fwd(u, b4_re, b4_im, c4_re, c4_im, pow_r, pow_i, tt):
    L = u.shape[0]

    def body(u_ref, bre_ref, bim_ref, cre_ref, cim_ref, pr_ref, pi_ref, sre_o, sim_o, y_o, car_r, car_i):
        @pl.when(pl.program_id(0) == 0)
        def _():
            car_r[...] = jnp.zeros_like(car_r)
            car_i[...] = jnp.zeros_like(car_i)

        uv = u_ref[...]
        for q in range(S5_BLOCKS):
            uq = uv[:, q * LANES:(q + 1) * LANES]
            cols = slice(q * 512, (q + 1) * 512)
            sre_o[:, cols] = _dot_bf(uq, bre_ref[q], ((1,), (0,)))
            sim_o[:, cols] = _dot_bf(uq, bim_ref[q], ((1,), (0,)))
        _tile_scan(sre_o, sim_o, pr_ref, pi_ref, car_r, car_i, reverse=False)
        for q in range(S5_BLOCKS):
            cols = slice(q * 512, (q + 1) * 512)
            y_o[:, q * LANES:(q + 1) * LANES] = (_dot_bf(sre_o[:, cols], cre_ref[q], ((1,), (0,)))
                                                 - _dot_bf(sim_o[:, cols], cim_ref[q], ((1,), (0,))))

    row = lambda n: pl.BlockSpec((tt, n), lambda i: (i, 0))
    return pl.pallas_call(
        body, name="s5_fwd", grid=(L // tt,),
        in_specs=[row(D_S5)] + [_const_spec((S5_BLOCKS, LANES, 512))] * 2 + [_const_spec((S5_BLOCKS, 512, LANES))] * 2
        + [_const_spec((8, N_STATE))] * 2,
        out_specs=[row(N_STATE), row(N_STATE), row(D_S5)],
        out_shape=[_sds((L, N_STATE)), _sds((L, N_STATE)), _sds((L, D_S5))],
        scratch_shapes=[pltpu.VMEM((8, N_STATE), F32)] * 2,
        compiler_params=_params("arbitrary"),
    )(u, b4_re, b4_im, c4_re, c4_im, pow_r, pow_i)


def _rwkv_pre_fn(r, k, v, wa, w0, w2p, a0, a2p, k_k, k_a, ee):
    w = -_softplus(-(w0 + mm_bf(jnp.tanh(wa), w2p))) - 0.5
    logw = -jnp.exp(w)
    a = _sigmoid(a0 + mm_bf(wa, a2p))
    kk = _normalize_heads(k * k_k, ee)
    k2 = k * (1.0 + (a - 1.0) * k_a)
    return r, logw, k2, v, -kk, kk * a


N_PAIRS = N_HEADS // 2


def _head_spec(tt):
    return pl.BlockSpec((N_PAIRS, tt, LANES), lambda i: (0, i, 0))


def _load_heads(ref):
    return jnp.concatenate([ref[p] for p in range(N_PAIRS)], axis=-1)


def _store_heads(ref, val):
    for p in range(N_PAIRS):
        ref[p] = val[:, p * LANES:(p + 1) * LANES]


def _split_pairs(x):
    return jnp.concatenate([x[:, :, :HEAD], x[:, :, HEAD:]], axis=0)


def _join_pairs(x):
    return jnp.concatenate([x[:N_PAIRS], x[N_PAIRS:]], axis=-1)


def _shifted(rw, prev_blk, first):
    rolled = pltpu.roll(rw, 1, axis=0)
    prev_row = jnp.where(first, 0.0, prev_blk[7:8, :])
    rid = lax.broadcasted_iota(jnp.int32, rw.shape, 0)
    return jnp.where(rid == 0, jnp.broadcast_to(prev_row, rw.shape), rolled)


def _split_rw(t):
    return t[:, 0:512], t[:, 512:1024], t[:, 1024:1536], t[:, 1536:1664]


def _rwkv_pre_specs(tt):
    row = pl.BlockSpec((tt, D_SHIFT), lambda i: (i, 0))
    prev = pl.BlockSpec((8, D_SHIFT), lambda i: (jnp.maximum(i * (tt // 8) - 1, 0), 0))
    consts = [_const_spec((1, D_SHIFT)), _const_spec((1, D_RWKV)), _const_spec((LANES, D_RWKV)),
              _const_spec((1, D_RWKV)), _const_spec((LANES, D_RWKV)), _const_spec((1, D_RWKV)),
              _const_spec((1, D_RWKV)), _const_spec((D_RWKV, D_RWKV))]
    return [row, prev] + consts


def _rwkv_pre_fwd(rw, mu, w0, w2p, a0, a2p, k_k, k_a, ee, tt):
    L = rw.shape[0]

    def body(rw_ref, prev_ref, mu_ref, w0_ref, w2_ref, a0_ref, a2_ref, kk_ref, ka_ref, ee_ref, *outs):
        rwv = rw_ref[...]
        rws = rwv + (_shifted(rwv, prev_ref[...], pl.program_id(0) == 0) - rwv) * mu_ref[...]
        res = _rwkv_pre_fn(*_split_rw(rws), w0_ref[...], w2_ref[...], a0_ref[...], a2_ref[...],
                           kk_ref[...], ka_ref[...], ee_ref[...])
        for o, val in zip(outs, res):
            _store_heads(o, val)

    return pl.pallas_call(
        body, name="rwkv_pre_fwd", grid=(L // tt,),
        in_specs=_rwkv_pre_specs(tt), out_specs=[_head_spec(tt)] * 6, out_shape=[_sds((N_PAIRS, L, LANES))] * 6,
        compiler_params=_params("parallel"),
    )(rw, rw, mu, w0, w2p, a0, a2p, k_k, k_a, ee)


def _rwkv_pre_bwd(rw, mu, w0, w2p, a0, a2p, k_k, k_a, ee, cots, tt):
    L = rw.shape[0]
    n_t = L // tt

    def body(rw_ref, prev_ref, mu_ref, w0_ref, w2_ref, a0_ref, a2_ref, kk_ref, ka_ref, ee_ref,
             c_r, c_w, c_k, c_v, c_a, c_b, cb_r, cb_k, cb_v,
             drws_ref, dmu_o, dw0_o, dw2_o, da0_o, da2_o, dkk_o, dka_o,
             dmu, dw0, dw2, da0, da2, dkk, dka):
        i = pl.program_id(0)
        accs = (dmu, dw0, dw2, da0, da2, dkk, dka)

        @pl.when(i == 0)
        def _():
            for acc in accs:
                acc[...] = jnp.zeros_like(acc)

        rwv = rw_ref[...]
        diff = _shifted(rwv, prev_ref[...], i == 0) - rwv
        rws = rwv + diff * mu_ref[...]
        consts = (w0_ref[...], w2_ref[...], a0_ref[...], a2_ref[...], kk_ref[...], ka_ref[...])
        _, vjp = jax.vjp(lambda *a: _rwkv_pre_fn(*a, ee_ref[...]), *_split_rw(rws), *consts)
        scan = [_load_heads(c) for c in (c_r, c_w, c_k, c_v, c_a, c_b)]
        g = vjp((scan[0] + cb_r[...], scan[1], scan[2] + cb_k[...], scan[3] + cb_v[...], scan[4], scan[5]))
        drws = jnp.concatenate(g[0:4], axis=-1)
        drws_ref[...] = drws
        dmu[...] += jnp.sum(drws * diff, axis=0, keepdims=True)
        for acc, val in zip(accs[1:], g[4:]):
            acc[...] += val

        @pl.when(i == n_t - 1)
        def _():
            for acc, out in zip(accs, (dmu_o, dw0_o, dw2_o, da0_o, da2_o, dkk_o, dka_o)):
                out[...] = acc[...]

    row = pl.BlockSpec((tt, D_RWKV), lambda i: (i, 0))
    shapes = [(1, D_SHIFT), (1, D_RWKV), (LANES, D_RWKV), (1, D_RWKV), (LANES, D_RWKV), (1, D_RWKV), (1, D_RWKV)]
    return pl.pallas_call(
        body, name="rwkv_pre_bwd", grid=(n_t,),
        in_specs=_rwkv_pre_specs(tt) + [_head_spec(tt)] * 6 + [row] * 3,
        out_specs=[pl.BlockSpec((tt, D_SHIFT), lambda i: (i, 0))] + [_acc_spec(s) for s in shapes],
        out_shape=[_sds((L, D_SHIFT))] + [_sds(s) for s in shapes],
        scratch_shapes=[pltpu.VMEM(s, F32) for s in shapes],
        compiler_params=_params("arbitrary"),
    )(rw, rw, mu, w0, w2p, a0, a2p, k_k, k_a, ee, *cots)


def _bmm(a, b):
    return lax.dot_general(a, b, (((2,), (1,)), ((0,), (0,))), precision=HI, preferred_element_type=F32)


def _bmm_nt(a, b):
    return lax.dot_general(a, b, (((2,), (2,)), ((0,), (0,))), precision=HI, preferred_element_type=F32)


def _bmm_tn(a, b):
    return lax.dot_general(a, b, (((1,), (1,)), ((0,), (0,))), precision=HI, preferred_element_type=F32)


def _bdot_bf(a, b, lhs_dim, rhs_dim):
    return lax.dot_general(a.astype(BF16), b.astype(BF16), (((lhs_dim,), (rhs_dim,)), ((0,), (0,))),
                           preferred_element_type=F32)


@jax.custom_vjp
def _bmm_bf(a, b):
    return _bdot_bf(a, b, 2, 1)


def _bmm_bf_fwd(a, b):
    return _bmm_bf(a, b), (a, b)


def _bmm_bf_bwd(res, g):
    a, b = res
    return _bdot_bf(g, b, 2, 2), _bdot_bf(a, g, 1, 1)


_bmm_bf.defvjp(_bmm_bf_fwd, _bmm_bf_bwd)


@jax.custom_vjp
def _bmm_tn_bf(a, b):
    return _bdot_bf(a, b, 1, 1)


def _bmm_tn_bf_fwd(a, b):
    return _bmm_tn_bf(a, b), (a, b)


def _bmm_tn_bf_bwd(res, g):
    a, b = res
    return _bdot_bf(b, g, 2, 2), _bdot_bf(a, g, 2, 1)


_bmm_tn_bf.defvjp(_bmm_tn_bf_fwd, _bmm_tn_bf_bwd)


def _unit_lower_inverse(a):
    t = a.shape[-1]
    ti = lax.broadcasted_iota(jnp.int32, (t, t), 0)
    si = lax.broadcasted_iota(jnp.int32, (t, t), 1)

    def same_block(bits):
        shift = jnp.int32(bits)
        return (lax.shift_right_logical(ti, shift) == lax.shift_right_logical(si, shift))[None]

    def mm(x, y):
        return _bdot_bf(x, y, 2, 1)

    d = jnp.where(same_block(3), a, 0.0)
    inv = jnp.where(ti == si, 1.0, 0.0)[None] + d
    pw = mm(d, d)
    both = mm(jnp.concatenate([inv, pw], axis=1), pw)
    inv = inv + both[:, :t]
    inv = inv + mm(inv, both[:, t:])
    bits = 3
    while (1 << bits) < t:
        e = jnp.where(same_block(bits), 0.0, jnp.where(same_block(bits + 1), a, 0.0))
        inv = inv + mm(mm(inv, e), inv)
        bits += 1
    return inv


def _tri_mask(t):
    ri = lax.broadcasted_iota(jnp.int32, (2 * t, 2 * t), 0)
    ci = lax.broadcasted_iota(jnp.int32, (2 * t, 2 * t), 1)
    top_rows = ri < t
    diff = jnp.where(top_rows, ri, ri - t) - jnp.where(ci < t, ci, ci - t)
    return (diff >= jnp.where(top_rows, 1, 0))[None]


def _ones_tri(n_h, t):
    ti = lax.broadcasted_iota(jnp.int32, (t, t), 0)
    si = lax.broadcasted_iota(jnp.int32, (t, t), 1)
    return jnp.broadcast_to(jnp.where(ti >= si, 1.0, 0.0)[None], (n_h, t, t))


@jax.custom_vjp
def _running_sum_kept(logw, kept):
    return kept


def _running_sum_kept_bwd(shape, g):
    return _bmm_tn(_ones_tri(shape[0], shape[1]), g), jnp.zeros_like(g)


_running_sum_kept.defvjp(lambda logw, kept: (kept, logw.shape), _running_sum_kept_bwd)


@jax.custom_vjp
def _tri_products_kept(ar, bk, kept):
    return kept


def _tri_products_kept_bwd(res, g):
    ar, bk = res
    g = jnp.where(_tri_mask(ar.shape[1] // 2), g, 0.0)
    return _bmm(g, bk), _bmm_tn(g, ar), jnp.zeros_like(g)


_tri_products_kept.defvjp(lambda ar, bk, kept: (kept, (ar, bk)), _tri_products_kept_bwd)


@jax.custom_vjp
def _solve_unit_lower(a, rhs, inv, kept=None):
    return _bmm(inv, rhs) if kept is None else kept


def _solve_fwd(a, rhs, inv, kept=None):
    u = _bmm(inv, rhs) if kept is None else kept
    return u, (inv, u, kept is not None)


def _solve_bwd(res, du):
    inv, u, had_kept = res
    d_rhs = _bmm_tn(inv, du)
    return _bmm_nt(d_rhs, u), d_rhs, jnp.zeros_like(inv), (jnp.zeros_like(u) if had_kept else None)


_solve_unit_lower.defvjp(_solve_fwd, _solve_bwd)


def _rwkv_chunk(st0, r, logw, k, v, a, b, kept=None):
    n_h, t, _ = r.shape
    log_p = _bmm(_ones_tri(n_h, t), logw) if kept is None else _running_sum_kept(logw, kept[0])
    p_in = jnp.exp(log_p)
    p_inv = jnp.exp(-log_p)
    at = a * jnp.exp(log_p - logw)
    rt = r * p_in
    ar = jnp.concatenate([at, rt], axis=1)
    bk = jnp.concatenate([b * p_inv, k * p_inv], axis=1)
    if kept is None:
        m = jnp.where(_tri_mask(t), _bmm_nt(ar, bk), 0.0)
        inv = _unit_lower_inverse(m[:, :t, :t])
    else:
        m = _tri_products_kept(ar, bk, kept[1])
        inv = kept[2]
    top, bottom = m[:, :t], m[:, t:]
    rhs = _bmm_bf(jnp.concatenate([at, top[:, :, t:]], axis=2), jnp.concatenate([st0, v], axis=1))
    u = _solve_unit_lower(top[:, :, :t], rhs, inv, None if kept is None else kept[3])
    y = _bmm_bf(jnp.concatenate([rt, bottom], axis=2), jnp.concatenate([st0, u, v], axis=1))
    p_end = jnp.swapaxes(p_in[:, t - 1:t, :], 1, 2)
    st1 = (st0 + _bmm_tn_bf(bk, jnp.concatenate([u, v], axis=1))) * p_end
    return y, st1, (log_p, m, inv, u)


def _rwkv_scan_fwd(ops):
    n_h, L, n = N_HEADS, ops[0].shape[1], HEAD
    t = RWKV_CHUNK
    per = min(RWKV_CHUNKS_PER_STEP, L // t)
    n_c = L // t
    n_s = n_c // per

    def body(r_ref, w_ref, k_ref, v_ref, a_ref, b_ref, y_ref, st_ref, logp_ref, m_ref, inv_ref, u_ref, st):
        @pl.when(pl.program_id(0) == 0)
        def _():
            st[...] = jnp.zeros_like(st)

        st0 = st[...]
        for j in range(per):
            rows = slice(j * t, (j + 1) * t)
            st_ref[j] = st0
            y, st0, (log_p, m, inv, u) = _rwkv_chunk(
                st0, *(_split_pairs(ref[:, rows, :]) for ref in (r_ref, w_ref, k_ref, v_ref, a_ref, b_ref)))
            y_ref[:, rows, :] = _join_pairs(y)
            logp_ref[:, rows, :] = log_p
            u_ref[:, rows, :] = u
            m_ref[j] = m
            inv_ref[j] = inv
        st[...] = st0

    pairs = pl.BlockSpec((N_PAIRS, per * t, LANES), lambda c: (0, c, 0))
    blk = pl.BlockSpec((n_h, per * t, n), lambda c: (0, c, 0))
    per_chunk = lambda m: pl.BlockSpec((per, n_h, m, m), lambda c: (c, 0, 0, 0))
    return pl.pallas_call(
        body, name="rwkv_scan_fwd", grid=(n_s,), in_specs=[pairs] * 6,
        out_specs=[pairs, per_chunk(n), blk, per_chunk(2 * t), per_chunk(t), blk],
        out_shape=[_sds((N_PAIRS, L, LANES)), _sds((n_c, n_h, n, n)), _sds((n_h, L, n)),
                   _sds((n_c, n_h, 2 * t, 2 * t)), _sds((n_c, n_h, t, t)), _sds((n_h, L, n))],
        scratch_shapes=[pltpu.VMEM((n_h, n, n), F32)],
        compiler_params=_params("arbitrary"),
    )(*ops)


def _rwkv_scan_bwd(ops, states, kept, dy):
    n_h, L, n = N_HEADS, ops[0].shape[1], HEAD
    t = RWKV_CHUNK
    per = min(RWKV_CHUNKS_PER_STEP, L // t)
    n_s = L // t // per

    def body(r_ref, w_ref, k_ref, v_ref, a_ref, b_ref, st_ref, logp_ref, m_ref, inv_ref, u_ref, dy_ref,
             dr, dw, dk, dv, da, db, dst):
        @pl.when(pl.program_id(0) == 0)
        def _():
            dst[...] = jnp.zeros_like(dst)

        vjps = []
        for j in range(per):
            rows = slice(j * t, (j + 1) * t)
            have = (logp_ref[:, rows, :], m_ref[j], inv_ref[j], u_ref[:, rows, :])
            args = [_split_pairs(ref[:, rows, :]) for ref in (r_ref, w_ref, k_ref, v_ref, a_ref, b_ref)]
            vjps.append(jax.vjp(lambda *a, have=have: _rwkv_chunk(*a, kept=have)[:2], st_ref[j], *args)[1])
        d_state = dst[...]
        for j in reversed(range(per)):
            rows = slice(j * t, (j + 1) * t)
            g = vjps[j]((_split_pairs(dy_ref[:, rows, :]), d_state))
            d_state = g[0]
            for out, val in zip((dr, dw, dk, dv, da, db), g[1:]):
                out[:, rows, :] = _join_pairs(val)
        dst[...] = d_state

    pairs = pl.BlockSpec((N_PAIRS, per * t, LANES), lambda c: (0, n_s - 1 - c, 0))
    blk = pl.BlockSpec((n_h, per * t, n), lambda c: (0, n_s - 1 - c, 0))
    per_chunk = lambda m: pl.BlockSpec((per, n_h, m, m), lambda c: (n_s - 1 - c, 0, 0, 0))
    return pl.pallas_call(
        body, name="rwkv_scan_bwd", grid=(n_s,),
        in_specs=[pairs] * 6 + [per_chunk(n), blk, per_chunk(2 * t), per_chunk(t), blk, pairs],
        out_specs=[pairs] * 6, out_shape=[_sds((N_PAIRS, L, LANES))] * 6,
        scratch_shapes=[pltpu.VMEM((n_h, n, n), F32)],
        compiler_params=_params("arbitrary"),
    )(*ops, states, *kept, dy)


def _post_fn(x, u, zs, zr, ysc, r, k2, v, y_ssm, gate_in, d, glu_b, ln_w, ln_b, r_k, gf,
              glu_w, wo_s5, wo_rwkv, tgt, ee):
    y3 = _gelu(y_ssm + d * u)
    y_s5 = y3 * _sigmoid(mm_w(y3, glu_w) + glu_b + gate_in) * _silu(zs)
    mean = head_sum_split(ysc, ee) * (1.0 / HEAD)
    yc = ysc - mean
    var = head_sum(yc * yc, ee) * (1.0 / HEAD)
    gn = yc * lax.rsqrt(var + GN_EPS) * ln_w + ln_b
    bonus = head_sum(r * k2 * r_k, ee) * v
    y_rwkv = (gn + bonus) * _silu(zr)
    x2 = x + mm_w(y_s5, wo_s5) + mm_w(y_rwkv, wo_rwkv)
    err = _rms(x2, gf) - tgt
    return 0.5 * jnp.mean(err * err, axis=-1, keepdims=True), (y3, y_s5, y_rwkv)


def _post(x, u, zs, zr, ysc, r, k2, v, y_ssm, d, glu_w, glu_b, ln_w, ln_b, r_k, w_out, gf, tgt, ee, tt):
    L = x.shape[0]
    n_t = L // tt
    acc_shapes = [(1, D_S5), (D_S5, D_S5), (1, D_S5), (1, D_RWKV), (1, D_RWKV), (1, D_RWKV),
                  (D_MODEL, D_MODEL), (1, D_MODEL), (8, LANES)]

    def body(x_ref, u_ref, zs_ref, zr_ref, ysc_ref, r_ref, k2_ref, v_ref, yssm_ref,
             d_ref, gw_ref, gb_ref, lw_ref, lb_ref, rk_ref, wo_ref, gf_ref, tgt_ref, ee_ref,
             dx_o, du_o, dzs_o, dzr_o, dysc_o, dr_o, dk2_o, dv_o, dyssm_o,
             dd_o, dgw_o, dgb_o, dlw_o, dlb_o, drk_o, dwo_o, dgf_o, loss_o,
             dd, dgw, dgb, dlw, dlb, drk, dwo, dgf, loss):
        i = pl.program_id(0)
        accs = (dd, dgw, dgb, dlw, dlb, drk, dwo, dgf, loss)

        @pl.when(i == 0)
        def _():
            for acc in accs:
                acc[...] = jnp.zeros_like(acc)

        args = (x_ref[...], u_ref[...], zs_ref[...], zr_ref[...],
                _load_heads(ysc_ref), _load_heads(r_ref), _load_heads(k2_ref), _load_heads(v_ref), yssm_ref[...],
                jnp.zeros((tt, D_S5), F32), d_ref[...], gb_ref[...], lw_ref[...], lb_ref[...], rk_ref[...], gf_ref[...])
        rows, vjp, (y3, y_s5, y_rwkv) = jax.vjp(
            lambda *a: _post_fn(*a, gw_ref[...], wo_ref[0:D_S5, :], wo_ref[D_S5:D_MODEL, :], tgt_ref[...],
                                ee_ref[...]), *args, has_aux=True)
        g = vjp(jnp.ones_like(rows))
        for out, val in zip((dx_o, du_o, dzs_o, dzr_o), g[0:4]):
            out[...] = val
        _store_heads(dysc_o, g[4])
        for out, val in zip((dr_o, dk2_o, dv_o, dyssm_o), g[5:9]):
            out[...] = val
        for acc, val in zip((dd, dgb, dlw, dlb, drk, dgf), g[10:16]):
            acc[...] += val
        dgw[...] += _dot_bf(y3, g[9], ((0,), (0,)))
        dwo[0:D_S5, :] += _dot_bf(y_s5, g[0], ((0,), (0,)))
        dwo[D_S5:D_MODEL, :] += _dot_bf(y_rwkv, g[0], ((0,), (0,)))
        loss[...] += jnp.broadcast_to(jnp.sum(rows, axis=0, keepdims=True), loss.shape)

        @pl.when(i == n_t - 1)
        def _():
            for acc, out in zip(accs, (dd_o, dgw_o, dgb_o, dlw_o, dlb_o, drk_o, dwo_o, dgf_o, loss_o)):
                pltpu.sync_copy(acc, out)

    row = lambda n: pl.BlockSpec((tt, n), lambda i: (i, 0))
    in_specs = ([row(D_MODEL)] + [row(512)] * 3 + [_head_spec(tt)] * 4 + [row(D_S5)]
                + [_const_spec(s) for s in [(1, D_S5), (D_S5, D_S5), (1, D_S5), (1, D_RWKV), (1, D_RWKV), (1, D_RWKV),
                                            (D_MODEL, D_MODEL), (1, D_MODEL)]]
                + [row(D_MODEL), _const_spec((D_RWKV, D_RWKV))])
    out_rows = [D_MODEL] + [512] * 3 + [None] + [512] * 4
    return pl.pallas_call(
        body, name="post_fwd_bwd", grid=(n_t,), in_specs=in_specs,
        out_specs=[row(n) if n else _head_spec(tt) for n in out_rows] + [_ANY] * len(acc_shapes),
        out_shape=([_sds((L, n)) if n else _sds((N_PAIRS, L, LANES)) for n in out_rows]
                   + [_sds(s) for s in acc_shapes]),
        scratch_shapes=[pltpu.VMEM(s, F32) for s in acc_shapes],
        compiler_params=_params("arbitrary"),
    )(x, u, zs, zr, ysc, r, k2, v, y_ssm, d, glu_w, glu_b, ln_w, ln_b, r_k, w_out, gf, tgt, ee)


def _s5_bwd(u, du_direct, dy, s_re, s_im, b4_re, b4_im, c4_re, c4_im, pow_r, pow_i, tt):
    L = u.shape[0]
    n_t = L // tt
    acc_shapes = ([(S5_BLOCKS, LANES, 512)] * 2 + [(S5_BLOCKS, 512, LANES)] * 2 + [(1, N_STATE)] * 2)

    def body(u_ref, dud_ref, dy_ref, sre_ref, sim_ref, pre_ref, pim_ref, bre_ref, bim_ref, cre_ref, cim_ref,
             pr_ref, pi_ref, du_o, dbre_o, dbim_o, dcre_o, dcim_o, dlr_o, dli_o,
             dbre, dbim, dcre, dcim, dlr, dli, gre, gim, car_r, car_i):
        i = pl.program_id(0)

        @pl.when(i == 0)
        def _():
            for acc in (dbre, dbim, dcre, dcim, dlr, dli, car_r, car_i):
                acc[...] = jnp.zeros_like(acc)

        uv = u_ref[...]
        dyv = dy_ref[...]
        blocks = [slice(q * 512, (q + 1) * 512) for q in range(S5_BLOCKS)]
        lanes = [slice(q * LANES, (q + 1) * LANES) for q in range(S5_BLOCKS)]
        for q in range(S5_BLOCKS):
            gre[:, blocks[q]] = _dot_bf(dyv[:, lanes[q]], cre_ref[q], ((1,), (1,)))
            gim[:, blocks[q]] = -_dot_bf(dyv[:, lanes[q]], cim_ref[q], ((1,), (1,)))
        _tile_scan(gre, gim, pr_ref, pi_ref, car_r, car_i, reverse=True)
        for q in range(S5_BLOCKS):
            gr = gre[:, blocks[q]]
            gi = gim[:, blocks[q]]
            sr = sre_ref[:, blocks[q]]
            si = sim_ref[:, blocks[q]]
            du_o[:, lanes[q]] = (dud_ref[:, lanes[q]] + _dot_bf(gr, bre_ref[q], ((1,), (1,)))
                                 + _dot_bf(gi, bim_ref[q], ((1,), (1,))))
            dbre[q] += _dot_bf(uv[:, lanes[q]], gr, ((0,), (0,)))
            dbim[q] += _dot_bf(uv[:, lanes[q]], gi, ((0,), (0,)))
            dcre[q] += _dot_bf(sr, dyv[:, lanes[q]], ((0,), (0,)))
            dcim[q] -= _dot_bf(si, dyv[:, lanes[q]], ((0,), (0,)))
            rid = lax.broadcasted_iota(jnp.int32, sr.shape, 0)
            first = i == n_t - 1
            prev_r = jnp.where(first, 0.0, pre_ref[7:8, blocks[q]])
            prev_i = jnp.where(first, 0.0, pim_ref[7:8, blocks[q]])
            pr = jnp.where(rid == 0, jnp.broadcast_to(prev_r, sr.shape), pltpu.roll(sr, 1, axis=0))
            pi_ = jnp.where(rid == 0, jnp.broadcast_to(prev_i, si.shape), pltpu.roll(si, 1, axis=0))
            dlr[:, blocks[q]] += jnp.sum(pr * gr + pi_ * gi, axis=0, keepdims=True)
            dli[:, blocks[q]] += jnp.sum(pr * gi - pi_ * gr, axis=0, keepdims=True)

        @pl.when(i == n_t - 1)
        def _():
            for acc, out in zip((dbre, dbim, dcre, dcim, dlr, dli), (dbre_o, dbim_o, dcre_o, dcim_o, dlr_o, dli_o)):
                out[...] = acc[...]

    row = lambda n: pl.BlockSpec((tt, n), lambda i: (n_t - 1 - i, 0))
    prev = pl.BlockSpec((8, N_STATE), lambda i: (jnp.maximum((n_t - 1 - i) * (tt // 8) - 1, 0), 0))
    return pl.pallas_call(
        body, name="s5_bwd", grid=(n_t,),
        in_specs=[row(D_S5)] * 3 + [row(N_STATE)] * 2 + [prev] * 2
        + [_const_spec((S5_BLOCKS, LANES, 512))] * 2 + [_const_spec((S5_BLOCKS, 512, LANES))] * 2
        + [_const_spec((8, N_STATE))] * 2,
        out_specs=[row(D_S5)] + [_acc_spec(s) for s in acc_shapes],
        out_shape=[_sds((L, D_S5))] + [_sds(s) for s in acc_shapes],
        scratch_shapes=[pltpu.VMEM(s, F32) for s in acc_shapes] + [pltpu.VMEM((tt, N_STATE), F32)] * 2
        + [pltpu.VMEM((8, N_STATE), F32)] * 2,
        compiler_params=_params("arbitrary"),
    )(u, du_direct, dy, s_re, s_im, s_re, s_im, b4_re, b4_im, c4_re, c4_im, pow_r, pow_i)


def _bwd_in(x, norm_g, w_in_bf, mu, dx2, du, dzs, drws, dzr, tt, early=()):
    L = x.shape[0]
    n_t = L // tt
    n_e = len(early)
    everyone = ("x", "y", "c")

    def body(x_ref, g_ref, w_ref, mu_ref, dx2_ref, du_ref, dzs_ref, drws_ref, nxt_ref, dzr_ref, *refs):
        gx_o, dw_o, dg_o = refs[n_e:n_e + 3]
        dproj, dw, dg, stage = refs[2 * n_e + 3:2 * n_e + 7]
        copies = lambda: _exchange_copies(refs[:n_e], refs[n_e + 3:2 * n_e + 3], *refs[2 * n_e + 7:],
                                          (False,) * n_e, everyone)
        i = pl.program_id(0)

        @pl.when(i == 0)
        def _():
            dw[...] = jnp.zeros_like(dw)
            dg[...] = jnp.zeros_like(dg)
            if n_e:
                _start_copies(copies())

        drws_v = drws_ref[...]
        rid = lax.broadcasted_iota(jnp.int32, drws_v.shape, 0)
        nxt_row = jnp.where(i == n_t - 1, 0.0, nxt_ref[0:1, :])
        nxt = jnp.where(rid == tt - 1, jnp.broadcast_to(nxt_row, drws_v.shape), pltpu.roll(drws_v, tt - 1, axis=0))
        muv = mu_ref[...]
        drw = drws_v * (1.0 - muv) + nxt * muv
        dproj[:, 0:D_S5] = du_ref[...].astype(BF16)
        dproj[:, D_S5:2 * D_S5] = dzs_ref[...].astype(BF16)
        dproj[:, 2 * D_S5:2 * D_S5 + D_SHIFT] = drw.astype(BF16)
        dproj[:, 2 * D_S5 + D_SHIFT:D_IN] = dzr_ref[...].astype(BF16)
        dh = _dot(dproj[...], w_ref[...], ((1,), (0,)), None)
        h, vjp = jax.vjp(_rms, x_ref[...], g_ref[...])
        dxh, dgv = vjp(dh)
        gx_o[...] = dx2_ref[...] + dxh
        dg[...] += dgv
        dw[...] += _dot(dproj[...], h.astype(BF16), ((0,), (0,)), None)

        @pl.when(i == n_t - 1)
        def _():
            dg_o[...] = dg[...]
            for j in range(D_IN // stage.shape[0]):
                rows = pl.ds(j * stage.shape[0], stage.shape[0])
                stage[...] = dw[rows, :].astype(BF16)
                pltpu.sync_copy(stage, dw_o.at[rows])
            if n_e:
                _wait_copies(copies())

    row = lambda n: pl.BlockSpec((tt, n), lambda i: (i, 0))
    nxt = pl.BlockSpec((8, D_SHIFT), lambda i: (jnp.minimum((i + 1) * (tt // 8), L // 8 - 1), 0))
    return pl.pallas_call(
        body, name="bwd_in", grid=(n_t,),
        in_specs=[row(D_MODEL), _const_spec((1, D_MODEL)), _const_spec((D_IN, D_MODEL)), _const_spec((1, D_SHIFT)),
                  row(D_MODEL), row(D_S5), row(D_S5), row(D_SHIFT), nxt, row(D_RWKV)] + [_ANY] * n_e,
        out_specs=[row(D_MODEL), _ANY, _acc_spec((1, D_MODEL))] + [_ANY] * n_e,
        out_shape=[_sds((L, D_MODEL)), jax.ShapeDtypeStruct((D_IN, D_MODEL), BF16), _sds((1, D_MODEL))]
        + _exchange_results(early, (False,) * n_e, everyone),
        scratch_shapes=[pltpu.VMEM((tt, D_IN), BF16), pltpu.VMEM((D_IN, D_MODEL), F32), pltpu.VMEM((1, D_MODEL), F32),
                        pltpu.VMEM((D_IN // N_DEV, D_MODEL), BF16)] + (_exchange_scratch(n_e, everyone) if n_e else []),
        compiler_params=pltpu.CompilerParams(dimension_semantics=("arbitrary",), vmem_limit_bytes=VMEM_LIMIT,
                                             has_side_effects=bool(n_e)),
    )(x, norm_g, w_in_bf, mu, dx2, du, dzs, drws, drws, dzr, *early)


def _block_diag_b(bbar):
    bb = bbar.reshape(S5_GROUP, S5_BLOCKS, 8, S5_STATE)
    return jnp.einsum('hqgp,Gg->qGhgp', bb, jnp.eye(8, dtype=F32)).reshape(S5_BLOCKS, LANES, 512)


def _block_diag_b_t(db4):
    d = db4.reshape(S5_BLOCKS, 8, S5_GROUP, 8, S5_STATE)
    return jnp.einsum('qGhgp,Gg->hqgp', d, jnp.eye(8, dtype=F32)).reshape(S5_GROUP, N_STATE)


def _block_diag_c(c):
    cc = c.reshape(S5_BLOCKS, 8, S5_GROUP, S5_STATE)
    return jnp.einsum('qghp,gG->qgpGh', cc, jnp.eye(8, dtype=F32)).reshape(S5_BLOCKS, 512, LANES)


def _block_diag_c_t(dc4):
    d = dc4.reshape(S5_BLOCKS, 8, S5_STATE, 8, S5_GROUP)
    return jnp.einsum('qgpGh,gG->qghp', d, jnp.eye(8, dtype=F32)).reshape(S5_GROUPS, S5_GROUP, S5_STATE)


def _local_step(x, tgt, w, late=()):
    L = x.shape[0]
    tt = min(512, L)
    tp = min(256, L)
    ee = _head_sum_matrix()

    lam_re = w['s5_lam_re'].reshape(1, N_STATE)
    lam_im = w['s5_lam_im'].reshape(1, N_STATE)
    logdt = jnp.repeat(w['s5_log_dt'], S5_STATE).reshape(1, N_STATE)
    b_re_t = w['s5_b_re'].transpose(2, 0, 1).reshape(S5_GROUP, N_STATE)
    b_im_t = w['s5_b_im'].transpose(2, 0, 1).reshape(S5_GROUP, N_STATE)
    bbr, bbi, pow_r, pow_i, rpow_r, rpow_i = _s5_param_fwd(lam_re, lam_im, logdt, b_re_t, b_im_t)
    b4_re, b4_im = _block_diag_b(bbr), _block_diag_b(bbi)
    c4_re, c4_im = _block_diag_c(w['s5_c_re']), _block_diag_c(w['s5_c_im'])

    norm_g = w['norm_g'].reshape(1, D_MODEL)
    w_in_bf = (w['w_in_t'] if 'w_in_t' in w else w['w_in'].T).astype(BF16)
    u, zs, rw, zr, *gathered = _fwd_in(x, norm_g, w_in_bf, tt, [shard for _, shard in late])
    w = dict(w, **{n: _join_shards(n, blocks) for (n, _), blocks in zip(late, gathered)})
    s_re, s_im, y_ssm = _s5_fwd(u, b4_re, b4_im, c4_re, c4_im, pow_r, pow_i, tt)

    row = lambda t: t.reshape(1, -1)
    zpad = jnp.zeros((HEAD, D_RWKV), F32)
    w2p = jnp.concatenate([w['rwkv_w2'], zpad], axis=0)
    a2p = jnp.concatenate([zpad, w['rwkv_a2']], axis=0)
    pre_consts = (row(w['rwkv_mu']), row(w['rwkv_w0']), w2p, row(w['rwkv_a0']), a2p,
                  row(w['rwkv_k_k']), row(w['rwkv_k_a']), ee)
    ops = _rwkv_pre_fwd(rw, *pre_consts, tt)
    ysc, states, *kept = _rwkv_scan_fwd(ops)

    post = _post(x, u, zs, zr, ysc, ops[0], ops[2], ops[3], y_ssm,
                 row(w['s5_d']), w['s5_glu_w'].astype(BF16), row(w['s5_glu_b']), row(w['rwkv_ln_w']), row(w['rwkv_ln_b']),
                 row(w['rwkv_r_k']), w['w_out'].astype(BF16), row(w['final_g']), tgt, ee, tp)
    (dx2, du_d, dzs, dzr, dysc, dr_b, dk2_b, dv_b, dy_ssm,
     dd, dglu_w, dglu_b, dln_w, dln_b, dr_k, dw_out, dgf, loss) = post

    du, db4_re, db4_im, dc4_re, dc4_im, dlbr, dlbi = _s5_bwd(
        u, du_d, dy_ssm, s_re, s_im, b4_re, b4_im, c4_re, c4_im, rpow_r, rpow_i, tt)
    group_ind = (jnp.arange(N_STATE)[:, None] // S5_STATE == jnp.arange(LANES)[None, :]).astype(F32)
    dlam_re, dlam_im, dlogdt, db_re_t, db_im_t = _s5_param_bwd(
        lam_re, lam_im, logdt, b_re_t, b_im_t, dlbr, dlbi, _block_diag_b_t(db4_re), _block_diag_b_t(db4_im), group_ind)

    cots = list(_rwkv_scan_bwd(ops, states, kept, dysc)) + [dr_b, dk2_b, dv_b]
    drws, dmu, dw0, dw2p, da0, da2p, dk_k, dk_a = _rwkv_pre_bwd(rw, *pre_consts, cots, tt)

    ready = {'s5_glu_w': dglu_w, 'w_out': dw_out}
    early = [_split_shards(n, ready[n], shard.shape).astype(BF16) for n, shard in late]
    grad_x, dw_in, dnorm_g, *arrived = _bwd_in(x, norm_g, w_in_bf, row(w['rwkv_mu']), dx2, du, dzs, drws, dzr, tt, early)

    unb = lambda t: t.reshape(S5_GROUP, S5_GROUPS, S5_STATE).transpose(1, 2, 0)
    grads = {
        'norm_g': dnorm_g.reshape(D_MODEL), 'w_in_t': dw_in,
        's5_lam_re': dlam_re.reshape(S5_GROUPS, S5_STATE), 's5_lam_im': dlam_im.reshape(S5_GROUPS, S5_STATE),
        's5_log_dt': dlogdt[0, :S5_GROUPS], 's5_b_re': unb(db_re_t), 's5_b_im': unb(db_im_t),
        's5_c_re': _block_diag_c_t(dc4_re), 's5_c_im': _block_diag_c_t(dc4_im),
        's5_d': dd.reshape(D_S5), 's5_glu_w': dglu_w, 's5_glu_b': dglu_b.reshape(D_S5),
        'rwkv_mu': dmu.reshape(-1), 'rwkv_w0': dw0.reshape(-1), 'rwkv_w2': dw2p[:HEAD], 'rwkv_a0': da0.reshape(-1),
        'rwkv_a2': da2p[HEAD:], 'rwkv_k_k': dk_k.reshape(-1), 'rwkv_k_a': dk_a.reshape(-1),
        'rwkv_r_k': dr_k.reshape(N_HEADS, HEAD), 'rwkv_ln_w': dln_w.reshape(-1), 'rwkv_ln_b': dln_b.reshape(-1),
        'w_out': dw_out, 'final_g': dgf.reshape(D_MODEL),
    }
    grads.update({n: blocks for (n, _), blocks in zip(late, arrived)})
    return loss, grad_x, grads


def _exchange(arrays, gather, axes, name):
    n = len(arrays)

    def body(*refs):
        copies = _exchange_copies(refs[:n], refs[n:2 * n], *refs[2 * n:], gather, axes)
        _start_copies(copies)
        _wait_copies(copies)

    return pl.pallas_call(
        body, name=name, in_specs=[_ANY] * n, out_specs=[_ANY] * n,
        out_shape=_exchange_results(arrays, gather, axes), scratch_shapes=_exchange_scratch(n, axes),
        compiler_params=pltpu.CompilerParams(has_side_effects=True),
    )(*arrays)


def _exchange_results(arrays, gather, axes):
    return [jax.ShapeDtypeStruct(((2 ** len(axes),) + a.shape) if whole else a.shape, a.dtype)
            for a, whole in zip(arrays, gather)]


def _exchange_scratch(n, axes):
    peers = 2 ** len(axes) - 1
    return [pltpu.SemaphoreType.DMA((n, peers)), pltpu.SemaphoreType.DMA((n, peers)), pltpu.SemaphoreType.DMA((n,))]


def _exchange_copies(send_refs, recv_refs, send_sems, recv_sems, local_sems, gather, axes):
    pos = {ax: lax.axis_index(ax) for ax in ("x", "y", "c")}

    def index_of(p):
        idx = 0
        for ax in axes:
            idx = 2 * idx + p[ax]
        return idx

    me = index_of(pos)
    own, outs, arrivals = [], [], []
    for i, (send_ref, recv_ref) in enumerate(zip(send_refs, recv_refs)):
        def block_for(dev, send_ref=send_ref, whole=gather[i]):
            return send_ref if whole else send_ref.at[dev]

        own.append(pltpu.make_async_copy(block_for(me), recv_ref.at[me], local_sems.at[i]))
        for k in range(1, 2 ** len(axes)):
            peer = dict(pos)
            for bit, ax in enumerate(axes):
                if (k >> bit) & 1:
                    peer[ax] = 1 - pos[ax]
            peer_idx = index_of(peer)
            sems = dict(send_sem=send_sems.at[i, k - 1], recv_sem=recv_sems.at[i, k - 1],
                        device_id=(peer["x"], peer["y"], peer["c"]), device_id_type=pl.DeviceIdType.MESH)
            outs.append(pltpu.make_async_remote_copy(src_ref=block_for(peer_idx), dst_ref=recv_ref.at[me], **sems))
            arrivals.append(
                pltpu.make_async_remote_copy(src_ref=block_for(peer_idx), dst_ref=recv_ref.at[peer_idx], **sems))
    return own, outs, arrivals


def _start_copies(copies):
    own, outs, _ = copies
    for copy in own + outs:
        copy.start()


def _wait_copies(copies):
    own, outs, arrivals = copies
    for copy in arrivals:
        copy.wait_recv()
    for copy in outs:
        copy.wait_send()
    for copy in own:
        copy.wait()


def _sum_devices(ref):
    g = ref[0].astype(F32)
    for s in range(1, ref.shape[0]):
        g = g + ref[s].astype(F32)
    return g


def _adamw_math(g, w, m, v):
    m_new = ADAM_B1 * m + (1.0 - ADAM_B1) * g
    v_new = ADAM_B2 * v + (1.0 - ADAM_B2) * (g * g)
    m_hat = m_new / (1.0 - ADAM_B1 ** ADAM_STEP)
    v_hat = v_new / (1.0 - ADAM_B2 ** ADAM_STEP)
    return -ADAM_LR * (m_hat / (jnp.sqrt(v_hat) + ADAM_EPS) + ADAM_WD * w), m_new, v_new


def _adamw(gs, ws, ms, vs, reduce, name):
    n = len(ws)

    def body(*refs):
        g_refs, w_refs, m_refs, v_refs = (refs[j * n:(j + 1) * n] for j in range(4))
        outs = refs[4 * n:]
        for i in range(n):
            g = _sum_devices(g_refs[i]) if reduce else g_refs[i][...]
            res = _adamw_math(g, w_refs[i][...], m_refs[i][...], v_refs[i][...])
            for j, val in enumerate(((g,) if reduce else ()) + res):
                outs[j * n + i][...] = val

    return pl.pallas_call(
        body, name=name, out_shape=[_sds(w.shape) for w in ws] * (4 if reduce else 3),
        compiler_params=pltpu.CompilerParams(vmem_limit_bytes=VMEM_LIMIT),
    )(*gs, *ws, *ms, *vs)


def _sum_blocks(recv):
    def body(recv_ref, out_ref):
        out_ref[...] = _sum_devices(recv_ref)

    return pl.pallas_call(body, name="sum_small_grads", out_shape=_sds(recv.shape[1:]))(recv)


_WEIGHTS = [
    ('norm_g', (1, 1024), False), ('w_in', (1, 1024, 400), True), ('s5_lam_re', (1, 32, 64), False),
    ('s5_lam_im', (1, 32, 64), False), ('s5_log_dt', (1, 32), False), ('s5_b_re', (1, 32, 64, 16), False),
    ('s5_b_im', (1, 32, 64, 16), False), ('s5_c_re', (1, 32, 16, 64), False), ('s5_c_im', (1, 32, 16, 64), False),
    ('s5_d', (1, 512), False), ('s5_glu_w', (1, 64, 512), True), ('s5_glu_b', (1, 512), False),
    ('rwkv_mu', (1, 1664), False), ('rwkv_w0', (1, 512), False), ('rwkv_w2', (1, 64, 64), True),
    ('rwkv_a0', (1, 512), False), ('rwkv_a2', (1, 64, 64), True), ('rwkv_k_k', (1, 512), False),
    ('rwkv_k_a', (1, 512), False), ('rwkv_r_k', (1, 8, 64), False), ('rwkv_ln_w', (1, 512), False),
    ('rwkv_ln_b', (1, 512), False), ('w_out', (1, 128, 1024), True), ('final_g', (1024,), False),
]
_SHARDED = [(n, s) for n, s, sharded in _WEIGHTS if sharded]
_SMALL = [(n, s) for n, s, sharded in _WEIGHTS if not sharded]
_COLUMN_SHARDED = ('w_in', 'rwkv_w2', 'rwkv_a2')
_SMALL_SIZE = sum(math.prod(s) for _, s in _SMALL) + 1
_SMALL_ROWS = -(-_SMALL_SIZE // (8 * LANES)) * 8


_MINOR_SWAPPED = ('s5_b_re', 's5_b_im')


def _stored(name, t):
    return jnp.swapaxes(t, -1, -2) if name in _MINOR_SWAPPED else t


def _pack_small(grads, loss):
    flat = [_stored(n, grads[n]).reshape(-1) for n, _ in _SMALL] + [loss.reshape(1)]
    pad = _SMALL_ROWS * LANES - _SMALL_SIZE
    return jnp.concatenate(flat + [jnp.zeros((pad,), F32)]).reshape(_SMALL_ROWS, LANES)


def _unpack_small(packed):
    flat = packed.reshape(-1)
    out, off = {}, 0
    for n, s in _SMALL:
        size = math.prod(s)
        out[n] = flat[off:off + size].reshape(s[:-2] + (s[-1], s[-2]) if n in _MINOR_SWAPPED else s)
        off += size
    return out, flat[off]


_BF16_OPERANDS = ('w_in', 's5_glu_w', 'w_out')
_LATE_WEIGHTS = ('s5_glu_w', 'w_out')


def _join_shards(name, blocks):
    _, rows, cols = blocks.shape
    if name in _COLUMN_SHARDED:
        return blocks.transpose(1, 0, 2).reshape(rows, N_DEV * cols)
    return blocks.reshape(N_DEV * rows, cols)


def _split_shards(name, full, shard_shape):
    rows, cols = shard_shape
    if name in _COLUMN_SHARDED:
        return full.reshape(rows, N_DEV, cols).transpose(1, 0, 2)
    return full.reshape(N_DEV, rows, cols)


def kernel(x, norm_g, w_in, s5_lam_re, s5_lam_im, s5_log_dt, s5_b_re, s5_b_im, s5_c_re, s5_c_im, s5_d, s5_glu_w, s5_glu_b, rwkv_mu, rwkv_w0, rwkv_w2, rwkv_a0, rwkv_a2, rwkv_k_k, rwkv_k_a, rwkv_r_k, rwkv_ln_w, rwkv_ln_b, w_out, final_g, loss_target, m_norm_g, m_w_in, m_s5_lam_re, m_s5_lam_im, m_s5_log_dt, m_s5_b_re, m_s5_b_im, m_s5_c_re, m_s5_c_im, m_s5_d, m_s5_glu_w, m_s5_glu_b, m_rwkv_mu, m_rwkv_w0, m_rwkv_w2, m_rwkv_a0, m_rwkv_a2, m_rwkv_k_k, m_rwkv_k_a, m_rwkv_r_k, m_rwkv_ln_w, m_rwkv_ln_b, m_w_out, m_final_g, v_norm_g, v_w_in, v_s5_lam_re, v_s5_lam_im, v_s5_log_dt, v_s5_b_re, v_s5_b_im, v_s5_c_re, v_s5_c_im, v_s5_d, v_s5_glu_w, v_s5_glu_b, v_rwkv_mu, v_rwkv_w0, v_rwkv_w2, v_rwkv_a0, v_rwkv_a2, v_rwkv_k_k, v_rwkv_k_a, v_rwkv_r_k, v_rwkv_ln_w, v_rwkv_ln_b, v_w_out, v_final_g):
    given = dict(locals())

    n_sh = len(_SHARDED)
    everyone = ("x", "y", "c")
    shards = [given[n][0].astype(BF16 if n in _BF16_OPERANDS else F32) for n, _ in _SHARDED]
    shards[0] = shards[0].T
    names = [n for n, _ in _SHARDED]
    first = [i for i, n in enumerate(names) if n not in _LATE_WEIGHTS]
    gathered = _exchange([shards[i] for i in first], (True,) * len(first), everyone, "gather_weights")
    local = {names[i]: _join_shards(names[i], blocks) for i, blocks in list(zip(first, gathered))[1:]}
    local['w_in_t'] = gathered[0].reshape(D_IN, D_MODEL)
    local.update({n: (given[n][0] if len(s) > 1 else given[n]) for n, s in _SMALL})
    late = [(n, shards[names.index(n)]) for n in _LATE_WEIGHTS]

    loss, grad_x, grads = _local_step(x[0], loss_target[0], local, late)

    blocks = [grads['w_in_t'].reshape(N_DEV, D_IN // N_DEV, D_MODEL)]
    blocks += [_split_shards(names[i], grads[names[i]], _SHARDED[i][1][1:]).astype(BF16) for i in first[1:]]
    small = _pack_small(grads, loss[0, 0]).reshape(N_DEV, _SMALL_ROWS // N_DEV, LANES)
    recv = _exchange(blocks + [small], (False,) * (len(first) + 1), everyone, "exchange_grads")
    small_sum = _exchange([_sum_blocks(recv[-1])], (True,), everyone, "gather_small_grads")[0]
    arrived = dict(zip([names[i] for i in first], recv), **{n: grads[n] for n in _LATE_WEIGHTS})

    result = {}
    for group, name in (([0], "adamw_w_in"), ([1, 2, 3, 4], "adamw_shards")):
        ns = [_SHARDED[i][0] for i in group]
        own = (lambda t: t[0].T) if group == [0] else (lambda t: t[0])
        back = (lambda t: t.T[None]) if group == [0] else (lambda t: t[None])
        res = _adamw([arrived[n] for n in ns], [own(given[n]) for n in ns], [own(given['m_' + n]) for n in ns],
                     [own(given['v_' + n]) for n in ns], True, name)
        for j, n in enumerate(ns):
            result[n] = [back(res[k * len(ns) + j]) for k in range(4)]
    g_small, total = _unpack_small(small_sum)
    two_d = lambda t: t.reshape(1, -1) if t.ndim == 1 else t
    ns = [n for n, _ in _SMALL]
    res = _adamw([two_d(g_small[n]) for n in ns], [two_d(_stored(n, given[n])) for n in ns],
                 [two_d(_stored(n, given['m_' + n])) for n in ns], [two_d(_stored(n, given['v_' + n])) for n in ns],
                 False, "adamw_small")
    for j, (n, s) in enumerate(_SMALL):
        result[n] = [_stored(n, g_small[n])] + [_stored(n, res[k * len(ns) + j]).reshape(s) for k in range(3)]

    outs = [total, grad_x[None]]
    for k in range(4):
        outs += [result[n][k] for n, _, _ in _WEIGHTS]
    return tuple(outs)
```

```python
import math

import jax
import jax.numpy as jnp
from jax import lax
from jax.experimental import pallas as pl
from jax.experimental.pallas import tpu as pltpu

F32 = jnp.float32
BF16 = jnp.bfloat16
HI = lax.Precision.HIGH

D_MODEL = 1024
D_S5 = 512
D_RWKV = 512
S5_GROUPS = 32
S5_GROUP = 16
S5_STATE = 64
N_STATE = S5_GROUPS * S5_STATE
N_HEADS = 8
HEAD = 64
D_SHIFT = 3 * D_RWKV + 128
D_IN = 2 * D_S5 + D_SHIFT + D_RWKV
NORM_EPS = 1e-6
GN_EPS = 64e-5
N_DEV = 8
LANES = 128
S5_BLOCKS = 4
RWKV_CHUNK = 64
RWKV_CHUNKS_PER_STEP = 4
VMEM_LIMIT = 56 * 1024 * 1024

ADAM_LR = 0.001
ADAM_B1 = 0.9
ADAM_B2 = 0.999
ADAM_EPS = 1e-08
ADAM_WD = 0.01
ADAM_STEP = 10


def _dot(a, b, dims, prec):
    return lax.dot_general(a, b, (dims, ((), ())), precision=prec, preferred_element_type=F32)


def _dot_bf(a, b, dims):
    return _dot(a.astype(BF16), b.astype(BF16), dims, None)


def _make_mm(cast, prec):
    @jax.custom_vjp
    def mm(a, b):
        return _dot(cast(a), cast(b), ((1,), (0,)), prec)

    def fwd(a, b):
        return mm(a, b), (a, b)

    def bwd(res, g):
        a, b = res
        return (_dot(cast(g), cast(b), ((1,), (1,)), prec), _dot(cast(a), cast(g), ((0,), (0,)), prec))

    mm.defvjp(fwd, bwd)
    return mm


mm_bf = _make_mm(lambda t: t.astype(BF16), None)


@jax.custom_vjp
def mm_w(a, w):
    return _dot(a.astype(BF16), w, ((1,), (0,)), None)


def _mm_w_fwd(a, w):
    return mm_w(a, w), w


def _mm_w_bwd(w, g):
    return _dot(g.astype(BF16), w, ((1,), (1,)), None), jnp.zeros_like(w)


mm_w.defvjp(_mm_w_fwd, _mm_w_bwd)


def _make_head_sum(split):
    def product(x, ee):
        hi = x.astype(BF16)
        out = _dot(hi, ee, ((1,), (0,)), None)
        if split:
            out = out + _dot((x - hi.astype(F32)).astype(BF16), ee, ((1,), (0,)), None)
        return out

    @jax.custom_vjp
    def head_sum(x, ee):
        return product(x, ee)

    def fwd(x, ee):
        return product(x, ee), ee

    def bwd(ee, g):
        return product(g, ee), jnp.zeros_like(ee)

    head_sum.defvjp(fwd, bwd)
    return head_sum


head_sum = _make_head_sum(False)
head_sum_split = _make_head_sum(True)


@jax.custom_vjp
def _sigmoid(x):
    return 1.0 / (1.0 + jnp.exp(-x))


def _sigmoid_fwd(x):
    s = _sigmoid(x)
    return s, s


_sigmoid.defvjp(_sigmoid_fwd, lambda s, g: (g * s * (1.0 - s),))


@jax.custom_vjp
def _silu(x):
    return x * _sigmoid(x)


def _silu_fwd(x):
    s = _sigmoid(x)
    return x * s, (x, s)


def _silu_bwd(res, g):
    x, s = res
    return (g * s * (1.0 + x * (1.0 - s)),)


_silu.defvjp(_silu_fwd, _silu_bwd)


@jax.custom_vjp
def _softplus(x):
    return jnp.maximum(x, 0.0) + jnp.log(1.0 + jnp.exp(-jnp.abs(x)))


def _softplus_fwd(x):
    e = jnp.exp(-jnp.abs(x))
    return jnp.maximum(x, 0.0) + jnp.log(1.0 + e), (x, e)


def _softplus_bwd(res, g):
    x, e = res
    return (g * jnp.where(x >= 0.0, 1.0, e) / (1.0 + e),)


_softplus.defvjp(_softplus_fwd, _softplus_bwd)


@jax.custom_vjp
def _normalize_heads(x, ee):
    return x / jnp.maximum(jnp.sqrt(head_sum(x * x, ee)), 1e-12)


def _normalize_heads_fwd(x, ee):
    norm = jnp.sqrt(head_sum(x * x, ee))
    inv = 1.0 / jnp.maximum(norm, 1e-12)
    y = x * inv
    return y, (y, inv, norm, ee)


def _normalize_heads_bwd(res, g):
    y, inv, norm, ee = res
    along = jnp.where(norm > 1e-12, head_sum(g * y, ee), 0.0)
    return inv * (g - y * along), jnp.zeros_like(ee)


_normalize_heads.defvjp(_normalize_heads_fwd, _normalize_heads_bwd)


_GELU_C = 2.0 * math.sqrt(2.0 / math.pi)


def _gelu_gate(x):
    return 1.0 / (1.0 + jnp.exp(-_GELU_C * x * (1.0 + 0.044715 * (x * x))))


@jax.custom_vjp
def _gelu(x):
    return x * _gelu_gate(x)


def _gelu_fwd(x):
    s = _gelu_gate(x)
    return x * s, (x, s)


def _gelu_bwd(res, g):
    x, s = res
    return (g * (s + x * s * (1.0 - s) * (_GELU_C * (1.0 + 3.0 * 0.044715 * (x * x)))),)


_gelu.defvjp(_gelu_fwd, _gelu_bwd)


def _rms(x, g):
    return x * lax.rsqrt(jnp.mean(x * x, axis=-1, keepdims=True) + NORM_EPS) * g


def _const_spec(shape):
    nd = len(shape)
    return pl.BlockSpec(shape, lambda *_: (0,) * nd, pipeline_mode=pl.Buffered(1))


def _acc_spec(shape):
    nd = len(shape)
    return pl.BlockSpec(shape, lambda *_: (0,) * nd)


def _params(sem):
    return pltpu.CompilerParams(dimension_semantics=(sem,), vmem_limit_bytes=VMEM_LIMIT)


_ANY = pl.BlockSpec(memory_space=pl.ANY)


def _sds(shape):
    return jax.ShapeDtypeStruct(shape, F32)


def _head_sum_matrix():
    i = jnp.arange(D_RWKV) // HEAD
    return (i[:, None] == i[None, :]).astype(BF16)


def _s5_param_fn(lam_re, lam_im, logdt, b_re, b_im):
    dt = jnp.exp(logdt)
    mag = jnp.exp(lam_re * dt)
    ang = lam_im * dt
    lbr = mag * jnp.cos(ang)
    lbi = mag * jnp.sin(ang)
    nr = lbr - 1.0
    den = lam_re * lam_re + lam_im * lam_im
    cr = (nr * lam_re + lbi * lam_im) / den
    ci = (lbi * lam_re - nr * lam_im) / den
    return lbr, lbi, cr * b_re - ci * b_im, cr * b_im + ci * b_re


def _cmul(ar, ai, br, bi):
    return ar * br - ai * bi, ar * bi + ai * br


def _s5_param_fwd(lam_re, lam_im, logdt, b_re, b_im):
    def body(lr, li, ld, br, bi, o_br, o_bi, o_pr, o_pi, o_qr, o_qi):
        lbr, lbi, bbr, bbi = _s5_param_fn(lr[...], li[...], ld[...], br[...], bi[...])
        o_br[...] = bbr
        o_bi[...] = bbi
        rid = lax.broadcasted_iota(jnp.int32, (8, N_STATE), 0)
        pr, pi_ = lbr, lbi
        fwd_r = rev_r = jnp.broadcast_to(pr, (8, N_STATE))
        fwd_i = rev_i = jnp.broadcast_to(pi_, (8, N_STATE))
        for j in range(1, 8):
            pr, pi_ = _cmul(pr, pi_, lbr, lbi)
            fwd_r = jnp.where(rid == j, jnp.broadcast_to(pr, (8, N_STATE)), fwd_r)
            fwd_i = jnp.where(rid == j, jnp.broadcast_to(pi_, (8, N_STATE)), fwd_i)
            rev_r = jnp.where(rid == 7 - j, jnp.broadcast_to(pr, (8, N_STATE)), rev_r)
            rev_i = jnp.where(rid == 7 - j, jnp.broadcast_to(pi_, (8, N_STATE)), rev_i)
        o_pr[...] = fwd_r
        o_pi[...] = fwd_i
        o_qr[...] = rev_r
        o_qi[...] = -rev_i

    return pl.pallas_call(
        body, name="s5_param_fwd",
        out_shape=[_sds((S5_GROUP, N_STATE))] * 2 + [_sds((8, N_STATE))] * 4,
    )(lam_re, lam_im, logdt, b_re, b_im)


def _s5_param_bwd(lam_re, lam_im, logdt, b_re, b_im, d_lbr, d_lbi, d_bbr, d_bbi, group_ind):
    def body(lr, li, ld, br, bi, g0, g1, g2, g3, ind, o_lr, o_li, o_ld, o_br, o_bi):
        _, vjp = jax.vjp(_s5_param_fn, lr[...], li[...], ld[...], br[...], bi[...])
        d_lr, d_li, d_ld, d_br, d_bi = vjp((g0[...], g1[...], g2[...], g3[...]))
        o_lr[...] = d_lr
        o_li[...] = d_li
        o_ld[...] = _dot(jnp.broadcast_to(d_ld, (8, N_STATE)), ind[...], ((1,), (0,)), HI)
        o_br[...] = d_br
        o_bi[...] = d_bi

    return pl.pallas_call(
        body, name="s5_param_bwd",
        out_shape=[_sds((1, N_STATE))] * 2 + [_sds((8, LANES))] + [_sds((S5_GROUP, N_STATE))] * 2,
    )(lam_re, lam_im, logdt, b_re, b_im, d_lbr, d_lbi, d_bbr, d_bbi, group_ind)


def _fwd_in(x, norm_g, w_in_bf, tt, late=()):
    L = x.shape[0]
    n_t = L // tt
    n_late = len(late)
    everyone = ("x", "y", "c")

    def body(x_ref, g_ref, w_ref, *refs):
        u_ref, zs_ref, rw_ref, zr_ref = refs[n_late:n_late + 4]
        if n_late:
            i = pl.program_id(0)
            copies = lambda: _exchange_copies(refs[:n_late], refs[n_late + 4:2 * n_late + 4], *refs[2 * n_late + 4:],
                                              (True,) * n_late, everyone)
            pl.when(i == 0)(lambda: _start_copies(copies()))
            pl.when(i == n_t - 1)(lambda: _wait_copies(copies()))
        h = _rms(x_ref[...], g_ref[...])
        proj = _dot(h.astype(BF16), w_ref[...], ((1,), (1,)), None)
        u_ref[...] = proj[:, 0:D_S5]
        zs_ref[...] = proj[:, D_S5:2 * D_S5]
        rw_ref[...] = proj[:, 2 * D_S5:2 * D_S5 + D_SHIFT]
        zr_ref[...] = proj[:, 2 * D_S5 + D_SHIFT:D_IN]

    row = lambda n: pl.BlockSpec((tt, n), lambda i: (i, 0))
    return pl.pallas_call(
        body, name="fwd_in", grid=(n_t,),
        in_specs=[row(D_MODEL), _const_spec((1, D_MODEL)), _const_spec((D_IN, D_MODEL))] + [_ANY] * n_late,
        out_specs=[row(D_S5), row(D_S5), row(D_SHIFT), row(D_RWKV)] + [_ANY] * n_late,
        out_shape=[_sds((L, D_S5)), _sds((L, D_S5)), _sds((L, D_SHIFT)), _sds((L, D_RWKV))]
        + _exchange_results(late, (True,) * n_late, everyone),
        scratch_shapes=_exchange_scratch(n_late, everyone) if n_late else [],
        compiler_params=pltpu.CompilerParams(dimension_semantics=("arbitrary",), vmem_limit_bytes=VMEM_LIMIT,
                                             has_side_effects=bool(n_late)),
    )(x, norm_g, w_in_bf, *late)


S5_LANE_CHUNK = 512


def _tile_scan(re_ref, im_ref, pow_r_ref, pow_i_ref, carry_r_ref, carry_i_ref, reverse):
    t, n = re_ref.shape
    n_groups = t // 8
    ch = S5_LANE_CHUNK
    rid = lax.broadcasted_iota(jnp.int32, (8, ch), 0)
    for c in range(n // ch):
        cols = slice(c * ch, (c + 1) * ch)
        pow_r = pow_r_ref[:, cols]
        pow_i = pow_i_ref[:, cols]
        row = lambda tile, j: jnp.broadcast_to(tile[j:j + 1], (8, ch))
        levels = []
        for d in (1, 2, 4):
            keep = (rid < 8 - d) if reverse else (rid >= d)
            j = 8 - d if reverse else d - 1
            levels.append(((8 - d) if reverse else d,
                           jnp.where(keep, row(pow_r, j), 0.0), jnp.where(keep, row(pow_i, j), 0.0)))

        def group(g, carry):
            r0 = pl.multiple_of(((n_groups - 1 - g) if reverse else g) * 8, 8)
            xr = re_ref[pl.ds(r0, 8), cols]
            xi = im_ref[pl.ds(r0, 8), cols]
            for shift, lr, li in levels:
                mr, mi = _cmul(lr, li, pltpu.roll(xr, shift, axis=0), pltpu.roll(xi, shift, axis=0))
                xr = xr + mr
                xi = xi + mi
            mr, mi = _cmul(pow_r, pow_i, carry[0], carry[1])
            xr = xr + mr
            xi = xi + mi
            re_ref[pl.ds(r0, 8), cols] = xr
            im_ref[pl.ds(r0, 8), cols] = xi
            last = 0 if reverse else 7
            return row(xr, last), row(xi, last)

        out = lax.fori_loop(0, n_groups, group, (carry_r_ref[:, cols], carry_i_ref[:, cols]))
        carry_r_ref[:, cols] = out[0]
        carry_i_ref[:, cols] = out[1]


def _s5_fwd(u, b4_re, b4_im, c4_re, c4_im, pow_r, pow_i, tt):
    L = u.shape[0]

    def body(u_ref, bre_ref, bim_ref, cre_ref, cim_ref, pr_ref, pi_ref, sre_o, sim_o, y_o, car_r, car_i):
        @pl.when(pl.program_id(0) == 0)
        def _():
            car_r[...] = jnp.zeros_like(car_r)
            car_i[...] = jnp.zeros_like(car_i)

        uv = u_ref[...]
        for q in range(S5_BLOCKS):
            uq = uv[:, q * LANES:(q + 1) * LANES]
            cols = slice(q * 512, (q + 1) * 512)
            sre_o[:, cols] = _dot_bf(uq, bre_ref[q], ((1,), (0,)))
            sim_o[:, cols] = _dot_bf(uq, bim_ref[q], ((1,), (0,)))
        _tile_scan(sre_o, sim_o, pr_ref, pi_ref, car_r, car_i, reverse=False)
        for q in range(S5_BLOCKS):
            cols = slice(q * 512, (q + 1) * 512)
            y_o[:, q * LANES:(q + 1) * LANES] = (_dot_bf(sre_o[:, cols], cre_ref[q], ((1,), (0,)))
                                                 - _dot_bf(sim_o[:, cols], cim_ref[q], ((1,), (0,))))

    row = lambda n: pl.BlockSpec((tt, n), lambda i: (i, 0))
    return pl.pallas_call(
        body, name="s5_fwd", grid=(L // tt,),
        in_specs=[row(D_S5)] + [_const_spec((S5_BLOCKS, LANES, 512))] * 2 + [_const_spec((S5_BLOCKS, 512, LANES))] * 2
        + [_const_spec((8, N_STATE))] * 2,
        out_specs=[row(N_STATE), row(N_STATE), row(D_S5)],
        out_shape=[_sds((L, N_STATE)), _sds((L, N_STATE)), _sds((L, D_S5))],
        scratch_shapes=[pltpu.VMEM((8, N_STATE), F32)] * 2,
        compiler_params=_params("arbitrary"),
    )(u, b4_re, b4_im, c4_re, c4_im, pow_r, pow_i)


def _rwkv_pre_fn(r, k, v, wa, w0, w2p, a0, a2p, k_k, k_a, ee):
    w = -_softplus(-(w0 + mm_bf(jnp.tanh(wa), w2p))) - 0.5
    logw = -jnp.exp(w)
    a = _sigmoid(a0 + mm_bf(wa, a2p))
    kk = _normalize_heads(k * k_k, ee)
    k2 = k * (1.0 + (a - 1.0) * k_a)
    return r, logw, k2, v, -kk, kk * a


N_PAIRS = N_HEADS // 2


def _head_spec(tt):
    return pl.BlockSpec((N_PAIRS, tt, LANES), lambda i: (0, i, 0))


def _load_heads(ref):
    return jnp.concatenate([ref[p] for p in range(N_PAIRS)], axis=-1)


def _store_heads(ref, val):
    for p in range(N_PAIRS):
        ref[p] = val[:, p * LANES:(p + 1) * LANES]


def _split_pairs(x):
    return jnp.concatenate([x[:, :, :HEAD], x[:, :, HEAD:]], axis=0)


def _join_pairs(x):
    return jnp.concatenate([x[:N_PAIRS], x[N_PAIRS:]], axis=-1)


def _shifted(rw, prev_blk, first):
    rolled = pltpu.roll(rw, 1, axis=0)
    prev_row = jnp.where(first, 0.0, prev_blk[7:8, :])
    rid = lax.broadcasted_iota(jnp.int32, rw.shape, 0)
    return jnp.where(rid == 0, jnp.broadcast_to(prev_row, rw.shape), rolled)


def _split_rw(t):
    return t[:, 0:512], t[:, 512:1024], t[:, 1024:1536], t[:, 1536:1664]


def _rwkv_pre_specs(tt):
    row = pl.BlockSpec((tt, D_SHIFT), lambda i: (i, 0))
    prev = pl.BlockSpec((8, D_SHIFT), lambda i: (jnp.maximum(i * (tt // 8) - 1, 0), 0))
    consts = [_const_spec((1, D_SHIFT)), _const_spec((1, D_RWKV)), _const_spec((LANES, D_RWKV)),
              _const_spec((1, D_RWKV)), _const_spec((LANES, D_RWKV)), _const_spec((1, D_RWKV)),
              _const_spec((1, D_RWKV)), _const_spec((D_RWKV, D_RWKV))]
    return [row, prev] + consts


def _rwkv_pre_fwd(rw, mu, w0, w2p, a0, a2p, k_k, k_a, ee, tt):
    L = rw.shape[0]

    def body(rw_ref, prev_ref, mu_ref, w0_ref, w2_ref, a0_ref, a2_ref, kk_ref, ka_ref, ee_ref, *outs):
        rwv = rw_ref[...]
        rws = rwv + (_shifted(rwv, prev_ref[...], pl.program_id(0) == 0) - rwv) * mu_ref[...]
        res = _rwkv_pre_fn(*_split_rw(rws), w0_ref[...], w2_ref[...], a0_ref[...], a2_ref[...],
                           kk_ref[...], ka_ref[...], ee_ref[...])
        for o, val in zip(outs, res):
            _store_heads(o, val)

    return pl.pallas_call(
        body, name="rwkv_pre_fwd", grid=(L // tt,),
        in_specs=_rwkv_pre_specs(tt), out_specs=[_head_spec(tt)] * 6, out_shape=[_sds((N_PAIRS, L, LANES))] * 6,
        compiler_params=_params("parallel"),
    )(rw, rw, mu, w0, w2p, a0, a2p, k_k, k_a, ee)


def _rwkv_pre_bwd(rw, mu, w0, w2p, a0, a2p, k_k, k_a, ee, cots, tt):
    L = rw.shape[0]
    n_t = L // tt

    def body(rw_ref, prev_ref, mu_ref, w0_ref, w2_ref, a0_ref, a2_ref, kk_ref, ka_ref, ee_ref,
             c_r, c_w, c_k, c_v, c_a, c_b, cb_r, cb_k, cb_v,
             drws_ref, dmu_o, dw0_o, dw2_o, da0_o, da2_o, dkk_o, dka_o,
             dmu, dw0, dw2, da0, da2, dkk, dka):
        i = pl.program_id(0)
        accs = (dmu, dw0, dw2, da0, da2, dkk, dka)

        @pl.when(i == 0)
        def _():
            for acc in accs:
                acc[...] = jnp.zeros_like(acc)

        rwv = rw_ref[...]
        diff = _shifted(rwv, prev_ref[...], i == 0) - rwv
        rws = rwv + diff * mu_ref[...]
        consts = (w0_ref[...], w2_ref[...], a0_ref[...], a2_ref[...], kk_ref[...], ka_ref[...])
        _, vjp = jax.vjp(lambda *a: _rwkv_pre_fn(*a, ee_ref[...]), *_split_rw(rws), *consts)
        scan = [_load_heads(c) for c in (c_r, c_w, c_k, c_v, c_a, c_b)]
        g = vjp((scan[0] + cb_r[...], scan[1], scan[2] + cb_k[...], scan[3] + cb_v[...], scan[4], scan[5]))
        drws = jnp.concatenate(g[0:4], axis=-1)
        drws_ref[...] = drws
        dmu[...] += jnp.sum(drws * diff, axis=0, keepdims=True)
        for acc, val in zip(accs[1:], g[4:]):
            acc[...] += val

        @pl.when(i == n_t - 1)
        def _():
            for acc, out in zip(accs, (dmu_o, dw0_o, dw2_o, da0_o, da2_o, dkk_o, dka_o)):
                out[...] = acc[...]

    row = pl.BlockSpec((tt, D_RWKV), lambda i: (i, 0))
    shapes = [(1, D_SHIFT), (1, D_RWKV), (LANES, D_RWKV), (1, D_RWKV), (LANES, D_RWKV), (1, D_RWKV), (1, D_RWKV)]
    return pl.pallas_call(
        body, name="rwkv_pre_bwd", grid=(n_t,),
        in_specs=_rwkv_pre_specs(tt) + [_head_spec(tt)] * 6 + [row] * 3,
        out_specs=[pl.BlockSpec((tt, D_SHIFT), lambda i: (i, 0))] + [_acc_spec(s) for s in shapes],
        out_shape=[_sds((L, D_SHIFT))] + [_sds(s) for s in shapes],
        scratch_shapes=[pltpu.VMEM(s, F32) for s in shapes],
        compiler_params=_params("arbitrary"),
    )(rw, rw, mu, w0, w2p, a0, a2p, k_k, k_a, ee, *cots)


def _bmm(a, b):
    return lax.dot_general(a, b, (((2,), (1,)), ((0,), (0,))), precision=HI, preferred_element_type=F32)


def _bmm_nt(a, b):
    return lax.dot_general(a, b, (((2,), (2,)), ((0,), (0,))), precision=HI, preferred_element_type=F32)


def _bmm_tn(a, b):
    return lax.dot_general(a, b, (((1,), (1,)), ((0,), (0,))), precision=HI, preferred_element_type=F32)


def _bdot_bf(a, b, lhs_dim, rhs_dim):
    return lax.dot_general(a.astype(BF16), b.astype(BF16), (((lhs_dim,), (rhs_dim,)), ((0,), (0,))),
                           preferred_element_type=F32)


@jax.custom_vjp
def _bmm_bf(a, b):
    return _bdot_bf(a, b, 2, 1)


def _bmm_bf_fwd(a, b):
    return _bmm_bf(a, b), (a, b)


def _bmm_bf_bwd(res, g):
    a, b = res
    return _bdot_bf(g, b, 2, 2), _bdot_bf(a, g, 1, 1)


_bmm_bf.defvjp(_bmm_bf_fwd, _bmm_bf_bwd)


@jax.custom_vjp
def _bmm_tn_bf(a, b):
    return _bdot_bf(a, b, 1, 1)


def _bmm_tn_bf_fwd(a, b):
    return _bmm_tn_bf(a, b), (a, b)


def _bmm_tn_bf_bwd(res, g):
    a, b = res
    return _bdot_bf(b, g, 2, 2), _bdot_bf(a, g, 2, 1)


_bmm_tn_bf.defvjp(_bmm_tn_bf_fwd, _bmm_tn_bf_bwd)


def _unit_lower_inverse(a):
    t = a.shape[-1]
    ti = lax.broadcasted_iota(jnp.int32, (t, t), 0)
    si = lax.broadcasted_iota(jnp.int32, (t, t), 1)

    def same_block(bits):
        shift = jnp.int32(bits)
        return (lax.shift_right_logical(ti, shift) == lax.shift_right_logical(si, shift))[None]

    def mm(x, y):
        return _bdot_bf(x, y, 2, 1)

    d = jnp.where(same_block(3), a, 0.0)
    inv = jnp.where(ti == si, 1.0, 0.0)[None] + d
    pw = mm(d, d)
    both = mm(jnp.concatenate([inv, pw], axis=1), pw)
    inv = inv + both[:, :t]
    inv = inv + mm(inv, both[:, t:])
    bits = 3
    while (1 << bits) < t:
        e = jnp.where(same_block(bits), 0.0, jnp.where(same_block(bits + 1), a, 0.0))
        inv = inv + mm(mm(inv, e), inv)
        bits += 1
    return inv


def _tri_mask(t):
    ri = lax.broadcasted_iota(jnp.int32, (2 * t, 2 * t), 0)
    ci = lax.broadcasted_iota(jnp.int32, (2 * t, 2 * t), 1)
    top_rows = ri < t
    diff = jnp.where(top_rows, ri, ri - t) - jnp.where(ci < t, ci, ci - t)
    return (diff >= jnp.where(top_rows, 1, 0))[None]


def _ones_tri(n_h, t):
    ti = lax.broadcasted_iota(jnp.int32, (t, t), 0)
    si = lax.broadcasted_iota(jnp.int32, (t, t), 1)
    return jnp.broadcast_to(jnp.where(ti >= si, 1.0, 0.0)[None], (n_h, t, t))


@jax.custom_vjp
def _running_sum_kept(logw, kept):
    return kept


def _running_sum_kept_bwd(shape, g):
    return _bmm_tn(_ones_tri(shape[0], shape[1]), g), jnp.zeros_like(g)


_running_sum_kept.defvjp(lambda logw, kept: (kept, logw.shape), _running_sum_kept_bwd)


@jax.custom_vjp
def _tri_products_kept(ar, bk, kept):
    return kept


def _tri_products_kept_bwd(res, g):
    ar, bk = res
    g = jnp.where(_tri_mask(ar.shape[1] // 2), g, 0.0)
    return _bmm(g, bk), _bmm_tn(g, ar), jnp.zeros_like(g)


_tri_products_kept.defvjp(lambda ar, bk, kept: (kept, (ar, bk)), _tri_products_kept_bwd)


@jax.custom_vjp
def _solve_unit_lower(a, rhs, inv, kept=None):
    return _bmm(inv, rhs) if kept is None else kept


def _solve_fwd(a, rhs, inv, kept=None):
    u = _bmm(inv, rhs) if kept is None else kept
    return u, (inv, u, kept is not None)


def _solve_bwd(res, du):
    inv, u, had_kept = res
    d_rhs = _bmm_tn(inv, du)
    return _bmm_nt(d_rhs, u), d_rhs, jnp.zeros_like(inv), (jnp.zeros_like(u) if had_kept else None)


_solve_unit_lower.defvjp(_solve_fwd, _solve_bwd)


def _rwkv_chunk(st0, r, logw, k, v, a, b, kept=None):
    n_h, t, _ = r.shape
    log_p = _bmm(_ones_tri(n_h, t), logw) if kept is None else _running_sum_kept(logw, kept[0])
    p_in = jnp.exp(log_p)
    p_inv = jnp.exp(-log_p)
    at = a * jnp.exp(log_p - logw)
    rt = r * p_in
    ar = jnp.concatenate([at, rt], axis=1)
    bk = jnp.concatenate([b * p_inv, k * p_inv], axis=1)
    if kept is None:
        m = jnp.where(_tri_mask(t), _bmm_nt(ar, bk), 0.0)
        inv = _unit_lower_inverse(m[:, :t, :t])
    else:
        m = _tri_products_kept(ar, bk, kept[1])
        inv = kept[2]
    top, bottom = m[:, :t], m[:, t:]
    rhs = _bmm_bf(jnp.concatenate([at, top[:, :, t:]], axis=2), jnp.concatenate([st0, v], axis=1))
    u = _solve_unit_lower(top[:, :, :t], rhs, inv, None if kept is None else kept[3])
    y = _bmm_bf(jnp.concatenate([rt, bottom], axis=2), jnp.concatenate([st0, u, v], axis=1))
    p_end = jnp.swapaxes(p_in[:, t - 1:t, :], 1, 2)
    st1 = (st0 + _bmm_tn_bf(bk, jnp.concatenate([u, v], axis=1))) * p_end
    return y, st1, (log_p, m, inv, u)


def _rwkv_scan_fwd(ops):
    n_h, L, n = N_HEADS, ops[0].shape[1], HEAD
    t = RWKV_CHUNK
    per = min(RWKV_CHUNKS_PER_STEP, L // t)
    n_c = L // t
    n_s = n_c // per

    def body(r_ref, w_ref, k_ref, v_ref, a_ref, b_ref, y_ref, st_ref, logp_ref, m_ref, inv_ref, u_ref, st):
        @pl.when(pl.program_id(0) == 0)
        def _():
            st[...] = jnp.zeros_like(st)

        st0 = st[...]
        for j in range(per):
            rows = slice(j * t, (j + 1) * t)
            st_ref[j] = st0
            y, st0, (log_p, m, inv, u) = _rwkv_chunk(
                st0, *(_split_pairs(ref[:, rows, :]) for ref in (r_ref, w_ref, k_ref, v_ref, a_ref, b_ref)))
            y_ref[:, rows, :] = _join_pairs(y)
            logp_ref[:, rows, :] = log_p
            u_ref[:, rows, :] = u
            m_ref[j] = m
            inv_ref[j] = inv
        st[...] = st0

    pairs = pl.BlockSpec((N_PAIRS, per * t, LANES), lambda c: (0, c, 0))
    blk = pl.BlockSpec((n_h, per * t, n), lambda c: (0, c, 0))
    per_chunk = lambda m: pl.BlockSpec((per, n_h, m, m), lambda c: (c, 0, 0, 0))
    return pl.pallas_call(
        body, name="rwkv_scan_fwd", grid=(n_s,), in_specs=[pairs] * 6,
        out_specs=[pairs, per_chunk(n), blk, per_chunk(2 * t), per_chunk(t), blk],
        out_shape=[_sds((N_PAIRS, L, LANES)), _sds((n_c, n_h, n, n)), _sds((n_h, L, n)),
                   _sds((n_c, n_h, 2 * t, 2 * t)), _sds((n_c, n_h, t, t)), _sds((n_h, L, n))],
        scratch_shapes=[pltpu.VMEM((n_h, n, n), F32)],
        compiler_params=_params("arbitrary"),
    )(*ops)


def _rwkv_scan_bwd(ops, states, kept, dy):
    n_h, L, n = N_HEADS, ops[0].shape[1], HEAD
    t = RWKV_CHUNK
    per = min(RWKV_CHUNKS_PER_STEP, L // t)
    n_s = L // t // per

    def body(r_ref, w_ref, k_ref, v_ref, a_ref, b_ref, st_ref, logp_ref, m_ref, inv_ref, u_ref, dy_ref,
             dr, dw, dk, dv, da, db, dst):
        @pl.when(pl.program_id(0) == 0)
        def _():
            dst[...] = jnp.zeros_like(dst)

        vjps = []
        for j in range(per):
            rows = slice(j * t, (j + 1) * t)
            have = (logp_ref[:, rows, :], m_ref[j], inv_ref[j], u_ref[:, rows, :])
            args = [_split_pairs(ref[:, rows, :]) for ref in (r_ref, w_ref, k_ref, v_ref, a_ref, b_ref)]
            vjps.append(jax.vjp(lambda *a, have=have: _rwkv_chunk(*a, kept=have)[:2], st_ref[j], *args)[1])
        d_state = dst[...]
        for j in reversed(range(per)):
            rows = slice(j * t, (j + 1) * t)
            g = vjps[j]((_split_pairs(dy_ref[:, rows, :]), d_state))
            d_state = g[0]
            for out, val in zip((dr, dw, dk, dv, da, db), g[1:]):
                out[:, rows, :] = _join_pairs(val)
        dst[...] = d_state

    pairs = pl.BlockSpec((N_PAIRS, per * t, LANES), lambda c: (0, n_s - 1 - c, 0))
    blk = pl.BlockSpec((n_h, per * t, n), lambda c: (0, n_s - 1 - c, 0))
    per_chunk = lambda m: pl.BlockSpec((per, n_h, m, m), lambda c: (n_s - 1 - c, 0, 0, 0))
    return pl.pallas_call(
        body, name="rwkv_scan_bwd", grid=(n_s,),
        in_specs=[pairs] * 6 + [per_chunk(n), blk, per_chunk(2 * t), per_chunk(t), blk, pairs],
        out_specs=[pairs] * 6, out_shape=[_sds((N_PAIRS, L, LANES))] * 6,
        scratch_shapes=[pltpu.VMEM((n_h, n, n), F32)],
        compiler_params=_params("arbitrary"),
    )(*ops, states, *kept, dy)


def _post_fn(x, u, zs, zr, ysc, r, k2, v, y_ssm, gate_in, d, glu_b, ln_w, ln_b, r_k, gf,
              glu_w, wo_s5, wo_rwkv, tgt, ee):
    y3 = _gelu(y_ssm + d * u)
    y_s5 = y3 * _sigmoid(mm_w(y3, glu_w) + glu_b + gate_in) * _silu(zs)
    mean = head_sum_split(ysc, ee) * (1.0 / HEAD)
    yc = ysc - mean
    var = head_sum(yc * yc, ee) * (1.0 / HEAD)
    gn = yc * lax.rsqrt(var + GN_EPS) * ln_w + ln_b
    bonus = head_sum(r * k2 * r_k, ee) * v
    y_rwkv = (gn + bonus) * _silu(zr)
    x2 = x + mm_w(y_s5, wo_s5) + mm_w(y_rwkv, wo_rwkv)
    err = _rms(x2, gf) - tgt
    return 0.5 * jnp.mean(err * err, axis=-1, keepdims=True), (y3, y_s5, y_rwkv)


def _post(x, u, zs, zr, ysc, r, k2, v, y_ssm, d, glu_w, glu_b, ln_w, ln_b, r_k, w_out, gf, tgt, ee, tt):
    L = x.shape[0]
    n_t = L // tt
    acc_shapes = [(1, D_S5), (D_S5, D_S5), (1, D_S5), (1, D_RWKV), (1, D_RWKV), (1, D_RWKV),
                  (D_MODEL, D_MODEL), (1, D_MODEL), (8, LANES)]

    def body(x_ref, u_ref, zs_ref, zr_ref, ysc_ref, r_ref, k2_ref, v_ref, yssm_ref,
             d_ref, gw_ref, gb_ref, lw_ref, lb_ref, rk_ref, wo_ref, gf_ref, tgt_ref, ee_ref,
             dx_o, du_o, dzs_o, dzr_o, dysc_o, dr_o, dk2_o, dv_o, dyssm_o,
             dd_o, dgw_o, dgb_o, dlw_o, dlb_o, drk_o, dwo_o, dgf_o, loss_o,
             dd, dgw, dgb, dlw, dlb, drk, dwo, dgf, loss):
        i = pl.program_id(0)
        accs = (dd, dgw, dgb, dlw, dlb, drk, dwo, dgf, loss)

        @pl.when(i == 0)
        def _():
            for acc in accs:
                acc[...] = jnp.zeros_like(acc)

        args = (x_ref[...], u_ref[...], zs_ref[...], zr_ref[...],
                _load_heads(ysc_ref), _load_heads(r_ref), _load_heads(k2_ref), _load_heads(v_ref), yssm_ref[...],
                jnp.zeros((tt, D_S5), F32), d_ref[...], gb_ref[...], lw_ref[...], lb_ref[...], rk_ref[...], gf_ref[...])
        rows, vjp, (y3, y_s5, y_rwkv) = jax.vjp(
            lambda *a: _post_fn(*a, gw_ref[...], wo_ref[0:D_S5, :], wo_ref[D_S5:D_MODEL, :], tgt_ref[...],
                                ee_ref[...]), *args, has_aux=True)
        g = vjp(jnp.ones_like(rows))
        for out, val in zip((dx_o, du_o, dzs_o, dzr_o), g[0:4]):
            out[...] = val
        _store_heads(dysc_o, g[4])
        for out, val in zip((dr_o, dk2_o, dv_o, dyssm_o), g[5:9]):
            out[...] = val
        for acc, val in zip((dd, dgb, dlw, dlb, drk, dgf), g[10:16]):
            acc[...] += val
        dgw[...] += _dot_bf(y3, g[9], ((0,), (0,)))
        dwo[0:D_S5, :] += _dot_bf(y_s5, g[0], ((0,), (0,)))
        dwo[D_S5:D_MODEL, :] += _dot_bf(y_rwkv, g[0], ((0,), (0,)))
        loss[...] += jnp.broadcast_to(jnp.sum(rows, axis=0, keepdims=True), loss.shape)

        @pl.when(i == n_t - 1)
        def _():
            for acc, out in zip(accs, (dd_o, dgw_o, dgb_o, dlw_o, dlb_o, drk_o, dwo_o, dgf_o, loss_o)):
                pltpu.sync_copy(acc, out)

    row = lambda n: pl.BlockSpec((tt, n), lambda i: (i, 0))
    in_specs = ([row(D_MODEL)] + [row(512)] * 3 + [_head_spec(tt)] * 4 + [row(D_S5)]
                + [_const_spec(s) for s in [(1, D_S5), (D_S5, D_S5), (1, D_S5), (1, D_RWKV), (1, D_RWKV), (1, D_RWKV),
                                            (D_MODEL, D_MODEL), (1, D_MODEL)]]
                + [row(D_MODEL), _const_spec((D_RWKV, D_RWKV))])
    out_rows = [D_MODEL] + [512] * 3 + [None] + [512] * 4
    return pl.pallas_call(
        body, name="post_fwd_bwd", grid=(n_t,), in_specs=in_specs,
        out_specs=[row(n) if n else _head_spec(tt) for n in out_rows] + [_ANY] * len(acc_shapes),
        out_shape=([_sds((L, n)) if n else _sds((N_PAIRS, L, LANES)) for n in out_rows]
                   + [_sds(s) for s in acc_shapes]),
        scratch_shapes=[pltpu.VMEM(s, F32) for s in acc_shapes],
        compiler_params=_params("arbitrary"),
    )(x, u, zs, zr, ysc, r, k2, v, y_ssm, d, glu_w, glu_b, ln_w, ln_b, r_k, w_out, gf, tgt, ee)


def _s5_bwd(u, du_direct, dy, s_re, s_im, b4_re, b4_im, c4_re, c4_im, pow_r, pow_i, tt, early=()):
    L = u.shape[0]
    n_t = L // tt
    n_e = len(early)
    everyone = ("x", "y", "c")
    acc_shapes = ([(S5_BLOCKS, LANES, 512)] * 2 + [(S5_BLOCKS, 512, LANES)] * 2 + [(1, N_STATE)] * 2)

    def body(u_ref, dud_ref, dy_ref, sre_ref, sim_ref, pre_ref, pim_ref, bre_ref, bim_ref, cre_ref, cim_ref,
             pr_ref, pi_ref, *refs):
        du_o, dbre_o, dbim_o, dcre_o, dcim_o, dlr_o, dli_o = refs[n_e:n_e + 7]
        dbre, dbim, dcre, dcim, dlr, dli, gre, gim, car_r, car_i = refs[2 * n_e + 7:2 * n_e + 17]
        copies = lambda: _exchange_copies(refs[:n_e], refs[n_e + 7:2 * n_e + 7], *refs[2 * n_e + 17:],
                                          (False,) * n_e, everyone)
        i = pl.program_id(0)

        @pl.when(i == 0)
        def _():
            for acc in (dbre, dbim, dcre, dcim, dlr, dli, car_r, car_i):
                acc[...] = jnp.zeros_like(acc)
            if n_e:
                _start_copies(copies())

        uv = u_ref[...]
        dyv = dy_ref[...]
        blocks = [slice(q * 512, (q + 1) * 512) for q in range(S5_BLOCKS)]
        lanes = [slice(q * LANES, (q + 1) * LANES) for q in range(S5_BLOCKS)]
        for q in range(S5_BLOCKS):
            gre[:, blocks[q]] = _dot_bf(dyv[:, lanes[q]], cre_ref[q], ((1,), (1,)))
            gim[:, blocks[q]] = -_dot_bf(dyv[:, lanes[q]], cim_ref[q], ((1,), (1,)))
        _tile_scan(gre, gim, pr_ref, pi_ref, car_r, car_i, reverse=True)
        for q in range(S5_BLOCKS):
            gr = gre[:, blocks[q]]
            gi = gim[:, blocks[q]]
            sr = sre_ref[:, blocks[q]]
            si = sim_ref[:, blocks[q]]
            du_o[:, lanes[q]] = (dud_ref[:, lanes[q]] + _dot_bf(gr, bre_ref[q], ((1,), (1,)))
                                 + _dot_bf(gi, bim_ref[q], ((1,), (1,))))
            dbre[q] += _dot_bf(uv[:, lanes[q]], gr, ((0,), (0,)))
            dbim[q] += _dot_bf(uv[:, lanes[q]], gi, ((0,), (0,)))
            dcre[q] += _dot_bf(sr, dyv[:, lanes[q]], ((0,), (0,)))
            dcim[q] -= _dot_bf(si, dyv[:, lanes[q]], ((0,), (0,)))
            rid = lax.broadcasted_iota(jnp.int32, sr.shape, 0)
            first = i == n_t - 1
            prev_r = jnp.where(first, 0.0, pre_ref[7:8, blocks[q]])
            prev_i = jnp.where(first, 0.0, pim_ref[7:8, blocks[q]])
            pr = jnp.where(rid == 0, jnp.broadcast_to(prev_r, sr.shape), pltpu.roll(sr, 1, axis=0))
            pi_ = jnp.where(rid == 0, jnp.broadcast_to(prev_i, si.shape), pltpu.roll(si, 1, axis=0))
            dlr[:, blocks[q]] += jnp.sum(pr * gr + pi_ * gi, axis=0, keepdims=True)
            dli[:, blocks[q]] += jnp.sum(pr * gi - pi_ * gr, axis=0, keepdims=True)

        @pl.when(i == n_t - 1)
        def _():
            for acc, out in zip((dbre, dbim, dcre, dcim, dlr, dli), (dbre_o, dbim_o, dcre_o, dcim_o, dlr_o, dli_o)):
                out[...] = acc[...]
            if n_e:
                _wait_copies(copies())

    row = lambda n: pl.BlockSpec((tt, n), lambda i: (n_t - 1 - i, 0))
    prev = pl.BlockSpec((8, N_STATE), lambda i: (jnp.maximum((n_t - 1 - i) * (tt // 8) - 1, 0), 0))
    return pl.pallas_call(
        body, name="s5_bwd", grid=(n_t,),
        in_specs=[row(D_S5)] * 3 + [row(N_STATE)] * 2 + [prev] * 2
        + [_const_spec((S5_BLOCKS, LANES, 512))] * 2 + [_const_spec((S5_BLOCKS, 512, LANES))] * 2
        + [_const_spec((8, N_STATE))] * 2 + [_ANY] * n_e,
        out_specs=[row(D_S5)] + [_acc_spec(s) for s in acc_shapes] + [_ANY] * n_e,
        out_shape=[_sds((L, D_S5))] + [_sds(s) for s in acc_shapes] + _exchange_results(early, (False,) * n_e, everyone),
        scratch_shapes=[pltpu.VMEM(s, F32) for s in acc_shapes] + [pltpu.VMEM((tt, N_STATE), F32)] * 2
        + [pltpu.VMEM((8, N_STATE), F32)] * 2 + (_exchange_scratch(n_e, everyone) if n_e else []),
        compiler_params=pltpu.CompilerParams(dimension_semantics=("arbitrary",), vmem_limit_bytes=VMEM_LIMIT,
                                             has_side_effects=bool(n_e)),
    )(u, du_direct, dy, s_re, s_im, s_re, s_im, b4_re, b4_im, c4_re, c4_im, pow_r, pow_i, *early)


def _bwd_in(x, norm_g, w_in_bf, mu, dx2, du, dzs, drws, dzr, tt):
    L = x.shape[0]
    n_t = L // tt

    def body(x_ref, g_ref, w_ref, mu_ref, dx2_ref, du_ref, dzs_ref, drws_ref, nxt_ref, dzr_ref,
             gx_o, dw_o, dg_o, dproj, dw, dg, stage):
        i = pl.program_id(0)

        @pl.when(i == 0)
        def _():
            dw[...] = jnp.zeros_like(dw)
            dg[...] = jnp.zeros_like(dg)

        drws_v = drws_ref[...]
        rid = lax.broadcasted_iota(jnp.int32, drws_v.shape, 0)
        nxt_row = jnp.where(i == n_t - 1, 0.0, nxt_ref[0:1, :])
        nxt = jnp.where(rid == tt - 1, jnp.broadcast_to(nxt_row, drws_v.shape), pltpu.roll(drws_v, tt - 1, axis=0))
        muv = mu_ref[...]
        drw = drws_v * (1.0 - muv) + nxt * muv
        dproj[:, 0:D_S5] = du_ref[...].astype(BF16)
        dproj[:, D_S5:2 * D_S5] = dzs_ref[...].astype(BF16)
        dproj[:, 2 * D_S5:2 * D_S5 + D_SHIFT] = drw.astype(BF16)
        dproj[:, 2 * D_S5 + D_SHIFT:D_IN] = dzr_ref[...].astype(BF16)
        dh = _dot(dproj[...], w_ref[...], ((1,), (0,)), None)
        h, vjp = jax.vjp(_rms, x_ref[...], g_ref[...])
        dxh, dgv = vjp(dh)
        gx_o[...] = dx2_ref[...] + dxh
        dg[...] += dgv
        dw[...] += _dot(dproj[...], h.astype(BF16), ((0,), (0,)), None)

        @pl.when(i == n_t - 1)
        def _():
            dg_o[...] = dg[...]
            for j in range(D_IN // stage.shape[0]):
                rows = pl.ds(j * stage.shape[0], stage.shape[0])
                stage[...] = dw[rows, :].astype(BF16)
                pltpu.sync_copy(stage, dw_o.at[rows])

    row = lambda n: pl.BlockSpec((tt, n), lambda i: (i, 0))
    nxt = pl.BlockSpec((8, D_SHIFT), lambda i: (jnp.minimum((i + 1) * (tt // 8), L // 8 - 1), 0))
    return pl.pallas_call(
        body, name="bwd_in", grid=(n_t,),
        in_specs=[row(D_MODEL), _const_spec((1, D_MODEL)), _const_spec((D_IN, D_MODEL)), _const_spec((1, D_SHIFT)),
                  row(D_MODEL), row(D_S5), row(D_S5), row(D_SHIFT), nxt, row(D_RWKV)],
        out_specs=[row(D_MODEL), _ANY, _acc_spec((1, D_MODEL))],
        out_shape=[_sds((L, D_MODEL)), jax.ShapeDtypeStruct((D_IN, D_MODEL), BF16), _sds((1, D_MODEL))],
        scratch_shapes=[pltpu.VMEM((tt, D_IN), BF16), pltpu.VMEM((D_IN, D_MODEL), F32), pltpu.VMEM((1, D_MODEL), F32),
                        pltpu.VMEM((D_IN // N_DEV, D_MODEL), BF16)],
        compiler_params=_params("arbitrary"),
    )(x, norm_g, w_in_bf, mu, dx2, du, dzs, drws, drws, dzr)


def _block_diag_b(bbar):
    bb = bbar.reshape(S5_GROUP, S5_BLOCKS, 8, S5_STATE)
    return jnp.einsum('hqgp,Gg->qGhgp', bb, jnp.eye(8, dtype=F32)).reshape(S5_BLOCKS, LANES, 512)


def _block_diag_b_t(db4):
    d = db4.reshape(S5_BLOCKS, 8, S5_GROUP, 8, S5_STATE)
    return jnp.einsum('qGhgp,Gg->hqgp', d, jnp.eye(8, dtype=F32)).reshape(S5_GROUP, N_STATE)


def _block_diag_c(c):
    cc = c.reshape(S5_BLOCKS, 8, S5_GROUP, S5_STATE)
    return jnp.einsum('qghp,gG->qgpGh', cc, jnp.eye(8, dtype=F32)).reshape(S5_BLOCKS, 512, LANES)


def _block_diag_c_t(dc4):
    d = dc4.reshape(S5_BLOCKS, 8, S5_STATE, 8, S5_GROUP)
    return jnp.einsum('qgpGh,gG->qghp', d, jnp.eye(8, dtype=F32)).reshape(S5_GROUPS, S5_GROUP, S5_STATE)


def _local_step(x, tgt, w, late=()):
    L = x.shape[0]
    tt = min(512, L)
    tp = min(256, L)
    ee = _head_sum_matrix()

    lam_re = w['s5_lam_re'].reshape(1, N_STATE)
    lam_im = w['s5_lam_im'].reshape(1, N_STATE)
    logdt = jnp.repeat(w['s5_log_dt'], S5_STATE).reshape(1, N_STATE)
    b_re_t = w['s5_b_re'].transpose(2, 0, 1).reshape(S5_GROUP, N_STATE)
    b_im_t = w['s5_b_im'].transpose(2, 0, 1).reshape(S5_GROUP, N_STATE)
    bbr, bbi, pow_r, pow_i, rpow_r, rpow_i = _s5_param_fwd(lam_re, lam_im, logdt, b_re_t, b_im_t)
    b4_re, b4_im = _block_diag_b(bbr), _block_diag_b(bbi)
    c4_re, c4_im = _block_diag_c(w['s5_c_re']), _block_diag_c(w['s5_c_im'])

    norm_g = w['norm_g'].reshape(1, D_MODEL)
    w_in_bf = (w['w_in_t'] if 'w_in_t' in w else w['w_in'].T).astype(BF16)
    u, zs, rw, zr, *gathered = _fwd_in(x, norm_g, w_in_bf, tt, [shard for _, shard in late])
    w = dict(w, **{n: _join_shards(n, blocks) for (n, _), blocks in zip(late, gathered)})
    s_re, s_im, y_ssm = _s5_fwd(u, b4_re, b4_im, c4_re, c4_im, pow_r, pow_i, tt)

    row = lambda t: t.reshape(1, -1)
    zpad = jnp.zeros((HEAD, D_RWKV), F32)
    w2p = jnp.concatenate([w['rwkv_w2'], zpad], axis=0)
    a2p = jnp.concatenate([zpad, w['rwkv_a2']], axis=0)
    pre_consts = (row(w['rwkv_mu']), row(w['rwkv_w0']), w2p, row(w['rwkv_a0']), a2p,
                  row(w['rwkv_k_k']), row(w['rwkv_k_a']), ee)
    ops = _rwkv_pre_fwd(rw, *pre_consts, tt)
    ysc, states, *kept = _rwkv_scan_fwd(ops)

    post = _post(x, u, zs, zr, ysc, ops[0], ops[2], ops[3], y_ssm,
                 row(w['s5_d']), w['s5_glu_w'].astype(BF16), row(w['s5_glu_b']), row(w['rwkv_ln_w']), row(w['rwkv_ln_b']),
                 row(w['rwkv_r_k']), w['w_out'].astype(BF16), row(w['final_g']), tgt, ee, tp)
    (dx2, du_d, dzs, dzr, dysc, dr_b, dk2_b, dv_b, dy_ssm,
     dd, dglu_w, dglu_b, dln_w, dln_b, dr_k, dw_out, dgf, loss) = post

    ready = {'s5_glu_w': dglu_w, 'w_out': dw_out}
    early = [_split_shards(n, ready[n], shard.shape).astype(BF16) for n, shard in late]
    du, db4_re, db4_im, dc4_re, dc4_im, dlbr, dlbi, *arrived = _s5_bwd(
        u, du_d, dy_ssm, s_re, s_im, b4_re, b4_im, c4_re, c4_im, rpow_r, rpow_i, tt, early)
    group_ind = (jnp.arange(N_STATE)[:, None] // S5_STATE == jnp.arange(LANES)[None, :]).astype(F32)
    dlam_re, dlam_im, dlogdt, db_re_t, db_im_t = _s5_param_bwd(
        lam_re, lam_im, logdt, b_re_t, b_im_t, dlbr, dlbi, _block_diag_b_t(db4_re), _block_diag_b_t(db4_im), group_ind)

    cots = list(_rwkv_scan_bwd(ops, states, kept, dysc)) + [dr_b, dk2_b, dv_b]
    drws, dmu, dw0, dw2p, da0, da2p, dk_k, dk_a = _rwkv_pre_bwd(rw, *pre_consts, cots, tt)

    grad_x, dw_in, dnorm_g = _bwd_in(x, norm_g, w_in_bf, row(w['rwkv_mu']), dx2, du, dzs, drws, dzr, tt)

    unb = lambda t: t.reshape(S5_GROUP, S5_GROUPS, S5_STATE).transpose(1, 2, 0)
    grads = {
        'norm_g': dnorm_g.reshape(D_MODEL), 'w_in_t': dw_in,
        's5_lam_re': dlam_re.reshape(S5_GROUPS, S5_STATE), 's5_lam_im': dlam_im.reshape(S5_GROUPS, S5_STATE),
        's5_log_dt': dlogdt[0, :S5_GROUPS], 's5_b_re': unb(db_re_t), 's5_b_im': unb(db_im_t),
        's5_c_re': _block_diag_c_t(dc4_re), 's5_c_im': _block_diag_c_t(dc4_im),
        's5_d': dd.reshape(D_S5), 's5_glu_w': dglu_w, 's5_glu_b': dglu_b.reshape(D_S5),
        'rwkv_mu': dmu.reshape(-1), 'rwkv_w0': dw0.reshape(-1), 'rwkv_w2': dw2p[:HEAD], 'rwkv_a0': da0.reshape(-1),
        'rwkv_a2': da2p[HEAD:], 'rwkv_k_k': dk_k.reshape(-1), 'rwkv_k_a': dk_a.reshape(-1),
        'rwkv_r_k': dr_k.reshape(N_HEADS, HEAD), 'rwkv_ln_w': dln_w.reshape(-1), 'rwkv_ln_b': dln_b.reshape(-1),
        'w_out': dw_out, 'final_g': dgf.reshape(D_MODEL),
    }
    grads.update({n: blocks for (n, _), blocks in zip(late, arrived)})
    return loss, grad_x, grads


def _exchange(arrays, gather, axes, name):
    n = len(arrays)

    def body(*refs):
        copies = _exchange_copies(refs[:n], refs[n:2 * n], *refs[2 * n:], gather, axes)
        _start_copies(copies)
        _wait_copies(copies)

    return pl.pallas_call(
        body, name=name, in_specs=[_ANY] * n, out_specs=[_ANY] * n,
        out_shape=_exchange_results(arrays, gather, axes), scratch_shapes=_exchange_scratch(n, axes),
        compiler_params=pltpu.CompilerParams(has_side_effects=True),
    )(*arrays)


def _exchange_results(arrays, gather, axes):
    return [jax.ShapeDtypeStruct(((2 ** len(axes),) + a.shape) if whole else a.shape, a.dtype)
            for a, whole in zip(arrays, gather)]


def _exchange_scratch(n, axes):
    peers = 2 ** len(axes) - 1
    return [pltpu.SemaphoreType.DMA((n, peers)), pltpu.SemaphoreType.DMA((n, peers)), pltpu.SemaphoreType.DMA((n,))]


def _exchange_copies(send_refs, recv_refs, send_sems, recv_sems, local_sems, gather, axes):
    pos = {ax: lax.axis_index(ax) for ax in ("x", "y", "c")}

    def index_of(p):
        idx = 0
        for ax in axes:
            idx = 2 * idx + p[ax]
        return idx

    me = index_of(pos)
    own, outs, arrivals = [], [], []
    for i, (send_ref, recv_ref) in enumerate(zip(send_refs, recv_refs)):
        def block_for(dev, send_ref=send_ref, whole=gather[i]):
            return send_ref if whole else send_ref.at[dev]

        own.append(pltpu.make_async_copy(block_for(me), recv_ref.at[me], local_sems.at[i]))
        for k in range(1, 2 ** len(axes)):
            peer = dict(pos)
            for bit, ax in enumerate(axes):
                if (k >> bit) & 1:
                    peer[ax] = 1 - pos[ax]
            peer_idx = index_of(peer)
            sems = dict(send_sem=send_sems.at[i, k - 1], recv_sem=recv_sems.at[i, k - 1],
                        device_id=(peer["x"], peer["y"], peer["c"]), device_id_type=pl.DeviceIdType.MESH)
            outs.append(pltpu.make_async_remote_copy(src_ref=block_for(peer_idx), dst_ref=recv_ref.at[me], **sems))
            arrivals.append(
                pltpu.make_async_remote_copy(src_ref=block_for(peer_idx), dst_ref=recv_ref.at[peer_idx], **sems))
    return own, outs, arrivals


def _start_copies(copies):
    own, outs, _ = copies
    for copy in own + outs:
        copy.start()


def _wait_copies(copies):
    own, outs, arrivals = copies
    for copy in arrivals:
        copy.wait_recv()
    for copy in outs:
        copy.wait_send()
    for copy in own:
        copy.wait()


def _sum_devices(ref):
    g = ref[0].astype(F32)
    for s in range(1, ref.shape[0]):
        g = g + ref[s].astype(F32)
    return g


def _adamw_math(g, w, m, v):
    m_new = ADAM_B1 * m + (1.0 - ADAM_B1) * g
    v_new = ADAM_B2 * v + (1.0 - ADAM_B2) * (g * g)
    m_hat = m_new / (1.0 - ADAM_B1 ** ADAM_STEP)
    v_hat = v_new / (1.0 - ADAM_B2 ** ADAM_STEP)
    return -ADAM_LR * (m_hat / (jnp.sqrt(v_hat) + ADAM_EPS) + ADAM_WD * w), m_new, v_new


def _adamw(gs, ws, ms, vs, reduce, name):
    n = len(ws)

    def body(*refs):
        g_refs, w_refs, m_refs, v_refs = (refs[j * n:(j + 1) * n] for j in range(4))
        outs = refs[4 * n:]
        for i in range(n):
            g = _sum_devices(g_refs[i]) if reduce else g_refs[i][...]
            res = _adamw_math(g, w_refs[i][...], m_refs[i][...], v_refs[i][...])
            for j, val in enumerate(((g,) if reduce else ()) + res):
                outs[j * n + i][...] = val

    return pl.pallas_call(
        body, name=name, out_shape=[_sds(w.shape) for w in ws] * (4 if reduce else 3),
        compiler_params=pltpu.CompilerParams(vmem_limit_bytes=VMEM_LIMIT),
    )(*gs, *ws, *ms, *vs)


def _sum_blocks(recv):
    def body(recv_ref, out_ref):
        out_ref[...] = _sum_devices(recv_ref)

    return pl.pallas_call(body, name="sum_small_grads", out_shape=_sds(recv.shape[1:]))(recv)


_WEIGHTS = [
    ('norm_g', (1, 1024), False), ('w_in', (1, 1024, 400), True), ('s5_lam_re', (1, 32, 64), False),
    ('s5_lam_im', (1, 32, 64), False), ('s5_log_dt', (1, 32), False), ('s5_b_re', (1, 32, 64, 16), False),
    ('s5_b_im', (1, 32, 64, 16), False), ('s5_c_re', (1, 32, 16, 64), False), ('s5_c_im', (1, 32, 16, 64), False),
    ('s5_d', (1, 512), False), ('s5_glu_w', (1, 64, 512), True), ('s5_glu_b', (1, 512), False),
    ('rwkv_mu', (1, 1664), False), ('rwkv_w0', (1, 512), False), ('rwkv_w2', (1, 64, 64), True),
    ('rwkv_a0', (1, 512), False), ('rwkv_a2', (1, 64, 64), True), ('rwkv_k_k', (1, 512), False),
    ('rwkv_k_a', (1, 512), False), ('rwkv_r_k', (1, 8, 64), False), ('rwkv_ln_w', (1, 512), False),
    ('rwkv_ln_b', (1, 512), False), ('w_out', (1, 128, 1024), True), ('final_g', (1024,), False),
]
_SHARDED = [(n, s) for n, s, sharded in _WEIGHTS if sharded]
_SMALL = [(n, s) for n, s, sharded in _WEIGHTS if not sharded]
_COLUMN_SHARDED = ('w_in', 'rwkv_w2', 'rwkv_a2')
_SMALL_SIZE = sum(math.prod(s) for _, s in _SMALL) + 1
_SMALL_ROWS = -(-_SMALL_SIZE // (8 * LANES)) * 8


_MINOR_SWAPPED = ('s5_b_re', 's5_b_im')


def _stored(name, t):
    return jnp.swapaxes(t, -1, -2) if name in _MINOR_SWAPPED else t


def _pack_small(grads, loss):
    flat = [_stored(n, grads[n]).reshape(-1) for n, _ in _SMALL] + [loss.reshape(1)]
    pad = _SMALL_ROWS * LANES - _SMALL_SIZE
    return jnp.concatenate(flat + [jnp.zeros((pad,), F32)]).reshape(_SMALL_ROWS, LANES)


def _unpack_small(packed):
    flat = packed.reshape(-1)
    out, off = {}, 0
    for n, s in _SMALL:
        size = math.prod(s)
        out[n] = flat[off:off + size].reshape(s[:-2] + (s[-1], s[-2]) if n in _MINOR_SWAPPED else s)
        off += size
    return out, flat[off]


_BF16_OPERANDS = ('w_in', 's5_glu_w', 'w_out')
_LATE_WEIGHTS = ('s5_glu_w', 'w_out')


def _join_shards(name, blocks):
    _, rows, cols = blocks.shape
    if name in _COLUMN_SHARDED:
        return blocks.transpose(1, 0, 2).reshape(rows, N_DEV * cols)
    return blocks.reshape(N_DEV * rows, cols)


def _split_shards(name, full, shard_shape):
    rows, cols = shard_shape
    if name in _COLUMN_SHARDED:
        return full.reshape(rows, N_DEV, cols).transpose(1, 0, 2)
    return full.reshape(N_DEV, rows, cols)


def kernel(x, norm_g, w_in, s5_lam_re, s5_lam_im, s5_log_dt, s5_b_re, s5_b_im, s5_c_re, s5_c_im, s5_d, s5_glu_w, s5_glu_b, rwkv_mu, rwkv_w0, rwkv_w2, rwkv_a0, rwkv_a2, rwkv_k_k, rwkv_k_a, rwkv_r_k, rwkv_ln_w, rwkv_ln_b, w_out, final_g, loss_target, m_norm_g, m_w_in, m_s5_lam_re, m_s5_lam_im, m_s5_log_dt, m_s5_b_re, m_s5_b_im, m_s5_c_re, m_s5_c_im, m_s5_d, m_s5_glu_w, m_s5_glu_b, m_rwkv_mu, m_rwkv_w0, m_rwkv_w2, m_rwkv_a0, m_rwkv_a2, m_rwkv_k_k, m_rwkv_k_a, m_rwkv_r_k, m_rwkv_ln_w, m_rwkv_ln_b, m_w_out, m_final_g, v_norm_g, v_w_in, v_s5_lam_re, v_s5_lam_im, v_s5_log_dt, v_s5_b_re, v_s5_b_im, v_s5_c_re, v_s5_c_im, v_s5_d, v_s5_glu_w, v_s5_glu_b, v_rwkv_mu, v_rwkv_w0, v_rwkv_w2, v_rwkv_a0, v_rwkv_a2, v_rwkv_k_k, v_rwkv_k_a, v_rwkv_r_k, v_rwkv_ln_w, v_rwkv_ln_b, v_w_out, v_final_g):
    given = dict(locals())

    n_sh = len(_SHARDED)
    everyone = ("x", "y", "c")
    shards = [given[n][0].astype(BF16 if n in _BF16_OPERANDS else F32) for n, _ in _SHARDED]
    shards[0] = shards[0].T
    names = [n for n, _ in _SHARDED]
    first = [i for i, n in enumerate(names) if n not in _LATE_WEIGHTS]
    gathered = _exchange([shards[i] for i in first], (True,) * len(first), everyone, "gather_weights")
    local = {names[i]: _join_shards(names[i], blocks) for i, blocks in list(zip(first, gathered))[1:]}
    local['w_in_t'] = gathered[0].reshape(D_IN, D_MODEL)
    local.update({n: (given[n][0] if len(s) > 1 else given[n]) for n, s in _SMALL})
    late = [(n, shards[names.index(n)]) for n in _LATE_WEIGHTS]

    loss, grad_x, grads = _local_step(x[0], loss_target[0], local, late)

    blocks = [grads['w_in_t'].reshape(N_DEV, D_IN // N_DEV, D_MODEL)]
    blocks += [_split_shards(names[i], grads[names[i]], _SHARDED[i][1][1:]).astype(BF16) for i in first[1:]]
    small = _pack_small(grads, loss[0, 0]).reshape(N_DEV, _SMALL_ROWS // N_DEV, LANES)
    recv = _exchange(blocks + [small], (False,) * (len(first) + 1), everyone, "exchange_grads")
    small_sum = _exchange([_sum_blocks(recv[-1])], (True,), everyone, "gather_small_grads")[0]
    arrived = dict(zip([names[i] for i in first], recv), **{n: grads[n] for n in _LATE_WEIGHTS})

    result = {}
    for group, name in (([0], "adamw_w_in"), ([1, 2, 3, 4], "adamw_shards")):
        ns = [_SHARDED[i][0] for i in group]
        own = (lambda t: t[0].T) if group == [0] else (lambda t: t[0])
        back = (lambda t: t.T[None]) if group == [0] else (lambda t: t[None])
        res = _adamw([arrived[n] for n in ns], [own(given[n]) for n in ns], [own(given['m_' + n]) for n in ns],
                     [own(given['v_' + n]) for n in ns], True, name)
        for j, n in enumerate(ns):
            result[n] = [back(res[k * len(ns) + j]) for k in range(4)]
    g_small, total = _unpack_small(small_sum)
    two_d = lambda t: t.reshape(1, -1) if t.ndim == 1 else t
    ns = [n for n, _ in _SMALL]
    res = _adamw([two_d(g_small[n]) for n in ns], [two_d(_stored(n, given[n])) for n in ns],
                 [two_d(_stored(n, given['m_' + n])) for n in ns], [two_d(_stored(n, given['v_' + n])) for n in ns],
                 False, "adamw_small")
    for j, (n, s) in enumerate(_SMALL):
        result[n] = [_stored(n, g_small[n])] + [_stored(n, res[k * len(ns) + j]).reshape(s) for k in range(3)]

    outs = [total, grad_x[None]]
    for k in range(4):
        outs += [result[n][k] for n, _, _ in _WEIGHTS]
    return tuple(outs)
```

```python
import math

import jax
import jax.numpy as jnp
from jax import lax
from jax.experimental import pallas as pl
from jax.experimental.pallas import tpu as pltpu

F32 = jnp.float32
BF16 = jnp.bfloat16
HI = lax.Precision.HIGH

D_MODEL = 1024
D_S5 = 512
D_RWKV = 512
S5_GROUPS = 32
S5_GROUP = 16
S5_STATE = 64
N_STATE = S5_GROUPS * S5_STATE
N_HEADS = 8
HEAD = 64
D_SHIFT = 3 * D_RWKV + 128
D_IN = 2 * D_S5 + D_SHIFT + D_RWKV
NORM_EPS = 1e-6
GN_EPS = 64e-5
N_DEV = 8
LANES = 128
S5_BLOCKS = 4
RWKV_CHUNK = 64
RWKV_CHUNKS_PER_STEP = 4
VMEM_LIMIT = 56 * 1024 * 1024

ADAM_LR = 0.001
ADAM_B1 = 0.9
ADAM_B2 = 0.999
ADAM_EPS = 1e-08
ADAM_WD = 0.01
ADAM_STEP = 10


def _dot(a, b, dims, prec):
    return lax.dot_general(a, b, (dims, ((), ())), precision=prec, preferred_element_type=F32)


def _dot_bf(a, b, dims):
    return _dot(a.astype(BF16), b.astype(BF16), dims, None)


def _make_mm(cast, prec):
    @jax.custom_vjp
    def mm(a, b):
        return _dot(cast(a), cast(b), ((1,), (0,)), prec)

    def fwd(a, b):
        return mm(a, b), (a, b)

    def bwd(res, g):
        a, b = res
        return (_dot(cast(g), cast(b), ((1,), (1,)), prec), _dot(cast(a), cast(g), ((0,), (0,)), prec))

    mm.defvjp(fwd, bwd)
    return mm


mm_bf = _make_mm(lambda t: t.astype(BF16), None)


@jax.custom_vjp
def mm_w(a, w):
    return _dot(a.astype(BF16), w, ((1,), (0,)), None)


def _mm_w_fwd(a, w):
    return mm_w(a, w), w


def _mm_w_bwd(w, g):
    return _dot(g.astype(BF16), w, ((1,), (1,)), None), jnp.zeros_like(w)


mm_w.defvjp(_mm_w_fwd, _mm_w_bwd)


def _make_head_sum(split):
    def product(x, ee):
        hi = x.astype(BF16)
        out = _dot(hi, ee, ((1,), (0,)), None)
        if split:
            out = out + _dot((x - hi.astype(F32)).astype(BF16), ee, ((1,), (0,)), None)
        return out

    @jax.custom_vjp
    def head_sum(x, ee):
        return product(x, ee)

    def fwd(x, ee):
        return product(x, ee), ee

    def bwd(ee, g):
        return product(g, ee), jnp.zeros_like(ee)

    head_sum.defvjp(fwd, bwd)
    return head_sum


head_sum = _make_head_sum(False)
head_sum_split = _make_head_sum(True)


@jax.custom_vjp
def _sigmoid(x):
    return 1.0 / (1.0 + jnp.exp(-x))


def _sigmoid_fwd(x):
    s = _sigmoid(x)
    return s, s


_sigmoid.defvjp(_sigmoid_fwd, lambda s, g: (g * s * (1.0 - s),))


@jax.custom_vjp
def _silu(x):
    return x * _sigmoid(x)


def _silu_fwd(x):
    s = _sigmoid(x)
    return x * s, (x, s)


def _silu_bwd(res, g):
    x, s = res
    return (g * s * (1.0 + x * (1.0 - s)),)


_silu.defvjp(_silu_fwd, _silu_bwd)


@jax.custom_vjp
def _softplus(x):
    return jnp.maximum(x, 0.0) + jnp.log(1.0 + jnp.exp(-jnp.abs(x)))


def _softplus_fwd(x):
    e = jnp.exp(-jnp.abs(x))
    return jnp.maximum(x, 0.0) + jnp.log(1.0 + e), (x, e)


def _softplus_bwd(res, g):
    x, e = res
    return (g * jnp.where(x >= 0.0, 1.0, e) / (1.0 + e),)


_softplus.defvjp(_softplus_fwd, _softplus_bwd)


@jax.custom_vjp
def _normalize_heads(x, ee):
    return x / jnp.maximum(jnp.sqrt(head_sum(x * x, ee)), 1e-12)


def _normalize_heads_fwd(x, ee):
    norm = jnp.sqrt(head_sum(x * x, ee))
    inv = 1.0 / jnp.maximum(norm, 1e-12)
    y = x * inv
    return y, (y, inv, norm, ee)


def _normalize_heads_bwd(res, g):
    y, inv, norm, ee = res
    along = jnp.where(norm > 1e-12, head_sum(g * y, ee), 0.0)
    return inv * (g - y * along), jnp.zeros_like(ee)


_normalize_heads.defvjp(_normalize_heads_fwd, _normalize_heads_bwd)


_GELU_C = 2.0 * math.sqrt(2.0 / math.pi)


def _gelu_gate(x):
    return 1.0 / (1.0 + jnp.exp(-_GELU_C * x * (1.0 + 0.044715 * (x * x))))


@jax.custom_vjp
def _gelu(x):
    return x * _gelu_gate(x)


def _gelu_fwd(x):
    s = _gelu_gate(x)
    return x * s, (x, s)


def _gelu_bwd(res, g):
    x, s = res
    return (g * (s + x * s * (1.0 - s) * (_GELU_C * (1.0 + 3.0 * 0.044715 * (x * x)))),)


_gelu.defvjp(_gelu_fwd, _gelu_bwd)


def _rms(x, g):
    return x * lax.rsqrt(jnp.mean(x * x, axis=-1, keepdims=True) + NORM_EPS) * g


def _const_spec(shape):
    nd = len(shape)
    return pl.BlockSpec(shape, lambda *_: (0,) * nd, pipeline_mode=pl.Buffered(1))


def _acc_spec(shape):
    nd = len(shape)
    return pl.BlockSpec(shape, lambda *_: (0,) * nd)


def _params(sem):
    return pltpu.CompilerParams(dimension_semantics=(sem,), vmem_limit_bytes=VMEM_LIMIT)


_ANY = pl.BlockSpec(memory_space=pl.ANY)


def _sds(shape):
    return jax.ShapeDtypeStruct(shape, F32)


def _head_sum_matrix():
    i = jnp.arange(D_RWKV) // HEAD
    return (i[:, None] == i[None, :]).astype(BF16)


def _s5_param_fn(lam_re, lam_im, logdt, b_re, b_im):
    dt = jnp.exp(logdt)
    mag = jnp.exp(lam_re * dt)
    ang = lam_im * dt
    lbr = mag * jnp.cos(ang)
    lbi = mag * jnp.sin(ang)
    nr = lbr - 1.0
    den = lam_re * lam_re + lam_im * lam_im
    cr = (nr * lam_re + lbi * lam_im) / den
    ci = (lbi * lam_re - nr * lam_im) / den
    return lbr, lbi, cr * b_re - ci * b_im, cr * b_im + ci * b_re


def _cmul(ar, ai, br, bi):
    return ar * br - ai * bi, ar * bi + ai * br


def _s5_param_fwd(lam_re, lam_im, logdt, b_re, b_im):
    def body(lr, li, ld, br, bi, o_br, o_bi, o_pr, o_pi, o_qr, o_qi):
        lbr, lbi, bbr, bbi = _s5_param_fn(lr[...], li[...], ld[...], br[...], bi[...])
        o_br[...] = bbr
        o_bi[...] = bbi
        rid = lax.broadcasted_iota(jnp.int32, (8, N_STATE), 0)
        pr, pi_ = lbr, lbi
        fwd_r = rev_r = jnp.broadcast_to(pr, (8, N_STATE))
        fwd_i = rev_i = jnp.broadcast_to(pi_, (8, N_STATE))
        for j in range(1, 8):
            pr, pi_ = _cmul(pr, pi_, lbr, lbi)
            fwd_r = jnp.where(rid == j, jnp.broadcast_to(pr, (8, N_STATE)), fwd_r)
            fwd_i = jnp.where(rid == j, jnp.broadcast_to(pi_, (8, N_STATE)), fwd_i)
            rev_r = jnp.where(rid == 7 - j, jnp.broadcast_to(pr, (8, N_STATE)), rev_r)
            rev_i = jnp.where(rid == 7 - j, jnp.broadcast_to(pi_, (8, N_STATE)), rev_i)
        o_pr[...] = fwd_r
        o_pi[...] = fwd_i
        o_qr[...] = rev_r
        o_qi[...] = -rev_i

    return pl.pallas_call(
        body, name="s5_param_fwd",
        out_shape=[_sds((S5_GROUP, N_STATE))] * 2 + [_sds((8, N_STATE))] * 4,
    )(lam_re, lam_im, logdt, b_re, b_im)


def _s5_param_bwd(lam_re, lam_im, logdt, b_re, b_im, d_lbr, d_lbi, d_bbr, d_bbi, group_ind):
    def body(lr, li, ld, br, bi, g0, g1, g2, g3, ind, o_lr, o_li, o_ld, o_br, o_bi):
        _, vjp = jax.vjp(_s5_param_fn, lr[...], li[...], ld[...], br[...], bi[...])
        d_lr, d_li, d_ld, d_br, d_bi = vjp((g0[...], g1[...], g2[...], g3[...]))
        o_lr[...] = d_lr
        o_li[...] = d_li
        o_ld[...] = _dot(jnp.broadcast_to(d_ld, (8, N_STATE)), ind[...], ((1,), (0,)), HI)
        o_br[...] = d_br
        o_bi[...] = d_bi

    return pl.pallas_call(
        body, name="s5_param_bwd",
        out_shape=[_sds((1, N_STATE))] * 2 + [_sds((8, LANES))] + [_sds((S5_GROUP, N_STATE))] * 2,
    )(lam_re, lam_im, logdt, b_re, b_im, d_lbr, d_lbi, d_bbr, d_bbi, group_ind)


def _fwd_in(x, norm_g, w_in_bf, tt, late=()):
    L = x.shape[0]
    n_t = L // tt
    n_late = len(late)
    everyone = ("x", "y", "c")

    def body(x_ref, g_ref, w_ref, *refs):
        u_ref, zs_ref, rw_ref, zr_ref = refs[n_late:n_late + 4]
        if n_late:
            i = pl.program_id(0)
            copies = lambda: _exchange_copies(refs[:n_late], refs[n_late + 4:2 * n_late + 4], *refs[2 * n_late + 4:],
                                              (True,) * n_late, everyone)
            pl.when(i == 0)(lambda: _start_copies(copies()))
            pl.when(i == n_t - 1)(lambda: _wait_copies(copies()))
        h = _rms(x_ref[...], g_ref[...])
        proj = _dot(h.astype(BF16), w_ref[...], ((1,), (1,)), None)
        u_ref[...] = proj[:, 0:D_S5]
        zs_ref[...] = proj[:, D_S5:2 * D_S5]
        rw_ref[...] = proj[:, 2 * D_S5:2 * D_S5 + D_SHIFT]
        zr_ref[...] = proj[:, 2 * D_S5 + D_SHIFT:D_IN]

    row = lambda n: pl.BlockSpec((tt, n), lambda i: (i, 0))
    return pl.pallas_call(
        body, name="fwd_in", grid=(n_t,),
        in_specs=[row(D_MODEL), _const_spec((1, D_MODEL)), _const_spec((D_IN, D_MODEL))] + [_ANY] * n_late,
        out_specs=[row(D_S5), row(D_S5), row(D_SHIFT), row(D_RWKV)] + [_ANY] * n_late,
        out_shape=[_sds((L, D_S5)), _sds((L, D_S5)), _sds((L, D_SHIFT)), _sds((L, D_RWKV))]
        + _exchange_results(late, (True,) * n_late, everyone),
        scratch_shapes=_exchange_scratch(n_late, everyone) if n_late else [],
        compiler_params=pltpu.CompilerParams(dimension_semantics=("arbitrary",), vmem_limit_bytes=VMEM_LIMIT,
                                             has_side_effects=bool(n_late)),
    )(x, norm_g, w_in_bf, *late)


S5_LANE_CHUNK = 512


def _tile_scan(re_ref, im_ref, pow_r_ref, pow_i_ref, carry_r_ref, carry_i_ref, reverse):
    t, n = re_ref.shape
    n_groups = t // 8
    ch = S5_LANE_CHUNK
    rid = lax.broadcasted_iota(jnp.int32, (8, ch), 0)
    for c in range(n // ch):
        cols = slice(c * ch, (c + 1) * ch)
        pow_r = pow_r_ref[:, cols]
        pow_i = pow_i_ref[:, cols]
        row = lambda tile, j: jnp.broadcast_to(tile[j:j + 1], (8, ch))
        levels = []
        for d in (1, 2, 4):
            keep = (rid < 8 - d) if reverse else (rid >= d)
            j = 8 - d if reverse else d - 1
            levels.append(((8 - d) if reverse else d,
                           jnp.where(keep, row(pow_r, j), 0.0), jnp.where(keep, row(pow_i, j), 0.0)))

        def group(g, carry):
            r0 = pl.multiple_of(((n_groups - 1 - g) if reverse else g) * 8, 8)
            xr = re_ref[pl.ds(r0, 8), cols]
            xi = im_ref[pl.ds(r0, 8), cols]
            for shift, lr, li in levels:
                mr, mi = _cmul(lr, li, pltpu.roll(xr, shift, axis=0), pltpu.roll(xi, shift, axis=0))
                xr = xr + mr
                xi = xi + mi
            mr, mi = _cmul(pow_r, pow_i, carry[0], carry[1])
            xr = xr + mr
            xi = xi + mi
            re_ref[pl.ds(r0, 8), cols] = xr
            im_ref[pl.ds(r0, 8), cols] = xi
            last = 0 if reverse else 7
            return row(xr, last), row(xi, last)

        out = lax.fori_loop(0, n_groups, group, (carry_r_ref[:, cols], carry_i_ref[:, cols]))
        carry_r_ref[:, cols] = out[0]
        carry_i_ref[:, cols] = out[1]


def _s5_fwd(u, b4_re, b4_im, c4_re, c4_im, pow_r, pow_i, tt):
    L = u.shape[0]

    def body(u_ref, bre_ref, bim_ref, cre_ref, cim_ref, pr_ref, pi_ref, sre_o, sim_o, y_o, car_r, car_i):
        @pl.when(pl.program_id(0) == 0)
        def _():
            car_r[...] = jnp.zeros_like(car_r)
            car_i[...] = jnp.zeros_like(car_i)

        uv = u_ref[...]
        for q in range(S5_BLOCKS):
            uq = uv[:, q * LANES:(q + 1) * LANES]
            cols = slice(q * 512, (q + 1) * 512)
            sre_o[:, cols] = _dot_bf(uq, bre_ref[q], ((1,), (0,)))
            sim_o[:, cols] = _dot_bf(uq, bim_ref[q], ((1,), (0,)))
        _tile_scan(sre_o, sim_o, pr_ref, pi_ref, car_r, car_i, reverse=False)
        for q in range(S5_BLOCKS):
            cols = slice(q * 512, (q + 1) * 512)
            y_o[:, q * LANES:(q + 1) * LANES] = (_dot_bf(sre_o[:, cols], cre_ref[q], ((1,), (0,)))
                                                 - _dot_bf(sim_o[:, cols], cim_ref[q], ((1,), (0,))))

    row = lambda n: pl.BlockSpec((tt, n), lambda i: (i, 0))
    return pl.pallas_call(
        body, name="s5_fwd", grid=(L // tt,),
        in_specs=[row(D_S5)] + [_const_spec((S5_BLOCKS, LANES, 512))] * 2 + [_const_spec((S5_BLOCKS, 512, LANES))] * 2
        + [_const_spec((8, N_STATE))] * 2,
        out_specs=[row(N_STATE), row(N_STATE), row(D_S5)],
        out_shape=[_sds((L, N_STATE)), _sds((L, N_STATE)), _sds((L, D_S5))],
        scratch_shapes=[pltpu.VMEM((8, N_STATE), F32)] * 2,
        compiler_params=_params("arbitrary"),
    )(u, b4_re, b4_im, c4_re, c4_im, pow_r, pow_i)


def _rwkv_pre_fn(r, k, v, wa, w0, w2p, a0, a2p, k_k, k_a, ee):
    w = -_softplus(-(w0 + mm_bf(jnp.tanh(wa), w2p))) - 0.5
    logw = -jnp.exp(w)
    a = _sigmoid(a0 + mm_bf(wa, a2p))
    kk = _normalize_heads(k * k_k, ee)
    k2 = k * (1.0 + (a - 1.0) * k_a)
    return r, logw, k2, v, -kk, kk * a


N_PAIRS = N_HEADS // 2


def _head_spec(tt):
    return pl.BlockSpec((N_PAIRS, tt, LANES), lambda i: (0, i, 0))


def _load_heads(ref):
    return jnp.concatenate([ref[p] for p in range(N_PAIRS)], axis=-1)


def _store_heads(ref, val):
    for p in range(N_PAIRS):
        ref[p] = val[:, p * LANES:(p + 1) * LANES]


def _split_pairs(x):
    return jnp.concatenate([x[:, :, :HEAD], x[:, :, HEAD:]], axis=0)


def _join_pairs(x):
    return jnp.concatenate([x[:N_PAIRS], x[N_PAIRS:]], axis=-1)


def _shifted(rw, prev_blk, first):
    rolled = pltpu.roll(rw, 1, axis=0)
    prev_row = jnp.where(first, 0.0, prev_blk[7:8, :])
    rid = lax.broadcasted_iota(jnp.int32, rw.shape, 0)
    return jnp.where(rid == 0, jnp.broadcast_to(prev_row, rw.shape), rolled)


def _split_rw(t):
    return t[:, 0:512], t[:, 512:1024], t[:, 1024:1536], t[:, 1536:1664]


def _rwkv_pre_specs(tt):
    row = pl.BlockSpec((tt, D_SHIFT), lambda i: (i, 0))
    prev = pl.BlockSpec((8, D_SHIFT), lambda i: (jnp.maximum(i * (tt // 8) - 1, 0), 0))
    consts = [_const_spec((1, D_SHIFT)), _const_spec((1, D_RWKV)), _const_spec((LANES, D_RWKV)),
              _const_spec((1, D_RWKV)), _const_spec((LANES, D_RWKV)), _const_spec((1, D_RWKV)),
              _const_spec((1, D_RWKV)), _const_spec((D_RWKV, D_RWKV))]
    return [row, prev] + consts


def _rwkv_pre_fwd(rw, mu, w0, w2p, a0, a2p, k_k, k_a, ee, tt):
    L = rw.shape[0]

    def body(rw_ref, prev_ref, mu_ref, w0_ref, w2_ref, a0_ref, a2_ref, kk_ref, ka_ref, ee_ref, *outs):
        rwv = rw_ref[...]
        rws = rwv + (_shifted(rwv, prev_ref[...], pl.program_id(0) == 0) - rwv) * mu_ref[...]
        res = _rwkv_pre_fn(*_split_rw(rws), w0_ref[...], w2_ref[...], a0_ref[...], a2_ref[...],
                           kk_ref[...], ka_ref[...], ee_ref[...])
        for o, val in zip(outs, res):
            _store_heads(o, val)

    return pl.pallas_call(
        body, name="rwkv_pre_fwd", grid=(L // tt,),
        in_specs=_rwkv_pre_specs(tt), out_specs=[_head_spec(tt)] * 6, out_shape=[_sds((N_PAIRS, L, LANES))] * 6,
        compiler_params=_params("parallel"),
    )(rw, rw, mu, w0, w2p, a0, a2p, k_k, k_a, ee)


def _rwkv_pre_bwd(rw, mu, w0, w2p, a0, a2p, k_k, k_a, ee, cots, tt):
    L = rw.shape[0]
    n_t = L // tt

    def body(rw_ref, prev_ref, mu_ref, w0_ref, w2_ref, a0_ref, a2_ref, kk_ref, ka_ref, ee_ref,
             c_r, c_w, c_k, c_v, c_a, c_b, cb_r, cb_k, cb_v,
             drws_ref, dmu_o, dw0_o, dw2_o, da0_o, da2_o, dkk_o, dka_o,
             dmu, dw0, dw2, da0, da2, dkk, dka):
        i = pl.program_id(0)
        accs = (dmu, dw0, dw2, da0, da2, dkk, dka)

        @pl.when(i == 0)
        def _():
            for acc in accs:
                acc[...] = jnp.zeros_like(acc)

        rwv = rw_ref[...]
        diff = _shifted(rwv, prev_ref[...], i == 0) - rwv
        rws = rwv + diff * mu_ref[...]
        consts = (w0_ref[...], w2_ref[...], a0_ref[...], a2_ref[...], kk_ref[...], ka_ref[...])
        _, vjp = jax.vjp(lambda *a: _rwkv_pre_fn(*a, ee_ref[...]), *_split_rw(rws), *consts)
        scan = [_load_heads(c) for c in (c_r, c_w, c_k, c_v, c_a, c_b)]
        g = vjp((scan[0] + cb_r[...], scan[1], scan[2] + cb_k[...], scan[3] + cb_v[...], scan[4], scan[5]))
        drws = jnp.concatenate(g[0:4], axis=-1)
        drws_ref[...] = drws
        dmu[...] += jnp.sum(drws * diff, axis=0, keepdims=True)
        for acc, val in zip(accs[1:], g[4:]):
            acc[...] += val

        @pl.when(i == n_t - 1)
        def _():
            for acc, out in zip(accs, (dmu_o, dw0_o, dw2_o, da0_o, da2_o, dkk_o, dka_o)):
                out[...] = acc[...]

    row = pl.BlockSpec((tt, D_RWKV), lambda i: (i, 0))
    shapes = [(1, D_SHIFT), (1, D_RWKV), (LANES, D_RWKV), (1, D_RWKV), (LANES, D_RWKV), (1, D_RWKV), (1, D_RWKV)]
    return pl.pallas_call(
        body, name="rwkv_pre_bwd", grid=(n_t,),
        in_specs=_rwkv_pre_specs(tt) + [_head_spec(tt)] * 6 + [row] * 3,
        out_specs=[pl.BlockSpec((tt, D_SHIFT), lambda i: (i, 0))] + [_acc_spec(s) for s in shapes],
        out_shape=[_sds((L, D_SHIFT))] + [_sds(s) for s in shapes],
        scratch_shapes=[pltpu.VMEM(s, F32) for s in shapes],
        compiler_params=_params("arbitrary"),
    )(rw, rw, mu, w0, w2p, a0, a2p, k_k, k_a, ee, *cots)


def _bmm(a, b):
    return lax.dot_general(a, b, (((2,), (1,)), ((0,), (0,))), precision=HI, preferred_element_type=F32)


def _bmm_nt(a, b):
    return lax.dot_general(a, b, (((2,), (2,)), ((0,), (0,))), precision=HI, preferred_element_type=F32)


def _bmm_tn(a, b):
    return lax.dot_general(a, b, (((1,), (1,)), ((0,), (0,))), precision=HI, preferred_element_type=F32)


def _bdot_bf(a, b, lhs_dim, rhs_dim):
    return lax.dot_general(a.astype(BF16), b.astype(BF16), (((lhs_dim,), (rhs_dim,)), ((0,), (0,))),
                           preferred_element_type=F32)


@jax.custom_vjp
def _bmm_bf(a, b):
    return _bdot_bf(a, b, 2, 1)


def _bmm_bf_fwd(a, b):
    return _bmm_bf(a, b), (a, b)


def _bmm_bf_bwd(res, g):
    a, b = res
    return _bdot_bf(g, b, 2, 2), _bdot_bf(a, g, 1, 1)


_bmm_bf.defvjp(_bmm_bf_fwd, _bmm_bf_bwd)


@jax.custom_vjp
def _bmm_tn_bf(a, b):
    return _bdot_bf(a, b, 1, 1)


def _bmm_tn_bf_fwd(a, b):
    return _bmm_tn_bf(a, b), (a, b)


def _bmm_tn_bf_bwd(res, g):
    a, b = res
    return _bdot_bf(b, g, 2, 2), _bdot_bf(a, g, 2, 1)


_bmm_tn_bf.defvjp(_bmm_tn_bf_fwd, _bmm_tn_bf_bwd)


def _unit_lower_inverse(a):
    t = a.shape[-1]
    ti = lax.broadcasted_iota(jnp.int32, (t, t), 0)
    si = lax.broadcasted_iota(jnp.int32, (t, t), 1)

    def same_block(bits):
        shift = jnp.int32(bits)
        return (lax.shift_right_logical(ti, shift) == lax.shift_right_logical(si, shift))[None]

    def mm(x, y):
        return _bdot_bf(x, y, 2, 1)

    d = jnp.where(same_block(3), a, 0.0)
    inv = jnp.where(ti == si, 1.0, 0.0)[None] + d
    pw = mm(d, d)
    both = mm(jnp.concatenate([inv, pw], axis=1), pw)
    inv = inv + both[:, :t]
    inv = inv + mm(inv, both[:, t:])
    bits = 3
    while (1 << bits) < t:
        e = jnp.where(same_block(bits), 0.0, jnp.where(same_block(bits + 1), a, 0.0))
        inv = inv + mm(mm(inv, e), inv)
        bits += 1
    return inv


def _tri_mask(t):
    ri = lax.broadcasted_iota(jnp.int32, (2 * t, 2 * t), 0)
    ci = lax.broadcasted_iota(jnp.int32, (2 * t, 2 * t), 1)
    top_rows = ri < t
    diff = jnp.where(top_rows, ri, ri - t) - jnp.where(ci < t, ci, ci - t)
    return (diff >= jnp.where(top_rows, 1, 0))[None]


def _ones_tri(n_h, t):
    ti = lax.broadcasted_iota(jnp.int32, (t, t), 0)
    si = lax.broadcasted_iota(jnp.int32, (t, t), 1)
    return jnp.broadcast_to(jnp.where(ti >= si, 1.0, 0.0)[None], (n_h, t, t))


@jax.custom_vjp
def _running_sum_kept(logw, kept):
    return kept


def _running_sum_kept_bwd(shape, g):
    return _bmm_tn(_ones_tri(shape[0], shape[1]), g), jnp.zeros_like(g)


_running_sum_kept.defvjp(lambda logw, kept: (kept, logw.shape), _running_sum_kept_bwd)


@jax.custom_vjp
def _tri_products_kept(ar, bk, kept):
    return kept


def _tri_products_kept_bwd(res, g):
    ar, bk = res
    g = jnp.where(_tri_mask(ar.shape[1] // 2), g, 0.0)
    return _bmm(g, bk), _bmm_tn(g, ar), jnp.zeros_like(g)


_tri_products_kept.defvjp(lambda ar, bk, kept: (kept, (ar, bk)), _tri_products_kept_bwd)


@jax.custom_vjp
def _solve_unit_lower(a, rhs, inv, kept=None):
    return _bmm(inv, rhs) if kept is None else kept


def _solve_fwd(a, rhs, inv, kept=None):
    u = _bmm(inv, rhs) if kept is None else kept
    return u, (inv, u, kept is not None)


def _solve_bwd(res, du):
    inv, u, had_kept = res
    d_rhs = _bmm_tn(inv, du)
    return _bmm_nt(d_rhs, u), d_rhs, jnp.zeros_like(inv), (jnp.zeros_like(u) if had_kept else None)


_solve_unit_lower.defvjp(_solve_fwd, _solve_bwd)


def _rwkv_chunk(st0, r, logw, k, v, a, b, kept=None):
    n_h, t, _ = r.shape
    log_p = _bmm(_ones_tri(n_h, t), logw) if kept is None else _running_sum_kept(logw, kept[0])
    p_in = jnp.exp(log_p)
    p_inv = jnp.exp(-log_p)
    at = a * jnp.exp(log_p - logw)
    rt = r * p_in
    ar = jnp.concatenate([at, rt], axis=1)
    bk = jnp.concatenate([b * p_inv, k * p_inv], axis=1)
    if kept is None:
        m = jnp.where(_tri_mask(t), _bmm_nt(ar, bk), 0.0)
        inv = _unit_lower_inverse(m[:, :t, :t])
    else:
        m = _tri_products_kept(ar, bk, kept[1])
        inv = kept[2]
    top, bottom = m[:, :t], m[:, t:]
    rhs = _bmm_bf(jnp.concatenate([at, top[:, :, t:]], axis=2), jnp.concatenate([st0, v], axis=1))
    u = _solve_unit_lower(top[:, :, :t], rhs, inv, None if kept is None else kept[3])
    y = _bmm_bf(jnp.concatenate([rt, bottom], axis=2), jnp.concatenate([st0, u, v], axis=1))
    p_end = jnp.swapaxes(p_in[:, t - 1:t, :], 1, 2)
    st1 = (st0 + _bmm_tn_bf(bk, jnp.concatenate([u, v], axis=1))) * p_end
    return y, st1, (log_p, m, inv, u)


def _rwkv_scan_fwd(ops):
    n_h, L, n = N_HEADS, ops[0].shape[1], HEAD
    t = RWKV_CHUNK
    per = min(RWKV_CHUNKS_PER_STEP, L // t)
    n_c = L // t
    n_s = n_c // per

    def body(r_ref, w_ref, k_ref, v_ref, a_ref, b_ref, y_ref, st_ref, logp_ref, m_ref, inv_ref, u_ref, st):
        @pl.when(pl.program_id(0) == 0)
        def _():
            st[...] = jnp.zeros_like(st)

        st0 = st[...]
        for j in range(per):
            rows = slice(j * t, (j + 1) * t)
            st_ref[j] = st0
            y, st0, (log_p, m, inv, u) = _rwkv_chunk(
                st0, *(_split_pairs(ref[:, rows, :]) for ref in (r_ref, w_ref, k_ref, v_ref, a_ref, b_ref)))
            y_ref[:, rows, :] = _join_pairs(y)
            logp_ref[:, rows, :] = log_p
            u_ref[:, rows, :] = u
            m_ref[j] = m
            inv_ref[j] = inv
        st[...] = st0

    pairs = pl.BlockSpec((N_PAIRS, per * t, LANES), lambda c: (0, c, 0))
    blk = pl.BlockSpec((n_h, per * t, n), lambda c: (0, c, 0))
    per_chunk = lambda m: pl.BlockSpec((per, n_h, m, m), lambda c: (c, 0, 0, 0))
    return pl.pallas_call(
        body, name="rwkv_scan_fwd", grid=(n_s,), in_specs=[pairs] * 6,
        out_specs=[pairs, per_chunk(n), blk, per_chunk(2 * t), per_chunk(t), blk],
        out_shape=[_sds((N_PAIRS, L, LANES)), _sds((n_c, n_h, n, n)), _sds((n_h, L, n)),
                   _sds((n_c, n_h, 2 * t, 2 * t)), _sds((n_c, n_h, t, t)), _sds((n_h, L, n))],
        scratch_shapes=[pltpu.VMEM((n_h, n, n), F32)],
        compiler_params=_params("arbitrary"),
    )(*ops)


def _rwkv_scan_bwd(ops, states, kept, dy):
    n_h, L, n = N_HEADS, ops[0].shape[1], HEAD
    t = RWKV_CHUNK
    per = min(RWKV_CHUNKS_PER_STEP, L // t)
    n_s = L // t // per

    def body(r_ref, w_ref, k_ref, v_ref, a_ref, b_ref, st_ref, logp_ref, m_ref, inv_ref, u_ref, dy_ref,
             dr, dw, dk, dv, da, db, dst):
        @pl.when(pl.program_id(0) == 0)
        def _():
            dst[...] = jnp.zeros_like(dst)

        vjps = []
        for j in range(per):
            rows = slice(j * t, (j + 1) * t)
            have = (logp_ref[:, rows, :], m_ref[j], inv_ref[j], u_ref[:, rows, :])
            args = [_split_pairs(ref[:, rows, :]) for ref in (r_ref, w_ref, k_ref, v_ref, a_ref, b_ref)]
            vjps.append(jax.vjp(lambda *a, have=have: _rwkv_chunk(*a, kept=have)[:2], st_ref[j], *args)[1])
        d_state = dst[...]
        for j in reversed(range(per)):
            rows = slice(j * t, (j + 1) * t)
            g = vjps[j]((_split_pairs(dy_ref[:, rows, :]), d_state))
            d_state = g[0]
            for out, val in zip((dr, dw, dk, dv, da, db), g[1:]):
                out[:, rows, :] = _join_pairs(val)
        dst[...] = d_state

    pairs = pl.BlockSpec((N_PAIRS, per * t, LANES), lambda c: (0, n_s - 1 - c, 0))
    blk = pl.BlockSpec((n_h, per * t, n), lambda c: (0, n_s - 1 - c, 0))
    per_chunk = lambda m: pl.BlockSpec((per, n_h, m, m), lambda c: (n_s - 1 - c, 0, 0, 0))
    return pl.pallas_call(
        body, name="rwkv_scan_bwd", grid=(n_s,),
        in_specs=[pairs] * 6 + [per_chunk(n), blk, per_chunk(2 * t), per_chunk(t), blk, pairs],
        out_specs=[pairs] * 6, out_shape=[_sds((N_PAIRS, L, LANES))] * 6,
        scratch_shapes=[pltpu.VMEM((n_h, n, n), F32)],
        compiler_params=_params("arbitrary"),
    )(*ops, states, *kept, dy)


def _post_fn(x, u, zs, zr, ysc, r, k2, v, y_ssm, gate_in, d, glu_b, ln_w, ln_b, r_k, gf,
              glu_w, wo_s5, wo_rwkv, tgt, ee):
    y3 = _gelu(y_ssm + d * u)
    y_s5 = y3 * _sigmoid(mm_w(y3, glu_w) + glu_b + gate_in) * _silu(zs)
    mean = head_sum_split(ysc, ee) * (1.0 / HEAD)
    yc = ysc - mean
    var = head_sum(yc * yc, ee) * (1.0 / HEAD)
    gn = yc * lax.rsqrt(var + GN_EPS) * ln_w + ln_b
    bonus = head_sum(r * k2 * r_k, ee) * v
    y_rwkv = (gn + bonus) * _silu(zr)
    x2 = x + mm_w(y_s5, wo_s5) + mm_w(y_rwkv, wo_rwkv)
    err = _rms(x2, gf) - tgt
    return 0.5 * jnp.mean(err * err, axis=-1, keepdims=True), (y3, y_s5, y_rwkv)


def _post(x, u, zs, zr, ysc, r, k2, v, y_ssm, d, glu_w, glu_b, ln_w, ln_b, r_k, w_out, gf, tgt, ee, tt):
    L = x.shape[0]
    n_t = L // tt
    acc_shapes = [(1, D_S5), (D_S5, D_S5), (1, D_S5), (1, D_RWKV), (1, D_RWKV), (1, D_RWKV),
                  (D_MODEL, D_MODEL), (1, D_MODEL), (8, LANES)]

    def body(x_ref, u_ref, zs_ref, zr_ref, ysc_ref, r_ref, k2_ref, v_ref, yssm_ref,
             d_ref, gw_ref, gb_ref, lw_ref, lb_ref, rk_ref, wo_ref, gf_ref, tgt_ref, ee_ref,
             dx_o, du_o, dzs_o, dzr_o, dysc_o, dr_o, dk2_o, dv_o, dyssm_o,
             dd_o, dgw_o, dgb_o, dlw_o, dlb_o, drk_o, dwo_o, dgf_o, loss_o,
             dd, dgw, dgb, dlw, dlb, drk, dwo, dgf, loss):
        i = pl.program_id(0)
        accs = (dd, dgw, dgb, dlw, dlb, drk, dwo, dgf, loss)

        @pl.when(i == 0)
        def _():
            for acc in accs:
                acc[...] = jnp.zeros_like(acc)

        args = (x_ref[...], u_ref[...], zs_ref[...], zr_ref[...],
                _load_heads(ysc_ref), _load_heads(r_ref), _load_heads(k2_ref), _load_heads(v_ref), yssm_ref[...],
                jnp.zeros((tt, D_S5), F32), d_ref[...], gb_ref[...], lw_ref[...], lb_ref[...], rk_ref[...], gf_ref[...])
        rows, vjp, (y3, y_s5, y_rwkv) = jax.vjp(
            lambda *a: _post_fn(*a, gw_ref[...], wo_ref[0:D_S5, :], wo_ref[D_S5:D_MODEL, :], tgt_ref[...],
                                ee_ref[...]), *args, has_aux=True)
        g = vjp(jnp.ones_like(rows))
        for out, val in zip((dx_o, du_o, dzs_o, dzr_o), g[0:4]):
            out[...] = val
        _store_heads(dysc_o, g[4])
        for out, val in zip((dr_o, dk2_o, dv_o, dyssm_o), g[5:9]):
            out[...] = val
        for acc, val in zip((dd, dgb, dlw, dlb, drk, dgf), g[10:16]):
            acc[...] += val
        dgw[...] += _dot_bf(y3, g[9], ((0,), (0,)))
        dwo[0:D_S5, :] += _dot_bf(y_s5, g[0], ((0,), (0,)))
        dwo[D_S5:D_MODEL, :] += _dot_bf(y_rwkv, g[0], ((0,), (0,)))
        loss[...] += jnp.broadcast_to(jnp.sum(rows, axis=0, keepdims=True), loss.shape)

        @pl.when(i == n_t - 1)
        def _():
            for acc, out in zip(accs, (dd_o, dgw_o, dgb_o, dlw_o, dlb_o, drk_o, dwo_o, dgf_o, loss_o)):
                pltpu.sync_copy(acc, out)

    row = lambda n: pl.BlockSpec((tt, n), lambda i: (i, 0))
    in_specs = ([row(D_MODEL)] + [row(512)] * 3 + [_head_spec(tt)] * 4 + [row(D_S5)]
                + [_const_spec(s) for s in [(1, D_S5), (D_S5, D_S5), (1, D_S5), (1, D_RWKV), (1, D_RWKV), (1, D_RWKV),
                                            (D_MODEL, D_MODEL), (1, D_MODEL)]]
                + [row(D_MODEL), _const_spec((D_RWKV, D_RWKV))])
    out_rows = [D_MODEL] + [512] * 3 + [None] + [512] * 4
    return pl.pallas_call(
        body, name="post_fwd_bwd", grid=(n_t,), in_specs=in_specs,
        out_specs=[row(n) if n else _head_spec(tt) for n in out_rows] + [_ANY] * len(acc_shapes),
        out_shape=([_sds((L, n)) if n else _sds((N_PAIRS, L, LANES)) for n in out_rows]
                   + [_sds(s) for s in acc_shapes]),
        scratch_shapes=[pltpu.VMEM(s, F32) for s in acc_shapes],
        compiler_params=_params("arbitrary"),
    )(x, u, zs, zr, ysc, r, k2, v, y_ssm, d, glu_w, glu_b, ln_w, ln_b, r_k, w_out, gf, tgt, ee)


def _s5_bwd(u, du_direct, dy, s_re, s_im, b4_re, b4_im, c4_re, c4_im, pow_r, pow_i, tt, early=()):
    L = u.shape[0]
    n_t = L // tt
    n_e = len(early)
    everyone = ("x", "y", "c")
    acc_shapes = ([(S5_BLOCKS, LANES, 512)] * 2 + [(S5_BLOCKS, 512, LANES)] * 2 + [(1, N_STATE)] * 2)

    def body(u_ref, dud_ref, dy_ref, sre_ref, sim_ref, pre_ref, pim_ref, bre_ref, bim_ref, cre_ref, cim_ref,
             pr_ref, pi_ref, *refs):
        du_o, dbre_o, dbim_o, dcre_o, dcim_o, dlr_o, dli_o = refs[n_e:n_e + 7]
        dbre, dbim, dcre, dcim, dlr, dli, gre, gim, car_r, car_i = refs[2 * n_e + 7:2 * n_e + 17]
        copies = lambda: _exchange_copies(refs[:n_e], refs[n_e + 7:2 * n_e + 7], *refs[2 * n_e + 17:],
                                          (False,) * n_e, everyone)
        i = pl.program_id(0)

        @pl.when(i == 0)
        def _():
            for acc in (dbre, dbim, dcre, dcim, dlr, dli, car_r, car_i):
                acc[...] = jnp.zeros_like(acc)
            if n_e:
                _start_copies(copies())

        uv = u_ref[...]
        dyv = dy_ref[...]
        blocks = [slice(q * 512, (q + 1) * 512) for q in range(S5_BLOCKS)]
        lanes = [slice(q * LANES, (q + 1) * LANES) for q in range(S5_BLOCKS)]
        for q in range(S5_BLOCKS):
            gre[:, blocks[q]] = _dot_bf(dyv[:, lanes[q]], cre_ref[q], ((1,), (1,)))
            gim[:, blocks[q]] = -_dot_bf(dyv[:, lanes[q]], cim_ref[q], ((1,), (1,)))
        _tile_scan(gre, gim, pr_ref, pi_ref, car_r, car_i, reverse=True)
        for q in range(S5_BLOCKS):
            gr = gre[:, blocks[q]]
            gi = gim[:, blocks[q]]
            sr = sre_ref[:, blocks[q]]
            si = sim_ref[:, blocks[q]]
            du_o[:, lanes[q]] = (dud_ref[:, lanes[q]] + _dot_bf(gr, bre_ref[q], ((1,), (1,)))
                                 + _dot_bf(gi, bim_ref[q], ((1,), (1,))))
            dbre[q] += _dot_bf(uv[:, lanes[q]], gr, ((0,), (0,)))
            dbim[q] += _dot_bf(uv[:, lanes[q]], gi, ((0,), (0,)))
            dcre[q] += _dot_bf(sr, dyv[:, lanes[q]], ((0,), (0,)))
            dcim[q] -= _dot_bf(si, dyv[:, lanes[q]], ((0,), (0,)))
            rid = lax.broadcasted_iota(jnp.int32, sr.shape, 0)
            first = i == n_t - 1
            prev_r = jnp.where(first, 0.0, pre_ref[7:8, blocks[q]])
            prev_i = jnp.where(first, 0.0, pim_ref[7:8, blocks[q]])
            pr = jnp.where(rid == 0, jnp.broadcast_to(prev_r, sr.shape), pltpu.roll(sr, 1, axis=0))
            pi_ = jnp.where(rid == 0, jnp.broadcast_to(prev_i, si.shape), pltpu.roll(si, 1, axis=0))
            dlr[:, blocks[q]] += jnp.sum(pr * gr + pi_ * gi, axis=0, keepdims=True)
            dli[:, blocks[q]] += jnp.sum(pr * gi - pi_ * gr, axis=0, keepdims=True)

        @pl.when(i == n_t - 1)
        def _():
            for acc, out in zip((dbre, dbim, dcre, dcim, dlr, dli), (dbre_o, dbim_o, dcre_o, dcim_o, dlr_o, dli_o)):
                out[...] = acc[...]
            if n_e:
                _wait_copies(copies())

    row = lambda n: pl.BlockSpec((tt, n), lambda i: (n_t - 1 - i, 0))
    prev = pl.BlockSpec((8, N_STATE), lambda i: (jnp.maximum((n_t - 1 - i) * (tt // 8) - 1, 0), 0))
    return pl.pallas_call(
        body, name="s5_bwd", grid=(n_t,),
        in_specs=[row(D_S5)] * 3 + [row(N_STATE)] * 2 + [prev] * 2
        + [_const_spec((S5_BLOCKS, LANES, 512))] * 2 + [_const_spec((S5_BLOCKS, 512, LANES))] * 2
        + [_const_spec((8, N_STATE))] * 2 + [_ANY] * n_e,
        out_specs=[row(D_S5)] + [_acc_spec(s) for s in acc_shapes] + [_ANY] * n_e,
        out_shape=[_sds((L, D_S5))] + [_sds(s) for s in acc_shapes] + _exchange_results(early, (False,) * n_e, everyone),
        scratch_shapes=[pltpu.VMEM(s, F32) for s in acc_shapes] + [pltpu.VMEM((tt, N_STATE), F32)] * 2
        + [pltpu.VMEM((8, N_STATE), F32)] * 2 + (_exchange_scratch(n_e, everyone) if n_e else []),
        compiler_params=pltpu.CompilerParams(dimension_semantics=("arbitrary",), vmem_limit_bytes=VMEM_LIMIT,
                                             has_side_effects=bool(n_e)),
    )(u, du_direct, dy, s_re, s_im, s_re, s_im, b4_re, b4_im, c4_re, c4_im, pow_r, pow_i, *early)


def _bwd_in(x, norm_g, w_in_bf, mu, dx2, du, dzs, drws, dzr, tt):
    L = x.shape[0]
    n_t = L // tt

    def body(x_ref, g_ref, w_ref, mu_ref, dx2_ref, du_ref, dzs_ref, drws_ref, nxt_ref, dzr_ref,
             gx_o, dw_o, dg_o, dproj, dw, dg, stage):
        i = pl.program_id(0)

        @pl.when(i == 0)
        def _():
            dw[...] = jnp.zeros_like(dw)
            dg[...] = jnp.zeros_like(dg)

        drws_v = drws_ref[...]
        rid = lax.broadcasted_iota(jnp.int32, drws_v.shape, 0)
        nxt_row = jnp.where(i == n_t - 1, 0.0, nxt_ref[0:1, :])
        nxt = jnp.where(rid == tt - 1, jnp.broadcast_to(nxt_row, drws_v.shape), pltpu.roll(drws_v, tt - 1, axis=0))
        muv = mu_ref[...]
        drw = drws_v * (1.0 - muv) + nxt * muv
        dproj[:, 0:D_S5] = du_ref[...].astype(BF16)
        dproj[:, D_S5:2 * D_S5] = dzs_ref[...].astype(BF16)
        dproj[:, 2 * D_S5:2 * D_S5 + D_SHIFT] = drw.astype(BF16)
        dproj[:, 2 * D_S5 + D_SHIFT:D_IN] = dzr_ref[...].astype(BF16)
        dh = _dot(dproj[...], w_ref[...], ((1,), (0,)), None)
        h, vjp = jax.vjp(_rms, x_ref[...], g_ref[...])
        dxh, dgv = vjp(dh)
        gx_o[...] = dx2_ref[...] + dxh
        dg[...] += dgv
        dw[...] += _dot(dproj[...], h.astype(BF16), ((0,), (0,)), None)

        @pl.when(i == n_t - 1)
        def _():
            dg_o[...] = dg[...]
            for j in range(D_IN // stage.shape[0]):
                rows = pl.ds(j * stage.shape[0], stage.shape[0])
                stage[...] = dw[rows, :].astype(BF16)
                pltpu.sync_copy(stage, dw_o.at[rows])

    row = lambda n: pl.BlockSpec((tt, n), lambda i: (i, 0))
    nxt = pl.BlockSpec((8, D_SHIFT), lambda i: (jnp.minimum((i + 1) * (tt // 8), L // 8 - 1), 0))
    return pl.pallas_call(
        body, name="bwd_in", grid=(n_t,),
        in_specs=[row(D_MODEL), _const_spec((1, D_MODEL)), _const_spec((D_IN, D_MODEL)), _const_spec((1, D_SHIFT)),
                  row(D_MODEL), row(D_S5), row(D_S5), row(D_SHIFT), nxt, row(D_RWKV)],
        out_specs=[row(D_MODEL), _ANY, _acc_spec((1, D_MODEL))],
        out_shape=[_sds((L, D_MODEL)), jax.ShapeDtypeStruct((D_IN, D_MODEL), BF16), _sds((1, D_MODEL))],
        scratch_shapes=[pltpu.VMEM((tt, D_IN), BF16), pltpu.VMEM((D_IN, D_MODEL), F32), pltpu.VMEM((1, D_MODEL), F32),
                        pltpu.VMEM((D_IN // N_DEV, D_MODEL), BF16)],
        compiler_params=_params("arbitrary"),
    )(x, norm_g, w_in_bf, mu, dx2, du, dzs, drws, drws, dzr)


def _block_diag_b(bbar):
    bb = bbar.reshape(S5_GROUP, S5_BLOCKS, 8, S5_STATE)
    return jnp.einsum('hqgp,Gg->qGhgp', bb, jnp.eye(8, dtype=F32)).reshape(S5_BLOCKS, LANES, 512)


def _block_diag_b_t(db4):
    d = db4.reshape(S5_BLOCKS, 8, S5_GROUP, 8, S5_STATE)
    return jnp.einsum('qGhgp,Gg->hqgp', d, jnp.eye(8, dtype=F32)).reshape(S5_GROUP, N_STATE)


def _block_diag_c(c):
    cc = c.reshape(S5_BLOCKS, 8, S5_GROUP, S5_STATE)
    return jnp.einsum('qghp,gG->qgpGh', cc, jnp.eye(8, dtype=F32)).reshape(S5_BLOCKS, 512, LANES)


def _block_diag_c_t(dc4):
    d = dc4.reshape(S5_BLOCKS, 8, S5_STATE, 8, S5_GROUP)
    return jnp.einsum('qgpGh,gG->qghp', d, jnp.eye(8, dtype=F32)).reshape(S5_GROUPS, S5_GROUP, S5_STATE)


def _local_step(x, tgt, w, late=()):
    L = x.shape[0]
    tt = min(512, L)
    tp = min(256, L)
    ee = _head_sum_matrix()

    lam_re = w['s5_lam_re'].reshape(1, N_STATE)
    lam_im = w['s5_lam_im'].reshape(1, N_STATE)
    logdt = jnp.repeat(w['s5_log_dt'], S5_STATE).reshape(1, N_STATE)
    b_re_t = w['s5_b_re'].transpose(2, 0, 1).reshape(S5_GROUP, N_STATE)
    b_im_t = w['s5_b_im'].transpose(2, 0, 1).reshape(S5_GROUP, N_STATE)
    bbr, bbi, pow_r, pow_i, rpow_r, rpow_i = _s5_param_fwd(lam_re, lam_im, logdt, b_re_t, b_im_t)
    b4_re, b4_im = _block_diag_b(bbr), _block_diag_b(bbi)
    c4_re, c4_im = _block_diag_c(w['s5_c_re']), _block_diag_c(w['s5_c_im'])

    norm_g = w['norm_g'].reshape(1, D_MODEL)
    w_in_bf = (w['w_in_t'] if 'w_in_t' in w else w['w_in'].T).astype(BF16)
    u, zs, rw, zr, *gathered = _fwd_in(x, norm_g, w_in_bf, tt, [shard for _, shard in late])
    w = dict(w, **{n: _join_shards(n, blocks) for (n, _), blocks in zip(late, gathered)})
    s_re, s_im, y_ssm = _s5_fwd(u, b4_re, b4_im, c4_re, c4_im, pow_r, pow_i, tt)

    row = lambda t: t.reshape(1, -1)
    zpad = jnp.zeros((HEAD, D_RWKV), F32)
    w2p = jnp.concatenate([w['rwkv_w2'], zpad], axis=0)
    a2p = jnp.concatenate([zpad, w['rwkv_a2']], axis=0)
    pre_consts = (row(w['rwkv_mu']), row(w['rwkv_w0']), w2p, row(w['rwkv_a0']), a2p,
                  row(w['rwkv_k_k']), row(w['rwkv_k_a']), ee)
    ops = _rwkv_pre_fwd(rw, *pre_consts, tt)
    ysc, states, *kept = _rwkv_scan_fwd(ops)

    post = _post(x, u, zs, zr, ysc, ops[0], ops[2], ops[3], y_ssm,
                 row(w['s5_d']), w['s5_glu_w'].astype(BF16), row(w['s5_glu_b']), row(w['rwkv_ln_w']), row(w['rwkv_ln_b']),
                 row(w['rwkv_r_k']), w['w_out'].astype(BF16), row(w['final_g']), tgt, ee, tp)
    (dx2, du_d, dzs, dzr, dysc, dr_b, dk2_b, dv_b, dy_ssm,
     dd, dglu_w, dglu_b, dln_w, dln_b, dr_k, dw_out, dgf, loss) = post

    ready = {'s5_glu_w': dglu_w, 'w_out': dw_out}
    early = [_split_shards(n, ready[n], shard.shape).astype(BF16) for n, shard in late]
    du, db4_re, db4_im, dc4_re, dc4_im, dlbr, dlbi, *arrived = _s5_bwd(
        u, du_d, dy_ssm, s_re, s_im, b4_re, b4_im, c4_re, c4_im, rpow_r, rpow_i, tt, early)
    group_ind = (jnp.arange(N_STATE)[:, None] // S5_STATE == jnp.arange(LANES)[None, :]).astype(F32)
    dlam_re, dlam_im, dlogdt, db_re_t, db_im_t = _s5_param_bwd(
        lam_re, lam_im, logdt, b_re_t, b_im_t, dlbr, dlbi, _block_diag_b_t(db4_re), _block_diag_b_t(db4_im), group_ind)

    cots = list(_rwkv_scan_bwd(ops, states, kept, dysc)) + [dr_b, dk2_b, dv_b]
    drws, dmu, dw0, dw2p, da0, da2p, dk_k, dk_a = _rwkv_pre_bwd(rw, *pre_consts, cots, tt)

    grad_x, dw_in, dnorm_g = _bwd_in(x, norm_g, w_in_bf, row(w['rwkv_mu']), dx2, du, dzs, drws, dzr, tt)

    unb = lambda t: t.reshape(S5_GROUP, S5_GROUPS, S5_STATE).transpose(1, 2, 0)
    grads = {
        'norm_g': dnorm_g.reshape(D_MODEL), 'w_in_t': dw_in,
        's5_lam_re': dlam_re.reshape(S5_GROUPS, S5_STATE), 's5_lam_im': dlam_im.reshape(S5_GROUPS, S5_STATE),
        's5_log_dt': dlogdt[0, :S5_GROUPS], 's5_b_re': unb(db_re_t), 's5_b_im': unb(db_im_t),
        's5_c_re': _block_diag_c_t(dc4_re), 's5_c_im': _block_diag_c_t(dc4_im),
        's5_d': dd.reshape(D_S5), 's5_glu_w': dglu_w, 's5_glu_b': dglu_b.reshape(D_S5),
        'rwkv_mu': dmu.reshape(-1), 'rwkv_w0': dw0.reshape(-1), 'rwkv_w2': dw2p[:HEAD], 'rwkv_a0': da0.reshape(-1),
        'rwkv_a2': da2p[HEAD:], 'rwkv_k_k': dk_k.reshape(-1), 'rwkv_k_a': dk_a.reshape(-1),
        'rwkv_r_k': dr_k.reshape(N_HEADS, HEAD), 'rwkv_ln_w': dln_w.reshape(-1), 'rwkv_ln_b': dln_b.reshape(-1),
        'w_out': dw_out, 'final_g': dgf.reshape(D_MODEL),
    }
    grads.update({n: blocks for (n, _), blocks in zip(late, arrived)})
    return loss, grad_x, grads


def _exchange(arrays, gather, axes, name):
    n = len(arrays)

    def body(*refs):
        copies = _exchange_copies(refs[:n], refs[n:2 * n], *refs[2 * n:], gather, axes)
        _start_copies(copies)
        _wait_copies(copies)

    return pl.pallas_call(
        body, name=name, in_specs=[_ANY] * n, out_specs=[_ANY] * n,
        out_shape=_exchange_results(arrays, gather, axes), scratch_shapes=_exchange_scratch(n, axes),
        compiler_params=pltpu.CompilerParams(has_side_effects=True),
    )(*arrays)


def _exchange_results(arrays, gather, axes):
    return [jax.ShapeDtypeStruct(((2 ** len(axes),) + a.shape) if whole else a.shape, a.dtype)
            for a, whole in zip(arrays, gather)]


def _exchange_scratch(n, axes):
    peers = 2 ** len(axes) - 1
    return [pltpu.SemaphoreType.DMA((n, peers)), pltpu.SemaphoreType.DMA((n, peers)), pltpu.SemaphoreType.DMA((n,))]


def _exchange_copies(send_refs, recv_refs, send_sems, recv_sems, local_sems, gather, axes):
    pos = {ax: lax.axis_index(ax) for ax in ("x", "y", "c")}

    def index_of(p):
        idx = 0
        for ax in axes:
            idx = 2 * idx + p[ax]
        return idx

    me = index_of(pos)
    own, outs, arrivals = [], [], []
    for i, (send_ref, recv_ref) in enumerate(zip(send_refs, recv_refs)):
        def block_for(dev, send_ref=send_ref, whole=gather[i]):
            return send_ref if whole else send_ref.at[dev]

        own.append(pltpu.make_async_copy(block_for(me), recv_ref.at[me], local_sems.at[i]))
        for k in range(1, 2 ** len(axes)):
            peer = dict(pos)
            for bit, ax in enumerate(axes):
                if (k >> bit) & 1:
                    peer[ax] = 1 - pos[ax]
            peer_idx = index_of(peer)
            sems = dict(send_sem=send_sems.at[i, k - 1], recv_sem=recv_sems.at[i, k - 1],
                        device_id=(peer["x"], peer["y"], peer["c"]), device_id_type=pl.DeviceIdType.MESH)
            outs.append(pltpu.make_async_remote_copy(src_ref=block_for(peer_idx), dst_ref=recv_ref.at[me], **sems))
            arrivals.append(
                pltpu.make_async_remote_copy(src_ref=block_for(peer_idx), dst_ref=recv_ref.at[peer_idx], **sems))
    return own, outs, arrivals


def _start_copies(copies):
    own, outs, _ = copies
    for copy in own + outs:
        copy.start()


def _wait_copies(copies):
    own, outs, arrivals = copies
    for copy in arrivals:
        copy.wait_recv()
    for copy in outs:
        copy.wait_send()
    for copy in own:
        copy.wait()


def _sum_devices(ref):
    g = ref[0].astype(F32)
    for s in range(1, ref.shape[0]):
        g = g + ref[s].astype(F32)
    return g


def _adamw_math(g, w, m, v):
    m_new = ADAM_B1 * m + (1.0 - ADAM_B1) * g
    v_new = ADAM_B2 * v + (1.0 - ADAM_B2) * (g * g)
    m_hat = m_new / (1.0 - ADAM_B1 ** ADAM_STEP)
    v_hat = v_new / (1.0 - ADAM_B2 ** ADAM_STEP)
    return -ADAM_LR * (m_hat / (jnp.sqrt(v_hat) + ADAM_EPS) + ADAM_WD * w), m_new, v_new


def _adamw(gs, ws, ms, vs, reduce, name, gathered=()):
    n = len(ws)
    n_g = len(gathered)
    n_out = (4 if reduce else 3) * n
    everyone = ("x", "y", "c")

    def body(*refs):
        g_refs, w_refs, m_refs, v_refs = (refs[j * n:(j + 1) * n] for j in range(4))
        outs = refs[4 * n + n_g:4 * n + n_g + n_out]
        if n_g:
            copies = _exchange_copies(refs[4 * n:4 * n + n_g], refs[4 * n + n_g + n_out:4 * n + 2 * n_g + n_out],
                                      *refs[4 * n + 2 * n_g + n_out:], (True,) * n_g, everyone)
            _start_copies(copies)
        for i in range(n):
            g = _sum_devices(g_refs[i]) if reduce else g_refs[i][...]
            res = _adamw_math(g, w_refs[i][...], m_refs[i][...], v_refs[i][...])
            for j, val in enumerate(((g,) if reduce else ()) + res):
                outs[j * n + i][...] = val
        if n_g:
            _wait_copies(copies)

    vmem = pl.BlockSpec(memory_space=pltpu.VMEM)
    return pl.pallas_call(
        body, name=name, in_specs=[vmem] * (4 * n) + [_ANY] * n_g, out_specs=[vmem] * n_out + [_ANY] * n_g,
        out_shape=[_sds(w.shape) for w in ws] * (n_out // n) + _exchange_results(gathered, (True,) * n_g, everyone),
        scratch_shapes=_exchange_scratch(n_g, everyone) if n_g else [],
        compiler_params=pltpu.CompilerParams(vmem_limit_bytes=VMEM_LIMIT, has_side_effects=bool(n_g)),
    )(*gs, *ws, *ms, *vs, *gathered)


def _sum_blocks(recv):
    def body(recv_ref, out_ref):
        out_ref[...] = _sum_devices(recv_ref)

    return pl.pallas_call(body, name="sum_small_grads", out_shape=_sds(recv.shape[1:]))(recv)


_WEIGHTS = [
    ('norm_g', (1, 1024), False), ('w_in', (1, 1024, 400), True), ('s5_lam_re', (1, 32, 64), False),
    ('s5_lam_im', (1, 32, 64), False), ('s5_log_dt', (1, 32), False), ('s5_b_re', (1, 32, 64, 16), False),
    ('s5_b_im', (1, 32, 64, 16), False), ('s5_c_re', (1, 32, 16, 64), False), ('s5_c_im', (1, 32, 16, 64), False),
    ('s5_d', (1, 512), False), ('s5_glu_w', (1, 64, 512), True), ('s5_glu_b', (1, 512), False),
    ('rwkv_mu', (1, 1664), False), ('rwkv_w0', (1, 512), False), ('rwkv_w2', (1, 64, 64), True),
    ('rwkv_a0', (1, 512), False), ('rwkv_a2', (1, 64, 64), True), ('rwkv_k_k', (1, 512), False),
    ('rwkv_k_a', (1, 512), False), ('rwkv_r_k', (1, 8, 64), False), ('rwkv_ln_w', (1, 512), False),
    ('rwkv_ln_b', (1, 512), False), ('w_out', (1, 128, 1024), True), ('final_g', (1024,), False),
]
_SHARDED = [(n, s) for n, s, sharded in _WEIGHTS if sharded]
_SMALL = [(n, s) for n, s, sharded in _WEIGHTS if not sharded]
_COLUMN_SHARDED = ('w_in', 'rwkv_w2', 'rwkv_a2')
_SMALL_SIZE = sum(math.prod(s) for _, s in _SMALL) + 1
_SMALL_ROWS = -(-_SMALL_SIZE // (8 * LANES)) * 8


_MINOR_SWAPPED = ('s5_b_re', 's5_b_im')


def _stored(name, t):
    return jnp.swapaxes(t, -1, -2) if name in _MINOR_SWAPPED else t


def _pack_small(grads, loss):
    flat = [_stored(n, grads[n]).reshape(-1) for n, _ in _SMALL] + [loss.reshape(1)]
    pad = _SMALL_ROWS * LANES - _SMALL_SIZE
    return jnp.concatenate(flat + [jnp.zeros((pad,), F32)]).reshape(_SMALL_ROWS, LANES)


def _unpack_small(packed):
    flat = packed.reshape(-1)
    out, off = {}, 0
    for n, s in _SMALL:
        size = math.prod(s)
        out[n] = flat[off:off + size].reshape(s[:-2] + (s[-1], s[-2]) if n in _MINOR_SWAPPED else s)
        off += size
    return out, flat[off]


_BF16_OPERANDS = ('w_in', 's5_glu_w', 'w_out')
_LATE_WEIGHTS = ('s5_glu_w', 'w_out')


def _join_shards(name, blocks):
    _, rows, cols = blocks.shape
    if name in _COLUMN_SHARDED:
        return blocks.transpose(1, 0, 2).reshape(rows, N_DEV * cols)
    return blocks.reshape(N_DEV * rows, cols)


def _split_shards(name, full, shard_shape):
    rows, cols = shard_shape
    if name in _COLUMN_SHARDED:
        return full.reshape(rows, N_DEV, cols).transpose(1, 0, 2)
    return full.reshape(N_DEV, rows, cols)


def kernel(x, norm_g, w_in, s5_lam_re, s5_lam_im, s5_log_dt, s5_b_re, s5_b_im, s5_c_re, s5_c_im, s5_d, s5_glu_w, s5_glu_b, rwkv_mu, rwkv_w0, rwkv_w2, rwkv_a0, rwkv_a2, rwkv_k_k, rwkv_k_a, rwkv_r_k, rwkv_ln_w, rwkv_ln_b, w_out, final_g, loss_target, m_norm_g, m_w_in, m_s5_lam_re, m_s5_lam_im, m_s5_log_dt, m_s5_b_re, m_s5_b_im, m_s5_c_re, m_s5_c_im, m_s5_d, m_s5_glu_w, m_s5_glu_b, m_rwkv_mu, m_rwkv_w0, m_rwkv_w2, m_rwkv_a0, m_rwkv_a2, m_rwkv_k_k, m_rwkv_k_a, m_rwkv_r_k, m_rwkv_ln_w, m_rwkv_ln_b, m_w_out, m_final_g, v_norm_g, v_w_in, v_s5_lam_re, v_s5_lam_im, v_s5_log_dt, v_s5_b_re, v_s5_b_im, v_s5_c_re, v_s5_c_im, v_s5_d, v_s5_glu_w, v_s5_glu_b, v_rwkv_mu, v_rwkv_w0, v_rwkv_w2, v_rwkv_a0, v_rwkv_a2, v_rwkv_k_k, v_rwkv_k_a, v_rwkv_r_k, v_rwkv_ln_w, v_rwkv_ln_b, v_w_out, v_final_g):
    given = dict(locals())

    n_sh = len(_SHARDED)
    everyone = ("x", "y", "c")
    shards = [given[n][0].astype(BF16 if n in _BF16_OPERANDS else F32) for n, _ in _SHARDED]
    shards[0] = shards[0].T
    names = [n for n, _ in _SHARDED]
    first = [i for i, n in enumerate(names) if n not in _LATE_WEIGHTS]
    gathered = _exchange([shards[i] for i in first], (True,) * len(first), everyone, "gather_weights")
    local = {names[i]: _join_shards(names[i], blocks) for i, blocks in list(zip(first, gathered))[1:]}
    local['w_in_t'] = gathered[0].reshape(D_IN, D_MODEL)
    local.update({n: (given[n][0] if len(s) > 1 else given[n]) for n, s in _SMALL})
    late = [(n, shards[names.index(n)]) for n in _LATE_WEIGHTS]

    loss, grad_x, grads = _local_step(x[0], loss_target[0], local, late)

    blocks = [grads['w_in_t'].reshape(N_DEV, D_IN // N_DEV, D_MODEL)]
    blocks += [_split_shards(names[i], grads[names[i]], _SHARDED[i][1][1:]).astype(BF16) for i in first[1:]]
    small = _pack_small(grads, loss[0, 0]).reshape(N_DEV, _SMALL_ROWS // N_DEV, LANES)
    recv = _exchange(blocks + [small], (False,) * (len(first) + 1), everyone, "exchange_grads")
    small_piece = _sum_blocks(recv[-1])
    arrived = dict(zip([names[i] for i in first], recv), **{n: grads[n] for n in _LATE_WEIGHTS})

    result = {}
    for group, name in (([0], "adamw_w_in"), ([1, 2, 3, 4], "adamw_shards")):
        ns = [_SHARDED[i][0] for i in group]
        own = (lambda t: t[0].T) if group == [0] else (lambda t: t[0])
        back = (lambda t: t.T[None]) if group == [0] else (lambda t: t[None])
        res = _adamw([arrived[n] for n in ns], [own(given[n]) for n in ns], [own(given['m_' + n]) for n in ns],
                     [own(given['v_' + n]) for n in ns], True, name, [small_piece] if group == [0] else [])
        if group == [0]:
            small_sum = res[-1]
        for j, n in enumerate(ns):
            result[n] = [back(res[k * len(ns) + j]) for k in range(4)]
    g_small, total = _unpack_small(small_sum)
    two_d = lambda t: t.reshape(1, -1) if t.ndim == 1 else t
    ns = [n for n, _ in _SMALL]
    res = _adamw([two_d(g_small[n]) for n in ns], [two_d(_stored(n, given[n])) for n in ns],
                 [two_d(_stored(n, given['m_' + n])) for n in ns], [two_d(_stored(n, given['v_' + n])) for n in ns],
                 False, "adamw_small")
    for j, (n, s) in enumerate(_SMALL):
        result[n] = [_stored(n, g_small[n])] + [_stored(n, res[k * len(ns) + j]).reshape(s) for k in range(3)]

    outs = [total, grad_x[None]]
    for k in range(4):
        outs += [result[n][k] for n, _, _ in _WEIGHTS]
    return tuple(outs)
```
